```python
import jax, jax.numpy as jnp
from jax import lax
import numpy as np

D_MODEL = 2048
BATCH = 8
SEQ = 8192
DEPTH = 1

D_MIX = D_MODEL
D_GMLP = D_MIX // 2
D_LRU = D_MIX - D_GMLP
CHUNK = 128
GMLP_HEAD_DIM = 128
N_GMLP_HEADS = D_GMLP // GMLP_HEAD_DIM
LRU_BLOCK = 128
N_LRU_BLOCKS = D_LRU // LRU_BLOCK
CONV_WIDTH = 4
LRU_C = 8.0
D_PLE = 256
EPS = 1e-6
D_IN_PROJ = 3 * D_GMLP + 2 * D_LRU

kernel_name = "hymba_gmlp_rglru_sandwich_ple"


def rmsnorm(x, g):
    xf = x.astype(jnp.float32)
    y = xf * lax.rsqrt(jnp.mean(xf * xf, axis=-1, keepdims=True) + EPS)
    return (y * g.astype(jnp.float32)).astype(x.dtype)


def layernorm(x, g, b):
    xf = x.astype(jnp.float32)
    mu = jnp.mean(xf, axis=-1, keepdims=True)
    xc = xf - mu
    y = xc * lax.rsqrt(jnp.mean(xc * xc, axis=-1, keepdims=True) + EPS)
    return (y * g.astype(jnp.float32) + b.astype(jnp.float32)).astype(x.dtype)


def gmlp_branch(u, v, ln_g, ln_b, w_s, b_s):
    bsz, s, _ = v.shape
    u = jax.nn.gelu(u)
    v = layernorm(jax.nn.gelu(v), ln_g, ln_b)
    vc = v.reshape(bsz, s // CHUNK, CHUNK, N_GMLP_HEADS, GMLP_HEAD_DIM)
    causal = jnp.tril(jnp.ones((CHUNK, CHUNK), dtype=bool))
    w = jnp.where(causal[None], w_s, jnp.zeros_like(w_s))
    mixed = jnp.einsum('hts,bcshd->bcthd', w, vc) + jnp.transpose(b_s)[None, None, :, :, None]
    return u * mixed.reshape(bsz, s, D_GMLP)


def _lin_rec_combine(left, right):
    a_l, b_l = left
    a_r, b_r = right
    return a_l * a_r, a_r * b_l + b_r


def rglru_branch(xb, conv_w, conv_b, w_a, b_a, w_x, b_x, lam):
    bsz, s, c = xb.shape
    xc = lax.conv_general_dilated(
        xb, conv_w, window_strides=(1,), padding=[(CONV_WIDTH - 1, 0)],
        dimension_numbers=('NWC', 'WIO', 'NWC'), feature_group_count=c) + conv_b
    xh = xc.reshape(bsz, s, N_LRU_BLOCKS, LRU_BLOCK)
    r = jax.nn.sigmoid(jnp.einsum('bshi,hij->bshj', xh, w_a) + b_a).reshape(bsz, s, c)
    i = jax.nn.sigmoid(jnp.einsum('bshi,hij->bshj', xh, w_x) + b_x).reshape(bsz, s, c)
    log_a = -LRU_C * r.astype(jnp.float32) * jax.nn.softplus(-lam.astype(jnp.float32))
    a = jnp.exp(log_a)
    mult = jnp.sqrt(-jnp.expm1(2.0 * log_a))
    is_first = (jnp.arange(s) == 0)[None, :, None]
    mult = jnp.where(is_first, jnp.ones_like(mult), mult)
    bt = mult * (i * xc).astype(jnp.float32)
    _, h = lax.associative_scan(_lin_rec_combine, (a, bt), axis=1)
    return h.astype(xb.dtype)


def _fwd_setup_inputs(seed: int = 0) -> dict:
    key = jax.random.key(seed)
    ks = jax.random.split(key, 24)
    f32 = jnp.float32
    n = lambda k, shape, scale: jax.random.normal(k, shape, f32) * scale
    gain = lambda k, shape: 1.0 + 0.01 * jax.random.normal(k, shape, f32)
    x = jax.random.normal(ks[0], (BATCH, SEQ, D_MODEL), f32)
    p = jax.random.normal(ks[1], (DEPTH, BATCH, SEQ, D_PLE), f32)
    pre_g = gain(ks[2], (DEPTH, D_MODEL))
    w_in = n(ks[3], (DEPTH, D_MODEL, D_IN_PROJ), D_MODEL ** -0.5)
    gmlp_ln_g = gain(ks[4], (DEPTH, D_GMLP))
    gmlp_ln_b = n(ks[5], (DEPTH, D_GMLP), 0.01)
    gmlp_ws = n(ks[6], (DEPTH, N_GMLP_HEADS, CHUNK, CHUNK), CHUNK ** -0.5)
    gmlp_bs = gain(ks[7], (DEPTH, N_GMLP_HEADS, CHUNK))
    conv_w = n(ks[8], (DEPTH, CONV_WIDTH, 1, D_LRU), CONV_WIDTH ** -0.5)
    conv_b = n(ks[9], (DEPTH, D_LRU), 0.01)
    w_a = n(ks[10], (DEPTH, N_LRU_BLOCKS, LRU_BLOCK, LRU_BLOCK), LRU_BLOCK ** -0.5)
    b_a = n(ks[11], (DEPTH, N_LRU_BLOCKS, LRU_BLOCK), 0.01)
    w_x = n(ks[12], (DEPTH, N_LRU_BLOCKS, LRU_BLOCK, LRU_BLOCK), LRU_BLOCK ** -0.5)
    b_x = n(ks[13], (DEPTH, N_LRU_BLOCKS, LRU_BLOCK), 0.01)
    a0 = jax.random.uniform(ks[14], (DEPTH, D_LRU), f32, 0.9, 0.999)
    s0 = a0 ** (1.0 / LRU_C)
    lam = jnp.log(s0) - jnp.log1p(-s0)
    gmlp_out_g = gain(ks[15], (DEPTH, D_GMLP))
    lru_out_g = gain(ks[16], (DEPTH, D_LRU))
    w_out = n(ks[17], (DEPTH, D_MIX, D_MODEL), D_MIX ** -0.5)
    post_g = gain(ks[18], (DEPTH, D_MODEL))
    w_pe = n(ks[19], (DEPTH, D_PLE, D_MODEL), D_PLE ** -0.5)
    w_pg = n(ks[20], (DEPTH, D_MODEL, D_MODEL), D_MODEL ** -0.5)
    return {"x": x, "p": p, "pre_g": pre_g, "w_in": w_in, "gmlp_ln_g": gmlp_ln_g,
            "gmlp_ln_b": gmlp_ln_b, "gmlp_ws": gmlp_ws, "gmlp_bs": gmlp_bs,
            "conv_w": conv_w, "conv_b": conv_b, "w_a": w_a, "b_a": b_a, "w_x": w_x,
            "b_x": b_x, "lam": lam, "gmlp_out_g": gmlp_out_g, "lru_out_g": lru_out_g,
            "w_out": w_out, "post_g": post_g, "w_pe": w_pe, "w_pg": w_pg}


def _fwd_reference(x, p, pre_g, w_in, gmlp_ln_g, gmlp_ln_b, gmlp_ws, gmlp_bs, conv_w, conv_b,
              w_a, b_a, w_x, b_x, lam, gmlp_out_g, lru_out_g, w_out, post_g, w_pe, w_pg):
    h = x
    splits = [D_GMLP, 2 * D_GMLP, 3 * D_GMLP, 3 * D_GMLP + D_LRU]
    for l in range(DEPTH):
        hn = rmsnorm(h, pre_g[l])
        z = hn @ w_in[l]
        u, v, gate_a, xb, gate_b = jnp.split(z, splits, axis=-1)
        ya = gmlp_branch(u, v, gmlp_ln_g[l], gmlp_ln_b[l], gmlp_ws[l], gmlp_bs[l]) * jax.nn.silu(gate_a)
        yb = rglru_branch(xb, conv_w[l], conv_b[l], w_a[l], b_a[l], w_x[l], b_x[l], lam[l]) * jax.nn.silu(gate_b)
        y = jnp.concatenate([rmsnorm(ya, gmlp_out_g[l]), rmsnorm(yb, lru_out_g[l])], axis=-1)
        h = h + rmsnorm(y @ w_out[l], post_g[l])
        h = h + (p[l] @ w_pe[l]) * jax.nn.sigmoid(h @ w_pg[l])
    return h


import jax as _jax
import jax.numpy as _jnp

TWIN_FORMAT = 'train_step'
FWD_PARAMS = ['x', 'p', 'pre_g', 'w_in', 'gmlp_ln_g', 'gmlp_ln_b', 'gmlp_ws', 'gmlp_bs', 'conv_w', 'conv_b', 'w_a', 'b_a', 'w_x', 'b_x', 'lam', 'gmlp_out_g', 'lru_out_g', 'w_out', 'post_g', 'w_pe', 'w_pg']
TWIN_WEIGHTS = ['pre_g', 'w_in', 'gmlp_ln_g', 'gmlp_ln_b', 'gmlp_ws', 'gmlp_bs', 'conv_w', 'conv_b', 'w_a', 'b_a', 'w_x', 'b_x', 'lam', 'gmlp_out_g', 'lru_out_g', 'w_out', 'post_g', 'w_pe', 'w_pg']
TWIN_DIFF_INPUT = 'x'
TWIN_INPUTS = ['x', 'p', 'pre_g', 'w_in', 'gmlp_ln_g', 'gmlp_ln_b', 'gmlp_ws', 'gmlp_bs', 'conv_w', 'conv_b', 'w_a', 'b_a', 'w_x', 'b_x', 'lam', 'gmlp_out_g', 'lru_out_g', 'w_out', 'post_g', 'w_pe', 'w_pg', 'loss_target', 'm_pre_g', 'm_w_in', 'm_gmlp_ln_g', 'm_gmlp_ln_b', 'm_gmlp_ws', 'm_gmlp_bs', 'm_conv_w', 'm_conv_b', 'm_w_a', 'm_b_a', 'm_w_x', 'm_b_x', 'm_lam', 'm_gmlp_out_g', 'm_lru_out_g', 'm_w_out', 'm_post_g', 'm_w_pe', 'm_w_pg', 'v_pre_g', 'v_w_in', 'v_gmlp_ln_g', 'v_gmlp_ln_b', 'v_gmlp_ws', 'v_gmlp_bs', 'v_conv_w', 'v_conv_b', 'v_w_a', 'v_b_a', 'v_w_x', 'v_b_x', 'v_lam', 'v_gmlp_out_g', 'v_lru_out_g', 'v_w_out', 'v_post_g', 'v_w_pe', 'v_w_pg']
TWIN_OUTPUTS = ['loss', 'grad_x', 'grad_pre_g', 'grad_w_in', 'grad_gmlp_ln_g', 'grad_gmlp_ln_b', 'grad_gmlp_ws', 'grad_gmlp_bs', 'grad_conv_w', 'grad_conv_b', 'grad_w_a', 'grad_b_a', 'grad_w_x', 'grad_b_x', 'grad_lam', 'grad_gmlp_out_g', 'grad_lru_out_g', 'grad_w_out', 'grad_post_g', 'grad_w_pe', 'grad_w_pg', 'delta_pre_g', 'delta_w_in', 'delta_gmlp_ln_g', 'delta_gmlp_ln_b', 'delta_gmlp_ws', 'delta_gmlp_bs', 'delta_conv_w', 'delta_conv_b', 'delta_w_a', 'delta_b_a', 'delta_w_x', 'delta_b_x', 'delta_lam', 'delta_gmlp_out_g', 'delta_lru_out_g', 'delta_w_out', 'delta_post_g', 'delta_w_pe', 'delta_w_pg', 'new_m_pre_g', 'new_m_w_in', 'new_m_gmlp_ln_g', 'new_m_gmlp_ln_b', 'new_m_gmlp_ws', 'new_m_gmlp_bs', 'new_m_conv_w', 'new_m_conv_b', 'new_m_w_a', 'new_m_b_a', 'new_m_w_x', 'new_m_b_x', 'new_m_lam', 'new_m_gmlp_out_g', 'new_m_lru_out_g', 'new_m_w_out', 'new_m_post_g', 'new_m_w_pe', 'new_m_w_pg', 'new_v_pre_g', 'new_v_w_in', 'new_v_gmlp_ln_g', 'new_v_gmlp_ln_b', 'new_v_gmlp_ws', 'new_v_gmlp_bs', 'new_v_conv_w', 'new_v_conv_b', 'new_v_w_a', 'new_v_b_a', 'new_v_w_x', 'new_v_b_x', 'new_v_lam', 'new_v_gmlp_out_g', 'new_v_lru_out_g', 'new_v_w_out', 'new_v_post_g', 'new_v_w_pe', 'new_v_w_pg']
TWIN_LEAF_KINDS = {'loss': 'loss', 'grad_x': 'grad_x', 'grad_pre_g': 'grad_w', 'grad_w_in': 'grad_w', 'grad_gmlp_ln_g': 'grad_w', 'grad_gmlp_ln_b': 'grad_w', 'grad_gmlp_ws': 'grad_w', 'grad_gmlp_bs': 'grad_w', 'grad_conv_w': 'grad_w', 'grad_conv_b': 'grad_w', 'grad_w_a': 'grad_w', 'grad_b_a': 'grad_w', 'grad_w_x': 'grad_w', 'grad_b_x': 'grad_w', 'grad_lam': 'grad_w', 'grad_gmlp_out_g': 'grad_w', 'grad_lru_out_g': 'grad_w', 'grad_w_out': 'grad_w', 'grad_post_g': 'grad_w', 'grad_w_pe': 'grad_w', 'grad_w_pg': 'grad_w', 'delta_pre_g': 'delta_w', 'delta_w_in': 'delta_w', 'delta_gmlp_ln_g': 'delta_w', 'delta_gmlp_ln_b': 'delta_w', 'delta_gmlp_ws': 'delta_w', 'delta_gmlp_bs': 'delta_w', 'delta_conv_w': 'delta_w', 'delta_conv_b': 'delta_w', 'delta_w_a': 'delta_w', 'delta_b_a': 'delta_w', 'delta_w_x': 'delta_w', 'delta_b_x': 'delta_w', 'delta_lam': 'delta_w', 'delta_gmlp_out_g': 'delta_w', 'delta_lru_out_g': 'delta_w', 'delta_w_out': 'delta_w', 'delta_post_g': 'delta_w', 'delta_w_pe': 'delta_w', 'delta_w_pg': 'delta_w', 'new_m_pre_g': 'new_m', 'new_m_w_in': 'new_m', 'new_m_gmlp_ln_g': 'new_m', 'new_m_gmlp_ln_b': 'new_m', 'new_m_gmlp_ws': 'new_m', 'new_m_gmlp_bs': 'new_m', 'new_m_conv_w': 'new_m', 'new_m_conv_b': 'new_m', 'new_m_w_a': 'new_m', 'new_m_b_a': 'new_m', 'new_m_w_x': 'new_m', 'new_m_b_x': 'new_m', 'new_m_lam': 'new_m', 'new_m_gmlp_out_g': 'new_m', 'new_m_lru_out_g': 'new_m', 'new_m_w_out': 'new_m', 'new_m_post_g': 'new_m', 'new_m_w_pe': 'new_m', 'new_m_w_pg': 'new_m', 'new_v_pre_g': 'new_v', 'new_v_w_in': 'new_v', 'new_v_gmlp_ln_g': 'new_v', 'new_v_gmlp_ln_b': 'new_v', 'new_v_gmlp_ws': 'new_v', 'new_v_gmlp_bs': 'new_v', 'new_v_conv_w': 'new_v', 'new_v_conv_b': 'new_v', 'new_v_w_a': 'new_v', 'new_v_b_a': 'new_v', 'new_v_w_x': 'new_v', 'new_v_b_x': 'new_v', 'new_v_lam': 'new_v', 'new_v_gmlp_out_g': 'new_v', 'new_v_lru_out_g': 'new_v', 'new_v_w_out': 'new_v', 'new_v_post_g': 'new_v', 'new_v_w_pe': 'new_v', 'new_v_w_pg': 'new_v'}


def _forward(args):
    return _fwd_reference(*[args[k] for k in FWD_PARAMS])


def _output_shape():
    def fwd():
        inp = _fwd_setup_inputs(0)
        return _fwd_reference(*[inp[k] for k in FWD_PARAMS])
    out = _jax.eval_shape(fwd)
    return out.shape, out.dtype

N_MICROBATCH = 1
ADAM_LR = 0.001
ADAM_B1 = 0.9
ADAM_B2 = 0.999
ADAM_EPS = 1e-08
ADAM_WD = 0.01
ADAM_STEP = 10
PER_EXAMPLE_BATCH_AXIS = {'x': 0, 'p': 1, 'loss_target': 0}
SHARED_INPUTS = []
_WEIGHT_DTYPES = {'pre_g': _jnp.float32, 'w_in': _jnp.float32, 'gmlp_ln_g': _jnp.float32, 'gmlp_ln_b': _jnp.float32, 'gmlp_ws': _jnp.float32, 'gmlp_bs': _jnp.float32, 'conv_w': _jnp.float32, 'conv_b': _jnp.float32, 'w_a': _jnp.float32, 'b_a': _jnp.float32, 'w_x': _jnp.float32, 'b_x': _jnp.float32, 'lam': _jnp.float32, 'gmlp_out_g': _jnp.float32, 'lru_out_g': _jnp.float32, 'w_out': _jnp.float32, 'post_g': _jnp.float32, 'w_pe': _jnp.float32, 'w_pg': _jnp.float32}
MOMENT_SCALE = {'pre_g': 3.322208e-01, 'w_in': 2.122017e-01, 'gmlp_ln_g': 1.135307e-01, 'gmlp_ln_b': 1.166643e-01, 'gmlp_ws': 1.108349e-01, 'gmlp_bs': 1.571889e-01, 'conv_w': 3.262525e-01, 'conv_b': 1.018181e+01, 'w_a': 2.003448e-01, 'b_a': 1.304141e-01, 'w_x': 3.751105e-01, 'b_x': 1.010507e-01, 'lam': 1.836722e-01, 'gmlp_out_g': 3.326269e-01, 'lru_out_g': 3.977059e-01, 'w_out': 3.584677e-01, 'post_g': 3.266014e+01, 'w_pe': 4.844279e-01, 'w_pg': 3.532164e-01}


def _to_microbatches(a, axis):
    t = _jnp.moveaxis(a, axis, 0)
    t = t.reshape((N_MICROBATCH, t.shape[0] // N_MICROBATCH) + t.shape[1:])
    return _jnp.moveaxis(t, 1, axis + 1)


def setup_inputs(seed: int = 0) -> dict:
    inp = _fwd_setup_inputs(seed)
    key = _jax.random.fold_in(_jax.random.key(seed), 7919)
    shape, _ = _output_shape()
    out = dict(inp)
    out["loss_target"] = _jax.random.normal(_jax.random.fold_in(key, 0), shape, _jnp.float32)
    for i, name in enumerate(TWIN_WEIGHTS):
        w = inp[name].astype(_jnp.float32)
        if MOMENT_SCALE is None:
            s = _jnp.sqrt(_jnp.mean(_jnp.square(w)) + 1e-30)
        else:
            s = MOMENT_SCALE[name]
        km, kv = _jax.random.split(_jax.random.fold_in(key, i + 1))
        out[name] = w
        out["m_" + name] = s * _jax.random.normal(km, w.shape, _jnp.float32)
        out["v_" + name] = (s * s) * _jax.random.uniform(kv, w.shape, _jnp.float32, 0.5, 1.5)
    if N_MICROBATCH > 1:
        for name, axis in PER_EXAMPLE_BATCH_AXIS.items():
            out[name] = _to_microbatches(out[name], axis)
    return {'x': out['x'], 'p': out['p'], 'pre_g': out['pre_g'], 'w_in': out['w_in'], 'gmlp_ln_g': out['gmlp_ln_g'], 'gmlp_ln_b': out['gmlp_ln_b'], 'gmlp_ws': out['gmlp_ws'], 'gmlp_bs': out['gmlp_bs'], 'conv_w': out['conv_w'], 'conv_b': out['conv_b'], 'w_a': out['w_a'], 'b_a': out['b_a'], 'w_x': out['w_x'], 'b_x': out['b_x'], 'lam': out['lam'], 'gmlp_out_g': out['gmlp_out_g'], 'lru_out_g': out['lru_out_g'], 'w_out': out['w_out'], 'post_g': out['post_g'], 'w_pe': out['w_pe'], 'w_pg': out['w_pg'], 'loss_target': out['loss_target'], 'm_pre_g': out['m_pre_g'], 'm_w_in': out['m_w_in'], 'm_gmlp_ln_g': out['m_gmlp_ln_g'], 'm_gmlp_ln_b': out['m_gmlp_ln_b'], 'm_gmlp_ws': out['m_gmlp_ws'], 'm_gmlp_bs': out['m_gmlp_bs'], 'm_conv_w': out['m_conv_w'], 'm_conv_b': out['m_conv_b'], 'm_w_a': out['m_w_a'], 'm_b_a': out['m_b_a'], 'm_w_x': out['m_w_x'], 'm_b_x': out['m_b_x'], 'm_lam': out['m_lam'], 'm_gmlp_out_g': out['m_gmlp_out_g'], 'm_lru_out_g': out['m_lru_out_g'], 'm_w_out': out['m_w_out'], 'm_post_g': out['m_post_g'], 'm_w_pe': out['m_w_pe'], 'm_w_pg': out['m_w_pg'], 'v_pre_g': out['v_pre_g'], 'v_w_in': out['v_w_in'], 'v_gmlp_ln_g': out['v_gmlp_ln_g'], 'v_gmlp_ln_b': out['v_gmlp_ln_b'], 'v_gmlp_ws': out['v_gmlp_ws'], 'v_gmlp_bs': out['v_gmlp_bs'], 'v_conv_w': out['v_conv_w'], 'v_conv_b': out['v_conv_b'], 'v_w_a': out['v_w_a'], 'v_b_a': out['v_b_a'], 'v_w_x': out['v_w_x'], 'v_b_x': out['v_b_x'], 'v_lam': out['v_lam'], 'v_gmlp_out_g': out['v_gmlp_out_g'], 'v_lru_out_g': out['v_lru_out_g'], 'v_w_out': out['v_w_out'], 'v_post_g': out['v_post_g'], 'v_w_pe': out['v_w_pe'], 'v_w_pg': out['v_w_pg']}


def _loss(weights, diff, rest, loss_target):
    with _jax.named_scope("forward"):
        args = {**rest, TWIN_DIFF_INPUT: diff, **{k: w.astype(_WEIGHT_DTYPES[k]) for k, w in weights.items()}}
        y = _forward(args)
    with _jax.named_scope("loss_head"):
        err = _jnp.square(y.astype(_jnp.float32) - loss_target)
        return 0.5 * _jnp.sum(_jnp.mean(err, axis=-1)) if err.ndim else 0.5 * err


def _adamw(w, g, m, v):
    m = ADAM_B1 * m + (1.0 - ADAM_B1) * g
    v = ADAM_B2 * v + (1.0 - ADAM_B2) * _jnp.square(g)
    m_hat = m / (1.0 - ADAM_B1 ** ADAM_STEP)
    v_hat = v / (1.0 - ADAM_B2 ** ADAM_STEP)
    delta = -ADAM_LR * (m_hat / (_jnp.sqrt(v_hat) + ADAM_EPS) + ADAM_WD * w)
    return delta, m, v


def reference(x, p, pre_g, w_in, gmlp_ln_g, gmlp_ln_b, gmlp_ws, gmlp_bs, conv_w, conv_b, w_a, b_a, w_x, b_x, lam, gmlp_out_g, lru_out_g, w_out, post_g, w_pe, w_pg, loss_target, m_pre_g, m_w_in, m_gmlp_ln_g, m_gmlp_ln_b, m_gmlp_ws, m_gmlp_bs, m_conv_w, m_conv_b, m_w_a, m_b_a, m_w_x, m_b_x, m_lam, m_gmlp_out_g, m_lru_out_g, m_w_out, m_post_g, m_w_pe, m_w_pg, v_pre_g, v_w_in, v_gmlp_ln_g, v_gmlp_ln_b, v_gmlp_ws, v_gmlp_bs, v_conv_w, v_conv_b, v_w_a, v_b_a, v_w_x, v_b_x, v_lam, v_gmlp_out_g, v_lru_out_g, v_w_out, v_post_g, v_w_pe, v_w_pg):
    given = dict(x=x, p=p, pre_g=pre_g, w_in=w_in, gmlp_ln_g=gmlp_ln_g, gmlp_ln_b=gmlp_ln_b, gmlp_ws=gmlp_ws, gmlp_bs=gmlp_bs, conv_w=conv_w, conv_b=conv_b, w_a=w_a, b_a=b_a, w_x=w_x, b_x=b_x, lam=lam, gmlp_out_g=gmlp_out_g, lru_out_g=lru_out_g, w_out=w_out, post_g=post_g, w_pe=w_pe, w_pg=w_pg, loss_target=loss_target, m_pre_g=m_pre_g, m_w_in=m_w_in, m_gmlp_ln_g=m_gmlp_ln_g, m_gmlp_ln_b=m_gmlp_ln_b, m_gmlp_ws=m_gmlp_ws, m_gmlp_bs=m_gmlp_bs, m_conv_w=m_conv_w, m_conv_b=m_conv_b, m_w_a=m_w_a, m_b_a=m_b_a, m_w_x=m_w_x, m_b_x=m_b_x, m_lam=m_lam, m_gmlp_out_g=m_gmlp_out_g, m_lru_out_g=m_lru_out_g, m_w_out=m_w_out, m_post_g=m_post_g, m_w_pe=m_w_pe, m_w_pg=m_w_pg, v_pre_g=v_pre_g, v_w_in=v_w_in, v_gmlp_ln_g=v_gmlp_ln_g, v_gmlp_ln_b=v_gmlp_ln_b, v_gmlp_ws=v_gmlp_ws, v_gmlp_bs=v_gmlp_bs, v_conv_w=v_conv_w, v_conv_b=v_conv_b, v_w_a=v_w_a, v_b_a=v_b_a, v_w_x=v_w_x, v_b_x=v_b_x, v_lam=v_lam, v_gmlp_out_g=v_gmlp_out_g, v_lru_out_g=v_lru_out_g, v_w_out=v_w_out, v_post_g=v_post_g, v_w_pe=v_w_pe, v_w_pg=v_w_pg)
    weights = {n: given[n] for n in TWIN_WEIGHTS}
    shared = {n: given[n] for n in SHARED_INPUTS}
    per_example = {n: given[n] for n in ['x', 'p']}
    grad_fn = _jax.value_and_grad(_loss, argnums=(0, 1))

    def one_microbatch(ex, loss_target):
        ex = dict(ex)
        diff = ex.pop(TWIN_DIFF_INPUT)
        return grad_fn(weights, diff, {**shared, **ex}, loss_target)

    if N_MICROBATCH == 1:
        loss, (grad_w, grad_x) = one_microbatch(per_example, given["loss_target"])
    else:
        def body(carry, xs):
            loss_sum, grad_sum = carry
            l_k, (gw_k, gx_k) = one_microbatch(xs[0], xs[1])
            with _jax.named_scope("update"):
                return (loss_sum + l_k, _jax.tree.map(_jnp.add, grad_sum, gw_k)), gx_k

        init = (_jnp.zeros((), _jnp.float32), _jax.tree.map(_jnp.zeros_like, weights))
        (loss, grad_w), grad_x = _jax.lax.scan(body, init, (per_example, given["loss_target"]))
    with _jax.named_scope("update"):
        delta_w, new_m, new_v = {}, {}, {}
        for n in TWIN_WEIGHTS:
            delta_w[n], new_m[n], new_v[n] = _adamw(weights[n], grad_w[n], given["m_" + n], given["v_" + n])
    return (loss, grad_x, *[grad_w[n] for n in TWIN_WEIGHTS], *[delta_w[n] for n in TWIN_WEIGHTS],
            *[new_m[n] for n in TWIN_WEIGHTS], *[new_v[n] for n in TWIN_WEIGHTS])
```

```python
import functools

import jax
import jax.numpy as jnp
from jax import lax
from jax.experimental import pallas as pl
from jax.experimental.pallas import tpu as pltpu

F32 = jnp.float32
BF16 = jnp.bfloat16
SDS = jax.ShapeDtypeStruct

D_MODEL = 2048
D_BR = 1024
D_IN = 5 * D_BR
D_PLE = 256
N_HEAD = 8
HEAD = 128
CHUNK = 128
ROWS = 8
N_GROUP = CHUNK // ROWS
N_DEV = 8
W_IN_SHARD = D_IN // N_DEV
EPS = 1e-6
LRU_C = 8.0
CONV_W = 4
MESH_AXES = ("x", "y", "c")
MIB = 1 << 20

ADAM_LR, ADAM_B1, ADAM_B2, ADAM_EPS, ADAM_WD, ADAM_STEP = 0.001, 0.9, 0.999, 1e-08, 0.01, 10

_GELU_C = 0.7978845608028654
_GELU_A = 0.044715

V_LN_G, V_LN_B, V_CONV_B, V_B_A, V_B_X, V_LAM, V_GOUT_A, V_GOUT_B, V_CONV_W = 0, 1, 2, 3, 4, 5, 6, 7, 8
N_VEC = 16


def _params(sem, vmem_mib):
    return pltpu.CompilerParams(dimension_semantics=sem, vmem_limit_bytes=int(vmem_mib * MIB))


def _sig(x):
    return 1.0 / (1.0 + jnp.exp(-x))


def _gelu(x):
    t = jnp.tanh(_GELU_C * (x + _GELU_A * x * x * x))
    return 0.5 * x * (1.0 + t), t


def _gelu_grad(x, t):
    return 0.5 * (1.0 + t) + 0.5 * x * (1.0 - t * t) * (_GELU_C * (1.0 + 3.0 * _GELU_A * x * x))


def _expm1_nonpos(y):
    poly = y * (1.0 + y * 0.5 * (1.0 + y * (1.0 / 3.0) * (1.0 + y * 0.25 * (1.0 + y * 0.2 * (
        1.0 + y * (1.0 / 6.0) * (1.0 + y * (1.0 / 7.0)))))))
    return jnp.where(y > -0.3, poly, jnp.exp(y) - 1.0)


def _softplus(x):
    return jnp.maximum(x, 0.0) + jnp.log(1.0 + jnp.exp(-jnp.abs(x)))


def _row_ids(width):
    return lax.broadcasted_iota(jnp.int32, (ROWS, width), 0)


def _shift_down(cur, prev, k, rid):
    return jnp.where(rid >= k, pltpu.roll(cur, k, 0), pltpu.roll(prev, k, 0))


def _shift_up(cur, nxt, k, rid):
    return jnp.where(rid < ROWS - k, pltpu.roll(cur, ROWS - k, 0), pltpu.roll(nxt, ROWS - k, 0))


def _mean_last(x):
    return jnp.mean(x, axis=-1, keepdims=True)


def _rows(g):
    return pl.ds(pl.multiple_of(g * ROWS, ROWS), ROWS)


TILE_ROWS = 16


def _tile_rows(q):
    return pl.ds(pl.multiple_of(q * TILE_ROWS, TILE_ROWS), TILE_ROWS)


def _fold_rows(x):
    return x[0:ROWS, :] + x[ROWS:TILE_ROWS, :]


def _bcast_row(x, r):
    return jnp.broadcast_to(x[r:r + 1, :], x.shape)


def _dot(a, b):
    return jnp.dot(a, b, preferred_element_type=F32)


def _dot_nt(a, b):
    return lax.dot_general(a, b, (((1,), (1,)), ((), ())), preferred_element_type=F32)


def _dot_tn(a, b):
    return lax.dot_general(a, b, (((0,), (0,)), ((), ())), preferred_element_type=F32)


def _mesh_place():
    x, y, c = lax.axis_index("x"), lax.axis_index("y"), lax.axis_index("c")
    return x, y, c, 4 * x + 2 * y + c


def _peer(x, y, c, k):
    px = 1 - x if k & 4 else x
    py = 1 - y if k & 2 else y
    pc = 1 - c if k & 1 else c
    return (px, py, pc), 4 * px + 2 * py + pc


def _exchange(arrs, scatter, name):
    n = len(arrs)

    def body(*refs):
        ins, outs = refs[:n], refs[n:2 * n]
        send_sems, recv_sems, local_sems = refs[2 * n:]
        x, y, c, me = _mesh_place()
        local = []
        for a in range(n):
            src = ins[a].at[me] if scatter else ins[a]
            cp = pltpu.make_async_copy(src, outs[a].at[me], local_sems.at[a])
            cp.start()
            local.append(cp)
        sends = []
        for k in range(1, N_DEV):
            dev, lin = _peer(x, y, c, k)
            for a in range(n):
                src = ins[a].at[lin] if scatter else ins[a]
                cp = pltpu.make_async_remote_copy(
                    src_ref=src, dst_ref=outs[a].at[me], send_sem=send_sems.at[a * N_DEV + k], recv_sem=recv_sems.at[a * N_DEV + k],
                    device_id=dev, device_id_type=pl.DeviceIdType.MESH)
                cp.start()
                sends.append(cp)
        for k in range(1, N_DEV):
            dev, lin = _peer(x, y, c, k)
            for a in range(n):
                src = ins[a].at[lin] if scatter else ins[a]
                pltpu.make_async_remote_copy(
                    src_ref=src, dst_ref=outs[a].at[lin], send_sem=send_sems.at[a * N_DEV + k], recv_sem=recv_sems.at[a * N_DEV + k],
                    device_id=dev, device_id_type=pl.DeviceIdType.MESH).wait_recv()
        for cp in sends:
            cp.wait_send()
        for cp in local:
            cp.wait()

    any_spec = pl.BlockSpec(memory_space=pl.ANY)
    out_shape = [SDS(a.shape if scatter else (N_DEV,) + a.shape, a.dtype) for a in arrs]
    return pl.pallas_call(
        body, name=name, out_shape=out_shape,
        in_specs=[any_spec] * n, out_specs=[any_spec] * n,
        scratch_shapes=[pltpu.SemaphoreType.DMA((n * N_DEV,)), pltpu.SemaphoreType.DMA((n * N_DEV,)),
                        pltpu.SemaphoreType.DMA((n,))],
    )(*arrs)


def _in_proj(x, pre_g, w_in_g, tm=512):
    t_len = x.shape[0]

    def body(x_ref, g_ref, w_ref, z_ref, hn_ref):
        @pl.when(pl.program_id(1) == 0)
        def _():
            g = g_ref[...]

            def rows_body(q, _):
                rows = _tile_rows(q)
                xv = x_ref[rows, :]
                hn_ref[rows, :] = (xv * lax.rsqrt(_mean_last(xv * xv) + EPS) * g).astype(BF16)
                return 0

            lax.fori_loop(0, tm // TILE_ROWS, rows_body, 0)

        z_ref[...] = _dot(hn_ref[...], w_ref[...])

    return pl.pallas_call(
        body, name="in_proj", grid=(t_len // tm, N_DEV),
        in_specs=[pl.BlockSpec((tm, D_MODEL), lambda i, j: (i, 0)),
                  pl.BlockSpec((1, D_MODEL), lambda i, j: (0, 0)),
                  pl.BlockSpec((None, D_MODEL, W_IN_SHARD), lambda i, j: (j, 0, 0))],
        out_specs=[pl.BlockSpec((tm, W_IN_SHARD), lambda i, j: (i, j)),
                   pl.BlockSpec((tm, D_MODEL), lambda i, j: (i, 0))],
        out_shape=[SDS((t_len, D_IN), F32), SDS((t_len, D_MODEL), BF16)],
        compiler_params=_params(("arbitrary", "arbitrary"), 40),
    )(x, pre_g, w_in_g)


def _conv_rows(cur, prev, cw_ref, cb, rid):
    acc = cw_ref[3:4, :] * cur + cb
    for k in range(1, CONV_W):
        acc = acc + cw_ref[3 - k:4 - k, :] * _shift_down(cur, prev, k, rid)
    return acc


def _lru_gates(pa, px, ba, bx, sp8, first_row):
    r = _sig(pa + ba)
    i = _sig(px + bx)
    la = -(r * sp8)
    a = jnp.exp(la)
    mult = jnp.sqrt(-_expm1_nonpos(2.0 * la))
    mult = jnp.where(first_row, 1.0, mult)
    return r, i, a, mult


def _mix_fwd(z, ln_g, ln_b, wm, bias, cw, cb, wax, ba, bx, lam, goa, gob):
    t_len = z.shape[0]
    n_chunk = t_len // CHUNK

    def body(z_ref, lng_ref, lnb_ref, wm_ref, bias_ref, cw_ref, cb_ref, wax_ref, ba_ref, bx_ref, lam_ref, goa_ref,
             gob_ref, y_ref, h_ref, vn_s, xc_s, mixed_s, pre_s, y_s, carry_s, halo_s):
        c_id = pl.program_id(0)
        rid = _row_ids(D_BR)

        @pl.when(c_id == 0)
        def _():
            carry_s[...] = jnp.zeros_like(carry_s)
            halo_s[...] = jnp.zeros_like(halo_s)

        lng, lnb, cb = lng_ref[...], lnb_ref[...], cb_ref[...]

        def phase1(g, prev):
            rows = _rows(g)
            vg, _ = _gelu(z_ref[rows, D_BR:2 * D_BR])
            xm = vg - _mean_last(vg)
            rs = lax.rsqrt(_mean_last(xm * xm) + EPS)
            vn_s[rows, :] = xm * rs * lng + lnb
            xb = z_ref[rows, 3 * D_BR:4 * D_BR]
            xc_s[rows, :] = _conv_rows(xb, prev, cw_ref, cb, rid)
            return xb

        halo_s[...] = lax.fori_loop(0, N_GROUP, phase1, halo_s[...])

        for h in range(N_HEAD):
            cs = slice(h * HEAD, (h + 1) * HEAD)
            mixed_s[:, cs] = _dot(wm_ref[h], vn_s[:, cs].astype(BF16))
            pre = _dot(xc_s[:, cs].astype(BF16), wax_ref[h])
            pre_s[:, cs] = pre[:, :HEAD]
            pre_s[:, D_BR + h * HEAD:D_BR + (h + 1) * HEAD] = pre[:, HEAD:]

        ba, bx, goa, gob = ba_ref[...], bx_ref[...], goa_ref[...], gob_ref[...]
        sp8 = LRU_C * _softplus(-lam_ref[...])

        def phase3(g, carry):
            rows = _rows(g)
            ug, _ = _gelu(z_ref[rows, 0:D_BR])
            ga = z_ref[rows, 2 * D_BR:3 * D_BR]
            ya = ug * (mixed_s[rows, :] + bias_ref[rows, :]) * (ga * _sig(ga))
            y_s[rows, 0:D_BR] = ya * lax.rsqrt(_mean_last(ya * ya) + EPS) * goa

            first_row = jnp.logical_and(jnp.logical_and(c_id == 0, g == 0), rid == 0)
            _, i, a, mult = _lru_gates(pre_s[rows, 0:D_BR], pre_s[rows, D_BR:2 * D_BR], ba, bx, sp8, first_row)
            b = mult * i * xc_s[rows, :]
            for d in (1, 2, 4):
                a_sh = jnp.where(rid >= d, pltpu.roll(a, d, 0), 1.0)
                b_sh = jnp.where(rid >= d, pltpu.roll(b, d, 0), 0.0)
                b = a * b_sh + b
                a = a * a_sh
            hh = b + a * carry
            h_ref[rows, :] = hh
            gb = z_ref[rows, 4 * D_BR:5 * D_BR]
            yb = hh * (gb * _sig(gb))
            y_s[rows, D_BR:2 * D_BR] = yb * lax.rsqrt(_mean_last(yb * yb) + EPS) * gob
            return _bcast_row(hh, ROWS - 1)

        carry_s[...] = lax.fori_loop(0, N_GROUP, phase3, carry_s[...])
        y_ref[...] = y_s[...].astype(BF16)

    vec = pl.BlockSpec((1, D_BR), lambda i: (0, 0))
    return pl.pallas_call(
        body, name="mix_fwd", grid=(n_chunk,),
        in_specs=[pl.BlockSpec((CHUNK, D_IN), lambda i: (i, 0)), vec, vec,
                  pl.BlockSpec((N_HEAD, HEAD, HEAD), lambda i: (0, 0, 0)),
                  pl.BlockSpec((CHUNK, D_BR), lambda i: (0, 0)),
                  pl.BlockSpec((ROWS, D_BR), lambda i: (0, 0)), vec,
                  pl.BlockSpec((N_HEAD, HEAD, 2 * HEAD), lambda i: (0, 0, 0)), vec, vec, vec, vec, vec],
        out_specs=[pl.BlockSpec((CHUNK, 2 * D_BR), lambda i: (i, 0)), pl.BlockSpec((CHUNK, D_BR), lambda i: (i, 0))],
        out_shape=[SDS((t_len, 2 * D_BR), BF16), SDS((t_len, D_BR), F32)],
        scratch_shapes=[pltpu.VMEM((CHUNK, D_BR), F32), pltpu.VMEM((CHUNK, D_BR), F32), pltpu.VMEM((CHUNK, D_BR), F32),
                        pltpu.VMEM((CHUNK, 2 * D_BR), F32), pltpu.VMEM((CHUNK, 2 * D_BR), F32),
                        pltpu.VMEM((ROWS, D_BR), F32), pltpu.VMEM((ROWS, D_BR), F32)],
        compiler_params=_params(("arbitrary",), 32),
    )(z, ln_g, ln_b, wm, bias, cw, cb, wax, ba, bx, lam, goa, gob)


def _load_weight(w_hbm, w_vmem, sem):
    @pl.when(pl.program_id(0) == 0)
    def _():
        cp = pltpu.make_async_copy(w_hbm, w_vmem, sem)
        cp.start()
        cp.wait()


def _out_proj(y, x, w_out, post_g, tm=512):
    t_len = y.shape[0]

    def body(y_ref, x_ref, w_hbm, g_ref, h1_ref, ob_ref, w_s, o_s, sem):
        _load_weight(w_hbm, w_s, sem)
        o_s[...] = _dot(y_ref[...], w_s[...])
        g = g_ref[...]

        def rows_body(q, _):
            rows = _tile_rows(q)
            o = o_s[rows, :]
            h1_ref[rows, :] = x_ref[rows, :] + o * lax.rsqrt(_mean_last(o * o) + EPS) * g
            ob_ref[rows, :] = o.astype(BF16)
            return 0

        lax.fori_loop(0, tm // TILE_ROWS, rows_body, 0)

    tile = pl.BlockSpec((tm, D_MODEL), lambda i: (i, 0))
    return pl.pallas_call(
        body, name="out_proj", grid=(t_len // tm,),
        in_specs=[tile, tile, pl.BlockSpec(memory_space=pl.ANY), pl.BlockSpec((1, D_MODEL), lambda i: (0, 0))],
        out_specs=[tile, tile],
        out_shape=[SDS((t_len, D_MODEL), F32), SDS((t_len, D_MODEL), BF16)],
        scratch_shapes=[pltpu.VMEM((D_MODEL, D_MODEL), BF16), pltpu.VMEM((tm, D_MODEL), F32), pltpu.SemaphoreType.DMA],
        compiler_params=_params(("arbitrary",), 44),
    )(y, x, w_out, post_g)


def _ple_loss(h1, p, tgt, w_pg, w_pe_g, tm=256):
    t_len = h1.shape[0]
    n_tile = t_len // tm
    pe_shard = D_MODEL // N_DEV

    def body(h1_ref, p_ref, t_ref, w_hbm, wpe_ref, dh2_ref, dpe_ref, dgl_ref, h1b_ref, loss_ref, w_s, pe_s, gl_s, acc_s,
             sem):
        _load_weight(w_hbm, w_s, sem)
        i = pl.program_id(0)

        @pl.when(i == 0)
        def _():
            acc_s[...] = jnp.zeros_like(acc_s)

        h1b_ref[...] = h1_ref[...].astype(BF16)
        pb = p_ref[...].astype(BF16)
        for j in range(N_DEV):
            pe_s[:, j * pe_shard:(j + 1) * pe_shard] = _dot(pb, wpe_ref[j])
        gl_s[...] = _dot(h1b_ref[...], w_s[...])

        def rows_body(q, acc):
            rows = _tile_rows(q)
            pe = pe_s[rows, :]
            g = _sig(gl_s[rows, :])
            e = h1_ref[rows, :] + pe * g - t_ref[rows, :]
            dh2 = e * (1.0 / D_MODEL)
            dh2_ref[rows, :] = dh2
            dpe_ref[rows, :] = (dh2 * g).astype(BF16)
            dgl_ref[rows, :] = (dh2 * pe * g * (1.0 - g)).astype(BF16)
            return acc + _fold_rows(e * e)

        acc_s[...] = lax.fori_loop(0, tm // TILE_ROWS, rows_body, acc_s[...])

        @pl.when(i == n_tile - 1)
        def _():
            loss_ref[...] = jnp.full(loss_ref.shape, 0.5 / D_MODEL * jnp.sum(acc_s[...]), F32)

    tile = pl.BlockSpec((tm, D_MODEL), lambda i: (i, 0))
    return pl.pallas_call(
        body, name="ple_loss", grid=(n_tile,),
        in_specs=[tile, pl.BlockSpec((tm, D_PLE), lambda i: (i, 0)), tile, pl.BlockSpec(memory_space=pl.ANY),
                  pl.BlockSpec((N_DEV, D_PLE, pe_shard), lambda i: (0, 0, 0))],
        out_specs=[tile, tile, tile, tile, pl.BlockSpec((ROWS, HEAD), lambda i: (0, 0))],
        out_shape=[SDS((t_len, D_MODEL), F32), SDS((t_len, D_MODEL), BF16), SDS((t_len, D_MODEL), BF16),
                   SDS((t_len, D_MODEL), BF16), SDS((ROWS, HEAD), F32)],
        scratch_shapes=[pltpu.VMEM((D_MODEL, D_MODEL), BF16), pltpu.VMEM((tm, D_MODEL), F32),
                        pltpu.VMEM((tm, D_MODEL), F32), pltpu.VMEM((ROWS, D_MODEL), F32), pltpu.SemaphoreType.DMA],
        compiler_params=_params(("arbitrary",), 44),
    )(h1, p, tgt, w_pg, w_pe_g)


def _tail_bwd(dh2, dgl, ob, w_pg, w_out, post_g, tm=256):
    t_len = dh2.shape[0]
    n_tile = t_len // tm

    def body(dh2_ref, dgl_ref, ob_ref, wpg_hbm, wout_hbm, g_ref, dh1_ref, do_ref, dy_ref, dg_ref, wpg_s, wout_s, t_s,
             acc_s, sems):
        _load_weight(wpg_hbm, wpg_s, sems.at[0])
        _load_weight(wout_hbm, wout_s, sems.at[1])
        i = pl.program_id(0)

        @pl.when(i == 0)
        def _():
            acc_s[...] = jnp.zeros_like(acc_s)

        t_s[...] = _dot_nt(dgl_ref[...], wpg_s[...])
        g = g_ref[...]

        def rows_body(q, acc):
            rows = _tile_rows(q)
            dh1 = dh2_ref[rows, :] + t_s[rows, :]
            dh1_ref[rows, :] = dh1
            o = ob_ref[rows, :].astype(F32)
            rr = lax.rsqrt(_mean_last(o * o) + EPS)
            on = o * rr
            dog = dh1 * g
            do_ref[rows, :] = (rr * (dog - on * _mean_last(dog * on))).astype(BF16)
            return acc + _fold_rows(dh1 * on)

        acc_s[...] = lax.fori_loop(0, tm // TILE_ROWS, rows_body, acc_s[...])
        dy_ref[...] = _dot_nt(do_ref[...], wout_s[...]).astype(BF16)

        @pl.when(i == n_tile - 1)
        def _():
            dg_ref[...] = jnp.sum(acc_s[...], axis=0, keepdims=True)

    tile = pl.BlockSpec((tm, D_MODEL), lambda i: (i, 0))
    vec = pl.BlockSpec((1, D_MODEL), lambda i: (0, 0))
    hbm = pl.BlockSpec(memory_space=pl.ANY)
    return pl.pallas_call(
        body, name="tail_bwd", grid=(n_tile,),
        in_specs=[tile, tile, tile, hbm, hbm, vec],
        out_specs=[tile, tile, tile, vec],
        out_shape=[SDS((t_len, D_MODEL), F32), SDS((t_len, D_MODEL), BF16), SDS((t_len, D_MODEL), BF16),
                   SDS((1, D_MODEL), F32)],
        scratch_shapes=[pltpu.VMEM((D_MODEL, D_MODEL), BF16), pltpu.VMEM((D_MODEL, D_MODEL), BF16),
                        pltpu.VMEM((tm, D_MODEL), F32), pltpu.VMEM((ROWS, D_MODEL), F32), pltpu.SemaphoreType.DMA((2,))],
        compiler_params=_params(("arbitrary",), 48),
    )(dh2, dgl, ob, w_pg, w_out, post_g)


def _mix_bwd(z, dy, h, ln_g, ln_b, wm, wm_t, bias, cw, cb, wax, wax_t, ba, bx, lam, goa, gob):
    t_len = z.shape[0]
    n_chunk = t_len // CHUNK
    halo_blocks = CHUNK // ROWS

    def body(z_ref, zhalo_ref, dy_ref, h_ref, hhalo_ref, lng_ref, lnb_ref, wm_ref, wmt_ref, bias_ref, cw_ref, cb_ref,
             wax_ref, waxt_ref, ba_ref, bx_ref, lam_ref, goa_ref, gob_ref,
             dz_ref, vecs_ref, dws_ref, dwax_ref, dbs_ref,
             vn_s, vh_s, rs_s, xc_s, mixed_s, pre_s, dmix_s, dvn_s, dho_s, dxc_s, dpre_s, dz_s, acc_s, accdm_s,
             cg_s, ca_s, dxchalo_s):
        step = pl.program_id(0)
        c_id = n_chunk - 1 - step
        rid = _row_ids(D_BR)
        first_chunk = c_id == 0

        @pl.when(step == 0)
        def _():
            acc_s[...] = jnp.zeros_like(acc_s)
            accdm_s[...] = jnp.zeros_like(accdm_s)
            cg_s[...] = jnp.zeros_like(cg_s)
            ca_s[...] = jnp.zeros_like(ca_s)
            dxchalo_s[...] = jnp.zeros_like(dxchalo_s)
            dws_ref[...] = jnp.zeros_like(dws_ref)
            dwax_ref[...] = jnp.zeros_like(dwax_ref)

        lng, lnb, cb = lng_ref[...], lnb_ref[...], cb_ref[...]
        xb_halo = jnp.where(first_chunk, 0.0, zhalo_ref[...])
        h_halo = jnp.where(first_chunk, 0.0, hhalo_ref[...])

        def prev_rows(ref, cols, g, halo):
            before = ref[pl.ds(pl.multiple_of(jnp.maximum(g - 1, 0) * ROWS, ROWS), ROWS), cols]
            return jnp.where(g > 0, before, halo)

        def phase1(g, prev):
            rows = _rows(g)
            vg, _ = _gelu(z_ref[rows, D_BR:2 * D_BR])
            xm = vg - _mean_last(vg)
            rs = lax.rsqrt(_mean_last(xm * xm) + EPS)
            vh = xm * rs
            vh_s[rows, :] = vh
            rs_s[rows, :] = jnp.broadcast_to(rs, (ROWS, HEAD))
            vn_s[rows, :] = vh * lng + lnb
            xb = z_ref[rows, 3 * D_BR:4 * D_BR]
            xc_s[rows, :] = _conv_rows(xb, prev, cw_ref, cb, rid)
            return xb

        lax.fori_loop(0, N_GROUP, phase1, xb_halo)

        for hd in range(N_HEAD):
            cs = slice(hd * HEAD, (hd + 1) * HEAD)
            mixed_s[:, cs] = _dot(wm_ref[hd], vn_s[:, cs].astype(BF16))
            pre = _dot(xc_s[:, cs].astype(BF16), wax_ref[hd])
            pre_s[:, cs] = pre[:, :HEAD]
            pre_s[:, D_BR + hd * HEAD:D_BR + (hd + 1) * HEAD] = pre[:, HEAD:]

        goa, gob = goa_ref[...], gob_ref[...]

        def phase3(g, _):
            rows = _rows(g)
            u = z_ref[rows, 0:D_BR]
            ug, tu = _gelu(u)
            ga = z_ref[rows, 2 * D_BR:3 * D_BR]
            sga = _sig(ga)
            sa = ga * sga
            mixed = mixed_s[rows, :] + bias_ref[rows, :]
            ya0 = ug * mixed
            ya = ya0 * sa
            ra = lax.rsqrt(_mean_last(ya * ya) + EPS)
            dyan = dy_ref[rows, 0:D_BR].astype(F32)
            acc_s[V_GOUT_A] += dyan * ya * ra
            dyg = dyan * goa
            dya = ra * dyg - ya * (ra * ra * ra) * _mean_last(dyg * ya)
            dya0 = dya * sa
            dz_s[rows, 2 * D_BR:3 * D_BR] = dya * ya0 * (sga * (1.0 + ga * (1.0 - sga)))
            dmix = dya0 * ug
            dmix_s[rows, :] = dmix
            accdm_s[rows, :] += dmix
            dz_s[rows, 0:D_BR] = dya0 * mixed * _gelu_grad(u, tu)

            hh = h_ref[rows, :]
            gb = z_ref[rows, 4 * D_BR:5 * D_BR]
            sgb = _sig(gb)
            sb = gb * sgb
            yb = hh * sb
            rb = lax.rsqrt(_mean_last(yb * yb) + EPS)
            dybn = dy_ref[rows, D_BR:2 * D_BR].astype(F32)
            acc_s[V_GOUT_B] += dybn * yb * rb
            dyg = dybn * gob
            dyb = rb * dyg - yb * (rb * rb * rb) * _mean_last(dyg * yb)
            dho_s[rows, :] = dyb * sb
            dz_s[rows, 4 * D_BR:5 * D_BR] = dyb * hh * (sgb * (1.0 + gb * (1.0 - sgb)))
            return 0

        lax.fori_loop(0, N_GROUP, phase3, 0)

        for hd in range(N_HEAD):
            cs = slice(hd * HEAD, (hd + 1) * HEAD)
            dmb = dmix_s[:, cs].astype(BF16)
            dvn_s[:, cs] = _dot(wmt_ref[hd], dmb)
            dws_ref[hd] += _dot_nt(dmb, vn_s[:, cs].astype(BF16))

        def phase5(g, _):
            rows = _rows(g)
            dvn = dvn_s[rows, :]
            vh = vh_s[rows, :]
            acc_s[V_LN_G] += dvn * vh
            acc_s[V_LN_B] += dvn
            dvh = dvn * lng
            rs = rs_s[rows, 0:1]
            dvg = rs * (dvh - _mean_last(dvh) - vh * _mean_last(dvh * vh))
            v = z_ref[rows, D_BR:2 * D_BR]
            _, tv = _gelu(v)
            dz_s[rows, D_BR:2 * D_BR] = dvg * _gelu_grad(v, tv)
            return 0

        lax.fori_loop(0, N_GROUP, phase5, 0)

        ba, bx = ba_ref[...], bx_ref[...]
        sp8 = LRU_C * _softplus(-lam_ref[...])

        def phase6(k, carry):
            cg, ca = carry
            g = N_GROUP - 1 - k
            rows = _rows(g)
            first_row = jnp.logical_and(jnp.logical_and(first_chunk, g == 0), rid == 0)
            r, i, a, mult = _lru_gates(pre_s[rows, 0:D_BR], pre_s[rows, D_BR:2 * D_BR], ba, bx, sp8, first_row)
            a_nx = jnp.where(rid < ROWS - 1, pltpu.roll(a, ROWS - 1, 0), ca)
            aa, bb = a_nx, dho_s[rows, :]
            for d in (1, 2, 4):
                a_sh = jnp.where(rid < ROWS - d, pltpu.roll(aa, ROWS - d, 0), 1.0)
                b_sh = jnp.where(rid < ROWS - d, pltpu.roll(bb, ROWS - d, 0), 0.0)
                bb = aa * b_sh + bb
                aa = aa * a_sh
            gg = bb + aa * cg
            hh = h_ref[rows, :]
            hprev = _shift_down(hh, prev_rows(h_ref, slice(None), g, h_halo), 1, rid)
            xc = xc_s[rows, :]
            gx = gg * xc
            dla = gg * hprev * a - jnp.where(first_row, 0.0, gx * i * (a * a) / mult)
            acc_s[V_LAM] += -(dla * r)
            dpa = -(dla * sp8) * r * (1.0 - r)
            dpx = gx * mult * i * (1.0 - i)
            acc_s[V_B_A] += dpa
            acc_s[V_B_X] += dpx
            dpre_s[rows, 0:D_BR] = dpa
            dpre_s[rows, D_BR:2 * D_BR] = dpx
            dxc_s[rows, :] = gg * mult * i
            return _bcast_row(gg, 0), _bcast_row(a, 0)

        cg, ca = lax.fori_loop(0, N_GROUP, phase6, (cg_s[...], ca_s[...]))
        cg_s[...] = cg
        ca_s[...] = ca

        for hd in range(N_HEAD):
            cs = slice(hd * HEAD, (hd + 1) * HEAD)
            dpre = jnp.concatenate([dpre_s[:, cs], dpre_s[:, D_BR + hd * HEAD:D_BR + (hd + 1) * HEAD]], axis=1).astype(BF16)
            dxc_s[:, cs] += _dot(dpre, waxt_ref[hd])
            dwax_ref[hd] += _dot_tn(xc_s[:, cs].astype(BF16), dpre)

        def phase8(k, nxt):
            g = N_GROUP - 1 - k
            rows = _rows(g)
            dxc = dxc_s[rows, :]
            acc_s[V_CONV_B] += dxc
            xb = z_ref[rows, 3 * D_BR:4 * D_BR]
            xb_prev = prev_rows(z_ref, slice(3 * D_BR, 4 * D_BR), g, xb_halo)
            dxb = cw_ref[3:4, :] * dxc
            acc_s[V_CONV_W + 3] += dxc * xb
            for j in range(1, CONV_W):
                dxb = dxb + cw_ref[3 - j:4 - j, :] * _shift_up(dxc, nxt, j, rid)
                acc_s[V_CONV_W + 3 - j] += dxc * _shift_down(xb, xb_prev, j, rid)
            dz_s[rows, 3 * D_BR:4 * D_BR] = dxb
            return dxc

        dxchalo_s[...] = lax.fori_loop(0, N_GROUP, phase8, dxchalo_s[...])
        dz_ref[...] = dz_s[...].astype(BF16)

        @pl.when(step == n_chunk - 1)
        def _():
            for v in range(N_VEC):
                vecs_ref[v:v + 1, :] = jnp.sum(acc_s[v], axis=0, keepdims=True)
            lam = lam_ref[...]
            vecs_ref[V_LAM:V_LAM + 1, :] = vecs_ref[V_LAM:V_LAM + 1, :] * (-LRU_C * _sig(-lam))
            tril = (lax.broadcasted_iota(jnp.int32, (HEAD, HEAD), 0) >= lax.broadcasted_iota(jnp.int32, (HEAD, HEAD), 1))
            ones = jnp.ones((ROWS, HEAD), BF16)
            for hd in range(N_HEAD):
                cs = slice(hd * HEAD, (hd + 1) * HEAD)
                dws_ref[hd] = jnp.where(tril, dws_ref[hd], 0.0)
                blk = accdm_s[:, cs]
                hi = blk.astype(BF16)
                lo = (blk - hi.astype(F32)).astype(BF16)
                dbs_ref[hd:hd + 1, :] = (_dot_nt(ones, hi) + _dot_nt(ones, lo))[0:1, :]

    vec = pl.BlockSpec((1, D_BR), lambda i: (0, 0))
    rev = lambda i: (n_chunk - 1 - i, 0)
    halo = lambda col: (lambda i: (jnp.maximum((n_chunk - 1 - i) * halo_blocks - 1, 0), col))
    full3 = lambda a, b, c: pl.BlockSpec((a, b, c), lambda i: (0, 0, 0))
    big = lambda w: pltpu.VMEM((CHUNK, w), F32)
    return pl.pallas_call(
        body, name="mix_bwd", grid=(n_chunk,),
        in_specs=[pl.BlockSpec((CHUNK, D_IN), rev), pl.BlockSpec((ROWS, D_BR), halo(3)),
                  pl.BlockSpec((CHUNK, 2 * D_BR), rev), pl.BlockSpec((CHUNK, D_BR), rev),
                  pl.BlockSpec((ROWS, D_BR), halo(0)), vec, vec,
                  full3(N_HEAD, HEAD, HEAD), full3(N_HEAD, HEAD, HEAD),
                  pl.BlockSpec((CHUNK, D_BR), lambda i: (0, 0)), pl.BlockSpec((ROWS, D_BR), lambda i: (0, 0)), vec,
                  full3(N_HEAD, HEAD, 2 * HEAD), full3(N_HEAD, 2 * HEAD, HEAD), vec, vec, vec, vec, vec],
        out_specs=[pl.BlockSpec((CHUNK, D_IN), rev), pl.BlockSpec((N_VEC, D_BR), lambda i: (0, 0)),
                   full3(N_HEAD, HEAD, HEAD), full3(N_HEAD, HEAD, 2 * HEAD),
                   pl.BlockSpec((N_HEAD, HEAD), lambda i: (0, 0))],
        out_shape=[SDS((t_len, D_IN), BF16), SDS((N_VEC, D_BR), F32), SDS((N_HEAD, HEAD, HEAD), F32),
                   SDS((N_HEAD, HEAD, 2 * HEAD), F32), SDS((N_HEAD, HEAD), F32)],
        scratch_shapes=[big(D_BR), big(D_BR), big(HEAD), big(D_BR), big(D_BR), big(2 * D_BR), big(D_BR), big(D_BR),
                        big(D_BR), big(D_BR), big(2 * D_BR), big(D_IN),
                        pltpu.VMEM((N_VEC, ROWS, D_BR), F32), big(D_BR),
                        pltpu.VMEM((ROWS, D_BR), F32), pltpu.VMEM((ROWS, D_BR), F32), pltpu.VMEM((ROWS, D_BR), F32)],
        compiler_params=_params(("arbitrary",), 48),
    )(z, z, dy, h, h, ln_g, ln_b, wm, wm_t, bias, cw, cb, wax, wax_t, ba, bx, lam, goa, gob)


def _in_bwd(dz, w_in_g, x, dh1, pre_g, tm=512):
    t_len = x.shape[0]
    n_tile = t_len // tm

    def body(dz_ref, w_ref, x_ref, dh1_ref, g_ref, gx_ref, dg_ref, acc_s, dg_s):
        i, k = pl.program_id(0), pl.program_id(1)

        @pl.when(jnp.logical_and(i == 0, k == 0))
        def _():
            dg_s[...] = jnp.zeros_like(dg_s)

        part = _dot_nt(dz_ref[...], w_ref[...])

        @pl.when(k == 0)
        def _():
            acc_s[...] = part

        @pl.when(k > 0)
        def _():
            acc_s[...] += part

        @pl.when(k == N_DEV - 1)
        def _():
            g = g_ref[...]

            def rows_body(q, acc):
                rows = _tile_rows(q)
                xv = x_ref[rows, :]
                r = lax.rsqrt(_mean_last(xv * xv) + EPS)
                xh = xv * r
                dhn = acc_s[rows, :]
                dg = dhn * g
                gx_ref[rows, :] = dh1_ref[rows, :] + r * (dg - xh * _mean_last(dg * xh))
                return acc + _fold_rows(dhn * xh)

            dg_s[...] = lax.fori_loop(0, tm // TILE_ROWS, rows_body, dg_s[...])

        @pl.when(jnp.logical_and(i == n_tile - 1, k == N_DEV - 1))
        def _():
            dg_ref[...] = jnp.sum(dg_s[...], axis=0, keepdims=True)

    tile = pl.BlockSpec((tm, D_MODEL), lambda i, k: (i, 0))
    vec = pl.BlockSpec((1, D_MODEL), lambda i, k: (0, 0))
    return pl.pallas_call(
        body, name="in_bwd", grid=(n_tile, N_DEV),
        in_specs=[pl.BlockSpec((tm, W_IN_SHARD), lambda i, k: (i, k)),
                  pl.BlockSpec((None, D_MODEL, W_IN_SHARD), lambda i, k: (k, 0, 0)), tile, tile, vec],
        out_specs=[tile, vec],
        out_shape=[SDS((t_len, D_MODEL), F32), SDS((1, D_MODEL), F32)],
        scratch_shapes=[pltpu.VMEM((tm, D_MODEL), F32), pltpu.VMEM((ROWS, D_MODEL), F32)],
        compiler_params=_params(("arbitrary", "arbitrary"), 48),
    )(dz, w_in_g, x, dh1, pre_g)


def _grad_w(a, b, bn, shard_major, name, tk=512):
    t_len, m = a.shape
    n = b.shape[1]
    n_k = t_len // tk

    def body(a_ref, b_ref, o_ref, acc_s):
        k = pl.program_id(1)
        part = _dot_tn(a_ref[...], b_ref[...])

        @pl.when(k == 0)
        def _():
            acc_s[...] = part

        @pl.when(k > 0)
        def _():
            acc_s[...] += part

        @pl.when(k == n_k - 1)
        def _():
            o_ref[...] = acc_s[...]

    if shard_major:
        out_spec, out_shape = pl.BlockSpec((None, m, bn), lambda j, k: (j, 0, 0)), SDS((n // bn, m, bn), F32)
    else:
        out_spec, out_shape = pl.BlockSpec((m, bn), lambda j, k: (0, j)), SDS((m, n), F32)
    return pl.pallas_call(
        body, name=name, grid=(n // bn, n_k),
        in_specs=[pl.BlockSpec((tk, m), lambda j, k: (k, 0)), pl.BlockSpec((tk, bn), lambda j, k: (k, j))],
        out_specs=out_spec, out_shape=out_shape,
        scratch_shapes=[pltpu.VMEM((m, bn), F32)],
        compiler_params=_params(("arbitrary", "arbitrary"), 40),
    )(a, b)


def _adamw(parts, w, m, v, name, tr):
    rows, cols = w.shape
    c1 = 1.0 - ADAM_B1 ** ADAM_STEP
    c2 = 1.0 - ADAM_B2 ** ADAM_STEP

    def body(p_ref, w_ref, m_ref, v_ref, g_ref, d_ref, nm_ref, nv_ref):
        g = p_ref[0]
        for s in range(1, N_DEV):
            g = g + p_ref[s]
        g_ref[...] = g
        nm = ADAM_B1 * m_ref[...] + (1.0 - ADAM_B1) * g
        nv = ADAM_B2 * v_ref[...] + (1.0 - ADAM_B2) * (g * g)
        nm_ref[...] = nm
        nv_ref[...] = nv
        d_ref[...] = -ADAM_LR * ((nm / c1) / (jnp.sqrt(nv / c2) + ADAM_EPS) + ADAM_WD * w_ref[...])

    tile = pl.BlockSpec((tr, cols), lambda i: (i, 0))
    return pl.pallas_call(
        body, name=name, grid=(rows // tr,),
        in_specs=[pl.BlockSpec((N_DEV, tr, cols), lambda i: (0, i, 0)), tile, tile, tile],
        out_specs=[tile] * 4, out_shape=[SDS((rows, cols), F32)] * 4,
        compiler_params=_params(("arbitrary",), 40),
    )(parts, w, m, v)


SMALL = ("pre_g", "gmlp_ln_g", "gmlp_ln_b", "gmlp_ws", "gmlp_bs", "conv_b", "w_a", "b_a", "w_x", "b_x", "lam",
         "gmlp_out_g", "lru_out_g", "post_g")
WEIGHTS = ("pre_g", "w_in", "gmlp_ln_g", "gmlp_ln_b", "gmlp_ws", "gmlp_bs", "conv_w", "conv_b", "w_a", "b_a", "w_x",
           "b_x", "lam", "gmlp_out_g", "lru_out_g", "w_out", "post_g", "w_pe", "w_pg")
LANES = 128


PACK_ROWS = 3200
PACK_TILE = 640


def _pack(parts):
    rows = [p.reshape(-1, LANES) for p in parts]
    used = sum(r.shape[0] for r in rows)
    return jnp.concatenate(rows + [jnp.zeros((PACK_ROWS - used, LANES), F32)], axis=0)


def _local_step(x, p, tgt, sm, w_in_g, w_out, w_pe_g, w_pg, cw_full):
    vec = lambda a: a.reshape(1, -1)
    tril = jnp.tril(jnp.ones((CHUNK, CHUNK), dtype=bool))
    wm32 = jnp.where(tril[None], sm["gmlp_ws"], 0.0)
    wm, wm_t = wm32.astype(BF16), jnp.swapaxes(wm32, 1, 2).astype(BF16)
    bias = jnp.repeat(sm["gmlp_bs"].T, HEAD, axis=1)
    wax32 = jnp.concatenate([sm["w_a"], sm["w_x"]], axis=2)
    wax, wax_t = wax32.astype(BF16), jnp.swapaxes(wax32, 1, 2).astype(BF16)
    cw8 = jnp.concatenate([cw_full, jnp.zeros((ROWS - CONV_W, D_BR), F32)], axis=0)
    mixer_consts = dict(cw=cw8, cb=vec(sm["conv_b"]), ba=vec(sm["b_a"]), bx=vec(sm["b_x"]), lam=vec(sm["lam"]),
                        goa=vec(sm["gmlp_out_g"]), gob=vec(sm["lru_out_g"]))
    ln_g, ln_b = vec(sm["gmlp_ln_g"]), vec(sm["gmlp_ln_b"])
    pre_g, post_g = vec(sm["pre_g"]), vec(sm["post_g"])

    z, hn = _in_proj(x, pre_g, w_in_g)
    y, h = _mix_fwd(z, ln_g, ln_b, wm, bias, wax=wax, **mixer_consts)
    h1, ob = _out_proj(y, x, w_out, post_g)
    dh2, dpe, dgl, h1b, loss = _ple_loss(h1, p, tgt, w_pg, w_pe_g)
    dh1, do, dy, d_post_g = _tail_bwd(dh2, dgl, ob, w_pg, w_out, post_g)
    dz, vecs, d_ws, d_wax, d_bs = _mix_bwd(z, dy, h, ln_g, ln_b, wm, wm_t, bias, wax=wax, wax_t=wax_t, **mixer_consts)
    grad_x, d_pre_g = _in_bwd(dz, w_in_g, x, dh1, pre_g)

    d_w_in = _grad_w(hn, dz, W_IN_SHARD, True, "grad_w_in")
    d_w_out = _grad_w(y, do, 512, False, "grad_w_out")
    d_w_pg = _grad_w(h1b, dgl, 512, False, "grad_w_pg")
    d_w_pe = _grad_w(p.astype(BF16), dpe, D_MODEL // N_DEV, True, "grad_w_pe")

    small = {"pre_g": d_pre_g, "gmlp_ln_g": vecs[V_LN_G], "gmlp_ln_b": vecs[V_LN_B], "gmlp_ws": d_ws, "gmlp_bs": d_bs,
             "conv_b": vecs[V_CONV_B], "w_a": d_wax[:, :, :HEAD], "b_a": vecs[V_B_A], "w_x": d_wax[:, :, HEAD:],
             "b_x": vecs[V_B_X], "lam": vecs[V_LAM], "gmlp_out_g": vecs[V_GOUT_A], "lru_out_g": vecs[V_GOUT_B],
             "post_g": d_post_g}
    d_cw = vecs[V_CONV_W:V_CONV_W + CONV_W]
    return loss[0, 0], grad_x, small, d_w_in, d_w_out, d_w_pe, d_w_pg, d_cw


def kernel(x, p, pre_g, w_in, gmlp_ln_g, gmlp_ln_b, gmlp_ws, gmlp_bs, conv_w, conv_b, w_a, b_a, w_x, b_x, lam, gmlp_out_g, lru_out_g, w_out, post_g, w_pe, w_pg, loss_target, m_pre_g, m_w_in, m_gmlp_ln_g, m_gmlp_ln_b, m_gmlp_ws, m_gmlp_bs, m_conv_w, m_conv_b, m_w_a, m_b_a, m_w_x, m_b_x, m_lam, m_gmlp_out_g, m_lru_out_g, m_w_out, m_post_g, m_w_pe, m_w_pg, v_pre_g, v_w_in, v_gmlp_ln_g, v_gmlp_ln_b, v_gmlp_ws, v_gmlp_bs, v_conv_w, v_conv_b, v_w_a, v_b_a, v_w_x, v_b_x, v_lam, v_gmlp_out_g, v_lru_out_g, v_w_out, v_post_g, v_w_pe, v_w_pg):
    args = dict(locals())
    weights = {n: args[n] for n in WEIGHTS}
    m_in = {n: args["m_" + n] for n in WEIGHTS}
    v_in = {n: args["v_" + n] for n in WEIGHTS}
    sm = {n: weights[n][0] for n in SMALL}
    shard_rows = D_MODEL // N_DEV

    cw_shard = jnp.concatenate([conv_w.reshape(CONV_W, HEAD), jnp.zeros((ROWS - CONV_W, HEAD), F32)], axis=0)
    w_in_g, w_out_g, w_pe_g, w_pg_g, cw_g = _exchange(
        [w_in[0].astype(BF16), w_out[0].astype(BF16), w_pe[0].astype(BF16), w_pg[0].astype(BF16), cw_shard],
        False, "gather_weights")
    cw_full = jnp.transpose(cw_g[:, :CONV_W, :], (1, 0, 2)).reshape(CONV_W, D_BR)

    loss_part, grad_x, small, d_w_in, d_w_out, d_w_pe, d_w_pg, d_cw = _local_step(
        x[0], p[0, 0], loss_target[0], sm, w_in_g, w_out_g.reshape(D_MODEL, D_MODEL), w_pe_g,
        w_pg_g.reshape(D_MODEL, D_MODEL), cw_full)
    loss = lax.psum(loss_part, MESH_AXES)

    d_cw_blocks = jnp.transpose(d_cw.reshape(CONV_W, N_DEV, HEAD), (1, 0, 2))
    d_cw_blocks = jnp.concatenate([d_cw_blocks, jnp.zeros((N_DEV, ROWS - CONV_W, HEAD), F32)], axis=1)
    parts = _exchange(
        [d_w_in, d_w_out.reshape(N_DEV, shard_rows, D_MODEL), d_w_pe, d_w_pg.reshape(N_DEV, shard_rows, D_MODEL),
         d_cw_blocks], True, "scatter_grads")
    pad_cw = lambda a: jnp.concatenate([a.reshape(CONV_W, HEAD), jnp.zeros((ROWS - CONV_W, HEAD), F32)], axis=0)
    big = {
        "w_in": _adamw(parts[0], w_in[0], m_w_in[0], v_w_in[0], "adamw_w_in", 256),
        "w_out": _adamw(parts[1], w_out[0], m_w_out[0], v_w_out[0], "adamw_w_out", 128),
        "w_pe": _adamw(parts[2], w_pe[0], m_w_pe[0], v_w_pe[0], "adamw_w_pe", 256),
        "w_pg": _adamw(parts[3], w_pg[0], m_w_pg[0], v_w_pg[0], "adamw_w_pg", 128),
        "conv_w": [a[:CONV_W] for a in
                   _adamw(parts[4], pad_cw(conv_w), pad_cw(m_conv_w), pad_cw(v_conv_w), "adamw_conv_w", ROWS)],
    }

    small_parts = _exchange([_pack([small[n] for n in SMALL])], False, "gather_small_grads")[0]
    packed = _adamw(small_parts, _pack([weights[n] for n in SMALL]), _pack([m_in[n] for n in SMALL]),
                    _pack([v_in[n] for n in SMALL]), "adamw_small", PACK_TILE)
    outs = {n: [None] * 4 for n in WEIGHTS}
    row = 0
    for n in SMALL:
        n_rows = weights[n].size // LANES
        for q in range(4):
            outs[n][q] = packed[q][row:row + n_rows].reshape(weights[n].shape)
        row += n_rows
    for n, res in big.items():
        outs[n] = [r.reshape(weights[n].shape) for r in res]

    result = [loss, grad_x[None]]
    for q in range(4):
        result += [outs[n][q] for n in WEIGHTS]
    return tuple(result)
```

```python
import functools

import jax
import jax.numpy as jnp
from jax import lax
from jax.experimental import pallas as pl
from jax.experimental.pallas import tpu as pltpu

F32 = jnp.float32
BF16 = jnp.bfloat16
SDS = jax.ShapeDtypeStruct

D_MODEL = 2048
D_BR = 1024
D_IN = 5 * D_BR
D_PLE = 256
N_HEAD = 8
HEAD = 128
CHUNK = 128
ROWS = 8
N_GROUP = CHUNK // ROWS
N_DEV = 8
W_IN_SHARD = D_IN // N_DEV
EPS = 1e-6
LRU_C = 8.0
CONV_W = 4
MESH_AXES = ("x", "y", "c")
MIB = 1 << 20

ADAM_LR, ADAM_B1, ADAM_B2, ADAM_EPS, ADAM_WD, ADAM_STEP = 0.001, 0.9, 0.999, 1e-08, 0.01, 10

_GELU_C = 0.7978845608028654
_GELU_A = 0.044715

V_LN_G, V_LN_B, V_CONV_B, V_B_A, V_B_X, V_LAM, V_GOUT_A, V_GOUT_B, V_CONV_W = 0, 1, 2, 3, 4, 5, 6, 7, 8
N_VEC = 16


def _params(sem, vmem_mib):
    return pltpu.CompilerParams(dimension_semantics=sem, vmem_limit_bytes=int(vmem_mib * MIB))


def _sig(x):
    return 1.0 / (1.0 + jnp.exp(-x))


def _gelu(x):
    t = jnp.tanh(_GELU_C * (x + _GELU_A * x * x * x))
    return 0.5 * x * (1.0 + t), t


def _gelu_grad(x, t):
    return 0.5 * (1.0 + t) + 0.5 * x * (1.0 - t * t) * (_GELU_C * (1.0 + 3.0 * _GELU_A * x * x))


def _expm1_nonpos(y):
    poly = y * (1.0 + y * 0.5 * (1.0 + y * (1.0 / 3.0) * (1.0 + y * 0.25 * (1.0 + y * 0.2 * (
        1.0 + y * (1.0 / 6.0) * (1.0 + y * (1.0 / 7.0)))))))
    return jnp.where(y > -0.3, poly, jnp.exp(y) - 1.0)


def _softplus(x):
    return jnp.maximum(x, 0.0) + jnp.log(1.0 + jnp.exp(-jnp.abs(x)))


def _row_ids(width):
    return lax.broadcasted_iota(jnp.int32, (ROWS, width), 0)


def _shift_down(cur, prev, k, rid):
    return jnp.where(rid >= k, pltpu.roll(cur, k, 0), pltpu.roll(prev, k, 0))


def _shift_up(cur, nxt, k, rid):
    return jnp.where(rid < ROWS - k, pltpu.roll(cur, ROWS - k, 0), pltpu.roll(nxt, ROWS - k, 0))


def _mean_last(x):
    return jnp.mean(x, axis=-1, keepdims=True)


def _rows(g):
    return pl.ds(pl.multiple_of(g * ROWS, ROWS), ROWS)


TILE_ROWS = 16


def _tile_rows(q):
    return pl.ds(pl.multiple_of(q * TILE_ROWS, TILE_ROWS), TILE_ROWS)


UNROLL = 4


def _loop(n, body, init, unroll=UNROLL):
    def wide(i, carry):
        for u in range(unroll):
            carry = body(i * unroll + u, carry)
        return carry

    return lax.fori_loop(0, n // unroll, wide, init)


def _fold_rows(x):
    return x[0:ROWS, :] + x[ROWS:TILE_ROWS, :]


def _bcast_row(x, r):
    return jnp.broadcast_to(x[r:r + 1, :], x.shape)


def _dot(a, b):
    return jnp.dot(a, b, preferred_element_type=F32)


def _dot_nt(a, b):
    return lax.dot_general(a, b, (((1,), (1,)), ((), ())), preferred_element_type=F32)


def _dot_tn(a, b):
    return lax.dot_general(a, b, (((0,), (0,)), ((), ())), preferred_element_type=F32)


def _mesh_place():
    x, y, c = lax.axis_index("x"), lax.axis_index("y"), lax.axis_index("c")
    return x, y, c, 4 * x + 2 * y + c


def _peer(x, y, c, k):
    px = 1 - x if k & 4 else x
    py = 1 - y if k & 2 else y
    pc = 1 - c if k & 1 else c
    return (px, py, pc), 4 * px + 2 * py + pc


def _exchange(arrs, scatter, name):
    n = len(arrs)

    def body(*refs):
        ins, outs = refs[:n], refs[n:2 * n]
        send_sems, recv_sems, local_sems = refs[2 * n:]
        x, y, c, me = _mesh_place()
        local = []
        for a in range(n):
            src = ins[a].at[me] if scatter else ins[a]
            cp = pltpu.make_async_copy(src, outs[a].at[me], local_sems.at[a])
            cp.start()
            local.append(cp)
        sends = []
        for k in range(1, N_DEV):
            dev, lin = _peer(x, y, c, k)
            for a in range(n):
                src = ins[a].at[lin] if scatter else ins[a]
                cp = pltpu.make_async_remote_copy(
                    src_ref=src, dst_ref=outs[a].at[me], send_sem=send_sems.at[a * N_DEV + k], recv_sem=recv_sems.at[a * N_DEV + k],
                    device_id=dev, device_id_type=pl.DeviceIdType.MESH)
                cp.start()
                sends.append(cp)
        for k in range(1, N_DEV):
            dev, lin = _peer(x, y, c, k)
            for a in range(n):
                src = ins[a].at[lin] if scatter else ins[a]
                pltpu.make_async_remote_copy(
                    src_ref=src, dst_ref=outs[a].at[lin], send_sem=send_sems.at[a * N_DEV + k], recv_sem=recv_sems.at[a * N_DEV + k],
                    device_id=dev, device_id_type=pl.DeviceIdType.MESH).wait_recv()
        for cp in sends:
            cp.wait_send()
        for cp in local:
            cp.wait()

    any_spec = pl.BlockSpec(memory_space=pl.ANY)
    out_shape = [SDS(a.shape if scatter else (N_DEV,) + a.shape, a.dtype) for a in arrs]
    return pl.pallas_call(
        body, name=name, out_shape=out_shape,
        in_specs=[any_spec] * n, out_specs=[any_spec] * n,
        scratch_shapes=[pltpu.SemaphoreType.DMA((n * N_DEV,)), pltpu.SemaphoreType.DMA((n * N_DEV,)),
                        pltpu.SemaphoreType.DMA((n,))],
    )(*arrs)


def _in_proj(x, pre_g, w_in_g, tm=512):
    t_len = x.shape[0]

    def body(x_ref, g_ref, w_ref, z_ref, hn_ref):
        @pl.when(pl.program_id(1) == 0)
        def _():
            g = g_ref[...]

            def rows_body(q, _):
                rows = _tile_rows(q)
                xv = x_ref[rows, :]
                hn_ref[rows, :] = (xv * lax.rsqrt(_mean_last(xv * xv) + EPS) * g).astype(BF16)
                return 0

            _loop(tm // TILE_ROWS, rows_body, 0)

        z_ref[...] = _dot(hn_ref[...], w_ref[...])

    return pl.pallas_call(
        body, name="in_proj", grid=(t_len // tm, N_DEV),
        in_specs=[pl.BlockSpec((tm, D_MODEL), lambda i, j: (i, 0)),
                  pl.BlockSpec((1, D_MODEL), lambda i, j: (0, 0)),
                  pl.BlockSpec((None, D_MODEL, W_IN_SHARD), lambda i, j: (j, 0, 0))],
        out_specs=[pl.BlockSpec((tm, W_IN_SHARD), lambda i, j: (i, j)),
                   pl.BlockSpec((tm, D_MODEL), lambda i, j: (i, 0))],
        out_shape=[SDS((t_len, D_IN), F32), SDS((t_len, D_MODEL), BF16)],
        compiler_params=_params(("arbitrary", "arbitrary"), 40),
    )(x, pre_g, w_in_g)


def _conv_rows(cur, prev, cw_ref, cb, rid):
    acc = cw_ref[3:4, :] * cur + cb
    for k in range(1, CONV_W):
        acc = acc + cw_ref[3 - k:4 - k, :] * _shift_down(cur, prev, k, rid)
    return acc


def _lru_gates(pa, px, ba, bx, sp8, first_row):
    r = _sig(pa + ba)
    i = _sig(px + bx)
    la = -(r * sp8)
    a = jnp.exp(la)
    mult = jnp.sqrt(-_expm1_nonpos(2.0 * la))
    mult = jnp.where(first_row, 1.0, mult)
    return r, i, a, mult


def _mix_fwd(z, ln_g, ln_b, wm, bias, cw, cb, wax, ba, bx, lam, goa, gob):
    t_len = z.shape[0]
    n_chunk = t_len // CHUNK

    def body(z_ref, lng_ref, lnb_ref, wm_ref, bias_ref, cw_ref, cb_ref, wax_ref, ba_ref, bx_ref, lam_ref, goa_ref,
             gob_ref, y_ref, h_ref, vn_s, xc_s, mixed_s, pre_s, y_s, carry_s, halo_s):
        c_id = pl.program_id(0)
        rid = _row_ids(D_BR)

        @pl.when(c_id == 0)
        def _():
            carry_s[...] = jnp.zeros_like(carry_s)
            halo_s[...] = jnp.zeros_like(halo_s)

        lng, lnb, cb = lng_ref[...], lnb_ref[...], cb_ref[...]

        def phase1(g, prev):
            rows = _rows(g)
            vg, _ = _gelu(z_ref[rows, D_BR:2 * D_BR])
            xm = vg - _mean_last(vg)
            rs = lax.rsqrt(_mean_last(xm * xm) + EPS)
            vn_s[rows, :] = xm * rs * lng + lnb
            xb = z_ref[rows, 3 * D_BR:4 * D_BR]
            xc_s[rows, :] = _conv_rows(xb, prev, cw_ref, cb, rid)
            return xb

        halo_s[...] = _loop(N_GROUP, phase1, halo_s[...])

        for h in range(N_HEAD):
            cs = slice(h * HEAD, (h + 1) * HEAD)
            mixed_s[:, cs] = _dot(wm_ref[h], vn_s[:, cs].astype(BF16))
            pre = _dot(xc_s[:, cs].astype(BF16), wax_ref[h])
            pre_s[:, cs] = pre[:, :HEAD]
            pre_s[:, D_BR + h * HEAD:D_BR + (h + 1) * HEAD] = pre[:, HEAD:]

        ba, bx, goa, gob = ba_ref[...], bx_ref[...], goa_ref[...], gob_ref[...]
        sp8 = LRU_C * _softplus(-lam_ref[...])

        def phase3(g, carry):
            rows = _rows(g)
            ug, _ = _gelu(z_ref[rows, 0:D_BR])
            ga = z_ref[rows, 2 * D_BR:3 * D_BR]
            ya = ug * (mixed_s[rows, :] + bias_ref[rows, :]) * (ga * _sig(ga))
            y_s[rows, 0:D_BR] = ya * lax.rsqrt(_mean_last(ya * ya) + EPS) * goa

            first_row = jnp.logical_and(jnp.logical_and(c_id == 0, g == 0), rid == 0)
            _, i, a, mult = _lru_gates(pre_s[rows, 0:D_BR], pre_s[rows, D_BR:2 * D_BR], ba, bx, sp8, first_row)
            b = mult * i * xc_s[rows, :]
            for d in (1, 2, 4):
                a_sh = jnp.where(rid >= d, pltpu.roll(a, d, 0), 1.0)
                b_sh = jnp.where(rid >= d, pltpu.roll(b, d, 0), 0.0)
                b = a * b_sh + b
                a = a * a_sh
            hh = b + a * carry
            h_ref[rows, :] = hh
            gb = z_ref[rows, 4 * D_BR:5 * D_BR]
            yb = hh * (gb * _sig(gb))
            y_s[rows, D_BR:2 * D_BR] = yb * lax.rsqrt(_mean_last(yb * yb) + EPS) * gob
            return _bcast_row(hh, ROWS - 1)

        carry_s[...] = _loop(N_GROUP, phase3, carry_s[...])
        y_ref[...] = y_s[...].astype(BF16)

    vec = pl.BlockSpec((1, D_BR), lambda i: (0, 0))
    return pl.pallas_call(
        body, name="mix_fwd", grid=(n_chunk,),
        in_specs=[pl.BlockSpec((CHUNK, D_IN), lambda i: (i, 0)), vec, vec,
                  pl.BlockSpec((N_HEAD, HEAD, HEAD), lambda i: (0, 0, 0)),
                  pl.BlockSpec((CHUNK, D_BR), lambda i: (0, 0)),
                  pl.BlockSpec((ROWS, D_BR), lambda i: (0, 0)), vec,
                  pl.BlockSpec((N_HEAD, HEAD, 2 * HEAD), lambda i: (0, 0, 0)), vec, vec, vec, vec, vec],
        out_specs=[pl.BlockSpec((CHUNK, 2 * D_BR), lambda i: (i, 0)), pl.BlockSpec((CHUNK, D_BR), lambda i: (i, 0))],
        out_shape=[SDS((t_len, 2 * D_BR), BF16), SDS((t_len, D_BR), F32)],
        scratch_shapes=[pltpu.VMEM((CHUNK, D_BR), F32), pltpu.VMEM((CHUNK, D_BR), F32), pltpu.VMEM((CHUNK, D_BR), F32),
                        pltpu.VMEM((CHUNK, 2 * D_BR), F32), pltpu.VMEM((CHUNK, 2 * D_BR), F32),
                        pltpu.VMEM((ROWS, D_BR), F32), pltpu.VMEM((ROWS, D_BR), F32)],
        compiler_params=_params(("arbitrary",), 32),
    )(z, ln_g, ln_b, wm, bias, cw, cb, wax, ba, bx, lam, goa, gob)


def _load_weight(w_hbm, w_vmem, sem):
    @pl.when(pl.program_id(0) == 0)
    def _():
        cp = pltpu.make_async_copy(w_hbm, w_vmem, sem)
        cp.start()
        cp.wait()


def _out_proj(y, x, w_out, post_g, tm=512):
    t_len = y.shape[0]

    def body(y_ref, x_ref, w_hbm, g_ref, h1_ref, ob_ref, w_s, o_s, sem):
        _load_weight(w_hbm, w_s, sem)
        o_s[...] = _dot(y_ref[...], w_s[...])
        g = g_ref[...]

        def rows_body(q, _):
            rows = _tile_rows(q)
            o = o_s[rows, :]
            h1_ref[rows, :] = x_ref[rows, :] + o * lax.rsqrt(_mean_last(o * o) + EPS) * g
            ob_ref[rows, :] = o.astype(BF16)
            return 0

        _loop(tm // TILE_ROWS, rows_body, 0)

    tile = pl.BlockSpec((tm, D_MODEL), lambda i: (i, 0))
    return pl.pallas_call(
        body, name="out_proj", grid=(t_len // tm,),
        in_specs=[tile, tile, pl.BlockSpec(memory_space=pl.ANY), pl.BlockSpec((1, D_MODEL), lambda i: (0, 0))],
        out_specs=[tile, tile],
        out_shape=[SDS((t_len, D_MODEL), F32), SDS((t_len, D_MODEL), BF16)],
        scratch_shapes=[pltpu.VMEM((D_MODEL, D_MODEL), BF16), pltpu.VMEM((tm, D_MODEL), F32), pltpu.SemaphoreType.DMA],
        compiler_params=_params(("arbitrary",), 44),
    )(y, x, w_out, post_g)


def _ple_loss(h1, p, tgt, w_pg, w_pe_g, tm=256):
    t_len = h1.shape[0]
    n_tile = t_len // tm
    pe_shard = D_MODEL // N_DEV

    def body(h1_ref, p_ref, t_ref, w_hbm, wpe_ref, dh2_ref, dpe_ref, dgl_ref, h1b_ref, loss_ref, w_s, pe_s, gl_s, acc_s,
             sem):
        _load_weight(w_hbm, w_s, sem)
        i = pl.program_id(0)

        @pl.when(i == 0)
        def _():
            acc_s[...] = jnp.zeros_like(acc_s)

        h1b_ref[...] = h1_ref[...].astype(BF16)
        pb = p_ref[...].astype(BF16)
        for j in range(N_DEV):
            pe_s[:, j * pe_shard:(j + 1) * pe_shard] = _dot(pb, wpe_ref[j])
        gl_s[...] = _dot(h1b_ref[...], w_s[...])

        def rows_body(q, acc):
            rows = _tile_rows(q)
            pe = pe_s[rows, :]
            g = _sig(gl_s[rows, :])
            e = h1_ref[rows, :] + pe * g - t_ref[rows, :]
            dh2 = e * (1.0 / D_MODEL)
            dh2_ref[rows, :] = dh2
            dpe_ref[rows, :] = (dh2 * g).astype(BF16)
            dgl_ref[rows, :] = (dh2 * pe * g * (1.0 - g)).astype(BF16)
            return acc + _fold_rows(e * e)

        acc_s[...] = _loop(tm // TILE_ROWS, rows_body, acc_s[...])

        @pl.when(i == n_tile - 1)
        def _():
            loss_ref[...] = jnp.full(loss_ref.shape, 0.5 / D_MODEL * jnp.sum(acc_s[...]), F32)

    tile = pl.BlockSpec((tm, D_MODEL), lambda i: (i, 0))
    return pl.pallas_call(
        body, name="ple_loss", grid=(n_tile,),
        in_specs=[tile, pl.BlockSpec((tm, D_PLE), lambda i: (i, 0)), tile, pl.BlockSpec(memory_space=pl.ANY),
                  pl.BlockSpec((N_DEV, D_PLE, pe_shard), lambda i: (0, 0, 0))],
        out_specs=[tile, tile, tile, tile, pl.BlockSpec((ROWS, HEAD), lambda i: (0, 0))],
        out_shape=[SDS((t_len, D_MODEL), F32), SDS((t_len, D_MODEL), BF16), SDS((t_len, D_MODEL), BF16),
                   SDS((t_len, D_MODEL), BF16), SDS((ROWS, HEAD), F32)],
        scratch_shapes=[pltpu.VMEM((D_MODEL, D_MODEL), BF16), pltpu.VMEM((tm, D_MODEL), F32),
                        pltpu.VMEM((tm, D_MODEL), F32), pltpu.VMEM((ROWS, D_MODEL), F32), pltpu.SemaphoreType.DMA],
        compiler_params=_params(("arbitrary",), 44),
    )(h1, p, tgt, w_pg, w_pe_g)


def _tail_bwd(dh2, dgl, ob, w_pg, w_out, post_g, tm=256):
    t_len = dh2.shape[0]
    n_tile = t_len // tm

    def body(dh2_ref, dgl_ref, ob_ref, wpg_hbm, wout_hbm, g_ref, dh1_ref, do_ref, dy_ref, dg_ref, wpg_s, wout_s, t_s,
             acc_s, sems):
        _load_weight(wpg_hbm, wpg_s, sems.at[0])
        _load_weight(wout_hbm, wout_s, sems.at[1])
        i = pl.program_id(0)

        @pl.when(i == 0)
        def _():
            acc_s[...] = jnp.zeros_like(acc_s)

        t_s[...] = _dot_nt(dgl_ref[...], wpg_s[...])
        g = g_ref[...]

        def rows_body(q, acc):
            rows = _tile_rows(q)
            dh1 = dh2_ref[rows, :] + t_s[rows, :]
            dh1_ref[rows, :] = dh1
            o = ob_ref[rows, :].astype(F32)
            rr = lax.rsqrt(_mean_last(o * o) + EPS)
            on = o * rr
            dog = dh1 * g
            do_ref[rows, :] = (rr * (dog - on * _mean_last(dog * on))).astype(BF16)
            return acc + _fold_rows(dh1 * on)

        acc_s[...] = _loop(tm // TILE_ROWS, rows_body, acc_s[...])
        dy_ref[...] = _dot_nt(do_ref[...], wout_s[...]).astype(BF16)

        @pl.when(i == n_tile - 1)
        def _():
            dg_ref[...] = jnp.sum(acc_s[...], axis=0, keepdims=True)

    tile = pl.BlockSpec((tm, D_MODEL), lambda i: (i, 0))
    vec = pl.BlockSpec((1, D_MODEL), lambda i: (0, 0))
    hbm = pl.BlockSpec(memory_space=pl.ANY)
    return pl.pallas_call(
        body, name="tail_bwd", grid=(n_tile,),
        in_specs=[tile, tile, tile, hbm, hbm, vec],
        out_specs=[tile, tile, tile, vec],
        out_shape=[SDS((t_len, D_MODEL), F32), SDS((t_len, D_MODEL), BF16), SDS((t_len, D_MODEL), BF16),
                   SDS((1, D_MODEL), F32)],
        scratch_shapes=[pltpu.VMEM((D_MODEL, D_MODEL), BF16), pltpu.VMEM((D_MODEL, D_MODEL), BF16),
                        pltpu.VMEM((tm, D_MODEL), F32), pltpu.VMEM((ROWS, D_MODEL), F32), pltpu.SemaphoreType.DMA((2,))],
        compiler_params=_params(("arbitrary",), 48),
    )(dh2, dgl, ob, w_pg, w_out, post_g)


def _mix_bwd(z, dy, h, ln_g, ln_b, wm, wm_t, bias, cw, cb, wax, wax_t, ba, bx, lam, goa, gob):
    t_len = z.shape[0]
    n_chunk = t_len // CHUNK
    halo_blocks = CHUNK // ROWS

    def body(z_ref, zhalo_ref, dy_ref, h_ref, hhalo_ref, lng_ref, lnb_ref, wm_ref, wmt_ref, bias_ref, cw_ref, cb_ref,
             wax_ref, waxt_ref, ba_ref, bx_ref, lam_ref, goa_ref, gob_ref,
             dz_ref, vecs_ref, dws_ref, dwax_ref, dbs_ref,
             vn_s, vh_s, rs_s, xc_s, mixed_s, pre_s, dmix_s, dvn_s, dho_s, dxc_s, dpre_s, dz_s, acc_s, accdm_s,
             cg_s, ca_s, dxchalo_s):
        step = pl.program_id(0)
        c_id = n_chunk - 1 - step
        rid = _row_ids(D_BR)
        first_chunk = c_id == 0

        @pl.when(step == 0)
        def _():
            acc_s[...] = jnp.zeros_like(acc_s)
            accdm_s[...] = jnp.zeros_like(accdm_s)
            cg_s[...] = jnp.zeros_like(cg_s)
            ca_s[...] = jnp.zeros_like(ca_s)
            dxchalo_s[...] = jnp.zeros_like(dxchalo_s)
            dws_ref[...] = jnp.zeros_like(dws_ref)
            dwax_ref[...] = jnp.zeros_like(dwax_ref)

        lng, lnb, cb = lng_ref[...], lnb_ref[...], cb_ref[...]
        xb_halo = jnp.where(first_chunk, 0.0, zhalo_ref[...])
        h_halo = jnp.where(first_chunk, 0.0, hhalo_ref[...])

        def prev_rows(ref, cols, g, halo):
            before = ref[pl.ds(pl.multiple_of(jnp.maximum(g - 1, 0) * ROWS, ROWS), ROWS), cols]
            return jnp.where(g > 0, before, halo)

        def phase1(g, prev):
            rows = _rows(g)
            vg, _ = _gelu(z_ref[rows, D_BR:2 * D_BR])
            xm = vg - _mean_last(vg)
            rs = lax.rsqrt(_mean_last(xm * xm) + EPS)
            vh = xm * rs
            vh_s[rows, :] = vh
            rs_s[rows, :] = jnp.broadcast_to(rs, (ROWS, HEAD))
            vn_s[rows, :] = vh * lng + lnb
            xb = z_ref[rows, 3 * D_BR:4 * D_BR]
            xc_s[rows, :] = _conv_rows(xb, prev, cw_ref, cb, rid)
            return xb

        _loop(N_GROUP, phase1, xb_halo)

        for hd in range(N_HEAD):
            cs = slice(hd * HEAD, (hd + 1) * HEAD)
            mixed_s[:, cs] = _dot(wm_ref[hd], vn_s[:, cs].astype(BF16))
            pre = _dot(xc_s[:, cs].astype(BF16), wax_ref[hd])
            pre_s[:, cs] = pre[:, :HEAD]
            pre_s[:, D_BR + hd * HEAD:D_BR + (hd + 1) * HEAD] = pre[:, HEAD:]

        goa, gob = goa_ref[...], gob_ref[...]

        def phase3(g, _):
            rows = _rows(g)
            u = z_ref[rows, 0:D_BR]
            ug, tu = _gelu(u)
            ga = z_ref[rows, 2 * D_BR:3 * D_BR]
            sga = _sig(ga)
            sa = ga * sga
            mixed = mixed_s[rows, :] + bias_ref[rows, :]
            ya0 = ug * mixed
            ya = ya0 * sa
            ra = lax.rsqrt(_mean_last(ya * ya) + EPS)
            dyan = dy_ref[rows, 0:D_BR].astype(F32)
            acc_s[V_GOUT_A] += dyan * ya * ra
            dyg = dyan * goa
            dya = ra * dyg - ya * (ra * ra * ra) * _mean_last(dyg * ya)
            dya0 = dya * sa
            dz_s[rows, 2 * D_BR:3 * D_BR] = dya * ya0 * (sga * (1.0 + ga * (1.0 - sga)))
            dmix = dya0 * ug
            dmix_s[rows, :] = dmix
            accdm_s[rows, :] += dmix
            dz_s[rows, 0:D_BR] = dya0 * mixed * _gelu_grad(u, tu)

            hh = h_ref[rows, :]
            gb = z_ref[rows, 4 * D_BR:5 * D_BR]
            sgb = _sig(gb)
            sb = gb * sgb
            yb = hh * sb
            rb = lax.rsqrt(_mean_last(yb * yb) + EPS)
            dybn = dy_ref[rows, D_BR:2 * D_BR].astype(F32)
            acc_s[V_GOUT_B] += dybn * yb * rb
            dyg = dybn * gob
            dyb = rb * dyg - yb * (rb * rb * rb) * _mean_last(dyg * yb)
            dho_s[rows, :] = dyb * sb
            dz_s[rows, 4 * D_BR:5 * D_BR] = dyb * hh * (sgb * (1.0 + gb * (1.0 - sgb)))
            return 0

        _loop(N_GROUP, phase3, 0)

        for hd in range(N_HEAD):
            cs = slice(hd * HEAD, (hd + 1) * HEAD)
            dmb = dmix_s[:, cs].astype(BF16)
            dvn_s[:, cs] = _dot(wmt_ref[hd], dmb)
            dws_ref[hd] += _dot_nt(dmb, vn_s[:, cs].astype(BF16))

        def phase5(g, _):
            rows = _rows(g)
            dvn = dvn_s[rows, :]
            vh = vh_s[rows, :]
            acc_s[V_LN_G] += dvn * vh
            acc_s[V_LN_B] += dvn
            dvh = dvn * lng
            rs = rs_s[rows, 0:1]
            dvg = rs * (dvh - _mean_last(dvh) - vh * _mean_last(dvh * vh))
            v = z_ref[rows, D_BR:2 * D_BR]
            _, tv = _gelu(v)
            dz_s[rows, D_BR:2 * D_BR] = dvg * _gelu_grad(v, tv)
            return 0

        _loop(N_GROUP, phase5, 0)

        ba, bx = ba_ref[...], bx_ref[...]
        sp8 = LRU_C * _softplus(-lam_ref[...])

        def phase6(k, carry):
            cg, ca = carry
            g = N_GROUP - 1 - k
            rows = _rows(g)
            first_row = jnp.logical_and(jnp.logical_and(first_chunk, g == 0), rid == 0)
            r, i, a, mult = _lru_gates(pre_s[rows, 0:D_BR], pre_s[rows, D_BR:2 * D_BR], ba, bx, sp8, first_row)
            a_nx = jnp.where(rid < ROWS - 1, pltpu.roll(a, ROWS - 1, 0), ca)
            aa, bb = a_nx, dho_s[rows, :]
            for d in (1, 2, 4):
                a_sh = jnp.where(rid < ROWS - d, pltpu.roll(aa, ROWS - d, 0), 1.0)
                b_sh = jnp.where(rid < ROWS - d, pltpu.roll(bb, ROWS - d, 0), 0.0)
                bb = aa * b_sh + bb
                aa = aa * a_sh
            gg = bb + aa * cg
            hh = h_ref[rows, :]
            hprev = _shift_down(hh, prev_rows(h_ref, slice(None), g, h_halo), 1, rid)
            xc = xc_s[rows, :]
            gx = gg * xc
            dla = gg * hprev * a - jnp.where(first_row, 0.0, gx * i * (a * a) / mult)
            acc_s[V_LAM] += -(dla * r)
            dpa = -(dla * sp8) * r * (1.0 - r)
            dpx = gx * mult * i * (1.0 - i)
            acc_s[V_B_A] += dpa
            acc_s[V_B_X] += dpx
            dpre_s[rows, 0:D_BR] = dpa
            dpre_s[rows, D_BR:2 * D_BR] = dpx
            dxc_s[rows, :] = gg * mult * i
            return _bcast_row(gg, 0), _bcast_row(a, 0)

        cg, ca = _loop(N_GROUP, phase6, (cg_s[...], ca_s[...]))
        cg_s[...] = cg
        ca_s[...] = ca

        for hd in range(N_HEAD):
            cs = slice(hd * HEAD, (hd + 1) * HEAD)
            dpre = jnp.concatenate([dpre_s[:, cs], dpre_s[:, D_BR + hd * HEAD:D_BR + (hd + 1) * HEAD]], axis=1).astype(BF16)
            dxc_s[:, cs] += _dot(dpre, waxt_ref[hd])
            dwax_ref[hd] += _dot_tn(xc_s[:, cs].astype(BF16), dpre)

        def phase8(k, nxt):
            g = N_GROUP - 1 - k
            rows = _rows(g)
            dxc = dxc_s[rows, :]
            acc_s[V_CONV_B] += dxc
            xb = z_ref[rows, 3 * D_BR:4 * D_BR]
            xb_prev = prev_rows(z_ref, slice(3 * D_BR, 4 * D_BR), g, xb_halo)
            dxb = cw_ref[3:4, :] * dxc
            acc_s[V_CONV_W + 3] += dxc * xb
            for j in range(1, CONV_W):
                dxb = dxb + cw_ref[3 - j:4 - j, :] * _shift_up(dxc, nxt, j, rid)
                acc_s[V_CONV_W + 3 - j] += dxc * _shift_down(xb, xb_prev, j, rid)
            dz_s[rows, 3 * D_BR:4 * D_BR] = dxb
            return dxc

        dxchalo_s[...] = _loop(N_GROUP, phase8, dxchalo_s[...])
        dz_ref[...] = dz_s[...].astype(BF16)

        @pl.when(step == n_chunk - 1)
        def _():
            for v in range(N_VEC):
                vecs_ref[v:v + 1, :] = jnp.sum(acc_s[v], axis=0, keepdims=True)
            lam = lam_ref[...]
            vecs_ref[V_LAM:V_LAM + 1, :] = vecs_ref[V_LAM:V_LAM + 1, :] * (-LRU_C * _sig(-lam))
            tril = (lax.broadcasted_iota(jnp.int32, (HEAD, HEAD), 0) >= lax.broadcasted_iota(jnp.int32, (HEAD, HEAD), 1))
            ones = jnp.ones((ROWS, HEAD), BF16)
            for hd in range(N_HEAD):
                cs = slice(hd * HEAD, (hd + 1) * HEAD)
                dws_ref[hd] = jnp.where(tril, dws_ref[hd], 0.0)
                blk = accdm_s[:, cs]
                hi = blk.astype(BF16)
                lo = (blk - hi.astype(F32)).astype(BF16)
                dbs_ref[hd:hd + 1, :] = (_dot_nt(ones, hi) + _dot_nt(ones, lo))[0:1, :]

    vec = pl.BlockSpec((1, D_BR), lambda i: (0, 0))
    rev = lambda i: (n_chunk - 1 - i, 0)
    halo = lambda col: (lambda i: (jnp.maximum((n_chunk - 1 - i) * halo_blocks - 1, 0), col))
    full3 = lambda a, b, c: pl.BlockSpec((a, b, c), lambda i: (0, 0, 0))
    big = lambda w: pltpu.VMEM((CHUNK, w), F32)
    return pl.pallas_call(
        body, name="mix_bwd", grid=(n_chunk,),
        in_specs=[pl.BlockSpec((CHUNK, D_IN), rev), pl.BlockSpec((ROWS, D_BR), halo(3)),
                  pl.BlockSpec((CHUNK, 2 * D_BR), rev), pl.BlockSpec((CHUNK, D_BR), rev),
                  pl.BlockSpec((ROWS, D_BR), halo(0)), vec, vec,
                  full3(N_HEAD, HEAD, HEAD), full3(N_HEAD, HEAD, HEAD),
                  pl.BlockSpec((CHUNK, D_BR), lambda i: (0, 0)), pl.BlockSpec((ROWS, D_BR), lambda i: (0, 0)), vec,
                  full3(N_HEAD, HEAD, 2 * HEAD), full3(N_HEAD, 2 * HEAD, HEAD), vec, vec, vec, vec, vec],
        out_specs=[pl.BlockSpec((CHUNK, D_IN), rev), pl.BlockSpec((N_VEC, D_BR), lambda i: (0, 0)),
                   full3(N_HEAD, HEAD, HEAD), full3(N_HEAD, HEAD, 2 * HEAD),
                   pl.BlockSpec((N_HEAD, HEAD), lambda i: (0, 0))],
        out_shape=[SDS((t_len, D_IN), BF16), SDS((N_VEC, D_BR), F32), SDS((N_HEAD, HEAD, HEAD), F32),
                   SDS((N_HEAD, HEAD, 2 * HEAD), F32), SDS((N_HEAD, HEAD), F32)],
        scratch_shapes=[big(D_BR), big(D_BR), big(HEAD), big(D_BR), big(D_BR), big(2 * D_BR), big(D_BR), big(D_BR),
                        big(D_BR), big(D_BR), big(2 * D_BR), big(D_IN),
                        pltpu.VMEM((N_VEC, ROWS, D_BR), F32), big(D_BR),
                        pltpu.VMEM((ROWS, D_BR), F32), pltpu.VMEM((ROWS, D_BR), F32), pltpu.VMEM((ROWS, D_BR), F32)],
        compiler_params=_params(("arbitrary",), 48),
    )(z, z, dy, h, h, ln_g, ln_b, wm, wm_t, bias, cw, cb, wax, wax_t, ba, bx, lam, goa, gob)


def _in_bwd(dz, w_in_g, x, dh1, pre_g, tm=512):
    t_len = x.shape[0]
    n_tile = t_len // tm

    def body(dz_ref, w_ref, x_ref, dh1_ref, g_ref, gx_ref, dg_ref, acc_s, dg_s):
        i, k = pl.program_id(0), pl.program_id(1)

        @pl.when(jnp.logical_and(i == 0, k == 0))
        def _():
            dg_s[...] = jnp.zeros_like(dg_s)

        part = _dot_nt(dz_ref[...], w_ref[...])

        @pl.when(k == 0)
        def _():
            acc_s[...] = part

        @pl.when(k > 0)
        def _():
            acc_s[...] += part

        @pl.when(k == N_DEV - 1)
        def _():
            g = g_ref[...]

            def rows_body(q, acc):
                rows = _tile_rows(q)
                xv = x_ref[rows, :]
                r = lax.rsqrt(_mean_last(xv * xv) + EPS)
                xh = xv * r
                dhn = acc_s[rows, :]
                dg = dhn * g
                gx_ref[rows, :] = dh1_ref[rows, :] + r * (dg - xh * _mean_last(dg * xh))
                return acc + _fold_rows(dhn * xh)

            dg_s[...] = _loop(tm // TILE_ROWS, rows_body, dg_s[...])

        @pl.when(jnp.logical_and(i == n_tile - 1, k == N_DEV - 1))
        def _():
            dg_ref[...] = jnp.sum(dg_s[...], axis=0, keepdims=True)

    tile = pl.BlockSpec((tm, D_MODEL), lambda i, k: (i, 0))
    vec = pl.BlockSpec((1, D_MODEL), lambda i, k: (0, 0))
    return pl.pallas_call(
        body, name="in_bwd", grid=(n_tile, N_DEV),
        in_specs=[pl.BlockSpec((tm, W_IN_SHARD), lambda i, k: (i, k)),
                  pl.BlockSpec((None, D_MODEL, W_IN_SHARD), lambda i, k: (k, 0, 0)), tile, tile, vec],
        out_specs=[tile, vec],
        out_shape=[SDS((t_len, D_MODEL), F32), SDS((1, D_MODEL), F32)],
        scratch_shapes=[pltpu.VMEM((tm, D_MODEL), F32), pltpu.VMEM((ROWS, D_MODEL), F32)],
        compiler_params=_params(("arbitrary", "arbitrary"), 48),
    )(dz, w_in_g, x, dh1, pre_g)


def _grad_w(a, b, bn, shard_major, name, tk=1024):
    t_len, m = a.shape
    n = b.shape[1]
    n_k = t_len // tk

    def body(a_ref, b_ref, o_ref, acc_s):
        k = pl.program_id(1)
        part = _dot_tn(a_ref[...], b_ref[...])

        @pl.when(k == 0)
        def _():
            acc_s[...] = part

        @pl.when(k > 0)
        def _():
            acc_s[...] += part

        @pl.when(k == n_k - 1)
        def _():
            o_ref[...] = acc_s[...].astype(BF16)

    if shard_major:
        out_spec, out_shape = pl.BlockSpec((None, m, bn), lambda j, k: (j, 0, 0)), SDS((n // bn, m, bn), BF16)
    else:
        out_spec, out_shape = pl.BlockSpec((m, bn), lambda j, k: (0, j)), SDS((m, n), BF16)
    return pl.pallas_call(
        body, name=name, grid=(n // bn, n_k),
        in_specs=[pl.BlockSpec((tk, m), lambda j, k: (k, 0)), pl.BlockSpec((tk, bn), lambda j, k: (k, j))],
        out_specs=out_spec, out_shape=out_shape,
        scratch_shapes=[pltpu.VMEM((m, bn), F32)],
        compiler_params=_params(("arbitrary", "arbitrary"), 40),
    )(a, b)


def _adamw(parts, w, m, v, name, tr):
    rows, cols = w.shape
    c1 = 1.0 - ADAM_B1 ** ADAM_STEP
    c2 = 1.0 - ADAM_B2 ** ADAM_STEP

    def body(p_ref, w_ref, m_ref, v_ref, g_ref, d_ref, nm_ref, nv_ref):
        g = p_ref[0].astype(F32)
        for s in range(1, N_DEV):
            g = g + p_ref[s].astype(F32)
        g_ref[...] = g
        nm = ADAM_B1 * m_ref[...] + (1.0 - ADAM_B1) * g
        nv = ADAM_B2 * v_ref[...] + (1.0 - ADAM_B2) * (g * g)
        nm_ref[...] = nm
        nv_ref[...] = nv
        d_ref[...] = -ADAM_LR * ((nm / c1) / (jnp.sqrt(nv / c2) + ADAM_EPS) + ADAM_WD * w_ref[...])

    tile = pl.BlockSpec((tr, cols), lambda i: (i, 0))
    return pl.pallas_call(
        body, name=name, grid=(rows // tr,),
        in_specs=[pl.BlockSpec((N_DEV, tr, cols), lambda i: (0, i, 0)), tile, tile, tile],
        out_specs=[tile] * 4, out_shape=[SDS((rows, cols), F32)] * 4,
        compiler_params=_params(("arbitrary",), 40),
    )(parts, w, m, v)


SMALL = ("pre_g", "gmlp_ln_g", "gmlp_ln_b", "gmlp_ws", "gmlp_bs", "conv_b", "w_a", "b_a", "w_x", "b_x", "lam",
         "gmlp_out_g", "lru_out_g", "post_g")
WEIGHTS = ("pre_g", "w_in", "gmlp_ln_g", "gmlp_ln_b", "gmlp_ws", "gmlp_bs", "conv_w", "conv_b", "w_a", "b_a", "w_x",
           "b_x", "lam", "gmlp_out_g", "lru_out_g", "w_out", "post_g", "w_pe", "w_pg")
LANES = 128


PACK_ROWS = 3200
PACK_TILE = 640


def _pack(parts):
    rows = [p.reshape(-1, LANES) for p in parts]
    used = sum(r.shape[0] for r in rows)
    return jnp.concatenate(rows + [jnp.zeros((PACK_ROWS - used, LANES), F32)], axis=0)


def _local_step(x, p, tgt, sm, w_in_g, w_out, w_pe_g, w_pg, cw_full):
    vec = lambda a: a.reshape(1, -1)
    tril = jnp.tril(jnp.ones((CHUNK, CHUNK), dtype=bool))
    wm32 = jnp.where(tril[None], sm["gmlp_ws"], 0.0)
    wm, wm_t = wm32.astype(BF16), jnp.swapaxes(wm32, 1, 2).astype(BF16)
    bias = jnp.repeat(sm["gmlp_bs"].T, HEAD, axis=1)
    wax32 = jnp.concatenate([sm["w_a"], sm["w_x"]], axis=2)
    wax, wax_t = wax32.astype(BF16), jnp.swapaxes(wax32, 1, 2).astype(BF16)
    cw8 = jnp.concatenate([cw_full, jnp.zeros((ROWS - CONV_W, D_BR), F32)], axis=0)
    mixer_consts = dict(cw=cw8, cb=vec(sm["conv_b"]), ba=vec(sm["b_a"]), bx=vec(sm["b_x"]), lam=vec(sm["lam"]),
                        goa=vec(sm["gmlp_out_g"]), gob=vec(sm["lru_out_g"]))
    ln_g, ln_b = vec(sm["gmlp_ln_g"]), vec(sm["gmlp_ln_b"])
    pre_g, post_g = vec(sm["pre_g"]), vec(sm["post_g"])

    z, hn = _in_proj(x, pre_g, w_in_g)
    y, h = _mix_fwd(z, ln_g, ln_b, wm, bias, wax=wax, **mixer_consts)
    h1, ob = _out_proj(y, x, w_out, post_g)
    dh2, dpe, dgl, h1b, loss = _ple_loss(h1, p, tgt, w_pg, w_pe_g)
    dh1, do, dy, d_post_g = _tail_bwd(dh2, dgl, ob, w_pg, w_out, post_g)
    dz, vecs, d_ws, d_wax, d_bs = _mix_bwd(z, dy, h, ln_g, ln_b, wm, wm_t, bias, wax=wax, wax_t=wax_t, **mixer_consts)
    grad_x, d_pre_g = _in_bwd(dz, w_in_g, x, dh1, pre_g)

    d_w_in = _grad_w(hn, dz, W_IN_SHARD, True, "grad_w_in")
    d_w_out = _grad_w(y, do, 512, False, "grad_w_out")
    d_w_pg = _grad_w(h1b, dgl, 512, False, "grad_w_pg")
    d_w_pe = _grad_w(p.astype(BF16), dpe, D_MODEL // N_DEV, True, "grad_w_pe")

    small = {"pre_g": d_pre_g, "gmlp_ln_g": vecs[V_LN_G], "gmlp_ln_b": vecs[V_LN_B], "gmlp_ws": d_ws, "gmlp_bs": d_bs,
             "conv_b": vecs[V_CONV_B], "w_a": d_wax[:, :, :HEAD], "b_a": vecs[V_B_A], "w_x": d_wax[:, :, HEAD:],
             "b_x": vecs[V_B_X], "lam": vecs[V_LAM], "gmlp_out_g": vecs[V_GOUT_A], "lru_out_g": vecs[V_GOUT_B],
             "post_g": d_post_g}
    d_cw = vecs[V_CONV_W:V_CONV_W + CONV_W]
    return loss[0, 0], grad_x, small, d_w_in, d_w_out, d_w_pe, d_w_pg, d_cw


def kernel(x, p, pre_g, w_in, gmlp_ln_g, gmlp_ln_b, gmlp_ws, gmlp_bs, conv_w, conv_b, w_a, b_a, w_x, b_x, lam, gmlp_out_g, lru_out_g, w_out, post_g, w_pe, w_pg, loss_target, m_pre_g, m_w_in, m_gmlp_ln_g, m_gmlp_ln_b, m_gmlp_ws, m_gmlp_bs, m_conv_w, m_conv_b, m_w_a, m_b_a, m_w_x, m_b_x, m_lam, m_gmlp_out_g, m_lru_out_g, m_w_out, m_post_g, m_w_pe, m_w_pg, v_pre_g, v_w_in, v_gmlp_ln_g, v_gmlp_ln_b, v_gmlp_ws, v_gmlp_bs, v_conv_w, v_conv_b, v_w_a, v_b_a, v_w_x, v_b_x, v_lam, v_gmlp_out_g, v_lru_out_g, v_w_out, v_post_g, v_w_pe, v_w_pg):
    args = dict(locals())
    weights = {n: args[n] for n in WEIGHTS}
    m_in = {n: args["m_" + n] for n in WEIGHTS}
    v_in = {n: args["v_" + n] for n in WEIGHTS}
    sm = {n: weights[n][0] for n in SMALL}
    shard_rows = D_MODEL // N_DEV

    cw_shard = jnp.concatenate([conv_w.reshape(CONV_W, HEAD), jnp.zeros((ROWS - CONV_W, HEAD), F32)], axis=0)
    w_in_g, w_out_g, w_pe_g, w_pg_g, cw_g = _exchange(
        [w_in[0].astype(BF16), w_out[0].astype(BF16), w_pe[0].astype(BF16), w_pg[0].astype(BF16), cw_shard],
        False, "gather_weights")
    cw_full = jnp.transpose(cw_g[:, :CONV_W, :], (1, 0, 2)).reshape(CONV_W, D_BR)

    loss_part, grad_x, small, d_w_in, d_w_out, d_w_pe, d_w_pg, d_cw = _local_step(
        x[0], p[0, 0], loss_target[0], sm, w_in_g, w_out_g.reshape(D_MODEL, D_MODEL), w_pe_g,
        w_pg_g.reshape(D_MODEL, D_MODEL), cw_full)
    loss = lax.psum(loss_part, MESH_AXES)

    d_cw_blocks = jnp.transpose(d_cw.reshape(CONV_W, N_DEV, HEAD), (1, 0, 2))
    d_cw_blocks = jnp.concatenate([d_cw_blocks, jnp.zeros((N_DEV, ROWS - CONV_W, HEAD), F32)], axis=1)
    parts = _exchange(
        [d_w_in, d_w_out.reshape(N_DEV, shard_rows, D_MODEL), d_w_pe, d_w_pg.reshape(N_DEV, shard_rows, D_MODEL),
         d_cw_blocks], True, "scatter_grads")
    pad_cw = lambda a: jnp.concatenate([a.reshape(CONV_W, HEAD), jnp.zeros((ROWS - CONV_W, HEAD), F32)], axis=0)
    big = {
        "w_in": _adamw(parts[0], w_in[0], m_w_in[0], v_w_in[0], "adamw_w_in", 256),
        "w_out": _adamw(parts[1], w_out[0], m_w_out[0], v_w_out[0], "adamw_w_out", 128),
        "w_pe": _adamw(parts[2], w_pe[0], m_w_pe[0], v_w_pe[0], "adamw_w_pe", 256),
        "w_pg": _adamw(parts[3], w_pg[0], m_w_pg[0], v_w_pg[0], "adamw_w_pg", 128),
        "conv_w": [a[:CONV_W] for a in
                   _adamw(parts[4], pad_cw(conv_w), pad_cw(m_conv_w), pad_cw(v_conv_w), "adamw_conv_w", ROWS)],
    }

    small_parts = _exchange([_pack([small[n] for n in SMALL])], False, "gather_small_grads")[0]
    packed = _adamw(small_parts, _pack([weights[n] for n in SMALL]), _pack([m_in[n] for n in SMALL]),
                    _pack([v_in[n] for n in SMALL]), "adamw_small", PACK_TILE)
    outs = {n: [None] * 4 for n in WEIGHTS}
    row = 0
    for n in SMALL:
        n_rows = weights[n].size // LANES
        for q in range(4):
            outs[n][q] = packed[q][row:row + n_rows].reshape(weights[n].shape)
        row += n_rows
    for n, res in big.items():
        outs[n] = [r.reshape(weights[n].shape) for r in res]

    result = [loss, grad_x[None]]
    for q in range(4):
        result += [outs[n][q] for n in WEIGHTS]
    return tuple(result)
```

```python
import functools

import jax
import jax.numpy as jnp
from jax import lax
from jax.experimental import pallas as pl
from jax.experimental.pallas import tpu as pltpu

F32 = jnp.float32
BF16 = jnp.bfloat16
SDS = jax.ShapeDtypeStruct

D_MODEL = 2048
D_BR = 1024
D_IN = 5 * D_BR
D_PLE = 256
N_HEAD = 8
HEAD = 128
CHUNK = 128
ROWS = 8
N_GROUP = CHUNK // ROWS
N_DEV = 8
W_IN_SHARD = D_IN // N_DEV
EPS = 1e-6
LRU_C = 8.0
CONV_W = 4
MESH_AXES = ("x", "y", "c")
MIB = 1 << 20

ADAM_LR, ADAM_B1, ADAM_B2, ADAM_EPS, ADAM_WD, ADAM_STEP = 0.001, 0.9, 0.999, 1e-08, 0.01, 10

_GELU_C = 0.7978845608028654
_GELU_A = 0.044715

V_LN_G, V_LN_B, V_CONV_B, V_B_A, V_B_X, V_LAM, V_GOUT_A, V_GOUT_B, V_CONV_W = 0, 1, 2, 3, 4, 5, 6, 7, 8
N_VEC = 16


def _params(sem, vmem_mib):
    return pltpu.CompilerParams(dimension_semantics=sem, vmem_limit_bytes=int(vmem_mib * MIB))


def _sig(x):
    return 1.0 / (1.0 + jnp.exp(-x))


def _gelu(x):
    t = jnp.tanh(_GELU_C * (x + _GELU_A * x * x * x))
    return 0.5 * x * (1.0 + t), t


def _gelu_grad(x, t):
    return 0.5 * (1.0 + t) + 0.5 * x * (1.0 - t * t) * (_GELU_C * (1.0 + 3.0 * _GELU_A * x * x))


def _expm1_nonpos(y):
    poly = y * (1.0 + y * 0.5 * (1.0 + y * (1.0 / 3.0) * (1.0 + y * 0.25 * (1.0 + y * 0.2 * (
        1.0 + y * (1.0 / 6.0) * (1.0 + y * (1.0 / 7.0)))))))
    return jnp.where(y > -0.3, poly, jnp.exp(y) - 1.0)


def _softplus(x):
    return jnp.maximum(x, 0.0) + jnp.log(1.0 + jnp.exp(-jnp.abs(x)))


def _row_ids(width):
    return lax.broadcasted_iota(jnp.int32, (ROWS, width), 0)


def _shift_down(cur, prev, k, rid):
    return jnp.where(rid >= k, pltpu.roll(cur, k, 0), pltpu.roll(prev, k, 0))


def _shift_up(cur, nxt, k, rid):
    return jnp.where(rid < ROWS - k, pltpu.roll(cur, ROWS - k, 0), pltpu.roll(nxt, ROWS - k, 0))


def _mean_last(x):
    return jnp.mean(x, axis=-1, keepdims=True)


def _rows(g):
    return pl.ds(pl.multiple_of(g * ROWS, ROWS), ROWS)


TILE_ROWS = 16


def _tile_rows(q):
    return pl.ds(pl.multiple_of(q * TILE_ROWS, TILE_ROWS), TILE_ROWS)


UNROLL = 4


def _loop(n, body, init, unroll=UNROLL):
    def wide(i, carry):
        for u in range(unroll):
            carry = body(i * unroll + u, carry)
        return carry

    return lax.fori_loop(0, n // unroll, wide, init)


def _fold_rows(x):
    return x[0:ROWS, :] + x[ROWS:TILE_ROWS, :]


def _bcast_row(x, r):
    return jnp.broadcast_to(x[r:r + 1, :], x.shape)


def _dot(a, b):
    return jnp.dot(a, b, preferred_element_type=F32)


def _dot_nt(a, b):
    return lax.dot_general(a, b, (((1,), (1,)), ((), ())), preferred_element_type=F32)


def _dot_tn(a, b):
    return lax.dot_general(a, b, (((0,), (0,)), ((), ())), preferred_element_type=F32)


def _mesh_place():
    x, y, c = lax.axis_index("x"), lax.axis_index("y"), lax.axis_index("c")
    return x, y, c, 4 * x + 2 * y + c


def _peer(x, y, c, k):
    px = 1 - x if k & 4 else x
    py = 1 - y if k & 2 else y
    pc = 1 - c if k & 1 else c
    return (px, py, pc), 4 * px + 2 * py + pc


def _remote(src, dst, send_sem, recv_sem, dev):
    return pltpu.make_async_remote_copy(src_ref=src, dst_ref=dst, send_sem=send_sem, recv_sem=recv_sem, device_id=dev,
                                        device_id_type=pl.DeviceIdType.MESH)


ANY_SPEC = pl.BlockSpec(memory_space=pl.ANY)


class _Exchange:
    def __init__(self, arrs, scatter):
        self.n = len(arrs)
        self.scatter = tuple(scatter)
        self.out_shape = [SDS(a.shape if s else (N_DEV,) + a.shape, a.dtype) for a, s in zip(arrs, scatter)]
        self.scratch = [pltpu.SemaphoreType.DMA((self.n * N_DEV,)), pltpu.SemaphoreType.DMA((self.n * N_DEV,)),
                        pltpu.SemaphoreType.DMA((self.n,))]

    def _copies(self, ins, outs, sems):
        send_sems, recv_sems, local_sems = sems
        x, y, c, me = _mesh_place()
        local, sends, recvs = [], [], []
        for a in range(self.n):
            src = ins[a].at[me] if self.scatter[a] else ins[a]
            local.append(pltpu.make_async_copy(src, outs[a].at[me], local_sems.at[a]))
        for k in range(1, N_DEV):
            dev, lin = _peer(x, y, c, k)
            for a in range(self.n):
                src = ins[a].at[lin] if self.scatter[a] else ins[a]
                pair = (send_sems.at[a * N_DEV + k], recv_sems.at[a * N_DEV + k], dev)
                sends.append(_remote(src, outs[a].at[me], *pair))
                recvs.append(_remote(src, outs[a].at[lin], *pair))
        return local, sends, recvs

    def start(self, ins, outs, sems):
        local, sends, _ = self._copies(ins, outs, sems)
        for cp in local + sends:
            cp.start()

    def wait(self, ins, outs, sems):
        local, sends, recvs = self._copies(ins, outs, sems)
        for cp in recvs:
            cp.wait_recv()
        for cp in sends:
            cp.wait_send()
        for cp in local:
            cp.wait()


def _exchange(arrs, scatter, name):
    ex = _Exchange(arrs, [scatter] * len(arrs))
    n = ex.n

    def body(*refs):
        ins, outs, sems = refs[:n], refs[n:2 * n], refs[2 * n:]
        ex.start(ins, outs, sems)
        ex.wait(ins, outs, sems)

    return pl.pallas_call(
        body, name=name, out_shape=ex.out_shape, in_specs=[ANY_SPEC] * n, out_specs=[ANY_SPEC] * n,
        scratch_shapes=ex.scratch,
    )(*arrs)


def _pre_norm(x, pre_g, tm=512):
    t_len = x.shape[0]

    def body(x_ref, g_ref, hn_ref):
        g = g_ref[...]

        def rows_body(q, _):
            rows = _tile_rows(q)
            xv = x_ref[rows, :]
            hn_ref[rows, :] = (xv * lax.rsqrt(_mean_last(xv * xv) + EPS) * g).astype(BF16)
            return 0

        _loop(tm // TILE_ROWS, rows_body, 0)

    tile = pl.BlockSpec((tm, D_MODEL), lambda i: (i, 0))
    return pl.pallas_call(
        body, name="pre_norm", grid=(t_len // tm,),
        in_specs=[tile, pl.BlockSpec((1, D_MODEL), lambda i: (0, 0))], out_specs=tile,
        out_shape=SDS((t_len, D_MODEL), BF16),
        compiler_params=_params(("arbitrary",), 24),
    )(x, pre_g)


AG_ORDER = (0, 1, 2, 4, 6, 3, 5, 7)
SIBLING = 1
ICI_MASKS = (2, 4, 6)
DIRECT_MASKS = (SIBLING,) + ICI_MASKS


def _in_proj(hn, w_shard, others, tm=512):
    t_len = hn.shape[0]
    n_i = t_len // tm
    n_o = len(others)
    me_out = 4 * lax.axis_index("x") + 2 * lax.axis_index("y") + lax.axis_index("c")
    order = jnp.stack([me_out ^ k for k in AG_ORDER]).astype(jnp.int32)

    def body(order_ref, hn_ref, w_hbm, *refs):
        o_in = refs[:n_o]
        z_ref, wg_hbm = refs[n_o], refs[n_o + 1]
        o_out = refs[n_o + 2:2 * n_o + 2]
        wbuf, send_w, recv_w, fsend_w, frecv_w, send_o, recv_o, fsend_o, frecv_o, wb_sems, loc_sems = refs[2 * n_o + 2:]
        j, i = pl.program_id(0), pl.program_id(1)
        x, y, c, me = _mesh_place()
        sib = _peer(x, y, c, SIBLING)[0]

        def direct(k, a=None):
            dev, lin = _peer(x, y, c, k)
            if a is None:
                return (_remote(w_hbm, wbuf.at[me], send_w.at[k], recv_w.at[k], dev),
                        _remote(w_hbm, wbuf.at[lin], send_w.at[k], recv_w.at[k], dev))
            pair = (send_o.at[a * N_DEV + k], recv_o.at[a * N_DEV + k], dev)
            return _remote(o_in[a], o_out[a].at[me], *pair), _remote(o_in[a], o_out[a].at[lin], *pair)

        def passed(k, a=None):
            mine, theirs = _peer(x, y, c, k)[1], _peer(x, y, c, k ^ SIBLING)[1]
            if a is None:
                pair = (fsend_w.at[k], frecv_w.at[k], sib)
                return _remote(wbuf.at[mine], wbuf.at[mine], *pair), _remote(wbuf.at[theirs], wbuf.at[theirs], *pair)
            pair = (fsend_o.at[a * N_DEV + k], frecv_o.at[a * N_DEV + k], sib)
            return (_remote(o_out[a].at[mine], o_out[a].at[mine], *pair),
                    _remote(o_out[a].at[theirs], o_out[a].at[theirs], *pair))

        def own_copies():
            return [pltpu.make_async_copy(o_in[a], o_out[a].at[me], loc_sems.at[1 + a]) for a in range(n_o)]

        @pl.when(jnp.logical_and(j == 0, i == 0))
        def _():
            own = pltpu.make_async_copy(w_hbm, wbuf.at[me], loc_sems.at[0])
            own.start()
            for cp in own_copies():
                cp.start()
            for k in DIRECT_MASKS:
                direct(k)[0].start()
            for k in DIRECT_MASKS:
                for a in range(n_o):
                    direct(k, a)[0].start()
            own.wait()

        for jj in range(1, N_DEV):
            mask = AG_ORDER[jj]

            @pl.when(jnp.logical_and(j == jj, i == 0))
            def _(jj=jj, mask=mask):
                if mask in DIRECT_MASKS:
                    direct(mask)[1].wait_recv()
                    if mask in ICI_MASKS:
                        passed(mask)[0].start()
                else:
                    passed(mask ^ SIBLING)[1].wait_recv()
                late = jj - (N_DEV - len(ICI_MASKS))
                if late >= 0:
                    for a in range(n_o):
                        direct(ICI_MASKS[late], a)[1].wait_recv()
                        passed(ICI_MASKS[late], a)[0].start()

        slot = order_ref[j]

        @pl.when(i == 0)
        def _():
            pltpu.make_async_copy(wbuf.at[slot], wg_hbm.at[slot], wb_sems.at[j]).start()

        z_ref[...] = _dot(hn_ref[...], wbuf[slot])

        @pl.when(jnp.logical_and(j == N_DEV - 1, i == n_i - 1))
        def _():
            for a in range(n_o):
                direct(SIBLING, a)[1].wait_recv()
            for k in ICI_MASKS:
                for a in range(n_o):
                    passed(k, a)[1].wait_recv()
            for k in DIRECT_MASKS:
                direct(k)[0].wait_send()
                for a in range(n_o):
                    direct(k, a)[0].wait_send()
            for k in ICI_MASKS:
                passed(k)[0].wait_send()
                for a in range(n_o):
                    passed(k, a)[0].wait_send()
            for cp in own_copies():
                cp.wait()
            for jj in range(N_DEV):
                pltpu.make_async_copy(wbuf.at[0], wg_hbm.at[0], wb_sems.at[jj]).wait()

    dma = lambda n: pltpu.SemaphoreType.DMA((n,))
    grid_spec = pltpu.PrefetchScalarGridSpec(
        num_scalar_prefetch=1, grid=(N_DEV, n_i),
        in_specs=[pl.BlockSpec((tm, D_MODEL), lambda j, i, order: (i, 0)), ANY_SPEC] + [ANY_SPEC] * n_o,
        out_specs=[pl.BlockSpec((tm, W_IN_SHARD), lambda j, i, order: (i, order[j])), ANY_SPEC] + [ANY_SPEC] * n_o,
        scratch_shapes=[pltpu.VMEM((N_DEV, D_MODEL, W_IN_SHARD), BF16), dma(N_DEV), dma(N_DEV), dma(N_DEV), dma(N_DEV),
                        dma(n_o * N_DEV), dma(n_o * N_DEV), dma(n_o * N_DEV), dma(n_o * N_DEV), dma(N_DEV), dma(1 + n_o)])
    res = pl.pallas_call(
        body, name="in_proj", grid_spec=grid_spec,
        out_shape=[SDS((t_len, D_IN), F32), SDS((N_DEV, D_MODEL, W_IN_SHARD), BF16)]
        + [SDS((N_DEV,) + o.shape, o.dtype) for o in others],
        compiler_params=_params(("arbitrary", "arbitrary"), 44),
    )(order, hn, w_shard, *others)
    return res[0], res[1], res[2:]


def _conv_rows(cur, prev, cw_ref, cb, rid):
    acc = cw_ref[3:4, :] * cur + cb
    for k in range(1, CONV_W):
        acc = acc + cw_ref[3 - k:4 - k, :] * _shift_down(cur, prev, k, rid)
    return acc


def _lru_gates(pa, px, ba, bx, sp8, first_row):
    r = _sig(pa + ba)
    i = _sig(px + bx)
    la = -(r * sp8)
    a = jnp.exp(la)
    mult = jnp.sqrt(-_expm1_nonpos(2.0 * la))
    mult = jnp.where(first_row, 1.0, mult)
    return r, i, a, mult


def _mix_fwd(z, ln_g, ln_b, wm, bias, cw, cb, wax, ba, bx, lam, goa, gob):
    t_len = z.shape[0]
    n_chunk = t_len // CHUNK

    def body(z_ref, lng_ref, lnb_ref, wm_ref, bias_ref, cw_ref, cb_ref, wax_ref, ba_ref, bx_ref, lam_ref, goa_ref,
             gob_ref, y_ref, h_ref, vn_s, xc_s, mixed_s, pre_s, y_s, carry_s, halo_s):
        c_id = pl.program_id(0)
        rid = _row_ids(D_BR)

        @pl.when(c_id == 0)
        def _():
            carry_s[...] = jnp.zeros_like(carry_s)
            halo_s[...] = jnp.zeros_like(halo_s)

        lng, lnb, cb = lng_ref[...], lnb_ref[...], cb_ref[...]

        def phase1(g, prev):
            rows = _rows(g)
            vg, _ = _gelu(z_ref[rows, D_BR:2 * D_BR])
            xm = vg - _mean_last(vg)
            rs = lax.rsqrt(_mean_last(xm * xm) + EPS)
            vn_s[rows, :] = xm * rs * lng + lnb
            xb = z_ref[rows, 3 * D_BR:4 * D_BR]
            xc_s[rows, :] = _conv_rows(xb, prev, cw_ref, cb, rid)
            return xb

        halo_s[...] = _loop(N_GROUP, phase1, halo_s[...])

        for h in range(N_HEAD):
            cs = slice(h * HEAD, (h + 1) * HEAD)
            mixed_s[:, cs] = _dot(wm_ref[h], vn_s[:, cs].astype(BF16))
            pre = _dot(xc_s[:, cs].astype(BF16), wax_ref[h])
            pre_s[:, cs] = pre[:, :HEAD]
            pre_s[:, D_BR + h * HEAD:D_BR + (h + 1) * HEAD] = pre[:, HEAD:]

        ba, bx, goa, gob = ba_ref[...], bx_ref[...], goa_ref[...], gob_ref[...]
        sp8 = LRU_C * _softplus(-lam_ref[...])

        def phase3(g, carry):
            rows = _rows(g)
            ug, _ = _gelu(z_ref[rows, 0:D_BR])
            ga = z_ref[rows, 2 * D_BR:3 * D_BR]
            ya = ug * (mixed_s[rows, :] + bias_ref[rows, :]) * (ga * _sig(ga))
            y_s[rows, 0:D_BR] = ya * lax.rsqrt(_mean_last(ya * ya) + EPS) * goa

            first_row = jnp.logical_and(jnp.logical_and(c_id == 0, g == 0), rid == 0)
            _, i, a, mult = _lru_gates(pre_s[rows, 0:D_BR], pre_s[rows, D_BR:2 * D_BR], ba, bx, sp8, first_row)
            b = mult * i * xc_s[rows, :]
            for d in (1, 2, 4):
                a_sh = jnp.where(rid >= d, pltpu.roll(a, d, 0), 1.0)
                b_sh = jnp.where(rid >= d, pltpu.roll(b, d, 0), 0.0)
                b = a * b_sh + b
                a = a * a_sh
            hh = b + a * carry
            h_ref[rows, :] = hh
            gb = z_ref[rows, 4 * D_BR:5 * D_BR]
            yb = hh * (gb * _sig(gb))
            y_s[rows, D_BR:2 * D_BR] = yb * lax.rsqrt(_mean_last(yb * yb) + EPS) * gob
            return _bcast_row(hh, ROWS - 1)

        carry_s[...] = _loop(N_GROUP, phase3, carry_s[...])
        y_ref[...] = y_s[...].astype(BF16)

    vec = pl.BlockSpec((1, D_BR), lambda i: (0, 0))
    return pl.pallas_call(
        body, name="mix_fwd", grid=(n_chunk,),
        in_specs=[pl.BlockSpec((CHUNK, D_IN), lambda i: (i, 0)), vec, vec,
                  pl.BlockSpec((N_HEAD, HEAD, HEAD), lambda i: (0, 0, 0)),
                  pl.BlockSpec((CHUNK, D_BR), lambda i: (0, 0)),
                  pl.BlockSpec((ROWS, D_BR), lambda i: (0, 0)), vec,
                  pl.BlockSpec((N_HEAD, HEAD, 2 * HEAD), lambda i: (0, 0, 0)), vec, vec, vec, vec, vec],
        out_specs=[pl.BlockSpec((CHUNK, 2 * D_BR), lambda i: (i, 0)), pl.BlockSpec((CHUNK, D_BR), lambda i: (i, 0))],
        out_shape=[SDS((t_len, 2 * D_BR), BF16), SDS((t_len, D_BR), F32)],
        scratch_shapes=[pltpu.VMEM((CHUNK, D_BR), F32), pltpu.VMEM((CHUNK, D_BR), F32), pltpu.VMEM((CHUNK, D_BR), F32),
                        pltpu.VMEM((CHUNK, 2 * D_BR), F32), pltpu.VMEM((CHUNK, 2 * D_BR), F32),
                        pltpu.VMEM((ROWS, D_BR), F32), pltpu.VMEM((ROWS, D_BR), F32)],
        compiler_params=_params(("arbitrary",), 32),
    )(z, ln_g, ln_b, wm, bias, cw, cb, wax, ba, bx, lam, goa, gob)


def _load_weight(w_hbm, w_vmem, sem):
    @pl.when(pl.program_id(0) == 0)
    def _():
        cp = pltpu.make_async_copy(w_hbm, w_vmem, sem)
        cp.start()
        cp.wait()


def _out_proj(y, x, w_out, post_g, tm=512):
    t_len = y.shape[0]

    def body(y_ref, x_ref, w_hbm, g_ref, h1_ref, ob_ref, w_s, o_s, sem):
        _load_weight(w_hbm, w_s, sem)
        o_s[...] = _dot(y_ref[...], w_s[...])
        g = g_ref[...]

        def rows_body(q, _):
            rows = _tile_rows(q)
            o = o_s[rows, :]
            h1_ref[rows, :] = x_ref[rows, :] + o * lax.rsqrt(_mean_last(o * o) + EPS) * g
            ob_ref[rows, :] = o.astype(BF16)
            return 0

        _loop(tm // TILE_ROWS, rows_body, 0)

    tile = pl.BlockSpec((tm, D_MODEL), lambda i: (i, 0))
    return pl.pallas_call(
        body, name="out_proj", grid=(t_len // tm,),
        in_specs=[tile, tile, pl.BlockSpec(memory_space=pl.ANY), pl.BlockSpec((1, D_MODEL), lambda i: (0, 0))],
        out_specs=[tile, tile],
        out_shape=[SDS((t_len, D_MODEL), F32), SDS((t_len, D_MODEL), BF16)],
        scratch_shapes=[pltpu.VMEM((D_MODEL, D_MODEL), BF16), pltpu.VMEM((tm, D_MODEL), F32), pltpu.SemaphoreType.DMA],
        compiler_params=_params(("arbitrary",), 44),
    )(y, x, w_out, post_g)


def _ple_loss(h1, p, tgt, w_pg, w_pe_g, tm=256):
    t_len = h1.shape[0]
    n_tile = t_len // tm
    pe_shard = D_MODEL // N_DEV

    def body(h1_ref, p_ref, t_ref, w_hbm, wpe_ref, dh2_ref, dpe_ref, dgl_ref, h1b_ref, loss_ref, w_s, pe_s, gl_s, acc_s,
             sem):
        _load_weight(w_hbm, w_s, sem)
        i = pl.program_id(0)

        @pl.when(i == 0)
        def _():
            acc_s[...] = jnp.zeros_like(acc_s)

        h1b_ref[...] = h1_ref[...].astype(BF16)
        pb = p_ref[...].astype(BF16)
        for j in range(N_DEV):
            pe_s[:, j * pe_shard:(j + 1) * pe_shard] = _dot(pb, wpe_ref[j])
        gl_s[...] = _dot(h1b_ref[...], w_s[...])

        def rows_body(q, acc):
            rows = _tile_rows(q)
            pe = pe_s[rows, :]
            g = _sig(gl_s[rows, :])
            e = h1_ref[rows, :] + pe * g - t_ref[rows, :]
            dh2 = e * (1.0 / D_MODEL)
            dh2_ref[rows, :] = dh2
            dpe_ref[rows, :] = (dh2 * g).astype(BF16)
            dgl_ref[rows, :] = (dh2 * pe * g * (1.0 - g)).astype(BF16)
            return acc + _fold_rows(e * e)

        acc_s[...] = _loop(tm // TILE_ROWS, rows_body, acc_s[...])

        @pl.when(i == n_tile - 1)
        def _():
            loss_ref[...] = jnp.full(loss_ref.shape, 0.5 / D_MODEL * jnp.sum(acc_s[...]), F32)

    tile = pl.BlockSpec((tm, D_MODEL), lambda i: (i, 0))
    return pl.pallas_call(
        body, name="ple_loss", grid=(n_tile,),
        in_specs=[tile, pl.BlockSpec((tm, D_PLE), lambda i: (i, 0)), tile, pl.BlockSpec(memory_space=pl.ANY),
                  pl.BlockSpec((N_DEV, D_PLE, pe_shard), lambda i: (0, 0, 0))],
        out_specs=[tile, tile, tile, tile, pl.BlockSpec((ROWS, HEAD), lambda i: (0, 0))],
        out_shape=[SDS((t_len, D_MODEL), F32), SDS((t_len, D_MODEL), BF16), SDS((t_len, D_MODEL), BF16),
                   SDS((t_len, D_MODEL), BF16), SDS((ROWS, HEAD), F32)],
        scratch_shapes=[pltpu.VMEM((D_MODEL, D_MODEL), BF16), pltpu.VMEM((tm, D_MODEL), F32),
                        pltpu.VMEM((tm, D_MODEL), F32), pltpu.VMEM((ROWS, D_MODEL), F32), pltpu.SemaphoreType.DMA],
        compiler_params=_params(("arbitrary",), 44),
    )(h1, p, tgt, w_pg, w_pe_g)


def _tail_bwd(dh2, dgl, ob, w_pg, w_out, post_g, tm=256):
    t_len = dh2.shape[0]
    n_tile = t_len // tm

    def body(dh2_ref, dgl_ref, ob_ref, wpg_hbm, wout_hbm, g_ref, dh1_ref, do_ref, dy_ref, dg_ref, wpg_s, wout_s, t_s,
             acc_s, sems):
        _load_weight(wpg_hbm, wpg_s, sems.at[0])
        _load_weight(wout_hbm, wout_s, sems.at[1])
        i = pl.program_id(0)

        @pl.when(i == 0)
        def _():
            acc_s[...] = jnp.zeros_like(acc_s)

        t_s[...] = _dot_nt(dgl_ref[...], wpg_s[...])
        g = g_ref[...]

        def rows_body(q, acc):
            rows = _tile_rows(q)
            dh1 = dh2_ref[rows, :] + t_s[rows, :]
            dh1_ref[rows, :] = dh1
            o = ob_ref[rows, :].astype(F32)
            rr = lax.rsqrt(_mean_last(o * o) + EPS)
            on = o * rr
            dog = dh1 * g
            do_ref[rows, :] = (rr * (dog - on * _mean_last(dog * on))).astype(BF16)
            return acc + _fold_rows(dh1 * on)

        acc_s[...] = _loop(tm // TILE_ROWS, rows_body, acc_s[...])
        dy_ref[...] = _dot_nt(do_ref[...], wout_s[...]).astype(BF16)

        @pl.when(i == n_tile - 1)
        def _():
            dg_ref[...] = jnp.sum(acc_s[...], axis=0, keepdims=True)

    tile = pl.BlockSpec((tm, D_MODEL), lambda i: (i, 0))
    vec = pl.BlockSpec((1, D_MODEL), lambda i: (0, 0))
    hbm = pl.BlockSpec(memory_space=pl.ANY)
    return pl.pallas_call(
        body, name="tail_bwd", grid=(n_tile,),
        in_specs=[tile, tile, tile, hbm, hbm, vec],
        out_specs=[tile, tile, tile, vec],
        out_shape=[SDS((t_len, D_MODEL), F32), SDS((t_len, D_MODEL), BF16), SDS((t_len, D_MODEL), BF16),
                   SDS((1, D_MODEL), F32)],
        scratch_shapes=[pltpu.VMEM((D_MODEL, D_MODEL), BF16), pltpu.VMEM((D_MODEL, D_MODEL), BF16),
                        pltpu.VMEM((tm, D_MODEL), F32), pltpu.VMEM((ROWS, D_MODEL), F32), pltpu.SemaphoreType.DMA((2,))],
        compiler_params=_params(("arbitrary",), 48),
    )(dh2, dgl, ob, w_pg, w_out, post_g)


def _mix_bwd(z, dy, h, ln_g, ln_b, wm, wm_t, bias, cw, cb, wax, wax_t, ba, bx, lam, goa, gob):
    t_len = z.shape[0]
    n_chunk = t_len // CHUNK
    halo_blocks = CHUNK // ROWS

    def body(z_ref, zhalo_ref, dy_ref, h_ref, hhalo_ref, lng_ref, lnb_ref, wm_ref, wmt_ref, bias_ref, cw_ref, cb_ref,
             wax_ref, waxt_ref, ba_ref, bx_ref, lam_ref, goa_ref, gob_ref,
             dz_ref, vecs_ref, dws_ref, dwax_ref, dbs_ref,
             vn_s, vh_s, rs_s, xc_s, mixed_s, pre_s, dmix_s, dvn_s, dho_s, dxc_s, dpre_s, dz_s, acc_s, accdm_s,
             cg_s, ca_s, dxchalo_s):
        step = pl.program_id(0)
        c_id = n_chunk - 1 - step
        rid = _row_ids(D_BR)
        first_chunk = c_id == 0

        @pl.when(step == 0)
        def _():
            acc_s[...] = jnp.zeros_like(acc_s)
            accdm_s[...] = jnp.zeros_like(accdm_s)
            cg_s[...] = jnp.zeros_like(cg_s)
            ca_s[...] = jnp.zeros_like(ca_s)
            dxchalo_s[...] = jnp.zeros_like(dxchalo_s)
            dws_ref[...] = jnp.zeros_like(dws_ref)
            dwax_ref[...] = jnp.zeros_like(dwax_ref)

        lng, lnb, cb = lng_ref[...], lnb_ref[...], cb_ref[...]
        xb_halo = jnp.where(first_chunk, 0.0, zhalo_ref[...])
        h_halo = jnp.where(first_chunk, 0.0, hhalo_ref[...])

        def prev_rows(ref, cols, g, halo):
            before = ref[pl.ds(pl.multiple_of(jnp.maximum(g - 1, 0) * ROWS, ROWS), ROWS), cols]
            return jnp.where(g > 0, before, halo)

        def phase1(g, prev):
            rows = _rows(g)
            vg, _ = _gelu(z_ref[rows, D_BR:2 * D_BR])
            xm = vg - _mean_last(vg)
            rs = lax.rsqrt(_mean_last(xm * xm) + EPS)
            vh = xm * rs
            vh_s[rows, :] = vh
            rs_s[rows, :] = jnp.broadcast_to(rs, (ROWS, HEAD))
            vn_s[rows, :] = vh * lng + lnb
            xb = z_ref[rows, 3 * D_BR:4 * D_BR]
            xc_s[rows, :] = _conv_rows(xb, prev, cw_ref, cb, rid)
            return xb

        _loop(N_GROUP, phase1, xb_halo)

        for hd in range(N_HEAD):
            cs = slice(hd * HEAD, (hd + 1) * HEAD)
            mixed_s[:, cs] = _dot(wm_ref[hd], vn_s[:, cs].astype(BF16))
            pre = _dot(xc_s[:, cs].astype(BF16), wax_ref[hd])
            pre_s[:, cs] = pre[:, :HEAD]
            pre_s[:, D_BR + hd * HEAD:D_BR + (hd + 1) * HEAD] = pre[:, HEAD:]

        goa, gob = goa_ref[...], gob_ref[...]

        def phase3(g, _):
            rows = _rows(g)
            u = z_ref[rows, 0:D_BR]
            ug, tu = _gelu(u)
            ga = z_ref[rows, 2 * D_BR:3 * D_BR]
            sga = _sig(ga)
            sa = ga * sga
            mixed = mixed_s[rows, :] + bias_ref[rows, :]
            ya0 = ug * mixed
            ya = ya0 * sa
            ra = lax.rsqrt(_mean_last(ya * ya) + EPS)
            dyan = dy_ref[rows, 0:D_BR].astype(F32)
            acc_s[V_GOUT_A] += dyan * ya * ra
            dyg = dyan * goa
            dya = ra * dyg - ya * (ra * ra * ra) * _mean_last(dyg * ya)
            dya0 = dya * sa
            dz_s[rows, 2 * D_BR:3 * D_BR] = dya * ya0 * (sga * (1.0 + ga * (1.0 - sga)))
            dmix = dya0 * ug
            dmix_s[rows, :] = dmix
            accdm_s[rows, :] += dmix
            dz_s[rows, 0:D_BR] = dya0 * mixed * _gelu_grad(u, tu)

            hh = h_ref[rows, :]
            gb = z_ref[rows, 4 * D_BR:5 * D_BR]
            sgb = _sig(gb)
            sb = gb * sgb
            yb = hh * sb
            rb = lax.rsqrt(_mean_last(yb * yb) + EPS)
            dybn = dy_ref[rows, D_BR:2 * D_BR].astype(F32)
            acc_s[V_GOUT_B] += dybn * yb * rb
            dyg = dybn * gob
            dyb = rb * dyg - yb * (rb * rb * rb) * _mean_last(dyg * yb)
            dho_s[rows, :] = dyb * sb
            dz_s[rows, 4 * D_BR:5 * D_BR] = dyb * hh * (sgb * (1.0 + gb * (1.0 - sgb)))
            return 0

        _loop(N_GROUP, phase3, 0)

        for hd in range(N_HEAD):
            cs = slice(hd * HEAD, (hd + 1) * HEAD)
            dmb = dmix_s[:, cs].astype(BF16)
            dvn_s[:, cs] = _dot(wmt_ref[hd], dmb)
            dws_ref[hd] += _dot_nt(dmb, vn_s[:, cs].astype(BF16))

        def phase5(g, _):
            rows = _rows(g)
            dvn = dvn_s[rows, :]
            vh = vh_s[rows, :]
            acc_s[V_LN_G] += dvn * vh
            acc_s[V_LN_B] += dvn
            dvh = dvn * lng
            rs = rs_s[rows, 0:1]
            dvg = rs * (dvh - _mean_last(dvh) - vh * _mean_last(dvh * vh))
            v = z_ref[rows, D_BR:2 * D_BR]
            _, tv = _gelu(v)
            dz_s[rows, D_BR:2 * D_BR] = dvg * _gelu_grad(v, tv)
            return 0

        _loop(N_GROUP, phase5, 0)

        ba, bx = ba_ref[...], bx_ref[...]
        sp8 = LRU_C * _softplus(-lam_ref[...])

        def phase6(k, carry):
            cg, ca = carry
            g = N_GROUP - 1 - k
            rows = _rows(g)
            first_row = jnp.logical_and(jnp.logical_and(first_chunk, g == 0), rid == 0)
            r, i, a, mult = _lru_gates(pre_s[rows, 0:D_BR], pre_s[rows, D_BR:2 * D_BR], ba, bx, sp8, first_row)
            a_nx = jnp.where(rid < ROWS - 1, pltpu.roll(a, ROWS - 1, 0), ca)
            aa, bb = a_nx, dho_s[rows, :]
            for d in (1, 2, 4):
                a_sh = jnp.where(rid < ROWS - d, pltpu.roll(aa, ROWS - d, 0), 1.0)
                b_sh = jnp.where(rid < ROWS - d, pltpu.roll(bb, ROWS - d, 0), 0.0)
                bb = aa * b_sh + bb
                aa = aa * a_sh
            gg = bb + aa * cg
            hh = h_ref[rows, :]
            hprev = _shift_down(hh, prev_rows(h_ref, slice(None), g, h_halo), 1, rid)
            xc = xc_s[rows, :]
            gx = gg * xc
            dla = gg * hprev * a - jnp.where(first_row, 0.0, gx * i * (a * a) / mult)
            acc_s[V_LAM] += -(dla * r)
            dpa = -(dla * sp8) * r * (1.0 - r)
            dpx = gx * mult * i * (1.0 - i)
            acc_s[V_B_A] += dpa
            acc_s[V_B_X] += dpx
            dpre_s[rows, 0:D_BR] = dpa
            dpre_s[rows, D_BR:2 * D_BR] = dpx
            dxc_s[rows, :] = gg * mult * i
            return _bcast_row(gg, 0), _bcast_row(a, 0)

        cg, ca = _loop(N_GROUP, phase6, (cg_s[...], ca_s[...]))
        cg_s[...] = cg
        ca_s[...] = ca

        for hd in range(N_HEAD):
            cs = slice(hd * HEAD, (hd + 1) * HEAD)
            dpre = jnp.concatenate([dpre_s[:, cs], dpre_s[:, D_BR + hd * HEAD:D_BR + (hd + 1) * HEAD]], axis=1).astype(BF16)
            dxc_s[:, cs] += _dot(dpre, waxt_ref[hd])
            dwax_ref[hd] += _dot_tn(xc_s[:, cs].astype(BF16), dpre)

        def phase8(k, nxt):
            g = N_GROUP - 1 - k
            rows = _rows(g)
            dxc = dxc_s[rows, :]
            acc_s[V_CONV_B] += dxc
            xb = z_ref[rows, 3 * D_BR:4 * D_BR]
            xb_prev = prev_rows(z_ref, slice(3 * D_BR, 4 * D_BR), g, xb_halo)
            dxb = cw_ref[3:4, :] * dxc
            acc_s[V_CONV_W + 3] += dxc * xb
            for j in range(1, CONV_W):
                dxb = dxb + cw_ref[3 - j:4 - j, :] * _shift_up(dxc, nxt, j, rid)
                acc_s[V_CONV_W + 3 - j] += dxc * _shift_down(xb, xb_prev, j, rid)
            dz_s[rows, 3 * D_BR:4 * D_BR] = dxb
            return dxc

        dxchalo_s[...] = _loop(N_GROUP, phase8, dxchalo_s[...])
        dz_ref[...] = dz_s[...].astype(BF16)

        @pl.when(step == n_chunk - 1)
        def _():
            for v in range(N_VEC):
                vecs_ref[v:v + 1, :] = jnp.sum(acc_s[v], axis=0, keepdims=True)
            lam = lam_ref[...]
            vecs_ref[V_LAM:V_LAM + 1, :] = vecs_ref[V_LAM:V_LAM + 1, :] * (-LRU_C * _sig(-lam))
            tril = (lax.broadcasted_iota(jnp.int32, (HEAD, HEAD), 0) >= lax.broadcasted_iota(jnp.int32, (HEAD, HEAD), 1))
            ones = jnp.ones((ROWS, HEAD), BF16)
            for hd in range(N_HEAD):
                cs = slice(hd * HEAD, (hd + 1) * HEAD)
                dws_ref[hd] = jnp.where(tril, dws_ref[hd], 0.0)
                blk = accdm_s[:, cs]
                hi = blk.astype(BF16)
                lo = (blk - hi.astype(F32)).astype(BF16)
                dbs_ref[hd:hd + 1, :] = (_dot_nt(ones, hi) + _dot_nt(ones, lo))[0:1, :]

    vec = pl.BlockSpec((1, D_BR), lambda i: (0, 0))
    rev = lambda i: (n_chunk - 1 - i, 0)
    halo = lambda col: (lambda i: (jnp.maximum((n_chunk - 1 - i) * halo_blocks - 1, 0), col))
    full3 = lambda a, b, c: pl.BlockSpec((a, b, c), lambda i: (0, 0, 0))
    big = lambda w: pltpu.VMEM((CHUNK, w), F32)
    return pl.pallas_call(
        body, name="mix_bwd", grid=(n_chunk,),
        in_specs=[pl.BlockSpec((CHUNK, D_IN), rev), pl.BlockSpec((ROWS, D_BR), halo(3)),
                  pl.BlockSpec((CHUNK, 2 * D_BR), rev), pl.BlockSpec((CHUNK, D_BR), rev),
                  pl.BlockSpec((ROWS, D_BR), halo(0)), vec, vec,
                  full3(N_HEAD, HEAD, HEAD), full3(N_HEAD, HEAD, HEAD),
                  pl.BlockSpec((CHUNK, D_BR), lambda i: (0, 0)), pl.BlockSpec((ROWS, D_BR), lambda i: (0, 0)), vec,
                  full3(N_HEAD, HEAD, 2 * HEAD), full3(N_HEAD, 2 * HEAD, HEAD), vec, vec, vec, vec, vec],
        out_specs=[pl.BlockSpec((CHUNK, D_IN), rev), pl.BlockSpec((N_VEC, D_BR), lambda i: (0, 0)),
                   full3(N_HEAD, HEAD, HEAD), full3(N_HEAD, HEAD, 2 * HEAD),
                   pl.BlockSpec((N_HEAD, HEAD), lambda i: (0, 0))],
        out_shape=[SDS((t_len, D_IN), BF16), SDS((N_VEC, D_BR), F32), SDS((N_HEAD, HEAD, HEAD), F32),
                   SDS((N_HEAD, HEAD, 2 * HEAD), F32), SDS((N_HEAD, HEAD), F32)],
        scratch_shapes=[big(D_BR), big(D_BR), big(HEAD), big(D_BR), big(D_BR), big(2 * D_BR), big(D_BR), big(D_BR),
                        big(D_BR), big(D_BR), big(2 * D_BR), big(D_IN),
                        pltpu.VMEM((N_VEC, ROWS, D_BR), F32), big(D_BR),
                        pltpu.VMEM((ROWS, D_BR), F32), pltpu.VMEM((ROWS, D_BR), F32), pltpu.VMEM((ROWS, D_BR), F32)],
        compiler_params=_params(("arbitrary",), 48),
    )(z, z, dy, h, h, ln_g, ln_b, wm, wm_t, bias, cw, cb, wax, wax_t, ba, bx, lam, goa, gob)


def _in_bwd(dz, w_in_g, x, dh1, pre_g, ex_arrs, ex_scatter, tm=512):
    t_len = x.shape[0]
    n_tile = t_len // tm
    ex = _Exchange(ex_arrs, ex_scatter)

    def body(dz_ref, w_ref, x_ref, dh1_ref, g_ref, *refs):
        ex_in, (gx_ref, dg_ref), ex_out = refs[:ex.n], refs[ex.n:ex.n + 2], refs[ex.n + 2:2 * ex.n + 2]
        acc_s, dg_s = refs[2 * ex.n + 2:2 * ex.n + 4]
        ex_sems = refs[2 * ex.n + 4:]
        i, k = pl.program_id(0), pl.program_id(1)

        @pl.when(jnp.logical_and(i == 0, k == 0))
        def _():
            ex.start(ex_in, ex_out, ex_sems)
            dg_s[...] = jnp.zeros_like(dg_s)

        part = _dot_nt(dz_ref[...], w_ref[...])

        @pl.when(k == 0)
        def _():
            acc_s[...] = part

        @pl.when(k > 0)
        def _():
            acc_s[...] += part

        @pl.when(k == N_DEV - 1)
        def _():
            g = g_ref[...]

            def rows_body(q, acc):
                rows = _tile_rows(q)
                xv = x_ref[rows, :]
                r = lax.rsqrt(_mean_last(xv * xv) + EPS)
                xh = xv * r
                dhn = acc_s[rows, :]
                dg = dhn * g
                gx_ref[rows, :] = dh1_ref[rows, :] + r * (dg - xh * _mean_last(dg * xh))
                return acc + _fold_rows(dhn * xh)

            dg_s[...] = _loop(tm // TILE_ROWS, rows_body, dg_s[...])

        @pl.when(jnp.logical_and(i == n_tile - 1, k == N_DEV - 1))
        def _():
            dg_ref[...] = jnp.sum(dg_s[...], axis=0, keepdims=True)
            ex.wait(ex_in, ex_out, ex_sems)

    tile = pl.BlockSpec((tm, D_MODEL), lambda i, k: (i, 0))
    vec = pl.BlockSpec((1, D_MODEL), lambda i, k: (0, 0))
    res = pl.pallas_call(
        body, name="in_bwd", grid=(n_tile, N_DEV),
        in_specs=[pl.BlockSpec((tm, W_IN_SHARD), lambda i, k: (i, k)),
                  pl.BlockSpec((None, D_MODEL, W_IN_SHARD), lambda i, k: (k, 0, 0)), tile, tile, vec] + [ANY_SPEC] * ex.n,
        out_specs=[tile, vec] + [ANY_SPEC] * ex.n,
        out_shape=[SDS((t_len, D_MODEL), F32), SDS((1, D_MODEL), F32)] + ex.out_shape,
        scratch_shapes=[pltpu.VMEM((tm, D_MODEL), F32), pltpu.VMEM((ROWS, D_MODEL), F32)] + ex.scratch,
        compiler_params=_params(("arbitrary", "arbitrary"), 48),
    )(dz, w_in_g, x, dh1, pre_g, *ex_arrs)
    return res[0], res[1], res[2:]


def _grad_w(a, b, bn, shard_major, name, tk=1024, ex_arrs=(), ex_scatter=()):
    t_len, m = a.shape
    n = b.shape[1]
    n_j, n_k = n // bn, t_len // tk
    ex = _Exchange(ex_arrs, ex_scatter)

    def body(a_ref, b_ref, *refs):
        ex_in, o_ref, ex_out = refs[:ex.n], refs[ex.n], refs[ex.n + 1:2 * ex.n + 1]
        acc_s, ex_sems = refs[2 * ex.n + 1], refs[2 * ex.n + 2:]
        j, k = pl.program_id(0), pl.program_id(1)
        if ex.n:
            @pl.when(jnp.logical_and(j == 0, k == 0))
            def _():
                ex.start(ex_in, ex_out, ex_sems)

        part = _dot_tn(a_ref[...], b_ref[...])

        @pl.when(k == 0)
        def _():
            acc_s[...] = part

        @pl.when(k > 0)
        def _():
            acc_s[...] += part

        @pl.when(k == n_k - 1)
        def _():
            o_ref[...] = acc_s[...].astype(BF16)

        if ex.n:
            @pl.when(jnp.logical_and(j == n_j - 1, k == n_k - 1))
            def _():
                ex.wait(ex_in, ex_out, ex_sems)

    if shard_major:
        out_spec, out_shape = pl.BlockSpec((None, m, bn), lambda j, k: (j, 0, 0)), SDS((n_j, m, bn), BF16)
    else:
        out_spec, out_shape = pl.BlockSpec((m, bn), lambda j, k: (0, j)), SDS((m, n), BF16)
    res = pl.pallas_call(
        body, name=name, grid=(n_j, n_k),
        in_specs=[pl.BlockSpec((tk, m), lambda j, k: (k, 0)), pl.BlockSpec((tk, bn), lambda j, k: (k, j))]
        + [ANY_SPEC] * ex.n,
        out_specs=[out_spec] + [ANY_SPEC] * ex.n, out_shape=[out_shape] + ex.out_shape,
        scratch_shapes=[pltpu.VMEM((m, bn), F32)] + (ex.scratch if ex.n else []),
        compiler_params=_params(("arbitrary", "arbitrary"), 40),
    )(a, b, *ex_arrs)
    return res[0], res[1:]


def _adamw(parts, w, m, v, name, tr):
    rows, cols = w.shape
    c1 = 1.0 - ADAM_B1 ** ADAM_STEP
    c2 = 1.0 - ADAM_B2 ** ADAM_STEP

    def body(p_ref, w_ref, m_ref, v_ref, g_ref, d_ref, nm_ref, nv_ref):
        g = p_ref[0].astype(F32)
        for s in range(1, N_DEV):
            g = g + p_ref[s].astype(F32)
        g_ref[...] = g
        nm = ADAM_B1 * m_ref[...] + (1.0 - ADAM_B1) * g
        nv = ADAM_B2 * v_ref[...] + (1.0 - ADAM_B2) * (g * g)
        nm_ref[...] = nm
        nv_ref[...] = nv
        d_ref[...] = -ADAM_LR * ((nm / c1) / (jnp.sqrt(nv / c2) + ADAM_EPS) + ADAM_WD * w_ref[...])

    tile = pl.BlockSpec((tr, cols), lambda i: (i, 0))
    return pl.pallas_call(
        body, name=name, grid=(rows // tr,),
        in_specs=[pl.BlockSpec((N_DEV, tr, cols), lambda i: (0, i, 0)), tile, tile, tile],
        out_specs=[tile] * 4, out_shape=[SDS((rows, cols), F32)] * 4,
        compiler_params=_params(("arbitrary",), 40),
    )(parts, w, m, v)


PACKED = ("gmlp_ln_g", "gmlp_ln_b", "gmlp_ws", "gmlp_bs", "conv_b", "w_a", "b_a", "w_x", "b_x", "lam", "gmlp_out_g",
          "lru_out_g", "post_g")
WEIGHTS = ("pre_g", "w_in", "gmlp_ln_g", "gmlp_ln_b", "gmlp_ws", "gmlp_bs", "conv_w", "conv_b", "w_a", "b_a", "w_x",
           "b_x", "lam", "gmlp_out_g", "lru_out_g", "w_out", "post_g", "w_pe", "w_pg")
LANES = 128


PACK_ROWS = 3200
PACK_TILE = 640


def _pack(parts):
    rows = [p.reshape(-1, LANES) for p in parts]
    used = sum(r.shape[0] for r in rows)
    return jnp.concatenate(rows + [jnp.zeros((PACK_ROWS - used, LANES), F32)], axis=0)


def _pad_rows(a, rows):
    return jnp.concatenate([a, jnp.zeros((rows - a.shape[0],) + a.shape[1:], a.dtype)], axis=0)


def kernel(x, p, pre_g, w_in, gmlp_ln_g, gmlp_ln_b, gmlp_ws, gmlp_bs, conv_w, conv_b, w_a, b_a, w_x, b_x, lam, gmlp_out_g, lru_out_g, w_out, post_g, w_pe, w_pg, loss_target, m_pre_g, m_w_in, m_gmlp_ln_g, m_gmlp_ln_b, m_gmlp_ws, m_gmlp_bs, m_conv_w, m_conv_b, m_w_a, m_b_a, m_w_x, m_b_x, m_lam, m_gmlp_out_g, m_lru_out_g, m_w_out, m_post_g, m_w_pe, m_w_pg, v_pre_g, v_w_in, v_gmlp_ln_g, v_gmlp_ln_b, v_gmlp_ws, v_gmlp_bs, v_conv_w, v_conv_b, v_w_a, v_b_a, v_w_x, v_b_x, v_lam, v_gmlp_out_g, v_lru_out_g, v_w_out, v_post_g, v_w_pe, v_w_pg):
    args = dict(locals())
    weights = {n: args[n] for n in WEIGHTS}
    m_in = {n: args["m_" + n] for n in WEIGHTS}
    v_in = {n: args["v_" + n] for n in WEIGHTS}
    sm = {n: weights[n][0] for n in PACKED}
    shard_rows = D_MODEL // N_DEV
    xs, ps, tgt = x[0], p[0, 0], loss_target[0]

    vec = lambda a: a.reshape(1, -1)
    tril = jnp.tril(jnp.ones((CHUNK, CHUNK), dtype=bool))
    wm32 = jnp.where(tril[None], sm["gmlp_ws"], 0.0)
    wm, wm_t = wm32.astype(BF16), jnp.swapaxes(wm32, 1, 2).astype(BF16)
    bias = jnp.repeat(sm["gmlp_bs"].T, HEAD, axis=1)
    wax32 = jnp.concatenate([sm["w_a"], sm["w_x"]], axis=2)
    wax, wax_t = wax32.astype(BF16), jnp.swapaxes(wax32, 1, 2).astype(BF16)
    ln_g, ln_b = vec(sm["gmlp_ln_g"]), vec(sm["gmlp_ln_b"])
    post_g_v = vec(sm["post_g"])

    hn = _pre_norm(xs, pre_g)
    cw_shard = _pad_rows(conv_w.reshape(CONV_W, HEAD), ROWS)
    z, w_in_g, (w_out_g, w_pe_g, w_pg_g, cw_g) = _in_proj(
        hn, w_in[0].astype(BF16), [w_out[0].astype(BF16), w_pe[0].astype(BF16), w_pg[0].astype(BF16), cw_shard])
    w_out_f, w_pg_f = w_out_g.reshape(D_MODEL, D_MODEL), w_pg_g.reshape(D_MODEL, D_MODEL)
    cw_full = jnp.transpose(cw_g[:, :CONV_W, :], (1, 0, 2)).reshape(CONV_W, D_BR)
    mixer_consts = dict(cw=_pad_rows(cw_full, ROWS), cb=vec(sm["conv_b"]), ba=vec(sm["b_a"]), bx=vec(sm["b_x"]),
                        lam=vec(sm["lam"]), goa=vec(sm["gmlp_out_g"]), gob=vec(sm["lru_out_g"]))
    y, h = _mix_fwd(z, ln_g, ln_b, wm, bias, wax=wax, **mixer_consts)
    h1, ob = _out_proj(y, xs, w_out_f, post_g_v)
    dh2, dpe, dgl, h1b, loss_part = _ple_loss(h1, ps, tgt, w_pg_f, w_pe_g)
    loss = lax.psum(loss_part[0, 0], MESH_AXES)

    dh1, do, dy, d_post_g = _tail_bwd(dh2, dgl, ob, w_pg_f, w_out_f, post_g_v)
    d_w_out, _ = _grad_w(y, do, 512, False, "grad_w_out")
    d_w_pg, _ = _grad_w(h1b, dgl, 512, False, "grad_w_pg")
    d_w_pe, _ = _grad_w(ps.astype(BF16), dpe, shard_rows, True, "grad_w_pe")
    dz, vecs, d_ws, d_wax, d_bs = _mix_bwd(z, dy, h, ln_g, ln_b, wm, wm_t, bias, wax=wax, wax_t=wax_t, **mixer_consts)
    d_w_in, (parts_out, parts_pg, parts_pe) = _grad_w(
        hn, dz, W_IN_SHARD, True, "grad_w_in",
        ex_arrs=[d_w_out.reshape(N_DEV, shard_rows, D_MODEL), d_w_pg.reshape(N_DEV, shard_rows, D_MODEL), d_w_pe],
        ex_scatter=[True, True, True])

    small = {"gmlp_ln_g": vecs[V_LN_G], "gmlp_ln_b": vecs[V_LN_B], "gmlp_ws": d_ws, "gmlp_bs": d_bs,
             "conv_b": vecs[V_CONV_B], "w_a": d_wax[:, :, :HEAD], "b_a": vecs[V_B_A], "w_x": d_wax[:, :, HEAD:],
             "b_x": vecs[V_B_X], "lam": vecs[V_LAM], "gmlp_out_g": vecs[V_GOUT_A], "lru_out_g": vecs[V_GOUT_B],
             "post_g": d_post_g}
    d_cw_blocks = jnp.transpose(vecs[V_CONV_W:V_CONV_W + CONV_W].reshape(CONV_W, N_DEV, HEAD), (1, 0, 2))
    d_cw_blocks = jnp.concatenate([d_cw_blocks, jnp.zeros((N_DEV, ROWS - CONV_W, HEAD), F32)], axis=1)
    grad_x, d_pre_g, (parts_in, parts_cw, parts_small) = _in_bwd(
        dz, w_in_g, xs, dh1, pre_g,
        ex_arrs=[d_w_in, d_cw_blocks, _pack([small[n] for n in PACKED])], ex_scatter=[True, True, False])
    pre_rows = D_MODEL // LANES
    parts_pre = _exchange([d_pre_g.reshape(pre_rows, LANES)], False, "gather_pre_g")[0]

    pad_cw = lambda a: _pad_rows(a.reshape(CONV_W, HEAD), ROWS)
    flat = lambda a: a.reshape(pre_rows, LANES)
    outs = {
        "w_in": _adamw(parts_in, w_in[0], m_w_in[0], v_w_in[0], "adamw_w_in", 256),
        "w_out": _adamw(parts_out, w_out[0], m_w_out[0], v_w_out[0], "adamw_w_out", 128),
        "w_pe": _adamw(parts_pe, w_pe[0], m_w_pe[0], v_w_pe[0], "adamw_w_pe", 256),
        "w_pg": _adamw(parts_pg, w_pg[0], m_w_pg[0], v_w_pg[0], "adamw_w_pg", 128),
        "conv_w": [a[:CONV_W] for a in
                   _adamw(parts_cw, pad_cw(conv_w), pad_cw(m_conv_w), pad_cw(v_conv_w), "adamw_conv_w", ROWS)],
        "pre_g": _adamw(parts_pre, flat(pre_g), flat(m_pre_g), flat(v_pre_g), "adamw_pre_g", pre_rows),
    }
    packed = _adamw(parts_small, _pack([weights[n] for n in PACKED]), _pack([m_in[n] for n in PACKED]),
                    _pack([v_in[n] for n in PACKED]), "adamw_small", PACK_TILE)
    row = 0
    for n in PACKED:
        n_rows = weights[n].size // LANES
        outs[n] = [packed[q][row:row + n_rows] for q in range(4)]
        row += n_rows

    result = [loss, grad_x[None]]
    for q in range(4):
        result += [outs[n][q].reshape(weights[n].shape) for n in WEIGHTS]
    return tuple(result)
```

```python
import functools

import jax
import jax.numpy as jnp
from jax import lax
from jax.experimental import pallas as pl
from jax.experimental.pallas import tpu as pltpu

F32 = jnp.float32
BF16 = jnp.bfloat16
SDS = jax.ShapeDtypeStruct

D_MODEL = 2048
D_BR = 1024
D_IN = 5 * D_BR
D_PLE = 256
N_HEAD = 8
HEAD = 128
CHUNK = 128
ROWS = 8
N_GROUP = CHUNK // ROWS
N_DEV = 8
W_IN_SHARD = D_IN // N_DEV
EPS = 1e-6
LRU_C = 8.0
CONV_W = 4
MESH_AXES = ("x", "y", "c")
MIB = 1 << 20

ADAM_LR, ADAM_B1, ADAM_B2, ADAM_EPS, ADAM_WD, ADAM_STEP = 0.001, 0.9, 0.999, 1e-08, 0.01, 10

_GELU_C = 0.7978845608028654
_GELU_A = 0.044715

V_LN_G, V_LN_B, V_CONV_B, V_B_A, V_B_X, V_LAM, V_GOUT_A, V_GOUT_B, V_CONV_W = 0, 1, 2, 3, 4, 5, 6, 7, 8
N_VEC = 16


def _params(sem, vmem_mib):
    return pltpu.CompilerParams(dimension_semantics=sem, vmem_limit_bytes=int(vmem_mib * MIB))


def _sig(x):
    return 0.5 * jnp.tanh(0.5 * x) + 0.5


def _gelu(x):
    t = jnp.tanh(_GELU_C * (x + _GELU_A * x * x * x))
    return 0.5 * x * (1.0 + t), t


def _gelu_grad(x, t):
    return 0.5 * (1.0 + t) + 0.5 * x * (1.0 - t * t) * (_GELU_C * (1.0 + 3.0 * _GELU_A * x * x))


def _neg_expm1(y, exp_y):
    series = -y * (1.0 + y * (0.5 + y * (1.0 / 6.0)))
    return jnp.where(y > -0.01, series, 1.0 - exp_y)


def _softplus(x):
    return jnp.maximum(x, 0.0) + jnp.log(1.0 + jnp.exp(-jnp.abs(x)))


def _row_ids(width):
    return lax.broadcasted_iota(jnp.int32, (ROWS, width), 0)


def _shift_down(cur, prev, k, rid):
    return jnp.where(rid >= k, pltpu.roll(cur, k, 0), pltpu.roll(prev, k, 0))


def _shift_up(cur, nxt, k, rid):
    return jnp.where(rid < ROWS - k, pltpu.roll(cur, ROWS - k, 0), pltpu.roll(nxt, ROWS - k, 0))


def _mean_last(x):
    return jnp.mean(x, axis=-1, keepdims=True)


def _rows(g):
    return pl.ds(pl.multiple_of(g * ROWS, ROWS), ROWS)


TILE_ROWS = 16


def _tile_rows(q):
    return pl.ds(pl.multiple_of(q * TILE_ROWS, TILE_ROWS), TILE_ROWS)


UNROLL = 4


def _loop(n, body, init, unroll=UNROLL):
    def wide(i, carry):
        for u in range(unroll):
            carry = body(i * unroll + u, carry)
        return carry

    return lax.fori_loop(0, n // unroll, wide, init)


def _fold_rows(x):
    return x[0:ROWS, :] + x[ROWS:TILE_ROWS, :]


def _bcast_row(x, r):
    return jnp.broadcast_to(x[r:r + 1, :], x.shape)


def _dot(a, b):
    return jnp.dot(a, b, preferred_element_type=F32)


def _dot_nt(a, b):
    return lax.dot_general(a, b, (((1,), (1,)), ((), ())), preferred_element_type=F32)


def _dot_tn(a, b):
    return lax.dot_general(a, b, (((0,), (0,)), ((), ())), preferred_element_type=F32)


def _mesh_place():
    x, y, c = lax.axis_index("x"), lax.axis_index("y"), lax.axis_index("c")
    return x, y, c, 4 * x + 2 * y + c


def _peer(x, y, c, k):
    px = 1 - x if k & 4 else x
    py = 1 - y if k & 2 else y
    pc = 1 - c if k & 1 else c
    return (px, py, pc), 4 * px + 2 * py + pc


def _remote(src, dst, send_sem, recv_sem, dev):
    return pltpu.make_async_remote_copy(src_ref=src, dst_ref=dst, send_sem=send_sem, recv_sem=recv_sem, device_id=dev,
                                        device_id_type=pl.DeviceIdType.MESH)


ANY_SPEC = pl.BlockSpec(memory_space=pl.ANY)


class _Exchange:
    def __init__(self, arrs, scatter):
        self.n = len(arrs)
        self.scatter = tuple(scatter)
        self.out_shape = [SDS(a.shape if s else (N_DEV,) + a.shape, a.dtype) for a, s in zip(arrs, scatter)]
        self.scratch = [pltpu.SemaphoreType.DMA((self.n * N_DEV,)), pltpu.SemaphoreType.DMA((self.n * N_DEV,)),
                        pltpu.SemaphoreType.DMA((self.n,))]

    def _copies(self, ins, outs, sems):
        send_sems, recv_sems, local_sems = sems
        x, y, c, me = _mesh_place()
        local, sends, recvs = [], [], []
        for a in range(self.n):
            src = ins[a].at[me] if self.scatter[a] else ins[a]
            local.append(pltpu.make_async_copy(src, outs[a].at[me], local_sems.at[a]))
        for k in range(1, N_DEV):
            dev, lin = _peer(x, y, c, k)
            for a in range(self.n):
                src = ins[a].at[lin] if self.scatter[a] else ins[a]
                pair = (send_sems.at[a * N_DEV + k], recv_sems.at[a * N_DEV + k], dev)
                sends.append(_remote(src, outs[a].at[me], *pair))
                recvs.append(_remote(src, outs[a].at[lin], *pair))
        return local, sends, recvs

    def start(self, ins, outs, sems):
        local, sends, _ = self._copies(ins, outs, sems)
        for cp in local + sends:
            cp.start()

    def wait(self, ins, outs, sems):
        local, sends, recvs = self._copies(ins, outs, sems)
        for cp in recvs:
            cp.wait_recv()
        for cp in sends:
            cp.wait_send()
        for cp in local:
            cp.wait()


def _exchange(arrs, scatter, name):
    ex = _Exchange(arrs, [scatter] * len(arrs))
    n = ex.n

    def body(*refs):
        ins, outs, sems = refs[:n], refs[n:2 * n], refs[2 * n:]
        ex.start(ins, outs, sems)
        ex.wait(ins, outs, sems)

    return pl.pallas_call(
        body, name=name, out_shape=ex.out_shape, in_specs=[ANY_SPEC] * n, out_specs=[ANY_SPEC] * n,
        scratch_shapes=ex.scratch,
    )(*arrs)


def _pre_norm(x, pre_g, tm=512):
    t_len = x.shape[0]

    def body(x_ref, g_ref, hn_ref):
        g = g_ref[...]

        def rows_body(q, _):
            rows = _tile_rows(q)
            xv = x_ref[rows, :]
            hn_ref[rows, :] = (xv * lax.rsqrt(_mean_last(xv * xv) + EPS) * g).astype(BF16)
            return 0

        _loop(tm // TILE_ROWS, rows_body, 0)

    tile = pl.BlockSpec((tm, D_MODEL), lambda i: (i, 0))
    return pl.pallas_call(
        body, name="pre_norm", grid=(t_len // tm,),
        in_specs=[tile, pl.BlockSpec((1, D_MODEL), lambda i: (0, 0))], out_specs=tile,
        out_shape=SDS((t_len, D_MODEL), BF16),
        compiler_params=_params(("arbitrary",), 24),
    )(x, pre_g)


AG_ORDER = (0, 1, 2, 4, 6, 3, 5, 7)
SIBLING = 1
ICI_MASKS = (2, 4, 6)
DIRECT_MASKS = (SIBLING,) + ICI_MASKS


def _in_proj(hn, w_shard, others, tm=512):
    t_len = hn.shape[0]
    n_i = t_len // tm
    n_o = len(others)
    me_out = 4 * lax.axis_index("x") + 2 * lax.axis_index("y") + lax.axis_index("c")
    order = jnp.stack([me_out ^ k for k in AG_ORDER]).astype(jnp.int32)

    def body(order_ref, hn_ref, w_hbm, *refs):
        o_in = refs[:n_o]
        z_ref, wg_hbm = refs[n_o], refs[n_o + 1]
        o_out = refs[n_o + 2:2 * n_o + 2]
        wbuf, send_w, recv_w, fsend_w, frecv_w, send_o, recv_o, fsend_o, frecv_o, wb_sems, loc_sems = refs[2 * n_o + 2:]
        j, i = pl.program_id(0), pl.program_id(1)
        x, y, c, me = _mesh_place()
        sib = _peer(x, y, c, SIBLING)[0]

        def direct(k, a=None):
            dev, lin = _peer(x, y, c, k)
            if a is None:
                return (_remote(w_hbm, wbuf.at[me], send_w.at[k], recv_w.at[k], dev),
                        _remote(w_hbm, wbuf.at[lin], send_w.at[k], recv_w.at[k], dev))
            pair = (send_o.at[a * N_DEV + k], recv_o.at[a * N_DEV + k], dev)
            return _remote(o_in[a], o_out[a].at[me], *pair), _remote(o_in[a], o_out[a].at[lin], *pair)

        def passed(k, a=None):
            mine, theirs = _peer(x, y, c, k)[1], _peer(x, y, c, k ^ SIBLING)[1]
            if a is None:
                pair = (fsend_w.at[k], frecv_w.at[k], sib)
                return _remote(wbuf.at[mine], wbuf.at[mine], *pair), _remote(wbuf.at[theirs], wbuf.at[theirs], *pair)
            pair = (fsend_o.at[a * N_DEV + k], frecv_o.at[a * N_DEV + k], sib)
            return (_remote(o_out[a].at[mine], o_out[a].at[mine], *pair),
                    _remote(o_out[a].at[theirs], o_out[a].at[theirs], *pair))

        def own_copies():
            return [pltpu.make_async_copy(o_in[a], o_out[a].at[me], loc_sems.at[1 + a]) for a in range(n_o)]

        @pl.when(jnp.logical_and(j == 0, i == 0))
        def _():
            own = pltpu.make_async_copy(w_hbm, wbuf.at[me], loc_sems.at[0])
            own.start()
            for cp in own_copies():
                cp.start()
            for k in DIRECT_MASKS:
                direct(k)[0].start()
            for k in DIRECT_MASKS:
                for a in range(n_o):
                    direct(k, a)[0].start()
            own.wait()

        for jj in range(1, N_DEV):
            mask = AG_ORDER[jj]

            @pl.when(jnp.logical_and(j == jj, i == 0))
            def _(jj=jj, mask=mask):
                if mask in DIRECT_MASKS:
                    direct(mask)[1].wait_recv()
                    if mask in ICI_MASKS:
                        passed(mask)[0].start()
                else:
                    passed(mask ^ SIBLING)[1].wait_recv()
                late = jj - (N_DEV - len(ICI_MASKS))
                if late >= 0:
                    for a in range(n_o):
                        direct(ICI_MASKS[late], a)[1].wait_recv()
                        passed(ICI_MASKS[late], a)[0].start()

        slot = order_ref[j]

        @pl.when(i == 0)
        def _():
            pltpu.make_async_copy(wbuf.at[slot], wg_hbm.at[slot], wb_sems.at[j]).start()

        z_ref[...] = _dot(hn_ref[...], wbuf[slot])

        @pl.when(jnp.logical_and(j == N_DEV - 1, i == n_i - 1))
        def _():
            for a in range(n_o):
                direct(SIBLING, a)[1].wait_recv()
            for k in ICI_MASKS:
                for a in range(n_o):
                    passed(k, a)[1].wait_recv()
            for k in DIRECT_MASKS:
                direct(k)[0].wait_send()
                for a in range(n_o):
                    direct(k, a)[0].wait_send()
            for k in ICI_MASKS:
                passed(k)[0].wait_send()
                for a in range(n_o):
                    passed(k, a)[0].wait_send()
            for cp in own_copies():
                cp.wait()
            for jj in range(N_DEV):
                pltpu.make_async_copy(wbuf.at[0], wg_hbm.at[0], wb_sems.at[jj]).wait()

    dma = lambda n: pltpu.SemaphoreType.DMA((n,))
    grid_spec = pltpu.PrefetchScalarGridSpec(
        num_scalar_prefetch=1, grid=(N_DEV, n_i),
        in_specs=[pl.BlockSpec((tm, D_MODEL), lambda j, i, order: (i, 0)), ANY_SPEC] + [ANY_SPEC] * n_o,
        out_specs=[pl.BlockSpec((tm, W_IN_SHARD), lambda j, i, order: (i, order[j])), ANY_SPEC] + [ANY_SPEC] * n_o,
        scratch_shapes=[pltpu.VMEM((N_DEV, D_MODEL, W_IN_SHARD), BF16), dma(N_DEV), dma(N_DEV), dma(N_DEV), dma(N_DEV),
                        dma(n_o * N_DEV), dma(n_o * N_DEV), dma(n_o * N_DEV), dma(n_o * N_DEV), dma(N_DEV), dma(1 + n_o)])
    res = pl.pallas_call(
        body, name="in_proj", grid_spec=grid_spec,
        out_shape=[SDS((t_len, D_IN), F32), SDS((N_DEV, D_MODEL, W_IN_SHARD), BF16)]
        + [SDS((N_DEV,) + o.shape, o.dtype) for o in others],
        compiler_params=_params(("arbitrary", "arbitrary"), 44),
    )(order, hn, w_shard, *others)
    return res[0], res[1], res[2:]


def _conv_rows(cur, prev, cw_ref, cb, rid):
    acc = cw_ref[3:4, :] * cur + cb
    for k in range(1, CONV_W):
        acc = acc + cw_ref[3 - k:4 - k, :] * _shift_down(cur, prev, k, rid)
    return acc


def _lru_gates(pa, px, ba, bx, sp8, first_row):
    r = _sig(pa + ba)
    i = _sig(px + bx)
    la = -(r * sp8)
    a = jnp.exp(la)
    mult = jnp.where(first_row, 1.0, jnp.sqrt(_neg_expm1(2.0 * la, a * a)))
    return r, i, a, mult


def _mix_fwd(z, ln_g, ln_b, wm, bias, cw, cb, wax, ba, bx, lam, goa, gob):
    t_len = z.shape[0]
    n_chunk = t_len // CHUNK

    def body(z_ref, lng_ref, lnb_ref, wm_ref, bias_ref, cw_ref, cb_ref, wax_ref, ba_ref, bx_ref, lam_ref, goa_ref,
             gob_ref, y_ref, h_ref, vn_s, xc_s, mixed_s, pre_s, y_s, carry_s, halo_s):
        c_id = pl.program_id(0)
        rid = _row_ids(D_BR)

        @pl.when(c_id == 0)
        def _():
            carry_s[...] = jnp.zeros_like(carry_s)
            halo_s[...] = jnp.zeros_like(halo_s)

        lng, lnb, cb = lng_ref[...], lnb_ref[...], cb_ref[...]

        def phase1(g, prev):
            rows = _rows(g)
            vg, _ = _gelu(z_ref[rows, D_BR:2 * D_BR])
            xm = vg - _mean_last(vg)
            rs = lax.rsqrt(_mean_last(xm * xm) + EPS)
            vn_s[rows, :] = xm * rs * lng + lnb
            xb = z_ref[rows, 3 * D_BR:4 * D_BR]
            xc_s[rows, :] = _conv_rows(xb, prev, cw_ref, cb, rid)
            return xb

        halo_s[...] = _loop(N_GROUP, phase1, halo_s[...])

        for h in range(N_HEAD):
            cs = slice(h * HEAD, (h + 1) * HEAD)
            mixed_s[:, cs] = _dot(wm_ref[h], vn_s[:, cs].astype(BF16))
            pre = _dot(xc_s[:, cs].astype(BF16), wax_ref[h])
            pre_s[:, cs] = pre[:, :HEAD]
            pre_s[:, D_BR + h * HEAD:D_BR + (h + 1) * HEAD] = pre[:, HEAD:]

        ba, bx, goa, gob = ba_ref[...], bx_ref[...], goa_ref[...], gob_ref[...]
        sp8 = LRU_C * _softplus(-lam_ref[...])

        def phase3(g, carry):
            rows = _rows(g)
            ug, _ = _gelu(z_ref[rows, 0:D_BR])
            ga = z_ref[rows, 2 * D_BR:3 * D_BR]
            ya = ug * (mixed_s[rows, :] + bias_ref[rows, :]) * (ga * _sig(ga))
            y_s[rows, 0:D_BR] = ya * lax.rsqrt(_mean_last(ya * ya) + EPS) * goa

            first_row = jnp.logical_and(jnp.logical_and(c_id == 0, g == 0), rid == 0)
            _, i, a, mult = _lru_gates(pre_s[rows, 0:D_BR], pre_s[rows, D_BR:2 * D_BR], ba, bx, sp8, first_row)
            b = mult * i * xc_s[rows, :]
            for d in (1, 2, 4):
                a_sh = jnp.where(rid >= d, pltpu.roll(a, d, 0), 1.0)
                b_sh = jnp.where(rid >= d, pltpu.roll(b, d, 0), 0.0)
                b = a * b_sh + b
                a = a * a_sh
            hh = b + a * carry
            h_ref[rows, :] = hh
            gb = z_ref[rows, 4 * D_BR:5 * D_BR]
            yb = hh * (gb * _sig(gb))
            y_s[rows, D_BR:2 * D_BR] = yb * lax.rsqrt(_mean_last(yb * yb) + EPS) * gob
            return _bcast_row(hh, ROWS - 1)

        carry_s[...] = _loop(N_GROUP, phase3, carry_s[...])
        y_ref[...] = y_s[...].astype(BF16)

    vec = pl.BlockSpec((1, D_BR), lambda i: (0, 0))
    return pl.pallas_call(
        body, name="mix_fwd", grid=(n_chunk,),
        in_specs=[pl.BlockSpec((CHUNK, D_IN), lambda i: (i, 0)), vec, vec,
                  pl.BlockSpec((N_HEAD, HEAD, HEAD), lambda i: (0, 0, 0)),
                  pl.BlockSpec((CHUNK, D_BR), lambda i: (0, 0)),
                  pl.BlockSpec((ROWS, D_BR), lambda i: (0, 0)), vec,
                  pl.BlockSpec((N_HEAD, HEAD, 2 * HEAD), lambda i: (0, 0, 0)), vec, vec, vec, vec, vec],
        out_specs=[pl.BlockSpec((CHUNK, 2 * D_BR), lambda i: (i, 0)), pl.BlockSpec((CHUNK, D_BR), lambda i: (i, 0))],
        out_shape=[SDS((t_len, 2 * D_BR), BF16), SDS((t_len, D_BR), F32)],
        scratch_shapes=[pltpu.VMEM((CHUNK, D_BR), F32), pltpu.VMEM((CHUNK, D_BR), F32), pltpu.VMEM((CHUNK, D_BR), F32),
                        pltpu.VMEM((CHUNK, 2 * D_BR), F32), pltpu.VMEM((CHUNK, 2 * D_BR), F32),
                        pltpu.VMEM((ROWS, D_BR), F32), pltpu.VMEM((ROWS, D_BR), F32)],
        compiler_params=_params(("arbitrary",), 32),
    )(z, ln_g, ln_b, wm, bias, cw, cb, wax, ba, bx, lam, goa, gob)


def _load_weight(w_hbm, w_vmem, sem):
    @pl.when(pl.program_id(0) == 0)
    def _():
        cp = pltpu.make_async_copy(w_hbm, w_vmem, sem)
        cp.start()
        cp.wait()


def _out_proj(y, x, w_out, post_g, tm=512):
    t_len = y.shape[0]

    def body(y_ref, x_ref, w_hbm, g_ref, h1_ref, ob_ref, w_s, o_s, sem):
        _load_weight(w_hbm, w_s, sem)
        o_s[...] = _dot(y_ref[...], w_s[...])
        g = g_ref[...]

        def rows_body(q, _):
            rows = _tile_rows(q)
            o = o_s[rows, :]
            h1_ref[rows, :] = x_ref[rows, :] + o * lax.rsqrt(_mean_last(o * o) + EPS) * g
            ob_ref[rows, :] = o.astype(BF16)
            return 0

        _loop(tm // TILE_ROWS, rows_body, 0)

    tile = pl.BlockSpec((tm, D_MODEL), lambda i: (i, 0))
    return pl.pallas_call(
        body, name="out_proj", grid=(t_len // tm,),
        in_specs=[tile, tile, pl.BlockSpec(memory_space=pl.ANY), pl.BlockSpec((1, D_MODEL), lambda i: (0, 0))],
        out_specs=[tile, tile],
        out_shape=[SDS((t_len, D_MODEL), F32), SDS((t_len, D_MODEL), BF16)],
        scratch_shapes=[pltpu.VMEM((D_MODEL, D_MODEL), BF16), pltpu.VMEM((tm, D_MODEL), F32), pltpu.SemaphoreType.DMA],
        compiler_params=_params(("arbitrary",), 44),
    )(y, x, w_out, post_g)


def _ple_loss(h1, p, tgt, w_pg, w_pe_g, tm=256):
    t_len = h1.shape[0]
    n_tile = t_len // tm
    pe_shard = D_MODEL // N_DEV

    def body(h1_ref, p_ref, t_ref, w_hbm, wpe_ref, dh2_ref, dpe_ref, dgl_ref, h1b_ref, loss_ref, w_s, pe_s, gl_s, acc_s,
             sem):
        _load_weight(w_hbm, w_s, sem)
        i = pl.program_id(0)

        @pl.when(i == 0)
        def _():
            acc_s[...] = jnp.zeros_like(acc_s)

        h1b_ref[...] = h1_ref[...].astype(BF16)
        pb = p_ref[...].astype(BF16)
        for j in range(N_DEV):
            pe_s[:, j * pe_shard:(j + 1) * pe_shard] = _dot(pb, wpe_ref[j])
        gl_s[...] = _dot(h1b_ref[...], w_s[...])

        def rows_body(q, acc):
            rows = _tile_rows(q)
            pe = pe_s[rows, :]
            g = _sig(gl_s[rows, :])
            e = h1_ref[rows, :] + pe * g - t_ref[rows, :]
            dh2 = e * (1.0 / D_MODEL)
            dh2_ref[rows, :] = dh2
            dpe_ref[rows, :] = (dh2 * g).astype(BF16)
            dgl_ref[rows, :] = (dh2 * pe * g * (1.0 - g)).astype(BF16)
            return acc + _fold_rows(e * e)

        acc_s[...] = _loop(tm // TILE_ROWS, rows_body, acc_s[...])

        @pl.when(i == n_tile - 1)
        def _():
            loss_ref[...] = jnp.full(loss_ref.shape, 0.5 / D_MODEL * jnp.sum(acc_s[...]), F32)

    tile = pl.BlockSpec((tm, D_MODEL), lambda i: (i, 0))
    return pl.pallas_call(
        body, name="ple_loss", grid=(n_tile,),
        in_specs=[tile, pl.BlockSpec((tm, D_PLE), lambda i: (i, 0)), tile, pl.BlockSpec(memory_space=pl.ANY),
                  pl.BlockSpec((N_DEV, D_PLE, pe_shard), lambda i: (0, 0, 0))],
        out_specs=[tile, tile, tile, tile, pl.BlockSpec((ROWS, HEAD), lambda i: (0, 0))],
        out_shape=[SDS((t_len, D_MODEL), F32), SDS((t_len, D_MODEL), BF16), SDS((t_len, D_MODEL), BF16),
                   SDS((t_len, D_MODEL), BF16), SDS((ROWS, HEAD), F32)],
        scratch_shapes=[pltpu.VMEM((D_MODEL, D_MODEL), BF16), pltpu.VMEM((tm, D_MODEL), F32),
                        pltpu.VMEM((tm, D_MODEL), F32), pltpu.VMEM((ROWS, D_MODEL), F32), pltpu.SemaphoreType.DMA],
        compiler_params=_params(("arbitrary",), 44),
    )(h1, p, tgt, w_pg, w_pe_g)


def _tail_bwd(dh2, dgl, ob, w_pg, w_out, post_g, tm=256):
    t_len = dh2.shape[0]
    n_tile = t_len // tm

    def body(dh2_ref, dgl_ref, ob_ref, wpg_hbm, wout_hbm, g_ref, dh1_ref, do_ref, dy_ref, dg_ref, wpg_s, wout_s, t_s,
             acc_s, sems):
        _load_weight(wpg_hbm, wpg_s, sems.at[0])
        _load_weight(wout_hbm, wout_s, sems.at[1])
        i = pl.program_id(0)

        @pl.when(i == 0)
        def _():
            acc_s[...] = jnp.zeros_like(acc_s)

        t_s[...] = _dot_nt(dgl_ref[...], wpg_s[...])
        g = g_ref[...]

        def rows_body(q, acc):
            rows = _tile_rows(q)
            dh1 = dh2_ref[rows, :] + t_s[rows, :]
            dh1_ref[rows, :] = dh1
            o = ob_ref[rows, :].astype(F32)
            rr = lax.rsqrt(_mean_last(o * o) + EPS)
            on = o * rr
            dog = dh1 * g
            do_ref[rows, :] = (rr * (dog - on * _mean_last(dog * on))).astype(BF16)
            return acc + _fold_rows(dh1 * on)

        acc_s[...] = _loop(tm // TILE_ROWS, rows_body, acc_s[...])
        dy_ref[...] = _dot_nt(do_ref[...], wout_s[...]).astype(BF16)

        @pl.when(i == n_tile - 1)
        def _():
            dg_ref[...] = jnp.sum(acc_s[...], axis=0, keepdims=True)

    tile = pl.BlockSpec((tm, D_MODEL), lambda i: (i, 0))
    vec = pl.BlockSpec((1, D_MODEL), lambda i: (0, 0))
    hbm = pl.BlockSpec(memory_space=pl.ANY)
    return pl.pallas_call(
        body, name="tail_bwd", grid=(n_tile,),
        in_specs=[tile, tile, tile, hbm, hbm, vec],
        out_specs=[tile, tile, tile, vec],
        out_shape=[SDS((t_len, D_MODEL), F32), SDS((t_len, D_MODEL), BF16), SDS((t_len, D_MODEL), BF16),
                   SDS((1, D_MODEL), F32)],
        scratch_shapes=[pltpu.VMEM((D_MODEL, D_MODEL), BF16), pltpu.VMEM((D_MODEL, D_MODEL), BF16),
                        pltpu.VMEM((tm, D_MODEL), F32), pltpu.VMEM((ROWS, D_MODEL), F32), pltpu.SemaphoreType.DMA((2,))],
        compiler_params=_params(("arbitrary",), 48),
    )(dh2, dgl, ob, w_pg, w_out, post_g)


def _mix_bwd(z, dy, h, ln_g, ln_b, wm, wm_t, bias, cw, cb, wax, wax_t, ba, bx, lam, goa, gob):
    t_len = z.shape[0]
    n_chunk = t_len // CHUNK
    halo_blocks = CHUNK // ROWS

    def body(z_ref, zhalo_ref, dy_ref, h_ref, hhalo_ref, lng_ref, lnb_ref, wm_ref, wmt_ref, bias_ref, cw_ref, cb_ref,
             wax_ref, waxt_ref, ba_ref, bx_ref, lam_ref, goa_ref, gob_ref,
             dz_ref, vecs_ref, dws_ref, dwax_ref, dbs_ref,
             vn_s, vh_s, rs_s, xc_s, mixed_s, pre_s, dmix_s, dvn_s, dho_s, dxc_s, dpre_s, dz_s, acc_s, accdm_s,
             cg_s, ca_s, dxchalo_s):
        step = pl.program_id(0)
        c_id = n_chunk - 1 - step
        rid = _row_ids(D_BR)
        first_chunk = c_id == 0

        @pl.when(step == 0)
        def _():
            acc_s[...] = jnp.zeros_like(acc_s)
            accdm_s[...] = jnp.zeros_like(accdm_s)
            cg_s[...] = jnp.zeros_like(cg_s)
            ca_s[...] = jnp.zeros_like(ca_s)
            dxchalo_s[...] = jnp.zeros_like(dxchalo_s)
            dws_ref[...] = jnp.zeros_like(dws_ref)
            dwax_ref[...] = jnp.zeros_like(dwax_ref)

        lng, lnb, cb = lng_ref[...], lnb_ref[...], cb_ref[...]
        xb_halo = jnp.where(first_chunk, 0.0, zhalo_ref[...])
        h_halo = jnp.where(first_chunk, 0.0, hhalo_ref[...])

        def prev_rows(ref, cols, g, halo):
            before = ref[pl.ds(pl.multiple_of(jnp.maximum(g - 1, 0) * ROWS, ROWS), ROWS), cols]
            return jnp.where(g > 0, before, halo)

        def phase1(g, prev):
            rows = _rows(g)
            vg, _ = _gelu(z_ref[rows, D_BR:2 * D_BR])
            xm = vg - _mean_last(vg)
            rs = lax.rsqrt(_mean_last(xm * xm) + EPS)
            vh = xm * rs
            vh_s[rows, :] = vh
            rs_s[rows, :] = jnp.broadcast_to(rs, (ROWS, HEAD))
            vn_s[rows, :] = vh * lng + lnb
            xb = z_ref[rows, 3 * D_BR:4 * D_BR]
            xc_s[rows, :] = _conv_rows(xb, prev, cw_ref, cb, rid)
            return xb

        _loop(N_GROUP, phase1, xb_halo)

        for hd in range(N_HEAD):
            cs = slice(hd * HEAD, (hd + 1) * HEAD)
            mixed_s[:, cs] = _dot(wm_ref[hd], vn_s[:, cs].astype(BF16))
            pre = _dot(xc_s[:, cs].astype(BF16), wax_ref[hd])
            pre_s[:, cs] = pre[:, :HEAD]
            pre_s[:, D_BR + hd * HEAD:D_BR + (hd + 1) * HEAD] = pre[:, HEAD:]

        goa, gob = goa_ref[...], gob_ref[...]

        def phase3(g, _):
            rows = _rows(g)
            u = z_ref[rows, 0:D_BR]
            ug, tu = _gelu(u)
            ga = z_ref[rows, 2 * D_BR:3 * D_BR]
            sga = _sig(ga)
            sa = ga * sga
            mixed = mixed_s[rows, :] + bias_ref[rows, :]
            ya0 = ug * mixed
            ya = ya0 * sa
            ra = lax.rsqrt(_mean_last(ya * ya) + EPS)
            dyan = dy_ref[rows, 0:D_BR].astype(F32)
            acc_s[V_GOUT_A] += dyan * ya * ra
            dyg = dyan * goa
            dya = ra * dyg - ya * (ra * ra * ra) * _mean_last(dyg * ya)
            dya0 = dya * sa
            dz_s[rows, 2 * D_BR:3 * D_BR] = dya * ya0 * (sga * (1.0 + ga * (1.0 - sga)))
            dmix = dya0 * ug
            dmix_s[rows, :] = dmix
            accdm_s[rows, :] += dmix
            dz_s[rows, 0:D_BR] = dya0 * mixed * _gelu_grad(u, tu)

            hh = h_ref[rows, :]
            gb = z_ref[rows, 4 * D_BR:5 * D_BR]
            sgb = _sig(gb)
            sb = gb * sgb
            yb = hh * sb
            rb = lax.rsqrt(_mean_last(yb * yb) + EPS)
            dybn = dy_ref[rows, D_BR:2 * D_BR].astype(F32)
            acc_s[V_GOUT_B] += dybn * yb * rb
            dyg = dybn * gob
            dyb = rb * dyg - yb * (rb * rb * rb) * _mean_last(dyg * yb)
            dho_s[rows, :] = dyb * sb
            dz_s[rows, 4 * D_BR:5 * D_BR] = dyb * hh * (sgb * (1.0 + gb * (1.0 - sgb)))
            return 0

        _loop(N_GROUP, phase3, 0)

        for hd in range(N_HEAD):
            cs = slice(hd * HEAD, (hd + 1) * HEAD)
            dmb = dmix_s[:, cs].astype(BF16)
            dvn_s[:, cs] = _dot(wmt_ref[hd], dmb)
            dws_ref[hd] += _dot_nt(dmb, vn_s[:, cs].astype(BF16))

        def phase5(g, _):
            rows = _rows(g)
            dvn = dvn_s[rows, :]
            vh = vh_s[rows, :]
            acc_s[V_LN_G] += dvn * vh
            acc_s[V_LN_B] += dvn
            dvh = dvn * lng
            rs = rs_s[rows, 0:1]
            dvg = rs * (dvh - _mean_last(dvh) - vh * _mean_last(dvh * vh))
            v = z_ref[rows, D_BR:2 * D_BR]
            _, tv = _gelu(v)
            dz_s[rows, D_BR:2 * D_BR] = dvg * _gelu_grad(v, tv)
            return 0

        _loop(N_GROUP, phase5, 0)

        ba, bx = ba_ref[...], bx_ref[...]
        sp8 = LRU_C * _softplus(-lam_ref[...])

        def phase6(k, carry):
            cg, ca = carry
            g = N_GROUP - 1 - k
            rows = _rows(g)
            first_row = jnp.logical_and(jnp.logical_and(first_chunk, g == 0), rid == 0)
            r, i, a, mult = _lru_gates(pre_s[rows, 0:D_BR], pre_s[rows, D_BR:2 * D_BR], ba, bx, sp8, first_row)
            a_nx = jnp.where(rid < ROWS - 1, pltpu.roll(a, ROWS - 1, 0), ca)
            aa, bb = a_nx, dho_s[rows, :]
            for d in (1, 2, 4):
                a_sh = jnp.where(rid < ROWS - d, pltpu.roll(aa, ROWS - d, 0), 1.0)
                b_sh = jnp.where(rid < ROWS - d, pltpu.roll(bb, ROWS - d, 0), 0.0)
                bb = aa * b_sh + bb
                aa = aa * a_sh
            gg = bb + aa * cg
            hh = h_ref[rows, :]
            hprev = _shift_down(hh, prev_rows(h_ref, slice(None), g, h_halo), 1, rid)
            xc = xc_s[rows, :]
            gx = gg * xc
            dla = gg * hprev * a - jnp.where(first_row, 0.0, gx * i * (a * a) * lax.rsqrt(mult * mult))
            acc_s[V_LAM] += -(dla * r)
            dpa = -(dla * sp8) * r * (1.0 - r)
            dpx = gx * mult * i * (1.0 - i)
            acc_s[V_B_A] += dpa
            acc_s[V_B_X] += dpx
            dpre_s[rows, 0:D_BR] = dpa
            dpre_s[rows, D_BR:2 * D_BR] = dpx
            dxc_s[rows, :] = gg * mult * i
            return _bcast_row(gg, 0), _bcast_row(a, 0)

        cg, ca = _loop(N_GROUP, phase6, (cg_s[...], ca_s[...]))
        cg_s[...] = cg
        ca_s[...] = ca

        for hd in range(N_HEAD):
            cs = slice(hd * HEAD, (hd + 1) * HEAD)
            dpre = jnp.concatenate([dpre_s[:, cs], dpre_s[:, D_BR + hd * HEAD:D_BR + (hd + 1) * HEAD]], axis=1).astype(BF16)
            dxc_s[:, cs] += _dot(dpre, waxt_ref[hd])
            dwax_ref[hd] += _dot_tn(xc_s[:, cs].astype(BF16), dpre)

        def phase8(k, nxt):
            g = N_GROUP - 1 - k
            rows = _rows(g)
            dxc = dxc_s[rows, :]
            acc_s[V_CONV_B] += dxc
            xb = z_ref[rows, 3 * D_BR:4 * D_BR]
            xb_prev = prev_rows(z_ref, slice(3 * D_BR, 4 * D_BR), g, xb_halo)
            dxb = cw_ref[3:4, :] * dxc
            acc_s[V_CONV_W + 3] += dxc * xb
            for j in range(1, CONV_W):
                dxb = dxb + cw_ref[3 - j:4 - j, :] * _shift_up(dxc, nxt, j, rid)
                acc_s[V_CONV_W + 3 - j] += dxc * _shift_down(xb, xb_prev, j, rid)
            dz_s[rows, 3 * D_BR:4 * D_BR] = dxb
            return dxc

        dxchalo_s[...] = _loop(N_GROUP, phase8, dxchalo_s[...])
        dz_ref[...] = dz_s[...].astype(BF16)

        @pl.when(step == n_chunk - 1)
        def _():
            for v in range(N_VEC):
                vecs_ref[v:v + 1, :] = jnp.sum(acc_s[v], axis=0, keepdims=True)
            lam = lam_ref[...]
            vecs_ref[V_LAM:V_LAM + 1, :] = vecs_ref[V_LAM:V_LAM + 1, :] * (-LRU_C * _sig(-lam))
            tril = (lax.broadcasted_iota(jnp.int32, (HEAD, HEAD), 0) >= lax.broadcasted_iota(jnp.int32, (HEAD, HEAD), 1))
            ones = jnp.ones((ROWS, HEAD), BF16)
            for hd in range(N_HEAD):
                cs = slice(hd * HEAD, (hd + 1) * HEAD)
                dws_ref[hd] = jnp.where(tril, dws_ref[hd], 0.0)
                blk = accdm_s[:, cs]
                hi = blk.astype(BF16)
                lo = (blk - hi.astype(F32)).astype(BF16)
                dbs_ref[hd:hd + 1, :] = (_dot_nt(ones, hi) + _dot_nt(ones, lo))[0:1, :]

    vec = pl.BlockSpec((1, D_BR), lambda i: (0, 0))
    rev = lambda i: (n_chunk - 1 - i, 0)
    halo = lambda col: (lambda i: (jnp.maximum((n_chunk - 1 - i) * halo_blocks - 1, 0), col))
    full3 = lambda a, b, c: pl.BlockSpec((a, b, c), lambda i: (0, 0, 0))
    big = lambda w: pltpu.VMEM((CHUNK, w), F32)
    return pl.pallas_call(
        body, name="mix_bwd", grid=(n_chunk,),
        in_specs=[pl.BlockSpec((CHUNK, D_IN), rev), pl.BlockSpec((ROWS, D_BR), halo(3)),
                  pl.BlockSpec((CHUNK, 2 * D_BR), rev), pl.BlockSpec((CHUNK, D_BR), rev),
                  pl.BlockSpec((ROWS, D_BR), halo(0)), vec, vec,
                  full3(N_HEAD, HEAD, HEAD), full3(N_HEAD, HEAD, HEAD),
                  pl.BlockSpec((CHUNK, D_BR), lambda i: (0, 0)), pl.BlockSpec((ROWS, D_BR), lambda i: (0, 0)), vec,
                  full3(N_HEAD, HEAD, 2 * HEAD), full3(N_HEAD, 2 * HEAD, HEAD), vec, vec, vec, vec, vec],
        out_specs=[pl.BlockSpec((CHUNK, D_IN), rev), pl.BlockSpec((N_VEC, D_BR), lambda i: (0, 0)),
                   full3(N_HEAD, HEAD, HEAD), full3(N_HEAD, HEAD, 2 * HEAD),
                   pl.BlockSpec((N_HEAD, HEAD), lambda i: (0, 0))],
        out_shape=[SDS((t_len, D_IN), BF16), SDS((N_VEC, D_BR), F32), SDS((N_HEAD, HEAD, HEAD), F32),
                   SDS((N_HEAD, HEAD, 2 * HEAD), F32), SDS((N_HEAD, HEAD), F32)],
        scratch_shapes=[big(D_BR), big(D_BR), big(HEAD), big(D_BR), big(D_BR), big(2 * D_BR), big(D_BR), big(D_BR),
                        big(D_BR), big(D_BR), big(2 * D_BR), big(D_IN),
                        pltpu.VMEM((N_VEC, ROWS, D_BR), F32), big(D_BR),
                        pltpu.VMEM((ROWS, D_BR), F32), pltpu.VMEM((ROWS, D_BR), F32), pltpu.VMEM((ROWS, D_BR), F32)],
        compiler_params=_params(("arbitrary",), 48),
    )(z, z, dy, h, h, ln_g, ln_b, wm, wm_t, bias, cw, cb, wax, wax_t, ba, bx, lam, goa, gob)


def _in_bwd(dz, w_in_g, x, dh1, pre_g, ex_arrs, ex_scatter, tm=256):
    t_len = x.shape[0]
    n_tile = t_len // tm
    ex = _Exchange(ex_arrs, ex_scatter)

    def body(dz_ref, w_hbm, x_ref, dh1_ref, g_ref, *refs):
        ex_in, (gx_ref, dg_ref), ex_out = refs[:ex.n], refs[ex.n:ex.n + 2], refs[ex.n + 2:2 * ex.n + 2]
        w_s, t_s, dg_s, w_sems = refs[2 * ex.n + 2:2 * ex.n + 6]
        ex_sems = refs[2 * ex.n + 6:]
        i = pl.program_id(0)

        @pl.when(i == 0)
        def _():
            ex.start(ex_in, ex_out, ex_sems)
            loads = [pltpu.make_async_copy(w_hbm.at[s], w_s.at[:, s * W_IN_SHARD:(s + 1) * W_IN_SHARD], w_sems.at[s])
                     for s in range(N_DEV)]
            for cp in loads:
                cp.start()
            dg_s[...] = jnp.zeros_like(dg_s)
            for cp in loads:
                cp.wait()

        t_s[...] = _dot_nt(dz_ref[...], w_s[...])
        g = g_ref[...]

        def rows_body(q, acc):
            rows = _tile_rows(q)
            xv = x_ref[rows, :]
            r = lax.rsqrt(_mean_last(xv * xv) + EPS)
            xh = xv * r
            dhn = t_s[rows, :]
            dg = dhn * g
            gx_ref[rows, :] = dh1_ref[rows, :] + r * (dg - xh * _mean_last(dg * xh))
            return acc + _fold_rows(dhn * xh)

        dg_s[...] = _loop(tm // TILE_ROWS, rows_body, dg_s[...])

        @pl.when(i == n_tile - 1)
        def _():
            dg_ref[...] = jnp.sum(dg_s[...], axis=0, keepdims=True)
            ex.wait(ex_in, ex_out, ex_sems)

    tile = pl.BlockSpec((tm, D_MODEL), lambda i: (i, 0))
    vec = pl.BlockSpec((1, D_MODEL), lambda i: (0, 0))
    res = pl.pallas_call(
        body, name="in_bwd", grid=(n_tile,),
        in_specs=[pl.BlockSpec((tm, D_IN), lambda i: (i, 0)), ANY_SPEC, tile, tile, vec] + [ANY_SPEC] * ex.n,
        out_specs=[tile, vec] + [ANY_SPEC] * ex.n,
        out_shape=[SDS((t_len, D_MODEL), F32), SDS((1, D_MODEL), F32)] + ex.out_shape,
        scratch_shapes=[pltpu.VMEM((D_MODEL, D_IN), BF16), pltpu.VMEM((tm, D_MODEL), F32), pltpu.VMEM((ROWS, D_MODEL), F32),
                        pltpu.SemaphoreType.DMA((N_DEV,))] + ex.scratch,
        compiler_params=_params(("arbitrary",), 54),
    )(dz, w_in_g, x, dh1, pre_g, *ex_arrs)
    return res[0], res[1], res[2:]


def _grad_w(a, b, bn, shard_major, name, tk=1024, ex_arrs=(), ex_scatter=()):
    t_len, m = a.shape
    n = b.shape[1]
    n_j, n_k = n // bn, t_len // tk
    ex = _Exchange(ex_arrs, ex_scatter)

    def body(a_ref, b_ref, *refs):
        ex_in, o_ref, ex_out = refs[:ex.n], refs[ex.n], refs[ex.n + 1:2 * ex.n + 1]
        acc_s, ex_sems = refs[2 * ex.n + 1], refs[2 * ex.n + 2:]
        j, k = pl.program_id(0), pl.program_id(1)
        if ex.n:
            @pl.when(jnp.logical_and(j == 0, k == 0))
            def _():
                ex.start(ex_in, ex_out, ex_sems)

        part = _dot_tn(a_ref[...], b_ref[...])

        @pl.when(k == 0)
        def _():
            acc_s[...] = part

        @pl.when(k > 0)
        def _():
            acc_s[...] += part

        @pl.when(k == n_k - 1)
        def _():
            o_ref[...] = acc_s[...].astype(BF16)

        if ex.n:
            @pl.when(jnp.logical_and(j == n_j - 1, k == n_k - 1))
            def _():
                ex.wait(ex_in, ex_out, ex_sems)

    if shard_major:
        out_spec, out_shape = pl.BlockSpec((None, m, bn), lambda j, k: (j, 0, 0)), SDS((n_j, m, bn), BF16)
    else:
        out_spec, out_shape = pl.BlockSpec((m, bn), lambda j, k: (0, j)), SDS((m, n), BF16)
    res = pl.pallas_call(
        body, name=name, grid=(n_j, n_k),
        in_specs=[pl.BlockSpec((tk, m), lambda j, k: (k, 0)), pl.BlockSpec((tk, bn), lambda j, k: (k, j))]
        + [ANY_SPEC] * ex.n,
        out_specs=[out_spec] + [ANY_SPEC] * ex.n, out_shape=[out_shape] + ex.out_shape,
        scratch_shapes=[pltpu.VMEM((m, bn), F32)] + (ex.scratch if ex.n else []),
        compiler_params=_params(("arbitrary", "arbitrary"), 40),
    )(a, b, *ex_arrs)
    return res[0], res[1:]


def _adamw(parts, w, m, v, name, tr):
    rows, cols = w.shape
    c1 = 1.0 - ADAM_B1 ** ADAM_STEP
    c2 = 1.0 - ADAM_B2 ** ADAM_STEP

    def body(p_ref, w_ref, m_ref, v_ref, g_ref, d_ref, nm_ref, nv_ref):
        g = p_ref[0].astype(F32)
        for s in range(1, N_DEV):
            g = g + p_ref[s].astype(F32)
        g_ref[...] = g
        nm = ADAM_B1 * m_ref[...] + (1.0 - ADAM_B1) * g
        nv = ADAM_B2 * v_ref[...] + (1.0 - ADAM_B2) * (g * g)
        nm_ref[...] = nm
        nv_ref[...] = nv
        d_ref[...] = -ADAM_LR * ((nm / c1) / (jnp.sqrt(nv / c2) + ADAM_EPS) + ADAM_WD * w_ref[...])

    tile = pl.BlockSpec((tr, cols), lambda i: (i, 0))
    return pl.pallas_call(
        body, name=name, grid=(rows // tr,),
        in_specs=[pl.BlockSpec((N_DEV, tr, cols), lambda i: (0, i, 0)), tile, tile, tile],
        out_specs=[tile] * 4, out_shape=[SDS((rows, cols), F32)] * 4,
        compiler_params=_params(("arbitrary",), 40),
    )(parts, w, m, v)


PACKED = ("gmlp_ln_g", "gmlp_ln_b", "gmlp_ws", "gmlp_bs", "conv_b", "w_a", "b_a", "w_x", "b_x", "lam", "gmlp_out_g",
          "lru_out_g", "post_g")
WEIGHTS = ("pre_g", "w_in", "gmlp_ln_g", "gmlp_ln_b", "gmlp_ws", "gmlp_bs", "conv_w", "conv_b", "w_a", "b_a", "w_x",
           "b_x", "lam", "gmlp_out_g", "lru_out_g", "w_out", "post_g", "w_pe", "w_pg")
LANES = 128


PACK_ROWS = 3200
PACK_TILE = 640


def _pack(parts):
    rows = [p.reshape(-1, LANES) for p in parts]
    used = sum(r.shape[0] for r in rows)
    return jnp.concatenate(rows + [jnp.zeros((PACK_ROWS - used, LANES), F32)], axis=0)


def _pad_rows(a, rows):
    return jnp.concatenate([a, jnp.zeros((rows - a.shape[0],) + a.shape[1:], a.dtype)], axis=0)


def kernel(x, p, pre_g, w_in, gmlp_ln_g, gmlp_ln_b, gmlp_ws, gmlp_bs, conv_w, conv_b, w_a, b_a, w_x, b_x, lam, gmlp_out_g, lru_out_g, w_out, post_g, w_pe, w_pg, loss_target, m_pre_g, m_w_in, m_gmlp_ln_g, m_gmlp_ln_b, m_gmlp_ws, m_gmlp_bs, m_conv_w, m_conv_b, m_w_a, m_b_a, m_w_x, m_b_x, m_lam, m_gmlp_out_g, m_lru_out_g, m_w_out, m_post_g, m_w_pe, m_w_pg, v_pre_g, v_w_in, v_gmlp_ln_g, v_gmlp_ln_b, v_gmlp_ws, v_gmlp_bs, v_conv_w, v_conv_b, v_w_a, v_b_a, v_w_x, v_b_x, v_lam, v_gmlp_out_g, v_lru_out_g, v_w_out, v_post_g, v_w_pe, v_w_pg):
    args = dict(locals())
    weights = {n: args[n] for n in WEIGHTS}
    m_in = {n: args["m_" + n] for n in WEIGHTS}
    v_in = {n: args["v_" + n] for n in WEIGHTS}
    sm = {n: weights[n][0] for n in PACKED}
    shard_rows = D_MODEL // N_DEV
    xs, ps, tgt = x[0], p[0, 0], loss_target[0]

    vec = lambda a: a.reshape(1, -1)
    tril = jnp.tril(jnp.ones((CHUNK, CHUNK), dtype=bool))
    wm32 = jnp.where(tril[None], sm["gmlp_ws"], 0.0)
    wm, wm_t = wm32.astype(BF16), jnp.swapaxes(wm32, 1, 2).astype(BF16)
    bias = jnp.repeat(sm["gmlp_bs"].T, HEAD, axis=1)
    wax32 = jnp.concatenate([sm["w_a"], sm["w_x"]], axis=2)
    wax, wax_t = wax32.astype(BF16), jnp.swapaxes(wax32, 1, 2).astype(BF16)
    ln_g, ln_b = vec(sm["gmlp_ln_g"]), vec(sm["gmlp_ln_b"])
    post_g_v = vec(sm["post_g"])

    hn = _pre_norm(xs, pre_g)
    cw_shard = _pad_rows(conv_w.reshape(CONV_W, HEAD), ROWS)
    z, w_in_g, (w_out_g, w_pe_g, w_pg_g, cw_g) = _in_proj(
        hn, w_in[0].astype(BF16), [w_out[0].astype(BF16), w_pe[0].astype(BF16), w_pg[0].astype(BF16), cw_shard])
    w_out_f, w_pg_f = w_out_g.reshape(D_MODEL, D_MODEL), w_pg_g.reshape(D_MODEL, D_MODEL)
    cw_full = jnp.transpose(cw_g[:, :CONV_W, :], (1, 0, 2)).reshape(CONV_W, D_BR)
    mixer_consts = dict(cw=_pad_rows(cw_full, ROWS), cb=vec(sm["conv_b"]), ba=vec(sm["b_a"]), bx=vec(sm["b_x"]),
                        lam=vec(sm["lam"]), goa=vec(sm["gmlp_out_g"]), gob=vec(sm["lru_out_g"]))
    y, h = _mix_fwd(z, ln_g, ln_b, wm, bias, wax=wax, **mixer_consts)
    h1, ob = _out_proj(y, xs, w_out_f, post_g_v)
    dh2, dpe, dgl, h1b, loss_part = _ple_loss(h1, ps, tgt, w_pg_f, w_pe_g)

    dh1, do, dy, d_post_g = _tail_bwd(dh2, dgl, ob, w_pg_f, w_out_f, post_g_v)
    d_w_out, _ = _grad_w(y, do, 512, False, "grad_w_out")
    d_w_pg, _ = _grad_w(h1b, dgl, 512, False, "grad_w_pg")
    d_w_pe, _ = _grad_w(ps.astype(BF16), dpe, shard_rows, True, "grad_w_pe")
    dz, vecs, d_ws, d_wax, d_bs = _mix_bwd(z, dy, h, ln_g, ln_b, wm, wm_t, bias, wax=wax, wax_t=wax_t, **mixer_consts)
    d_w_in, (parts_out, parts_pg, parts_pe) = _grad_w(
        hn, dz, W_IN_SHARD, True, "grad_w_in",
        ex_arrs=[d_w_out.reshape(N_DEV, shard_rows, D_MODEL), d_w_pg.reshape(N_DEV, shard_rows, D_MODEL), d_w_pe],
        ex_scatter=[True, True, True])

    small = {"gmlp_ln_g": vecs[V_LN_G], "gmlp_ln_b": vecs[V_LN_B], "gmlp_ws": d_ws, "gmlp_bs": d_bs,
             "conv_b": vecs[V_CONV_B], "w_a": d_wax[:, :, :HEAD], "b_a": vecs[V_B_A], "w_x": d_wax[:, :, HEAD:],
             "b_x": vecs[V_B_X], "lam": vecs[V_LAM], "gmlp_out_g": vecs[V_GOUT_A], "lru_out_g": vecs[V_GOUT_B],
             "post_g": d_post_g}
    d_cw_blocks = jnp.transpose(vecs[V_CONV_W:V_CONV_W + CONV_W].reshape(CONV_W, N_DEV, HEAD), (1, 0, 2))
    d_cw_blocks = jnp.concatenate([d_cw_blocks, jnp.zeros((N_DEV, ROWS - CONV_W, HEAD), F32)], axis=1)
    grad_x, d_pre_g, (parts_in, parts_cw, parts_small) = _in_bwd(
        dz, w_in_g, xs, dh1, pre_g,
        ex_arrs=[d_w_in, d_cw_blocks, _pack([small[n] for n in PACKED] + [loss_part])], ex_scatter=[True, True, False])
    pre_rows = D_MODEL // LANES
    parts_pre = _exchange([d_pre_g.reshape(pre_rows, LANES)], False, "gather_pre_g")[0]

    pad_cw = lambda a: _pad_rows(a.reshape(CONV_W, HEAD), ROWS)
    flat = lambda a: a.reshape(pre_rows, LANES)
    outs = {
        "w_in": _adamw(parts_in, w_in[0], m_w_in[0], v_w_in[0], "adamw_w_in", 256),
        "w_out": _adamw(parts_out, w_out[0], m_w_out[0], v_w_out[0], "adamw_w_out", 128),
        "w_pe": _adamw(parts_pe, w_pe[0], m_w_pe[0], v_w_pe[0], "adamw_w_pe", 256),
        "w_pg": _adamw(parts_pg, w_pg[0], m_w_pg[0], v_w_pg[0], "adamw_w_pg", 128),
        "conv_w": [a[:CONV_W] for a in
                   _adamw(parts_cw, pad_cw(conv_w), pad_cw(m_conv_w), pad_cw(v_conv_w), "adamw_conv_w", ROWS)],
        "pre_g": _adamw(parts_pre, flat(pre_g), flat(m_pre_g), flat(v_pre_g), "adamw_pre_g", pre_rows),
    }
    packed = _adamw(parts_small, _pack([weights[n] for n in PACKED]), _pack([m_in[n] for n in PACKED]),
                    _pack([v_in[n] for n in PACKED]), "adamw_small", PACK_TILE)
    row = 0
    for n in PACKED:
        n_rows = weights[n].size // LANES
        outs[n] = [packed[q][row:row + n_rows] for q in range(4)]
        row += n_rows
    loss = packed[0][row, 0]

    result = [loss, grad_x[None]]
    for q in range(4):
        result += [outs[n][q].reshape(weights[n].shape) for n in WEIGHTS]
    return tuple(result)
```

```python
import functools

import jax
import jax.numpy as jnp
from jax import lax
from jax.experimental import pallas as pl
from jax.experimental.pallas import tpu as pltpu

F32 = jnp.float32
BF16 = jnp.bfloat16
SDS = jax.ShapeDtypeStruct

D_MODEL = 2048
D_BR = 1024
D_IN = 5 * D_BR
D_PLE = 256
N_HEAD = 8
HEAD = 128
CHUNK = 128
ROWS = 8
N_GROUP = CHUNK // ROWS
N_DEV = 8
W_IN_SHARD = D_IN // N_DEV
EPS = 1e-6
LRU_C = 8.0
CONV_W = 4
MESH_AXES = ("x", "y", "c")
MIB = 1 << 20

ADAM_LR, ADAM_B1, ADAM_B2, ADAM_EPS, ADAM_WD, ADAM_STEP = 0.001, 0.9, 0.999, 1e-08, 0.01, 10

_GELU_C = 0.7978845608028654
_GELU_A = 0.044715

V_LN_G, V_LN_B, V_CONV_B, V_B_A, V_B_X, V_LAM, V_GOUT_A, V_GOUT_B, V_CONV_W = 0, 1, 2, 3, 4, 5, 6, 7, 8
N_VEC = 16


def _params(sem, vmem_mib):
    return pltpu.CompilerParams(dimension_semantics=sem, vmem_limit_bytes=int(vmem_mib * MIB))


def _sig(x):
    return 0.5 * jnp.tanh(0.5 * x) + 0.5


def _gelu(x):
    t = jnp.tanh(_GELU_C * (x + _GELU_A * x * x * x))
    return 0.5 * x * (1.0 + t), t


def _gelu_grad(x, t):
    return 0.5 * (1.0 + t) + 0.5 * x * (1.0 - t * t) * (_GELU_C * (1.0 + 3.0 * _GELU_A * x * x))


def _neg_expm1(y, exp_y):
    series = -y * (1.0 + y * (0.5 + y * (1.0 / 6.0)))
    return jnp.where(y > -0.01, series, 1.0 - exp_y)


def _softplus(x):
    return jnp.maximum(x, 0.0) + jnp.log(1.0 + jnp.exp(-jnp.abs(x)))


def _row_ids(width):
    return lax.broadcasted_iota(jnp.int32, (ROWS, width), 0)


def _shift_down(cur, prev, k, rid):
    return jnp.where(rid >= k, pltpu.roll(cur, k, 0), pltpu.roll(prev, k, 0))


def _shift_up(cur, nxt, k, rid):
    return jnp.where(rid < ROWS - k, pltpu.roll(cur, ROWS - k, 0), pltpu.roll(nxt, ROWS - k, 0))


def _mean_last(x):
    return jnp.mean(x, axis=-1, keepdims=True)


def _rows(g):
    return pl.ds(pl.multiple_of(g * ROWS, ROWS), ROWS)


TILE_ROWS = 16


def _tile_rows(q):
    return pl.ds(pl.multiple_of(q * TILE_ROWS, TILE_ROWS), TILE_ROWS)


UNROLL = 4


def _loop(n, body, init, unroll=UNROLL):
    def wide(i, carry):
        for u in range(unroll):
            carry = body(i * unroll + u, carry)
        return carry

    return lax.fori_loop(0, n // unroll, wide, init)


def _fold_rows(x):
    return x[0:ROWS, :] + x[ROWS:TILE_ROWS, :]


def _bcast_row(x, r):
    return jnp.broadcast_to(x[r:r + 1, :], x.shape)


def _dot(a, b):
    return jnp.dot(a, b, preferred_element_type=F32)


def _dot_nt(a, b):
    return lax.dot_general(a, b, (((1,), (1,)), ((), ())), preferred_element_type=F32)


def _dot_tn(a, b):
    return lax.dot_general(a, b, (((0,), (0,)), ((), ())), preferred_element_type=F32)


def _mesh_place():
    x, y, c = lax.axis_index("x"), lax.axis_index("y"), lax.axis_index("c")
    return x, y, c, 4 * x + 2 * y + c


def _peer(x, y, c, k):
    px = 1 - x if k & 4 else x
    py = 1 - y if k & 2 else y
    pc = 1 - c if k & 1 else c
    return (px, py, pc), 4 * px + 2 * py + pc


def _remote(src, dst, send_sem, recv_sem, dev):
    return pltpu.make_async_remote_copy(src_ref=src, dst_ref=dst, send_sem=send_sem, recv_sem=recv_sem, device_id=dev,
                                        device_id_type=pl.DeviceIdType.MESH)


ANY_SPEC = pl.BlockSpec(memory_space=pl.ANY)


class _Exchange:
    def __init__(self, arrs, scatter):
        self.n = len(arrs)
        self.scatter = tuple(scatter)
        self.out_shape = [SDS(a.shape if s else (N_DEV,) + a.shape, a.dtype) for a, s in zip(arrs, scatter)]
        self.scratch = [pltpu.SemaphoreType.DMA((self.n * N_DEV,)), pltpu.SemaphoreType.DMA((self.n * N_DEV,)),
                        pltpu.SemaphoreType.DMA((self.n,))]

    def _copies(self, ins, outs, sems):
        send_sems, recv_sems, local_sems = sems
        x, y, c, me = _mesh_place()
        local, sends, recvs = [], [], []
        for a in range(self.n):
            src = ins[a].at[me] if self.scatter[a] else ins[a]
            local.append(pltpu.make_async_copy(src, outs[a].at[me], local_sems.at[a]))
        for k in range(1, N_DEV):
            dev, lin = _peer(x, y, c, k)
            for a in range(self.n):
                src = ins[a].at[lin] if self.scatter[a] else ins[a]
                pair = (send_sems.at[a * N_DEV + k], recv_sems.at[a * N_DEV + k], dev)
                sends.append(_remote(src, outs[a].at[me], *pair))
                recvs.append(_remote(src, outs[a].at[lin], *pair))
        return local, sends, recvs

    def start(self, ins, outs, sems):
        local, sends, _ = self._copies(ins, outs, sems)
        for cp in local + sends:
            cp.start()

    def wait(self, ins, outs, sems):
        local, sends, recvs = self._copies(ins, outs, sems)
        for cp in recvs:
            cp.wait_recv()
        for cp in sends:
            cp.wait_send()
        for cp in local:
            cp.wait()


def _exchange(arrs, scatter, name):
    ex = _Exchange(arrs, [scatter] * len(arrs))
    n = ex.n

    def body(*refs):
        ins, outs, sems = refs[:n], refs[n:2 * n], refs[2 * n:]
        ex.start(ins, outs, sems)
        ex.wait(ins, outs, sems)

    return pl.pallas_call(
        body, name=name, out_shape=ex.out_shape, in_specs=[ANY_SPEC] * n, out_specs=[ANY_SPEC] * n,
        scratch_shapes=ex.scratch,
    )(*arrs)


def _pre_norm(x, pre_g, tm=512):
    t_len = x.shape[0]

    def body(x_ref, g_ref, hn_ref):
        g = g_ref[...]

        def rows_body(q, _):
            rows = _tile_rows(q)
            xv = x_ref[rows, :]
            hn_ref[rows, :] = (xv * lax.rsqrt(_mean_last(xv * xv) + EPS) * g).astype(BF16)
            return 0

        _loop(tm // TILE_ROWS, rows_body, 0)

    tile = pl.BlockSpec((tm, D_MODEL), lambda i: (i, 0))
    return pl.pallas_call(
        body, name="pre_norm", grid=(t_len // tm,),
        in_specs=[tile, pl.BlockSpec((1, D_MODEL), lambda i: (0, 0))], out_specs=tile,
        out_shape=SDS((t_len, D_MODEL), BF16),
        compiler_params=_params(("arbitrary",), 24),
    )(x, pre_g)


AG_ORDER = (0, 1, 2, 4, 6, 3, 5, 7)
SIBLING = 1
ICI_MASKS = (2, 4, 6)
DIRECT_MASKS = (SIBLING,) + ICI_MASKS


def _in_proj(hn, w_shard, others, tm=512):
    t_len = hn.shape[0]
    n_i = t_len // tm
    n_o = len(others)
    me_out = 4 * lax.axis_index("x") + 2 * lax.axis_index("y") + lax.axis_index("c")
    order = jnp.stack([me_out ^ k for k in AG_ORDER]).astype(jnp.int32)

    def body(order_ref, hn_ref, w_hbm, *refs):
        o_in = refs[:n_o]
        z_ref, wg_hbm = refs[n_o], refs[n_o + 1]
        o_out = refs[n_o + 2:2 * n_o + 2]
        wbuf, send_w, recv_w, fsend_w, frecv_w, send_o, recv_o, fsend_o, frecv_o, wb_sems, loc_sems = refs[2 * n_o + 2:]
        j, i = pl.program_id(0), pl.program_id(1)
        x, y, c, me = _mesh_place()
        sib = _peer(x, y, c, SIBLING)[0]

        def direct(k, a=None):
            dev, lin = _peer(x, y, c, k)
            if a is None:
                return (_remote(w_hbm, wbuf.at[me], send_w.at[k], recv_w.at[k], dev),
                        _remote(w_hbm, wbuf.at[lin], send_w.at[k], recv_w.at[k], dev))
            pair = (send_o.at[a * N_DEV + k], recv_o.at[a * N_DEV + k], dev)
            return _remote(o_in[a], o_out[a].at[me], *pair), _remote(o_in[a], o_out[a].at[lin], *pair)

        def passed(k, a=None):
            mine, theirs = _peer(x, y, c, k)[1], _peer(x, y, c, k ^ SIBLING)[1]
            if a is None:
                pair = (fsend_w.at[k], frecv_w.at[k], sib)
                return _remote(wbuf.at[mine], wbuf.at[mine], *pair), _remote(wbuf.at[theirs], wbuf.at[theirs], *pair)
            pair = (fsend_o.at[a * N_DEV + k], frecv_o.at[a * N_DEV + k], sib)
            return (_remote(o_out[a].at[mine], o_out[a].at[mine], *pair),
                    _remote(o_out[a].at[theirs], o_out[a].at[theirs], *pair))

        def own_copies():
            return [pltpu.make_async_copy(o_in[a], o_out[a].at[me], loc_sems.at[1 + a]) for a in range(n_o)]

        @pl.when(jnp.logical_and(j == 0, i == 0))
        def _():
            own = pltpu.make_async_copy(w_hbm, wbuf.at[me], loc_sems.at[0])
            own.start()
            for cp in own_copies():
                cp.start()
            for k in DIRECT_MASKS:
                direct(k)[0].start()
            for k in DIRECT_MASKS:
                for a in range(n_o):
                    direct(k, a)[0].start()
            own.wait()

        for jj in range(1, N_DEV):
            mask = AG_ORDER[jj]

            @pl.when(jnp.logical_and(j == jj, i == 0))
            def _(jj=jj, mask=mask):
                if mask in DIRECT_MASKS:
                    direct(mask)[1].wait_recv()
                    if mask in ICI_MASKS:
                        passed(mask)[0].start()
                else:
                    passed(mask ^ SIBLING)[1].wait_recv()
                late = jj - (N_DEV - len(ICI_MASKS))
                if late >= 0:
                    for a in range(n_o):
                        direct(ICI_MASKS[late], a)[1].wait_recv()
                        passed(ICI_MASKS[late], a)[0].start()

        slot = order_ref[j]

        @pl.when(i == 0)
        def _():
            pltpu.make_async_copy(wbuf.at[slot], wg_hbm.at[slot], wb_sems.at[j]).start()

        z_ref[...] = _dot(hn_ref[...], wbuf[slot])

        @pl.when(jnp.logical_and(j == N_DEV - 1, i == n_i - 1))
        def _():
            for a in range(n_o):
                direct(SIBLING, a)[1].wait_recv()
            for k in ICI_MASKS:
                for a in range(n_o):
                    passed(k, a)[1].wait_recv()
            for k in DIRECT_MASKS:
                direct(k)[0].wait_send()
                for a in range(n_o):
                    direct(k, a)[0].wait_send()
            for k in ICI_MASKS:
                passed(k)[0].wait_send()
                for a in range(n_o):
                    passed(k, a)[0].wait_send()
            for cp in own_copies():
                cp.wait()
            for jj in range(N_DEV):
                pltpu.make_async_copy(wbuf.at[0], wg_hbm.at[0], wb_sems.at[jj]).wait()

    dma = lambda n: pltpu.SemaphoreType.DMA((n,))
    grid_spec = pltpu.PrefetchScalarGridSpec(
        num_scalar_prefetch=1, grid=(N_DEV, n_i),
        in_specs=[pl.BlockSpec((tm, D_MODEL), lambda j, i, order: (i, 0)), ANY_SPEC] + [ANY_SPEC] * n_o,
        out_specs=[pl.BlockSpec((tm, W_IN_SHARD), lambda j, i, order: (i, order[j])), ANY_SPEC] + [ANY_SPEC] * n_o,
        scratch_shapes=[pltpu.VMEM((N_DEV, D_MODEL, W_IN_SHARD), BF16), dma(N_DEV), dma(N_DEV), dma(N_DEV), dma(N_DEV),
                        dma(n_o * N_DEV), dma(n_o * N_DEV), dma(n_o * N_DEV), dma(n_o * N_DEV), dma(N_DEV), dma(1 + n_o)])
    res = pl.pallas_call(
        body, name="in_proj", grid_spec=grid_spec,
        out_shape=[SDS((t_len, D_IN), F32), SDS((N_DEV, D_MODEL, W_IN_SHARD), BF16)]
        + [SDS((N_DEV,) + o.shape, o.dtype) for o in others],
        compiler_params=_params(("arbitrary", "arbitrary"), 44),
    )(order, hn, w_shard, *others)
    return res[0], res[1], res[2:]


def _conv_rows(cur, prev, cw_ref, cb, rid):
    acc = cw_ref[3:4, :] * cur + cb
    for k in range(1, CONV_W):
        acc = acc + cw_ref[3 - k:4 - k, :] * _shift_down(cur, prev, k, rid)
    return acc


def _lru_gates(pa, px, ba, bx, sp8, first_row):
    r = _sig(pa + ba)
    i = _sig(px + bx)
    la = -(r * sp8)
    a = jnp.exp(la)
    mult = jnp.where(first_row, 1.0, jnp.sqrt(_neg_expm1(2.0 * la, a * a)))
    return r, i, a, mult


def _mix_fwd(z, ln_g, ln_b, wm, bias, cw, cb, wax, ba, bx, lam, goa, gob):
    t_len = z.shape[0]
    n_chunk = t_len // CHUNK

    def body(z_ref, lng_ref, lnb_ref, wm_ref, bias_ref, cw_ref, cb_ref, wax_ref, ba_ref, bx_ref, lam_ref, goa_ref,
             gob_ref, y_ref, h_ref, vn_s, xc_s, mixed_s, pre_s, y_s, carry_s, halo_s):
        c_id = pl.program_id(0)
        rid = _row_ids(D_BR)

        @pl.when(c_id == 0)
        def _():
            carry_s[...] = jnp.zeros_like(carry_s)
            halo_s[...] = jnp.zeros_like(halo_s)

        lng, lnb, cb = lng_ref[...], lnb_ref[...], cb_ref[...]

        def phase1(g, prev):
            rows = _rows(g)
            vg, _ = _gelu(z_ref[rows, D_BR:2 * D_BR])
            xm = vg - _mean_last(vg)
            rs = lax.rsqrt(_mean_last(xm * xm) + EPS)
            vn_s[rows, :] = xm * rs * lng + lnb
            xb = z_ref[rows, 3 * D_BR:4 * D_BR]
            xc_s[rows, :] = _conv_rows(xb, prev, cw_ref, cb, rid)
            return xb

        halo_s[...] = _loop(N_GROUP, phase1, halo_s[...])

        for h in range(N_HEAD):
            cs = slice(h * HEAD, (h + 1) * HEAD)
            mixed_s[:, cs] = _dot(wm_ref[h], vn_s[:, cs].astype(BF16))
            pre = _dot(xc_s[:, cs].astype(BF16), wax_ref[h])
            pre_s[:, cs] = pre[:, :HEAD]
            pre_s[:, D_BR + h * HEAD:D_BR + (h + 1) * HEAD] = pre[:, HEAD:]

        ba, bx, goa, gob = ba_ref[...], bx_ref[...], goa_ref[...], gob_ref[...]
        sp8 = LRU_C * _softplus(-lam_ref[...])

        def phase3(g, carry):
            rows = _rows(g)
            ug, _ = _gelu(z_ref[rows, 0:D_BR])
            ga = z_ref[rows, 2 * D_BR:3 * D_BR]
            ya = ug * (mixed_s[rows, :] + bias_ref[rows, :]) * (ga * _sig(ga))
            y_s[rows, 0:D_BR] = ya * lax.rsqrt(_mean_last(ya * ya) + EPS) * goa

            first_row = jnp.logical_and(jnp.logical_and(c_id == 0, g == 0), rid == 0)
            _, i, a, mult = _lru_gates(pre_s[rows, 0:D_BR], pre_s[rows, D_BR:2 * D_BR], ba, bx, sp8, first_row)
            b = mult * i * xc_s[rows, :]
            for d in (1, 2, 4):
                a_sh = jnp.where(rid >= d, pltpu.roll(a, d, 0), 1.0)
                b_sh = jnp.where(rid >= d, pltpu.roll(b, d, 0), 0.0)
                b = a * b_sh + b
                a = a * a_sh
            hh = b + a * carry
            h_ref[rows, :] = hh
            gb = z_ref[rows, 4 * D_BR:5 * D_BR]
            yb = hh * (gb * _sig(gb))
            y_s[rows, D_BR:2 * D_BR] = yb * lax.rsqrt(_mean_last(yb * yb) + EPS) * gob
            return _bcast_row(hh, ROWS - 1)

        carry_s[...] = _loop(N_GROUP, phase3, carry_s[...])
        y_ref[...] = y_s[...].astype(BF16)

    vec = pl.BlockSpec((1, D_BR), lambda i: (0, 0))
    return pl.pallas_call(
        body, name="mix_fwd", grid=(n_chunk,),
        in_specs=[pl.BlockSpec((CHUNK, D_IN), lambda i: (i, 0)), vec, vec,
                  pl.BlockSpec((N_HEAD, HEAD, HEAD), lambda i: (0, 0, 0)),
                  pl.BlockSpec((CHUNK, D_BR), lambda i: (0, 0)),
                  pl.BlockSpec((ROWS, D_BR), lambda i: (0, 0)), vec,
                  pl.BlockSpec((N_HEAD, HEAD, 2 * HEAD), lambda i: (0, 0, 0)), vec, vec, vec, vec, vec],
        out_specs=[pl.BlockSpec((CHUNK, 2 * D_BR), lambda i: (i, 0)), pl.BlockSpec((CHUNK, D_BR), lambda i: (i, 0))],
        out_shape=[SDS((t_len, 2 * D_BR), BF16), SDS((t_len, D_BR), F32)],
        scratch_shapes=[pltpu.VMEM((CHUNK, D_BR), F32), pltpu.VMEM((CHUNK, D_BR), F32), pltpu.VMEM((CHUNK, D_BR), F32),
                        pltpu.VMEM((CHUNK, 2 * D_BR), F32), pltpu.VMEM((CHUNK, 2 * D_BR), F32),
                        pltpu.VMEM((ROWS, D_BR), F32), pltpu.VMEM((ROWS, D_BR), F32)],
        compiler_params=_params(("arbitrary",), 32),
    )(z, ln_g, ln_b, wm, bias, cw, cb, wax, ba, bx, lam, goa, gob)


def _load_weight(w_hbm, w_vmem, sem):
    @pl.when(pl.program_id(0) == 0)
    def _():
        cp = pltpu.make_async_copy(w_hbm, w_vmem, sem)
        cp.start()
        cp.wait()


def _out_proj(y, x, w_out, post_g, tm=512):
    t_len = y.shape[0]

    def body(y_ref, x_ref, w_hbm, g_ref, h1_ref, ob_ref, w_s, o_s, sem):
        _load_weight(w_hbm, w_s, sem)
        o_s[...] = _dot(y_ref[...], w_s[...])
        g = g_ref[...]

        def rows_body(q, _):
            rows = _tile_rows(q)
            o = o_s[rows, :]
            h1_ref[rows, :] = x_ref[rows, :] + o * lax.rsqrt(_mean_last(o * o) + EPS) * g
            ob_ref[rows, :] = o.astype(BF16)
            return 0

        _loop(tm // TILE_ROWS, rows_body, 0)

    tile = pl.BlockSpec((tm, D_MODEL), lambda i: (i, 0))
    return pl.pallas_call(
        body, name="out_proj", grid=(t_len // tm,),
        in_specs=[tile, tile, pl.BlockSpec(memory_space=pl.ANY), pl.BlockSpec((1, D_MODEL), lambda i: (0, 0))],
        out_specs=[tile, tile],
        out_shape=[SDS((t_len, D_MODEL), F32), SDS((t_len, D_MODEL), BF16)],
        scratch_shapes=[pltpu.VMEM((D_MODEL, D_MODEL), BF16), pltpu.VMEM((tm, D_MODEL), F32), pltpu.SemaphoreType.DMA],
        compiler_params=_params(("arbitrary",), 44),
    )(y, x, w_out, post_g)


def _ple_loss(h1, p, tgt, w_pg, w_pe_g, tm=256):
    t_len = h1.shape[0]
    n_tile = t_len // tm
    pe_shard = D_MODEL // N_DEV

    def body(h1_ref, p_ref, t_ref, w_hbm, wpe_ref, dh2_ref, dpe_ref, dgl_ref, h1b_ref, loss_ref, w_s, pe_s, gl_s, acc_s,
             sem):
        _load_weight(w_hbm, w_s, sem)
        i = pl.program_id(0)

        @pl.when(i == 0)
        def _():
            acc_s[...] = jnp.zeros_like(acc_s)

        h1b_ref[...] = h1_ref[...].astype(BF16)
        pb = p_ref[...].astype(BF16)
        for j in range(N_DEV):
            pe_s[:, j * pe_shard:(j + 1) * pe_shard] = _dot(pb, wpe_ref[j])
        gl_s[...] = _dot(h1b_ref[...], w_s[...])

        def rows_body(q, acc):
            rows = _tile_rows(q)
            pe = pe_s[rows, :]
            g = _sig(gl_s[rows, :])
            e = h1_ref[rows, :] + pe * g - t_ref[rows, :]
            dh2 = e * (1.0 / D_MODEL)
            dh2_ref[rows, :] = dh2
            dpe_ref[rows, :] = (dh2 * g).astype(BF16)
            dgl_ref[rows, :] = (dh2 * pe * g * (1.0 - g)).astype(BF16)
            return acc + _fold_rows(e * e)

        acc_s[...] = _loop(tm // TILE_ROWS, rows_body, acc_s[...])

        @pl.when(i == n_tile - 1)
        def _():
            loss_ref[...] = jnp.full(loss_ref.shape, 0.5 / D_MODEL * jnp.sum(acc_s[...]), F32)

    tile = pl.BlockSpec((tm, D_MODEL), lambda i: (i, 0))
    return pl.pallas_call(
        body, name="ple_loss", grid=(n_tile,),
        in_specs=[tile, pl.BlockSpec((tm, D_PLE), lambda i: (i, 0)), tile, pl.BlockSpec(memory_space=pl.ANY),
                  pl.BlockSpec((N_DEV, D_PLE, pe_shard), lambda i: (0, 0, 0))],
        out_specs=[tile, tile, tile, tile, pl.BlockSpec((ROWS, HEAD), lambda i: (0, 0))],
        out_shape=[SDS((t_len, D_MODEL), F32), SDS((t_len, D_MODEL), BF16), SDS((t_len, D_MODEL), BF16),
                   SDS((t_len, D_MODEL), BF16), SDS((ROWS, HEAD), F32)],
        scratch_shapes=[pltpu.VMEM((D_MODEL, D_MODEL), BF16), pltpu.VMEM((tm, D_MODEL), F32),
                        pltpu.VMEM((tm, D_MODEL), F32), pltpu.VMEM((ROWS, D_MODEL), F32), pltpu.SemaphoreType.DMA],
        compiler_params=_params(("arbitrary",), 44),
    )(h1, p, tgt, w_pg, w_pe_g)


def _tail_bwd(dh2, dgl, ob, w_pg, w_out, post_g, tm=256):
    t_len = dh2.shape[0]
    n_tile = t_len // tm

    def body(dh2_ref, dgl_ref, ob_ref, wpg_hbm, wout_hbm, g_ref, dh1_ref, do_ref, dy_ref, dg_ref, wpg_s, wout_s, t_s,
             acc_s, sems):
        _load_weight(wpg_hbm, wpg_s, sems.at[0])
        _load_weight(wout_hbm, wout_s, sems.at[1])
        i = pl.program_id(0)

        @pl.when(i == 0)
        def _():
            acc_s[...] = jnp.zeros_like(acc_s)

        t_s[...] = _dot_nt(dgl_ref[...], wpg_s[...])
        g = g_ref[...]

        def rows_body(q, acc):
            rows = _tile_rows(q)
            dh1 = dh2_ref[rows, :] + t_s[rows, :]
            dh1_ref[rows, :] = dh1
            o = ob_ref[rows, :].astype(F32)
            rr = lax.rsqrt(_mean_last(o * o) + EPS)
            on = o * rr
            dog = dh1 * g
            do_ref[rows, :] = (rr * (dog - on * _mean_last(dog * on))).astype(BF16)
            return acc + _fold_rows(dh1 * on)

        acc_s[...] = _loop(tm // TILE_ROWS, rows_body, acc_s[...])
        dy_ref[...] = _dot_nt(do_ref[...], wout_s[...]).astype(BF16)

        @pl.when(i == n_tile - 1)
        def _():
            dg_ref[...] = jnp.sum(acc_s[...], axis=0, keepdims=True)

    tile = pl.BlockSpec((tm, D_MODEL), lambda i: (i, 0))
    vec = pl.BlockSpec((1, D_MODEL), lambda i: (0, 0))
    hbm = pl.BlockSpec(memory_space=pl.ANY)
    return pl.pallas_call(
        body, name="tail_bwd", grid=(n_tile,),
        in_specs=[tile, tile, tile, hbm, hbm, vec],
        out_specs=[tile, tile, tile, vec],
        out_shape=[SDS((t_len, D_MODEL), F32), SDS((t_len, D_MODEL), BF16), SDS((t_len, D_MODEL), BF16),
                   SDS((1, D_MODEL), F32)],
        scratch_shapes=[pltpu.VMEM((D_MODEL, D_MODEL), BF16), pltpu.VMEM((D_MODEL, D_MODEL), BF16),
                        pltpu.VMEM((tm, D_MODEL), F32), pltpu.VMEM((ROWS, D_MODEL), F32), pltpu.SemaphoreType.DMA((2,))],
        compiler_params=_params(("arbitrary",), 48),
    )(dh2, dgl, ob, w_pg, w_out, post_g)


def _mix_bwd(z, dy, h, ln_g, ln_b, wm, wm_t, bias, cw, cb, wax, wax_t, ba, bx, lam, goa, gob, ex_arrs, ex_scatter):
    t_len = z.shape[0]
    n_chunk = t_len // CHUNK
    halo_blocks = CHUNK // ROWS
    ex = _Exchange(ex_arrs, ex_scatter)
    n_in, n_out, n_scratch = 19, 5, 17

    def body(*refs):
        (z_ref, zhalo_ref, dy_ref, h_ref, hhalo_ref, lng_ref, lnb_ref, wm_ref, wmt_ref, bias_ref, cw_ref, cb_ref,
         wax_ref, waxt_ref, ba_ref, bx_ref, lam_ref, goa_ref, gob_ref) = refs[:n_in]
        ex_in = refs[n_in:n_in + ex.n]
        dz_ref, vecs_ref, dws_ref, dwax_ref, dbs_ref = refs[n_in + ex.n:n_in + ex.n + n_out]
        ex_out = refs[n_in + ex.n + n_out:n_in + 2 * ex.n + n_out]
        (vn_s, vh_s, rs_s, xc_s, mixed_s, pre_s, dmix_s, dvn_s, dho_s, dxc_s, dpre_s, dz_s, acc_s, accdm_s,
         cg_s, ca_s, dxchalo_s) = refs[n_in + 2 * ex.n + n_out:n_in + 2 * ex.n + n_out + n_scratch]
        ex_sems = refs[n_in + 2 * ex.n + n_out + n_scratch:]
        step = pl.program_id(0)
        c_id = n_chunk - 1 - step
        rid = _row_ids(D_BR)
        first_chunk = c_id == 0

        @pl.when(step == 0)
        def _():
            ex.start(ex_in, ex_out, ex_sems)
            acc_s[...] = jnp.zeros_like(acc_s)
            accdm_s[...] = jnp.zeros_like(accdm_s)
            cg_s[...] = jnp.zeros_like(cg_s)
            ca_s[...] = jnp.zeros_like(ca_s)
            dxchalo_s[...] = jnp.zeros_like(dxchalo_s)
            dws_ref[...] = jnp.zeros_like(dws_ref)
            dwax_ref[...] = jnp.zeros_like(dwax_ref)

        lng, lnb, cb = lng_ref[...], lnb_ref[...], cb_ref[...]
        xb_halo = jnp.where(first_chunk, 0.0, zhalo_ref[...])
        h_halo = jnp.where(first_chunk, 0.0, hhalo_ref[...])

        def prev_rows(ref, cols, g, halo):
            before = ref[pl.ds(pl.multiple_of(jnp.maximum(g - 1, 0) * ROWS, ROWS), ROWS), cols]
            return jnp.where(g > 0, before, halo)

        def phase1(g, prev):
            rows = _rows(g)
            vg, _ = _gelu(z_ref[rows, D_BR:2 * D_BR])
            xm = vg - _mean_last(vg)
            rs = lax.rsqrt(_mean_last(xm * xm) + EPS)
            vh = xm * rs
            vh_s[rows, :] = vh
            rs_s[rows, :] = jnp.broadcast_to(rs, (ROWS, HEAD))
            vn_s[rows, :] = vh * lng + lnb
            xb = z_ref[rows, 3 * D_BR:4 * D_BR]
            xc_s[rows, :] = _conv_rows(xb, prev, cw_ref, cb, rid)
            return xb

        _loop(N_GROUP, phase1, xb_halo)

        for hd in range(N_HEAD):
            cs = slice(hd * HEAD, (hd + 1) * HEAD)
            mixed_s[:, cs] = _dot(wm_ref[hd], vn_s[:, cs].astype(BF16))
            pre = _dot(xc_s[:, cs].astype(BF16), wax_ref[hd])
            pre_s[:, cs] = pre[:, :HEAD]
            pre_s[:, D_BR + hd * HEAD:D_BR + (hd + 1) * HEAD] = pre[:, HEAD:]

        goa, gob = goa_ref[...], gob_ref[...]

        def phase3(g, _):
            rows = _rows(g)
            u = z_ref[rows, 0:D_BR]
            ug, tu = _gelu(u)
            ga = z_ref[rows, 2 * D_BR:3 * D_BR]
            sga = _sig(ga)
            sa = ga * sga
            mixed = mixed_s[rows, :] + bias_ref[rows, :]
            ya0 = ug * mixed
            ya = ya0 * sa
            ra = lax.rsqrt(_mean_last(ya * ya) + EPS)
            dyan = dy_ref[rows, 0:D_BR].astype(F32)
            acc_s[V_GOUT_A] += dyan * ya * ra
            dyg = dyan * goa
            dya = ra * dyg - ya * (ra * ra * ra) * _mean_last(dyg * ya)
            dya0 = dya * sa
            dz_s[rows, 2 * D_BR:3 * D_BR] = dya * ya0 * (sga * (1.0 + ga * (1.0 - sga)))
            dmix = dya0 * ug
            dmix_s[rows, :] = dmix
            accdm_s[rows, :] += dmix
            dz_s[rows, 0:D_BR] = dya0 * mixed * _gelu_grad(u, tu)

            hh = h_ref[rows, :]
            gb = z_ref[rows, 4 * D_BR:5 * D_BR]
            sgb = _sig(gb)
            sb = gb * sgb
            yb = hh * sb
            rb = lax.rsqrt(_mean_last(yb * yb) + EPS)
            dybn = dy_ref[rows, D_BR:2 * D_BR].astype(F32)
            acc_s[V_GOUT_B] += dybn * yb * rb
            dyg = dybn * gob
            dyb = rb * dyg - yb * (rb * rb * rb) * _mean_last(dyg * yb)
            dho_s[rows, :] = dyb * sb
            dz_s[rows, 4 * D_BR:5 * D_BR] = dyb * hh * (sgb * (1.0 + gb * (1.0 - sgb)))
            return 0

        _loop(N_GROUP, phase3, 0)

        for hd in range(N_HEAD):
            cs = slice(hd * HEAD, (hd + 1) * HEAD)
            dmb = dmix_s[:, cs].astype(BF16)
            dvn_s[:, cs] = _dot(wmt_ref[hd], dmb)
            dws_ref[hd] += _dot_nt(dmb, vn_s[:, cs].astype(BF16))

        def phase5(g, _):
            rows = _rows(g)
            dvn = dvn_s[rows, :]
            vh = vh_s[rows, :]
            acc_s[V_LN_G] += dvn * vh
            acc_s[V_LN_B] += dvn
            dvh = dvn * lng
            rs = rs_s[rows, 0:1]
            dvg = rs * (dvh - _mean_last(dvh) - vh * _mean_last(dvh * vh))
            v = z_ref[rows, D_BR:2 * D_BR]
            _, tv = _gelu(v)
            dz_s[rows, D_BR:2 * D_BR] = dvg * _gelu_grad(v, tv)
            return 0

        _loop(N_GROUP, phase5, 0)

        ba, bx = ba_ref[...], bx_ref[...]
        sp8 = LRU_C * _softplus(-lam_ref[...])

        def phase6(k, carry):
            cg, ca = carry
            g = N_GROUP - 1 - k
            rows = _rows(g)
            first_row = jnp.logical_and(jnp.logical_and(first_chunk, g == 0), rid == 0)
            r, i, a, mult = _lru_gates(pre_s[rows, 0:D_BR], pre_s[rows, D_BR:2 * D_BR], ba, bx, sp8, first_row)
            a_nx = jnp.where(rid < ROWS - 1, pltpu.roll(a, ROWS - 1, 0), ca)
            aa, bb = a_nx, dho_s[rows, :]
            for d in (1, 2, 4):
                a_sh = jnp.where(rid < ROWS - d, pltpu.roll(aa, ROWS - d, 0), 1.0)
                b_sh = jnp.where(rid < ROWS - d, pltpu.roll(bb, ROWS - d, 0), 0.0)
                bb = aa * b_sh + bb
                aa = aa * a_sh
            gg = bb + aa * cg
            hh = h_ref[rows, :]
            hprev = _shift_down(hh, prev_rows(h_ref, slice(None), g, h_halo), 1, rid)
            xc = xc_s[rows, :]
            gx = gg * xc
            dla = gg * hprev * a - jnp.where(first_row, 0.0, gx * i * (a * a) * lax.rsqrt(mult * mult))
            acc_s[V_LAM] += -(dla * r)
            dpa = -(dla * sp8) * r * (1.0 - r)
            dpx = gx * mult * i * (1.0 - i)
            acc_s[V_B_A] += dpa
            acc_s[V_B_X] += dpx
            dpre_s[rows, 0:D_BR] = dpa
            dpre_s[rows, D_BR:2 * D_BR] = dpx
            dxc_s[rows, :] = gg * mult * i
            return _bcast_row(gg, 0), _bcast_row(a, 0)

        cg, ca = _loop(N_GROUP, phase6, (cg_s[...], ca_s[...]))
        cg_s[...] = cg
        ca_s[...] = ca

        for hd in range(N_HEAD):
            cs = slice(hd * HEAD, (hd + 1) * HEAD)
            dpre = jnp.concatenate([dpre_s[:, cs], dpre_s[:, D_BR + hd * HEAD:D_BR + (hd + 1) * HEAD]], axis=1).astype(BF16)
            dxc_s[:, cs] += _dot(dpre, waxt_ref[hd])
            dwax_ref[hd] += _dot_tn(xc_s[:, cs].astype(BF16), dpre)

        def phase8(k, nxt):
            g = N_GROUP - 1 - k
            rows = _rows(g)
            dxc = dxc_s[rows, :]
            acc_s[V_CONV_B] += dxc
            xb = z_ref[rows, 3 * D_BR:4 * D_BR]
            xb_prev = prev_rows(z_ref, slice(3 * D_BR, 4 * D_BR), g, xb_halo)
            dxb = cw_ref[3:4, :] * dxc
            acc_s[V_CONV_W + 3] += dxc * xb
            for j in range(1, CONV_W):
                dxb = dxb + cw_ref[3 - j:4 - j, :] * _shift_up(dxc, nxt, j, rid)
                acc_s[V_CONV_W + 3 - j] += dxc * _shift_down(xb, xb_prev, j, rid)
            dz_s[rows, 3 * D_BR:4 * D_BR] = dxb
            return dxc

        dxchalo_s[...] = _loop(N_GROUP, phase8, dxchalo_s[...])
        dz_ref[...] = dz_s[...].astype(BF16)

        @pl.when(step == n_chunk - 1)
        def _():
            for v in range(N_VEC):
                vecs_ref[v:v + 1, :] = jnp.sum(acc_s[v], axis=0, keepdims=True)
            lam = lam_ref[...]
            vecs_ref[V_LAM:V_LAM + 1, :] = vecs_ref[V_LAM:V_LAM + 1, :] * (-LRU_C * _sig(-lam))
            tril = (lax.broadcasted_iota(jnp.int32, (HEAD, HEAD), 0) >= lax.broadcasted_iota(jnp.int32, (HEAD, HEAD), 1))
            ones = jnp.ones((ROWS, HEAD), BF16)
            for hd in range(N_HEAD):
                cs = slice(hd * HEAD, (hd + 1) * HEAD)
                dws_ref[hd] = jnp.where(tril, dws_ref[hd], 0.0)
                blk = accdm_s[:, cs]
                hi = blk.astype(BF16)
                lo = (blk - hi.astype(F32)).astype(BF16)
                dbs_ref[hd:hd + 1, :] = (_dot_nt(ones, hi) + _dot_nt(ones, lo))[0:1, :]
            ex.wait(ex_in, ex_out, ex_sems)

    vec = pl.BlockSpec((1, D_BR), lambda i: (0, 0))
    rev = lambda i: (n_chunk - 1 - i, 0)
    halo = lambda col: (lambda i: (jnp.maximum((n_chunk - 1 - i) * halo_blocks - 1, 0), col))
    full3 = lambda a, b, c: pl.BlockSpec((a, b, c), lambda i: (0, 0, 0))
    big = lambda w: pltpu.VMEM((CHUNK, w), F32)
    res = pl.pallas_call(
        body, name="mix_bwd", grid=(n_chunk,),
        in_specs=[pl.BlockSpec((CHUNK, D_IN), rev), pl.BlockSpec((ROWS, D_BR), halo(3)),
                  pl.BlockSpec((CHUNK, 2 * D_BR), rev), pl.BlockSpec((CHUNK, D_BR), rev),
                  pl.BlockSpec((ROWS, D_BR), halo(0)), vec, vec,
                  full3(N_HEAD, HEAD, HEAD), full3(N_HEAD, HEAD, HEAD),
                  pl.BlockSpec((CHUNK, D_BR), lambda i: (0, 0)), pl.BlockSpec((ROWS, D_BR), lambda i: (0, 0)), vec,
                  full3(N_HEAD, HEAD, 2 * HEAD), full3(N_HEAD, 2 * HEAD, HEAD), vec, vec, vec, vec, vec]
        + [ANY_SPEC] * ex.n,
        out_specs=[pl.BlockSpec((CHUNK, D_IN), rev), pl.BlockSpec((N_VEC, D_BR), lambda i: (0, 0)),
                   full3(N_HEAD, HEAD, HEAD), full3(N_HEAD, HEAD, 2 * HEAD),
                   pl.BlockSpec((N_HEAD, HEAD), lambda i: (0, 0))] + [ANY_SPEC] * ex.n,
        out_shape=[SDS((t_len, D_IN), BF16), SDS((N_VEC, D_BR), F32), SDS((N_HEAD, HEAD, HEAD), F32),
                   SDS((N_HEAD, HEAD, 2 * HEAD), F32), SDS((N_HEAD, HEAD), F32)] + ex.out_shape,
        scratch_shapes=[big(D_BR), big(D_BR), big(HEAD), big(D_BR), big(D_BR), big(2 * D_BR), big(D_BR), big(D_BR),
                        big(D_BR), big(D_BR), big(2 * D_BR), big(D_IN),
                        pltpu.VMEM((N_VEC, ROWS, D_BR), F32), big(D_BR),
                        pltpu.VMEM((ROWS, D_BR), F32), pltpu.VMEM((ROWS, D_BR), F32), pltpu.VMEM((ROWS, D_BR), F32)]
        + ex.scratch,
        compiler_params=_params(("arbitrary",), 48),
    )(z, z, dy, h, h, ln_g, ln_b, wm, wm_t, bias, cw, cb, wax, wax_t, ba, bx, lam, goa, gob, *ex_arrs)
    return res[:n_out], res[n_out:]


def _in_bwd(dz, w_in_g, x, dh1, pre_g, ex_arrs, ex_scatter, tm=256):
    t_len = x.shape[0]
    n_tile = t_len // tm
    ex = _Exchange(ex_arrs, ex_scatter)

    def body(dz_ref, w_hbm, x_ref, dh1_ref, g_ref, *refs):
        ex_in, (gx_ref, dg_ref), ex_out = refs[:ex.n], refs[ex.n:ex.n + 2], refs[ex.n + 2:2 * ex.n + 2]
        w_s, t_s, dg_s, w_sems = refs[2 * ex.n + 2:2 * ex.n + 6]
        ex_sems = refs[2 * ex.n + 6:]
        i = pl.program_id(0)

        @pl.when(i == 0)
        def _():
            ex.start(ex_in, ex_out, ex_sems)
            loads = [pltpu.make_async_copy(w_hbm.at[s], w_s.at[:, s * W_IN_SHARD:(s + 1) * W_IN_SHARD], w_sems.at[s])
                     for s in range(N_DEV)]
            for cp in loads:
                cp.start()
            dg_s[...] = jnp.zeros_like(dg_s)
            for cp in loads:
                cp.wait()

        t_s[...] = _dot_nt(dz_ref[...], w_s[...])
        g = g_ref[...]

        def rows_body(q, acc):
            rows = _tile_rows(q)
            xv = x_ref[rows, :]
            r = lax.rsqrt(_mean_last(xv * xv) + EPS)
            xh = xv * r
            dhn = t_s[rows, :]
            dg = dhn * g
            gx_ref[rows, :] = dh1_ref[rows, :] + r * (dg - xh * _mean_last(dg * xh))
            return acc + _fold_rows(dhn * xh)

        dg_s[...] = _loop(tm // TILE_ROWS, rows_body, dg_s[...])

        @pl.when(i == n_tile - 1)
        def _():
            dg_ref[...] = jnp.sum(dg_s[...], axis=0, keepdims=True)
            ex.wait(ex_in, ex_out, ex_sems)

    tile = pl.BlockSpec((tm, D_MODEL), lambda i: (i, 0))
    vec = pl.BlockSpec((1, D_MODEL), lambda i: (0, 0))
    res = pl.pallas_call(
        body, name="in_bwd", grid=(n_tile,),
        in_specs=[pl.BlockSpec((tm, D_IN), lambda i: (i, 0)), ANY_SPEC, tile, tile, vec] + [ANY_SPEC] * ex.n,
        out_specs=[tile, vec] + [ANY_SPEC] * ex.n,
        out_shape=[SDS((t_len, D_MODEL), F32), SDS((1, D_MODEL), F32)] + ex.out_shape,
        scratch_shapes=[pltpu.VMEM((D_MODEL, D_IN), BF16), pltpu.VMEM((tm, D_MODEL), F32), pltpu.VMEM((ROWS, D_MODEL), F32),
                        pltpu.SemaphoreType.DMA((N_DEV,))] + ex.scratch,
        compiler_params=_params(("arbitrary",), 54),
    )(dz, w_in_g, x, dh1, pre_g, *ex_arrs)
    return res[0], res[1], res[2:]


def _grad_w(a, b, bn, shard_major, name, tk=1024, ex_arrs=(), ex_scatter=()):
    t_len, m = a.shape
    n = b.shape[1]
    n_j, n_k = n // bn, t_len // tk
    ex = _Exchange(ex_arrs, ex_scatter)

    def body(a_ref, b_ref, *refs):
        ex_in, o_ref, ex_out = refs[:ex.n], refs[ex.n], refs[ex.n + 1:2 * ex.n + 1]
        acc_s, ex_sems = refs[2 * ex.n + 1], refs[2 * ex.n + 2:]
        j, k = pl.program_id(0), pl.program_id(1)
        if ex.n:
            @pl.when(jnp.logical_and(j == 0, k == 0))
            def _():
                ex.start(ex_in, ex_out, ex_sems)

        part = _dot_tn(a_ref[...], b_ref[...])

        @pl.when(k == 0)
        def _():
            acc_s[...] = part

        @pl.when(k > 0)
        def _():
            acc_s[...] += part

        @pl.when(k == n_k - 1)
        def _():
            o_ref[...] = acc_s[...].astype(BF16)

        if ex.n:
            @pl.when(jnp.logical_and(j == n_j - 1, k == n_k - 1))
            def _():
                ex.wait(ex_in, ex_out, ex_sems)

    if shard_major:
        out_spec, out_shape = pl.BlockSpec((None, m, bn), lambda j, k: (j, 0, 0)), SDS((n_j, m, bn), BF16)
    else:
        out_spec, out_shape = pl.BlockSpec((m, bn), lambda j, k: (0, j)), SDS((m, n), BF16)
    res = pl.pallas_call(
        body, name=name, grid=(n_j, n_k),
        in_specs=[pl.BlockSpec((tk, m), lambda j, k: (k, 0)), pl.BlockSpec((tk, bn), lambda j, k: (k, j))]
        + [ANY_SPEC] * ex.n,
        out_specs=[out_spec] + [ANY_SPEC] * ex.n, out_shape=[out_shape] + ex.out_shape,
        scratch_shapes=[pltpu.VMEM((m, bn), F32)] + (ex.scratch if ex.n else []),
        compiler_params=_params(("arbitrary", "arbitrary"), 40),
    )(a, b, *ex_arrs)
    return res[0], res[1:]


def _sum_parts(parts, name):
    def body(p_ref, o_ref):
        g = p_ref[0].astype(F32)
        for s in range(1, parts.shape[0]):
            g = g + p_ref[s].astype(F32)
        o_ref[...] = g

    return pl.pallas_call(body, name=name, out_shape=SDS(parts.shape[1:], F32))(parts)


def _adamw(parts, w, m, v, name, tr):
    rows, cols = w.shape
    n_parts = parts.shape[0]
    c1 = 1.0 - ADAM_B1 ** ADAM_STEP
    c2 = 1.0 - ADAM_B2 ** ADAM_STEP

    def body(p_ref, w_ref, m_ref, v_ref, g_ref, d_ref, nm_ref, nv_ref):
        g = p_ref[0].astype(F32)
        for s in range(1, n_parts):
            g = g + p_ref[s].astype(F32)
        g_ref[...] = g
        nm = ADAM_B1 * m_ref[...] + (1.0 - ADAM_B1) * g
        nv = ADAM_B2 * v_ref[...] + (1.0 - ADAM_B2) * (g * g)
        nm_ref[...] = nm
        nv_ref[...] = nv
        d_ref[...] = -ADAM_LR * ((nm / c1) / (jnp.sqrt(nv / c2) + ADAM_EPS) + ADAM_WD * w_ref[...])

    tile = pl.BlockSpec((tr, cols), lambda i: (i, 0))
    return pl.pallas_call(
        body, name=name, grid=(rows // tr,),
        in_specs=[pl.BlockSpec((n_parts, tr, cols), lambda i: (0, i, 0)), tile, tile, tile],
        out_specs=[tile] * 4, out_shape=[SDS((rows, cols), F32)] * 4,
        compiler_params=_params(("arbitrary",), 40),
    )(parts, w, m, v)


PACKED = ("gmlp_ln_g", "gmlp_ln_b", "gmlp_ws", "gmlp_bs", "conv_b", "w_a", "b_a", "w_x", "b_x", "lam", "gmlp_out_g",
          "lru_out_g", "post_g")
WEIGHTS = ("pre_g", "w_in", "gmlp_ln_g", "gmlp_ln_b", "gmlp_ws", "gmlp_bs", "conv_w", "conv_b", "w_a", "b_a", "w_x",
           "b_x", "lam", "gmlp_out_g", "lru_out_g", "w_out", "post_g", "w_pe", "w_pg")
LANES = 128


PACK_ROWS = 3200
PACK_TILE = 640


def _pack(parts):
    rows = [p.reshape(-1, LANES) for p in parts]
    used = sum(r.shape[0] for r in rows)
    return jnp.concatenate(rows + [jnp.zeros((PACK_ROWS - used, LANES), F32)], axis=0)


def _pad_rows(a, rows):
    return jnp.concatenate([a, jnp.zeros((rows - a.shape[0],) + a.shape[1:], a.dtype)], axis=0)


def kernel(x, p, pre_g, w_in, gmlp_ln_g, gmlp_ln_b, gmlp_ws, gmlp_bs, conv_w, conv_b, w_a, b_a, w_x, b_x, lam, gmlp_out_g, lru_out_g, w_out, post_g, w_pe, w_pg, loss_target, m_pre_g, m_w_in, m_gmlp_ln_g, m_gmlp_ln_b, m_gmlp_ws, m_gmlp_bs, m_conv_w, m_conv_b, m_w_a, m_b_a, m_w_x, m_b_x, m_lam, m_gmlp_out_g, m_lru_out_g, m_w_out, m_post_g, m_w_pe, m_w_pg, v_pre_g, v_w_in, v_gmlp_ln_g, v_gmlp_ln_b, v_gmlp_ws, v_gmlp_bs, v_conv_w, v_conv_b, v_w_a, v_b_a, v_w_x, v_b_x, v_lam, v_gmlp_out_g, v_lru_out_g, v_w_out, v_post_g, v_w_pe, v_w_pg):
    args = dict(locals())
    weights = {n: args[n] for n in WEIGHTS}
    m_in = {n: args["m_" + n] for n in WEIGHTS}
    v_in = {n: args["v_" + n] for n in WEIGHTS}
    sm = {n: weights[n][0] for n in PACKED}
    shard_rows = D_MODEL // N_DEV
    xs, ps, tgt = x[0], p[0, 0], loss_target[0]

    vec = lambda a: a.reshape(1, -1)
    tril = jnp.tril(jnp.ones((CHUNK, CHUNK), dtype=bool))
    wm32 = jnp.where(tril[None], sm["gmlp_ws"], 0.0)
    wm, wm_t = wm32.astype(BF16), jnp.swapaxes(wm32, 1, 2).astype(BF16)
    bias = jnp.repeat(sm["gmlp_bs"].T, HEAD, axis=1)
    wax32 = jnp.concatenate([sm["w_a"], sm["w_x"]], axis=2)
    wax, wax_t = wax32.astype(BF16), jnp.swapaxes(wax32, 1, 2).astype(BF16)
    ln_g, ln_b = vec(sm["gmlp_ln_g"]), vec(sm["gmlp_ln_b"])
    post_g_v = vec(sm["post_g"])

    hn = _pre_norm(xs, pre_g)
    cw_shard = _pad_rows(conv_w.reshape(CONV_W, HEAD), ROWS)
    z, w_in_g, (w_out_g, w_pe_g, w_pg_g, cw_g) = _in_proj(
        hn, w_in[0].astype(BF16), [w_out[0].astype(BF16), w_pe[0].astype(BF16), w_pg[0].astype(BF16), cw_shard])
    w_out_f, w_pg_f = w_out_g.reshape(D_MODEL, D_MODEL), w_pg_g.reshape(D_MODEL, D_MODEL)
    cw_full = jnp.transpose(cw_g[:, :CONV_W, :], (1, 0, 2)).reshape(CONV_W, D_BR)
    mixer_consts = dict(cw=_pad_rows(cw_full, ROWS), cb=vec(sm["conv_b"]), ba=vec(sm["b_a"]), bx=vec(sm["b_x"]),
                        lam=vec(sm["lam"]), goa=vec(sm["gmlp_out_g"]), gob=vec(sm["lru_out_g"]))
    y, h = _mix_fwd(z, ln_g, ln_b, wm, bias, wax=wax, **mixer_consts)
    h1, ob = _out_proj(y, xs, w_out_f, post_g_v)
    dh2, dpe, dgl, h1b, loss_part = _ple_loss(h1, ps, tgt, w_pg_f, w_pe_g)

    dh1, do, dy, d_post_g = _tail_bwd(dh2, dgl, ob, w_pg_f, w_out_f, post_g_v)
    d_w_out, _ = _grad_w(y, do, 512, False, "grad_w_out")
    d_w_pg, _ = _grad_w(h1b, dgl, 512, False, "grad_w_pg")
    d_w_pe, _ = _grad_w(ps.astype(BF16), dpe, shard_rows, True, "grad_w_pe")
    (dz, vecs, d_ws, d_wax, d_bs), (parts_out, parts_pg, parts_pe) = _mix_bwd(
        z, dy, h, ln_g, ln_b, wm, wm_t, bias, wax=wax, wax_t=wax_t, **mixer_consts,
        ex_arrs=[d_w_out.reshape(N_DEV, shard_rows, D_MODEL), d_w_pg.reshape(N_DEV, shard_rows, D_MODEL), d_w_pe],
        ex_scatter=[True, True, True])

    small = {"gmlp_ln_g": vecs[V_LN_G], "gmlp_ln_b": vecs[V_LN_B], "gmlp_ws": d_ws, "gmlp_bs": d_bs,
             "conv_b": vecs[V_CONV_B], "w_a": d_wax[:, :, :HEAD], "b_a": vecs[V_B_A], "w_x": d_wax[:, :, HEAD:],
             "b_x": vecs[V_B_X], "lam": vecs[V_LAM], "gmlp_out_g": vecs[V_GOUT_A], "lru_out_g": vecs[V_GOUT_B],
             "post_g": d_post_g}
    small_part = _pack([small[n] for n in PACKED] + [loss_part]).reshape(N_DEV, PACK_ROWS // N_DEV, LANES)
    d_w_in, (small_blocks,) = _grad_w(hn, dz, W_IN_SHARD, True, "grad_w_in", ex_arrs=[small_part], ex_scatter=[True])
    small_sum = _sum_parts(small_blocks, "sum_small")
    d_cw_blocks = jnp.transpose(vecs[V_CONV_W:V_CONV_W + CONV_W].reshape(CONV_W, N_DEV, HEAD), (1, 0, 2))
    d_cw_blocks = jnp.concatenate([d_cw_blocks, jnp.zeros((N_DEV, ROWS - CONV_W, HEAD), F32)], axis=1)
    grad_x, d_pre_g, (parts_in, parts_cw, small_all) = _in_bwd(
        dz, w_in_g, xs, dh1, pre_g, ex_arrs=[d_w_in, d_cw_blocks, small_sum], ex_scatter=[True, True, False])
    parts_small = small_all.reshape(1, PACK_ROWS, LANES)
    pre_rows = D_MODEL // LANES
    parts_pre = _exchange([d_pre_g.reshape(pre_rows, LANES)], False, "gather_pre_g")[0]

    pad_cw = lambda a: _pad_rows(a.reshape(CONV_W, HEAD), ROWS)
    flat = lambda a: a.reshape(pre_rows, LANES)
    outs = {
        "w_in": _adamw(parts_in, w_in[0], m_w_in[0], v_w_in[0], "adamw_w_in", 256),
        "w_out": _adamw(parts_out, w_out[0], m_w_out[0], v_w_out[0], "adamw_w_out", 128),
        "w_pe": _adamw(parts_pe, w_pe[0], m_w_pe[0], v_w_pe[0], "adamw_w_pe", 256),
        "w_pg": _adamw(parts_pg, w_pg[0], m_w_pg[0], v_w_pg[0], "adamw_w_pg", 128),
        "conv_w": [a[:CONV_W] for a in
                   _adamw(parts_cw, pad_cw(conv_w), pad_cw(m_conv_w), pad_cw(v_conv_w), "adamw_conv_w", ROWS)],
        "pre_g": _adamw(parts_pre, flat(pre_g), flat(m_pre_g), flat(v_pre_g), "adamw_pre_g", pre_rows),
    }
    packed = _adamw(parts_small, _pack([weights[n] for n in PACKED]), _pack([m_in[n] for n in PACKED]),
                    _pack([v_in[n] for n in PACKED]), "adamw_small", PACK_TILE)
    row = 0
    for n in PACKED:
        n_rows = weights[n].size // LANES
        outs[n] = [packed[q][row:row + n_rows] for q in range(4)]
        row += n_rows
    loss = packed[0][row, 0]

    result = [loss, grad_x[None]]
    for q in range(4):
        result += [outs[n][q].reshape(weights[n].shape) for n in WEIGHTS]
    return tuple(result)
```

```python
import functools

import jax
import jax.numpy as jnp
from jax import lax
from jax.experimental import pallas as pl
from jax.experimental.pallas import tpu as pltpu

F32 = jnp.float32
BF16 = jnp.bfloat16
SDS = jax.ShapeDtypeStruct

D_MODEL = 2048
D_BR = 1024
D_IN = 5 * D_BR
D_PLE = 256
N_HEAD = 8
HEAD = 128
CHUNK = 128
ROWS = 8
N_GROUP = CHUNK // ROWS
N_DEV = 8
W_IN_SHARD = D_IN // N_DEV
EPS = 1e-6
LRU_C = 8.0
CONV_W = 4
MESH_AXES = ("x", "y", "c")
MIB = 1 << 20

ADAM_LR, ADAM_B1, ADAM_B2, ADAM_EPS, ADAM_WD, ADAM_STEP = 0.001, 0.9, 0.999, 1e-08, 0.01, 10

_GELU_C = 0.7978845608028654
_GELU_A = 0.044715

V_LN_G, V_LN_B, V_CONV_B, V_B_A, V_B_X, V_LAM, V_GOUT_A, V_GOUT_B, V_CONV_W = 0, 1, 2, 3, 4, 5, 6, 7, 8
N_VEC = 16


def _params(sem, vmem_mib):
    return pltpu.CompilerParams(dimension_semantics=sem, vmem_limit_bytes=int(vmem_mib * MIB))


def _sig(x):
    return 0.5 * jnp.tanh(0.5 * x) + 0.5


def _gelu(x):
    t = jnp.tanh(_GELU_C * (x + _GELU_A * x * x * x))
    return 0.5 * x * (1.0 + t), t


def _gelu_grad(x, t):
    return 0.5 * (1.0 + t) + 0.5 * x * (1.0 - t * t) * (_GELU_C * (1.0 + 3.0 * _GELU_A * x * x))


def _neg_expm1(y, exp_y):
    series = -y * (1.0 + y * (0.5 + y * (1.0 / 6.0)))
    return jnp.where(y > -0.01, series, 1.0 - exp_y)


def _softplus(x):
    return jnp.maximum(x, 0.0) + jnp.log(1.0 + jnp.exp(-jnp.abs(x)))


def _row_ids(width):
    return lax.broadcasted_iota(jnp.int32, (ROWS, width), 0)


def _shift_down(cur, prev, k, rid):
    return jnp.where(rid >= k, pltpu.roll(cur, k, 0), pltpu.roll(prev, k, 0))


def _shift_up(cur, nxt, k, rid):
    return jnp.where(rid < ROWS - k, pltpu.roll(cur, ROWS - k, 0), pltpu.roll(nxt, ROWS - k, 0))


def _mean_last(x):
    return jnp.mean(x, axis=-1, keepdims=True)


def _rows(g):
    return pl.ds(pl.multiple_of(g * ROWS, ROWS), ROWS)


TILE_ROWS = 16


def _tile_rows(q):
    return pl.ds(pl.multiple_of(q * TILE_ROWS, TILE_ROWS), TILE_ROWS)


UNROLL = 4


def _loop(n, body, init, unroll=UNROLL):
    def wide(i, carry):
        for u in range(unroll):
            carry = body(i * unroll + u, carry)
        return carry

    return lax.fori_loop(0, n // unroll, wide, init)


def _fold_rows(x):
    return x[0:ROWS, :] + x[ROWS:TILE_ROWS, :]


def _bcast_row(x, r):
    return jnp.broadcast_to(x[r:r + 1, :], x.shape)


def _dot(a, b):
    return jnp.dot(a, b, preferred_element_type=F32)


def _dot_nt(a, b):
    return lax.dot_general(a, b, (((1,), (1,)), ((), ())), preferred_element_type=F32)


def _dot_tn(a, b):
    return lax.dot_general(a, b, (((0,), (0,)), ((), ())), preferred_element_type=F32)


def _mesh_place():
    x, y, c = lax.axis_index("x"), lax.axis_index("y"), lax.axis_index("c")
    return x, y, c, 4 * x + 2 * y + c


def _peer(x, y, c, k):
    px = 1 - x if k & 4 else x
    py = 1 - y if k & 2 else y
    pc = 1 - c if k & 1 else c
    return (px, py, pc), 4 * px + 2 * py + pc


def _remote(src, dst, send_sem, recv_sem, dev):
    return pltpu.make_async_remote_copy(src_ref=src, dst_ref=dst, send_sem=send_sem, recv_sem=recv_sem, device_id=dev,
                                        device_id_type=pl.DeviceIdType.MESH)


ANY_SPEC = pl.BlockSpec(memory_space=pl.ANY)


class _Exchange:
    def __init__(self, arrs, scatter):
        self.n = len(arrs)
        self.scatter = tuple(scatter)
        self.out_shape = [SDS(a.shape if s else (N_DEV,) + a.shape, a.dtype) for a, s in zip(arrs, scatter)]
        self.scratch = [pltpu.SemaphoreType.DMA((self.n * N_DEV,)), pltpu.SemaphoreType.DMA((self.n * N_DEV,)),
                        pltpu.SemaphoreType.DMA((self.n,))]

    def _copies(self, ins, outs, sems):
        send_sems, recv_sems, local_sems = sems
        x, y, c, me = _mesh_place()
        local, sends, recvs = [], [], []
        for a in range(self.n):
            src = ins[a].at[me] if self.scatter[a] else ins[a]
            local.append(pltpu.make_async_copy(src, outs[a].at[me], local_sems.at[a]))
        for k in range(1, N_DEV):
            dev, lin = _peer(x, y, c, k)
            for a in range(self.n):
                src = ins[a].at[lin] if self.scatter[a] else ins[a]
                pair = (send_sems.at[a * N_DEV + k], recv_sems.at[a * N_DEV + k], dev)
                sends.append(_remote(src, outs[a].at[me], *pair))
                recvs.append(_remote(src, outs[a].at[lin], *pair))
        return local, sends, recvs

    def start(self, ins, outs, sems):
        local, sends, _ = self._copies(ins, outs, sems)
        for cp in local + sends:
            cp.start()

    def wait(self, ins, outs, sems):
        local, sends, recvs = self._copies(ins, outs, sems)
        for cp in recvs:
            cp.wait_recv()
        for cp in sends:
            cp.wait_send()
        for cp in local:
            cp.wait()


def _exchange(arrs, scatter, name):
    ex = _Exchange(arrs, [scatter] * len(arrs))
    n = ex.n

    def body(*refs):
        ins, outs, sems = refs[:n], refs[n:2 * n], refs[2 * n:]
        ex.start(ins, outs, sems)
        ex.wait(ins, outs, sems)

    return pl.pallas_call(
        body, name=name, out_shape=ex.out_shape, in_specs=[ANY_SPEC] * n, out_specs=[ANY_SPEC] * n,
        scratch_shapes=ex.scratch,
    )(*arrs)


def _pre_norm(x, pre_g, tm=512):
    t_len = x.shape[0]

    def body(x_ref, g_ref, hn_ref):
        g = g_ref[...]

        def rows_body(q, _):
            rows = _tile_rows(q)
            xv = x_ref[rows, :]
            hn_ref[rows, :] = (xv * lax.rsqrt(_mean_last(xv * xv) + EPS) * g).astype(BF16)
            return 0

        _loop(tm // TILE_ROWS, rows_body, 0)

    tile = pl.BlockSpec((tm, D_MODEL), lambda i: (i, 0))
    return pl.pallas_call(
        body, name="pre_norm", grid=(t_len // tm,),
        in_specs=[tile, pl.BlockSpec((1, D_MODEL), lambda i: (0, 0))], out_specs=tile,
        out_shape=SDS((t_len, D_MODEL), BF16),
        compiler_params=_params(("arbitrary",), 24),
    )(x, pre_g)


AG_ORDER = (0, 1, 2, 4, 6, 3, 5, 7)
SIBLING = 1
ICI_MASKS = (2, 4, 6)
DIRECT_MASKS = (SIBLING,) + ICI_MASKS


def _in_proj(hn, w_shard, others, tm=512):
    t_len = hn.shape[0]
    n_i = t_len // tm
    n_o = len(others)
    me_out = 4 * lax.axis_index("x") + 2 * lax.axis_index("y") + lax.axis_index("c")
    order = jnp.stack([me_out ^ k for k in AG_ORDER]).astype(jnp.int32)

    def body(order_ref, hn_ref, w_hbm, *refs):
        o_in = refs[:n_o]
        z_ref, wg_hbm = refs[n_o], refs[n_o + 1]
        o_out = refs[n_o + 2:2 * n_o + 2]
        wbuf, send_w, recv_w, fsend_w, frecv_w, send_o, recv_o, fsend_o, frecv_o, wb_sems, loc_sems = refs[2 * n_o + 2:]
        j, i = pl.program_id(0), pl.program_id(1)
        x, y, c, me = _mesh_place()
        sib = _peer(x, y, c, SIBLING)[0]

        def direct(k, a=None):
            dev, lin = _peer(x, y, c, k)
            if a is None:
                return (_remote(w_hbm, wbuf.at[me], send_w.at[k], recv_w.at[k], dev),
                        _remote(w_hbm, wbuf.at[lin], send_w.at[k], recv_w.at[k], dev))
            pair = (send_o.at[a * N_DEV + k], recv_o.at[a * N_DEV + k], dev)
            return _remote(o_in[a], o_out[a].at[me], *pair), _remote(o_in[a], o_out[a].at[lin], *pair)

        def passed(k, a=None):
            mine, theirs = _peer(x, y, c, k)[1], _peer(x, y, c, k ^ SIBLING)[1]
            if a is None:
                pair = (fsend_w.at[k], frecv_w.at[k], sib)
                return _remote(wbuf.at[mine], wbuf.at[mine], *pair), _remote(wbuf.at[theirs], wbuf.at[theirs], *pair)
            pair = (fsend_o.at[a * N_DEV + k], frecv_o.at[a * N_DEV + k], sib)
            return (_remote(o_out[a].at[mine], o_out[a].at[mine], *pair),
                    _remote(o_out[a].at[theirs], o_out[a].at[theirs], *pair))

        def own_copies():
            return [pltpu.make_async_copy(o_in[a], o_out[a].at[me], loc_sems.at[1 + a]) for a in range(n_o)]

        @pl.when(jnp.logical_and(j == 0, i == 0))
        def _():
            own = pltpu.make_async_copy(w_hbm, wbuf.at[me], loc_sems.at[0])
            own.start()
            for cp in own_copies():
                cp.start()
            for k in DIRECT_MASKS:
                direct(k)[0].start()
            for k in DIRECT_MASKS:
                for a in range(n_o):
                    direct(k, a)[0].start()
            own.wait()

        for jj in range(1, N_DEV):
            mask = AG_ORDER[jj]

            @pl.when(jnp.logical_and(j == jj, i == 0))
            def _(jj=jj, mask=mask):
                if mask in DIRECT_MASKS:
                    direct(mask)[1].wait_recv()
                    if mask in ICI_MASKS:
                        passed(mask)[0].start()
                else:
                    passed(mask ^ SIBLING)[1].wait_recv()
                late = jj - (N_DEV - len(ICI_MASKS))
                if late >= 0:
                    for a in range(n_o):
                        direct(ICI_MASKS[late], a)[1].wait_recv()
                        passed(ICI_MASKS[late], a)[0].start()

        slot = order_ref[j]

        @pl.when(i == 0)
        def _():
            pltpu.make_async_copy(wbuf.at[slot], wg_hbm.at[slot], wb_sems.at[j]).start()

        z_ref[...] = _dot(hn_ref[...], wbuf[slot])

        @pl.when(jnp.logical_and(j == N_DEV - 1, i == n_i - 1))
        def _():
            for a in range(n_o):
                direct(SIBLING, a)[1].wait_recv()
            for k in ICI_MASKS:
                for a in range(n_o):
                    passed(k, a)[1].wait_recv()
            for k in DIRECT_MASKS:
                direct(k)[0].wait_send()
                for a in range(n_o):
                    direct(k, a)[0].wait_send()
            for k in ICI_MASKS:
                passed(k)[0].wait_send()
                for a in range(n_o):
                    passed(k, a)[0].wait_send()
            for cp in own_copies():
                cp.wait()
            for jj in range(N_DEV):
                pltpu.make_async_copy(wbuf.at[0], wg_hbm.at[0], wb_sems.at[jj]).wait()

    dma = lambda n: pltpu.SemaphoreType.DMA((n,))
    grid_spec = pltpu.PrefetchScalarGridSpec(
        num_scalar_prefetch=1, grid=(N_DEV, n_i),
        in_specs=[pl.BlockSpec((tm, D_MODEL), lambda j, i, order: (i, 0)), ANY_SPEC] + [ANY_SPEC] * n_o,
        out_specs=[pl.BlockSpec((tm, W_IN_SHARD), lambda j, i, order: (i, order[j])), ANY_SPEC] + [ANY_SPEC] * n_o,
        scratch_shapes=[pltpu.VMEM((N_DEV, D_MODEL, W_IN_SHARD), BF16), dma(N_DEV), dma(N_DEV), dma(N_DEV), dma(N_DEV),
                        dma(n_o * N_DEV), dma(n_o * N_DEV), dma(n_o * N_DEV), dma(n_o * N_DEV), dma(N_DEV), dma(1 + n_o)])
    res = pl.pallas_call(
        body, name="in_proj", grid_spec=grid_spec,
        out_shape=[SDS((t_len, D_IN), F32), SDS((N_DEV, D_MODEL, W_IN_SHARD), BF16)]
        + [SDS((N_DEV,) + o.shape, o.dtype) for o in others],
        compiler_params=_params(("arbitrary", "arbitrary"), 44),
    )(order, hn, w_shard, *others)
    return res[0], res[1], res[2:]


def _conv_rows(cur, prev, cw_ref, cb, rid):
    acc = cw_ref[3:4, :] * cur + cb
    for k in range(1, CONV_W):
        acc = acc + cw_ref[3 - k:4 - k, :] * _shift_down(cur, prev, k, rid)
    return acc


def _lru_gates(pa, px, ba, bx, sp8, first_row):
    r = _sig(pa + ba)
    i = _sig(px + bx)
    la = -(r * sp8)
    a = jnp.exp(la)
    mult = jnp.where(first_row, 1.0, jnp.sqrt(_neg_expm1(2.0 * la, a * a)))
    return r, i, a, mult


def _mix_fwd(z, ln_g, ln_b, wm, bias, cw, cb, wax, ba, bx, lam, goa, gob):
    t_len = z.shape[0]
    n_chunk = t_len // CHUNK

    def body(z_ref, lng_ref, lnb_ref, wm_ref, bias_ref, cw_ref, cb_ref, wax_ref, ba_ref, bx_ref, lam_ref, goa_ref,
             gob_ref, y_ref, h_ref, vn_s, xc_s, mixed_s, pre_s, y_s, carry_s, halo_s):
        c_id = pl.program_id(0)
        rid = _row_ids(D_BR)

        @pl.when(c_id == 0)
        def _():
            carry_s[...] = jnp.zeros_like(carry_s)
            halo_s[...] = jnp.zeros_like(halo_s)

        lng, lnb, cb = lng_ref[...], lnb_ref[...], cb_ref[...]

        def phase1(g, prev):
            rows = _rows(g)
            vg, _ = _gelu(z_ref[rows, D_BR:2 * D_BR])
            xm = vg - _mean_last(vg)
            rs = lax.rsqrt(_mean_last(xm * xm) + EPS)
            vn_s[rows, :] = xm * rs * lng + lnb
            xb = z_ref[rows, 3 * D_BR:4 * D_BR]
            xc_s[rows, :] = _conv_rows(xb, prev, cw_ref, cb, rid)
            return xb

        halo_s[...] = _loop(N_GROUP, phase1, halo_s[...])

        for h in range(N_HEAD):
            cs = slice(h * HEAD, (h + 1) * HEAD)
            mixed_s[:, cs] = _dot(wm_ref[h], vn_s[:, cs].astype(BF16))
            pre = _dot(xc_s[:, cs].astype(BF16), wax_ref[h])
            pre_s[:, cs] = pre[:, :HEAD]
            pre_s[:, D_BR + h * HEAD:D_BR + (h + 1) * HEAD] = pre[:, HEAD:]

        ba, bx, goa, gob = ba_ref[...], bx_ref[...], goa_ref[...], gob_ref[...]
        sp8 = LRU_C * _softplus(-lam_ref[...])

        def phase3(g, carry):
            rows = _rows(g)
            ug, _ = _gelu(z_ref[rows, 0:D_BR])
            ga = z_ref[rows, 2 * D_BR:3 * D_BR]
            ya = ug * (mixed_s[rows, :] + bias_ref[rows, :]) * (ga * _sig(ga))
            y_s[rows, 0:D_BR] = ya * lax.rsqrt(_mean_last(ya * ya) + EPS) * goa

            first_row = jnp.logical_and(jnp.logical_and(c_id == 0, g == 0), rid == 0)
            _, i, a, mult = _lru_gates(pre_s[rows, 0:D_BR], pre_s[rows, D_BR:2 * D_BR], ba, bx, sp8, first_row)
            b = mult * i * xc_s[rows, :]
            for d in (1, 2, 4):
                a_sh = jnp.where(rid >= d, pltpu.roll(a, d, 0), 1.0)
                b_sh = jnp.where(rid >= d, pltpu.roll(b, d, 0), 0.0)
                b = a * b_sh + b
                a = a * a_sh
            hh = b + a * carry
            h_ref[rows, :] = hh
            gb = z_ref[rows, 4 * D_BR:5 * D_BR]
            yb = hh * (gb * _sig(gb))
            y_s[rows, D_BR:2 * D_BR] = yb * lax.rsqrt(_mean_last(yb * yb) + EPS) * gob
            return _bcast_row(hh, ROWS - 1)

        carry_s[...] = _loop(N_GROUP, phase3, carry_s[...])
        y_ref[...] = y_s[...].astype(BF16)

    vec = pl.BlockSpec((1, D_BR), lambda i: (0, 0))
    return pl.pallas_call(
        body, name="mix_fwd", grid=(n_chunk,),
        in_specs=[pl.BlockSpec((CHUNK, D_IN), lambda i: (i, 0)), vec, vec,
                  pl.BlockSpec((N_HEAD, HEAD, HEAD), lambda i: (0, 0, 0)),
                  pl.BlockSpec((CHUNK, D_BR), lambda i: (0, 0)),
                  pl.BlockSpec((ROWS, D_BR), lambda i: (0, 0)), vec,
                  pl.BlockSpec((N_HEAD, HEAD, 2 * HEAD), lambda i: (0, 0, 0)), vec, vec, vec, vec, vec],
        out_specs=[pl.BlockSpec((CHUNK, 2 * D_BR), lambda i: (i, 0)), pl.BlockSpec((CHUNK, D_BR), lambda i: (i, 0))],
        out_shape=[SDS((t_len, 2 * D_BR), BF16), SDS((t_len, D_BR), F32)],
        scratch_shapes=[pltpu.VMEM((CHUNK, D_BR), F32), pltpu.VMEM((CHUNK, D_BR), F32), pltpu.VMEM((CHUNK, D_BR), F32),
                        pltpu.VMEM((CHUNK, 2 * D_BR), F32), pltpu.VMEM((CHUNK, 2 * D_BR), F32),
                        pltpu.VMEM((ROWS, D_BR), F32), pltpu.VMEM((ROWS, D_BR), F32)],
        compiler_params=_params(("arbitrary",), 32),
    )(z, ln_g, ln_b, wm, bias, cw, cb, wax, ba, bx, lam, goa, gob)


def _load_weight(w_hbm, w_vmem, sem):
    @pl.when(pl.program_id(0) == 0)
    def _():
        cp = pltpu.make_async_copy(w_hbm, w_vmem, sem)
        cp.start()
        cp.wait()


def _out_proj(y, x, w_out, post_g, tm=512):
    t_len = y.shape[0]

    def body(y_ref, x_ref, w_hbm, g_ref, h1_ref, ob_ref, w_s, o_s, sem):
        _load_weight(w_hbm, w_s, sem)
        o_s[...] = _dot(y_ref[...], w_s[...])
        g = g_ref[...]

        def rows_body(q, _):
            rows = _tile_rows(q)
            o = o_s[rows, :]
            h1_ref[rows, :] = x_ref[rows, :] + o * lax.rsqrt(_mean_last(o * o) + EPS) * g
            ob_ref[rows, :] = o.astype(BF16)
            return 0

        _loop(tm // TILE_ROWS, rows_body, 0)

    tile = pl.BlockSpec((tm, D_MODEL), lambda i: (i, 0))
    return pl.pallas_call(
        body, name="out_proj", grid=(t_len // tm,),
        in_specs=[tile, tile, pl.BlockSpec(memory_space=pl.ANY), pl.BlockSpec((1, D_MODEL), lambda i: (0, 0))],
        out_specs=[tile, tile],
        out_shape=[SDS((t_len, D_MODEL), F32), SDS((t_len, D_MODEL), BF16)],
        scratch_shapes=[pltpu.VMEM((D_MODEL, D_MODEL), BF16), pltpu.VMEM((tm, D_MODEL), F32), pltpu.SemaphoreType.DMA],
        compiler_params=_params(("arbitrary",), 44),
    )(y, x, w_out, post_g)


def _ple_loss(h1, p, tgt, w_pg, w_pe_g, tm=256):
    t_len = h1.shape[0]
    n_tile = t_len // tm
    pe_shard = D_MODEL // N_DEV

    def body(h1_ref, p_ref, t_ref, w_hbm, wpe_ref, dh2_ref, dgl_ref, h1b_ref, loss_ref, dwpe_ref, w_s, pe_s, gl_s, acc_s,
             dpe_s, gpe_s, sem):
        _load_weight(w_hbm, w_s, sem)
        i = pl.program_id(0)

        @pl.when(i == 0)
        def _():
            acc_s[...] = jnp.zeros_like(acc_s)
            gpe_s[...] = jnp.zeros_like(gpe_s)

        h1b_ref[...] = h1_ref[...].astype(BF16)
        pb = p_ref[...].astype(BF16)
        for j in range(N_DEV):
            pe_s[:, j * pe_shard:(j + 1) * pe_shard] = _dot(pb, wpe_ref[j])
        gl_s[...] = _dot(h1b_ref[...], w_s[...])

        def rows_body(q, acc):
            rows = _tile_rows(q)
            pe = pe_s[rows, :]
            g = _sig(gl_s[rows, :])
            e = h1_ref[rows, :] + pe * g - t_ref[rows, :]
            dh2 = e * (1.0 / D_MODEL)
            dh2_ref[rows, :] = dh2
            dpe_s[rows, :] = (dh2 * g).astype(BF16)
            dgl_ref[rows, :] = (dh2 * pe * g * (1.0 - g)).astype(BF16)
            return acc + _fold_rows(e * e)

        acc_s[...] = _loop(tm // TILE_ROWS, rows_body, acc_s[...])
        gpe_s[...] += _dot_tn(pb, dpe_s[...])

        @pl.when(i == n_tile - 1)
        def _():
            loss_ref[...] = jnp.full(loss_ref.shape, 0.5 / D_MODEL * jnp.sum(acc_s[...]), F32)
            for j in range(N_DEV):
                dwpe_ref[j] = gpe_s[:, j * pe_shard:(j + 1) * pe_shard].astype(BF16)

    tile = pl.BlockSpec((tm, D_MODEL), lambda i: (i, 0))
    pe_blocks = pl.BlockSpec((N_DEV, D_PLE, pe_shard), lambda i: (0, 0, 0))
    return pl.pallas_call(
        body, name="ple_loss", grid=(n_tile,),
        in_specs=[tile, pl.BlockSpec((tm, D_PLE), lambda i: (i, 0)), tile, pl.BlockSpec(memory_space=pl.ANY), pe_blocks],
        out_specs=[tile, tile, tile, pl.BlockSpec((ROWS, HEAD), lambda i: (0, 0)), pe_blocks],
        out_shape=[SDS((t_len, D_MODEL), F32), SDS((t_len, D_MODEL), BF16), SDS((t_len, D_MODEL), BF16),
                   SDS((ROWS, HEAD), F32), SDS((N_DEV, D_PLE, pe_shard), BF16)],
        scratch_shapes=[pltpu.VMEM((D_MODEL, D_MODEL), BF16), pltpu.VMEM((tm, D_MODEL), F32),
                        pltpu.VMEM((tm, D_MODEL), F32), pltpu.VMEM((ROWS, D_MODEL), F32), pltpu.VMEM((tm, D_MODEL), BF16),
                        pltpu.VMEM((D_PLE, D_MODEL), F32), pltpu.SemaphoreType.DMA],
        compiler_params=_params(("arbitrary",), 48),
    )(h1, p, tgt, w_pg, w_pe_g)


def _tail_bwd(dh2, dgl, ob, w_pg, w_out, post_g, tm=256):
    t_len = dh2.shape[0]
    n_tile = t_len // tm

    def body(dh2_ref, dgl_ref, ob_ref, wpg_hbm, wout_hbm, g_ref, dh1_ref, do_ref, dy_ref, dg_ref, wpg_s, wout_s, t_s,
             acc_s, sems):
        _load_weight(wpg_hbm, wpg_s, sems.at[0])
        _load_weight(wout_hbm, wout_s, sems.at[1])
        i = pl.program_id(0)

        @pl.when(i == 0)
        def _():
            acc_s[...] = jnp.zeros_like(acc_s)

        t_s[...] = _dot_nt(dgl_ref[...], wpg_s[...])
        g = g_ref[...]

        def rows_body(q, acc):
            rows = _tile_rows(q)
            dh1 = dh2_ref[rows, :] + t_s[rows, :]
            dh1_ref[rows, :] = dh1
            o = ob_ref[rows, :].astype(F32)
            rr = lax.rsqrt(_mean_last(o * o) + EPS)
            on = o * rr
            dog = dh1 * g
            do_ref[rows, :] = (rr * (dog - on * _mean_last(dog * on))).astype(BF16)
            return acc + _fold_rows(dh1 * on)

        acc_s[...] = _loop(tm // TILE_ROWS, rows_body, acc_s[...])
        dy_ref[...] = _dot_nt(do_ref[...], wout_s[...]).astype(BF16)

        @pl.when(i == n_tile - 1)
        def _():
            dg_ref[...] = jnp.sum(acc_s[...], axis=0, keepdims=True)

    tile = pl.BlockSpec((tm, D_MODEL), lambda i: (i, 0))
    vec = pl.BlockSpec((1, D_MODEL), lambda i: (0, 0))
    hbm = pl.BlockSpec(memory_space=pl.ANY)
    return pl.pallas_call(
        body, name="tail_bwd", grid=(n_tile,),
        in_specs=[tile, tile, tile, hbm, hbm, vec],
        out_specs=[tile, tile, tile, vec],
        out_shape=[SDS((t_len, D_MODEL), F32), SDS((t_len, D_MODEL), BF16), SDS((t_len, D_MODEL), BF16),
                   SDS((1, D_MODEL), F32)],
        scratch_shapes=[pltpu.VMEM((D_MODEL, D_MODEL), BF16), pltpu.VMEM((D_MODEL, D_MODEL), BF16),
                        pltpu.VMEM((tm, D_MODEL), F32), pltpu.VMEM((ROWS, D_MODEL), F32), pltpu.SemaphoreType.DMA((2,))],
        compiler_params=_params(("arbitrary",), 48),
    )(dh2, dgl, ob, w_pg, w_out, post_g)


def _mix_bwd(z, dy, h, ln_g, ln_b, wm, wm_t, bias, cw, cb, wax, wax_t, ba, bx, lam, goa, gob, ex_arrs, ex_scatter):
    t_len = z.shape[0]
    n_chunk = t_len // CHUNK
    halo_blocks = CHUNK // ROWS
    ex = _Exchange(ex_arrs, ex_scatter)
    n_in, n_out, n_scratch = 19, 5, 17

    def body(*refs):
        (z_ref, zhalo_ref, dy_ref, h_ref, hhalo_ref, lng_ref, lnb_ref, wm_ref, wmt_ref, bias_ref, cw_ref, cb_ref,
         wax_ref, waxt_ref, ba_ref, bx_ref, lam_ref, goa_ref, gob_ref) = refs[:n_in]
        ex_in = refs[n_in:n_in + ex.n]
        dz_ref, vecs_ref, dws_ref, dwax_ref, dbs_ref = refs[n_in + ex.n:n_in + ex.n + n_out]
        ex_out = refs[n_in + ex.n + n_out:n_in + 2 * ex.n + n_out]
        (vn_s, vh_s, rs_s, xc_s, mixed_s, pre_s, dmix_s, dvn_s, dho_s, dxc_s, dpre_s, dz_s, acc_s, accdm_s,
         cg_s, ca_s, dxchalo_s) = refs[n_in + 2 * ex.n + n_out:n_in + 2 * ex.n + n_out + n_scratch]
        ex_sems = refs[n_in + 2 * ex.n + n_out + n_scratch:]
        step = pl.program_id(0)
        c_id = n_chunk - 1 - step
        rid = _row_ids(D_BR)
        first_chunk = c_id == 0

        @pl.when(step == 0)
        def _():
            ex.start(ex_in, ex_out, ex_sems)
            acc_s[...] = jnp.zeros_like(acc_s)
            accdm_s[...] = jnp.zeros_like(accdm_s)
            cg_s[...] = jnp.zeros_like(cg_s)
            ca_s[...] = jnp.zeros_like(ca_s)
            dxchalo_s[...] = jnp.zeros_like(dxchalo_s)
            dws_ref[...] = jnp.zeros_like(dws_ref)
            dwax_ref[...] = jnp.zeros_like(dwax_ref)

        lng, lnb, cb = lng_ref[...], lnb_ref[...], cb_ref[...]
        xb_halo = jnp.where(first_chunk, 0.0, zhalo_ref[...])
        h_halo = jnp.where(first_chunk, 0.0, hhalo_ref[...])

        def prev_rows(ref, cols, g, halo):
            before = ref[pl.ds(pl.multiple_of(jnp.maximum(g - 1, 0) * ROWS, ROWS), ROWS), cols]
            return jnp.where(g > 0, before, halo)

        def phase1(g, prev):
            rows = _rows(g)
            vg, _ = _gelu(z_ref[rows, D_BR:2 * D_BR])
            xm = vg - _mean_last(vg)
            rs = lax.rsqrt(_mean_last(xm * xm) + EPS)
            vh = xm * rs
            vh_s[rows, :] = vh
            rs_s[rows, :] = jnp.broadcast_to(rs, (ROWS, HEAD))
            vn_s[rows, :] = vh * lng + lnb
            xb = z_ref[rows, 3 * D_BR:4 * D_BR]
            xc_s[rows, :] = _conv_rows(xb, prev, cw_ref, cb, rid)
            return xb

        _loop(N_GROUP, phase1, xb_halo)

        for hd in range(N_HEAD):
            cs = slice(hd * HEAD, (hd + 1) * HEAD)
            mixed_s[:, cs] = _dot(wm_ref[hd], vn_s[:, cs].astype(BF16))
            pre = _dot(xc_s[:, cs].astype(BF16), wax_ref[hd])
            pre_s[:, cs] = pre[:, :HEAD]
            pre_s[:, D_BR + hd * HEAD:D_BR + (hd + 1) * HEAD] = pre[:, HEAD:]

        goa, gob = goa_ref[...], gob_ref[...]

        def phase3(g, _):
            rows = _rows(g)
            u = z_ref[rows, 0:D_BR]
            ug, tu = _gelu(u)
            ga = z_ref[rows, 2 * D_BR:3 * D_BR]
            sga = _sig(ga)
            sa = ga * sga
            mixed = mixed_s[rows, :] + bias_ref[rows, :]
            ya0 = ug * mixed
            ya = ya0 * sa
            ra = lax.rsqrt(_mean_last(ya * ya) + EPS)
            dyan = dy_ref[rows, 0:D_BR].astype(F32)
            acc_s[V_GOUT_A] += dyan * ya * ra
            dyg = dyan * goa
            dya = ra * dyg - ya * (ra * ra * ra) * _mean_last(dyg * ya)
            dya0 = dya * sa
            dz_s[rows, 2 * D_BR:3 * D_BR] = dya * ya0 * (sga * (1.0 + ga * (1.0 - sga)))
            dmix = dya0 * ug
            dmix_s[rows, :] = dmix
            accdm_s[rows, :] += dmix
            dz_s[rows, 0:D_BR] = dya0 * mixed * _gelu_grad(u, tu)

            hh = h_ref[rows, :]
            gb = z_ref[rows, 4 * D_BR:5 * D_BR]
            sgb = _sig(gb)
            sb = gb * sgb
            yb = hh * sb
            rb = lax.rsqrt(_mean_last(yb * yb) + EPS)
            dybn = dy_ref[rows, D_BR:2 * D_BR].astype(F32)
            acc_s[V_GOUT_B] += dybn * yb * rb
            dyg = dybn * gob
            dyb = rb * dyg - yb * (rb * rb * rb) * _mean_last(dyg * yb)
            dho_s[rows, :] = dyb * sb
            dz_s[rows, 4 * D_BR:5 * D_BR] = dyb * hh * (sgb * (1.0 + gb * (1.0 - sgb)))
            return 0

        _loop(N_GROUP, phase3, 0)

        for hd in range(N_HEAD):
            cs = slice(hd * HEAD, (hd + 1) * HEAD)
            dmb = dmix_s[:, cs].astype(BF16)
            dvn_s[:, cs] = _dot(wmt_ref[hd], dmb)
            dws_ref[hd] += _dot_nt(dmb, vn_s[:, cs].astype(BF16))

        def phase5(g, _):
            rows = _rows(g)
            dvn = dvn_s[rows, :]
            vh = vh_s[rows, :]
            acc_s[V_LN_G] += dvn * vh
            acc_s[V_LN_B] += dvn
            dvh = dvn * lng
            rs = rs_s[rows, 0:1]
            dvg = rs * (dvh - _mean_last(dvh) - vh * _mean_last(dvh * vh))
            v = z_ref[rows, D_BR:2 * D_BR]
            _, tv = _gelu(v)
            dz_s[rows, D_BR:2 * D_BR] = dvg * _gelu_grad(v, tv)
            return 0

        _loop(N_GROUP, phase5, 0)

        ba, bx = ba_ref[...], bx_ref[...]
        sp8 = LRU_C * _softplus(-lam_ref[...])

        def phase6(k, carry):
            cg, ca = carry
            g = N_GROUP - 1 - k
            rows = _rows(g)
            first_row = jnp.logical_and(jnp.logical_and(first_chunk, g == 0), rid == 0)
            r, i, a, mult = _lru_gates(pre_s[rows, 0:D_BR], pre_s[rows, D_BR:2 * D_BR], ba, bx, sp8, first_row)
            a_nx = jnp.where(rid < ROWS - 1, pltpu.roll(a, ROWS - 1, 0), ca)
            aa, bb = a_nx, dho_s[rows, :]
            for d in (1, 2, 4):
                a_sh = jnp.where(rid < ROWS - d, pltpu.roll(aa, ROWS - d, 0), 1.0)
                b_sh = jnp.where(rid < ROWS - d, pltpu.roll(bb, ROWS - d, 0), 0.0)
                bb = aa * b_sh + bb
                aa = aa * a_sh
            gg = bb + aa * cg
            hh = h_ref[rows, :]
            hprev = _shift_down(hh, prev_rows(h_ref, slice(None), g, h_halo), 1, rid)
            xc = xc_s[rows, :]
            gx = gg * xc
            dla = gg * hprev * a - jnp.where(first_row, 0.0, gx * i * (a * a) * lax.rsqrt(mult * mult))
            acc_s[V_LAM] += -(dla * r)
            dpa = -(dla * sp8) * r * (1.0 - r)
            dpx = gx * mult * i * (1.0 - i)
            acc_s[V_B_A] += dpa
            acc_s[V_B_X] += dpx
            dpre_s[rows, 0:D_BR] = dpa
            dpre_s[rows, D_BR:2 * D_BR] = dpx
            dxc_s[rows, :] = gg * mult * i
            return _bcast_row(gg, 0), _bcast_row(a, 0)

        cg, ca = _loop(N_GROUP, phase6, (cg_s[...], ca_s[...]))
        cg_s[...] = cg
        ca_s[...] = ca

        for hd in range(N_HEAD):
            cs = slice(hd * HEAD, (hd + 1) * HEAD)
            dpre = jnp.concatenate([dpre_s[:, cs], dpre_s[:, D_BR + hd * HEAD:D_BR + (hd + 1) * HEAD]], axis=1).astype(BF16)
            dxc_s[:, cs] += _dot(dpre, waxt_ref[hd])
            dwax_ref[hd] += _dot_tn(xc_s[:, cs].astype(BF16), dpre)

        def phase8(k, nxt):
            g = N_GROUP - 1 - k
            rows = _rows(g)
            dxc = dxc_s[rows, :]
            acc_s[V_CONV_B] += dxc
            xb = z_ref[rows, 3 * D_BR:4 * D_BR]
            xb_prev = prev_rows(z_ref, slice(3 * D_BR, 4 * D_BR), g, xb_halo)
            dxb = cw_ref[3:4, :] * dxc
            acc_s[V_CONV_W + 3] += dxc * xb
            for j in range(1, CONV_W):
                dxb = dxb + cw_ref[3 - j:4 - j, :] * _shift_up(dxc, nxt, j, rid)
                acc_s[V_CONV_W + 3 - j] += dxc * _shift_down(xb, xb_prev, j, rid)
            dz_s[rows, 3 * D_BR:4 * D_BR] = dxb
            return dxc

        dxchalo_s[...] = _loop(N_GROUP, phase8, dxchalo_s[...])
        dz_ref[...] = dz_s[...].astype(BF16)

        @pl.when(step == n_chunk - 1)
        def _():
            for v in range(N_VEC):
                vecs_ref[v:v + 1, :] = jnp.sum(acc_s[v], axis=0, keepdims=True)
            lam = lam_ref[...]
            vecs_ref[V_LAM:V_LAM + 1, :] = vecs_ref[V_LAM:V_LAM + 1, :] * (-LRU_C * _sig(-lam))
            tril = (lax.broadcasted_iota(jnp.int32, (HEAD, HEAD), 0) >= lax.broadcasted_iota(jnp.int32, (HEAD, HEAD), 1))
            ones = jnp.ones((ROWS, HEAD), BF16)
            for hd in range(N_HEAD):
                cs = slice(hd * HEAD, (hd + 1) * HEAD)
                dws_ref[hd] = jnp.where(tril, dws_ref[hd], 0.0)
                blk = accdm_s[:, cs]
                hi = blk.astype(BF16)
                lo = (blk - hi.astype(F32)).astype(BF16)
                dbs_ref[hd:hd + 1, :] = (_dot_nt(ones, hi) + _dot_nt(ones, lo))[0:1, :]
            ex.wait(ex_in, ex_out, ex_sems)

    vec = pl.BlockSpec((1, D_BR), lambda i: (0, 0))
    rev = lambda i: (n_chunk - 1 - i, 0)
    halo = lambda col: (lambda i: (jnp.maximum((n_chunk - 1 - i) * halo_blocks - 1, 0), col))
    full3 = lambda a, b, c: pl.BlockSpec((a, b, c), lambda i: (0, 0, 0))
    big = lambda w: pltpu.VMEM((CHUNK, w), F32)
    res = pl.pallas_call(
        body, name="mix_bwd", grid=(n_chunk,),
        in_specs=[pl.BlockSpec((CHUNK, D_IN), rev), pl.BlockSpec((ROWS, D_BR), halo(3)),
                  pl.BlockSpec((CHUNK, 2 * D_BR), rev), pl.BlockSpec((CHUNK, D_BR), rev),
                  pl.BlockSpec((ROWS, D_BR), halo(0)), vec, vec,
                  full3(N_HEAD, HEAD, HEAD), full3(N_HEAD, HEAD, HEAD),
                  pl.BlockSpec((CHUNK, D_BR), lambda i: (0, 0)), pl.BlockSpec((ROWS, D_BR), lambda i: (0, 0)), vec,
                  full3(N_HEAD, HEAD, 2 * HEAD), full3(N_HEAD, 2 * HEAD, HEAD), vec, vec, vec, vec, vec]
        + [ANY_SPEC] * ex.n,
        out_specs=[pl.BlockSpec((CHUNK, D_IN), rev), pl.BlockSpec((N_VEC, D_BR), lambda i: (0, 0)),
                   full3(N_HEAD, HEAD, HEAD), full3(N_HEAD, HEAD, 2 * HEAD),
                   pl.BlockSpec((N_HEAD, HEAD), lambda i: (0, 0))] + [ANY_SPEC] * ex.n,
        out_shape=[SDS((t_len, D_IN), BF16), SDS((N_VEC, D_BR), F32), SDS((N_HEAD, HEAD, HEAD), F32),
                   SDS((N_HEAD, HEAD, 2 * HEAD), F32), SDS((N_HEAD, HEAD), F32)] + ex.out_shape,
        scratch_shapes=[big(D_BR), big(D_BR), big(HEAD), big(D_BR), big(D_BR), big(2 * D_BR), big(D_BR), big(D_BR),
                        big(D_BR), big(D_BR), big(2 * D_BR), big(D_IN),
                        pltpu.VMEM((N_VEC, ROWS, D_BR), F32), big(D_BR),
                        pltpu.VMEM((ROWS, D_BR), F32), pltpu.VMEM((ROWS, D_BR), F32), pltpu.VMEM((ROWS, D_BR), F32)]
        + ex.scratch,
        compiler_params=_params(("arbitrary",), 48),
    )(z, z, dy, h, h, ln_g, ln_b, wm, wm_t, bias, cw, cb, wax, wax_t, ba, bx, lam, goa, gob, *ex_arrs)
    return res[:n_out], res[n_out:]


def _in_bwd(dz, w_in_g, x, dh1, pre_g, first_tile, n_tile, prev, name, ex_arrs=(), ex_scatter=(), tm=256):
    t_len = x.shape[0]
    ex = _Exchange(ex_arrs, ex_scatter)
    n_prev = 0 if prev is None else 2

    def body(dz_ref, w_hbm, x_ref, dh1_ref, g_ref, *refs):
        prev_refs, refs = refs[:n_prev], refs[n_prev:]
        ex_in, (gx_ref, dg_ref), ex_out = refs[:ex.n], refs[ex.n:ex.n + 2], refs[ex.n + 2:2 * ex.n + 2]
        w_s, t_s, dg_s, w_sems = refs[2 * ex.n + 2:2 * ex.n + 6]
        ex_sems = refs[2 * ex.n + 6:]
        i = pl.program_id(0)

        @pl.when(i == 0)
        def _():
            if ex.n:
                ex.start(ex_in, ex_out, ex_sems)
            loads = [pltpu.make_async_copy(w_hbm.at[s], w_s.at[:, s * W_IN_SHARD:(s + 1) * W_IN_SHARD], w_sems.at[s])
                     for s in range(N_DEV)]
            for cp in loads:
                cp.start()
            dg_s[...] = jnp.zeros_like(dg_s)
            for cp in loads:
                cp.wait()

        t_s[...] = _dot_nt(dz_ref[...], w_s[...])
        g = g_ref[...]

        def rows_body(q, acc):
            rows = _tile_rows(q)
            xv = x_ref[rows, :]
            r = lax.rsqrt(_mean_last(xv * xv) + EPS)
            xh = xv * r
            dhn = t_s[rows, :]
            dg = dhn * g
            gx_ref[rows, :] = dh1_ref[rows, :] + r * (dg - xh * _mean_last(dg * xh))
            return acc + _fold_rows(dhn * xh)

        dg_s[...] = _loop(tm // TILE_ROWS, rows_body, dg_s[...])

        @pl.when(i == n_tile - 1)
        def _():
            dg = jnp.sum(dg_s[...], axis=0, keepdims=True)
            dg_ref[...] = dg + prev_refs[1][...] if n_prev else dg
            if ex.n:
                ex.wait(ex_in, ex_out, ex_sems)

    tile = pl.BlockSpec((tm, D_MODEL), lambda i: (first_tile + i, 0))
    vec = pl.BlockSpec((1, D_MODEL), lambda i: (0, 0))
    prev_specs = [ANY_SPEC, vec] if n_prev else []
    res = pl.pallas_call(
        body, name=name, grid=(n_tile,),
        in_specs=[pl.BlockSpec((tm, D_IN), lambda i: (first_tile + i, 0)), ANY_SPEC, tile, tile, vec] + prev_specs
        + [ANY_SPEC] * ex.n,
        out_specs=[tile, vec] + [ANY_SPEC] * ex.n,
        out_shape=[SDS((t_len, D_MODEL), F32), SDS((1, D_MODEL), F32)] + ex.out_shape,
        scratch_shapes=[pltpu.VMEM((D_MODEL, D_IN), BF16), pltpu.VMEM((tm, D_MODEL), F32), pltpu.VMEM((ROWS, D_MODEL), F32),
                        pltpu.SemaphoreType.DMA((N_DEV,))] + (ex.scratch if ex.n else []),
        input_output_aliases={5: 0} if n_prev else {},
        compiler_params=_params(("arbitrary",), 54),
    )(dz, w_in_g, x, dh1, pre_g, *(prev or ()), *ex_arrs)
    return res[0], res[1], res[2:]


def _grad_w(a, b, bn, shard_major, name, tk=1024, ex_arrs=(), ex_scatter=()):
    t_len, m = a.shape
    n = b.shape[1]
    n_j, n_k = n // bn, t_len // tk
    ex = _Exchange(ex_arrs, ex_scatter)

    def body(a_ref, b_ref, *refs):
        ex_in, o_ref, ex_out = refs[:ex.n], refs[ex.n], refs[ex.n + 1:2 * ex.n + 1]
        acc_s, ex_sems = refs[2 * ex.n + 1], refs[2 * ex.n + 2:]
        j, k = pl.program_id(0), pl.program_id(1)
        if ex.n:
            @pl.when(jnp.logical_and(j == 0, k == 0))
            def _():
                ex.start(ex_in, ex_out, ex_sems)

        @pl.when(k == 0)
        def _():
            acc_s[...] = jnp.zeros_like(acc_s)

        acc_s[...] += _dot_tn(a_ref[...], b_ref[...])

        @pl.when(k == n_k - 1)
        def _():
            o_ref[...] = acc_s[...].astype(BF16)

        if ex.n:
            @pl.when(jnp.logical_and(j == n_j - 1, k == n_k - 1))
            def _():
                ex.wait(ex_in, ex_out, ex_sems)

    if shard_major:
        out_spec, out_shape = pl.BlockSpec((None, m, bn), lambda j, k: (j, 0, 0)), SDS((n_j, m, bn), BF16)
    else:
        out_spec, out_shape = pl.BlockSpec((m, bn), lambda j, k: (0, j)), SDS((m, n), BF16)
    res = pl.pallas_call(
        body, name=name, grid=(n_j, n_k),
        in_specs=[pl.BlockSpec((tk, m), lambda j, k: (k, 0)), pl.BlockSpec((tk, bn), lambda j, k: (k, j))]
        + [ANY_SPEC] * ex.n,
        out_specs=[out_spec] + [ANY_SPEC] * ex.n, out_shape=[out_shape] + ex.out_shape,
        scratch_shapes=[pltpu.VMEM((m, bn), F32)] + (ex.scratch if ex.n else []),
        compiler_params=_params(("arbitrary", "arbitrary"), 40),
    )(a, b, *ex_arrs)
    return res[0], res[1:]


def _sum_parts(parts, name):
    def body(p_ref, o_ref):
        g = p_ref[0].astype(F32)
        for s in range(1, parts.shape[0]):
            g = g + p_ref[s].astype(F32)
        o_ref[...] = g

    return pl.pallas_call(body, name=name, out_shape=SDS(parts.shape[1:], F32))(parts)


def _adamw(parts, w, m, v, name, tr):
    rows, cols = w.shape
    n_parts = parts.shape[0]
    c1 = 1.0 - ADAM_B1 ** ADAM_STEP
    c2 = 1.0 - ADAM_B2 ** ADAM_STEP

    def body(p_ref, w_ref, m_ref, v_ref, g_ref, d_ref, nm_ref, nv_ref):
        g = p_ref[0].astype(F32)
        for s in range(1, n_parts):
            g = g + p_ref[s].astype(F32)
        g_ref[...] = g
        nm = ADAM_B1 * m_ref[...] + (1.0 - ADAM_B1) * g
        nv = ADAM_B2 * v_ref[...] + (1.0 - ADAM_B2) * (g * g)
        nm_ref[...] = nm
        nv_ref[...] = nv
        d_ref[...] = -ADAM_LR * ((nm / c1) / (jnp.sqrt(nv / c2) + ADAM_EPS) + ADAM_WD * w_ref[...])

    tile = pl.BlockSpec((tr, cols), lambda i: (i, 0))
    return pl.pallas_call(
        body, name=name, grid=(rows // tr,),
        in_specs=[pl.BlockSpec((n_parts, tr, cols), lambda i: (0, i, 0)), tile, tile, tile],
        out_specs=[tile] * 4, out_shape=[SDS((rows, cols), F32)] * 4,
        compiler_params=_params(("arbitrary",), 40),
    )(parts, w, m, v)


PACKED = ("gmlp_ln_g", "gmlp_ln_b", "gmlp_ws", "gmlp_bs", "conv_b", "w_a", "b_a", "w_x", "b_x", "lam", "gmlp_out_g",
          "lru_out_g", "post_g")
WEIGHTS = ("pre_g", "w_in", "gmlp_ln_g", "gmlp_ln_b", "gmlp_ws", "gmlp_bs", "conv_w", "conv_b", "w_a", "b_a", "w_x",
           "b_x", "lam", "gmlp_out_g", "lru_out_g", "w_out", "post_g", "w_pe", "w_pg")
LANES = 128


PACK_ROWS = 3200
PACK_TILE = 640
IN_BWD_TILE = 256


def _pack(parts):
    rows = [p.reshape(-1, LANES) for p in parts]
    used = sum(r.shape[0] for r in rows)
    return jnp.concatenate(rows + [jnp.zeros((PACK_ROWS - used, LANES), F32)], axis=0)


def _pad_rows(a, rows):
    return jnp.concatenate([a, jnp.zeros((rows - a.shape[0],) + a.shape[1:], a.dtype)], axis=0)


def kernel(x, p, pre_g, w_in, gmlp_ln_g, gmlp_ln_b, gmlp_ws, gmlp_bs, conv_w, conv_b, w_a, b_a, w_x, b_x, lam, gmlp_out_g, lru_out_g, w_out, post_g, w_pe, w_pg, loss_target, m_pre_g, m_w_in, m_gmlp_ln_g, m_gmlp_ln_b, m_gmlp_ws, m_gmlp_bs, m_conv_w, m_conv_b, m_w_a, m_b_a, m_w_x, m_b_x, m_lam, m_gmlp_out_g, m_lru_out_g, m_w_out, m_post_g, m_w_pe, m_w_pg, v_pre_g, v_w_in, v_gmlp_ln_g, v_gmlp_ln_b, v_gmlp_ws, v_gmlp_bs, v_conv_w, v_conv_b, v_w_a, v_b_a, v_w_x, v_b_x, v_lam, v_gmlp_out_g, v_lru_out_g, v_w_out, v_post_g, v_w_pe, v_w_pg):
    args = dict(locals())
    weights = {n: args[n] for n in WEIGHTS}
    m_in = {n: args["m_" + n] for n in WEIGHTS}
    v_in = {n: args["v_" + n] for n in WEIGHTS}
    sm = {n: weights[n][0] for n in PACKED}
    shard_rows = D_MODEL // N_DEV
    xs, ps, tgt = x[0], p[0, 0], loss_target[0]

    vec = lambda a: a.reshape(1, -1)
    tril = jnp.tril(jnp.ones((CHUNK, CHUNK), dtype=bool))
    wm32 = jnp.where(tril[None], sm["gmlp_ws"], 0.0)
    wm, wm_t = wm32.astype(BF16), jnp.swapaxes(wm32, 1, 2).astype(BF16)
    bias = jnp.repeat(sm["gmlp_bs"].T, HEAD, axis=1)
    wax32 = jnp.concatenate([sm["w_a"], sm["w_x"]], axis=2)
    wax, wax_t = wax32.astype(BF16), jnp.swapaxes(wax32, 1, 2).astype(BF16)
    ln_g, ln_b = vec(sm["gmlp_ln_g"]), vec(sm["gmlp_ln_b"])
    post_g_v = vec(sm["post_g"])

    hn = _pre_norm(xs, pre_g)
    cw_shard = _pad_rows(conv_w.reshape(CONV_W, HEAD), ROWS)
    z, w_in_g, (w_out_g, w_pe_g, w_pg_g, cw_g) = _in_proj(
        hn, w_in[0].astype(BF16), [w_out[0].astype(BF16), w_pe[0].astype(BF16), w_pg[0].astype(BF16), cw_shard])
    w_out_f, w_pg_f = w_out_g.reshape(D_MODEL, D_MODEL), w_pg_g.reshape(D_MODEL, D_MODEL)
    cw_full = jnp.transpose(cw_g[:, :CONV_W, :], (1, 0, 2)).reshape(CONV_W, D_BR)
    mixer_consts = dict(cw=_pad_rows(cw_full, ROWS), cb=vec(sm["conv_b"]), ba=vec(sm["b_a"]), bx=vec(sm["b_x"]),
                        lam=vec(sm["lam"]), goa=vec(sm["gmlp_out_g"]), gob=vec(sm["lru_out_g"]))
    y, h = _mix_fwd(z, ln_g, ln_b, wm, bias, wax=wax, **mixer_consts)
    h1, ob = _out_proj(y, xs, w_out_f, post_g_v)
    dh2, dgl, h1b, loss_part, d_w_pe = _ple_loss(h1, ps, tgt, w_pg_f, w_pe_g)

    dh1, do, dy, d_post_g = _tail_bwd(dh2, dgl, ob, w_pg_f, w_out_f, post_g_v)
    d_w_out, _ = _grad_w(y, do, 512, False, "grad_w_out")
    d_w_pg, _ = _grad_w(h1b, dgl, 512, False, "grad_w_pg")
    (dz, vecs, d_ws, d_wax, d_bs), (parts_out, parts_pg, parts_pe) = _mix_bwd(
        z, dy, h, ln_g, ln_b, wm, wm_t, bias, wax=wax, wax_t=wax_t, **mixer_consts,
        ex_arrs=[d_w_out.reshape(N_DEV, shard_rows, D_MODEL), d_w_pg.reshape(N_DEV, shard_rows, D_MODEL), d_w_pe],
        ex_scatter=[True, True, True])

    small = {"gmlp_ln_g": vecs[V_LN_G], "gmlp_ln_b": vecs[V_LN_B], "gmlp_ws": d_ws, "gmlp_bs": d_bs,
             "conv_b": vecs[V_CONV_B], "w_a": d_wax[:, :, :HEAD], "b_a": vecs[V_B_A], "w_x": d_wax[:, :, HEAD:],
             "b_x": vecs[V_B_X], "lam": vecs[V_LAM], "gmlp_out_g": vecs[V_GOUT_A], "lru_out_g": vecs[V_GOUT_B],
             "post_g": d_post_g}
    small_part = _pack([small[n] for n in PACKED] + [loss_part]).reshape(N_DEV, PACK_ROWS // N_DEV, LANES)
    d_w_in, (small_blocks,) = _grad_w(hn, dz, W_IN_SHARD, True, "grad_w_in", ex_arrs=[small_part], ex_scatter=[True])
    small_sum = _sum_parts(small_blocks, "sum_small")
    d_cw_blocks = jnp.transpose(vecs[V_CONV_W:V_CONV_W + CONV_W].reshape(CONV_W, N_DEV, HEAD), (1, 0, 2))
    d_cw_blocks = jnp.concatenate([d_cw_blocks, jnp.zeros((N_DEV, ROWS - CONV_W, HEAD), F32)], axis=1)
    n_tile = xs.shape[0] // IN_BWD_TILE
    gx_most, dg_most, (parts_in, parts_cw, small_all) = _in_bwd(
        dz, w_in_g, xs, dh1, pre_g, 0, n_tile - 1, None, "in_bwd",
        ex_arrs=[d_w_in, d_cw_blocks, small_sum], ex_scatter=[True, True, False], tm=IN_BWD_TILE)
    grad_x, d_pre_g, _ = _in_bwd(dz, w_in_g, xs, dh1, pre_g, n_tile - 1, 1, (gx_most, dg_most), "in_bwd_last",
                                 tm=IN_BWD_TILE)
    parts_small = small_all.reshape(1, PACK_ROWS, LANES)
    pre_rows = D_MODEL // LANES
    parts_pre = _exchange([d_pre_g.reshape(pre_rows, LANES)], False, "gather_pre_g")[0]

    pad_cw = lambda a: _pad_rows(a.reshape(CONV_W, HEAD), ROWS)
    flat = lambda a: a.reshape(pre_rows, LANES)
    outs = {
        "w_in": _adamw(parts_in, w_in[0], m_w_in[0], v_w_in[0], "adamw_w_in", 256),
        "w_out": _adamw(parts_out, w_out[0], m_w_out[0], v_w_out[0], "adamw_w_out", 128),
        "w_pe": _adamw(parts_pe, w_pe[0], m_w_pe[0], v_w_pe[0], "adamw_w_pe", 256),
        "w_pg": _adamw(parts_pg, w_pg[0], m_w_pg[0], v_w_pg[0], "adamw_w_pg", 128),
        "conv_w": [a[:CONV_W] for a in
                   _adamw(parts_cw, pad_cw(conv_w), pad_cw(m_conv_w), pad_cw(v_conv_w), "adamw_conv_w", ROWS)],
        "pre_g": _adamw(parts_pre, flat(pre_g), flat(m_pre_g), flat(v_pre_g), "adamw_pre_g", pre_rows),
    }
    packed = _adamw(parts_small, _pack([weights[n] for n in PACKED]), _pack([m_in[n] for n in PACKED]),
                    _pack([v_in[n] for n in PACKED]), "adamw_small", PACK_TILE)
    row = 0
    for n in PACKED:
        n_rows = weights[n].size // LANES
        outs[n] = [packed[q][row:row + n_rows] for q in range(4)]
        row += n_rows
    loss = packed[0][row, 0]

    result = [loss, grad_x[None]]
    for q in range(4):
        result += [outs[n][q].reshape(weights[n].shape) for n in WEIGHTS]
    return tuple(result)
```

```python
import functools

import jax
import jax.numpy as jnp
from jax import lax
from jax.experimental import pallas as pl
from jax.experimental.pallas import tpu as pltpu

F32 = jnp.float32
BF16 = jnp.bfloat16
SDS = jax.ShapeDtypeStruct

D_MODEL = 2048
D_BR = 1024
D_IN = 5 * D_BR
D_PLE = 256
N_HEAD = 8
HEAD = 128
CHUNK = 128
ROWS = 8
N_GROUP = CHUNK // ROWS
N_DEV = 8
W_IN_SHARD = D_IN // N_DEV
EPS = 1e-6
LRU_C = 8.0
CONV_W = 4
MESH_AXES = ("x", "y", "c")
MIB = 1 << 20

ADAM_LR, ADAM_B1, ADAM_B2, ADAM_EPS, ADAM_WD, ADAM_STEP = 0.001, 0.9, 0.999, 1e-08, 0.01, 10

_GELU_C = 0.7978845608028654
_GELU_A = 0.044715

V_LN_G, V_LN_B, V_CONV_B, V_B_A, V_B_X, V_LAM, V_GOUT_A, V_GOUT_B, V_CONV_W = 0, 1, 2, 3, 4, 5, 6, 7, 8
N_VEC = 16


def _params(sem, vmem_mib):
    return pltpu.CompilerParams(dimension_semantics=sem, vmem_limit_bytes=int(vmem_mib * MIB))


def _sig(x):
    return 0.5 * jnp.tanh(0.5 * x) + 0.5


def _gelu(x):
    t = jnp.tanh(_GELU_C * (x + _GELU_A * x * x * x))
    return 0.5 * x * (1.0 + t), t


def _gelu_grad(x, t):
    return 0.5 * (1.0 + t) + 0.5 * x * (1.0 - t * t) * (_GELU_C * (1.0 + 3.0 * _GELU_A * x * x))


def _neg_expm1(y, exp_y):
    series = -y * (1.0 + y * (0.5 + y * (1.0 / 6.0)))
    return jnp.where(y > -0.01, series, 1.0 - exp_y)


def _softplus(x):
    return jnp.maximum(x, 0.0) + jnp.log(1.0 + jnp.exp(-jnp.abs(x)))


def _row_ids(width):
    return lax.broadcasted_iota(jnp.int32, (ROWS, width), 0)


def _shift_down(cur, prev, k, rid):
    return jnp.where(rid >= k, pltpu.roll(cur, k, 0), pltpu.roll(prev, k, 0))


def _shift_up(cur, nxt, k, rid):
    return jnp.where(rid < ROWS - k, pltpu.roll(cur, ROWS - k, 0), pltpu.roll(nxt, ROWS - k, 0))


def _mean_last(x):
    return jnp.mean(x, axis=-1, keepdims=True)


def _rows(g):
    return pl.ds(pl.multiple_of(g * ROWS, ROWS), ROWS)


TILE_ROWS = 16


def _tile_rows(q):
    return pl.ds(pl.multiple_of(q * TILE_ROWS, TILE_ROWS), TILE_ROWS)


UNROLL = 4


def _loop(n, body, init, unroll=UNROLL):
    def wide(i, carry):
        for u in range(unroll):
            carry = body(i * unroll + u, carry)
        return carry

    return lax.fori_loop(0, n // unroll, wide, init)


def _fold_rows(x):
    return x[0:ROWS, :] + x[ROWS:TILE_ROWS, :]


def _bcast_row(x, r):
    return jnp.broadcast_to(x[r:r + 1, :], x.shape)


def _dot(a, b):
    return jnp.dot(a, b, preferred_element_type=F32)


def _dot_nt(a, b):
    return lax.dot_general(a, b, (((1,), (1,)), ((), ())), preferred_element_type=F32)


def _dot_tn(a, b):
    return lax.dot_general(a, b, (((0,), (0,)), ((), ())), preferred_element_type=F32)


def _mesh_place():
    x, y, c = lax.axis_index("x"), lax.axis_index("y"), lax.axis_index("c")
    return x, y, c, 4 * x + 2 * y + c


def _peer(x, y, c, k):
    px = 1 - x if k & 4 else x
    py = 1 - y if k & 2 else y
    pc = 1 - c if k & 1 else c
    return (px, py, pc), 4 * px + 2 * py + pc


def _remote(src, dst, send_sem, recv_sem, dev):
    return pltpu.make_async_remote_copy(src_ref=src, dst_ref=dst, send_sem=send_sem, recv_sem=recv_sem, device_id=dev,
                                        device_id_type=pl.DeviceIdType.MESH)


ANY_SPEC = pl.BlockSpec(memory_space=pl.ANY)


class _Exchange:
    def __init__(self, arrs, scatter):
        self.n = len(arrs)
        self.scatter = tuple(scatter)
        self.out_shape = [SDS(a.shape if s else (N_DEV,) + a.shape, a.dtype) for a, s in zip(arrs, scatter)]
        self.scratch = [pltpu.SemaphoreType.DMA((self.n * N_DEV,)), pltpu.SemaphoreType.DMA((self.n * N_DEV,)),
                        pltpu.SemaphoreType.DMA((self.n,))]

    def _copies(self, ins, outs, sems):
        send_sems, recv_sems, local_sems = sems
        x, y, c, me = _mesh_place()
        local, sends, recvs = [], [], []
        for a in range(self.n):
            src = ins[a].at[me] if self.scatter[a] else ins[a]
            local.append(pltpu.make_async_copy(src, outs[a].at[me], local_sems.at[a]))
        for k in range(1, N_DEV):
            dev, lin = _peer(x, y, c, k)
            for a in range(self.n):
                src = ins[a].at[lin] if self.scatter[a] else ins[a]
                pair = (send_sems.at[a * N_DEV + k], recv_sems.at[a * N_DEV + k], dev)
                sends.append(_remote(src, outs[a].at[me], *pair))
                recvs.append(_remote(src, outs[a].at[lin], *pair))
        return local, sends, recvs

    def start(self, ins, outs, sems):
        local, sends, _ = self._copies(ins, outs, sems)
        for cp in local + sends:
            cp.start()

    def wait(self, ins, outs, sems):
        local, sends, recvs = self._copies(ins, outs, sems)
        for cp in recvs:
            cp.wait_recv()
        for cp in sends:
            cp.wait_send()
        for cp in local:
            cp.wait()


def _exchange(arrs, scatter, name):
    ex = _Exchange(arrs, [scatter] * len(arrs))
    n = ex.n

    def body(*refs):
        ins, outs, sems = refs[:n], refs[n:2 * n], refs[2 * n:]
        ex.start(ins, outs, sems)
        ex.wait(ins, outs, sems)

    return pl.pallas_call(
        body, name=name, out_shape=ex.out_shape, in_specs=[ANY_SPEC] * n, out_specs=[ANY_SPEC] * n,
        scratch_shapes=ex.scratch,
    )(*arrs)


def _pre_norm(x, pre_g, tm=512):
    t_len = x.shape[0]

    def body(x_ref, g_ref, hn_ref):
        g = g_ref[...]

        def rows_body(q, _):
            rows = _tile_rows(q)
            xv = x_ref[rows, :]
            hn_ref[rows, :] = (xv * lax.rsqrt(_mean_last(xv * xv) + EPS) * g).astype(BF16)
            return 0

        _loop(tm // TILE_ROWS, rows_body, 0)

    tile = pl.BlockSpec((tm, D_MODEL), lambda i: (i, 0))
    return pl.pallas_call(
        body, name="pre_norm", grid=(t_len // tm,),
        in_specs=[tile, pl.BlockSpec((1, D_MODEL), lambda i: (0, 0))], out_specs=tile,
        out_shape=SDS((t_len, D_MODEL), BF16),
        compiler_params=_params(("arbitrary",), 24),
    )(x, pre_g)


AG_ORDER = (0, 1, 2, 4, 6, 3, 5, 7)
SIBLING = 1
ICI_MASKS = (2, 4, 6)
DIRECT_MASKS = (SIBLING,) + ICI_MASKS


def _in_proj(hn, w_shard, others, tm=512):
    t_len = hn.shape[0]
    n_i = t_len // tm
    n_o = len(others)
    me_out = 4 * lax.axis_index("x") + 2 * lax.axis_index("y") + lax.axis_index("c")
    order = jnp.stack([me_out ^ k for k in AG_ORDER]).astype(jnp.int32)

    def body(order_ref, hn_ref, w_hbm, *refs):
        o_in = refs[:n_o]
        z_ref, wg_hbm = refs[n_o], refs[n_o + 1]
        o_out = refs[n_o + 2:2 * n_o + 2]
        wbuf, send_w, recv_w, fsend_w, frecv_w, send_o, recv_o, fsend_o, frecv_o, wb_sems, loc_sems = refs[2 * n_o + 2:]
        j, i = pl.program_id(0), pl.program_id(1)
        x, y, c, me = _mesh_place()
        sib = _peer(x, y, c, SIBLING)[0]

        def direct(k, a=None):
            dev, lin = _peer(x, y, c, k)
            if a is None:
                return (_remote(w_hbm, wbuf.at[me], send_w.at[k], recv_w.at[k], dev),
                        _remote(w_hbm, wbuf.at[lin], send_w.at[k], recv_w.at[k], dev))
            pair = (send_o.at[a * N_DEV + k], recv_o.at[a * N_DEV + k], dev)
            return _remote(o_in[a], o_out[a].at[me], *pair), _remote(o_in[a], o_out[a].at[lin], *pair)

        def passed(k, a=None):
            mine, theirs = _peer(x, y, c, k)[1], _peer(x, y, c, k ^ SIBLING)[1]
            if a is None:
                pair = (fsend_w.at[k], frecv_w.at[k], sib)
                return _remote(wbuf.at[mine], wbuf.at[mine], *pair), _remote(wbuf.at[theirs], wbuf.at[theirs], *pair)
            pair = (fsend_o.at[a * N_DEV + k], frecv_o.at[a * N_DEV + k], sib)
            return (_remote(o_out[a].at[mine], o_out[a].at[mine], *pair),
                    _remote(o_out[a].at[theirs], o_out[a].at[theirs], *pair))

        def own_copies():
            return [pltpu.make_async_copy(o_in[a], o_out[a].at[me], loc_sems.at[1 + a]) for a in range(n_o)]

        @pl.when(jnp.logical_and(j == 0, i == 0))
        def _():
            own = pltpu.make_async_copy(w_hbm, wbuf.at[me], loc_sems.at[0])
            own.start()
            for cp in own_copies():
                cp.start()
            for k in DIRECT_MASKS:
                direct(k)[0].start()
            for k in DIRECT_MASKS:
                for a in range(n_o):
                    direct(k, a)[0].start()
            own.wait()

        for jj in range(1, N_DEV):
            mask = AG_ORDER[jj]

            @pl.when(jnp.logical_and(j == jj, i == 0))
            def _(jj=jj, mask=mask):
                if mask in DIRECT_MASKS:
                    direct(mask)[1].wait_recv()
                    if mask in ICI_MASKS:
                        passed(mask)[0].start()
                else:
                    passed(mask ^ SIBLING)[1].wait_recv()
                late = jj - (N_DEV - len(ICI_MASKS))
                if late >= 0:
                    for a in range(n_o):
                        direct(ICI_MASKS[late], a)[1].wait_recv()
                        passed(ICI_MASKS[late], a)[0].start()

        slot = order_ref[j]

        @pl.when(i == 0)
        def _():
            pltpu.make_async_copy(wbuf.at[slot], wg_hbm.at[slot], wb_sems.at[j]).start()

        z_ref[...] = _dot(hn_ref[...], wbuf[slot])

        @pl.when(jnp.logical_and(j == N_DEV - 1, i == n_i - 1))
        def _():
            for a in range(n_o):
                direct(SIBLING, a)[1].wait_recv()
            for k in ICI_MASKS:
                for a in range(n_o):
                    passed(k, a)[1].wait_recv()
            for k in DIRECT_MASKS:
                direct(k)[0].wait_send()
                for a in range(n_o):
                    direct(k, a)[0].wait_send()
            for k in ICI_MASKS:
                passed(k)[0].wait_send()
                for a in range(n_o):
                    passed(k, a)[0].wait_send()
            for cp in own_copies():
                cp.wait()
            for jj in range(N_DEV):
                pltpu.make_async_copy(wbuf.at[0], wg_hbm.at[0], wb_sems.at[jj]).wait()

    dma = lambda n: pltpu.SemaphoreType.DMA((n,))
    grid_spec = pltpu.PrefetchScalarGridSpec(
        num_scalar_prefetch=1, grid=(N_DEV, n_i),
        in_specs=[pl.BlockSpec((tm, D_MODEL), lambda j, i, order: (i, 0)), ANY_SPEC] + [ANY_SPEC] * n_o,
        out_specs=[pl.BlockSpec((tm, W_IN_SHARD), lambda j, i, order: (i, order[j])), ANY_SPEC] + [ANY_SPEC] * n_o,
        scratch_shapes=[pltpu.VMEM((N_DEV, D_MODEL, W_IN_SHARD), BF16), dma(N_DEV), dma(N_DEV), dma(N_DEV), dma(N_DEV),
                        dma(n_o * N_DEV), dma(n_o * N_DEV), dma(n_o * N_DEV), dma(n_o * N_DEV), dma(N_DEV), dma(1 + n_o)])
    res = pl.pallas_call(
        body, name="in_proj", grid_spec=grid_spec,
        out_shape=[SDS((t_len, D_IN), F32), SDS((N_DEV, D_MODEL, W_IN_SHARD), BF16)]
        + [SDS((N_DEV,) + o.shape, o.dtype) for o in others],
        compiler_params=_params(("arbitrary", "arbitrary"), 44),
    )(order, hn, w_shard, *others)
    return res[0], res[1], res[2:]


def _conv_rows(cur, prev, cw_ref, cb, rid):
    acc = cw_ref[3:4, :] * cur + cb
    for k in range(1, CONV_W):
        acc = acc + cw_ref[3 - k:4 - k, :] * _shift_down(cur, prev, k, rid)
    return acc


def _lru_gates(pa, px, ba, bx, sp8, first_row):
    r = _sig(pa + ba)
    i = _sig(px + bx)
    la = -(r * sp8)
    a = jnp.exp(la)
    mult = jnp.where(first_row, 1.0, jnp.sqrt(_neg_expm1(2.0 * la, a * a)))
    return r, i, a, mult


def _mix_fwd(z, ln_g, ln_b, wm, bias, cw, cb, wax, ba, bx, lam, goa, gob):
    t_len = z.shape[0]
    n_chunk = t_len // CHUNK

    def body(z_ref, lng_ref, lnb_ref, wm_ref, bias_ref, cw_ref, cb_ref, wax_ref, ba_ref, bx_ref, lam_ref, goa_ref,
             gob_ref, y_ref, h_ref, vn_s, xc_s, mixed_s, pre_s, y_s, carry_s, halo_s):
        c_id = pl.program_id(0)
        rid = _row_ids(D_BR)

        @pl.when(c_id == 0)
        def _():
            carry_s[...] = jnp.zeros_like(carry_s)
            halo_s[...] = jnp.zeros_like(halo_s)

        lng, lnb, cb = lng_ref[...], lnb_ref[...], cb_ref[...]

        def phase1(g, prev):
            rows = _rows(g)
            vg, _ = _gelu(z_ref[rows, D_BR:2 * D_BR])
            xm = vg - _mean_last(vg)
            rs = lax.rsqrt(_mean_last(xm * xm) + EPS)
            vn_s[rows, :] = xm * rs * lng + lnb
            xb = z_ref[rows, 3 * D_BR:4 * D_BR]
            xc_s[rows, :] = _conv_rows(xb, prev, cw_ref, cb, rid)
            return xb

        halo_s[...] = _loop(N_GROUP, phase1, halo_s[...])

        for h in range(N_HEAD):
            cs = slice(h * HEAD, (h + 1) * HEAD)
            mixed_s[:, cs] = _dot(wm_ref[h], vn_s[:, cs].astype(BF16))
            pre = _dot(xc_s[:, cs].astype(BF16), wax_ref[h])
            pre_s[:, cs] = pre[:, :HEAD]
            pre_s[:, D_BR + h * HEAD:D_BR + (h + 1) * HEAD] = pre[:, HEAD:]

        ba, bx, goa, gob = ba_ref[...], bx_ref[...], goa_ref[...], gob_ref[...]
        sp8 = LRU_C * _softplus(-lam_ref[...])

        def phase3(g, carry):
            rows = _rows(g)
            ug, _ = _gelu(z_ref[rows, 0:D_BR])
            ga = z_ref[rows, 2 * D_BR:3 * D_BR]
            ya = ug * (mixed_s[rows, :] + bias_ref[rows, :]) * (ga * _sig(ga))
            y_s[rows, 0:D_BR] = ya * lax.rsqrt(_mean_last(ya * ya) + EPS) * goa

            first_row = jnp.logical_and(jnp.logical_and(c_id == 0, g == 0), rid == 0)
            _, i, a, mult = _lru_gates(pre_s[rows, 0:D_BR], pre_s[rows, D_BR:2 * D_BR], ba, bx, sp8, first_row)
            b = mult * i * xc_s[rows, :]
            for d in (1, 2, 4):
                a_sh = jnp.where(rid >= d, pltpu.roll(a, d, 0), 1.0)
                b_sh = jnp.where(rid >= d, pltpu.roll(b, d, 0), 0.0)
                b = a * b_sh + b
                a = a * a_sh
            hh = b + a * carry
            h_ref[rows, :] = hh
            gb = z_ref[rows, 4 * D_BR:5 * D_BR]
            yb = hh * (gb * _sig(gb))
            y_s[rows, D_BR:2 * D_BR] = yb * lax.rsqrt(_mean_last(yb * yb) + EPS) * gob
            return _bcast_row(hh, ROWS - 1)

        carry_s[...] = _loop(N_GROUP, phase3, carry_s[...])
        y_ref[...] = y_s[...].astype(BF16)

    vec = pl.BlockSpec((1, D_BR), lambda i: (0, 0))
    return pl.pallas_call(
        body, name="mix_fwd", grid=(n_chunk,),
        in_specs=[pl.BlockSpec((CHUNK, D_IN), lambda i: (i, 0)), vec, vec,
                  pl.BlockSpec((N_HEAD, HEAD, HEAD), lambda i: (0, 0, 0)),
                  pl.BlockSpec((CHUNK, D_BR), lambda i: (0, 0)),
                  pl.BlockSpec((ROWS, D_BR), lambda i: (0, 0)), vec,
                  pl.BlockSpec((N_HEAD, HEAD, 2 * HEAD), lambda i: (0, 0, 0)), vec, vec, vec, vec, vec],
        out_specs=[pl.BlockSpec((CHUNK, 2 * D_BR), lambda i: (i, 0)), pl.BlockSpec((CHUNK, D_BR), lambda i: (i, 0))],
        out_shape=[SDS((t_len, 2 * D_BR), BF16), SDS((t_len, D_BR), F32)],
        scratch_shapes=[pltpu.VMEM((CHUNK, D_BR), F32), pltpu.VMEM((CHUNK, D_BR), F32), pltpu.VMEM((CHUNK, D_BR), F32),
                        pltpu.VMEM((CHUNK, 2 * D_BR), F32), pltpu.VMEM((CHUNK, 2 * D_BR), F32),
                        pltpu.VMEM((ROWS, D_BR), F32), pltpu.VMEM((ROWS, D_BR), F32)],
        compiler_params=_params(("arbitrary",), 32),
    )(z, ln_g, ln_b, wm, bias, cw, cb, wax, ba, bx, lam, goa, gob)


def _load_weight(w_hbm, w_vmem, sem):
    @pl.when(pl.program_id(0) == 0)
    def _():
        cp = pltpu.make_async_copy(w_hbm, w_vmem, sem)
        cp.start()
        cp.wait()


def _out_proj(y, x, w_out, post_g, tm=512):
    t_len = y.shape[0]

    def body(y_ref, x_ref, w_hbm, g_ref, h1_ref, ob_ref, w_s, o_s, sem):
        _load_weight(w_hbm, w_s, sem)
        o_s[...] = _dot(y_ref[...], w_s[...])
        g = g_ref[...]

        def rows_body(q, _):
            rows = _tile_rows(q)
            o = o_s[rows, :]
            h1_ref[rows, :] = x_ref[rows, :] + o * lax.rsqrt(_mean_last(o * o) + EPS) * g
            ob_ref[rows, :] = o.astype(BF16)
            return 0

        _loop(tm // TILE_ROWS, rows_body, 0)

    tile = pl.BlockSpec((tm, D_MODEL), lambda i: (i, 0))
    return pl.pallas_call(
        body, name="out_proj", grid=(t_len // tm,),
        in_specs=[tile, tile, pl.BlockSpec(memory_space=pl.ANY), pl.BlockSpec((1, D_MODEL), lambda i: (0, 0))],
        out_specs=[tile, tile],
        out_shape=[SDS((t_len, D_MODEL), F32), SDS((t_len, D_MODEL), BF16)],
        scratch_shapes=[pltpu.VMEM((D_MODEL, D_MODEL), BF16), pltpu.VMEM((tm, D_MODEL), F32), pltpu.SemaphoreType.DMA],
        compiler_params=_params(("arbitrary",), 44),
    )(y, x, w_out, post_g)


def _ple_loss(h1, p, tgt, w_pg, w_pe_g, tm=256):
    t_len = h1.shape[0]
    n_tile = t_len // tm
    pe_shard = D_MODEL // N_DEV

    def body(h1_ref, p_ref, t_ref, w_hbm, wpe_ref, dh2_ref, dgl_ref, h1b_ref, loss_ref, dwpe_ref, w_s, pe_s, gl_s, acc_s,
             dpe_s, gpe_s, sem):
        _load_weight(w_hbm, w_s, sem)
        i = pl.program_id(0)

        @pl.when(i == 0)
        def _():
            acc_s[...] = jnp.zeros_like(acc_s)
            gpe_s[...] = jnp.zeros_like(gpe_s)

        h1b_ref[...] = h1_ref[...].astype(BF16)
        pb = p_ref[...].astype(BF16)
        for j in range(N_DEV):
            pe_s[:, j * pe_shard:(j + 1) * pe_shard] = _dot(pb, wpe_ref[j])
        gl_s[...] = _dot(h1b_ref[...], w_s[...])

        def rows_body(q, acc):
            rows = _tile_rows(q)
            pe = pe_s[rows, :]
            g = _sig(gl_s[rows, :])
            e = h1_ref[rows, :] + pe * g - t_ref[rows, :]
            dh2 = e * (1.0 / D_MODEL)
            dh2_ref[rows, :] = dh2
            dpe_s[rows, :] = (dh2 * g).astype(BF16)
            dgl_ref[rows, :] = (dh2 * pe * g * (1.0 - g)).astype(BF16)
            return acc + _fold_rows(e * e)

        acc_s[...] = _loop(tm // TILE_ROWS, rows_body, acc_s[...])
        gpe_s[...] += _dot_tn(pb, dpe_s[...])

        @pl.when(i == n_tile - 1)
        def _():
            loss_ref[...] = jnp.full(loss_ref.shape, 0.5 / D_MODEL * jnp.sum(acc_s[...]), F32)
            for j in range(N_DEV):
                dwpe_ref[j] = gpe_s[:, j * pe_shard:(j + 1) * pe_shard].astype(BF16)

    tile = pl.BlockSpec((tm, D_MODEL), lambda i: (i, 0))
    pe_blocks = pl.BlockSpec((N_DEV, D_PLE, pe_shard), lambda i: (0, 0, 0))
    return pl.pallas_call(
        body, name="ple_loss", grid=(n_tile,),
        in_specs=[tile, pl.BlockSpec((tm, D_PLE), lambda i: (i, 0)), tile, pl.BlockSpec(memory_space=pl.ANY), pe_blocks],
        out_specs=[tile, tile, tile, pl.BlockSpec((ROWS, HEAD), lambda i: (0, 0)), pe_blocks],
        out_shape=[SDS((t_len, D_MODEL), F32), SDS((t_len, D_MODEL), BF16), SDS((t_len, D_MODEL), BF16),
                   SDS((ROWS, HEAD), F32), SDS((N_DEV, D_PLE, pe_shard), BF16)],
        scratch_shapes=[pltpu.VMEM((D_MODEL, D_MODEL), BF16), pltpu.VMEM((tm, D_MODEL), F32),
                        pltpu.VMEM((tm, D_MODEL), F32), pltpu.VMEM((ROWS, D_MODEL), F32), pltpu.VMEM((tm, D_MODEL), BF16),
                        pltpu.VMEM((D_PLE, D_MODEL), F32), pltpu.SemaphoreType.DMA],
        compiler_params=_params(("arbitrary",), 48),
    )(h1, p, tgt, w_pg, w_pe_g)


def _tail_bwd(dh2, dgl, ob, w_pg, w_out, post_g, tm=256):
    t_len = dh2.shape[0]
    n_tile = t_len // tm

    def body(dh2_ref, dgl_ref, ob_ref, wpg_hbm, wout_hbm, g_ref, dh1_ref, do_ref, dy_ref, dg_ref, wpg_s, wout_s, t_s,
             acc_s, sems):
        _load_weight(wpg_hbm, wpg_s, sems.at[0])
        _load_weight(wout_hbm, wout_s, sems.at[1])
        i = pl.program_id(0)

        @pl.when(i == 0)
        def _():
            acc_s[...] = jnp.zeros_like(acc_s)

        t_s[...] = _dot_nt(dgl_ref[...], wpg_s[...])
        g = g_ref[...]

        def rows_body(q, acc):
            rows = _tile_rows(q)
            dh1 = dh2_ref[rows, :] + t_s[rows, :]
            dh1_ref[rows, :] = dh1
            o = ob_ref[rows, :].astype(F32)
            rr = lax.rsqrt(_mean_last(o * o) + EPS)
            on = o * rr
            dog = dh1 * g
            do_ref[rows, :] = (rr * (dog - on * _mean_last(dog * on))).astype(BF16)
            return acc + _fold_rows(dh1 * on)

        acc_s[...] = _loop(tm // TILE_ROWS, rows_body, acc_s[...])
        dy_ref[...] = _dot_nt(do_ref[...], wout_s[...]).astype(BF16)

        @pl.when(i == n_tile - 1)
        def _():
            dg_ref[...] = jnp.sum(acc_s[...], axis=0, keepdims=True)

    tile = pl.BlockSpec((tm, D_MODEL), lambda i: (i, 0))
    vec = pl.BlockSpec((1, D_MODEL), lambda i: (0, 0))
    hbm = pl.BlockSpec(memory_space=pl.ANY)
    return pl.pallas_call(
        body, name="tail_bwd", grid=(n_tile,),
        in_specs=[tile, tile, tile, hbm, hbm, vec],
        out_specs=[tile, tile, tile, vec],
        out_shape=[SDS((t_len, D_MODEL), F32), SDS((t_len, D_MODEL), BF16), SDS((t_len, D_MODEL), BF16),
                   SDS((1, D_MODEL), F32)],
        scratch_shapes=[pltpu.VMEM((D_MODEL, D_MODEL), BF16), pltpu.VMEM((D_MODEL, D_MODEL), BF16),
                        pltpu.VMEM((tm, D_MODEL), F32), pltpu.VMEM((ROWS, D_MODEL), F32), pltpu.SemaphoreType.DMA((2,))],
        compiler_params=_params(("arbitrary",), 48),
    )(dh2, dgl, ob, w_pg, w_out, post_g)


def _mix_bwd(z, dy, h, ln_g, ln_b, wm, wm_t, bias, cw, cb, wax, wax_t, ba, bx, lam, goa, gob, ex_arrs, ex_scatter):
    t_len = z.shape[0]
    n_chunk = t_len // CHUNK
    halo_blocks = CHUNK // ROWS
    ex = _Exchange(ex_arrs, ex_scatter)
    n_in, n_out, n_scratch = 19, 5, 17

    def body(*refs):
        (z_ref, zhalo_ref, dy_ref, h_ref, hhalo_ref, lng_ref, lnb_ref, wm_ref, wmt_ref, bias_ref, cw_ref, cb_ref,
         wax_ref, waxt_ref, ba_ref, bx_ref, lam_ref, goa_ref, gob_ref) = refs[:n_in]
        ex_in = refs[n_in:n_in + ex.n]
        dz_ref, vecs_ref, dws_ref, dwax_ref, dbs_ref = refs[n_in + ex.n:n_in + ex.n + n_out]
        ex_out = refs[n_in + ex.n + n_out:n_in + 2 * ex.n + n_out]
        (vn_s, vh_s, rs_s, xc_s, mixed_s, pre_s, dmix_s, dvn_s, dho_s, dxc_s, dpre_s, dz_s, acc_s, accdm_s,
         cg_s, ca_s, dxchalo_s) = refs[n_in + 2 * ex.n + n_out:n_in + 2 * ex.n + n_out + n_scratch]
        ex_sems = refs[n_in + 2 * ex.n + n_out + n_scratch:]
        step = pl.program_id(0)
        c_id = n_chunk - 1 - step
        rid = _row_ids(D_BR)
        first_chunk = c_id == 0

        @pl.when(step == 0)
        def _():
            ex.start(ex_in, ex_out, ex_sems)
            acc_s[...] = jnp.zeros_like(acc_s)
            accdm_s[...] = jnp.zeros_like(accdm_s)
            cg_s[...] = jnp.zeros_like(cg_s)
            ca_s[...] = jnp.zeros_like(ca_s)
            dxchalo_s[...] = jnp.zeros_like(dxchalo_s)
            dws_ref[...] = jnp.zeros_like(dws_ref)
            dwax_ref[...] = jnp.zeros_like(dwax_ref)

        lng, lnb, cb = lng_ref[...], lnb_ref[...], cb_ref[...]
        xb_halo = jnp.where(first_chunk, 0.0, zhalo_ref[...])
        h_halo = jnp.where(first_chunk, 0.0, hhalo_ref[...])

        def prev_rows(ref, cols, g, halo):
            before = ref[pl.ds(pl.multiple_of(jnp.maximum(g - 1, 0) * ROWS, ROWS), ROWS), cols]
            return jnp.where(g > 0, before, halo)

        def phase1(g, prev):
            rows = _rows(g)
            vg, _ = _gelu(z_ref[rows, D_BR:2 * D_BR])
            xm = vg - _mean_last(vg)
            rs = lax.rsqrt(_mean_last(xm * xm) + EPS)
            vh = xm * rs
            vh_s[rows, :] = vh
            rs_s[rows, :] = jnp.broadcast_to(rs, (ROWS, HEAD))
            vn_s[rows, :] = vh * lng + lnb
            xb = z_ref[rows, 3 * D_BR:4 * D_BR]
            xc_s[rows, :] = _conv_rows(xb, prev, cw_ref, cb, rid)
            return xb

        _loop(N_GROUP, phase1, xb_halo)

        for hd in range(N_HEAD):
            cs = slice(hd * HEAD, (hd + 1) * HEAD)
            mixed_s[:, cs] = _dot(wm_ref[hd], vn_s[:, cs].astype(BF16))
            pre = _dot(xc_s[:, cs].astype(BF16), wax_ref[hd])
            pre_s[:, cs] = pre[:, :HEAD]
            pre_s[:, D_BR + hd * HEAD:D_BR + (hd + 1) * HEAD] = pre[:, HEAD:]

        goa, gob = goa_ref[...], gob_ref[...]

        def phase3(g, _):
            rows = _rows(g)
            u = z_ref[rows, 0:D_BR]
            ug, tu = _gelu(u)
            ga = z_ref[rows, 2 * D_BR:3 * D_BR]
            sga = _sig(ga)
            sa = ga * sga
            mixed = mixed_s[rows, :] + bias_ref[rows, :]
            ya0 = ug * mixed
            ya = ya0 * sa
            ra = lax.rsqrt(_mean_last(ya * ya) + EPS)
            dyan = dy_ref[rows, 0:D_BR].astype(F32)
            acc_s[V_GOUT_A] += dyan * ya * ra
            dyg = dyan * goa
            dya = ra * dyg - ya * (ra * ra * ra) * _mean_last(dyg * ya)
            dya0 = dya * sa
            dz_s[rows, 2 * D_BR:3 * D_BR] = dya * ya0 * (sga * (1.0 + ga * (1.0 - sga)))
            dmix = dya0 * ug
            dmix_s[rows, :] = dmix
            accdm_s[rows, :] += dmix
            dz_s[rows, 0:D_BR] = dya0 * mixed * _gelu_grad(u, tu)

            hh = h_ref[rows, :]
            gb = z_ref[rows, 4 * D_BR:5 * D_BR]
            sgb = _sig(gb)
            sb = gb * sgb
            yb = hh * sb
            rb = lax.rsqrt(_mean_last(yb * yb) + EPS)
            dybn = dy_ref[rows, D_BR:2 * D_BR].astype(F32)
            acc_s[V_GOUT_B] += dybn * yb * rb
            dyg = dybn * gob
            dyb = rb * dyg - yb * (rb * rb * rb) * _mean_last(dyg * yb)
            dho_s[rows, :] = dyb * sb
            dz_s[rows, 4 * D_BR:5 * D_BR] = dyb * hh * (sgb * (1.0 + gb * (1.0 - sgb)))
            return 0

        _loop(N_GROUP, phase3, 0)

        for hd in range(N_HEAD):
            cs = slice(hd * HEAD, (hd + 1) * HEAD)
            dmb = dmix_s[:, cs].astype(BF16)
            dvn_s[:, cs] = _dot(wmt_ref[hd], dmb)
            dws_ref[hd] += _dot_nt(dmb, vn_s[:, cs].astype(BF16))

        def phase5(g, _):
            rows = _rows(g)
            dvn = dvn_s[rows, :]
            vh = vh_s[rows, :]
            acc_s[V_LN_G] += dvn * vh
            acc_s[V_LN_B] += dvn
            dvh = dvn * lng
            rs = rs_s[rows, 0:1]
            dvg = rs * (dvh - _mean_last(dvh) - vh * _mean_last(dvh * vh))
            v = z_ref[rows, D_BR:2 * D_BR]
            _, tv = _gelu(v)
            dz_s[rows, D_BR:2 * D_BR] = dvg * _gelu_grad(v, tv)
            return 0

        _loop(N_GROUP, phase5, 0)

        ba, bx = ba_ref[...], bx_ref[...]
        sp8 = LRU_C * _softplus(-lam_ref[...])

        def phase6(k, carry):
            cg, ca = carry
            g = N_GROUP - 1 - k
            rows = _rows(g)
            first_row = jnp.logical_and(jnp.logical_and(first_chunk, g == 0), rid == 0)
            r, i, a, mult = _lru_gates(pre_s[rows, 0:D_BR], pre_s[rows, D_BR:2 * D_BR], ba, bx, sp8, first_row)
            a_nx = jnp.where(rid < ROWS - 1, pltpu.roll(a, ROWS - 1, 0), ca)
            aa, bb = a_nx, dho_s[rows, :]
            for d in (1, 2, 4):
                a_sh = jnp.where(rid < ROWS - d, pltpu.roll(aa, ROWS - d, 0), 1.0)
                b_sh = jnp.where(rid < ROWS - d, pltpu.roll(bb, ROWS - d, 0), 0.0)
                bb = aa * b_sh + bb
                aa = aa * a_sh
            gg = bb + aa * cg
            hh = h_ref[rows, :]
            hprev = _shift_down(hh, prev_rows(h_ref, slice(None), g, h_halo), 1, rid)
            xc = xc_s[rows, :]
            gx = gg * xc
            dla = gg * hprev * a - jnp.where(first_row, 0.0, gx * i * (a * a) * lax.rsqrt(mult * mult))
            acc_s[V_LAM] += -(dla * r)
            dpa = -(dla * sp8) * r * (1.0 - r)
            dpx = gx * mult * i * (1.0 - i)
            acc_s[V_B_A] += dpa
            acc_s[V_B_X] += dpx
            dpre_s[rows, 0:D_BR] = dpa
            dpre_s[rows, D_BR:2 * D_BR] = dpx
            dxc_s[rows, :] = gg * mult * i
            return _bcast_row(gg, 0), _bcast_row(a, 0)

        cg, ca = _loop(N_GROUP, phase6, (cg_s[...], ca_s[...]))
        cg_s[...] = cg
        ca_s[...] = ca

        for hd in range(N_HEAD):
            cs = slice(hd * HEAD, (hd + 1) * HEAD)
            dpre = jnp.concatenate([dpre_s[:, cs], dpre_s[:, D_BR + hd * HEAD:D_BR + (hd + 1) * HEAD]], axis=1).astype(BF16)
            dxc_s[:, cs] += _dot(dpre, waxt_ref[hd])
            dwax_ref[hd] += _dot_tn(xc_s[:, cs].astype(BF16), dpre)

        def phase8(k, nxt):
            g = N_GROUP - 1 - k
            rows = _rows(g)
            dxc = dxc_s[rows, :]
            acc_s[V_CONV_B] += dxc
            xb = z_ref[rows, 3 * D_BR:4 * D_BR]
            xb_prev = prev_rows(z_ref, slice(3 * D_BR, 4 * D_BR), g, xb_halo)
            dxb = cw_ref[3:4, :] * dxc
            acc_s[V_CONV_W + 3] += dxc * xb
            for j in range(1, CONV_W):
                dxb = dxb + cw_ref[3 - j:4 - j, :] * _shift_up(dxc, nxt, j, rid)
                acc_s[V_CONV_W + 3 - j] += dxc * _shift_down(xb, xb_prev, j, rid)
            dz_s[rows, 3 * D_BR:4 * D_BR] = dxb
            return dxc

        dxchalo_s[...] = _loop(N_GROUP, phase8, dxchalo_s[...])
        dz_ref[...] = dz_s[...].astype(BF16)

        @pl.when(step == n_chunk - 1)
        def _():
            for v in range(N_VEC):
                vecs_ref[v:v + 1, :] = jnp.sum(acc_s[v], axis=0, keepdims=True)
            lam = lam_ref[...]
            vecs_ref[V_LAM:V_LAM + 1, :] = vecs_ref[V_LAM:V_LAM + 1, :] * (-LRU_C * _sig(-lam))
            tril = (lax.broadcasted_iota(jnp.int32, (HEAD, HEAD), 0) >= lax.broadcasted_iota(jnp.int32, (HEAD, HEAD), 1))
            ones = jnp.ones((ROWS, HEAD), BF16)
            for hd in range(N_HEAD):
                cs = slice(hd * HEAD, (hd + 1) * HEAD)
                dws_ref[hd] = jnp.where(tril, dws_ref[hd], 0.0)
                blk = accdm_s[:, cs]
                hi = blk.astype(BF16)
                lo = (blk - hi.astype(F32)).astype(BF16)
                dbs_ref[hd:hd + 1, :] = (_dot_nt(ones, hi) + _dot_nt(ones, lo))[0:1, :]
            ex.wait(ex_in, ex_out, ex_sems)

    vec = pl.BlockSpec((1, D_BR), lambda i: (0, 0))
    rev = lambda i: (n_chunk - 1 - i, 0)
    halo = lambda col: (lambda i: (jnp.maximum((n_chunk - 1 - i) * halo_blocks - 1, 0), col))
    full3 = lambda a, b, c: pl.BlockSpec((a, b, c), lambda i: (0, 0, 0))
    big = lambda w: pltpu.VMEM((CHUNK, w), F32)
    res = pl.pallas_call(
        body, name="mix_bwd", grid=(n_chunk,),
        in_specs=[pl.BlockSpec((CHUNK, D_IN), rev), pl.BlockSpec((ROWS, D_BR), halo(3)),
                  pl.BlockSpec((CHUNK, 2 * D_BR), rev), pl.BlockSpec((CHUNK, D_BR), rev),
                  pl.BlockSpec((ROWS, D_BR), halo(0)), vec, vec,
                  full3(N_HEAD, HEAD, HEAD), full3(N_HEAD, HEAD, HEAD),
                  pl.BlockSpec((CHUNK, D_BR), lambda i: (0, 0)), pl.BlockSpec((ROWS, D_BR), lambda i: (0, 0)), vec,
                  full3(N_HEAD, HEAD, 2 * HEAD), full3(N_HEAD, 2 * HEAD, HEAD), vec, vec, vec, vec, vec]
        + [ANY_SPEC] * ex.n,
        out_specs=[pl.BlockSpec((CHUNK, D_IN), rev), pl.BlockSpec((N_VEC, D_BR), lambda i: (0, 0)),
                   full3(N_HEAD, HEAD, HEAD), full3(N_HEAD, HEAD, 2 * HEAD),
                   pl.BlockSpec((N_HEAD, HEAD), lambda i: (0, 0))] + [ANY_SPEC] * ex.n,
        out_shape=[SDS((t_len, D_IN), BF16), SDS((N_VEC, D_BR), F32), SDS((N_HEAD, HEAD, HEAD), F32),
                   SDS((N_HEAD, HEAD, 2 * HEAD), F32), SDS((N_HEAD, HEAD), F32)] + ex.out_shape,
        scratch_shapes=[big(D_BR), big(D_BR), big(HEAD), big(D_BR), big(D_BR), big(2 * D_BR), big(D_BR), big(D_BR),
                        big(D_BR), big(D_BR), big(2 * D_BR), big(D_IN),
                        pltpu.VMEM((N_VEC, ROWS, D_BR), F32), big(D_BR),
                        pltpu.VMEM((ROWS, D_BR), F32), pltpu.VMEM((ROWS, D_BR), F32), pltpu.VMEM((ROWS, D_BR), F32)]
        + ex.scratch,
        compiler_params=_params(("arbitrary",), 48),
    )(z, z, dy, h, h, ln_g, ln_b, wm, wm_t, bias, cw, cb, wax, wax_t, ba, bx, lam, goa, gob, *ex_arrs)
    return res[:n_out], res[n_out:]


def _in_bwd(dz, w_in_g, x, dh1, pre_g, first_tile, n_tile, prev, name, ex_arrs=(), ex_scatter=(), tm=256):
    t_len = x.shape[0]
    ex = _Exchange(ex_arrs, ex_scatter)
    n_prev = 0 if prev is None else 2

    def body(dz_ref, w_hbm, x_ref, dh1_ref, g_ref, *refs):
        prev_refs, refs = refs[:n_prev], refs[n_prev:]
        ex_in, (gx_ref, dg_ref), ex_out = refs[:ex.n], refs[ex.n:ex.n + 2], refs[ex.n + 2:2 * ex.n + 2]
        w_s, t_s, dg_s, w_sems = refs[2 * ex.n + 2:2 * ex.n + 6]
        ex_sems = refs[2 * ex.n + 6:]
        i = pl.program_id(0)

        @pl.when(i == 0)
        def _():
            if ex.n:
                ex.start(ex_in, ex_out, ex_sems)
            loads = [pltpu.make_async_copy(w_hbm.at[s], w_s.at[:, s * W_IN_SHARD:(s + 1) * W_IN_SHARD], w_sems.at[s])
                     for s in range(N_DEV)]
            for cp in loads:
                cp.start()
            dg_s[...] = jnp.zeros_like(dg_s)
            for cp in loads:
                cp.wait()

        t_s[...] = _dot_nt(dz_ref[...], w_s[...])
        g = g_ref[...]

        def rows_body(q, acc):
            rows = _tile_rows(q)
            xv = x_ref[rows, :]
            r = lax.rsqrt(_mean_last(xv * xv) + EPS)
            xh = xv * r
            dhn = t_s[rows, :]
            dg = dhn * g
            gx_ref[rows, :] = dh1_ref[rows, :] + r * (dg - xh * _mean_last(dg * xh))
            return acc + _fold_rows(dhn * xh)

        dg_s[...] = _loop(tm // TILE_ROWS, rows_body, dg_s[...])

        @pl.when(i == n_tile - 1)
        def _():
            dg = jnp.sum(dg_s[...], axis=0, keepdims=True)
            dg_ref[...] = dg + prev_refs[1][...] if n_prev else dg
            if ex.n:
                ex.wait(ex_in, ex_out, ex_sems)

    tile = pl.BlockSpec((tm, D_MODEL), lambda i: (first_tile + i, 0))
    vec = pl.BlockSpec((1, D_MODEL), lambda i: (0, 0))
    prev_specs = [ANY_SPEC, vec] if n_prev else []
    res = pl.pallas_call(
        body, name=name, grid=(n_tile,),
        in_specs=[pl.BlockSpec((tm, D_IN), lambda i: (first_tile + i, 0)), ANY_SPEC, tile, tile, vec] + prev_specs
        + [ANY_SPEC] * ex.n,
        out_specs=[tile, vec] + [ANY_SPEC] * ex.n,
        out_shape=[SDS((t_len, D_MODEL), F32), SDS((1, D_MODEL), F32)] + ex.out_shape,
        scratch_shapes=[pltpu.VMEM((D_MODEL, D_IN), BF16), pltpu.VMEM((tm, D_MODEL), F32), pltpu.VMEM((ROWS, D_MODEL), F32),
                        pltpu.SemaphoreType.DMA((N_DEV,))] + (ex.scratch if ex.n else []),
        input_output_aliases={5: 0} if n_prev else {},
        compiler_params=_params(("arbitrary",), 54),
    )(dz, w_in_g, x, dh1, pre_g, *(prev or ()), *ex_arrs)
    return res[0], res[1], res[2:]


def _grad_w(a, b, bn, shard_major, name, tk=1024, ex_arrs=(), ex_scatter=()):
    t_len, m = a.shape
    n = b.shape[1]
    n_j, n_k = n // bn, t_len // tk
    ex = _Exchange(ex_arrs, ex_scatter)

    def body(a_ref, b_ref, *refs):
        ex_in, o_ref, ex_out = refs[:ex.n], refs[ex.n], refs[ex.n + 1:2 * ex.n + 1]
        acc_s, ex_sems = refs[2 * ex.n + 1], refs[2 * ex.n + 2:]
        j, k = pl.program_id(0), pl.program_id(1)
        if ex.n:
            @pl.when(jnp.logical_and(j == 0, k == 0))
            def _():
                ex.start(ex_in, ex_out, ex_sems)

        @pl.when(k == 0)
        def _():
            acc_s[...] = jnp.zeros_like(acc_s)

        acc_s[...] += _dot_tn(a_ref[...], b_ref[...])

        @pl.when(k == n_k - 1)
        def _():
            o_ref[...] = acc_s[...].astype(BF16)

        if ex.n:
            @pl.when(jnp.logical_and(j == n_j - 1, k == n_k - 1))
            def _():
                ex.wait(ex_in, ex_out, ex_sems)

    if shard_major:
        out_spec, out_shape = pl.BlockSpec((None, m, bn), lambda j, k: (j, 0, 0)), SDS((n_j, m, bn), BF16)
    else:
        out_spec, out_shape = pl.BlockSpec((m, bn), lambda j, k: (0, j)), SDS((m, n), BF16)
    res = pl.pallas_call(
        body, name=name, grid=(n_j, n_k),
        in_specs=[pl.BlockSpec((tk, m), lambda j, k: (k, 0)), pl.BlockSpec((tk, bn), lambda j, k: (k, j))]
        + [ANY_SPEC] * ex.n,
        out_specs=[out_spec] + [ANY_SPEC] * ex.n, out_shape=[out_shape] + ex.out_shape,
        scratch_shapes=[pltpu.VMEM((m, bn), F32)] + (ex.scratch if ex.n else []),
        compiler_params=_params(("arbitrary", "arbitrary"), 40),
    )(a, b, *ex_arrs)
    return res[0], res[1:]


RS_ORDER = (2, 4, 6, 3, 5, 7, 1, 0)


def _grad_w_in(hn, dz, ex_arrs, ex_scatter, tk=1024):
    t_len = hn.shape[0]
    n_k = t_len // tk
    ex = _Exchange(ex_arrs, ex_scatter)
    me_out = 4 * lax.axis_index("x") + 2 * lax.axis_index("y") + lax.axis_index("c")
    order = jnp.stack([me_out ^ k for k in RS_ORDER]).astype(jnp.int32)
    n_stage = 2

    def body(order_ref, a_ref, b_ref, *refs):
        ex_in, parts_hbm, ex_out = refs[:ex.n], refs[ex.n], refs[ex.n + 1:2 * ex.n + 1]
        acc_s, stage_s, send_sems, recv_sems, loc_sem = refs[2 * ex.n + 1:2 * ex.n + 6]
        ex_sems = refs[2 * ex.n + 6:]
        j, k = pl.program_id(0), pl.program_id(1)
        x, y, c, me = _mesh_place()

        def send(jj):
            mask, src = RS_ORDER[jj], stage_s.at[jj % n_stage]
            if mask == 0:
                return pltpu.make_async_copy(src, parts_hbm.at[me], loc_sem.at[0])
            dev, _ = _peer(x, y, c, mask)
            return _remote(src, parts_hbm.at[me], send_sems.at[mask], recv_sems.at[mask], dev)

        @pl.when(jnp.logical_and(j == 0, k == 0))
        def _():
            ex.start(ex_in, ex_out, ex_sems)

        @pl.when(k == 0)
        def _():
            acc_s[...] = jnp.zeros_like(acc_s)

        acc_s[...] += _dot_tn(a_ref[...], b_ref[...])

        for jj in range(N_DEV):
            @pl.when(jnp.logical_and(j == jj, k == n_k - 1))
            def _(jj=jj):
                if jj >= n_stage:
                    send(jj - n_stage).wait_send()
                stage_s[jj % n_stage] = acc_s[...].astype(BF16)
                send(jj).start()

        @pl.when(jnp.logical_and(j == N_DEV - 1, k == n_k - 1))
        def _():
            for jj in range(N_DEV - n_stage, N_DEV):
                cp = send(jj)
                cp.wait() if RS_ORDER[jj] == 0 else cp.wait_send()
            for mask in range(1, N_DEV):
                dev, lin = _peer(x, y, c, mask)
                _remote(stage_s.at[0], parts_hbm.at[lin], send_sems.at[mask], recv_sems.at[mask], dev).wait_recv()
            ex.wait(ex_in, ex_out, ex_sems)

    dma = lambda n: pltpu.SemaphoreType.DMA((n,))
    grid_spec = pltpu.PrefetchScalarGridSpec(
        num_scalar_prefetch=1, grid=(N_DEV, n_k),
        in_specs=[pl.BlockSpec((tk, D_MODEL), lambda j, k, order: (k, 0)),
                  pl.BlockSpec((tk, W_IN_SHARD), lambda j, k, order: (k, order[j]))] + [ANY_SPEC] * ex.n,
        out_specs=[ANY_SPEC] * (1 + ex.n),
        scratch_shapes=[pltpu.VMEM((D_MODEL, W_IN_SHARD), F32), pltpu.VMEM((n_stage, D_MODEL, W_IN_SHARD), BF16),
                        dma(N_DEV), dma(N_DEV), dma(1)] + ex.scratch)
    res = pl.pallas_call(
        body, name="grad_w_in", grid_spec=grid_spec,
        out_shape=[SDS((N_DEV, D_MODEL, W_IN_SHARD), BF16)] + ex.out_shape,
        compiler_params=_params(("arbitrary", "arbitrary"), 40),
    )(order, hn, dz, *ex_arrs)
    return res[0], res[1:]


def _sum_parts(parts, name):
    def body(p_ref, o_ref):
        g = p_ref[0].astype(F32)
        for s in range(1, parts.shape[0]):
            g = g + p_ref[s].astype(F32)
        o_ref[...] = g

    return pl.pallas_call(body, name=name, out_shape=SDS(parts.shape[1:], F32))(parts)


def _adamw(parts, w, m, v, name, tr):
    rows, cols = w.shape
    n_parts = parts.shape[0]
    c1 = 1.0 - ADAM_B1 ** ADAM_STEP
    c2 = 1.0 - ADAM_B2 ** ADAM_STEP

    def body(p_ref, w_ref, m_ref, v_ref, g_ref, d_ref, nm_ref, nv_ref):
        g = p_ref[0].astype(F32)
        for s in range(1, n_parts):
            g = g + p_ref[s].astype(F32)
        g_ref[...] = g
        nm = ADAM_B1 * m_ref[...] + (1.0 - ADAM_B1) * g
        nv = ADAM_B2 * v_ref[...] + (1.0 - ADAM_B2) * (g * g)
        nm_ref[...] = nm
        nv_ref[...] = nv
        d_ref[...] = -ADAM_LR * ((nm / c1) / (jnp.sqrt(nv / c2) + ADAM_EPS) + ADAM_WD * w_ref[...])

    tile = pl.BlockSpec((tr, cols), lambda i: (i, 0))
    return pl.pallas_call(
        body, name=name, grid=(rows // tr,),
        in_specs=[pl.BlockSpec((n_parts, tr, cols), lambda i: (0, i, 0)), tile, tile, tile],
        out_specs=[tile] * 4, out_shape=[SDS((rows, cols), F32)] * 4,
        compiler_params=_params(("arbitrary",), 40),
    )(parts, w, m, v)


PACKED = ("gmlp_ln_g", "gmlp_ln_b", "gmlp_ws", "gmlp_bs", "conv_b", "w_a", "b_a", "w_x", "b_x", "lam", "gmlp_out_g",
          "lru_out_g", "post_g")
WEIGHTS = ("pre_g", "w_in", "gmlp_ln_g", "gmlp_ln_b", "gmlp_ws", "gmlp_bs", "conv_w", "conv_b", "w_a", "b_a", "w_x",
           "b_x", "lam", "gmlp_out_g", "lru_out_g", "w_out", "post_g", "w_pe", "w_pg")
LANES = 128


PACK_ROWS = 3200
PACK_TILE = 640
IN_BWD_TILE = 256


def _pack(parts):
    rows = [p.reshape(-1, LANES) for p in parts]
    used = sum(r.shape[0] for r in rows)
    return jnp.concatenate(rows + [jnp.zeros((PACK_ROWS - used, LANES), F32)], axis=0)


def _pad_rows(a, rows):
    return jnp.concatenate([a, jnp.zeros((rows - a.shape[0],) + a.shape[1:], a.dtype)], axis=0)


def kernel(x, p, pre_g, w_in, gmlp_ln_g, gmlp_ln_b, gmlp_ws, gmlp_bs, conv_w, conv_b, w_a, b_a, w_x, b_x, lam, gmlp_out_g, lru_out_g, w_out, post_g, w_pe, w_pg, loss_target, m_pre_g, m_w_in, m_gmlp_ln_g, m_gmlp_ln_b, m_gmlp_ws, m_gmlp_bs, m_conv_w, m_conv_b, m_w_a, m_b_a, m_w_x, m_b_x, m_lam, m_gmlp_out_g, m_lru_out_g, m_w_out, m_post_g, m_w_pe, m_w_pg, v_pre_g, v_w_in, v_gmlp_ln_g, v_gmlp_ln_b, v_gmlp_ws, v_gmlp_bs, v_conv_w, v_conv_b, v_w_a, v_b_a, v_w_x, v_b_x, v_lam, v_gmlp_out_g, v_lru_out_g, v_w_out, v_post_g, v_w_pe, v_w_pg):
    args = dict(locals())
    weights = {n: args[n] for n in WEIGHTS}
    m_in = {n: args["m_" + n] for n in WEIGHTS}
    v_in = {n: args["v_" + n] for n in WEIGHTS}
    sm = {n: weights[n][0] for n in PACKED}
    shard_rows = D_MODEL // N_DEV
    xs, ps, tgt = x[0], p[0, 0], loss_target[0]

    vec = lambda a: a.reshape(1, -1)
    tril = jnp.tril(jnp.ones((CHUNK, CHUNK), dtype=bool))
    wm32 = jnp.where(tril[None], sm["gmlp_ws"], 0.0)
    wm, wm_t = wm32.astype(BF16), jnp.swapaxes(wm32, 1, 2).astype(BF16)
    bias = jnp.repeat(sm["gmlp_bs"].T, HEAD, axis=1)
    wax32 = jnp.concatenate([sm["w_a"], sm["w_x"]], axis=2)
    wax, wax_t = wax32.astype(BF16), jnp.swapaxes(wax32, 1, 2).astype(BF16)
    ln_g, ln_b = vec(sm["gmlp_ln_g"]), vec(sm["gmlp_ln_b"])
    post_g_v = vec(sm["post_g"])

    hn = _pre_norm(xs, pre_g)
    cw_shard = _pad_rows(conv_w.reshape(CONV_W, HEAD), ROWS)
    z, w_in_g, (w_out_g, w_pe_g, w_pg_g, cw_g) = _in_proj(
        hn, w_in[0].astype(BF16), [w_out[0].astype(BF16), w_pe[0].astype(BF16), w_pg[0].astype(BF16), cw_shard])
    w_out_f, w_pg_f = w_out_g.reshape(D_MODEL, D_MODEL), w_pg_g.reshape(D_MODEL, D_MODEL)
    cw_full = jnp.transpose(cw_g[:, :CONV_W, :], (1, 0, 2)).reshape(CONV_W, D_BR)
    mixer_consts = dict(cw=_pad_rows(cw_full, ROWS), cb=vec(sm["conv_b"]), ba=vec(sm["b_a"]), bx=vec(sm["b_x"]),
                        lam=vec(sm["lam"]), goa=vec(sm["gmlp_out_g"]), gob=vec(sm["lru_out_g"]))
    y, h = _mix_fwd(z, ln_g, ln_b, wm, bias, wax=wax, **mixer_consts)
    h1, ob = _out_proj(y, xs, w_out_f, post_g_v)
    dh2, dgl, h1b, loss_part, d_w_pe = _ple_loss(h1, ps, tgt, w_pg_f, w_pe_g)

    dh1, do, dy, d_post_g = _tail_bwd(dh2, dgl, ob, w_pg_f, w_out_f, post_g_v)
    d_w_out, _ = _grad_w(y, do, 512, False, "grad_w_out")
    d_w_pg, _ = _grad_w(h1b, dgl, 512, False, "grad_w_pg")
    (dz, vecs, d_ws, d_wax, d_bs), (parts_out, parts_pg, parts_pe) = _mix_bwd(
        z, dy, h, ln_g, ln_b, wm, wm_t, bias, wax=wax, wax_t=wax_t, **mixer_consts,
        ex_arrs=[d_w_out.reshape(N_DEV, shard_rows, D_MODEL), d_w_pg.reshape(N_DEV, shard_rows, D_MODEL), d_w_pe],
        ex_scatter=[True, True, True])

    small = {"gmlp_ln_g": vecs[V_LN_G], "gmlp_ln_b": vecs[V_LN_B], "gmlp_ws": d_ws, "gmlp_bs": d_bs,
             "conv_b": vecs[V_CONV_B], "w_a": d_wax[:, :, :HEAD], "b_a": vecs[V_B_A], "w_x": d_wax[:, :, HEAD:],
             "b_x": vecs[V_B_X], "lam": vecs[V_LAM], "gmlp_out_g": vecs[V_GOUT_A], "lru_out_g": vecs[V_GOUT_B],
             "post_g": d_post_g}
    small_part = _pack([small[n] for n in PACKED] + [loss_part]).reshape(N_DEV, PACK_ROWS // N_DEV, LANES)
    d_cw_blocks = jnp.transpose(vecs[V_CONV_W:V_CONV_W + CONV_W].reshape(CONV_W, N_DEV, HEAD), (1, 0, 2))
    d_cw_blocks = jnp.concatenate([d_cw_blocks, jnp.zeros((N_DEV, ROWS - CONV_W, HEAD), F32)], axis=1)
    parts_in, (small_blocks, parts_cw) = _grad_w_in(hn, dz, ex_arrs=[small_part, d_cw_blocks], ex_scatter=[True, True])
    small_sum = _sum_parts(small_blocks, "sum_small")
    grad_x, d_pre_g, _ = _in_bwd(dz, w_in_g, xs, dh1, pre_g, 0, xs.shape[0] // IN_BWD_TILE, None, "in_bwd",
                                 tm=IN_BWD_TILE)
    pre_rows = D_MODEL // LANES
    small_all, parts_pre = _exchange([small_sum, d_pre_g.reshape(pre_rows, LANES)], False, "gather_small_grads")
    parts_small = small_all.reshape(1, PACK_ROWS, LANES)

    pad_cw = lambda a: _pad_rows(a.reshape(CONV_W, HEAD), ROWS)
    flat = lambda a: a.reshape(pre_rows, LANES)
    outs = {
        "w_in": _adamw(parts_in, w_in[0], m_w_in[0], v_w_in[0], "adamw_w_in", 256),
        "w_out": _adamw(parts_out, w_out[0], m_w_out[0], v_w_out[0], "adamw_w_out", 128),
        "w_pe": _adamw(parts_pe, w_pe[0], m_w_pe[0], v_w_pe[0], "adamw_w_pe", 256),
        "w_pg": _adamw(parts_pg, w_pg[0], m_w_pg[0], v_w_pg[0], "adamw_w_pg", 128),
        "conv_w": [a[:CONV_W] for a in
                   _adamw(parts_cw, pad_cw(conv_w), pad_cw(m_conv_w), pad_cw(v_conv_w), "adamw_conv_w", ROWS)],
        "pre_g": _adamw(parts_pre, flat(pre_g), flat(m_pre_g), flat(v_pre_g), "adamw_pre_g", pre_rows),
    }
    packed = _adamw(parts_small, _pack([weights[n] for n in PACKED]), _pack([m_in[n] for n in PACKED]),
                    _pack([v_in[n] for n in PACKED]), "adamw_small", PACK_TILE)
    row = 0
    for n in PACKED:
        n_rows = weights[n].size // LANES
        outs[n] = [packed[q][row:row + n_rows] for q in range(4)]
        row += n_rows
    loss = packed[0][row, 0]

    result = [loss, grad_x[None]]
    for q in range(4):
        result += [outs[n][q].reshape(weights[n].shape) for n in WEIGHTS]
    return tuple(result)
```

```python
import functools

import jax
import jax.numpy as jnp
from jax import lax
from jax.experimental import pallas as pl
from jax.experimental.pallas import tpu as pltpu

F32 = jnp.float32
BF16 = jnp.bfloat16
SDS = jax.ShapeDtypeStruct

D_MODEL = 2048
D_BR = 1024
D_IN = 5 * D_BR
D_PLE = 256
N_HEAD = 8
HEAD = 128
CHUNK = 128
ROWS = 8
N_GROUP = CHUNK // ROWS
N_DEV = 8
W_IN_SHARD = D_IN // N_DEV
EPS = 1e-6
LRU_C = 8.0
CONV_W = 4
MESH_AXES = ("x", "y", "c")
MIB = 1 << 20

ADAM_LR, ADAM_B1, ADAM_B2, ADAM_EPS, ADAM_WD, ADAM_STEP = 0.001, 0.9, 0.999, 1e-08, 0.01, 10

_GELU_C = 0.7978845608028654
_GELU_A = 0.044715

V_LN_G, V_LN_B, V_CONV_B, V_B_A, V_B_X, V_LAM, V_GOUT_A, V_GOUT_B, V_CONV_W = 0, 1, 2, 3, 4, 5, 6, 7, 8
N_VEC = 16


def _params(sem, vmem_mib):
    return pltpu.CompilerParams(dimension_semantics=sem, vmem_limit_bytes=int(vmem_mib * MIB))


def _sig(x):
    return 0.5 * jnp.tanh(0.5 * x) + 0.5


def _gelu(x):
    t = jnp.tanh(_GELU_C * (x + _GELU_A * x * x * x))
    return 0.5 * x * (1.0 + t), t


def _gelu_grad(x, t):
    return 0.5 * (1.0 + t) + 0.5 * x * (1.0 - t * t) * (_GELU_C * (1.0 + 3.0 * _GELU_A * x * x))


def _neg_expm1(y, exp_y):
    series = -y * (1.0 + y * (0.5 + y * (1.0 / 6.0)))
    return jnp.where(y > -0.01, series, 1.0 - exp_y)


def _softplus(x):
    return jnp.maximum(x, 0.0) + jnp.log(1.0 + jnp.exp(-jnp.abs(x)))


def _row_ids(width):
    return lax.broadcasted_iota(jnp.int32, (ROWS, width), 0)


def _shift_down(cur, prev, k, rid):
    return jnp.where(rid >= k, pltpu.roll(cur, k, 0), pltpu.roll(prev, k, 0))


def _shift_up(cur, nxt, k, rid):
    return jnp.where(rid < ROWS - k, pltpu.roll(cur, ROWS - k, 0), pltpu.roll(nxt, ROWS - k, 0))


def _mean_last(x):
    return jnp.mean(x, axis=-1, keepdims=True)


def _rows(g):
    return pl.ds(pl.multiple_of(g * ROWS, ROWS), ROWS)


TILE_ROWS = 16


def _tile_rows(q):
    return pl.ds(pl.multiple_of(q * TILE_ROWS, TILE_ROWS), TILE_ROWS)


UNROLL = 4


def _loop(n, body, init, unroll=UNROLL):
    def wide(i, carry):
        for u in range(unroll):
            carry = body(i * unroll + u, carry)
        return carry

    return lax.fori_loop(0, n // unroll, wide, init)


def _fold_rows(x):
    return x[0:ROWS, :] + x[ROWS:TILE_ROWS, :]


def _bcast_row(x, r):
    return jnp.broadcast_to(x[r:r + 1, :], x.shape)


def _dot(a, b):
    return jnp.dot(a, b, preferred_element_type=F32)


def _dot_nt(a, b):
    return lax.dot_general(a, b, (((1,), (1,)), ((), ())), preferred_element_type=F32)


def _dot_tn(a, b):
    return lax.dot_general(a, b, (((0,), (0,)), ((), ())), preferred_element_type=F32)


def _mesh_place():
    x, y, c = lax.axis_index("x"), lax.axis_index("y"), lax.axis_index("c")
    return x, y, c, 4 * x + 2 * y + c


def _peer(x, y, c, k):
    px = 1 - x if k & 4 else x
    py = 1 - y if k & 2 else y
    pc = 1 - c if k & 1 else c
    return (px, py, pc), 4 * px + 2 * py + pc


def _remote(src, dst, send_sem, recv_sem, dev):
    return pltpu.make_async_remote_copy(src_ref=src, dst_ref=dst, send_sem=send_sem, recv_sem=recv_sem, device_id=dev,
                                        device_id_type=pl.DeviceIdType.MESH)


ANY_SPEC = pl.BlockSpec(memory_space=pl.ANY)


class _Exchange:
    def __init__(self, arrs, scatter):
        self.n = len(arrs)
        self.scatter = tuple(scatter)
        self.out_shape = [SDS(a.shape if s else (N_DEV,) + a.shape, a.dtype) for a, s in zip(arrs, scatter)]
        self.scratch = [pltpu.SemaphoreType.DMA((self.n * N_DEV,)), pltpu.SemaphoreType.DMA((self.n * N_DEV,)),
                        pltpu.SemaphoreType.DMA((self.n,))]

    def _copies(self, ins, outs, sems):
        send_sems, recv_sems, local_sems = sems
        x, y, c, me = _mesh_place()
        local, sends, recvs = [], [], []
        for a in range(self.n):
            src = ins[a].at[me] if self.scatter[a] else ins[a]
            local.append(pltpu.make_async_copy(src, outs[a].at[me], local_sems.at[a]))
        for k in range(1, N_DEV):
            dev, lin = _peer(x, y, c, k)
            for a in range(self.n):
                src = ins[a].at[lin] if self.scatter[a] else ins[a]
                pair = (send_sems.at[a * N_DEV + k], recv_sems.at[a * N_DEV + k], dev)
                sends.append(_remote(src, outs[a].at[me], *pair))
                recvs.append(_remote(src, outs[a].at[lin], *pair))
        return local, sends, recvs

    def start(self, ins, outs, sems):
        local, sends, _ = self._copies(ins, outs, sems)
        for cp in local + sends:
            cp.start()

    def wait(self, ins, outs, sems):
        local, sends, recvs = self._copies(ins, outs, sems)
        for cp in recvs:
            cp.wait_recv()
        for cp in sends:
            cp.wait_send()
        for cp in local:
            cp.wait()


def _exchange(arrs, scatter, name):
    ex = _Exchange(arrs, [scatter] * len(arrs))
    n = ex.n

    def body(*refs):
        ins, outs, sems = refs[:n], refs[n:2 * n], refs[2 * n:]
        ex.start(ins, outs, sems)
        ex.wait(ins, outs, sems)

    return pl.pallas_call(
        body, name=name, out_shape=ex.out_shape, in_specs=[ANY_SPEC] * n, out_specs=[ANY_SPEC] * n,
        scratch_shapes=ex.scratch,
    )(*arrs)


def _pre_norm(x, pre_g, tm=512):
    t_len = x.shape[0]

    def body(x_ref, g_ref, hn_ref):
        g = g_ref[...]

        def rows_body(q, _):
            rows = _tile_rows(q)
            xv = x_ref[rows, :]
            hn_ref[rows, :] = (xv * lax.rsqrt(_mean_last(xv * xv) + EPS) * g).astype(BF16)
            return 0

        _loop(tm // TILE_ROWS, rows_body, 0)

    tile = pl.BlockSpec((tm, D_MODEL), lambda i: (i, 0))
    return pl.pallas_call(
        body, name="pre_norm", grid=(t_len // tm,),
        in_specs=[tile, pl.BlockSpec((1, D_MODEL), lambda i: (0, 0))], out_specs=tile,
        out_shape=SDS((t_len, D_MODEL), BF16),
        compiler_params=_params(("arbitrary",), 24),
    )(x, pre_g)


AG_ORDER = (0, 1, 2, 4, 6, 3, 5, 7)
SIBLING = 1
ICI_MASKS = (2, 4, 6)
DIRECT_MASKS = (SIBLING,) + ICI_MASKS


def _in_proj(hn, w_shard, others, tm=512):
    t_len = hn.shape[0]
    n_i = t_len // tm
    n_o = len(others)
    me_out = 4 * lax.axis_index("x") + 2 * lax.axis_index("y") + lax.axis_index("c")
    order = jnp.stack([me_out ^ k for k in AG_ORDER]).astype(jnp.int32)

    def body(order_ref, hn_ref, w_hbm, *refs):
        o_in = refs[:n_o]
        z_ref, wg_hbm = refs[n_o], refs[n_o + 1]
        o_out = refs[n_o + 2:2 * n_o + 2]
        wbuf, send_w, recv_w, fsend_w, frecv_w, send_o, recv_o, fsend_o, frecv_o, wb_sems, loc_sems = refs[2 * n_o + 2:]
        j, i = pl.program_id(0), pl.program_id(1)
        x, y, c, me = _mesh_place()
        sib = _peer(x, y, c, SIBLING)[0]

        def direct(k, a=None):
            dev, lin = _peer(x, y, c, k)
            if a is None:
                return (_remote(w_hbm, wbuf.at[me], send_w.at[k], recv_w.at[k], dev),
                        _remote(w_hbm, wbuf.at[lin], send_w.at[k], recv_w.at[k], dev))
            pair = (send_o.at[a * N_DEV + k], recv_o.at[a * N_DEV + k], dev)
            return _remote(o_in[a], o_out[a].at[me], *pair), _remote(o_in[a], o_out[a].at[lin], *pair)

        def passed(k, a=None):
            mine, theirs = _peer(x, y, c, k)[1], _peer(x, y, c, k ^ SIBLING)[1]
            if a is None:
                pair = (fsend_w.at[k], frecv_w.at[k], sib)
                return _remote(wbuf.at[mine], wbuf.at[mine], *pair), _remote(wbuf.at[theirs], wbuf.at[theirs], *pair)
            pair = (fsend_o.at[a * N_DEV + k], frecv_o.at[a * N_DEV + k], sib)
            return (_remote(o_out[a].at[mine], o_out[a].at[mine], *pair),
                    _remote(o_out[a].at[theirs], o_out[a].at[theirs], *pair))

        def own_copies():
            return [pltpu.make_async_copy(o_in[a], o_out[a].at[me], loc_sems.at[1 + a]) for a in range(n_o)]

        @pl.when(jnp.logical_and(j == 0, i == 0))
        def _():
            own = pltpu.make_async_copy(w_hbm, wbuf.at[me], loc_sems.at[0])
            own.start()
            for cp in own_copies():
                cp.start()
            for k in DIRECT_MASKS:
                direct(k)[0].start()
            for k in DIRECT_MASKS:
                for a in range(n_o):
                    direct(k, a)[0].start()
            own.wait()

        for jj in range(1, N_DEV):
            mask = AG_ORDER[jj]

            @pl.when(jnp.logical_and(j == jj, i == 0))
            def _(jj=jj, mask=mask):
                if mask in DIRECT_MASKS:
                    direct(mask)[1].wait_recv()
                    if mask in ICI_MASKS:
                        passed(mask)[0].start()
                else:
                    passed(mask ^ SIBLING)[1].wait_recv()
                late = jj - (N_DEV - len(ICI_MASKS))
                if late >= 0:
                    for a in range(n_o):
                        direct(ICI_MASKS[late], a)[1].wait_recv()
                        passed(ICI_MASKS[late], a)[0].start()

        slot = order_ref[j]

        @pl.when(i == 0)
        def _():
            pltpu.make_async_copy(wbuf.at[slot], wg_hbm.at[slot], wb_sems.at[j]).start()

        z_ref[...] = _dot(hn_ref[...], wbuf[slot])

        @pl.when(jnp.logical_and(j == N_DEV - 1, i == n_i - 1))
        def _():
            for a in range(n_o):
                direct(SIBLING, a)[1].wait_recv()
            for k in ICI_MASKS:
                for a in range(n_o):
                    passed(k, a)[1].wait_recv()
            for k in DIRECT_MASKS:
                direct(k)[0].wait_send()
                for a in range(n_o):
                    direct(k, a)[0].wait_send()
            for k in ICI_MASKS:
                passed(k)[0].wait_send()
                for a in range(n_o):
                    passed(k, a)[0].wait_send()
            for cp in own_copies():
                cp.wait()
            for jj in range(N_DEV):
                pltpu.make_async_copy(wbuf.at[0], wg_hbm.at[0], wb_sems.at[jj]).wait()

    dma = lambda n: pltpu.SemaphoreType.DMA((n,))
    grid_spec = pltpu.PrefetchScalarGridSpec(
        num_scalar_prefetch=1, grid=(N_DEV, n_i),
        in_specs=[pl.BlockSpec((tm, D_MODEL), lambda j, i, order: (i, 0)), ANY_SPEC] + [ANY_SPEC] * n_o,
        out_specs=[pl.BlockSpec((tm, W_IN_SHARD), lambda j, i, order: (i, order[j])), ANY_SPEC] + [ANY_SPEC] * n_o,
        scratch_shapes=[pltpu.VMEM((N_DEV, D_MODEL, W_IN_SHARD), BF16), dma(N_DEV), dma(N_DEV), dma(N_DEV), dma(N_DEV),
                        dma(n_o * N_DEV), dma(n_o * N_DEV), dma(n_o * N_DEV), dma(n_o * N_DEV), dma(N_DEV), dma(1 + n_o)])
    res = pl.pallas_call(
        body, name="in_proj", grid_spec=grid_spec,
        out_shape=[SDS((t_len, D_IN), F32), SDS((N_DEV, D_MODEL, W_IN_SHARD), BF16)]
        + [SDS((N_DEV,) + o.shape, o.dtype) for o in others],
        compiler_params=_params(("arbitrary", "arbitrary"), 44),
    )(order, hn, w_shard, *others)
    return res[0], res[1], res[2:]


def _conv_rows(cur, prev, cw_ref, cb, rid):
    acc = cw_ref[3:4, :] * cur + cb
    for k in range(1, CONV_W):
        acc = acc + cw_ref[3 - k:4 - k, :] * _shift_down(cur, prev, k, rid)
    return acc


def _lru_gates(pa, px, ba, bx, sp8, first_row):
    r = _sig(pa + ba)
    i = _sig(px + bx)
    la = -(r * sp8)
    a = jnp.exp(la)
    mult = jnp.where(first_row, 1.0, jnp.sqrt(_neg_expm1(2.0 * la, a * a)))
    return r, i, a, mult


def _mix_fwd(z, ln_g, ln_b, wm, bias, cw, cb, wax, ba, bx, lam, goa, gob):
    t_len = z.shape[0]
    n_chunk = t_len // CHUNK

    def body(z_ref, lng_ref, lnb_ref, wm_ref, bias_ref, cw_ref, cb_ref, wax_ref, ba_ref, bx_ref, lam_ref, goa_ref,
             gob_ref, y_ref, h_ref, vn_s, xc_s, mixed_s, pre_s, y_s, carry_s, halo_s):
        c_id = pl.program_id(0)
        rid = _row_ids(D_BR)

        @pl.when(c_id == 0)
        def _():
            carry_s[...] = jnp.zeros_like(carry_s)
            halo_s[...] = jnp.zeros_like(halo_s)

        lng, lnb, cb = lng_ref[...], lnb_ref[...], cb_ref[...]

        def phase1(g, prev):
            rows = _rows(g)
            vg, _ = _gelu(z_ref[rows, D_BR:2 * D_BR])
            xm = vg - _mean_last(vg)
            rs = lax.rsqrt(_mean_last(xm * xm) + EPS)
            vn_s[rows, :] = xm * rs * lng + lnb
            xb = z_ref[rows, 3 * D_BR:4 * D_BR]
            xc_s[rows, :] = _conv_rows(xb, prev, cw_ref, cb, rid)
            return xb

        halo_s[...] = _loop(N_GROUP, phase1, halo_s[...])

        for h in range(N_HEAD):
            cs = slice(h * HEAD, (h + 1) * HEAD)
            mixed_s[:, cs] = _dot(wm_ref[h], vn_s[:, cs].astype(BF16))
            pre = _dot(xc_s[:, cs].astype(BF16), wax_ref[h])
            pre_s[:, cs] = pre[:, :HEAD]
            pre_s[:, D_BR + h * HEAD:D_BR + (h + 1) * HEAD] = pre[:, HEAD:]

        ba, bx, goa, gob = ba_ref[...], bx_ref[...], goa_ref[...], gob_ref[...]
        sp8 = LRU_C * _softplus(-lam_ref[...])

        def phase3(g, carry):
            rows = _rows(g)
            ug, _ = _gelu(z_ref[rows, 0:D_BR])
            ga = z_ref[rows, 2 * D_BR:3 * D_BR]
            ya = ug * (mixed_s[rows, :] + bias_ref[rows, :]) * (ga * _sig(ga))
            y_s[rows, 0:D_BR] = ya * lax.rsqrt(_mean_last(ya * ya) + EPS) * goa

            first_row = jnp.logical_and(jnp.logical_and(c_id == 0, g == 0), rid == 0)
            _, i, a, mult = _lru_gates(pre_s[rows, 0:D_BR], pre_s[rows, D_BR:2 * D_BR], ba, bx, sp8, first_row)
            b = mult * i * xc_s[rows, :]
            for d in (1, 2, 4):
                a_sh = jnp.where(rid >= d, pltpu.roll(a, d, 0), 1.0)
                b_sh = jnp.where(rid >= d, pltpu.roll(b, d, 0), 0.0)
                b = a * b_sh + b
                a = a * a_sh
            hh = b + a * carry
            h_ref[rows, :] = hh
            gb = z_ref[rows, 4 * D_BR:5 * D_BR]
            yb = hh * (gb * _sig(gb))
            y_s[rows, D_BR:2 * D_BR] = yb * lax.rsqrt(_mean_last(yb * yb) + EPS) * gob
            return _bcast_row(hh, ROWS - 1)

        carry_s[...] = _loop(N_GROUP, phase3, carry_s[...])
        y_ref[...] = y_s[...].astype(BF16)

    vec = pl.BlockSpec((1, D_BR), lambda i: (0, 0))
    return pl.pallas_call(
        body, name="mix_fwd", grid=(n_chunk,),
        in_specs=[pl.BlockSpec((CHUNK, D_IN), lambda i: (i, 0)), vec, vec,
                  pl.BlockSpec((N_HEAD, HEAD, HEAD), lambda i: (0, 0, 0)),
                  pl.BlockSpec((CHUNK, D_BR), lambda i: (0, 0)),
                  pl.BlockSpec((ROWS, D_BR), lambda i: (0, 0)), vec,
                  pl.BlockSpec((N_HEAD, HEAD, 2 * HEAD), lambda i: (0, 0, 0)), vec, vec, vec, vec, vec],
        out_specs=[pl.BlockSpec((CHUNK, 2 * D_BR), lambda i: (i, 0)), pl.BlockSpec((CHUNK, D_BR), lambda i: (i, 0))],
        out_shape=[SDS((t_len, 2 * D_BR), BF16), SDS((t_len, D_BR), F32)],
        scratch_shapes=[pltpu.VMEM((CHUNK, D_BR), F32), pltpu.VMEM((CHUNK, D_BR), F32), pltpu.VMEM((CHUNK, D_BR), F32),
                        pltpu.VMEM((CHUNK, 2 * D_BR), F32), pltpu.VMEM((CHUNK, 2 * D_BR), F32),
                        pltpu.VMEM((ROWS, D_BR), F32), pltpu.VMEM((ROWS, D_BR), F32)],
        compiler_params=_params(("arbitrary",), 32),
    )(z, ln_g, ln_b, wm, bias, cw, cb, wax, ba, bx, lam, goa, gob)


def _load_weight(w_hbm, w_vmem, sem):
    @pl.when(pl.program_id(0) == 0)
    def _():
        cp = pltpu.make_async_copy(w_hbm, w_vmem, sem)
        cp.start()
        cp.wait()


def _out_proj(y, x, w_out, post_g, tm=512):
    t_len = y.shape[0]

    def body(y_ref, x_ref, w_hbm, g_ref, h1_ref, ob_ref, w_s, o_s, sem):
        _load_weight(w_hbm, w_s, sem)
        o_s[...] = _dot(y_ref[...], w_s[...])
        g = g_ref[...]

        def rows_body(q, _):
            rows = _tile_rows(q)
            o = o_s[rows, :]
            h1_ref[rows, :] = x_ref[rows, :] + o * lax.rsqrt(_mean_last(o * o) + EPS) * g
            ob_ref[rows, :] = o.astype(BF16)
            return 0

        _loop(tm // TILE_ROWS, rows_body, 0)

    tile = pl.BlockSpec((tm, D_MODEL), lambda i: (i, 0))
    return pl.pallas_call(
        body, name="out_proj", grid=(t_len // tm,),
        in_specs=[tile, tile, pl.BlockSpec(memory_space=pl.ANY), pl.BlockSpec((1, D_MODEL), lambda i: (0, 0))],
        out_specs=[tile, tile],
        out_shape=[SDS((t_len, D_MODEL), F32), SDS((t_len, D_MODEL), BF16)],
        scratch_shapes=[pltpu.VMEM((D_MODEL, D_MODEL), BF16), pltpu.VMEM((tm, D_MODEL), F32), pltpu.SemaphoreType.DMA],
        compiler_params=_params(("arbitrary",), 44),
    )(y, x, w_out, post_g)


def _ple_loss(h1, p, tgt, w_pg, w_pe_g, tm=256):
    t_len = h1.shape[0]
    n_tile = t_len // tm
    pe_shard = D_MODEL // N_DEV

    def body(h1_ref, p_ref, t_ref, w_hbm, wpe_ref, dh2_ref, dgl_ref, h1b_ref, loss_ref, dwpe_ref, w_s, pe_s, gl_s, acc_s,
             dpe_s, gpe_s, sem):
        _load_weight(w_hbm, w_s, sem)
        i = pl.program_id(0)

        @pl.when(i == 0)
        def _():
            acc_s[...] = jnp.zeros_like(acc_s)
            gpe_s[...] = jnp.zeros_like(gpe_s)

        h1b_ref[...] = h1_ref[...].astype(BF16)
        pb = p_ref[...].astype(BF16)
        for j in range(N_DEV):
            pe_s[:, j * pe_shard:(j + 1) * pe_shard] = _dot(pb, wpe_ref[j])
        gl_s[...] = _dot(h1b_ref[...], w_s[...])

        def rows_body(q, acc):
            rows = _tile_rows(q)
            pe = pe_s[rows, :]
            g = _sig(gl_s[rows, :])
            e = h1_ref[rows, :] + pe * g - t_ref[rows, :]
            dh2 = e * (1.0 / D_MODEL)
            dh2_ref[rows, :] = dh2
            dpe_s[rows, :] = (dh2 * g).astype(BF16)
            dgl_ref[rows, :] = (dh2 * pe * g * (1.0 - g)).astype(BF16)
            return acc + _fold_rows(e * e)

        acc_s[...] = _loop(tm // TILE_ROWS, rows_body, acc_s[...])
        gpe_s[...] += _dot_tn(pb, dpe_s[...])

        @pl.when(i == n_tile - 1)
        def _():
            loss_ref[...] = jnp.full(loss_ref.shape, 0.5 / D_MODEL * jnp.sum(acc_s[...]), F32)
            for j in range(N_DEV):
                dwpe_ref[j] = gpe_s[:, j * pe_shard:(j + 1) * pe_shard].astype(BF16)

    tile = pl.BlockSpec((tm, D_MODEL), lambda i: (i, 0))
    pe_blocks = pl.BlockSpec((N_DEV, D_PLE, pe_shard), lambda i: (0, 0, 0))
    return pl.pallas_call(
        body, name="ple_loss", grid=(n_tile,),
        in_specs=[tile, pl.BlockSpec((tm, D_PLE), lambda i: (i, 0)), tile, pl.BlockSpec(memory_space=pl.ANY), pe_blocks],
        out_specs=[tile, tile, tile, pl.BlockSpec((ROWS, HEAD), lambda i: (0, 0)), pe_blocks],
        out_shape=[SDS((t_len, D_MODEL), F32), SDS((t_len, D_MODEL), BF16), SDS((t_len, D_MODEL), BF16),
                   SDS((ROWS, HEAD), F32), SDS((N_DEV, D_PLE, pe_shard), BF16)],
        scratch_shapes=[pltpu.VMEM((D_MODEL, D_MODEL), BF16), pltpu.VMEM((tm, D_MODEL), F32),
                        pltpu.VMEM((tm, D_MODEL), F32), pltpu.VMEM((ROWS, D_MODEL), F32), pltpu.VMEM((tm, D_MODEL), BF16),
                        pltpu.VMEM((D_PLE, D_MODEL), F32), pltpu.SemaphoreType.DMA],
        compiler_params=_params(("arbitrary",), 48),
    )(h1, p, tgt, w_pg, w_pe_g)


def _tail_bwd(dh2, dgl, ob, w_pg, w_out, post_g, tm=256):
    t_len = dh2.shape[0]
    n_tile = t_len // tm

    def body(dh2_ref, dgl_ref, ob_ref, wpg_hbm, wout_hbm, g_ref, dh1_ref, do_ref, dy_ref, dg_ref, wpg_s, wout_s, t_s,
             acc_s, sems):
        _load_weight(wpg_hbm, wpg_s, sems.at[0])
        _load_weight(wout_hbm, wout_s, sems.at[1])
        i = pl.program_id(0)

        @pl.when(i == 0)
        def _():
            acc_s[...] = jnp.zeros_like(acc_s)

        t_s[...] = _dot_nt(dgl_ref[...], wpg_s[...])
        g = g_ref[...]

        def rows_body(q, acc):
            rows = _tile_rows(q)
            dh1 = dh2_ref[rows, :] + t_s[rows, :]
            dh1_ref[rows, :] = dh1
            o = ob_ref[rows, :].astype(F32)
            rr = lax.rsqrt(_mean_last(o * o) + EPS)
            on = o * rr
            dog = dh1 * g
            do_ref[rows, :] = (rr * (dog - on * _mean_last(dog * on))).astype(BF16)
            return acc + _fold_rows(dh1 * on)

        acc_s[...] = _loop(tm // TILE_ROWS, rows_body, acc_s[...])
        dy_ref[...] = _dot_nt(do_ref[...], wout_s[...]).astype(BF16)

        @pl.when(i == n_tile - 1)
        def _():
            dg_ref[...] = jnp.sum(acc_s[...], axis=0, keepdims=True)

    tile = pl.BlockSpec((tm, D_MODEL), lambda i: (i, 0))
    vec = pl.BlockSpec((1, D_MODEL), lambda i: (0, 0))
    hbm = pl.BlockSpec(memory_space=pl.ANY)
    return pl.pallas_call(
        body, name="tail_bwd", grid=(n_tile,),
        in_specs=[tile, tile, tile, hbm, hbm, vec],
        out_specs=[tile, tile, tile, vec],
        out_shape=[SDS((t_len, D_MODEL), F32), SDS((t_len, D_MODEL), BF16), SDS((t_len, D_MODEL), BF16),
                   SDS((1, D_MODEL), F32)],
        scratch_shapes=[pltpu.VMEM((D_MODEL, D_MODEL), BF16), pltpu.VMEM((D_MODEL, D_MODEL), BF16),
                        pltpu.VMEM((tm, D_MODEL), F32), pltpu.VMEM((ROWS, D_MODEL), F32), pltpu.SemaphoreType.DMA((2,))],
        compiler_params=_params(("arbitrary",), 48),
    )(dh2, dgl, ob, w_pg, w_out, post_g)


def _mix_bwd(z, dy, h, ln_g, ln_b, wm, wm_t, bias, cw, cb, wax, wax_t, ba, bx, lam, goa, gob, ex_arrs, ex_scatter):
    t_len = z.shape[0]
    n_chunk = t_len // CHUNK
    halo_blocks = CHUNK // ROWS
    ex = _Exchange(ex_arrs, ex_scatter)
    n_in, n_out, n_scratch = 19, 5, 17

    def body(*refs):
        (z_ref, zhalo_ref, dy_ref, h_ref, hhalo_ref, lng_ref, lnb_ref, wm_ref, wmt_ref, bias_ref, cw_ref, cb_ref,
         wax_ref, waxt_ref, ba_ref, bx_ref, lam_ref, goa_ref, gob_ref) = refs[:n_in]
        ex_in = refs[n_in:n_in + ex.n]
        dz_ref, vecs_ref, dws_ref, dwax_ref, dbs_ref = refs[n_in + ex.n:n_in + ex.n + n_out]
        ex_out = refs[n_in + ex.n + n_out:n_in + 2 * ex.n + n_out]
        (vn_s, vh_s, rs_s, xc_s, mixed_s, pre_s, dmix_s, dvn_s, dho_s, dxc_s, dpre_s, dz_s, acc_s, accdm_s,
         cg_s, ca_s, dxchalo_s) = refs[n_in + 2 * ex.n + n_out:n_in + 2 * ex.n + n_out + n_scratch]
        ex_sems = refs[n_in + 2 * ex.n + n_out + n_scratch:]
        step = pl.program_id(0)
        c_id = n_chunk - 1 - step
        rid = _row_ids(D_BR)
        first_chunk = c_id == 0

        @pl.when(step == 0)
        def _():
            ex.start(ex_in, ex_out, ex_sems)
            acc_s[...] = jnp.zeros_like(acc_s)
            accdm_s[...] = jnp.zeros_like(accdm_s)
            cg_s[...] = jnp.zeros_like(cg_s)
            ca_s[...] = jnp.zeros_like(ca_s)
            dxchalo_s[...] = jnp.zeros_like(dxchalo_s)
            dws_ref[...] = jnp.zeros_like(dws_ref)
            dwax_ref[...] = jnp.zeros_like(dwax_ref)

        lng, lnb, cb = lng_ref[...], lnb_ref[...], cb_ref[...]
        xb_halo = jnp.where(first_chunk, 0.0, zhalo_ref[...])
        h_halo = jnp.where(first_chunk, 0.0, hhalo_ref[...])

        def prev_rows(ref, cols, g, halo):
            before = ref[pl.ds(pl.multiple_of(jnp.maximum(g - 1, 0) * ROWS, ROWS), ROWS), cols]
            return jnp.where(g > 0, before, halo)

        def phase1(g, prev):
            rows = _rows(g)
            vg, _ = _gelu(z_ref[rows, D_BR:2 * D_BR])
            xm = vg - _mean_last(vg)
            rs = lax.rsqrt(_mean_last(xm * xm) + EPS)
            vh = xm * rs
            vh_s[rows, :] = vh
            rs_s[rows, :] = jnp.broadcast_to(rs, (ROWS, HEAD))
            vn_s[rows, :] = vh * lng + lnb
            xb = z_ref[rows, 3 * D_BR:4 * D_BR]
            xc_s[rows, :] = _conv_rows(xb, prev, cw_ref, cb, rid)
            return xb

        _loop(N_GROUP, phase1, xb_halo)

        for hd in range(N_HEAD):
            cs = slice(hd * HEAD, (hd + 1) * HEAD)
            mixed_s[:, cs] = _dot(wm_ref[hd], vn_s[:, cs].astype(BF16))
            pre = _dot(xc_s[:, cs].astype(BF16), wax_ref[hd])
            pre_s[:, cs] = pre[:, :HEAD]
            pre_s[:, D_BR + hd * HEAD:D_BR + (hd + 1) * HEAD] = pre[:, HEAD:]

        goa, gob = goa_ref[...], gob_ref[...]

        def phase3(g, _):
            rows = _rows(g)
            u = z_ref[rows, 0:D_BR]
            ug, tu = _gelu(u)
            ga = z_ref[rows, 2 * D_BR:3 * D_BR]
            sga = _sig(ga)
            sa = ga * sga
            mixed = mixed_s[rows, :] + bias_ref[rows, :]
            ya0 = ug * mixed
            ya = ya0 * sa
            ra = lax.rsqrt(_mean_last(ya * ya) + EPS)
            dyan = dy_ref[rows, 0:D_BR].astype(F32)
            acc_s[V_GOUT_A] += dyan * ya * ra
            dyg = dyan * goa
            dya = ra * dyg - ya * (ra * ra * ra) * _mean_last(dyg * ya)
            dya0 = dya * sa
            dz_s[rows, 2 * D_BR:3 * D_BR] = dya * ya0 * (sga * (1.0 + ga * (1.0 - sga)))
            dmix = dya0 * ug
            dmix_s[rows, :] = dmix
            accdm_s[rows, :] += dmix
            dz_s[rows, 0:D_BR] = dya0 * mixed * _gelu_grad(u, tu)

            hh = h_ref[rows, :]
            gb = z_ref[rows, 4 * D_BR:5 * D_BR]
            sgb = _sig(gb)
            sb = gb * sgb
            yb = hh * sb
            rb = lax.rsqrt(_mean_last(yb * yb) + EPS)
            dybn = dy_ref[rows, D_BR:2 * D_BR].astype(F32)
            acc_s[V_GOUT_B] += dybn * yb * rb
            dyg = dybn * gob
            dyb = rb * dyg - yb * (rb * rb * rb) * _mean_last(dyg * yb)
            dho_s[rows, :] = dyb * sb
            dz_s[rows, 4 * D_BR:5 * D_BR] = dyb * hh * (sgb * (1.0 + gb * (1.0 - sgb)))
            return 0

        _loop(N_GROUP, phase3, 0)

        for hd in range(N_HEAD):
            cs = slice(hd * HEAD, (hd + 1) * HEAD)
            dmb = dmix_s[:, cs].astype(BF16)
            dvn_s[:, cs] = _dot(wmt_ref[hd], dmb)
            dws_ref[hd] += _dot_nt(dmb, vn_s[:, cs].astype(BF16))

        def phase5(g, _):
            rows = _rows(g)
            dvn = dvn_s[rows, :]
            vh = vh_s[rows, :]
            acc_s[V_LN_G] += dvn * vh
            acc_s[V_LN_B] += dvn
            dvh = dvn * lng
            rs = rs_s[rows, 0:1]
            dvg = rs * (dvh - _mean_last(dvh) - vh * _mean_last(dvh * vh))
            v = z_ref[rows, D_BR:2 * D_BR]
            _, tv = _gelu(v)
            dz_s[rows, D_BR:2 * D_BR] = dvg * _gelu_grad(v, tv)
            return 0

        _loop(N_GROUP, phase5, 0)

        ba, bx = ba_ref[...], bx_ref[...]
        sp8 = LRU_C * _softplus(-lam_ref[...])

        def phase6(k, carry):
            cg, ca = carry
            g = N_GROUP - 1 - k
            rows = _rows(g)
            first_row = jnp.logical_and(jnp.logical_and(first_chunk, g == 0), rid == 0)
            r, i, a, mult = _lru_gates(pre_s[rows, 0:D_BR], pre_s[rows, D_BR:2 * D_BR], ba, bx, sp8, first_row)
            a_nx = jnp.where(rid < ROWS - 1, pltpu.roll(a, ROWS - 1, 0), ca)
            aa, bb = a_nx, dho_s[rows, :]
            for d in (1, 2, 4):
                a_sh = jnp.where(rid < ROWS - d, pltpu.roll(aa, ROWS - d, 0), 1.0)
                b_sh = jnp.where(rid < ROWS - d, pltpu.roll(bb, ROWS - d, 0), 0.0)
                bb = aa * b_sh + bb
                aa = aa * a_sh
            gg = bb + aa * cg
            hh = h_ref[rows, :]
            hprev = _shift_down(hh, prev_rows(h_ref, slice(None), g, h_halo), 1, rid)
            xc = xc_s[rows, :]
            gx = gg * xc
            dla = gg * hprev * a - jnp.where(first_row, 0.0, gx * i * (a * a) * lax.rsqrt(mult * mult))
            acc_s[V_LAM] += -(dla * r)
            dpa = -(dla * sp8) * r * (1.0 - r)
            dpx = gx * mult * i * (1.0 - i)
            acc_s[V_B_A] += dpa
            acc_s[V_B_X] += dpx
            dpre_s[rows, 0:D_BR] = dpa
            dpre_s[rows, D_BR:2 * D_BR] = dpx
            dxc_s[rows, :] = gg * mult * i
            return _bcast_row(gg, 0), _bcast_row(a, 0)

        cg, ca = _loop(N_GROUP, phase6, (cg_s[...], ca_s[...]))
        cg_s[...] = cg
        ca_s[...] = ca

        for hd in range(N_HEAD):
            cs = slice(hd * HEAD, (hd + 1) * HEAD)
            dpre = jnp.concatenate([dpre_s[:, cs], dpre_s[:, D_BR + hd * HEAD:D_BR + (hd + 1) * HEAD]], axis=1).astype(BF16)
            dxc_s[:, cs] += _dot(dpre, waxt_ref[hd])
            dwax_ref[hd] += _dot_tn(xc_s[:, cs].astype(BF16), dpre)

        def phase8(k, nxt):
            g = N_GROUP - 1 - k
            rows = _rows(g)
            dxc = dxc_s[rows, :]
            acc_s[V_CONV_B] += dxc
            xb = z_ref[rows, 3 * D_BR:4 * D_BR]
            dxb = cw_ref[3:4, :] * dxc
            acc_s[V_CONV_W + 3] += dxc * xb
            for j in range(1, CONV_W):
                later = _shift_up(dxc, nxt, j, rid)
                dxb = dxb + cw_ref[3 - j:4 - j, :] * later
                acc_s[V_CONV_W + 3 - j] += later * xb
            dz_s[rows, 3 * D_BR:4 * D_BR] = dxb
            return dxc

        dxchalo_s[...] = _loop(N_GROUP, phase8, dxchalo_s[...])
        dz_ref[...] = dz_s[...].astype(BF16)

        @pl.when(step == n_chunk - 1)
        def _():
            for v in range(N_VEC):
                vecs_ref[v:v + 1, :] = jnp.sum(acc_s[v], axis=0, keepdims=True)
            lam = lam_ref[...]
            vecs_ref[V_LAM:V_LAM + 1, :] = vecs_ref[V_LAM:V_LAM + 1, :] * (-LRU_C * _sig(-lam))
            tril = (lax.broadcasted_iota(jnp.int32, (HEAD, HEAD), 0) >= lax.broadcasted_iota(jnp.int32, (HEAD, HEAD), 1))
            ones = jnp.ones((ROWS, HEAD), BF16)
            for hd in range(N_HEAD):
                cs = slice(hd * HEAD, (hd + 1) * HEAD)
                dws_ref[hd] = jnp.where(tril, dws_ref[hd], 0.0)
                blk = accdm_s[:, cs]
                hi = blk.astype(BF16)
                lo = (blk - hi.astype(F32)).astype(BF16)
                dbs_ref[hd:hd + 1, :] = (_dot_nt(ones, hi) + _dot_nt(ones, lo))[0:1, :]
            ex.wait(ex_in, ex_out, ex_sems)

    vec = pl.BlockSpec((1, D_BR), lambda i: (0, 0))
    rev = lambda i: (n_chunk - 1 - i, 0)
    halo = lambda col: (lambda i: (jnp.maximum((n_chunk - 1 - i) * halo_blocks - 1, 0), col))
    full3 = lambda a, b, c: pl.BlockSpec((a, b, c), lambda i: (0, 0, 0))
    big = lambda w: pltpu.VMEM((CHUNK, w), F32)
    res = pl.pallas_call(
        body, name="mix_bwd", grid=(n_chunk,),
        in_specs=[pl.BlockSpec((CHUNK, D_IN), rev), pl.BlockSpec((ROWS, D_BR), halo(3)),
                  pl.BlockSpec((CHUNK, 2 * D_BR), rev), pl.BlockSpec((CHUNK, D_BR), rev),
                  pl.BlockSpec((ROWS, D_BR), halo(0)), vec, vec,
                  full3(N_HEAD, HEAD, HEAD), full3(N_HEAD, HEAD, HEAD),
                  pl.BlockSpec((CHUNK, D_BR), lambda i: (0, 0)), pl.BlockSpec((ROWS, D_BR), lambda i: (0, 0)), vec,
                  full3(N_HEAD, HEAD, 2 * HEAD), full3(N_HEAD, 2 * HEAD, HEAD), vec, vec, vec, vec, vec]
        + [ANY_SPEC] * ex.n,
        out_specs=[pl.BlockSpec((CHUNK, D_IN), rev), pl.BlockSpec((N_VEC, D_BR), lambda i: (0, 0)),
                   full3(N_HEAD, HEAD, HEAD), full3(N_HEAD, HEAD, 2 * HEAD),
                   pl.BlockSpec((N_HEAD, HEAD), lambda i: (0, 0))] + [ANY_SPEC] * ex.n,
        out_shape=[SDS((t_len, D_IN), BF16), SDS((N_VEC, D_BR), F32), SDS((N_HEAD, HEAD, HEAD), F32),
                   SDS((N_HEAD, HEAD, 2 * HEAD), F32), SDS((N_HEAD, HEAD), F32)] + ex.out_shape,
        scratch_shapes=[big(D_BR), big(D_BR), big(HEAD), big(D_BR), big(D_BR), big(2 * D_BR), big(D_BR), big(D_BR),
                        big(D_BR), big(D_BR), big(2 * D_BR), big(D_IN),
                        pltpu.VMEM((N_VEC, ROWS, D_BR), F32), big(D_BR),
                        pltpu.VMEM((ROWS, D_BR), F32), pltpu.VMEM((ROWS, D_BR), F32), pltpu.VMEM((ROWS, D_BR), F32)]
        + ex.scratch,
        compiler_params=_params(("arbitrary",), 48),
    )(z, z, dy, h, h, ln_g, ln_b, wm, wm_t, bias, cw, cb, wax, wax_t, ba, bx, lam, goa, gob, *ex_arrs)
    return res[:n_out], res[n_out:]


def _in_bwd(dz, w_in_g, x, dh1, pre_g, first_tile, n_tile, prev, name, ex_arrs=(), ex_scatter=(), tm=256):
    t_len = x.shape[0]
    ex = _Exchange(ex_arrs, ex_scatter)
    n_prev = 0 if prev is None else 2

    def body(dz_ref, w_hbm, x_ref, dh1_ref, g_ref, *refs):
        prev_refs, refs = refs[:n_prev], refs[n_prev:]
        ex_in, (gx_ref, dg_ref), ex_out = refs[:ex.n], refs[ex.n:ex.n + 2], refs[ex.n + 2:2 * ex.n + 2]
        w_s, t_s, dg_s, w_sems = refs[2 * ex.n + 2:2 * ex.n + 6]
        ex_sems = refs[2 * ex.n + 6:]
        i = pl.program_id(0)

        @pl.when(i == 0)
        def _():
            if ex.n:
                ex.start(ex_in, ex_out, ex_sems)
            loads = [pltpu.make_async_copy(w_hbm.at[s], w_s.at[:, s * W_IN_SHARD:(s + 1) * W_IN_SHARD], w_sems.at[s])
                     for s in range(N_DEV)]
            for cp in loads:
                cp.start()
            dg_s[...] = jnp.zeros_like(dg_s)
            for cp in loads:
                cp.wait()

        t_s[...] = _dot_nt(dz_ref[...], w_s[...])
        g = g_ref[...]

        def rows_body(q, acc):
            rows = _tile_rows(q)
            xv = x_ref[rows, :]
            r = lax.rsqrt(_mean_last(xv * xv) + EPS)
            xh = xv * r
            dhn = t_s[rows, :]
            dg = dhn * g
            gx_ref[rows, :] = dh1_ref[rows, :] + r * (dg - xh * _mean_last(dg * xh))
            return acc + _fold_rows(dhn * xh)

        dg_s[...] = _loop(tm // TILE_ROWS, rows_body, dg_s[...])

        @pl.when(i == n_tile - 1)
        def _():
            dg = jnp.sum(dg_s[...], axis=0, keepdims=True)
            dg_ref[...] = dg + prev_refs[1][...] if n_prev else dg
            if ex.n:
                ex.wait(ex_in, ex_out, ex_sems)

    tile = pl.BlockSpec((tm, D_MODEL), lambda i: (first_tile + i, 0))
    vec = pl.BlockSpec((1, D_MODEL), lambda i: (0, 0))
    prev_specs = [ANY_SPEC, vec] if n_prev else []
    res = pl.pallas_call(
        body, name=name, grid=(n_tile,),
        in_specs=[pl.BlockSpec((tm, D_IN), lambda i: (first_tile + i, 0)), ANY_SPEC, tile, tile, vec] + prev_specs
        + [ANY_SPEC] * ex.n,
        out_specs=[tile, vec] + [ANY_SPEC] * ex.n,
        out_shape=[SDS((t_len, D_MODEL), F32), SDS((1, D_MODEL), F32)] + ex.out_shape,
        scratch_shapes=[pltpu.VMEM((D_MODEL, D_IN), BF16), pltpu.VMEM((tm, D_MODEL), F32), pltpu.VMEM((ROWS, D_MODEL), F32),
                        pltpu.SemaphoreType.DMA((N_DEV,))] + (ex.scratch if ex.n else []),
        input_output_aliases={5: 0} if n_prev else {},
        compiler_params=_params(("arbitrary",), 54),
    )(dz, w_in_g, x, dh1, pre_g, *(prev or ()), *ex_arrs)
    return res[0], res[1], res[2:]


def _grad_w(a, b, bn, shard_major, name, tk=1024, ex_arrs=(), ex_scatter=()):
    t_len, m = a.shape
    n = b.shape[1]
    n_j, n_k = n // bn, t_len // tk
    ex = _Exchange(ex_arrs, ex_scatter)

    def body(a_ref, b_ref, *refs):
        ex_in, o_ref, ex_out = refs[:ex.n], refs[ex.n], refs[ex.n + 1:2 * ex.n + 1]
        acc_s, ex_sems = refs[2 * ex.n + 1], refs[2 * ex.n + 2:]
        j, k = pl.program_id(0), pl.program_id(1)
        if ex.n:
            @pl.when(jnp.logical_and(j == 0, k == 0))
            def _():
                ex.start(ex_in, ex_out, ex_sems)

        @pl.when(k == 0)
        def _():
            acc_s[...] = jnp.zeros_like(acc_s)

        acc_s[...] += _dot_tn(a_ref[...], b_ref[...])

        @pl.when(k == n_k - 1)
        def _():
            o_ref[...] = acc_s[...].astype(BF16)

        if ex.n:
            @pl.when(jnp.logical_and(j == n_j - 1, k == n_k - 1))
            def _():
                ex.wait(ex_in, ex_out, ex_sems)

    if shard_major:
        out_spec, out_shape = pl.BlockSpec((None, m, bn), lambda j, k: (j, 0, 0)), SDS((n_j, m, bn), BF16)
    else:
        out_spec, out_shape = pl.BlockSpec((m, bn), lambda j, k: (0, j)), SDS((m, n), BF16)
    res = pl.pallas_call(
        body, name=name, grid=(n_j, n_k),
        in_specs=[pl.BlockSpec((tk, m), lambda j, k: (k, 0)), pl.BlockSpec((tk, bn), lambda j, k: (k, j))]
        + [ANY_SPEC] * ex.n,
        out_specs=[out_spec] + [ANY_SPEC] * ex.n, out_shape=[out_shape] + ex.out_shape,
        scratch_shapes=[pltpu.VMEM((m, bn), F32)] + (ex.scratch if ex.n else []),
        compiler_params=_params(("arbitrary", "arbitrary"), 40),
    )(a, b, *ex_arrs)
    return res[0], res[1:]


RS_ORDER = (3, 2, 5, 4, 7, 6, 1, 0)
RS_SLOTS = (0, 1, 2, 4, 6)


def _grad_w_in(hn, dz, ex_arrs, ex_scatter, tk=1024):
    t_len = hn.shape[0]
    n_k = t_len // tk
    ex = _Exchange(ex_arrs, ex_scatter)
    me_out = 4 * lax.axis_index("x") + 2 * lax.axis_index("y") + lax.axis_index("c")
    order = jnp.stack([me_out ^ k for k in RS_ORDER]).astype(jnp.int32)
    slots = jnp.stack([me_out ^ k for k in RS_SLOTS]).astype(jnp.int32)
    n_stage = 2

    def body(order_ref, a_ref, b_ref, *refs):
        ex_in, parts_hbm, ex_out = refs[:ex.n], refs[ex.n], refs[ex.n + 1:2 * ex.n + 1]
        acc_s, stage_s, rx_s, send_sems, recv_sems, loc_sem = refs[2 * ex.n + 1:2 * ex.n + 7]
        ex_sems = refs[2 * ex.n + 7:]
        j, k = pl.program_id(0), pl.program_id(1)
        x, y, c, me = _mesh_place()
        sib = _peer(x, y, c, SIBLING)[0]

        def send(jj):
            mask, src = RS_ORDER[jj], stage_s.at[jj % n_stage]
            if mask == 0:
                return pltpu.make_async_copy(src, parts_hbm.at[me], loc_sem.at[0])
            pair = (send_sems.at[mask], recv_sems.at[mask])
            if mask in ICI_MASKS or mask == SIBLING:
                return _remote(src, parts_hbm.at[me], *pair, _peer(x, y, c, mask)[0])
            return _remote(src, rx_s.at[mask // 2 - 1], *pair, sib)

        def from_sibling(mask):
            return _remote(stage_s.at[0], rx_s.at[mask // 2 - 1], send_sems.at[mask], recv_sems.at[mask], sib)

        @pl.when(jnp.logical_and(j == 0, k == 0))
        def _():
            ex.start(ex_in, ex_out, ex_sems)

        @pl.when(k == 0)
        def _():
            acc_s[...] = jnp.zeros_like(acc_s)

        acc_s[...] += _dot_tn(a_ref[...], b_ref[...])

        for jj in range(N_DEV):
            @pl.when(jnp.logical_and(j == jj, k == n_k - 1))
            def _(jj=jj):
                mask = RS_ORDER[jj]
                if jj >= n_stage:
                    send(jj - n_stage).wait_send()
                if mask in ICI_MASKS:
                    from_sibling(mask + 1).wait_recv()
                    stage_s[jj % n_stage] = (acc_s[...] + rx_s[mask // 2 - 1].astype(F32)).astype(BF16)
                else:
                    stage_s[jj % n_stage] = acc_s[...].astype(BF16)
                send(jj).start()

        @pl.when(jnp.logical_and(j == N_DEV - 1, k == n_k - 1))
        def _():
            for jj in range(N_DEV - n_stage, N_DEV):
                cp = send(jj)
                cp.wait() if RS_ORDER[jj] == 0 else cp.wait_send()
            for mask in DIRECT_MASKS:
                dev, lin = _peer(x, y, c, mask)
                _remote(stage_s.at[0], parts_hbm.at[lin], send_sems.at[mask], recv_sems.at[mask], dev).wait_recv()
            ex.wait(ex_in, ex_out, ex_sems)

    dma = lambda n: pltpu.SemaphoreType.DMA((n,))
    grid_spec = pltpu.PrefetchScalarGridSpec(
        num_scalar_prefetch=1, grid=(N_DEV, n_k),
        in_specs=[pl.BlockSpec((tk, D_MODEL), lambda j, k, order: (k, 0)),
                  pl.BlockSpec((tk, W_IN_SHARD), lambda j, k, order: (k, order[j]))] + [ANY_SPEC] * ex.n,
        out_specs=[ANY_SPEC] * (1 + ex.n),
        scratch_shapes=[pltpu.VMEM((D_MODEL, W_IN_SHARD), F32), pltpu.VMEM((n_stage, D_MODEL, W_IN_SHARD), BF16),
                        pltpu.VMEM((len(ICI_MASKS), D_MODEL, W_IN_SHARD), BF16), dma(N_DEV), dma(N_DEV), dma(1)]
        + ex.scratch)
    res = pl.pallas_call(
        body, name="grad_w_in", grid_spec=grid_spec,
        out_shape=[SDS((N_DEV, D_MODEL, W_IN_SHARD), BF16)] + ex.out_shape,
        compiler_params=_params(("arbitrary", "arbitrary"), 44),
    )(order, hn, dz, *ex_arrs)
    return res[0], slots, res[1:]


def _sum_parts(parts, name):
    def body(p_ref, o_ref):
        g = p_ref[0].astype(F32)
        for s in range(1, parts.shape[0]):
            g = g + p_ref[s].astype(F32)
        o_ref[...] = g

    return pl.pallas_call(body, name=name, out_shape=SDS(parts.shape[1:], F32))(parts)


def _adamw_math(g, w_ref, m_ref, v_ref, g_ref, d_ref, nm_ref, nv_ref):
    c1 = 1.0 - ADAM_B1 ** ADAM_STEP
    c2 = 1.0 - ADAM_B2 ** ADAM_STEP
    g_ref[...] = g
    nm = ADAM_B1 * m_ref[...] + (1.0 - ADAM_B1) * g
    nv = ADAM_B2 * v_ref[...] + (1.0 - ADAM_B2) * (g * g)
    nm_ref[...] = nm
    nv_ref[...] = nv
    d_ref[...] = -ADAM_LR * ((nm / c1) / (jnp.sqrt(nv / c2) + ADAM_EPS) + ADAM_WD * w_ref[...])


def _adamw(parts, w, m, v, name, tr):
    rows, cols = w.shape
    n_parts = parts.shape[0]

    def body(p_ref, *refs):
        g = p_ref[0].astype(F32)
        for s in range(1, n_parts):
            g = g + p_ref[s].astype(F32)
        _adamw_math(g, *refs)

    tile = pl.BlockSpec((tr, cols), lambda i: (i, 0))
    return pl.pallas_call(
        body, name=name, grid=(rows // tr,),
        in_specs=[pl.BlockSpec((n_parts, tr, cols), lambda i: (0, i, 0)), tile, tile, tile],
        out_specs=[tile] * 4, out_shape=[SDS((rows, cols), F32)] * 4,
        compiler_params=_params(("arbitrary",), 40),
    )(parts, w, m, v)


def _adamw_slots(parts, slots, w, m, v, name, tr):
    rows, cols = w.shape
    n_slots = slots.shape[0]

    def body(slots_ref, *refs):
        g = refs[0][...].astype(F32)
        for s in range(1, n_slots):
            g = g + refs[s][...].astype(F32)
        _adamw_math(g, *refs[n_slots:])

    tile = pl.BlockSpec((tr, cols), lambda i, slots: (i, 0))
    part = lambda s: pl.BlockSpec((None, tr, cols), lambda i, slots: (slots[s], i, 0))
    grid_spec = pltpu.PrefetchScalarGridSpec(
        num_scalar_prefetch=1, grid=(rows // tr,),
        in_specs=[part(s) for s in range(n_slots)] + [tile, tile, tile], out_specs=[tile] * 4)
    return pl.pallas_call(
        body, name=name, grid_spec=grid_spec, out_shape=[SDS((rows, cols), F32)] * 4,
        compiler_params=_params(("arbitrary",), 40),
    )(slots, *([parts] * n_slots), w, m, v)


PACKED = ("gmlp_ln_g", "gmlp_ln_b", "gmlp_ws", "gmlp_bs", "conv_b", "w_a", "b_a", "w_x", "b_x", "lam", "gmlp_out_g",
          "lru_out_g", "post_g")
WEIGHTS = ("pre_g", "w_in", "gmlp_ln_g", "gmlp_ln_b", "gmlp_ws", "gmlp_bs", "conv_w", "conv_b", "w_a", "b_a", "w_x",
           "b_x", "lam", "gmlp_out_g", "lru_out_g", "w_out", "post_g", "w_pe", "w_pg")
LANES = 128


PACK_ROWS = 3200
PACK_TILE = 640
IN_BWD_TILE = 256


def _pack(parts):
    rows = [p.reshape(-1, LANES) for p in parts]
    used = sum(r.shape[0] for r in rows)
    return jnp.concatenate(rows + [jnp.zeros((PACK_ROWS - used, LANES), F32)], axis=0)


def _pad_rows(a, rows):
    return jnp.concatenate([a, jnp.zeros((rows - a.shape[0],) + a.shape[1:], a.dtype)], axis=0)


def kernel(x, p, pre_g, w_in, gmlp_ln_g, gmlp_ln_b, gmlp_ws, gmlp_bs, conv_w, conv_b, w_a, b_a, w_x, b_x, lam, gmlp_out_g, lru_out_g, w_out, post_g, w_pe, w_pg, loss_target, m_pre_g, m_w_in, m_gmlp_ln_g, m_gmlp_ln_b, m_gmlp_ws, m_gmlp_bs, m_conv_w, m_conv_b, m_w_a, m_b_a, m_w_x, m_b_x, m_lam, m_gmlp_out_g, m_lru_out_g, m_w_out, m_post_g, m_w_pe, m_w_pg, v_pre_g, v_w_in, v_gmlp_ln_g, v_gmlp_ln_b, v_gmlp_ws, v_gmlp_bs, v_conv_w, v_conv_b, v_w_a, v_b_a, v_w_x, v_b_x, v_lam, v_gmlp_out_g, v_lru_out_g, v_w_out, v_post_g, v_w_pe, v_w_pg):
    args = dict(locals())
    weights = {n: args[n] for n in WEIGHTS}
    m_in = {n: args["m_" + n] for n in WEIGHTS}
    v_in = {n: args["v_" + n] for n in WEIGHTS}
    sm = {n: weights[n][0] for n in PACKED}
    shard_rows = D_MODEL // N_DEV
    xs, ps, tgt = x[0], p[0, 0], loss_target[0]

    vec = lambda a: a.reshape(1, -1)
    tril = jnp.tril(jnp.ones((CHUNK, CHUNK), dtype=bool))
    wm32 = jnp.where(tril[None], sm["gmlp_ws"], 0.0)
    wm, wm_t = wm32.astype(BF16), jnp.swapaxes(wm32, 1, 2).astype(BF16)
    bias = jnp.repeat(sm["gmlp_bs"].T, HEAD, axis=1)
    wax32 = jnp.concatenate([sm["w_a"], sm["w_x"]], axis=2)
    wax, wax_t = wax32.astype(BF16), jnp.swapaxes(wax32, 1, 2).astype(BF16)
    ln_g, ln_b = vec(sm["gmlp_ln_g"]), vec(sm["gmlp_ln_b"])
    post_g_v = vec(sm["post_g"])

    hn = _pre_norm(xs, pre_g)
    cw_shard = _pad_rows(conv_w.reshape(CONV_W, HEAD), ROWS)
    z, w_in_g, (w_out_g, w_pe_g, w_pg_g, cw_g) = _in_proj(
        hn, w_in[0].astype(BF16), [w_out[0].astype(BF16), w_pe[0].astype(BF16), w_pg[0].astype(BF16), cw_shard])
    w_out_f, w_pg_f = w_out_g.reshape(D_MODEL, D_MODEL), w_pg_g.reshape(D_MODEL, D_MODEL)
    cw_full = jnp.transpose(cw_g[:, :CONV_W, :], (1, 0, 2)).reshape(CONV_W, D_BR)
    mixer_consts = dict(cw=_pad_rows(cw_full, ROWS), cb=vec(sm["conv_b"]), ba=vec(sm["b_a"]), bx=vec(sm["b_x"]),
                        lam=vec(sm["lam"]), goa=vec(sm["gmlp_out_g"]), gob=vec(sm["lru_out_g"]))
    y, h = _mix_fwd(z, ln_g, ln_b, wm, bias, wax=wax, **mixer_consts)
    h1, ob = _out_proj(y, xs, w_out_f, post_g_v)
    dh2, dgl, h1b, loss_part, d_w_pe = _ple_loss(h1, ps, tgt, w_pg_f, w_pe_g)

    dh1, do, dy, d_post_g = _tail_bwd(dh2, dgl, ob, w_pg_f, w_out_f, post_g_v)
    d_w_out, _ = _grad_w(y, do, 512, False, "grad_w_out")
    d_w_pg, _ = _grad_w(h1b, dgl, 512, False, "grad_w_pg")
    (dz, vecs, d_ws, d_wax, d_bs), (parts_out, parts_pg, parts_pe) = _mix_bwd(
        z, dy, h, ln_g, ln_b, wm, wm_t, bias, wax=wax, wax_t=wax_t, **mixer_consts,
        ex_arrs=[d_w_out.reshape(N_DEV, shard_rows, D_MODEL), d_w_pg.reshape(N_DEV, shard_rows, D_MODEL), d_w_pe],
        ex_scatter=[True, True, True])

    small = {"gmlp_ln_g": vecs[V_LN_G], "gmlp_ln_b": vecs[V_LN_B], "gmlp_ws": d_ws, "gmlp_bs": d_bs,
             "conv_b": vecs[V_CONV_B], "w_a": d_wax[:, :, :HEAD], "b_a": vecs[V_B_A], "w_x": d_wax[:, :, HEAD:],
             "b_x": vecs[V_B_X], "lam": vecs[V_LAM], "gmlp_out_g": vecs[V_GOUT_A], "lru_out_g": vecs[V_GOUT_B],
             "post_g": d_post_g}
    small_part = _pack([small[n] for n in PACKED] + [loss_part]).reshape(N_DEV, PACK_ROWS // N_DEV, LANES)
    d_cw_blocks = jnp.transpose(vecs[V_CONV_W:V_CONV_W + CONV_W].reshape(CONV_W, N_DEV, HEAD), (1, 0, 2))
    d_cw_blocks = jnp.concatenate([d_cw_blocks, jnp.zeros((N_DEV, ROWS - CONV_W, HEAD), F32)], axis=1)
    parts_in, slots_in, (small_blocks, parts_cw) = _grad_w_in(
        hn, dz, ex_arrs=[small_part, d_cw_blocks], ex_scatter=[True, True])
    small_sum = _sum_parts(small_blocks, "sum_small")
    grad_x, d_pre_g, _ = _in_bwd(dz, w_in_g, xs, dh1, pre_g, 0, xs.shape[0] // IN_BWD_TILE, None, "in_bwd",
                                 tm=IN_BWD_TILE)
    pre_rows = D_MODEL // LANES
    small_all, parts_pre = _exchange([small_sum, d_pre_g.reshape(pre_rows, LANES)], False, "gather_small_grads")
    parts_small = small_all.reshape(1, PACK_ROWS, LANES)

    pad_cw = lambda a: _pad_rows(a.reshape(CONV_W, HEAD), ROWS)
    flat = lambda a: a.reshape(pre_rows, LANES)
    outs = {
        "w_in": _adamw_slots(parts_in, slots_in, w_in[0], m_w_in[0], v_w_in[0], "adamw_w_in", 256),
        "w_out": _adamw(parts_out, w_out[0], m_w_out[0], v_w_out[0], "adamw_w_out", 128),
        "w_pe": _adamw(parts_pe, w_pe[0], m_w_pe[0], v_w_pe[0], "adamw_w_pe", 256),
        "w_pg": _adamw(parts_pg, w_pg[0], m_w_pg[0], v_w_pg[0], "adamw_w_pg", 128),
        "conv_w": [a[:CONV_W] for a in
                   _adamw(parts_cw, pad_cw(conv_w), pad_cw(m_conv_w), pad_cw(v_conv_w), "adamw_conv_w", ROWS)],
        "pre_g": _adamw(parts_pre, flat(pre_g), flat(m_pre_g), flat(v_pre_g), "adamw_pre_g", pre_rows),
    }
    packed = _adamw(parts_small, _pack([weights[n] for n in PACKED]), _pack([m_in[n] for n in PACKED]),
                    _pack([v_in[n] for n in PACKED]), "adamw_small", PACK_TILE)
    row = 0
    for n in PACKED:
        n_rows = weights[n].size // LANES
        outs[n] = [packed[q][row:row + n_rows] for q in range(4)]
        row += n_rows
    loss = packed[0][row, 0]

    result = [loss, grad_x[None]]
    for q in range(4):
        result += [outs[n][q].reshape(weights[n].shape) for n in WEIGHTS]
    return tuple(result)
```

```python
import functools

import jax
import jax.numpy as jnp
from jax import lax
from jax.experimental import pallas as pl
from jax.experimental.pallas import tpu as pltpu

F32 = jnp.float32
BF16 = jnp.bfloat16
SDS = jax.ShapeDtypeStruct

D_MODEL = 2048
D_BR = 1024
D_IN = 5 * D_BR
D_PLE = 256
N_HEAD = 8
HEAD = 128
CHUNK = 128
ROWS = 8
N_GROUP = CHUNK // ROWS
N_DEV = 8
W_IN_SHARD = D_IN // N_DEV
EPS = 1e-6
LRU_C = 8.0
CONV_W = 4
MESH_AXES = ("x", "y", "c")
MIB = 1 << 20

ADAM_LR, ADAM_B1, ADAM_B2, ADAM_EPS, ADAM_WD, ADAM_STEP = 0.001, 0.9, 0.999, 1e-08, 0.01, 10

_GELU_C = 0.7978845608028654
_GELU_A = 0.044715

V_LN_G, V_LN_B, V_CONV_B, V_B_A, V_B_X, V_LAM, V_GOUT_A, V_GOUT_B, V_CONV_W = 0, 1, 2, 3, 4, 5, 6, 7, 8
N_VEC = 16


def _params(sem, vmem_mib):
    return pltpu.CompilerParams(dimension_semantics=sem, vmem_limit_bytes=int(vmem_mib * MIB))


def _sig(x):
    return 0.5 * jnp.tanh(0.5 * x) + 0.5


def _gelu(x):
    t = jnp.tanh(_GELU_C * (x + _GELU_A * x * x * x))
    return 0.5 * x * (1.0 + t), t


def _gelu_grad(x, t):
    return 0.5 * (1.0 + t) + 0.5 * x * (1.0 - t * t) * (_GELU_C * (1.0 + 3.0 * _GELU_A * x * x))


def _neg_expm1(y, exp_y):
    series = -y * (1.0 + y * (0.5 + y * (1.0 / 6.0)))
    return jnp.where(y > -0.01, series, 1.0 - exp_y)


def _softplus(x):
    return jnp.maximum(x, 0.0) + jnp.log(1.0 + jnp.exp(-jnp.abs(x)))


def _row_ids(width):
    return lax.broadcasted_iota(jnp.int32, (ROWS, width), 0)


def _shift_down(cur, prev, k, rid):
    return jnp.where(rid >= k, pltpu.roll(cur, k, 0), pltpu.roll(prev, k, 0))


def _shift_up(cur, nxt, k, rid):
    return jnp.where(rid < ROWS - k, pltpu.roll(cur, ROWS - k, 0), pltpu.roll(nxt, ROWS - k, 0))


def _mean_last(x):
    return jnp.mean(x, axis=-1, keepdims=True)


def _rows(g):
    return pl.ds(pl.multiple_of(g * ROWS, ROWS), ROWS)


TILE_ROWS = 16


def _tile_rows(q):
    return pl.ds(pl.multiple_of(q * TILE_ROWS, TILE_ROWS), TILE_ROWS)


UNROLL = 4


def _loop(n, body, init, unroll=UNROLL):
    def wide(i, carry):
        for u in range(unroll):
            carry = body(i * unroll + u, carry)
        return carry

    return lax.fori_loop(0, n // unroll, wide, init)


def _fold_rows(x):
    return x[0:ROWS, :] + x[ROWS:TILE_ROWS, :]


def _bcast_row(x, r):
    return jnp.broadcast_to(x[r:r + 1, :], x.shape)


def _dot(a, b):
    return jnp.dot(a, b, preferred_element_type=F32)


def _dot_nt(a, b):
    return lax.dot_general(a, b, (((1,), (1,)), ((), ())), preferred_element_type=F32)


def _dot_tn(a, b):
    return lax.dot_general(a, b, (((0,), (0,)), ((), ())), preferred_element_type=F32)


def _mesh_place():
    x, y, c = lax.axis_index("x"), lax.axis_index("y"), lax.axis_index("c")
    return x, y, c, 4 * x + 2 * y + c


def _peer(x, y, c, k):
    px = 1 - x if k & 4 else x
    py = 1 - y if k & 2 else y
    pc = 1 - c if k & 1 else c
    return (px, py, pc), 4 * px + 2 * py + pc


def _remote(src, dst, send_sem, recv_sem, dev):
    return pltpu.make_async_remote_copy(src_ref=src, dst_ref=dst, send_sem=send_sem, recv_sem=recv_sem, device_id=dev,
                                        device_id_type=pl.DeviceIdType.MESH)


ANY_SPEC = pl.BlockSpec(memory_space=pl.ANY)


class _Exchange:
    def __init__(self, arrs, scatter):
        self.n = len(arrs)
        self.scatter = tuple(scatter)
        self.out_shape = [SDS(a.shape if s else (N_DEV,) + a.shape, a.dtype) for a, s in zip(arrs, scatter)]
        self.scratch = [pltpu.SemaphoreType.DMA((self.n * N_DEV,)), pltpu.SemaphoreType.DMA((self.n * N_DEV,)),
                        pltpu.SemaphoreType.DMA((self.n,))]

    def _copies(self, ins, outs, sems):
        send_sems, recv_sems, local_sems = sems
        x, y, c, me = _mesh_place()
        local, sends, recvs = [], [], []
        for a in range(self.n):
            src = ins[a].at[me] if self.scatter[a] else ins[a]
            local.append(pltpu.make_async_copy(src, outs[a].at[me], local_sems.at[a]))
        for k in range(1, N_DEV):
            dev, lin = _peer(x, y, c, k)
            for a in range(self.n):
                src = ins[a].at[lin] if self.scatter[a] else ins[a]
                pair = (send_sems.at[a * N_DEV + k], recv_sems.at[a * N_DEV + k], dev)
                sends.append(_remote(src, outs[a].at[me], *pair))
                recvs.append(_remote(src, outs[a].at[lin], *pair))
        return local, sends, recvs

    def start(self, ins, outs, sems):
        local, sends, _ = self._copies(ins, outs, sems)
        for cp in local + sends:
            cp.start()

    def wait(self, ins, outs, sems):
        local, sends, recvs = self._copies(ins, outs, sems)
        for cp in recvs:
            cp.wait_recv()
        for cp in sends:
            cp.wait_send()
        for cp in local:
            cp.wait()


def _exchange(arrs, scatter, name):
    ex = _Exchange(arrs, [scatter] * len(arrs))
    n = ex.n

    def body(*refs):
        ins, outs, sems = refs[:n], refs[n:2 * n], refs[2 * n:]
        ex.start(ins, outs, sems)
        ex.wait(ins, outs, sems)

    return pl.pallas_call(
        body, name=name, out_shape=ex.out_shape, in_specs=[ANY_SPEC] * n, out_specs=[ANY_SPEC] * n,
        scratch_shapes=ex.scratch,
    )(*arrs)


def _pre_norm(x, pre_g, tm=512):
    t_len = x.shape[0]

    def body(x_ref, g_ref, hn_ref, hnt_ref):
        g = g_ref[...]

        def rows_body(q, _):
            rows = _tile_rows(q)
            xv = x_ref[rows, :]
            hn_ref[rows, :] = (xv * lax.rsqrt(_mean_last(xv * xv) + EPS) * g).astype(BF16)
            return 0

        _loop(tm // TILE_ROWS, rows_body, 0)
        hnt_ref[...] = hn_ref[...].T

    tile = pl.BlockSpec((tm, D_MODEL), lambda i: (i, 0))
    return pl.pallas_call(
        body, name="pre_norm", grid=(t_len // tm,),
        in_specs=[tile, pl.BlockSpec((1, D_MODEL), lambda i: (0, 0))],
        out_specs=[tile, pl.BlockSpec((D_MODEL, tm), lambda i: (0, i))],
        out_shape=[SDS((t_len, D_MODEL), BF16), SDS((D_MODEL, t_len), BF16)],
        compiler_params=_params(("arbitrary",), 24),
    )(x, pre_g)


AG_ORDER = (0, 1, 2, 4, 3, 5, 6, 7)
SIBLING = 1
ICI_MASKS = (2, 4, 6)
DIRECT_MASKS = (SIBLING,) + ICI_MASKS


def _in_proj(hn, w_shard, others, tm=1024):
    t_len = hn.shape[0]
    n_i = t_len // tm
    n_o = len(others)
    me_out = 4 * lax.axis_index("x") + 2 * lax.axis_index("y") + lax.axis_index("c")
    order = jnp.stack([me_out ^ k for k in AG_ORDER]).astype(jnp.int32)

    def body(order_ref, hn_ref, w_hbm, *refs):
        o_in = refs[:n_o]
        z_ref, wg_hbm = refs[n_o], refs[n_o + 1]
        o_out = refs[n_o + 2:2 * n_o + 2]
        wbuf, send_w, recv_w, fsend_w, frecv_w, send_o, recv_o, fsend_o, frecv_o, wb_sems, loc_sems = refs[2 * n_o + 2:]
        j, i = pl.program_id(0), pl.program_id(1)
        x, y, c, me = _mesh_place()
        sib = _peer(x, y, c, SIBLING)[0]

        def direct(k, a=None):
            dev, lin = _peer(x, y, c, k)
            if a is None:
                return (_remote(w_hbm, wbuf.at[me], send_w.at[k], recv_w.at[k], dev),
                        _remote(w_hbm, wbuf.at[lin], send_w.at[k], recv_w.at[k], dev))
            pair = (send_o.at[a * N_DEV + k], recv_o.at[a * N_DEV + k], dev)
            return _remote(o_in[a], o_out[a].at[me], *pair), _remote(o_in[a], o_out[a].at[lin], *pair)

        def passed(k, a=None):
            mine, theirs = _peer(x, y, c, k)[1], _peer(x, y, c, k ^ SIBLING)[1]
            if a is None:
                pair = (fsend_w.at[k], frecv_w.at[k], sib)
                return _remote(wbuf.at[mine], wbuf.at[mine], *pair), _remote(wbuf.at[theirs], wbuf.at[theirs], *pair)
            pair = (fsend_o.at[a * N_DEV + k], frecv_o.at[a * N_DEV + k], sib)
            return (_remote(o_out[a].at[mine], o_out[a].at[mine], *pair),
                    _remote(o_out[a].at[theirs], o_out[a].at[theirs], *pair))

        def own_copies():
            return [pltpu.make_async_copy(o_in[a], o_out[a].at[me], loc_sems.at[1 + a]) for a in range(n_o)]

        @pl.when(jnp.logical_and(j == 0, i == 0))
        def _():
            own = pltpu.make_async_copy(w_hbm, wbuf.at[me], loc_sems.at[0])
            own.start()
            for cp in own_copies():
                cp.start()
            for k in DIRECT_MASKS:
                direct(k)[0].start()
            for k in DIRECT_MASKS:
                for a in range(n_o):
                    direct(k, a)[0].start()
            own.wait()

        for jj in range(1, N_DEV):
            mask = AG_ORDER[jj]

            @pl.when(jnp.logical_and(j == jj, i == 0))
            def _(jj=jj, mask=mask):
                if mask in DIRECT_MASKS:
                    direct(mask)[1].wait_recv()
                    if mask in ICI_MASKS:
                        passed(mask)[0].start()
                else:
                    passed(mask ^ SIBLING)[1].wait_recv()
                late = jj - (N_DEV - len(ICI_MASKS))
                if late >= 0:
                    for a in range(n_o):
                        direct(ICI_MASKS[late], a)[1].wait_recv()
                        passed(ICI_MASKS[late], a)[0].start()

        slot = order_ref[j]

        @pl.when(i == 0)
        def _():
            pltpu.make_async_copy(wbuf.at[slot], wg_hbm.at[slot], wb_sems.at[j]).start()

        z_ref[...] = _dot(hn_ref[...], wbuf[slot])

        @pl.when(jnp.logical_and(j == N_DEV - 1, i == n_i - 1))
        def _():
            for a in range(n_o):
                direct(SIBLING, a)[1].wait_recv()
            for k in ICI_MASKS:
                for a in range(n_o):
                    passed(k, a)[1].wait_recv()
            for k in DIRECT_MASKS:
                direct(k)[0].wait_send()
                for a in range(n_o):
                    direct(k, a)[0].wait_send()
            for k in ICI_MASKS:
                passed(k)[0].wait_send()
                for a in range(n_o):
                    passed(k, a)[0].wait_send()
            for cp in own_copies():
                cp.wait()
            for jj in range(N_DEV):
                pltpu.make_async_copy(wbuf.at[0], wg_hbm.at[0], wb_sems.at[jj]).wait()

    dma = lambda n: pltpu.SemaphoreType.DMA((n,))
    grid_spec = pltpu.PrefetchScalarGridSpec(
        num_scalar_prefetch=1, grid=(N_DEV, n_i),
        in_specs=[pl.BlockSpec((tm, D_MODEL), lambda j, i, order: (i, 0)), ANY_SPEC] + [ANY_SPEC] * n_o,
        out_specs=[pl.BlockSpec((tm, W_IN_SHARD), lambda j, i, order: (i, order[j])), ANY_SPEC] + [ANY_SPEC] * n_o,
        scratch_shapes=[pltpu.VMEM((N_DEV, D_MODEL, W_IN_SHARD), BF16), dma(N_DEV), dma(N_DEV), dma(N_DEV), dma(N_DEV),
                        dma(n_o * N_DEV), dma(n_o * N_DEV), dma(n_o * N_DEV), dma(n_o * N_DEV), dma(N_DEV), dma(1 + n_o)])
    res = pl.pallas_call(
        body, name="in_proj", grid_spec=grid_spec,
        out_shape=[SDS((t_len, D_IN), F32), SDS((N_DEV, D_MODEL, W_IN_SHARD), BF16)]
        + [SDS((N_DEV,) + o.shape, o.dtype) for o in others],
        compiler_params=_params(("arbitrary", "arbitrary"), 48),
    )(order, hn, w_shard, *others)
    return res[0], res[1], res[2:]


def _conv_rows(cur, prev, cw_ref, cb, rid):
    acc = cw_ref[3:4, :] * cur + cb
    for k in range(1, CONV_W):
        acc = acc + cw_ref[3 - k:4 - k, :] * _shift_down(cur, prev, k, rid)
    return acc


def _lru_gates(pa, px, ba, bx, sp8, first_row):
    r = _sig(pa + ba)
    i = _sig(px + bx)
    la = -(r * sp8)
    a = jnp.exp(la)
    mult = jnp.where(first_row, 1.0, jnp.sqrt(_neg_expm1(2.0 * la, a * a)))
    return r, i, a, mult


def _mix_fwd(z, ln_g, ln_b, wm, bias, cw, cb, wax, ba, bx, lam, goa, gob):
    t_len = z.shape[0]
    n_chunk = t_len // CHUNK

    def body(z_ref, lng_ref, lnb_ref, wm_ref, bias_ref, cw_ref, cb_ref, wax_ref, ba_ref, bx_ref, lam_ref, goa_ref,
             gob_ref, y_ref, h_ref, vn_s, xc_s, mixed_s, pre_s, y_s, carry_s, halo_s):
        c_id = pl.program_id(0)
        rid = _row_ids(D_BR)

        @pl.when(c_id == 0)
        def _():
            carry_s[...] = jnp.zeros_like(carry_s)
            halo_s[...] = jnp.zeros_like(halo_s)

        lng, lnb, cb = lng_ref[...], lnb_ref[...], cb_ref[...]

        def phase1(g, prev):
            rows = _rows(g)
            vg, _ = _gelu(z_ref[rows, D_BR:2 * D_BR])
            xm = vg - _mean_last(vg)
            rs = lax.rsqrt(_mean_last(xm * xm) + EPS)
            vn_s[rows, :] = xm * rs * lng + lnb
            xb = z_ref[rows, 3 * D_BR:4 * D_BR]
            xc_s[rows, :] = _conv_rows(xb, prev, cw_ref, cb, rid)
            return xb

        halo_s[...] = _loop(N_GROUP, phase1, halo_s[...])

        for h in range(N_HEAD):
            cs = slice(h * HEAD, (h + 1) * HEAD)
            mixed_s[:, cs] = _dot(wm_ref[h], vn_s[:, cs].astype(BF16))
            pre = _dot(xc_s[:, cs].astype(BF16), wax_ref[h])
            pre_s[:, cs] = pre[:, :HEAD]
            pre_s[:, D_BR + h * HEAD:D_BR + (h + 1) * HEAD] = pre[:, HEAD:]

        ba, bx, goa, gob = ba_ref[...], bx_ref[...], goa_ref[...], gob_ref[...]
        sp8 = LRU_C * _softplus(-lam_ref[...])

        def phase3(g, carry):
            rows = _rows(g)
            ug, _ = _gelu(z_ref[rows, 0:D_BR])
            ga = z_ref[rows, 2 * D_BR:3 * D_BR]
            ya = ug * (mixed_s[rows, :] + bias_ref[rows, :]) * (ga * _sig(ga))
            y_s[rows, 0:D_BR] = ya * lax.rsqrt(_mean_last(ya * ya) + EPS) * goa

            first_row = jnp.logical_and(jnp.logical_and(c_id == 0, g == 0), rid == 0)
            _, i, a, mult = _lru_gates(pre_s[rows, 0:D_BR], pre_s[rows, D_BR:2 * D_BR], ba, bx, sp8, first_row)
            b = mult * i * xc_s[rows, :]
            for d in (1, 2, 4):
                a_sh = jnp.where(rid >= d, pltpu.roll(a, d, 0), 1.0)
                b_sh = jnp.where(rid >= d, pltpu.roll(b, d, 0), 0.0)
                b = a * b_sh + b
                a = a * a_sh
            hh = b + a * carry
            h_ref[rows, :] = hh
            gb = z_ref[rows, 4 * D_BR:5 * D_BR]
            yb = hh * (gb * _sig(gb))
            y_s[rows, D_BR:2 * D_BR] = yb * lax.rsqrt(_mean_last(yb * yb) + EPS) * gob
            return _bcast_row(hh, ROWS - 1)

        carry_s[...] = _loop(N_GROUP, phase3, carry_s[...])
        y_ref[...] = y_s[...].astype(BF16)

    vec = pl.BlockSpec((1, D_BR), lambda i: (0, 0))
    return pl.pallas_call(
        body, name="mix_fwd", grid=(n_chunk,),
        in_specs=[pl.BlockSpec((CHUNK, D_IN), lambda i: (i, 0)), vec, vec,
                  pl.BlockSpec((N_HEAD, HEAD, HEAD), lambda i: (0, 0, 0)),
                  pl.BlockSpec((CHUNK, D_BR), lambda i: (0, 0)),
                  pl.BlockSpec((ROWS, D_BR), lambda i: (0, 0)), vec,
                  pl.BlockSpec((N_HEAD, HEAD, 2 * HEAD), lambda i: (0, 0, 0)), vec, vec, vec, vec, vec],
        out_specs=[pl.BlockSpec((CHUNK, 2 * D_BR), lambda i: (i, 0)), pl.BlockSpec((CHUNK, D_BR), lambda i: (i, 0))],
        out_shape=[SDS((t_len, 2 * D_BR), BF16), SDS((t_len, D_BR), F32)],
        scratch_shapes=[pltpu.VMEM((CHUNK, D_BR), F32), pltpu.VMEM((CHUNK, D_BR), F32), pltpu.VMEM((CHUNK, D_BR), F32),
                        pltpu.VMEM((CHUNK, 2 * D_BR), F32), pltpu.VMEM((CHUNK, 2 * D_BR), F32),
                        pltpu.VMEM((ROWS, D_BR), F32), pltpu.VMEM((ROWS, D_BR), F32)],
        compiler_params=_params(("arbitrary",), 32),
    )(z, ln_g, ln_b, wm, bias, cw, cb, wax, ba, bx, lam, goa, gob)


def _load_weight(w_hbm, w_vmem, sem):
    @pl.when(pl.program_id(0) == 0)
    def _():
        cp = pltpu.make_async_copy(w_hbm, w_vmem, sem)
        cp.start()
        cp.wait()


def _out_proj(y, x, w_out, post_g, tm=512):
    t_len = y.shape[0]

    def body(y_ref, x_ref, w_hbm, g_ref, h1_ref, ob_ref, w_s, o_s, sem):
        _load_weight(w_hbm, w_s, sem)
        o_s[...] = _dot(y_ref[...], w_s[...])
        g = g_ref[...]

        def rows_body(q, _):
            rows = _tile_rows(q)
            o = o_s[rows, :]
            h1_ref[rows, :] = x_ref[rows, :] + o * lax.rsqrt(_mean_last(o * o) + EPS) * g
            ob_ref[rows, :] = o.astype(BF16)
            return 0

        _loop(tm // TILE_ROWS, rows_body, 0)

    tile = pl.BlockSpec((tm, D_MODEL), lambda i: (i, 0))
    return pl.pallas_call(
        body, name="out_proj", grid=(t_len // tm,),
        in_specs=[tile, tile, pl.BlockSpec(memory_space=pl.ANY), pl.BlockSpec((1, D_MODEL), lambda i: (0, 0))],
        out_specs=[tile, tile],
        out_shape=[SDS((t_len, D_MODEL), F32), SDS((t_len, D_MODEL), BF16)],
        scratch_shapes=[pltpu.VMEM((D_MODEL, D_MODEL), BF16), pltpu.VMEM((tm, D_MODEL), F32), pltpu.SemaphoreType.DMA],
        compiler_params=_params(("arbitrary",), 44),
    )(y, x, w_out, post_g)


def _ple_loss(h1, p, tgt, w_pg, w_pe_g, tm=256):
    t_len = h1.shape[0]
    n_tile = t_len // tm
    pe_shard = D_MODEL // N_DEV

    def body(h1_ref, p_ref, t_ref, w_hbm, wpe_ref, dh2_ref, dgl_ref, h1b_ref, loss_ref, dwpe_ref, w_s, pe_s, gl_s, acc_s,
             dpe_s, gpe_s, sem):
        _load_weight(w_hbm, w_s, sem)
        i = pl.program_id(0)

        @pl.when(i == 0)
        def _():
            acc_s[...] = jnp.zeros_like(acc_s)
            gpe_s[...] = jnp.zeros_like(gpe_s)

        h1b_ref[...] = h1_ref[...].astype(BF16)
        pb = p_ref[...].astype(BF16)
        for j in range(N_DEV):
            pe_s[:, j * pe_shard:(j + 1) * pe_shard] = _dot(pb, wpe_ref[j])
        gl_s[...] = _dot(h1b_ref[...], w_s[...])

        def rows_body(q, acc):
            rows = _tile_rows(q)
            pe = pe_s[rows, :]
            g = _sig(gl_s[rows, :])
            e = h1_ref[rows, :] + pe * g - t_ref[rows, :]
            dh2 = e * (1.0 / D_MODEL)
            dh2_ref[rows, :] = dh2
            dpe_s[rows, :] = (dh2 * g).astype(BF16)
            dgl_ref[rows, :] = (dh2 * pe * g * (1.0 - g)).astype(BF16)
            return acc + _fold_rows(e * e)

        acc_s[...] = _loop(tm // TILE_ROWS, rows_body, acc_s[...])
        gpe_s[...] += _dot_tn(pb, dpe_s[...])

        @pl.when(i == n_tile - 1)
        def _():
            loss_ref[...] = jnp.full(loss_ref.shape, 0.5 / D_MODEL * jnp.sum(acc_s[...]), F32)
            for j in range(N_DEV):
                dwpe_ref[j] = gpe_s[:, j * pe_shard:(j + 1) * pe_shard].astype(BF16)

    tile = pl.BlockSpec((tm, D_MODEL), lambda i: (i, 0))
    pe_blocks = pl.BlockSpec((N_DEV, D_PLE, pe_shard), lambda i: (0, 0, 0))
    return pl.pallas_call(
        body, name="ple_loss", grid=(n_tile,),
        in_specs=[tile, pl.BlockSpec((tm, D_PLE), lambda i: (i, 0)), tile, pl.BlockSpec(memory_space=pl.ANY), pe_blocks],
        out_specs=[tile, tile, tile, pl.BlockSpec((ROWS, HEAD), lambda i: (0, 0)), pe_blocks],
        out_shape=[SDS((t_len, D_MODEL), F32), SDS((t_len, D_MODEL), BF16), SDS((t_len, D_MODEL), BF16),
                   SDS((ROWS, HEAD), F32), SDS((N_DEV, D_PLE, pe_shard), BF16)],
        scratch_shapes=[pltpu.VMEM((D_MODEL, D_MODEL), BF16), pltpu.VMEM((tm, D_MODEL), F32),
                        pltpu.VMEM((tm, D_MODEL), F32), pltpu.VMEM((ROWS, D_MODEL), F32), pltpu.VMEM((tm, D_MODEL), BF16),
                        pltpu.VMEM((D_PLE, D_MODEL), F32), pltpu.SemaphoreType.DMA],
        compiler_params=_params(("arbitrary",), 48),
    )(h1, p, tgt, w_pg, w_pe_g)


def _tail_bwd(dh2, dgl, ob, w_pg, w_out, post_g, tm=256):
    t_len = dh2.shape[0]
    n_tile = t_len // tm

    def body(dh2_ref, dgl_ref, ob_ref, wpg_hbm, wout_hbm, g_ref, dh1_ref, do_ref, dy_ref, dg_ref, wpg_s, wout_s, t_s,
             acc_s, sems):
        _load_weight(wpg_hbm, wpg_s, sems.at[0])
        _load_weight(wout_hbm, wout_s, sems.at[1])
        i = pl.program_id(0)

        @pl.when(i == 0)
        def _():
            acc_s[...] = jnp.zeros_like(acc_s)

        t_s[...] = _dot_nt(dgl_ref[...], wpg_s[...])
        g = g_ref[...]

        def rows_body(q, acc):
            rows = _tile_rows(q)
            dh1 = dh2_ref[rows, :] + t_s[rows, :]
            dh1_ref[rows, :] = dh1
            o = ob_ref[rows, :].astype(F32)
            rr = lax.rsqrt(_mean_last(o * o) + EPS)
            on = o * rr
            dog = dh1 * g
            do_ref[rows, :] = (rr * (dog - on * _mean_last(dog * on))).astype(BF16)
            return acc + _fold_rows(dh1 * on)

        acc_s[...] = _loop(tm // TILE_ROWS, rows_body, acc_s[...])
        dy_ref[...] = _dot_nt(do_ref[...], wout_s[...]).astype(BF16)

        @pl.when(i == n_tile - 1)
        def _():
            dg_ref[...] = jnp.sum(acc_s[...], axis=0, keepdims=True)

    tile = pl.BlockSpec((tm, D_MODEL), lambda i: (i, 0))
    vec = pl.BlockSpec((1, D_MODEL), lambda i: (0, 0))
    hbm = pl.BlockSpec(memory_space=pl.ANY)
    return pl.pallas_call(
        body, name="tail_bwd", grid=(n_tile,),
        in_specs=[tile, tile, tile, hbm, hbm, vec],
        out_specs=[tile, tile, tile, vec],
        out_shape=[SDS((t_len, D_MODEL), F32), SDS((t_len, D_MODEL), BF16), SDS((t_len, D_MODEL), BF16),
                   SDS((1, D_MODEL), F32)],
        scratch_shapes=[pltpu.VMEM((D_MODEL, D_MODEL), BF16), pltpu.VMEM((D_MODEL, D_MODEL), BF16),
                        pltpu.VMEM((tm, D_MODEL), F32), pltpu.VMEM((ROWS, D_MODEL), F32), pltpu.SemaphoreType.DMA((2,))],
        compiler_params=_params(("arbitrary",), 48),
    )(dh2, dgl, ob, w_pg, w_out, post_g)


def _mix_bwd(z, dy, h, ln_g, ln_b, wm, wm_t, bias, cw, cb, wax, wax_t, ba, bx, lam, goa, gob, ex_arrs, ex_scatter):
    t_len = z.shape[0]
    n_chunk = t_len // CHUNK
    halo_blocks = CHUNK // ROWS
    ex = _Exchange(ex_arrs, ex_scatter)
    n_in, n_out, n_scratch = 19, 5, 17

    def body(*refs):
        (z_ref, zhalo_ref, dy_ref, h_ref, hhalo_ref, lng_ref, lnb_ref, wm_ref, wmt_ref, bias_ref, cw_ref, cb_ref,
         wax_ref, waxt_ref, ba_ref, bx_ref, lam_ref, goa_ref, gob_ref) = refs[:n_in]
        ex_in = refs[n_in:n_in + ex.n]
        dz_ref, vecs_ref, dws_ref, dwax_ref, dbs_ref = refs[n_in + ex.n:n_in + ex.n + n_out]
        ex_out = refs[n_in + ex.n + n_out:n_in + 2 * ex.n + n_out]
        (vn_s, vh_s, rs_s, xc_s, mixed_s, pre_s, dmix_s, dvn_s, dho_s, dxc_s, dpre_s, dz_s, acc_s, accdm_s,
         cg_s, ca_s, dxchalo_s) = refs[n_in + 2 * ex.n + n_out:n_in + 2 * ex.n + n_out + n_scratch]
        ex_sems = refs[n_in + 2 * ex.n + n_out + n_scratch:]
        step = pl.program_id(0)
        c_id = n_chunk - 1 - step
        rid = _row_ids(D_BR)
        first_chunk = c_id == 0

        @pl.when(step == 0)
        def _():
            ex.start(ex_in, ex_out, ex_sems)
            acc_s[...] = jnp.zeros_like(acc_s)
            accdm_s[...] = jnp.zeros_like(accdm_s)
            cg_s[...] = jnp.zeros_like(cg_s)
            ca_s[...] = jnp.zeros_like(ca_s)
            dxchalo_s[...] = jnp.zeros_like(dxchalo_s)
            dws_ref[...] = jnp.zeros_like(dws_ref)
            dwax_ref[...] = jnp.zeros_like(dwax_ref)

        lng, lnb, cb = lng_ref[...], lnb_ref[...], cb_ref[...]
        xb_halo = jnp.where(first_chunk, 0.0, zhalo_ref[...])
        h_halo = jnp.where(first_chunk, 0.0, hhalo_ref[...])

        def prev_rows(ref, cols, g, halo):
            before = ref[pl.ds(pl.multiple_of(jnp.maximum(g - 1, 0) * ROWS, ROWS), ROWS), cols]
            return jnp.where(g > 0, before, halo)

        def phase1(g, prev):
            rows = _rows(g)
            vg, _ = _gelu(z_ref[rows, D_BR:2 * D_BR])
            xm = vg - _mean_last(vg)
            rs = lax.rsqrt(_mean_last(xm * xm) + EPS)
            vh = xm * rs
            vh_s[rows, :] = vh
            rs_s[rows, :] = jnp.broadcast_to(rs, (ROWS, HEAD))
            vn_s[rows, :] = vh * lng + lnb
            xb = z_ref[rows, 3 * D_BR:4 * D_BR]
            xc_s[rows, :] = _conv_rows(xb, prev, cw_ref, cb, rid)
            return xb

        _loop(N_GROUP, phase1, xb_halo)

        for hd in range(N_HEAD):
            cs = slice(hd * HEAD, (hd + 1) * HEAD)
            mixed_s[:, cs] = _dot(wm_ref[hd], vn_s[:, cs].astype(BF16))
            pre = _dot(xc_s[:, cs].astype(BF16), wax_ref[hd])
            pre_s[:, cs] = pre[:, :HEAD]
            pre_s[:, D_BR + hd * HEAD:D_BR + (hd + 1) * HEAD] = pre[:, HEAD:]

        goa, gob = goa_ref[...], gob_ref[...]

        def phase3(g, _):
            rows = _rows(g)
            u = z_ref[rows, 0:D_BR]
            ug, tu = _gelu(u)
            ga = z_ref[rows, 2 * D_BR:3 * D_BR]
            sga = _sig(ga)
            sa = ga * sga
            mixed = mixed_s[rows, :] + bias_ref[rows, :]
            ya0 = ug * mixed
            ya = ya0 * sa
            ra = lax.rsqrt(_mean_last(ya * ya) + EPS)
            dyan = dy_ref[rows, 0:D_BR].astype(F32)
            acc_s[V_GOUT_A] += dyan * ya * ra
            dyg = dyan * goa
            dya = ra * dyg - ya * (ra * ra * ra) * _mean_last(dyg * ya)
            dya0 = dya * sa
            dz_s[rows, 2 * D_BR:3 * D_BR] = dya * ya0 * (sga * (1.0 + ga * (1.0 - sga)))
            dmix = dya0 * ug
            dmix_s[rows, :] = dmix
            accdm_s[rows, :] += dmix
            dz_s[rows, 0:D_BR] = dya0 * mixed * _gelu_grad(u, tu)

            hh = h_ref[rows, :]
            gb = z_ref[rows, 4 * D_BR:5 * D_BR]
            sgb = _sig(gb)
            sb = gb * sgb
            yb = hh * sb
            rb = lax.rsqrt(_mean_last(yb * yb) + EPS)
            dybn = dy_ref[rows, D_BR:2 * D_BR].astype(F32)
            acc_s[V_GOUT_B] += dybn * yb * rb
            dyg = dybn * gob
            dyb = rb * dyg - yb * (rb * rb * rb) * _mean_last(dyg * yb)
            dho_s[rows, :] = dyb * sb
            dz_s[rows, 4 * D_BR:5 * D_BR] = dyb * hh * (sgb * (1.0 + gb * (1.0 - sgb)))
            return 0

        _loop(N_GROUP, phase3, 0)

        for hd in range(N_HEAD):
            cs = slice(hd * HEAD, (hd + 1) * HEAD)
            dmb = dmix_s[:, cs].astype(BF16)
            dvn_s[:, cs] = _dot(wmt_ref[hd], dmb)
            dws_ref[hd] += _dot_nt(dmb, vn_s[:, cs].astype(BF16))

        def phase5(g, _):
            rows = _rows(g)
            dvn = dvn_s[rows, :]
            vh = vh_s[rows, :]
            acc_s[V_LN_G] += dvn * vh
            acc_s[V_LN_B] += dvn
            dvh = dvn * lng
            rs = rs_s[rows, 0:1]
            dvg = rs * (dvh - _mean_last(dvh) - vh * _mean_last(dvh * vh))
            v = z_ref[rows, D_BR:2 * D_BR]
            _, tv = _gelu(v)
            dz_s[rows, D_BR:2 * D_BR] = dvg * _gelu_grad(v, tv)
            return 0

        _loop(N_GROUP, phase5, 0)

        ba, bx = ba_ref[...], bx_ref[...]
        sp8 = LRU_C * _softplus(-lam_ref[...])

        def phase6(k, carry):
            cg, ca = carry
            g = N_GROUP - 1 - k
            rows = _rows(g)
            first_row = jnp.logical_and(jnp.logical_and(first_chunk, g == 0), rid == 0)
            r, i, a, mult = _lru_gates(pre_s[rows, 0:D_BR], pre_s[rows, D_BR:2 * D_BR], ba, bx, sp8, first_row)
            a_nx = jnp.where(rid < ROWS - 1, pltpu.roll(a, ROWS - 1, 0), ca)
            aa, bb = a_nx, dho_s[rows, :]
            for d in (1, 2, 4):
                a_sh = jnp.where(rid < ROWS - d, pltpu.roll(aa, ROWS - d, 0), 1.0)
                b_sh = jnp.where(rid < ROWS - d, pltpu.roll(bb, ROWS - d, 0), 0.0)
                bb = aa * b_sh + bb
                aa = aa * a_sh
            gg = bb + aa * cg
            hh = h_ref[rows, :]
            hprev = _shift_down(hh, prev_rows(h_ref, slice(None), g, h_halo), 1, rid)
            xc = xc_s[rows, :]
            gx = gg * xc
            dla = gg * hprev * a - jnp.where(first_row, 0.0, gx * i * (a * a) * lax.rsqrt(mult * mult))
            acc_s[V_LAM] += -(dla * r)
            dpa = -(dla * sp8) * r * (1.0 - r)
            dpx = gx * mult * i * (1.0 - i)
            acc_s[V_B_A] += dpa
            acc_s[V_B_X] += dpx
            dpre_s[rows, 0:D_BR] = dpa
            dpre_s[rows, D_BR:2 * D_BR] = dpx
            dxc_s[rows, :] = gg * mult * i
            return _bcast_row(gg, 0), _bcast_row(a, 0)

        cg, ca = _loop(N_GROUP, phase6, (cg_s[...], ca_s[...]))
        cg_s[...] = cg
        ca_s[...] = ca

        for hd in range(N_HEAD):
            cs = slice(hd * HEAD, (hd + 1) * HEAD)
            dpre = jnp.concatenate([dpre_s[:, cs], dpre_s[:, D_BR + hd * HEAD:D_BR + (hd + 1) * HEAD]], axis=1).astype(BF16)
            dxc_s[:, cs] += _dot(dpre, waxt_ref[hd])
            dwax_ref[hd] += _dot_tn(xc_s[:, cs].astype(BF16), dpre)

        def phase8(k, nxt):
            g = N_GROUP - 1 - k
            rows = _rows(g)
            dxc = dxc_s[rows, :]
            acc_s[V_CONV_B] += dxc
            xb = z_ref[rows, 3 * D_BR:4 * D_BR]
            dxb = cw_ref[3:4, :] * dxc
            acc_s[V_CONV_W + 3] += dxc * xb
            for j in range(1, CONV_W):
                later = _shift_up(dxc, nxt, j, rid)
                dxb = dxb + cw_ref[3 - j:4 - j, :] * later
                acc_s[V_CONV_W + 3 - j] += later * xb
            dz_s[rows, 3 * D_BR:4 * D_BR] = dxb
            return dxc

        dxchalo_s[...] = _loop(N_GROUP, phase8, dxchalo_s[...])
        dz_ref[...] = dz_s[...].astype(BF16)

        @pl.when(step == n_chunk - 1)
        def _():
            for v in range(N_VEC):
                vecs_ref[v:v + 1, :] = jnp.sum(acc_s[v], axis=0, keepdims=True)
            lam = lam_ref[...]
            vecs_ref[V_LAM:V_LAM + 1, :] = vecs_ref[V_LAM:V_LAM + 1, :] * (-LRU_C * _sig(-lam))
            tril = (lax.broadcasted_iota(jnp.int32, (HEAD, HEAD), 0) >= lax.broadcasted_iota(jnp.int32, (HEAD, HEAD), 1))
            ones = jnp.ones((ROWS, HEAD), BF16)
            for hd in range(N_HEAD):
                cs = slice(hd * HEAD, (hd + 1) * HEAD)
                dws_ref[hd] = jnp.where(tril, dws_ref[hd], 0.0)
                blk = accdm_s[:, cs]
                hi = blk.astype(BF16)
                lo = (blk - hi.astype(F32)).astype(BF16)
                dbs_ref[hd:hd + 1, :] = (_dot_nt(ones, hi) + _dot_nt(ones, lo))[0:1, :]
            ex.wait(ex_in, ex_out, ex_sems)

    vec = pl.BlockSpec((1, D_BR), lambda i: (0, 0))
    rev = lambda i: (n_chunk - 1 - i, 0)
    halo = lambda col: (lambda i: (jnp.maximum((n_chunk - 1 - i) * halo_blocks - 1, 0), col))
    full3 = lambda a, b, c: pl.BlockSpec((a, b, c), lambda i: (0, 0, 0))
    big = lambda w: pltpu.VMEM((CHUNK, w), F32)
    res = pl.pallas_call(
        body, name="mix_bwd", grid=(n_chunk,),
        in_specs=[pl.BlockSpec((CHUNK, D_IN), rev), pl.BlockSpec((ROWS, D_BR), halo(3)),
                  pl.BlockSpec((CHUNK, 2 * D_BR), rev), pl.BlockSpec((CHUNK, D_BR), rev),
                  pl.BlockSpec((ROWS, D_BR), halo(0)), vec, vec,
                  full3(N_HEAD, HEAD, HEAD), full3(N_HEAD, HEAD, HEAD),
                  pl.BlockSpec((CHUNK, D_BR), lambda i: (0, 0)), pl.BlockSpec((ROWS, D_BR), lambda i: (0, 0)), vec,
                  full3(N_HEAD, HEAD, 2 * HEAD), full3(N_HEAD, 2 * HEAD, HEAD), vec, vec, vec, vec, vec]
        + [ANY_SPEC] * ex.n,
        out_specs=[pl.BlockSpec((CHUNK, D_IN), rev), pl.BlockSpec((N_VEC, D_BR), lambda i: (0, 0)),
                   full3(N_HEAD, HEAD, HEAD), full3(N_HEAD, HEAD, 2 * HEAD),
                   pl.BlockSpec((N_HEAD, HEAD), lambda i: (0, 0))] + [ANY_SPEC] * ex.n,
        out_shape=[SDS((t_len, D_IN), BF16), SDS((N_VEC, D_BR), F32), SDS((N_HEAD, HEAD, HEAD), F32),
                   SDS((N_HEAD, HEAD, 2 * HEAD), F32), SDS((N_HEAD, HEAD), F32)] + ex.out_shape,
        scratch_shapes=[big(D_BR), big(D_BR), big(HEAD), big(D_BR), big(D_BR), big(2 * D_BR), big(D_BR), big(D_BR),
                        big(D_BR), big(D_BR), big(2 * D_BR), big(D_IN),
                        pltpu.VMEM((N_VEC, ROWS, D_BR), F32), big(D_BR),
                        pltpu.VMEM((ROWS, D_BR), F32), pltpu.VMEM((ROWS, D_BR), F32), pltpu.VMEM((ROWS, D_BR), F32)]
        + ex.scratch,
        compiler_params=_params(("arbitrary",), 48),
    )(z, z, dy, h, h, ln_g, ln_b, wm, wm_t, bias, cw, cb, wax, wax_t, ba, bx, lam, goa, gob, *ex_arrs)
    return res[:n_out], res[n_out:]


def _in_bwd(dz, w_in_g, x, dh1, pre_g, first_tile, n_tile, prev, name, ex_arrs=(), ex_scatter=(), tm=256):
    t_len = x.shape[0]
    ex = _Exchange(ex_arrs, ex_scatter)
    n_prev = 0 if prev is None else 2

    def body(dz_ref, w_hbm, x_ref, dh1_ref, g_ref, *refs):
        prev_refs, refs = refs[:n_prev], refs[n_prev:]
        ex_in, (gx_ref, dg_ref), ex_out = refs[:ex.n], refs[ex.n:ex.n + 2], refs[ex.n + 2:2 * ex.n + 2]
        w_s, t_s, dg_s, w_sems = refs[2 * ex.n + 2:2 * ex.n + 6]
        ex_sems = refs[2 * ex.n + 6:]
        i = pl.program_id(0)

        @pl.when(i == 0)
        def _():
            if ex.n:
                ex.start(ex_in, ex_out, ex_sems)
            loads = [pltpu.make_async_copy(w_hbm.at[s], w_s.at[:, s * W_IN_SHARD:(s + 1) * W_IN_SHARD], w_sems.at[s])
                     for s in range(N_DEV)]
            for cp in loads:
                cp.start()
            dg_s[...] = jnp.zeros_like(dg_s)
            for cp in loads:
                cp.wait()

        t_s[...] = _dot_nt(dz_ref[...], w_s[...])
        g = g_ref[...]

        def rows_body(q, acc):
            rows = _tile_rows(q)
            xv = x_ref[rows, :]
            r = lax.rsqrt(_mean_last(xv * xv) + EPS)
            xh = xv * r
            dhn = t_s[rows, :]
            dg = dhn * g
            gx_ref[rows, :] = dh1_ref[rows, :] + r * (dg - xh * _mean_last(dg * xh))
            return acc + _fold_rows(dhn * xh)

        dg_s[...] = _loop(tm // TILE_ROWS, rows_body, dg_s[...])

        @pl.when(i == n_tile - 1)
        def _():
            dg = jnp.sum(dg_s[...], axis=0, keepdims=True)
            dg_ref[...] = dg + prev_refs[1][...] if n_prev else dg
            if ex.n:
                ex.wait(ex_in, ex_out, ex_sems)

    tile = pl.BlockSpec((tm, D_MODEL), lambda i: (first_tile + i, 0))
    vec = pl.BlockSpec((1, D_MODEL), lambda i: (0, 0))
    prev_specs = [ANY_SPEC, vec] if n_prev else []
    res = pl.pallas_call(
        body, name=name, grid=(n_tile,),
        in_specs=[pl.BlockSpec((tm, D_IN), lambda i: (first_tile + i, 0)), ANY_SPEC, tile, tile, vec] + prev_specs
        + [ANY_SPEC] * ex.n,
        out_specs=[tile, vec] + [ANY_SPEC] * ex.n,
        out_shape=[SDS((t_len, D_MODEL), F32), SDS((1, D_MODEL), F32)] + ex.out_shape,
        scratch_shapes=[pltpu.VMEM((D_MODEL, D_IN), BF16), pltpu.VMEM((tm, D_MODEL), F32), pltpu.VMEM((ROWS, D_MODEL), F32),
                        pltpu.SemaphoreType.DMA((N_DEV,))] + (ex.scratch if ex.n else []),
        input_output_aliases={5: 0} if n_prev else {},
        compiler_params=_params(("arbitrary",), 54),
    )(dz, w_in_g, x, dh1, pre_g, *(prev or ()), *ex_arrs)
    return res[0], res[1], res[2:]


def _grad_w(a, b, bn, shard_major, name, tk=1024, ex_arrs=(), ex_scatter=()):
    t_len, m = a.shape
    n = b.shape[1]
    n_j, n_k = n // bn, t_len // tk
    ex = _Exchange(ex_arrs, ex_scatter)

    def body(a_ref, b_ref, *refs):
        ex_in, o_ref, ex_out = refs[:ex.n], refs[ex.n], refs[ex.n + 1:2 * ex.n + 1]
        acc_s, ex_sems = refs[2 * ex.n + 1], refs[2 * ex.n + 2:]
        j, k = pl.program_id(0), pl.program_id(1)
        if ex.n:
            @pl.when(jnp.logical_and(j == 0, k == 0))
            def _():
                ex.start(ex_in, ex_out, ex_sems)

        @pl.when(k == 0)
        def _():
            acc_s[...] = jnp.zeros_like(acc_s)

        acc_s[...] += _dot_tn(a_ref[...], b_ref[...])

        @pl.when(k == n_k - 1)
        def _():
            o_ref[...] = acc_s[...].astype(BF16)

        if ex.n:
            @pl.when(jnp.logical_and(j == n_j - 1, k == n_k - 1))
            def _():
                ex.wait(ex_in, ex_out, ex_sems)

    if shard_major:
        out_spec, out_shape = pl.BlockSpec((None, m, bn), lambda j, k: (j, 0, 0)), SDS((n_j, m, bn), BF16)
    else:
        out_spec, out_shape = pl.BlockSpec((m, bn), lambda j, k: (0, j)), SDS((m, n), BF16)
    res = pl.pallas_call(
        body, name=name, grid=(n_j, n_k),
        in_specs=[pl.BlockSpec((tk, m), lambda j, k: (k, 0)), pl.BlockSpec((tk, bn), lambda j, k: (k, j))]
        + [ANY_SPEC] * ex.n,
        out_specs=[out_spec] + [ANY_SPEC] * ex.n, out_shape=[out_shape] + ex.out_shape,
        scratch_shapes=[pltpu.VMEM((m, bn), F32)] + (ex.scratch if ex.n else []),
        compiler_params=_params(("arbitrary", "arbitrary"), 40),
    )(a, b, *ex_arrs)
    return res[0], res[1:]


RS_ORDER = (3, 2, 5, 4, 7, 6, 1, 0)
RS_SLOTS = (0, 1, 2, 4, 6)


def _grad_w_in(hn_t, dz, ex_arrs, ex_scatter, tk=1024):
    t_len = hn_t.shape[1]
    n_k = t_len // tk
    ex = _Exchange(ex_arrs, ex_scatter)
    me_out = 4 * lax.axis_index("x") + 2 * lax.axis_index("y") + lax.axis_index("c")
    order = jnp.stack([me_out ^ k for k in RS_ORDER]).astype(jnp.int32)
    slots = jnp.stack([me_out ^ k for k in RS_SLOTS]).astype(jnp.int32)
    n_stage = 2

    def body(order_ref, a_ref, b_ref, *refs):
        ex_in, parts_hbm, ex_out = refs[:ex.n], refs[ex.n], refs[ex.n + 1:2 * ex.n + 1]
        acc_s, stage_s, rx_s, send_sems, recv_sems, loc_sem = refs[2 * ex.n + 1:2 * ex.n + 7]
        ex_sems = refs[2 * ex.n + 7:]
        j, k = pl.program_id(0), pl.program_id(1)
        x, y, c, me = _mesh_place()
        sib = _peer(x, y, c, SIBLING)[0]

        def send(jj):
            mask, src = RS_ORDER[jj], stage_s.at[jj % n_stage]
            if mask == 0:
                return pltpu.make_async_copy(src, parts_hbm.at[me], loc_sem.at[0])
            pair = (send_sems.at[mask], recv_sems.at[mask])
            if mask in ICI_MASKS or mask == SIBLING:
                return _remote(src, parts_hbm.at[me], *pair, _peer(x, y, c, mask)[0])
            return _remote(src, rx_s.at[mask // 2 - 1], *pair, sib)

        def from_sibling(mask):
            return _remote(stage_s.at[0], rx_s.at[mask // 2 - 1], send_sems.at[mask], recv_sems.at[mask], sib)

        @pl.when(jnp.logical_and(j == 0, k == 0))
        def _():
            ex.start(ex_in, ex_out, ex_sems)

        @pl.when(k == 0)
        def _():
            acc_s[...] = jnp.zeros_like(acc_s)

        acc_s[...] += _dot(a_ref[...], b_ref[...])

        for jj in range(N_DEV):
            @pl.when(jnp.logical_and(j == jj, k == n_k - 1))
            def _(jj=jj):
                mask = RS_ORDER[jj]
                if jj >= n_stage:
                    send(jj - n_stage).wait_send()
                if mask in ICI_MASKS:
                    from_sibling(mask + 1).wait_recv()
                    stage_s[jj % n_stage] = (acc_s[...] + rx_s[mask // 2 - 1].astype(F32)).astype(BF16)
                else:
                    stage_s[jj % n_stage] = acc_s[...].astype(BF16)
                send(jj).start()

        @pl.when(jnp.logical_and(j == N_DEV - 1, k == n_k - 1))
        def _():
            for jj in range(N_DEV - n_stage, N_DEV):
                cp = send(jj)
                cp.wait() if RS_ORDER[jj] == 0 else cp.wait_send()
            for mask in DIRECT_MASKS:
                dev, lin = _peer(x, y, c, mask)
                _remote(stage_s.at[0], parts_hbm.at[lin], send_sems.at[mask], recv_sems.at[mask], dev).wait_recv()
            ex.wait(ex_in, ex_out, ex_sems)

    dma = lambda n: pltpu.SemaphoreType.DMA((n,))
    grid_spec = pltpu.PrefetchScalarGridSpec(
        num_scalar_prefetch=1, grid=(N_DEV, n_k),
        in_specs=[pl.BlockSpec((D_MODEL, tk), lambda j, k, order: (0, k)),
                  pl.BlockSpec((tk, W_IN_SHARD), lambda j, k, order: (k, order[j]))] + [ANY_SPEC] * ex.n,
        out_specs=[ANY_SPEC] * (1 + ex.n),
        scratch_shapes=[pltpu.VMEM((D_MODEL, W_IN_SHARD), F32), pltpu.VMEM((n_stage, D_MODEL, W_IN_SHARD), BF16),
                        pltpu.VMEM((len(ICI_MASKS), D_MODEL, W_IN_SHARD), BF16), dma(N_DEV), dma(N_DEV), dma(1)]
        + ex.scratch)
    res = pl.pallas_call(
        body, name="grad_w_in", grid_spec=grid_spec,
        out_shape=[SDS((N_DEV, D_MODEL, W_IN_SHARD), BF16)] + ex.out_shape,
        compiler_params=_params(("arbitrary", "arbitrary"), 44),
    )(order, hn_t, dz, *ex_arrs)
    return res[0], slots, res[1:]


def _sum_parts(parts, name):
    def body(p_ref, o_ref):
        g = p_ref[0].astype(F32)
        for s in range(1, parts.shape[0]):
            g = g + p_ref[s].astype(F32)
        o_ref[...] = g

    return pl.pallas_call(body, name=name, out_shape=SDS(parts.shape[1:], F32))(parts)


def _adamw_math(g, w_ref, m_ref, v_ref, g_ref, d_ref, nm_ref, nv_ref):
    c1 = 1.0 - ADAM_B1 ** ADAM_STEP
    c2 = 1.0 - ADAM_B2 ** ADAM_STEP
    g_ref[...] = g
    nm = ADAM_B1 * m_ref[...] + (1.0 - ADAM_B1) * g
    nv = ADAM_B2 * v_ref[...] + (1.0 - ADAM_B2) * (g * g)
    nm_ref[...] = nm
    nv_ref[...] = nv
    d_ref[...] = -ADAM_LR * ((nm / c1) / (jnp.sqrt(nv / c2) + ADAM_EPS) + ADAM_WD * w_ref[...])


def _adamw(parts, w, m, v, name, tr):
    rows, cols = w.shape
    n_parts = parts.shape[0]

    def body(p_ref, *refs):
        g = p_ref[0].astype(F32)
        for s in range(1, n_parts):
            g = g + p_ref[s].astype(F32)
        _adamw_math(g, *refs)

    tile = pl.BlockSpec((tr, cols), lambda i: (i, 0))
    return pl.pallas_call(
        body, name=name, grid=(rows // tr,),
        in_specs=[pl.BlockSpec((n_parts, tr, cols), lambda i: (0, i, 0)), tile, tile, tile],
        out_specs=[tile] * 4, out_shape=[SDS((rows, cols), F32)] * 4,
        compiler_params=_params(("arbitrary",), 40),
    )(parts, w, m, v)


def _adamw_slots(parts, slots, w, m, v, name, tr):
    rows, cols = w.shape
    n_slots = slots.shape[0]

    def body(slots_ref, *refs):
        g = refs[0][...].astype(F32)
        for s in range(1, n_slots):
            g = g + refs[s][...].astype(F32)
        _adamw_math(g, *refs[n_slots:])

    tile = pl.BlockSpec((tr, cols), lambda i, slots: (i, 0))
    part = lambda s: pl.BlockSpec((None, tr, cols), lambda i, slots: (slots[s], i, 0))
    grid_spec = pltpu.PrefetchScalarGridSpec(
        num_scalar_prefetch=1, grid=(rows // tr,),
        in_specs=[part(s) for s in range(n_slots)] + [tile, tile, tile], out_specs=[tile] * 4)
    return pl.pallas_call(
        body, name=name, grid_spec=grid_spec, out_shape=[SDS((rows, cols), F32)] * 4,
        compiler_params=_params(("arbitrary",), 40),
    )(slots, *([parts] * n_slots), w, m, v)


PACKED = ("gmlp_ln_g", "gmlp_ln_b", "gmlp_ws", "gmlp_bs", "conv_b", "w_a", "b_a", "w_x", "b_x", "lam", "gmlp_out_g",
          "lru_out_g", "post_g")
WEIGHTS = ("pre_g", "w_in", "gmlp_ln_g", "gmlp_ln_b", "gmlp_ws", "gmlp_bs", "conv_w", "conv_b", "w_a", "b_a", "w_x",
           "b_x", "lam", "gmlp_out_g", "lru_out_g", "w_out", "post_g", "w_pe", "w_pg")
LANES = 128


PACK_ROWS = 3200
PACK_TILE = 640
IN_BWD_TILE = 256


def _pack(parts):
    rows = [p.reshape(-1, LANES) for p in parts]
    used = sum(r.shape[0] for r in rows)
    return jnp.concatenate(rows + [jnp.zeros((PACK_ROWS - used, LANES), F32)], axis=0)


def _pad_rows(a, rows):
    return jnp.concatenate([a, jnp.zeros((rows - a.shape[0],) + a.shape[1:], a.dtype)], axis=0)


def kernel(x, p, pre_g, w_in, gmlp_ln_g, gmlp_ln_b, gmlp_ws, gmlp_bs, conv_w, conv_b, w_a, b_a, w_x, b_x, lam, gmlp_out_g, lru_out_g, w_out, post_g, w_pe, w_pg, loss_target, m_pre_g, m_w_in, m_gmlp_ln_g, m_gmlp_ln_b, m_gmlp_ws, m_gmlp_bs, m_conv_w, m_conv_b, m_w_a, m_b_a, m_w_x, m_b_x, m_lam, m_gmlp_out_g, m_lru_out_g, m_w_out, m_post_g, m_w_pe, m_w_pg, v_pre_g, v_w_in, v_gmlp_ln_g, v_gmlp_ln_b, v_gmlp_ws, v_gmlp_bs, v_conv_w, v_conv_b, v_w_a, v_b_a, v_w_x, v_b_x, v_lam, v_gmlp_out_g, v_lru_out_g, v_w_out, v_post_g, v_w_pe, v_w_pg):
    args = dict(locals())
    weights = {n: args[n] for n in WEIGHTS}
    m_in = {n: args["m_" + n] for n in WEIGHTS}
    v_in = {n: args["v_" + n] for n in WEIGHTS}
    sm = {n: weights[n][0] for n in PACKED}
    shard_rows = D_MODEL // N_DEV
    xs, ps, tgt = x[0], p[0, 0], loss_target[0]

    vec = lambda a: a.reshape(1, -1)
    tril = jnp.tril(jnp.ones((CHUNK, CHUNK), dtype=bool))
    wm32 = jnp.where(tril[None], sm["gmlp_ws"], 0.0)
    wm, wm_t = wm32.astype(BF16), jnp.swapaxes(wm32, 1, 2).astype(BF16)
    bias = jnp.repeat(sm["gmlp_bs"].T, HEAD, axis=1)
    wax32 = jnp.concatenate([sm["w_a"], sm["w_x"]], axis=2)
    wax, wax_t = wax32.astype(BF16), jnp.swapaxes(wax32, 1, 2).astype(BF16)
    ln_g, ln_b = vec(sm["gmlp_ln_g"]), vec(sm["gmlp_ln_b"])
    post_g_v = vec(sm["post_g"])

    hn, hn_t = _pre_norm(xs, pre_g)
    cw_shard = _pad_rows(conv_w.reshape(CONV_W, HEAD), ROWS)
    z, w_in_g, (w_out_g, w_pe_g, w_pg_g, cw_g) = _in_proj(
        hn, w_in[0].astype(BF16), [w_out[0].astype(BF16), w_pe[0].astype(BF16), w_pg[0].astype(BF16), cw_shard])
    w_out_f, w_pg_f = w_out_g.reshape(D_MODEL, D_MODEL), w_pg_g.reshape(D_MODEL, D_MODEL)
    cw_full = jnp.transpose(cw_g[:, :CONV_W, :], (1, 0, 2)).reshape(CONV_W, D_BR)
    mixer_consts = dict(cw=_pad_rows(cw_full, ROWS), cb=vec(sm["conv_b"]), ba=vec(sm["b_a"]), bx=vec(sm["b_x"]),
                        lam=vec(sm["lam"]), goa=vec(sm["gmlp_out_g"]), gob=vec(sm["lru_out_g"]))
    y, h = _mix_fwd(z, ln_g, ln_b, wm, bias, wax=wax, **mixer_consts)
    h1, ob = _out_proj(y, xs, w_out_f, post_g_v)
    dh2, dgl, h1b, loss_part, d_w_pe = _ple_loss(h1, ps, tgt, w_pg_f, w_pe_g)

    dh1, do, dy, d_post_g = _tail_bwd(dh2, dgl, ob, w_pg_f, w_out_f, post_g_v)
    d_w_out, _ = _grad_w(y, do, 512, False, "grad_w_out")
    d_w_pg, _ = _grad_w(h1b, dgl, 512, False, "grad_w_pg")
    (dz, vecs, d_ws, d_wax, d_bs), (parts_out, parts_pg, parts_pe) = _mix_bwd(
        z, dy, h, ln_g, ln_b, wm, wm_t, bias, wax=wax, wax_t=wax_t, **mixer_consts,
        ex_arrs=[d_w_out.reshape(N_DEV, shard_rows, D_MODEL), d_w_pg.reshape(N_DEV, shard_rows, D_MODEL), d_w_pe],
        ex_scatter=[True, True, True])

    small = {"gmlp_ln_g": vecs[V_LN_G], "gmlp_ln_b": vecs[V_LN_B], "gmlp_ws": d_ws, "gmlp_bs": d_bs,
             "conv_b": vecs[V_CONV_B], "w_a": d_wax[:, :, :HEAD], "b_a": vecs[V_B_A], "w_x": d_wax[:, :, HEAD:],
             "b_x": vecs[V_B_X], "lam": vecs[V_LAM], "gmlp_out_g": vecs[V_GOUT_A], "lru_out_g": vecs[V_GOUT_B],
             "post_g": d_post_g}
    small_part = _pack([small[n] for n in PACKED] + [loss_part]).reshape(N_DEV, PACK_ROWS // N_DEV, LANES)
    d_cw_blocks = jnp.transpose(vecs[V_CONV_W:V_CONV_W + CONV_W].reshape(CONV_W, N_DEV, HEAD), (1, 0, 2))
    d_cw_blocks = jnp.concatenate([d_cw_blocks, jnp.zeros((N_DEV, ROWS - CONV_W, HEAD), F32)], axis=1)
    parts_in, slots_in, (small_blocks, parts_cw) = _grad_w_in(
        hn_t, dz, ex_arrs=[small_part, d_cw_blocks], ex_scatter=[True, True])
    small_sum = _sum_parts(small_blocks, "sum_small")
    grad_x, d_pre_g, _ = _in_bwd(dz, w_in_g, xs, dh1, pre_g, 0, xs.shape[0] // IN_BWD_TILE, None, "in_bwd",
                                 tm=IN_BWD_TILE)
    pre_rows = D_MODEL // LANES
    small_all, parts_pre = _exchange([small_sum, d_pre_g.reshape(pre_rows, LANES)], False, "gather_small_grads")
    parts_small = small_all.reshape(1, PACK_ROWS, LANES)

    pad_cw = lambda a: _pad_rows(a.reshape(CONV_W, HEAD), ROWS)
    flat = lambda a: a.reshape(pre_rows, LANES)
    outs = {
        "w_in": _adamw_slots(parts_in, slots_in, w_in[0], m_w_in[0], v_w_in[0], "adamw_w_in", 256),
        "w_out": _adamw(parts_out, w_out[0], m_w_out[0], v_w_out[0], "adamw_w_out", 128),
        "w_pe": _adamw(parts_pe, w_pe[0], m_w_pe[0], v_w_pe[0], "adamw_w_pe", 256),
        "w_pg": _adamw(parts_pg, w_pg[0], m_w_pg[0], v_w_pg[0], "adamw_w_pg", 128),
        "conv_w": [a[:CONV_W] for a in
                   _adamw(parts_cw, pad_cw(conv_w), pad_cw(m_conv_w), pad_cw(v_conv_w), "adamw_conv_w", ROWS)],
        "pre_g": _adamw(parts_pre, flat(pre_g), flat(m_pre_g), flat(v_pre_g), "adamw_pre_g", pre_rows),
    }
    packed = _adamw(parts_small, _pack([weights[n] for n in PACKED]), _pack([m_in[n] for n in PACKED]),
                    _pack([v_in[n] for n in PACKED]), "adamw_small", PACK_TILE)
    row = 0
    for n in PACKED:
        n_rows = weights[n].size // LANES
        outs[n] = [packed[q][row:row + n_rows] for q in range(4)]
        row += n_rows
    loss = packed[0][row, 0]

    result = [loss, grad_x[None]]
    for q in range(4):
        result += [outs[n][q].reshape(weights[n].shape) for n in WEIGHTS]
    return tuple(result)
```

```python
import functools

import jax
import jax.numpy as jnp
from jax import lax
from jax.experimental import pallas as pl
from jax.experimental.pallas import tpu as pltpu

F32 = jnp.float32
BF16 = jnp.bfloat16
SDS = jax.ShapeDtypeStruct

D_MODEL = 2048
D_BR = 1024
D_IN = 5 * D_BR
D_PLE = 256
N_HEAD = 8
HEAD = 128
CHUNK = 128
ROWS = 8
N_GROUP = CHUNK // ROWS
N_DEV = 8
W_IN_SHARD = D_IN // N_DEV
EPS = 1e-6
LRU_C = 8.0
CONV_W = 4
MESH_AXES = ("x", "y", "c")
MIB = 1 << 20

ADAM_LR, ADAM_B1, ADAM_B2, ADAM_EPS, ADAM_WD, ADAM_STEP = 0.001, 0.9, 0.999, 1e-08, 0.01, 10

_GELU_C = 0.7978845608028654
_GELU_A = 0.044715

V_LN_G, V_LN_B, V_CONV_B, V_B_A, V_B_X, V_LAM, V_GOUT_A, V_GOUT_B, V_CONV_W = 0, 1, 2, 3, 4, 5, 6, 7, 8
N_VEC = 16


def _params(sem, vmem_mib):
    return pltpu.CompilerParams(dimension_semantics=sem, vmem_limit_bytes=int(vmem_mib * MIB))


def _sig(x):
    return 0.5 * jnp.tanh(0.5 * x) + 0.5


def _gelu(x):
    t = jnp.tanh(_GELU_C * (x + _GELU_A * x * x * x))
    return 0.5 * x * (1.0 + t), t


def _gelu_grad(x, t):
    return 0.5 * (1.0 + t) + 0.5 * x * (1.0 - t * t) * (_GELU_C * (1.0 + 3.0 * _GELU_A * x * x))


def _neg_expm1(y, exp_y):
    series = -y * (1.0 + y * (0.5 + y * (1.0 / 6.0)))
    return jnp.where(y > -0.01, series, 1.0 - exp_y)


def _softplus(x):
    return jnp.maximum(x, 0.0) + jnp.log(1.0 + jnp.exp(-jnp.abs(x)))


def _row_ids(width):
    return lax.broadcasted_iota(jnp.int32, (ROWS, width), 0)


def _shift_down(cur, prev, k, rid):
    return jnp.where(rid >= k, pltpu.roll(cur, k, 0), pltpu.roll(prev, k, 0))


def _shift_up(cur, nxt, k, rid):
    return jnp.where(rid < ROWS - k, pltpu.roll(cur, ROWS - k, 0), pltpu.roll(nxt, ROWS - k, 0))


def _mean_last(x):
    return jnp.mean(x, axis=-1, keepdims=True)


def _rows(g):
    return pl.ds(pl.multiple_of(g * ROWS, ROWS), ROWS)


TILE_ROWS = 16


def _tile_rows(q):
    return pl.ds(pl.multiple_of(q * TILE_ROWS, TILE_ROWS), TILE_ROWS)


UNROLL = 4


def _loop(n, body, init, unroll=UNROLL):
    def wide(i, carry):
        for u in range(unroll):
            carry = body(i * unroll + u, carry)
        return carry

    return lax.fori_loop(0, n // unroll, wide, init)


def _fold_rows(x):
    return x[0:ROWS, :] + x[ROWS:TILE_ROWS, :]


def _bcast_row(x, r):
    return jnp.broadcast_to(x[r:r + 1, :], x.shape)


def _dot(a, b):
    return jnp.dot(a, b, preferred_element_type=F32)


def _dot_nt(a, b):
    return lax.dot_general(a, b, (((1,), (1,)), ((), ())), preferred_element_type=F32)


def _dot_tn(a, b):
    return lax.dot_general(a, b, (((0,), (0,)), ((), ())), preferred_element_type=F32)


def _mesh_place():
    x, y, c = lax.axis_index("x"), lax.axis_index("y"), lax.axis_index("c")
    return x, y, c, 4 * x + 2 * y + c


def _peer(x, y, c, k):
    px = 1 - x if k & 4 else x
    py = 1 - y if k & 2 else y
    pc = 1 - c if k & 1 else c
    return (px, py, pc), 4 * px + 2 * py + pc


def _remote(src, dst, send_sem, recv_sem, dev):
    return pltpu.make_async_remote_copy(src_ref=src, dst_ref=dst, send_sem=send_sem, recv_sem=recv_sem, device_id=dev,
                                        device_id_type=pl.DeviceIdType.MESH)


ANY_SPEC = pl.BlockSpec(memory_space=pl.ANY)


class _Exchange:
    def __init__(self, arrs, scatter):
        self.n = len(arrs)
        self.scatter = tuple(scatter)
        self.out_shape = [SDS(a.shape if s else (N_DEV,) + a.shape, a.dtype) for a, s in zip(arrs, scatter)]
        self.scratch = [pltpu.SemaphoreType.DMA((self.n * N_DEV,)), pltpu.SemaphoreType.DMA((self.n * N_DEV,)),
                        pltpu.SemaphoreType.DMA((self.n,))]

    def _copies(self, ins, outs, sems):
        send_sems, recv_sems, local_sems = sems
        x, y, c, me = _mesh_place()
        local, sends, recvs = [], [], []
        for a in range(self.n):
            src = ins[a].at[me] if self.scatter[a] else ins[a]
            local.append(pltpu.make_async_copy(src, outs[a].at[me], local_sems.at[a]))
        for k in range(1, N_DEV):
            dev, lin = _peer(x, y, c, k)
            for a in range(self.n):
                src = ins[a].at[lin] if self.scatter[a] else ins[a]
                pair = (send_sems.at[a * N_DEV + k], recv_sems.at[a * N_DEV + k], dev)
                sends.append(_remote(src, outs[a].at[me], *pair))
                recvs.append(_remote(src, outs[a].at[lin], *pair))
        return local, sends, recvs

    def start(self, ins, outs, sems):
        local, sends, _ = self._copies(ins, outs, sems)
        for cp in local + sends:
            cp.start()

    def wait(self, ins, outs, sems):
        local, sends, recvs = self._copies(ins, outs, sems)
        for cp in recvs:
            cp.wait_recv()
        for cp in sends:
            cp.wait_send()
        for cp in local:
            cp.wait()


def _exchange(arrs, scatter, name):
    ex = _Exchange(arrs, [scatter] * len(arrs))
    n = ex.n

    def body(*refs):
        ins, outs, sems = refs[:n], refs[n:2 * n], refs[2 * n:]
        ex.start(ins, outs, sems)
        ex.wait(ins, outs, sems)

    return pl.pallas_call(
        body, name=name, out_shape=ex.out_shape, in_specs=[ANY_SPEC] * n, out_specs=[ANY_SPEC] * n,
        scratch_shapes=ex.scratch,
    )(*arrs)


def _pre_norm(x, pre_g, tm=512):
    t_len = x.shape[0]

    def body(x_ref, g_ref, hn_ref):
        g = g_ref[...]

        def rows_body(q, _):
            rows = _tile_rows(q)
            xv = x_ref[rows, :]
            hn_ref[rows, :] = (xv * lax.rsqrt(_mean_last(xv * xv) + EPS) * g).astype(BF16)
            return 0

        _loop(tm // TILE_ROWS, rows_body, 0)

    tile = pl.BlockSpec((tm, D_MODEL), lambda i: (i, 0))
    return pl.pallas_call(
        body, name="pre_norm", grid=(t_len // tm,),
        in_specs=[tile, pl.BlockSpec((1, D_MODEL), lambda i: (0, 0))], out_specs=tile,
        out_shape=SDS((t_len, D_MODEL), BF16),
        compiler_params=_params(("arbitrary",), 24),
    )(x, pre_g)


AG_ORDER = (0, 1, 2, 4, 3, 5, 6, 7)
SIBLING = 1
ICI_MASKS = (2, 4, 6)
DIRECT_MASKS = (SIBLING,) + ICI_MASKS


def _in_proj(hn, w_shard, others, tm=1024):
    t_len = hn.shape[0]
    n_i = t_len // tm
    n_o = len(others)
    me_out = 4 * lax.axis_index("x") + 2 * lax.axis_index("y") + lax.axis_index("c")
    order = jnp.stack([me_out ^ k for k in AG_ORDER]).astype(jnp.int32)

    def body(order_ref, hn_ref, w_hbm, *refs):
        o_in = refs[:n_o]
        z_ref, wg_hbm = refs[n_o], refs[n_o + 1]
        o_out = refs[n_o + 2:2 * n_o + 2]
        wbuf, send_w, recv_w, fsend_w, frecv_w, send_o, recv_o, fsend_o, frecv_o, wb_sems, loc_sems = refs[2 * n_o + 2:]
        j, i = pl.program_id(0), pl.program_id(1)
        x, y, c, me = _mesh_place()
        sib = _peer(x, y, c, SIBLING)[0]

        def direct(k, a=None):
            dev, lin = _peer(x, y, c, k)
            if a is None:
                return (_remote(w_hbm, wbuf.at[me], send_w.at[k], recv_w.at[k], dev),
                        _remote(w_hbm, wbuf.at[lin], send_w.at[k], recv_w.at[k], dev))
            pair = (send_o.at[a * N_DEV + k], recv_o.at[a * N_DEV + k], dev)
            return _remote(o_in[a], o_out[a].at[me], *pair), _remote(o_in[a], o_out[a].at[lin], *pair)

        def passed(k, a=None):
            mine, theirs = _peer(x, y, c, k)[1], _peer(x, y, c, k ^ SIBLING)[1]
            if a is None:
                pair = (fsend_w.at[k], frecv_w.at[k], sib)
                return _remote(wbuf.at[mine], wbuf.at[mine], *pair), _remote(wbuf.at[theirs], wbuf.at[theirs], *pair)
            pair = (fsend_o.at[a * N_DEV + k], frecv_o.at[a * N_DEV + k], sib)
            return (_remote(o_out[a].at[mine], o_out[a].at[mine], *pair),
                    _remote(o_out[a].at[theirs], o_out[a].at[theirs], *pair))

        def own_copies():
            return [pltpu.make_async_copy(o_in[a], o_out[a].at[me], loc_sems.at[1 + a]) for a in range(n_o)]

        @pl.when(jnp.logical_and(j == 0, i == 0))
        def _():
            own = pltpu.make_async_copy(w_hbm, wbuf.at[me], loc_sems.at[0])
            own.start()
            for cp in own_copies():
                cp.start()
            for k in DIRECT_MASKS:
                direct(k)[0].start()
            for k in DIRECT_MASKS:
                for a in range(n_o):
                    direct(k, a)[0].start()
            own.wait()

        for jj in range(1, N_DEV):
            mask = AG_ORDER[jj]

            @pl.when(jnp.logical_and(j == jj, i == 0))
            def _(jj=jj, mask=mask):
                if mask in DIRECT_MASKS:
                    direct(mask)[1].wait_recv()
                    if mask in ICI_MASKS:
                        passed(mask)[0].start()
                else:
                    passed(mask ^ SIBLING)[1].wait_recv()
                late = jj - (N_DEV - len(ICI_MASKS))
                if late >= 0:
                    for a in range(n_o):
                        direct(ICI_MASKS[late], a)[1].wait_recv()
                        passed(ICI_MASKS[late], a)[0].start()

        slot = order_ref[j]

        @pl.when(i == 0)
        def _():
            pltpu.make_async_copy(wbuf.at[slot], wg_hbm.at[slot], wb_sems.at[j]).start()

        z_ref[...] = _dot(hn_ref[...], wbuf[slot])

        @pl.when(jnp.logical_and(j == N_DEV - 1, i == n_i - 1))
        def _():
            for a in range(n_o):
                direct(SIBLING, a)[1].wait_recv()
            for k in ICI_MASKS:
                for a in range(n_o):
                    passed(k, a)[1].wait_recv()
            for k in DIRECT_MASKS:
                direct(k)[0].wait_send()
                for a in range(n_o):
                    direct(k, a)[0].wait_send()
            for k in ICI_MASKS:
                passed(k)[0].wait_send()
                for a in range(n_o):
                    passed(k, a)[0].wait_send()
            for cp in own_copies():
                cp.wait()
            for jj in range(N_DEV):
                pltpu.make_async_copy(wbuf.at[0], wg_hbm.at[0], wb_sems.at[jj]).wait()

    dma = lambda n: pltpu.SemaphoreType.DMA((n,))
    grid_spec = pltpu.PrefetchScalarGridSpec(
        num_scalar_prefetch=1, grid=(N_DEV, n_i),
        in_specs=[pl.BlockSpec((tm, D_MODEL), lambda j, i, order: (i, 0)), ANY_SPEC] + [ANY_SPEC] * n_o,
        out_specs=[pl.BlockSpec((tm, W_IN_SHARD), lambda j, i, order: (i, order[j])), ANY_SPEC] + [ANY_SPEC] * n_o,
        scratch_shapes=[pltpu.VMEM((N_DEV, D_MODEL, W_IN_SHARD), BF16), dma(N_DEV), dma(N_DEV), dma(N_DEV), dma(N_DEV),
                        dma(n_o * N_DEV), dma(n_o * N_DEV), dma(n_o * N_DEV), dma(n_o * N_DEV), dma(N_DEV), dma(1 + n_o)])
    res = pl.pallas_call(
        body, name="in_proj", grid_spec=grid_spec,
        out_shape=[SDS((t_len, D_IN), F32), SDS((N_DEV, D_MODEL, W_IN_SHARD), BF16)]
        + [SDS((N_DEV,) + o.shape, o.dtype) for o in others],
        compiler_params=_params(("arbitrary", "arbitrary"), 48),
    )(order, hn, w_shard, *others)
    return res[0], res[1], res[2:]


def _conv_rows(cur, prev, cw_ref, cb, rid):
    acc = cw_ref[3:4, :] * cur + cb
    for k in range(1, CONV_W):
        acc = acc + cw_ref[3 - k:4 - k, :] * _shift_down(cur, prev, k, rid)
    return acc


def _lru_gates(pa, px, ba, bx, sp8, first_row):
    r = _sig(pa + ba)
    i = _sig(px + bx)
    la = -(r * sp8)
    a = jnp.exp(la)
    mult = jnp.where(first_row, 1.0, jnp.sqrt(_neg_expm1(2.0 * la, a * a)))
    return r, i, a, mult


def _mix_fwd(z, ln_g, ln_b, wm, bias, cw, cb, wax, ba, bx, lam, goa, gob, ex_arrs, ex_scatter):
    t_len = z.shape[0]
    n_chunk = t_len // CHUNK
    ex = _Exchange(ex_arrs, ex_scatter)
    n_in, n_out, n_scratch = 13, 2, 7

    def body(*refs):
        (z_ref, lng_ref, lnb_ref, wm_ref, bias_ref, cw_ref, cb_ref, wax_ref, ba_ref, bx_ref, lam_ref, goa_ref,
         gob_ref) = refs[:n_in]
        ex_in = refs[n_in:n_in + ex.n]
        y_ref, h_ref = refs[n_in + ex.n:n_in + ex.n + n_out]
        ex_out = refs[n_in + ex.n + n_out:n_in + 2 * ex.n + n_out]
        vn_s, xc_s, mixed_s, pre_s, y_s, carry_s, halo_s = refs[n_in + 2 * ex.n + n_out:n_in + 2 * ex.n + n_out + n_scratch]
        ex_sems = refs[n_in + 2 * ex.n + n_out + n_scratch:]
        c_id = pl.program_id(0)
        rid = _row_ids(D_BR)

        @pl.when(c_id == 0)
        def _():
            ex.start(ex_in, ex_out, ex_sems)
            carry_s[...] = jnp.zeros_like(carry_s)
            halo_s[...] = jnp.zeros_like(halo_s)

        lng, lnb, cb = lng_ref[...], lnb_ref[...], cb_ref[...]

        def phase1(g, prev):
            rows = _rows(g)
            vg, _ = _gelu(z_ref[rows, D_BR:2 * D_BR])
            xm = vg - _mean_last(vg)
            rs = lax.rsqrt(_mean_last(xm * xm) + EPS)
            vn_s[rows, :] = xm * rs * lng + lnb
            xb = z_ref[rows, 3 * D_BR:4 * D_BR]
            xc_s[rows, :] = _conv_rows(xb, prev, cw_ref, cb, rid)
            return xb

        halo_s[...] = _loop(N_GROUP, phase1, halo_s[...])

        for h in range(N_HEAD):
            cs = slice(h * HEAD, (h + 1) * HEAD)
            mixed_s[:, cs] = _dot(wm_ref[h], vn_s[:, cs].astype(BF16))
            pre = _dot(xc_s[:, cs].astype(BF16), wax_ref[h])
            pre_s[:, cs] = pre[:, :HEAD]
            pre_s[:, D_BR + h * HEAD:D_BR + (h + 1) * HEAD] = pre[:, HEAD:]

        ba, bx, goa, gob = ba_ref[...], bx_ref[...], goa_ref[...], gob_ref[...]
        sp8 = LRU_C * _softplus(-lam_ref[...])

        def phase3(g, carry):
            rows = _rows(g)
            ug, _ = _gelu(z_ref[rows, 0:D_BR])
            ga = z_ref[rows, 2 * D_BR:3 * D_BR]
            ya = ug * (mixed_s[rows, :] + bias_ref[rows, :]) * (ga * _sig(ga))
            y_s[rows, 0:D_BR] = ya * lax.rsqrt(_mean_last(ya * ya) + EPS) * goa

            first_row = jnp.logical_and(jnp.logical_and(c_id == 0, g == 0), rid == 0)
            _, i, a, mult = _lru_gates(pre_s[rows, 0:D_BR], pre_s[rows, D_BR:2 * D_BR], ba, bx, sp8, first_row)
            b = mult * i * xc_s[rows, :]
            for d in (1, 2, 4):
                a_sh = jnp.where(rid >= d, pltpu.roll(a, d, 0), 1.0)
                b_sh = jnp.where(rid >= d, pltpu.roll(b, d, 0), 0.0)
                b = a * b_sh + b
                a = a * a_sh
            hh = b + a * carry
            h_ref[rows, :] = hh
            gb = z_ref[rows, 4 * D_BR:5 * D_BR]
            yb = hh * (gb * _sig(gb))
            y_s[rows, D_BR:2 * D_BR] = yb * lax.rsqrt(_mean_last(yb * yb) + EPS) * gob
            return _bcast_row(hh, ROWS - 1)

        carry_s[...] = _loop(N_GROUP, phase3, carry_s[...])
        y_ref[...] = y_s[...].astype(BF16)

        @pl.when(c_id == n_chunk - 1)
        def _():
            ex.wait(ex_in, ex_out, ex_sems)

    vec = pl.BlockSpec((1, D_BR), lambda i: (0, 0))
    res = pl.pallas_call(
        body, name="mix_fwd", grid=(n_chunk,),
        in_specs=[pl.BlockSpec((CHUNK, D_IN), lambda i: (i, 0)), vec, vec,
                  pl.BlockSpec((N_HEAD, HEAD, HEAD), lambda i: (0, 0, 0)),
                  pl.BlockSpec((CHUNK, D_BR), lambda i: (0, 0)),
                  pl.BlockSpec((ROWS, D_BR), lambda i: (0, 0)), vec,
                  pl.BlockSpec((N_HEAD, HEAD, 2 * HEAD), lambda i: (0, 0, 0)), vec, vec, vec, vec, vec]
        + [ANY_SPEC] * ex.n,
        out_specs=[pl.BlockSpec((CHUNK, 2 * D_BR), lambda i: (i, 0)), pl.BlockSpec((CHUNK, D_BR), lambda i: (i, 0))]
        + [ANY_SPEC] * ex.n,
        out_shape=[SDS((t_len, 2 * D_BR), BF16), SDS((t_len, D_BR), F32)] + ex.out_shape,
        scratch_shapes=[pltpu.VMEM((CHUNK, D_BR), F32), pltpu.VMEM((CHUNK, D_BR), F32), pltpu.VMEM((CHUNK, D_BR), F32),
                        pltpu.VMEM((CHUNK, 2 * D_BR), F32), pltpu.VMEM((CHUNK, 2 * D_BR), F32),
                        pltpu.VMEM((ROWS, D_BR), F32), pltpu.VMEM((ROWS, D_BR), F32)] + ex.scratch,
        compiler_params=_params(("arbitrary",), 32),
    )(z, ln_g, ln_b, wm, bias, cw, cb, wax, ba, bx, lam, goa, gob, *ex_arrs)
    return res[:n_out], res[n_out:]


def _load_weight(w_hbm, w_vmem, sem):
    @pl.when(pl.program_id(0) == 0)
    def _():
        cp = pltpu.make_async_copy(w_hbm, w_vmem, sem)
        cp.start()
        cp.wait()


def _out_proj(y, x, w_out, post_g, tm=512):
    t_len = y.shape[0]

    def body(y_ref, x_ref, w_hbm, g_ref, h1_ref, ob_ref, w_s, o_s, sem):
        _load_weight(w_hbm, w_s, sem)
        o_s[...] = _dot(y_ref[...], w_s[...])
        g = g_ref[...]

        def rows_body(q, _):
            rows = _tile_rows(q)
            o = o_s[rows, :]
            h1_ref[rows, :] = x_ref[rows, :] + o * lax.rsqrt(_mean_last(o * o) + EPS) * g
            ob_ref[rows, :] = o.astype(BF16)
            return 0

        _loop(tm // TILE_ROWS, rows_body, 0)

    tile = pl.BlockSpec((tm, D_MODEL), lambda i: (i, 0))
    return pl.pallas_call(
        body, name="out_proj", grid=(t_len // tm,),
        in_specs=[tile, tile, pl.BlockSpec(memory_space=pl.ANY), pl.BlockSpec((1, D_MODEL), lambda i: (0, 0))],
        out_specs=[tile, tile],
        out_shape=[SDS((t_len, D_MODEL), F32), SDS((t_len, D_MODEL), BF16)],
        scratch_shapes=[pltpu.VMEM((D_MODEL, D_MODEL), BF16), pltpu.VMEM((tm, D_MODEL), F32), pltpu.SemaphoreType.DMA],
        compiler_params=_params(("arbitrary",), 44),
    )(y, x, w_out, post_g)


def _ple_loss(h1, p, tgt, w_pg, w_pe_g, tm=256):
    t_len = h1.shape[0]
    n_tile = t_len // tm
    pe_shard = D_MODEL // N_DEV

    def body(h1_ref, p_ref, t_ref, w_hbm, wpe_ref, dh2_ref, dgl_ref, h1b_ref, loss_ref, dwpe_ref, w_s, pe_s, gl_s, acc_s,
             dpe_s, gpe_s, sem):
        _load_weight(w_hbm, w_s, sem)
        i = pl.program_id(0)

        @pl.when(i == 0)
        def _():
            acc_s[...] = jnp.zeros_like(acc_s)
            gpe_s[...] = jnp.zeros_like(gpe_s)

        h1b_ref[...] = h1_ref[...].astype(BF16)
        pb = p_ref[...].astype(BF16)
        for j in range(N_DEV):
            pe_s[:, j * pe_shard:(j + 1) * pe_shard] = _dot(pb, wpe_ref[j])
        gl_s[...] = _dot(h1b_ref[...], w_s[...])

        def rows_body(q, acc):
            rows = _tile_rows(q)
            pe = pe_s[rows, :]
            g = _sig(gl_s[rows, :])
            e = h1_ref[rows, :] + pe * g - t_ref[rows, :]
            dh2 = e * (1.0 / D_MODEL)
            dh2_ref[rows, :] = dh2
            dpe_s[rows, :] = (dh2 * g).astype(BF16)
            dgl_ref[rows, :] = (dh2 * pe * g * (1.0 - g)).astype(BF16)
            return acc + _fold_rows(e * e)

        acc_s[...] = _loop(tm // TILE_ROWS, rows_body, acc_s[...])
        gpe_s[...] += _dot_tn(pb, dpe_s[...])

        @pl.when(i == n_tile - 1)
        def _():
            loss_ref[...] = jnp.full(loss_ref.shape, 0.5 / D_MODEL * jnp.sum(acc_s[...]), F32)
            for j in range(N_DEV):
                dwpe_ref[j] = gpe_s[:, j * pe_shard:(j + 1) * pe_shard].astype(BF16)

    tile = pl.BlockSpec((tm, D_MODEL), lambda i: (i, 0))
    pe_blocks = pl.BlockSpec((N_DEV, D_PLE, pe_shard), lambda i: (0, 0, 0))
    return pl.pallas_call(
        body, name="ple_loss", grid=(n_tile,),
        in_specs=[tile, pl.BlockSpec((tm, D_PLE), lambda i: (i, 0)), tile, pl.BlockSpec(memory_space=pl.ANY), pe_blocks],
        out_specs=[tile, tile, tile, pl.BlockSpec((ROWS, HEAD), lambda i: (0, 0)), pe_blocks],
        out_shape=[SDS((t_len, D_MODEL), F32), SDS((t_len, D_MODEL), BF16), SDS((t_len, D_MODEL), BF16),
                   SDS((ROWS, HEAD), F32), SDS((N_DEV, D_PLE, pe_shard), BF16)],
        scratch_shapes=[pltpu.VMEM((D_MODEL, D_MODEL), BF16), pltpu.VMEM((tm, D_MODEL), F32),
                        pltpu.VMEM((tm, D_MODEL), F32), pltpu.VMEM((ROWS, D_MODEL), F32), pltpu.VMEM((tm, D_MODEL), BF16),
                        pltpu.VMEM((D_PLE, D_MODEL), F32), pltpu.SemaphoreType.DMA],
        compiler_params=_params(("arbitrary",), 48),
    )(h1, p, tgt, w_pg, w_pe_g)


def _tail_bwd(dh2, dgl, ob, w_pg, w_out, post_g, tm=256):
    t_len = dh2.shape[0]
    n_tile = t_len // tm

    def body(dh2_ref, dgl_ref, ob_ref, wpg_hbm, wout_hbm, g_ref, dh1_ref, do_ref, dy_ref, dg_ref, wpg_s, wout_s, t_s,
             acc_s, sems):
        _load_weight(wpg_hbm, wpg_s, sems.at[0])
        _load_weight(wout_hbm, wout_s, sems.at[1])
        i = pl.program_id(0)

        @pl.when(i == 0)
        def _():
            acc_s[...] = jnp.zeros_like(acc_s)

        t_s[...] = _dot_nt(dgl_ref[...], wpg_s[...])
        g = g_ref[...]

        def rows_body(q, acc):
            rows = _tile_rows(q)
            dh1 = dh2_ref[rows, :] + t_s[rows, :]
            dh1_ref[rows, :] = dh1
            o = ob_ref[rows, :].astype(F32)
            rr = lax.rsqrt(_mean_last(o * o) + EPS)
            on = o * rr
            dog = dh1 * g
            do_ref[rows, :] = (rr * (dog - on * _mean_last(dog * on))).astype(BF16)
            return acc + _fold_rows(dh1 * on)

        acc_s[...] = _loop(tm // TILE_ROWS, rows_body, acc_s[...])
        dy_ref[...] = _dot_nt(do_ref[...], wout_s[...]).astype(BF16)

        @pl.when(i == n_tile - 1)
        def _():
            dg_ref[...] = jnp.sum(acc_s[...], axis=0, keepdims=True)

    tile = pl.BlockSpec((tm, D_MODEL), lambda i: (i, 0))
    vec = pl.BlockSpec((1, D_MODEL), lambda i: (0, 0))
    hbm = pl.BlockSpec(memory_space=pl.ANY)
    return pl.pallas_call(
        body, name="tail_bwd", grid=(n_tile,),
        in_specs=[tile, tile, tile, hbm, hbm, vec],
        out_specs=[tile, tile, tile, vec],
        out_shape=[SDS((t_len, D_MODEL), F32), SDS((t_len, D_MODEL), BF16), SDS((t_len, D_MODEL), BF16),
                   SDS((1, D_MODEL), F32)],
        scratch_shapes=[pltpu.VMEM((D_MODEL, D_MODEL), BF16), pltpu.VMEM((D_MODEL, D_MODEL), BF16),
                        pltpu.VMEM((tm, D_MODEL), F32), pltpu.VMEM((ROWS, D_MODEL), F32), pltpu.SemaphoreType.DMA((2,))],
        compiler_params=_params(("arbitrary",), 48),
    )(dh2, dgl, ob, w_pg, w_out, post_g)


def _mix_bwd(z, dy, h, ln_g, ln_b, wm, wm_t, bias, cw, cb, wax, wax_t, ba, bx, lam, goa, gob, ex_arrs, ex_scatter):
    t_len = z.shape[0]
    n_chunk = t_len // CHUNK
    halo_blocks = CHUNK // ROWS
    ex = _Exchange(ex_arrs, ex_scatter)
    n_in, n_out, n_scratch = 19, 5, 17

    def body(*refs):
        (z_ref, zhalo_ref, dy_ref, h_ref, hhalo_ref, lng_ref, lnb_ref, wm_ref, wmt_ref, bias_ref, cw_ref, cb_ref,
         wax_ref, waxt_ref, ba_ref, bx_ref, lam_ref, goa_ref, gob_ref) = refs[:n_in]
        ex_in = refs[n_in:n_in + ex.n]
        dz_ref, vecs_ref, dws_ref, dwax_ref, dbs_ref = refs[n_in + ex.n:n_in + ex.n + n_out]
        ex_out = refs[n_in + ex.n + n_out:n_in + 2 * ex.n + n_out]
        (vn_s, vh_s, rs_s, xc_s, mixed_s, pre_s, dmix_s, dvn_s, dho_s, dxc_s, dpre_s, dz_s, acc_s, accdm_s,
         cg_s, ca_s, dxchalo_s) = refs[n_in + 2 * ex.n + n_out:n_in + 2 * ex.n + n_out + n_scratch]
        ex_sems = refs[n_in + 2 * ex.n + n_out + n_scratch:]
        step = pl.program_id(0)
        c_id = n_chunk - 1 - step
        rid = _row_ids(D_BR)
        first_chunk = c_id == 0

        @pl.when(step == 0)
        def _():
            ex.start(ex_in, ex_out, ex_sems)
            acc_s[...] = jnp.zeros_like(acc_s)
            accdm_s[...] = jnp.zeros_like(accdm_s)
            cg_s[...] = jnp.zeros_like(cg_s)
            ca_s[...] = jnp.zeros_like(ca_s)
            dxchalo_s[...] = jnp.zeros_like(dxchalo_s)
            dws_ref[...] = jnp.zeros_like(dws_ref)
            dwax_ref[...] = jnp.zeros_like(dwax_ref)

        lng, lnb, cb = lng_ref[...], lnb_ref[...], cb_ref[...]
        xb_halo = jnp.where(first_chunk, 0.0, zhalo_ref[...])
        h_halo = jnp.where(first_chunk, 0.0, hhalo_ref[...])

        def prev_rows(ref, cols, g, halo):
            before = ref[pl.ds(pl.multiple_of(jnp.maximum(g - 1, 0) * ROWS, ROWS), ROWS), cols]
            return jnp.where(g > 0, before, halo)

        def phase1(g, prev):
            rows = _rows(g)
            vg, _ = _gelu(z_ref[rows, D_BR:2 * D_BR])
            xm = vg - _mean_last(vg)
            rs = lax.rsqrt(_mean_last(xm * xm) + EPS)
            vh = xm * rs
            vh_s[rows, :] = vh
            rs_s[rows, :] = jnp.broadcast_to(rs, (ROWS, HEAD))
            vn_s[rows, :] = vh * lng + lnb
            xb = z_ref[rows, 3 * D_BR:4 * D_BR]
            xc_s[rows, :] = _conv_rows(xb, prev, cw_ref, cb, rid)
            return xb

        _loop(N_GROUP, phase1, xb_halo)

        for hd in range(N_HEAD):
            cs = slice(hd * HEAD, (hd + 1) * HEAD)
            mixed_s[:, cs] = _dot(wm_ref[hd], vn_s[:, cs].astype(BF16))
            pre = _dot(xc_s[:, cs].astype(BF16), wax_ref[hd])
            pre_s[:, cs] = pre[:, :HEAD]
            pre_s[:, D_BR + hd * HEAD:D_BR + (hd + 1) * HEAD] = pre[:, HEAD:]

        goa, gob = goa_ref[...], gob_ref[...]

        def phase3(g, _):
            rows = _rows(g)
            u = z_ref[rows, 0:D_BR]
            ug, tu = _gelu(u)
            ga = z_ref[rows, 2 * D_BR:3 * D_BR]
            sga = _sig(ga)
            sa = ga * sga
            mixed = mixed_s[rows, :] + bias_ref[rows, :]
            ya0 = ug * mixed
            ya = ya0 * sa
            ra = lax.rsqrt(_mean_last(ya * ya) + EPS)
            dyan = dy_ref[rows, 0:D_BR].astype(F32)
            acc_s[V_GOUT_A] += dyan * ya * ra
            dyg = dyan * goa
            dya = ra * dyg - ya * (ra * ra * ra) * _mean_last(dyg * ya)
            dya0 = dya * sa
            dz_s[rows, 2 * D_BR:3 * D_BR] = dya * ya0 * (sga * (1.0 + ga * (1.0 - sga)))
            dmix = dya0 * ug
            dmix_s[rows, :] = dmix
            accdm_s[rows, :] += dmix
            dz_s[rows, 0:D_BR] = dya0 * mixed * _gelu_grad(u, tu)

            hh = h_ref[rows, :]
            gb = z_ref[rows, 4 * D_BR:5 * D_BR]
            sgb = _sig(gb)
            sb = gb * sgb
            yb = hh * sb
            rb = lax.rsqrt(_mean_last(yb * yb) + EPS)
            dybn = dy_ref[rows, D_BR:2 * D_BR].astype(F32)
            acc_s[V_GOUT_B] += dybn * yb * rb
            dyg = dybn * gob
            dyb = rb * dyg - yb * (rb * rb * rb) * _mean_last(dyg * yb)
            dho_s[rows, :] = dyb * sb
            dz_s[rows, 4 * D_BR:5 * D_BR] = dyb * hh * (sgb * (1.0 + gb * (1.0 - sgb)))
            return 0

        _loop(N_GROUP, phase3, 0)

        for hd in range(N_HEAD):
            cs = slice(hd * HEAD, (hd + 1) * HEAD)
            dmb = dmix_s[:, cs].astype(BF16)
            dvn_s[:, cs] = _dot(wmt_ref[hd], dmb)
            dws_ref[hd] += _dot_nt(dmb, vn_s[:, cs].astype(BF16))

        def phase5(g, _):
            rows = _rows(g)
            dvn = dvn_s[rows, :]
            vh = vh_s[rows, :]
            acc_s[V_LN_G] += dvn * vh
            acc_s[V_LN_B] += dvn
            dvh = dvn * lng
            rs = rs_s[rows, 0:1]
            dvg = rs * (dvh - _mean_last(dvh) - vh * _mean_last(dvh * vh))
            v = z_ref[rows, D_BR:2 * D_BR]
            _, tv = _gelu(v)
            dz_s[rows, D_BR:2 * D_BR] = dvg * _gelu_grad(v, tv)
            return 0

        _loop(N_GROUP, phase5, 0)

        ba, bx = ba_ref[...], bx_ref[...]
        sp8 = LRU_C * _softplus(-lam_ref[...])

        def phase6(k, carry):
            cg, ca = carry
            g = N_GROUP - 1 - k
            rows = _rows(g)
            first_row = jnp.logical_and(jnp.logical_and(first_chunk, g == 0), rid == 0)
            r, i, a, mult = _lru_gates(pre_s[rows, 0:D_BR], pre_s[rows, D_BR:2 * D_BR], ba, bx, sp8, first_row)
            a_nx = jnp.where(rid < ROWS - 1, pltpu.roll(a, ROWS - 1, 0), ca)
            aa, bb = a_nx, dho_s[rows, :]
            for d in (1, 2, 4):
                a_sh = jnp.where(rid < ROWS - d, pltpu.roll(aa, ROWS - d, 0), 1.0)
                b_sh = jnp.where(rid < ROWS - d, pltpu.roll(bb, ROWS - d, 0), 0.0)
                bb = aa * b_sh + bb
                aa = aa * a_sh
            gg = bb + aa * cg
            hh = h_ref[rows, :]
            hprev = _shift_down(hh, prev_rows(h_ref, slice(None), g, h_halo), 1, rid)
            xc = xc_s[rows, :]
            gx = gg * xc
            dla = gg * hprev * a - jnp.where(first_row, 0.0, gx * i * (a * a) * lax.rsqrt(mult * mult))
            acc_s[V_LAM] += -(dla * r)
            dpa = -(dla * sp8) * r * (1.0 - r)
            dpx = gx * mult * i * (1.0 - i)
            acc_s[V_B_A] += dpa
            acc_s[V_B_X] += dpx
            dpre_s[rows, 0:D_BR] = dpa
            dpre_s[rows, D_BR:2 * D_BR] = dpx
            dxc_s[rows, :] = gg * mult * i
            return _bcast_row(gg, 0), _bcast_row(a, 0)

        cg, ca = _loop(N_GROUP, phase6, (cg_s[...], ca_s[...]))
        cg_s[...] = cg
        ca_s[...] = ca

        for hd in range(N_HEAD):
            cs = slice(hd * HEAD, (hd + 1) * HEAD)
            dpre = jnp.concatenate([dpre_s[:, cs], dpre_s[:, D_BR + hd * HEAD:D_BR + (hd + 1) * HEAD]], axis=1).astype(BF16)
            dxc_s[:, cs] += _dot(dpre, waxt_ref[hd])
            dwax_ref[hd] += _dot_tn(xc_s[:, cs].astype(BF16), dpre)

        def phase8(k, nxt):
            g = N_GROUP - 1 - k
            rows = _rows(g)
            dxc = dxc_s[rows, :]
            acc_s[V_CONV_B] += dxc
            xb = z_ref[rows, 3 * D_BR:4 * D_BR]
            dxb = cw_ref[3:4, :] * dxc
            acc_s[V_CONV_W + 3] += dxc * xb
            for j in range(1, CONV_W):
                later = _shift_up(dxc, nxt, j, rid)
                dxb = dxb + cw_ref[3 - j:4 - j, :] * later
                acc_s[V_CONV_W + 3 - j] += later * xb
            dz_s[rows, 3 * D_BR:4 * D_BR] = dxb
            return dxc

        dxchalo_s[...] = _loop(N_GROUP, phase8, dxchalo_s[...])
        dz_ref[...] = dz_s[...].astype(BF16)

        @pl.when(step == n_chunk - 1)
        def _():
            for v in range(N_VEC):
                vecs_ref[v:v + 1, :] = jnp.sum(acc_s[v], axis=0, keepdims=True)
            lam = lam_ref[...]
            vecs_ref[V_LAM:V_LAM + 1, :] = vecs_ref[V_LAM:V_LAM + 1, :] * (-LRU_C * _sig(-lam))
            tril = (lax.broadcasted_iota(jnp.int32, (HEAD, HEAD), 0) >= lax.broadcasted_iota(jnp.int32, (HEAD, HEAD), 1))
            ones = jnp.ones((ROWS, HEAD), BF16)
            for hd in range(N_HEAD):
                cs = slice(hd * HEAD, (hd + 1) * HEAD)
                dws_ref[hd] = jnp.where(tril, dws_ref[hd], 0.0)
                blk = accdm_s[:, cs]
                hi = blk.astype(BF16)
                lo = (blk - hi.astype(F32)).astype(BF16)
                dbs_ref[hd:hd + 1, :] = (_dot_nt(ones, hi) + _dot_nt(ones, lo))[0:1, :]
            ex.wait(ex_in, ex_out, ex_sems)

    vec = pl.BlockSpec((1, D_BR), lambda i: (0, 0))
    rev = lambda i: (n_chunk - 1 - i, 0)
    halo = lambda col: (lambda i: (jnp.maximum((n_chunk - 1 - i) * halo_blocks - 1, 0), col))
    full3 = lambda a, b, c: pl.BlockSpec((a, b, c), lambda i: (0, 0, 0))
    big = lambda w: pltpu.VMEM((CHUNK, w), F32)
    res = pl.pallas_call(
        body, name="mix_bwd", grid=(n_chunk,),
        in_specs=[pl.BlockSpec((CHUNK, D_IN), rev), pl.BlockSpec((ROWS, D_BR), halo(3)),
                  pl.BlockSpec((CHUNK, 2 * D_BR), rev), pl.BlockSpec((CHUNK, D_BR), rev),
                  pl.BlockSpec((ROWS, D_BR), halo(0)), vec, vec,
                  full3(N_HEAD, HEAD, HEAD), full3(N_HEAD, HEAD, HEAD),
                  pl.BlockSpec((CHUNK, D_BR), lambda i: (0, 0)), pl.BlockSpec((ROWS, D_BR), lambda i: (0, 0)), vec,
                  full3(N_HEAD, HEAD, 2 * HEAD), full3(N_HEAD, 2 * HEAD, HEAD), vec, vec, vec, vec, vec]
        + [ANY_SPEC] * ex.n,
        out_specs=[pl.BlockSpec((CHUNK, D_IN), rev), pl.BlockSpec((N_VEC, D_BR), lambda i: (0, 0)),
                   full3(N_HEAD, HEAD, HEAD), full3(N_HEAD, HEAD, 2 * HEAD),
                   pl.BlockSpec((N_HEAD, HEAD), lambda i: (0, 0))] + [ANY_SPEC] * ex.n,
        out_shape=[SDS((t_len, D_IN), BF16), SDS((N_VEC, D_BR), F32), SDS((N_HEAD, HEAD, HEAD), F32),
                   SDS((N_HEAD, HEAD, 2 * HEAD), F32), SDS((N_HEAD, HEAD), F32)] + ex.out_shape,
        scratch_shapes=[big(D_BR), big(D_BR), big(HEAD), big(D_BR), big(D_BR), big(2 * D_BR), big(D_BR), big(D_BR),
                        big(D_BR), big(D_BR), big(2 * D_BR), big(D_IN),
                        pltpu.VMEM((N_VEC, ROWS, D_BR), F32), big(D_BR),
                        pltpu.VMEM((ROWS, D_BR), F32), pltpu.VMEM((ROWS, D_BR), F32), pltpu.VMEM((ROWS, D_BR), F32)]
        + ex.scratch,
        compiler_params=_params(("arbitrary",), 48),
    )(z, z, dy, h, h, ln_g, ln_b, wm, wm_t, bias, cw, cb, wax, wax_t, ba, bx, lam, goa, gob, *ex_arrs)
    return res[:n_out], res[n_out:]


def _in_bwd(dz, w_in_g, x, dh1, pre_g, first_tile, n_tile, prev, name, ex_arrs=(), ex_scatter=(), tm=256):
    t_len = x.shape[0]
    ex = _Exchange(ex_arrs, ex_scatter)
    n_prev = 0 if prev is None else 2

    def body(dz_ref, w_hbm, x_ref, dh1_ref, g_ref, *refs):
        prev_refs, refs = refs[:n_prev], refs[n_prev:]
        ex_in, (gx_ref, dg_ref), ex_out = refs[:ex.n], refs[ex.n:ex.n + 2], refs[ex.n + 2:2 * ex.n + 2]
        w_s, t_s, dg_s, w_sems = refs[2 * ex.n + 2:2 * ex.n + 6]
        ex_sems = refs[2 * ex.n + 6:]
        i = pl.program_id(0)

        @pl.when(i == 0)
        def _():
            if ex.n:
                ex.start(ex_in, ex_out, ex_sems)
            loads = [pltpu.make_async_copy(w_hbm.at[s], w_s.at[:, s * W_IN_SHARD:(s + 1) * W_IN_SHARD], w_sems.at[s])
                     for s in range(N_DEV)]
            for cp in loads:
                cp.start()
            dg_s[...] = jnp.zeros_like(dg_s)
            for cp in loads:
                cp.wait()

        t_s[...] = _dot_nt(dz_ref[...], w_s[...])
        g = g_ref[...]

        def rows_body(q, acc):
            rows = _tile_rows(q)
            xv = x_ref[rows, :]
            r = lax.rsqrt(_mean_last(xv * xv) + EPS)
            xh = xv * r
            dhn = t_s[rows, :]
            dg = dhn * g
            gx_ref[rows, :] = dh1_ref[rows, :] + r * (dg - xh * _mean_last(dg * xh))
            return acc + _fold_rows(dhn * xh)

        dg_s[...] = _loop(tm // TILE_ROWS, rows_body, dg_s[...])

        @pl.when(i == n_tile - 1)
        def _():
            dg = jnp.sum(dg_s[...], axis=0, keepdims=True)
            dg_ref[...] = dg + prev_refs[1][...] if n_prev else dg
            if ex.n:
                ex.wait(ex_in, ex_out, ex_sems)

    tile = pl.BlockSpec((tm, D_MODEL), lambda i: (first_tile + i, 0))
    vec = pl.BlockSpec((1, D_MODEL), lambda i: (0, 0))
    prev_specs = [ANY_SPEC, vec] if n_prev else []
    res = pl.pallas_call(
        body, name=name, grid=(n_tile,),
        in_specs=[pl.BlockSpec((tm, D_IN), lambda i: (first_tile + i, 0)), ANY_SPEC, tile, tile, vec] + prev_specs
        + [ANY_SPEC] * ex.n,
        out_specs=[tile, vec] + [ANY_SPEC] * ex.n,
        out_shape=[SDS((t_len, D_MODEL), F32), SDS((1, D_MODEL), F32)] + ex.out_shape,
        scratch_shapes=[pltpu.VMEM((D_MODEL, D_IN), BF16), pltpu.VMEM((tm, D_MODEL), F32), pltpu.VMEM((ROWS, D_MODEL), F32),
                        pltpu.SemaphoreType.DMA((N_DEV,))] + (ex.scratch if ex.n else []),
        input_output_aliases={5: 0} if n_prev else {},
        compiler_params=_params(("arbitrary",), 54),
    )(dz, w_in_g, x, dh1, pre_g, *(prev or ()), *ex_arrs)
    return res[0], res[1], res[2:]


def _grad_w(a, b, bn, shard_major, name, tk=1024, ex_arrs=(), ex_scatter=()):
    t_len, m = a.shape
    n = b.shape[1]
    n_j, n_k = n // bn, t_len // tk
    ex = _Exchange(ex_arrs, ex_scatter)

    def body(a_ref, b_ref, *refs):
        ex_in, o_ref, ex_out = refs[:ex.n], refs[ex.n], refs[ex.n + 1:2 * ex.n + 1]
        acc_s, ex_sems = refs[2 * ex.n + 1], refs[2 * ex.n + 2:]
        j, k = pl.program_id(0), pl.program_id(1)
        if ex.n:
            @pl.when(jnp.logical_and(j == 0, k == 0))
            def _():
                ex.start(ex_in, ex_out, ex_sems)

        @pl.when(k == 0)
        def _():
            acc_s[...] = jnp.zeros_like(acc_s)

        acc_s[...] += _dot_tn(a_ref[...], b_ref[...])

        @pl.when(k == n_k - 1)
        def _():
            o_ref[...] = acc_s[...].astype(BF16)

        if ex.n:
            @pl.when(jnp.logical_and(j == n_j - 1, k == n_k - 1))
            def _():
                ex.wait(ex_in, ex_out, ex_sems)

    if shard_major:
        out_spec, out_shape = pl.BlockSpec((None, m, bn), lambda j, k: (j, 0, 0)), SDS((n_j, m, bn), BF16)
    else:
        out_spec, out_shape = pl.BlockSpec((m, bn), lambda j, k: (0, j)), SDS((m, n), BF16)
    res = pl.pallas_call(
        body, name=name, grid=(n_j, n_k),
        in_specs=[pl.BlockSpec((tk, m), lambda j, k: (k, 0)), pl.BlockSpec((tk, bn), lambda j, k: (k, j))]
        + [ANY_SPEC] * ex.n,
        out_specs=[out_spec] + [ANY_SPEC] * ex.n, out_shape=[out_shape] + ex.out_shape,
        scratch_shapes=[pltpu.VMEM((m, bn), F32)] + (ex.scratch if ex.n else []),
        compiler_params=_params(("arbitrary", "arbitrary"), 40),
    )(a, b, *ex_arrs)
    return res[0], res[1:]


RS_ORDER = (3, 2, 5, 4, 7, 6, 1, 0)
RS_SLOTS = (0, 1, 2, 4, 6)


def _grad_w_in(hn, dz, ex_arrs, ex_scatter, tk=1024):
    t_len = hn.shape[0]
    n_k = t_len // tk
    ex = _Exchange(ex_arrs, ex_scatter)
    me_out = 4 * lax.axis_index("x") + 2 * lax.axis_index("y") + lax.axis_index("c")
    order = jnp.stack([me_out ^ k for k in RS_ORDER]).astype(jnp.int32)
    slots = jnp.stack([me_out ^ k for k in RS_SLOTS]).astype(jnp.int32)
    n_stage = 2

    def body(order_ref, a_ref, b_ref, *refs):
        ex_in, parts_hbm, ex_out = refs[:ex.n], refs[ex.n], refs[ex.n + 1:2 * ex.n + 1]
        acc_s, stage_s, rx_s, send_sems, recv_sems, loc_sem = refs[2 * ex.n + 1:2 * ex.n + 7]
        ex_sems = refs[2 * ex.n + 7:]
        j, k = pl.program_id(0), pl.program_id(1)
        x, y, c, me = _mesh_place()
        sib = _peer(x, y, c, SIBLING)[0]

        def send(jj):
            mask, src = RS_ORDER[jj], stage_s.at[jj % n_stage]
            if mask == 0:
                return pltpu.make_async_copy(src, parts_hbm.at[me], loc_sem.at[0])
            pair = (send_sems.at[mask], recv_sems.at[mask])
            if mask in ICI_MASKS or mask == SIBLING:
                return _remote(src, parts_hbm.at[me], *pair, _peer(x, y, c, mask)[0])
            return _remote(src, rx_s.at[mask // 2 - 1], *pair, sib)

        def from_sibling(mask):
            return _remote(stage_s.at[0], rx_s.at[mask // 2 - 1], send_sems.at[mask], recv_sems.at[mask], sib)

        @pl.when(jnp.logical_and(j == 0, k == 0))
        def _():
            ex.start(ex_in, ex_out, ex_sems)

        @pl.when(k == 0)
        def _():
            acc_s[...] = jnp.zeros_like(acc_s)

        acc_s[...] += _dot_tn(a_ref[...], b_ref[...])

        for jj in range(N_DEV):
            @pl.when(jnp.logical_and(j == jj, k == n_k - 1))
            def _(jj=jj):
                mask = RS_ORDER[jj]
                if jj >= n_stage:
                    send(jj - n_stage).wait_send()
                if mask in ICI_MASKS:
                    from_sibling(mask + 1).wait_recv()
                    stage_s[jj % n_stage] = (acc_s[...] + rx_s[mask // 2 - 1].astype(F32)).astype(BF16)
                else:
                    stage_s[jj % n_stage] = acc_s[...].astype(BF16)
                send(jj).start()

        @pl.when(jnp.logical_and(j == N_DEV - 1, k == n_k - 1))
        def _():
            for jj in range(N_DEV - n_stage, N_DEV):
                cp = send(jj)
                cp.wait() if RS_ORDER[jj] == 0 else cp.wait_send()
            for mask in DIRECT_MASKS:
                dev, lin = _peer(x, y, c, mask)
                _remote(stage_s.at[0], parts_hbm.at[lin], send_sems.at[mask], recv_sems.at[mask], dev).wait_recv()
            ex.wait(ex_in, ex_out, ex_sems)

    dma = lambda n: pltpu.SemaphoreType.DMA((n,))
    grid_spec = pltpu.PrefetchScalarGridSpec(
        num_scalar_prefetch=1, grid=(N_DEV, n_k),
        in_specs=[pl.BlockSpec((tk, D_MODEL), lambda j, k, order: (k, 0)),
                  pl.BlockSpec((tk, W_IN_SHARD), lambda j, k, order: (k, order[j]))] + [ANY_SPEC] * ex.n,
        out_specs=[ANY_SPEC] * (1 + ex.n),
        scratch_shapes=[pltpu.VMEM((D_MODEL, W_IN_SHARD), F32), pltpu.VMEM((n_stage, D_MODEL, W_IN_SHARD), BF16),
                        pltpu.VMEM((len(ICI_MASKS), D_MODEL, W_IN_SHARD), BF16), dma(N_DEV), dma(N_DEV), dma(1)]
        + ex.scratch)
    res = pl.pallas_call(
        body, name="grad_w_in", grid_spec=grid_spec,
        out_shape=[SDS((N_DEV, D_MODEL, W_IN_SHARD), BF16)] + ex.out_shape,
        compiler_params=_params(("arbitrary", "arbitrary"), 44),
    )(order, hn, dz, *ex_arrs)
    return res[0], slots, res[1:]


def _sum_parts(parts, name):
    def body(p_ref, o_ref):
        g = p_ref[0].astype(F32)
        for s in range(1, parts.shape[0]):
            g = g + p_ref[s].astype(F32)
        o_ref[...] = g

    return pl.pallas_call(body, name=name, out_shape=SDS(parts.shape[1:], F32))(parts)


def _adamw_math(g, w_ref, m_ref, v_ref, g_ref, d_ref, nm_ref, nv_ref):
    c1 = 1.0 - ADAM_B1 ** ADAM_STEP
    c2 = 1.0 - ADAM_B2 ** ADAM_STEP
    g_ref[...] = g
    nm = ADAM_B1 * m_ref[...] + (1.0 - ADAM_B1) * g
    nv = ADAM_B2 * v_ref[...] + (1.0 - ADAM_B2) * (g * g)
    nm_ref[...] = nm
    nv_ref[...] = nv
    d_ref[...] = -ADAM_LR * ((nm / c1) / (jnp.sqrt(nv / c2) + ADAM_EPS) + ADAM_WD * w_ref[...])


def _adamw(parts, w, m, v, name, tr):
    rows, cols = w.shape
    n_parts = parts.shape[0]

    def body(p_ref, *refs):
        g = p_ref[0].astype(F32)
        for s in range(1, n_parts):
            g = g + p_ref[s].astype(F32)
        _adamw_math(g, *refs)

    tile = pl.BlockSpec((tr, cols), lambda i: (i, 0))
    return pl.pallas_call(
        body, name=name, grid=(rows // tr,),
        in_specs=[pl.BlockSpec((n_parts, tr, cols), lambda i: (0, i, 0)), tile, tile, tile],
        out_specs=[tile] * 4, out_shape=[SDS((rows, cols), F32)] * 4,
        compiler_params=_params(("arbitrary",), 40),
    )(parts, w, m, v)


def _adamw_slots(parts, slots, w, m, v, name, tr):
    rows, cols = w.shape
    n_slots = slots.shape[0]

    def body(slots_ref, *refs):
        g = refs[0][...].astype(F32)
        for s in range(1, n_slots):
            g = g + refs[s][...].astype(F32)
        _adamw_math(g, *refs[n_slots:])

    tile = pl.BlockSpec((tr, cols), lambda i, slots: (i, 0))
    part = lambda s: pl.BlockSpec((None, tr, cols), lambda i, slots: (slots[s], i, 0))
    grid_spec = pltpu.PrefetchScalarGridSpec(
        num_scalar_prefetch=1, grid=(rows // tr,),
        in_specs=[part(s) for s in range(n_slots)] + [tile, tile, tile], out_specs=[tile] * 4)
    return pl.pallas_call(
        body, name=name, grid_spec=grid_spec, out_shape=[SDS((rows, cols), F32)] * 4,
        compiler_params=_params(("arbitrary",), 40),
    )(slots, *([parts] * n_slots), w, m, v)


PACKED = ("gmlp_ln_g", "gmlp_ln_b", "gmlp_ws", "gmlp_bs", "conv_b", "w_a", "b_a", "w_x", "b_x", "lam", "gmlp_out_g",
          "lru_out_g", "post_g")
WEIGHTS = ("pre_g", "w_in", "gmlp_ln_g", "gmlp_ln_b", "gmlp_ws", "gmlp_bs", "conv_w", "conv_b", "w_a", "b_a", "w_x",
           "b_x", "lam", "gmlp_out_g", "lru_out_g", "w_out", "post_g", "w_pe", "w_pg")
LANES = 128


PACK_ROWS = 3200
PACK_TILE = 640
IN_BWD_TILE = 256


def _pack(parts):
    rows = [p.reshape(-1, LANES) for p in parts]
    used = sum(r.shape[0] for r in rows)
    return jnp.concatenate(rows + [jnp.zeros((PACK_ROWS - used, LANES), F32)], axis=0)


def _pad_rows(a, rows):
    return jnp.concatenate([a, jnp.zeros((rows - a.shape[0],) + a.shape[1:], a.dtype)], axis=0)


def kernel(x, p, pre_g, w_in, gmlp_ln_g, gmlp_ln_b, gmlp_ws, gmlp_bs, conv_w, conv_b, w_a, b_a, w_x, b_x, lam, gmlp_out_g, lru_out_g, w_out, post_g, w_pe, w_pg, loss_target, m_pre_g, m_w_in, m_gmlp_ln_g, m_gmlp_ln_b, m_gmlp_ws, m_gmlp_bs, m_conv_w, m_conv_b, m_w_a, m_b_a, m_w_x, m_b_x, m_lam, m_gmlp_out_g, m_lru_out_g, m_w_out, m_post_g, m_w_pe, m_w_pg, v_pre_g, v_w_in, v_gmlp_ln_g, v_gmlp_ln_b, v_gmlp_ws, v_gmlp_bs, v_conv_w, v_conv_b, v_w_a, v_b_a, v_w_x, v_b_x, v_lam, v_gmlp_out_g, v_lru_out_g, v_w_out, v_post_g, v_w_pe, v_w_pg):
    args = dict(locals())
    weights = {n: args[n] for n in WEIGHTS}
    m_in = {n: args["m_" + n] for n in WEIGHTS}
    v_in = {n: args["v_" + n] for n in WEIGHTS}
    sm = {n: weights[n][0] for n in PACKED}
    shard_rows = D_MODEL // N_DEV
    xs, ps, tgt = x[0], p[0, 0], loss_target[0]

    vec = lambda a: a.reshape(1, -1)
    tril = jnp.tril(jnp.ones((CHUNK, CHUNK), dtype=bool))
    wm32 = jnp.where(tril[None], sm["gmlp_ws"], 0.0)
    wm, wm_t = wm32.astype(BF16), jnp.swapaxes(wm32, 1, 2).astype(BF16)
    bias = jnp.repeat(sm["gmlp_bs"].T, HEAD, axis=1)
    wax32 = jnp.concatenate([sm["w_a"], sm["w_x"]], axis=2)
    wax, wax_t = wax32.astype(BF16), jnp.swapaxes(wax32, 1, 2).astype(BF16)
    ln_g, ln_b = vec(sm["gmlp_ln_g"]), vec(sm["gmlp_ln_b"])
    post_g_v = vec(sm["post_g"])

    hn = _pre_norm(xs, pre_g)
    cw_shard = _pad_rows(conv_w.reshape(CONV_W, HEAD), ROWS)
    z, w_in_g, (cw_g,) = _in_proj(hn, w_in[0].astype(BF16), [cw_shard])
    cw_full = jnp.transpose(cw_g[:, :CONV_W, :], (1, 0, 2)).reshape(CONV_W, D_BR)
    mixer_consts = dict(cw=_pad_rows(cw_full, ROWS), cb=vec(sm["conv_b"]), ba=vec(sm["b_a"]), bx=vec(sm["b_x"]),
                        lam=vec(sm["lam"]), goa=vec(sm["gmlp_out_g"]), gob=vec(sm["lru_out_g"]))
    (y, h), (w_out_g, w_pe_g, w_pg_g) = _mix_fwd(
        z, ln_g, ln_b, wm, bias, wax=wax, **mixer_consts,
        ex_arrs=[w_out[0].astype(BF16), w_pe[0].astype(BF16), w_pg[0].astype(BF16)], ex_scatter=[False, False, False])
    w_out_f, w_pg_f = w_out_g.reshape(D_MODEL, D_MODEL), w_pg_g.reshape(D_MODEL, D_MODEL)
    h1, ob = _out_proj(y, xs, w_out_f, post_g_v)
    dh2, dgl, h1b, loss_part, d_w_pe = _ple_loss(h1, ps, tgt, w_pg_f, w_pe_g)

    dh1, do, dy, d_post_g = _tail_bwd(dh2, dgl, ob, w_pg_f, w_out_f, post_g_v)
    d_w_out, _ = _grad_w(y, do, 512, False, "grad_w_out")
    d_w_pg, _ = _grad_w(h1b, dgl, 512, False, "grad_w_pg")
    (dz, vecs, d_ws, d_wax, d_bs), (parts_out, parts_pg, parts_pe) = _mix_bwd(
        z, dy, h, ln_g, ln_b, wm, wm_t, bias, wax=wax, wax_t=wax_t, **mixer_consts,
        ex_arrs=[d_w_out.reshape(N_DEV, shard_rows, D_MODEL), d_w_pg.reshape(N_DEV, shard_rows, D_MODEL), d_w_pe],
        ex_scatter=[True, True, True])

    small = {"gmlp_ln_g": vecs[V_LN_G], "gmlp_ln_b": vecs[V_LN_B], "gmlp_ws": d_ws, "gmlp_bs": d_bs,
             "conv_b": vecs[V_CONV_B], "w_a": d_wax[:, :, :HEAD], "b_a": vecs[V_B_A], "w_x": d_wax[:, :, HEAD:],
             "b_x": vecs[V_B_X], "lam": vecs[V_LAM], "gmlp_out_g": vecs[V_GOUT_A], "lru_out_g": vecs[V_GOUT_B],
             "post_g": d_post_g}
    small_part = _pack([small[n] for n in PACKED] + [loss_part]).reshape(N_DEV, PACK_ROWS // N_DEV, LANES)
    d_cw_blocks = jnp.transpose(vecs[V_CONV_W:V_CONV_W + CONV_W].reshape(CONV_W, N_DEV, HEAD), (1, 0, 2))
    d_cw_blocks = jnp.concatenate([d_cw_blocks, jnp.zeros((N_DEV, ROWS - CONV_W, HEAD), F32)], axis=1)
    parts_in, slots_in, (small_blocks, parts_cw) = _grad_w_in(
        hn, dz, ex_arrs=[small_part, d_cw_blocks], ex_scatter=[True, True])
    small_sum = _sum_parts(small_blocks, "sum_small")
    grad_x, d_pre_g, _ = _in_bwd(dz, w_in_g, xs, dh1, pre_g, 0, xs.shape[0] // IN_BWD_TILE, None, "in_bwd",
                                 tm=IN_BWD_TILE)
    pre_rows = D_MODEL // LANES
    small_all, parts_pre = _exchange([small_sum, d_pre_g.reshape(pre_rows, LANES)], False, "gather_small_grads")
    parts_small = small_all.reshape(1, PACK_ROWS, LANES)

    pad_cw = lambda a: _pad_rows(a.reshape(CONV_W, HEAD), ROWS)
    flat = lambda a: a.reshape(pre_rows, LANES)
    outs = {
        "w_in": _adamw_slots(parts_in, slots_in, w_in[0], m_w_in[0], v_w_in[0], "adamw_w_in", 256),
        "w_out": _adamw(parts_out, w_out[0], m_w_out[0], v_w_out[0], "adamw_w_out", 128),
        "w_pe": _adamw(parts_pe, w_pe[0], m_w_pe[0], v_w_pe[0], "adamw_w_pe", 256),
        "w_pg": _adamw(parts_pg, w_pg[0], m_w_pg[0], v_w_pg[0], "adamw_w_pg", 128),
        "conv_w": [a[:CONV_W] for a in
                   _adamw(parts_cw, pad_cw(conv_w), pad_cw(m_conv_w), pad_cw(v_conv_w), "adamw_conv_w", ROWS)],
        "pre_g": _adamw(parts_pre, flat(pre_g), flat(m_pre_g), flat(v_pre_g), "adamw_pre_g", pre_rows),
    }
    packed = _adamw(parts_small, _pack([weights[n] for n in PACKED]), _pack([m_in[n] for n in PACKED]),
                    _pack([v_in[n] for n in PACKED]), "adamw_small", PACK_TILE)
    row = 0
    for n in PACKED:
        n_rows = weights[n].size // LANES
        outs[n] = [packed[q][row:row + n_rows] for q in range(4)]
        row += n_rows
    loss = packed[0][row, 0]

    result = [loss, grad_x[None]]
    for q in range(4):
        result += [outs[n][q].reshape(weights[n].shape) for n in WEIGHTS]
    return tuple(result)
```

```python
import functools

import jax
import jax.numpy as jnp
from jax import lax
from jax.experimental import pallas as pl
from jax.experimental.pallas import tpu as pltpu

F32 = jnp.float32
BF16 = jnp.bfloat16
SDS = jax.ShapeDtypeStruct

D_MODEL = 2048
D_BR = 1024
D_IN = 5 * D_BR
D_PLE = 256
N_HEAD = 8
HEAD = 128
CHUNK = 128
ROWS = 8
N_GROUP = CHUNK // ROWS
N_DEV = 8
W_IN_SHARD = D_IN // N_DEV
EPS = 1e-6
LRU_C = 8.0
CONV_W = 4
MESH_AXES = ("x", "y", "c")
MIB = 1 << 20

ADAM_LR, ADAM_B1, ADAM_B2, ADAM_EPS, ADAM_WD, ADAM_STEP = 0.001, 0.9, 0.999, 1e-08, 0.01, 10

_GELU_C = 0.7978845608028654
_GELU_A = 0.044715

V_LN_G, V_LN_B, V_CONV_B, V_B_A, V_B_X, V_LAM, V_GOUT_A, V_GOUT_B, V_CONV_W = 0, 1, 2, 3, 4, 5, 6, 7, 8
N_VEC = 16


def _params(sem, vmem_mib):
    return pltpu.CompilerParams(dimension_semantics=sem, vmem_limit_bytes=int(vmem_mib * MIB))


def _sig(x):
    return 0.5 * jnp.tanh(0.5 * x) + 0.5


def _gelu(x):
    t = jnp.tanh(_GELU_C * (x + _GELU_A * x * x * x))
    return 0.5 * x * (1.0 + t), t


def _gelu_grad(x, t):
    return 0.5 * (1.0 + t) + 0.5 * x * (1.0 - t * t) * (_GELU_C * (1.0 + 3.0 * _GELU_A * x * x))


def _neg_expm1(y, exp_y):
    series = -y * (1.0 + y * (0.5 + y * (1.0 / 6.0)))
    return jnp.where(y > -0.01, series, 1.0 - exp_y)


def _softplus(x):
    return jnp.maximum(x, 0.0) + jnp.log(1.0 + jnp.exp(-jnp.abs(x)))


def _row_ids(width):
    return lax.broadcasted_iota(jnp.int32, (ROWS, width), 0)


def _shift_down(cur, prev, k, rid):
    return jnp.where(rid >= k, pltpu.roll(cur, k, 0), pltpu.roll(prev, k, 0))


def _shift_up(cur, nxt, k, rid):
    return jnp.where(rid < ROWS - k, pltpu.roll(cur, ROWS - k, 0), pltpu.roll(nxt, ROWS - k, 0))


def _mean_last(x):
    return jnp.mean(x, axis=-1, keepdims=True)


def _rows(g):
    return pl.ds(pl.multiple_of(g * ROWS, ROWS), ROWS)


TILE_ROWS = 16


def _tile_rows(q):
    return pl.ds(pl.multiple_of(q * TILE_ROWS, TILE_ROWS), TILE_ROWS)


UNROLL = 4


def _loop(n, body, init, unroll=UNROLL):
    def wide(i, carry):
        for u in range(unroll):
            carry = body(i * unroll + u, carry)
        return carry

    return lax.fori_loop(0, n // unroll, wide, init)


def _fold_rows(x):
    return x[0:ROWS, :] + x[ROWS:TILE_ROWS, :]


def _bcast_row(x, r):
    return jnp.broadcast_to(x[r:r + 1, :], x.shape)


def _dot(a, b):
    return jnp.dot(a, b, preferred_element_type=F32)


def _dot_nt(a, b):
    return lax.dot_general(a, b, (((1,), (1,)), ((), ())), preferred_element_type=F32)


def _dot_tn(a, b):
    return lax.dot_general(a, b, (((0,), (0,)), ((), ())), preferred_element_type=F32)


def _mesh_place():
    x, y, c = lax.axis_index("x"), lax.axis_index("y"), lax.axis_index("c")
    return x, y, c, 4 * x + 2 * y + c


def _peer(x, y, c, k):
    px = 1 - x if k & 4 else x
    py = 1 - y if k & 2 else y
    pc = 1 - c if k & 1 else c
    return (px, py, pc), 4 * px + 2 * py + pc


def _remote(src, dst, send_sem, recv_sem, dev):
    return pltpu.make_async_remote_copy(src_ref=src, dst_ref=dst, send_sem=send_sem, recv_sem=recv_sem, device_id=dev,
                                        device_id_type=pl.DeviceIdType.MESH)


ANY_SPEC = pl.BlockSpec(memory_space=pl.ANY)


class _Exchange:
    def __init__(self, arrs, scatter):
        self.n = len(arrs)
        self.scatter = tuple(scatter)
        self.out_shape = [SDS(a.shape if s else (N_DEV,) + a.shape, a.dtype) for a, s in zip(arrs, scatter)]
        self.scratch = [pltpu.SemaphoreType.DMA((self.n * N_DEV,)), pltpu.SemaphoreType.DMA((self.n * N_DEV,)),
                        pltpu.SemaphoreType.DMA((self.n,))]

    def _copies(self, ins, outs, sems):
        send_sems, recv_sems, local_sems = sems
        x, y, c, me = _mesh_place()
        local, sends, recvs = [], [], []
        for a in range(self.n):
            src = ins[a].at[me] if self.scatter[a] else ins[a]
            local.append(pltpu.make_async_copy(src, outs[a].at[me], local_sems.at[a]))
        for k in range(1, N_DEV):
            dev, lin = _peer(x, y, c, k)
            for a in range(self.n):
                src = ins[a].at[lin] if self.scatter[a] else ins[a]
                pair = (send_sems.at[a * N_DEV + k], recv_sems.at[a * N_DEV + k], dev)
                sends.append(_remote(src, outs[a].at[me], *pair))
                recvs.append(_remote(src, outs[a].at[lin], *pair))
        return local, sends, recvs

    def start(self, ins, outs, sems):
        local, sends, _ = self._copies(ins, outs, sems)
        for cp in local + sends:
            cp.start()

    def wait(self, ins, outs, sems):
        local, sends, recvs = self._copies(ins, outs, sems)
        for cp in recvs:
            cp.wait_recv()
        for cp in sends:
            cp.wait_send()
        for cp in local:
            cp.wait()


def _exchange(arrs, scatter, name):
    ex = _Exchange(arrs, [scatter] * len(arrs))
    n = ex.n

    def body(*refs):
        ins, outs, sems = refs[:n], refs[n:2 * n], refs[2 * n:]
        ex.start(ins, outs, sems)
        ex.wait(ins, outs, sems)

    return pl.pallas_call(
        body, name=name, out_shape=ex.out_shape, in_specs=[ANY_SPEC] * n, out_specs=[ANY_SPEC] * n,
        scratch_shapes=ex.scratch,
    )(*arrs)


def _pre_norm(x, pre_g, tm=512):
    t_len = x.shape[0]

    def body(x_ref, g_ref, hn_ref):
        g = g_ref[...]

        def rows_body(q, _):
            rows = _tile_rows(q)
            xv = x_ref[rows, :]
            hn_ref[rows, :] = (xv * lax.rsqrt(_mean_last(xv * xv) + EPS) * g).astype(BF16)
            return 0

        _loop(tm // TILE_ROWS, rows_body, 0)

    tile = pl.BlockSpec((tm, D_MODEL), lambda i: (i, 0))
    return pl.pallas_call(
        body, name="pre_norm", grid=(t_len // tm,),
        in_specs=[tile, pl.BlockSpec((1, D_MODEL), lambda i: (0, 0))], out_specs=tile,
        out_shape=SDS((t_len, D_MODEL), BF16),
        compiler_params=_params(("arbitrary",), 24),
    )(x, pre_g)


AG_ORDER = (0, 1, 2, 4, 3, 5, 6, 7)
SIBLING = 1
ICI_MASKS = (2, 4, 6)
DIRECT_MASKS = (SIBLING,) + ICI_MASKS
Y_NEIGHBOUR, X_NEIGHBOUR, DIAGONAL = 2, 4, 6
W_DIRECT = (SIBLING, Y_NEIGHBOUR, X_NEIGHBOUR)


def _in_proj(hn, w_shard, others, tm=1024):
    t_len = hn.shape[0]
    n_i = t_len // tm
    n_o = len(others)
    me_out = 4 * lax.axis_index("x") + 2 * lax.axis_index("y") + lax.axis_index("c")
    order = jnp.stack([me_out ^ k for k in AG_ORDER]).astype(jnp.int32)

    def body(order_ref, hn_ref, w_hbm, *refs):
        o_in = refs[:n_o]
        z_ref, wg_hbm = refs[n_o], refs[n_o + 1]
        o_out = refs[n_o + 2:2 * n_o + 2]
        (wbuf, send_w, recv_w, fsend_w, frecv_w, send_o, recv_o, fsend_o, frecv_o, wb_sems, loc_sems, rsend,
         rrecv) = refs[2 * n_o + 2:]
        j, i = pl.program_id(0), pl.program_id(1)
        x, y, c, me = _mesh_place()
        sib = _peer(x, y, c, SIBLING)[0]

        def relay(core):
            src, dst = (Y_NEIGHBOUR, X_NEIGHBOUR) if core == 0 else (X_NEIGHBOUR, Y_NEIGHBOUR)
            held, diag = _peer(x, y, c, src)[1], _peer(x, y, c, DIAGONAL)[1]
            pair = (rsend.at[0], rrecv.at[0], _peer(x, y, c, dst)[0])
            return _remote(wbuf.at[held], wbuf.at[held], *pair), _remote(wbuf.at[diag], wbuf.at[diag], *pair)

        def direct(k, a=None):
            dev, lin = _peer(x, y, c, k)
            if a is None:
                return (_remote(w_hbm, wbuf.at[me], send_w.at[k], recv_w.at[k], dev),
                        _remote(w_hbm, wbuf.at[lin], send_w.at[k], recv_w.at[k], dev))
            pair = (send_o.at[a * N_DEV + k], recv_o.at[a * N_DEV + k], dev)
            return _remote(o_in[a], o_out[a].at[me], *pair), _remote(o_in[a], o_out[a].at[lin], *pair)

        def passed(k, a=None):
            mine, theirs = _peer(x, y, c, k)[1], _peer(x, y, c, k ^ SIBLING)[1]
            if a is None:
                pair = (fsend_w.at[k], frecv_w.at[k], sib)
                return _remote(wbuf.at[mine], wbuf.at[mine], *pair), _remote(wbuf.at[theirs], wbuf.at[theirs], *pair)
            pair = (fsend_o.at[a * N_DEV + k], frecv_o.at[a * N_DEV + k], sib)
            return (_remote(o_out[a].at[mine], o_out[a].at[mine], *pair),
                    _remote(o_out[a].at[theirs], o_out[a].at[theirs], *pair))

        def own_copies():
            return [pltpu.make_async_copy(o_in[a], o_out[a].at[me], loc_sems.at[1 + a]) for a in range(n_o)]

        @pl.when(jnp.logical_and(j == 0, i == 0))
        def _():
            own = pltpu.make_async_copy(w_hbm, wbuf.at[me], loc_sems.at[0])
            own.start()
            for cp in own_copies():
                cp.start()
            for k in W_DIRECT:
                direct(k)[0].start()
            for k in DIRECT_MASKS:
                for a in range(n_o):
                    direct(k, a)[0].start()
            own.wait()

        for jj in range(1, N_DEV):
            mask = AG_ORDER[jj]

            @pl.when(jnp.logical_and(j == jj, i == 0))
            def _(jj=jj, mask=mask):
                if mask in W_DIRECT:
                    direct(mask)[1].wait_recv()
                    if mask != SIBLING:
                        passed(mask)[0].start()

                        @pl.when(c == (0 if mask == Y_NEIGHBOUR else 1))
                        def _():
                            relay(0 if mask == Y_NEIGHBOUR else 1)[0].start()
                elif mask == DIAGONAL:
                    for core in (0, 1):
                        @pl.when(c == core)
                        def _(core=core):
                            relay(core)[1].wait_recv()
                    passed(mask)[0].start()
                else:
                    passed(mask ^ SIBLING)[1].wait_recv()
                late = jj - (N_DEV - len(ICI_MASKS))
                if late >= 0:
                    for a in range(n_o):
                        direct(ICI_MASKS[late], a)[1].wait_recv()
                        passed(ICI_MASKS[late], a)[0].start()

        slot = order_ref[j]

        @pl.when(i == 0)
        def _():
            pltpu.make_async_copy(wbuf.at[slot], wg_hbm.at[slot], wb_sems.at[j]).start()

        z_ref[...] = _dot(hn_ref[...], wbuf[slot])

        @pl.when(jnp.logical_and(j == N_DEV - 1, i == n_i - 1))
        def _():
            for a in range(n_o):
                direct(SIBLING, a)[1].wait_recv()
            for k in ICI_MASKS:
                for a in range(n_o):
                    passed(k, a)[1].wait_recv()
            for k in W_DIRECT:
                direct(k)[0].wait_send()
            for core in (0, 1):
                @pl.when(c == core)
                def _(core=core):
                    relay(core)[0].wait_send()
            for k in DIRECT_MASKS:
                for a in range(n_o):
                    direct(k, a)[0].wait_send()
            for k in ICI_MASKS:
                passed(k)[0].wait_send()
                for a in range(n_o):
                    passed(k, a)[0].wait_send()
            for cp in own_copies():
                cp.wait()
            for jj in range(N_DEV):
                pltpu.make_async_copy(wbuf.at[0], wg_hbm.at[0], wb_sems.at[jj]).wait()

    dma = lambda n: pltpu.SemaphoreType.DMA((n,))
    grid_spec = pltpu.PrefetchScalarGridSpec(
        num_scalar_prefetch=1, grid=(N_DEV, n_i),
        in_specs=[pl.BlockSpec((tm, D_MODEL), lambda j, i, order: (i, 0)), ANY_SPEC] + [ANY_SPEC] * n_o,
        out_specs=[pl.BlockSpec((tm, W_IN_SHARD), lambda j, i, order: (i, order[j])), ANY_SPEC] + [ANY_SPEC] * n_o,
        scratch_shapes=[pltpu.VMEM((N_DEV, D_MODEL, W_IN_SHARD), BF16), dma(N_DEV), dma(N_DEV), dma(N_DEV), dma(N_DEV),
                        dma(n_o * N_DEV), dma(n_o * N_DEV), dma(n_o * N_DEV), dma(n_o * N_DEV), dma(N_DEV), dma(1 + n_o),
                        dma(1), dma(1)])
    res = pl.pallas_call(
        body, name="in_proj", grid_spec=grid_spec,
        out_shape=[SDS((t_len, D_IN), F32), SDS((N_DEV, D_MODEL, W_IN_SHARD), BF16)]
        + [SDS((N_DEV,) + o.shape, o.dtype) for o in others],
        compiler_params=_params(("arbitrary", "arbitrary"), 48),
    )(order, hn, w_shard, *others)
    return res[0], res[1], res[2:]


def _conv_rows(cur, prev, cw_ref, cb, rid):
    acc = cw_ref[3:4, :] * cur + cb
    for k in range(1, CONV_W):
        acc = acc + cw_ref[3 - k:4 - k, :] * _shift_down(cur, prev, k, rid)
    return acc


def _lru_gates(pa, px, ba, bx, sp8, first_row):
    r = _sig(pa + ba)
    i = _sig(px + bx)
    la = -(r * sp8)
    a = jnp.exp(la)
    mult = jnp.where(first_row, 1.0, jnp.sqrt(_neg_expm1(2.0 * la, a * a)))
    return r, i, a, mult


def _mix_fwd(z, ln_g, ln_b, wm, bias, cw, cb, wax, ba, bx, lam, goa, gob, ex_arrs, ex_scatter):
    t_len = z.shape[0]
    n_chunk = t_len // CHUNK
    ex = _Exchange(ex_arrs, ex_scatter)
    n_in, n_out, n_scratch = 13, 2, 7

    def body(*refs):
        (z_ref, lng_ref, lnb_ref, wm_ref, bias_ref, cw_ref, cb_ref, wax_ref, ba_ref, bx_ref, lam_ref, goa_ref,
         gob_ref) = refs[:n_in]
        ex_in = refs[n_in:n_in + ex.n]
        y_ref, h_ref = refs[n_in + ex.n:n_in + ex.n + n_out]
        ex_out = refs[n_in + ex.n + n_out:n_in + 2 * ex.n + n_out]
        vn_s, xc_s, mixed_s, pre_s, y_s, carry_s, halo_s = refs[n_in + 2 * ex.n + n_out:n_in + 2 * ex.n + n_out + n_scratch]
        ex_sems = refs[n_in + 2 * ex.n + n_out + n_scratch:]
        c_id = pl.program_id(0)
        rid = _row_ids(D_BR)

        @pl.when(c_id == 0)
        def _():
            ex.start(ex_in, ex_out, ex_sems)
            carry_s[...] = jnp.zeros_like(carry_s)
            halo_s[...] = jnp.zeros_like(halo_s)

        lng, lnb, cb = lng_ref[...], lnb_ref[...], cb_ref[...]

        def phase1(g, prev):
            rows = _rows(g)
            vg, _ = _gelu(z_ref[rows, D_BR:2 * D_BR])
            xm = vg - _mean_last(vg)
            rs = lax.rsqrt(_mean_last(xm * xm) + EPS)
            vn_s[rows, :] = xm * rs * lng + lnb
            xb = z_ref[rows, 3 * D_BR:4 * D_BR]
            xc_s[rows, :] = _conv_rows(xb, prev, cw_ref, cb, rid)
            return xb

        halo_s[...] = _loop(N_GROUP, phase1, halo_s[...])

        for h in range(N_HEAD):
            cs = slice(h * HEAD, (h + 1) * HEAD)
            mixed_s[:, cs] = _dot(wm_ref[h], vn_s[:, cs].astype(BF16))
            pre = _dot(xc_s[:, cs].astype(BF16), wax_ref[h])
            pre_s[:, cs] = pre[:, :HEAD]
            pre_s[:, D_BR + h * HEAD:D_BR + (h + 1) * HEAD] = pre[:, HEAD:]

        ba, bx, goa, gob = ba_ref[...], bx_ref[...], goa_ref[...], gob_ref[...]
        sp8 = LRU_C * _softplus(-lam_ref[...])

        def phase3(g, carry):
            rows = _rows(g)
            ug, _ = _gelu(z_ref[rows, 0:D_BR])
            ga = z_ref[rows, 2 * D_BR:3 * D_BR]
            ya = ug * (mixed_s[rows, :] + bias_ref[rows, :]) * (ga * _sig(ga))
            y_s[rows, 0:D_BR] = ya * lax.rsqrt(_mean_last(ya * ya) + EPS) * goa

            first_row = jnp.logical_and(jnp.logical_and(c_id == 0, g == 0), rid == 0)
            _, i, a, mult = _lru_gates(pre_s[rows, 0:D_BR], pre_s[rows, D_BR:2 * D_BR], ba, bx, sp8, first_row)
            b = mult * i * xc_s[rows, :]
            for d in (1, 2, 4):
                a_sh = jnp.where(rid >= d, pltpu.roll(a, d, 0), 1.0)
                b_sh = jnp.where(rid >= d, pltpu.roll(b, d, 0), 0.0)
                b = a * b_sh + b
                a = a * a_sh
            hh = b + a * carry
            h_ref[rows, :] = hh
            gb = z_ref[rows, 4 * D_BR:5 * D_BR]
            yb = hh * (gb * _sig(gb))
            y_s[rows, D_BR:2 * D_BR] = yb * lax.rsqrt(_mean_last(yb * yb) + EPS) * gob
            return _bcast_row(hh, ROWS - 1)

        carry_s[...] = _loop(N_GROUP, phase3, carry_s[...])
        y_ref[...] = y_s[...].astype(BF16)

        @pl.when(c_id == n_chunk - 1)
        def _():
            ex.wait(ex_in, ex_out, ex_sems)

    vec = pl.BlockSpec((1, D_BR), lambda i: (0, 0))
    res = pl.pallas_call(
        body, name="mix_fwd", grid=(n_chunk,),
        in_specs=[pl.BlockSpec((CHUNK, D_IN), lambda i: (i, 0)), vec, vec,
                  pl.BlockSpec((N_HEAD, HEAD, HEAD), lambda i: (0, 0, 0)),
                  pl.BlockSpec((CHUNK, D_BR), lambda i: (0, 0)),
                  pl.BlockSpec((ROWS, D_BR), lambda i: (0, 0)), vec,
                  pl.BlockSpec((N_HEAD, HEAD, 2 * HEAD), lambda i: (0, 0, 0)), vec, vec, vec, vec, vec]
        + [ANY_SPEC] * ex.n,
        out_specs=[pl.BlockSpec((CHUNK, 2 * D_BR), lambda i: (i, 0)), pl.BlockSpec((CHUNK, D_BR), lambda i: (i, 0))]
        + [ANY_SPEC] * ex.n,
        out_shape=[SDS((t_len, 2 * D_BR), BF16), SDS((t_len, D_BR), F32)] + ex.out_shape,
        scratch_shapes=[pltpu.VMEM((CHUNK, D_BR), F32), pltpu.VMEM((CHUNK, D_BR), F32), pltpu.VMEM((CHUNK, D_BR), F32),
                        pltpu.VMEM((CHUNK, 2 * D_BR), F32), pltpu.VMEM((CHUNK, 2 * D_BR), F32),
                        pltpu.VMEM((ROWS, D_BR), F32), pltpu.VMEM((ROWS, D_BR), F32)] + ex.scratch,
        compiler_params=_params(("arbitrary",), 32),
    )(z, ln_g, ln_b, wm, bias, cw, cb, wax, ba, bx, lam, goa, gob, *ex_arrs)
    return res[:n_out], res[n_out:]


def _load_weight(w_hbm, w_vmem, sem):
    @pl.when(pl.program_id(0) == 0)
    def _():
        cp = pltpu.make_async_copy(w_hbm, w_vmem, sem)
        cp.start()
        cp.wait()


def _out_proj(y, x, w_out, post_g, tm=512):
    t_len = y.shape[0]

    def body(y_ref, x_ref, w_hbm, g_ref, h1_ref, ob_ref, w_s, o_s, sem):
        _load_weight(w_hbm, w_s, sem)
        o_s[...] = _dot(y_ref[...], w_s[...])
        g = g_ref[...]

        def rows_body(q, _):
            rows = _tile_rows(q)
            o = o_s[rows, :]
            h1_ref[rows, :] = x_ref[rows, :] + o * lax.rsqrt(_mean_last(o * o) + EPS) * g
            ob_ref[rows, :] = o.astype(BF16)
            return 0

        _loop(tm // TILE_ROWS, rows_body, 0)

    tile = pl.BlockSpec((tm, D_MODEL), lambda i: (i, 0))
    return pl.pallas_call(
        body, name="out_proj", grid=(t_len // tm,),
        in_specs=[tile, tile, pl.BlockSpec(memory_space=pl.ANY), pl.BlockSpec((1, D_MODEL), lambda i: (0, 0))],
        out_specs=[tile, tile],
        out_shape=[SDS((t_len, D_MODEL), F32), SDS((t_len, D_MODEL), BF16)],
        scratch_shapes=[pltpu.VMEM((D_MODEL, D_MODEL), BF16), pltpu.VMEM((tm, D_MODEL), F32), pltpu.SemaphoreType.DMA],
        compiler_params=_params(("arbitrary",), 44),
    )(y, x, w_out, post_g)


def _ple_loss(h1, p, tgt, w_pg, w_pe_g, tm=256):
    t_len = h1.shape[0]
    n_tile = t_len // tm
    pe_shard = D_MODEL // N_DEV

    def body(h1_ref, p_ref, t_ref, w_hbm, wpe_ref, dh2_ref, dgl_ref, h1b_ref, loss_ref, dwpe_ref, w_s, pe_s, gl_s, acc_s,
             dpe_s, gpe_s, sem):
        _load_weight(w_hbm, w_s, sem)
        i = pl.program_id(0)

        @pl.when(i == 0)
        def _():
            acc_s[...] = jnp.zeros_like(acc_s)
            gpe_s[...] = jnp.zeros_like(gpe_s)

        h1b_ref[...] = h1_ref[...].astype(BF16)
        pb = p_ref[...].astype(BF16)
        for j in range(N_DEV):
            pe_s[:, j * pe_shard:(j + 1) * pe_shard] = _dot(pb, wpe_ref[j])
        gl_s[...] = _dot(h1b_ref[...], w_s[...])

        def rows_body(q, acc):
            rows = _tile_rows(q)
            pe = pe_s[rows, :]
            g = _sig(gl_s[rows, :])
            e = h1_ref[rows, :] + pe * g - t_ref[rows, :]
            dh2 = e * (1.0 / D_MODEL)
            dh2_ref[rows, :] = dh2
            dpe_s[rows, :] = (dh2 * g).astype(BF16)
            dgl_ref[rows, :] = (dh2 * pe * g * (1.0 - g)).astype(BF16)
            return acc + _fold_rows(e * e)

        acc_s[...] = _loop(tm // TILE_ROWS, rows_body, acc_s[...])
        gpe_s[...] += _dot_tn(pb, dpe_s[...])

        @pl.when(i == n_tile - 1)
        def _():
            loss_ref[...] = jnp.full(loss_ref.shape, 0.5 / D_MODEL * jnp.sum(acc_s[...]), F32)
            for j in range(N_DEV):
                dwpe_ref[j] = gpe_s[:, j * pe_shard:(j + 1) * pe_shard].astype(BF16)

    tile = pl.BlockSpec((tm, D_MODEL), lambda i: (i, 0))
    pe_blocks = pl.BlockSpec((N_DEV, D_PLE, pe_shard), lambda i: (0, 0, 0))
    return pl.pallas_call(
        body, name="ple_loss", grid=(n_tile,),
        in_specs=[tile, pl.BlockSpec((tm, D_PLE), lambda i: (i, 0)), tile, pl.BlockSpec(memory_space=pl.ANY), pe_blocks],
        out_specs=[tile, tile, tile, pl.BlockSpec((ROWS, HEAD), lambda i: (0, 0)), pe_blocks],
        out_shape=[SDS((t_len, D_MODEL), F32), SDS((t_len, D_MODEL), BF16), SDS((t_len, D_MODEL), BF16),
                   SDS((ROWS, HEAD), F32), SDS((N_DEV, D_PLE, pe_shard), BF16)],
        scratch_shapes=[pltpu.VMEM((D_MODEL, D_MODEL), BF16), pltpu.VMEM((tm, D_MODEL), F32),
                        pltpu.VMEM((tm, D_MODEL), F32), pltpu.VMEM((ROWS, D_MODEL), F32), pltpu.VMEM((tm, D_MODEL), BF16),
                        pltpu.VMEM((D_PLE, D_MODEL), F32), pltpu.SemaphoreType.DMA],
        compiler_params=_params(("arbitrary",), 48),
    )(h1, p, tgt, w_pg, w_pe_g)


def _tail_bwd(dh2, dgl, ob, w_pg, w_out, post_g, tm=256):
    t_len = dh2.shape[0]
    n_tile = t_len // tm

    def body(dh2_ref, dgl_ref, ob_ref, wpg_hbm, wout_hbm, g_ref, dh1_ref, do_ref, dy_ref, dg_ref, wpg_s, wout_s, t_s,
             acc_s, sems):
        _load_weight(wpg_hbm, wpg_s, sems.at[0])
        _load_weight(wout_hbm, wout_s, sems.at[1])
        i = pl.program_id(0)

        @pl.when(i == 0)
        def _():
            acc_s[...] = jnp.zeros_like(acc_s)

        t_s[...] = _dot_nt(dgl_ref[...], wpg_s[...])
        g = g_ref[...]

        def rows_body(q, acc):
            rows = _tile_rows(q)
            dh1 = dh2_ref[rows, :] + t_s[rows, :]
            dh1_ref[rows, :] = dh1
            o = ob_ref[rows, :].astype(F32)
            rr = lax.rsqrt(_mean_last(o * o) + EPS)
            on = o * rr
            dog = dh1 * g
            do_ref[rows, :] = (rr * (dog - on * _mean_last(dog * on))).astype(BF16)
            return acc + _fold_rows(dh1 * on)

        acc_s[...] = _loop(tm // TILE_ROWS, rows_body, acc_s[...])
        dy_ref[...] = _dot_nt(do_ref[...], wout_s[...]).astype(BF16)

        @pl.when(i == n_tile - 1)
        def _():
            dg_ref[...] = jnp.sum(acc_s[...], axis=0, keepdims=True)

    tile = pl.BlockSpec((tm, D_MODEL), lambda i: (i, 0))
    vec = pl.BlockSpec((1, D_MODEL), lambda i: (0, 0))
    hbm = pl.BlockSpec(memory_space=pl.ANY)
    return pl.pallas_call(
        body, name="tail_bwd", grid=(n_tile,),
        in_specs=[tile, tile, tile, hbm, hbm, vec],
        out_specs=[tile, tile, tile, vec],
        out_shape=[SDS((t_len, D_MODEL), F32), SDS((t_len, D_MODEL), BF16), SDS((t_len, D_MODEL), BF16),
                   SDS((1, D_MODEL), F32)],
        scratch_shapes=[pltpu.VMEM((D_MODEL, D_MODEL), BF16), pltpu.VMEM((D_MODEL, D_MODEL), BF16),
                        pltpu.VMEM((tm, D_MODEL), F32), pltpu.VMEM((ROWS, D_MODEL), F32), pltpu.SemaphoreType.DMA((2,))],
        compiler_params=_params(("arbitrary",), 48),
    )(dh2, dgl, ob, w_pg, w_out, post_g)


def _mix_bwd(z, dy, h, ln_g, ln_b, wm, wm_t, bias, cw, cb, wax, wax_t, ba, bx, lam, goa, gob, ex_arrs, ex_scatter):
    t_len = z.shape[0]
    n_chunk = t_len // CHUNK
    halo_blocks = CHUNK // ROWS
    ex = _Exchange(ex_arrs, ex_scatter)
    n_in, n_out, n_scratch = 19, 5, 17

    def body(*refs):
        (z_ref, zhalo_ref, dy_ref, h_ref, hhalo_ref, lng_ref, lnb_ref, wm_ref, wmt_ref, bias_ref, cw_ref, cb_ref,
         wax_ref, waxt_ref, ba_ref, bx_ref, lam_ref, goa_ref, gob_ref) = refs[:n_in]
        ex_in = refs[n_in:n_in + ex.n]
        dz_ref, vecs_ref, dws_ref, dwax_ref, dbs_ref = refs[n_in + ex.n:n_in + ex.n + n_out]
        ex_out = refs[n_in + ex.n + n_out:n_in + 2 * ex.n + n_out]
        (vn_s, vh_s, rs_s, xc_s, mixed_s, pre_s, dmix_s, dvn_s, dho_s, dxc_s, dpre_s, dz_s, acc_s, accdm_s,
         cg_s, ca_s, dxchalo_s) = refs[n_in + 2 * ex.n + n_out:n_in + 2 * ex.n + n_out + n_scratch]
        ex_sems = refs[n_in + 2 * ex.n + n_out + n_scratch:]
        step = pl.program_id(0)
        c_id = n_chunk - 1 - step
        rid = _row_ids(D_BR)
        first_chunk = c_id == 0

        @pl.when(step == 0)
        def _():
            ex.start(ex_in, ex_out, ex_sems)
            acc_s[...] = jnp.zeros_like(acc_s)
            accdm_s[...] = jnp.zeros_like(accdm_s)
            cg_s[...] = jnp.zeros_like(cg_s)
            ca_s[...] = jnp.zeros_like(ca_s)
            dxchalo_s[...] = jnp.zeros_like(dxchalo_s)
            dws_ref[...] = jnp.zeros_like(dws_ref)
            dwax_ref[...] = jnp.zeros_like(dwax_ref)

        lng, lnb, cb = lng_ref[...], lnb_ref[...], cb_ref[...]
        xb_halo = jnp.where(first_chunk, 0.0, zhalo_ref[...])
        h_halo = jnp.where(first_chunk, 0.0, hhalo_ref[...])

        def prev_rows(ref, cols, g, halo):
            before = ref[pl.ds(pl.multiple_of(jnp.maximum(g - 1, 0) * ROWS, ROWS), ROWS), cols]
            return jnp.where(g > 0, before, halo)

        def phase1(g, prev):
            rows = _rows(g)
            vg, _ = _gelu(z_ref[rows, D_BR:2 * D_BR])
            xm = vg - _mean_last(vg)
            rs = lax.rsqrt(_mean_last(xm * xm) + EPS)
            vh = xm * rs
            vh_s[rows, :] = vh
            rs_s[rows, :] = jnp.broadcast_to(rs, (ROWS, HEAD))
            vn_s[rows, :] = vh * lng + lnb
            xb = z_ref[rows, 3 * D_BR:4 * D_BR]
            xc_s[rows, :] = _conv_rows(xb, prev, cw_ref, cb, rid)
            return xb

        _loop(N_GROUP, phase1, xb_halo)

        for hd in range(N_HEAD):
            cs = slice(hd * HEAD, (hd + 1) * HEAD)
            mixed_s[:, cs] = _dot(wm_ref[hd], vn_s[:, cs].astype(BF16))
            pre = _dot(xc_s[:, cs].astype(BF16), wax_ref[hd])
            pre_s[:, cs] = pre[:, :HEAD]
            pre_s[:, D_BR + hd * HEAD:D_BR + (hd + 1) * HEAD] = pre[:, HEAD:]

        goa, gob = goa_ref[...], gob_ref[...]

        def phase3(g, _):
            rows = _rows(g)
            u = z_ref[rows, 0:D_BR]
            ug, tu = _gelu(u)
            ga = z_ref[rows, 2 * D_BR:3 * D_BR]
            sga = _sig(ga)
            sa = ga * sga
            mixed = mixed_s[rows, :] + bias_ref[rows, :]
            ya0 = ug * mixed
            ya = ya0 * sa
            ra = lax.rsqrt(_mean_last(ya * ya) + EPS)
            dyan = dy_ref[rows, 0:D_BR].astype(F32)
            acc_s[V_GOUT_A] += dyan * ya * ra
            dyg = dyan * goa
            dya = ra * dyg - ya * (ra * ra * ra) * _mean_last(dyg * ya)
            dya0 = dya * sa
            dz_s[rows, 2 * D_BR:3 * D_BR] = dya * ya0 * (sga * (1.0 + ga * (1.0 - sga)))
            dmix = dya0 * ug
            dmix_s[rows, :] = dmix
            accdm_s[rows, :] += dmix
            dz_s[rows, 0:D_BR] = dya0 * mixed * _gelu_grad(u, tu)

            hh = h_ref[rows, :]
            gb = z_ref[rows, 4 * D_BR:5 * D_BR]
            sgb = _sig(gb)
            sb = gb * sgb
            yb = hh * sb
            rb = lax.rsqrt(_mean_last(yb * yb) + EPS)
            dybn = dy_ref[rows, D_BR:2 * D_BR].astype(F32)
            acc_s[V_GOUT_B] += dybn * yb * rb
            dyg = dybn * gob
            dyb = rb * dyg - yb * (rb * rb * rb) * _mean_last(dyg * yb)
            dho_s[rows, :] = dyb * sb
            dz_s[rows, 4 * D_BR:5 * D_BR] = dyb * hh * (sgb * (1.0 + gb * (1.0 - sgb)))
            return 0

        _loop(N_GROUP, phase3, 0)

        for hd in range(N_HEAD):
            cs = slice(hd * HEAD, (hd + 1) * HEAD)
            dmb = dmix_s[:, cs].astype(BF16)
            dvn_s[:, cs] = _dot(wmt_ref[hd], dmb)
            dws_ref[hd] += _dot_nt(dmb, vn_s[:, cs].astype(BF16))

        def phase5(g, _):
            rows = _rows(g)
            dvn = dvn_s[rows, :]
            vh = vh_s[rows, :]
            acc_s[V_LN_G] += dvn * vh
            acc_s[V_LN_B] += dvn
            dvh = dvn * lng
            rs = rs_s[rows, 0:1]
            dvg = rs * (dvh - _mean_last(dvh) - vh * _mean_last(dvh * vh))
            v = z_ref[rows, D_BR:2 * D_BR]
            _, tv = _gelu(v)
            dz_s[rows, D_BR:2 * D_BR] = dvg * _gelu_grad(v, tv)
            return 0

        _loop(N_GROUP, phase5, 0)

        ba, bx = ba_ref[...], bx_ref[...]
        sp8 = LRU_C * _softplus(-lam_ref[...])

        def phase6(k, carry):
            cg, ca = carry
            g = N_GROUP - 1 - k
            rows = _rows(g)
            first_row = jnp.logical_and(jnp.logical_and(first_chunk, g == 0), rid == 0)
            r, i, a, mult = _lru_gates(pre_s[rows, 0:D_BR], pre_s[rows, D_BR:2 * D_BR], ba, bx, sp8, first_row)
            a_nx = jnp.where(rid < ROWS - 1, pltpu.roll(a, ROWS - 1, 0), ca)
            aa, bb = a_nx, dho_s[rows, :]
            for d in (1, 2, 4):
                a_sh = jnp.where(rid < ROWS - d, pltpu.roll(aa, ROWS - d, 0), 1.0)
                b_sh = jnp.where(rid < ROWS - d, pltpu.roll(bb, ROWS - d, 0), 0.0)
                bb = aa * b_sh + bb
                aa = aa * a_sh
            gg = bb + aa * cg
            hh = h_ref[rows, :]
            hprev = _shift_down(hh, prev_rows(h_ref, slice(None), g, h_halo), 1, rid)
            xc = xc_s[rows, :]
            gx = gg * xc
            dla = gg * hprev * a - jnp.where(first_row, 0.0, gx * i * (a * a) * lax.rsqrt(mult * mult))
            acc_s[V_LAM] += -(dla * r)
            dpa = -(dla * sp8) * r * (1.0 - r)
            dpx = gx * mult * i * (1.0 - i)
            acc_s[V_B_A] += dpa
            acc_s[V_B_X] += dpx
            dpre_s[rows, 0:D_BR] = dpa
            dpre_s[rows, D_BR:2 * D_BR] = dpx
            dxc_s[rows, :] = gg * mult * i
            return _bcast_row(gg, 0), _bcast_row(a, 0)

        cg, ca = _loop(N_GROUP, phase6, (cg_s[...], ca_s[...]))
        cg_s[...] = cg
        ca_s[...] = ca

        for hd in range(N_HEAD):
            cs = slice(hd * HEAD, (hd + 1) * HEAD)
            dpre = jnp.concatenate([dpre_s[:, cs], dpre_s[:, D_BR + hd * HEAD:D_BR + (hd + 1) * HEAD]], axis=1).astype(BF16)
            dxc_s[:, cs] += _dot(dpre, waxt_ref[hd])
            dwax_ref[hd] += _dot_tn(xc_s[:, cs].astype(BF16), dpre)

        def phase8(k, nxt):
            g = N_GROUP - 1 - k
            rows = _rows(g)
            dxc = dxc_s[rows, :]
            acc_s[V_CONV_B] += dxc
            xb = z_ref[rows, 3 * D_BR:4 * D_BR]
            dxb = cw_ref[3:4, :] * dxc
            acc_s[V_CONV_W + 3] += dxc * xb
            for j in range(1, CONV_W):
                later = _shift_up(dxc, nxt, j, rid)
                dxb = dxb + cw_ref[3 - j:4 - j, :] * later
                acc_s[V_CONV_W + 3 - j] += later * xb
            dz_s[rows, 3 * D_BR:4 * D_BR] = dxb
            return dxc

        dxchalo_s[...] = _loop(N_GROUP, phase8, dxchalo_s[...])
        dz_ref[...] = dz_s[...].astype(BF16)

        @pl.when(step == n_chunk - 1)
        def _():
            for v in range(N_VEC):
                vecs_ref[v:v + 1, :] = jnp.sum(acc_s[v], axis=0, keepdims=True)
            lam = lam_ref[...]
            vecs_ref[V_LAM:V_LAM + 1, :] = vecs_ref[V_LAM:V_LAM + 1, :] * (-LRU_C * _sig(-lam))
            tril = (lax.broadcasted_iota(jnp.int32, (HEAD, HEAD), 0) >= lax.broadcasted_iota(jnp.int32, (HEAD, HEAD), 1))
            ones = jnp.ones((ROWS, HEAD), BF16)
            for hd in range(N_HEAD):
                cs = slice(hd * HEAD, (hd + 1) * HEAD)
                dws_ref[hd] = jnp.where(tril, dws_ref[hd], 0.0)
                blk = accdm_s[:, cs]
                hi = blk.astype(BF16)
                lo = (blk - hi.astype(F32)).astype(BF16)
                dbs_ref[hd:hd + 1, :] = (_dot_nt(ones, hi) + _dot_nt(ones, lo))[0:1, :]
            ex.wait(ex_in, ex_out, ex_sems)

    vec = pl.BlockSpec((1, D_BR), lambda i: (0, 0))
    rev = lambda i: (n_chunk - 1 - i, 0)
    halo = lambda col: (lambda i: (jnp.maximum((n_chunk - 1 - i) * halo_blocks - 1, 0), col))
    full3 = lambda a, b, c: pl.BlockSpec((a, b, c), lambda i: (0, 0, 0))
    big = lambda w: pltpu.VMEM((CHUNK, w), F32)
    res = pl.pallas_call(
        body, name="mix_bwd", grid=(n_chunk,),
        in_specs=[pl.BlockSpec((CHUNK, D_IN), rev), pl.BlockSpec((ROWS, D_BR), halo(3)),
                  pl.BlockSpec((CHUNK, 2 * D_BR), rev), pl.BlockSpec((CHUNK, D_BR), rev),
                  pl.BlockSpec((ROWS, D_BR), halo(0)), vec, vec,
                  full3(N_HEAD, HEAD, HEAD), full3(N_HEAD, HEAD, HEAD),
                  pl.BlockSpec((CHUNK, D_BR), lambda i: (0, 0)), pl.BlockSpec((ROWS, D_BR), lambda i: (0, 0)), vec,
                  full3(N_HEAD, HEAD, 2 * HEAD), full3(N_HEAD, 2 * HEAD, HEAD), vec, vec, vec, vec, vec]
        + [ANY_SPEC] * ex.n,
        out_specs=[pl.BlockSpec((CHUNK, D_IN), rev), pl.BlockSpec((N_VEC, D_BR), lambda i: (0, 0)),
                   full3(N_HEAD, HEAD, HEAD), full3(N_HEAD, HEAD, 2 * HEAD),
                   pl.BlockSpec((N_HEAD, HEAD), lambda i: (0, 0))] + [ANY_SPEC] * ex.n,
        out_shape=[SDS((t_len, D_IN), BF16), SDS((N_VEC, D_BR), F32), SDS((N_HEAD, HEAD, HEAD), F32),
                   SDS((N_HEAD, HEAD, 2 * HEAD), F32), SDS((N_HEAD, HEAD), F32)] + ex.out_shape,
        scratch_shapes=[big(D_BR), big(D_BR), big(HEAD), big(D_BR), big(D_BR), big(2 * D_BR), big(D_BR), big(D_BR),
                        big(D_BR), big(D_BR), big(2 * D_BR), big(D_IN),
                        pltpu.VMEM((N_VEC, ROWS, D_BR), F32), big(D_BR),
                        pltpu.VMEM((ROWS, D_BR), F32), pltpu.VMEM((ROWS, D_BR), F32), pltpu.VMEM((ROWS, D_BR), F32)]
        + ex.scratch,
        compiler_params=_params(("arbitrary",), 48),
    )(z, z, dy, h, h, ln_g, ln_b, wm, wm_t, bias, cw, cb, wax, wax_t, ba, bx, lam, goa, gob, *ex_arrs)
    return res[:n_out], res[n_out:]


def _in_bwd(dz, w_in_g, x, dh1, pre_g, first_tile, n_tile, prev, name, ex_arrs=(), ex_scatter=(), tm=256):
    t_len = x.shape[0]
    ex = _Exchange(ex_arrs, ex_scatter)
    n_prev = 0 if prev is None else 2

    def body(dz_ref, w_hbm, x_ref, dh1_ref, g_ref, *refs):
        prev_refs, refs = refs[:n_prev], refs[n_prev:]
        ex_in, (gx_ref, dg_ref), ex_out = refs[:ex.n], refs[ex.n:ex.n + 2], refs[ex.n + 2:2 * ex.n + 2]
        w_s, t_s, dg_s, w_sems = refs[2 * ex.n + 2:2 * ex.n + 6]
        ex_sems = refs[2 * ex.n + 6:]
        i = pl.program_id(0)

        @pl.when(i == 0)
        def _():
            if ex.n:
                ex.start(ex_in, ex_out, ex_sems)
            loads = [pltpu.make_async_copy(w_hbm.at[s], w_s.at[:, s * W_IN_SHARD:(s + 1) * W_IN_SHARD], w_sems.at[s])
                     for s in range(N_DEV)]
            for cp in loads:
                cp.start()
            dg_s[...] = jnp.zeros_like(dg_s)
            for cp in loads:
                cp.wait()

        t_s[...] = _dot_nt(dz_ref[...], w_s[...])
        g = g_ref[...]

        def rows_body(q, acc):
            rows = _tile_rows(q)
            xv = x_ref[rows, :]
            r = lax.rsqrt(_mean_last(xv * xv) + EPS)
            xh = xv * r
            dhn = t_s[rows, :]
            dg = dhn * g
            gx_ref[rows, :] = dh1_ref[rows, :] + r * (dg - xh * _mean_last(dg * xh))
            return acc + _fold_rows(dhn * xh)

        dg_s[...] = _loop(tm // TILE_ROWS, rows_body, dg_s[...])

        @pl.when(i == n_tile - 1)
        def _():
            dg = jnp.sum(dg_s[...], axis=0, keepdims=True)
            dg_ref[...] = dg + prev_refs[1][...] if n_prev else dg
            if ex.n:
                ex.wait(ex_in, ex_out, ex_sems)

    tile = pl.BlockSpec((tm, D_MODEL), lambda i: (first_tile + i, 0))
    vec = pl.BlockSpec((1, D_MODEL), lambda i: (0, 0))
    prev_specs = [ANY_SPEC, vec] if n_prev else []
    res = pl.pallas_call(
        body, name=name, grid=(n_tile,),
        in_specs=[pl.BlockSpec((tm, D_IN), lambda i: (first_tile + i, 0)), ANY_SPEC, tile, tile, vec] + prev_specs
        + [ANY_SPEC] * ex.n,
        out_specs=[tile, vec] + [ANY_SPEC] * ex.n,
        out_shape=[SDS((t_len, D_MODEL), F32), SDS((1, D_MODEL), F32)] + ex.out_shape,
        scratch_shapes=[pltpu.VMEM((D_MODEL, D_IN), BF16), pltpu.VMEM((tm, D_MODEL), F32), pltpu.VMEM((ROWS, D_MODEL), F32),
                        pltpu.SemaphoreType.DMA((N_DEV,))] + (ex.scratch if ex.n else []),
        input_output_aliases={5: 0} if n_prev else {},
        compiler_params=_params(("arbitrary",), 54),
    )(dz, w_in_g, x, dh1, pre_g, *(prev or ()), *ex_arrs)
    return res[0], res[1], res[2:]


def _grad_w(a, b, bn, shard_major, name, tk=1024, ex_arrs=(), ex_scatter=()):
    t_len, m = a.shape
    n = b.shape[1]
    n_j, n_k = n // bn, t_len // tk
    ex = _Exchange(ex_arrs, ex_scatter)

    def body(a_ref, b_ref, *refs):
        ex_in, o_ref, ex_out = refs[:ex.n], refs[ex.n], refs[ex.n + 1:2 * ex.n + 1]
        acc_s, ex_sems = refs[2 * ex.n + 1], refs[2 * ex.n + 2:]
        j, k = pl.program_id(0), pl.program_id(1)
        if ex.n:
            @pl.when(jnp.logical_and(j == 0, k == 0))
            def _():
                ex.start(ex_in, ex_out, ex_sems)

        @pl.when(k == 0)
        def _():
            acc_s[...] = jnp.zeros_like(acc_s)

        acc_s[...] += _dot_tn(a_ref[...], b_ref[...])

        @pl.when(k == n_k - 1)
        def _():
            o_ref[...] = acc_s[...].astype(BF16)

        if ex.n:
            @pl.when(jnp.logical_and(j == n_j - 1, k == n_k - 1))
            def _():
                ex.wait(ex_in, ex_out, ex_sems)

    if shard_major:
        out_spec, out_shape = pl.BlockSpec((None, m, bn), lambda j, k: (j, 0, 0)), SDS((n_j, m, bn), BF16)
    else:
        out_spec, out_shape = pl.BlockSpec((m, bn), lambda j, k: (0, j)), SDS((m, n), BF16)
    res = pl.pallas_call(
        body, name=name, grid=(n_j, n_k),
        in_specs=[pl.BlockSpec((tk, m), lambda j, k: (k, 0)), pl.BlockSpec((tk, bn), lambda j, k: (k, j))]
        + [ANY_SPEC] * ex.n,
        out_specs=[out_spec] + [ANY_SPEC] * ex.n, out_shape=[out_shape] + ex.out_shape,
        scratch_shapes=[pltpu.VMEM((m, bn), F32)] + (ex.scratch if ex.n else []),
        compiler_params=_params(("arbitrary", "arbitrary"), 40),
    )(a, b, *ex_arrs)
    return res[0], res[1:]


RS_ORDER = (3, 2, 5, 4, 7, 6, 1, 0)
RS_SLOTS = (0, 1, 2, 4, 6)


def _grad_w_in(hn, dz, ex_arrs, ex_scatter, tk=1024):
    t_len = hn.shape[0]
    n_k = t_len // tk
    ex = _Exchange(ex_arrs, ex_scatter)
    me_out = 4 * lax.axis_index("x") + 2 * lax.axis_index("y") + lax.axis_index("c")
    order = jnp.stack([me_out ^ k for k in RS_ORDER]).astype(jnp.int32)
    slots = jnp.stack([me_out ^ k for k in RS_SLOTS]).astype(jnp.int32)
    n_stage = 2

    def body(order_ref, a_ref, b_ref, *refs):
        ex_in, parts_hbm, ex_out = refs[:ex.n], refs[ex.n], refs[ex.n + 1:2 * ex.n + 1]
        acc_s, stage_s, rx_s, send_sems, recv_sems, loc_sem = refs[2 * ex.n + 1:2 * ex.n + 7]
        ex_sems = refs[2 * ex.n + 7:]
        j, k = pl.program_id(0), pl.program_id(1)
        x, y, c, me = _mesh_place()
        sib = _peer(x, y, c, SIBLING)[0]

        def send(jj):
            mask, src = RS_ORDER[jj], stage_s.at[jj % n_stage]
            if mask == 0:
                return pltpu.make_async_copy(src, parts_hbm.at[me], loc_sem.at[0])
            pair = (send_sems.at[mask], recv_sems.at[mask])
            if mask in ICI_MASKS or mask == SIBLING:
                return _remote(src, parts_hbm.at[me], *pair, _peer(x, y, c, mask)[0])
            return _remote(src, rx_s.at[mask // 2 - 1], *pair, sib)

        def from_sibling(mask):
            return _remote(stage_s.at[0], rx_s.at[mask // 2 - 1], send_sems.at[mask], recv_sems.at[mask], sib)

        @pl.when(jnp.logical_and(j == 0, k == 0))
        def _():
            ex.start(ex_in, ex_out, ex_sems)

        @pl.when(k == 0)
        def _():
            acc_s[...] = jnp.zeros_like(acc_s)

        acc_s[...] += _dot_tn(a_ref[...], b_ref[...])

        for jj in range(N_DEV):
            @pl.when(jnp.logical_and(j == jj, k == n_k - 1))
            def _(jj=jj):
                mask = RS_ORDER[jj]
                if jj >= n_stage:
                    send(jj - n_stage).wait_send()
                if mask in ICI_MASKS:
                    from_sibling(mask + 1).wait_recv()
                    stage_s[jj % n_stage] = (acc_s[...] + rx_s[mask // 2 - 1].astype(F32)).astype(BF16)
                else:
                    stage_s[jj % n_stage] = acc_s[...].astype(BF16)
                send(jj).start()

        @pl.when(jnp.logical_and(j == N_DEV - 1, k == n_k - 1))
        def _():
            for jj in range(N_DEV - n_stage, N_DEV):
                cp = send(jj)
                cp.wait() if RS_ORDER[jj] == 0 else cp.wait_send()
            for mask in DIRECT_MASKS:
                dev, lin = _peer(x, y, c, mask)
                _remote(stage_s.at[0], parts_hbm.at[lin], send_sems.at[mask], recv_sems.at[mask], dev).wait_recv()
            ex.wait(ex_in, ex_out, ex_sems)

    dma = lambda n: pltpu.SemaphoreType.DMA((n,))
    grid_spec = pltpu.PrefetchScalarGridSpec(
        num_scalar_prefetch=1, grid=(N_DEV, n_k),
        in_specs=[pl.BlockSpec((tk, D_MODEL), lambda j, k, order: (k, 0)),
                  pl.BlockSpec((tk, W_IN_SHARD), lambda j, k, order: (k, order[j]))] + [ANY_SPEC] * ex.n,
        out_specs=[ANY_SPEC] * (1 + ex.n),
        scratch_shapes=[pltpu.VMEM((D_MODEL, W_IN_SHARD), F32), pltpu.VMEM((n_stage, D_MODEL, W_IN_SHARD), BF16),
                        pltpu.VMEM((len(ICI_MASKS), D_MODEL, W_IN_SHARD), BF16), dma(N_DEV), dma(N_DEV), dma(1)]
        + ex.scratch)
    res = pl.pallas_call(
        body, name="grad_w_in", grid_spec=grid_spec,
        out_shape=[SDS((N_DEV, D_MODEL, W_IN_SHARD), BF16)] + ex.out_shape,
        compiler_params=_params(("arbitrary", "arbitrary"), 44),
    )(order, hn, dz, *ex_arrs)
    return res[0], slots, res[1:]


def _sum_parts(parts, name):
    def body(p_ref, o_ref):
        g = p_ref[0].astype(F32)
        for s in range(1, parts.shape[0]):
            g = g + p_ref[s].astype(F32)
        o_ref[...] = g

    return pl.pallas_call(body, name=name, out_shape=SDS(parts.shape[1:], F32))(parts)


def _adamw_math(g, w_ref, m_ref, v_ref, g_ref, d_ref, nm_ref, nv_ref):
    c1 = 1.0 - ADAM_B1 ** ADAM_STEP
    c2 = 1.0 - ADAM_B2 ** ADAM_STEP
    g_ref[...] = g
    nm = ADAM_B1 * m_ref[...] + (1.0 - ADAM_B1) * g
    nv = ADAM_B2 * v_ref[...] + (1.0 - ADAM_B2) * (g * g)
    nm_ref[...] = nm
    nv_ref[...] = nv
    d_ref[...] = -ADAM_LR * ((nm / c1) / (jnp.sqrt(nv / c2) + ADAM_EPS) + ADAM_WD * w_ref[...])


def _adamw(parts, w, m, v, name, tr):
    rows, cols = w.shape
    n_parts = parts.shape[0]

    def body(p_ref, *refs):
        g = p_ref[0].astype(F32)
        for s in range(1, n_parts):
            g = g + p_ref[s].astype(F32)
        _adamw_math(g, *refs)

    tile = pl.BlockSpec((tr, cols), lambda i: (i, 0))
    return pl.pallas_call(
        body, name=name, grid=(rows // tr,),
        in_specs=[pl.BlockSpec((n_parts, tr, cols), lambda i: (0, i, 0)), tile, tile, tile],
        out_specs=[tile] * 4, out_shape=[SDS((rows, cols), F32)] * 4,
        compiler_params=_params(("arbitrary",), 40),
    )(parts, w, m, v)


def _adamw_slots(parts, slots, w, m, v, name, tr):
    rows, cols = w.shape
    n_slots = slots.shape[0]

    def body(slots_ref, *refs):
        g = refs[0][...].astype(F32)
        for s in range(1, n_slots):
            g = g + refs[s][...].astype(F32)
        _adamw_math(g, *refs[n_slots:])

    tile = pl.BlockSpec((tr, cols), lambda i, slots: (i, 0))
    part = lambda s: pl.BlockSpec((None, tr, cols), lambda i, slots: (slots[s], i, 0))
    grid_spec = pltpu.PrefetchScalarGridSpec(
        num_scalar_prefetch=1, grid=(rows // tr,),
        in_specs=[part(s) for s in range(n_slots)] + [tile, tile, tile], out_specs=[tile] * 4)
    return pl.pallas_call(
        body, name=name, grid_spec=grid_spec, out_shape=[SDS((rows, cols), F32)] * 4,
        compiler_params=_params(("arbitrary",), 40),
    )(slots, *([parts] * n_slots), w, m, v)


PACKED = ("gmlp_ln_g", "gmlp_ln_b", "gmlp_ws", "gmlp_bs", "conv_b", "w_a", "b_a", "w_x", "b_x", "lam", "gmlp_out_g",
          "lru_out_g", "post_g")
WEIGHTS = ("pre_g", "w_in", "gmlp_ln_g", "gmlp_ln_b", "gmlp_ws", "gmlp_bs", "conv_w", "conv_b", "w_a", "b_a", "w_x",
           "b_x", "lam", "gmlp_out_g", "lru_out_g", "w_out", "post_g", "w_pe", "w_pg")
LANES = 128


PACK_ROWS = 3200
PACK_TILE = 640
IN_BWD_TILE = 256


def _pack(parts):
    rows = [p.reshape(-1, LANES) for p in parts]
    used = sum(r.shape[0] for r in rows)
    return jnp.concatenate(rows + [jnp.zeros((PACK_ROWS - used, LANES), F32)], axis=0)


def _pad_rows(a, rows):
    return jnp.concatenate([a, jnp.zeros((rows - a.shape[0],) + a.shape[1:], a.dtype)], axis=0)


def kernel(x, p, pre_g, w_in, gmlp_ln_g, gmlp_ln_b, gmlp_ws, gmlp_bs, conv_w, conv_b, w_a, b_a, w_x, b_x, lam, gmlp_out_g, lru_out_g, w_out, post_g, w_pe, w_pg, loss_target, m_pre_g, m_w_in, m_gmlp_ln_g, m_gmlp_ln_b, m_gmlp_ws, m_gmlp_bs, m_conv_w, m_conv_b, m_w_a, m_b_a, m_w_x, m_b_x, m_lam, m_gmlp_out_g, m_lru_out_g, m_w_out, m_post_g, m_w_pe, m_w_pg, v_pre_g, v_w_in, v_gmlp_ln_g, v_gmlp_ln_b, v_gmlp_ws, v_gmlp_bs, v_conv_w, v_conv_b, v_w_a, v_b_a, v_w_x, v_b_x, v_lam, v_gmlp_out_g, v_lru_out_g, v_w_out, v_post_g, v_w_pe, v_w_pg):
    args = dict(locals())
    weights = {n: args[n] for n in WEIGHTS}
    m_in = {n: args["m_" + n] for n in WEIGHTS}
    v_in = {n: args["v_" + n] for n in WEIGHTS}
    sm = {n: weights[n][0] for n in PACKED}
    shard_rows = D_MODEL // N_DEV
    xs, ps, tgt = x[0], p[0, 0], loss_target[0]

    vec = lambda a: a.reshape(1, -1)
    tril = jnp.tril(jnp.ones((CHUNK, CHUNK), dtype=bool))
    wm32 = jnp.where(tril[None], sm["gmlp_ws"], 0.0)
    wm, wm_t = wm32.astype(BF16), jnp.swapaxes(wm32, 1, 2).astype(BF16)
    bias = jnp.repeat(sm["gmlp_bs"].T, HEAD, axis=1)
    wax32 = jnp.concatenate([sm["w_a"], sm["w_x"]], axis=2)
    wax, wax_t = wax32.astype(BF16), jnp.swapaxes(wax32, 1, 2).astype(BF16)
    ln_g, ln_b = vec(sm["gmlp_ln_g"]), vec(sm["gmlp_ln_b"])
    post_g_v = vec(sm["post_g"])

    hn = _pre_norm(xs, pre_g)
    cw_shard = _pad_rows(conv_w.reshape(CONV_W, HEAD), ROWS)
    z, w_in_g, (cw_g,) = _in_proj(hn, w_in[0].astype(BF16), [cw_shard])
    cw_full = jnp.transpose(cw_g[:, :CONV_W, :], (1, 0, 2)).reshape(CONV_W, D_BR)
    mixer_consts = dict(cw=_pad_rows(cw_full, ROWS), cb=vec(sm["conv_b"]), ba=vec(sm["b_a"]), bx=vec(sm["b_x"]),
                        lam=vec(sm["lam"]), goa=vec(sm["gmlp_out_g"]), gob=vec(sm["lru_out_g"]))
    (y, h), (w_out_g, w_pe_g, w_pg_g) = _mix_fwd(
        z, ln_g, ln_b, wm, bias, wax=wax, **mixer_consts,
        ex_arrs=[w_out[0].astype(BF16), w_pe[0].astype(BF16), w_pg[0].astype(BF16)], ex_scatter=[False, False, False])
    w_out_f, w_pg_f = w_out_g.reshape(D_MODEL, D_MODEL), w_pg_g.reshape(D_MODEL, D_MODEL)
    h1, ob = _out_proj(y, xs, w_out_f, post_g_v)
    dh2, dgl, h1b, loss_part, d_w_pe = _ple_loss(h1, ps, tgt, w_pg_f, w_pe_g)

    dh1, do, dy, d_post_g = _tail_bwd(dh2, dgl, ob, w_pg_f, w_out_f, post_g_v)
    d_w_out, _ = _grad_w(y, do, 512, False, "grad_w_out")
    d_w_pg, _ = _grad_w(h1b, dgl, 512, False, "grad_w_pg")
    (dz, vecs, d_ws, d_wax, d_bs), (parts_out, parts_pg, parts_pe) = _mix_bwd(
        z, dy, h, ln_g, ln_b, wm, wm_t, bias, wax=wax, wax_t=wax_t, **mixer_consts,
        ex_arrs=[d_w_out.reshape(N_DEV, shard_rows, D_MODEL), d_w_pg.reshape(N_DEV, shard_rows, D_MODEL), d_w_pe],
        ex_scatter=[True, True, True])

    small = {"gmlp_ln_g": vecs[V_LN_G], "gmlp_ln_b": vecs[V_LN_B], "gmlp_ws": d_ws, "gmlp_bs": d_bs,
             "conv_b": vecs[V_CONV_B], "w_a": d_wax[:, :, :HEAD], "b_a": vecs[V_B_A], "w_x": d_wax[:, :, HEAD:],
             "b_x": vecs[V_B_X], "lam": vecs[V_LAM], "gmlp_out_g": vecs[V_GOUT_A], "lru_out_g": vecs[V_GOUT_B],
             "post_g": d_post_g}
    small_part = _pack([small[n] for n in PACKED] + [loss_part]).reshape(N_DEV, PACK_ROWS // N_DEV, LANES)
    d_cw_blocks = jnp.transpose(vecs[V_CONV_W:V_CONV_W + CONV_W].reshape(CONV_W, N_DEV, HEAD), (1, 0, 2))
    d_cw_blocks = jnp.concatenate([d_cw_blocks, jnp.zeros((N_DEV, ROWS - CONV_W, HEAD), F32)], axis=1)
    parts_in, slots_in, (small_blocks, parts_cw) = _grad_w_in(
        hn, dz, ex_arrs=[small_part, d_cw_blocks], ex_scatter=[True, True])
    small_sum = _sum_parts(small_blocks, "sum_small")
    grad_x, d_pre_g, _ = _in_bwd(dz, w_in_g, xs, dh1, pre_g, 0, xs.shape[0] // IN_BWD_TILE, None, "in_bwd",
                                 tm=IN_BWD_TILE)
    pre_rows = D_MODEL // LANES
    small_all, parts_pre = _exchange([small_sum, d_pre_g.reshape(pre_rows, LANES)], False, "gather_small_grads")
    parts_small = small_all.reshape(1, PACK_ROWS, LANES)

    pad_cw = lambda a: _pad_rows(a.reshape(CONV_W, HEAD), ROWS)
    flat = lambda a: a.reshape(pre_rows, LANES)
    outs = {
        "w_in": _adamw_slots(parts_in, slots_in, w_in[0], m_w_in[0], v_w_in[0], "adamw_w_in", 256),
        "w_out": _adamw(parts_out, w_out[0], m_w_out[0], v_w_out[0], "adamw_w_out", 128),
        "w_pe": _adamw(parts_pe, w_pe[0], m_w_pe[0], v_w_pe[0], "adamw_w_pe", 256),
        "w_pg": _adamw(parts_pg, w_pg[0], m_w_pg[0], v_w_pg[0], "adamw_w_pg", 128),
        "conv_w": [a[:CONV_W] for a in
                   _adamw(parts_cw, pad_cw(conv_w), pad_cw(m_conv_w), pad_cw(v_conv_w), "adamw_conv_w", ROWS)],
        "pre_g": _adamw(parts_pre, flat(pre_g), flat(m_pre_g), flat(v_pre_g), "adamw_pre_g", pre_rows),
    }
    packed = _adamw(parts_small, _pack([weights[n] for n in PACKED]), _pack([m_in[n] for n in PACKED]),
                    _pack([v_in[n] for n in PACKED]), "adamw_small", PACK_TILE)
    row = 0
    for n in PACKED:
        n_rows = weights[n].size // LANES
        outs[n] = [packed[q][row:row + n_rows] for q in range(4)]
        row += n_rows
    loss = packed[0][row, 0]

    result = [loss, grad_x[None]]
    for q in range(4):
        result += [outs[n][q].reshape(weights[n].shape) for n in WEIGHTS]
    return tuple(result)
```

```python
import functools

import jax
import jax.numpy as jnp
from jax import lax
from jax.experimental import pallas as pl
from jax.experimental.pallas import tpu as pltpu

F32 = jnp.float32
BF16 = jnp.bfloat16
SDS = jax.ShapeDtypeStruct

D_MODEL = 2048
D_BR = 1024
D_IN = 5 * D_BR
D_PLE = 256
N_HEAD = 8
HEAD = 128
CHUNK = 128
ROWS = 8
N_GROUP = CHUNK // ROWS
N_DEV = 8
W_IN_SHARD = D_IN // N_DEV
EPS = 1e-6
LRU_C = 8.0
CONV_W = 4
MESH_AXES = ("x", "y", "c")
MIB = 1 << 20

ADAM_LR, ADAM_B1, ADAM_B2, ADAM_EPS, ADAM_WD, ADAM_STEP = 0.001, 0.9, 0.999, 1e-08, 0.01, 10

_GELU_C = 0.7978845608028654
_GELU_A = 0.044715

V_LN_G, V_LN_B, V_CONV_B, V_B_A, V_B_X, V_LAM, V_GOUT_A, V_GOUT_B, V_CONV_W = 0, 1, 2, 3, 4, 5, 6, 7, 8
N_VEC = 16


def _params(sem, vmem_mib):
    return pltpu.CompilerParams(dimension_semantics=sem, vmem_limit_bytes=int(vmem_mib * MIB))


def _sig(x):
    return 0.5 * jnp.tanh(0.5 * x) + 0.5


def _gelu(x):
    t = jnp.tanh(_GELU_C * (x + _GELU_A * x * x * x))
    return 0.5 * x * (1.0 + t), t


def _gelu_grad(x, t):
    return 0.5 * (1.0 + t) + 0.5 * x * (1.0 - t * t) * (_GELU_C * (1.0 + 3.0 * _GELU_A * x * x))


def _neg_expm1(y, exp_y):
    series = -y * (1.0 + y * (0.5 + y * (1.0 / 6.0)))
    return jnp.where(y > -0.01, series, 1.0 - exp_y)


def _softplus(x):
    return jnp.maximum(x, 0.0) + jnp.log(1.0 + jnp.exp(-jnp.abs(x)))


def _row_ids(width):
    return lax.broadcasted_iota(jnp.int32, (ROWS, width), 0)


def _shift_down(cur, prev, k, rid):
    return jnp.where(rid >= k, pltpu.roll(cur, k, 0), pltpu.roll(prev, k, 0))


def _shift_up(cur, nxt, k, rid):
    return jnp.where(rid < ROWS - k, pltpu.roll(cur, ROWS - k, 0), pltpu.roll(nxt, ROWS - k, 0))


def _mean_last(x):
    return jnp.mean(x, axis=-1, keepdims=True)


def _rows(g):
    return pl.ds(pl.multiple_of(g * ROWS, ROWS), ROWS)


TILE_ROWS = 16


def _tile_rows(q):
    return pl.ds(pl.multiple_of(q * TILE_ROWS, TILE_ROWS), TILE_ROWS)


UNROLL = 4


def _loop(n, body, init, unroll=UNROLL):
    def wide(i, carry):
        for u in range(unroll):
            carry = body(i * unroll + u, carry)
        return carry

    return lax.fori_loop(0, n // unroll, wide, init)


def _fold_rows(x):
    return x[0:ROWS, :] + x[ROWS:TILE_ROWS, :]


def _bcast_row(x, r):
    return jnp.broadcast_to(x[r:r + 1, :], x.shape)


def _dot(a, b):
    return jnp.dot(a, b, preferred_element_type=F32)


def _dot_nt(a, b):
    return lax.dot_general(a, b, (((1,), (1,)), ((), ())), preferred_element_type=F32)


def _dot_tn(a, b):
    return lax.dot_general(a, b, (((0,), (0,)), ((), ())), preferred_element_type=F32)


def _mesh_place():
    x, y, c = lax.axis_index("x"), lax.axis_index("y"), lax.axis_index("c")
    return x, y, c, 4 * x + 2 * y + c


def _peer(x, y, c, k):
    px = 1 - x if k & 4 else x
    py = 1 - y if k & 2 else y
    pc = 1 - c if k & 1 else c
    return (px, py, pc), 4 * px + 2 * py + pc


def _remote(src, dst, send_sem, recv_sem, dev):
    return pltpu.make_async_remote_copy(src_ref=src, dst_ref=dst, send_sem=send_sem, recv_sem=recv_sem, device_id=dev,
                                        device_id_type=pl.DeviceIdType.MESH)


ANY_SPEC = pl.BlockSpec(memory_space=pl.ANY)


class _Exchange:
    def __init__(self, arrs, scatter):
        self.n = len(arrs)
        self.scatter = tuple(scatter)
        self.out_shape = [SDS(a.shape if s else (N_DEV,) + a.shape, a.dtype) for a, s in zip(arrs, scatter)]
        self.scratch = [pltpu.SemaphoreType.DMA((self.n * N_DEV,)), pltpu.SemaphoreType.DMA((self.n * N_DEV,)),
                        pltpu.SemaphoreType.DMA((self.n,))]

    def _copies(self, ins, outs, sems):
        send_sems, recv_sems, local_sems = sems
        x, y, c, me = _mesh_place()
        local, sends, recvs = [], [], []
        for a in range(self.n):
            src = ins[a].at[me] if self.scatter[a] else ins[a]
            local.append(pltpu.make_async_copy(src, outs[a].at[me], local_sems.at[a]))
        for k in range(1, N_DEV):
            dev, lin = _peer(x, y, c, k)
            for a in range(self.n):
                src = ins[a].at[lin] if self.scatter[a] else ins[a]
                pair = (send_sems.at[a * N_DEV + k], recv_sems.at[a * N_DEV + k], dev)
                sends.append(_remote(src, outs[a].at[me], *pair))
                recvs.append(_remote(src, outs[a].at[lin], *pair))
        return local, sends, recvs

    def start(self, ins, outs, sems):
        local, sends, _ = self._copies(ins, outs, sems)
        for cp in local + sends:
            cp.start()

    def wait(self, ins, outs, sems):
        local, sends, recvs = self._copies(ins, outs, sems)
        for cp in recvs:
            cp.wait_recv()
        for cp in sends:
            cp.wait_send()
        for cp in local:
            cp.wait()


def _exchange(arrs, scatter, name):
    ex = _Exchange(arrs, [scatter] * len(arrs))
    n = ex.n

    def body(*refs):
        ins, outs, sems = refs[:n], refs[n:2 * n], refs[2 * n:]
        ex.start(ins, outs, sems)
        ex.wait(ins, outs, sems)

    return pl.pallas_call(
        body, name=name, out_shape=ex.out_shape, in_specs=[ANY_SPEC] * n, out_specs=[ANY_SPEC] * n,
        scratch_shapes=ex.scratch,
    )(*arrs)


def _pre_norm(x, pre_g, tm=512):
    t_len = x.shape[0]

    def body(x_ref, g_ref, hn_ref):
        g = g_ref[...]

        def rows_body(q, _):
            rows = _tile_rows(q)
            xv = x_ref[rows, :]
            hn_ref[rows, :] = (xv * lax.rsqrt(_mean_last(xv * xv) + EPS) * g).astype(BF16)
            return 0

        _loop(tm // TILE_ROWS, rows_body, 0)

    tile = pl.BlockSpec((tm, D_MODEL), lambda i: (i, 0))
    return pl.pallas_call(
        body, name="pre_norm", grid=(t_len // tm,),
        in_specs=[tile, pl.BlockSpec((1, D_MODEL), lambda i: (0, 0))], out_specs=tile,
        out_shape=SDS((t_len, D_MODEL), BF16),
        compiler_params=_params(("arbitrary",), 24),
    )(x, pre_g)


AG_ORDER = (0, 1, 2, 4, 3, 5, 6, 7)
SIBLING = 1
ICI_MASKS = (2, 4, 6)
DIRECT_MASKS = (SIBLING,) + ICI_MASKS
Y_NEIGHBOUR, X_NEIGHBOUR, DIAGONAL = 2, 4, 6
W_DIRECT = (SIBLING, Y_NEIGHBOUR, X_NEIGHBOUR)


def _in_proj(hn, w_shard, others, tm=1024):
    t_len = hn.shape[0]
    n_i = t_len // tm
    n_o = len(others)
    me_out = 4 * lax.axis_index("x") + 2 * lax.axis_index("y") + lax.axis_index("c")
    order = jnp.stack([me_out ^ k for k in AG_ORDER]).astype(jnp.int32)

    def body(order_ref, hn_ref, w_hbm, *refs):
        o_in = refs[:n_o]
        z_ref, wg_hbm = refs[n_o], refs[n_o + 1]
        o_out = refs[n_o + 2:2 * n_o + 2]
        (wbuf, send_w, recv_w, fsend_w, frecv_w, send_o, recv_o, fsend_o, frecv_o, wb_sems, loc_sems, rsend,
         rrecv) = refs[2 * n_o + 2:]
        j, i = pl.program_id(0), pl.program_id(1)
        x, y, c, me = _mesh_place()
        sib = _peer(x, y, c, SIBLING)[0]

        def relay(core):
            src, dst = (Y_NEIGHBOUR, X_NEIGHBOUR) if core == 0 else (X_NEIGHBOUR, Y_NEIGHBOUR)
            held, diag = _peer(x, y, c, src)[1], _peer(x, y, c, DIAGONAL)[1]
            pair = (rsend.at[0], rrecv.at[0], _peer(x, y, c, dst)[0])
            return _remote(wbuf.at[held], wbuf.at[held], *pair), _remote(wbuf.at[diag], wbuf.at[diag], *pair)

        def direct(k, a=None):
            dev, lin = _peer(x, y, c, k)
            if a is None:
                return (_remote(w_hbm, wbuf.at[me], send_w.at[k], recv_w.at[k], dev),
                        _remote(w_hbm, wbuf.at[lin], send_w.at[k], recv_w.at[k], dev))
            pair = (send_o.at[a * N_DEV + k], recv_o.at[a * N_DEV + k], dev)
            return _remote(o_in[a], o_out[a].at[me], *pair), _remote(o_in[a], o_out[a].at[lin], *pair)

        def passed(k, a=None):
            mine, theirs = _peer(x, y, c, k)[1], _peer(x, y, c, k ^ SIBLING)[1]
            if a is None:
                pair = (fsend_w.at[k], frecv_w.at[k], sib)
                return _remote(wbuf.at[mine], wbuf.at[mine], *pair), _remote(wbuf.at[theirs], wbuf.at[theirs], *pair)
            pair = (fsend_o.at[a * N_DEV + k], frecv_o.at[a * N_DEV + k], sib)
            return (_remote(o_out[a].at[mine], o_out[a].at[mine], *pair),
                    _remote(o_out[a].at[theirs], o_out[a].at[theirs], *pair))

        def own_copies():
            return [pltpu.make_async_copy(o_in[a], o_out[a].at[me], loc_sems.at[1 + a]) for a in range(n_o)]

        @pl.when(jnp.logical_and(j == 0, i == 0))
        def _():
            own = pltpu.make_async_copy(w_hbm, wbuf.at[me], loc_sems.at[0])
            own.start()
            for cp in own_copies():
                cp.start()
            for k in W_DIRECT:
                direct(k)[0].start()
            for k in DIRECT_MASKS:
                for a in range(n_o):
                    direct(k, a)[0].start()
            own.wait()

        for jj in range(1, N_DEV):
            mask = AG_ORDER[jj]

            @pl.when(jnp.logical_and(j == jj, i == 0))
            def _(jj=jj, mask=mask):
                if mask in W_DIRECT:
                    direct(mask)[1].wait_recv()
                    if mask != SIBLING:
                        passed(mask)[0].start()

                        @pl.when(c == (0 if mask == Y_NEIGHBOUR else 1))
                        def _():
                            relay(0 if mask == Y_NEIGHBOUR else 1)[0].start()
                elif mask == DIAGONAL:
                    for core in (0, 1):
                        @pl.when(c == core)
                        def _(core=core):
                            relay(core)[1].wait_recv()
                    passed(mask)[0].start()
                else:
                    passed(mask ^ SIBLING)[1].wait_recv()
                late = jj - (N_DEV - len(ICI_MASKS))
                if late >= 0:
                    for a in range(n_o):
                        direct(ICI_MASKS[late], a)[1].wait_recv()
                        passed(ICI_MASKS[late], a)[0].start()

        slot = order_ref[j]

        @pl.when(i == 0)
        def _():
            pltpu.make_async_copy(wbuf.at[slot], wg_hbm.at[slot], wb_sems.at[j]).start()

        z_ref[...] = _dot(hn_ref[...], wbuf[slot])

        @pl.when(jnp.logical_and(j == N_DEV - 1, i == n_i - 1))
        def _():
            for a in range(n_o):
                direct(SIBLING, a)[1].wait_recv()
            for k in ICI_MASKS:
                for a in range(n_o):
                    passed(k, a)[1].wait_recv()
            for k in W_DIRECT:
                direct(k)[0].wait_send()
            for core in (0, 1):
                @pl.when(c == core)
                def _(core=core):
                    relay(core)[0].wait_send()
            for k in DIRECT_MASKS:
                for a in range(n_o):
                    direct(k, a)[0].wait_send()
            for k in ICI_MASKS:
                passed(k)[0].wait_send()
                for a in range(n_o):
                    passed(k, a)[0].wait_send()
            for cp in own_copies():
                cp.wait()
            for jj in range(N_DEV):
                pltpu.make_async_copy(wbuf.at[0], wg_hbm.at[0], wb_sems.at[jj]).wait()

    dma = lambda n: pltpu.SemaphoreType.DMA((n,))
    grid_spec = pltpu.PrefetchScalarGridSpec(
        num_scalar_prefetch=1, grid=(N_DEV, n_i),
        in_specs=[pl.BlockSpec((tm, D_MODEL), lambda j, i, order: (i, 0)), ANY_SPEC] + [ANY_SPEC] * n_o,
        out_specs=[pl.BlockSpec((tm, W_IN_SHARD), lambda j, i, order: (i, order[j])), ANY_SPEC] + [ANY_SPEC] * n_o,
        scratch_shapes=[pltpu.VMEM((N_DEV, D_MODEL, W_IN_SHARD), BF16), dma(N_DEV), dma(N_DEV), dma(N_DEV), dma(N_DEV),
                        dma(n_o * N_DEV), dma(n_o * N_DEV), dma(n_o * N_DEV), dma(n_o * N_DEV), dma(N_DEV), dma(1 + n_o),
                        dma(1), dma(1)])
    res = pl.pallas_call(
        body, name="in_proj", grid_spec=grid_spec,
        out_shape=[SDS((t_len, D_IN), F32), SDS((N_DEV, D_MODEL, W_IN_SHARD), BF16)]
        + [SDS((N_DEV,) + o.shape, o.dtype) for o in others],
        compiler_params=_params(("arbitrary", "arbitrary"), 48),
    )(order, hn, w_shard, *others)
    return res[0], res[1], res[2:]


def _conv_rows(cur, prev, cw_ref, cb, rid):
    acc = cw_ref[3:4, :] * cur + cb
    for k in range(1, CONV_W):
        acc = acc + cw_ref[3 - k:4 - k, :] * _shift_down(cur, prev, k, rid)
    return acc


def _lru_gates(pa, px, ba, bx, sp8, first_row):
    r = _sig(pa + ba)
    i = _sig(px + bx)
    la = -(r * sp8)
    a = jnp.exp(la)
    mult = jnp.where(first_row, 1.0, jnp.sqrt(_neg_expm1(2.0 * la, a * a)))
    return r, i, a, mult


def _mix_fwd(z, ln_g, ln_b, wm, bias, cw, cb, wax, ba, bx, lam, goa, gob, ex_arrs, ex_scatter):
    t_len = z.shape[0]
    n_chunk = t_len // CHUNK
    ex = _Exchange(ex_arrs, ex_scatter)
    n_in, n_out, n_scratch = 13, 5, 7

    def body(*refs):
        (z_ref, lng_ref, lnb_ref, wm_ref, bias_ref, cw_ref, cb_ref, wax_ref, ba_ref, bx_ref, lam_ref, goa_ref,
         gob_ref) = refs[:n_in]
        ex_in = refs[n_in:n_in + ex.n]
        y_ref, h_ref, vhb_ref, xcb_ref, rs_ref = refs[n_in + ex.n:n_in + ex.n + n_out]
        ex_out = refs[n_in + ex.n + n_out:n_in + 2 * ex.n + n_out]
        vn_s, xc_s, mixed_s, pre_s, y_s, carry_s, halo_s = refs[n_in + 2 * ex.n + n_out:n_in + 2 * ex.n + n_out + n_scratch]
        ex_sems = refs[n_in + 2 * ex.n + n_out + n_scratch:]
        c_id = pl.program_id(0)
        rid = _row_ids(D_BR)

        @pl.when(c_id == 0)
        def _():
            ex.start(ex_in, ex_out, ex_sems)
            carry_s[...] = jnp.zeros_like(carry_s)
            halo_s[...] = jnp.zeros_like(halo_s)

        lng, lnb, cb = lng_ref[...], lnb_ref[...], cb_ref[...]

        def phase1(g, prev):
            rows = _rows(g)
            vg, _ = _gelu(z_ref[rows, D_BR:2 * D_BR])
            xm = vg - _mean_last(vg)
            rs = lax.rsqrt(_mean_last(xm * xm) + EPS)
            vn_s[rows, :] = xm * rs
            rs_ref[rows, :] = jnp.broadcast_to(rs, (ROWS, HEAD))
            xb = z_ref[rows, 3 * D_BR:4 * D_BR]
            xc_s[rows, :] = _conv_rows(xb, prev, cw_ref, cb, rid)
            return xb

        halo_s[...] = _loop(N_GROUP, phase1, halo_s[...])
        vhb_ref[...] = vn_s[...].astype(BF16)
        xcb_ref[...] = xc_s[...].astype(BF16)

        for h in range(N_HEAD):
            cs = slice(h * HEAD, (h + 1) * HEAD)
            mixed_s[:, cs] = _dot(wm_ref[h], (vn_s[:, cs] * lng[:, cs] + lnb[:, cs]).astype(BF16))
            pre = _dot(xcb_ref[:, cs], wax_ref[h])
            pre_s[:, cs] = pre[:, :HEAD]
            pre_s[:, D_BR + h * HEAD:D_BR + (h + 1) * HEAD] = pre[:, HEAD:]

        ba, bx, goa, gob = ba_ref[...], bx_ref[...], goa_ref[...], gob_ref[...]
        sp8 = LRU_C * _softplus(-lam_ref[...])

        def phase3(g, carry):
            rows = _rows(g)
            ug, _ = _gelu(z_ref[rows, 0:D_BR])
            ga = z_ref[rows, 2 * D_BR:3 * D_BR]
            ya = ug * (mixed_s[rows, :] + bias_ref[rows, :]) * (ga * _sig(ga))
            y_s[rows, 0:D_BR] = ya * lax.rsqrt(_mean_last(ya * ya) + EPS) * goa

            first_row = jnp.logical_and(jnp.logical_and(c_id == 0, g == 0), rid == 0)
            _, i, a, mult = _lru_gates(pre_s[rows, 0:D_BR], pre_s[rows, D_BR:2 * D_BR], ba, bx, sp8, first_row)
            b = mult * i * xc_s[rows, :]
            for d in (1, 2, 4):
                a_sh = jnp.where(rid >= d, pltpu.roll(a, d, 0), 1.0)
                b_sh = jnp.where(rid >= d, pltpu.roll(b, d, 0), 0.0)
                b = a * b_sh + b
                a = a * a_sh
            hh = b + a * carry
            h_ref[rows, :] = hh
            gb = z_ref[rows, 4 * D_BR:5 * D_BR]
            yb = hh * (gb * _sig(gb))
            y_s[rows, D_BR:2 * D_BR] = yb * lax.rsqrt(_mean_last(yb * yb) + EPS) * gob
            return _bcast_row(hh, ROWS - 1)

        carry_s[...] = _loop(N_GROUP, phase3, carry_s[...])
        y_ref[...] = y_s[...].astype(BF16)

        @pl.when(c_id == n_chunk - 1)
        def _():
            ex.wait(ex_in, ex_out, ex_sems)

    vec = pl.BlockSpec((1, D_BR), lambda i: (0, 0))
    res = pl.pallas_call(
        body, name="mix_fwd", grid=(n_chunk,),
        in_specs=[pl.BlockSpec((CHUNK, D_IN), lambda i: (i, 0)), vec, vec,
                  pl.BlockSpec((N_HEAD, HEAD, HEAD), lambda i: (0, 0, 0)),
                  pl.BlockSpec((CHUNK, D_BR), lambda i: (0, 0)),
                  pl.BlockSpec((ROWS, D_BR), lambda i: (0, 0)), vec,
                  pl.BlockSpec((N_HEAD, HEAD, 2 * HEAD), lambda i: (0, 0, 0)), vec, vec, vec, vec, vec]
        + [ANY_SPEC] * ex.n,
        out_specs=[pl.BlockSpec((CHUNK, 2 * D_BR), lambda i: (i, 0)), pl.BlockSpec((CHUNK, D_BR), lambda i: (i, 0)),
                   pl.BlockSpec((CHUNK, D_BR), lambda i: (i, 0)), pl.BlockSpec((CHUNK, D_BR), lambda i: (i, 0)),
                   pl.BlockSpec((CHUNK, HEAD), lambda i: (i, 0))] + [ANY_SPEC] * ex.n,
        out_shape=[SDS((t_len, 2 * D_BR), BF16), SDS((t_len, D_BR), F32), SDS((t_len, D_BR), BF16),
                   SDS((t_len, D_BR), BF16), SDS((t_len, HEAD), F32)] + ex.out_shape,
        scratch_shapes=[pltpu.VMEM((CHUNK, D_BR), F32), pltpu.VMEM((CHUNK, D_BR), F32), pltpu.VMEM((CHUNK, D_BR), F32),
                        pltpu.VMEM((CHUNK, 2 * D_BR), F32), pltpu.VMEM((CHUNK, 2 * D_BR), F32),
                        pltpu.VMEM((ROWS, D_BR), F32), pltpu.VMEM((ROWS, D_BR), F32)] + ex.scratch,
        compiler_params=_params(("arbitrary",), 32),
    )(z, ln_g, ln_b, wm, bias, cw, cb, wax, ba, bx, lam, goa, gob, *ex_arrs)
    return res[:n_out], res[n_out:]


def _load_weight(w_hbm, w_vmem, sem):
    @pl.when(pl.program_id(0) == 0)
    def _():
        cp = pltpu.make_async_copy(w_hbm, w_vmem, sem)
        cp.start()
        cp.wait()


def _out_proj(y, x, w_out, post_g, tm=512):
    t_len = y.shape[0]

    def body(y_ref, x_ref, w_hbm, g_ref, h1_ref, ob_ref, w_s, o_s, sem):
        _load_weight(w_hbm, w_s, sem)
        o_s[...] = _dot(y_ref[...], w_s[...])
        g = g_ref[...]

        def rows_body(q, _):
            rows = _tile_rows(q)
            o = o_s[rows, :]
            h1_ref[rows, :] = x_ref[rows, :] + o * lax.rsqrt(_mean_last(o * o) + EPS) * g
            ob_ref[rows, :] = o.astype(BF16)
            return 0

        _loop(tm // TILE_ROWS, rows_body, 0)

    tile = pl.BlockSpec((tm, D_MODEL), lambda i: (i, 0))
    return pl.pallas_call(
        body, name="out_proj", grid=(t_len // tm,),
        in_specs=[tile, tile, pl.BlockSpec(memory_space=pl.ANY), pl.BlockSpec((1, D_MODEL), lambda i: (0, 0))],
        out_specs=[tile, tile],
        out_shape=[SDS((t_len, D_MODEL), F32), SDS((t_len, D_MODEL), BF16)],
        scratch_shapes=[pltpu.VMEM((D_MODEL, D_MODEL), BF16), pltpu.VMEM((tm, D_MODEL), F32), pltpu.SemaphoreType.DMA],
        compiler_params=_params(("arbitrary",), 44),
    )(y, x, w_out, post_g)


def _ple_loss(h1, p, tgt, w_pg, w_pe_g, tm=256):
    t_len = h1.shape[0]
    n_tile = t_len // tm
    pe_shard = D_MODEL // N_DEV

    def body(h1_ref, p_ref, t_ref, w_hbm, wpe_ref, dh2_ref, dgl_ref, h1b_ref, loss_ref, dwpe_ref, w_s, pe_s, gl_s, acc_s,
             dpe_s, gpe_s, sem):
        _load_weight(w_hbm, w_s, sem)
        i = pl.program_id(0)

        @pl.when(i == 0)
        def _():
            acc_s[...] = jnp.zeros_like(acc_s)
            gpe_s[...] = jnp.zeros_like(gpe_s)

        h1b_ref[...] = h1_ref[...].astype(BF16)
        pb = p_ref[...].astype(BF16)
        for j in range(N_DEV):
            pe_s[:, j * pe_shard:(j + 1) * pe_shard] = _dot(pb, wpe_ref[j])
        gl_s[...] = _dot(h1b_ref[...], w_s[...])

        def rows_body(q, acc):
            rows = _tile_rows(q)
            pe = pe_s[rows, :]
            g = _sig(gl_s[rows, :])
            e = h1_ref[rows, :] + pe * g - t_ref[rows, :]
            dh2 = e * (1.0 / D_MODEL)
            dh2_ref[rows, :] = dh2
            dpe_s[rows, :] = (dh2 * g).astype(BF16)
            dgl_ref[rows, :] = (dh2 * pe * g * (1.0 - g)).astype(BF16)
            return acc + _fold_rows(e * e)

        acc_s[...] = _loop(tm // TILE_ROWS, rows_body, acc_s[...])
        gpe_s[...] += _dot_tn(pb, dpe_s[...])

        @pl.when(i == n_tile - 1)
        def _():
            loss_ref[...] = jnp.full(loss_ref.shape, 0.5 / D_MODEL * jnp.sum(acc_s[...]), F32)
            for j in range(N_DEV):
                dwpe_ref[j] = gpe_s[:, j * pe_shard:(j + 1) * pe_shard].astype(BF16)

    tile = pl.BlockSpec((tm, D_MODEL), lambda i: (i, 0))
    pe_blocks = pl.BlockSpec((N_DEV, D_PLE, pe_shard), lambda i: (0, 0, 0))
    return pl.pallas_call(
        body, name="ple_loss", grid=(n_tile,),
        in_specs=[tile, pl.BlockSpec((tm, D_PLE), lambda i: (i, 0)), tile, pl.BlockSpec(memory_space=pl.ANY), pe_blocks],
        out_specs=[tile, tile, tile, pl.BlockSpec((ROWS, HEAD), lambda i: (0, 0)), pe_blocks],
        out_shape=[SDS((t_len, D_MODEL), F32), SDS((t_len, D_MODEL), BF16), SDS((t_len, D_MODEL), BF16),
                   SDS((ROWS, HEAD), F32), SDS((N_DEV, D_PLE, pe_shard), BF16)],
        scratch_shapes=[pltpu.VMEM((D_MODEL, D_MODEL), BF16), pltpu.VMEM((tm, D_MODEL), F32),
                        pltpu.VMEM((tm, D_MODEL), F32), pltpu.VMEM((ROWS, D_MODEL), F32), pltpu.VMEM((tm, D_MODEL), BF16),
                        pltpu.VMEM((D_PLE, D_MODEL), F32), pltpu.SemaphoreType.DMA],
        compiler_params=_params(("arbitrary",), 48),
    )(h1, p, tgt, w_pg, w_pe_g)


def _tail_bwd(dh2, dgl, ob, w_pg, w_out, post_g, tm=256):
    t_len = dh2.shape[0]
    n_tile = t_len // tm

    def body(dh2_ref, dgl_ref, ob_ref, wpg_hbm, wout_hbm, g_ref, dh1_ref, do_ref, dy_ref, dg_ref, wpg_s, wout_s, t_s,
             acc_s, sems):
        _load_weight(wpg_hbm, wpg_s, sems.at[0])
        _load_weight(wout_hbm, wout_s, sems.at[1])
        i = pl.program_id(0)

        @pl.when(i == 0)
        def _():
            acc_s[...] = jnp.zeros_like(acc_s)

        t_s[...] = _dot_nt(dgl_ref[...], wpg_s[...])
        g = g_ref[...]

        def rows_body(q, acc):
            rows = _tile_rows(q)
            dh1 = dh2_ref[rows, :] + t_s[rows, :]
            dh1_ref[rows, :] = dh1
            o = ob_ref[rows, :].astype(F32)
            rr = lax.rsqrt(_mean_last(o * o) + EPS)
            on = o * rr
            dog = dh1 * g
            do_ref[rows, :] = (rr * (dog - on * _mean_last(dog * on))).astype(BF16)
            return acc + _fold_rows(dh1 * on)

        acc_s[...] = _loop(tm // TILE_ROWS, rows_body, acc_s[...])
        dy_ref[...] = _dot_nt(do_ref[...], wout_s[...]).astype(BF16)

        @pl.when(i == n_tile - 1)
        def _():
            dg_ref[...] = jnp.sum(acc_s[...], axis=0, keepdims=True)

    tile = pl.BlockSpec((tm, D_MODEL), lambda i: (i, 0))
    vec = pl.BlockSpec((1, D_MODEL), lambda i: (0, 0))
    hbm = pl.BlockSpec(memory_space=pl.ANY)
    return pl.pallas_call(
        body, name="tail_bwd", grid=(n_tile,),
        in_specs=[tile, tile, tile, hbm, hbm, vec],
        out_specs=[tile, tile, tile, vec],
        out_shape=[SDS((t_len, D_MODEL), F32), SDS((t_len, D_MODEL), BF16), SDS((t_len, D_MODEL), BF16),
                   SDS((1, D_MODEL), F32)],
        scratch_shapes=[pltpu.VMEM((D_MODEL, D_MODEL), BF16), pltpu.VMEM((D_MODEL, D_MODEL), BF16),
                        pltpu.VMEM((tm, D_MODEL), F32), pltpu.VMEM((ROWS, D_MODEL), F32), pltpu.SemaphoreType.DMA((2,))],
        compiler_params=_params(("arbitrary",), 48),
    )(dh2, dgl, ob, w_pg, w_out, post_g)


def _mix_bwd(z, dy, h, vhb, xcb, rs, ln_g, ln_b, wm, wm_t, bias, cw, cb, wax, wax_t, ba, bx, lam, goa, gob, ex_arrs,
             ex_scatter):
    t_len = z.shape[0]
    n_chunk = t_len // CHUNK
    halo_blocks = CHUNK // ROWS
    ex = _Exchange(ex_arrs, ex_scatter)
    n_in, n_out, n_scratch = 21, 5, 16

    def body(*refs):
        (z_ref, dy_ref, h_ref, hhalo_ref, vhb_ref, xcb_ref, rs_ref, lng_ref, lnb_ref, wm_ref, wmt_ref, bias_ref, cw_ref,
         cb_ref, wax_ref, waxt_ref, ba_ref, bx_ref, lam_ref, goa_ref, gob_ref) = refs[:n_in]
        ex_in = refs[n_in:n_in + ex.n]
        dz_ref, vecs_ref, dws_ref, dwax_ref, dbs_ref = refs[n_in + ex.n:n_in + ex.n + n_out]
        ex_out = refs[n_in + ex.n + n_out:n_in + 2 * ex.n + n_out]
        (vnb_s, vh_s, xc_s, mixed_s, pre_s, dmix_s, dvn_s, dho_s, dxc_s, dpre_s, dz_s, acc_s, accdm_s,
         cg_s, ca_s, dxchalo_s) = refs[n_in + 2 * ex.n + n_out:n_in + 2 * ex.n + n_out + n_scratch]
        ex_sems = refs[n_in + 2 * ex.n + n_out + n_scratch:]
        step = pl.program_id(0)
        c_id = n_chunk - 1 - step
        rid = _row_ids(D_BR)
        first_chunk = c_id == 0

        @pl.when(step == 0)
        def _():
            ex.start(ex_in, ex_out, ex_sems)
            acc_s[...] = jnp.zeros_like(acc_s)
            accdm_s[...] = jnp.zeros_like(accdm_s)
            cg_s[...] = jnp.zeros_like(cg_s)
            ca_s[...] = jnp.zeros_like(ca_s)
            dxchalo_s[...] = jnp.zeros_like(dxchalo_s)
            dws_ref[...] = jnp.zeros_like(dws_ref)
            dwax_ref[...] = jnp.zeros_like(dwax_ref)

        lng, lnb = lng_ref[...], lnb_ref[...]
        h_halo = jnp.where(first_chunk, 0.0, hhalo_ref[...])

        def prev_rows(ref, cols, g, halo):
            before = ref[pl.ds(pl.multiple_of(jnp.maximum(g - 1, 0) * ROWS, ROWS), ROWS), cols]
            return jnp.where(g > 0, before, halo)

        vh_s[...] = vhb_ref[...].astype(F32)
        xc_s[...] = xcb_ref[...].astype(F32)

        for hd in range(N_HEAD):
            cs = slice(hd * HEAD, (hd + 1) * HEAD)
            vnb_s[:, cs] = (vh_s[:, cs] * lng[:, cs] + lnb[:, cs]).astype(BF16)
            mixed_s[:, cs] = _dot(wm_ref[hd], vnb_s[:, cs])
            pre = _dot(xcb_ref[:, cs], wax_ref[hd])
            pre_s[:, cs] = pre[:, :HEAD]
            pre_s[:, D_BR + hd * HEAD:D_BR + (hd + 1) * HEAD] = pre[:, HEAD:]

        goa, gob = goa_ref[...], gob_ref[...]

        def phase3(g, _):
            rows = _rows(g)
            u = z_ref[rows, 0:D_BR]
            ug, tu = _gelu(u)
            ga = z_ref[rows, 2 * D_BR:3 * D_BR]
            sga = _sig(ga)
            sa = ga * sga
            mixed = mixed_s[rows, :] + bias_ref[rows, :]
            ya0 = ug * mixed
            ya = ya0 * sa
            ra = lax.rsqrt(_mean_last(ya * ya) + EPS)
            dyan = dy_ref[rows, 0:D_BR].astype(F32)
            acc_s[V_GOUT_A] += dyan * ya * ra
            dyg = dyan * goa
            dya = ra * dyg - ya * (ra * ra * ra) * _mean_last(dyg * ya)
            dya0 = dya * sa
            dz_s[rows, 2 * D_BR:3 * D_BR] = dya * ya0 * (sga * (1.0 + ga * (1.0 - sga)))
            dmix = dya0 * ug
            dmix_s[rows, :] = dmix
            accdm_s[rows, :] += dmix
            dz_s[rows, 0:D_BR] = dya0 * mixed * _gelu_grad(u, tu)

            hh = h_ref[rows, :]
            gb = z_ref[rows, 4 * D_BR:5 * D_BR]
            sgb = _sig(gb)
            sb = gb * sgb
            yb = hh * sb
            rb = lax.rsqrt(_mean_last(yb * yb) + EPS)
            dybn = dy_ref[rows, D_BR:2 * D_BR].astype(F32)
            acc_s[V_GOUT_B] += dybn * yb * rb
            dyg = dybn * gob
            dyb = rb * dyg - yb * (rb * rb * rb) * _mean_last(dyg * yb)
            dho_s[rows, :] = dyb * sb
            dz_s[rows, 4 * D_BR:5 * D_BR] = dyb * hh * (sgb * (1.0 + gb * (1.0 - sgb)))
            return 0

        _loop(N_GROUP, phase3, 0)

        for hd in range(N_HEAD):
            cs = slice(hd * HEAD, (hd + 1) * HEAD)
            dmb = dmix_s[:, cs].astype(BF16)
            dvn_s[:, cs] = _dot(wmt_ref[hd], dmb)
            dws_ref[hd] += _dot_nt(dmb, vnb_s[:, cs])

        def phase5(g, _):
            rows = _rows(g)
            dvn = dvn_s[rows, :]
            vh = vh_s[rows, :]
            acc_s[V_LN_G] += dvn * vh
            acc_s[V_LN_B] += dvn
            dvh = dvn * lng
            rs = rs_ref[rows, 0:1]
            dvg = rs * (dvh - _mean_last(dvh) - vh * _mean_last(dvh * vh))
            v = z_ref[rows, D_BR:2 * D_BR]
            _, tv = _gelu(v)
            dz_s[rows, D_BR:2 * D_BR] = dvg * _gelu_grad(v, tv)
            return 0

        _loop(N_GROUP, phase5, 0)

        ba, bx = ba_ref[...], bx_ref[...]
        sp8 = LRU_C * _softplus(-lam_ref[...])

        def phase6(k, carry):
            cg, ca = carry
            g = N_GROUP - 1 - k
            rows = _rows(g)
            first_row = jnp.logical_and(jnp.logical_and(first_chunk, g == 0), rid == 0)
            r, i, a, mult = _lru_gates(pre_s[rows, 0:D_BR], pre_s[rows, D_BR:2 * D_BR], ba, bx, sp8, first_row)
            a_nx = jnp.where(rid < ROWS - 1, pltpu.roll(a, ROWS - 1, 0), ca)
            aa, bb = a_nx, dho_s[rows, :]
            for d in (1, 2, 4):
                a_sh = jnp.where(rid < ROWS - d, pltpu.roll(aa, ROWS - d, 0), 1.0)
                b_sh = jnp.where(rid < ROWS - d, pltpu.roll(bb, ROWS - d, 0), 0.0)
                bb = aa * b_sh + bb
                aa = aa * a_sh
            gg = bb + aa * cg
            hh = h_ref[rows, :]
            hprev = _shift_down(hh, prev_rows(h_ref, slice(None), g, h_halo), 1, rid)
            xc = xc_s[rows, :]
            gx = gg * xc
            dla = gg * hprev * a - jnp.where(first_row, 0.0, gx * i * (a * a) * lax.rsqrt(mult * mult))
            acc_s[V_LAM] += -(dla * r)
            dpa = -(dla * sp8) * r * (1.0 - r)
            dpx = gx * mult * i * (1.0 - i)
            acc_s[V_B_A] += dpa
            acc_s[V_B_X] += dpx
            dpre_s[rows, 0:D_BR] = dpa
            dpre_s[rows, D_BR:2 * D_BR] = dpx
            dxc_s[rows, :] = gg * mult * i
            return _bcast_row(gg, 0), _bcast_row(a, 0)

        cg, ca = _loop(N_GROUP, phase6, (cg_s[...], ca_s[...]))
        cg_s[...] = cg
        ca_s[...] = ca

        for hd in range(N_HEAD):
            cs = slice(hd * HEAD, (hd + 1) * HEAD)
            dpre = jnp.concatenate([dpre_s[:, cs], dpre_s[:, D_BR + hd * HEAD:D_BR + (hd + 1) * HEAD]], axis=1).astype(BF16)
            dxc_s[:, cs] += _dot(dpre, waxt_ref[hd])
            dwax_ref[hd] += _dot_tn(xcb_ref[:, cs], dpre)

        def phase8(k, nxt):
            g = N_GROUP - 1 - k
            rows = _rows(g)
            dxc = dxc_s[rows, :]
            acc_s[V_CONV_B] += dxc
            xb = z_ref[rows, 3 * D_BR:4 * D_BR]
            dxb = cw_ref[3:4, :] * dxc
            acc_s[V_CONV_W + 3] += dxc * xb
            for j in range(1, CONV_W):
                later = _shift_up(dxc, nxt, j, rid)
                dxb = dxb + cw_ref[3 - j:4 - j, :] * later
                acc_s[V_CONV_W + 3 - j] += later * xb
            dz_s[rows, 3 * D_BR:4 * D_BR] = dxb
            return dxc

        dxchalo_s[...] = _loop(N_GROUP, phase8, dxchalo_s[...])
        dz_ref[...] = dz_s[...].astype(BF16)

        @pl.when(step == n_chunk - 1)
        def _():
            for v in range(N_VEC):
                vecs_ref[v:v + 1, :] = jnp.sum(acc_s[v], axis=0, keepdims=True)
            lam = lam_ref[...]
            vecs_ref[V_LAM:V_LAM + 1, :] = vecs_ref[V_LAM:V_LAM + 1, :] * (-LRU_C * _sig(-lam))
            tril = (lax.broadcasted_iota(jnp.int32, (HEAD, HEAD), 0) >= lax.broadcasted_iota(jnp.int32, (HEAD, HEAD), 1))
            ones = jnp.ones((ROWS, HEAD), BF16)
            for hd in range(N_HEAD):
                cs = slice(hd * HEAD, (hd + 1) * HEAD)
                dws_ref[hd] = jnp.where(tril, dws_ref[hd], 0.0)
                blk = accdm_s[:, cs]
                hi = blk.astype(BF16)
                lo = (blk - hi.astype(F32)).astype(BF16)
                dbs_ref[hd:hd + 1, :] = (_dot_nt(ones, hi) + _dot_nt(ones, lo))[0:1, :]
            ex.wait(ex_in, ex_out, ex_sems)

    vec = pl.BlockSpec((1, D_BR), lambda i: (0, 0))
    rev = lambda i: (n_chunk - 1 - i, 0)
    halo = lambda col: (lambda i: (jnp.maximum((n_chunk - 1 - i) * halo_blocks - 1, 0), col))
    full3 = lambda a, b, c: pl.BlockSpec((a, b, c), lambda i: (0, 0, 0))
    big = lambda w: pltpu.VMEM((CHUNK, w), F32)
    res = pl.pallas_call(
        body, name="mix_bwd", grid=(n_chunk,),
        in_specs=[pl.BlockSpec((CHUNK, D_IN), rev), pl.BlockSpec((CHUNK, 2 * D_BR), rev), pl.BlockSpec((CHUNK, D_BR), rev),
                  pl.BlockSpec((ROWS, D_BR), halo(0)), pl.BlockSpec((CHUNK, D_BR), rev), pl.BlockSpec((CHUNK, D_BR), rev),
                  pl.BlockSpec((CHUNK, HEAD), rev), vec, vec,
                  full3(N_HEAD, HEAD, HEAD), full3(N_HEAD, HEAD, HEAD),
                  pl.BlockSpec((CHUNK, D_BR), lambda i: (0, 0)), pl.BlockSpec((ROWS, D_BR), lambda i: (0, 0)), vec,
                  full3(N_HEAD, HEAD, 2 * HEAD), full3(N_HEAD, 2 * HEAD, HEAD), vec, vec, vec, vec, vec]
        + [ANY_SPEC] * ex.n,
        out_specs=[pl.BlockSpec((CHUNK, D_IN), rev), pl.BlockSpec((N_VEC, D_BR), lambda i: (0, 0)),
                   full3(N_HEAD, HEAD, HEAD), full3(N_HEAD, HEAD, 2 * HEAD),
                   pl.BlockSpec((N_HEAD, HEAD), lambda i: (0, 0))] + [ANY_SPEC] * ex.n,
        out_shape=[SDS((t_len, D_IN), BF16), SDS((N_VEC, D_BR), F32), SDS((N_HEAD, HEAD, HEAD), F32),
                   SDS((N_HEAD, HEAD, 2 * HEAD), F32), SDS((N_HEAD, HEAD), F32)] + ex.out_shape,
        scratch_shapes=[pltpu.VMEM((CHUNK, D_BR), BF16), big(D_BR), big(D_BR), big(D_BR), big(2 * D_BR), big(D_BR),
                        big(D_BR), big(D_BR), big(D_BR), big(2 * D_BR), big(D_IN),
                        pltpu.VMEM((N_VEC, ROWS, D_BR), F32), big(D_BR),
                        pltpu.VMEM((ROWS, D_BR), F32), pltpu.VMEM((ROWS, D_BR), F32), pltpu.VMEM((ROWS, D_BR), F32)]
        + ex.scratch,
        compiler_params=_params(("arbitrary",), 48),
    )(z, dy, h, h, vhb, xcb, rs, ln_g, ln_b, wm, wm_t, bias, cw, cb, wax, wax_t, ba, bx, lam, goa, gob, *ex_arrs)
    return res[:n_out], res[n_out:]


def _in_bwd(dz, w_in_g, x, dh1, pre_g, first_tile, n_tile, prev, name, ex_arrs=(), ex_scatter=(), tm=256):
    t_len = x.shape[0]
    ex = _Exchange(ex_arrs, ex_scatter)
    n_prev = 0 if prev is None else 2

    def body(dz_ref, w_hbm, x_ref, dh1_ref, g_ref, *refs):
        prev_refs, refs = refs[:n_prev], refs[n_prev:]
        ex_in, (gx_ref, dg_ref), ex_out = refs[:ex.n], refs[ex.n:ex.n + 2], refs[ex.n + 2:2 * ex.n + 2]
        w_s, t_s, dg_s, w_sems = refs[2 * ex.n + 2:2 * ex.n + 6]
        ex_sems = refs[2 * ex.n + 6:]
        i = pl.program_id(0)

        @pl.when(i == 0)
        def _():
            if ex.n:
                ex.start(ex_in, ex_out, ex_sems)
            loads = [pltpu.make_async_copy(w_hbm.at[s], w_s.at[:, s * W_IN_SHARD:(s + 1) * W_IN_SHARD], w_sems.at[s])
                     for s in range(N_DEV)]
            for cp in loads:
                cp.start()
            dg_s[...] = jnp.zeros_like(dg_s)
            for cp in loads:
                cp.wait()

        t_s[...] = _dot_nt(dz_ref[...], w_s[...])
        g = g_ref[...]

        def rows_body(q, acc):
            rows = _tile_rows(q)
            xv = x_ref[rows, :]
            r = lax.rsqrt(_mean_last(xv * xv) + EPS)
            xh = xv * r
            dhn = t_s[rows, :]
            dg = dhn * g
            gx_ref[rows, :] = dh1_ref[rows, :] + r * (dg - xh * _mean_last(dg * xh))
            return acc + _fold_rows(dhn * xh)

        dg_s[...] = _loop(tm // TILE_ROWS, rows_body, dg_s[...])

        @pl.when(i == n_tile - 1)
        def _():
            dg = jnp.sum(dg_s[...], axis=0, keepdims=True)
            dg_ref[...] = dg + prev_refs[1][...] if n_prev else dg
            if ex.n:
                ex.wait(ex_in, ex_out, ex_sems)

    tile = pl.BlockSpec((tm, D_MODEL), lambda i: (first_tile + i, 0))
    vec = pl.BlockSpec((1, D_MODEL), lambda i: (0, 0))
    prev_specs = [ANY_SPEC, vec] if n_prev else []
    res = pl.pallas_call(
        body, name=name, grid=(n_tile,),
        in_specs=[pl.BlockSpec((tm, D_IN), lambda i: (first_tile + i, 0)), ANY_SPEC, tile, tile, vec] + prev_specs
        + [ANY_SPEC] * ex.n,
        out_specs=[tile, vec] + [ANY_SPEC] * ex.n,
        out_shape=[SDS((t_len, D_MODEL), F32), SDS((1, D_MODEL), F32)] + ex.out_shape,
        scratch_shapes=[pltpu.VMEM((D_MODEL, D_IN), BF16), pltpu.VMEM((tm, D_MODEL), F32), pltpu.VMEM((ROWS, D_MODEL), F32),
                        pltpu.SemaphoreType.DMA((N_DEV,))] + (ex.scratch if ex.n else []),
        input_output_aliases={5: 0} if n_prev else {},
        compiler_params=_params(("arbitrary",), 54),
    )(dz, w_in_g, x, dh1, pre_g, *(prev or ()), *ex_arrs)
    return res[0], res[1], res[2:]


def _grad_w(a, b, bn, shard_major, name, tk=1024, ex_arrs=(), ex_scatter=()):
    t_len, m = a.shape
    n = b.shape[1]
    n_j, n_k = n // bn, t_len // tk
    ex = _Exchange(ex_arrs, ex_scatter)

    def body(a_ref, b_ref, *refs):
        ex_in, o_ref, ex_out = refs[:ex.n], refs[ex.n], refs[ex.n + 1:2 * ex.n + 1]
        acc_s, ex_sems = refs[2 * ex.n + 1], refs[2 * ex.n + 2:]
        j, k = pl.program_id(0), pl.program_id(1)
        if ex.n:
            @pl.when(jnp.logical_and(j == 0, k == 0))
            def _():
                ex.start(ex_in, ex_out, ex_sems)

        @pl.when(k == 0)
        def _():
            acc_s[...] = jnp.zeros_like(acc_s)

        acc_s[...] += _dot_tn(a_ref[...], b_ref[...])

        @pl.when(k == n_k - 1)
        def _():
            o_ref[...] = acc_s[...].astype(BF16)

        if ex.n:
            @pl.when(jnp.logical_and(j == n_j - 1, k == n_k - 1))
            def _():
                ex.wait(ex_in, ex_out, ex_sems)

    if shard_major:
        out_spec, out_shape = pl.BlockSpec((None, m, bn), lambda j, k: (j, 0, 0)), SDS((n_j, m, bn), BF16)
    else:
        out_spec, out_shape = pl.BlockSpec((m, bn), lambda j, k: (0, j)), SDS((m, n), BF16)
    res = pl.pallas_call(
        body, name=name, grid=(n_j, n_k),
        in_specs=[pl.BlockSpec((tk, m), lambda j, k: (k, 0)), pl.BlockSpec((tk, bn), lambda j, k: (k, j))]
        + [ANY_SPEC] * ex.n,
        out_specs=[out_spec] + [ANY_SPEC] * ex.n, out_shape=[out_shape] + ex.out_shape,
        scratch_shapes=[pltpu.VMEM((m, bn), F32)] + (ex.scratch if ex.n else []),
        compiler_params=_params(("arbitrary", "arbitrary"), 40),
    )(a, b, *ex_arrs)
    return res[0], res[1:]


RS_ORDER = (3, 2, 5, 4, 7, 6, 1, 0)
RS_SLOTS = (0, 1, 2, 4, 6)


def _grad_w_in(hn, dz, ex_arrs, ex_scatter, tk=1024):
    t_len = hn.shape[0]
    n_k = t_len // tk
    ex = _Exchange(ex_arrs, ex_scatter)
    me_out = 4 * lax.axis_index("x") + 2 * lax.axis_index("y") + lax.axis_index("c")
    order = jnp.stack([me_out ^ k for k in RS_ORDER]).astype(jnp.int32)
    slots = jnp.stack([me_out ^ k for k in RS_SLOTS]).astype(jnp.int32)
    n_stage = 2

    def body(order_ref, a_ref, b_ref, *refs):
        ex_in, parts_hbm, ex_out = refs[:ex.n], refs[ex.n], refs[ex.n + 1:2 * ex.n + 1]
        acc_s, stage_s, rx_s, send_sems, recv_sems, loc_sem = refs[2 * ex.n + 1:2 * ex.n + 7]
        ex_sems = refs[2 * ex.n + 7:]
        j, k = pl.program_id(0), pl.program_id(1)
        x, y, c, me = _mesh_place()
        sib = _peer(x, y, c, SIBLING)[0]

        def send(jj):
            mask, src = RS_ORDER[jj], stage_s.at[jj % n_stage]
            if mask == 0:
                return pltpu.make_async_copy(src, parts_hbm.at[me], loc_sem.at[0])
            pair = (send_sems.at[mask], recv_sems.at[mask])
            if mask in ICI_MASKS or mask == SIBLING:
                return _remote(src, parts_hbm.at[me], *pair, _peer(x, y, c, mask)[0])
            return _remote(src, rx_s.at[mask // 2 - 1], *pair, sib)

        def from_sibling(mask):
            return _remote(stage_s.at[0], rx_s.at[mask // 2 - 1], send_sems.at[mask], recv_sems.at[mask], sib)

        @pl.when(jnp.logical_and(j == 0, k == 0))
        def _():
            ex.start(ex_in, ex_out, ex_sems)

        @pl.when(k == 0)
        def _():
            acc_s[...] = jnp.zeros_like(acc_s)

        acc_s[...] += _dot_tn(a_ref[...], b_ref[...])

        for jj in range(N_DEV):
            @pl.when(jnp.logical_and(j == jj, k == n_k - 1))
            def _(jj=jj):
                mask = RS_ORDER[jj]
                if jj >= n_stage:
                    send(jj - n_stage).wait_send()
                if mask in ICI_MASKS:
                    from_sibling(mask + 1).wait_recv()
                    stage_s[jj % n_stage] = (acc_s[...] + rx_s[mask // 2 - 1].astype(F32)).astype(BF16)
                else:
                    stage_s[jj % n_stage] = acc_s[...].astype(BF16)
                send(jj).start()

        @pl.when(jnp.logical_and(j == N_DEV - 1, k == n_k - 1))
        def _():
            for jj in range(N_DEV - n_stage, N_DEV):
                cp = send(jj)
                cp.wait() if RS_ORDER[jj] == 0 else cp.wait_send()
            for mask in DIRECT_MASKS:
                dev, lin = _peer(x, y, c, mask)
                _remote(stage_s.at[0], parts_hbm.at[lin], send_sems.at[mask], recv_sems.at[mask], dev).wait_recv()
            ex.wait(ex_in, ex_out, ex_sems)

    dma = lambda n: pltpu.SemaphoreType.DMA((n,))
    grid_spec = pltpu.PrefetchScalarGridSpec(
        num_scalar_prefetch=1, grid=(N_DEV, n_k),
        in_specs=[pl.BlockSpec((tk, D_MODEL), lambda j, k, order: (k, 0)),
                  pl.BlockSpec((tk, W_IN_SHARD), lambda j, k, order: (k, order[j]))] + [ANY_SPEC] * ex.n,
        out_specs=[ANY_SPEC] * (1 + ex.n),
        scratch_shapes=[pltpu.VMEM((D_MODEL, W_IN_SHARD), F32), pltpu.VMEM((n_stage, D_MODEL, W_IN_SHARD), BF16),
                        pltpu.VMEM((len(ICI_MASKS), D_MODEL, W_IN_SHARD), BF16), dma(N_DEV), dma(N_DEV), dma(1)]
        + ex.scratch)
    res = pl.pallas_call(
        body, name="grad_w_in", grid_spec=grid_spec,
        out_shape=[SDS((N_DEV, D_MODEL, W_IN_SHARD), BF16)] + ex.out_shape,
        compiler_params=_params(("arbitrary", "arbitrary"), 44),
    )(order, hn, dz, *ex_arrs)
    return res[0], slots, res[1:]


def _sum_parts(parts, name):
    def body(p_ref, o_ref):
        g = p_ref[0].astype(F32)
        for s in range(1, parts.shape[0]):
            g = g + p_ref[s].astype(F32)
        o_ref[...] = g

    return pl.pallas_call(body, name=name, out_shape=SDS(parts.shape[1:], F32))(parts)


def _adamw_math(g, w_ref, m_ref, v_ref, g_ref, d_ref, nm_ref, nv_ref):
    c1 = 1.0 - ADAM_B1 ** ADAM_STEP
    c2 = 1.0 - ADAM_B2 ** ADAM_STEP
    g_ref[...] = g
    nm = ADAM_B1 * m_ref[...] + (1.0 - ADAM_B1) * g
    nv = ADAM_B2 * v_ref[...] + (1.0 - ADAM_B2) * (g * g)
    nm_ref[...] = nm
    nv_ref[...] = nv
    d_ref[...] = -ADAM_LR * ((nm / c1) / (jnp.sqrt(nv / c2) + ADAM_EPS) + ADAM_WD * w_ref[...])


def _adamw(parts, w, m, v, name, tr):
    rows, cols = w.shape
    n_parts = parts.shape[0]

    def body(p_ref, *refs):
        g = p_ref[0].astype(F32)
        for s in range(1, n_parts):
            g = g + p_ref[s].astype(F32)
        _adamw_math(g, *refs)

    tile = pl.BlockSpec((tr, cols), lambda i: (i, 0))
    return pl.pallas_call(
        body, name=name, grid=(rows // tr,),
        in_specs=[pl.BlockSpec((n_parts, tr, cols), lambda i: (0, i, 0)), tile, tile, tile],
        out_specs=[tile] * 4, out_shape=[SDS((rows, cols), F32)] * 4,
        compiler_params=_params(("arbitrary",), 40),
    )(parts, w, m, v)


def _adamw_slots(parts, slots, w, m, v, name, tr):
    rows, cols = w.shape
    n_slots = slots.shape[0]

    def body(slots_ref, *refs):
        g = refs[0][...].astype(F32)
        for s in range(1, n_slots):
            g = g + refs[s][...].astype(F32)
        _adamw_math(g, *refs[n_slots:])

    tile = pl.BlockSpec((tr, cols), lambda i, slots: (i, 0))
    part = lambda s: pl.BlockSpec((None, tr, cols), lambda i, slots: (slots[s], i, 0))
    grid_spec = pltpu.PrefetchScalarGridSpec(
        num_scalar_prefetch=1, grid=(rows // tr,),
        in_specs=[part(s) for s in range(n_slots)] + [tile, tile, tile], out_specs=[tile] * 4)
    return pl.pallas_call(
        body, name=name, grid_spec=grid_spec, out_shape=[SDS((rows, cols), F32)] * 4,
        compiler_params=_params(("arbitrary",), 40),
    )(slots, *([parts] * n_slots), w, m, v)


PACKED = ("gmlp_ln_g", "gmlp_ln_b", "gmlp_ws", "gmlp_bs", "conv_b", "w_a", "b_a", "w_x", "b_x", "lam", "gmlp_out_g",
          "lru_out_g", "post_g")
WEIGHTS = ("pre_g", "w_in", "gmlp_ln_g", "gmlp_ln_b", "gmlp_ws", "gmlp_bs", "conv_w", "conv_b", "w_a", "b_a", "w_x",
           "b_x", "lam", "gmlp_out_g", "lru_out_g", "w_out", "post_g", "w_pe", "w_pg")
LANES = 128


PACK_ROWS = 3200
PACK_TILE = 640
IN_BWD_TILE = 256


def _pack(parts):
    rows = [p.reshape(-1, LANES) for p in parts]
    used = sum(r.shape[0] for r in rows)
    return jnp.concatenate(rows + [jnp.zeros((PACK_ROWS - used, LANES), F32)], axis=0)


def _pad_rows(a, rows):
    return jnp.concatenate([a, jnp.zeros((rows - a.shape[0],) + a.shape[1:], a.dtype)], axis=0)


def kernel(x, p, pre_g, w_in, gmlp_ln_g, gmlp_ln_b, gmlp_ws, gmlp_bs, conv_w, conv_b, w_a, b_a, w_x, b_x, lam, gmlp_out_g, lru_out_g, w_out, post_g, w_pe, w_pg, loss_target, m_pre_g, m_w_in, m_gmlp_ln_g, m_gmlp_ln_b, m_gmlp_ws, m_gmlp_bs, m_conv_w, m_conv_b, m_w_a, m_b_a, m_w_x, m_b_x, m_lam, m_gmlp_out_g, m_lru_out_g, m_w_out, m_post_g, m_w_pe, m_w_pg, v_pre_g, v_w_in, v_gmlp_ln_g, v_gmlp_ln_b, v_gmlp_ws, v_gmlp_bs, v_conv_w, v_conv_b, v_w_a, v_b_a, v_w_x, v_b_x, v_lam, v_gmlp_out_g, v_lru_out_g, v_w_out, v_post_g, v_w_pe, v_w_pg):
    args = dict(locals())
    weights = {n: args[n] for n in WEIGHTS}
    m_in = {n: args["m_" + n] for n in WEIGHTS}
    v_in = {n: args["v_" + n] for n in WEIGHTS}
    sm = {n: weights[n][0] for n in PACKED}
    shard_rows = D_MODEL // N_DEV
    xs, ps, tgt = x[0], p[0, 0], loss_target[0]

    vec = lambda a: a.reshape(1, -1)
    tril = jnp.tril(jnp.ones((CHUNK, CHUNK), dtype=bool))
    wm32 = jnp.where(tril[None], sm["gmlp_ws"], 0.0)
    wm, wm_t = wm32.astype(BF16), jnp.swapaxes(wm32, 1, 2).astype(BF16)
    bias = jnp.repeat(sm["gmlp_bs"].T, HEAD, axis=1)
    wax32 = jnp.concatenate([sm["w_a"], sm["w_x"]], axis=2)
    wax, wax_t = wax32.astype(BF16), jnp.swapaxes(wax32, 1, 2).astype(BF16)
    ln_g, ln_b = vec(sm["gmlp_ln_g"]), vec(sm["gmlp_ln_b"])
    post_g_v = vec(sm["post_g"])

    hn = _pre_norm(xs, pre_g)
    cw_shard = _pad_rows(conv_w.reshape(CONV_W, HEAD), ROWS)
    z, w_in_g, (cw_g,) = _in_proj(hn, w_in[0].astype(BF16), [cw_shard])
    cw_full = jnp.transpose(cw_g[:, :CONV_W, :], (1, 0, 2)).reshape(CONV_W, D_BR)
    mixer_consts = dict(cw=_pad_rows(cw_full, ROWS), cb=vec(sm["conv_b"]), ba=vec(sm["b_a"]), bx=vec(sm["b_x"]),
                        lam=vec(sm["lam"]), goa=vec(sm["gmlp_out_g"]), gob=vec(sm["lru_out_g"]))
    (y, h, vhb, xcb, v_rs), (w_out_g, w_pe_g, w_pg_g) = _mix_fwd(
        z, ln_g, ln_b, wm, bias, wax=wax, **mixer_consts,
        ex_arrs=[w_out[0].astype(BF16), w_pe[0].astype(BF16), w_pg[0].astype(BF16)], ex_scatter=[False, False, False])
    w_out_f, w_pg_f = w_out_g.reshape(D_MODEL, D_MODEL), w_pg_g.reshape(D_MODEL, D_MODEL)
    h1, ob = _out_proj(y, xs, w_out_f, post_g_v)
    dh2, dgl, h1b, loss_part, d_w_pe = _ple_loss(h1, ps, tgt, w_pg_f, w_pe_g)

    dh1, do, dy, d_post_g = _tail_bwd(dh2, dgl, ob, w_pg_f, w_out_f, post_g_v)
    d_w_out, _ = _grad_w(y, do, 512, False, "grad_w_out")
    d_w_pg, _ = _grad_w(h1b, dgl, 512, False, "grad_w_pg")
    (dz, vecs, d_ws, d_wax, d_bs), (parts_out, parts_pg, parts_pe) = _mix_bwd(
        z, dy, h, vhb, xcb, v_rs, ln_g, ln_b, wm, wm_t, bias, wax=wax, wax_t=wax_t, **mixer_consts,
        ex_arrs=[d_w_out.reshape(N_DEV, shard_rows, D_MODEL), d_w_pg.reshape(N_DEV, shard_rows, D_MODEL), d_w_pe],
        ex_scatter=[True, True, True])

    small = {"gmlp_ln_g": vecs[V_LN_G], "gmlp_ln_b": vecs[V_LN_B], "gmlp_ws": d_ws, "gmlp_bs": d_bs,
             "conv_b": vecs[V_CONV_B], "w_a": d_wax[:, :, :HEAD], "b_a": vecs[V_B_A], "w_x": d_wax[:, :, HEAD:],
             "b_x": vecs[V_B_X], "lam": vecs[V_LAM], "gmlp_out_g": vecs[V_GOUT_A], "lru_out_g": vecs[V_GOUT_B],
             "post_g": d_post_g}
    small_part = _pack([small[n] for n in PACKED] + [loss_part]).reshape(N_DEV, PACK_ROWS // N_DEV, LANES)
    d_cw_blocks = jnp.transpose(vecs[V_CONV_W:V_CONV_W + CONV_W].reshape(CONV_W, N_DEV, HEAD), (1, 0, 2))
    d_cw_blocks = jnp.concatenate([d_cw_blocks, jnp.zeros((N_DEV, ROWS - CONV_W, HEAD), F32)], axis=1)
    parts_in, slots_in, (small_blocks, parts_cw) = _grad_w_in(
        hn, dz, ex_arrs=[small_part, d_cw_blocks], ex_scatter=[True, True])
    small_sum = _sum_parts(small_blocks, "sum_small")
    grad_x, d_pre_g, _ = _in_bwd(dz, w_in_g, xs, dh1, pre_g, 0, xs.shape[0] // IN_BWD_TILE, None, "in_bwd",
                                 tm=IN_BWD_TILE)
    pre_rows = D_MODEL // LANES
    small_all, parts_pre = _exchange([small_sum, d_pre_g.reshape(pre_rows, LANES)], False, "gather_small_grads")
    parts_small = small_all.reshape(1, PACK_ROWS, LANES)

    pad_cw = lambda a: _pad_rows(a.reshape(CONV_W, HEAD), ROWS)
    flat = lambda a: a.reshape(pre_rows, LANES)
    outs = {
        "w_in": _adamw_slots(parts_in, slots_in, w_in[0], m_w_in[0], v_w_in[0], "adamw_w_in", 256),
        "w_out": _adamw(parts_out, w_out[0], m_w_out[0], v_w_out[0], "adamw_w_out", 128),
        "w_pe": _adamw(parts_pe, w_pe[0], m_w_pe[0], v_w_pe[0], "adamw_w_pe", 256),
        "w_pg": _adamw(parts_pg, w_pg[0], m_w_pg[0], v_w_pg[0], "adamw_w_pg", 128),
        "conv_w": [a[:CONV_W] for a in
                   _adamw(parts_cw, pad_cw(conv_w), pad_cw(m_conv_w), pad_cw(v_conv_w), "adamw_conv_w", ROWS)],
        "pre_g": _adamw(parts_pre, flat(pre_g), flat(m_pre_g), flat(v_pre_g), "adamw_pre_g", pre_rows),
    }
    packed = _adamw(parts_small, _pack([weights[n] for n in PACKED]), _pack([m_in[n] for n in PACKED]),
                    _pack([v_in[n] for n in PACKED]), "adamw_small", PACK_TILE)
    row = 0
    for n in PACKED:
        n_rows = weights[n].size // LANES
        outs[n] = [packed[q][row:row + n_rows] for q in range(4)]
        row += n_rows
    loss = packed[0][row, 0]

    result = [loss, grad_x[None]]
    for q in range(4):
        result += [outs[n][q].reshape(weights[n].shape) for n in WEIGHTS]
    return tuple(result)
```

```python
import functools

import jax
import jax.numpy as jnp
from jax import lax
from jax.experimental import pallas as pl
from jax.experimental.pallas import tpu as pltpu

F32 = jnp.float32
BF16 = jnp.bfloat16
SDS = jax.ShapeDtypeStruct

D_MODEL = 2048
D_BR = 1024
D_IN = 5 * D_BR
D_PLE = 256
N_HEAD = 8
HEAD = 128
CHUNK = 128
ROWS = 8
N_GROUP = CHUNK // ROWS
N_DEV = 8
W_IN_SHARD = D_IN // N_DEV
EPS = 1e-6
LRU_C = 8.0
CONV_W = 4
MESH_AXES = ("x", "y", "c")
MIB = 1 << 20

ADAM_LR, ADAM_B1, ADAM_B2, ADAM_EPS, ADAM_WD, ADAM_STEP = 0.001, 0.9, 0.999, 1e-08, 0.01, 10

_GELU_C = 0.7978845608028654
_GELU_A = 0.044715

V_LN_G, V_LN_B, V_CONV_B, V_B_A, V_B_X, V_LAM, V_GOUT_A, V_GOUT_B, V_CONV_W = 0, 1, 2, 3, 4, 5, 6, 7, 8
N_VEC = 16


def _params(sem, vmem_mib):
    return pltpu.CompilerParams(dimension_semantics=sem, vmem_limit_bytes=int(vmem_mib * MIB))


def _sig(x):
    return 0.5 * jnp.tanh(0.5 * x) + 0.5


def _gelu(x):
    t = jnp.tanh(_GELU_C * (x + _GELU_A * x * x * x))
    return 0.5 * x * (1.0 + t), t


def _gelu_grad(x, t):
    return 0.5 * (1.0 + t) + 0.5 * x * (1.0 - t * t) * (_GELU_C * (1.0 + 3.0 * _GELU_A * x * x))


def _neg_expm1(y, exp_y):
    series = -y * (1.0 + y * (0.5 + y * (1.0 / 6.0)))
    return jnp.where(y > -0.01, series, 1.0 - exp_y)


def _softplus(x):
    return jnp.maximum(x, 0.0) + jnp.log(1.0 + jnp.exp(-jnp.abs(x)))


def _row_ids(width):
    return lax.broadcasted_iota(jnp.int32, (ROWS, width), 0)


def _shift_down(cur, prev, k, rid):
    return jnp.where(rid >= k, pltpu.roll(cur, k, 0), pltpu.roll(prev, k, 0))


def _shift_up(cur, nxt, k, rid):
    return jnp.where(rid < ROWS - k, pltpu.roll(cur, ROWS - k, 0), pltpu.roll(nxt, ROWS - k, 0))


def _mean_last(x):
    return jnp.mean(x, axis=-1, keepdims=True)


def _rows(g):
    return pl.ds(pl.multiple_of(g * ROWS, ROWS), ROWS)


TILE_ROWS = 16


def _tile_rows(q):
    return pl.ds(pl.multiple_of(q * TILE_ROWS, TILE_ROWS), TILE_ROWS)


UNROLL = 4
TILE_UNROLL = 8


def _loop(n, body, init, unroll=UNROLL):
    def wide(i, carry):
        for u in range(unroll):
            carry = body(i * unroll + u, carry)
        return carry

    return lax.fori_loop(0, n // unroll, wide, init)


def _fold_rows(x):
    return x[0:ROWS, :] + x[ROWS:TILE_ROWS, :]


def _bcast_row(x, r):
    return jnp.broadcast_to(x[r:r + 1, :], x.shape)


def _dot(a, b):
    return jnp.dot(a, b, preferred_element_type=F32)


def _dot_nt(a, b):
    return lax.dot_general(a, b, (((1,), (1,)), ((), ())), preferred_element_type=F32)


def _dot_tn(a, b):
    return lax.dot_general(a, b, (((0,), (0,)), ((), ())), preferred_element_type=F32)


def _mesh_place():
    x, y, c = lax.axis_index("x"), lax.axis_index("y"), lax.axis_index("c")
    return x, y, c, 4 * x + 2 * y + c


def _peer(x, y, c, k):
    px = 1 - x if k & 4 else x
    py = 1 - y if k & 2 else y
    pc = 1 - c if k & 1 else c
    return (px, py, pc), 4 * px + 2 * py + pc


def _remote(src, dst, send_sem, recv_sem, dev):
    return pltpu.make_async_remote_copy(src_ref=src, dst_ref=dst, send_sem=send_sem, recv_sem=recv_sem, device_id=dev,
                                        device_id_type=pl.DeviceIdType.MESH)


ANY_SPEC = pl.BlockSpec(memory_space=pl.ANY)


class _Exchange:
    def __init__(self, arrs, scatter):
        self.n = len(arrs)
        self.scatter = tuple(scatter)
        self.out_shape = [SDS(a.shape if s else (N_DEV,) + a.shape, a.dtype) for a, s in zip(arrs, scatter)]
        self.scratch = [pltpu.SemaphoreType.DMA((self.n * N_DEV,)), pltpu.SemaphoreType.DMA((self.n * N_DEV,)),
                        pltpu.SemaphoreType.DMA((self.n,))]

    def _copies(self, ins, outs, sems):
        send_sems, recv_sems, local_sems = sems
        x, y, c, me = _mesh_place()
        local, sends, recvs = [], [], []
        for a in range(self.n):
            src = ins[a].at[me] if self.scatter[a] else ins[a]
            local.append(pltpu.make_async_copy(src, outs[a].at[me], local_sems.at[a]))
        for k in range(1, N_DEV):
            dev, lin = _peer(x, y, c, k)
            for a in range(self.n):
                src = ins[a].at[lin] if self.scatter[a] else ins[a]
                pair = (send_sems.at[a * N_DEV + k], recv_sems.at[a * N_DEV + k], dev)
                sends.append(_remote(src, outs[a].at[me], *pair))
                recvs.append(_remote(src, outs[a].at[lin], *pair))
        return local, sends, recvs

    def start(self, ins, outs, sems):
        local, sends, _ = self._copies(ins, outs, sems)
        for cp in local + sends:
            cp.start()

    def wait(self, ins, outs, sems):
        local, sends, recvs = self._copies(ins, outs, sems)
        for cp in recvs:
            cp.wait_recv()
        for cp in sends:
            cp.wait_send()
        for cp in local:
            cp.wait()


def _exchange(arrs, scatter, name):
    ex = _Exchange(arrs, [scatter] * len(arrs))
    n = ex.n

    def body(*refs):
        ins, outs, sems = refs[:n], refs[n:2 * n], refs[2 * n:]
        ex.start(ins, outs, sems)
        ex.wait(ins, outs, sems)

    return pl.pallas_call(
        body, name=name, out_shape=ex.out_shape, in_specs=[ANY_SPEC] * n, out_specs=[ANY_SPEC] * n,
        scratch_shapes=ex.scratch,
    )(*arrs)


def _pre_norm(x, pre_g, tm=512):
    t_len = x.shape[0]

    def body(x_ref, g_ref, hn_ref):
        g = g_ref[...]

        def rows_body(q, _):
            rows = _tile_rows(q)
            xv = x_ref[rows, :]
            hn_ref[rows, :] = (xv * lax.rsqrt(_mean_last(xv * xv) + EPS) * g).astype(BF16)
            return 0

        _loop(tm // TILE_ROWS, rows_body, 0, unroll=TILE_UNROLL)

    tile = pl.BlockSpec((tm, D_MODEL), lambda i: (i, 0))
    return pl.pallas_call(
        body, name="pre_norm", grid=(t_len // tm,),
        in_specs=[tile, pl.BlockSpec((1, D_MODEL), lambda i: (0, 0))], out_specs=tile,
        out_shape=SDS((t_len, D_MODEL), BF16),
        compiler_params=_params(("arbitrary",), 24),
    )(x, pre_g)


CHIP_ORDER = (0, 2, 4, 6)
W_BODY, W_TAIL = 512, 128
SIBLING = 1
ICI_MASKS = (2, 4, 6)
DIRECT_MASKS = (SIBLING,) + ICI_MASKS
Y_NEIGHBOUR, X_NEIGHBOUR, DIAGONAL = 2, 4, 6
W_DIRECT = (SIBLING, Y_NEIGHBOUR, X_NEIGHBOUR)


def _in_proj(hn, w_shard, others, tm=1024):
    t_len = hn.shape[0]
    n_i = t_len // tm
    n_o = len(others)
    me_out = 4 * lax.axis_index("x") + 2 * lax.axis_index("y") + lax.axis_index("c")
    order = jnp.stack([(me_out ^ chip) // 2 for chip in CHIP_ORDER]).astype(jnp.int32)

    def body(order_ref, hn_ref, w_hbm, *refs):
        o_in = refs[:n_o]
        z_ref, wg_hbm = refs[n_o], refs[n_o + 1]
        o_out = refs[n_o + 2:2 * n_o + 2]
        (wbuf, tail_s, send_w, recv_w, fsend_w, frecv_w, send_o, recv_o, fsend_o, frecv_o, wb_sems, loc_sems, rsend,
         rrecv) = refs[2 * n_o + 2:]
        j, i = pl.program_id(0), pl.program_id(1)
        x, y, c, me = _mesh_place()
        sib = _peer(x, y, c, SIBLING)[0]

        def relay(core):
            src, dst = (Y_NEIGHBOUR, X_NEIGHBOUR) if core == 0 else (X_NEIGHBOUR, Y_NEIGHBOUR)
            held, diag = _peer(x, y, c, src)[1], _peer(x, y, c, DIAGONAL)[1]
            pair = (rsend.at[0], rrecv.at[0], _peer(x, y, c, dst)[0])
            return _remote(wbuf.at[held], wbuf.at[held], *pair), _remote(wbuf.at[diag], wbuf.at[diag], *pair)

        def direct(k, a=None):
            dev, lin = _peer(x, y, c, k)
            if a is None:
                return (_remote(w_hbm, wbuf.at[me], send_w.at[k], recv_w.at[k], dev),
                        _remote(w_hbm, wbuf.at[lin], send_w.at[k], recv_w.at[k], dev))
            pair = (send_o.at[a * N_DEV + k], recv_o.at[a * N_DEV + k], dev)
            return _remote(o_in[a], o_out[a].at[me], *pair), _remote(o_in[a], o_out[a].at[lin], *pair)

        def passed(k, a=None):
            mine, theirs = _peer(x, y, c, k)[1], _peer(x, y, c, k ^ SIBLING)[1]
            if a is None:
                pair = (fsend_w.at[k], frecv_w.at[k], sib)
                return _remote(wbuf.at[mine], wbuf.at[mine], *pair), _remote(wbuf.at[theirs], wbuf.at[theirs], *pair)
            pair = (fsend_o.at[a * N_DEV + k], frecv_o.at[a * N_DEV + k], sib)
            return (_remote(o_out[a].at[mine], o_out[a].at[mine], *pair),
                    _remote(o_out[a].at[theirs], o_out[a].at[theirs], *pair))

        def own_copies():
            return [pltpu.make_async_copy(o_in[a], o_out[a].at[me], loc_sems.at[1 + a]) for a in range(n_o)]

        @pl.when(jnp.logical_and(j == 0, i == 0))
        def _():
            own = pltpu.make_async_copy(w_hbm, wbuf.at[me], loc_sems.at[0])
            own.start()
            for cp in own_copies():
                cp.start()
            for k in W_DIRECT:
                direct(k)[0].start()
            for k in DIRECT_MASKS:
                for a in range(n_o):
                    direct(k, a)[0].start()
            own.wait()

        low = 2 * order_ref[j]

        for jp, chip in enumerate(CHIP_ORDER):
            @pl.when(jnp.logical_and(j == jp, i == 0))
            def _(jp=jp, chip=chip):
                if chip == 0:
                    direct(SIBLING)[1].wait_recv()
                elif chip == Y_NEIGHBOUR:
                    for mask in (Y_NEIGHBOUR, X_NEIGHBOUR):
                        direct(mask)[1].wait_recv()
                        passed(mask)[0].start()
                    for core in (0, 1):
                        @pl.when(c == core)
                        def _(core=core):
                            relay(core)[0].start()
                    passed(Y_NEIGHBOUR)[1].wait_recv()
                elif chip == X_NEIGHBOUR:
                    passed(X_NEIGHBOUR)[1].wait_recv()
                    for core in (0, 1):
                        @pl.when(c == core)
                        def _(core=core):
                            relay(core)[1].wait_recv()
                    passed(DIAGONAL)[0].start()
                    for k in ICI_MASKS:
                        for a in range(n_o):
                            direct(k, a)[1].wait_recv()
                            passed(k, a)[0].start()
                else:
                    passed(DIAGONAL)[1].wait_recv()
                for half in (0, 1):
                    pltpu.make_async_copy(wbuf.at[low + half], wg_hbm.at[low + half], wb_sems.at[2 * jp + half]).start()
                tail_s[:, 0:W_TAIL] = wbuf[low, :, W_BODY:W_IN_SHARD]
                tail_s[:, W_TAIL:2 * W_TAIL] = wbuf[low + 1, :, W_BODY:W_IN_SHARD]

        hn = hn_ref[...]
        z_ref[:, 0:W_BODY] = _dot(hn, wbuf[low, :, 0:W_BODY])
        z_ref[:, W_IN_SHARD:W_IN_SHARD + W_BODY] = _dot(hn, wbuf[low + 1, :, 0:W_BODY])
        tails = _dot(hn, tail_s[...])
        z_ref[:, W_BODY:W_IN_SHARD] = tails[:, 0:W_TAIL]
        z_ref[:, W_IN_SHARD + W_BODY:2 * W_IN_SHARD] = tails[:, W_TAIL:2 * W_TAIL]

        @pl.when(jnp.logical_and(j == len(CHIP_ORDER) - 1, i == n_i - 1))
        def _():
            for a in range(n_o):
                direct(SIBLING, a)[1].wait_recv()
            for k in ICI_MASKS:
                for a in range(n_o):
                    passed(k, a)[1].wait_recv()
            for k in W_DIRECT:
                direct(k)[0].wait_send()
            for core in (0, 1):
                @pl.when(c == core)
                def _(core=core):
                    relay(core)[0].wait_send()
            for k in DIRECT_MASKS:
                for a in range(n_o):
                    direct(k, a)[0].wait_send()
            for k in ICI_MASKS:
                passed(k)[0].wait_send()
                for a in range(n_o):
                    passed(k, a)[0].wait_send()
            for cp in own_copies():
                cp.wait()
            for jj in range(N_DEV):
                pltpu.make_async_copy(wbuf.at[0], wg_hbm.at[0], wb_sems.at[jj]).wait()

    dma = lambda n: pltpu.SemaphoreType.DMA((n,))
    grid_spec = pltpu.PrefetchScalarGridSpec(
        num_scalar_prefetch=1, grid=(len(CHIP_ORDER), n_i),
        in_specs=[pl.BlockSpec((tm, D_MODEL), lambda j, i, order: (i, 0)), ANY_SPEC] + [ANY_SPEC] * n_o,
        out_specs=[pl.BlockSpec((tm, 2 * W_IN_SHARD), lambda j, i, order: (i, order[j])), ANY_SPEC] + [ANY_SPEC] * n_o,
        scratch_shapes=[pltpu.VMEM((N_DEV, D_MODEL, W_IN_SHARD), BF16), pltpu.VMEM((D_MODEL, 2 * W_TAIL), BF16),
                        dma(N_DEV), dma(N_DEV), dma(N_DEV), dma(N_DEV),
                        dma(n_o * N_DEV), dma(n_o * N_DEV), dma(n_o * N_DEV), dma(n_o * N_DEV), dma(N_DEV), dma(1 + n_o),
                        dma(1), dma(1)])
    res = pl.pallas_call(
        body, name="in_proj", grid_spec=grid_spec,
        out_shape=[SDS((t_len, D_IN), F32), SDS((N_DEV, D_MODEL, W_IN_SHARD), BF16)]
        + [SDS((N_DEV,) + o.shape, o.dtype) for o in others],
        compiler_params=_params(("arbitrary", "arbitrary"), 54),
    )(order, hn, w_shard, *others)
    return res[0], res[1], res[2:]


def _conv_rows(cur, prev, cw_ref, cb, rid):
    acc = cw_ref[3:4, :] * cur + cb
    for k in range(1, CONV_W):
        acc = acc + cw_ref[3 - k:4 - k, :] * _shift_down(cur, prev, k, rid)
    return acc


def _lru_gates(pa, px, ba, bx, sp8, first_row):
    r = _sig(pa + ba)
    i = _sig(px + bx)
    la = -(r * sp8)
    a = jnp.exp(la)
    mult = jnp.where(first_row, 1.0, jnp.sqrt(_neg_expm1(2.0 * la, a * a)))
    return r, i, a, mult


def _mix_fwd(z, ln_g, ln_b, wm, bias, cw, cb, wax, ba, bx, lam, goa, gob, ex_arrs, ex_scatter):
    t_len = z.shape[0]
    n_chunk = t_len // CHUNK
    ex = _Exchange(ex_arrs, ex_scatter)
    n_in, n_out, n_scratch = 13, 5, 7

    def body(*refs):
        (z_ref, lng_ref, lnb_ref, wm_ref, bias_ref, cw_ref, cb_ref, wax_ref, ba_ref, bx_ref, lam_ref, goa_ref,
         gob_ref) = refs[:n_in]
        ex_in = refs[n_in:n_in + ex.n]
        y_ref, h_ref, vhb_ref, xcb_ref, rs_ref = refs[n_in + ex.n:n_in + ex.n + n_out]
        ex_out = refs[n_in + ex.n + n_out:n_in + 2 * ex.n + n_out]
        vn_s, xc_s, mixed_s, pre_s, y_s, carry_s, halo_s = refs[n_in + 2 * ex.n + n_out:n_in + 2 * ex.n + n_out + n_scratch]
        ex_sems = refs[n_in + 2 * ex.n + n_out + n_scratch:]
        c_id = pl.program_id(0)
        rid = _row_ids(D_BR)

        @pl.when(c_id == 0)
        def _():
            ex.start(ex_in, ex_out, ex_sems)
            carry_s[...] = jnp.zeros_like(carry_s)
            halo_s[...] = jnp.zeros_like(halo_s)

        lng, lnb, cb = lng_ref[...], lnb_ref[...], cb_ref[...]

        def phase1(g, prev):
            rows = _rows(g)
            vg, _ = _gelu(z_ref[rows, D_BR:2 * D_BR])
            xm = vg - _mean_last(vg)
            rs = lax.rsqrt(_mean_last(xm * xm) + EPS)
            vn_s[rows, :] = xm * rs
            rs_ref[rows, :] = jnp.broadcast_to(rs, (ROWS, HEAD))
            xb = z_ref[rows, 3 * D_BR:4 * D_BR]
            xc_s[rows, :] = _conv_rows(xb, prev, cw_ref, cb, rid)
            return xb

        halo_s[...] = _loop(N_GROUP, phase1, halo_s[...], unroll=8)
        vhb_ref[...] = vn_s[...].astype(BF16)
        xcb_ref[...] = xc_s[...].astype(BF16)

        for h in range(N_HEAD):
            cs = slice(h * HEAD, (h + 1) * HEAD)
            mixed_s[:, cs] = _dot(wm_ref[h], (vn_s[:, cs] * lng[:, cs] + lnb[:, cs]).astype(BF16))
            pre = _dot(xcb_ref[:, cs], wax_ref[h])
            pre_s[:, cs] = pre[:, :HEAD]
            pre_s[:, D_BR + h * HEAD:D_BR + (h + 1) * HEAD] = pre[:, HEAD:]

        ba, bx, goa, gob = ba_ref[...], bx_ref[...], goa_ref[...], gob_ref[...]
        sp8 = LRU_C * _softplus(-lam_ref[...])

        def phase3(g, carry):
            rows = _rows(g)
            ug, _ = _gelu(z_ref[rows, 0:D_BR])
            ga = z_ref[rows, 2 * D_BR:3 * D_BR]
            ya = ug * (mixed_s[rows, :] + bias_ref[rows, :]) * (ga * _sig(ga))
            y_s[rows, 0:D_BR] = ya * lax.rsqrt(_mean_last(ya * ya) + EPS) * goa

            first_row = jnp.logical_and(jnp.logical_and(c_id == 0, g == 0), rid == 0)
            _, i, a, mult = _lru_gates(pre_s[rows, 0:D_BR], pre_s[rows, D_BR:2 * D_BR], ba, bx, sp8, first_row)
            b = mult * i * xc_s[rows, :]
            for d in (1, 2, 4):
                a_sh = jnp.where(rid >= d, pltpu.roll(a, d, 0), 1.0)
                b_sh = jnp.where(rid >= d, pltpu.roll(b, d, 0), 0.0)
                b = a * b_sh + b
                a = a * a_sh
            hh = b + a * carry
            h_ref[rows, :] = hh
            gb = z_ref[rows, 4 * D_BR:5 * D_BR]
            yb = hh * (gb * _sig(gb))
            y_s[rows, D_BR:2 * D_BR] = yb * lax.rsqrt(_mean_last(yb * yb) + EPS) * gob
            return _bcast_row(hh, ROWS - 1)

        carry_s[...] = _loop(N_GROUP, phase3, carry_s[...])
        y_ref[...] = y_s[...].astype(BF16)

        @pl.when(c_id == n_chunk - 1)
        def _():
            ex.wait(ex_in, ex_out, ex_sems)

    vec = pl.BlockSpec((1, D_BR), lambda i: (0, 0))
    res = pl.pallas_call(
        body, name="mix_fwd", grid=(n_chunk,),
        in_specs=[pl.BlockSpec((CHUNK, D_IN), lambda i: (i, 0)), vec, vec,
                  pl.BlockSpec((N_HEAD, HEAD, HEAD), lambda i: (0, 0, 0)),
                  pl.BlockSpec((CHUNK, D_BR), lambda i: (0, 0)),
                  pl.BlockSpec((ROWS, D_BR), lambda i: (0, 0)), vec,
                  pl.BlockSpec((N_HEAD, HEAD, 2 * HEAD), lambda i: (0, 0, 0)), vec, vec, vec, vec, vec]
        + [ANY_SPEC] * ex.n,
        out_specs=[pl.BlockSpec((CHUNK, 2 * D_BR), lambda i: (i, 0)), pl.BlockSpec((CHUNK, D_BR), lambda i: (i, 0)),
                   pl.BlockSpec((CHUNK, D_BR), lambda i: (i, 0)), pl.BlockSpec((CHUNK, D_BR), lambda i: (i, 0)),
                   pl.BlockSpec((CHUNK, HEAD), lambda i: (i, 0))] + [ANY_SPEC] * ex.n,
        out_shape=[SDS((t_len, 2 * D_BR), BF16), SDS((t_len, D_BR), F32), SDS((t_len, D_BR), BF16),
                   SDS((t_len, D_BR), BF16), SDS((t_len, HEAD), F32)] + ex.out_shape,
        scratch_shapes=[pltpu.VMEM((CHUNK, D_BR), F32), pltpu.VMEM((CHUNK, D_BR), F32), pltpu.VMEM((CHUNK, D_BR), F32),
                        pltpu.VMEM((CHUNK, 2 * D_BR), F32), pltpu.VMEM((CHUNK, 2 * D_BR), F32),
                        pltpu.VMEM((ROWS, D_BR), F32), pltpu.VMEM((ROWS, D_BR), F32)] + ex.scratch,
        compiler_params=_params(("arbitrary",), 32),
    )(z, ln_g, ln_b, wm, bias, cw, cb, wax, ba, bx, lam, goa, gob, *ex_arrs)
    return res[:n_out], res[n_out:]


def _load_weight(w_hbm, w_vmem, sem):
    @pl.when(pl.program_id(0) == 0)
    def _():
        cp = pltpu.make_async_copy(w_hbm, w_vmem, sem)
        cp.start()
        cp.wait()


def _out_proj(y, x, w_out, post_g, tm=512):
    t_len = y.shape[0]

    def body(y_ref, x_ref, w_hbm, g_ref, h1_ref, ob_ref, w_s, o_s, sem):
        _load_weight(w_hbm, w_s, sem)
        o_s[...] = _dot(y_ref[...], w_s[...])
        g = g_ref[...]

        def rows_body(q, _):
            rows = _tile_rows(q)
            o = o_s[rows, :]
            h1_ref[rows, :] = x_ref[rows, :] + o * lax.rsqrt(_mean_last(o * o) + EPS) * g
            ob_ref[rows, :] = o.astype(BF16)
            return 0

        _loop(tm // TILE_ROWS, rows_body, 0, unroll=TILE_UNROLL)

    tile = pl.BlockSpec((tm, D_MODEL), lambda i: (i, 0))
    return pl.pallas_call(
        body, name="out_proj", grid=(t_len // tm,),
        in_specs=[tile, tile, pl.BlockSpec(memory_space=pl.ANY), pl.BlockSpec((1, D_MODEL), lambda i: (0, 0))],
        out_specs=[tile, tile],
        out_shape=[SDS((t_len, D_MODEL), F32), SDS((t_len, D_MODEL), BF16)],
        scratch_shapes=[pltpu.VMEM((D_MODEL, D_MODEL), BF16), pltpu.VMEM((tm, D_MODEL), F32), pltpu.SemaphoreType.DMA],
        compiler_params=_params(("arbitrary",), 44),
    )(y, x, w_out, post_g)


def _ple_loss(h1, p, tgt, w_pg, w_pe_g, tm=256):
    t_len = h1.shape[0]
    n_tile = t_len // tm
    pe_shard = D_MODEL // N_DEV

    def body(h1_ref, p_ref, t_ref, w_hbm, wpe_ref, dh2_ref, dgl_ref, h1b_ref, loss_ref, dwpe_ref, w_s, pe_s, gl_s, acc_s,
             dpe_s, gpe_s, sem):
        _load_weight(w_hbm, w_s, sem)
        i = pl.program_id(0)

        @pl.when(i == 0)
        def _():
            acc_s[...] = jnp.zeros_like(acc_s)
            gpe_s[...] = jnp.zeros_like(gpe_s)

        h1b_ref[...] = h1_ref[...].astype(BF16)
        pb = p_ref[...].astype(BF16)
        for j in range(N_DEV):
            pe_s[:, j * pe_shard:(j + 1) * pe_shard] = _dot(pb, wpe_ref[j])
        gl_s[...] = _dot(h1b_ref[...], w_s[...])

        def rows_body(q, acc):
            rows = _tile_rows(q)
            pe = pe_s[rows, :]
            g = _sig(gl_s[rows, :])
            e = h1_ref[rows, :] + pe * g - t_ref[rows, :]
            dh2 = e * (1.0 / D_MODEL)
            dh2_ref[rows, :] = dh2
            dpe_s[rows, :] = (dh2 * g).astype(BF16)
            dgl_ref[rows, :] = (dh2 * pe * g * (1.0 - g)).astype(BF16)
            return acc + _fold_rows(e * e)

        acc_s[...] = _loop(tm // TILE_ROWS, rows_body, acc_s[...], unroll=TILE_UNROLL)
        gpe_s[...] += _dot_tn(pb, dpe_s[...])

        @pl.when(i == n_tile - 1)
        def _():
            loss_ref[...] = jnp.full(loss_ref.shape, 0.5 / D_MODEL * jnp.sum(acc_s[...]), F32)
            for j in range(N_DEV):
                dwpe_ref[j] = gpe_s[:, j * pe_shard:(j + 1) * pe_shard].astype(BF16)

    tile = pl.BlockSpec((tm, D_MODEL), lambda i: (i, 0))
    pe_blocks = pl.BlockSpec((N_DEV, D_PLE, pe_shard), lambda i: (0, 0, 0))
    return pl.pallas_call(
        body, name="ple_loss", grid=(n_tile,),
        in_specs=[tile, pl.BlockSpec((tm, D_PLE), lambda i: (i, 0)), tile, pl.BlockSpec(memory_space=pl.ANY), pe_blocks],
        out_specs=[tile, tile, tile, pl.BlockSpec((ROWS, HEAD), lambda i: (0, 0)), pe_blocks],
        out_shape=[SDS((t_len, D_MODEL), F32), SDS((t_len, D_MODEL), BF16), SDS((t_len, D_MODEL), BF16),
                   SDS((ROWS, HEAD), F32), SDS((N_DEV, D_PLE, pe_shard), BF16)],
        scratch_shapes=[pltpu.VMEM((D_MODEL, D_MODEL), BF16), pltpu.VMEM((tm, D_MODEL), F32),
                        pltpu.VMEM((tm, D_MODEL), F32), pltpu.VMEM((ROWS, D_MODEL), F32), pltpu.VMEM((tm, D_MODEL), BF16),
                        pltpu.VMEM((D_PLE, D_MODEL), F32), pltpu.SemaphoreType.DMA],
        compiler_params=_params(("arbitrary",), 48),
    )(h1, p, tgt, w_pg, w_pe_g)


def _tail_bwd(dh2, dgl, ob, w_pg, w_out, post_g, tm=256):
    t_len = dh2.shape[0]
    n_tile = t_len // tm

    def body(dh2_ref, dgl_ref, ob_ref, wpg_hbm, wout_hbm, g_ref, dh1_ref, do_ref, dy_ref, dg_ref, wpg_s, wout_s, t_s,
             acc_s, sems):
        _load_weight(wpg_hbm, wpg_s, sems.at[0])
        _load_weight(wout_hbm, wout_s, sems.at[1])
        i = pl.program_id(0)

        @pl.when(i == 0)
        def _():
            acc_s[...] = jnp.zeros_like(acc_s)

        t_s[...] = _dot_nt(dgl_ref[...], wpg_s[...])
        g = g_ref[...]

        def rows_body(q, acc):
            rows = _tile_rows(q)
            dh1 = dh2_ref[rows, :] + t_s[rows, :]
            dh1_ref[rows, :] = dh1
            o = ob_ref[rows, :].astype(F32)
            rr = lax.rsqrt(_mean_last(o * o) + EPS)
            on = o * rr
            dog = dh1 * g
            do_ref[rows, :] = (rr * (dog - on * _mean_last(dog * on))).astype(BF16)
            return acc + _fold_rows(dh1 * on)

        acc_s[...] = _loop(tm // TILE_ROWS, rows_body, acc_s[...], unroll=TILE_UNROLL)
        dy_ref[...] = _dot_nt(do_ref[...], wout_s[...]).astype(BF16)

        @pl.when(i == n_tile - 1)
        def _():
            dg_ref[...] = jnp.sum(acc_s[...], axis=0, keepdims=True)

    tile = pl.BlockSpec((tm, D_MODEL), lambda i: (i, 0))
    vec = pl.BlockSpec((1, D_MODEL), lambda i: (0, 0))
    hbm = pl.BlockSpec(memory_space=pl.ANY)
    return pl.pallas_call(
        body, name="tail_bwd", grid=(n_tile,),
        in_specs=[tile, tile, tile, hbm, hbm, vec],
        out_specs=[tile, tile, tile, vec],
        out_shape=[SDS((t_len, D_MODEL), F32), SDS((t_len, D_MODEL), BF16), SDS((t_len, D_MODEL), BF16),
                   SDS((1, D_MODEL), F32)],
        scratch_shapes=[pltpu.VMEM((D_MODEL, D_MODEL), BF16), pltpu.VMEM((D_MODEL, D_MODEL), BF16),
                        pltpu.VMEM((tm, D_MODEL), F32), pltpu.VMEM((ROWS, D_MODEL), F32), pltpu.SemaphoreType.DMA((2,))],
        compiler_params=_params(("arbitrary",), 48),
    )(dh2, dgl, ob, w_pg, w_out, post_g)


def _mix_bwd(z, dy, h, vhb, xcb, rs, ln_g, ln_b, wm, wm_t, bias, cw, cb, wax, wax_t, ba, bx, lam, goa, gob, ex_arrs,
             ex_scatter):
    t_len = z.shape[0]
    n_chunk = t_len // CHUNK
    halo_blocks = CHUNK // ROWS
    ex = _Exchange(ex_arrs, ex_scatter)
    n_in, n_out, n_scratch = 21, 5, 16

    def body(*refs):
        (z_ref, dy_ref, h_ref, hhalo_ref, vhb_ref, xcb_ref, rs_ref, lng_ref, lnb_ref, wm_ref, wmt_ref, bias_ref, cw_ref,
         cb_ref, wax_ref, waxt_ref, ba_ref, bx_ref, lam_ref, goa_ref, gob_ref) = refs[:n_in]
        ex_in = refs[n_in:n_in + ex.n]
        dz_ref, vecs_ref, dws_ref, dwax_ref, dbs_ref = refs[n_in + ex.n:n_in + ex.n + n_out]
        ex_out = refs[n_in + ex.n + n_out:n_in + 2 * ex.n + n_out]
        (vnb_s, vh_s, xc_s, mixed_s, pre_s, dmix_s, dvn_s, dho_s, dxc_s, dpre_s, dz_s, acc_s, accdm_s,
         cg_s, ca_s, dxchalo_s) = refs[n_in + 2 * ex.n + n_out:n_in + 2 * ex.n + n_out + n_scratch]
        ex_sems = refs[n_in + 2 * ex.n + n_out + n_scratch:]
        step = pl.program_id(0)
        c_id = n_chunk - 1 - step
        rid = _row_ids(D_BR)
        first_chunk = c_id == 0

        @pl.when(step == 0)
        def _():
            ex.start(ex_in, ex_out, ex_sems)
            acc_s[...] = jnp.zeros_like(acc_s)
            accdm_s[...] = jnp.zeros_like(accdm_s)
            cg_s[...] = jnp.zeros_like(cg_s)
            ca_s[...] = jnp.zeros_like(ca_s)
            dxchalo_s[...] = jnp.zeros_like(dxchalo_s)
            dws_ref[...] = jnp.zeros_like(dws_ref)
            dwax_ref[...] = jnp.zeros_like(dwax_ref)

        lng, lnb = lng_ref[...], lnb_ref[...]
        h_halo = jnp.where(first_chunk, 0.0, hhalo_ref[...])

        def prev_rows(ref, cols, g, halo):
            before = ref[pl.ds(pl.multiple_of(jnp.maximum(g - 1, 0) * ROWS, ROWS), ROWS), cols]
            return jnp.where(g > 0, before, halo)

        vh_s[...] = vhb_ref[...].astype(F32)
        xc_s[...] = xcb_ref[...].astype(F32)

        for hd in range(N_HEAD):
            cs = slice(hd * HEAD, (hd + 1) * HEAD)
            vnb_s[:, cs] = (vh_s[:, cs] * lng[:, cs] + lnb[:, cs]).astype(BF16)
            mixed_s[:, cs] = _dot(wm_ref[hd], vnb_s[:, cs])
            pre = _dot(xcb_ref[:, cs], wax_ref[hd])
            pre_s[:, cs] = pre[:, :HEAD]
            pre_s[:, D_BR + hd * HEAD:D_BR + (hd + 1) * HEAD] = pre[:, HEAD:]

        goa, gob = goa_ref[...], gob_ref[...]

        def phase3(g, _):
            rows = _rows(g)
            u = z_ref[rows, 0:D_BR]
            ug, tu = _gelu(u)
            ga = z_ref[rows, 2 * D_BR:3 * D_BR]
            sga = _sig(ga)
            sa = ga * sga
            mixed = mixed_s[rows, :] + bias_ref[rows, :]
            ya0 = ug * mixed
            ya = ya0 * sa
            ra = lax.rsqrt(_mean_last(ya * ya) + EPS)
            dyan = dy_ref[rows, 0:D_BR].astype(F32)
            acc_s[V_GOUT_A] += dyan * ya * ra
            dyg = dyan * goa
            dya = ra * dyg - ya * (ra * ra * ra) * _mean_last(dyg * ya)
            dya0 = dya * sa
            dz_s[rows, 2 * D_BR:3 * D_BR] = dya * ya0 * (sga * (1.0 + ga * (1.0 - sga)))
            dmix = dya0 * ug
            dmix_s[rows, :] = dmix
            accdm_s[rows, :] += dmix
            dz_s[rows, 0:D_BR] = dya0 * mixed * _gelu_grad(u, tu)

            hh = h_ref[rows, :]
            gb = z_ref[rows, 4 * D_BR:5 * D_BR]
            sgb = _sig(gb)
            sb = gb * sgb
            yb = hh * sb
            rb = lax.rsqrt(_mean_last(yb * yb) + EPS)
            dybn = dy_ref[rows, D_BR:2 * D_BR].astype(F32)
            acc_s[V_GOUT_B] += dybn * yb * rb
            dyg = dybn * gob
            dyb = rb * dyg - yb * (rb * rb * rb) * _mean_last(dyg * yb)
            dho_s[rows, :] = dyb * sb
            dz_s[rows, 4 * D_BR:5 * D_BR] = dyb * hh * (sgb * (1.0 + gb * (1.0 - sgb)))
            return 0

        _loop(N_GROUP, phase3, 0)

        for hd in range(N_HEAD):
            cs = slice(hd * HEAD, (hd + 1) * HEAD)
            dmb = dmix_s[:, cs].astype(BF16)
            dvn_s[:, cs] = _dot(wmt_ref[hd], dmb)
            dws_ref[hd] += _dot_nt(dmb, vnb_s[:, cs])

        def phase5(g, _):
            rows = _rows(g)
            dvn = dvn_s[rows, :]
            vh = vh_s[rows, :]
            acc_s[V_LN_G] += dvn * vh
            acc_s[V_LN_B] += dvn
            dvh = dvn * lng
            rs = rs_ref[rows, 0:1]
            dvg = rs * (dvh - _mean_last(dvh) - vh * _mean_last(dvh * vh))
            v = z_ref[rows, D_BR:2 * D_BR]
            _, tv = _gelu(v)
            dz_s[rows, D_BR:2 * D_BR] = dvg * _gelu_grad(v, tv)
            return 0

        _loop(N_GROUP, phase5, 0)

        ba, bx = ba_ref[...], bx_ref[...]
        sp8 = LRU_C * _softplus(-lam_ref[...])

        def phase6(k, carry):
            cg, ca = carry
            g = N_GROUP - 1 - k
            rows = _rows(g)
            first_row = jnp.logical_and(jnp.logical_and(first_chunk, g == 0), rid == 0)
            r, i, a, mult = _lru_gates(pre_s[rows, 0:D_BR], pre_s[rows, D_BR:2 * D_BR], ba, bx, sp8, first_row)
            a_nx = jnp.where(rid < ROWS - 1, pltpu.roll(a, ROWS - 1, 0), ca)
            aa, bb = a_nx, dho_s[rows, :]
            for d in (1, 2, 4):
                a_sh = jnp.where(rid < ROWS - d, pltpu.roll(aa, ROWS - d, 0), 1.0)
                b_sh = jnp.where(rid < ROWS - d, pltpu.roll(bb, ROWS - d, 0), 0.0)
                bb = aa * b_sh + bb
                aa = aa * a_sh
            gg = bb + aa * cg
            hh = h_ref[rows, :]
            hprev = _shift_down(hh, prev_rows(h_ref, slice(None), g, h_halo), 1, rid)
            xc = xc_s[rows, :]
            gx = gg * xc
            dla = gg * hprev * a - jnp.where(first_row, 0.0, gx * i * (a * a) * lax.rsqrt(mult * mult))
            acc_s[V_LAM] += -(dla * r)
            dpa = -(dla * sp8) * r * (1.0 - r)
            dpx = gx * mult * i * (1.0 - i)
            acc_s[V_B_A] += dpa
            acc_s[V_B_X] += dpx
            dpre_s[rows, 0:D_BR] = dpa
            dpre_s[rows, D_BR:2 * D_BR] = dpx
            dxc_s[rows, :] = gg * mult * i
            return _bcast_row(gg, 0), _bcast_row(a, 0)

        cg, ca = _loop(N_GROUP, phase6, (cg_s[...], ca_s[...]))
        cg_s[...] = cg
        ca_s[...] = ca

        for hd in range(N_HEAD):
            cs = slice(hd * HEAD, (hd + 1) * HEAD)
            dpre = jnp.concatenate([dpre_s[:, cs], dpre_s[:, D_BR + hd * HEAD:D_BR + (hd + 1) * HEAD]], axis=1).astype(BF16)
            dxc_s[:, cs] += _dot(dpre, waxt_ref[hd])
            dwax_ref[hd] += _dot_tn(xcb_ref[:, cs], dpre)

        def phase8(k, nxt):
            g = N_GROUP - 1 - k
            rows = _rows(g)
            dxc = dxc_s[rows, :]
            acc_s[V_CONV_B] += dxc
            xb = z_ref[rows, 3 * D_BR:4 * D_BR]
            dxb = cw_ref[3:4, :] * dxc
            acc_s[V_CONV_W + 3] += dxc * xb
            for j in range(1, CONV_W):
                later = _shift_up(dxc, nxt, j, rid)
                dxb = dxb + cw_ref[3 - j:4 - j, :] * later
                acc_s[V_CONV_W + 3 - j] += later * xb
            dz_s[rows, 3 * D_BR:4 * D_BR] = dxb
            return dxc

        dxchalo_s[...] = _loop(N_GROUP, phase8, dxchalo_s[...])
        dz_ref[...] = dz_s[...].astype(BF16)

        @pl.when(step == n_chunk - 1)
        def _():
            for v in range(N_VEC):
                vecs_ref[v:v + 1, :] = jnp.sum(acc_s[v], axis=0, keepdims=True)
            lam = lam_ref[...]
            vecs_ref[V_LAM:V_LAM + 1, :] = vecs_ref[V_LAM:V_LAM + 1, :] * (-LRU_C * _sig(-lam))
            tril = (lax.broadcasted_iota(jnp.int32, (HEAD, HEAD), 0) >= lax.broadcasted_iota(jnp.int32, (HEAD, HEAD), 1))
            ones = jnp.ones((ROWS, HEAD), BF16)
            for hd in range(N_HEAD):
                cs = slice(hd * HEAD, (hd + 1) * HEAD)
                dws_ref[hd] = jnp.where(tril, dws_ref[hd], 0.0)
                blk = accdm_s[:, cs]
                hi = blk.astype(BF16)
                lo = (blk - hi.astype(F32)).astype(BF16)
                dbs_ref[hd:hd + 1, :] = (_dot_nt(ones, hi) + _dot_nt(ones, lo))[0:1, :]
            ex.wait(ex_in, ex_out, ex_sems)

    vec = pl.BlockSpec((1, D_BR), lambda i: (0, 0))
    rev = lambda i: (n_chunk - 1 - i, 0)
    halo = lambda col: (lambda i: (jnp.maximum((n_chunk - 1 - i) * halo_blocks - 1, 0), col))
    full3 = lambda a, b, c: pl.BlockSpec((a, b, c), lambda i: (0, 0, 0))
    big = lambda w: pltpu.VMEM((CHUNK, w), F32)
    res = pl.pallas_call(
        body, name="mix_bwd", grid=(n_chunk,),
        in_specs=[pl.BlockSpec((CHUNK, D_IN), rev), pl.BlockSpec((CHUNK, 2 * D_BR), rev), pl.BlockSpec((CHUNK, D_BR), rev),
                  pl.BlockSpec((ROWS, D_BR), halo(0)), pl.BlockSpec((CHUNK, D_BR), rev), pl.BlockSpec((CHUNK, D_BR), rev),
                  pl.BlockSpec((CHUNK, HEAD), rev), vec, vec,
                  full3(N_HEAD, HEAD, HEAD), full3(N_HEAD, HEAD, HEAD),
                  pl.BlockSpec((CHUNK, D_BR), lambda i: (0, 0)), pl.BlockSpec((ROWS, D_BR), lambda i: (0, 0)), vec,
                  full3(N_HEAD, HEAD, 2 * HEAD), full3(N_HEAD, 2 * HEAD, HEAD), vec, vec, vec, vec, vec]
        + [ANY_SPEC] * ex.n,
        out_specs=[pl.BlockSpec((CHUNK, D_IN), rev), pl.BlockSpec((N_VEC, D_BR), lambda i: (0, 0)),
                   full3(N_HEAD, HEAD, HEAD), full3(N_HEAD, HEAD, 2 * HEAD),
                   pl.BlockSpec((N_HEAD, HEAD), lambda i: (0, 0))] + [ANY_SPEC] * ex.n,
        out_shape=[SDS((t_len, D_IN), BF16), SDS((N_VEC, D_BR), F32), SDS((N_HEAD, HEAD, HEAD), F32),
                   SDS((N_HEAD, HEAD, 2 * HEAD), F32), SDS((N_HEAD, HEAD), F32)] + ex.out_shape,
        scratch_shapes=[pltpu.VMEM((CHUNK, D_BR), BF16), big(D_BR), big(D_BR), big(D_BR), big(2 * D_BR), big(D_BR),
                        big(D_BR), big(D_BR), big(D_BR), big(2 * D_BR), big(D_IN),
                        pltpu.VMEM((N_VEC, ROWS, D_BR), F32), big(D_BR),
                        pltpu.VMEM((ROWS, D_BR), F32), pltpu.VMEM((ROWS, D_BR), F32), pltpu.VMEM((ROWS, D_BR), F32)]
        + ex.scratch,
        compiler_params=_params(("arbitrary",), 48),
    )(z, dy, h, h, vhb, xcb, rs, ln_g, ln_b, wm, wm_t, bias, cw, cb, wax, wax_t, ba, bx, lam, goa, gob, *ex_arrs)
    return res[:n_out], res[n_out:]


def _in_bwd(dz, w_in_g, x, dh1, pre_g, first_tile, n_tile, prev, name, ex_arrs=(), ex_scatter=(), tm=256):
    t_len = x.shape[0]
    ex = _Exchange(ex_arrs, ex_scatter)
    n_prev = 0 if prev is None else 2

    def body(dz_ref, w_hbm, x_ref, dh1_ref, g_ref, *refs):
        prev_refs, refs = refs[:n_prev], refs[n_prev:]
        ex_in, (gx_ref, dg_ref), ex_out = refs[:ex.n], refs[ex.n:ex.n + 2], refs[ex.n + 2:2 * ex.n + 2]
        w_s, t_s, dg_s, w_sems = refs[2 * ex.n + 2:2 * ex.n + 6]
        ex_sems = refs[2 * ex.n + 6:]
        i = pl.program_id(0)

        @pl.when(i == 0)
        def _():
            if ex.n:
                ex.start(ex_in, ex_out, ex_sems)
            loads = [pltpu.make_async_copy(w_hbm.at[s], w_s.at[:, s * W_IN_SHARD:(s + 1) * W_IN_SHARD], w_sems.at[s])
                     for s in range(N_DEV)]
            for cp in loads:
                cp.start()
            dg_s[...] = jnp.zeros_like(dg_s)
            for cp in loads:
                cp.wait()

        t_s[...] = _dot_nt(dz_ref[...], w_s[...])
        g = g_ref[...]

        def rows_body(q, acc):
            rows = _tile_rows(q)
            xv = x_ref[rows, :]
            r = lax.rsqrt(_mean_last(xv * xv) + EPS)
            xh = xv * r
            dhn = t_s[rows, :]
            dg = dhn * g
            gx_ref[rows, :] = dh1_ref[rows, :] + r * (dg - xh * _mean_last(dg * xh))
            return acc + _fold_rows(dhn * xh)

        dg_s[...] = _loop(tm // TILE_ROWS, rows_body, dg_s[...], unroll=TILE_UNROLL)

        @pl.when(i == n_tile - 1)
        def _():
            dg = jnp.sum(dg_s[...], axis=0, keepdims=True)
            dg_ref[...] = dg + prev_refs[1][...] if n_prev else dg
            if ex.n:
                ex.wait(ex_in, ex_out, ex_sems)

    tile = pl.BlockSpec((tm, D_MODEL), lambda i: (first_tile + i, 0))
    vec = pl.BlockSpec((1, D_MODEL), lambda i: (0, 0))
    prev_specs = [ANY_SPEC, vec] if n_prev else []
    res = pl.pallas_call(
        body, name=name, grid=(n_tile,),
        in_specs=[pl.BlockSpec((tm, D_IN), lambda i: (first_tile + i, 0)), ANY_SPEC, tile, tile, vec] + prev_specs
        + [ANY_SPEC] * ex.n,
        out_specs=[tile, vec] + [ANY_SPEC] * ex.n,
        out_shape=[SDS((t_len, D_MODEL), F32), SDS((1, D_MODEL), F32)] + ex.out_shape,
        scratch_shapes=[pltpu.VMEM((D_MODEL, D_IN), BF16), pltpu.VMEM((tm, D_MODEL), F32), pltpu.VMEM((ROWS, D_MODEL), F32),
                        pltpu.SemaphoreType.DMA((N_DEV,))] + (ex.scratch if ex.n else []),
        input_output_aliases={5: 0} if n_prev else {},
        compiler_params=_params(("arbitrary",), 54),
    )(dz, w_in_g, x, dh1, pre_g, *(prev or ()), *ex_arrs)
    return res[0], res[1], res[2:]


def _grad_w(a, b, bn, shard_major, name, tk=1024, ex_arrs=(), ex_scatter=()):
    t_len, m = a.shape
    n = b.shape[1]
    n_j, n_k = n // bn, t_len // tk
    ex = _Exchange(ex_arrs, ex_scatter)

    def body(a_ref, b_ref, *refs):
        ex_in, o_ref, ex_out = refs[:ex.n], refs[ex.n], refs[ex.n + 1:2 * ex.n + 1]
        acc_s, ex_sems = refs[2 * ex.n + 1], refs[2 * ex.n + 2:]
        j, k = pl.program_id(0), pl.program_id(1)
        if ex.n:
            @pl.when(jnp.logical_and(j == 0, k == 0))
            def _():
                ex.start(ex_in, ex_out, ex_sems)

        @pl.when(k == 0)
        def _():
            acc_s[...] = jnp.zeros_like(acc_s)

        acc_s[...] += _dot_tn(a_ref[...], b_ref[...])

        @pl.when(k == n_k - 1)
        def _():
            o_ref[...] = acc_s[...].astype(BF16)

        if ex.n:
            @pl.when(jnp.logical_and(j == n_j - 1, k == n_k - 1))
            def _():
                ex.wait(ex_in, ex_out, ex_sems)

    if shard_major:
        out_spec, out_shape = pl.BlockSpec((None, m, bn), lambda j, k: (j, 0, 0)), SDS((n_j, m, bn), BF16)
    else:
        out_spec, out_shape = pl.BlockSpec((m, bn), lambda j, k: (0, j)), SDS((m, n), BF16)
    res = pl.pallas_call(
        body, name=name, grid=(n_j, n_k),
        in_specs=[pl.BlockSpec((tk, m), lambda j, k: (k, 0)), pl.BlockSpec((tk, bn), lambda j, k: (k, j))]
        + [ANY_SPEC] * ex.n,
        out_specs=[out_spec] + [ANY_SPEC] * ex.n, out_shape=[out_shape] + ex.out_shape,
        scratch_shapes=[pltpu.VMEM((m, bn), F32)] + (ex.scratch if ex.n else []),
        compiler_params=_params(("arbitrary", "arbitrary"), 40),
    )(a, b, *ex_arrs)
    return res[0], res[1:]


RS_ORDER = (3, 2, 5, 4, 7, 6, 1, 0)
RS_SLOTS = (0, 1, 2, 4, 6)


def _grad_w_in(hn, dz, ex_arrs, ex_scatter, tk=1024):
    t_len = hn.shape[0]
    n_k = t_len // tk
    ex = _Exchange(ex_arrs, ex_scatter)
    me_out = 4 * lax.axis_index("x") + 2 * lax.axis_index("y") + lax.axis_index("c")
    order = jnp.stack([me_out ^ k for k in RS_ORDER]).astype(jnp.int32)
    slots = jnp.stack([me_out ^ k for k in RS_SLOTS]).astype(jnp.int32)
    n_stage = 2

    def body(order_ref, a_ref, b_ref, *refs):
        ex_in, parts_hbm, ex_out = refs[:ex.n], refs[ex.n], refs[ex.n + 1:2 * ex.n + 1]
        acc_s, stage_s, rx_s, send_sems, recv_sems, loc_sem = refs[2 * ex.n + 1:2 * ex.n + 7]
        ex_sems = refs[2 * ex.n + 7:]
        j, k = pl.program_id(0), pl.program_id(1)
        x, y, c, me = _mesh_place()
        sib = _peer(x, y, c, SIBLING)[0]

        def send(jj):
            mask, src = RS_ORDER[jj], stage_s.at[jj % n_stage]
            if mask == 0:
                return pltpu.make_async_copy(src, parts_hbm.at[me], loc_sem.at[0])
            pair = (send_sems.at[mask], recv_sems.at[mask])
            if mask in ICI_MASKS or mask == SIBLING:
                return _remote(src, parts_hbm.at[me], *pair, _peer(x, y, c, mask)[0])
            return _remote(src, rx_s.at[mask // 2 - 1], *pair, sib)

        def from_sibling(mask):
            return _remote(stage_s.at[0], rx_s.at[mask // 2 - 1], send_sems.at[mask], recv_sems.at[mask], sib)

        @pl.when(jnp.logical_and(j == 0, k == 0))
        def _():
            ex.start(ex_in, ex_out, ex_sems)

        @pl.when(k == 0)
        def _():
            acc_s[...] = jnp.zeros_like(acc_s)

        acc_s[...] += _dot_tn(a_ref[...], b_ref[...])

        for jj in range(N_DEV):
            @pl.when(jnp.logical_and(j == jj, k == n_k - 1))
            def _(jj=jj):
                mask = RS_ORDER[jj]
                if jj >= n_stage:
                    send(jj - n_stage).wait_send()
                if mask in ICI_MASKS:
                    from_sibling(mask + 1).wait_recv()
                    stage_s[jj % n_stage] = (acc_s[...] + rx_s[mask // 2 - 1].astype(F32)).astype(BF16)
                else:
                    stage_s[jj % n_stage] = acc_s[...].astype(BF16)
                send(jj).start()

        @pl.when(jnp.logical_and(j == N_DEV - 1, k == n_k - 1))
        def _():
            for jj in range(N_DEV - n_stage, N_DEV):
                cp = send(jj)
                cp.wait() if RS_ORDER[jj] == 0 else cp.wait_send()
            for mask in DIRECT_MASKS:
                dev, lin = _peer(x, y, c, mask)
                _remote(stage_s.at[0], parts_hbm.at[lin], send_sems.at[mask], recv_sems.at[mask], dev).wait_recv()
            ex.wait(ex_in, ex_out, ex_sems)

    dma = lambda n: pltpu.SemaphoreType.DMA((n,))
    grid_spec = pltpu.PrefetchScalarGridSpec(
        num_scalar_prefetch=1, grid=(N_DEV, n_k),
        in_specs=[pl.BlockSpec((tk, D_MODEL), lambda j, k, order: (k, 0)),
                  pl.BlockSpec((tk, W_IN_SHARD), lambda j, k, order: (k, order[j]))] + [ANY_SPEC] * ex.n,
        out_specs=[ANY_SPEC] * (1 + ex.n),
        scratch_shapes=[pltpu.VMEM((D_MODEL, W_IN_SHARD), F32), pltpu.VMEM((n_stage, D_MODEL, W_IN_SHARD), BF16),
                        pltpu.VMEM((len(ICI_MASKS), D_MODEL, W_IN_SHARD), BF16), dma(N_DEV), dma(N_DEV), dma(1)]
        + ex.scratch)
    res = pl.pallas_call(
        body, name="grad_w_in", grid_spec=grid_spec,
        out_shape=[SDS((N_DEV, D_MODEL, W_IN_SHARD), BF16)] + ex.out_shape,
        compiler_params=_params(("arbitrary", "arbitrary"), 44),
    )(order, hn, dz, *ex_arrs)
    return res[0], slots, res[1:]


def _sum_parts(parts, name):
    def body(p_ref, o_ref):
        g = p_ref[0].astype(F32)
        for s in range(1, parts.shape[0]):
            g = g + p_ref[s].astype(F32)
        o_ref[...] = g

    return pl.pallas_call(body, name=name, out_shape=SDS(parts.shape[1:], F32))(parts)


def _adamw_math(g, w_ref, m_ref, v_ref, g_ref, d_ref, nm_ref, nv_ref):
    c1 = 1.0 - ADAM_B1 ** ADAM_STEP
    c2 = 1.0 - ADAM_B2 ** ADAM_STEP
    g_ref[...] = g
    nm = ADAM_B1 * m_ref[...] + (1.0 - ADAM_B1) * g
    nv = ADAM_B2 * v_ref[...] + (1.0 - ADAM_B2) * (g * g)
    nm_ref[...] = nm
    nv_ref[...] = nv
    d_ref[...] = -ADAM_LR * ((nm / c1) / (jnp.sqrt(nv / c2) + ADAM_EPS) + ADAM_WD * w_ref[...])


def _adamw(parts, w, m, v, name, tr):
    rows, cols = w.shape
    n_parts = parts.shape[0]

    def body(p_ref, *refs):
        g = p_ref[0].astype(F32)
        for s in range(1, n_parts):
            g = g + p_ref[s].astype(F32)
        _adamw_math(g, *refs)

    tile = pl.BlockSpec((tr, cols), lambda i: (i, 0))
    return pl.pallas_call(
        body, name=name, grid=(rows // tr,),
        in_specs=[pl.BlockSpec((n_parts, tr, cols), lambda i: (0, i, 0)), tile, tile, tile],
        out_specs=[tile] * 4, out_shape=[SDS((rows, cols), F32)] * 4,
        compiler_params=_params(("arbitrary",), 40),
    )(parts, w, m, v)


def _adamw_slots(parts, slots, w, m, v, name, tr):
    rows, cols = w.shape
    n_slots = slots.shape[0]

    def body(slots_ref, *refs):
        g = refs[0][...].astype(F32)
        for s in range(1, n_slots):
            g = g + refs[s][...].astype(F32)
        _adamw_math(g, *refs[n_slots:])

    tile = pl.BlockSpec((tr, cols), lambda i, slots: (i, 0))
    part = lambda s: pl.BlockSpec((None, tr, cols), lambda i, slots: (slots[s], i, 0))
    grid_spec = pltpu.PrefetchScalarGridSpec(
        num_scalar_prefetch=1, grid=(rows // tr,),
        in_specs=[part(s) for s in range(n_slots)] + [tile, tile, tile], out_specs=[tile] * 4)
    return pl.pallas_call(
        body, name=name, grid_spec=grid_spec, out_shape=[SDS((rows, cols), F32)] * 4,
        compiler_params=_params(("arbitrary",), 40),
    )(slots, *([parts] * n_slots), w, m, v)


PACKED = ("gmlp_ln_g", "gmlp_ln_b", "gmlp_ws", "gmlp_bs", "conv_b", "w_a", "b_a", "w_x", "b_x", "lam", "gmlp_out_g",
          "lru_out_g", "post_g")
WEIGHTS = ("pre_g", "w_in", "gmlp_ln_g", "gmlp_ln_b", "gmlp_ws", "gmlp_bs", "conv_w", "conv_b", "w_a", "b_a", "w_x",
           "b_x", "lam", "gmlp_out_g", "lru_out_g", "w_out", "post_g", "w_pe", "w_pg")
LANES = 128


PACK_ROWS = 3200
PACK_TILE = 640
IN_BWD_TILE = 256


def _pack(parts):
    rows = [p.reshape(-1, LANES) for p in parts]
    used = sum(r.shape[0] for r in rows)
    return jnp.concatenate(rows + [jnp.zeros((PACK_ROWS - used, LANES), F32)], axis=0)


def _pad_rows(a, rows):
    return jnp.concatenate([a, jnp.zeros((rows - a.shape[0],) + a.shape[1:], a.dtype)], axis=0)


def kernel(x, p, pre_g, w_in, gmlp_ln_g, gmlp_ln_b, gmlp_ws, gmlp_bs, conv_w, conv_b, w_a, b_a, w_x, b_x, lam, gmlp_out_g, lru_out_g, w_out, post_g, w_pe, w_pg, loss_target, m_pre_g, m_w_in, m_gmlp_ln_g, m_gmlp_ln_b, m_gmlp_ws, m_gmlp_bs, m_conv_w, m_conv_b, m_w_a, m_b_a, m_w_x, m_b_x, m_lam, m_gmlp_out_g, m_lru_out_g, m_w_out, m_post_g, m_w_pe, m_w_pg, v_pre_g, v_w_in, v_gmlp_ln_g, v_gmlp_ln_b, v_gmlp_ws, v_gmlp_bs, v_conv_w, v_conv_b, v_w_a, v_b_a, v_w_x, v_b_x, v_lam, v_gmlp_out_g, v_lru_out_g, v_w_out, v_post_g, v_w_pe, v_w_pg):
    args = dict(locals())
    weights = {n: args[n] for n in WEIGHTS}
    m_in = {n: args["m_" + n] for n in WEIGHTS}
    v_in = {n: args["v_" + n] for n in WEIGHTS}
    sm = {n: weights[n][0] for n in PACKED}
    shard_rows = D_MODEL // N_DEV
    xs, ps, tgt = x[0], p[0, 0], loss_target[0]

    vec = lambda a: a.reshape(1, -1)
    tril = jnp.tril(jnp.ones((CHUNK, CHUNK), dtype=bool))
    wm32 = jnp.where(tril[None], sm["gmlp_ws"], 0.0)
    wm, wm_t = wm32.astype(BF16), jnp.swapaxes(wm32, 1, 2).astype(BF16)
    bias = jnp.repeat(sm["gmlp_bs"].T, HEAD, axis=1)
    wax32 = jnp.concatenate([sm["w_a"], sm["w_x"]], axis=2)
    wax, wax_t = wax32.astype(BF16), jnp.swapaxes(wax32, 1, 2).astype(BF16)
    ln_g, ln_b = vec(sm["gmlp_ln_g"]), vec(sm["gmlp_ln_b"])
    post_g_v = vec(sm["post_g"])

    hn = _pre_norm(xs, pre_g)
    cw_shard = _pad_rows(conv_w.reshape(CONV_W, HEAD), ROWS)
    z, w_in_g, (cw_g,) = _in_proj(hn, w_in[0].astype(BF16), [cw_shard])
    cw_full = jnp.transpose(cw_g[:, :CONV_W, :], (1, 0, 2)).reshape(CONV_W, D_BR)
    mixer_consts = dict(cw=_pad_rows(cw_full, ROWS), cb=vec(sm["conv_b"]), ba=vec(sm["b_a"]), bx=vec(sm["b_x"]),
                        lam=vec(sm["lam"]), goa=vec(sm["gmlp_out_g"]), gob=vec(sm["lru_out_g"]))
    (y, h, vhb, xcb, v_rs), (w_out_g, w_pe_g, w_pg_g) = _mix_fwd(
        z, ln_g, ln_b, wm, bias, wax=wax, **mixer_consts,
        ex_arrs=[w_out[0].astype(BF16), w_pe[0].astype(BF16), w_pg[0].astype(BF16)], ex_scatter=[False, False, False])
    w_out_f, w_pg_f = w_out_g.reshape(D_MODEL, D_MODEL), w_pg_g.reshape(D_MODEL, D_MODEL)
    h1, ob = _out_proj(y, xs, w_out_f, post_g_v)
    dh2, dgl, h1b, loss_part, d_w_pe = _ple_loss(h1, ps, tgt, w_pg_f, w_pe_g)

    dh1, do, dy, d_post_g = _tail_bwd(dh2, dgl, ob, w_pg_f, w_out_f, post_g_v)
    d_w_out, _ = _grad_w(y, do, 1024, False, "grad_w_out")
    d_w_pg, _ = _grad_w(h1b, dgl, 1024, False, "grad_w_pg")
    (dz, vecs, d_ws, d_wax, d_bs), (parts_out, parts_pg, parts_pe) = _mix_bwd(
        z, dy, h, vhb, xcb, v_rs, ln_g, ln_b, wm, wm_t, bias, wax=wax, wax_t=wax_t, **mixer_consts,
        ex_arrs=[d_w_out.reshape(N_DEV, shard_rows, D_MODEL), d_w_pg.reshape(N_DEV, shard_rows, D_MODEL), d_w_pe],
        ex_scatter=[True, True, True])

    small = {"gmlp_ln_g": vecs[V_LN_G], "gmlp_ln_b": vecs[V_LN_B], "gmlp_ws": d_ws, "gmlp_bs": d_bs,
             "conv_b": vecs[V_CONV_B], "w_a": d_wax[:, :, :HEAD], "b_a": vecs[V_B_A], "w_x": d_wax[:, :, HEAD:],
             "b_x": vecs[V_B_X], "lam": vecs[V_LAM], "gmlp_out_g": vecs[V_GOUT_A], "lru_out_g": vecs[V_GOUT_B],
             "post_g": d_post_g}
    small_part = _pack([small[n] for n in PACKED] + [loss_part]).reshape(N_DEV, PACK_ROWS // N_DEV, LANES)
    d_cw_blocks = jnp.transpose(vecs[V_CONV_W:V_CONV_W + CONV_W].reshape(CONV_W, N_DEV, HEAD), (1, 0, 2))
    d_cw_blocks = jnp.concatenate([d_cw_blocks, jnp.zeros((N_DEV, ROWS - CONV_W, HEAD), F32)], axis=1)
    parts_in, slots_in, (small_blocks, parts_cw) = _grad_w_in(
        hn, dz, ex_arrs=[small_part, d_cw_blocks], ex_scatter=[True, True])
    small_sum = _sum_parts(small_blocks, "sum_small")
    grad_x, d_pre_g, _ = _in_bwd(dz, w_in_g, xs, dh1, pre_g, 0, xs.shape[0] // IN_BWD_TILE, None, "in_bwd",
                                 tm=IN_BWD_TILE)
    pre_rows = D_MODEL // LANES
    small_all, parts_pre = _exchange([small_sum, d_pre_g.reshape(pre_rows, LANES)], False, "gather_small_grads")
    parts_small = small_all.reshape(1, PACK_ROWS, LANES)

    pad_cw = lambda a: _pad_rows(a.reshape(CONV_W, HEAD), ROWS)
    flat = lambda a: a.reshape(pre_rows, LANES)
    outs = {
        "w_in": _adamw_slots(parts_in, slots_in, w_in[0], m_w_in[0], v_w_in[0], "adamw_w_in", 256),
        "w_out": _adamw(parts_out, w_out[0], m_w_out[0], v_w_out[0], "adamw_w_out", 128),
        "w_pe": _adamw(parts_pe, w_pe[0], m_w_pe[0], v_w_pe[0], "adamw_w_pe", 256),
        "w_pg": _adamw(parts_pg, w_pg[0], m_w_pg[0], v_w_pg[0], "adamw_w_pg", 128),
        "conv_w": [a[:CONV_W] for a in
                   _adamw(parts_cw, pad_cw(conv_w), pad_cw(m_conv_w), pad_cw(v_conv_w), "adamw_conv_w", ROWS)],
        "pre_g": _adamw(parts_pre, flat(pre_g), flat(m_pre_g), flat(v_pre_g), "adamw_pre_g", pre_rows),
    }
    packed = _adamw(parts_small, _pack([weights[n] for n in PACKED]), _pack([m_in[n] for n in PACKED]),
                    _pack([v_in[n] for n in PACKED]), "adamw_small", PACK_TILE)
    row = 0
    for n in PACKED:
        n_rows = weights[n].size // LANES
        outs[n] = [packed[q][row:row + n_rows] for q in range(4)]
        row += n_rows
    loss = packed[0][row, 0]

    result = [loss, grad_x[None]]
    for q in range(4):
        result += [outs[n][q].reshape(weights[n].shape) for n in WEIGHTS]
    return tuple(result)
```

```python
import functools

import jax
import jax.numpy as jnp
from jax import lax
from jax.experimental import pallas as pl
from jax.experimental.pallas import tpu as pltpu

F32 = jnp.float32
BF16 = jnp.bfloat16
SDS = jax.ShapeDtypeStruct

D_MODEL = 2048
D_BR = 1024
D_IN = 5 * D_BR
D_PLE = 256
N_HEAD = 8
HEAD = 128
CHUNK = 128
ROWS = 8
N_GROUP = CHUNK // ROWS
N_DEV = 8
W_IN_SHARD = D_IN // N_DEV
EPS = 1e-6
LRU_C = 8.0
CONV_W = 4
MESH_AXES = ("x", "y", "c")
MIB = 1 << 20

ADAM_LR, ADAM_B1, ADAM_B2, ADAM_EPS, ADAM_WD, ADAM_STEP = 0.001, 0.9, 0.999, 1e-08, 0.01, 10

_GELU_C = 0.7978845608028654
_GELU_A = 0.044715

V_LN_G, V_LN_B, V_CONV_B, V_B_A, V_B_X, V_LAM, V_GOUT_A, V_GOUT_B, V_CONV_W = 0, 1, 2, 3, 4, 5, 6, 7, 8
N_VEC = 16


def _params(sem, vmem_mib):
    return pltpu.CompilerParams(dimension_semantics=sem, vmem_limit_bytes=int(vmem_mib * MIB))


def _sig(x):
    return 0.5 * jnp.tanh(0.5 * x) + 0.5


def _gelu(x):
    t = jnp.tanh(_GELU_C * (x + _GELU_A * x * x * x))
    return 0.5 * x * (1.0 + t), t


def _gelu_grad(x, t):
    return 0.5 * (1.0 + t) + 0.5 * x * (1.0 - t * t) * (_GELU_C * (1.0 + 3.0 * _GELU_A * x * x))


def _neg_expm1(y, exp_y):
    series = -y * (1.0 + y * (0.5 + y * (1.0 / 6.0)))
    return jnp.where(y > -0.01, series, 1.0 - exp_y)


def _softplus(x):
    return jnp.maximum(x, 0.0) + jnp.log(1.0 + jnp.exp(-jnp.abs(x)))


def _row_ids(width):
    return lax.broadcasted_iota(jnp.int32, (ROWS, width), 0)


def _shift_down(cur, prev, k, rid):
    return jnp.where(rid >= k, pltpu.roll(cur, k, 0), pltpu.roll(prev, k, 0))


def _shift_up(cur, nxt, k, rid):
    return jnp.where(rid < ROWS - k, pltpu.roll(cur, ROWS - k, 0), pltpu.roll(nxt, ROWS - k, 0))


def _mean_last(x):
    return jnp.mean(x, axis=-1, keepdims=True)


def _rows(g):
    return pl.ds(pl.multiple_of(g * ROWS, ROWS), ROWS)


TILE_ROWS = 16


def _tile_rows(q):
    return pl.ds(pl.multiple_of(q * TILE_ROWS, TILE_ROWS), TILE_ROWS)


UNROLL = 4
TILE_UNROLL = 8


def _loop(n, body, init, unroll=UNROLL):
    def wide(i, carry):
        for u in range(unroll):
            carry = body(i * unroll + u, carry)
        return carry

    return lax.fori_loop(0, n // unroll, wide, init)


def _fold_rows(x):
    return x[0:ROWS, :] + x[ROWS:TILE_ROWS, :]


def _bcast_row(x, r):
    return jnp.broadcast_to(x[r:r + 1, :], x.shape)


def _dot(a, b):
    return jnp.dot(a, b, preferred_element_type=F32)


def _dot_nt(a, b):
    return lax.dot_general(a, b, (((1,), (1,)), ((), ())), preferred_element_type=F32)


def _dot_tn(a, b):
    return lax.dot_general(a, b, (((0,), (0,)), ((), ())), preferred_element_type=F32)


def _mesh_place():
    x, y, c = lax.axis_index("x"), lax.axis_index("y"), lax.axis_index("c")
    return x, y, c, 4 * x + 2 * y + c


def _peer(x, y, c, k):
    px = 1 - x if k & 4 else x
    py = 1 - y if k & 2 else y
    pc = 1 - c if k & 1 else c
    return (px, py, pc), 4 * px + 2 * py + pc


def _remote(src, dst, send_sem, recv_sem, dev):
    return pltpu.make_async_remote_copy(src_ref=src, dst_ref=dst, send_sem=send_sem, recv_sem=recv_sem, device_id=dev,
                                        device_id_type=pl.DeviceIdType.MESH)


ANY_SPEC = pl.BlockSpec(memory_space=pl.ANY)


class _Exchange:
    def __init__(self, arrs, scatter):
        self.n = len(arrs)
        self.scatter = tuple(scatter)
        self.out_shape = [SDS(a.shape if s else (N_DEV,) + a.shape, a.dtype) for a, s in zip(arrs, scatter)]
        self.scratch = [pltpu.SemaphoreType.DMA((self.n * N_DEV,)), pltpu.SemaphoreType.DMA((self.n * N_DEV,)),
                        pltpu.SemaphoreType.DMA((self.n,))]

    def _copies(self, ins, outs, sems):
        send_sems, recv_sems, local_sems = sems
        x, y, c, me = _mesh_place()
        local, sends, recvs = [], [], []
        for a in range(self.n):
            src = ins[a].at[me] if self.scatter[a] else ins[a]
            local.append(pltpu.make_async_copy(src, outs[a].at[me], local_sems.at[a]))
        for k in range(1, N_DEV):
            dev, lin = _peer(x, y, c, k)
            for a in range(self.n):
                src = ins[a].at[lin] if self.scatter[a] else ins[a]
                pair = (send_sems.at[a * N_DEV + k], recv_sems.at[a * N_DEV + k], dev)
                sends.append(_remote(src, outs[a].at[me], *pair))
                recvs.append(_remote(src, outs[a].at[lin], *pair))
        return local, sends, recvs

    def start(self, ins, outs, sems):
        local, sends, _ = self._copies(ins, outs, sems)
        for cp in local + sends:
            cp.start()

    def wait(self, ins, outs, sems):
        local, sends, recvs = self._copies(ins, outs, sems)
        for cp in recvs:
            cp.wait_recv()
        for cp in sends:
            cp.wait_send()
        for cp in local:
            cp.wait()


def _exchange(arrs, scatter, name):
    ex = _Exchange(arrs, [scatter] * len(arrs))
    n = ex.n

    def body(*refs):
        ins, outs, sems = refs[:n], refs[n:2 * n], refs[2 * n:]
        ex.start(ins, outs, sems)
        ex.wait(ins, outs, sems)

    return pl.pallas_call(
        body, name=name, out_shape=ex.out_shape, in_specs=[ANY_SPEC] * n, out_specs=[ANY_SPEC] * n,
        scratch_shapes=ex.scratch,
    )(*arrs)


def _pre_norm(x, pre_g, tm=512):
    t_len = x.shape[0]

    def body(x_ref, g_ref, hn_ref):
        g = g_ref[...]

        def rows_body(q, _):
            rows = _tile_rows(q)
            xv = x_ref[rows, :]
            hn_ref[rows, :] = (xv * lax.rsqrt(_mean_last(xv * xv) + EPS) * g).astype(BF16)
            return 0

        _loop(tm // TILE_ROWS, rows_body, 0, unroll=TILE_UNROLL)

    tile = pl.BlockSpec((tm, D_MODEL), lambda i: (i, 0))
    return pl.pallas_call(
        body, name="pre_norm", grid=(t_len // tm,),
        in_specs=[tile, pl.BlockSpec((1, D_MODEL), lambda i: (0, 0))], out_specs=tile,
        out_shape=SDS((t_len, D_MODEL), BF16),
        compiler_params=_params(("arbitrary",), 24),
    )(x, pre_g)


CHIP_ORDER = (0, 2, 4, 6)
W_BODY, W_TAIL = 512, 128
SIBLING = 1
ICI_MASKS = (2, 4, 6)
DIRECT_MASKS = (SIBLING,) + ICI_MASKS
Y_NEIGHBOUR, X_NEIGHBOUR, DIAGONAL = 2, 4, 6
W_DIRECT = (SIBLING, Y_NEIGHBOUR, X_NEIGHBOUR)


def _in_proj(hn, w_shard, others, tm=1024):
    t_len = hn.shape[0]
    n_i = t_len // tm
    n_o = len(others)
    me_out = 4 * lax.axis_index("x") + 2 * lax.axis_index("y") + lax.axis_index("c")
    order = jnp.stack([(me_out ^ chip) // 2 for chip in CHIP_ORDER]).astype(jnp.int32)

    def body(order_ref, hn_ref, w_hbm, *refs):
        o_in = refs[:n_o]
        z_ref, wg_hbm = refs[n_o], refs[n_o + 1]
        o_out = refs[n_o + 2:2 * n_o + 2]
        (wbuf, tail_s, send_w, recv_w, fsend_w, frecv_w, send_o, recv_o, fsend_o, frecv_o, wb_sems, loc_sems, rsend,
         rrecv) = refs[2 * n_o + 2:]
        j, i = pl.program_id(0), pl.program_id(1)
        x, y, c, me = _mesh_place()
        sib = _peer(x, y, c, SIBLING)[0]

        def relay(core):
            src, dst = (Y_NEIGHBOUR, X_NEIGHBOUR) if core == 0 else (X_NEIGHBOUR, Y_NEIGHBOUR)
            held, diag = _peer(x, y, c, src)[1], _peer(x, y, c, DIAGONAL)[1]
            pair = (rsend.at[0], rrecv.at[0], _peer(x, y, c, dst)[0])
            return _remote(wbuf.at[held], wbuf.at[held], *pair), _remote(wbuf.at[diag], wbuf.at[diag], *pair)

        def direct(k, a=None):
            dev, lin = _peer(x, y, c, k)
            if a is None:
                return (_remote(w_hbm, wbuf.at[me], send_w.at[k], recv_w.at[k], dev),
                        _remote(w_hbm, wbuf.at[lin], send_w.at[k], recv_w.at[k], dev))
            pair = (send_o.at[a * N_DEV + k], recv_o.at[a * N_DEV + k], dev)
            return _remote(o_in[a], o_out[a].at[me], *pair), _remote(o_in[a], o_out[a].at[lin], *pair)

        def passed(k, a=None):
            mine, theirs = _peer(x, y, c, k)[1], _peer(x, y, c, k ^ SIBLING)[1]
            if a is None:
                pair = (fsend_w.at[k], frecv_w.at[k], sib)
                return _remote(wbuf.at[mine], wbuf.at[mine], *pair), _remote(wbuf.at[theirs], wbuf.at[theirs], *pair)
            pair = (fsend_o.at[a * N_DEV + k], frecv_o.at[a * N_DEV + k], sib)
            return (_remote(o_out[a].at[mine], o_out[a].at[mine], *pair),
                    _remote(o_out[a].at[theirs], o_out[a].at[theirs], *pair))

        def own_copies():
            return [pltpu.make_async_copy(o_in[a], o_out[a].at[me], loc_sems.at[1 + a]) for a in range(n_o)]

        @pl.when(jnp.logical_and(j == 0, i == 0))
        def _():
            own = pltpu.make_async_copy(w_hbm, wbuf.at[me], loc_sems.at[0])
            own.start()
            for cp in own_copies():
                cp.start()
            for k in W_DIRECT:
                direct(k)[0].start()
            for k in DIRECT_MASKS:
                for a in range(n_o):
                    direct(k, a)[0].start()
            own.wait()

        low = 2 * order_ref[j]

        for jp, chip in enumerate(CHIP_ORDER):
            @pl.when(jnp.logical_and(j == jp, i == 0))
            def _(jp=jp, chip=chip):
                if chip == 0:
                    direct(SIBLING)[1].wait_recv()
                elif chip == Y_NEIGHBOUR:
                    for mask in (Y_NEIGHBOUR, X_NEIGHBOUR):
                        direct(mask)[1].wait_recv()
                        passed(mask)[0].start()
                    for core in (0, 1):
                        @pl.when(c == core)
                        def _(core=core):
                            relay(core)[0].start()
                    passed(Y_NEIGHBOUR)[1].wait_recv()
                elif chip == X_NEIGHBOUR:
                    passed(X_NEIGHBOUR)[1].wait_recv()
                    for core in (0, 1):
                        @pl.when(c == core)
                        def _(core=core):
                            relay(core)[1].wait_recv()
                    passed(DIAGONAL)[0].start()
                    for k in ICI_MASKS:
                        for a in range(n_o):
                            direct(k, a)[1].wait_recv()
                            passed(k, a)[0].start()
                else:
                    passed(DIAGONAL)[1].wait_recv()
                for half in (0, 1):
                    pltpu.make_async_copy(wbuf.at[low + half], wg_hbm.at[low + half], wb_sems.at[2 * jp + half]).start()
                tail_s[:, 0:W_TAIL] = wbuf[low, :, W_BODY:W_IN_SHARD]
                tail_s[:, W_TAIL:2 * W_TAIL] = wbuf[low + 1, :, W_BODY:W_IN_SHARD]

        hn = hn_ref[...]
        z_ref[:, 0:W_BODY] = _dot(hn, wbuf[low, :, 0:W_BODY])
        z_ref[:, W_IN_SHARD:W_IN_SHARD + W_BODY] = _dot(hn, wbuf[low + 1, :, 0:W_BODY])
        tails = _dot(hn, tail_s[...])
        z_ref[:, W_BODY:W_IN_SHARD] = tails[:, 0:W_TAIL]
        z_ref[:, W_IN_SHARD + W_BODY:2 * W_IN_SHARD] = tails[:, W_TAIL:2 * W_TAIL]

        @pl.when(jnp.logical_and(j == len(CHIP_ORDER) - 1, i == n_i - 1))
        def _():
            for a in range(n_o):
                direct(SIBLING, a)[1].wait_recv()
            for k in ICI_MASKS:
                for a in range(n_o):
                    passed(k, a)[1].wait_recv()
            for k in W_DIRECT:
                direct(k)[0].wait_send()
            for core in (0, 1):
                @pl.when(c == core)
                def _(core=core):
                    relay(core)[0].wait_send()
            for k in DIRECT_MASKS:
                for a in range(n_o):
                    direct(k, a)[0].wait_send()
            for k in ICI_MASKS:
                passed(k)[0].wait_send()
                for a in range(n_o):
                    passed(k, a)[0].wait_send()
            for cp in own_copies():
                cp.wait()
            for jj in range(N_DEV):
                pltpu.make_async_copy(wbuf.at[0], wg_hbm.at[0], wb_sems.at[jj]).wait()

    dma = lambda n: pltpu.SemaphoreType.DMA((n,))
    grid_spec = pltpu.PrefetchScalarGridSpec(
        num_scalar_prefetch=1, grid=(len(CHIP_ORDER), n_i),
        in_specs=[pl.BlockSpec((tm, D_MODEL), lambda j, i, order: (i, 0)), ANY_SPEC] + [ANY_SPEC] * n_o,
        out_specs=[pl.BlockSpec((tm, 2 * W_IN_SHARD), lambda j, i, order: (i, order[j])), ANY_SPEC] + [ANY_SPEC] * n_o,
        scratch_shapes=[pltpu.VMEM((N_DEV, D_MODEL, W_IN_SHARD), BF16), pltpu.VMEM((D_MODEL, 2 * W_TAIL), BF16),
                        dma(N_DEV), dma(N_DEV), dma(N_DEV), dma(N_DEV),
                        dma(n_o * N_DEV), dma(n_o * N_DEV), dma(n_o * N_DEV), dma(n_o * N_DEV), dma(N_DEV), dma(1 + n_o),
                        dma(1), dma(1)])
    res = pl.pallas_call(
        body, name="in_proj", grid_spec=grid_spec,
        out_shape=[SDS((t_len, D_IN), F32), SDS((N_DEV, D_MODEL, W_IN_SHARD), BF16)]
        + [SDS((N_DEV,) + o.shape, o.dtype) for o in others],
        compiler_params=_params(("arbitrary", "arbitrary"), 54),
    )(order, hn, w_shard, *others)
    return res[0], res[1], res[2:]


def _conv_rows(cur, prev, cw_ref, cb, rid):
    acc = cw_ref[3:4, :] * cur + cb
    for k in range(1, CONV_W):
        acc = acc + cw_ref[3 - k:4 - k, :] * _shift_down(cur, prev, k, rid)
    return acc


def _lru_gates(pa, px, ba, bx, sp8, first_row):
    r = _sig(pa + ba)
    i = _sig(px + bx)
    la = -(r * sp8)
    a = jnp.exp(la)
    mult = jnp.where(first_row, 1.0, jnp.sqrt(_neg_expm1(2.0 * la, a * a)))
    return r, i, a, mult


def _mix_fwd(z, ln_g, ln_b, wm, bias, cw, cb, wax, ba, bx, lam, goa, gob, ex_arrs, ex_scatter):
    t_len = z.shape[0]
    n_chunk = t_len // CHUNK
    ex = _Exchange(ex_arrs, ex_scatter)
    n_in, n_out, n_scratch = 13, 5, 7

    def body(*refs):
        (z_ref, lng_ref, lnb_ref, wm_ref, bias_ref, cw_ref, cb_ref, wax_ref, ba_ref, bx_ref, lam_ref, goa_ref,
         gob_ref) = refs[:n_in]
        ex_in = refs[n_in:n_in + ex.n]
        y_ref, h_ref, vhb_ref, xcb_ref, rs_ref = refs[n_in + ex.n:n_in + ex.n + n_out]
        ex_out = refs[n_in + ex.n + n_out:n_in + 2 * ex.n + n_out]
        vn_s, xc_s, mixed_s, pre_s, y_s, carry_s, halo_s = refs[n_in + 2 * ex.n + n_out:n_in + 2 * ex.n + n_out + n_scratch]
        ex_sems = refs[n_in + 2 * ex.n + n_out + n_scratch:]
        c_id = pl.program_id(0)
        rid = _row_ids(D_BR)

        @pl.when(c_id == 0)
        def _():
            ex.start(ex_in, ex_out, ex_sems)
            carry_s[...] = jnp.zeros_like(carry_s)
            halo_s[...] = jnp.zeros_like(halo_s)

        lng, lnb, cb = lng_ref[...], lnb_ref[...], cb_ref[...]

        def phase1(g, prev):
            rows = _rows(g)
            vg, _ = _gelu(z_ref[rows, D_BR:2 * D_BR])
            xm = vg - _mean_last(vg)
            rs = lax.rsqrt(_mean_last(xm * xm) + EPS)
            vn_s[rows, :] = xm * rs
            rs_ref[rows, :] = jnp.broadcast_to(rs, (ROWS, HEAD))
            xb = z_ref[rows, 3 * D_BR:4 * D_BR]
            xc_s[rows, :] = _conv_rows(xb, prev, cw_ref, cb, rid)
            return xb

        halo_s[...] = _loop(N_GROUP, phase1, halo_s[...], unroll=8)
        vhb_ref[...] = vn_s[...].astype(BF16)
        xcb_ref[...] = xc_s[...].astype(BF16)

        for h in range(N_HEAD):
            cs = slice(h * HEAD, (h + 1) * HEAD)
            mixed_s[:, cs] = _dot(wm_ref[h], (vn_s[:, cs] * lng[:, cs] + lnb[:, cs]).astype(BF16))
            pre = _dot(xcb_ref[:, cs], wax_ref[h])
            pre_s[:, cs] = pre[:, :HEAD]
            pre_s[:, D_BR + h * HEAD:D_BR + (h + 1) * HEAD] = pre[:, HEAD:]

        ba, bx, goa, gob = ba_ref[...], bx_ref[...], goa_ref[...], gob_ref[...]
        sp8 = LRU_C * _softplus(-lam_ref[...])

        def phase3(g, carry):
            rows = _rows(g)
            ug, _ = _gelu(z_ref[rows, 0:D_BR])
            ga = z_ref[rows, 2 * D_BR:3 * D_BR]
            ya = ug * (mixed_s[rows, :] + bias_ref[rows, :]) * (ga * _sig(ga))
            y_s[rows, 0:D_BR] = ya * lax.rsqrt(_mean_last(ya * ya) + EPS) * goa

            first_row = jnp.logical_and(jnp.logical_and(c_id == 0, g == 0), rid == 0)
            _, i, a, mult = _lru_gates(pre_s[rows, 0:D_BR], pre_s[rows, D_BR:2 * D_BR], ba, bx, sp8, first_row)
            b = mult * i * xc_s[rows, :]
            for d in (1, 2, 4):
                a_sh = jnp.where(rid >= d, pltpu.roll(a, d, 0), 1.0)
                b_sh = jnp.where(rid >= d, pltpu.roll(b, d, 0), 0.0)
                b = a * b_sh + b
                a = a * a_sh
            hh = b + a * carry
            h_ref[rows, :] = hh
            gb = z_ref[rows, 4 * D_BR:5 * D_BR]
            yb = hh * (gb * _sig(gb))
            y_s[rows, D_BR:2 * D_BR] = yb * lax.rsqrt(_mean_last(yb * yb) + EPS) * gob
            return _bcast_row(hh, ROWS - 1)

        carry_s[...] = _loop(N_GROUP, phase3, carry_s[...])
        y_ref[...] = y_s[...].astype(BF16)

        @pl.when(c_id == n_chunk - 1)
        def _():
            ex.wait(ex_in, ex_out, ex_sems)

    vec = pl.BlockSpec((1, D_BR), lambda i: (0, 0))
    res = pl.pallas_call(
        body, name="mix_fwd", grid=(n_chunk,),
        in_specs=[pl.BlockSpec((CHUNK, D_IN), lambda i: (i, 0)), vec, vec,
                  pl.BlockSpec((N_HEAD, HEAD, HEAD), lambda i: (0, 0, 0)),
                  pl.BlockSpec((CHUNK, D_BR), lambda i: (0, 0)),
                  pl.BlockSpec((ROWS, D_BR), lambda i: (0, 0)), vec,
                  pl.BlockSpec((N_HEAD, HEAD, 2 * HEAD), lambda i: (0, 0, 0)), vec, vec, vec, vec, vec]
        + [ANY_SPEC] * ex.n,
        out_specs=[pl.BlockSpec((CHUNK, 2 * D_BR), lambda i: (i, 0)), pl.BlockSpec((CHUNK, D_BR), lambda i: (i, 0)),
                   pl.BlockSpec((CHUNK, D_BR), lambda i: (i, 0)), pl.BlockSpec((CHUNK, D_BR), lambda i: (i, 0)),
                   pl.BlockSpec((CHUNK, HEAD), lambda i: (i, 0))] + [ANY_SPEC] * ex.n,
        out_shape=[SDS((t_len, 2 * D_BR), BF16), SDS((t_len, D_BR), F32), SDS((t_len, D_BR), BF16),
                   SDS((t_len, D_BR), BF16), SDS((t_len, HEAD), F32)] + ex.out_shape,
        scratch_shapes=[pltpu.VMEM((CHUNK, D_BR), F32), pltpu.VMEM((CHUNK, D_BR), F32), pltpu.VMEM((CHUNK, D_BR), F32),
                        pltpu.VMEM((CHUNK, 2 * D_BR), F32), pltpu.VMEM((CHUNK, 2 * D_BR), F32),
                        pltpu.VMEM((ROWS, D_BR), F32), pltpu.VMEM((ROWS, D_BR), F32)] + ex.scratch,
        compiler_params=_params(("arbitrary",), 32),
    )(z, ln_g, ln_b, wm, bias, cw, cb, wax, ba, bx, lam, goa, gob, *ex_arrs)
    return res[:n_out], res[n_out:]


def _load_weight(w_hbm, w_vmem, sem):
    @pl.when(pl.program_id(0) == 0)
    def _():
        cp = pltpu.make_async_copy(w_hbm, w_vmem, sem)
        cp.start()
        cp.wait()


def _out_proj(y, x, w_out, post_g, tm=512):
    t_len = y.shape[0]

    def body(y_ref, x_ref, w_hbm, g_ref, h1_ref, ob_ref, w_s, o_s, sem):
        _load_weight(w_hbm, w_s, sem)
        o_s[...] = _dot(y_ref[...], w_s[...])
        g = g_ref[...]

        def rows_body(q, _):
            rows = _tile_rows(q)
            o = o_s[rows, :]
            h1_ref[rows, :] = x_ref[rows, :] + o * lax.rsqrt(_mean_last(o * o) + EPS) * g
            ob_ref[rows, :] = o.astype(BF16)
            return 0

        _loop(tm // TILE_ROWS, rows_body, 0, unroll=TILE_UNROLL)

    tile = pl.BlockSpec((tm, D_MODEL), lambda i: (i, 0))
    return pl.pallas_call(
        body, name="out_proj", grid=(t_len // tm,),
        in_specs=[tile, tile, pl.BlockSpec(memory_space=pl.ANY), pl.BlockSpec((1, D_MODEL), lambda i: (0, 0))],
        out_specs=[tile, tile],
        out_shape=[SDS((t_len, D_MODEL), F32), SDS((t_len, D_MODEL), BF16)],
        scratch_shapes=[pltpu.VMEM((D_MODEL, D_MODEL), BF16), pltpu.VMEM((tm, D_MODEL), F32), pltpu.SemaphoreType.DMA],
        compiler_params=_params(("arbitrary",), 44),
    )(y, x, w_out, post_g)


def _ple_loss(h1, p, tgt, w_pg, w_pe_g, tm=256):
    t_len = h1.shape[0]
    n_tile = t_len // tm
    pe_shard = D_MODEL // N_DEV

    def body(h1_ref, p_ref, t_ref, w_hbm, wpe_ref, dh2_ref, dgl_ref, h1b_ref, loss_ref, dwpe_ref, w_s, pe_s, gl_s, acc_s,
             dpe_s, gpe_s, sem):
        _load_weight(w_hbm, w_s, sem)
        i = pl.program_id(0)

        @pl.when(i == 0)
        def _():
            acc_s[...] = jnp.zeros_like(acc_s)
            gpe_s[...] = jnp.zeros_like(gpe_s)

        h1b_ref[...] = h1_ref[...].astype(BF16)
        pb = p_ref[...].astype(BF16)
        for j in range(N_DEV):
            pe_s[:, j * pe_shard:(j + 1) * pe_shard] = _dot(pb, wpe_ref[j])
        gl_s[...] = _dot(h1b_ref[...], w_s[...])

        def rows_body(q, acc):
            rows = _tile_rows(q)
            pe = pe_s[rows, :]
            g = _sig(gl_s[rows, :])
            e = h1_ref[rows, :] + pe * g - t_ref[rows, :]
            dh2 = e * (1.0 / D_MODEL)
            dh2_ref[rows, :] = dh2
            dpe_s[rows, :] = (dh2 * g).astype(BF16)
            dgl_ref[rows, :] = (dh2 * pe * g * (1.0 - g)).astype(BF16)
            return acc + _fold_rows(e * e)

        acc_s[...] = _loop(tm // TILE_ROWS, rows_body, acc_s[...], unroll=TILE_UNROLL)
        gpe_s[...] += _dot_tn(pb, dpe_s[...])

        @pl.when(i == n_tile - 1)
        def _():
            loss_ref[...] = jnp.full(loss_ref.shape, 0.5 / D_MODEL * jnp.sum(acc_s[...]), F32)
            for j in range(N_DEV):
                dwpe_ref[j] = gpe_s[:, j * pe_shard:(j + 1) * pe_shard].astype(BF16)

    tile = pl.BlockSpec((tm, D_MODEL), lambda i: (i, 0))
    pe_blocks = pl.BlockSpec((N_DEV, D_PLE, pe_shard), lambda i: (0, 0, 0))
    return pl.pallas_call(
        body, name="ple_loss", grid=(n_tile,),
        in_specs=[tile, pl.BlockSpec((tm, D_PLE), lambda i: (i, 0)), tile, pl.BlockSpec(memory_space=pl.ANY), pe_blocks],
        out_specs=[tile, tile, tile, pl.BlockSpec((ROWS, HEAD), lambda i: (0, 0)), pe_blocks],
        out_shape=[SDS((t_len, D_MODEL), F32), SDS((t_len, D_MODEL), BF16), SDS((t_len, D_MODEL), BF16),
                   SDS((ROWS, HEAD), F32), SDS((N_DEV, D_PLE, pe_shard), BF16)],
        scratch_shapes=[pltpu.VMEM((D_MODEL, D_MODEL), BF16), pltpu.VMEM((tm, D_MODEL), F32),
                        pltpu.VMEM((tm, D_MODEL), F32), pltpu.VMEM((ROWS, D_MODEL), F32), pltpu.VMEM((tm, D_MODEL), BF16),
                        pltpu.VMEM((D_PLE, D_MODEL), F32), pltpu.SemaphoreType.DMA],
        compiler_params=_params(("arbitrary",), 48),
    )(h1, p, tgt, w_pg, w_pe_g)


def _tail_bwd(dh2, dgl, ob, w_pg, w_out, post_g, tm=256):
    t_len = dh2.shape[0]
    n_tile = t_len // tm

    def body(dh2_ref, dgl_ref, ob_ref, wpg_hbm, wout_hbm, g_ref, dh1_ref, do_ref, dy_ref, dg_ref, wpg_s, wout_s, t_s,
             acc_s, sems):
        _load_weight(wpg_hbm, wpg_s, sems.at[0])
        _load_weight(wout_hbm, wout_s, sems.at[1])
        i = pl.program_id(0)

        @pl.when(i == 0)
        def _():
            acc_s[...] = jnp.zeros_like(acc_s)

        t_s[...] = _dot_nt(dgl_ref[...], wpg_s[...])
        g = g_ref[...]

        def rows_body(q, acc):
            rows = _tile_rows(q)
            dh1 = dh2_ref[rows, :] + t_s[rows, :]
            dh1_ref[rows, :] = dh1
            o = ob_ref[rows, :].astype(F32)
            rr = lax.rsqrt(_mean_last(o * o) + EPS)
            on = o * rr
            dog = dh1 * g
            do_ref[rows, :] = (rr * (dog - on * _mean_last(dog * on))).astype(BF16)
            return acc + _fold_rows(dh1 * on)

        acc_s[...] = _loop(tm // TILE_ROWS, rows_body, acc_s[...], unroll=TILE_UNROLL)
        dy_ref[...] = _dot_nt(do_ref[...], wout_s[...]).astype(BF16)

        @pl.when(i == n_tile - 1)
        def _():
            dg_ref[...] = jnp.sum(acc_s[...], axis=0, keepdims=True)

    tile = pl.BlockSpec((tm, D_MODEL), lambda i: (i, 0))
    vec = pl.BlockSpec((1, D_MODEL), lambda i: (0, 0))
    hbm = pl.BlockSpec(memory_space=pl.ANY)
    return pl.pallas_call(
        body, name="tail_bwd", grid=(n_tile,),
        in_specs=[tile, tile, tile, hbm, hbm, vec],
        out_specs=[tile, tile, tile, vec],
        out_shape=[SDS((t_len, D_MODEL), F32), SDS((t_len, D_MODEL), BF16), SDS((t_len, D_MODEL), BF16),
                   SDS((1, D_MODEL), F32)],
        scratch_shapes=[pltpu.VMEM((D_MODEL, D_MODEL), BF16), pltpu.VMEM((D_MODEL, D_MODEL), BF16),
                        pltpu.VMEM((tm, D_MODEL), F32), pltpu.VMEM((ROWS, D_MODEL), F32), pltpu.SemaphoreType.DMA((2,))],
        compiler_params=_params(("arbitrary",), 48),
    )(dh2, dgl, ob, w_pg, w_out, post_g)


def _mix_bwd(z, dy, h, vhb, xcb, rs, ln_g, ln_b, wm, wm_t, bias, cw, cb, wax, wax_t, ba, bx, lam, goa, gob, ex_arrs,
             ex_scatter):
    t_len = z.shape[0]
    n_chunk = t_len // CHUNK
    halo_blocks = CHUNK // ROWS
    ex = _Exchange(ex_arrs, ex_scatter)
    n_in, n_out, n_scratch = 21, 5, 16

    def body(*refs):
        (z_ref, dy_ref, h_ref, hhalo_ref, vhb_ref, xcb_ref, rs_ref, lng_ref, lnb_ref, wm_ref, wmt_ref, bias_ref, cw_ref,
         cb_ref, wax_ref, waxt_ref, ba_ref, bx_ref, lam_ref, goa_ref, gob_ref) = refs[:n_in]
        ex_in = refs[n_in:n_in + ex.n]
        dz_ref, vecs_ref, dws_ref, dwax_ref, dbs_ref = refs[n_in + ex.n:n_in + ex.n + n_out]
        ex_out = refs[n_in + ex.n + n_out:n_in + 2 * ex.n + n_out]
        (vnb_s, vh_s, xc_s, mixed_s, pre_s, dmix_s, dvn_s, dho_s, dxc_s, dpre_s, dz_s, acc_s, accdm_s,
         cg_s, ca_s, dxchalo_s) = refs[n_in + 2 * ex.n + n_out:n_in + 2 * ex.n + n_out + n_scratch]
        ex_sems = refs[n_in + 2 * ex.n + n_out + n_scratch:]
        step = pl.program_id(0)
        c_id = n_chunk - 1 - step
        rid = _row_ids(D_BR)
        first_chunk = c_id == 0

        @pl.when(step == 0)
        def _():
            ex.start(ex_in, ex_out, ex_sems)
            acc_s[...] = jnp.zeros_like(acc_s)
            accdm_s[...] = jnp.zeros_like(accdm_s)
            cg_s[...] = jnp.zeros_like(cg_s)
            ca_s[...] = jnp.zeros_like(ca_s)
            dxchalo_s[...] = jnp.zeros_like(dxchalo_s)
            dws_ref[...] = jnp.zeros_like(dws_ref)
            dwax_ref[...] = jnp.zeros_like(dwax_ref)

        lng, lnb = lng_ref[...], lnb_ref[...]
        h_halo = jnp.where(first_chunk, 0.0, hhalo_ref[...])

        def prev_rows(ref, cols, g, halo):
            before = ref[pl.ds(pl.multiple_of(jnp.maximum(g - 1, 0) * ROWS, ROWS), ROWS), cols]
            return jnp.where(g > 0, before, halo)

        vh_s[...] = vhb_ref[...].astype(F32)
        xc_s[...] = xcb_ref[...].astype(F32)

        for hd in range(N_HEAD):
            cs = slice(hd * HEAD, (hd + 1) * HEAD)
            vnb_s[:, cs] = (vh_s[:, cs] * lng[:, cs] + lnb[:, cs]).astype(BF16)
            mixed_s[:, cs] = _dot(wm_ref[hd], vnb_s[:, cs])
            pre = _dot(xcb_ref[:, cs], wax_ref[hd])
            pre_s[:, cs] = pre[:, :HEAD]
            pre_s[:, D_BR + hd * HEAD:D_BR + (hd + 1) * HEAD] = pre[:, HEAD:]

        goa, gob = goa_ref[...], gob_ref[...]

        def phase3(g, _):
            rows = _rows(g)
            u = z_ref[rows, 0:D_BR]
            ug, tu = _gelu(u)
            ga = z_ref[rows, 2 * D_BR:3 * D_BR]
            sga = _sig(ga)
            sa = ga * sga
            mixed = mixed_s[rows, :] + bias_ref[rows, :]
            ya0 = ug * mixed
            ya = ya0 * sa
            ra = lax.rsqrt(_mean_last(ya * ya) + EPS)
            dyan = dy_ref[rows, 0:D_BR].astype(F32)
            acc_s[V_GOUT_A] += dyan * ya * ra
            dyg = dyan * goa
            dya = ra * dyg - ya * (ra * ra * ra) * _mean_last(dyg * ya)
            dya0 = dya * sa
            dz_s[rows, 2 * D_BR:3 * D_BR] = dya * ya0 * (sga * (1.0 + ga * (1.0 - sga)))
            dmix = dya0 * ug
            dmix_s[rows, :] = dmix
            accdm_s[rows, :] += dmix
            dz_s[rows, 0:D_BR] = dya0 * mixed * _gelu_grad(u, tu)

            hh = h_ref[rows, :]
            gb = z_ref[rows, 4 * D_BR:5 * D_BR]
            sgb = _sig(gb)
            sb = gb * sgb
            yb = hh * sb
            rb = lax.rsqrt(_mean_last(yb * yb) + EPS)
            dybn = dy_ref[rows, D_BR:2 * D_BR].astype(F32)
            acc_s[V_GOUT_B] += dybn * yb * rb
            dyg = dybn * gob
            dyb = rb * dyg - yb * (rb * rb * rb) * _mean_last(dyg * yb)
            dho_s[rows, :] = dyb * sb
            dz_s[rows, 4 * D_BR:5 * D_BR] = dyb * hh * (sgb * (1.0 + gb * (1.0 - sgb)))
            return 0

        _loop(N_GROUP, phase3, 0)

        for hd in range(N_HEAD):
            cs = slice(hd * HEAD, (hd + 1) * HEAD)
            dmb = dmix_s[:, cs].astype(BF16)
            dvn_s[:, cs] = _dot(wmt_ref[hd], dmb)
            dws_ref[hd] += _dot_nt(dmb, vnb_s[:, cs])

        def phase5(g, _):
            rows = _rows(g)
            dvn = dvn_s[rows, :]
            vh = vh_s[rows, :]
            acc_s[V_LN_G] += dvn * vh
            acc_s[V_LN_B] += dvn
            dvh = dvn * lng
            rs = rs_ref[rows, 0:1]
            dvg = rs * (dvh - _mean_last(dvh) - vh * _mean_last(dvh * vh))
            v = z_ref[rows, D_BR:2 * D_BR]
            _, tv = _gelu(v)
            dz_s[rows, D_BR:2 * D_BR] = dvg * _gelu_grad(v, tv)
            return 0

        _loop(N_GROUP, phase5, 0)

        ba, bx = ba_ref[...], bx_ref[...]
        sp8 = LRU_C * _softplus(-lam_ref[...])

        def phase6(k, carry):
            cg, ca = carry
            g = N_GROUP - 1 - k
            rows = _rows(g)
            first_row = jnp.logical_and(jnp.logical_and(first_chunk, g == 0), rid == 0)
            r, i, a, mult = _lru_gates(pre_s[rows, 0:D_BR], pre_s[rows, D_BR:2 * D_BR], ba, bx, sp8, first_row)
            a_nx = jnp.where(rid < ROWS - 1, pltpu.roll(a, ROWS - 1, 0), ca)
            aa, bb = a_nx, dho_s[rows, :]
            for d in (1, 2, 4):
                a_sh = jnp.where(rid < ROWS - d, pltpu.roll(aa, ROWS - d, 0), 1.0)
                b_sh = jnp.where(rid < ROWS - d, pltpu.roll(bb, ROWS - d, 0), 0.0)
                bb = aa * b_sh + bb
                aa = aa * a_sh
            gg = bb + aa * cg
            hh = h_ref[rows, :]
            hprev = _shift_down(hh, prev_rows(h_ref, slice(None), g, h_halo), 1, rid)
            xc = xc_s[rows, :]
            gx = gg * xc
            dla = gg * hprev * a - jnp.where(first_row, 0.0, gx * i * (a * a) * lax.rsqrt(mult * mult))
            acc_s[V_LAM] += -(dla * r)
            dpa = -(dla * sp8) * r * (1.0 - r)
            dpx = gx * mult * i * (1.0 - i)
            acc_s[V_B_A] += dpa
            acc_s[V_B_X] += dpx
            dpre_s[rows, 0:D_BR] = dpa
            dpre_s[rows, D_BR:2 * D_BR] = dpx
            dxc_s[rows, :] = gg * mult * i
            return _bcast_row(gg, 0), _bcast_row(a, 0)

        cg, ca = _loop(N_GROUP, phase6, (cg_s[...], ca_s[...]))
        cg_s[...] = cg
        ca_s[...] = ca

        for hd in range(N_HEAD):
            cs = slice(hd * HEAD, (hd + 1) * HEAD)
            dpre = jnp.concatenate([dpre_s[:, cs], dpre_s[:, D_BR + hd * HEAD:D_BR + (hd + 1) * HEAD]], axis=1).astype(BF16)
            dxc_s[:, cs] += _dot(dpre, waxt_ref[hd])
            dwax_ref[hd] += _dot_tn(xcb_ref[:, cs], dpre)

        def phase8(k, nxt):
            g = N_GROUP - 1 - k
            rows = _rows(g)
            dxc = dxc_s[rows, :]
            acc_s[V_CONV_B] += dxc
            xb = z_ref[rows, 3 * D_BR:4 * D_BR]
            dxb = cw_ref[3:4, :] * dxc
            acc_s[V_CONV_W + 3] += dxc * xb
            for j in range(1, CONV_W):
                later = _shift_up(dxc, nxt, j, rid)
                dxb = dxb + cw_ref[3 - j:4 - j, :] * later
                acc_s[V_CONV_W + 3 - j] += later * xb
            dz_s[rows, 3 * D_BR:4 * D_BR] = dxb
            return dxc

        dxchalo_s[...] = _loop(N_GROUP, phase8, dxchalo_s[...])
        dz_ref[...] = dz_s[...].astype(BF16)

        @pl.when(step == n_chunk - 1)
        def _():
            for v in range(N_VEC):
                vecs_ref[v:v + 1, :] = jnp.sum(acc_s[v], axis=0, keepdims=True)
            lam = lam_ref[...]
            vecs_ref[V_LAM:V_LAM + 1, :] = vecs_ref[V_LAM:V_LAM + 1, :] * (-LRU_C * _sig(-lam))
            tril = (lax.broadcasted_iota(jnp.int32, (HEAD, HEAD), 0) >= lax.broadcasted_iota(jnp.int32, (HEAD, HEAD), 1))
            ones = jnp.ones((ROWS, HEAD), BF16)
            for hd in range(N_HEAD):
                cs = slice(hd * HEAD, (hd + 1) * HEAD)
                dws_ref[hd] = jnp.where(tril, dws_ref[hd], 0.0)
                blk = accdm_s[:, cs]
                hi = blk.astype(BF16)
                lo = (blk - hi.astype(F32)).astype(BF16)
                dbs_ref[hd:hd + 1, :] = (_dot_nt(ones, hi) + _dot_nt(ones, lo))[0:1, :]
            ex.wait(ex_in, ex_out, ex_sems)

    vec = pl.BlockSpec((1, D_BR), lambda i: (0, 0))
    rev = lambda i: (n_chunk - 1 - i, 0)
    halo = lambda col: (lambda i: (jnp.maximum((n_chunk - 1 - i) * halo_blocks - 1, 0), col))
    full3 = lambda a, b, c: pl.BlockSpec((a, b, c), lambda i: (0, 0, 0))
    big = lambda w: pltpu.VMEM((CHUNK, w), F32)
    res = pl.pallas_call(
        body, name="mix_bwd", grid=(n_chunk,),
        in_specs=[pl.BlockSpec((CHUNK, D_IN), rev), pl.BlockSpec((CHUNK, 2 * D_BR), rev), pl.BlockSpec((CHUNK, D_BR), rev),
                  pl.BlockSpec((ROWS, D_BR), halo(0)), pl.BlockSpec((CHUNK, D_BR), rev), pl.BlockSpec((CHUNK, D_BR), rev),
                  pl.BlockSpec((CHUNK, HEAD), rev), vec, vec,
                  full3(N_HEAD, HEAD, HEAD), full3(N_HEAD, HEAD, HEAD),
                  pl.BlockSpec((CHUNK, D_BR), lambda i: (0, 0)), pl.BlockSpec((ROWS, D_BR), lambda i: (0, 0)), vec,
                  full3(N_HEAD, HEAD, 2 * HEAD), full3(N_HEAD, 2 * HEAD, HEAD), vec, vec, vec, vec, vec]
        + [ANY_SPEC] * ex.n,
        out_specs=[pl.BlockSpec((CHUNK, D_IN), rev), pl.BlockSpec((N_VEC, D_BR), lambda i: (0, 0)),
                   full3(N_HEAD, HEAD, HEAD), full3(N_HEAD, HEAD, 2 * HEAD),
                   pl.BlockSpec((N_HEAD, HEAD), lambda i: (0, 0))] + [ANY_SPEC] * ex.n,
        out_shape=[SDS((t_len, D_IN), BF16), SDS((N_VEC, D_BR), F32), SDS((N_HEAD, HEAD, HEAD), F32),
                   SDS((N_HEAD, HEAD, 2 * HEAD), F32), SDS((N_HEAD, HEAD), F32)] + ex.out_shape,
        scratch_shapes=[pltpu.VMEM((CHUNK, D_BR), BF16), big(D_BR), big(D_BR), big(D_BR), big(2 * D_BR), big(D_BR),
                        big(D_BR), big(D_BR), big(D_BR), big(2 * D_BR), big(D_IN),
                        pltpu.VMEM((N_VEC, ROWS, D_BR), F32), big(D_BR),
                        pltpu.VMEM((ROWS, D_BR), F32), pltpu.VMEM((ROWS, D_BR), F32), pltpu.VMEM((ROWS, D_BR), F32)]
        + ex.scratch,
        compiler_params=_params(("arbitrary",), 48),
    )(z, dy, h, h, vhb, xcb, rs, ln_g, ln_b, wm, wm_t, bias, cw, cb, wax, wax_t, ba, bx, lam, goa, gob, *ex_arrs)
    return res[:n_out], res[n_out:]


def _in_bwd(dz, w_in_g, x, dh1, pre_g, first_tile, n_tile, prev, name, ex_arrs=(), ex_scatter=(), tm=256):
    t_len = x.shape[0]
    ex = _Exchange(ex_arrs, ex_scatter)
    n_prev = 0 if prev is None else 2

    def body(dz_ref, w_hbm, x_ref, dh1_ref, g_ref, *refs):
        prev_refs, refs = refs[:n_prev], refs[n_prev:]
        ex_in, (gx_ref, dg_ref), ex_out = refs[:ex.n], refs[ex.n:ex.n + 2], refs[ex.n + 2:2 * ex.n + 2]
        w_s, t_s, dg_s, w_sems = refs[2 * ex.n + 2:2 * ex.n + 6]
        ex_sems = refs[2 * ex.n + 6:]
        i = pl.program_id(0)

        @pl.when(i == 0)
        def _():
            if ex.n:
                ex.start(ex_in, ex_out, ex_sems)
            loads = [pltpu.make_async_copy(w_hbm.at[s], w_s.at[:, s * W_IN_SHARD:(s + 1) * W_IN_SHARD], w_sems.at[s])
                     for s in range(N_DEV)]
            for cp in loads:
                cp.start()
            dg_s[...] = jnp.zeros_like(dg_s)
            for cp in loads:
                cp.wait()

        t_s[...] = _dot_nt(dz_ref[...], w_s[...])
        g = g_ref[...]

        def rows_body(q, acc):
            rows = _tile_rows(q)
            xv = x_ref[rows, :]
            r = lax.rsqrt(_mean_last(xv * xv) + EPS)
            xh = xv * r
            dhn = t_s[rows, :]
            dg = dhn * g
            gx_ref[rows, :] = dh1_ref[rows, :] + r * (dg - xh * _mean_last(dg * xh))
            return acc + _fold_rows(dhn * xh)

        dg_s[...] = _loop(tm // TILE_ROWS, rows_body, dg_s[...], unroll=TILE_UNROLL)

        @pl.when(i == n_tile - 1)
        def _():
            dg = jnp.sum(dg_s[...], axis=0, keepdims=True)
            dg_ref[...] = dg + prev_refs[1][...] if n_prev else dg
            if ex.n:
                ex.wait(ex_in, ex_out, ex_sems)

    tile = pl.BlockSpec((tm, D_MODEL), lambda i: (first_tile + i, 0))
    vec = pl.BlockSpec((1, D_MODEL), lambda i: (0, 0))
    prev_specs = [ANY_SPEC, vec] if n_prev else []
    res = pl.pallas_call(
        body, name=name, grid=(n_tile,),
        in_specs=[pl.BlockSpec((tm, D_IN), lambda i: (first_tile + i, 0)), ANY_SPEC, tile, tile, vec] + prev_specs
        + [ANY_SPEC] * ex.n,
        out_specs=[tile, vec] + [ANY_SPEC] * ex.n,
        out_shape=[SDS((t_len, D_MODEL), F32), SDS((1, D_MODEL), F32)] + ex.out_shape,
        scratch_shapes=[pltpu.VMEM((D_MODEL, D_IN), BF16), pltpu.VMEM((tm, D_MODEL), F32), pltpu.VMEM((ROWS, D_MODEL), F32),
                        pltpu.SemaphoreType.DMA((N_DEV,))] + (ex.scratch if ex.n else []),
        input_output_aliases={5: 0} if n_prev else {},
        compiler_params=_params(("arbitrary",), 54),
    )(dz, w_in_g, x, dh1, pre_g, *(prev or ()), *ex_arrs)
    return res[0], res[1], res[2:]


def _grad_w(a, b, bn, shard_major, name, tk=1024, ex_arrs=(), ex_scatter=()):
    t_len, m = a.shape
    n = b.shape[1]
    n_j, n_k = n // bn, t_len // tk
    ex = _Exchange(ex_arrs, ex_scatter)

    def body(a_ref, b_ref, *refs):
        ex_in, o_ref, ex_out = refs[:ex.n], refs[ex.n], refs[ex.n + 1:2 * ex.n + 1]
        acc_s, ex_sems = refs[2 * ex.n + 1], refs[2 * ex.n + 2:]
        j, k = pl.program_id(0), pl.program_id(1)
        if ex.n:
            @pl.when(jnp.logical_and(j == 0, k == 0))
            def _():
                ex.start(ex_in, ex_out, ex_sems)

        @pl.when(k == 0)
        def _():
            acc_s[...] = jnp.zeros_like(acc_s)

        acc_s[...] += _dot_tn(a_ref[...], b_ref[...])

        @pl.when(k == n_k - 1)
        def _():
            o_ref[...] = acc_s[...].astype(BF16)

        if ex.n:
            @pl.when(jnp.logical_and(j == n_j - 1, k == n_k - 1))
            def _():
                ex.wait(ex_in, ex_out, ex_sems)

    if shard_major:
        out_spec, out_shape = pl.BlockSpec((None, m, bn), lambda j, k: (j, 0, 0)), SDS((n_j, m, bn), BF16)
    else:
        out_spec, out_shape = pl.BlockSpec((m, bn), lambda j, k: (0, j)), SDS((m, n), BF16)
    res = pl.pallas_call(
        body, name=name, grid=(n_j, n_k),
        in_specs=[pl.BlockSpec((tk, m), lambda j, k: (k, 0)), pl.BlockSpec((tk, bn), lambda j, k: (k, j))]
        + [ANY_SPEC] * ex.n,
        out_specs=[out_spec] + [ANY_SPEC] * ex.n, out_shape=[out_shape] + ex.out_shape,
        scratch_shapes=[pltpu.VMEM((m, bn), F32)] + (ex.scratch if ex.n else []),
        compiler_params=_params(("arbitrary", "arbitrary"), 40),
    )(a, b, *ex_arrs)
    return res[0], res[1:]


RS_ORDER = (3, 2, 5, 4, 7, 6, 1, 0)
RS_SLOTS = (0, 1, 2, 4, 6)


def _grad_w_in(hn, dz, ex_arrs, ex_scatter, tk=1024):
    t_len = hn.shape[0]
    n_k = t_len // tk
    ex = _Exchange(ex_arrs, ex_scatter)
    me_out = 4 * lax.axis_index("x") + 2 * lax.axis_index("y") + lax.axis_index("c")
    order = jnp.stack([me_out ^ k for k in RS_ORDER]).astype(jnp.int32)
    slots = jnp.stack([me_out ^ k for k in RS_SLOTS]).astype(jnp.int32)
    n_stage = 2

    def body(order_ref, a_ref, b_ref, *refs):
        ex_in, parts_hbm, ex_out = refs[:ex.n], refs[ex.n], refs[ex.n + 1:2 * ex.n + 1]
        acc_s, stage_s, rx_s, send_sems, recv_sems, loc_sem = refs[2 * ex.n + 1:2 * ex.n + 7]
        ex_sems = refs[2 * ex.n + 7:]
        j, k = pl.program_id(0), pl.program_id(1)
        x, y, c, me = _mesh_place()
        sib = _peer(x, y, c, SIBLING)[0]

        def send(jj):
            mask, src = RS_ORDER[jj], stage_s.at[jj % n_stage]
            if mask == 0:
                return pltpu.make_async_copy(src, parts_hbm.at[me], loc_sem.at[0])
            pair = (send_sems.at[mask], recv_sems.at[mask])
            if mask in ICI_MASKS or mask == SIBLING:
                return _remote(src, parts_hbm.at[me], *pair, _peer(x, y, c, mask)[0])
            return _remote(src, rx_s.at[mask // 2 - 1], *pair, sib)

        def from_sibling(mask):
            return _remote(stage_s.at[0], rx_s.at[mask // 2 - 1], send_sems.at[mask], recv_sems.at[mask], sib)

        @pl.when(jnp.logical_and(j == 0, k == 0))
        def _():
            ex.start(ex_in, ex_out, ex_sems)

        @pl.when(k == 0)
        def _():
            acc_s[...] = jnp.zeros_like(acc_s)

        acc_s[...] += _dot_tn(a_ref[...], b_ref[...])

        for jj in range(N_DEV):
            @pl.when(jnp.logical_and(j == jj, k == n_k - 1))
            def _(jj=jj):
                mask = RS_ORDER[jj]
                if jj >= n_stage:
                    send(jj - n_stage).wait_send()
                if mask in ICI_MASKS:
                    from_sibling(mask + 1).wait_recv()
                    stage_s[jj % n_stage] = (acc_s[...] + rx_s[mask // 2 - 1].astype(F32)).astype(BF16)
                else:
                    stage_s[jj % n_stage] = acc_s[...].astype(BF16)
                send(jj).start()

        @pl.when(jnp.logical_and(j == N_DEV - 1, k == n_k - 1))
        def _():
            for jj in range(N_DEV - n_stage, N_DEV):
                cp = send(jj)
                cp.wait() if RS_ORDER[jj] == 0 else cp.wait_send()
            for mask in DIRECT_MASKS:
                dev, lin = _peer(x, y, c, mask)
                _remote(stage_s.at[0], parts_hbm.at[lin], send_sems.at[mask], recv_sems.at[mask], dev).wait_recv()
            ex.wait(ex_in, ex_out, ex_sems)

    dma = lambda n: pltpu.SemaphoreType.DMA((n,))
    grid_spec = pltpu.PrefetchScalarGridSpec(
        num_scalar_prefetch=1, grid=(N_DEV, n_k),
        in_specs=[pl.BlockSpec((tk, D_MODEL), lambda j, k, order: (k, 0)),
                  pl.BlockSpec((tk, W_IN_SHARD), lambda j, k, order: (k, order[j]))] + [ANY_SPEC] * ex.n,
        out_specs=[ANY_SPEC] * (1 + ex.n),
        scratch_shapes=[pltpu.VMEM((D_MODEL, W_IN_SHARD), F32), pltpu.VMEM((n_stage, D_MODEL, W_IN_SHARD), BF16),
                        pltpu.VMEM((len(ICI_MASKS), D_MODEL, W_IN_SHARD), BF16), dma(N_DEV), dma(N_DEV), dma(1)]
        + ex.scratch)
    res = pl.pallas_call(
        body, name="grad_w_in", grid_spec=grid_spec,
        out_shape=[SDS((N_DEV, D_MODEL, W_IN_SHARD), BF16)] + ex.out_shape,
        compiler_params=_params(("arbitrary", "arbitrary"), 44),
    )(order, hn, dz, *ex_arrs)
    return res[0], slots, res[1:]


RS_CHIPS = (6, 2, 4, 0)


def _grad_w_in_pairs(hn, dz, ex_arrs, ex_scatter, tk=1024):
    t_len = hn.shape[0]
    n_k = t_len // tk
    n_ph = len(RS_CHIPS)
    ex = _Exchange(ex_arrs, ex_scatter)
    me_out = 4 * lax.axis_index("x") + 2 * lax.axis_index("y") + lax.axis_index("c")
    order = jnp.stack([(me_out ^ chip) // 2 for chip in RS_CHIPS]).astype(jnp.int32)
    slots = jnp.stack([me_out ^ k for k in RS_SLOTS]).astype(jnp.int32)
    shard = W_IN_SHARD

    def body(order_ref, a_ref, b_ref, *refs):
        ex_in, parts_hbm, ex_out = refs[:ex.n], refs[ex.n], refs[ex.n + 1:2 * ex.n + 1]
        (acc_s, tb_s, stage_s, rx_s, d2d_send, d2d_recv, ici_send, ici_recv, sib_sems,
         loc_sem) = refs[2 * ex.n + 1:2 * ex.n + 11]
        ex_sems = refs[2 * ex.n + 11:]
        j, k = pl.program_id(0), pl.program_id(1)
        x, y, c, me = _mesh_place()
        sib = _peer(x, y, c, SIBLING)[0]

        def to_sibling(p):
            return _remote(stage_s.at[0], rx_s.at[p % 2], d2d_send.at[p], d2d_recv.at[p], sib)

        def over_ici(p):
            dev = _peer(x, y, c, RS_CHIPS[p])[0]
            return _remote(stage_s.at[1], parts_hbm.at[me], ici_send.at[p], ici_recv.at[p], dev)

        def own_chip():
            return (_remote(stage_s.at[0], parts_hbm.at[me], sib_sems.at[0], sib_sems.at[1], sib),
                    pltpu.make_async_copy(stage_s.at[1], parts_hbm.at[me], loc_sem.at[0]))

        @pl.when(jnp.logical_and(j == 0, k == 0))
        def _():
            ex.start(ex_in, ex_out, ex_sems)

        for p in range(n_ph - 1):
            for core in (0, 1):
                @pl.when(jnp.logical_and(jnp.logical_and(j == p + 1, k == 0), c == core))
                def _(p=p, core=core):
                    to_sibling(p).wait_recv()
                    if p >= 1:
                        over_ici(p - 1).wait_send()
                    mine = acc_s[:, core * shard:(core + 1) * shard]
                    stage_s[1] = (mine + rx_s[p % 2].astype(F32)).astype(BF16)
                    over_ici(p).start()

        @pl.when(k == 0)
        def _():
            acc_s[...] = jnp.zeros_like(acc_s)

        a = a_ref[...]
        acc_s[:, 0:W_BODY] += _dot_tn(a, b_ref[:, 0:W_BODY])
        acc_s[:, shard:shard + W_BODY] += _dot_tn(a, b_ref[:, shard:shard + W_BODY])
        tb_s[:, 0:W_TAIL] = b_ref[:, W_BODY:shard]
        tb_s[:, W_TAIL:2 * W_TAIL] = b_ref[:, shard + W_BODY:2 * shard]
        tails = _dot_tn(a, tb_s[...])
        acc_s[:, W_BODY:shard] += tails[:, 0:W_TAIL]
        acc_s[:, shard + W_BODY:2 * shard] += tails[:, W_TAIL:2 * W_TAIL]

        for p in range(n_ph):
            for core in (0, 1):
                @pl.when(jnp.logical_and(jnp.logical_and(j == p, k == n_k - 1), c == core))
                def _(p=p, core=core):
                    same = acc_s[:, core * shard:(core + 1) * shard]
                    other = acc_s[:, (1 - core) * shard:(2 - core) * shard]
                    if p >= 1:
                        to_sibling(p - 1).wait_send()
                    stage_s[0] = other.astype(BF16)
                    if p < n_ph - 1:
                        to_sibling(p).start()
                    else:
                        over_ici(n_ph - 2).wait_send()
                        stage_s[1] = same.astype(BF16)
                        for cp in own_chip():
                            cp.start()

        @pl.when(jnp.logical_and(j == n_ph - 1, k == n_k - 1))
        def _():
            to_sib, local = own_chip()
            to_sib.wait_send()
            local.wait()
            _remote(stage_s.at[0], parts_hbm.at[_peer(x, y, c, SIBLING)[1]], sib_sems.at[0], sib_sems.at[1], sib).wait_recv()
            for p in range(n_ph - 1):
                dev, lin = _peer(x, y, c, RS_CHIPS[p])
                _remote(stage_s.at[0], parts_hbm.at[lin], ici_send.at[p], ici_recv.at[p], dev).wait_recv()
            ex.wait(ex_in, ex_out, ex_sems)

    dma = lambda n: pltpu.SemaphoreType.DMA((n,))
    grid_spec = pltpu.PrefetchScalarGridSpec(
        num_scalar_prefetch=1, grid=(n_ph, n_k),
        in_specs=[pl.BlockSpec((tk, D_MODEL), lambda j, k, order: (k, 0)),
                  pl.BlockSpec((tk, 2 * shard), lambda j, k, order: (k, order[j]))] + [ANY_SPEC] * ex.n,
        out_specs=[ANY_SPEC] * (1 + ex.n),
        scratch_shapes=[pltpu.VMEM((D_MODEL, 2 * shard), F32), pltpu.VMEM((tk, 2 * W_TAIL), BF16),
                        pltpu.VMEM((2, D_MODEL, shard), BF16),
                        pltpu.VMEM((2, D_MODEL, shard), BF16), dma(n_ph - 1), dma(n_ph - 1), dma(n_ph - 1),
                        dma(n_ph - 1), dma(2), dma(1)] + ex.scratch)
    res = pl.pallas_call(
        body, name="grad_w_in", grid_spec=grid_spec,
        out_shape=[SDS((N_DEV, D_MODEL, shard), BF16)] + ex.out_shape,
        compiler_params=_params(("arbitrary", "arbitrary"), 54),
    )(order, hn, dz, *ex_arrs)
    return res[0], slots, res[1:]


def _sum_parts(parts, name):
    def body(p_ref, o_ref):
        g = p_ref[0].astype(F32)
        for s in range(1, parts.shape[0]):
            g = g + p_ref[s].astype(F32)
        o_ref[...] = g

    return pl.pallas_call(body, name=name, out_shape=SDS(parts.shape[1:], F32))(parts)


def _adamw_math(g, w_ref, m_ref, v_ref, g_ref, d_ref, nm_ref, nv_ref):
    c1 = 1.0 - ADAM_B1 ** ADAM_STEP
    c2 = 1.0 - ADAM_B2 ** ADAM_STEP
    g_ref[...] = g
    nm = ADAM_B1 * m_ref[...] + (1.0 - ADAM_B1) * g
    nv = ADAM_B2 * v_ref[...] + (1.0 - ADAM_B2) * (g * g)
    nm_ref[...] = nm
    nv_ref[...] = nv
    d_ref[...] = -ADAM_LR * ((nm / c1) / (jnp.sqrt(nv / c2) + ADAM_EPS) + ADAM_WD * w_ref[...])


def _adamw(parts, w, m, v, name, tr):
    rows, cols = w.shape
    n_parts = parts.shape[0]

    def body(p_ref, *refs):
        g = p_ref[0].astype(F32)
        for s in range(1, n_parts):
            g = g + p_ref[s].astype(F32)
        _adamw_math(g, *refs)

    tile = pl.BlockSpec((tr, cols), lambda i: (i, 0))
    return pl.pallas_call(
        body, name=name, grid=(rows // tr,),
        in_specs=[pl.BlockSpec((n_parts, tr, cols), lambda i: (0, i, 0)), tile, tile, tile],
        out_specs=[tile] * 4, out_shape=[SDS((rows, cols), F32)] * 4,
        compiler_params=_params(("arbitrary",), 40),
    )(parts, w, m, v)


def _adamw_slots(parts, slots, w, m, v, name, tr):
    rows, cols = w.shape
    n_slots = slots.shape[0]

    def body(slots_ref, *refs):
        g = refs[0][...].astype(F32)
        for s in range(1, n_slots):
            g = g + refs[s][...].astype(F32)
        _adamw_math(g, *refs[n_slots:])

    tile = pl.BlockSpec((tr, cols), lambda i, slots: (i, 0))
    part = lambda s: pl.BlockSpec((None, tr, cols), lambda i, slots: (slots[s], i, 0))
    grid_spec = pltpu.PrefetchScalarGridSpec(
        num_scalar_prefetch=1, grid=(rows // tr,),
        in_specs=[part(s) for s in range(n_slots)] + [tile, tile, tile], out_specs=[tile] * 4)
    return pl.pallas_call(
        body, name=name, grid_spec=grid_spec, out_shape=[SDS((rows, cols), F32)] * 4,
        compiler_params=_params(("arbitrary",), 40),
    )(slots, *([parts] * n_slots), w, m, v)


PACKED = ("gmlp_ln_g", "gmlp_ln_b", "gmlp_ws", "gmlp_bs", "conv_b", "w_a", "b_a", "w_x", "b_x", "lam", "gmlp_out_g",
          "lru_out_g", "post_g")
WEIGHTS = ("pre_g", "w_in", "gmlp_ln_g", "gmlp_ln_b", "gmlp_ws", "gmlp_bs", "conv_w", "conv_b", "w_a", "b_a", "w_x",
           "b_x", "lam", "gmlp_out_g", "lru_out_g", "w_out", "post_g", "w_pe", "w_pg")
LANES = 128


PACK_ROWS = 3200
PACK_TILE = 640
IN_BWD_TILE = 256


def _pack(parts):
    rows = [p.reshape(-1, LANES) for p in parts]
    used = sum(r.shape[0] for r in rows)
    return jnp.concatenate(rows + [jnp.zeros((PACK_ROWS - used, LANES), F32)], axis=0)


def _pad_rows(a, rows):
    return jnp.concatenate([a, jnp.zeros((rows - a.shape[0],) + a.shape[1:], a.dtype)], axis=0)


def kernel(x, p, pre_g, w_in, gmlp_ln_g, gmlp_ln_b, gmlp_ws, gmlp_bs, conv_w, conv_b, w_a, b_a, w_x, b_x, lam, gmlp_out_g, lru_out_g, w_out, post_g, w_pe, w_pg, loss_target, m_pre_g, m_w_in, m_gmlp_ln_g, m_gmlp_ln_b, m_gmlp_ws, m_gmlp_bs, m_conv_w, m_conv_b, m_w_a, m_b_a, m_w_x, m_b_x, m_lam, m_gmlp_out_g, m_lru_out_g, m_w_out, m_post_g, m_w_pe, m_w_pg, v_pre_g, v_w_in, v_gmlp_ln_g, v_gmlp_ln_b, v_gmlp_ws, v_gmlp_bs, v_conv_w, v_conv_b, v_w_a, v_b_a, v_w_x, v_b_x, v_lam, v_gmlp_out_g, v_lru_out_g, v_w_out, v_post_g, v_w_pe, v_w_pg):
    args = dict(locals())
    weights = {n: args[n] for n in WEIGHTS}
    m_in = {n: args["m_" + n] for n in WEIGHTS}
    v_in = {n: args["v_" + n] for n in WEIGHTS}
    sm = {n: weights[n][0] for n in PACKED}
    shard_rows = D_MODEL // N_DEV
    xs, ps, tgt = x[0], p[0, 0], loss_target[0]

    vec = lambda a: a.reshape(1, -1)
    tril = jnp.tril(jnp.ones((CHUNK, CHUNK), dtype=bool))
    wm32 = jnp.where(tril[None], sm["gmlp_ws"], 0.0)
    wm, wm_t = wm32.astype(BF16), jnp.swapaxes(wm32, 1, 2).astype(BF16)
    bias = jnp.repeat(sm["gmlp_bs"].T, HEAD, axis=1)
    wax32 = jnp.concatenate([sm["w_a"], sm["w_x"]], axis=2)
    wax, wax_t = wax32.astype(BF16), jnp.swapaxes(wax32, 1, 2).astype(BF16)
    ln_g, ln_b = vec(sm["gmlp_ln_g"]), vec(sm["gmlp_ln_b"])
    post_g_v = vec(sm["post_g"])

    hn = _pre_norm(xs, pre_g)
    cw_shard = _pad_rows(conv_w.reshape(CONV_W, HEAD), ROWS)
    z, w_in_g, (cw_g,) = _in_proj(hn, w_in[0].astype(BF16), [cw_shard])
    cw_full = jnp.transpose(cw_g[:, :CONV_W, :], (1, 0, 2)).reshape(CONV_W, D_BR)
    mixer_consts = dict(cw=_pad_rows(cw_full, ROWS), cb=vec(sm["conv_b"]), ba=vec(sm["b_a"]), bx=vec(sm["b_x"]),
                        lam=vec(sm["lam"]), goa=vec(sm["gmlp_out_g"]), gob=vec(sm["lru_out_g"]))
    (y, h, vhb, xcb, v_rs), (w_out_g, w_pe_g, w_pg_g) = _mix_fwd(
        z, ln_g, ln_b, wm, bias, wax=wax, **mixer_consts,
        ex_arrs=[w_out[0].astype(BF16), w_pe[0].astype(BF16), w_pg[0].astype(BF16)], ex_scatter=[False, False, False])
    w_out_f, w_pg_f = w_out_g.reshape(D_MODEL, D_MODEL), w_pg_g.reshape(D_MODEL, D_MODEL)
    h1, ob = _out_proj(y, xs, w_out_f, post_g_v)
    dh2, dgl, h1b, loss_part, d_w_pe = _ple_loss(h1, ps, tgt, w_pg_f, w_pe_g)

    dh1, do, dy, d_post_g = _tail_bwd(dh2, dgl, ob, w_pg_f, w_out_f, post_g_v)
    d_w_out, _ = _grad_w(y, do, 1024, False, "grad_w_out")
    d_w_pg, _ = _grad_w(h1b, dgl, 1024, False, "grad_w_pg")
    (dz, vecs, d_ws, d_wax, d_bs), (parts_out, parts_pg, parts_pe) = _mix_bwd(
        z, dy, h, vhb, xcb, v_rs, ln_g, ln_b, wm, wm_t, bias, wax=wax, wax_t=wax_t, **mixer_consts,
        ex_arrs=[d_w_out.reshape(N_DEV, shard_rows, D_MODEL), d_w_pg.reshape(N_DEV, shard_rows, D_MODEL), d_w_pe],
        ex_scatter=[True, True, True])

    small = {"gmlp_ln_g": vecs[V_LN_G], "gmlp_ln_b": vecs[V_LN_B], "gmlp_ws": d_ws, "gmlp_bs": d_bs,
             "conv_b": vecs[V_CONV_B], "w_a": d_wax[:, :, :HEAD], "b_a": vecs[V_B_A], "w_x": d_wax[:, :, HEAD:],
             "b_x": vecs[V_B_X], "lam": vecs[V_LAM], "gmlp_out_g": vecs[V_GOUT_A], "lru_out_g": vecs[V_GOUT_B],
             "post_g": d_post_g}
    small_part = _pack([small[n] for n in PACKED] + [loss_part]).reshape(N_DEV, PACK_ROWS // N_DEV, LANES)
    d_cw_blocks = jnp.transpose(vecs[V_CONV_W:V_CONV_W + CONV_W].reshape(CONV_W, N_DEV, HEAD), (1, 0, 2))
    d_cw_blocks = jnp.concatenate([d_cw_blocks, jnp.zeros((N_DEV, ROWS - CONV_W, HEAD), F32)], axis=1)
    parts_in, slots_in, (small_blocks, parts_cw) = _grad_w_in_pairs(
        hn, dz, ex_arrs=[small_part, d_cw_blocks], ex_scatter=[True, True])
    small_sum = _sum_parts(small_blocks, "sum_small")
    grad_x, d_pre_g, _ = _in_bwd(dz, w_in_g, xs, dh1, pre_g, 0, xs.shape[0] // IN_BWD_TILE, None, "in_bwd",
                                 tm=IN_BWD_TILE)
    pre_rows = D_MODEL // LANES
    small_all, parts_pre = _exchange([small_sum, d_pre_g.reshape(pre_rows, LANES)], False, "gather_small_grads")
    parts_small = small_all.reshape(1, PACK_ROWS, LANES)

    pad_cw = lambda a: _pad_rows(a.reshape(CONV_W, HEAD), ROWS)
    flat = lambda a: a.reshape(pre_rows, LANES)
    outs = {
        "w_in": _adamw_slots(parts_in, slots_in, w_in[0], m_w_in[0], v_w_in[0], "adamw_w_in", 256),
        "w_out": _adamw(parts_out, w_out[0], m_w_out[0], v_w_out[0], "adamw_w_out", 128),
        "w_pe": _adamw(parts_pe, w_pe[0], m_w_pe[0], v_w_pe[0], "adamw_w_pe", 256),
        "w_pg": _adamw(parts_pg, w_pg[0], m_w_pg[0], v_w_pg[0], "adamw_w_pg", 128),
        "conv_w": [a[:CONV_W] for a in
                   _adamw(parts_cw, pad_cw(conv_w), pad_cw(m_conv_w), pad_cw(v_conv_w), "adamw_conv_w", ROWS)],
        "pre_g": _adamw(parts_pre, flat(pre_g), flat(m_pre_g), flat(v_pre_g), "adamw_pre_g", pre_rows),
    }
    packed = _adamw(parts_small, _pack([weights[n] for n in PACKED]), _pack([m_in[n] for n in PACKED]),
                    _pack([v_in[n] for n in PACKED]), "adamw_small", PACK_TILE)
    row = 0
    for n in PACKED:
        n_rows = weights[n].size // LANES
        outs[n] = [packed[q][row:row + n_rows] for q in range(4)]
        row += n_rows
    loss = packed[0][row, 0]

    result = [loss, grad_x[None]]
    for q in range(4):
        result += [outs[n][q].reshape(weights[n].shape) for n in WEIGHTS]
    return tuple(result)
```

```python
import jax
import jax.numpy as jnp
from jax import lax
from jax.experimental import pallas as pl
from jax.experimental.pallas import tpu as pltpu

F32 = jnp.float32
BF16 = jnp.bfloat16
SDS = jax.ShapeDtypeStruct

D_MODEL = 2048
D_BR = 1024
D_IN = 5 * D_BR
D_PLE = 256
N_HEAD = 8
HEAD = 128
CHUNK = 128
ROWS = 8
N_GROUP = CHUNK // ROWS
N_DEV = 8
W_IN_SHARD = D_IN // N_DEV
EPS = 1e-6
LRU_C = 8.0
CONV_W = 4
MIB = 1 << 20

ADAM_LR, ADAM_B1, ADAM_B2, ADAM_EPS, ADAM_WD, ADAM_STEP = 0.001, 0.9, 0.999, 1e-08, 0.01, 10

_GELU_C = 0.7978845608028654
_GELU_A = 0.044715

V_LN_G, V_LN_B, V_CONV_B, V_B_A, V_B_X, V_LAM, V_GOUT_A, V_GOUT_B, V_CONV_W = 0, 1, 2, 3, 4, 5, 6, 7, 8
N_VEC = 16


def _params(sem, vmem_mib):
    return pltpu.CompilerParams(dimension_semantics=sem, vmem_limit_bytes=int(vmem_mib * MIB))


def _sig(x):
    return 0.5 * jnp.tanh(0.5 * x) + 0.5


def _gelu(x):
    t = jnp.tanh(_GELU_C * (x + _GELU_A * x * x * x))
    return 0.5 * x * (1.0 + t), t


def _gelu_grad(x, t):
    return 0.5 * (1.0 + t) + 0.5 * x * (1.0 - t * t) * (_GELU_C * (1.0 + 3.0 * _GELU_A * x * x))


def _neg_expm1(y, exp_y):
    series = -y * (1.0 + y * (0.5 + y * (1.0 / 6.0)))
    return jnp.where(y > -0.01, series, 1.0 - exp_y)


def _softplus(x):
    return jnp.maximum(x, 0.0) + jnp.log(1.0 + jnp.exp(-jnp.abs(x)))


def _row_ids(width):
    return lax.broadcasted_iota(jnp.int32, (ROWS, width), 0)


def _shift_down(cur, prev, k, rid):
    return jnp.where(rid >= k, pltpu.roll(cur, k, 0), pltpu.roll(prev, k, 0))


def _shift_up(cur, nxt, k, rid):
    return jnp.where(rid < ROWS - k, pltpu.roll(cur, ROWS - k, 0), pltpu.roll(nxt, ROWS - k, 0))


def _mean_last(x):
    return jnp.mean(x, axis=-1, keepdims=True)


def _rows(g):
    return pl.ds(pl.multiple_of(g * ROWS, ROWS), ROWS)


TILE_ROWS = 16


def _tile_rows(q):
    return pl.ds(pl.multiple_of(q * TILE_ROWS, TILE_ROWS), TILE_ROWS)


UNROLL = 4
TILE_UNROLL = 8


def _loop(n, body, init, unroll=UNROLL):
    def wide(i, carry):
        for u in range(unroll):
            carry = body(i * unroll + u, carry)
        return carry

    return lax.fori_loop(0, n // unroll, wide, init)


def _fold_rows(x):
    return x[0:ROWS, :] + x[ROWS:TILE_ROWS, :]


def _bcast_row(x, r):
    return jnp.broadcast_to(x[r:r + 1, :], x.shape)


def _dot(a, b):
    return jnp.dot(a, b, preferred_element_type=F32)


def _dot_nt(a, b):
    return lax.dot_general(a, b, (((1,), (1,)), ((), ())), preferred_element_type=F32)


def _dot_tn(a, b):
    return lax.dot_general(a, b, (((0,), (0,)), ((), ())), preferred_element_type=F32)


def _mesh_place():
    x, y, c = lax.axis_index("x"), lax.axis_index("y"), lax.axis_index("c")
    return x, y, c, 4 * x + 2 * y + c


def _peer(x, y, c, k):
    px = 1 - x if k & 4 else x
    py = 1 - y if k & 2 else y
    pc = 1 - c if k & 1 else c
    return (px, py, pc), 4 * px + 2 * py + pc


def _remote(src, dst, send_sem, recv_sem, dev):
    return pltpu.make_async_remote_copy(src_ref=src, dst_ref=dst, send_sem=send_sem, recv_sem=recv_sem, device_id=dev,
                                        device_id_type=pl.DeviceIdType.MESH)


ANY_SPEC = pl.BlockSpec(memory_space=pl.ANY)


class _Exchange:
    def __init__(self, arrs, scatter):
        self.n = len(arrs)
        self.scatter = tuple(scatter)
        self.out_shape = [SDS(a.shape if s else (N_DEV,) + a.shape, a.dtype) for a, s in zip(arrs, scatter)]
        self.scratch = [pltpu.SemaphoreType.DMA((self.n * N_DEV,)), pltpu.SemaphoreType.DMA((self.n * N_DEV,)),
                        pltpu.SemaphoreType.DMA((self.n,))]

    def _copies(self, ins, outs, sems):
        send_sems, recv_sems, local_sems = sems
        x, y, c, me = _mesh_place()
        local, sends, recvs = [], [], []
        for a in range(self.n):
            src = ins[a].at[me] if self.scatter[a] else ins[a]
            local.append(pltpu.make_async_copy(src, outs[a].at[me], local_sems.at[a]))
        for k in range(1, N_DEV):
            dev, lin = _peer(x, y, c, k)
            for a in range(self.n):
                src = ins[a].at[lin] if self.scatter[a] else ins[a]
                pair = (send_sems.at[a * N_DEV + k], recv_sems.at[a * N_DEV + k], dev)
                sends.append(_remote(src, outs[a].at[me], *pair))
                recvs.append(_remote(src, outs[a].at[lin], *pair))
        return local, sends, recvs

    def start(self, ins, outs, sems):
        local, sends, _ = self._copies(ins, outs, sems)
        for cp in local + sends:
            cp.start()

    def wait(self, ins, outs, sems):
        local, sends, recvs = self._copies(ins, outs, sems)
        for cp in recvs:
            cp.wait_recv()
        for cp in sends:
            cp.wait_send()
        for cp in local:
            cp.wait()


def _exchange(arrs, scatter, name):
    ex = _Exchange(arrs, [scatter] * len(arrs))
    n = ex.n

    def body(*refs):
        ins, outs, sems = refs[:n], refs[n:2 * n], refs[2 * n:]
        ex.start(ins, outs, sems)
        ex.wait(ins, outs, sems)

    return pl.pallas_call(
        body, name=name, out_shape=ex.out_shape, in_specs=[ANY_SPEC] * n, out_specs=[ANY_SPEC] * n,
        scratch_shapes=ex.scratch,
    )(*arrs)


def _pre_norm(x, pre_g, tm=512):
    t_len = x.shape[0]

    def body(x_ref, g_ref, hn_ref):
        g = g_ref[...]

        def rows_body(q, _):
            rows = _tile_rows(q)
            xv = x_ref[rows, :]
            hn_ref[rows, :] = (xv * lax.rsqrt(_mean_last(xv * xv) + EPS) * g).astype(BF16)
            return 0

        _loop(tm // TILE_ROWS, rows_body, 0, unroll=TILE_UNROLL)

    tile = pl.BlockSpec((tm, D_MODEL), lambda i: (i, 0))
    return pl.pallas_call(
        body, name="pre_norm", grid=(t_len // tm,),
        in_specs=[tile, pl.BlockSpec((1, D_MODEL), lambda i: (0, 0))], out_specs=tile,
        out_shape=SDS((t_len, D_MODEL), BF16),
        compiler_params=_params(("arbitrary",), 24),
    )(x, pre_g)


CHIP_ORDER = (0, 2, 4, 6)
W_BODY, W_TAIL = 512, 128
SIBLING = 1
ICI_MASKS = (2, 4, 6)
DIRECT_MASKS = (SIBLING,) + ICI_MASKS
Y_NEIGHBOUR, X_NEIGHBOUR, DIAGONAL = 2, 4, 6
W_DIRECT = (SIBLING, Y_NEIGHBOUR, X_NEIGHBOUR)


def _in_proj(hn, w_shard, others, tm=1024):
    t_len = hn.shape[0]
    n_i = t_len // tm
    n_o = len(others)
    me_out = 4 * lax.axis_index("x") + 2 * lax.axis_index("y") + lax.axis_index("c")
    order = jnp.stack([(me_out ^ chip) // 2 for chip in CHIP_ORDER]).astype(jnp.int32)

    def body(order_ref, hn_ref, w_hbm, *refs):
        o_in = refs[:n_o]
        z_ref, wg_hbm = refs[n_o], refs[n_o + 1]
        o_out = refs[n_o + 2:2 * n_o + 2]
        (wbuf, tail_s, send_w, recv_w, fsend_w, frecv_w, send_o, recv_o, fsend_o, frecv_o, wb_sems, loc_sems, rsend,
         rrecv) = refs[2 * n_o + 2:]
        j, i = pl.program_id(0), pl.program_id(1)
        x, y, c, me = _mesh_place()
        sib = _peer(x, y, c, SIBLING)[0]

        def relay(core):
            src, dst = (Y_NEIGHBOUR, X_NEIGHBOUR) if core == 0 else (X_NEIGHBOUR, Y_NEIGHBOUR)
            held, diag = _peer(x, y, c, src)[1], _peer(x, y, c, DIAGONAL)[1]
            pair = (rsend.at[0], rrecv.at[0], _peer(x, y, c, dst)[0])
            return _remote(wbuf.at[held], wbuf.at[held], *pair), _remote(wbuf.at[diag], wbuf.at[diag], *pair)

        def direct(k, a=None):
            dev, lin = _peer(x, y, c, k)
            if a is None:
                return (_remote(w_hbm, wbuf.at[me], send_w.at[k], recv_w.at[k], dev),
                        _remote(w_hbm, wbuf.at[lin], send_w.at[k], recv_w.at[k], dev))
            pair = (send_o.at[a * N_DEV + k], recv_o.at[a * N_DEV + k], dev)
            return _remote(o_in[a], o_out[a].at[me], *pair), _remote(o_in[a], o_out[a].at[lin], *pair)

        def passed(k, a=None):
            mine, theirs = _peer(x, y, c, k)[1], _peer(x, y, c, k ^ SIBLING)[1]
            if a is None:
                pair = (fsend_w.at[k], frecv_w.at[k], sib)
                return _remote(wbuf.at[mine], wbuf.at[mine], *pair), _remote(wbuf.at[theirs], wbuf.at[theirs], *pair)
            pair = (fsend_o.at[a * N_DEV + k], frecv_o.at[a * N_DEV + k], sib)
            return (_remote(o_out[a].at[mine], o_out[a].at[mine], *pair),
                    _remote(o_out[a].at[theirs], o_out[a].at[theirs], *pair))

        def own_copies():
            return [pltpu.make_async_copy(o_in[a], o_out[a].at[me], loc_sems.at[1 + a]) for a in range(n_o)]

        @pl.when(jnp.logical_and(j == 0, i == 0))
        def _():
            own = pltpu.make_async_copy(w_hbm, wbuf.at[me], loc_sems.at[0])
            own.start()
            for cp in own_copies():
                cp.start()
            for k in W_DIRECT:
                direct(k)[0].start()
            for k in DIRECT_MASKS:
                for a in range(n_o):
                    direct(k, a)[0].start()
            own.wait()

        low = 2 * order_ref[j]

        for jp, chip in enumerate(CHIP_ORDER):
            @pl.when(jnp.logical_and(j == jp, i == 0))
            def _(jp=jp, chip=chip):
                if chip == 0:
                    direct(SIBLING)[1].wait_recv()
                elif chip == Y_NEIGHBOUR:
                    for mask in (Y_NEIGHBOUR, X_NEIGHBOUR):
                        direct(mask)[1].wait_recv()
                        passed(mask)[0].start()
                    for core in (0, 1):
                        @pl.when(c == core)
                        def _(core=core):
                            relay(core)[0].start()
                    passed(Y_NEIGHBOUR)[1].wait_recv()
                elif chip == X_NEIGHBOUR:
                    passed(X_NEIGHBOUR)[1].wait_recv()
                    for core in (0, 1):
                        @pl.when(c == core)
                        def _(core=core):
                            relay(core)[1].wait_recv()
                    passed(DIAGONAL)[0].start()
                    for k in ICI_MASKS:
                        for a in range(n_o):
                            direct(k, a)[1].wait_recv()
                            passed(k, a)[0].start()
                else:
                    passed(DIAGONAL)[1].wait_recv()
                for half in (0, 1):
                    pltpu.make_async_copy(wbuf.at[low + half], wg_hbm.at[low + half], wb_sems.at[2 * jp + half]).start()
                tail_s[:, 0:W_TAIL] = wbuf[low, :, W_BODY:W_IN_SHARD]
                tail_s[:, W_TAIL:2 * W_TAIL] = wbuf[low + 1, :, W_BODY:W_IN_SHARD]

        hn = hn_ref[...]
        z_ref[:, 0:W_BODY] = _dot(hn, wbuf[low, :, 0:W_BODY])
        z_ref[:, W_IN_SHARD:W_IN_SHARD + W_BODY] = _dot(hn, wbuf[low + 1, :, 0:W_BODY])
        tails = _dot(hn, tail_s[...])
        z_ref[:, W_BODY:W_IN_SHARD] = tails[:, 0:W_TAIL]
        z_ref[:, W_IN_SHARD + W_BODY:2 * W_IN_SHARD] = tails[:, W_TAIL:2 * W_TAIL]

        @pl.when(jnp.logical_and(j == len(CHIP_ORDER) - 1, i == n_i - 1))
        def _():
            for a in range(n_o):
                direct(SIBLING, a)[1].wait_recv()
            for k in ICI_MASKS:
                for a in range(n_o):
                    passed(k, a)[1].wait_recv()
            for k in W_DIRECT:
                direct(k)[0].wait_send()
            for core in (0, 1):
                @pl.when(c == core)
                def _(core=core):
                    relay(core)[0].wait_send()
            for k in DIRECT_MASKS:
                for a in range(n_o):
                    direct(k, a)[0].wait_send()
            for k in ICI_MASKS:
                passed(k)[0].wait_send()
                for a in range(n_o):
                    passed(k, a)[0].wait_send()
            for cp in own_copies():
                cp.wait()
            for jj in range(N_DEV):
                pltpu.make_async_copy(wbuf.at[0], wg_hbm.at[0], wb_sems.at[jj]).wait()

    dma = lambda n: pltpu.SemaphoreType.DMA((n,))
    grid_spec = pltpu.PrefetchScalarGridSpec(
        num_scalar_prefetch=1, grid=(len(CHIP_ORDER), n_i),
        in_specs=[pl.BlockSpec((tm, D_MODEL), lambda j, i, order: (i, 0)), ANY_SPEC] + [ANY_SPEC] * n_o,
        out_specs=[pl.BlockSpec((tm, 2 * W_IN_SHARD), lambda j, i, order: (i, order[j])), ANY_SPEC] + [ANY_SPEC] * n_o,
        scratch_shapes=[pltpu.VMEM((N_DEV, D_MODEL, W_IN_SHARD), BF16), pltpu.VMEM((D_MODEL, 2 * W_TAIL), BF16),
                        dma(N_DEV), dma(N_DEV), dma(N_DEV), dma(N_DEV),
                        dma(n_o * N_DEV), dma(n_o * N_DEV), dma(n_o * N_DEV), dma(n_o * N_DEV), dma(N_DEV), dma(1 + n_o),
                        dma(1), dma(1)])
    res = pl.pallas_call(
        body, name="in_proj", grid_spec=grid_spec,
        out_shape=[SDS((t_len, D_IN), F32), SDS((N_DEV, D_MODEL, W_IN_SHARD), BF16)]
        + [SDS((N_DEV,) + o.shape, o.dtype) for o in others],
        compiler_params=_params(("arbitrary", "arbitrary"), 54),
    )(order, hn, w_shard, *others)
    return res[0], res[1], res[2:]


def _conv_rows(cur, prev, cw_ref, cb, rid):
    acc = cw_ref[3:4, :] * cur + cb
    for k in range(1, CONV_W):
        acc = acc + cw_ref[3 - k:4 - k, :] * _shift_down(cur, prev, k, rid)
    return acc


def _lru_gates(pa, px, ba, bx, sp8, first_row):
    r = _sig(pa + ba)
    i = _sig(px + bx)
    la = -(r * sp8)
    a = jnp.exp(la)
    mult = jnp.where(first_row, 1.0, jnp.sqrt(_neg_expm1(2.0 * la, a * a)))
    return r, i, a, mult


def _mix_fwd(z, ln_g, ln_b, wm, bias, cw, cb, wax, ba, bx, lam, goa, gob, ex_arrs, ex_scatter):
    t_len = z.shape[0]
    n_chunk = t_len // CHUNK
    ex = _Exchange(ex_arrs, ex_scatter)
    n_in, n_out, n_scratch = 13, 5, 7

    def body(*refs):
        (z_ref, lng_ref, lnb_ref, wm_ref, bias_ref, cw_ref, cb_ref, wax_ref, ba_ref, bx_ref, lam_ref, goa_ref,
         gob_ref) = refs[:n_in]
        ex_in = refs[n_in:n_in + ex.n]
        y_ref, h_ref, vhb_ref, xcb_ref, rs_ref = refs[n_in + ex.n:n_in + ex.n + n_out]
        ex_out = refs[n_in + ex.n + n_out:n_in + 2 * ex.n + n_out]
        vn_s, xc_s, mixed_s, pre_s, y_s, carry_s, halo_s = refs[n_in + 2 * ex.n + n_out:n_in + 2 * ex.n + n_out + n_scratch]
        ex_sems = refs[n_in + 2 * ex.n + n_out + n_scratch:]
        c_id = pl.program_id(0)
        rid = _row_ids(D_BR)

        @pl.when(c_id == 0)
        def _():
            ex.start(ex_in, ex_out, ex_sems)
            carry_s[...] = jnp.zeros_like(carry_s)
            halo_s[...] = jnp.zeros_like(halo_s)

        lng, lnb, cb = lng_ref[...], lnb_ref[...], cb_ref[...]

        def phase1(g, prev):
            rows = _rows(g)
            vg, _ = _gelu(z_ref[rows, D_BR:2 * D_BR])
            xm = vg - _mean_last(vg)
            rs = lax.rsqrt(_mean_last(xm * xm) + EPS)
            vn_s[rows, :] = xm * rs
            rs_ref[rows, :] = jnp.broadcast_to(rs, (ROWS, HEAD))
            xb = z_ref[rows, 3 * D_BR:4 * D_BR]
            xc_s[rows, :] = _conv_rows(xb, prev, cw_ref, cb, rid)
            return xb

        halo_s[...] = _loop(N_GROUP, phase1, halo_s[...], unroll=8)
        vhb_ref[...] = vn_s[...].astype(BF16)
        xcb_ref[...] = xc_s[...].astype(BF16)

        for h in range(N_HEAD):
            cs = slice(h * HEAD, (h + 1) * HEAD)
            mixed_s[:, cs] = _dot(wm_ref[h], (vn_s[:, cs] * lng[:, cs] + lnb[:, cs]).astype(BF16))
            pre = _dot(xcb_ref[:, cs], wax_ref[h])
            pre_s[:, cs] = pre[:, :HEAD]
            pre_s[:, D_BR + h * HEAD:D_BR + (h + 1) * HEAD] = pre[:, HEAD:]

        ba, bx, goa, gob = ba_ref[...], bx_ref[...], goa_ref[...], gob_ref[...]
        sp8 = LRU_C * _softplus(-lam_ref[...])

        def phase3(g, carry):
            rows = _rows(g)
            ug, _ = _gelu(z_ref[rows, 0:D_BR])
            ga = z_ref[rows, 2 * D_BR:3 * D_BR]
            ya = ug * (mixed_s[rows, :] + bias_ref[rows, :]) * (ga * _sig(ga))
            y_s[rows, 0:D_BR] = ya * lax.rsqrt(_mean_last(ya * ya) + EPS) * goa

            first_row = jnp.logical_and(jnp.logical_and(c_id == 0, g == 0), rid == 0)
            _, i, a, mult = _lru_gates(pre_s[rows, 0:D_BR], pre_s[rows, D_BR:2 * D_BR], ba, bx, sp8, first_row)
            b = mult * i * xc_s[rows, :]
            for d in (1, 2, 4):
                a_sh = jnp.where(rid >= d, pltpu.roll(a, d, 0), 1.0)
                b_sh = jnp.where(rid >= d, pltpu.roll(b, d, 0), 0.0)
                b = a * b_sh + b
                a = a * a_sh
            hh = b + a * carry
            h_ref[rows, :] = hh
            gb = z_ref[rows, 4 * D_BR:5 * D_BR]
            yb = hh * (gb * _sig(gb))
            y_s[rows, D_BR:2 * D_BR] = yb * lax.rsqrt(_mean_last(yb * yb) + EPS) * gob
            return _bcast_row(hh, ROWS - 1)

        carry_s[...] = _loop(N_GROUP, phase3, carry_s[...])
        y_ref[...] = y_s[...].astype(BF16)

        @pl.when(c_id == n_chunk - 1)
        def _():
            ex.wait(ex_in, ex_out, ex_sems)

    vec = pl.BlockSpec((1, D_BR), lambda i: (0, 0))
    res = pl.pallas_call(
        body, name="mix_fwd", grid=(n_chunk,),
        in_specs=[pl.BlockSpec((CHUNK, D_IN), lambda i: (i, 0)), vec, vec,
                  pl.BlockSpec((N_HEAD, HEAD, HEAD), lambda i: (0, 0, 0)),
                  pl.BlockSpec((CHUNK, D_BR), lambda i: (0, 0)),
                  pl.BlockSpec((ROWS, D_BR), lambda i: (0, 0)), vec,
                  pl.BlockSpec((N_HEAD, HEAD, 2 * HEAD), lambda i: (0, 0, 0)), vec, vec, vec, vec, vec]
        + [ANY_SPEC] * ex.n,
        out_specs=[pl.BlockSpec((CHUNK, 2 * D_BR), lambda i: (i, 0)), pl.BlockSpec((CHUNK, D_BR), lambda i: (i, 0)),
                   pl.BlockSpec((CHUNK, D_BR), lambda i: (i, 0)), pl.BlockSpec((CHUNK, D_BR), lambda i: (i, 0)),
                   pl.BlockSpec((CHUNK, HEAD), lambda i: (i, 0))] + [ANY_SPEC] * ex.n,
        out_shape=[SDS((t_len, 2 * D_BR), BF16), SDS((t_len, D_BR), F32), SDS((t_len, D_BR), BF16),
                   SDS((t_len, D_BR), BF16), SDS((t_len, HEAD), F32)] + ex.out_shape,
        scratch_shapes=[pltpu.VMEM((CHUNK, D_BR), F32), pltpu.VMEM((CHUNK, D_BR), F32), pltpu.VMEM((CHUNK, D_BR), F32),
                        pltpu.VMEM((CHUNK, 2 * D_BR), F32), pltpu.VMEM((CHUNK, 2 * D_BR), F32),
                        pltpu.VMEM((ROWS, D_BR), F32), pltpu.VMEM((ROWS, D_BR), F32)] + ex.scratch,
        compiler_params=_params(("arbitrary",), 32),
    )(z, ln_g, ln_b, wm, bias, cw, cb, wax, ba, bx, lam, goa, gob, *ex_arrs)
    return res[:n_out], res[n_out:]


def _load_weight(w_hbm, w_vmem, sem):
    @pl.when(pl.program_id(0) == 0)
    def _():
        cp = pltpu.make_async_copy(w_hbm, w_vmem, sem)
        cp.start()
        cp.wait()


def _out_proj(y, x, w_out, post_g, tm=512):
    t_len = y.shape[0]

    def body(y_ref, x_ref, w_hbm, g_ref, h1_ref, ob_ref, w_s, o_s, sem):
        _load_weight(w_hbm, w_s, sem)
        o_s[...] = _dot(y_ref[...], w_s[...])
        g = g_ref[...]

        def rows_body(q, _):
            rows = _tile_rows(q)
            o = o_s[rows, :]
            h1_ref[rows, :] = x_ref[rows, :] + o * lax.rsqrt(_mean_last(o * o) + EPS) * g
            ob_ref[rows, :] = o.astype(BF16)
            return 0

        _loop(tm // TILE_ROWS, rows_body, 0, unroll=TILE_UNROLL)

    tile = pl.BlockSpec((tm, D_MODEL), lambda i: (i, 0))
    return pl.pallas_call(
        body, name="out_proj", grid=(t_len // tm,),
        in_specs=[tile, tile, pl.BlockSpec(memory_space=pl.ANY), pl.BlockSpec((1, D_MODEL), lambda i: (0, 0))],
        out_specs=[tile, tile],
        out_shape=[SDS((t_len, D_MODEL), F32), SDS((t_len, D_MODEL), BF16)],
        scratch_shapes=[pltpu.VMEM((D_MODEL, D_MODEL), BF16), pltpu.VMEM((tm, D_MODEL), F32), pltpu.SemaphoreType.DMA],
        compiler_params=_params(("arbitrary",), 44),
    )(y, x, w_out, post_g)


def _ple_loss(h1, p, tgt, w_pg, w_pe_g, tm=256):
    t_len = h1.shape[0]
    n_tile = t_len // tm
    pe_shard = D_MODEL // N_DEV

    def body(h1_ref, p_ref, t_ref, w_hbm, wpe_ref, dh2_ref, dgl_ref, h1b_ref, loss_ref, dwpe_ref, w_s, pe_s, gl_s, acc_s,
             dpe_s, gpe_s, sem):
        _load_weight(w_hbm, w_s, sem)
        i = pl.program_id(0)

        @pl.when(i == 0)
        def _():
            acc_s[...] = jnp.zeros_like(acc_s)
            gpe_s[...] = jnp.zeros_like(gpe_s)

        h1b_ref[...] = h1_ref[...].astype(BF16)
        pb = p_ref[...].astype(BF16)
        for j in range(N_DEV):
            pe_s[:, j * pe_shard:(j + 1) * pe_shard] = _dot(pb, wpe_ref[j])
        gl_s[...] = _dot(h1b_ref[...], w_s[...])

        def rows_body(q, acc):
            rows = _tile_rows(q)
            pe = pe_s[rows, :]
            g = _sig(gl_s[rows, :])
            e = h1_ref[rows, :] + pe * g - t_ref[rows, :]
            dh2 = e * (1.0 / D_MODEL)
            dh2_ref[rows, :] = dh2
            dpe_s[rows, :] = (dh2 * g).astype(BF16)
            dgl_ref[rows, :] = (dh2 * pe * g * (1.0 - g)).astype(BF16)
            return acc + _fold_rows(e * e)

        acc_s[...] = _loop(tm // TILE_ROWS, rows_body, acc_s[...], unroll=TILE_UNROLL)
        gpe_s[...] += _dot_tn(pb, dpe_s[...])

        @pl.when(i == n_tile - 1)
        def _():
            loss_ref[...] = jnp.full(loss_ref.shape, 0.5 / D_MODEL * jnp.sum(acc_s[...]), F32)
            for j in range(N_DEV):
                dwpe_ref[j] = gpe_s[:, j * pe_shard:(j + 1) * pe_shard].astype(BF16)

    tile = pl.BlockSpec((tm, D_MODEL), lambda i: (i, 0))
    pe_blocks = pl.BlockSpec((N_DEV, D_PLE, pe_shard), lambda i: (0, 0, 0))
    return pl.pallas_call(
        body, name="ple_loss", grid=(n_tile,),
        in_specs=[tile, pl.BlockSpec((tm, D_PLE), lambda i: (i, 0)), tile, pl.BlockSpec(memory_space=pl.ANY), pe_blocks],
        out_specs=[tile, tile, tile, pl.BlockSpec((ROWS, HEAD), lambda i: (0, 0)), pe_blocks],
        out_shape=[SDS((t_len, D_MODEL), F32), SDS((t_len, D_MODEL), BF16), SDS((t_len, D_MODEL), BF16),
                   SDS((ROWS, HEAD), F32), SDS((N_DEV, D_PLE, pe_shard), BF16)],
        scratch_shapes=[pltpu.VMEM((D_MODEL, D_MODEL), BF16), pltpu.VMEM((tm, D_MODEL), F32),
                        pltpu.VMEM((tm, D_MODEL), F32), pltpu.VMEM((ROWS, D_MODEL), F32), pltpu.VMEM((tm, D_MODEL), BF16),
                        pltpu.VMEM((D_PLE, D_MODEL), F32), pltpu.SemaphoreType.DMA],
        compiler_params=_params(("arbitrary",), 48),
    )(h1, p, tgt, w_pg, w_pe_g)


def _tail_bwd(dh2, dgl, ob, w_pg, w_out, post_g, tm=256):
    t_len = dh2.shape[0]
    n_tile = t_len // tm

    def body(dh2_ref, dgl_ref, ob_ref, wpg_hbm, wout_hbm, g_ref, dh1_ref, do_ref, dy_ref, dg_ref, wpg_s, wout_s, t_s,
             acc_s, sems):
        _load_weight(wpg_hbm, wpg_s, sems.at[0])
        _load_weight(wout_hbm, wout_s, sems.at[1])
        i = pl.program_id(0)

        @pl.when(i == 0)
        def _():
            acc_s[...] = jnp.zeros_like(acc_s)

        t_s[...] = _dot_nt(dgl_ref[...], wpg_s[...])
        g = g_ref[...]

        def rows_body(q, acc):
            rows = _tile_rows(q)
            dh1 = dh2_ref[rows, :] + t_s[rows, :]
            dh1_ref[rows, :] = dh1
            o = ob_ref[rows, :].astype(F32)
            rr = lax.rsqrt(_mean_last(o * o) + EPS)
            on = o * rr
            dog = dh1 * g
            do_ref[rows, :] = (rr * (dog - on * _mean_last(dog * on))).astype(BF16)
            return acc + _fold_rows(dh1 * on)

        acc_s[...] = _loop(tm // TILE_ROWS, rows_body, acc_s[...], unroll=TILE_UNROLL)
        dy_ref[...] = _dot_nt(do_ref[...], wout_s[...]).astype(BF16)

        @pl.when(i == n_tile - 1)
        def _():
            dg_ref[...] = jnp.sum(acc_s[...], axis=0, keepdims=True)

    tile = pl.BlockSpec((tm, D_MODEL), lambda i: (i, 0))
    vec = pl.BlockSpec((1, D_MODEL), lambda i: (0, 0))
    hbm = pl.BlockSpec(memory_space=pl.ANY)
    return pl.pallas_call(
        body, name="tail_bwd", grid=(n_tile,),
        in_specs=[tile, tile, tile, hbm, hbm, vec],
        out_specs=[tile, tile, tile, vec],
        out_shape=[SDS((t_len, D_MODEL), F32), SDS((t_len, D_MODEL), BF16), SDS((t_len, D_MODEL), BF16),
                   SDS((1, D_MODEL), F32)],
        scratch_shapes=[pltpu.VMEM((D_MODEL, D_MODEL), BF16), pltpu.VMEM((D_MODEL, D_MODEL), BF16),
                        pltpu.VMEM((tm, D_MODEL), F32), pltpu.VMEM((ROWS, D_MODEL), F32), pltpu.SemaphoreType.DMA((2,))],
        compiler_params=_params(("arbitrary",), 48),
    )(dh2, dgl, ob, w_pg, w_out, post_g)


def _mix_bwd(z, dy, h, vhb, xcb, rs, ln_g, ln_b, wm, wm_t, bias, cw, cb, wax, wax_t, ba, bx, lam, goa, gob, ex_arrs,
             ex_scatter):
    t_len = z.shape[0]
    n_chunk = t_len // CHUNK
    halo_blocks = CHUNK // ROWS
    ex = _Exchange(ex_arrs, ex_scatter)
    n_in, n_out, n_scratch = 21, 5, 16

    def body(*refs):
        (z_ref, dy_ref, h_ref, hhalo_ref, vhb_ref, xcb_ref, rs_ref, lng_ref, lnb_ref, wm_ref, wmt_ref, bias_ref, cw_ref,
         cb_ref, wax_ref, waxt_ref, ba_ref, bx_ref, lam_ref, goa_ref, gob_ref) = refs[:n_in]
        ex_in = refs[n_in:n_in + ex.n]
        dz_ref, vecs_ref, dws_ref, dwax_ref, dbs_ref = refs[n_in + ex.n:n_in + ex.n + n_out]
        ex_out = refs[n_in + ex.n + n_out:n_in + 2 * ex.n + n_out]
        (vnb_s, vh_s, xc_s, mixed_s, pre_s, dmix_s, dvn_s, dho_s, dxc_s, dpre_s, dz_s, acc_s, accdm_s,
         cg_s, ca_s, dxchalo_s) = refs[n_in + 2 * ex.n + n_out:n_in + 2 * ex.n + n_out + n_scratch]
        ex_sems = refs[n_in + 2 * ex.n + n_out + n_scratch:]
        step = pl.program_id(0)
        c_id = n_chunk - 1 - step
        rid = _row_ids(D_BR)
        first_chunk = c_id == 0

        @pl.when(step == 0)
        def _():
            ex.start(ex_in, ex_out, ex_sems)
            acc_s[...] = jnp.zeros_like(acc_s)
            accdm_s[...] = jnp.zeros_like(accdm_s)
            cg_s[...] = jnp.zeros_like(cg_s)
            ca_s[...] = jnp.zeros_like(ca_s)
            dxchalo_s[...] = jnp.zeros_like(dxchalo_s)
            dws_ref[...] = jnp.zeros_like(dws_ref)
            dwax_ref[...] = jnp.zeros_like(dwax_ref)

        lng, lnb = lng_ref[...], lnb_ref[...]
        h_halo = jnp.where(first_chunk, 0.0, hhalo_ref[...])

        def prev_rows(ref, cols, g, halo):
            before = ref[pl.ds(pl.multiple_of(jnp.maximum(g - 1, 0) * ROWS, ROWS), ROWS), cols]
            return jnp.where(g > 0, before, halo)

        vh_s[...] = vhb_ref[...].astype(F32)
        xc_s[...] = xcb_ref[...].astype(F32)

        for hd in range(N_HEAD):
            cs = slice(hd * HEAD, (hd + 1) * HEAD)
            vnb_s[:, cs] = (vh_s[:, cs] * lng[:, cs] + lnb[:, cs]).astype(BF16)
            mixed_s[:, cs] = _dot(wm_ref[hd], vnb_s[:, cs])
            pre = _dot(xcb_ref[:, cs], wax_ref[hd])
            pre_s[:, cs] = pre[:, :HEAD]
            pre_s[:, D_BR + hd * HEAD:D_BR + (hd + 1) * HEAD] = pre[:, HEAD:]

        goa, gob = goa_ref[...], gob_ref[...]

        def phase3(g, _):
            rows = _rows(g)
            u = z_ref[rows, 0:D_BR]
            ug, tu = _gelu(u)
            ga = z_ref[rows, 2 * D_BR:3 * D_BR]
            sga = _sig(ga)
            sa = ga * sga
            mixed = mixed_s[rows, :] + bias_ref[rows, :]
            ya0 = ug * mixed
            ya = ya0 * sa
            ra = lax.rsqrt(_mean_last(ya * ya) + EPS)
            dyan = dy_ref[rows, 0:D_BR].astype(F32)
            acc_s[V_GOUT_A] += dyan * ya * ra
            dyg = dyan * goa
            dya = ra * dyg - ya * (ra * ra * ra) * _mean_last(dyg * ya)
            dya0 = dya * sa
            dz_s[rows, 2 * D_BR:3 * D_BR] = dya * ya0 * (sga * (1.0 + ga * (1.0 - sga)))
            dmix = dya0 * ug
            dmix_s[rows, :] = dmix
            accdm_s[rows, :] += dmix
            dz_s[rows, 0:D_BR] = dya0 * mixed * _gelu_grad(u, tu)

            hh = h_ref[rows, :]
            gb = z_ref[rows, 4 * D_BR:5 * D_BR]
            sgb = _sig(gb)
            sb = gb * sgb
            yb = hh * sb
            rb = lax.rsqrt(_mean_last(yb * yb) + EPS)
            dybn = dy_ref[rows, D_BR:2 * D_BR].astype(F32)
            acc_s[V_GOUT_B] += dybn * yb * rb
            dyg = dybn * gob
            dyb = rb * dyg - yb * (rb * rb * rb) * _mean_last(dyg * yb)
            dho_s[rows, :] = dyb * sb
            dz_s[rows, 4 * D_BR:5 * D_BR] = dyb * hh * (sgb * (1.0 + gb * (1.0 - sgb)))
            return 0

        _loop(N_GROUP, phase3, 0)

        for hd in range(N_HEAD):
            cs = slice(hd * HEAD, (hd + 1) * HEAD)
            dmb = dmix_s[:, cs].astype(BF16)
            dvn_s[:, cs] = _dot(wmt_ref[hd], dmb)
            dws_ref[hd] += _dot_nt(dmb, vnb_s[:, cs])

        def phase5(g, _):
            rows = _rows(g)
            dvn = dvn_s[rows, :]
            vh = vh_s[rows, :]
            acc_s[V_LN_G] += dvn * vh
            acc_s[V_LN_B] += dvn
            dvh = dvn * lng
            rs = rs_ref[rows, 0:1]
            dvg = rs * (dvh - _mean_last(dvh) - vh * _mean_last(dvh * vh))
            v = z_ref[rows, D_BR:2 * D_BR]
            _, tv = _gelu(v)
            dz_s[rows, D_BR:2 * D_BR] = dvg * _gelu_grad(v, tv)
            return 0

        _loop(N_GROUP, phase5, 0)

        ba, bx = ba_ref[...], bx_ref[...]
        sp8 = LRU_C * _softplus(-lam_ref[...])

        def phase6(k, carry):
            cg, ca = carry
            g = N_GROUP - 1 - k
            rows = _rows(g)
            first_row = jnp.logical_and(jnp.logical_and(first_chunk, g == 0), rid == 0)
            r, i, a, mult = _lru_gates(pre_s[rows, 0:D_BR], pre_s[rows, D_BR:2 * D_BR], ba, bx, sp8, first_row)
            a_nx = jnp.where(rid < ROWS - 1, pltpu.roll(a, ROWS - 1, 0), ca)
            aa, bb = a_nx, dho_s[rows, :]
            for d in (1, 2, 4):
                a_sh = jnp.where(rid < ROWS - d, pltpu.roll(aa, ROWS - d, 0), 1.0)
                b_sh = jnp.where(rid < ROWS - d, pltpu.roll(bb, ROWS - d, 0), 0.0)
                bb = aa * b_sh + bb
                aa = aa * a_sh
            gg = bb + aa * cg
            hh = h_ref[rows, :]
            hprev = _shift_down(hh, prev_rows(h_ref, slice(None), g, h_halo), 1, rid)
            xc = xc_s[rows, :]
            gx = gg * xc
            dla = gg * hprev * a - jnp.where(first_row, 0.0, gx * i * (a * a) * lax.rsqrt(mult * mult))
            acc_s[V_LAM] += -(dla * r)
            dpa = -(dla * sp8) * r * (1.0 - r)
            dpx = gx * mult * i * (1.0 - i)
            acc_s[V_B_A] += dpa
            acc_s[V_B_X] += dpx
            dpre_s[rows, 0:D_BR] = dpa
            dpre_s[rows, D_BR:2 * D_BR] = dpx
            dxc_s[rows, :] = gg * mult * i
            return _bcast_row(gg, 0), _bcast_row(a, 0)

        cg, ca = _loop(N_GROUP, phase6, (cg_s[...], ca_s[...]))
        cg_s[...] = cg
        ca_s[...] = ca

        for hd in range(N_HEAD):
            cs = slice(hd * HEAD, (hd + 1) * HEAD)
            dpre = jnp.concatenate([dpre_s[:, cs], dpre_s[:, D_BR + hd * HEAD:D_BR + (hd + 1) * HEAD]], axis=1).astype(BF16)
            dxc_s[:, cs] += _dot(dpre, waxt_ref[hd])
            dwax_ref[hd] += _dot_tn(xcb_ref[:, cs], dpre)

        def phase8(k, nxt):
            g = N_GROUP - 1 - k
            rows = _rows(g)
            dxc = dxc_s[rows, :]
            acc_s[V_CONV_B] += dxc
            xb = z_ref[rows, 3 * D_BR:4 * D_BR]
            dxb = cw_ref[3:4, :] * dxc
            acc_s[V_CONV_W + 3] += dxc * xb
            for j in range(1, CONV_W):
                later = _shift_up(dxc, nxt, j, rid)
                dxb = dxb + cw_ref[3 - j:4 - j, :] * later
                acc_s[V_CONV_W + 3 - j] += later * xb
            dz_s[rows, 3 * D_BR:4 * D_BR] = dxb
            return dxc

        dxchalo_s[...] = _loop(N_GROUP, phase8, dxchalo_s[...])
        dz_ref[...] = dz_s[...].astype(BF16)

        @pl.when(step == n_chunk - 1)
        def _():
            for v in range(N_VEC):
                vecs_ref[v:v + 1, :] = jnp.sum(acc_s[v], axis=0, keepdims=True)
            lam = lam_ref[...]
            vecs_ref[V_LAM:V_LAM + 1, :] = vecs_ref[V_LAM:V_LAM + 1, :] * (-LRU_C * _sig(-lam))
            tril = (lax.broadcasted_iota(jnp.int32, (HEAD, HEAD), 0) >= lax.broadcasted_iota(jnp.int32, (HEAD, HEAD), 1))
            ones = jnp.ones((ROWS, HEAD), BF16)
            for hd in range(N_HEAD):
                cs = slice(hd * HEAD, (hd + 1) * HEAD)
                dws_ref[hd] = jnp.where(tril, dws_ref[hd], 0.0)
                blk = accdm_s[:, cs]
                hi = blk.astype(BF16)
                lo = (blk - hi.astype(F32)).astype(BF16)
                dbs_ref[hd:hd + 1, :] = (_dot_nt(ones, hi) + _dot_nt(ones, lo))[0:1, :]
            ex.wait(ex_in, ex_out, ex_sems)

    vec = pl.BlockSpec((1, D_BR), lambda i: (0, 0))
    rev = lambda i: (n_chunk - 1 - i, 0)
    halo = lambda col: (lambda i: (jnp.maximum((n_chunk - 1 - i) * halo_blocks - 1, 0), col))
    full3 = lambda a, b, c: pl.BlockSpec((a, b, c), lambda i: (0, 0, 0))
    big = lambda w: pltpu.VMEM((CHUNK, w), F32)
    res = pl.pallas_call(
        body, name="mix_bwd", grid=(n_chunk,),
        in_specs=[pl.BlockSpec((CHUNK, D_IN), rev), pl.BlockSpec((CHUNK, 2 * D_BR), rev), pl.BlockSpec((CHUNK, D_BR), rev),
                  pl.BlockSpec((ROWS, D_BR), halo(0)), pl.BlockSpec((CHUNK, D_BR), rev), pl.BlockSpec((CHUNK, D_BR), rev),
                  pl.BlockSpec((CHUNK, HEAD), rev), vec, vec,
                  full3(N_HEAD, HEAD, HEAD), full3(N_HEAD, HEAD, HEAD),
                  pl.BlockSpec((CHUNK, D_BR), lambda i: (0, 0)), pl.BlockSpec((ROWS, D_BR), lambda i: (0, 0)), vec,
                  full3(N_HEAD, HEAD, 2 * HEAD), full3(N_HEAD, 2 * HEAD, HEAD), vec, vec, vec, vec, vec]
        + [ANY_SPEC] * ex.n,
        out_specs=[pl.BlockSpec((CHUNK, D_IN), rev), pl.BlockSpec((N_VEC, D_BR), lambda i: (0, 0)),
                   full3(N_HEAD, HEAD, HEAD), full3(N_HEAD, HEAD, 2 * HEAD),
                   pl.BlockSpec((N_HEAD, HEAD), lambda i: (0, 0))] + [ANY_SPEC] * ex.n,
        out_shape=[SDS((t_len, D_IN), BF16), SDS((N_VEC, D_BR), F32), SDS((N_HEAD, HEAD, HEAD), F32),
                   SDS((N_HEAD, HEAD, 2 * HEAD), F32), SDS((N_HEAD, HEAD), F32)] + ex.out_shape,
        scratch_shapes=[pltpu.VMEM((CHUNK, D_BR), BF16), big(D_BR), big(D_BR), big(D_BR), big(2 * D_BR), big(D_BR),
                        big(D_BR), big(D_BR), big(D_BR), big(2 * D_BR), big(D_IN),
                        pltpu.VMEM((N_VEC, ROWS, D_BR), F32), big(D_BR),
                        pltpu.VMEM((ROWS, D_BR), F32), pltpu.VMEM((ROWS, D_BR), F32), pltpu.VMEM((ROWS, D_BR), F32)]
        + ex.scratch,
        compiler_params=_params(("arbitrary",), 48),
    )(z, dy, h, h, vhb, xcb, rs, ln_g, ln_b, wm, wm_t, bias, cw, cb, wax, wax_t, ba, bx, lam, goa, gob, *ex_arrs)
    return res[:n_out], res[n_out:]


def _in_bwd(dz, w_in_g, x, dh1, pre_g, first_tile, n_tile, prev, name, ex_arrs=(), ex_scatter=(), tm=256):
    t_len = x.shape[0]
    ex = _Exchange(ex_arrs, ex_scatter)
    n_prev = 0 if prev is None else 2

    def body(dz_ref, w_hbm, x_ref, dh1_ref, g_ref, *refs):
        prev_refs, refs = refs[:n_prev], refs[n_prev:]
        ex_in, (gx_ref, dg_ref), ex_out = refs[:ex.n], refs[ex.n:ex.n + 2], refs[ex.n + 2:2 * ex.n + 2]
        w_s, t_s, dg_s, w_sems = refs[2 * ex.n + 2:2 * ex.n + 6]
        ex_sems = refs[2 * ex.n + 6:]
        i = pl.program_id(0)

        @pl.when(i == 0)
        def _():
            if ex.n:
                ex.start(ex_in, ex_out, ex_sems)
            loads = [pltpu.make_async_copy(w_hbm.at[s], w_s.at[:, s * W_IN_SHARD:(s + 1) * W_IN_SHARD], w_sems.at[s])
                     for s in range(N_DEV)]
            for cp in loads:
                cp.start()
            dg_s[...] = jnp.zeros_like(dg_s)
            for cp in loads:
                cp.wait()

        t_s[...] = _dot_nt(dz_ref[...], w_s[...])
        g = g_ref[...]

        def rows_body(q, acc):
            rows = _tile_rows(q)
            xv = x_ref[rows, :]
            r = lax.rsqrt(_mean_last(xv * xv) + EPS)
            xh = xv * r
            dhn = t_s[rows, :]
            dg = dhn * g
            gx_ref[rows, :] = dh1_ref[rows, :] + r * (dg - xh * _mean_last(dg * xh))
            return acc + _fold_rows(dhn * xh)

        dg_s[...] = _loop(tm // TILE_ROWS, rows_body, dg_s[...], unroll=TILE_UNROLL)

        @pl.when(i == n_tile - 1)
        def _():
            dg = jnp.sum(dg_s[...], axis=0, keepdims=True)
            dg_ref[...] = dg + prev_refs[1][...] if n_prev else dg
            if ex.n:
                ex.wait(ex_in, ex_out, ex_sems)

    tile = pl.BlockSpec((tm, D_MODEL), lambda i: (first_tile + i, 0))
    vec = pl.BlockSpec((1, D_MODEL), lambda i: (0, 0))
    prev_specs = [ANY_SPEC, vec] if n_prev else []
    res = pl.pallas_call(
        body, name=name, grid=(n_tile,),
        in_specs=[pl.BlockSpec((tm, D_IN), lambda i: (first_tile + i, 0)), ANY_SPEC, tile, tile, vec] + prev_specs
        + [ANY_SPEC] * ex.n,
        out_specs=[tile, vec] + [ANY_SPEC] * ex.n,
        out_shape=[SDS((t_len, D_MODEL), F32), SDS((1, D_MODEL), F32)] + ex.out_shape,
        scratch_shapes=[pltpu.VMEM((D_MODEL, D_IN), BF16), pltpu.VMEM((tm, D_MODEL), F32), pltpu.VMEM((ROWS, D_MODEL), F32),
                        pltpu.SemaphoreType.DMA((N_DEV,))] + (ex.scratch if ex.n else []),
        input_output_aliases={5: 0} if n_prev else {},
        compiler_params=_params(("arbitrary",), 54),
    )(dz, w_in_g, x, dh1, pre_g, *(prev or ()), *ex_arrs)
    return res[0], res[1], res[2:]


def _grad_w(a, b, bn, shard_major, name, tk=1024, ex_arrs=(), ex_scatter=()):
    t_len, m = a.shape
    n = b.shape[1]
    n_j, n_k = n // bn, t_len // tk
    ex = _Exchange(ex_arrs, ex_scatter)

    def body(a_ref, b_ref, *refs):
        ex_in, o_ref, ex_out = refs[:ex.n], refs[ex.n], refs[ex.n + 1:2 * ex.n + 1]
        acc_s, ex_sems = refs[2 * ex.n + 1], refs[2 * ex.n + 2:]
        j, k = pl.program_id(0), pl.program_id(1)
        if ex.n:
            @pl.when(jnp.logical_and(j == 0, k == 0))
            def _():
                ex.start(ex_in, ex_out, ex_sems)

        @pl.when(k == 0)
        def _():
            acc_s[...] = jnp.zeros_like(acc_s)

        acc_s[...] += _dot_tn(a_ref[...], b_ref[...])

        @pl.when(k == n_k - 1)
        def _():
            o_ref[...] = acc_s[...].astype(BF16)

        if ex.n:
            @pl.when(jnp.logical_and(j == n_j - 1, k == n_k - 1))
            def _():
                ex.wait(ex_in, ex_out, ex_sems)

    if shard_major:
        out_spec, out_shape = pl.BlockSpec((None, m, bn), lambda j, k: (j, 0, 0)), SDS((n_j, m, bn), BF16)
    else:
        out_spec, out_shape = pl.BlockSpec((m, bn), lambda j, k: (0, j)), SDS((m, n), BF16)
    res = pl.pallas_call(
        body, name=name, grid=(n_j, n_k),
        in_specs=[pl.BlockSpec((tk, m), lambda j, k: (k, 0)), pl.BlockSpec((tk, bn), lambda j, k: (k, j))]
        + [ANY_SPEC] * ex.n,
        out_specs=[out_spec] + [ANY_SPEC] * ex.n, out_shape=[out_shape] + ex.out_shape,
        scratch_shapes=[pltpu.VMEM((m, bn), F32)] + (ex.scratch if ex.n else []),
        compiler_params=_params(("arbitrary", "arbitrary"), 40),
    )(a, b, *ex_arrs)
    return res[0], res[1:]


RS_CHIPS = (6, 2, 4, 0)
RS_SLOTS = (0, 1, 2, 4, 6)


def _grad_w_in_pairs(hn, dz, ex_arrs, ex_scatter, tk=1024):
    t_len = hn.shape[0]
    n_k = t_len // tk
    n_ph = len(RS_CHIPS)
    ex = _Exchange(ex_arrs, ex_scatter)
    me_out = 4 * lax.axis_index("x") + 2 * lax.axis_index("y") + lax.axis_index("c")
    order = jnp.stack([(me_out ^ chip) // 2 for chip in RS_CHIPS]).astype(jnp.int32)
    slots = jnp.stack([me_out ^ k for k in RS_SLOTS]).astype(jnp.int32)
    shard = W_IN_SHARD

    def body(order_ref, a_ref, b_ref, *refs):
        ex_in, parts_hbm, ex_out = refs[:ex.n], refs[ex.n], refs[ex.n + 1:2 * ex.n + 1]
        (acc_s, tb_s, stage_s, rx_s, d2d_send, d2d_recv, ici_send, ici_recv, sib_sems,
         loc_sem) = refs[2 * ex.n + 1:2 * ex.n + 11]
        ex_sems = refs[2 * ex.n + 11:]
        j, k = pl.program_id(0), pl.program_id(1)
        x, y, c, me = _mesh_place()
        sib = _peer(x, y, c, SIBLING)[0]

        def to_sibling(p):
            return _remote(stage_s.at[0], rx_s.at[p % 2], d2d_send.at[p], d2d_recv.at[p], sib)

        def over_ici(p):
            dev = _peer(x, y, c, RS_CHIPS[p])[0]
            return _remote(stage_s.at[1], parts_hbm.at[me], ici_send.at[p], ici_recv.at[p], dev)

        def own_chip():
            return (_remote(stage_s.at[0], parts_hbm.at[me], sib_sems.at[0], sib_sems.at[1], sib),
                    pltpu.make_async_copy(stage_s.at[1], parts_hbm.at[me], loc_sem.at[0]))

        @pl.when(jnp.logical_and(j == 0, k == 0))
        def _():
            ex.start(ex_in, ex_out, ex_sems)

        for p in range(n_ph - 1):
            for core in (0, 1):
                @pl.when(jnp.logical_and(jnp.logical_and(j == p + 1, k == 0), c == core))
                def _(p=p, core=core):
                    to_sibling(p).wait_recv()
                    if p >= 1:
                        over_ici(p - 1).wait_send()
                    mine = acc_s[:, core * shard:(core + 1) * shard]
                    stage_s[1] = (mine + rx_s[p % 2].astype(F32)).astype(BF16)
                    over_ici(p).start()

        @pl.when(k == 0)
        def _():
            acc_s[...] = jnp.zeros_like(acc_s)

        a = a_ref[...]
        acc_s[:, 0:W_BODY] += _dot_tn(a, b_ref[:, 0:W_BODY])
        acc_s[:, shard:shard + W_BODY] += _dot_tn(a, b_ref[:, shard:shard + W_BODY])
        tb_s[:, 0:W_TAIL] = b_ref[:, W_BODY:shard]
        tb_s[:, W_TAIL:2 * W_TAIL] = b_ref[:, shard + W_BODY:2 * shard]
        tails = _dot_tn(a, tb_s[...])
        acc_s[:, W_BODY:shard] += tails[:, 0:W_TAIL]
        acc_s[:, shard + W_BODY:2 * shard] += tails[:, W_TAIL:2 * W_TAIL]

        for p in range(n_ph):
            for core in (0, 1):
                @pl.when(jnp.logical_and(jnp.logical_and(j == p, k == n_k - 1), c == core))
                def _(p=p, core=core):
                    same = acc_s[:, core * shard:(core + 1) * shard]
                    other = acc_s[:, (1 - core) * shard:(2 - core) * shard]
                    if p >= 1:
                        to_sibling(p - 1).wait_send()
                    stage_s[0] = other.astype(BF16)
                    if p < n_ph - 1:
                        to_sibling(p).start()
                    else:
                        over_ici(n_ph - 2).wait_send()
                        stage_s[1] = same.astype(BF16)
                        for cp in own_chip():
                            cp.start()

        @pl.when(jnp.logical_and(j == n_ph - 1, k == n_k - 1))
        def _():
            to_sib, local = own_chip()
            to_sib.wait_send()
            local.wait()
            _remote(stage_s.at[0], parts_hbm.at[_peer(x, y, c, SIBLING)[1]], sib_sems.at[0], sib_sems.at[1], sib).wait_recv()
            for p in range(n_ph - 1):
                dev, lin = _peer(x, y, c, RS_CHIPS[p])
                _remote(stage_s.at[0], parts_hbm.at[lin], ici_send.at[p], ici_recv.at[p], dev).wait_recv()
            ex.wait(ex_in, ex_out, ex_sems)

    dma = lambda n: pltpu.SemaphoreType.DMA((n,))
    grid_spec = pltpu.PrefetchScalarGridSpec(
        num_scalar_prefetch=1, grid=(n_ph, n_k),
        in_specs=[pl.BlockSpec((tk, D_MODEL), lambda j, k, order: (k, 0)),
                  pl.BlockSpec((tk, 2 * shard), lambda j, k, order: (k, order[j]))] + [ANY_SPEC] * ex.n,
        out_specs=[ANY_SPEC] * (1 + ex.n),
        scratch_shapes=[pltpu.VMEM((D_MODEL, 2 * shard), F32), pltpu.VMEM((tk, 2 * W_TAIL), BF16),
                        pltpu.VMEM((2, D_MODEL, shard), BF16),
                        pltpu.VMEM((2, D_MODEL, shard), BF16), dma(n_ph - 1), dma(n_ph - 1), dma(n_ph - 1),
                        dma(n_ph - 1), dma(2), dma(1)] + ex.scratch)
    res = pl.pallas_call(
        body, name="grad_w_in", grid_spec=grid_spec,
        out_shape=[SDS((N_DEV, D_MODEL, shard), BF16)] + ex.out_shape,
        compiler_params=_params(("arbitrary", "arbitrary"), 54),
    )(order, hn, dz, *ex_arrs)
    return res[0], slots, res[1:]


def _sum_parts(parts, name):
    def body(p_ref, o_ref):
        g = p_ref[0].astype(F32)
        for s in range(1, parts.shape[0]):
            g = g + p_ref[s].astype(F32)
        o_ref[...] = g

    return pl.pallas_call(body, name=name, out_shape=SDS(parts.shape[1:], F32))(parts)


def _adamw_math(g, w_ref, m_ref, v_ref, g_ref, d_ref, nm_ref, nv_ref):
    c1 = 1.0 - ADAM_B1 ** ADAM_STEP
    c2 = 1.0 - ADAM_B2 ** ADAM_STEP
    g_ref[...] = g
    nm = ADAM_B1 * m_ref[...] + (1.0 - ADAM_B1) * g
    nv = ADAM_B2 * v_ref[...] + (1.0 - ADAM_B2) * (g * g)
    nm_ref[...] = nm
    nv_ref[...] = nv
    d_ref[...] = -ADAM_LR * ((nm / c1) / (jnp.sqrt(nv / c2) + ADAM_EPS) + ADAM_WD * w_ref[...])


def _adamw(parts, w, m, v, name, tr):
    rows, cols = w.shape
    n_parts = parts.shape[0]

    def body(p_ref, *refs):
        g = p_ref[0].astype(F32)
        for s in range(1, n_parts):
            g = g + p_ref[s].astype(F32)
        _adamw_math(g, *refs)

    tile = pl.BlockSpec((tr, cols), lambda i: (i, 0))
    return pl.pallas_call(
        body, name=name, grid=(rows // tr,),
        in_specs=[pl.BlockSpec((n_parts, tr, cols), lambda i: (0, i, 0)), tile, tile, tile],
        out_specs=[tile] * 4, out_shape=[SDS((rows, cols), F32)] * 4,
        compiler_params=_params(("arbitrary",), 40),
    )(parts, w, m, v)


def _adamw_unpacked(grads, triples, name):
    n = len(triples)
    n_rows = [t[0].shape[0] for t in triples]

    def body(g_ref, *refs):
        ins, outs = refs[:3 * n], refs[3 * n:]
        row = 0
        for i in range(n):
            _adamw_math(g_ref[row:row + n_rows[i], :], *ins[3 * i:3 * i + 3], *outs[4 * i:4 * i + 4])
            row += n_rows[i]
        outs[4 * n][...] = g_ref[row:row + ROWS, :]

    out_shape = [SDS((r, LANES), F32) for r in n_rows for _ in range(4)] + [SDS((ROWS, LANES), F32)]
    return pl.pallas_call(
        body, name=name, out_shape=out_shape,
        compiler_params=pltpu.CompilerParams(vmem_limit_bytes=40 * MIB),
    )(grads, *[a for t in triples for a in t])


def _adamw_slots(parts, slots, w, m, v, name, tr):
    rows, cols = w.shape
    n_slots = slots.shape[0]

    def body(slots_ref, *refs):
        g = refs[0][...].astype(F32)
        for s in range(1, n_slots):
            g = g + refs[s][...].astype(F32)
        _adamw_math(g, *refs[n_slots:])

    tile = pl.BlockSpec((tr, cols), lambda i, slots: (i, 0))
    part = lambda s: pl.BlockSpec((None, tr, cols), lambda i, slots: (slots[s], i, 0))
    grid_spec = pltpu.PrefetchScalarGridSpec(
        num_scalar_prefetch=1, grid=(rows // tr,),
        in_specs=[part(s) for s in range(n_slots)] + [tile, tile, tile], out_specs=[tile] * 4)
    return pl.pallas_call(
        body, name=name, grid_spec=grid_spec, out_shape=[SDS((rows, cols), F32)] * 4,
        compiler_params=_params(("arbitrary",), 40),
    )(slots, *([parts] * n_slots), w, m, v)


PACKED = ("gmlp_ln_g", "gmlp_ln_b", "gmlp_ws", "gmlp_bs", "conv_b", "w_a", "b_a", "w_x", "b_x", "lam", "gmlp_out_g",
          "lru_out_g", "post_g")
WEIGHTS = ("pre_g", "w_in", "gmlp_ln_g", "gmlp_ln_b", "gmlp_ws", "gmlp_bs", "conv_w", "conv_b", "w_a", "b_a", "w_x",
           "b_x", "lam", "gmlp_out_g", "lru_out_g", "w_out", "post_g", "w_pe", "w_pg")
LANES = 128


PACK_ROWS = 3200
IN_BWD_TILE = 256


def _pack(parts):
    rows = [p.reshape(-1, LANES) for p in parts]
    used = sum(r.shape[0] for r in rows)
    return jnp.concatenate(rows + [jnp.zeros((PACK_ROWS - used, LANES), F32)], axis=0)


def _pad_rows(a, rows):
    return jnp.concatenate([a, jnp.zeros((rows - a.shape[0],) + a.shape[1:], a.dtype)], axis=0)


def kernel(x, p, pre_g, w_in, gmlp_ln_g, gmlp_ln_b, gmlp_ws, gmlp_bs, conv_w, conv_b, w_a, b_a, w_x, b_x, lam, gmlp_out_g, lru_out_g, w_out, post_g, w_pe, w_pg, loss_target, m_pre_g, m_w_in, m_gmlp_ln_g, m_gmlp_ln_b, m_gmlp_ws, m_gmlp_bs, m_conv_w, m_conv_b, m_w_a, m_b_a, m_w_x, m_b_x, m_lam, m_gmlp_out_g, m_lru_out_g, m_w_out, m_post_g, m_w_pe, m_w_pg, v_pre_g, v_w_in, v_gmlp_ln_g, v_gmlp_ln_b, v_gmlp_ws, v_gmlp_bs, v_conv_w, v_conv_b, v_w_a, v_b_a, v_w_x, v_b_x, v_lam, v_gmlp_out_g, v_lru_out_g, v_w_out, v_post_g, v_w_pe, v_w_pg):
    args = dict(locals())
    weights = {n: args[n] for n in WEIGHTS}
    m_in = {n: args["m_" + n] for n in WEIGHTS}
    v_in = {n: args["v_" + n] for n in WEIGHTS}
    sm = {n: weights[n][0] for n in PACKED}
    shard_rows = D_MODEL // N_DEV
    xs, ps, tgt = x[0], p[0, 0], loss_target[0]

    vec = lambda a: a.reshape(1, -1)
    tril = jnp.tril(jnp.ones((CHUNK, CHUNK), dtype=bool))
    wm32 = jnp.where(tril[None], sm["gmlp_ws"], 0.0)
    wm, wm_t = wm32.astype(BF16), jnp.swapaxes(wm32, 1, 2).astype(BF16)
    bias = jnp.repeat(sm["gmlp_bs"].T, HEAD, axis=1)
    wax32 = jnp.concatenate([sm["w_a"], sm["w_x"]], axis=2)
    wax, wax_t = wax32.astype(BF16), jnp.swapaxes(wax32, 1, 2).astype(BF16)
    ln_g, ln_b = vec(sm["gmlp_ln_g"]), vec(sm["gmlp_ln_b"])
    post_g_v = vec(sm["post_g"])

    hn = _pre_norm(xs, pre_g)
    cw_shard = _pad_rows(conv_w.reshape(CONV_W, HEAD), ROWS)
    z, w_in_g, (cw_g,) = _in_proj(hn, w_in[0].astype(BF16), [cw_shard])
    cw_full = jnp.transpose(cw_g[:, :CONV_W, :], (1, 0, 2)).reshape(CONV_W, D_BR)
    mixer_consts = dict(cw=_pad_rows(cw_full, ROWS), cb=vec(sm["conv_b"]), ba=vec(sm["b_a"]), bx=vec(sm["b_x"]),
                        lam=vec(sm["lam"]), goa=vec(sm["gmlp_out_g"]), gob=vec(sm["lru_out_g"]))
    (y, h, vhb, xcb, v_rs), (w_out_g, w_pe_g, w_pg_g) = _mix_fwd(
        z, ln_g, ln_b, wm, bias, wax=wax, **mixer_consts,
        ex_arrs=[w_out[0].astype(BF16), w_pe[0].astype(BF16), w_pg[0].astype(BF16)], ex_scatter=[False, False, False])
    w_out_f, w_pg_f = w_out_g.reshape(D_MODEL, D_MODEL), w_pg_g.reshape(D_MODEL, D_MODEL)
    h1, ob = _out_proj(y, xs, w_out_f, post_g_v)
    dh2, dgl, h1b, loss_part, d_w_pe = _ple_loss(h1, ps, tgt, w_pg_f, w_pe_g)

    dh1, do, dy, d_post_g = _tail_bwd(dh2, dgl, ob, w_pg_f, w_out_f, post_g_v)
    d_w_out, _ = _grad_w(y, do, 1024, False, "grad_w_out")
    d_w_pg, _ = _grad_w(h1b, dgl, 1024, False, "grad_w_pg")
    (dz, vecs, d_ws, d_wax, d_bs), (parts_out, parts_pg, parts_pe) = _mix_bwd(
        z, dy, h, vhb, xcb, v_rs, ln_g, ln_b, wm, wm_t, bias, wax=wax, wax_t=wax_t, **mixer_consts,
        ex_arrs=[d_w_out.reshape(N_DEV, shard_rows, D_MODEL), d_w_pg.reshape(N_DEV, shard_rows, D_MODEL), d_w_pe],
        ex_scatter=[True, True, True])

    small = {"gmlp_ln_g": vecs[V_LN_G], "gmlp_ln_b": vecs[V_LN_B], "gmlp_ws": d_ws, "gmlp_bs": d_bs,
             "conv_b": vecs[V_CONV_B], "w_a": d_wax[:, :, :HEAD], "b_a": vecs[V_B_A], "w_x": d_wax[:, :, HEAD:],
             "b_x": vecs[V_B_X], "lam": vecs[V_LAM], "gmlp_out_g": vecs[V_GOUT_A], "lru_out_g": vecs[V_GOUT_B],
             "post_g": d_post_g}
    small_part = _pack([small[n] for n in PACKED] + [loss_part]).reshape(N_DEV, PACK_ROWS // N_DEV, LANES)
    d_cw_blocks = jnp.transpose(vecs[V_CONV_W:V_CONV_W + CONV_W].reshape(CONV_W, N_DEV, HEAD), (1, 0, 2))
    d_cw_blocks = jnp.concatenate([d_cw_blocks, jnp.zeros((N_DEV, ROWS - CONV_W, HEAD), F32)], axis=1)
    parts_in, slots_in, (small_blocks, parts_cw) = _grad_w_in_pairs(
        hn, dz, ex_arrs=[small_part, d_cw_blocks], ex_scatter=[True, True])
    small_sum = _sum_parts(small_blocks, "sum_small")
    grad_x, d_pre_g, _ = _in_bwd(dz, w_in_g, xs, dh1, pre_g, 0, xs.shape[0] // IN_BWD_TILE, None, "in_bwd",
                                 tm=IN_BWD_TILE)
    pre_rows = D_MODEL // LANES
    small_all, parts_pre = _exchange([small_sum, d_pre_g.reshape(pre_rows, LANES)], False, "gather_small_grads")

    pad_cw = lambda a: _pad_rows(a.reshape(CONV_W, HEAD), ROWS)
    flat = lambda a: a.reshape(pre_rows, LANES)
    outs = {
        "w_in": _adamw_slots(parts_in, slots_in, w_in[0], m_w_in[0], v_w_in[0], "adamw_w_in", 256),
        "w_out": _adamw(parts_out, w_out[0], m_w_out[0], v_w_out[0], "adamw_w_out", 128),
        "w_pe": _adamw(parts_pe, w_pe[0], m_w_pe[0], v_w_pe[0], "adamw_w_pe", 256),
        "w_pg": _adamw(parts_pg, w_pg[0], m_w_pg[0], v_w_pg[0], "adamw_w_pg", 128),
        "conv_w": [a[:CONV_W] for a in
                   _adamw(parts_cw, pad_cw(conv_w), pad_cw(m_conv_w), pad_cw(v_conv_w), "adamw_conv_w", ROWS)],
        "pre_g": _adamw(parts_pre, flat(pre_g), flat(m_pre_g), flat(v_pre_g), "adamw_pre_g", pre_rows),
    }
    as_rows = lambda a: a.reshape(-1, LANES)
    small_res = _adamw_unpacked(small_all.reshape(PACK_ROWS, LANES),
                                [(as_rows(weights[n]), as_rows(m_in[n]), as_rows(v_in[n])) for n in PACKED], "adamw_small")
    for i, n in enumerate(PACKED):
        outs[n] = small_res[4 * i:4 * i + 4]
    loss = small_res[-1][0, 0]

    result = [loss, grad_x[None]]
    for q in range(4):
        result += [outs[n][q].reshape(weights[n].shape) for n in WEIGHTS]
    return tuple(result)
```

```python
import jax
import jax.numpy as jnp
from jax import lax
from jax.experimental import pallas as pl
from jax.experimental.pallas import tpu as pltpu

F32 = jnp.float32
BF16 = jnp.bfloat16
SDS = jax.ShapeDtypeStruct

D_MODEL = 2048
D_BR = 1024
D_IN = 5 * D_BR
D_PLE = 256
N_HEAD = 8
HEAD = 128
CHUNK = 128
ROWS = 8
N_GROUP = CHUNK // ROWS
N_DEV = 8
W_IN_SHARD = D_IN // N_DEV
EPS = 1e-6
LRU_C = 8.0
CONV_W = 4
MIB = 1 << 20

ADAM_LR, ADAM_B1, ADAM_B2, ADAM_EPS, ADAM_WD, ADAM_STEP = 0.001, 0.9, 0.999, 1e-08, 0.01, 10

_GELU_C = 0.7978845608028654
_GELU_A = 0.044715

V_LN_G, V_LN_B, V_CONV_B, V_B_A, V_B_X, V_LAM, V_GOUT_A, V_GOUT_B, V_CONV_W = 0, 1, 2, 3, 4, 5, 6, 7, 8
N_VEC = 16


def _params(sem, vmem_mib):
    return pltpu.CompilerParams(dimension_semantics=sem, vmem_limit_bytes=int(vmem_mib * MIB))


def _sig(x):
    return 0.5 * jnp.tanh(0.5 * x) + 0.5


def _gelu(x, with_grad=False):
    sq = x * x
    t = jnp.tanh(x * (_GELU_C + (_GELU_C * _GELU_A) * sq))
    half, one_t = 0.5 * x, 1.0 + t
    if not with_grad:
        return half * one_t
    grad = 0.5 * one_t + half * ((1.0 - t) * one_t) * (_GELU_C + (3.0 * _GELU_C * _GELU_A) * sq)
    return half * one_t, grad


def _silu_grad(s, xs):
    return s + xs * (1.0 - s)


def _neg_expm1(y, exp_y):
    series = -y * (1.0 + y * (0.5 + y * (1.0 / 6.0)))
    return jnp.where(y > -0.01, series, 1.0 - exp_y)


def _softplus(x):
    return jnp.maximum(x, 0.0) + jnp.log(1.0 + jnp.exp(-jnp.abs(x)))


def _row_ids(width):
    return lax.broadcasted_iota(jnp.int32, (ROWS, width), 0)


def _shift_down(cur, prev, k, rid):
    return jnp.where(rid >= k, pltpu.roll(cur, k, 0), pltpu.roll(prev, k, 0))


def _shift_up(cur, nxt, k, rid):
    return jnp.where(rid < ROWS - k, pltpu.roll(cur, ROWS - k, 0), pltpu.roll(nxt, ROWS - k, 0))


def _mean_last(x):
    return jnp.mean(x, axis=-1, keepdims=True)


def _rows(g):
    return pl.ds(pl.multiple_of(g * ROWS, ROWS), ROWS)


TILE_ROWS = 16


def _tile_rows(q):
    return pl.ds(pl.multiple_of(q * TILE_ROWS, TILE_ROWS), TILE_ROWS)


UNROLL = 4
TILE_UNROLL = 8


def _loop(n, body, init, unroll=UNROLL):
    def wide(i, carry):
        for u in range(unroll):
            carry = body(i * unroll + u, carry)
        return carry

    return lax.fori_loop(0, n // unroll, wide, init)


def _fold_rows(x):
    return x[0:ROWS, :] + x[ROWS:TILE_ROWS, :]


def _bcast_row(x, r):
    return jnp.broadcast_to(x[r:r + 1, :], x.shape)


def _dot(a, b):
    return jnp.dot(a, b, preferred_element_type=F32)


def _dot_nt(a, b):
    return lax.dot_general(a, b, (((1,), (1,)), ((), ())), preferred_element_type=F32)


def _dot_tn(a, b):
    return lax.dot_general(a, b, (((0,), (0,)), ((), ())), preferred_element_type=F32)


def _mesh_place():
    x, y, c = lax.axis_index("x"), lax.axis_index("y"), lax.axis_index("c")
    return x, y, c, 4 * x + 2 * y + c


def _peer(x, y, c, k):
    px = 1 - x if k & 4 else x
    py = 1 - y if k & 2 else y
    pc = 1 - c if k & 1 else c
    return (px, py, pc), 4 * px + 2 * py + pc


def _remote(src, dst, send_sem, recv_sem, dev):
    return pltpu.make_async_remote_copy(src_ref=src, dst_ref=dst, send_sem=send_sem, recv_sem=recv_sem, device_id=dev,
                                        device_id_type=pl.DeviceIdType.MESH)


ANY_SPEC = pl.BlockSpec(memory_space=pl.ANY)


class _Exchange:
    def __init__(self, arrs, scatter):
        self.n = len(arrs)
        self.scatter = tuple(scatter)
        self.out_shape = [SDS(a.shape if s else (N_DEV,) + a.shape, a.dtype) for a, s in zip(arrs, scatter)]
        self.scratch = [pltpu.SemaphoreType.DMA((self.n * N_DEV,)), pltpu.SemaphoreType.DMA((self.n * N_DEV,)),
                        pltpu.SemaphoreType.DMA((self.n,))]

    def _copies(self, ins, outs, sems):
        send_sems, recv_sems, local_sems = sems
        x, y, c, me = _mesh_place()
        local, sends, recvs = [], [], []
        for a in range(self.n):
            src = ins[a].at[me] if self.scatter[a] else ins[a]
            local.append(pltpu.make_async_copy(src, outs[a].at[me], local_sems.at[a]))
        for k in range(1, N_DEV):
            dev, lin = _peer(x, y, c, k)
            for a in range(self.n):
                src = ins[a].at[lin] if self.scatter[a] else ins[a]
                pair = (send_sems.at[a * N_DEV + k], recv_sems.at[a * N_DEV + k], dev)
                sends.append(_remote(src, outs[a].at[me], *pair))
                recvs.append(_remote(src, outs[a].at[lin], *pair))
        return local, sends, recvs

    def start(self, ins, outs, sems):
        local, sends, _ = self._copies(ins, outs, sems)
        for cp in local + sends:
            cp.start()

    def wait(self, ins, outs, sems):
        local, sends, recvs = self._copies(ins, outs, sems)
        for cp in recvs:
            cp.wait_recv()
        for cp in sends:
            cp.wait_send()
        for cp in local:
            cp.wait()


def _exchange(arrs, scatter, name):
    ex = _Exchange(arrs, [scatter] * len(arrs))
    n = ex.n

    def body(*refs):
        ins, outs, sems = refs[:n], refs[n:2 * n], refs[2 * n:]
        ex.start(ins, outs, sems)
        ex.wait(ins, outs, sems)

    return pl.pallas_call(
        body, name=name, out_shape=ex.out_shape, in_specs=[ANY_SPEC] * n, out_specs=[ANY_SPEC] * n,
        scratch_shapes=ex.scratch,
    )(*arrs)


def _pre_norm(x, pre_g, tm=512):
    t_len = x.shape[0]

    def body(x_ref, g_ref, hn_ref):
        g = g_ref[...]

        def rows_body(q, _):
            rows = _tile_rows(q)
            xv = x_ref[rows, :]
            hn_ref[rows, :] = (xv * lax.rsqrt(_mean_last(xv * xv) + EPS) * g).astype(BF16)
            return 0

        _loop(tm // TILE_ROWS, rows_body, 0, unroll=TILE_UNROLL)

    tile = pl.BlockSpec((tm, D_MODEL), lambda i: (i, 0))
    return pl.pallas_call(
        body, name="pre_norm", grid=(t_len // tm,),
        in_specs=[tile, pl.BlockSpec((1, D_MODEL), lambda i: (0, 0))], out_specs=tile,
        out_shape=SDS((t_len, D_MODEL), BF16),
        compiler_params=_params(("arbitrary",), 24),
    )(x, pre_g)


CHIP_ORDER = (0, 2, 4, 6)
W_BODY, W_TAIL = 512, 128
SIBLING = 1
ICI_MASKS = (2, 4, 6)
DIRECT_MASKS = (SIBLING,) + ICI_MASKS
Y_NEIGHBOUR, X_NEIGHBOUR, DIAGONAL = 2, 4, 6
W_DIRECT = (SIBLING, Y_NEIGHBOUR, X_NEIGHBOUR)


def _in_proj(hn, w_shard, others, tm=1024):
    t_len = hn.shape[0]
    n_i = t_len // tm
    n_o = len(others)
    me_out = 4 * lax.axis_index("x") + 2 * lax.axis_index("y") + lax.axis_index("c")
    order = jnp.stack([(me_out ^ chip) // 2 for chip in CHIP_ORDER]).astype(jnp.int32)

    def body(order_ref, hn_ref, w_hbm, *refs):
        o_in = refs[:n_o]
        z_ref, wg_hbm = refs[n_o], refs[n_o + 1]
        o_out = refs[n_o + 2:2 * n_o + 2]
        (wbuf, tail_s, send_w, recv_w, fsend_w, frecv_w, send_o, recv_o, fsend_o, frecv_o, wb_sems, loc_sems, rsend,
         rrecv) = refs[2 * n_o + 2:]
        j, i = pl.program_id(0), pl.program_id(1)
        x, y, c, me = _mesh_place()
        sib = _peer(x, y, c, SIBLING)[0]

        def relay(core):
            src, dst = (Y_NEIGHBOUR, X_NEIGHBOUR) if core == 0 else (X_NEIGHBOUR, Y_NEIGHBOUR)
            held, diag = _peer(x, y, c, src)[1], _peer(x, y, c, DIAGONAL)[1]
            pair = (rsend.at[0], rrecv.at[0], _peer(x, y, c, dst)[0])
            return _remote(wbuf.at[held], wbuf.at[held], *pair), _remote(wbuf.at[diag], wbuf.at[diag], *pair)

        def direct(k, a=None):
            dev, lin = _peer(x, y, c, k)
            if a is None:
                return (_remote(w_hbm, wbuf.at[me], send_w.at[k], recv_w.at[k], dev),
                        _remote(w_hbm, wbuf.at[lin], send_w.at[k], recv_w.at[k], dev))
            pair = (send_o.at[a * N_DEV + k], recv_o.at[a * N_DEV + k], dev)
            return _remote(o_in[a], o_out[a].at[me], *pair), _remote(o_in[a], o_out[a].at[lin], *pair)

        def passed(k, a=None):
            mine, theirs = _peer(x, y, c, k)[1], _peer(x, y, c, k ^ SIBLING)[1]
            if a is None:
                pair = (fsend_w.at[k], frecv_w.at[k], sib)
                return _remote(wbuf.at[mine], wbuf.at[mine], *pair), _remote(wbuf.at[theirs], wbuf.at[theirs], *pair)
            pair = (fsend_o.at[a * N_DEV + k], frecv_o.at[a * N_DEV + k], sib)
            return (_remote(o_out[a].at[mine], o_out[a].at[mine], *pair),
                    _remote(o_out[a].at[theirs], o_out[a].at[theirs], *pair))

        def own_copies():
            return [pltpu.make_async_copy(o_in[a], o_out[a].at[me], loc_sems.at[1 + a]) for a in range(n_o)]

        @pl.when(jnp.logical_and(j == 0, i == 0))
        def _():
            own = pltpu.make_async_copy(w_hbm, wbuf.at[me], loc_sems.at[0])
            own.start()
            for cp in own_copies():
                cp.start()
            for k in W_DIRECT:
                direct(k)[0].start()
            for k in DIRECT_MASKS:
                for a in range(n_o):
                    direct(k, a)[0].start()
            own.wait()

        low = 2 * order_ref[j]

        for jp, chip in enumerate(CHIP_ORDER):
            @pl.when(jnp.logical_and(j == jp, i == 0))
            def _(jp=jp, chip=chip):
                if chip == 0:
                    direct(SIBLING)[1].wait_recv()
                elif chip == Y_NEIGHBOUR:
                    for mask in (Y_NEIGHBOUR, X_NEIGHBOUR):
                        direct(mask)[1].wait_recv()
                        passed(mask)[0].start()
                    for core in (0, 1):
                        @pl.when(c == core)
                        def _(core=core):
                            relay(core)[0].start()
                    passed(Y_NEIGHBOUR)[1].wait_recv()
                elif chip == X_NEIGHBOUR:
                    passed(X_NEIGHBOUR)[1].wait_recv()
                    for core in (0, 1):
                        @pl.when(c == core)
                        def _(core=core):
                            relay(core)[1].wait_recv()
                    passed(DIAGONAL)[0].start()
                    for k in ICI_MASKS:
                        for a in range(n_o):
                            direct(k, a)[1].wait_recv()
                            passed(k, a)[0].start()
                else:
                    passed(DIAGONAL)[1].wait_recv()
                for half in (0, 1):
                    pltpu.make_async_copy(wbuf.at[low + half], wg_hbm.at[low + half], wb_sems.at[2 * jp + half]).start()
                tail_s[:, 0:W_TAIL] = wbuf[low, :, W_BODY:W_IN_SHARD]
                tail_s[:, W_TAIL:2 * W_TAIL] = wbuf[low + 1, :, W_BODY:W_IN_SHARD]

        hn = hn_ref[...]
        z_ref[:, 0:W_BODY] = _dot(hn, wbuf[low, :, 0:W_BODY])
        z_ref[:, W_IN_SHARD:W_IN_SHARD + W_BODY] = _dot(hn, wbuf[low + 1, :, 0:W_BODY])
        tails = _dot(hn, tail_s[...])
        z_ref[:, W_BODY:W_IN_SHARD] = tails[:, 0:W_TAIL]
        z_ref[:, W_IN_SHARD + W_BODY:2 * W_IN_SHARD] = tails[:, W_TAIL:2 * W_TAIL]

        @pl.when(jnp.logical_and(j == len(CHIP_ORDER) - 1, i == n_i - 1))
        def _():
            for a in range(n_o):
                direct(SIBLING, a)[1].wait_recv()
            for k in ICI_MASKS:
                for a in range(n_o):
                    passed(k, a)[1].wait_recv()
            for k in W_DIRECT:
                direct(k)[0].wait_send()
            for core in (0, 1):
                @pl.when(c == core)
                def _(core=core):
                    relay(core)[0].wait_send()
            for k in DIRECT_MASKS:
                for a in range(n_o):
                    direct(k, a)[0].wait_send()
            for k in ICI_MASKS:
                passed(k)[0].wait_send()
                for a in range(n_o):
                    passed(k, a)[0].wait_send()
            for cp in own_copies():
                cp.wait()
            for jj in range(N_DEV):
                pltpu.make_async_copy(wbuf.at[0], wg_hbm.at[0], wb_sems.at[jj]).wait()

    dma = lambda n: pltpu.SemaphoreType.DMA((n,))
    grid_spec = pltpu.PrefetchScalarGridSpec(
        num_scalar_prefetch=1, grid=(len(CHIP_ORDER), n_i),
        in_specs=[pl.BlockSpec((tm, D_MODEL), lambda j, i, order: (i, 0)), ANY_SPEC] + [ANY_SPEC] * n_o,
        out_specs=[pl.BlockSpec((tm, 2 * W_IN_SHARD), lambda j, i, order: (i, order[j])), ANY_SPEC] + [ANY_SPEC] * n_o,
        scratch_shapes=[pltpu.VMEM((N_DEV, D_MODEL, W_IN_SHARD), BF16), pltpu.VMEM((D_MODEL, 2 * W_TAIL), BF16),
                        dma(N_DEV), dma(N_DEV), dma(N_DEV), dma(N_DEV),
                        dma(n_o * N_DEV), dma(n_o * N_DEV), dma(n_o * N_DEV), dma(n_o * N_DEV), dma(N_DEV), dma(1 + n_o),
                        dma(1), dma(1)])
    res = pl.pallas_call(
        body, name="in_proj", grid_spec=grid_spec,
        out_shape=[SDS((t_len, D_IN), F32), SDS((N_DEV, D_MODEL, W_IN_SHARD), BF16)]
        + [SDS((N_DEV,) + o.shape, o.dtype) for o in others],
        compiler_params=_params(("arbitrary", "arbitrary"), 54),
    )(order, hn, w_shard, *others)
    return res[0], res[1], res[2:]


def _conv_rows(cur, prev, cw_ref, cb, rid):
    acc = cw_ref[3:4, :] * cur + cb
    for k in range(1, CONV_W):
        acc = acc + cw_ref[3 - k:4 - k, :] * _shift_down(cur, prev, k, rid)
    return acc


ROW0_LOG_A = -1e30


def _row0_mask(rid):
    return jnp.where(rid == 0, ROW0_LOG_A, 0.0)


def _row0_bias(is_first_group, row0_mask):
    return is_first_group.astype(F32) * row0_mask


def _lru_gates(pa, px, ba, bx, sp8, row0_bias):
    r = _sig(pa + ba)
    i = _sig(px + bx)
    la = row0_bias - r * sp8
    a = jnp.exp(la)
    return r, i, a, _neg_expm1(2.0 * la, a * a)


def _mix_fwd(z, ln_g, ln_b, wm, bias, cw, cb, wax, ba, bx, lam, goa, gob, ex_arrs, ex_scatter):
    t_len = z.shape[0]
    n_chunk = t_len // CHUNK
    ex = _Exchange(ex_arrs, ex_scatter)
    n_in, n_out, n_scratch = 13, 5, 7

    def body(*refs):
        (z_ref, lng_ref, lnb_ref, wm_ref, bias_ref, cw_ref, cb_ref, wax_ref, ba_ref, bx_ref, lam_ref, goa_ref,
         gob_ref) = refs[:n_in]
        ex_in = refs[n_in:n_in + ex.n]
        y_ref, h_ref, vhb_ref, xcb_ref, rs_ref = refs[n_in + ex.n:n_in + ex.n + n_out]
        ex_out = refs[n_in + ex.n + n_out:n_in + 2 * ex.n + n_out]
        vn_s, xc_s, mixed_s, pre_s, y_s, carry_s, halo_s = refs[n_in + 2 * ex.n + n_out:n_in + 2 * ex.n + n_out + n_scratch]
        ex_sems = refs[n_in + 2 * ex.n + n_out + n_scratch:]
        c_id = pl.program_id(0)
        rid = _row_ids(D_BR)

        @pl.when(c_id == 0)
        def _():
            ex.start(ex_in, ex_out, ex_sems)
            carry_s[...] = jnp.zeros_like(carry_s)
            halo_s[...] = jnp.zeros_like(halo_s)

        lng, lnb, cb = lng_ref[...], lnb_ref[...], cb_ref[...]

        def phase1(g, prev):
            rows = _rows(g)
            vg = _gelu(z_ref[rows, D_BR:2 * D_BR])
            xm = vg - _mean_last(vg)
            rs = lax.rsqrt(_mean_last(xm * xm) + EPS)
            vn_s[rows, :] = xm * rs
            rs_ref[rows, :] = jnp.broadcast_to(rs, (ROWS, HEAD))
            xb = z_ref[rows, 3 * D_BR:4 * D_BR]
            xc_s[rows, :] = _conv_rows(xb, prev, cw_ref, cb, rid)
            return xb

        halo_s[...] = _loop(N_GROUP, phase1, halo_s[...], unroll=8)
        vhb_ref[...] = vn_s[...].astype(BF16)
        xcb_ref[...] = xc_s[...].astype(BF16)

        for h in range(N_HEAD):
            cs = slice(h * HEAD, (h + 1) * HEAD)
            mixed_s[:, cs] = _dot(wm_ref[h], (vn_s[:, cs] * lng[:, cs] + lnb[:, cs]).astype(BF16))
            pre = _dot(xcb_ref[:, cs], wax_ref[h])
            pre_s[:, cs] = pre[:, :HEAD]
            pre_s[:, D_BR + h * HEAD:D_BR + (h + 1) * HEAD] = pre[:, HEAD:]

        ba, bx, goa, gob = ba_ref[...], bx_ref[...], goa_ref[...], gob_ref[...]
        sp8 = LRU_C * _softplus(-lam_ref[...])
        row0 = _row0_mask(rid)

        def phase3(g, carry):
            rows = _rows(g)
            ug = _gelu(z_ref[rows, 0:D_BR])
            ga = z_ref[rows, 2 * D_BR:3 * D_BR]
            ya = ug * (mixed_s[rows, :] + bias_ref[rows, :]) * (ga * _sig(ga))
            y_s[rows, 0:D_BR] = ya * lax.rsqrt(_mean_last(ya * ya) + EPS) * goa

            bias0 = _row0_bias(jnp.logical_and(c_id == 0, g == 0), row0)
            _, i, a, m2 = _lru_gates(pre_s[rows, 0:D_BR], pre_s[rows, D_BR:2 * D_BR], ba, bx, sp8, bias0)
            b = jnp.sqrt(m2) * i * xc_s[rows, :]
            for d in (1, 2, 4):
                a_sh = jnp.where(rid >= d, pltpu.roll(a, d, 0), 1.0)
                b_sh = jnp.where(rid >= d, pltpu.roll(b, d, 0), 0.0)
                b = a * b_sh + b
                a = a * a_sh
            hh = b + a * carry
            h_ref[rows, :] = hh
            gb = z_ref[rows, 4 * D_BR:5 * D_BR]
            yb = hh * (gb * _sig(gb))
            y_s[rows, D_BR:2 * D_BR] = yb * lax.rsqrt(_mean_last(yb * yb) + EPS) * gob
            return _bcast_row(hh, ROWS - 1)

        carry_s[...] = _loop(N_GROUP, phase3, carry_s[...])
        y_ref[...] = y_s[...].astype(BF16)

        @pl.when(c_id == n_chunk - 1)
        def _():
            ex.wait(ex_in, ex_out, ex_sems)

    vec = pl.BlockSpec((1, D_BR), lambda i: (0, 0))
    res = pl.pallas_call(
        body, name="mix_fwd", grid=(n_chunk,),
        in_specs=[pl.BlockSpec((CHUNK, D_IN), lambda i: (i, 0)), vec, vec,
                  pl.BlockSpec((N_HEAD, HEAD, HEAD), lambda i: (0, 0, 0)),
                  pl.BlockSpec((CHUNK, D_BR), lambda i: (0, 0)),
                  pl.BlockSpec((ROWS, D_BR), lambda i: (0, 0)), vec,
                  pl.BlockSpec((N_HEAD, HEAD, 2 * HEAD), lambda i: (0, 0, 0)), vec, vec, vec, vec, vec]
        + [ANY_SPEC] * ex.n,
        out_specs=[pl.BlockSpec((CHUNK, 2 * D_BR), lambda i: (i, 0)), pl.BlockSpec((CHUNK, D_BR), lambda i: (i, 0)),
                   pl.BlockSpec((CHUNK, D_BR), lambda i: (i, 0)), pl.BlockSpec((CHUNK, D_BR), lambda i: (i, 0)),
                   pl.BlockSpec((CHUNK, HEAD), lambda i: (i, 0))] + [ANY_SPEC] * ex.n,
        out_shape=[SDS((t_len, 2 * D_BR), BF16), SDS((t_len, D_BR), F32), SDS((t_len, D_BR), BF16),
                   SDS((t_len, D_BR), BF16), SDS((t_len, HEAD), F32)] + ex.out_shape,
        scratch_shapes=[pltpu.VMEM((CHUNK, D_BR), F32), pltpu.VMEM((CHUNK, D_BR), F32), pltpu.VMEM((CHUNK, D_BR), F32),
                        pltpu.VMEM((CHUNK, 2 * D_BR), F32), pltpu.VMEM((CHUNK, 2 * D_BR), F32),
                        pltpu.VMEM((ROWS, D_BR), F32), pltpu.VMEM((ROWS, D_BR), F32)] + ex.scratch,
        compiler_params=_params(("arbitrary",), 32),
    )(z, ln_g, ln_b, wm, bias, cw, cb, wax, ba, bx, lam, goa, gob, *ex_arrs)
    return res[:n_out], res[n_out:]


def _load_weight(w_hbm, w_vmem, sem):
    @pl.when(pl.program_id(0) == 0)
    def _():
        cp = pltpu.make_async_copy(w_hbm, w_vmem, sem)
        cp.start()
        cp.wait()


def _out_proj(y, x, w_out, post_g, tm=512):
    t_len = y.shape[0]

    def body(y_ref, x_ref, w_hbm, g_ref, h1_ref, ob_ref, w_s, o_s, sem):
        _load_weight(w_hbm, w_s, sem)
        o_s[...] = _dot(y_ref[...], w_s[...])
        g = g_ref[...]

        def rows_body(q, _):
            rows = _tile_rows(q)
            o = o_s[rows, :]
            h1_ref[rows, :] = x_ref[rows, :] + o * lax.rsqrt(_mean_last(o * o) + EPS) * g
            ob_ref[rows, :] = o.astype(BF16)
            return 0

        _loop(tm // TILE_ROWS, rows_body, 0, unroll=TILE_UNROLL)

    tile = pl.BlockSpec((tm, D_MODEL), lambda i: (i, 0))
    return pl.pallas_call(
        body, name="out_proj", grid=(t_len // tm,),
        in_specs=[tile, tile, pl.BlockSpec(memory_space=pl.ANY), pl.BlockSpec((1, D_MODEL), lambda i: (0, 0))],
        out_specs=[tile, tile],
        out_shape=[SDS((t_len, D_MODEL), F32), SDS((t_len, D_MODEL), BF16)],
        scratch_shapes=[pltpu.VMEM((D_MODEL, D_MODEL), BF16), pltpu.VMEM((tm, D_MODEL), F32), pltpu.SemaphoreType.DMA],
        compiler_params=_params(("arbitrary",), 44),
    )(y, x, w_out, post_g)


def _ple_loss(h1, p, tgt, w_pg, w_pe_g, tm=256):
    t_len = h1.shape[0]
    n_tile = t_len // tm
    pe_shard = D_MODEL // N_DEV

    def body(h1_ref, p_ref, t_ref, w_hbm, wpe_ref, dh2_ref, dgl_ref, h1b_ref, loss_ref, dwpe_ref, w_s, pe_s, gl_s, acc_s,
             dpe_s, gpe_s, sem):
        _load_weight(w_hbm, w_s, sem)
        i = pl.program_id(0)

        @pl.when(i == 0)
        def _():
            acc_s[...] = jnp.zeros_like(acc_s)
            gpe_s[...] = jnp.zeros_like(gpe_s)

        h1b_ref[...] = h1_ref[...].astype(BF16)
        pb = p_ref[...].astype(BF16)
        for j in range(N_DEV):
            pe_s[:, j * pe_shard:(j + 1) * pe_shard] = _dot(pb, wpe_ref[j])
        gl_s[...] = _dot(h1b_ref[...], w_s[...])

        def rows_body(q, acc):
            rows = _tile_rows(q)
            pe = pe_s[rows, :]
            g = _sig(gl_s[rows, :])
            e = h1_ref[rows, :] + pe * g - t_ref[rows, :]
            dh2 = e * (1.0 / D_MODEL)
            dh2_ref[rows, :] = dh2
            dpe_s[rows, :] = (dh2 * g).astype(BF16)
            dgl_ref[rows, :] = (dh2 * pe * g * (1.0 - g)).astype(BF16)
            return acc + _fold_rows(e * e)

        acc_s[...] = _loop(tm // TILE_ROWS, rows_body, acc_s[...], unroll=TILE_UNROLL)
        gpe_s[...] += _dot_tn(pb, dpe_s[...])

        @pl.when(i == n_tile - 1)
        def _():
            loss_ref[...] = jnp.full(loss_ref.shape, 0.5 / D_MODEL * jnp.sum(acc_s[...]), F32)
            for j in range(N_DEV):
                dwpe_ref[j] = gpe_s[:, j * pe_shard:(j + 1) * pe_shard].astype(BF16)

    tile = pl.BlockSpec((tm, D_MODEL), lambda i: (i, 0))
    pe_blocks = pl.BlockSpec((N_DEV, D_PLE, pe_shard), lambda i: (0, 0, 0))
    return pl.pallas_call(
        body, name="ple_loss", grid=(n_tile,),
        in_specs=[tile, pl.BlockSpec((tm, D_PLE), lambda i: (i, 0)), tile, pl.BlockSpec(memory_space=pl.ANY), pe_blocks],
        out_specs=[tile, tile, tile, pl.BlockSpec((ROWS, HEAD), lambda i: (0, 0)), pe_blocks],
        out_shape=[SDS((t_len, D_MODEL), F32), SDS((t_len, D_MODEL), BF16), SDS((t_len, D_MODEL), BF16),
                   SDS((ROWS, HEAD), F32), SDS((N_DEV, D_PLE, pe_shard), BF16)],
        scratch_shapes=[pltpu.VMEM((D_MODEL, D_MODEL), BF16), pltpu.VMEM((tm, D_MODEL), F32),
                        pltpu.VMEM((tm, D_MODEL), F32), pltpu.VMEM((ROWS, D_MODEL), F32), pltpu.VMEM((tm, D_MODEL), BF16),
                        pltpu.VMEM((D_PLE, D_MODEL), F32), pltpu.SemaphoreType.DMA],
        compiler_params=_params(("arbitrary",), 48),
    )(h1, p, tgt, w_pg, w_pe_g)


def _tail_bwd(dh2, dgl, ob, w_pg, w_out, post_g, tm=256):
    t_len = dh2.shape[0]
    n_tile = t_len // tm

    def body(dh2_ref, dgl_ref, ob_ref, wpg_hbm, wout_hbm, g_ref, dh1_ref, do_ref, dy_ref, dg_ref, wpg_s, wout_s, t_s,
             acc_s, sems):
        _load_weight(wpg_hbm, wpg_s, sems.at[0])
        _load_weight(wout_hbm, wout_s, sems.at[1])
        i = pl.program_id(0)

        @pl.when(i == 0)
        def _():
            acc_s[...] = jnp.zeros_like(acc_s)

        t_s[...] = _dot_nt(dgl_ref[...], wpg_s[...])
        g = g_ref[...]

        def rows_body(q, acc):
            rows = _tile_rows(q)
            dh1 = dh2_ref[rows, :] + t_s[rows, :]
            dh1_ref[rows, :] = dh1
            o = ob_ref[rows, :].astype(F32)
            rr = lax.rsqrt(_mean_last(o * o) + EPS)
            on = o * rr
            dog = dh1 * g
            do_ref[rows, :] = (rr * (dog - on * _mean_last(dog * on))).astype(BF16)
            return acc + _fold_rows(dh1 * on)

        acc_s[...] = _loop(tm // TILE_ROWS, rows_body, acc_s[...], unroll=TILE_UNROLL)
        dy_ref[...] = _dot_nt(do_ref[...], wout_s[...]).astype(BF16)

        @pl.when(i == n_tile - 1)
        def _():
            dg_ref[...] = jnp.sum(acc_s[...], axis=0, keepdims=True)

    tile = pl.BlockSpec((tm, D_MODEL), lambda i: (i, 0))
    vec = pl.BlockSpec((1, D_MODEL), lambda i: (0, 0))
    hbm = pl.BlockSpec(memory_space=pl.ANY)
    return pl.pallas_call(
        body, name="tail_bwd", grid=(n_tile,),
        in_specs=[tile, tile, tile, hbm, hbm, vec],
        out_specs=[tile, tile, tile, vec],
        out_shape=[SDS((t_len, D_MODEL), F32), SDS((t_len, D_MODEL), BF16), SDS((t_len, D_MODEL), BF16),
                   SDS((1, D_MODEL), F32)],
        scratch_shapes=[pltpu.VMEM((D_MODEL, D_MODEL), BF16), pltpu.VMEM((D_MODEL, D_MODEL), BF16),
                        pltpu.VMEM((tm, D_MODEL), F32), pltpu.VMEM((ROWS, D_MODEL), F32), pltpu.SemaphoreType.DMA((2,))],
        compiler_params=_params(("arbitrary",), 48),
    )(dh2, dgl, ob, w_pg, w_out, post_g)


def _mix_bwd(z, dy, h, vhb, xcb, rs, ln_g, ln_b, wm, wm_t, bias, cw, cb, wax, wax_t, ba, bx, lam, goa, gob, ex_arrs,
             ex_scatter):
    t_len = z.shape[0]
    n_chunk = t_len // CHUNK
    halo_blocks = CHUNK // ROWS
    ex = _Exchange(ex_arrs, ex_scatter)
    n_in, n_out, n_scratch = 21, 5, 16

    def body(*refs):
        (z_ref, dy_ref, h_ref, hhalo_ref, vhb_ref, xcb_ref, rs_ref, lng_ref, lnb_ref, wm_ref, wmt_ref, bias_ref, cw_ref,
         cb_ref, wax_ref, waxt_ref, ba_ref, bx_ref, lam_ref, goa_ref, gob_ref) = refs[:n_in]
        ex_in = refs[n_in:n_in + ex.n]
        dz_ref, vecs_ref, dws_ref, dwax_ref, dbs_ref = refs[n_in + ex.n:n_in + ex.n + n_out]
        ex_out = refs[n_in + ex.n + n_out:n_in + 2 * ex.n + n_out]
        (vnb_s, vh_s, xc_s, mixed_s, pre_s, dmix_s, dvn_s, dho_s, dxc_s, dpre_s, dz_s, acc_s, accdm_s,
         cg_s, ca_s, dxchalo_s) = refs[n_in + 2 * ex.n + n_out:n_in + 2 * ex.n + n_out + n_scratch]
        ex_sems = refs[n_in + 2 * ex.n + n_out + n_scratch:]
        step = pl.program_id(0)
        c_id = n_chunk - 1 - step
        rid = _row_ids(D_BR)
        first_chunk = c_id == 0

        @pl.when(step == 0)
        def _():
            ex.start(ex_in, ex_out, ex_sems)
            acc_s[...] = jnp.zeros_like(acc_s)
            accdm_s[...] = jnp.zeros_like(accdm_s)
            cg_s[...] = jnp.zeros_like(cg_s)
            ca_s[...] = jnp.zeros_like(ca_s)
            dxchalo_s[...] = jnp.zeros_like(dxchalo_s)
            dws_ref[...] = jnp.zeros_like(dws_ref)
            dwax_ref[...] = jnp.zeros_like(dwax_ref)

        lng, lnb = lng_ref[...], lnb_ref[...]
        h_halo = jnp.where(first_chunk, 0.0, hhalo_ref[...])

        def prev_rows(ref, cols, g, halo):
            before = ref[pl.ds(pl.multiple_of(jnp.maximum(g - 1, 0) * ROWS, ROWS), ROWS), cols]
            return jnp.where(g > 0, before, halo)

        vh_s[...] = vhb_ref[...].astype(F32)
        xc_s[...] = xcb_ref[...].astype(F32)

        for hd in range(N_HEAD):
            cs = slice(hd * HEAD, (hd + 1) * HEAD)
            vnb_s[:, cs] = (vh_s[:, cs] * lng[:, cs] + lnb[:, cs]).astype(BF16)
            mixed_s[:, cs] = _dot(wm_ref[hd], vnb_s[:, cs])
            pre = _dot(xcb_ref[:, cs], wax_ref[hd])
            pre_s[:, cs] = pre[:, :HEAD]
            pre_s[:, D_BR + hd * HEAD:D_BR + (hd + 1) * HEAD] = pre[:, HEAD:]

        goa, gob = goa_ref[...], gob_ref[...]

        def phase3(g, _):
            rows = _rows(g)
            ug, dug = _gelu(z_ref[rows, 0:D_BR], with_grad=True)
            ga = z_ref[rows, 2 * D_BR:3 * D_BR]
            sga = _sig(ga)
            sa = ga * sga
            mixed = mixed_s[rows, :] + bias_ref[rows, :]
            ya0 = ug * mixed
            ya = ya0 * sa
            ra = lax.rsqrt(_mean_last(ya * ya) + EPS)
            dyan = dy_ref[rows, 0:D_BR].astype(F32)
            acc_s[V_GOUT_A] += dyan * ya * ra
            dyg = dyan * goa
            dya = ra * dyg - ya * (ra * ra * ra) * _mean_last(dyg * ya)
            dya0 = dya * sa
            dz_s[rows, 2 * D_BR:3 * D_BR] = dya * ya0 * _silu_grad(sga, sa)
            dmix = dya0 * ug
            dmix_s[rows, :] = dmix
            accdm_s[rows, :] += dmix
            dz_s[rows, 0:D_BR] = dya0 * mixed * dug

            hh = h_ref[rows, :]
            gb = z_ref[rows, 4 * D_BR:5 * D_BR]
            sgb = _sig(gb)
            sb = gb * sgb
            yb = hh * sb
            rb = lax.rsqrt(_mean_last(yb * yb) + EPS)
            dybn = dy_ref[rows, D_BR:2 * D_BR].astype(F32)
            acc_s[V_GOUT_B] += dybn * yb * rb
            dyg = dybn * gob
            dyb = rb * dyg - yb * (rb * rb * rb) * _mean_last(dyg * yb)
            dho_s[rows, :] = dyb * sb
            dz_s[rows, 4 * D_BR:5 * D_BR] = dyb * hh * _silu_grad(sgb, sb)
            return 0

        _loop(N_GROUP, phase3, 0)

        for hd in range(N_HEAD):
            cs = slice(hd * HEAD, (hd + 1) * HEAD)
            dmb = dmix_s[:, cs].astype(BF16)
            dvn_s[:, cs] = _dot(wmt_ref[hd], dmb)
            dws_ref[hd] += _dot_nt(dmb, vnb_s[:, cs])

        def phase5(g, _):
            rows = _rows(g)
            dvn = dvn_s[rows, :]
            vh = vh_s[rows, :]
            acc_s[V_LN_G] += dvn * vh
            acc_s[V_LN_B] += dvn
            dvh = dvn * lng
            rs = rs_ref[rows, 0:1]
            dvg = rs * (dvh - _mean_last(dvh) - vh * _mean_last(dvh * vh))
            dz_s[rows, D_BR:2 * D_BR] = dvg * _gelu(z_ref[rows, D_BR:2 * D_BR], with_grad=True)[1]
            return 0

        _loop(N_GROUP, phase5, 0)

        ba, bx = ba_ref[...], bx_ref[...]
        sp8 = LRU_C * _softplus(-lam_ref[...])
        row0 = _row0_mask(rid)

        def phase6(k, carry):
            cg, ca = carry
            g = N_GROUP - 1 - k
            rows = _rows(g)
            bias0 = _row0_bias(jnp.logical_and(first_chunk, g == 0), row0)
            r, i, a, m2 = _lru_gates(pre_s[rows, 0:D_BR], pre_s[rows, D_BR:2 * D_BR], ba, bx, sp8, bias0)
            a_nx = jnp.where(rid < ROWS - 1, pltpu.roll(a, ROWS - 1, 0), ca)
            aa, bb = a_nx, dho_s[rows, :]
            for d in (1, 2, 4):
                a_sh = jnp.where(rid < ROWS - d, pltpu.roll(aa, ROWS - d, 0), 1.0)
                b_sh = jnp.where(rid < ROWS - d, pltpu.roll(bb, ROWS - d, 0), 0.0)
                bb = aa * b_sh + bb
                aa = aa * a_sh
            gg = bb + aa * cg
            hh = h_ref[rows, :]
            hprev = _shift_down(hh, prev_rows(h_ref, slice(None), g, h_halo), 1, rid)
            xc = xc_s[rows, :]
            gx = gg * xc
            dla = gg * hprev * a - gx * i * (a * a) * lax.rsqrt(m2)
            acc_s[V_LAM] += -(dla * r)
            dpa = -(dla * sp8) * r * (1.0 - r)
            mi = jnp.sqrt(m2) * i
            dpx = gx * mi * (1.0 - i)
            acc_s[V_B_A] += dpa
            acc_s[V_B_X] += dpx
            dpre_s[rows, 0:D_BR] = dpa
            dpre_s[rows, D_BR:2 * D_BR] = dpx
            dxc_s[rows, :] = gg * mi
            return _bcast_row(gg, 0), _bcast_row(a, 0)

        cg, ca = _loop(N_GROUP, phase6, (cg_s[...], ca_s[...]))
        cg_s[...] = cg
        ca_s[...] = ca

        for hd in range(N_HEAD):
            cs = slice(hd * HEAD, (hd + 1) * HEAD)
            dpre = jnp.concatenate([dpre_s[:, cs], dpre_s[:, D_BR + hd * HEAD:D_BR + (hd + 1) * HEAD]], axis=1).astype(BF16)
            dxc_s[:, cs] += _dot(dpre, waxt_ref[hd])
            dwax_ref[hd] += _dot_tn(xcb_ref[:, cs], dpre)

        def phase8(k, nxt):
            g = N_GROUP - 1 - k
            rows = _rows(g)
            dxc = dxc_s[rows, :]
            acc_s[V_CONV_B] += dxc
            xb = z_ref[rows, 3 * D_BR:4 * D_BR]
            dxb = cw_ref[3:4, :] * dxc
            acc_s[V_CONV_W + 3] += dxc * xb
            for j in range(1, CONV_W):
                later = _shift_up(dxc, nxt, j, rid)
                dxb = dxb + cw_ref[3 - j:4 - j, :] * later
                acc_s[V_CONV_W + 3 - j] += later * xb
            dz_s[rows, 3 * D_BR:4 * D_BR] = dxb
            return dxc

        dxchalo_s[...] = _loop(N_GROUP, phase8, dxchalo_s[...])
        dz_ref[...] = dz_s[...].astype(BF16)

        @pl.when(step == n_chunk - 1)
        def _():
            for v in range(N_VEC):
                vecs_ref[v:v + 1, :] = jnp.sum(acc_s[v], axis=0, keepdims=True)
            lam = lam_ref[...]
            vecs_ref[V_LAM:V_LAM + 1, :] = vecs_ref[V_LAM:V_LAM + 1, :] * (-LRU_C * _sig(-lam))
            tril = (lax.broadcasted_iota(jnp.int32, (HEAD, HEAD), 0) >= lax.broadcasted_iota(jnp.int32, (HEAD, HEAD), 1))
            ones = jnp.ones((ROWS, HEAD), BF16)
            for hd in range(N_HEAD):
                cs = slice(hd * HEAD, (hd + 1) * HEAD)
                dws_ref[hd] = jnp.where(tril, dws_ref[hd], 0.0)
                blk = accdm_s[:, cs]
                hi = blk.astype(BF16)
                lo = (blk - hi.astype(F32)).astype(BF16)
                dbs_ref[hd:hd + 1, :] = (_dot_nt(ones, hi) + _dot_nt(ones, lo))[0:1, :]
            ex.wait(ex_in, ex_out, ex_sems)

    vec = pl.BlockSpec((1, D_BR), lambda i: (0, 0))
    rev = lambda i: (n_chunk - 1 - i, 0)
    halo = lambda col: (lambda i: (jnp.maximum((n_chunk - 1 - i) * halo_blocks - 1, 0), col))
    full3 = lambda a, b, c: pl.BlockSpec((a, b, c), lambda i: (0, 0, 0))
    big = lambda w: pltpu.VMEM((CHUNK, w), F32)
    res = pl.pallas_call(
        body, name="mix_bwd", grid=(n_chunk,),
        in_specs=[pl.BlockSpec((CHUNK, D_IN), rev), pl.BlockSpec((CHUNK, 2 * D_BR), rev), pl.BlockSpec((CHUNK, D_BR), rev),
                  pl.BlockSpec((ROWS, D_BR), halo(0)), pl.BlockSpec((CHUNK, D_BR), rev), pl.BlockSpec((CHUNK, D_BR), rev),
                  pl.BlockSpec((CHUNK, HEAD), rev), vec, vec,
                  full3(N_HEAD, HEAD, HEAD), full3(N_HEAD, HEAD, HEAD),
                  pl.BlockSpec((CHUNK, D_BR), lambda i: (0, 0)), pl.BlockSpec((ROWS, D_BR), lambda i: (0, 0)), vec,
                  full3(N_HEAD, HEAD, 2 * HEAD), full3(N_HEAD, 2 * HEAD, HEAD), vec, vec, vec, vec, vec]
        + [ANY_SPEC] * ex.n,
        out_specs=[pl.BlockSpec((CHUNK, D_IN), rev), pl.BlockSpec((N_VEC, D_BR), lambda i: (0, 0)),
                   full3(N_HEAD, HEAD, HEAD), full3(N_HEAD, HEAD, 2 * HEAD),
                   pl.BlockSpec((N_HEAD, HEAD), lambda i: (0, 0))] + [ANY_SPEC] * ex.n,
        out_shape=[SDS((t_len, D_IN), BF16), SDS((N_VEC, D_BR), F32), SDS((N_HEAD, HEAD, HEAD), F32),
                   SDS((N_HEAD, HEAD, 2 * HEAD), F32), SDS((N_HEAD, HEAD), F32)] + ex.out_shape,
        scratch_shapes=[pltpu.VMEM((CHUNK, D_BR), BF16), big(D_BR), big(D_BR), big(D_BR), big(2 * D_BR), big(D_BR),
                        big(D_BR), big(D_BR), big(D_BR), big(2 * D_BR), big(D_IN),
                        pltpu.VMEM((N_VEC, ROWS, D_BR), F32), big(D_BR),
                        pltpu.VMEM((ROWS, D_BR), F32), pltpu.VMEM((ROWS, D_BR), F32), pltpu.VMEM((ROWS, D_BR), F32)]
        + ex.scratch,
        compiler_params=_params(("arbitrary",), 48),
    )(z, dy, h, h, vhb, xcb, rs, ln_g, ln_b, wm, wm_t, bias, cw, cb, wax, wax_t, ba, bx, lam, goa, gob, *ex_arrs)
    return res[:n_out], res[n_out:]


def _in_bwd(dz, w_in_g, x, dh1, pre_g, first_tile, n_tile, prev, name, ex_arrs=(), ex_scatter=(), tm=256):
    t_len = x.shape[0]
    ex = _Exchange(ex_arrs, ex_scatter)
    n_prev = 0 if prev is None else 2

    def body(dz_ref, w_hbm, x_ref, dh1_ref, g_ref, *refs):
        prev_refs, refs = refs[:n_prev], refs[n_prev:]
        ex_in, (gx_ref, dg_ref), ex_out = refs[:ex.n], refs[ex.n:ex.n + 2], refs[ex.n + 2:2 * ex.n + 2]
        w_s, t_s, dg_s, w_sems = refs[2 * ex.n + 2:2 * ex.n + 6]
        ex_sems = refs[2 * ex.n + 6:]
        i = pl.program_id(0)

        @pl.when(i == 0)
        def _():
            if ex.n:
                ex.start(ex_in, ex_out, ex_sems)
            loads = [pltpu.make_async_copy(w_hbm.at[s], w_s.at[:, s * W_IN_SHARD:(s + 1) * W_IN_SHARD], w_sems.at[s])
                     for s in range(N_DEV)]
            for cp in loads:
                cp.start()
            dg_s[...] = jnp.zeros_like(dg_s)
            for cp in loads:
                cp.wait()

        t_s[...] = _dot_nt(dz_ref[...], w_s[...])
        g = g_ref[...]

        def rows_body(q, acc):
            rows = _tile_rows(q)
            xv = x_ref[rows, :]
            r = lax.rsqrt(_mean_last(xv * xv) + EPS)
            xh = xv * r
            dhn = t_s[rows, :]
            dg = dhn * g
            gx_ref[rows, :] = dh1_ref[rows, :] + r * (dg - xh * _mean_last(dg * xh))
            return acc + _fold_rows(dhn * xh)

        dg_s[...] = _loop(tm // TILE_ROWS, rows_body, dg_s[...], unroll=TILE_UNROLL)

        @pl.when(i == n_tile - 1)
        def _():
            dg = jnp.sum(dg_s[...], axis=0, keepdims=True)
            dg_ref[...] = dg + prev_refs[1][...] if n_prev else dg
            if ex.n:
                ex.wait(ex_in, ex_out, ex_sems)

    tile = pl.BlockSpec((tm, D_MODEL), lambda i: (first_tile + i, 0))
    vec = pl.BlockSpec((1, D_MODEL), lambda i: (0, 0))
    prev_specs = [ANY_SPEC, vec] if n_prev else []
    res = pl.pallas_call(
        body, name=name, grid=(n_tile,),
        in_specs=[pl.BlockSpec((tm, D_IN), lambda i: (first_tile + i, 0)), ANY_SPEC, tile, tile, vec] + prev_specs
        + [ANY_SPEC] * ex.n,
        out_specs=[tile, vec] + [ANY_SPEC] * ex.n,
        out_shape=[SDS((t_len, D_MODEL), F32), SDS((1, D_MODEL), F32)] + ex.out_shape,
        scratch_shapes=[pltpu.VMEM((D_MODEL, D_IN), BF16), pltpu.VMEM((tm, D_MODEL), F32), pltpu.VMEM((ROWS, D_MODEL), F32),
                        pltpu.SemaphoreType.DMA((N_DEV,))] + (ex.scratch if ex.n else []),
        input_output_aliases={5: 0} if n_prev else {},
        compiler_params=_params(("arbitrary",), 54),
    )(dz, w_in_g, x, dh1, pre_g, *(prev or ()), *ex_arrs)
    return res[0], res[1], res[2:]


def _grad_w(a, b, bn, shard_major, name, tk=1024, ex_arrs=(), ex_scatter=()):
    t_len, m = a.shape
    n = b.shape[1]
    n_j, n_k = n // bn, t_len // tk
    ex = _Exchange(ex_arrs, ex_scatter)

    def body(a_ref, b_ref, *refs):
        ex_in, o_ref, ex_out = refs[:ex.n], refs[ex.n], refs[ex.n + 1:2 * ex.n + 1]
        acc_s, ex_sems = refs[2 * ex.n + 1], refs[2 * ex.n + 2:]
        j, k = pl.program_id(0), pl.program_id(1)
        if ex.n:
            @pl.when(jnp.logical_and(j == 0, k == 0))
            def _():
                ex.start(ex_in, ex_out, ex_sems)

        @pl.when(k == 0)
        def _():
            acc_s[...] = jnp.zeros_like(acc_s)

        acc_s[...] += _dot_tn(a_ref[...], b_ref[...])

        @pl.when(k == n_k - 1)
        def _():
            o_ref[...] = acc_s[...].astype(BF16)

        if ex.n:
            @pl.when(jnp.logical_and(j == n_j - 1, k == n_k - 1))
            def _():
                ex.wait(ex_in, ex_out, ex_sems)

    if shard_major:
        out_spec, out_shape = pl.BlockSpec((None, m, bn), lambda j, k: (j, 0, 0)), SDS((n_j, m, bn), BF16)
    else:
        out_spec, out_shape = pl.BlockSpec((m, bn), lambda j, k: (0, j)), SDS((m, n), BF16)
    res = pl.pallas_call(
        body, name=name, grid=(n_j, n_k),
        in_specs=[pl.BlockSpec((tk, m), lambda j, k: (k, 0)), pl.BlockSpec((tk, bn), lambda j, k: (k, j))]
        + [ANY_SPEC] * ex.n,
        out_specs=[out_spec] + [ANY_SPEC] * ex.n, out_shape=[out_shape] + ex.out_shape,
        scratch_shapes=[pltpu.VMEM((m, bn), F32)] + (ex.scratch if ex.n else []),
        compiler_params=_params(("arbitrary", "arbitrary"), 40),
    )(a, b, *ex_arrs)
    return res[0], res[1:]


RS_CHIPS = (6, 2, 4, 0)
RS_SLOTS = (0, 1, 2, 4, 6)


def _grad_w_in_pairs(hn, dz, ex_arrs, ex_scatter, tk=1024):
    t_len = hn.shape[0]
    n_k = t_len // tk
    n_ph = len(RS_CHIPS)
    ex = _Exchange(ex_arrs, ex_scatter)
    me_out = 4 * lax.axis_index("x") + 2 * lax.axis_index("y") + lax.axis_index("c")
    order = jnp.stack([(me_out ^ chip) // 2 for chip in RS_CHIPS]).astype(jnp.int32)
    slots = jnp.stack([me_out ^ k for k in RS_SLOTS]).astype(jnp.int32)
    shard = W_IN_SHARD

    def body(order_ref, a_ref, b_ref, *refs):
        ex_in, parts_hbm, ex_out = refs[:ex.n], refs[ex.n], refs[ex.n + 1:2 * ex.n + 1]
        (acc_s, tb_s, stage_s, rx_s, d2d_send, d2d_recv, ici_send, ici_recv, sib_sems,
         loc_sem) = refs[2 * ex.n + 1:2 * ex.n + 11]
        ex_sems = refs[2 * ex.n + 11:]
        j, k = pl.program_id(0), pl.program_id(1)
        x, y, c, me = _mesh_place()
        sib = _peer(x, y, c, SIBLING)[0]

        def to_sibling(p):
            return _remote(stage_s.at[0], rx_s.at[p % 2], d2d_send.at[p], d2d_recv.at[p], sib)

        def over_ici(p):
            dev = _peer(x, y, c, RS_CHIPS[p])[0]
            return _remote(stage_s.at[1], parts_hbm.at[me], ici_send.at[p], ici_recv.at[p], dev)

        def own_chip():
            return (_remote(stage_s.at[0], parts_hbm.at[me], sib_sems.at[0], sib_sems.at[1], sib),
                    pltpu.make_async_copy(stage_s.at[1], parts_hbm.at[me], loc_sem.at[0]))

        @pl.when(jnp.logical_and(j == 0, k == 0))
        def _():
            ex.start(ex_in, ex_out, ex_sems)

        for p in range(n_ph - 1):
            for core in (0, 1):
                @pl.when(jnp.logical_and(jnp.logical_and(j == p + 1, k == 0), c == core))
                def _(p=p, core=core):
                    to_sibling(p).wait_recv()
                    if p >= 1:
                        over_ici(p - 1).wait_send()
                    mine = acc_s[:, core * shard:(core + 1) * shard]
                    stage_s[1] = (mine + rx_s[p % 2].astype(F32)).astype(BF16)
                    over_ici(p).start()

        @pl.when(k == 0)
        def _():
            acc_s[...] = jnp.zeros_like(acc_s)

        a = a_ref[...]
        acc_s[:, 0:W_BODY] += _dot_tn(a, b_ref[:, 0:W_BODY])
        acc_s[:, shard:shard + W_BODY] += _dot_tn(a, b_ref[:, shard:shard + W_BODY])
        tb_s[:, 0:W_TAIL] = b_ref[:, W_BODY:shard]
        tb_s[:, W_TAIL:2 * W_TAIL] = b_ref[:, shard + W_BODY:2 * shard]
        tails = _dot_tn(a, tb_s[...])
        acc_s[:, W_BODY:shard] += tails[:, 0:W_TAIL]
        acc_s[:, shard + W_BODY:2 * shard] += tails[:, W_TAIL:2 * W_TAIL]

        for p in range(n_ph):
            for core in (0, 1):
                @pl.when(jnp.logical_and(jnp.logical_and(j == p, k == n_k - 1), c == core))
                def _(p=p, core=core):
                    same = acc_s[:, core * shard:(core + 1) * shard]
                    other = acc_s[:, (1 - core) * shard:(2 - core) * shard]
                    if p >= 1:
                        to_sibling(p - 1).wait_send()
                    stage_s[0] = other.astype(BF16)
                    if p < n_ph - 1:
                        to_sibling(p).start()
                    else:
                        over_ici(n_ph - 2).wait_send()
                        stage_s[1] = same.astype(BF16)
                        for cp in own_chip():
                            cp.start()

        @pl.when(jnp.logical_and(j == n_ph - 1, k == n_k - 1))
        def _():
            to_sib, local = own_chip()
            to_sib.wait_send()
            local.wait()
            _remote(stage_s.at[0], parts_hbm.at[_peer(x, y, c, SIBLING)[1]], sib_sems.at[0], sib_sems.at[1], sib).wait_recv()
            for p in range(n_ph - 1):
                dev, lin = _peer(x, y, c, RS_CHIPS[p])
                _remote(stage_s.at[0], parts_hbm.at[lin], ici_send.at[p], ici_recv.at[p], dev).wait_recv()
            ex.wait(ex_in, ex_out, ex_sems)

    dma = lambda n: pltpu.SemaphoreType.DMA((n,))
    grid_spec = pltpu.PrefetchScalarGridSpec(
        num_scalar_prefetch=1, grid=(n_ph, n_k),
        in_specs=[pl.BlockSpec((tk, D_MODEL), lambda j, k, order: (k, 0)),
                  pl.BlockSpec((tk, 2 * shard), lambda j, k, order: (k, order[j]))] + [ANY_SPEC] * ex.n,
        out_specs=[ANY_SPEC] * (1 + ex.n),
        scratch_shapes=[pltpu.VMEM((D_MODEL, 2 * shard), F32), pltpu.VMEM((tk, 2 * W_TAIL), BF16),
                        pltpu.VMEM((2, D_MODEL, shard), BF16),
                        pltpu.VMEM((2, D_MODEL, shard), BF16), dma(n_ph - 1), dma(n_ph - 1), dma(n_ph - 1),
                        dma(n_ph - 1), dma(2), dma(1)] + ex.scratch)
    res = pl.pallas_call(
        body, name="grad_w_in", grid_spec=grid_spec,
        out_shape=[SDS((N_DEV, D_MODEL, shard), BF16)] + ex.out_shape,
        compiler_params=_params(("arbitrary", "arbitrary"), 54),
    )(order, hn, dz, *ex_arrs)
    return res[0], slots, res[1:]


def _sum_parts(parts, name):
    def body(p_ref, o_ref):
        g = p_ref[0].astype(F32)
        for s in range(1, parts.shape[0]):
            g = g + p_ref[s].astype(F32)
        o_ref[...] = g

    return pl.pallas_call(body, name=name, out_shape=SDS(parts.shape[1:], F32))(parts)


def _adamw_math(g, w_ref, m_ref, v_ref, g_ref, d_ref, nm_ref, nv_ref):
    c1 = 1.0 - ADAM_B1 ** ADAM_STEP
    c2 = 1.0 - ADAM_B2 ** ADAM_STEP
    g_ref[...] = g
    nm = ADAM_B1 * m_ref[...] + (1.0 - ADAM_B1) * g
    nv = ADAM_B2 * v_ref[...] + (1.0 - ADAM_B2) * (g * g)
    nm_ref[...] = nm
    nv_ref[...] = nv
    d_ref[...] = -ADAM_LR * ((nm / c1) / (jnp.sqrt(nv / c2) + ADAM_EPS) + ADAM_WD * w_ref[...])


def _adamw(parts, w, m, v, name, tr):
    rows, cols = w.shape
    n_parts = parts.shape[0]

    def body(p_ref, *refs):
        g = p_ref[0].astype(F32)
        for s in range(1, n_parts):
            g = g + p_ref[s].astype(F32)
        _adamw_math(g, *refs)

    tile = pl.BlockSpec((tr, cols), lambda i: (i, 0))
    return pl.pallas_call(
        body, name=name, grid=(rows // tr,),
        in_specs=[pl.BlockSpec((n_parts, tr, cols), lambda i: (0, i, 0)), tile, tile, tile],
        out_specs=[tile] * 4, out_shape=[SDS((rows, cols), F32)] * 4,
        compiler_params=_params(("arbitrary",), 40),
    )(parts, w, m, v)


def _adamw_unpacked(grads, triples, name):
    n = len(triples)
    n_rows = [t[0].shape[0] for t in triples]

    def body(g_ref, *refs):
        ins, outs = refs[:3 * n], refs[3 * n:]
        row = 0
        for i in range(n):
            _adamw_math(g_ref[row:row + n_rows[i], :], *ins[3 * i:3 * i + 3], *outs[4 * i:4 * i + 4])
            row += n_rows[i]
        outs[4 * n][...] = g_ref[row:row + ROWS, :]

    out_shape = [SDS((r, LANES), F32) for r in n_rows for _ in range(4)] + [SDS((ROWS, LANES), F32)]
    return pl.pallas_call(
        body, name=name, out_shape=out_shape,
        compiler_params=pltpu.CompilerParams(vmem_limit_bytes=40 * MIB),
    )(grads, *[a for t in triples for a in t])


def _adamw_slots(parts, slots, w, m, v, name, tr):
    rows, cols = w.shape
    n_slots = slots.shape[0]

    def body(slots_ref, *refs):
        g = refs[0][...].astype(F32)
        for s in range(1, n_slots):
            g = g + refs[s][...].astype(F32)
        _adamw_math(g, *refs[n_slots:])

    tile = pl.BlockSpec((tr, cols), lambda i, slots: (i, 0))
    part = lambda s: pl.BlockSpec((None, tr, cols), lambda i, slots: (slots[s], i, 0))
    grid_spec = pltpu.PrefetchScalarGridSpec(
        num_scalar_prefetch=1, grid=(rows // tr,),
        in_specs=[part(s) for s in range(n_slots)] + [tile, tile, tile], out_specs=[tile] * 4)
    return pl.pallas_call(
        body, name=name, grid_spec=grid_spec, out_shape=[SDS((rows, cols), F32)] * 4,
        compiler_params=_params(("arbitrary",), 40),
    )(slots, *([parts] * n_slots), w, m, v)


PACKED = ("gmlp_ln_g", "gmlp_ln_b", "gmlp_ws", "gmlp_bs", "conv_b", "w_a", "b_a", "w_x", "b_x", "lam", "gmlp_out_g",
          "lru_out_g", "post_g")
WEIGHTS = ("pre_g", "w_in", "gmlp_ln_g", "gmlp_ln_b", "gmlp_ws", "gmlp_bs", "conv_w", "conv_b", "w_a", "b_a", "w_x",
           "b_x", "lam", "gmlp_out_g", "lru_out_g", "w_out", "post_g", "w_pe", "w_pg")
LANES = 128


PACK_ROWS = 3200
IN_BWD_TILE = 256


def _pack(parts):
    rows = [p.reshape(-1, LANES) for p in parts]
    used = sum(r.shape[0] for r in rows)
    return jnp.concatenate(rows + [jnp.zeros((PACK_ROWS - used, LANES), F32)], axis=0)


def _pad_rows(a, rows):
    return jnp.concatenate([a, jnp.zeros((rows - a.shape[0],) + a.shape[1:], a.dtype)], axis=0)


def kernel(x, p, pre_g, w_in, gmlp_ln_g, gmlp_ln_b, gmlp_ws, gmlp_bs, conv_w, conv_b, w_a, b_a, w_x, b_x, lam, gmlp_out_g, lru_out_g, w_out, post_g, w_pe, w_pg, loss_target, m_pre_g, m_w_in, m_gmlp_ln_g, m_gmlp_ln_b, m_gmlp_ws, m_gmlp_bs, m_conv_w, m_conv_b, m_w_a, m_b_a, m_w_x, m_b_x, m_lam, m_gmlp_out_g, m_lru_out_g, m_w_out, m_post_g, m_w_pe, m_w_pg, v_pre_g, v_w_in, v_gmlp_ln_g, v_gmlp_ln_b, v_gmlp_ws, v_gmlp_bs, v_conv_w, v_conv_b, v_w_a, v_b_a, v_w_x, v_b_x, v_lam, v_gmlp_out_g, v_lru_out_g, v_w_out, v_post_g, v_w_pe, v_w_pg):
    args = dict(locals())
    weights = {n: args[n] for n in WEIGHTS}
    m_in = {n: args["m_" + n] for n in WEIGHTS}
    v_in = {n: args["v_" + n] for n in WEIGHTS}
    sm = {n: weights[n][0] for n in PACKED}
    shard_rows = D_MODEL // N_DEV
    xs, ps, tgt = x[0], p[0, 0], loss_target[0]

    vec = lambda a: a.reshape(1, -1)
    tril = jnp.tril(jnp.ones((CHUNK, CHUNK), dtype=bool))
    wm32 = jnp.where(tril[None], sm["gmlp_ws"], 0.0)
    wm, wm_t = wm32.astype(BF16), jnp.swapaxes(wm32, 1, 2).astype(BF16)
    bias = jnp.repeat(sm["gmlp_bs"].T, HEAD, axis=1)
    wax32 = jnp.concatenate([sm["w_a"], sm["w_x"]], axis=2)
    wax, wax_t = wax32.astype(BF16), jnp.swapaxes(wax32, 1, 2).astype(BF16)
    ln_g, ln_b = vec(sm["gmlp_ln_g"]), vec(sm["gmlp_ln_b"])
    post_g_v = vec(sm["post_g"])

    hn = _pre_norm(xs, pre_g)
    cw_shard = _pad_rows(conv_w.reshape(CONV_W, HEAD), ROWS)
    z, w_in_g, (cw_g,) = _in_proj(hn, w_in[0].astype(BF16), [cw_shard])
    cw_full = jnp.transpose(cw_g[:, :CONV_W, :], (1, 0, 2)).reshape(CONV_W, D_BR)
    mixer_consts = dict(cw=_pad_rows(cw_full, ROWS), cb=vec(sm["conv_b"]), ba=vec(sm["b_a"]), bx=vec(sm["b_x"]),
                        lam=vec(sm["lam"]), goa=vec(sm["gmlp_out_g"]), gob=vec(sm["lru_out_g"]))
    (y, h, vhb, xcb, v_rs), (w_out_g, w_pe_g, w_pg_g) = _mix_fwd(
        z, ln_g, ln_b, wm, bias, wax=wax, **mixer_consts,
        ex_arrs=[w_out[0].astype(BF16), w_pe[0].astype(BF16), w_pg[0].astype(BF16)], ex_scatter=[False, False, False])
    w_out_f, w_pg_f = w_out_g.reshape(D_MODEL, D_MODEL), w_pg_g.reshape(D_MODEL, D_MODEL)
    h1, ob = _out_proj(y, xs, w_out_f, post_g_v)
    dh2, dgl, h1b, loss_part, d_w_pe = _ple_loss(h1, ps, tgt, w_pg_f, w_pe_g)

    dh1, do, dy, d_post_g = _tail_bwd(dh2, dgl, ob, w_pg_f, w_out_f, post_g_v)
    d_w_out, _ = _grad_w(y, do, 1024, False, "grad_w_out")
    d_w_pg, _ = _grad_w(h1b, dgl, 1024, False, "grad_w_pg")
    (dz, vecs, d_ws, d_wax, d_bs), (parts_out, parts_pg, parts_pe) = _mix_bwd(
        z, dy, h, vhb, xcb, v_rs, ln_g, ln_b, wm, wm_t, bias, wax=wax, wax_t=wax_t, **mixer_consts,
        ex_arrs=[d_w_out.reshape(N_DEV, shard_rows, D_MODEL), d_w_pg.reshape(N_DEV, shard_rows, D_MODEL), d_w_pe],
        ex_scatter=[True, True, True])

    small = {"gmlp_ln_g": vecs[V_LN_G], "gmlp_ln_b": vecs[V_LN_B], "gmlp_ws": d_ws, "gmlp_bs": d_bs,
             "conv_b": vecs[V_CONV_B], "w_a": d_wax[:, :, :HEAD], "b_a": vecs[V_B_A], "w_x": d_wax[:, :, HEAD:],
             "b_x": vecs[V_B_X], "lam": vecs[V_LAM], "gmlp_out_g": vecs[V_GOUT_A], "lru_out_g": vecs[V_GOUT_B],
             "post_g": d_post_g}
    small_part = _pack([small[n] for n in PACKED] + [loss_part]).reshape(N_DEV, PACK_ROWS // N_DEV, LANES)
    d_cw_blocks = jnp.transpose(vecs[V_CONV_W:V_CONV_W + CONV_W].reshape(CONV_W, N_DEV, HEAD), (1, 0, 2))
    d_cw_blocks = jnp.concatenate([d_cw_blocks, jnp.zeros((N_DEV, ROWS - CONV_W, HEAD), F32)], axis=1)
    parts_in, slots_in, (small_blocks, parts_cw) = _grad_w_in_pairs(
        hn, dz, ex_arrs=[small_part, d_cw_blocks], ex_scatter=[True, True])
    small_sum = _sum_parts(small_blocks, "sum_small")
    grad_x, d_pre_g, _ = _in_bwd(dz, w_in_g, xs, dh1, pre_g, 0, xs.shape[0] // IN_BWD_TILE, None, "in_bwd",
                                 tm=IN_BWD_TILE)
    pre_rows = D_MODEL // LANES
    small_all, parts_pre = _exchange([small_sum, d_pre_g.reshape(pre_rows, LANES)], False, "gather_small_grads")

    pad_cw = lambda a: _pad_rows(a.reshape(CONV_W, HEAD), ROWS)
    flat = lambda a: a.reshape(pre_rows, LANES)
    outs = {
        "w_in": _adamw_slots(parts_in, slots_in, w_in[0], m_w_in[0], v_w_in[0], "adamw_w_in", 256),
        "w_out": _adamw(parts_out, w_out[0], m_w_out[0], v_w_out[0], "adamw_w_out", 128),
        "w_pe": _adamw(parts_pe, w_pe[0], m_w_pe[0], v_w_pe[0], "adamw_w_pe", 256),
        "w_pg": _adamw(parts_pg, w_pg[0], m_w_pg[0], v_w_pg[0], "adamw_w_pg", 128),
        "conv_w": [a[:CONV_W] for a in
                   _adamw(parts_cw, pad_cw(conv_w), pad_cw(m_conv_w), pad_cw(v_conv_w), "adamw_conv_w", ROWS)],
        "pre_g": _adamw(parts_pre, flat(pre_g), flat(m_pre_g), flat(v_pre_g), "adamw_pre_g", pre_rows),
    }
    as_rows = lambda a: a.reshape(-1, LANES)
    small_res = _adamw_unpacked(small_all.reshape(PACK_ROWS, LANES),
                                [(as_rows(weights[n]), as_rows(m_in[n]), as_rows(v_in[n])) for n in PACKED], "adamw_small")
    for i, n in enumerate(PACKED):
        outs[n] = small_res[4 * i:4 * i + 4]
    loss = small_res[-1][0, 0]

    result = [loss, grad_x[None]]
    for q in range(4):
        result += [outs[n][q].reshape(weights[n].shape) for n in WEIGHTS]
    return tuple(result)
```

```python
import jax
import jax.numpy as jnp
from jax import lax
from jax.experimental import pallas as pl
from jax.experimental.pallas import tpu as pltpu

F32 = jnp.float32
BF16 = jnp.bfloat16
SDS = jax.ShapeDtypeStruct

D_MODEL = 2048
D_BR = 1024
D_IN = 5 * D_BR
D_PLE = 256
N_HEAD = 8
HEAD = 128
CHUNK = 128
ROWS = 8
N_GROUP = CHUNK // ROWS
N_DEV = 8
W_IN_SHARD = D_IN // N_DEV
EPS = 1e-6
LRU_C = 8.0
CONV_W = 4
MIB = 1 << 20

ADAM_LR, ADAM_B1, ADAM_B2, ADAM_EPS, ADAM_WD, ADAM_STEP = 0.001, 0.9, 0.999, 1e-08, 0.01, 10

_GELU_C = 0.7978845608028654
_GELU_A = 0.044715

V_LN_G, V_LN_B, V_CONV_B, V_B_A, V_B_X, V_LAM, V_GOUT_A, V_GOUT_B, V_CONV_W = 0, 1, 2, 3, 4, 5, 6, 7, 8
N_VEC = 16


def _params(sem, vmem_mib):
    return pltpu.CompilerParams(dimension_semantics=sem, vmem_limit_bytes=int(vmem_mib * MIB))


def _sig(x):
    return 0.5 * jnp.tanh(0.5 * x) + 0.5


def _gelu(x, with_grad=False):
    sq = x * x
    t = jnp.tanh(x * (_GELU_C + (_GELU_C * _GELU_A) * sq))
    half, one_t = 0.5 * x, 1.0 + t
    if not with_grad:
        return half * one_t
    grad = 0.5 * one_t + half * ((1.0 - t) * one_t) * (_GELU_C + (3.0 * _GELU_C * _GELU_A) * sq)
    return half * one_t, grad


def _silu_grad(s, xs):
    return s + xs * (1.0 - s)


def _neg_expm1(y, exp_y):
    series = -y * (1.0 + y * (0.5 + y * (1.0 / 6.0)))
    return jnp.where(y > -0.01, series, 1.0 - exp_y)


def _softplus(x):
    return jnp.maximum(x, 0.0) + jnp.log(1.0 + jnp.exp(-jnp.abs(x)))


def _row_ids(width):
    return lax.broadcasted_iota(jnp.int32, (ROWS, width), 0)


def _shift_down(cur, prev, k, rid):
    return jnp.where(rid >= k, pltpu.roll(cur, k, 0), pltpu.roll(prev, k, 0))


def _shift_up(cur, nxt, k, rid):
    return jnp.where(rid < ROWS - k, pltpu.roll(cur, ROWS - k, 0), pltpu.roll(nxt, ROWS - k, 0))


def _mean_last(x):
    return jnp.mean(x, axis=-1, keepdims=True)


def _rows(g):
    return pl.ds(pl.multiple_of(g * ROWS, ROWS), ROWS)


TILE_ROWS = 16


def _tile_rows(q):
    return pl.ds(pl.multiple_of(q * TILE_ROWS, TILE_ROWS), TILE_ROWS)


UNROLL = 4
TILE_UNROLL = 8


def _loop(n, body, init, unroll=UNROLL):
    def wide(i, carry):
        for u in range(unroll):
            carry = body(i * unroll + u, carry)
        return carry

    return lax.fori_loop(0, n // unroll, wide, init)


def _fold_rows(x):
    return x[0:ROWS, :] + x[ROWS:TILE_ROWS, :]


def _bcast_row(x, r):
    return jnp.broadcast_to(x[r:r + 1, :], x.shape)


def _dot(a, b):
    return jnp.dot(a, b, preferred_element_type=F32)


def _dot_nt(a, b):
    return lax.dot_general(a, b, (((1,), (1,)), ((), ())), preferred_element_type=F32)


def _dot_tn(a, b):
    return lax.dot_general(a, b, (((0,), (0,)), ((), ())), preferred_element_type=F32)


def _mesh_place():
    x, y, c = lax.axis_index("x"), lax.axis_index("y"), lax.axis_index("c")
    return x, y, c, 4 * x + 2 * y + c


def _peer(x, y, c, k):
    px = 1 - x if k & 4 else x
    py = 1 - y if k & 2 else y
    pc = 1 - c if k & 1 else c
    return (px, py, pc), 4 * px + 2 * py + pc


def _remote(src, dst, send_sem, recv_sem, dev):
    return pltpu.make_async_remote_copy(src_ref=src, dst_ref=dst, send_sem=send_sem, recv_sem=recv_sem, device_id=dev,
                                        device_id_type=pl.DeviceIdType.MESH)


ANY_SPEC = pl.BlockSpec(memory_space=pl.ANY)


class _Exchange:
    def __init__(self, arrs, scatter):
        self.n = len(arrs)
        self.scatter = tuple(scatter)
        self.out_shape = [SDS(a.shape if s else (N_DEV,) + a.shape, a.dtype) for a, s in zip(arrs, scatter)]
        self.scratch = [pltpu.SemaphoreType.DMA((self.n * N_DEV,)), pltpu.SemaphoreType.DMA((self.n * N_DEV,)),
                        pltpu.SemaphoreType.DMA((self.n,))]

    def _copies(self, ins, outs, sems):
        send_sems, recv_sems, local_sems = sems
        x, y, c, me = _mesh_place()
        local, sends, recvs = [], [], []
        for a in range(self.n):
            src = ins[a].at[me] if self.scatter[a] else ins[a]
            local.append(pltpu.make_async_copy(src, outs[a].at[me], local_sems.at[a]))
        for k in range(1, N_DEV):
            dev, lin = _peer(x, y, c, k)
            for a in range(self.n):
                src = ins[a].at[lin] if self.scatter[a] else ins[a]
                pair = (send_sems.at[a * N_DEV + k], recv_sems.at[a * N_DEV + k], dev)
                sends.append(_remote(src, outs[a].at[me], *pair))
                recvs.append(_remote(src, outs[a].at[lin], *pair))
        return local, sends, recvs

    def start(self, ins, outs, sems):
        local, sends, _ = self._copies(ins, outs, sems)
        for cp in local + sends:
            cp.start()

    def wait(self, ins, outs, sems):
        local, sends, recvs = self._copies(ins, outs, sems)
        for cp in recvs:
            cp.wait_recv()
        for cp in sends:
            cp.wait_send()
        for cp in local:
            cp.wait()


def _exchange(arrs, scatter, name):
    ex = _Exchange(arrs, [scatter] * len(arrs))
    n = ex.n

    def body(*refs):
        ins, outs, sems = refs[:n], refs[n:2 * n], refs[2 * n:]
        ex.start(ins, outs, sems)
        ex.wait(ins, outs, sems)

    return pl.pallas_call(
        body, name=name, out_shape=ex.out_shape, in_specs=[ANY_SPEC] * n, out_specs=[ANY_SPEC] * n,
        scratch_shapes=ex.scratch,
    )(*arrs)


def _pre_norm(x, pre_g, tm=512):
    t_len = x.shape[0]

    def body(x_ref, g_ref, hn_ref):
        g = g_ref[...]

        def rows_body(q, _):
            rows = _tile_rows(q)
            xv = x_ref[rows, :]
            hn_ref[rows, :] = (xv * lax.rsqrt(_mean_last(xv * xv) + EPS) * g).astype(BF16)
            return 0

        _loop(tm // TILE_ROWS, rows_body, 0, unroll=TILE_UNROLL)

    tile = pl.BlockSpec((tm, D_MODEL), lambda i: (i, 0))
    return pl.pallas_call(
        body, name="pre_norm", grid=(t_len // tm,),
        in_specs=[tile, pl.BlockSpec((1, D_MODEL), lambda i: (0, 0))], out_specs=tile,
        out_shape=SDS((t_len, D_MODEL), BF16),
        compiler_params=_params(("arbitrary",), 24),
    )(x, pre_g)


CHIP_ORDER = (0, 2, 4, 6)
W_BODY, W_TAIL = 512, 128
SIBLING = 1
ICI_MASKS = (2, 4, 6)
DIRECT_MASKS = (SIBLING,) + ICI_MASKS
Y_NEIGHBOUR, X_NEIGHBOUR, DIAGONAL = 2, 4, 6
W_DIRECT = (SIBLING, Y_NEIGHBOUR, X_NEIGHBOUR)


def _in_proj(hn, w_shard, others, tm=1024):
    t_len = hn.shape[0]
    n_i = t_len // tm
    n_o = len(others)
    me_out = 4 * lax.axis_index("x") + 2 * lax.axis_index("y") + lax.axis_index("c")
    order = jnp.stack([(me_out ^ chip) // 2 for chip in CHIP_ORDER]).astype(jnp.int32)

    def body(order_ref, hn_ref, w_hbm, *refs):
        o_in = refs[:n_o]
        z_ref, wg_hbm = refs[n_o], refs[n_o + 1]
        o_out = refs[n_o + 2:2 * n_o + 2]
        (wbuf, tail_s, send_w, recv_w, fsend_w, frecv_w, send_o, recv_o, fsend_o, frecv_o, wb_sems, loc_sems, rsend,
         rrecv) = refs[2 * n_o + 2:]
        j, i = pl.program_id(0), pl.program_id(1)
        x, y, c, me = _mesh_place()
        sib = _peer(x, y, c, SIBLING)[0]

        def relay(core):
            src, dst = (Y_NEIGHBOUR, X_NEIGHBOUR) if core == 0 else (X_NEIGHBOUR, Y_NEIGHBOUR)
            held, diag = _peer(x, y, c, src)[1], _peer(x, y, c, DIAGONAL)[1]
            pair = (rsend.at[0], rrecv.at[0], _peer(x, y, c, dst)[0])
            return _remote(wbuf.at[held], wbuf.at[held], *pair), _remote(wbuf.at[diag], wbuf.at[diag], *pair)

        def direct(k, a=None):
            dev, lin = _peer(x, y, c, k)
            if a is None:
                return (_remote(w_hbm, wbuf.at[me], send_w.at[k], recv_w.at[k], dev),
                        _remote(w_hbm, wbuf.at[lin], send_w.at[k], recv_w.at[k], dev))
            pair = (send_o.at[a * N_DEV + k], recv_o.at[a * N_DEV + k], dev)
            return _remote(o_in[a], o_out[a].at[me], *pair), _remote(o_in[a], o_out[a].at[lin], *pair)

        def passed(k, a=None):
            mine, theirs = _peer(x, y, c, k)[1], _peer(x, y, c, k ^ SIBLING)[1]
            if a is None:
                pair = (fsend_w.at[k], frecv_w.at[k], sib)
                return _remote(wbuf.at[mine], wbuf.at[mine], *pair), _remote(wbuf.at[theirs], wbuf.at[theirs], *pair)
            pair = (fsend_o.at[a * N_DEV + k], frecv_o.at[a * N_DEV + k], sib)
            return (_remote(o_out[a].at[mine], o_out[a].at[mine], *pair),
                    _remote(o_out[a].at[theirs], o_out[a].at[theirs], *pair))

        def own_copies():
            return [pltpu.make_async_copy(o_in[a], o_out[a].at[me], loc_sems.at[1 + a]) for a in range(n_o)]

        @pl.when(jnp.logical_and(j == 0, i == 0))
        def _():
            own = pltpu.make_async_copy(w_hbm, wbuf.at[me], loc_sems.at[0])
            own.start()
            for cp in own_copies():
                cp.start()
            for k in W_DIRECT:
                direct(k)[0].start()
            for k in DIRECT_MASKS:
                for a in range(n_o):
                    direct(k, a)[0].start()
            own.wait()

        low = 2 * order_ref[j]

        for jp, chip in enumerate(CHIP_ORDER):
            @pl.when(jnp.logical_and(j == jp, i == 0))
            def _(jp=jp, chip=chip):
                if chip == 0:
                    direct(SIBLING)[1].wait_recv()
                elif chip == Y_NEIGHBOUR:
                    for mask in (Y_NEIGHBOUR, X_NEIGHBOUR):
                        direct(mask)[1].wait_recv()
                        passed(mask)[0].start()
                    for core in (0, 1):
                        @pl.when(c == core)
                        def _(core=core):
                            relay(core)[0].start()
                    passed(Y_NEIGHBOUR)[1].wait_recv()
                elif chip == X_NEIGHBOUR:
                    passed(X_NEIGHBOUR)[1].wait_recv()
                    for core in (0, 1):
                        @pl.when(c == core)
                        def _(core=core):
                            relay(core)[1].wait_recv()
                    passed(DIAGONAL)[0].start()
                    for k in ICI_MASKS:
                        for a in range(n_o):
                            direct(k, a)[1].wait_recv()
                            passed(k, a)[0].start()
                else:
                    passed(DIAGONAL)[1].wait_recv()
                for half in (0, 1):
                    pltpu.make_async_copy(wbuf.at[low + half], wg_hbm.at[low + half], wb_sems.at[2 * jp + half]).start()
                tail_s[:, 0:W_TAIL] = wbuf[low, :, W_BODY:W_IN_SHARD]
                tail_s[:, W_TAIL:2 * W_TAIL] = wbuf[low + 1, :, W_BODY:W_IN_SHARD]

        hn = hn_ref[...]
        z_ref[:, 0:W_BODY] = _dot(hn, wbuf[low, :, 0:W_BODY])
        z_ref[:, W_IN_SHARD:W_IN_SHARD + W_BODY] = _dot(hn, wbuf[low + 1, :, 0:W_BODY])
        tails = _dot(hn, tail_s[...])
        z_ref[:, W_BODY:W_IN_SHARD] = tails[:, 0:W_TAIL]
        z_ref[:, W_IN_SHARD + W_BODY:2 * W_IN_SHARD] = tails[:, W_TAIL:2 * W_TAIL]

        @pl.when(jnp.logical_and(j == len(CHIP_ORDER) - 1, i == n_i - 1))
        def _():
            for a in range(n_o):
                direct(SIBLING, a)[1].wait_recv()
            for k in ICI_MASKS:
                for a in range(n_o):
                    passed(k, a)[1].wait_recv()
            for k in W_DIRECT:
                direct(k)[0].wait_send()
            for core in (0, 1):
                @pl.when(c == core)
                def _(core=core):
                    relay(core)[0].wait_send()
            for k in DIRECT_MASKS:
                for a in range(n_o):
                    direct(k, a)[0].wait_send()
            for k in ICI_MASKS:
                passed(k)[0].wait_send()
                for a in range(n_o):
                    passed(k, a)[0].wait_send()
            for cp in own_copies():
                cp.wait()
            for jj in range(N_DEV):
                pltpu.make_async_copy(wbuf.at[0], wg_hbm.at[0], wb_sems.at[jj]).wait()

    dma = lambda n: pltpu.SemaphoreType.DMA((n,))
    grid_spec = pltpu.PrefetchScalarGridSpec(
        num_scalar_prefetch=1, grid=(len(CHIP_ORDER), n_i),
        in_specs=[pl.BlockSpec((tm, D_MODEL), lambda j, i, order: (i, 0)), ANY_SPEC] + [ANY_SPEC] * n_o,
        out_specs=[pl.BlockSpec((tm, 2 * W_IN_SHARD), lambda j, i, order: (i, order[j])), ANY_SPEC] + [ANY_SPEC] * n_o,
        scratch_shapes=[pltpu.VMEM((N_DEV, D_MODEL, W_IN_SHARD), BF16), pltpu.VMEM((D_MODEL, 2 * W_TAIL), BF16),
                        dma(N_DEV), dma(N_DEV), dma(N_DEV), dma(N_DEV),
                        dma(n_o * N_DEV), dma(n_o * N_DEV), dma(n_o * N_DEV), dma(n_o * N_DEV), dma(N_DEV), dma(1 + n_o),
                        dma(1), dma(1)])
    res = pl.pallas_call(
        body, name="in_proj", grid_spec=grid_spec,
        out_shape=[SDS((t_len, D_IN), F32), SDS((N_DEV, D_MODEL, W_IN_SHARD), BF16)]
        + [SDS((N_DEV,) + o.shape, o.dtype) for o in others],
        compiler_params=_params(("arbitrary", "arbitrary"), 54),
    )(order, hn, w_shard, *others)
    return res[0], res[1], res[2:]


def _conv_rows(cur, prev, cw_ref, cb, rid):
    acc = cw_ref[3:4, :] * cur + cb
    for k in range(1, CONV_W):
        acc = acc + cw_ref[3 - k:4 - k, :] * _shift_down(cur, prev, k, rid)
    return acc


ROW0_LOG_A = -1e30


def _row0_mask(rid):
    return jnp.where(rid == 0, ROW0_LOG_A, 0.0)


def _row0_bias(is_first_group, row0_mask):
    return is_first_group.astype(F32) * row0_mask


def _lru_gates(pa, px, ba, bx, sp8, row0_bias):
    r = _sig(pa + ba)
    i = _sig(px + bx)
    la = row0_bias - r * sp8
    a = jnp.exp(la)
    return r, i, a, _neg_expm1(2.0 * la, a * a)


def _mix_fwd(z, ln_g, ln_b, wm, bias, cw, cb, wax, ba, bx, lam, goa, gob, ex_arrs, ex_scatter):
    t_len = z.shape[0]
    n_chunk = t_len // CHUNK
    ex = _Exchange(ex_arrs, ex_scatter)
    n_in, n_out, n_scratch = 13, 5, 7

    def body(*refs):
        (z_ref, lng_ref, lnb_ref, wm_ref, bias_ref, cw_ref, cb_ref, wax_ref, ba_ref, bx_ref, lam_ref, goa_ref,
         gob_ref) = refs[:n_in]
        ex_in = refs[n_in:n_in + ex.n]
        y_ref, h_ref, vhb_ref, xcb_ref, rs_ref = refs[n_in + ex.n:n_in + ex.n + n_out]
        ex_out = refs[n_in + ex.n + n_out:n_in + 2 * ex.n + n_out]
        vn_s, xc_s, mixed_s, pre_s, y_s, carry_s, halo_s = refs[n_in + 2 * ex.n + n_out:n_in + 2 * ex.n + n_out + n_scratch]
        ex_sems = refs[n_in + 2 * ex.n + n_out + n_scratch:]
        c_id = pl.program_id(0)
        rid = _row_ids(D_BR)

        @pl.when(c_id == 0)
        def _():
            ex.start(ex_in, ex_out, ex_sems)
            carry_s[...] = jnp.zeros_like(carry_s)
            halo_s[...] = jnp.zeros_like(halo_s)

        lng, lnb, cb = lng_ref[...], lnb_ref[...], cb_ref[...]

        def phase1(g, prev):
            rows = _rows(g)
            vg = _gelu(z_ref[rows, D_BR:2 * D_BR])
            xm = vg - _mean_last(vg)
            rs = lax.rsqrt(_mean_last(xm * xm) + EPS)
            vn_s[rows, :] = xm * rs
            rs_ref[rows, :] = jnp.broadcast_to(rs, (ROWS, HEAD))
            xb = z_ref[rows, 3 * D_BR:4 * D_BR]
            xc_s[rows, :] = _conv_rows(xb, prev, cw_ref, cb, rid)
            return xb

        halo_s[...] = _loop(N_GROUP, phase1, halo_s[...], unroll=8)
        vhb_ref[...] = vn_s[...].astype(BF16)
        xcb_ref[...] = xc_s[...].astype(BF16)

        for h in range(N_HEAD):
            cs = slice(h * HEAD, (h + 1) * HEAD)
            mixed_s[:, cs] = _dot(wm_ref[h], (vn_s[:, cs] * lng[:, cs] + lnb[:, cs]).astype(BF16))
            pre = _dot(xcb_ref[:, cs], wax_ref[h])
            pre_s[:, cs] = pre[:, :HEAD]
            pre_s[:, D_BR + h * HEAD:D_BR + (h + 1) * HEAD] = pre[:, HEAD:]

        ba, bx, goa, gob = ba_ref[...], bx_ref[...], goa_ref[...], gob_ref[...]
        sp8 = LRU_C * _softplus(-lam_ref[...])
        row0 = _row0_mask(rid)

        def phase3(g, carry):
            rows = _rows(g)
            ug = _gelu(z_ref[rows, 0:D_BR])
            ga = z_ref[rows, 2 * D_BR:3 * D_BR]
            ya = ug * (mixed_s[rows, :] + bias_ref[rows, :]) * (ga * _sig(ga))
            y_s[rows, 0:D_BR] = ya * lax.rsqrt(_mean_last(ya * ya) + EPS) * goa

            bias0 = _row0_bias(jnp.logical_and(c_id == 0, g == 0), row0)
            _, i, a, m2 = _lru_gates(pre_s[rows, 0:D_BR], pre_s[rows, D_BR:2 * D_BR], ba, bx, sp8, bias0)
            b = jnp.sqrt(m2) * i * xc_s[rows, :]
            for d in (1, 2, 4):
                a_sh = jnp.where(rid >= d, pltpu.roll(a, d, 0), 1.0)
                b_sh = jnp.where(rid >= d, pltpu.roll(b, d, 0), 0.0)
                b = a * b_sh + b
                a = a * a_sh
            hh = b + a * carry
            h_ref[rows, :] = hh
            gb = z_ref[rows, 4 * D_BR:5 * D_BR]
            yb = hh * (gb * _sig(gb))
            y_s[rows, D_BR:2 * D_BR] = yb * lax.rsqrt(_mean_last(yb * yb) + EPS) * gob
            return _bcast_row(hh, ROWS - 1)

        carry_s[...] = _loop(N_GROUP, phase3, carry_s[...])
        y_ref[...] = y_s[...].astype(BF16)

        @pl.when(c_id == n_chunk - 1)
        def _():
            ex.wait(ex_in, ex_out, ex_sems)

    vec = pl.BlockSpec((1, D_BR), lambda i: (0, 0))
    res = pl.pallas_call(
        body, name="mix_fwd", grid=(n_chunk,),
        in_specs=[pl.BlockSpec((CHUNK, D_IN), lambda i: (i, 0)), vec, vec,
                  pl.BlockSpec((N_HEAD, HEAD, HEAD), lambda i: (0, 0, 0)),
                  pl.BlockSpec((CHUNK, D_BR), lambda i: (0, 0)),
                  pl.BlockSpec((ROWS, D_BR), lambda i: (0, 0)), vec,
                  pl.BlockSpec((N_HEAD, HEAD, 2 * HEAD), lambda i: (0, 0, 0)), vec, vec, vec, vec, vec]
        + [ANY_SPEC] * ex.n,
        out_specs=[pl.BlockSpec((CHUNK, 2 * D_BR), lambda i: (i, 0)), pl.BlockSpec((CHUNK, D_BR), lambda i: (i, 0)),
                   pl.BlockSpec((CHUNK, D_BR), lambda i: (i, 0)), pl.BlockSpec((CHUNK, D_BR), lambda i: (i, 0)),
                   pl.BlockSpec((CHUNK, HEAD), lambda i: (i, 0))] + [ANY_SPEC] * ex.n,
        out_shape=[SDS((t_len, 2 * D_BR), BF16), SDS((t_len, D_BR), F32), SDS((t_len, D_BR), BF16),
                   SDS((t_len, D_BR), BF16), SDS((t_len, HEAD), F32)] + ex.out_shape,
        scratch_shapes=[pltpu.VMEM((CHUNK, D_BR), F32), pltpu.VMEM((CHUNK, D_BR), F32), pltpu.VMEM((CHUNK, D_BR), F32),
                        pltpu.VMEM((CHUNK, 2 * D_BR), F32), pltpu.VMEM((CHUNK, 2 * D_BR), F32),
                        pltpu.VMEM((ROWS, D_BR), F32), pltpu.VMEM((ROWS, D_BR), F32)] + ex.scratch,
        compiler_params=_params(("arbitrary",), 32),
    )(z, ln_g, ln_b, wm, bias, cw, cb, wax, ba, bx, lam, goa, gob, *ex_arrs)
    return res[:n_out], res[n_out:]


def _load_weight(w_hbm, w_vmem, sem):
    @pl.when(pl.program_id(0) == 0)
    def _():
        cp = pltpu.make_async_copy(w_hbm, w_vmem, sem)
        cp.start()
        cp.wait()


def _out_proj(y, x, w_out, post_g, tm=512):
    t_len = y.shape[0]

    def body(y_ref, x_ref, w_hbm, g_ref, h1_ref, ob_ref, w_s, o_s, sem):
        _load_weight(w_hbm, w_s, sem)
        o_s[...] = _dot(y_ref[...], w_s[...])
        g = g_ref[...]

        def rows_body(q, _):
            rows = _tile_rows(q)
            o = o_s[rows, :]
            h1_ref[rows, :] = x_ref[rows, :] + o * lax.rsqrt(_mean_last(o * o) + EPS) * g
            ob_ref[rows, :] = o.astype(BF16)
            return 0

        _loop(tm // TILE_ROWS, rows_body, 0, unroll=TILE_UNROLL)

    tile = pl.BlockSpec((tm, D_MODEL), lambda i: (i, 0))
    return pl.pallas_call(
        body, name="out_proj", grid=(t_len // tm,),
        in_specs=[tile, tile, pl.BlockSpec(memory_space=pl.ANY), pl.BlockSpec((1, D_MODEL), lambda i: (0, 0))],
        out_specs=[tile, tile],
        out_shape=[SDS((t_len, D_MODEL), F32), SDS((t_len, D_MODEL), BF16)],
        scratch_shapes=[pltpu.VMEM((D_MODEL, D_MODEL), BF16), pltpu.VMEM((tm, D_MODEL), F32), pltpu.SemaphoreType.DMA],
        compiler_params=_params(("arbitrary",), 44),
    )(y, x, w_out, post_g)


def _ple_loss(h1, p, tgt, w_pg, w_pe_g, tm=256):
    t_len = h1.shape[0]
    n_tile = t_len // tm
    pe_shard = D_MODEL // N_DEV

    def body(h1_ref, p_ref, t_ref, w_hbm, wpe_ref, dh2_ref, dgl_ref, h1b_ref, loss_ref, dwpe_ref, w_s, pe_s, gl_s, acc_s,
             dpe_s, gpe_s, sem):
        _load_weight(w_hbm, w_s, sem)
        i = pl.program_id(0)

        @pl.when(i == 0)
        def _():
            acc_s[...] = jnp.zeros_like(acc_s)
            gpe_s[...] = jnp.zeros_like(gpe_s)

        h1b_ref[...] = h1_ref[...].astype(BF16)
        pb = p_ref[...].astype(BF16)
        for j in range(N_DEV):
            pe_s[:, j * pe_shard:(j + 1) * pe_shard] = _dot(pb, wpe_ref[j])
        gl_s[...] = _dot(h1b_ref[...], w_s[...])

        def rows_body(q, acc):
            rows = _tile_rows(q)
            pe = pe_s[rows, :]
            g = _sig(gl_s[rows, :])
            e = h1_ref[rows, :] + pe * g - t_ref[rows, :]
            dh2 = e * (1.0 / D_MODEL)
            dh2_ref[rows, :] = dh2
            dpe_s[rows, :] = (dh2 * g).astype(BF16)
            dgl_ref[rows, :] = (dh2 * pe * g * (1.0 - g)).astype(BF16)
            return acc + _fold_rows(e * e)

        acc_s[...] = _loop(tm // TILE_ROWS, rows_body, acc_s[...], unroll=TILE_UNROLL)
        gpe_s[...] += _dot_tn(pb, dpe_s[...])

        @pl.when(i == n_tile - 1)
        def _():
            loss_ref[...] = jnp.full(loss_ref.shape, 0.5 / D_MODEL * jnp.sum(acc_s[...]), F32)
            for j in range(N_DEV):
                dwpe_ref[j] = gpe_s[:, j * pe_shard:(j + 1) * pe_shard].astype(BF16)

    tile = pl.BlockSpec((tm, D_MODEL), lambda i: (i, 0))
    pe_blocks = pl.BlockSpec((N_DEV, D_PLE, pe_shard), lambda i: (0, 0, 0))
    return pl.pallas_call(
        body, name="ple_loss", grid=(n_tile,),
        in_specs=[tile, pl.BlockSpec((tm, D_PLE), lambda i: (i, 0)), tile, pl.BlockSpec(memory_space=pl.ANY), pe_blocks],
        out_specs=[tile, tile, tile, pl.BlockSpec((ROWS, HEAD), lambda i: (0, 0)), pe_blocks],
        out_shape=[SDS((t_len, D_MODEL), F32), SDS((t_len, D_MODEL), BF16), SDS((t_len, D_MODEL), BF16),
                   SDS((ROWS, HEAD), F32), SDS((N_DEV, D_PLE, pe_shard), BF16)],
        scratch_shapes=[pltpu.VMEM((D_MODEL, D_MODEL), BF16), pltpu.VMEM((tm, D_MODEL), F32),
                        pltpu.VMEM((tm, D_MODEL), F32), pltpu.VMEM((ROWS, D_MODEL), F32), pltpu.VMEM((tm, D_MODEL), BF16),
                        pltpu.VMEM((D_PLE, D_MODEL), F32), pltpu.SemaphoreType.DMA],
        compiler_params=_params(("arbitrary",), 48),
    )(h1, p, tgt, w_pg, w_pe_g)


def _tail_bwd(dh2, dgl, ob, w_pg, w_out, post_g, tm=256):
    t_len = dh2.shape[0]
    n_tile = t_len // tm

    def body(dh2_ref, dgl_ref, ob_ref, wpg_hbm, wout_hbm, g_ref, dh1_ref, do_ref, dy_ref, dg_ref, wpg_s, wout_s, t_s,
             acc_s, sems):
        _load_weight(wpg_hbm, wpg_s, sems.at[0])
        _load_weight(wout_hbm, wout_s, sems.at[1])
        i = pl.program_id(0)

        @pl.when(i == 0)
        def _():
            acc_s[...] = jnp.zeros_like(acc_s)

        t_s[...] = _dot_nt(dgl_ref[...], wpg_s[...])
        g = g_ref[...]

        def rows_body(q, acc):
            rows = _tile_rows(q)
            dh1 = dh2_ref[rows, :] + t_s[rows, :]
            dh1_ref[rows, :] = dh1
            o = ob_ref[rows, :].astype(F32)
            rr = lax.rsqrt(_mean_last(o * o) + EPS)
            on = o * rr
            dog = dh1 * g
            do_ref[rows, :] = (rr * (dog - on * _mean_last(dog * on))).astype(BF16)
            return acc + _fold_rows(dh1 * on)

        acc_s[...] = _loop(tm // TILE_ROWS, rows_body, acc_s[...], unroll=TILE_UNROLL)
        dy_ref[...] = _dot_nt(do_ref[...], wout_s[...]).astype(BF16)

        @pl.when(i == n_tile - 1)
        def _():
            dg_ref[...] = jnp.sum(acc_s[...], axis=0, keepdims=True)

    tile = pl.BlockSpec((tm, D_MODEL), lambda i: (i, 0))
    vec = pl.BlockSpec((1, D_MODEL), lambda i: (0, 0))
    hbm = pl.BlockSpec(memory_space=pl.ANY)
    return pl.pallas_call(
        body, name="tail_bwd", grid=(n_tile,),
        in_specs=[tile, tile, tile, hbm, hbm, vec],
        out_specs=[tile, tile, tile, vec],
        out_shape=[SDS((t_len, D_MODEL), F32), SDS((t_len, D_MODEL), BF16), SDS((t_len, D_MODEL), BF16),
                   SDS((1, D_MODEL), F32)],
        scratch_shapes=[pltpu.VMEM((D_MODEL, D_MODEL), BF16), pltpu.VMEM((D_MODEL, D_MODEL), BF16),
                        pltpu.VMEM((tm, D_MODEL), F32), pltpu.VMEM((ROWS, D_MODEL), F32), pltpu.SemaphoreType.DMA((2,))],
        compiler_params=_params(("arbitrary",), 48),
    )(dh2, dgl, ob, w_pg, w_out, post_g)


def _mix_bwd(z, dy, h, vhb, xcb, rs, ln_g, ln_b, wm, wm_t, bias, cw, cb, wax, wax_t, ba, bx, lam, goa, gob, ex_arrs,
             ex_scatter):
    t_len = z.shape[0]
    n_chunk = t_len // CHUNK
    halo_blocks = CHUNK // ROWS
    ex = _Exchange(ex_arrs, ex_scatter)
    n_in, n_out, n_scratch = 21, 5, 16

    def body(*refs):
        (z_ref, dy_ref, h_ref, hhalo_ref, vhb_ref, xcb_ref, rs_ref, lng_ref, lnb_ref, wm_ref, wmt_ref, bias_ref, cw_ref,
         cb_ref, wax_ref, waxt_ref, ba_ref, bx_ref, lam_ref, goa_ref, gob_ref) = refs[:n_in]
        ex_in = refs[n_in:n_in + ex.n]
        dz_ref, vecs_ref, dws_ref, dwax_ref, dbs_ref = refs[n_in + ex.n:n_in + ex.n + n_out]
        ex_out = refs[n_in + ex.n + n_out:n_in + 2 * ex.n + n_out]
        (vnb_s, vh_s, xc_s, mixed_s, pre_s, dmix_s, dvn_s, dho_s, dxc_s, dpre_s, dz_s, acc_s, accdm_s,
         cg_s, ca_s, dxchalo_s) = refs[n_in + 2 * ex.n + n_out:n_in + 2 * ex.n + n_out + n_scratch]
        ex_sems = refs[n_in + 2 * ex.n + n_out + n_scratch:]
        step = pl.program_id(0)
        c_id = n_chunk - 1 - step
        rid = _row_ids(D_BR)
        first_chunk = c_id == 0

        @pl.when(step == 0)
        def _():
            ex.start(ex_in, ex_out, ex_sems)
            acc_s[...] = jnp.zeros_like(acc_s)
            accdm_s[...] = jnp.zeros_like(accdm_s)
            cg_s[...] = jnp.zeros_like(cg_s)
            ca_s[...] = jnp.zeros_like(ca_s)
            dxchalo_s[...] = jnp.zeros_like(dxchalo_s)
            dws_ref[...] = jnp.zeros_like(dws_ref)
            dwax_ref[...] = jnp.zeros_like(dwax_ref)

        lng, lnb = lng_ref[...], lnb_ref[...]
        h_halo = jnp.where(first_chunk, 0.0, hhalo_ref[...])

        def prev_rows(ref, cols, g, halo):
            before = ref[pl.ds(pl.multiple_of(jnp.maximum(g - 1, 0) * ROWS, ROWS), ROWS), cols]
            return jnp.where(g > 0, before, halo)

        vh_s[...] = vhb_ref[...].astype(F32)
        xc_s[...] = xcb_ref[...].astype(F32)

        for hd in range(N_HEAD):
            cs = slice(hd * HEAD, (hd + 1) * HEAD)
            vnb_s[:, cs] = (vh_s[:, cs] * lng[:, cs] + lnb[:, cs]).astype(BF16)
            mixed_s[:, cs] = _dot(wm_ref[hd], vnb_s[:, cs])
            pre = _dot(xcb_ref[:, cs], wax_ref[hd])
            pre_s[:, cs] = pre[:, :HEAD]
            pre_s[:, D_BR + hd * HEAD:D_BR + (hd + 1) * HEAD] = pre[:, HEAD:]

        goa, gob = goa_ref[...], gob_ref[...]

        def phase3(g, _):
            rows = _rows(g)
            ug, dug = _gelu(z_ref[rows, 0:D_BR], with_grad=True)
            ga = z_ref[rows, 2 * D_BR:3 * D_BR]
            sga = _sig(ga)
            sa = ga * sga
            mixed = mixed_s[rows, :] + bias_ref[rows, :]
            ya0 = ug * mixed
            ya = ya0 * sa
            ra = lax.rsqrt(_mean_last(ya * ya) + EPS)
            dyan = dy_ref[rows, 0:D_BR].astype(F32)
            acc_s[V_GOUT_A] += dyan * ya * ra
            dyg = dyan * goa
            dya = ra * dyg - ya * (ra * ra * ra) * _mean_last(dyg * ya)
            dya0 = dya * sa
            dz_s[rows, 2 * D_BR:3 * D_BR] = dya * ya0 * _silu_grad(sga, sa)
            dmix = dya0 * ug
            dmix_s[rows, :] = dmix
            accdm_s[rows, :] += dmix
            dz_s[rows, 0:D_BR] = dya0 * mixed * dug

            hh = h_ref[rows, :]
            gb = z_ref[rows, 4 * D_BR:5 * D_BR]
            sgb = _sig(gb)
            sb = gb * sgb
            yb = hh * sb
            rb = lax.rsqrt(_mean_last(yb * yb) + EPS)
            dybn = dy_ref[rows, D_BR:2 * D_BR].astype(F32)
            acc_s[V_GOUT_B] += dybn * yb * rb
            dyg = dybn * gob
            dyb = rb * dyg - yb * (rb * rb * rb) * _mean_last(dyg * yb)
            dho_s[rows, :] = dyb * sb
            dz_s[rows, 4 * D_BR:5 * D_BR] = dyb * hh * _silu_grad(sgb, sb)
            return 0

        _loop(N_GROUP, phase3, 0)

        for hd in range(N_HEAD):
            cs = slice(hd * HEAD, (hd + 1) * HEAD)
            dmb = dmix_s[:, cs].astype(BF16)
            dvn_s[:, cs] = _dot(wmt_ref[hd], dmb)
            dws_ref[hd] += _dot_nt(dmb, vnb_s[:, cs])

        def phase5(g, _):
            rows = _rows(g)
            dvn = dvn_s[rows, :]
            vh = vh_s[rows, :]
            acc_s[V_LN_G] += dvn * vh
            acc_s[V_LN_B] += dvn
            dvh = dvn * lng
            rs = rs_ref[rows, 0:1]
            dvg = rs * (dvh - _mean_last(dvh) - vh * _mean_last(dvh * vh))
            dz_s[rows, D_BR:2 * D_BR] = dvg * _gelu(z_ref[rows, D_BR:2 * D_BR], with_grad=True)[1]
            return 0

        _loop(N_GROUP, phase5, 0)

        ba, bx = ba_ref[...], bx_ref[...]
        sp8 = LRU_C * _softplus(-lam_ref[...])
        row0 = _row0_mask(rid)

        def phase6(k, carry):
            cg, ca = carry
            g = N_GROUP - 1 - k
            rows = _rows(g)
            bias0 = _row0_bias(jnp.logical_and(first_chunk, g == 0), row0)
            r, i, a, m2 = _lru_gates(pre_s[rows, 0:D_BR], pre_s[rows, D_BR:2 * D_BR], ba, bx, sp8, bias0)
            a_nx = jnp.where(rid < ROWS - 1, pltpu.roll(a, ROWS - 1, 0), ca)
            aa, bb = a_nx, dho_s[rows, :]
            for d in (1, 2, 4):
                a_sh = jnp.where(rid < ROWS - d, pltpu.roll(aa, ROWS - d, 0), 1.0)
                b_sh = jnp.where(rid < ROWS - d, pltpu.roll(bb, ROWS - d, 0), 0.0)
                bb = aa * b_sh + bb
                aa = aa * a_sh
            gg = bb + aa * cg
            hh = h_ref[rows, :]
            hprev = _shift_down(hh, prev_rows(h_ref, slice(None), g, h_halo), 1, rid)
            xc = xc_s[rows, :]
            gx = gg * xc
            dla = gg * hprev * a - gx * i * (a * a) * lax.rsqrt(m2)
            acc_s[V_LAM] += -(dla * r)
            dpa = -(dla * sp8) * r * (1.0 - r)
            mi = jnp.sqrt(m2) * i
            dpx = gx * mi * (1.0 - i)
            acc_s[V_B_A] += dpa
            acc_s[V_B_X] += dpx
            dpre_s[rows, 0:D_BR] = dpa
            dpre_s[rows, D_BR:2 * D_BR] = dpx
            dxc_s[rows, :] = gg * mi
            return _bcast_row(gg, 0), _bcast_row(a, 0)

        cg, ca = _loop(N_GROUP, phase6, (cg_s[...], ca_s[...]))
        cg_s[...] = cg
        ca_s[...] = ca

        for hd in range(N_HEAD):
            cs = slice(hd * HEAD, (hd + 1) * HEAD)
            dpre = jnp.concatenate([dpre_s[:, cs], dpre_s[:, D_BR + hd * HEAD:D_BR + (hd + 1) * HEAD]], axis=1).astype(BF16)
            dxc_s[:, cs] += _dot(dpre, waxt_ref[hd])
            dwax_ref[hd] += _dot_tn(xcb_ref[:, cs], dpre)

        def phase8(k, nxt):
            g = N_GROUP - 1 - k
            rows = _rows(g)
            dxc = dxc_s[rows, :]
            acc_s[V_CONV_B] += dxc
            xb = z_ref[rows, 3 * D_BR:4 * D_BR]
            dxb = cw_ref[3:4, :] * dxc
            acc_s[V_CONV_W + 3] += dxc * xb
            for j in range(1, CONV_W):
                later = _shift_up(dxc, nxt, j, rid)
                dxb = dxb + cw_ref[3 - j:4 - j, :] * later
                acc_s[V_CONV_W + 3 - j] += later * xb
            dz_s[rows, 3 * D_BR:4 * D_BR] = dxb
            return dxc

        dxchalo_s[...] = _loop(N_GROUP, phase8, dxchalo_s[...])
        dz_ref[...] = dz_s[...].astype(BF16)

        @pl.when(step == n_chunk - 1)
        def _():
            for v in range(N_VEC):
                vecs_ref[v:v + 1, :] = jnp.sum(acc_s[v], axis=0, keepdims=True)
            lam = lam_ref[...]
            vecs_ref[V_LAM:V_LAM + 1, :] = vecs_ref[V_LAM:V_LAM + 1, :] * (-LRU_C * _sig(-lam))
            tril = (lax.broadcasted_iota(jnp.int32, (HEAD, HEAD), 0) >= lax.broadcasted_iota(jnp.int32, (HEAD, HEAD), 1))
            ones = jnp.ones((ROWS, HEAD), BF16)
            for hd in range(N_HEAD):
                cs = slice(hd * HEAD, (hd + 1) * HEAD)
                dws_ref[hd] = jnp.where(tril, dws_ref[hd], 0.0)
                blk = accdm_s[:, cs]
                hi = blk.astype(BF16)
                lo = (blk - hi.astype(F32)).astype(BF16)
                dbs_ref[hd:hd + 1, :] = (_dot_nt(ones, hi) + _dot_nt(ones, lo))[0:1, :]
            ex.wait(ex_in, ex_out, ex_sems)

    vec = pl.BlockSpec((1, D_BR), lambda i: (0, 0))
    rev = lambda i: (n_chunk - 1 - i, 0)
    halo = lambda col: (lambda i: (jnp.maximum((n_chunk - 1 - i) * halo_blocks - 1, 0), col))
    full3 = lambda a, b, c: pl.BlockSpec((a, b, c), lambda i: (0, 0, 0))
    big = lambda w: pltpu.VMEM((CHUNK, w), F32)
    res = pl.pallas_call(
        body, name="mix_bwd", grid=(n_chunk,),
        in_specs=[pl.BlockSpec((CHUNK, D_IN), rev), pl.BlockSpec((CHUNK, 2 * D_BR), rev), pl.BlockSpec((CHUNK, D_BR), rev),
                  pl.BlockSpec((ROWS, D_BR), halo(0)), pl.BlockSpec((CHUNK, D_BR), rev), pl.BlockSpec((CHUNK, D_BR), rev),
                  pl.BlockSpec((CHUNK, HEAD), rev), vec, vec,
                  full3(N_HEAD, HEAD, HEAD), full3(N_HEAD, HEAD, HEAD),
                  pl.BlockSpec((CHUNK, D_BR), lambda i: (0, 0)), pl.BlockSpec((ROWS, D_BR), lambda i: (0, 0)), vec,
                  full3(N_HEAD, HEAD, 2 * HEAD), full3(N_HEAD, 2 * HEAD, HEAD), vec, vec, vec, vec, vec]
        + [ANY_SPEC] * ex.n,
        out_specs=[pl.BlockSpec((CHUNK, D_IN), rev), pl.BlockSpec((N_VEC, D_BR), lambda i: (0, 0)),
                   full3(N_HEAD, HEAD, HEAD), full3(N_HEAD, HEAD, 2 * HEAD),
                   pl.BlockSpec((N_HEAD, HEAD), lambda i: (0, 0))] + [ANY_SPEC] * ex.n,
        out_shape=[SDS((t_len, D_IN), BF16), SDS((N_VEC, D_BR), F32), SDS((N_HEAD, HEAD, HEAD), F32),
                   SDS((N_HEAD, HEAD, 2 * HEAD), F32), SDS((N_HEAD, HEAD), F32)] + ex.out_shape,
        scratch_shapes=[pltpu.VMEM((CHUNK, D_BR), BF16), big(D_BR), big(D_BR), big(D_BR), big(2 * D_BR), big(D_BR),
                        big(D_BR), big(D_BR), big(D_BR), big(2 * D_BR), big(D_IN),
                        pltpu.VMEM((N_VEC, ROWS, D_BR), F32), big(D_BR),
                        pltpu.VMEM((ROWS, D_BR), F32), pltpu.VMEM((ROWS, D_BR), F32), pltpu.VMEM((ROWS, D_BR), F32)]
        + ex.scratch,
        compiler_params=_params(("arbitrary",), 48),
    )(z, dy, h, h, vhb, xcb, rs, ln_g, ln_b, wm, wm_t, bias, cw, cb, wax, wax_t, ba, bx, lam, goa, gob, *ex_arrs)
    return res[:n_out], res[n_out:]


def _in_bwd(dz, w_in_g, x, dh1, pre_g, tm=256):
    t_len = x.shape[0]
    n_tile = t_len // tm

    def body(dz_ref, w_hbm, x_ref, dh1_ref, g_ref, gx_ref, dg_ref, w_s, t_even, t_odd, dg_s, w_sems):
        i = pl.program_id(0)

        @pl.when(i == 0)
        def _():
            loads = [pltpu.make_async_copy(w_hbm.at[s], w_s.at[:, s * W_IN_SHARD:(s + 1) * W_IN_SHARD], w_sems.at[s])
                     for s in range(N_DEV)]
            for cp in loads:
                cp.start()
            dg_s[...] = jnp.zeros_like(dg_s)
            t_odd[...] = jnp.zeros_like(t_odd)
            for cp in loads:
                cp.wait()

        def step(t_new, t_old):
            g = g_ref[...]
            acc = dg_s[...]
            for q in range(tm // TILE_ROWS):
                rows = slice(q * TILE_ROWS, (q + 1) * TILE_ROWS)
                xv = x_ref[rows, :]
                r = lax.rsqrt(_mean_last(xv * xv) + EPS)
                xh = xv * r
                dhn = t_old[rows, :]
                dg = dhn * g
                gx_ref[rows, :] = dh1_ref[rows, :] + r * (dg - xh * _mean_last(dg * xh))
                acc = acc + _fold_rows(dhn * xh)
            dg_s[...] = acc
            t_new[...] = _dot_nt(dz_ref[...], w_s[...])

        @pl.when(i % 2 == 0)
        def _():
            step(t_even, t_odd)

        @pl.when(i % 2 == 1)
        def _():
            step(t_odd, t_even)

        @pl.when(i == n_tile)
        def _():
            dg_ref[...] = jnp.sum(dg_s[...], axis=0, keepdims=True)

    matmul_tile = lambda i: (jnp.minimum(i, n_tile - 1), 0)
    rows_tile = lambda i: (jnp.maximum(i - 1, 0), 0)
    res = pl.pallas_call(
        body, name="in_bwd", grid=(n_tile + 1,),
        in_specs=[pl.BlockSpec((tm, D_IN), matmul_tile), ANY_SPEC, pl.BlockSpec((tm, D_MODEL), rows_tile),
                  pl.BlockSpec((tm, D_MODEL), rows_tile), pl.BlockSpec((1, D_MODEL), lambda i: (0, 0))],
        out_specs=[pl.BlockSpec((tm, D_MODEL), rows_tile), pl.BlockSpec((1, D_MODEL), lambda i: (0, 0))],
        out_shape=[SDS((t_len, D_MODEL), F32), SDS((1, D_MODEL), F32)],
        scratch_shapes=[pltpu.VMEM((D_MODEL, D_IN), BF16), pltpu.VMEM((tm, D_MODEL), F32), pltpu.VMEM((tm, D_MODEL), F32),
                        pltpu.VMEM((ROWS, D_MODEL), F32), pltpu.SemaphoreType.DMA((N_DEV,))],
        compiler_params=_params(("arbitrary",), 54),
    )(dz, w_in_g, x, dh1, pre_g)
    return res[0], res[1]


def _grad_w(a, b, bn, shard_major, name, tk=1024, ex_arrs=(), ex_scatter=()):
    t_len, m = a.shape
    n = b.shape[1]
    n_j, n_k = n // bn, t_len // tk
    ex = _Exchange(ex_arrs, ex_scatter)

    def body(a_ref, b_ref, *refs):
        ex_in, o_ref, ex_out = refs[:ex.n], refs[ex.n], refs[ex.n + 1:2 * ex.n + 1]
        acc_s, ex_sems = refs[2 * ex.n + 1], refs[2 * ex.n + 2:]
        j, k = pl.program_id(0), pl.program_id(1)
        if ex.n:
            @pl.when(jnp.logical_and(j == 0, k == 0))
            def _():
                ex.start(ex_in, ex_out, ex_sems)

        @pl.when(k == 0)
        def _():
            acc_s[...] = jnp.zeros_like(acc_s)

        acc_s[...] += _dot_tn(a_ref[...], b_ref[...])

        @pl.when(k == n_k - 1)
        def _():
            o_ref[...] = acc_s[...].astype(BF16)

        if ex.n:
            @pl.when(jnp.logical_and(j == n_j - 1, k == n_k - 1))
            def _():
                ex.wait(ex_in, ex_out, ex_sems)

    if shard_major:
        out_spec, out_shape = pl.BlockSpec((None, m, bn), lambda j, k: (j, 0, 0)), SDS((n_j, m, bn), BF16)
    else:
        out_spec, out_shape = pl.BlockSpec((m, bn), lambda j, k: (0, j)), SDS((m, n), BF16)
    res = pl.pallas_call(
        body, name=name, grid=(n_j, n_k),
        in_specs=[pl.BlockSpec((tk, m), lambda j, k: (k, 0)), pl.BlockSpec((tk, bn), lambda j, k: (k, j))]
        + [ANY_SPEC] * ex.n,
        out_specs=[out_spec] + [ANY_SPEC] * ex.n, out_shape=[out_shape] + ex.out_shape,
        scratch_shapes=[pltpu.VMEM((m, bn), F32)] + (ex.scratch if ex.n else []),
        compiler_params=_params(("arbitrary", "arbitrary"), 40),
    )(a, b, *ex_arrs)
    return res[0], res[1:]


RS_CHIPS = (6, 2, 4, 0)
RS_SLOTS = (0, 1, 2, 4, 6)


def _grad_w_in_pairs(hn, dz, ex_arrs, ex_scatter, tk=1024):
    t_len = hn.shape[0]
    n_k = t_len // tk
    n_ph = len(RS_CHIPS)
    ex = _Exchange(ex_arrs, ex_scatter)
    me_out = 4 * lax.axis_index("x") + 2 * lax.axis_index("y") + lax.axis_index("c")
    order = jnp.stack([(me_out ^ chip) // 2 for chip in RS_CHIPS]).astype(jnp.int32)
    slots = jnp.stack([me_out ^ k for k in RS_SLOTS]).astype(jnp.int32)
    shard = W_IN_SHARD

    def body(order_ref, a_ref, b_ref, *refs):
        ex_in, parts_hbm, ex_out = refs[:ex.n], refs[ex.n], refs[ex.n + 1:2 * ex.n + 1]
        (acc_s, tb_s, stage_s, rx_s, d2d_send, d2d_recv, ici_send, ici_recv, sib_sems,
         loc_sem) = refs[2 * ex.n + 1:2 * ex.n + 11]
        ex_sems = refs[2 * ex.n + 11:]
        j, k = pl.program_id(0), pl.program_id(1)
        x, y, c, me = _mesh_place()
        sib = _peer(x, y, c, SIBLING)[0]

        def to_sibling(p):
            return _remote(stage_s.at[0], rx_s.at[p % 2], d2d_send.at[p], d2d_recv.at[p], sib)

        def over_ici(p):
            dev = _peer(x, y, c, RS_CHIPS[p])[0]
            return _remote(stage_s.at[1], parts_hbm.at[me], ici_send.at[p], ici_recv.at[p], dev)

        def own_chip():
            return (_remote(stage_s.at[0], parts_hbm.at[me], sib_sems.at[0], sib_sems.at[1], sib),
                    pltpu.make_async_copy(stage_s.at[1], parts_hbm.at[me], loc_sem.at[0]))

        @pl.when(jnp.logical_and(j == 0, k == 0))
        def _():
            ex.start(ex_in, ex_out, ex_sems)

        for p in range(n_ph - 1):
            for core in (0, 1):
                @pl.when(jnp.logical_and(jnp.logical_and(j == p + 1, k == 0), c == core))
                def _(p=p, core=core):
                    to_sibling(p).wait_recv()
                    if p >= 1:
                        over_ici(p - 1).wait_send()
                    mine = acc_s[:, core * shard:(core + 1) * shard]
                    stage_s[1] = (mine + rx_s[p % 2].astype(F32)).astype(BF16)
                    over_ici(p).start()

        @pl.when(k == 0)
        def _():
            acc_s[...] = jnp.zeros_like(acc_s)

        a = a_ref[...]
        acc_s[:, 0:W_BODY] += _dot_tn(a, b_ref[:, 0:W_BODY])
        acc_s[:, shard:shard + W_BODY] += _dot_tn(a, b_ref[:, shard:shard + W_BODY])
        tb_s[:, 0:W_TAIL] = b_ref[:, W_BODY:shard]
        tb_s[:, W_TAIL:2 * W_TAIL] = b_ref[:, shard + W_BODY:2 * shard]
        tails = _dot_tn(a, tb_s[...])
        acc_s[:, W_BODY:shard] += tails[:, 0:W_TAIL]
        acc_s[:, shard + W_BODY:2 * shard] += tails[:, W_TAIL:2 * W_TAIL]

        for p in range(n_ph):
            for core in (0, 1):
                @pl.when(jnp.logical_and(jnp.logical_and(j == p, k == n_k - 1), c == core))
                def _(p=p, core=core):
                    same = acc_s[:, core * shard:(core + 1) * shard]
                    other = acc_s[:, (1 - core) * shard:(2 - core) * shard]
                    if p >= 1:
                        to_sibling(p - 1).wait_send()
                    stage_s[0] = other.astype(BF16)
                    if p < n_ph - 1:
                        to_sibling(p).start()
                    else:
                        over_ici(n_ph - 2).wait_send()
                        stage_s[1] = same.astype(BF16)
                        for cp in own_chip():
                            cp.start()

        @pl.when(jnp.logical_and(j == n_ph - 1, k == n_k - 1))
        def _():
            to_sib, local = own_chip()
            to_sib.wait_send()
            local.wait()
            _remote(stage_s.at[0], parts_hbm.at[_peer(x, y, c, SIBLING)[1]], sib_sems.at[0], sib_sems.at[1], sib).wait_recv()
            for p in range(n_ph - 1):
                dev, lin = _peer(x, y, c, RS_CHIPS[p])
                _remote(stage_s.at[0], parts_hbm.at[lin], ici_send.at[p], ici_recv.at[p], dev).wait_recv()
            ex.wait(ex_in, ex_out, ex_sems)

    dma = lambda n: pltpu.SemaphoreType.DMA((n,))
    grid_spec = pltpu.PrefetchScalarGridSpec(
        num_scalar_prefetch=1, grid=(n_ph, n_k),
        in_specs=[pl.BlockSpec((tk, D_MODEL), lambda j, k, order: (k, 0)),
                  pl.BlockSpec((tk, 2 * shard), lambda j, k, order: (k, order[j]))] + [ANY_SPEC] * ex.n,
        out_specs=[ANY_SPEC] * (1 + ex.n),
        scratch_shapes=[pltpu.VMEM((D_MODEL, 2 * shard), F32), pltpu.VMEM((tk, 2 * W_TAIL), BF16),
                        pltpu.VMEM((2, D_MODEL, shard), BF16),
                        pltpu.VMEM((2, D_MODEL, shard), BF16), dma(n_ph - 1), dma(n_ph - 1), dma(n_ph - 1),
                        dma(n_ph - 1), dma(2), dma(1)] + ex.scratch)
    res = pl.pallas_call(
        body, name="grad_w_in", grid_spec=grid_spec,
        out_shape=[SDS((N_DEV, D_MODEL, shard), BF16)] + ex.out_shape,
        compiler_params=_params(("arbitrary", "arbitrary"), 54),
    )(order, hn, dz, *ex_arrs)
    return res[0], slots, res[1:]


def _sum_parts(parts, name):
    def body(p_ref, o_ref):
        g = p_ref[0].astype(F32)
        for s in range(1, parts.shape[0]):
            g = g + p_ref[s].astype(F32)
        o_ref[...] = g

    return pl.pallas_call(body, name=name, out_shape=SDS(parts.shape[1:], F32))(parts)


def _adamw_math(g, w_ref, m_ref, v_ref, g_ref, d_ref, nm_ref, nv_ref):
    c1 = 1.0 - ADAM_B1 ** ADAM_STEP
    c2 = 1.0 - ADAM_B2 ** ADAM_STEP
    g_ref[...] = g
    nm = ADAM_B1 * m_ref[...] + (1.0 - ADAM_B1) * g
    nv = ADAM_B2 * v_ref[...] + (1.0 - ADAM_B2) * (g * g)
    nm_ref[...] = nm
    nv_ref[...] = nv
    d_ref[...] = -ADAM_LR * ((nm / c1) / (jnp.sqrt(nv / c2) + ADAM_EPS) + ADAM_WD * w_ref[...])


def _adamw(parts, w, m, v, name, tr):
    rows, cols = w.shape
    n_parts = parts.shape[0]

    def body(p_ref, *refs):
        g = p_ref[0].astype(F32)
        for s in range(1, n_parts):
            g = g + p_ref[s].astype(F32)
        _adamw_math(g, *refs)

    tile = pl.BlockSpec((tr, cols), lambda i: (i, 0))
    return pl.pallas_call(
        body, name=name, grid=(rows // tr,),
        in_specs=[pl.BlockSpec((n_parts, tr, cols), lambda i: (0, i, 0)), tile, tile, tile],
        out_specs=[tile] * 4, out_shape=[SDS((rows, cols), F32)] * 4,
        compiler_params=_params(("arbitrary",), 40),
    )(parts, w, m, v)


def _adamw_unpacked(grads, triples, name):
    n = len(triples)
    n_rows = [t[0].shape[0] for t in triples]

    def body(g_ref, *refs):
        ins, outs = refs[:3 * n], refs[3 * n:]
        row = 0
        for i in range(n):
            _adamw_math(g_ref[row:row + n_rows[i], :], *ins[3 * i:3 * i + 3], *outs[4 * i:4 * i + 4])
            row += n_rows[i]
        outs[4 * n][...] = g_ref[row:row + ROWS, :]

    out_shape = [SDS((r, LANES), F32) for r in n_rows for _ in range(4)] + [SDS((ROWS, LANES), F32)]
    return pl.pallas_call(
        body, name=name, out_shape=out_shape,
        compiler_params=pltpu.CompilerParams(vmem_limit_bytes=40 * MIB),
    )(grads, *[a for t in triples for a in t])


def _adamw_slots(parts, slots, w, m, v, name, tr):
    rows, cols = w.shape
    n_slots = slots.shape[0]

    def body(slots_ref, *refs):
        g = refs[0][...].astype(F32)
        for s in range(1, n_slots):
            g = g + refs[s][...].astype(F32)
        _adamw_math(g, *refs[n_slots:])

    tile = pl.BlockSpec((tr, cols), lambda i, slots: (i, 0))
    part = lambda s: pl.BlockSpec((None, tr, cols), lambda i, slots: (slots[s], i, 0))
    grid_spec = pltpu.PrefetchScalarGridSpec(
        num_scalar_prefetch=1, grid=(rows // tr,),
        in_specs=[part(s) for s in range(n_slots)] + [tile, tile, tile], out_specs=[tile] * 4)
    return pl.pallas_call(
        body, name=name, grid_spec=grid_spec, out_shape=[SDS((rows, cols), F32)] * 4,
        compiler_params=_params(("arbitrary",), 40),
    )(slots, *([parts] * n_slots), w, m, v)


PACKED = ("gmlp_ln_g", "gmlp_ln_b", "gmlp_ws", "gmlp_bs", "conv_b", "w_a", "b_a", "w_x", "b_x", "lam", "gmlp_out_g",
          "lru_out_g", "post_g")
WEIGHTS = ("pre_g", "w_in", "gmlp_ln_g", "gmlp_ln_b", "gmlp_ws", "gmlp_bs", "conv_w", "conv_b", "w_a", "b_a", "w_x",
           "b_x", "lam", "gmlp_out_g", "lru_out_g", "w_out", "post_g", "w_pe", "w_pg")
LANES = 128


PACK_ROWS = 3200


def _pack(parts):
    rows = [p.reshape(-1, LANES) for p in parts]
    used = sum(r.shape[0] for r in rows)
    return jnp.concatenate(rows + [jnp.zeros((PACK_ROWS - used, LANES), F32)], axis=0)


def _pad_rows(a, rows):
    return jnp.concatenate([a, jnp.zeros((rows - a.shape[0],) + a.shape[1:], a.dtype)], axis=0)


def kernel(x, p, pre_g, w_in, gmlp_ln_g, gmlp_ln_b, gmlp_ws, gmlp_bs, conv_w, conv_b, w_a, b_a, w_x, b_x, lam, gmlp_out_g, lru_out_g, w_out, post_g, w_pe, w_pg, loss_target, m_pre_g, m_w_in, m_gmlp_ln_g, m_gmlp_ln_b, m_gmlp_ws, m_gmlp_bs, m_conv_w, m_conv_b, m_w_a, m_b_a, m_w_x, m_b_x, m_lam, m_gmlp_out_g, m_lru_out_g, m_w_out, m_post_g, m_w_pe, m_w_pg, v_pre_g, v_w_in, v_gmlp_ln_g, v_gmlp_ln_b, v_gmlp_ws, v_gmlp_bs, v_conv_w, v_conv_b, v_w_a, v_b_a, v_w_x, v_b_x, v_lam, v_gmlp_out_g, v_lru_out_g, v_w_out, v_post_g, v_w_pe, v_w_pg):
    args = dict(locals())
    weights = {n: args[n] for n in WEIGHTS}
    m_in = {n: args["m_" + n] for n in WEIGHTS}
    v_in = {n: args["v_" + n] for n in WEIGHTS}
    sm = {n: weights[n][0] for n in PACKED}
    shard_rows = D_MODEL // N_DEV
    xs, ps, tgt = x[0], p[0, 0], loss_target[0]

    vec = lambda a: a.reshape(1, -1)
    tril = jnp.tril(jnp.ones((CHUNK, CHUNK), dtype=bool))
    wm32 = jnp.where(tril[None], sm["gmlp_ws"], 0.0)
    wm, wm_t = wm32.astype(BF16), jnp.swapaxes(wm32, 1, 2).astype(BF16)
    bias = jnp.repeat(sm["gmlp_bs"].T, HEAD, axis=1)
    wax32 = jnp.concatenate([sm["w_a"], sm["w_x"]], axis=2)
    wax, wax_t = wax32.astype(BF16), jnp.swapaxes(wax32, 1, 2).astype(BF16)
    ln_g, ln_b = vec(sm["gmlp_ln_g"]), vec(sm["gmlp_ln_b"])
    post_g_v = vec(sm["post_g"])

    hn = _pre_norm(xs, pre_g)
    cw_shard = _pad_rows(conv_w.reshape(CONV_W, HEAD), ROWS)
    z, w_in_g, (cw_g,) = _in_proj(hn, w_in[0].astype(BF16), [cw_shard])
    cw_full = jnp.transpose(cw_g[:, :CONV_W, :], (1, 0, 2)).reshape(CONV_W, D_BR)
    mixer_consts = dict(cw=_pad_rows(cw_full, ROWS), cb=vec(sm["conv_b"]), ba=vec(sm["b_a"]), bx=vec(sm["b_x"]),
                        lam=vec(sm["lam"]), goa=vec(sm["gmlp_out_g"]), gob=vec(sm["lru_out_g"]))
    (y, h, vhb, xcb, v_rs), (w_out_g, w_pe_g, w_pg_g) = _mix_fwd(
        z, ln_g, ln_b, wm, bias, wax=wax, **mixer_consts,
        ex_arrs=[w_out[0].astype(BF16), w_pe[0].astype(BF16), w_pg[0].astype(BF16)], ex_scatter=[False, False, False])
    w_out_f, w_pg_f = w_out_g.reshape(D_MODEL, D_MODEL), w_pg_g.reshape(D_MODEL, D_MODEL)
    h1, ob = _out_proj(y, xs, w_out_f, post_g_v)
    dh2, dgl, h1b, loss_part, d_w_pe = _ple_loss(h1, ps, tgt, w_pg_f, w_pe_g)

    dh1, do, dy, d_post_g = _tail_bwd(dh2, dgl, ob, w_pg_f, w_out_f, post_g_v)
    d_w_out, _ = _grad_w(y, do, 1024, False, "grad_w_out")
    d_w_pg, _ = _grad_w(h1b, dgl, 1024, False, "grad_w_pg")
    (dz, vecs, d_ws, d_wax, d_bs), (parts_out, parts_pg, parts_pe) = _mix_bwd(
        z, dy, h, vhb, xcb, v_rs, ln_g, ln_b, wm, wm_t, bias, wax=wax, wax_t=wax_t, **mixer_consts,
        ex_arrs=[d_w_out.reshape(N_DEV, shard_rows, D_MODEL), d_w_pg.reshape(N_DEV, shard_rows, D_MODEL), d_w_pe],
        ex_scatter=[True, True, True])

    small = {"gmlp_ln_g": vecs[V_LN_G], "gmlp_ln_b": vecs[V_LN_B], "gmlp_ws": d_ws, "gmlp_bs": d_bs,
             "conv_b": vecs[V_CONV_B], "w_a": d_wax[:, :, :HEAD], "b_a": vecs[V_B_A], "w_x": d_wax[:, :, HEAD:],
             "b_x": vecs[V_B_X], "lam": vecs[V_LAM], "gmlp_out_g": vecs[V_GOUT_A], "lru_out_g": vecs[V_GOUT_B],
             "post_g": d_post_g}
    small_part = _pack([small[n] for n in PACKED] + [loss_part]).reshape(N_DEV, PACK_ROWS // N_DEV, LANES)
    d_cw_blocks = jnp.transpose(vecs[V_CONV_W:V_CONV_W + CONV_W].reshape(CONV_W, N_DEV, HEAD), (1, 0, 2))
    d_cw_blocks = jnp.concatenate([d_cw_blocks, jnp.zeros((N_DEV, ROWS - CONV_W, HEAD), F32)], axis=1)
    parts_in, slots_in, (small_blocks, parts_cw) = _grad_w_in_pairs(
        hn, dz, ex_arrs=[small_part, d_cw_blocks], ex_scatter=[True, True])
    small_sum = _sum_parts(small_blocks, "sum_small")
    grad_x, d_pre_g = _in_bwd(dz, w_in_g, xs, dh1, pre_g)
    pre_rows = D_MODEL // LANES
    small_all, parts_pre = _exchange([small_sum, d_pre_g.reshape(pre_rows, LANES)], False, "gather_small_grads")

    pad_cw = lambda a: _pad_rows(a.reshape(CONV_W, HEAD), ROWS)
    flat = lambda a: a.reshape(pre_rows, LANES)
    outs = {
        "w_in": _adamw_slots(parts_in, slots_in, w_in[0], m_w_in[0], v_w_in[0], "adamw_w_in", 256),
        "w_out": _adamw(parts_out, w_out[0], m_w_out[0], v_w_out[0], "adamw_w_out", 128),
        "w_pe": _adamw(parts_pe, w_pe[0], m_w_pe[0], v_w_pe[0], "adamw_w_pe", 256),
        "w_pg": _adamw(parts_pg, w_pg[0], m_w_pg[0], v_w_pg[0], "adamw_w_pg", 128),
        "conv_w": [a[:CONV_W] for a in
                   _adamw(parts_cw, pad_cw(conv_w), pad_cw(m_conv_w), pad_cw(v_conv_w), "adamw_conv_w", ROWS)],
        "pre_g": _adamw(parts_pre, flat(pre_g), flat(m_pre_g), flat(v_pre_g), "adamw_pre_g", pre_rows),
    }
    as_rows = lambda a: a.reshape(-1, LANES)
    small_res = _adamw_unpacked(small_all.reshape(PACK_ROWS, LANES),
                                [(as_rows(weights[n]), as_rows(m_in[n]), as_rows(v_in[n])) for n in PACKED], "adamw_small")
    for i, n in enumerate(PACKED):
        outs[n] = small_res[4 * i:4 * i + 4]
    loss = small_res[-1][0, 0]

    result = [loss, grad_x[None]]
    for q in range(4):
        result += [outs[n][q].reshape(weights[n].shape) for n in WEIGHTS]
    return tuple(result)
```

```python
import jax
import jax.numpy as jnp
from jax import lax
from jax.experimental import pallas as pl
from jax.experimental.pallas import tpu as pltpu

F32 = jnp.float32
BF16 = jnp.bfloat16
SDS = jax.ShapeDtypeStruct

D_MODEL = 2048
D_BR = 1024
D_IN = 5 * D_BR
D_PLE = 256
N_HEAD = 8
HEAD = 128
CHUNK = 128
ROWS = 8
N_GROUP = CHUNK // ROWS
N_DEV = 8
W_IN_SHARD = D_IN // N_DEV
EPS = 1e-6
LRU_C = 8.0
CONV_W = 4
MIB = 1 << 20

ADAM_LR, ADAM_B1, ADAM_B2, ADAM_EPS, ADAM_WD, ADAM_STEP = 0.001, 0.9, 0.999, 1e-08, 0.01, 10

_GELU_C = 0.7978845608028654
_GELU_A = 0.044715

V_LN_G, V_LN_B, V_CONV_B, V_B_A, V_B_X, V_LAM, V_GOUT_A, V_GOUT_B, V_CONV_W = 0, 1, 2, 3, 4, 5, 6, 7, 8
N_VEC = 16


def _params(sem, vmem_mib):
    return pltpu.CompilerParams(dimension_semantics=sem, vmem_limit_bytes=int(vmem_mib * MIB))


def _sig(x):
    return 0.5 * jnp.tanh(0.5 * x) + 0.5


def _gelu(x, with_grad=False):
    sq = x * x
    t = jnp.tanh(x * (_GELU_C + (_GELU_C * _GELU_A) * sq))
    half, one_t = 0.5 * x, 1.0 + t
    if not with_grad:
        return half * one_t
    grad = 0.5 * one_t + half * ((1.0 - t) * one_t) * (_GELU_C + (3.0 * _GELU_C * _GELU_A) * sq)
    return half * one_t, grad


def _silu_grad(s, xs):
    return s + xs * (1.0 - s)


def _neg_expm1(y, exp_y):
    series = -y * (1.0 + y * (0.5 + y * (1.0 / 6.0)))
    return jnp.where(y > -0.01, series, 1.0 - exp_y)


def _softplus(x):
    return jnp.maximum(x, 0.0) + jnp.log(1.0 + jnp.exp(-jnp.abs(x)))


def _row_ids(width):
    return lax.broadcasted_iota(jnp.int32, (ROWS, width), 0)


def _shift_down(cur, prev, k, rid):
    return jnp.where(rid >= k, pltpu.roll(cur, k, 0), pltpu.roll(prev, k, 0))


def _shift_up(cur, nxt, k, rid):
    return jnp.where(rid < ROWS - k, pltpu.roll(cur, ROWS - k, 0), pltpu.roll(nxt, ROWS - k, 0))


def _mean_last(x):
    return jnp.mean(x, axis=-1, keepdims=True)


def _rows(g):
    return pl.ds(pl.multiple_of(g * ROWS, ROWS), ROWS)


TILE_ROWS = 16


def _tile_rows(q):
    return pl.ds(pl.multiple_of(q * TILE_ROWS, TILE_ROWS), TILE_ROWS)


UNROLL = 4
TILE_UNROLL = 8


def _loop(n, body, init, unroll=UNROLL):
    def wide(i, carry):
        for u in range(unroll):
            carry = body(i * unroll + u, carry)
        return carry

    return lax.fori_loop(0, n // unroll, wide, init)


def _fold_rows(x):
    return x[0:ROWS, :] + x[ROWS:TILE_ROWS, :]


def _bcast_row(x, r):
    return jnp.broadcast_to(x[r:r + 1, :], x.shape)


def _dot(a, b):
    return jnp.dot(a, b, preferred_element_type=F32)


def _dot_nt(a, b):
    return lax.dot_general(a, b, (((1,), (1,)), ((), ())), preferred_element_type=F32)


def _dot_tn(a, b):
    return lax.dot_general(a, b, (((0,), (0,)), ((), ())), preferred_element_type=F32)


def _mesh_place():
    x, y, c = lax.axis_index("x"), lax.axis_index("y"), lax.axis_index("c")
    return x, y, c, 4 * x + 2 * y + c


def _peer(x, y, c, k):
    px = 1 - x if k & 4 else x
    py = 1 - y if k & 2 else y
    pc = 1 - c if k & 1 else c
    return (px, py, pc), 4 * px + 2 * py + pc


def _remote(src, dst, send_sem, recv_sem, dev):
    return pltpu.make_async_remote_copy(src_ref=src, dst_ref=dst, send_sem=send_sem, recv_sem=recv_sem, device_id=dev,
                                        device_id_type=pl.DeviceIdType.MESH)


ANY_SPEC = pl.BlockSpec(memory_space=pl.ANY)
EXCHANGE_ORDER = (6, 7, 2, 3, 4, 5, 1)


class _Exchange:
    def __init__(self, arrs, scatter):
        self.n = len(arrs)
        self.scatter = tuple(scatter)
        self.out_shape = [SDS(a.shape if s else (N_DEV,) + a.shape, a.dtype) for a, s in zip(arrs, scatter)]
        self.scratch = [pltpu.SemaphoreType.DMA((self.n * N_DEV,)), pltpu.SemaphoreType.DMA((self.n * N_DEV,)),
                        pltpu.SemaphoreType.DMA((self.n,))]

    def _copies(self, ins, outs, sems):
        send_sems, recv_sems, local_sems = sems
        x, y, c, me = _mesh_place()
        local, sends, recvs = [], [], []
        for a in range(self.n):
            src = ins[a].at[me] if self.scatter[a] else ins[a]
            local.append(pltpu.make_async_copy(src, outs[a].at[me], local_sems.at[a]))
        for k in EXCHANGE_ORDER:
            dev, lin = _peer(x, y, c, k)
            for a in range(self.n):
                src = ins[a].at[lin] if self.scatter[a] else ins[a]
                pair = (send_sems.at[a * N_DEV + k], recv_sems.at[a * N_DEV + k], dev)
                sends.append(_remote(src, outs[a].at[me], *pair))
                recvs.append(_remote(src, outs[a].at[lin], *pair))
        return local, sends, recvs

    def start(self, ins, outs, sems):
        local, sends, _ = self._copies(ins, outs, sems)
        for cp in local + sends:
            cp.start()

    def wait(self, ins, outs, sems):
        local, sends, recvs = self._copies(ins, outs, sems)
        for cp in recvs:
            cp.wait_recv()
        for cp in sends:
            cp.wait_send()
        for cp in local:
            cp.wait()


def _exchange(arrs, scatter, name):
    ex = _Exchange(arrs, [scatter] * len(arrs))
    n = ex.n

    def body(*refs):
        ins, outs, sems = refs[:n], refs[n:2 * n], refs[2 * n:]
        ex.start(ins, outs, sems)
        ex.wait(ins, outs, sems)

    return pl.pallas_call(
        body, name=name, out_shape=ex.out_shape, in_specs=[ANY_SPEC] * n, out_specs=[ANY_SPEC] * n,
        scratch_shapes=ex.scratch,
    )(*arrs)


def _pre_norm(x, pre_g, tm=512):
    t_len = x.shape[0]

    def body(x_ref, g_ref, hn_ref):
        g = g_ref[...]

        def rows_body(q, _):
            rows = _tile_rows(q)
            xv = x_ref[rows, :]
            hn_ref[rows, :] = (xv * lax.rsqrt(_mean_last(xv * xv) + EPS) * g).astype(BF16)
            return 0

        _loop(tm // TILE_ROWS, rows_body, 0, unroll=TILE_UNROLL)

    tile = pl.BlockSpec((tm, D_MODEL), lambda i: (i, 0))
    return pl.pallas_call(
        body, name="pre_norm", grid=(t_len // tm,),
        in_specs=[tile, pl.BlockSpec((1, D_MODEL), lambda i: (0, 0))], out_specs=tile,
        out_shape=SDS((t_len, D_MODEL), BF16),
        compiler_params=_params(("arbitrary",), 24),
    )(x, pre_g)


CHIP_ORDER = (0, 2, 4, 6)
W_BODY, W_TAIL = 512, 128
SIBLING = 1
ICI_MASKS = (2, 4, 6)
DIRECT_MASKS = (SIBLING,) + ICI_MASKS
Y_NEIGHBOUR, X_NEIGHBOUR, DIAGONAL = 2, 4, 6
W_DIRECT = (SIBLING, Y_NEIGHBOUR, X_NEIGHBOUR)


def _in_proj(hn, w_shard, others, tm=1024):
    t_len = hn.shape[0]
    n_i = t_len // tm
    n_o = len(others)
    me_out = 4 * lax.axis_index("x") + 2 * lax.axis_index("y") + lax.axis_index("c")
    order = jnp.stack([(me_out ^ chip) // 2 for chip in CHIP_ORDER]).astype(jnp.int32)

    def body(order_ref, hn_ref, w_hbm, *refs):
        o_in = refs[:n_o]
        z_ref, wg_hbm = refs[n_o], refs[n_o + 1]
        o_out = refs[n_o + 2:2 * n_o + 2]
        (wbuf, tail_s, send_w, recv_w, fsend_w, frecv_w, send_o, recv_o, fsend_o, frecv_o, wb_sems, loc_sems, rsend,
         rrecv) = refs[2 * n_o + 2:]
        j, i = pl.program_id(0), pl.program_id(1)
        x, y, c, me = _mesh_place()
        sib = _peer(x, y, c, SIBLING)[0]

        def relay(core):
            src, dst = (Y_NEIGHBOUR, X_NEIGHBOUR) if core == 0 else (X_NEIGHBOUR, Y_NEIGHBOUR)
            held, diag = _peer(x, y, c, src)[1], _peer(x, y, c, DIAGONAL)[1]
            pair = (rsend.at[0], rrecv.at[0], _peer(x, y, c, dst)[0])
            return _remote(wbuf.at[held], wbuf.at[held], *pair), _remote(wbuf.at[diag], wbuf.at[diag], *pair)

        def direct(k, a=None):
            dev, lin = _peer(x, y, c, k)
            if a is None:
                return (_remote(w_hbm, wbuf.at[me], send_w.at[k], recv_w.at[k], dev),
                        _remote(w_hbm, wbuf.at[lin], send_w.at[k], recv_w.at[k], dev))
            pair = (send_o.at[a * N_DEV + k], recv_o.at[a * N_DEV + k], dev)
            return _remote(o_in[a], o_out[a].at[me], *pair), _remote(o_in[a], o_out[a].at[lin], *pair)

        def passed(k, a=None):
            mine, theirs = _peer(x, y, c, k)[1], _peer(x, y, c, k ^ SIBLING)[1]
            if a is None:
                pair = (fsend_w.at[k], frecv_w.at[k], sib)
                return _remote(wbuf.at[mine], wbuf.at[mine], *pair), _remote(wbuf.at[theirs], wbuf.at[theirs], *pair)
            pair = (fsend_o.at[a * N_DEV + k], frecv_o.at[a * N_DEV + k], sib)
            return (_remote(o_out[a].at[mine], o_out[a].at[mine], *pair),
                    _remote(o_out[a].at[theirs], o_out[a].at[theirs], *pair))

        def own_copies():
            return [pltpu.make_async_copy(o_in[a], o_out[a].at[me], loc_sems.at[1 + a]) for a in range(n_o)]

        @pl.when(jnp.logical_and(j == 0, i == 0))
        def _():
            own = pltpu.make_async_copy(w_hbm, wbuf.at[me], loc_sems.at[0])
            own.start()
            for cp in own_copies():
                cp.start()
            for k in W_DIRECT:
                direct(k)[0].start()
            for k in DIRECT_MASKS:
                for a in range(n_o):
                    direct(k, a)[0].start()
            own.wait()

        low = 2 * order_ref[j]

        for jp, chip in enumerate(CHIP_ORDER):
            @pl.when(jnp.logical_and(j == jp, i == 0))
            def _(jp=jp, chip=chip):
                if chip == 0:
                    direct(SIBLING)[1].wait_recv()
                elif chip == Y_NEIGHBOUR:
                    for mask in (Y_NEIGHBOUR, X_NEIGHBOUR):
                        direct(mask)[1].wait_recv()
                        passed(mask)[0].start()
                    for core in (0, 1):
                        @pl.when(c == core)
                        def _(core=core):
                            relay(core)[0].start()
                    passed(Y_NEIGHBOUR)[1].wait_recv()
                elif chip == X_NEIGHBOUR:
                    passed(X_NEIGHBOUR)[1].wait_recv()
                    for core in (0, 1):
                        @pl.when(c == core)
                        def _(core=core):
                            relay(core)[1].wait_recv()
                    passed(DIAGONAL)[0].start()
                    for k in ICI_MASKS:
                        for a in range(n_o):
                            direct(k, a)[1].wait_recv()
                            passed(k, a)[0].start()
                else:
                    passed(DIAGONAL)[1].wait_recv()
                for half in (0, 1):
                    pltpu.make_async_copy(wbuf.at[low + half], wg_hbm.at[low + half], wb_sems.at[2 * jp + half]).start()
                tail_s[:, 0:W_TAIL] = wbuf[low, :, W_BODY:W_IN_SHARD]
                tail_s[:, W_TAIL:2 * W_TAIL] = wbuf[low + 1, :, W_BODY:W_IN_SHARD]

        hn = hn_ref[...]
        z_ref[:, 0:W_BODY] = _dot(hn, wbuf[low, :, 0:W_BODY])
        z_ref[:, W_IN_SHARD:W_IN_SHARD + W_BODY] = _dot(hn, wbuf[low + 1, :, 0:W_BODY])
        tails = _dot(hn, tail_s[...])
        z_ref[:, W_BODY:W_IN_SHARD] = tails[:, 0:W_TAIL]
        z_ref[:, W_IN_SHARD + W_BODY:2 * W_IN_SHARD] = tails[:, W_TAIL:2 * W_TAIL]

        @pl.when(jnp.logical_and(j == len(CHIP_ORDER) - 1, i == n_i - 1))
        def _():
            for a in range(n_o):
                direct(SIBLING, a)[1].wait_recv()
            for k in ICI_MASKS:
                for a in range(n_o):
                    passed(k, a)[1].wait_recv()
            for k in W_DIRECT:
                direct(k)[0].wait_send()
            for core in (0, 1):
                @pl.when(c == core)
                def _(core=core):
                    relay(core)[0].wait_send()
            for k in DIRECT_MASKS:
                for a in range(n_o):
                    direct(k, a)[0].wait_send()
            for k in ICI_MASKS:
                passed(k)[0].wait_send()
                for a in range(n_o):
                    passed(k, a)[0].wait_send()
            for cp in own_copies():
                cp.wait()
            for jj in range(N_DEV):
                pltpu.make_async_copy(wbuf.at[0], wg_hbm.at[0], wb_sems.at[jj]).wait()

    dma = lambda n: pltpu.SemaphoreType.DMA((n,))
    grid_spec = pltpu.PrefetchScalarGridSpec(
        num_scalar_prefetch=1, grid=(len(CHIP_ORDER), n_i),
        in_specs=[pl.BlockSpec((tm, D_MODEL), lambda j, i, order: (i, 0)), ANY_SPEC] + [ANY_SPEC] * n_o,
        out_specs=[pl.BlockSpec((tm, 2 * W_IN_SHARD), lambda j, i, order: (i, order[j])), ANY_SPEC] + [ANY_SPEC] * n_o,
        scratch_shapes=[pltpu.VMEM((N_DEV, D_MODEL, W_IN_SHARD), BF16), pltpu.VMEM((D_MODEL, 2 * W_TAIL), BF16),
                        dma(N_DEV), dma(N_DEV), dma(N_DEV), dma(N_DEV),
                        dma(n_o * N_DEV), dma(n_o * N_DEV), dma(n_o * N_DEV), dma(n_o * N_DEV), dma(N_DEV), dma(1 + n_o),
                        dma(1), dma(1)])
    res = pl.pallas_call(
        body, name="in_proj", grid_spec=grid_spec,
        out_shape=[SDS((t_len, D_IN), F32), SDS((N_DEV, D_MODEL, W_IN_SHARD), BF16)]
        + [SDS((N_DEV,) + o.shape, o.dtype) for o in others],
        compiler_params=_params(("arbitrary", "arbitrary"), 54),
    )(order, hn, w_shard, *others)
    return res[0], res[1], res[2:]


def _conv_rows(cur, prev, cw_ref, cb, rid):
    acc = cw_ref[3:4, :] * cur + cb
    for k in range(1, CONV_W):
        acc = acc + cw_ref[3 - k:4 - k, :] * _shift_down(cur, prev, k, rid)
    return acc


ROW0_LOG_A = -1e30


def _row0_mask(rid):
    return jnp.where(rid == 0, ROW0_LOG_A, 0.0)


def _row0_bias(is_first_group, row0_mask):
    return is_first_group.astype(F32) * row0_mask


def _lru_gates(pa, px, ba, bx, sp8, row0_bias):
    r = _sig(pa + ba)
    i = _sig(px + bx)
    la = row0_bias - r * sp8
    a = jnp.exp(la)
    return r, i, a, _neg_expm1(2.0 * la, a * a)


def _mix_fwd(z, ln_g, ln_b, wm, bias, cw, cb, wax, ba, bx, lam, goa, gob, ex_arrs, ex_scatter):
    t_len = z.shape[0]
    n_chunk = t_len // CHUNK
    ex = _Exchange(ex_arrs, ex_scatter)
    n_in, n_out, n_scratch = 13, 5, 7

    def body(*refs):
        (z_ref, lng_ref, lnb_ref, wm_ref, bias_ref, cw_ref, cb_ref, wax_ref, ba_ref, bx_ref, lam_ref, goa_ref,
         gob_ref) = refs[:n_in]
        ex_in = refs[n_in:n_in + ex.n]
        y_ref, h_ref, vhb_ref, xcb_ref, rs_ref = refs[n_in + ex.n:n_in + ex.n + n_out]
        ex_out = refs[n_in + ex.n + n_out:n_in + 2 * ex.n + n_out]
        vn_s, xc_s, mixed_s, pre_s, y_s, carry_s, halo_s = refs[n_in + 2 * ex.n + n_out:n_in + 2 * ex.n + n_out + n_scratch]
        ex_sems = refs[n_in + 2 * ex.n + n_out + n_scratch:]
        c_id = pl.program_id(0)
        rid = _row_ids(D_BR)

        @pl.when(c_id == 0)
        def _():
            ex.start(ex_in, ex_out, ex_sems)
            carry_s[...] = jnp.zeros_like(carry_s)
            halo_s[...] = jnp.zeros_like(halo_s)

        lng, lnb, cb = lng_ref[...], lnb_ref[...], cb_ref[...]

        def phase1(g, prev):
            rows = _rows(g)
            vg = _gelu(z_ref[rows, D_BR:2 * D_BR])
            xm = vg - _mean_last(vg)
            rs = lax.rsqrt(_mean_last(xm * xm) + EPS)
            vn_s[rows, :] = xm * rs
            rs_ref[rows, :] = jnp.broadcast_to(rs, (ROWS, HEAD))
            xb = z_ref[rows, 3 * D_BR:4 * D_BR]
            xc_s[rows, :] = _conv_rows(xb, prev, cw_ref, cb, rid)
            return xb

        halo_s[...] = _loop(N_GROUP, phase1, halo_s[...], unroll=8)
        vhb_ref[...] = vn_s[...].astype(BF16)
        xcb_ref[...] = xc_s[...].astype(BF16)

        for h in range(N_HEAD):
            cs = slice(h * HEAD, (h + 1) * HEAD)
            mixed_s[:, cs] = _dot(wm_ref[h], (vn_s[:, cs] * lng[:, cs] + lnb[:, cs]).astype(BF16))
            pre = _dot(xcb_ref[:, cs], wax_ref[h])
            pre_s[:, cs] = pre[:, :HEAD]
            pre_s[:, D_BR + h * HEAD:D_BR + (h + 1) * HEAD] = pre[:, HEAD:]

        ba, bx, goa, gob = ba_ref[...], bx_ref[...], goa_ref[...], gob_ref[...]
        sp8 = LRU_C * _softplus(-lam_ref[...])
        row0 = _row0_mask(rid)

        def phase3(g, carry):
            rows = _rows(g)
            ug = _gelu(z_ref[rows, 0:D_BR])
            ga = z_ref[rows, 2 * D_BR:3 * D_BR]
            ya = ug * (mixed_s[rows, :] + bias_ref[rows, :]) * (ga * _sig(ga))
            y_s[rows, 0:D_BR] = ya * lax.rsqrt(_mean_last(ya * ya) + EPS) * goa

            bias0 = _row0_bias(jnp.logical_and(c_id == 0, g == 0), row0)
            _, i, a, m2 = _lru_gates(pre_s[rows, 0:D_BR], pre_s[rows, D_BR:2 * D_BR], ba, bx, sp8, bias0)
            b = jnp.sqrt(m2) * i * xc_s[rows, :]
            for d in (1, 2, 4):
                a_sh = jnp.where(rid >= d, pltpu.roll(a, d, 0), 1.0)
                b_sh = jnp.where(rid >= d, pltpu.roll(b, d, 0), 0.0)
                b = a * b_sh + b
                a = a * a_sh
            hh = b + a * carry
            h_ref[rows, :] = hh
            gb = z_ref[rows, 4 * D_BR:5 * D_BR]
            yb = hh * (gb * _sig(gb))
            y_s[rows, D_BR:2 * D_BR] = yb * lax.rsqrt(_mean_last(yb * yb) + EPS) * gob
            return _bcast_row(hh, ROWS - 1)

        carry_s[...] = _loop(N_GROUP, phase3, carry_s[...])
        y_ref[...] = y_s[...].astype(BF16)

        @pl.when(c_id == n_chunk - 1)
        def _():
            ex.wait(ex_in, ex_out, ex_sems)

    vec = pl.BlockSpec((1, D_BR), lambda i: (0, 0))
    res = pl.pallas_call(
        body, name="mix_fwd", grid=(n_chunk,),
        in_specs=[pl.BlockSpec((CHUNK, D_IN), lambda i: (i, 0)), vec, vec,
                  pl.BlockSpec((N_HEAD, HEAD, HEAD), lambda i: (0, 0, 0)),
                  pl.BlockSpec((CHUNK, D_BR), lambda i: (0, 0)),
                  pl.BlockSpec((ROWS, D_BR), lambda i: (0, 0)), vec,
                  pl.BlockSpec((N_HEAD, HEAD, 2 * HEAD), lambda i: (0, 0, 0)), vec, vec, vec, vec, vec]
        + [ANY_SPEC] * ex.n,
        out_specs=[pl.BlockSpec((CHUNK, 2 * D_BR), lambda i: (i, 0)), pl.BlockSpec((CHUNK, D_BR), lambda i: (i, 0)),
                   pl.BlockSpec((CHUNK, D_BR), lambda i: (i, 0)), pl.BlockSpec((CHUNK, D_BR), lambda i: (i, 0)),
                   pl.BlockSpec((CHUNK, HEAD), lambda i: (i, 0))] + [ANY_SPEC] * ex.n,
        out_shape=[SDS((t_len, 2 * D_BR), BF16), SDS((t_len, D_BR), F32), SDS((t_len, D_BR), BF16),
                   SDS((t_len, D_BR), BF16), SDS((t_len, HEAD), F32)] + ex.out_shape,
        scratch_shapes=[pltpu.VMEM((CHUNK, D_BR), F32), pltpu.VMEM((CHUNK, D_BR), F32), pltpu.VMEM((CHUNK, D_BR), F32),
                        pltpu.VMEM((CHUNK, 2 * D_BR), F32), pltpu.VMEM((CHUNK, 2 * D_BR), F32),
                        pltpu.VMEM((ROWS, D_BR), F32), pltpu.VMEM((ROWS, D_BR), F32)] + ex.scratch,
        compiler_params=_params(("arbitrary",), 32),
    )(z, ln_g, ln_b, wm, bias, cw, cb, wax, ba, bx, lam, goa, gob, *ex_arrs)
    return res[:n_out], res[n_out:]


def _load_weight(w_hbm, w_vmem, sem):
    @pl.when(pl.program_id(0) == 0)
    def _():
        cp = pltpu.make_async_copy(w_hbm, w_vmem, sem)
        cp.start()
        cp.wait()


def _out_proj(y, x, w_out, post_g, tm=512):
    t_len = y.shape[0]

    def body(y_ref, x_ref, w_hbm, g_ref, h1_ref, ob_ref, w_s, o_s, sem):
        _load_weight(w_hbm, w_s, sem)
        o_s[...] = _dot(y_ref[...], w_s[...])
        g = g_ref[...]

        def rows_body(q, _):
            rows = _tile_rows(q)
            o = o_s[rows, :]
            h1_ref[rows, :] = x_ref[rows, :] + o * lax.rsqrt(_mean_last(o * o) + EPS) * g
            ob_ref[rows, :] = o.astype(BF16)
            return 0

        _loop(tm // TILE_ROWS, rows_body, 0, unroll=TILE_UNROLL)

    tile = pl.BlockSpec((tm, D_MODEL), lambda i: (i, 0))
    return pl.pallas_call(
        body, name="out_proj", grid=(t_len // tm,),
        in_specs=[tile, tile, pl.BlockSpec(memory_space=pl.ANY), pl.BlockSpec((1, D_MODEL), lambda i: (0, 0))],
        out_specs=[tile, tile],
        out_shape=[SDS((t_len, D_MODEL), F32), SDS((t_len, D_MODEL), BF16)],
        scratch_shapes=[pltpu.VMEM((D_MODEL, D_MODEL), BF16), pltpu.VMEM((tm, D_MODEL), F32), pltpu.SemaphoreType.DMA],
        compiler_params=_params(("arbitrary",), 44),
    )(y, x, w_out, post_g)


def _ple_loss(h1, p, tgt, w_pg, w_pe_g, tm=256):
    t_len = h1.shape[0]
    n_tile = t_len // tm
    pe_shard = D_MODEL // N_DEV

    def body(h1_ref, p_ref, t_ref, w_hbm, wpe_ref, dh2_ref, dgl_ref, h1b_ref, loss_ref, dwpe_ref, w_s, pe_s, gl_s, acc_s,
             dpe_s, gpe_s, sem):
        _load_weight(w_hbm, w_s, sem)
        i = pl.program_id(0)

        @pl.when(i == 0)
        def _():
            acc_s[...] = jnp.zeros_like(acc_s)
            gpe_s[...] = jnp.zeros_like(gpe_s)

        h1b_ref[...] = h1_ref[...].astype(BF16)
        pb = p_ref[...].astype(BF16)
        for j in range(N_DEV):
            pe_s[:, j * pe_shard:(j + 1) * pe_shard] = _dot(pb, wpe_ref[j])
        gl_s[...] = _dot(h1b_ref[...], w_s[...])

        def rows_body(q, acc):
            rows = _tile_rows(q)
            pe = pe_s[rows, :]
            g = _sig(gl_s[rows, :])
            e = h1_ref[rows, :] + pe * g - t_ref[rows, :]
            dh2 = e * (1.0 / D_MODEL)
            dh2_ref[rows, :] = dh2
            dpe_s[rows, :] = (dh2 * g).astype(BF16)
            dgl_ref[rows, :] = (dh2 * pe * g * (1.0 - g)).astype(BF16)
            return acc + _fold_rows(e * e)

        acc_s[...] = _loop(tm // TILE_ROWS, rows_body, acc_s[...], unroll=TILE_UNROLL)
        gpe_s[...] += _dot_tn(pb, dpe_s[...])

        @pl.when(i == n_tile - 1)
        def _():
            loss_ref[...] = jnp.full(loss_ref.shape, 0.5 / D_MODEL * jnp.sum(acc_s[...]), F32)
            for j in range(N_DEV):
                dwpe_ref[j] = gpe_s[:, j * pe_shard:(j + 1) * pe_shard].astype(BF16)

    tile = pl.BlockSpec((tm, D_MODEL), lambda i: (i, 0))
    pe_blocks = pl.BlockSpec((N_DEV, D_PLE, pe_shard), lambda i: (0, 0, 0))
    return pl.pallas_call(
        body, name="ple_loss", grid=(n_tile,),
        in_specs=[tile, pl.BlockSpec((tm, D_PLE), lambda i: (i, 0)), tile, pl.BlockSpec(memory_space=pl.ANY), pe_blocks],
        out_specs=[tile, tile, tile, pl.BlockSpec((ROWS, HEAD), lambda i: (0, 0)), pe_blocks],
        out_shape=[SDS((t_len, D_MODEL), F32), SDS((t_len, D_MODEL), BF16), SDS((t_len, D_MODEL), BF16),
                   SDS((ROWS, HEAD), F32), SDS((N_DEV, D_PLE, pe_shard), BF16)],
        scratch_shapes=[pltpu.VMEM((D_MODEL, D_MODEL), BF16), pltpu.VMEM((tm, D_MODEL), F32),
                        pltpu.VMEM((tm, D_MODEL), F32), pltpu.VMEM((ROWS, D_MODEL), F32), pltpu.VMEM((tm, D_MODEL), BF16),
                        pltpu.VMEM((D_PLE, D_MODEL), F32), pltpu.SemaphoreType.DMA],
        compiler_params=_params(("arbitrary",), 48),
    )(h1, p, tgt, w_pg, w_pe_g)


def _tail_bwd(dh2, dgl, ob, w_pg, w_out, post_g, tm=256):
    t_len = dh2.shape[0]
    n_tile = t_len // tm

    def body(dh2_ref, dgl_ref, ob_ref, wpg_hbm, wout_hbm, g_ref, dh1_ref, do_ref, dy_ref, dg_ref, wpg_s, wout_s, t_s,
             acc_s, sems):
        _load_weight(wpg_hbm, wpg_s, sems.at[0])
        _load_weight(wout_hbm, wout_s, sems.at[1])
        i = pl.program_id(0)

        @pl.when(i == 0)
        def _():
            acc_s[...] = jnp.zeros_like(acc_s)

        t_s[...] = _dot_nt(dgl_ref[...], wpg_s[...])
        g = g_ref[...]

        def rows_body(q, acc):
            rows = _tile_rows(q)
            dh1 = dh2_ref[rows, :] + t_s[rows, :]
            dh1_ref[rows, :] = dh1
            o = ob_ref[rows, :].astype(F32)
            rr = lax.rsqrt(_mean_last(o * o) + EPS)
            on = o * rr
            dog = dh1 * g
            do_ref[rows, :] = (rr * (dog - on * _mean_last(dog * on))).astype(BF16)
            return acc + _fold_rows(dh1 * on)

        acc_s[...] = _loop(tm // TILE_ROWS, rows_body, acc_s[...], unroll=TILE_UNROLL)
        dy_ref[...] = _dot_nt(do_ref[...], wout_s[...]).astype(BF16)

        @pl.when(i == n_tile - 1)
        def _():
            dg_ref[...] = jnp.sum(acc_s[...], axis=0, keepdims=True)

    tile = pl.BlockSpec((tm, D_MODEL), lambda i: (i, 0))
    vec = pl.BlockSpec((1, D_MODEL), lambda i: (0, 0))
    hbm = pl.BlockSpec(memory_space=pl.ANY)
    return pl.pallas_call(
        body, name="tail_bwd", grid=(n_tile,),
        in_specs=[tile, tile, tile, hbm, hbm, vec],
        out_specs=[tile, tile, tile, vec],
        out_shape=[SDS((t_len, D_MODEL), F32), SDS((t_len, D_MODEL), BF16), SDS((t_len, D_MODEL), BF16),
                   SDS((1, D_MODEL), F32)],
        scratch_shapes=[pltpu.VMEM((D_MODEL, D_MODEL), BF16), pltpu.VMEM((D_MODEL, D_MODEL), BF16),
                        pltpu.VMEM((tm, D_MODEL), F32), pltpu.VMEM((ROWS, D_MODEL), F32), pltpu.SemaphoreType.DMA((2,))],
        compiler_params=_params(("arbitrary",), 48),
    )(dh2, dgl, ob, w_pg, w_out, post_g)


def _mix_bwd(z, dy, h, vhb, xcb, rs, ln_g, ln_b, wm, wm_t, bias, cw, cb, wax, wax_t, ba, bx, lam, goa, gob, ex_arrs,
             ex_scatter):
    t_len = z.shape[0]
    n_chunk = t_len // CHUNK
    halo_blocks = CHUNK // ROWS
    ex = _Exchange(ex_arrs, ex_scatter)
    n_in, n_out, n_scratch = 21, 5, 16

    def body(*refs):
        (z_ref, dy_ref, h_ref, hhalo_ref, vhb_ref, xcb_ref, rs_ref, lng_ref, lnb_ref, wm_ref, wmt_ref, bias_ref, cw_ref,
         cb_ref, wax_ref, waxt_ref, ba_ref, bx_ref, lam_ref, goa_ref, gob_ref) = refs[:n_in]
        ex_in = refs[n_in:n_in + ex.n]
        dz_ref, vecs_ref, dws_ref, dwax_ref, dbs_ref = refs[n_in + ex.n:n_in + ex.n + n_out]
        ex_out = refs[n_in + ex.n + n_out:n_in + 2 * ex.n + n_out]
        (vnb_s, vh_s, xc_s, mixed_s, pre_s, dmix_s, dvn_s, dho_s, dxc_s, dpre_s, dz_s, acc_s, accdm_s,
         cg_s, ca_s, dxchalo_s) = refs[n_in + 2 * ex.n + n_out:n_in + 2 * ex.n + n_out + n_scratch]
        ex_sems = refs[n_in + 2 * ex.n + n_out + n_scratch:]
        step = pl.program_id(0)
        c_id = n_chunk - 1 - step
        rid = _row_ids(D_BR)
        first_chunk = c_id == 0

        @pl.when(step == 0)
        def _():
            ex.start(ex_in, ex_out, ex_sems)
            acc_s[...] = jnp.zeros_like(acc_s)
            accdm_s[...] = jnp.zeros_like(accdm_s)
            cg_s[...] = jnp.zeros_like(cg_s)
            ca_s[...] = jnp.zeros_like(ca_s)
            dxchalo_s[...] = jnp.zeros_like(dxchalo_s)
            dws_ref[...] = jnp.zeros_like(dws_ref)
            dwax_ref[...] = jnp.zeros_like(dwax_ref)

        lng, lnb = lng_ref[...], lnb_ref[...]
        h_halo = jnp.where(first_chunk, 0.0, hhalo_ref[...])

        def prev_rows(ref, cols, g, halo):
            before = ref[pl.ds(pl.multiple_of(jnp.maximum(g - 1, 0) * ROWS, ROWS), ROWS), cols]
            return jnp.where(g > 0, before, halo)

        vh_s[...] = vhb_ref[...].astype(F32)
        xc_s[...] = xcb_ref[...].astype(F32)

        for hd in range(N_HEAD):
            cs = slice(hd * HEAD, (hd + 1) * HEAD)
            vnb_s[:, cs] = (vh_s[:, cs] * lng[:, cs] + lnb[:, cs]).astype(BF16)
            mixed_s[:, cs] = _dot(wm_ref[hd], vnb_s[:, cs])
            pre = _dot(xcb_ref[:, cs], wax_ref[hd])
            pre_s[:, cs] = pre[:, :HEAD]
            pre_s[:, D_BR + hd * HEAD:D_BR + (hd + 1) * HEAD] = pre[:, HEAD:]

        goa, gob = goa_ref[...], gob_ref[...]

        def phase3(g, _):
            rows = _rows(g)
            ug, dug = _gelu(z_ref[rows, 0:D_BR], with_grad=True)
            ga = z_ref[rows, 2 * D_BR:3 * D_BR]
            sga = _sig(ga)
            sa = ga * sga
            mixed = mixed_s[rows, :] + bias_ref[rows, :]
            ya0 = ug * mixed
            ya = ya0 * sa
            ra = lax.rsqrt(_mean_last(ya * ya) + EPS)
            dyan = dy_ref[rows, 0:D_BR].astype(F32)
            acc_s[V_GOUT_A] += dyan * ya * ra
            dyg = dyan * goa
            dya = ra * dyg - ya * (ra * ra * ra) * _mean_last(dyg * ya)
            dya0 = dya * sa
            dz_s[rows, 2 * D_BR:3 * D_BR] = dya * ya0 * _silu_grad(sga, sa)
            dmix = dya0 * ug
            dmix_s[rows, :] = dmix
            accdm_s[rows, :] += dmix
            dz_s[rows, 0:D_BR] = dya0 * mixed * dug

            hh = h_ref[rows, :]
            gb = z_ref[rows, 4 * D_BR:5 * D_BR]
            sgb = _sig(gb)
            sb = gb * sgb
            yb = hh * sb
            rb = lax.rsqrt(_mean_last(yb * yb) + EPS)
            dybn = dy_ref[rows, D_BR:2 * D_BR].astype(F32)
            acc_s[V_GOUT_B] += dybn * yb * rb
            dyg = dybn * gob
            dyb = rb * dyg - yb * (rb * rb * rb) * _mean_last(dyg * yb)
            dho_s[rows, :] = dyb * sb
            dz_s[rows, 4 * D_BR:5 * D_BR] = dyb * hh * _silu_grad(sgb, sb)
            return 0

        _loop(N_GROUP, phase3, 0)

        for hd in range(N_HEAD):
            cs = slice(hd * HEAD, (hd + 1) * HEAD)
            dmb = dmix_s[:, cs].astype(BF16)
            dvn_s[:, cs] = _dot(wmt_ref[hd], dmb)
            dws_ref[hd] += _dot_nt(dmb, vnb_s[:, cs])

        def phase5(g, _):
            rows = _rows(g)
            dvn = dvn_s[rows, :]
            vh = vh_s[rows, :]
            acc_s[V_LN_G] += dvn * vh
            acc_s[V_LN_B] += dvn
            dvh = dvn * lng
            rs = rs_ref[rows, 0:1]
            dvg = rs * (dvh - _mean_last(dvh) - vh * _mean_last(dvh * vh))
            dz_s[rows, D_BR:2 * D_BR] = dvg * _gelu(z_ref[rows, D_BR:2 * D_BR], with_grad=True)[1]
            return 0

        _loop(N_GROUP, phase5, 0)

        ba, bx = ba_ref[...], bx_ref[...]
        sp8 = LRU_C * _softplus(-lam_ref[...])
        row0 = _row0_mask(rid)

        def phase6(k, carry):
            cg, ca = carry
            g = N_GROUP - 1 - k
            rows = _rows(g)
            bias0 = _row0_bias(jnp.logical_and(first_chunk, g == 0), row0)
            r, i, a, m2 = _lru_gates(pre_s[rows, 0:D_BR], pre_s[rows, D_BR:2 * D_BR], ba, bx, sp8, bias0)
            a_nx = jnp.where(rid < ROWS - 1, pltpu.roll(a, ROWS - 1, 0), ca)
            aa, bb = a_nx, dho_s[rows, :]
            for d in (1, 2, 4):
                a_sh = jnp.where(rid < ROWS - d, pltpu.roll(aa, ROWS - d, 0), 1.0)
                b_sh = jnp.where(rid < ROWS - d, pltpu.roll(bb, ROWS - d, 0), 0.0)
                bb = aa * b_sh + bb
                aa = aa * a_sh
            gg = bb + aa * cg
            hh = h_ref[rows, :]
            hprev = _shift_down(hh, prev_rows(h_ref, slice(None), g, h_halo), 1, rid)
            xc = xc_s[rows, :]
            gx = gg * xc
            dla = gg * hprev * a - gx * i * (a * a) * lax.rsqrt(m2)
            acc_s[V_LAM] += -(dla * r)
            dpa = -(dla * sp8) * r * (1.0 - r)
            mi = jnp.sqrt(m2) * i
            dpx = gx * mi * (1.0 - i)
            acc_s[V_B_A] += dpa
            acc_s[V_B_X] += dpx
            dpre_s[rows, 0:D_BR] = dpa
            dpre_s[rows, D_BR:2 * D_BR] = dpx
            dxc_s[rows, :] = gg * mi
            return _bcast_row(gg, 0), _bcast_row(a, 0)

        cg, ca = _loop(N_GROUP, phase6, (cg_s[...], ca_s[...]))
        cg_s[...] = cg
        ca_s[...] = ca

        for hd in range(N_HEAD):
            cs = slice(hd * HEAD, (hd + 1) * HEAD)
            dpre = jnp.concatenate([dpre_s[:, cs], dpre_s[:, D_BR + hd * HEAD:D_BR + (hd + 1) * HEAD]], axis=1).astype(BF16)
            dxc_s[:, cs] += _dot(dpre, waxt_ref[hd])
            dwax_ref[hd] += _dot_tn(xcb_ref[:, cs], dpre)

        def phase8(k, nxt):
            g = N_GROUP - 1 - k
            rows = _rows(g)
            dxc = dxc_s[rows, :]
            acc_s[V_CONV_B] += dxc
            xb = z_ref[rows, 3 * D_BR:4 * D_BR]
            dxb = cw_ref[3:4, :] * dxc
            acc_s[V_CONV_W + 3] += dxc * xb
            for j in range(1, CONV_W):
                later = _shift_up(dxc, nxt, j, rid)
                dxb = dxb + cw_ref[3 - j:4 - j, :] * later
                acc_s[V_CONV_W + 3 - j] += later * xb
            dz_s[rows, 3 * D_BR:4 * D_BR] = dxb
            return dxc

        dxchalo_s[...] = _loop(N_GROUP, phase8, dxchalo_s[...])
        dz_ref[...] = dz_s[...].astype(BF16)

        @pl.when(step == n_chunk - 1)
        def _():
            for v in range(N_VEC):
                vecs_ref[v:v + 1, :] = jnp.sum(acc_s[v], axis=0, keepdims=True)
            lam = lam_ref[...]
            vecs_ref[V_LAM:V_LAM + 1, :] = vecs_ref[V_LAM:V_LAM + 1, :] * (-LRU_C * _sig(-lam))
            tril = (lax.broadcasted_iota(jnp.int32, (HEAD, HEAD), 0) >= lax.broadcasted_iota(jnp.int32, (HEAD, HEAD), 1))
            ones = jnp.ones((ROWS, HEAD), BF16)
            for hd in range(N_HEAD):
                cs = slice(hd * HEAD, (hd + 1) * HEAD)
                dws_ref[hd] = jnp.where(tril, dws_ref[hd], 0.0)
                blk = accdm_s[:, cs]
                hi = blk.astype(BF16)
                lo = (blk - hi.astype(F32)).astype(BF16)
                dbs_ref[hd:hd + 1, :] = (_dot_nt(ones, hi) + _dot_nt(ones, lo))[0:1, :]
            ex.wait(ex_in, ex_out, ex_sems)

    vec = pl.BlockSpec((1, D_BR), lambda i: (0, 0))
    rev = lambda i: (n_chunk - 1 - i, 0)
    halo = lambda col: (lambda i: (jnp.maximum((n_chunk - 1 - i) * halo_blocks - 1, 0), col))
    full3 = lambda a, b, c: pl.BlockSpec((a, b, c), lambda i: (0, 0, 0))
    big = lambda w: pltpu.VMEM((CHUNK, w), F32)
    res = pl.pallas_call(
        body, name="mix_bwd", grid=(n_chunk,),
        in_specs=[pl.BlockSpec((CHUNK, D_IN), rev), pl.BlockSpec((CHUNK, 2 * D_BR), rev), pl.BlockSpec((CHUNK, D_BR), rev),
                  pl.BlockSpec((ROWS, D_BR), halo(0)), pl.BlockSpec((CHUNK, D_BR), rev), pl.BlockSpec((CHUNK, D_BR), rev),
                  pl.BlockSpec((CHUNK, HEAD), rev), vec, vec,
                  full3(N_HEAD, HEAD, HEAD), full3(N_HEAD, HEAD, HEAD),
                  pl.BlockSpec((CHUNK, D_BR), lambda i: (0, 0)), pl.BlockSpec((ROWS, D_BR), lambda i: (0, 0)), vec,
                  full3(N_HEAD, HEAD, 2 * HEAD), full3(N_HEAD, 2 * HEAD, HEAD), vec, vec, vec, vec, vec]
        + [ANY_SPEC] * ex.n,
        out_specs=[pl.BlockSpec((CHUNK, D_IN), rev), pl.BlockSpec((N_VEC, D_BR), lambda i: (0, 0)),
                   full3(N_HEAD, HEAD, HEAD), full3(N_HEAD, HEAD, 2 * HEAD),
                   pl.BlockSpec((N_HEAD, HEAD), lambda i: (0, 0))] + [ANY_SPEC] * ex.n,
        out_shape=[SDS((t_len, D_IN), BF16), SDS((N_VEC, D_BR), F32), SDS((N_HEAD, HEAD, HEAD), F32),
                   SDS((N_HEAD, HEAD, 2 * HEAD), F32), SDS((N_HEAD, HEAD), F32)] + ex.out_shape,
        scratch_shapes=[pltpu.VMEM((CHUNK, D_BR), BF16), big(D_BR), big(D_BR), big(D_BR), big(2 * D_BR), big(D_BR),
                        big(D_BR), big(D_BR), big(D_BR), big(2 * D_BR), big(D_IN),
                        pltpu.VMEM((N_VEC, ROWS, D_BR), F32), big(D_BR),
                        pltpu.VMEM((ROWS, D_BR), F32), pltpu.VMEM((ROWS, D_BR), F32), pltpu.VMEM((ROWS, D_BR), F32)]
        + ex.scratch,
        compiler_params=_params(("arbitrary",), 48),
    )(z, dy, h, h, vhb, xcb, rs, ln_g, ln_b, wm, wm_t, bias, cw, cb, wax, wax_t, ba, bx, lam, goa, gob, *ex_arrs)
    return res[:n_out], res[n_out:]


def _in_bwd(dz, w_in_g, x, dh1, pre_g, tm=256):
    t_len = x.shape[0]
    n_tile = t_len // tm

    def body(dz_ref, w_hbm, x_ref, dh1_ref, g_ref, gx_ref, dg_ref, w_s, t_even, t_odd, dg_s, w_sems):
        i = pl.program_id(0)

        @pl.when(i == 0)
        def _():
            loads = [pltpu.make_async_copy(w_hbm.at[s], w_s.at[:, s * W_IN_SHARD:(s + 1) * W_IN_SHARD], w_sems.at[s])
                     for s in range(N_DEV)]
            for cp in loads:
                cp.start()
            dg_s[...] = jnp.zeros_like(dg_s)
            t_odd[...] = jnp.zeros_like(t_odd)
            for cp in loads:
                cp.wait()

        def step(t_new, t_old):
            g = g_ref[...]
            acc = dg_s[...]
            for q in range(tm // TILE_ROWS):
                rows = slice(q * TILE_ROWS, (q + 1) * TILE_ROWS)
                xv = x_ref[rows, :]
                r = lax.rsqrt(_mean_last(xv * xv) + EPS)
                xh = xv * r
                dhn = t_old[rows, :]
                dg = dhn * g
                gx_ref[rows, :] = dh1_ref[rows, :] + r * (dg - xh * _mean_last(dg * xh))
                acc = acc + _fold_rows(dhn * xh)
            dg_s[...] = acc
            t_new[...] = _dot_nt(dz_ref[...], w_s[...])

        @pl.when(i % 2 == 0)
        def _():
            step(t_even, t_odd)

        @pl.when(i % 2 == 1)
        def _():
            step(t_odd, t_even)

        @pl.when(i == n_tile)
        def _():
            dg_ref[...] = jnp.sum(dg_s[...], axis=0, keepdims=True)

    matmul_tile = lambda i: (jnp.minimum(i, n_tile - 1), 0)
    rows_tile = lambda i: (jnp.maximum(i - 1, 0), 0)
    res = pl.pallas_call(
        body, name="in_bwd", grid=(n_tile + 1,),
        in_specs=[pl.BlockSpec((tm, D_IN), matmul_tile), ANY_SPEC, pl.BlockSpec((tm, D_MODEL), rows_tile),
                  pl.BlockSpec((tm, D_MODEL), rows_tile), pl.BlockSpec((1, D_MODEL), lambda i: (0, 0))],
        out_specs=[pl.BlockSpec((tm, D_MODEL), rows_tile), pl.BlockSpec((1, D_MODEL), lambda i: (0, 0))],
        out_shape=[SDS((t_len, D_MODEL), F32), SDS((1, D_MODEL), F32)],
        scratch_shapes=[pltpu.VMEM((D_MODEL, D_IN), BF16), pltpu.VMEM((tm, D_MODEL), F32), pltpu.VMEM((tm, D_MODEL), F32),
                        pltpu.VMEM((ROWS, D_MODEL), F32), pltpu.SemaphoreType.DMA((N_DEV,))],
        compiler_params=_params(("arbitrary",), 54),
    )(dz, w_in_g, x, dh1, pre_g)
    return res[0], res[1]


def _grad_w(a, b, bn, shard_major, name, tk=1024, ex_arrs=(), ex_scatter=()):
    t_len, m = a.shape
    n = b.shape[1]
    n_j, n_k = n // bn, t_len // tk
    ex = _Exchange(ex_arrs, ex_scatter)

    def body(a_ref, b_ref, *refs):
        ex_in, o_ref, ex_out = refs[:ex.n], refs[ex.n], refs[ex.n + 1:2 * ex.n + 1]
        acc_s, ex_sems = refs[2 * ex.n + 1], refs[2 * ex.n + 2:]
        j, k = pl.program_id(0), pl.program_id(1)
        if ex.n:
            @pl.when(jnp.logical_and(j == 0, k == 0))
            def _():
                ex.start(ex_in, ex_out, ex_sems)

        @pl.when(k == 0)
        def _():
            acc_s[...] = jnp.zeros_like(acc_s)

        acc_s[...] += _dot_tn(a_ref[...], b_ref[...])

        @pl.when(k == n_k - 1)
        def _():
            o_ref[...] = acc_s[...].astype(BF16)

        if ex.n:
            @pl.when(jnp.logical_and(j == n_j - 1, k == n_k - 1))
            def _():
                ex.wait(ex_in, ex_out, ex_sems)

    if shard_major:
        out_spec, out_shape = pl.BlockSpec((None, m, bn), lambda j, k: (j, 0, 0)), SDS((n_j, m, bn), BF16)
    else:
        out_spec, out_shape = pl.BlockSpec((m, bn), lambda j, k: (0, j)), SDS((m, n), BF16)
    res = pl.pallas_call(
        body, name=name, grid=(n_j, n_k),
        in_specs=[pl.BlockSpec((tk, m), lambda j, k: (k, 0)), pl.BlockSpec((tk, bn), lambda j, k: (k, j))]
        + [ANY_SPEC] * ex.n,
        out_specs=[out_spec] + [ANY_SPEC] * ex.n, out_shape=[out_shape] + ex.out_shape,
        scratch_shapes=[pltpu.VMEM((m, bn), F32)] + (ex.scratch if ex.n else []),
        compiler_params=_params(("arbitrary", "arbitrary"), 40),
    )(a, b, *ex_arrs)
    return res[0], res[1:]


RS_CHIPS = (6, 2, 4, 0)
RS_SLOTS = (0, 1, 2, 4, 6)


def _grad_w_in_pairs(hn, dz, ex_arrs, ex_scatter, tk=1024):
    t_len = hn.shape[0]
    n_k = t_len // tk
    n_ph = len(RS_CHIPS)
    ex = _Exchange(ex_arrs, ex_scatter)
    me_out = 4 * lax.axis_index("x") + 2 * lax.axis_index("y") + lax.axis_index("c")
    order = jnp.stack([(me_out ^ chip) // 2 for chip in RS_CHIPS]).astype(jnp.int32)
    slots = jnp.stack([me_out ^ k for k in RS_SLOTS]).astype(jnp.int32)
    shard = W_IN_SHARD

    def body(order_ref, a_ref, b_ref, *refs):
        ex_in, parts_hbm, ex_out = refs[:ex.n], refs[ex.n], refs[ex.n + 1:2 * ex.n + 1]
        (acc_s, tb_s, stage_s, rx_s, d2d_send, d2d_recv, ici_send, ici_recv, sib_sems,
         loc_sem) = refs[2 * ex.n + 1:2 * ex.n + 11]
        ex_sems = refs[2 * ex.n + 11:]
        j, k = pl.program_id(0), pl.program_id(1)
        x, y, c, me = _mesh_place()
        sib = _peer(x, y, c, SIBLING)[0]

        def to_sibling(p):
            return _remote(stage_s.at[0], rx_s.at[p % 2], d2d_send.at[p], d2d_recv.at[p], sib)

        def over_ici(p):
            dev = _peer(x, y, c, RS_CHIPS[p])[0]
            return _remote(stage_s.at[1], parts_hbm.at[me], ici_send.at[p], ici_recv.at[p], dev)

        def own_chip():
            return (_remote(stage_s.at[0], parts_hbm.at[me], sib_sems.at[0], sib_sems.at[1], sib),
                    pltpu.make_async_copy(stage_s.at[1], parts_hbm.at[me], loc_sem.at[0]))

        @pl.when(jnp.logical_and(j == 0, k == 0))
        def _():
            ex.start(ex_in, ex_out, ex_sems)

        for p in range(n_ph - 1):
            for core in (0, 1):
                @pl.when(jnp.logical_and(jnp.logical_and(j == p + 1, k == 0), c == core))
                def _(p=p, core=core):
                    to_sibling(p).wait_recv()
                    if p >= 1:
                        over_ici(p - 1).wait_send()
                    mine = acc_s[:, core * shard:(core + 1) * shard]
                    stage_s[1] = (mine + rx_s[p % 2].astype(F32)).astype(BF16)
                    over_ici(p).start()

        @pl.when(k == 0)
        def _():
            acc_s[...] = jnp.zeros_like(acc_s)

        a = a_ref[...]
        acc_s[:, 0:W_BODY] += _dot_tn(a, b_ref[:, 0:W_BODY])
        acc_s[:, shard:shard + W_BODY] += _dot_tn(a, b_ref[:, shard:shard + W_BODY])
        tb_s[:, 0:W_TAIL] = b_ref[:, W_BODY:shard]
        tb_s[:, W_TAIL:2 * W_TAIL] = b_ref[:, shard + W_BODY:2 * shard]
        tails = _dot_tn(a, tb_s[...])
        acc_s[:, W_BODY:shard] += tails[:, 0:W_TAIL]
        acc_s[:, shard + W_BODY:2 * shard] += tails[:, W_TAIL:2 * W_TAIL]

        for p in range(n_ph):
            for core in (0, 1):
                @pl.when(jnp.logical_and(jnp.logical_and(j == p, k == n_k - 1), c == core))
                def _(p=p, core=core):
                    same = acc_s[:, core * shard:(core + 1) * shard]
                    other = acc_s[:, (1 - core) * shard:(2 - core) * shard]
                    if p >= 1:
                        to_sibling(p - 1).wait_send()
                    stage_s[0] = other.astype(BF16)
                    if p < n_ph - 1:
                        to_sibling(p).start()
                    else:
                        over_ici(n_ph - 2).wait_send()
                        stage_s[1] = same.astype(BF16)
                        for cp in own_chip():
                            cp.start()

        @pl.when(jnp.logical_and(j == n_ph - 1, k == n_k - 1))
        def _():
            to_sib, local = own_chip()
            to_sib.wait_send()
            local.wait()
            _remote(stage_s.at[0], parts_hbm.at[_peer(x, y, c, SIBLING)[1]], sib_sems.at[0], sib_sems.at[1], sib).wait_recv()
            for p in range(n_ph - 1):
                dev, lin = _peer(x, y, c, RS_CHIPS[p])
                _remote(stage_s.at[0], parts_hbm.at[lin], ici_send.at[p], ici_recv.at[p], dev).wait_recv()
            ex.wait(ex_in, ex_out, ex_sems)

    dma = lambda n: pltpu.SemaphoreType.DMA((n,))
    grid_spec = pltpu.PrefetchScalarGridSpec(
        num_scalar_prefetch=1, grid=(n_ph, n_k),
        in_specs=[pl.BlockSpec((tk, D_MODEL), lambda j, k, order: (k, 0)),
                  pl.BlockSpec((tk, 2 * shard), lambda j, k, order: (k, order[j]))] + [ANY_SPEC] * ex.n,
        out_specs=[ANY_SPEC] * (1 + ex.n),
        scratch_shapes=[pltpu.VMEM((D_MODEL, 2 * shard), F32), pltpu.VMEM((tk, 2 * W_TAIL), BF16),
                        pltpu.VMEM((2, D_MODEL, shard), BF16),
                        pltpu.VMEM((2, D_MODEL, shard), BF16), dma(n_ph - 1), dma(n_ph - 1), dma(n_ph - 1),
                        dma(n_ph - 1), dma(2), dma(1)] + ex.scratch)
    res = pl.pallas_call(
        body, name="grad_w_in", grid_spec=grid_spec,
        out_shape=[SDS((N_DEV, D_MODEL, shard), BF16)] + ex.out_shape,
        compiler_params=_params(("arbitrary", "arbitrary"), 54),
    )(order, hn, dz, *ex_arrs)
    return res[0], slots, res[1:]


def _sum_parts(parts, name):
    def body(p_ref, o_ref):
        g = p_ref[0].astype(F32)
        for s in range(1, parts.shape[0]):
            g = g + p_ref[s].astype(F32)
        o_ref[...] = g

    return pl.pallas_call(body, name=name, out_shape=SDS(parts.shape[1:], F32))(parts)


def _adamw_math(g, w_ref, m_ref, v_ref, g_ref, d_ref, nm_ref, nv_ref):
    c1 = 1.0 - ADAM_B1 ** ADAM_STEP
    c2 = 1.0 - ADAM_B2 ** ADAM_STEP
    g_ref[...] = g
    nm = ADAM_B1 * m_ref[...] + (1.0 - ADAM_B1) * g
    nv = ADAM_B2 * v_ref[...] + (1.0 - ADAM_B2) * (g * g)
    nm_ref[...] = nm
    nv_ref[...] = nv
    d_ref[...] = -ADAM_LR * ((nm / c1) / (jnp.sqrt(nv / c2) + ADAM_EPS) + ADAM_WD * w_ref[...])


def _adamw(parts, w, m, v, name, tr):
    rows, cols = w.shape
    n_parts = parts.shape[0]

    def body(p_ref, *refs):
        g = p_ref[0].astype(F32)
        for s in range(1, n_parts):
            g = g + p_ref[s].astype(F32)
        _adamw_math(g, *refs)

    tile = pl.BlockSpec((tr, cols), lambda i: (i, 0))
    return pl.pallas_call(
        body, name=name, grid=(rows // tr,),
        in_specs=[pl.BlockSpec((n_parts, tr, cols), lambda i: (0, i, 0)), tile, tile, tile],
        out_specs=[tile] * 4, out_shape=[SDS((rows, cols), F32)] * 4,
        compiler_params=_params(("arbitrary",), 40),
    )(parts, w, m, v)


def _adamw_unpacked(grads, triples, name):
    n = len(triples)
    n_rows = [t[0].shape[0] for t in triples]

    def body(g_ref, *refs):
        ins, outs = refs[:3 * n], refs[3 * n:]
        row = 0
        for i in range(n):
            _adamw_math(g_ref[row:row + n_rows[i], :], *ins[3 * i:3 * i + 3], *outs[4 * i:4 * i + 4])
            row += n_rows[i]
        outs[4 * n][...] = g_ref[row:row + ROWS, :]

    out_shape = [SDS((r, LANES), F32) for r in n_rows for _ in range(4)] + [SDS((ROWS, LANES), F32)]
    return pl.pallas_call(
        body, name=name, out_shape=out_shape,
        compiler_params=pltpu.CompilerParams(vmem_limit_bytes=40 * MIB),
    )(grads, *[a for t in triples for a in t])


def _adamw_slots(parts, slots, w, m, v, name, tr):
    rows, cols = w.shape
    n_slots = slots.shape[0]

    def body(slots_ref, *refs):
        g = refs[0][...].astype(F32)
        for s in range(1, n_slots):
            g = g + refs[s][...].astype(F32)
        _adamw_math(g, *refs[n_slots:])

    tile = pl.BlockSpec((tr, cols), lambda i, slots: (i, 0))
    part = lambda s: pl.BlockSpec((None, tr, cols), lambda i, slots: (slots[s], i, 0))
    grid_spec = pltpu.PrefetchScalarGridSpec(
        num_scalar_prefetch=1, grid=(rows // tr,),
        in_specs=[part(s) for s in range(n_slots)] + [tile, tile, tile], out_specs=[tile] * 4)
    return pl.pallas_call(
        body, name=name, grid_spec=grid_spec, out_shape=[SDS((rows, cols), F32)] * 4,
        compiler_params=_params(("arbitrary",), 40),
    )(slots, *([parts] * n_slots), w, m, v)


PACKED = ("gmlp_ln_g", "gmlp_ln_b", "gmlp_ws", "gmlp_bs", "conv_b", "w_a", "b_a", "w_x", "b_x", "lam", "gmlp_out_g",
          "lru_out_g", "post_g")
WEIGHTS = ("pre_g", "w_in", "gmlp_ln_g", "gmlp_ln_b", "gmlp_ws", "gmlp_bs", "conv_w", "conv_b", "w_a", "b_a", "w_x",
           "b_x", "lam", "gmlp_out_g", "lru_out_g", "w_out", "post_g", "w_pe", "w_pg")
LANES = 128


PACK_ROWS = 3200


def _pack(parts):
    rows = [p.reshape(-1, LANES) for p in parts]
    used = sum(r.shape[0] for r in rows)
    return jnp.concatenate(rows + [jnp.zeros((PACK_ROWS - used, LANES), F32)], axis=0)


def _pad_rows(a, rows):
    return jnp.concatenate([a, jnp.zeros((rows - a.shape[0],) + a.shape[1:], a.dtype)], axis=0)


def kernel(x, p, pre_g, w_in, gmlp_ln_g, gmlp_ln_b, gmlp_ws, gmlp_bs, conv_w, conv_b, w_a, b_a, w_x, b_x, lam, gmlp_out_g, lru_out_g, w_out, post_g, w_pe, w_pg, loss_target, m_pre_g, m_w_in, m_gmlp_ln_g, m_gmlp_ln_b, m_gmlp_ws, m_gmlp_bs, m_conv_w, m_conv_b, m_w_a, m_b_a, m_w_x, m_b_x, m_lam, m_gmlp_out_g, m_lru_out_g, m_w_out, m_post_g, m_w_pe, m_w_pg, v_pre_g, v_w_in, v_gmlp_ln_g, v_gmlp_ln_b, v_gmlp_ws, v_gmlp_bs, v_conv_w, v_conv_b, v_w_a, v_b_a, v_w_x, v_b_x, v_lam, v_gmlp_out_g, v_lru_out_g, v_w_out, v_post_g, v_w_pe, v_w_pg):
    args = dict(locals())
    weights = {n: args[n] for n in WEIGHTS}
    m_in = {n: args["m_" + n] for n in WEIGHTS}
    v_in = {n: args["v_" + n] for n in WEIGHTS}
    sm = {n: weights[n][0] for n in PACKED}
    shard_rows = D_MODEL // N_DEV
    xs, ps, tgt = x[0], p[0, 0], loss_target[0]

    vec = lambda a: a.reshape(1, -1)
    tril = jnp.tril(jnp.ones((CHUNK, CHUNK), dtype=bool))
    wm32 = jnp.where(tril[None], sm["gmlp_ws"], 0.0)
    wm, wm_t = wm32.astype(BF16), jnp.swapaxes(wm32, 1, 2).astype(BF16)
    bias = jnp.repeat(sm["gmlp_bs"].T, HEAD, axis=1)
    wax32 = jnp.concatenate([sm["w_a"], sm["w_x"]], axis=2)
    wax, wax_t = wax32.astype(BF16), jnp.swapaxes(wax32, 1, 2).astype(BF16)
    ln_g, ln_b = vec(sm["gmlp_ln_g"]), vec(sm["gmlp_ln_b"])
    post_g_v = vec(sm["post_g"])

    hn = _pre_norm(xs, pre_g)
    cw_shard = _pad_rows(conv_w.reshape(CONV_W, HEAD), ROWS)
    z, w_in_g, (cw_g,) = _in_proj(hn, w_in[0].astype(BF16), [cw_shard])
    cw_full = jnp.transpose(cw_g[:, :CONV_W, :], (1, 0, 2)).reshape(CONV_W, D_BR)
    mixer_consts = dict(cw=_pad_rows(cw_full, ROWS), cb=vec(sm["conv_b"]), ba=vec(sm["b_a"]), bx=vec(sm["b_x"]),
                        lam=vec(sm["lam"]), goa=vec(sm["gmlp_out_g"]), gob=vec(sm["lru_out_g"]))
    (y, h, vhb, xcb, v_rs), (w_out_g, w_pe_g, w_pg_g) = _mix_fwd(
        z, ln_g, ln_b, wm, bias, wax=wax, **mixer_consts,
        ex_arrs=[w_out[0].astype(BF16), w_pe[0].astype(BF16), w_pg[0].astype(BF16)], ex_scatter=[False, False, False])
    w_out_f, w_pg_f = w_out_g.reshape(D_MODEL, D_MODEL), w_pg_g.reshape(D_MODEL, D_MODEL)
    h1, ob = _out_proj(y, xs, w_out_f, post_g_v)
    dh2, dgl, h1b, loss_part, d_w_pe = _ple_loss(h1, ps, tgt, w_pg_f, w_pe_g)

    dh1, do, dy, d_post_g = _tail_bwd(dh2, dgl, ob, w_pg_f, w_out_f, post_g_v)
    d_w_out, _ = _grad_w(y, do, 1024, False, "grad_w_out")
    d_w_pg, _ = _grad_w(h1b, dgl, 1024, False, "grad_w_pg")
    (dz, vecs, d_ws, d_wax, d_bs), (parts_out, parts_pg, parts_pe) = _mix_bwd(
        z, dy, h, vhb, xcb, v_rs, ln_g, ln_b, wm, wm_t, bias, wax=wax, wax_t=wax_t, **mixer_consts,
        ex_arrs=[d_w_out.reshape(N_DEV, shard_rows, D_MODEL), d_w_pg.reshape(N_DEV, shard_rows, D_MODEL), d_w_pe],
        ex_scatter=[True, True, True])

    small = {"gmlp_ln_g": vecs[V_LN_G], "gmlp_ln_b": vecs[V_LN_B], "gmlp_ws": d_ws, "gmlp_bs": d_bs,
             "conv_b": vecs[V_CONV_B], "w_a": d_wax[:, :, :HEAD], "b_a": vecs[V_B_A], "w_x": d_wax[:, :, HEAD:],
             "b_x": vecs[V_B_X], "lam": vecs[V_LAM], "gmlp_out_g": vecs[V_GOUT_A], "lru_out_g": vecs[V_GOUT_B],
             "post_g": d_post_g}
    small_part = _pack([small[n] for n in PACKED] + [loss_part]).reshape(N_DEV, PACK_ROWS // N_DEV, LANES)
    d_cw_blocks = jnp.transpose(vecs[V_CONV_W:V_CONV_W + CONV_W].reshape(CONV_W, N_DEV, HEAD), (1, 0, 2))
    d_cw_blocks = jnp.concatenate([d_cw_blocks, jnp.zeros((N_DEV, ROWS - CONV_W, HEAD), F32)], axis=1)
    parts_in, slots_in, (small_blocks, parts_cw) = _grad_w_in_pairs(
        hn, dz, ex_arrs=[small_part, d_cw_blocks], ex_scatter=[True, True])
    small_sum = _sum_parts(small_blocks, "sum_small")
    grad_x, d_pre_g = _in_bwd(dz, w_in_g, xs, dh1, pre_g)
    pre_rows = D_MODEL // LANES
    small_all, parts_pre = _exchange([small_sum, d_pre_g.reshape(pre_rows, LANES)], False, "gather_small_grads")

    pad_cw = lambda a: _pad_rows(a.reshape(CONV_W, HEAD), ROWS)
    flat = lambda a: a.reshape(pre_rows, LANES)
    outs = {
        "w_in": _adamw_slots(parts_in, slots_in, w_in[0], m_w_in[0], v_w_in[0], "adamw_w_in", 256),
        "w_out": _adamw(parts_out, w_out[0], m_w_out[0], v_w_out[0], "adamw_w_out", 128),
        "w_pe": _adamw(parts_pe, w_pe[0], m_w_pe[0], v_w_pe[0], "adamw_w_pe", 256),
        "w_pg": _adamw(parts_pg, w_pg[0], m_w_pg[0], v_w_pg[0], "adamw_w_pg", 128),
        "conv_w": [a[:CONV_W] for a in
                   _adamw(parts_cw, pad_cw(conv_w), pad_cw(m_conv_w), pad_cw(v_conv_w), "adamw_conv_w", ROWS)],
        "pre_g": _adamw(parts_pre, flat(pre_g), flat(m_pre_g), flat(v_pre_g), "adamw_pre_g", pre_rows),
    }
    as_rows = lambda a: a.reshape(-1, LANES)
    small_res = _adamw_unpacked(small_all.reshape(PACK_ROWS, LANES),
                                [(as_rows(weights[n]), as_rows(m_in[n]), as_rows(v_in[n])) for n in PACKED], "adamw_small")
    for i, n in enumerate(PACKED):
        outs[n] = small_res[4 * i:4 * i + 4]
    loss = small_res[-1][0, 0]

    result = [loss, grad_x[None]]
    for q in range(4):
        result += [outs[n][q].reshape(weights[n].shape) for n in WEIGHTS]
    return tuple(result)
```

```python
import jax
import jax.numpy as jnp
from jax import lax
from jax.experimental import pallas as pl
from jax.experimental.pallas import tpu as pltpu

F32 = jnp.float32
BF16 = jnp.bfloat16
SDS = jax.ShapeDtypeStruct

D_MODEL = 2048
D_BR = 1024
D_IN = 5 * D_BR
D_PLE = 256
N_HEAD = 8
HEAD = 128
CHUNK = 128
ROWS = 8
N_GROUP = CHUNK // ROWS
N_DEV = 8
W_IN_SHARD = D_IN // N_DEV
EPS = 1e-6
LRU_C = 8.0
CONV_W = 4
MIB = 1 << 20

ADAM_LR, ADAM_B1, ADAM_B2, ADAM_EPS, ADAM_WD, ADAM_STEP = 0.001, 0.9, 0.999, 1e-08, 0.01, 10

_GELU_C = 0.7978845608028654
_GELU_A = 0.044715

V_LN_G, V_LN_B, V_CONV_B, V_B_A, V_B_X, V_LAM, V_GOUT_A, V_GOUT_B, V_CONV_W = 0, 1, 2, 3, 4, 5, 6, 7, 8
N_VEC = 16


def _params(sem, vmem_mib):
    return pltpu.CompilerParams(dimension_semantics=sem, vmem_limit_bytes=int(vmem_mib * MIB))


def _sig(x):
    return 0.5 * jnp.tanh(0.5 * x) + 0.5


def _gelu(x, with_grad=False):
    sq = x * x
    t = jnp.tanh(x * (_GELU_C + (_GELU_C * _GELU_A) * sq))
    half, one_t = 0.5 * x, 1.0 + t
    if not with_grad:
        return half * one_t
    grad = 0.5 * one_t + half * ((1.0 - t) * one_t) * (_GELU_C + (3.0 * _GELU_C * _GELU_A) * sq)
    return half * one_t, grad


def _silu_grad(s, xs):
    return s + xs * (1.0 - s)


def _neg_expm1(y, exp_y):
    series = -y * (1.0 + y * (0.5 + y * (1.0 / 6.0)))
    return jnp.where(y > -0.01, series, 1.0 - exp_y)


def _softplus(x):
    return jnp.maximum(x, 0.0) + jnp.log(1.0 + jnp.exp(-jnp.abs(x)))


def _row_ids(width):
    return lax.broadcasted_iota(jnp.int32, (ROWS, width), 0)


def _shift_down(cur, prev, k, rid):
    return jnp.where(rid >= k, pltpu.roll(cur, k, 0), pltpu.roll(prev, k, 0))


def _shift_up(cur, nxt, k, rid):
    return jnp.where(rid < ROWS - k, pltpu.roll(cur, ROWS - k, 0), pltpu.roll(nxt, ROWS - k, 0))


def _mean_last(x):
    return jnp.mean(x, axis=-1, keepdims=True)


def _rows(g):
    return pl.ds(pl.multiple_of(g * ROWS, ROWS), ROWS)


TILE_ROWS = 16


def _tile_rows(q):
    return pl.ds(pl.multiple_of(q * TILE_ROWS, TILE_ROWS), TILE_ROWS)


UNROLL = 4
TILE_UNROLL = 8


def _loop(n, body, init, unroll=UNROLL):
    def wide(i, carry):
        for u in range(unroll):
            carry = body(i * unroll + u, carry)
        return carry

    return lax.fori_loop(0, n // unroll, wide, init)


def _fold_rows(x):
    return x[0:ROWS, :] + x[ROWS:TILE_ROWS, :]


def _bcast_row(x, r):
    return jnp.broadcast_to(x[r:r + 1, :], x.shape)


def _dot(a, b):
    return jnp.dot(a, b, preferred_element_type=F32)


def _dot_nt(a, b):
    return lax.dot_general(a, b, (((1,), (1,)), ((), ())), preferred_element_type=F32)


def _dot_tn(a, b):
    return lax.dot_general(a, b, (((0,), (0,)), ((), ())), preferred_element_type=F32)


def _mesh_place():
    x, y, c = lax.axis_index("x"), lax.axis_index("y"), lax.axis_index("c")
    return x, y, c, 4 * x + 2 * y + c


def _peer(x, y, c, k):
    px = 1 - x if k & 4 else x
    py = 1 - y if k & 2 else y
    pc = 1 - c if k & 1 else c
    return (px, py, pc), 4 * px + 2 * py + pc


def _remote(src, dst, send_sem, recv_sem, dev):
    return pltpu.make_async_remote_copy(src_ref=src, dst_ref=dst, send_sem=send_sem, recv_sem=recv_sem, device_id=dev,
                                        device_id_type=pl.DeviceIdType.MESH)


ANY_SPEC = pl.BlockSpec(memory_space=pl.ANY)
EXCHANGE_ORDER = (6, 7, 2, 3, 4, 5, 1)


class _Exchange:
    def __init__(self, arrs, scatter):
        self.n = len(arrs)
        self.scatter = tuple(scatter)
        self.out_shape = [SDS(a.shape if s else (N_DEV,) + a.shape, a.dtype) for a, s in zip(arrs, scatter)]
        self.scratch = [pltpu.SemaphoreType.DMA((self.n * N_DEV,)), pltpu.SemaphoreType.DMA((self.n * N_DEV,)),
                        pltpu.SemaphoreType.DMA((self.n,))]

    def _copies(self, ins, outs, sems):
        send_sems, recv_sems, local_sems = sems
        x, y, c, me = _mesh_place()
        local, sends, recvs = [], [], []
        for a in range(self.n):
            src = ins[a].at[me] if self.scatter[a] else ins[a]
            local.append(pltpu.make_async_copy(src, outs[a].at[me], local_sems.at[a]))
        for k in EXCHANGE_ORDER:
            dev, lin = _peer(x, y, c, k)
            for a in range(self.n):
                src = ins[a].at[lin] if self.scatter[a] else ins[a]
                pair = (send_sems.at[a * N_DEV + k], recv_sems.at[a * N_DEV + k], dev)
                sends.append(_remote(src, outs[a].at[me], *pair))
                recvs.append(_remote(src, outs[a].at[lin], *pair))
        return local, sends, recvs

    def start(self, ins, outs, sems):
        local, sends, _ = self._copies(ins, outs, sems)
        for cp in local + sends:
            cp.start()

    def start_spread(self, ins, outs, sems, step):
        local, sends, _ = self._copies(ins, outs, sems)
        for s in range(N_DEV - 1):
            @pl.when(step == s)
            def _(s=s):
                for cp in (local if s == 0 else []) + sends[s * self.n:(s + 1) * self.n]:
                    cp.start()

    def wait(self, ins, outs, sems):
        local, sends, recvs = self._copies(ins, outs, sems)
        for cp in recvs:
            cp.wait_recv()
        for cp in sends:
            cp.wait_send()
        for cp in local:
            cp.wait()


def _exchange(arrs, scatter, name):
    ex = _Exchange(arrs, [scatter] * len(arrs))
    n = ex.n

    def body(*refs):
        ins, outs, sems = refs[:n], refs[n:2 * n], refs[2 * n:]
        ex.start(ins, outs, sems)
        ex.wait(ins, outs, sems)

    return pl.pallas_call(
        body, name=name, out_shape=ex.out_shape, in_specs=[ANY_SPEC] * n, out_specs=[ANY_SPEC] * n,
        scratch_shapes=ex.scratch,
    )(*arrs)


def _pre_norm(x, pre_g, tm=512):
    t_len = x.shape[0]

    def body(x_ref, g_ref, hn_ref):
        g = g_ref[...]

        def rows_body(q, _):
            rows = _tile_rows(q)
            xv = x_ref[rows, :]
            hn_ref[rows, :] = (xv * lax.rsqrt(_mean_last(xv * xv) + EPS) * g).astype(BF16)
            return 0

        _loop(tm // TILE_ROWS, rows_body, 0, unroll=TILE_UNROLL)

    tile = pl.BlockSpec((tm, D_MODEL), lambda i: (i, 0))
    return pl.pallas_call(
        body, name="pre_norm", grid=(t_len // tm,),
        in_specs=[tile, pl.BlockSpec((1, D_MODEL), lambda i: (0, 0))], out_specs=tile,
        out_shape=SDS((t_len, D_MODEL), BF16),
        compiler_params=_params(("arbitrary",), 24),
    )(x, pre_g)


CHIP_ORDER = (0, 2, 4, 6)
W_BODY, W_TAIL = 512, 128
SIBLING = 1
ICI_MASKS = (2, 4, 6)
DIRECT_MASKS = (SIBLING,) + ICI_MASKS
Y_NEIGHBOUR, X_NEIGHBOUR, DIAGONAL = 2, 4, 6
W_DIRECT = (SIBLING, Y_NEIGHBOUR, X_NEIGHBOUR)


def _in_proj(hn, w_shard, others, tm=1024):
    t_len = hn.shape[0]
    n_i = t_len // tm
    n_o = len(others)
    me_out = 4 * lax.axis_index("x") + 2 * lax.axis_index("y") + lax.axis_index("c")
    order = jnp.stack([(me_out ^ chip) // 2 for chip in CHIP_ORDER]).astype(jnp.int32)

    def body(order_ref, hn_ref, w_hbm, *refs):
        o_in = refs[:n_o]
        z_ref, wg_hbm = refs[n_o], refs[n_o + 1]
        o_out = refs[n_o + 2:2 * n_o + 2]
        (wbuf, tail_s, send_w, recv_w, fsend_w, frecv_w, send_o, recv_o, fsend_o, frecv_o, wb_sems, loc_sems, rsend,
         rrecv) = refs[2 * n_o + 2:]
        j, i = pl.program_id(0), pl.program_id(1)
        x, y, c, me = _mesh_place()
        sib = _peer(x, y, c, SIBLING)[0]

        def relay(core):
            src, dst = (Y_NEIGHBOUR, X_NEIGHBOUR) if core == 0 else (X_NEIGHBOUR, Y_NEIGHBOUR)
            held, diag = _peer(x, y, c, src)[1], _peer(x, y, c, DIAGONAL)[1]
            pair = (rsend.at[0], rrecv.at[0], _peer(x, y, c, dst)[0])
            return _remote(wbuf.at[held], wbuf.at[held], *pair), _remote(wbuf.at[diag], wbuf.at[diag], *pair)

        def direct(k, a=None):
            dev, lin = _peer(x, y, c, k)
            if a is None:
                return (_remote(w_hbm, wbuf.at[me], send_w.at[k], recv_w.at[k], dev),
                        _remote(w_hbm, wbuf.at[lin], send_w.at[k], recv_w.at[k], dev))
            pair = (send_o.at[a * N_DEV + k], recv_o.at[a * N_DEV + k], dev)
            return _remote(o_in[a], o_out[a].at[me], *pair), _remote(o_in[a], o_out[a].at[lin], *pair)

        def passed(k, a=None):
            mine, theirs = _peer(x, y, c, k)[1], _peer(x, y, c, k ^ SIBLING)[1]
            if a is None:
                pair = (fsend_w.at[k], frecv_w.at[k], sib)
                return _remote(wbuf.at[mine], wbuf.at[mine], *pair), _remote(wbuf.at[theirs], wbuf.at[theirs], *pair)
            pair = (fsend_o.at[a * N_DEV + k], frecv_o.at[a * N_DEV + k], sib)
            return (_remote(o_out[a].at[mine], o_out[a].at[mine], *pair),
                    _remote(o_out[a].at[theirs], o_out[a].at[theirs], *pair))

        def own_copies():
            return [pltpu.make_async_copy(o_in[a], o_out[a].at[me], loc_sems.at[1 + a]) for a in range(n_o)]

        @pl.when(jnp.logical_and(j == 0, i == 0))
        def _():
            own = pltpu.make_async_copy(w_hbm, wbuf.at[me], loc_sems.at[0])
            own.start()
            for cp in own_copies():
                cp.start()
            for k in W_DIRECT:
                direct(k)[0].start()
            for k in DIRECT_MASKS:
                for a in range(n_o):
                    direct(k, a)[0].start()
            own.wait()

        low = 2 * order_ref[j]

        for jp, chip in enumerate(CHIP_ORDER):
            @pl.when(jnp.logical_and(j == jp, i == 0))
            def _(jp=jp, chip=chip):
                if chip == 0:
                    direct(SIBLING)[1].wait_recv()
                elif chip == Y_NEIGHBOUR:
                    for mask in (Y_NEIGHBOUR, X_NEIGHBOUR):
                        direct(mask)[1].wait_recv()
                        passed(mask)[0].start()
                    for core in (0, 1):
                        @pl.when(c == core)
                        def _(core=core):
                            relay(core)[0].start()
                    passed(Y_NEIGHBOUR)[1].wait_recv()
                elif chip == X_NEIGHBOUR:
                    passed(X_NEIGHBOUR)[1].wait_recv()
                    for core in (0, 1):
                        @pl.when(c == core)
                        def _(core=core):
                            relay(core)[1].wait_recv()
                    passed(DIAGONAL)[0].start()
                    for k in ICI_MASKS:
                        for a in range(n_o):
                            direct(k, a)[1].wait_recv()
                            passed(k, a)[0].start()
                else:
                    passed(DIAGONAL)[1].wait_recv()
                for half in (0, 1):
                    pltpu.make_async_copy(wbuf.at[low + half], wg_hbm.at[low + half], wb_sems.at[2 * jp + half]).start()
                tail_s[:, 0:W_TAIL] = wbuf[low, :, W_BODY:W_IN_SHARD]
                tail_s[:, W_TAIL:2 * W_TAIL] = wbuf[low + 1, :, W_BODY:W_IN_SHARD]

        hn = hn_ref[...]
        z_ref[:, 0:W_BODY] = _dot(hn, wbuf[low, :, 0:W_BODY])
        z_ref[:, W_IN_SHARD:W_IN_SHARD + W_BODY] = _dot(hn, wbuf[low + 1, :, 0:W_BODY])
        tails = _dot(hn, tail_s[...])
        z_ref[:, W_BODY:W_IN_SHARD] = tails[:, 0:W_TAIL]
        z_ref[:, W_IN_SHARD + W_BODY:2 * W_IN_SHARD] = tails[:, W_TAIL:2 * W_TAIL]

        @pl.when(jnp.logical_and(j == len(CHIP_ORDER) - 1, i == n_i - 1))
        def _():
            for a in range(n_o):
                direct(SIBLING, a)[1].wait_recv()
            for k in ICI_MASKS:
                for a in range(n_o):
                    passed(k, a)[1].wait_recv()
            for k in W_DIRECT:
                direct(k)[0].wait_send()
            for core in (0, 1):
                @pl.when(c == core)
                def _(core=core):
                    relay(core)[0].wait_send()
            for k in DIRECT_MASKS:
                for a in range(n_o):
                    direct(k, a)[0].wait_send()
            for k in ICI_MASKS:
                passed(k)[0].wait_send()
                for a in range(n_o):
                    passed(k, a)[0].wait_send()
            for cp in own_copies():
                cp.wait()
            for jj in range(N_DEV):
                pltpu.make_async_copy(wbuf.at[0], wg_hbm.at[0], wb_sems.at[jj]).wait()

    dma = lambda n: pltpu.SemaphoreType.DMA((n,))
    grid_spec = pltpu.PrefetchScalarGridSpec(
        num_scalar_prefetch=1, grid=(len(CHIP_ORDER), n_i),
        in_specs=[pl.BlockSpec((tm, D_MODEL), lambda j, i, order: (i, 0)), ANY_SPEC] + [ANY_SPEC] * n_o,
        out_specs=[pl.BlockSpec((tm, 2 * W_IN_SHARD), lambda j, i, order: (i, order[j])), ANY_SPEC] + [ANY_SPEC] * n_o,
        scratch_shapes=[pltpu.VMEM((N_DEV, D_MODEL, W_IN_SHARD), BF16), pltpu.VMEM((D_MODEL, 2 * W_TAIL), BF16),
                        dma(N_DEV), dma(N_DEV), dma(N_DEV), dma(N_DEV),
                        dma(n_o * N_DEV), dma(n_o * N_DEV), dma(n_o * N_DEV), dma(n_o * N_DEV), dma(N_DEV), dma(1 + n_o),
                        dma(1), dma(1)])
    res = pl.pallas_call(
        body, name="in_proj", grid_spec=grid_spec,
        out_shape=[SDS((t_len, D_IN), F32), SDS((N_DEV, D_MODEL, W_IN_SHARD), BF16)]
        + [SDS((N_DEV,) + o.shape, o.dtype) for o in others],
        compiler_params=_params(("arbitrary", "arbitrary"), 54),
    )(order, hn, w_shard, *others)
    return res[0], res[1], res[2:]


def _conv_rows(cur, prev, cw_ref, cb, rid):
    acc = cw_ref[3:4, :] * cur + cb
    for k in range(1, CONV_W):
        acc = acc + cw_ref[3 - k:4 - k, :] * _shift_down(cur, prev, k, rid)
    return acc


ROW0_LOG_A = -1e30


def _row0_mask(rid):
    return jnp.where(rid == 0, ROW0_LOG_A, 0.0)


def _row0_bias(is_first_group, row0_mask):
    return is_first_group.astype(F32) * row0_mask


def _lru_gates(pa, px, ba, bx, sp8, row0_bias):
    r = _sig(pa + ba)
    i = _sig(px + bx)
    la = row0_bias - r * sp8
    a = jnp.exp(la)
    return r, i, a, _neg_expm1(2.0 * la, a * a)


def _mix_fwd(z, ln_g, ln_b, wm, bias, cw, cb, wax, ba, bx, lam, goa, gob, ex_arrs, ex_scatter):
    t_len = z.shape[0]
    n_chunk = t_len // CHUNK
    assert n_chunk >= N_DEV, "the exchange is started over the first N_DEV - 1 grid steps and waited in the last"
    ex = _Exchange(ex_arrs, ex_scatter)
    n_in, n_out, n_scratch = 13, 5, 7

    def body(*refs):
        (z_ref, lng_ref, lnb_ref, wm_ref, bias_ref, cw_ref, cb_ref, wax_ref, ba_ref, bx_ref, lam_ref, goa_ref,
         gob_ref) = refs[:n_in]
        ex_in = refs[n_in:n_in + ex.n]
        y_ref, h_ref, vhb_ref, xcb_ref, rs_ref = refs[n_in + ex.n:n_in + ex.n + n_out]
        ex_out = refs[n_in + ex.n + n_out:n_in + 2 * ex.n + n_out]
        vn_s, xc_s, mixed_s, pre_s, y_s, carry_s, halo_s = refs[n_in + 2 * ex.n + n_out:n_in + 2 * ex.n + n_out + n_scratch]
        ex_sems = refs[n_in + 2 * ex.n + n_out + n_scratch:]
        c_id = pl.program_id(0)
        rid = _row_ids(D_BR)

        ex.start_spread(ex_in, ex_out, ex_sems, c_id)

        @pl.when(c_id == 0)
        def _():
            carry_s[...] = jnp.zeros_like(carry_s)
            halo_s[...] = jnp.zeros_like(halo_s)

        lng, lnb, cb = lng_ref[...], lnb_ref[...], cb_ref[...]

        def phase1(g, prev):
            rows = _rows(g)
            vg = _gelu(z_ref[rows, D_BR:2 * D_BR])
            xm = vg - _mean_last(vg)
            rs = lax.rsqrt(_mean_last(xm * xm) + EPS)
            vn_s[rows, :] = xm * rs
            rs_ref[rows, :] = jnp.broadcast_to(rs, (ROWS, HEAD))
            xb = z_ref[rows, 3 * D_BR:4 * D_BR]
            xc_s[rows, :] = _conv_rows(xb, prev, cw_ref, cb, rid)
            return xb

        halo_s[...] = _loop(N_GROUP, phase1, halo_s[...], unroll=8)
        vhb_ref[...] = vn_s[...].astype(BF16)
        xcb_ref[...] = xc_s[...].astype(BF16)

        for h in range(N_HEAD):
            cs = slice(h * HEAD, (h + 1) * HEAD)
            mixed_s[:, cs] = _dot(wm_ref[h], (vn_s[:, cs] * lng[:, cs] + lnb[:, cs]).astype(BF16))
            pre = _dot(xcb_ref[:, cs], wax_ref[h])
            pre_s[:, cs] = pre[:, :HEAD]
            pre_s[:, D_BR + h * HEAD:D_BR + (h + 1) * HEAD] = pre[:, HEAD:]

        ba, bx, goa, gob = ba_ref[...], bx_ref[...], goa_ref[...], gob_ref[...]
        sp8 = LRU_C * _softplus(-lam_ref[...])
        row0 = _row0_mask(rid)

        def phase3(g, carry):
            rows = _rows(g)
            ug = _gelu(z_ref[rows, 0:D_BR])
            ga = z_ref[rows, 2 * D_BR:3 * D_BR]
            ya = ug * (mixed_s[rows, :] + bias_ref[rows, :]) * (ga * _sig(ga))
            y_s[rows, 0:D_BR] = ya * lax.rsqrt(_mean_last(ya * ya) + EPS) * goa

            bias0 = _row0_bias(jnp.logical_and(c_id == 0, g == 0), row0)
            _, i, a, m2 = _lru_gates(pre_s[rows, 0:D_BR], pre_s[rows, D_BR:2 * D_BR], ba, bx, sp8, bias0)
            b = jnp.sqrt(m2) * i * xc_s[rows, :]
            for d in (1, 2, 4):
                a_sh = jnp.where(rid >= d, pltpu.roll(a, d, 0), 1.0)
                b_sh = jnp.where(rid >= d, pltpu.roll(b, d, 0), 0.0)
                b = a * b_sh + b
                a = a * a_sh
            hh = b + a * carry
            h_ref[rows, :] = hh
            gb = z_ref[rows, 4 * D_BR:5 * D_BR]
            yb = hh * (gb * _sig(gb))
            y_s[rows, D_BR:2 * D_BR] = yb * lax.rsqrt(_mean_last(yb * yb) + EPS) * gob
            return _bcast_row(hh, ROWS - 1)

        carry_s[...] = _loop(N_GROUP, phase3, carry_s[...])
        y_ref[...] = y_s[...].astype(BF16)

        @pl.when(c_id == n_chunk - 1)
        def _():
            ex.wait(ex_in, ex_out, ex_sems)

    vec = pl.BlockSpec((1, D_BR), lambda i: (0, 0))
    res = pl.pallas_call(
        body, name="mix_fwd", grid=(n_chunk,),
        in_specs=[pl.BlockSpec((CHUNK, D_IN), lambda i: (i, 0)), vec, vec,
                  pl.BlockSpec((N_HEAD, HEAD, HEAD), lambda i: (0, 0, 0)),
                  pl.BlockSpec((CHUNK, D_BR), lambda i: (0, 0)),
                  pl.BlockSpec((ROWS, D_BR), lambda i: (0, 0)), vec,
                  pl.BlockSpec((N_HEAD, HEAD, 2 * HEAD), lambda i: (0, 0, 0)), vec, vec, vec, vec, vec]
        + [ANY_SPEC] * ex.n,
        out_specs=[pl.BlockSpec((CHUNK, 2 * D_BR), lambda i: (i, 0)), pl.BlockSpec((CHUNK, D_BR), lambda i: (i, 0)),
                   pl.BlockSpec((CHUNK, D_BR), lambda i: (i, 0)), pl.BlockSpec((CHUNK, D_BR), lambda i: (i, 0)),
                   pl.BlockSpec((CHUNK, HEAD), lambda i: (i, 0))] + [ANY_SPEC] * ex.n,
        out_shape=[SDS((t_len, 2 * D_BR), BF16), SDS((t_len, D_BR), F32), SDS((t_len, D_BR), BF16),
                   SDS((t_len, D_BR), BF16), SDS((t_len, HEAD), F32)] + ex.out_shape,
        scratch_shapes=[pltpu.VMEM((CHUNK, D_BR), F32), pltpu.VMEM((CHUNK, D_BR), F32), pltpu.VMEM((CHUNK, D_BR), F32),
                        pltpu.VMEM((CHUNK, 2 * D_BR), F32), pltpu.VMEM((CHUNK, 2 * D_BR), F32),
                        pltpu.VMEM((ROWS, D_BR), F32), pltpu.VMEM((ROWS, D_BR), F32)] + ex.scratch,
        compiler_params=_params(("arbitrary",), 32),
    )(z, ln_g, ln_b, wm, bias, cw, cb, wax, ba, bx, lam, goa, gob, *ex_arrs)
    return res[:n_out], res[n_out:]


def _load_weight(w_hbm, w_vmem, sem):
    @pl.when(pl.program_id(0) == 0)
    def _():
        cp = pltpu.make_async_copy(w_hbm, w_vmem, sem)
        cp.start()
        cp.wait()


def _out_proj(y, x, w_out, post_g, tm=512):
    t_len = y.shape[0]

    def body(y_ref, x_ref, w_hbm, g_ref, h1_ref, ob_ref, w_s, o_s, sem):
        _load_weight(w_hbm, w_s, sem)
        o_s[...] = _dot(y_ref[...], w_s[...])
        g = g_ref[...]

        def rows_body(q, _):
            rows = _tile_rows(q)
            o = o_s[rows, :]
            h1_ref[rows, :] = x_ref[rows, :] + o * lax.rsqrt(_mean_last(o * o) + EPS) * g
            ob_ref[rows, :] = o.astype(BF16)
            return 0

        _loop(tm // TILE_ROWS, rows_body, 0, unroll=TILE_UNROLL)

    tile = pl.BlockSpec((tm, D_MODEL), lambda i: (i, 0))
    return pl.pallas_call(
        body, name="out_proj", grid=(t_len // tm,),
        in_specs=[tile, tile, pl.BlockSpec(memory_space=pl.ANY), pl.BlockSpec((1, D_MODEL), lambda i: (0, 0))],
        out_specs=[tile, tile],
        out_shape=[SDS((t_len, D_MODEL), F32), SDS((t_len, D_MODEL), BF16)],
        scratch_shapes=[pltpu.VMEM((D_MODEL, D_MODEL), BF16), pltpu.VMEM((tm, D_MODEL), F32), pltpu.SemaphoreType.DMA],
        compiler_params=_params(("arbitrary",), 44),
    )(y, x, w_out, post_g)


def _ple_loss(h1, p, tgt, w_pg, w_pe_g, tm=256):
    t_len = h1.shape[0]
    n_tile = t_len // tm
    pe_shard = D_MODEL // N_DEV

    def body(h1_ref, p_ref, t_ref, w_hbm, wpe_ref, dh2_ref, dgl_ref, h1b_ref, loss_ref, dwpe_ref, w_s, pe_s, gl_s, acc_s,
             dpe_s, gpe_s, sem):
        _load_weight(w_hbm, w_s, sem)
        i = pl.program_id(0)

        @pl.when(i == 0)
        def _():
            acc_s[...] = jnp.zeros_like(acc_s)
            gpe_s[...] = jnp.zeros_like(gpe_s)

        h1b_ref[...] = h1_ref[...].astype(BF16)
        pb = p_ref[...].astype(BF16)
        for j in range(N_DEV):
            pe_s[:, j * pe_shard:(j + 1) * pe_shard] = _dot(pb, wpe_ref[j])
        gl_s[...] = _dot(h1b_ref[...], w_s[...])

        def rows_body(q, acc):
            rows = _tile_rows(q)
            pe = pe_s[rows, :]
            g = _sig(gl_s[rows, :])
            e = h1_ref[rows, :] + pe * g - t_ref[rows, :]
            dh2 = e * (1.0 / D_MODEL)
            dh2_ref[rows, :] = dh2
            dpe_s[rows, :] = (dh2 * g).astype(BF16)
            dgl_ref[rows, :] = (dh2 * pe * g * (1.0 - g)).astype(BF16)
            return acc + _fold_rows(e * e)

        acc_s[...] = _loop(tm // TILE_ROWS, rows_body, acc_s[...], unroll=TILE_UNROLL)
        gpe_s[...] += _dot_tn(pb, dpe_s[...])

        @pl.when(i == n_tile - 1)
        def _():
            loss_ref[...] = jnp.full(loss_ref.shape, 0.5 / D_MODEL * jnp.sum(acc_s[...]), F32)
            for j in range(N_DEV):
                dwpe_ref[j] = gpe_s[:, j * pe_shard:(j + 1) * pe_shard].astype(BF16)

    tile = pl.BlockSpec((tm, D_MODEL), lambda i: (i, 0))
    pe_blocks = pl.BlockSpec((N_DEV, D_PLE, pe_shard), lambda i: (0, 0, 0))
    return pl.pallas_call(
        body, name="ple_loss", grid=(n_tile,),
        in_specs=[tile, pl.BlockSpec((tm, D_PLE), lambda i: (i, 0)), tile, pl.BlockSpec(memory_space=pl.ANY), pe_blocks],
        out_specs=[tile, tile, tile, pl.BlockSpec((ROWS, HEAD), lambda i: (0, 0)), pe_blocks],
        out_shape=[SDS((t_len, D_MODEL), F32), SDS((t_len, D_MODEL), BF16), SDS((t_len, D_MODEL), BF16),
                   SDS((ROWS, HEAD), F32), SDS((N_DEV, D_PLE, pe_shard), BF16)],
        scratch_shapes=[pltpu.VMEM((D_MODEL, D_MODEL), BF16), pltpu.VMEM((tm, D_MODEL), F32),
                        pltpu.VMEM((tm, D_MODEL), F32), pltpu.VMEM((ROWS, D_MODEL), F32), pltpu.VMEM((tm, D_MODEL), BF16),
                        pltpu.VMEM((D_PLE, D_MODEL), F32), pltpu.SemaphoreType.DMA],
        compiler_params=_params(("arbitrary",), 48),
    )(h1, p, tgt, w_pg, w_pe_g)


def _tail_bwd(dh2, dgl, ob, w_pg, w_out, post_g, tm=256):
    t_len = dh2.shape[0]
    n_tile = t_len // tm

    def body(dh2_ref, dgl_ref, ob_ref, wpg_hbm, wout_hbm, g_ref, dh1_ref, do_ref, dy_ref, dg_ref, wpg_s, wout_s, t_s,
             acc_s, sems):
        _load_weight(wpg_hbm, wpg_s, sems.at[0])
        _load_weight(wout_hbm, wout_s, sems.at[1])
        i = pl.program_id(0)

        @pl.when(i == 0)
        def _():
            acc_s[...] = jnp.zeros_like(acc_s)

        t_s[...] = _dot_nt(dgl_ref[...], wpg_s[...])
        g = g_ref[...]

        def rows_body(q, acc):
            rows = _tile_rows(q)
            dh1 = dh2_ref[rows, :] + t_s[rows, :]
            dh1_ref[rows, :] = dh1
            o = ob_ref[rows, :].astype(F32)
            rr = lax.rsqrt(_mean_last(o * o) + EPS)
            on = o * rr
            dog = dh1 * g
            do_ref[rows, :] = (rr * (dog - on * _mean_last(dog * on))).astype(BF16)
            return acc + _fold_rows(dh1 * on)

        acc_s[...] = _loop(tm // TILE_ROWS, rows_body, acc_s[...], unroll=TILE_UNROLL)
        dy_ref[...] = _dot_nt(do_ref[...], wout_s[...]).astype(BF16)

        @pl.when(i == n_tile - 1)
        def _():
            dg_ref[...] = jnp.sum(acc_s[...], axis=0, keepdims=True)

    tile = pl.BlockSpec((tm, D_MODEL), lambda i: (i, 0))
    vec = pl.BlockSpec((1, D_MODEL), lambda i: (0, 0))
    hbm = pl.BlockSpec(memory_space=pl.ANY)
    return pl.pallas_call(
        body, name="tail_bwd", grid=(n_tile,),
        in_specs=[tile, tile, tile, hbm, hbm, vec],
        out_specs=[tile, tile, tile, vec],
        out_shape=[SDS((t_len, D_MODEL), F32), SDS((t_len, D_MODEL), BF16), SDS((t_len, D_MODEL), BF16),
                   SDS((1, D_MODEL), F32)],
        scratch_shapes=[pltpu.VMEM((D_MODEL, D_MODEL), BF16), pltpu.VMEM((D_MODEL, D_MODEL), BF16),
                        pltpu.VMEM((tm, D_MODEL), F32), pltpu.VMEM((ROWS, D_MODEL), F32), pltpu.SemaphoreType.DMA((2,))],
        compiler_params=_params(("arbitrary",), 48),
    )(dh2, dgl, ob, w_pg, w_out, post_g)


def _mix_bwd(z, dy, h, vhb, xcb, rs, ln_g, ln_b, wm, wm_t, bias, cw, cb, wax, wax_t, ba, bx, lam, goa, gob, ex_arrs,
             ex_scatter):
    t_len = z.shape[0]
    n_chunk = t_len // CHUNK
    assert n_chunk >= N_DEV, "the exchange is started over the first N_DEV - 1 grid steps and waited in the last"
    halo_blocks = CHUNK // ROWS
    ex = _Exchange(ex_arrs, ex_scatter)
    n_in, n_out, n_scratch = 21, 5, 16

    def body(*refs):
        (z_ref, dy_ref, h_ref, hhalo_ref, vhb_ref, xcb_ref, rs_ref, lng_ref, lnb_ref, wm_ref, wmt_ref, bias_ref, cw_ref,
         cb_ref, wax_ref, waxt_ref, ba_ref, bx_ref, lam_ref, goa_ref, gob_ref) = refs[:n_in]
        ex_in = refs[n_in:n_in + ex.n]
        dz_ref, vecs_ref, dws_ref, dwax_ref, dbs_ref = refs[n_in + ex.n:n_in + ex.n + n_out]
        ex_out = refs[n_in + ex.n + n_out:n_in + 2 * ex.n + n_out]
        (vnb_s, vh_s, xc_s, mixed_s, pre_s, dmix_s, dvn_s, dho_s, dxc_s, dpre_s, dz_s, acc_s, accdm_s,
         cg_s, ca_s, dxchalo_s) = refs[n_in + 2 * ex.n + n_out:n_in + 2 * ex.n + n_out + n_scratch]
        ex_sems = refs[n_in + 2 * ex.n + n_out + n_scratch:]
        step = pl.program_id(0)
        c_id = n_chunk - 1 - step
        rid = _row_ids(D_BR)
        first_chunk = c_id == 0

        ex.start_spread(ex_in, ex_out, ex_sems, step)

        @pl.when(step == 0)
        def _():
            acc_s[...] = jnp.zeros_like(acc_s)
            accdm_s[...] = jnp.zeros_like(accdm_s)
            cg_s[...] = jnp.zeros_like(cg_s)
            ca_s[...] = jnp.zeros_like(ca_s)
            dxchalo_s[...] = jnp.zeros_like(dxchalo_s)
            dws_ref[...] = jnp.zeros_like(dws_ref)
            dwax_ref[...] = jnp.zeros_like(dwax_ref)

        lng, lnb = lng_ref[...], lnb_ref[...]
        h_halo = jnp.where(first_chunk, 0.0, hhalo_ref[...])

        def prev_rows(ref, cols, g, halo):
            before = ref[pl.ds(pl.multiple_of(jnp.maximum(g - 1, 0) * ROWS, ROWS), ROWS), cols]
            return jnp.where(g > 0, before, halo)

        vh_s[...] = vhb_ref[...].astype(F32)
        xc_s[...] = xcb_ref[...].astype(F32)

        for hd in range(N_HEAD):
            cs = slice(hd * HEAD, (hd + 1) * HEAD)
            vnb_s[:, cs] = (vh_s[:, cs] * lng[:, cs] + lnb[:, cs]).astype(BF16)
            mixed_s[:, cs] = _dot(wm_ref[hd], vnb_s[:, cs])
            pre = _dot(xcb_ref[:, cs], wax_ref[hd])
            pre_s[:, cs] = pre[:, :HEAD]
            pre_s[:, D_BR + hd * HEAD:D_BR + (hd + 1) * HEAD] = pre[:, HEAD:]

        goa, gob = goa_ref[...], gob_ref[...]

        def phase3(g, _):
            rows = _rows(g)
            ug, dug = _gelu(z_ref[rows, 0:D_BR], with_grad=True)
            ga = z_ref[rows, 2 * D_BR:3 * D_BR]
            sga = _sig(ga)
            sa = ga * sga
            mixed = mixed_s[rows, :] + bias_ref[rows, :]
            ya0 = ug * mixed
            ya = ya0 * sa
            ra = lax.rsqrt(_mean_last(ya * ya) + EPS)
            dyan = dy_ref[rows, 0:D_BR].astype(F32)
            acc_s[V_GOUT_A] += dyan * ya * ra
            dyg = dyan * goa
            dya = ra * dyg - ya * (ra * ra * ra) * _mean_last(dyg * ya)
            dya0 = dya * sa
            dz_s[rows, 2 * D_BR:3 * D_BR] = dya * ya0 * _silu_grad(sga, sa)
            dmix = dya0 * ug
            dmix_s[rows, :] = dmix
            accdm_s[rows, :] += dmix
            dz_s[rows, 0:D_BR] = dya0 * mixed * dug

            hh = h_ref[rows, :]
            gb = z_ref[rows, 4 * D_BR:5 * D_BR]
            sgb = _sig(gb)
            sb = gb * sgb
            yb = hh * sb
            rb = lax.rsqrt(_mean_last(yb * yb) + EPS)
            dybn = dy_ref[rows, D_BR:2 * D_BR].astype(F32)
            acc_s[V_GOUT_B] += dybn * yb * rb
            dyg = dybn * gob
            dyb = rb * dyg - yb * (rb * rb * rb) * _mean_last(dyg * yb)
            dho_s[rows, :] = dyb * sb
            dz_s[rows, 4 * D_BR:5 * D_BR] = dyb * hh * _silu_grad(sgb, sb)
            return 0

        _loop(N_GROUP, phase3, 0)

        for hd in range(N_HEAD):
            cs = slice(hd * HEAD, (hd + 1) * HEAD)
            dmb = dmix_s[:, cs].astype(BF16)
            dvn_s[:, cs] = _dot(wmt_ref[hd], dmb)
            dws_ref[hd] += _dot_nt(dmb, vnb_s[:, cs])

        def phase5(g, _):
            rows = _rows(g)
            dvn = dvn_s[rows, :]
            vh = vh_s[rows, :]
            acc_s[V_LN_G] += dvn * vh
            acc_s[V_LN_B] += dvn
            dvh = dvn * lng
            rs = rs_ref[rows, 0:1]
            dvg = rs * (dvh - _mean_last(dvh) - vh * _mean_last(dvh * vh))
            dz_s[rows, D_BR:2 * D_BR] = dvg * _gelu(z_ref[rows, D_BR:2 * D_BR], with_grad=True)[1]
            return 0

        _loop(N_GROUP, phase5, 0)

        ba, bx = ba_ref[...], bx_ref[...]
        sp8 = LRU_C * _softplus(-lam_ref[...])
        row0 = _row0_mask(rid)

        def phase6(k, carry):
            cg, ca = carry
            g = N_GROUP - 1 - k
            rows = _rows(g)
            bias0 = _row0_bias(jnp.logical_and(first_chunk, g == 0), row0)
            r, i, a, m2 = _lru_gates(pre_s[rows, 0:D_BR], pre_s[rows, D_BR:2 * D_BR], ba, bx, sp8, bias0)
            a_nx = jnp.where(rid < ROWS - 1, pltpu.roll(a, ROWS - 1, 0), ca)
            aa, bb = a_nx, dho_s[rows, :]
            for d in (1, 2, 4):
                a_sh = jnp.where(rid < ROWS - d, pltpu.roll(aa, ROWS - d, 0), 1.0)
                b_sh = jnp.where(rid < ROWS - d, pltpu.roll(bb, ROWS - d, 0), 0.0)
                bb = aa * b_sh + bb
                aa = aa * a_sh
            gg = bb + aa * cg
            hh = h_ref[rows, :]
            hprev = _shift_down(hh, prev_rows(h_ref, slice(None), g, h_halo), 1, rid)
            xc = xc_s[rows, :]
            gx = gg * xc
            dla = gg * hprev * a - gx * i * (a * a) * lax.rsqrt(m2)
            acc_s[V_LAM] += -(dla * r)
            dpa = -(dla * sp8) * r * (1.0 - r)
            mi = jnp.sqrt(m2) * i
            dpx = gx * mi * (1.0 - i)
            acc_s[V_B_A] += dpa
            acc_s[V_B_X] += dpx
            dpre_s[rows, 0:D_BR] = dpa
            dpre_s[rows, D_BR:2 * D_BR] = dpx
            dxc_s[rows, :] = gg * mi
            return _bcast_row(gg, 0), _bcast_row(a, 0)

        cg, ca = _loop(N_GROUP, phase6, (cg_s[...], ca_s[...]))
        cg_s[...] = cg
        ca_s[...] = ca

        for hd in range(N_HEAD):
            cs = slice(hd * HEAD, (hd + 1) * HEAD)
            dpre = jnp.concatenate([dpre_s[:, cs], dpre_s[:, D_BR + hd * HEAD:D_BR + (hd + 1) * HEAD]], axis=1).astype(BF16)
            dxc_s[:, cs] += _dot(dpre, waxt_ref[hd])
            dwax_ref[hd] += _dot_tn(xcb_ref[:, cs], dpre)

        def phase8(k, nxt):
            g = N_GROUP - 1 - k
            rows = _rows(g)
            dxc = dxc_s[rows, :]
            acc_s[V_CONV_B] += dxc
            xb = z_ref[rows, 3 * D_BR:4 * D_BR]
            dxb = cw_ref[3:4, :] * dxc
            acc_s[V_CONV_W + 3] += dxc * xb
            for j in range(1, CONV_W):
                later = _shift_up(dxc, nxt, j, rid)
                dxb = dxb + cw_ref[3 - j:4 - j, :] * later
                acc_s[V_CONV_W + 3 - j] += later * xb
            dz_s[rows, 3 * D_BR:4 * D_BR] = dxb
            return dxc

        dxchalo_s[...] = _loop(N_GROUP, phase8, dxchalo_s[...])
        dz_ref[...] = dz_s[...].astype(BF16)

        @pl.when(step == n_chunk - 1)
        def _():
            for v in range(N_VEC):
                vecs_ref[v:v + 1, :] = jnp.sum(acc_s[v], axis=0, keepdims=True)
            lam = lam_ref[...]
            vecs_ref[V_LAM:V_LAM + 1, :] = vecs_ref[V_LAM:V_LAM + 1, :] * (-LRU_C * _sig(-lam))
            tril = (lax.broadcasted_iota(jnp.int32, (HEAD, HEAD), 0) >= lax.broadcasted_iota(jnp.int32, (HEAD, HEAD), 1))
            ones = jnp.ones((ROWS, HEAD), BF16)
            for hd in range(N_HEAD):
                cs = slice(hd * HEAD, (hd + 1) * HEAD)
                dws_ref[hd] = jnp.where(tril, dws_ref[hd], 0.0)
                blk = accdm_s[:, cs]
                hi = blk.astype(BF16)
                lo = (blk - hi.astype(F32)).astype(BF16)
                dbs_ref[hd:hd + 1, :] = (_dot_nt(ones, hi) + _dot_nt(ones, lo))[0:1, :]
            ex.wait(ex_in, ex_out, ex_sems)

    vec = pl.BlockSpec((1, D_BR), lambda i: (0, 0))
    rev = lambda i: (n_chunk - 1 - i, 0)
    halo = lambda col: (lambda i: (jnp.maximum((n_chunk - 1 - i) * halo_blocks - 1, 0), col))
    full3 = lambda a, b, c: pl.BlockSpec((a, b, c), lambda i: (0, 0, 0))
    big = lambda w: pltpu.VMEM((CHUNK, w), F32)
    res = pl.pallas_call(
        body, name="mix_bwd", grid=(n_chunk,),
        in_specs=[pl.BlockSpec((CHUNK, D_IN), rev), pl.BlockSpec((CHUNK, 2 * D_BR), rev), pl.BlockSpec((CHUNK, D_BR), rev),
                  pl.BlockSpec((ROWS, D_BR), halo(0)), pl.BlockSpec((CHUNK, D_BR), rev), pl.BlockSpec((CHUNK, D_BR), rev),
                  pl.BlockSpec((CHUNK, HEAD), rev), vec, vec,
                  full3(N_HEAD, HEAD, HEAD), full3(N_HEAD, HEAD, HEAD),
                  pl.BlockSpec((CHUNK, D_BR), lambda i: (0, 0)), pl.BlockSpec((ROWS, D_BR), lambda i: (0, 0)), vec,
                  full3(N_HEAD, HEAD, 2 * HEAD), full3(N_HEAD, 2 * HEAD, HEAD), vec, vec, vec, vec, vec]
        + [ANY_SPEC] * ex.n,
        out_specs=[pl.BlockSpec((CHUNK, D_IN), rev), pl.BlockSpec((N_VEC, D_BR), lambda i: (0, 0)),
                   full3(N_HEAD, HEAD, HEAD), full3(N_HEAD, HEAD, 2 * HEAD),
                   pl.BlockSpec((N_HEAD, HEAD), lambda i: (0, 0))] + [ANY_SPEC] * ex.n,
        out_shape=[SDS((t_len, D_IN), BF16), SDS((N_VEC, D_BR), F32), SDS((N_HEAD, HEAD, HEAD), F32),
                   SDS((N_HEAD, HEAD, 2 * HEAD), F32), SDS((N_HEAD, HEAD), F32)] + ex.out_shape,
        scratch_shapes=[pltpu.VMEM((CHUNK, D_BR), BF16), big(D_BR), big(D_BR), big(D_BR), big(2 * D_BR), big(D_BR),
                        big(D_BR), big(D_BR), big(D_BR), big(2 * D_BR), big(D_IN),
                        pltpu.VMEM((N_VEC, ROWS, D_BR), F32), big(D_BR),
                        pltpu.VMEM((ROWS, D_BR), F32), pltpu.VMEM((ROWS, D_BR), F32), pltpu.VMEM((ROWS, D_BR), F32)]
        + ex.scratch,
        compiler_params=_params(("arbitrary",), 48),
    )(z, dy, h, h, vhb, xcb, rs, ln_g, ln_b, wm, wm_t, bias, cw, cb, wax, wax_t, ba, bx, lam, goa, gob, *ex_arrs)
    return res[:n_out], res[n_out:]


def _in_bwd(dz, w_in_g, x, dh1, pre_g, tm=256):
    t_len = x.shape[0]
    n_tile = t_len // tm

    def body(dz_ref, w_hbm, x_ref, dh1_ref, g_ref, gx_ref, dg_ref, w_s, t_even, t_odd, dg_s, w_sems):
        i = pl.program_id(0)

        @pl.when(i == 0)
        def _():
            loads = [pltpu.make_async_copy(w_hbm.at[s], w_s.at[:, s * W_IN_SHARD:(s + 1) * W_IN_SHARD], w_sems.at[s])
                     for s in range(N_DEV)]
            for cp in loads:
                cp.start()
            dg_s[...] = jnp.zeros_like(dg_s)
            t_odd[...] = jnp.zeros_like(t_odd)
            for cp in loads:
                cp.wait()

        def step(t_new, t_old):
            g = g_ref[...]
            acc = dg_s[...]
            for q in range(tm // TILE_ROWS):
                rows = slice(q * TILE_ROWS, (q + 1) * TILE_ROWS)
                xv = x_ref[rows, :]
                r = lax.rsqrt(_mean_last(xv * xv) + EPS)
                xh = xv * r
                dhn = t_old[rows, :]
                dg = dhn * g
                gx_ref[rows, :] = dh1_ref[rows, :] + r * (dg - xh * _mean_last(dg * xh))
                acc = acc + _fold_rows(dhn * xh)
            dg_s[...] = acc
            t_new[...] = _dot_nt(dz_ref[...], w_s[...])

        @pl.when(i % 2 == 0)
        def _():
            step(t_even, t_odd)

        @pl.when(i % 2 == 1)
        def _():
            step(t_odd, t_even)

        @pl.when(i == n_tile)
        def _():
            dg_ref[...] = jnp.sum(dg_s[...], axis=0, keepdims=True)

    matmul_tile = lambda i: (jnp.minimum(i, n_tile - 1), 0)
    rows_tile = lambda i: (jnp.maximum(i - 1, 0), 0)
    res = pl.pallas_call(
        body, name="in_bwd", grid=(n_tile + 1,),
        in_specs=[pl.BlockSpec((tm, D_IN), matmul_tile), ANY_SPEC, pl.BlockSpec((tm, D_MODEL), rows_tile),
                  pl.BlockSpec((tm, D_MODEL), rows_tile), pl.BlockSpec((1, D_MODEL), lambda i: (0, 0))],
        out_specs=[pl.BlockSpec((tm, D_MODEL), rows_tile), pl.BlockSpec((1, D_MODEL), lambda i: (0, 0))],
        out_shape=[SDS((t_len, D_MODEL), F32), SDS((1, D_MODEL), F32)],
        scratch_shapes=[pltpu.VMEM((D_MODEL, D_IN), BF16), pltpu.VMEM((tm, D_MODEL), F32), pltpu.VMEM((tm, D_MODEL), F32),
                        pltpu.VMEM((ROWS, D_MODEL), F32), pltpu.SemaphoreType.DMA((N_DEV,))],
        compiler_params=_params(("arbitrary",), 54),
    )(dz, w_in_g, x, dh1, pre_g)
    return res[0], res[1]


def _grad_w(a, b, bn, shard_major, name, tk=1024, ex_arrs=(), ex_scatter=()):
    t_len, m = a.shape
    n = b.shape[1]
    n_j, n_k = n // bn, t_len // tk
    ex = _Exchange(ex_arrs, ex_scatter)

    def body(a_ref, b_ref, *refs):
        ex_in, o_ref, ex_out = refs[:ex.n], refs[ex.n], refs[ex.n + 1:2 * ex.n + 1]
        acc_s, ex_sems = refs[2 * ex.n + 1], refs[2 * ex.n + 2:]
        j, k = pl.program_id(0), pl.program_id(1)
        if ex.n:
            @pl.when(jnp.logical_and(j == 0, k == 0))
            def _():
                ex.start(ex_in, ex_out, ex_sems)

        @pl.when(k == 0)
        def _():
            acc_s[...] = jnp.zeros_like(acc_s)

        acc_s[...] += _dot_tn(a_ref[...], b_ref[...])

        @pl.when(k == n_k - 1)
        def _():
            o_ref[...] = acc_s[...].astype(BF16)

        if ex.n:
            @pl.when(jnp.logical_and(j == n_j - 1, k == n_k - 1))
            def _():
                ex.wait(ex_in, ex_out, ex_sems)

    if shard_major:
        out_spec, out_shape = pl.BlockSpec((None, m, bn), lambda j, k: (j, 0, 0)), SDS((n_j, m, bn), BF16)
    else:
        out_spec, out_shape = pl.BlockSpec((m, bn), lambda j, k: (0, j)), SDS((m, n), BF16)
    res = pl.pallas_call(
        body, name=name, grid=(n_j, n_k),
        in_specs=[pl.BlockSpec((tk, m), lambda j, k: (k, 0)), pl.BlockSpec((tk, bn), lambda j, k: (k, j))]
        + [ANY_SPEC] * ex.n,
        out_specs=[out_spec] + [ANY_SPEC] * ex.n, out_shape=[out_shape] + ex.out_shape,
        scratch_shapes=[pltpu.VMEM((m, bn), F32)] + (ex.scratch if ex.n else []),
        compiler_params=_params(("arbitrary", "arbitrary"), 40),
    )(a, b, *ex_arrs)
    return res[0], res[1:]


RS_CHIPS = (6, 2, 4, 0)
RS_SLOTS = (0, 1, 2, 4, 6)


def _grad_w_in_pairs(hn, dz, ex_arrs, ex_scatter, tk=1024):
    t_len = hn.shape[0]
    n_k = t_len // tk
    n_ph = len(RS_CHIPS)
    ex = _Exchange(ex_arrs, ex_scatter)
    me_out = 4 * lax.axis_index("x") + 2 * lax.axis_index("y") + lax.axis_index("c")
    order = jnp.stack([(me_out ^ chip) // 2 for chip in RS_CHIPS]).astype(jnp.int32)
    slots = jnp.stack([me_out ^ k for k in RS_SLOTS]).astype(jnp.int32)
    shard = W_IN_SHARD

    def body(order_ref, a_ref, b_ref, *refs):
        ex_in, parts_hbm, ex_out = refs[:ex.n], refs[ex.n], refs[ex.n + 1:2 * ex.n + 1]
        (acc_s, tb_s, stage_s, rx_s, d2d_send, d2d_recv, ici_send, ici_recv, sib_sems,
         loc_sem) = refs[2 * ex.n + 1:2 * ex.n + 11]
        ex_sems = refs[2 * ex.n + 11:]
        j, k = pl.program_id(0), pl.program_id(1)
        x, y, c, me = _mesh_place()
        sib = _peer(x, y, c, SIBLING)[0]

        def to_sibling(p):
            return _remote(stage_s.at[0], rx_s.at[p % 2], d2d_send.at[p], d2d_recv.at[p], sib)

        def over_ici(p):
            dev = _peer(x, y, c, RS_CHIPS[p])[0]
            return _remote(stage_s.at[1], parts_hbm.at[me], ici_send.at[p], ici_recv.at[p], dev)

        def own_chip():
            return (_remote(stage_s.at[0], parts_hbm.at[me], sib_sems.at[0], sib_sems.at[1], sib),
                    pltpu.make_async_copy(stage_s.at[1], parts_hbm.at[me], loc_sem.at[0]))

        @pl.when(jnp.logical_and(j == 0, k == 0))
        def _():
            ex.start(ex_in, ex_out, ex_sems)

        for p in range(n_ph - 1):
            for core in (0, 1):
                @pl.when(jnp.logical_and(jnp.logical_and(j == p + 1, k == 0), c == core))
                def _(p=p, core=core):
                    to_sibling(p).wait_recv()
                    if p >= 1:
                        over_ici(p - 1).wait_send()
                    mine = acc_s[:, core * shard:(core + 1) * shard]
                    stage_s[1] = (mine + rx_s[p % 2].astype(F32)).astype(BF16)
                    over_ici(p).start()

        @pl.when(k == 0)
        def _():
            acc_s[...] = jnp.zeros_like(acc_s)

        a = a_ref[...]
        acc_s[:, 0:W_BODY] += _dot_tn(a, b_ref[:, 0:W_BODY])
        acc_s[:, shard:shard + W_BODY] += _dot_tn(a, b_ref[:, shard:shard + W_BODY])
        tb_s[:, 0:W_TAIL] = b_ref[:, W_BODY:shard]
        tb_s[:, W_TAIL:2 * W_TAIL] = b_ref[:, shard + W_BODY:2 * shard]
        tails = _dot_tn(a, tb_s[...])
        acc_s[:, W_BODY:shard] += tails[:, 0:W_TAIL]
        acc_s[:, shard + W_BODY:2 * shard] += tails[:, W_TAIL:2 * W_TAIL]

        for p in range(n_ph):
            for core in (0, 1):
                @pl.when(jnp.logical_and(jnp.logical_and(j == p, k == n_k - 1), c == core))
                def _(p=p, core=core):
                    same = acc_s[:, core * shard:(core + 1) * shard]
                    other = acc_s[:, (1 - core) * shard:(2 - core) * shard]
                    if p >= 1:
                        to_sibling(p - 1).wait_send()
                    stage_s[0] = other.astype(BF16)
                    if p < n_ph - 1:
                        to_sibling(p).start()
                    else:
                        over_ici(n_ph - 2).wait_send()
                        stage_s[1] = same.astype(BF16)
                        for cp in own_chip():
                            cp.start()

        @pl.when(jnp.logical_and(j == n_ph - 1, k == n_k - 1))
        def _():
            to_sib, local = own_chip()
            to_sib.wait_send()
            local.wait()
            _remote(stage_s.at[0], parts_hbm.at[_peer(x, y, c, SIBLING)[1]], sib_sems.at[0], sib_sems.at[1], sib).wait_recv()
            for p in range(n_ph - 1):
                dev, lin = _peer(x, y, c, RS_CHIPS[p])
                _remote(stage_s.at[0], parts_hbm.at[lin], ici_send.at[p], ici_recv.at[p], dev).wait_recv()
            ex.wait(ex_in, ex_out, ex_sems)

    dma = lambda n: pltpu.SemaphoreType.DMA((n,))
    grid_spec = pltpu.PrefetchScalarGridSpec(
        num_scalar_prefetch=1, grid=(n_ph, n_k),
        in_specs=[pl.BlockSpec((tk, D_MODEL), lambda j, k, order: (k, 0)),
                  pl.BlockSpec((tk, 2 * shard), lambda j, k, order: (k, order[j]))] + [ANY_SPEC] * ex.n,
        out_specs=[ANY_SPEC] * (1 + ex.n),
        scratch_shapes=[pltpu.VMEM((D_MODEL, 2 * shard), F32), pltpu.VMEM((tk, 2 * W_TAIL), BF16),
                        pltpu.VMEM((2, D_MODEL, shard), BF16),
                        pltpu.VMEM((2, D_MODEL, shard), BF16), dma(n_ph - 1), dma(n_ph - 1), dma(n_ph - 1),
                        dma(n_ph - 1), dma(2), dma(1)] + ex.scratch)
    res = pl.pallas_call(
        body, name="grad_w_in", grid_spec=grid_spec,
        out_shape=[SDS((N_DEV, D_MODEL, shard), BF16)] + ex.out_shape,
        compiler_params=_params(("arbitrary", "arbitrary"), 54),
    )(order, hn, dz, *ex_arrs)
    return res[0], slots, res[1:]


def _sum_parts(parts, name):
    def body(p_ref, o_ref):
        g = p_ref[0].astype(F32)
        for s in range(1, parts.shape[0]):
            g = g + p_ref[s].astype(F32)
        o_ref[...] = g

    return pl.pallas_call(body, name=name, out_shape=SDS(parts.shape[1:], F32))(parts)


def _adamw_math(g, w_ref, m_ref, v_ref, g_ref, d_ref, nm_ref, nv_ref):
    c1 = 1.0 - ADAM_B1 ** ADAM_STEP
    c2 = 1.0 - ADAM_B2 ** ADAM_STEP
    g_ref[...] = g
    nm = ADAM_B1 * m_ref[...] + (1.0 - ADAM_B1) * g
    nv = ADAM_B2 * v_ref[...] + (1.0 - ADAM_B2) * (g * g)
    nm_ref[...] = nm
    nv_ref[...] = nv
    d_ref[...] = -ADAM_LR * ((nm / c1) / (jnp.sqrt(nv / c2) + ADAM_EPS) + ADAM_WD * w_ref[...])


def _adamw(parts, w, m, v, name, tr):
    rows, cols = w.shape
    n_parts = parts.shape[0]

    def body(p_ref, *refs):
        g = p_ref[0].astype(F32)
        for s in range(1, n_parts):
            g = g + p_ref[s].astype(F32)
        _adamw_math(g, *refs)

    tile = pl.BlockSpec((tr, cols), lambda i: (i, 0))
    return pl.pallas_call(
        body, name=name, grid=(rows // tr,),
        in_specs=[pl.BlockSpec((n_parts, tr, cols), lambda i: (0, i, 0)), tile, tile, tile],
        out_specs=[tile] * 4, out_shape=[SDS((rows, cols), F32)] * 4,
        compiler_params=_params(("arbitrary",), 40),
    )(parts, w, m, v)


def _adamw_unpacked(grads, triples, name):
    n = len(triples)
    n_rows = [t[0].shape[0] for t in triples]

    def body(g_ref, *refs):
        ins, outs = refs[:3 * n], refs[3 * n:]
        row = 0
        for i in range(n):
            _adamw_math(g_ref[row:row + n_rows[i], :], *ins[3 * i:3 * i + 3], *outs[4 * i:4 * i + 4])
            row += n_rows[i]
        outs[4 * n][...] = g_ref[row:row + ROWS, :]

    out_shape = [SDS((r, LANES), F32) for r in n_rows for _ in range(4)] + [SDS((ROWS, LANES), F32)]
    return pl.pallas_call(
        body, name=name, out_shape=out_shape,
        compiler_params=pltpu.CompilerParams(vmem_limit_bytes=40 * MIB),
    )(grads, *[a for t in triples for a in t])


def _adamw_slots(parts, slots, w, m, v, name, tr):
    rows, cols = w.shape
    n_slots = slots.shape[0]

    def body(slots_ref, *refs):
        g = refs[0][...].astype(F32)
        for s in range(1, n_slots):
            g = g + refs[s][...].astype(F32)
        _adamw_math(g, *refs[n_slots:])

    tile = pl.BlockSpec((tr, cols), lambda i, slots: (i, 0))
    part = lambda s: pl.BlockSpec((None, tr, cols), lambda i, slots: (slots[s], i, 0))
    grid_spec = pltpu.PrefetchScalarGridSpec(
        num_scalar_prefetch=1, grid=(rows // tr,),
        in_specs=[part(s) for s in range(n_slots)] + [tile, tile, tile], out_specs=[tile] * 4)
    return pl.pallas_call(
        body, name=name, grid_spec=grid_spec, out_shape=[SDS((rows, cols), F32)] * 4,
        compiler_params=_params(("arbitrary",), 40),
    )(slots, *([parts] * n_slots), w, m, v)


PACKED = ("gmlp_ln_g", "gmlp_ln_b", "gmlp_ws", "gmlp_bs", "conv_b", "w_a", "b_a", "w_x", "b_x", "lam", "gmlp_out_g",
          "lru_out_g", "post_g")
WEIGHTS = ("pre_g", "w_in", "gmlp_ln_g", "gmlp_ln_b", "gmlp_ws", "gmlp_bs", "conv_w", "conv_b", "w_a", "b_a", "w_x",
           "b_x", "lam", "gmlp_out_g", "lru_out_g", "w_out", "post_g", "w_pe", "w_pg")
LANES = 128


PACK_ROWS = 3200


def _pack(parts):
    rows = [p.reshape(-1, LANES) for p in parts]
    used = sum(r.shape[0] for r in rows)
    return jnp.concatenate(rows + [jnp.zeros((PACK_ROWS - used, LANES), F32)], axis=0)


def _pad_rows(a, rows):
    return jnp.concatenate([a, jnp.zeros((rows - a.shape[0],) + a.shape[1:], a.dtype)], axis=0)


def kernel(x, p, pre_g, w_in, gmlp_ln_g, gmlp_ln_b, gmlp_ws, gmlp_bs, conv_w, conv_b, w_a, b_a, w_x, b_x, lam, gmlp_out_g, lru_out_g, w_out, post_g, w_pe, w_pg, loss_target, m_pre_g, m_w_in, m_gmlp_ln_g, m_gmlp_ln_b, m_gmlp_ws, m_gmlp_bs, m_conv_w, m_conv_b, m_w_a, m_b_a, m_w_x, m_b_x, m_lam, m_gmlp_out_g, m_lru_out_g, m_w_out, m_post_g, m_w_pe, m_w_pg, v_pre_g, v_w_in, v_gmlp_ln_g, v_gmlp_ln_b, v_gmlp_ws, v_gmlp_bs, v_conv_w, v_conv_b, v_w_a, v_b_a, v_w_x, v_b_x, v_lam, v_gmlp_out_g, v_lru_out_g, v_w_out, v_post_g, v_w_pe, v_w_pg):
    args = dict(locals())
    weights = {n: args[n] for n in WEIGHTS}
    m_in = {n: args["m_" + n] for n in WEIGHTS}
    v_in = {n: args["v_" + n] for n in WEIGHTS}
    sm = {n: weights[n][0] for n in PACKED}
    shard_rows = D_MODEL // N_DEV
    xs, ps, tgt = x[0], p[0, 0], loss_target[0]

    vec = lambda a: a.reshape(1, -1)
    tril = jnp.tril(jnp.ones((CHUNK, CHUNK), dtype=bool))
    wm32 = jnp.where(tril[None], sm["gmlp_ws"], 0.0)
    wm, wm_t = wm32.astype(BF16), jnp.swapaxes(wm32, 1, 2).astype(BF16)
    bias = jnp.repeat(sm["gmlp_bs"].T, HEAD, axis=1)
    wax32 = jnp.concatenate([sm["w_a"], sm["w_x"]], axis=2)
    wax, wax_t = wax32.astype(BF16), jnp.swapaxes(wax32, 1, 2).astype(BF16)
    ln_g, ln_b = vec(sm["gmlp_ln_g"]), vec(sm["gmlp_ln_b"])
    post_g_v = vec(sm["post_g"])

    hn = _pre_norm(xs, pre_g)
    cw_shard = _pad_rows(conv_w.reshape(CONV_W, HEAD), ROWS)
    z, w_in_g, (cw_g,) = _in_proj(hn, w_in[0].astype(BF16), [cw_shard])
    cw_full = jnp.transpose(cw_g[:, :CONV_W, :], (1, 0, 2)).reshape(CONV_W, D_BR)
    mixer_consts = dict(cw=_pad_rows(cw_full, ROWS), cb=vec(sm["conv_b"]), ba=vec(sm["b_a"]), bx=vec(sm["b_x"]),
                        lam=vec(sm["lam"]), goa=vec(sm["gmlp_out_g"]), gob=vec(sm["lru_out_g"]))
    (y, h, vhb, xcb, v_rs), (w_out_g, w_pe_g, w_pg_g) = _mix_fwd(
        z, ln_g, ln_b, wm, bias, wax=wax, **mixer_consts,
        ex_arrs=[w_out[0].astype(BF16), w_pe[0].astype(BF16), w_pg[0].astype(BF16)], ex_scatter=[False, False, False])
    w_out_f, w_pg_f = w_out_g.reshape(D_MODEL, D_MODEL), w_pg_g.reshape(D_MODEL, D_MODEL)
    h1, ob = _out_proj(y, xs, w_out_f, post_g_v)
    dh2, dgl, h1b, loss_part, d_w_pe = _ple_loss(h1, ps, tgt, w_pg_f, w_pe_g)

    dh1, do, dy, d_post_g = _tail_bwd(dh2, dgl, ob, w_pg_f, w_out_f, post_g_v)
    d_w_out, _ = _grad_w(y, do, 1024, False, "grad_w_out")
    d_w_pg, _ = _grad_w(h1b, dgl, 1024, False, "grad_w_pg")
    (dz, vecs, d_ws, d_wax, d_bs), (parts_out, parts_pg, parts_pe) = _mix_bwd(
        z, dy, h, vhb, xcb, v_rs, ln_g, ln_b, wm, wm_t, bias, wax=wax, wax_t=wax_t, **mixer_consts,
        ex_arrs=[d_w_out.reshape(N_DEV, shard_rows, D_MODEL), d_w_pg.reshape(N_DEV, shard_rows, D_MODEL), d_w_pe],
        ex_scatter=[True, True, True])

    small = {"gmlp_ln_g": vecs[V_LN_G], "gmlp_ln_b": vecs[V_LN_B], "gmlp_ws": d_ws, "gmlp_bs": d_bs,
             "conv_b": vecs[V_CONV_B], "w_a": d_wax[:, :, :HEAD], "b_a": vecs[V_B_A], "w_x": d_wax[:, :, HEAD:],
             "b_x": vecs[V_B_X], "lam": vecs[V_LAM], "gmlp_out_g": vecs[V_GOUT_A], "lru_out_g": vecs[V_GOUT_B],
             "post_g": d_post_g}
    small_part = _pack([small[n] for n in PACKED] + [loss_part]).reshape(N_DEV, PACK_ROWS // N_DEV, LANES)
    d_cw_blocks = jnp.transpose(vecs[V_CONV_W:V_CONV_W + CONV_W].reshape(CONV_W, N_DEV, HEAD), (1, 0, 2))
    d_cw_blocks = jnp.concatenate([d_cw_blocks, jnp.zeros((N_DEV, ROWS - CONV_W, HEAD), F32)], axis=1)
    parts_in, slots_in, (small_blocks, parts_cw) = _grad_w_in_pairs(
        hn, dz, ex_arrs=[small_part, d_cw_blocks], ex_scatter=[True, True])
    small_sum = _sum_parts(small_blocks, "sum_small")
    grad_x, d_pre_g = _in_bwd(dz, w_in_g, xs, dh1, pre_g)
    pre_rows = D_MODEL // LANES
    small_all, parts_pre = _exchange([small_sum, d_pre_g.reshape(pre_rows, LANES)], False, "gather_small_grads")

    pad_cw = lambda a: _pad_rows(a.reshape(CONV_W, HEAD), ROWS)
    flat = lambda a: a.reshape(pre_rows, LANES)
    outs = {
        "w_in": _adamw_slots(parts_in, slots_in, w_in[0], m_w_in[0], v_w_in[0], "adamw_w_in", 256),
        "w_out": _adamw(parts_out, w_out[0], m_w_out[0], v_w_out[0], "adamw_w_out", 128),
        "w_pe": _adamw(parts_pe, w_pe[0], m_w_pe[0], v_w_pe[0], "adamw_w_pe", 256),
        "w_pg": _adamw(parts_pg, w_pg[0], m_w_pg[0], v_w_pg[0], "adamw_w_pg", 128),
        "conv_w": [a[:CONV_W] for a in
                   _adamw(parts_cw, pad_cw(conv_w), pad_cw(m_conv_w), pad_cw(v_conv_w), "adamw_conv_w", ROWS)],
        "pre_g": _adamw(parts_pre, flat(pre_g), flat(m_pre_g), flat(v_pre_g), "adamw_pre_g", pre_rows),
    }
    as_rows = lambda a: a.reshape(-1, LANES)
    small_res = _adamw_unpacked(small_all.reshape(PACK_ROWS, LANES),
                                [(as_rows(weights[n]), as_rows(m_in[n]), as_rows(v_in[n])) for n in PACKED], "adamw_small")
    for i, n in enumerate(PACKED):
        outs[n] = small_res[4 * i:4 * i + 4]
    loss = small_res[-1][0, 0]

    result = [loss, grad_x[None]]
    for q in range(4):
        result += [outs[n][q].reshape(weights[n].shape) for n in WEIGHTS]
    return tuple(result)
```

```python
import jax
import jax.numpy as jnp
from jax import lax
from jax.experimental import pallas as pl
from jax.experimental.pallas import tpu as pltpu

F32 = jnp.float32
BF16 = jnp.bfloat16
SDS = jax.ShapeDtypeStruct

D_MODEL = 2048
D_BR = 1024
D_IN = 5 * D_BR
D_PLE = 256
N_HEAD = 8
HEAD = 128
CHUNK = 128
ROWS = 8
N_GROUP = CHUNK // ROWS
MIX_SUB = 2
N_DEV = 8
W_IN_SHARD = D_IN // N_DEV
EPS = 1e-6
LRU_C = 8.0
CONV_W = 4
MIB = 1 << 20

ADAM_LR, ADAM_B1, ADAM_B2, ADAM_EPS, ADAM_WD, ADAM_STEP = 0.001, 0.9, 0.999, 1e-08, 0.01, 10

_GELU_C = 0.7978845608028654
_GELU_A = 0.044715

V_LN_G, V_LN_B, V_CONV_B, V_B_A, V_B_X, V_LAM, V_GOUT_A, V_GOUT_B, V_CONV_W = 0, 1, 2, 3, 4, 5, 6, 7, 8
N_VEC = 16


def _params(sem, vmem_mib):
    return pltpu.CompilerParams(dimension_semantics=sem, vmem_limit_bytes=int(vmem_mib * MIB))


def _sig(x):
    return 0.5 * jnp.tanh(0.5 * x) + 0.5


def _gelu(x, with_grad=False):
    sq = x * x
    t = jnp.tanh(x * (_GELU_C + (_GELU_C * _GELU_A) * sq))
    half, one_t = 0.5 * x, 1.0 + t
    if not with_grad:
        return half * one_t
    grad = 0.5 * one_t + half * ((1.0 - t) * one_t) * (_GELU_C + (3.0 * _GELU_C * _GELU_A) * sq)
    return half * one_t, grad


def _silu_grad(s, xs):
    return s + xs * (1.0 - s)


def _neg_expm1(y, exp_y):
    series = -y * (1.0 + y * (0.5 + y * (1.0 / 6.0)))
    return jnp.where(y > -0.01, series, 1.0 - exp_y)


def _softplus(x):
    return jnp.maximum(x, 0.0) + jnp.log(1.0 + jnp.exp(-jnp.abs(x)))


def _row_ids(width):
    return lax.broadcasted_iota(jnp.int32, (ROWS, width), 0)


def _shift_down(cur, prev, k, rid):
    return jnp.where(rid >= k, pltpu.roll(cur, k, 0), pltpu.roll(prev, k, 0))


def _shift_up(cur, nxt, k, rid):
    return jnp.where(rid < ROWS - k, pltpu.roll(cur, ROWS - k, 0), pltpu.roll(nxt, ROWS - k, 0))


def _mean_last(x):
    return jnp.mean(x, axis=-1, keepdims=True)


def _rows(g):
    return pl.ds(pl.multiple_of(g * ROWS, ROWS), ROWS)


TILE_ROWS = 16


def _tile_rows(q):
    return pl.ds(pl.multiple_of(q * TILE_ROWS, TILE_ROWS), TILE_ROWS)


UNROLL = 4
TILE_UNROLL = 8


def _loop(n, body, init, unroll=UNROLL):
    def wide(i, carry):
        for u in range(unroll):
            carry = body(i * unroll + u, carry)
        return carry

    return lax.fori_loop(0, n // unroll, wide, init)


def _fold_rows(x):
    return x[0:ROWS, :] + x[ROWS:TILE_ROWS, :]


def _bcast_row(x, r):
    return jnp.broadcast_to(x[r:r + 1, :], x.shape)


def _dot(a, b):
    return jnp.dot(a, b, preferred_element_type=F32)


def _dot_nt(a, b):
    return lax.dot_general(a, b, (((1,), (1,)), ((), ())), preferred_element_type=F32)


def _dot_tn(a, b):
    return lax.dot_general(a, b, (((0,), (0,)), ((), ())), preferred_element_type=F32)


def _mesh_place():
    x, y, c = lax.axis_index("x"), lax.axis_index("y"), lax.axis_index("c")
    return x, y, c, 4 * x + 2 * y + c


def _peer(x, y, c, k):
    px = 1 - x if k & 4 else x
    py = 1 - y if k & 2 else y
    pc = 1 - c if k & 1 else c
    return (px, py, pc), 4 * px + 2 * py + pc


def _remote(src, dst, send_sem, recv_sem, dev):
    return pltpu.make_async_remote_copy(src_ref=src, dst_ref=dst, send_sem=send_sem, recv_sem=recv_sem, device_id=dev,
                                        device_id_type=pl.DeviceIdType.MESH)


ANY_SPEC = pl.BlockSpec(memory_space=pl.ANY)


class _Exchange:
    def __init__(self, arrs, scatter):
        self.n = len(arrs)
        self.scatter = tuple(scatter)
        self.out_shape = [SDS(a.shape if s else (N_DEV,) + a.shape, a.dtype) for a, s in zip(arrs, scatter)]
        self.scratch = [pltpu.SemaphoreType.DMA((self.n * N_DEV,)), pltpu.SemaphoreType.DMA((self.n * N_DEV,)),
                        pltpu.SemaphoreType.DMA((self.n,))]

    def _copies(self, ins, outs, sems):
        send_sems, recv_sems, local_sems = sems
        x, y, c, me = _mesh_place()
        local, sends, recvs = [], [], []
        for a in range(self.n):
            src = ins[a].at[me] if self.scatter[a] else ins[a]
            local.append(pltpu.make_async_copy(src, outs[a].at[me], local_sems.at[a]))
        for k in range(1, N_DEV):
            dev, lin = _peer(x, y, c, k)
            for a in range(self.n):
                src = ins[a].at[lin] if self.scatter[a] else ins[a]
                pair = (send_sems.at[a * N_DEV + k], recv_sems.at[a * N_DEV + k], dev)
                sends.append(_remote(src, outs[a].at[me], *pair))
                recvs.append(_remote(src, outs[a].at[lin], *pair))
        return local, sends, recvs

    def start(self, ins, outs, sems):
        local, sends, _ = self._copies(ins, outs, sems)
        for cp in local + sends:
            cp.start()

    def wait(self, ins, outs, sems):
        local, sends, recvs = self._copies(ins, outs, sems)
        for cp in recvs:
            cp.wait_recv()
        for cp in sends:
            cp.wait_send()
        for cp in local:
            cp.wait()


def _exchange(arrs, scatter, name):
    ex = _Exchange(arrs, [scatter] * len(arrs))
    n = ex.n

    def body(*refs):
        ins, outs, sems = refs[:n], refs[n:2 * n], refs[2 * n:]
        ex.start(ins, outs, sems)
        ex.wait(ins, outs, sems)

    return pl.pallas_call(
        body, name=name, out_shape=ex.out_shape, in_specs=[ANY_SPEC] * n, out_specs=[ANY_SPEC] * n,
        scratch_shapes=ex.scratch,
    )(*arrs)


def _pre_norm(x, pre_g, tm=512):
    t_len = x.shape[0]

    def body(x_ref, g_ref, hn_ref):
        g = g_ref[...]

        def rows_body(q, _):
            rows = _tile_rows(q)
            xv = x_ref[rows, :]
            hn_ref[rows, :] = (xv * lax.rsqrt(_mean_last(xv * xv) + EPS) * g).astype(BF16)
            return 0

        _loop(tm // TILE_ROWS, rows_body, 0, unroll=TILE_UNROLL)

    tile = pl.BlockSpec((tm, D_MODEL), lambda i: (i, 0))
    return pl.pallas_call(
        body, name="pre_norm", grid=(t_len // tm,),
        in_specs=[tile, pl.BlockSpec((1, D_MODEL), lambda i: (0, 0))], out_specs=tile,
        out_shape=SDS((t_len, D_MODEL), BF16),
        compiler_params=_params(("arbitrary",), 24),
    )(x, pre_g)


CHIP_ORDER = (0, 2, 4, 6)
W_BODY, W_TAIL = 512, 128
SIBLING = 1
ICI_MASKS = (2, 4, 6)
DIRECT_MASKS = (SIBLING,) + ICI_MASKS
Y_NEIGHBOUR, X_NEIGHBOUR, DIAGONAL = 2, 4, 6
W_DIRECT = (SIBLING, Y_NEIGHBOUR, X_NEIGHBOUR)


def _in_proj(hn, w_shard, others, tm=1024):
    t_len = hn.shape[0]
    n_i = t_len // tm
    n_o = len(others)
    me_out = 4 * lax.axis_index("x") + 2 * lax.axis_index("y") + lax.axis_index("c")
    order = jnp.stack([(me_out ^ chip) // 2 for chip in CHIP_ORDER]).astype(jnp.int32)

    def body(order_ref, hn_ref, w_hbm, *refs):
        o_in = refs[:n_o]
        z_ref, wg_hbm = refs[n_o], refs[n_o + 1]
        o_out = refs[n_o + 2:2 * n_o + 2]
        (wbuf, tail_s, send_w, recv_w, fsend_w, frecv_w, send_o, recv_o, fsend_o, frecv_o, wb_sems, loc_sems, rsend,
         rrecv) = refs[2 * n_o + 2:]
        j, i = pl.program_id(0), pl.program_id(1)
        x, y, c, me = _mesh_place()
        sib = _peer(x, y, c, SIBLING)[0]

        def relay(core):
            src, dst = (Y_NEIGHBOUR, X_NEIGHBOUR) if core == 0 else (X_NEIGHBOUR, Y_NEIGHBOUR)
            held, diag = _peer(x, y, c, src)[1], _peer(x, y, c, DIAGONAL)[1]
            pair = (rsend.at[0], rrecv.at[0], _peer(x, y, c, dst)[0])
            return _remote(wbuf.at[held], wbuf.at[held], *pair), _remote(wbuf.at[diag], wbuf.at[diag], *pair)

        def direct(k, a=None):
            dev, lin = _peer(x, y, c, k)
            if a is None:
                return (_remote(w_hbm, wbuf.at[me], send_w.at[k], recv_w.at[k], dev),
                        _remote(w_hbm, wbuf.at[lin], send_w.at[k], recv_w.at[k], dev))
            pair = (send_o.at[a * N_DEV + k], recv_o.at[a * N_DEV + k], dev)
            return _remote(o_in[a], o_out[a].at[me], *pair), _remote(o_in[a], o_out[a].at[lin], *pair)

        def passed(k, a=None):
            mine, theirs = _peer(x, y, c, k)[1], _peer(x, y, c, k ^ SIBLING)[1]
            if a is None:
                pair = (fsend_w.at[k], frecv_w.at[k], sib)
                return _remote(wbuf.at[mine], wbuf.at[mine], *pair), _remote(wbuf.at[theirs], wbuf.at[theirs], *pair)
            pair = (fsend_o.at[a * N_DEV + k], frecv_o.at[a * N_DEV + k], sib)
            return (_remote(o_out[a].at[mine], o_out[a].at[mine], *pair),
                    _remote(o_out[a].at[theirs], o_out[a].at[theirs], *pair))

        def own_copies():
            return [pltpu.make_async_copy(o_in[a], o_out[a].at[me], loc_sems.at[1 + a]) for a in range(n_o)]

        @pl.when(jnp.logical_and(j == 0, i == 0))
        def _():
            own = pltpu.make_async_copy(w_hbm, wbuf.at[me], loc_sems.at[0])
            own.start()
            for cp in own_copies():
                cp.start()
            for k in W_DIRECT:
                direct(k)[0].start()
            for k in DIRECT_MASKS:
                for a in range(n_o):
                    direct(k, a)[0].start()
            own.wait()

        low = 2 * order_ref[j]

        for jp, chip in enumerate(CHIP_ORDER):
            @pl.when(jnp.logical_and(j == jp, i == 0))
            def _(jp=jp, chip=chip):
                if chip == 0:
                    direct(SIBLING)[1].wait_recv()
                elif chip == Y_NEIGHBOUR:
                    for mask in (Y_NEIGHBOUR, X_NEIGHBOUR):
                        direct(mask)[1].wait_recv()
                        passed(mask)[0].start()
                    for core in (0, 1):
                        @pl.when(c == core)
                        def _(core=core):
                            relay(core)[0].start()
                    passed(Y_NEIGHBOUR)[1].wait_recv()
                elif chip == X_NEIGHBOUR:
                    passed(X_NEIGHBOUR)[1].wait_recv()
                    for core in (0, 1):
                        @pl.when(c == core)
                        def _(core=core):
                            relay(core)[1].wait_recv()
                    passed(DIAGONAL)[0].start()
                    for k in ICI_MASKS:
                        for a in range(n_o):
                            direct(k, a)[1].wait_recv()
                            passed(k, a)[0].start()
                else:
                    passed(DIAGONAL)[1].wait_recv()
                for half in (0, 1):
                    pltpu.make_async_copy(wbuf.at[low + half], wg_hbm.at[low + half], wb_sems.at[2 * jp + half]).start()
                tail_s[:, 0:W_TAIL] = wbuf[low, :, W_BODY:W_IN_SHARD]
                tail_s[:, W_TAIL:2 * W_TAIL] = wbuf[low + 1, :, W_BODY:W_IN_SHARD]

        hn = hn_ref[...]
        z_ref[:, 0:W_BODY] = _dot(hn, wbuf[low, :, 0:W_BODY])
        z_ref[:, W_IN_SHARD:W_IN_SHARD + W_BODY] = _dot(hn, wbuf[low + 1, :, 0:W_BODY])
        tails = _dot(hn, tail_s[...])
        z_ref[:, W_BODY:W_IN_SHARD] = tails[:, 0:W_TAIL]
        z_ref[:, W_IN_SHARD + W_BODY:2 * W_IN_SHARD] = tails[:, W_TAIL:2 * W_TAIL]

        @pl.when(jnp.logical_and(j == len(CHIP_ORDER) - 1, i == n_i - 1))
        def _():
            for a in range(n_o):
                direct(SIBLING, a)[1].wait_recv()
            for k in ICI_MASKS:
                for a in range(n_o):
                    passed(k, a)[1].wait_recv()
            for k in W_DIRECT:
                direct(k)[0].wait_send()
            for core in (0, 1):
                @pl.when(c == core)
                def _(core=core):
                    relay(core)[0].wait_send()
            for k in DIRECT_MASKS:
                for a in range(n_o):
                    direct(k, a)[0].wait_send()
            for k in ICI_MASKS:
                passed(k)[0].wait_send()
                for a in range(n_o):
                    passed(k, a)[0].wait_send()
            for cp in own_copies():
                cp.wait()
            for jj in range(N_DEV):
                pltpu.make_async_copy(wbuf.at[0], wg_hbm.at[0], wb_sems.at[jj]).wait()

    dma = lambda n: pltpu.SemaphoreType.DMA((n,))
    grid_spec = pltpu.PrefetchScalarGridSpec(
        num_scalar_prefetch=1, grid=(len(CHIP_ORDER), n_i),
        in_specs=[pl.BlockSpec((tm, D_MODEL), lambda j, i, order: (i, 0)), ANY_SPEC] + [ANY_SPEC] * n_o,
        out_specs=[pl.BlockSpec((tm, 2 * W_IN_SHARD), lambda j, i, order: (i, order[j])), ANY_SPEC] + [ANY_SPEC] * n_o,
        scratch_shapes=[pltpu.VMEM((N_DEV, D_MODEL, W_IN_SHARD), BF16), pltpu.VMEM((D_MODEL, 2 * W_TAIL), BF16),
                        dma(N_DEV), dma(N_DEV), dma(N_DEV), dma(N_DEV),
                        dma(n_o * N_DEV), dma(n_o * N_DEV), dma(n_o * N_DEV), dma(n_o * N_DEV), dma(N_DEV), dma(1 + n_o),
                        dma(1), dma(1)])
    res = pl.pallas_call(
        body, name="in_proj", grid_spec=grid_spec,
        out_shape=[SDS((t_len, D_IN), F32), SDS((N_DEV, D_MODEL, W_IN_SHARD), BF16)]
        + [SDS((N_DEV,) + o.shape, o.dtype) for o in others],
        compiler_params=_params(("arbitrary", "arbitrary"), 54),
    )(order, hn, w_shard, *others)
    return res[0], res[1], res[2:]


def _conv_rows(cur, prev, cw_ref, cb, rid):
    acc = cw_ref[3:4, :] * cur + cb
    for k in range(1, CONV_W):
        acc = acc + cw_ref[3 - k:4 - k, :] * _shift_down(cur, prev, k, rid)
    return acc


ROW0_LOG_A = -1e30


def _row0_mask(rid):
    return jnp.where(rid == 0, ROW0_LOG_A, 0.0)


def _row0_bias(is_first_group, row0_mask):
    return is_first_group.astype(F32) * row0_mask


def _lru_gates(pa, px, ba, bx, sp8, row0_bias):
    r = _sig(pa + ba)
    i = _sig(px + bx)
    la = row0_bias - r * sp8
    a = jnp.exp(la)
    return r, i, a, _neg_expm1(2.0 * la, a * a)


def _mix_fwd(z, ln_g, ln_b, wm, bias, cw, cb, wax, ba, bx, lam, goa, gob, ex_arrs, ex_scatter):
    t_len = z.shape[0]
    n_chunk = t_len // CHUNK
    ex = _Exchange(ex_arrs, ex_scatter)
    n_in, n_out, n_scratch = 13, 5, 7

    def body(*refs):
        (z_ref, lng_ref, lnb_ref, wm_ref, bias_ref, cw_ref, cb_ref, wax_ref, ba_ref, bx_ref, lam_ref, goa_ref,
         gob_ref) = refs[:n_in]
        ex_in = refs[n_in:n_in + ex.n]
        y_ref, h_ref, vhb_ref, xcb_ref, rs_ref = refs[n_in + ex.n:n_in + ex.n + n_out]
        ex_out = refs[n_in + ex.n + n_out:n_in + 2 * ex.n + n_out]
        vn_s, xc_s, mixed_s, pre_s, y_s, carry_s, halo_s = refs[n_in + 2 * ex.n + n_out:n_in + 2 * ex.n + n_out + n_scratch]
        ex_sems = refs[n_in + 2 * ex.n + n_out + n_scratch:]
        step = pl.program_id(0)
        rid = _row_ids(D_BR)

        @pl.when(step == 0)
        def _():
            ex.start(ex_in, ex_out, ex_sems)
            carry_s[...] = jnp.zeros_like(carry_s)
            halo_s[...] = jnp.zeros_like(halo_s)

        lng, lnb, cb = lng_ref[...], lnb_ref[...], cb_ref[...]
        ba, bx, goa, gob = ba_ref[...], bx_ref[...], goa_ref[...], gob_ref[...]
        sp8 = LRU_C * _softplus(-lam_ref[...])
        row0 = _row0_mask(rid)

        def chunk(c_id, z_ref, y_ref, h_ref, vhb_ref, xcb_ref, rs_ref):
            def phase1(g, prev):
                rows = _rows(g)
                vg = _gelu(z_ref[rows, D_BR:2 * D_BR])
                xm = vg - _mean_last(vg)
                rs = lax.rsqrt(_mean_last(xm * xm) + EPS)
                vn_s[rows, :] = xm * rs
                rs_ref[rows, :] = jnp.broadcast_to(rs, (ROWS, HEAD))
                xb = z_ref[rows, 3 * D_BR:4 * D_BR]
                xc_s[rows, :] = _conv_rows(xb, prev, cw_ref, cb, rid)
                return xb

            halo_s[...] = _loop(N_GROUP, phase1, halo_s[...], unroll=8)
            vhb_ref[...] = vn_s[...].astype(BF16)
            xcb_ref[...] = xc_s[...].astype(BF16)

            for h in range(N_HEAD):
                cs = slice(h * HEAD, (h + 1) * HEAD)
                mixed_s[:, cs] = _dot(wm_ref[h], (vn_s[:, cs] * lng[:, cs] + lnb[:, cs]).astype(BF16))
                pre = _dot(xcb_ref[:, cs], wax_ref[h])
                pre_s[:, cs] = pre[:, :HEAD]
                pre_s[:, D_BR + h * HEAD:D_BR + (h + 1) * HEAD] = pre[:, HEAD:]

            def phase3(g, carry):
                rows = _rows(g)
                ug = _gelu(z_ref[rows, 0:D_BR])
                ga = z_ref[rows, 2 * D_BR:3 * D_BR]
                ya = ug * (mixed_s[rows, :] + bias_ref[rows, :]) * (ga * _sig(ga))
                y_s[rows, 0:D_BR] = ya * lax.rsqrt(_mean_last(ya * ya) + EPS) * goa

                bias0 = _row0_bias(jnp.logical_and(c_id == 0, g == 0), row0)
                _, i, a, m2 = _lru_gates(pre_s[rows, 0:D_BR], pre_s[rows, D_BR:2 * D_BR], ba, bx, sp8, bias0)
                b = jnp.sqrt(m2) * i * xc_s[rows, :]
                for d in (1, 2, 4):
                    a_sh = jnp.where(rid >= d, pltpu.roll(a, d, 0), 1.0)
                    b_sh = jnp.where(rid >= d, pltpu.roll(b, d, 0), 0.0)
                    b = a * b_sh + b
                    a = a * a_sh
                hh = b + a * carry
                h_ref[rows, :] = hh
                gb = z_ref[rows, 4 * D_BR:5 * D_BR]
                yb = hh * (gb * _sig(gb))
                y_s[rows, D_BR:2 * D_BR] = yb * lax.rsqrt(_mean_last(yb * yb) + EPS) * gob
                return _bcast_row(hh, ROWS - 1)

            carry_s[...] = _loop(N_GROUP, phase3, carry_s[...])
            y_ref[...] = y_s[...].astype(BF16)

        for sub in range(MIX_SUB):
            part = lambda ref, sub=sub: ref.at[pl.ds(sub * CHUNK, CHUNK)]
            chunk(step * MIX_SUB + sub, part(z_ref), part(y_ref), part(h_ref), part(vhb_ref), part(xcb_ref),
                  part(rs_ref))

        @pl.when(step == n_chunk // MIX_SUB - 1)
        def _():
            ex.wait(ex_in, ex_out, ex_sems)

    vec = pl.BlockSpec((1, D_BR), lambda i: (0, 0))
    blk = MIX_SUB * CHUNK
    res = pl.pallas_call(
        body, name="mix_fwd", grid=(n_chunk // MIX_SUB,),
        in_specs=[pl.BlockSpec((blk, D_IN), lambda i: (i, 0)), vec, vec,
                  pl.BlockSpec((N_HEAD, HEAD, HEAD), lambda i: (0, 0, 0)),
                  pl.BlockSpec((CHUNK, D_BR), lambda i: (0, 0)),
                  pl.BlockSpec((ROWS, D_BR), lambda i: (0, 0)), vec,
                  pl.BlockSpec((N_HEAD, HEAD, 2 * HEAD), lambda i: (0, 0, 0)), vec, vec, vec, vec, vec]
        + [ANY_SPEC] * ex.n,
        out_specs=[pl.BlockSpec((blk, 2 * D_BR), lambda i: (i, 0)), pl.BlockSpec((blk, D_BR), lambda i: (i, 0)),
                   pl.BlockSpec((blk, D_BR), lambda i: (i, 0)), pl.BlockSpec((blk, D_BR), lambda i: (i, 0)),
                   pl.BlockSpec((blk, HEAD), lambda i: (i, 0))] + [ANY_SPEC] * ex.n,
        out_shape=[SDS((t_len, 2 * D_BR), BF16), SDS((t_len, D_BR), F32), SDS((t_len, D_BR), BF16),
                   SDS((t_len, D_BR), BF16), SDS((t_len, HEAD), F32)] + ex.out_shape,
        scratch_shapes=[pltpu.VMEM((CHUNK, D_BR), F32), pltpu.VMEM((CHUNK, D_BR), F32), pltpu.VMEM((CHUNK, D_BR), F32),
                        pltpu.VMEM((CHUNK, 2 * D_BR), F32), pltpu.VMEM((CHUNK, 2 * D_BR), F32),
                        pltpu.VMEM((ROWS, D_BR), F32), pltpu.VMEM((ROWS, D_BR), F32)] + ex.scratch,
        compiler_params=_params(("arbitrary",), 32),
    )(z, ln_g, ln_b, wm, bias, cw, cb, wax, ba, bx, lam, goa, gob, *ex_arrs)
    return res[:n_out], res[n_out:]


def _load_weight(w_hbm, w_vmem, sem):
    @pl.when(pl.program_id(0) == 0)
    def _():
        cp = pltpu.make_async_copy(w_hbm, w_vmem, sem)
        cp.start()
        cp.wait()


def _out_proj(y, x, w_out, post_g, tm=512):
    t_len = y.shape[0]

    def body(y_ref, x_ref, w_hbm, g_ref, h1_ref, ob_ref, w_s, o_s, sem):
        _load_weight(w_hbm, w_s, sem)
        o_s[...] = _dot(y_ref[...], w_s[...])
        g = g_ref[...]

        def rows_body(q, _):
            rows = _tile_rows(q)
            o = o_s[rows, :]
            h1_ref[rows, :] = x_ref[rows, :] + o * lax.rsqrt(_mean_last(o * o) + EPS) * g
            ob_ref[rows, :] = o.astype(BF16)
            return 0

        _loop(tm // TILE_ROWS, rows_body, 0, unroll=TILE_UNROLL)

    tile = pl.BlockSpec((tm, D_MODEL), lambda i: (i, 0))
    return pl.pallas_call(
        body, name="out_proj", grid=(t_len // tm,),
        in_specs=[tile, tile, pl.BlockSpec(memory_space=pl.ANY), pl.BlockSpec((1, D_MODEL), lambda i: (0, 0))],
        out_specs=[tile, tile],
        out_shape=[SDS((t_len, D_MODEL), F32), SDS((t_len, D_MODEL), BF16)],
        scratch_shapes=[pltpu.VMEM((D_MODEL, D_MODEL), BF16), pltpu.VMEM((tm, D_MODEL), F32), pltpu.SemaphoreType.DMA],
        compiler_params=_params(("arbitrary",), 44),
    )(y, x, w_out, post_g)


def _ple_loss(h1, p, tgt, w_pg, w_pe_g, tm=256):
    t_len = h1.shape[0]
    n_tile = t_len // tm
    pe_shard = D_MODEL // N_DEV

    def body(h1_ref, p_ref, t_ref, w_hbm, wpe_ref, dh2_ref, dgl_ref, h1b_ref, loss_ref, dwpe_ref, w_s, pe_s, gl_s, acc_s,
             dpe_s, gpe_s, sem):
        _load_weight(w_hbm, w_s, sem)
        i = pl.program_id(0)

        @pl.when(i == 0)
        def _():
            acc_s[...] = jnp.zeros_like(acc_s)
            gpe_s[...] = jnp.zeros_like(gpe_s)

        h1b_ref[...] = h1_ref[...].astype(BF16)
        pb = p_ref[...].astype(BF16)
        for j in range(N_DEV):
            pe_s[:, j * pe_shard:(j + 1) * pe_shard] = _dot(pb, wpe_ref[j])
        gl_s[...] = _dot(h1b_ref[...], w_s[...])

        def rows_body(q, acc):
            rows = _tile_rows(q)
            pe = pe_s[rows, :]
            g = _sig(gl_s[rows, :])
            e = h1_ref[rows, :] + pe * g - t_ref[rows, :]
            dh2 = e * (1.0 / D_MODEL)
            dh2_ref[rows, :] = dh2
            dpe_s[rows, :] = (dh2 * g).astype(BF16)
            dgl_ref[rows, :] = (dh2 * pe * g * (1.0 - g)).astype(BF16)
            return acc + _fold_rows(e * e)

        acc_s[...] = _loop(tm // TILE_ROWS, rows_body, acc_s[...], unroll=TILE_UNROLL)
        gpe_s[...] += _dot_tn(pb, dpe_s[...])

        @pl.when(i == n_tile - 1)
        def _():
            loss_ref[...] = jnp.full(loss_ref.shape, 0.5 / D_MODEL * jnp.sum(acc_s[...]), F32)
            for j in range(N_DEV):
                dwpe_ref[j] = gpe_s[:, j * pe_shard:(j + 1) * pe_shard].astype(BF16)

    tile = pl.BlockSpec((tm, D_MODEL), lambda i: (i, 0))
    pe_blocks = pl.BlockSpec((N_DEV, D_PLE, pe_shard), lambda i: (0, 0, 0))
    return pl.pallas_call(
        body, name="ple_loss", grid=(n_tile,),
        in_specs=[tile, pl.BlockSpec((tm, D_PLE), lambda i: (i, 0)), tile, pl.BlockSpec(memory_space=pl.ANY), pe_blocks],
        out_specs=[tile, tile, tile, pl.BlockSpec((ROWS, HEAD), lambda i: (0, 0)), pe_blocks],
        out_shape=[SDS((t_len, D_MODEL), F32), SDS((t_len, D_MODEL), BF16), SDS((t_len, D_MODEL), BF16),
                   SDS((ROWS, HEAD), F32), SDS((N_DEV, D_PLE, pe_shard), BF16)],
        scratch_shapes=[pltpu.VMEM((D_MODEL, D_MODEL), BF16), pltpu.VMEM((tm, D_MODEL), F32),
                        pltpu.VMEM((tm, D_MODEL), F32), pltpu.VMEM((ROWS, D_MODEL), F32), pltpu.VMEM((tm, D_MODEL), BF16),
                        pltpu.VMEM((D_PLE, D_MODEL), F32), pltpu.SemaphoreType.DMA],
        compiler_params=_params(("arbitrary",), 48),
    )(h1, p, tgt, w_pg, w_pe_g)


def _tail_bwd(dh2, dgl, ob, w_pg, w_out, post_g, tm=256):
    t_len = dh2.shape[0]
    n_tile = t_len // tm

    def body(dh2_ref, dgl_ref, ob_ref, wpg_hbm, wout_hbm, g_ref, dh1_ref, do_ref, dy_ref, dg_ref, wpg_s, wout_s, t_s,
             acc_s, sems):
        _load_weight(wpg_hbm, wpg_s, sems.at[0])
        _load_weight(wout_hbm, wout_s, sems.at[1])
        i = pl.program_id(0)

        @pl.when(i == 0)
        def _():
            acc_s[...] = jnp.zeros_like(acc_s)

        t_s[...] = _dot_nt(dgl_ref[...], wpg_s[...])
        g = g_ref[...]

        def rows_body(q, acc):
            rows = _tile_rows(q)
            dh1 = dh2_ref[rows, :] + t_s[rows, :]
            dh1_ref[rows, :] = dh1
            o = ob_ref[rows, :].astype(F32)
            rr = lax.rsqrt(_mean_last(o * o) + EPS)
            on = o * rr
            dog = dh1 * g
            do_ref[rows, :] = (rr * (dog - on * _mean_last(dog * on))).astype(BF16)
            return acc + _fold_rows(dh1 * on)

        acc_s[...] = _loop(tm // TILE_ROWS, rows_body, acc_s[...], unroll=TILE_UNROLL)
        dy_ref[...] = _dot_nt(do_ref[...], wout_s[...]).astype(BF16)

        @pl.when(i == n_tile - 1)
        def _():
            dg_ref[...] = jnp.sum(acc_s[...], axis=0, keepdims=True)

    tile = pl.BlockSpec((tm, D_MODEL), lambda i: (i, 0))
    vec = pl.BlockSpec((1, D_MODEL), lambda i: (0, 0))
    hbm = pl.BlockSpec(memory_space=pl.ANY)
    return pl.pallas_call(
        body, name="tail_bwd", grid=(n_tile,),
        in_specs=[tile, tile, tile, hbm, hbm, vec],
        out_specs=[tile, tile, tile, vec],
        out_shape=[SDS((t_len, D_MODEL), F32), SDS((t_len, D_MODEL), BF16), SDS((t_len, D_MODEL), BF16),
                   SDS((1, D_MODEL), F32)],
        scratch_shapes=[pltpu.VMEM((D_MODEL, D_MODEL), BF16), pltpu.VMEM((D_MODEL, D_MODEL), BF16),
                        pltpu.VMEM((tm, D_MODEL), F32), pltpu.VMEM((ROWS, D_MODEL), F32), pltpu.SemaphoreType.DMA((2,))],
        compiler_params=_params(("arbitrary",), 48),
    )(dh2, dgl, ob, w_pg, w_out, post_g)


def _mix_bwd(z, dy, h, vhb, xcb, rs, ln_g, ln_b, wm, wm_t, bias, cw, cb, wax, wax_t, ba, bx, lam, goa, gob, ex_arrs,
             ex_scatter):
    t_len = z.shape[0]
    n_chunk = t_len // CHUNK
    halo_blocks = CHUNK // ROWS
    ex = _Exchange(ex_arrs, ex_scatter)
    n_in, n_out, n_scratch = 21, 5, 16

    blocked = (0, 1, 2, 4, 5, 6, n_in + ex.n)

    def body(*refs):
        step = pl.program_id(0)
        for sub in reversed(range(MIX_SUB)):
            views = list(refs)
            for idx in blocked:
                views[idx] = refs[idx].at[pl.ds(sub * CHUNK, CHUNK)]
            h_before = refs[2].at[pl.ds(sub * CHUNK - ROWS, ROWS)] if sub else refs[3]
            chunk((n_chunk // MIX_SUB - 1 - step) * MIX_SUB + sub,
                  step == 0 if sub == MIX_SUB - 1 else None,
                  step == n_chunk // MIX_SUB - 1 if sub == 0 else None, h_before, *views)

    def chunk(c_id, first, last, h_before, *refs):
        (z_ref, dy_ref, h_ref, hhalo_ref, vhb_ref, xcb_ref, rs_ref, lng_ref, lnb_ref, wm_ref, wmt_ref, bias_ref, cw_ref,
         cb_ref, wax_ref, waxt_ref, ba_ref, bx_ref, lam_ref, goa_ref, gob_ref) = refs[:n_in]
        ex_in = refs[n_in:n_in + ex.n]
        dz_ref, vecs_ref, dws_ref, dwax_ref, dbs_ref = refs[n_in + ex.n:n_in + ex.n + n_out]
        ex_out = refs[n_in + ex.n + n_out:n_in + 2 * ex.n + n_out]
        (vnb_s, vh_s, xc_s, mixed_s, pre_s, dmix_s, dvn_s, dho_s, dxc_s, dpre_s, dz_s, acc_s, accdm_s,
         cg_s, ca_s, dxchalo_s) = refs[n_in + 2 * ex.n + n_out:n_in + 2 * ex.n + n_out + n_scratch]
        ex_sems = refs[n_in + 2 * ex.n + n_out + n_scratch:]
        rid = _row_ids(D_BR)
        first_chunk = c_id == 0

        if first is not None:
            @pl.when(first)
            def _():
                ex.start(ex_in, ex_out, ex_sems)
                acc_s[...] = jnp.zeros_like(acc_s)
                accdm_s[...] = jnp.zeros_like(accdm_s)
                cg_s[...] = jnp.zeros_like(cg_s)
                ca_s[...] = jnp.zeros_like(ca_s)
                dxchalo_s[...] = jnp.zeros_like(dxchalo_s)
                dws_ref[...] = jnp.zeros_like(dws_ref)
                dwax_ref[...] = jnp.zeros_like(dwax_ref)

        lng, lnb = lng_ref[...], lnb_ref[...]
        h_halo = jnp.where(first_chunk, 0.0, h_before[...])

        def prev_rows(ref, cols, g, halo):
            before = ref[pl.ds(pl.multiple_of(jnp.maximum(g - 1, 0) * ROWS, ROWS), ROWS), cols]
            return jnp.where(g > 0, before, halo)

        vh_s[...] = vhb_ref[...].astype(F32)
        xc_s[...] = xcb_ref[...].astype(F32)

        for hd in range(N_HEAD):
            cs = slice(hd * HEAD, (hd + 1) * HEAD)
            vnb_s[:, cs] = (vh_s[:, cs] * lng[:, cs] + lnb[:, cs]).astype(BF16)
            mixed_s[:, cs] = _dot(wm_ref[hd], vnb_s[:, cs])
            pre = _dot(xcb_ref[:, cs], wax_ref[hd])
            pre_s[:, cs] = pre[:, :HEAD]
            pre_s[:, D_BR + hd * HEAD:D_BR + (hd + 1) * HEAD] = pre[:, HEAD:]

        goa, gob = goa_ref[...], gob_ref[...]

        def phase3(g, _):
            rows = _rows(g)
            ug, dug = _gelu(z_ref[rows, 0:D_BR], with_grad=True)
            ga = z_ref[rows, 2 * D_BR:3 * D_BR]
            sga = _sig(ga)
            sa = ga * sga
            mixed = mixed_s[rows, :] + bias_ref[rows, :]
            ya0 = ug * mixed
            ya = ya0 * sa
            ra = lax.rsqrt(_mean_last(ya * ya) + EPS)
            dyan = dy_ref[rows, 0:D_BR].astype(F32)
            acc_s[V_GOUT_A] += dyan * ya * ra
            dyg = dyan * goa
            dya = ra * dyg - ya * (ra * ra * ra) * _mean_last(dyg * ya)
            dya0 = dya * sa
            dz_s[rows, 2 * D_BR:3 * D_BR] = dya * ya0 * _silu_grad(sga, sa)
            dmix = dya0 * ug
            dmix_s[rows, :] = dmix
            accdm_s[rows, :] += dmix
            dz_s[rows, 0:D_BR] = dya0 * mixed * dug

            hh = h_ref[rows, :]
            gb = z_ref[rows, 4 * D_BR:5 * D_BR]
            sgb = _sig(gb)
            sb = gb * sgb
            yb = hh * sb
            rb = lax.rsqrt(_mean_last(yb * yb) + EPS)
            dybn = dy_ref[rows, D_BR:2 * D_BR].astype(F32)
            acc_s[V_GOUT_B] += dybn * yb * rb
            dyg = dybn * gob
            dyb = rb * dyg - yb * (rb * rb * rb) * _mean_last(dyg * yb)
            dho_s[rows, :] = dyb * sb
            dz_s[rows, 4 * D_BR:5 * D_BR] = dyb * hh * _silu_grad(sgb, sb)
            return 0

        _loop(N_GROUP, phase3, 0)

        for hd in range(N_HEAD):
            cs = slice(hd * HEAD, (hd + 1) * HEAD)
            dmb = dmix_s[:, cs].astype(BF16)
            dvn_s[:, cs] = _dot(wmt_ref[hd], dmb)
            dws_ref[hd] += _dot_nt(dmb, vnb_s[:, cs])

        def phase5(g, _):
            rows = _rows(g)
            dvn = dvn_s[rows, :]
            vh = vh_s[rows, :]
            acc_s[V_LN_G] += dvn * vh
            acc_s[V_LN_B] += dvn
            dvh = dvn * lng
            rs = rs_ref[rows, 0:1]
            dvg = rs * (dvh - _mean_last(dvh) - vh * _mean_last(dvh * vh))
            dz_s[rows, D_BR:2 * D_BR] = dvg * _gelu(z_ref[rows, D_BR:2 * D_BR], with_grad=True)[1]
            return 0

        _loop(N_GROUP, phase5, 0)

        ba, bx = ba_ref[...], bx_ref[...]
        sp8 = LRU_C * _softplus(-lam_ref[...])
        row0 = _row0_mask(rid)

        def phase6(k, carry):
            cg, ca = carry
            g = N_GROUP - 1 - k
            rows = _rows(g)
            bias0 = _row0_bias(jnp.logical_and(first_chunk, g == 0), row0)
            r, i, a, m2 = _lru_gates(pre_s[rows, 0:D_BR], pre_s[rows, D_BR:2 * D_BR], ba, bx, sp8, bias0)
            a_nx = jnp.where(rid < ROWS - 1, pltpu.roll(a, ROWS - 1, 0), ca)
            aa, bb = a_nx, dho_s[rows, :]
            for d in (1, 2, 4):
                a_sh = jnp.where(rid < ROWS - d, pltpu.roll(aa, ROWS - d, 0), 1.0)
                b_sh = jnp.where(rid < ROWS - d, pltpu.roll(bb, ROWS - d, 0), 0.0)
                bb = aa * b_sh + bb
                aa = aa * a_sh
            gg = bb + aa * cg
            hh = h_ref[rows, :]
            hprev = _shift_down(hh, prev_rows(h_ref, slice(None), g, h_halo), 1, rid)
            xc = xc_s[rows, :]
            gx = gg * xc
            dla = gg * hprev * a - gx * i * (a * a) * lax.rsqrt(m2)
            acc_s[V_LAM] += -(dla * r)
            dpa = -(dla * sp8) * r * (1.0 - r)
            mi = jnp.sqrt(m2) * i
            dpx = gx * mi * (1.0 - i)
            acc_s[V_B_A] += dpa
            acc_s[V_B_X] += dpx
            dpre_s[rows, 0:D_BR] = dpa
            dpre_s[rows, D_BR:2 * D_BR] = dpx
            dxc_s[rows, :] = gg * mi
            return _bcast_row(gg, 0), _bcast_row(a, 0)

        cg, ca = _loop(N_GROUP, phase6, (cg_s[...], ca_s[...]))
        cg_s[...] = cg
        ca_s[...] = ca

        for hd in range(N_HEAD):
            cs = slice(hd * HEAD, (hd + 1) * HEAD)
            dpre = jnp.concatenate([dpre_s[:, cs], dpre_s[:, D_BR + hd * HEAD:D_BR + (hd + 1) * HEAD]], axis=1).astype(BF16)
            dxc_s[:, cs] += _dot(dpre, waxt_ref[hd])
            dwax_ref[hd] += _dot_tn(xcb_ref[:, cs], dpre)

        def phase8(k, nxt):
            g = N_GROUP - 1 - k
            rows = _rows(g)
            dxc = dxc_s[rows, :]
            acc_s[V_CONV_B] += dxc
            xb = z_ref[rows, 3 * D_BR:4 * D_BR]
            dxb = cw_ref[3:4, :] * dxc
            acc_s[V_CONV_W + 3] += dxc * xb
            for j in range(1, CONV_W):
                later = _shift_up(dxc, nxt, j, rid)
                dxb = dxb + cw_ref[3 - j:4 - j, :] * later
                acc_s[V_CONV_W + 3 - j] += later * xb
            dz_s[rows, 3 * D_BR:4 * D_BR] = dxb
            return dxc

        dxchalo_s[...] = _loop(N_GROUP, phase8, dxchalo_s[...])
        dz_ref[...] = dz_s[...].astype(BF16)

        if last is not None:
            @pl.when(last)
            def _():
                for v in range(N_VEC):
                    vecs_ref[v:v + 1, :] = jnp.sum(acc_s[v], axis=0, keepdims=True)
                lam = lam_ref[...]
                vecs_ref[V_LAM:V_LAM + 1, :] = vecs_ref[V_LAM:V_LAM + 1, :] * (-LRU_C * _sig(-lam))
                tril = (lax.broadcasted_iota(jnp.int32, (HEAD, HEAD), 0)
                        >= lax.broadcasted_iota(jnp.int32, (HEAD, HEAD), 1))
                ones = jnp.ones((ROWS, HEAD), BF16)
                for hd in range(N_HEAD):
                    cs = slice(hd * HEAD, (hd + 1) * HEAD)
                    dws_ref[hd] = jnp.where(tril, dws_ref[hd], 0.0)
                    blk = accdm_s[:, cs]
                    hi = blk.astype(BF16)
                    lo = (blk - hi.astype(F32)).astype(BF16)
                    dbs_ref[hd:hd + 1, :] = (_dot_nt(ones, hi) + _dot_nt(ones, lo))[0:1, :]
                ex.wait(ex_in, ex_out, ex_sems)

    vec = pl.BlockSpec((1, D_BR), lambda i: (0, 0))
    n_step = n_chunk // MIX_SUB
    rows_blk = MIX_SUB * CHUNK
    rev = lambda i: (n_step - 1 - i, 0)
    halo = lambda col: (lambda i: (jnp.maximum((n_step - 1 - i) * MIX_SUB * halo_blocks - 1, 0), col))
    full3 = lambda a, b, c: pl.BlockSpec((a, b, c), lambda i: (0, 0, 0))
    big = lambda w: pltpu.VMEM((CHUNK, w), F32)
    res = pl.pallas_call(
        body, name="mix_bwd", grid=(n_step,),
        in_specs=[pl.BlockSpec((rows_blk, D_IN), rev), pl.BlockSpec((rows_blk, 2 * D_BR), rev),
                  pl.BlockSpec((rows_blk, D_BR), rev),
                  pl.BlockSpec((ROWS, D_BR), halo(0)), pl.BlockSpec((rows_blk, D_BR), rev),
                  pl.BlockSpec((rows_blk, D_BR), rev),
                  pl.BlockSpec((rows_blk, HEAD), rev), vec, vec,
                  full3(N_HEAD, HEAD, HEAD), full3(N_HEAD, HEAD, HEAD),
                  pl.BlockSpec((CHUNK, D_BR), lambda i: (0, 0)), pl.BlockSpec((ROWS, D_BR), lambda i: (0, 0)), vec,
                  full3(N_HEAD, HEAD, 2 * HEAD), full3(N_HEAD, 2 * HEAD, HEAD), vec, vec, vec, vec, vec]
        + [ANY_SPEC] * ex.n,
        out_specs=[pl.BlockSpec((rows_blk, D_IN), rev), pl.BlockSpec((N_VEC, D_BR), lambda i: (0, 0)),
                   full3(N_HEAD, HEAD, HEAD), full3(N_HEAD, HEAD, 2 * HEAD),
                   pl.BlockSpec((N_HEAD, HEAD), lambda i: (0, 0))] + [ANY_SPEC] * ex.n,
        out_shape=[SDS((t_len, D_IN), BF16), SDS((N_VEC, D_BR), F32), SDS((N_HEAD, HEAD, HEAD), F32),
                   SDS((N_HEAD, HEAD, 2 * HEAD), F32), SDS((N_HEAD, HEAD), F32)] + ex.out_shape,
        scratch_shapes=[pltpu.VMEM((CHUNK, D_BR), BF16), big(D_BR), big(D_BR), big(D_BR), big(2 * D_BR), big(D_BR),
                        big(D_BR), big(D_BR), big(D_BR), big(2 * D_BR), big(D_IN),
                        pltpu.VMEM((N_VEC, ROWS, D_BR), F32), big(D_BR),
                        pltpu.VMEM((ROWS, D_BR), F32), pltpu.VMEM((ROWS, D_BR), F32), pltpu.VMEM((ROWS, D_BR), F32)]
        + ex.scratch,
        compiler_params=_params(("arbitrary",), 48),
    )(z, dy, h, h, vhb, xcb, rs, ln_g, ln_b, wm, wm_t, bias, cw, cb, wax, wax_t, ba, bx, lam, goa, gob, *ex_arrs)
    return res[:n_out], res[n_out:]


def _in_bwd(dz, w_in_g, x, dh1, pre_g, tm=256):
    t_len = x.shape[0]
    n_tile = t_len // tm

    def body(dz_ref, w_hbm, x_ref, dh1_ref, g_ref, gx_ref, dg_ref, w_s, t_even, t_odd, dg_s, w_sems):
        i = pl.program_id(0)

        @pl.when(i == 0)
        def _():
            loads = [pltpu.make_async_copy(w_hbm.at[s], w_s.at[:, s * W_IN_SHARD:(s + 1) * W_IN_SHARD], w_sems.at[s])
                     for s in range(N_DEV)]
            for cp in loads:
                cp.start()
            dg_s[...] = jnp.zeros_like(dg_s)
            t_odd[...] = jnp.zeros_like(t_odd)
            for cp in loads:
                cp.wait()

        def step(t_new, t_old):
            g = g_ref[...]
            acc = dg_s[...]
            for q in range(tm // TILE_ROWS):
                rows = slice(q * TILE_ROWS, (q + 1) * TILE_ROWS)
                xv = x_ref[rows, :]
                r = lax.rsqrt(_mean_last(xv * xv) + EPS)
                xh = xv * r
                dhn = t_old[rows, :]
                dg = dhn * g
                gx_ref[rows, :] = dh1_ref[rows, :] + r * (dg - xh * _mean_last(dg * xh))
                acc = acc + _fold_rows(dhn * xh)
            dg_s[...] = acc
            t_new[...] = _dot_nt(dz_ref[...], w_s[...])

        @pl.when(i % 2 == 0)
        def _():
            step(t_even, t_odd)

        @pl.when(i % 2 == 1)
        def _():
            step(t_odd, t_even)

        @pl.when(i == n_tile)
        def _():
            dg_ref[...] = jnp.sum(dg_s[...], axis=0, keepdims=True)

    matmul_tile = lambda i: (jnp.minimum(i, n_tile - 1), 0)
    rows_tile = lambda i: (jnp.maximum(i - 1, 0), 0)
    res = pl.pallas_call(
        body, name="in_bwd", grid=(n_tile + 1,),
        in_specs=[pl.BlockSpec((tm, D_IN), matmul_tile), ANY_SPEC, pl.BlockSpec((tm, D_MODEL), rows_tile),
                  pl.BlockSpec((tm, D_MODEL), rows_tile), pl.BlockSpec((1, D_MODEL), lambda i: (0, 0))],
        out_specs=[pl.BlockSpec((tm, D_MODEL), rows_tile), pl.BlockSpec((1, D_MODEL), lambda i: (0, 0))],
        out_shape=[SDS((t_len, D_MODEL), F32), SDS((1, D_MODEL), F32)],
        scratch_shapes=[pltpu.VMEM((D_MODEL, D_IN), BF16), pltpu.VMEM((tm, D_MODEL), F32), pltpu.VMEM((tm, D_MODEL), F32),
                        pltpu.VMEM((ROWS, D_MODEL), F32), pltpu.SemaphoreType.DMA((N_DEV,))],
        compiler_params=_params(("arbitrary",), 54),
    )(dz, w_in_g, x, dh1, pre_g)
    return res[0], res[1]


def _grad_w(a, b, bn, shard_major, name, tk=1024, ex_arrs=(), ex_scatter=()):
    t_len, m = a.shape
    n = b.shape[1]
    n_j, n_k = n // bn, t_len // tk
    ex = _Exchange(ex_arrs, ex_scatter)

    def body(a_ref, b_ref, *refs):
        ex_in, o_ref, ex_out = refs[:ex.n], refs[ex.n], refs[ex.n + 1:2 * ex.n + 1]
        acc_s, ex_sems = refs[2 * ex.n + 1], refs[2 * ex.n + 2:]
        j, k = pl.program_id(0), pl.program_id(1)
        if ex.n:
            @pl.when(jnp.logical_and(j == 0, k == 0))
            def _():
                ex.start(ex_in, ex_out, ex_sems)

        @pl.when(k == 0)
        def _():
            acc_s[...] = jnp.zeros_like(acc_s)

        acc_s[...] += _dot_tn(a_ref[...], b_ref[...])

        @pl.when(k == n_k - 1)
        def _():
            o_ref[...] = acc_s[...].astype(BF16)

        if ex.n:
            @pl.when(jnp.logical_and(j == n_j - 1, k == n_k - 1))
            def _():
                ex.wait(ex_in, ex_out, ex_sems)

    if shard_major:
        out_spec, out_shape = pl.BlockSpec((None, m, bn), lambda j, k: (j, 0, 0)), SDS((n_j, m, bn), BF16)
    else:
        out_spec, out_shape = pl.BlockSpec((m, bn), lambda j, k: (0, j)), SDS((m, n), BF16)
    res = pl.pallas_call(
        body, name=name, grid=(n_j, n_k),
        in_specs=[pl.BlockSpec((tk, m), lambda j, k: (k, 0)), pl.BlockSpec((tk, bn), lambda j, k: (k, j))]
        + [ANY_SPEC] * ex.n,
        out_specs=[out_spec] + [ANY_SPEC] * ex.n, out_shape=[out_shape] + ex.out_shape,
        scratch_shapes=[pltpu.VMEM((m, bn), F32)] + (ex.scratch if ex.n else []),
        compiler_params=_params(("arbitrary", "arbitrary"), 40),
    )(a, b, *ex_arrs)
    return res[0], res[1:]


RS_CHIPS = (6, 2, 4, 0)
RS_SLOTS = (0, 1, 2, 4, 6)


def _grad_w_in_pairs(hn, dz, ex_arrs, ex_scatter, tk=1024):
    t_len = hn.shape[0]
    n_k = t_len // tk
    n_ph = len(RS_CHIPS)
    ex = _Exchange(ex_arrs, ex_scatter)
    me_out = 4 * lax.axis_index("x") + 2 * lax.axis_index("y") + lax.axis_index("c")
    order = jnp.stack([(me_out ^ chip) // 2 for chip in RS_CHIPS]).astype(jnp.int32)
    slots = jnp.stack([me_out ^ k for k in RS_SLOTS]).astype(jnp.int32)
    shard = W_IN_SHARD

    def body(order_ref, a_ref, b_ref, *refs):
        ex_in, parts_hbm, ex_out = refs[:ex.n], refs[ex.n], refs[ex.n + 1:2 * ex.n + 1]
        (acc_s, tb_s, stage_s, rx_s, d2d_send, d2d_recv, ici_send, ici_recv, sib_sems,
         loc_sem) = refs[2 * ex.n + 1:2 * ex.n + 11]
        ex_sems = refs[2 * ex.n + 11:]
        j, k = pl.program_id(0), pl.program_id(1)
        x, y, c, me = _mesh_place()
        sib = _peer(x, y, c, SIBLING)[0]

        def to_sibling(p):
            return _remote(stage_s.at[0], rx_s.at[p % 2], d2d_send.at[p], d2d_recv.at[p], sib)

        def over_ici(p):
            dev = _peer(x, y, c, RS_CHIPS[p])[0]
            return _remote(stage_s.at[1], parts_hbm.at[me], ici_send.at[p], ici_recv.at[p], dev)

        def own_chip():
            return (_remote(stage_s.at[0], parts_hbm.at[me], sib_sems.at[0], sib_sems.at[1], sib),
                    pltpu.make_async_copy(stage_s.at[1], parts_hbm.at[me], loc_sem.at[0]))

        @pl.when(jnp.logical_and(j == 0, k == 0))
        def _():
            ex.start(ex_in, ex_out, ex_sems)

        for p in range(n_ph - 1):
            for core in (0, 1):
                @pl.when(jnp.logical_and(jnp.logical_and(j == p + 1, k == 0), c == core))
                def _(p=p, core=core):
                    to_sibling(p).wait_recv()
                    if p >= 1:
                        over_ici(p - 1).wait_send()
                    mine = acc_s[:, core * shard:(core + 1) * shard]
                    stage_s[1] = (mine + rx_s[p % 2].astype(F32)).astype(BF16)
                    over_ici(p).start()

        @pl.when(k == 0)
        def _():
            acc_s[...] = jnp.zeros_like(acc_s)

        a = a_ref[...]
        acc_s[:, 0:W_BODY] += _dot_tn(a, b_ref[:, 0:W_BODY])
        acc_s[:, shard:shard + W_BODY] += _dot_tn(a, b_ref[:, shard:shard + W_BODY])
        tb_s[:, 0:W_TAIL] = b_ref[:, W_BODY:shard]
        tb_s[:, W_TAIL:2 * W_TAIL] = b_ref[:, shard + W_BODY:2 * shard]
        tails = _dot_tn(a, tb_s[...])
        acc_s[:, W_BODY:shard] += tails[:, 0:W_TAIL]
        acc_s[:, shard + W_BODY:2 * shard] += tails[:, W_TAIL:2 * W_TAIL]

        for p in range(n_ph):
            for core in (0, 1):
                @pl.when(jnp.logical_and(jnp.logical_and(j == p, k == n_k - 1), c == core))
                def _(p=p, core=core):
                    same = acc_s[:, core * shard:(core + 1) * shard]
                    other = acc_s[:, (1 - core) * shard:(2 - core) * shard]
                    if p >= 1:
                        to_sibling(p - 1).wait_send()
                    stage_s[0] = other.astype(BF16)
                    if p < n_ph - 1:
                        to_sibling(p).start()
                    else:
                        over_ici(n_ph - 2).wait_send()
                        stage_s[1] = same.astype(BF16)
                        for cp in own_chip():
                            cp.start()

        @pl.when(jnp.logical_and(j == n_ph - 1, k == n_k - 1))
        def _():
            to_sib, local = own_chip()
            to_sib.wait_send()
            local.wait()
            _remote(stage_s.at[0], parts_hbm.at[_peer(x, y, c, SIBLING)[1]], sib_sems.at[0], sib_sems.at[1], sib).wait_recv()
            for p in range(n_ph - 1):
                dev, lin = _peer(x, y, c, RS_CHIPS[p])
                _remote(stage_s.at[0], parts_hbm.at[lin], ici_send.at[p], ici_recv.at[p], dev).wait_recv()
            ex.wait(ex_in, ex_out, ex_sems)

    dma = lambda n: pltpu.SemaphoreType.DMA((n,))
    grid_spec = pltpu.PrefetchScalarGridSpec(
        num_scalar_prefetch=1, grid=(n_ph, n_k),
        in_specs=[pl.BlockSpec((tk, D_MODEL), lambda j, k, order: (k, 0)),
                  pl.BlockSpec((tk, 2 * shard), lambda j, k, order: (k, order[j]))] + [ANY_SPEC] * ex.n,
        out_specs=[ANY_SPEC] * (1 + ex.n),
        scratch_shapes=[pltpu.VMEM((D_MODEL, 2 * shard), F32), pltpu.VMEM((tk, 2 * W_TAIL), BF16),
                        pltpu.VMEM((2, D_MODEL, shard), BF16),
                        pltpu.VMEM((2, D_MODEL, shard), BF16), dma(n_ph - 1), dma(n_ph - 1), dma(n_ph - 1),
                        dma(n_ph - 1), dma(2), dma(1)] + ex.scratch)
    res = pl.pallas_call(
        body, name="grad_w_in", grid_spec=grid_spec,
        out_shape=[SDS((N_DEV, D_MODEL, shard), BF16)] + ex.out_shape,
        compiler_params=_params(("arbitrary", "arbitrary"), 54),
    )(order, hn, dz, *ex_arrs)
    return res[0], slots, res[1:]


def _sum_parts(parts, name):
    def body(p_ref, o_ref):
        g = p_ref[0].astype(F32)
        for s in range(1, parts.shape[0]):
            g = g + p_ref[s].astype(F32)
        o_ref[...] = g

    return pl.pallas_call(body, name=name, out_shape=SDS(parts.shape[1:], F32))(parts)


def _adamw_math(g, w_ref, m_ref, v_ref, g_ref, d_ref, nm_ref, nv_ref):
    c1 = 1.0 - ADAM_B1 ** ADAM_STEP
    c2 = 1.0 - ADAM_B2 ** ADAM_STEP
    g_ref[...] = g
    nm = ADAM_B1 * m_ref[...] + (1.0 - ADAM_B1) * g
    nv = ADAM_B2 * v_ref[...] + (1.0 - ADAM_B2) * (g * g)
    nm_ref[...] = nm
    nv_ref[...] = nv
    d_ref[...] = -ADAM_LR * ((nm / c1) / (jnp.sqrt(nv / c2) + ADAM_EPS) + ADAM_WD * w_ref[...])


def _adamw(parts, w, m, v, name, tr):
    rows, cols = w.shape
    n_parts = parts.shape[0]

    def body(p_ref, *refs):
        g = p_ref[0].astype(F32)
        for s in range(1, n_parts):
            g = g + p_ref[s].astype(F32)
        _adamw_math(g, *refs)

    tile = pl.BlockSpec((tr, cols), lambda i: (i, 0))
    return pl.pallas_call(
        body, name=name, grid=(rows // tr,),
        in_specs=[pl.BlockSpec((n_parts, tr, cols), lambda i: (0, i, 0)), tile, tile, tile],
        out_specs=[tile] * 4, out_shape=[SDS((rows, cols), F32)] * 4,
        compiler_params=_params(("arbitrary",), 40),
    )(parts, w, m, v)


def _adamw_unpacked(grads, triples, name):
    n = len(triples)
    n_rows = [t[0].shape[0] for t in triples]

    def body(g_ref, *refs):
        ins, outs = refs[:3 * n], refs[3 * n:]
        row = 0
        for i in range(n):
            _adamw_math(g_ref[row:row + n_rows[i], :], *ins[3 * i:3 * i + 3], *outs[4 * i:4 * i + 4])
            row += n_rows[i]
        outs[4 * n][...] = g_ref[row:row + ROWS, :]

    out_shape = [SDS((r, LANES), F32) for r in n_rows for _ in range(4)] + [SDS((ROWS, LANES), F32)]
    return pl.pallas_call(
        body, name=name, out_shape=out_shape,
        compiler_params=pltpu.CompilerParams(vmem_limit_bytes=40 * MIB),
    )(grads, *[a for t in triples for a in t])


def _adamw_slots(parts, slots, w, m, v, name, tr):
    rows, cols = w.shape
    n_slots = slots.shape[0]

    def body(slots_ref, *refs):
        g = refs[0][...].astype(F32)
        for s in range(1, n_slots):
            g = g + refs[s][...].astype(F32)
        _adamw_math(g, *refs[n_slots:])

    tile = pl.BlockSpec((tr, cols), lambda i, slots: (i, 0))
    part = lambda s: pl.BlockSpec((None, tr, cols), lambda i, slots: (slots[s], i, 0))
    grid_spec = pltpu.PrefetchScalarGridSpec(
        num_scalar_prefetch=1, grid=(rows // tr,),
        in_specs=[part(s) for s in range(n_slots)] + [tile, tile, tile], out_specs=[tile] * 4)
    return pl.pallas_call(
        body, name=name, grid_spec=grid_spec, out_shape=[SDS((rows, cols), F32)] * 4,
        compiler_params=_params(("arbitrary",), 40),
    )(slots, *([parts] * n_slots), w, m, v)


PACKED = ("gmlp_ln_g", "gmlp_ln_b", "gmlp_ws", "gmlp_bs", "conv_b", "w_a", "b_a", "w_x", "b_x", "lam", "gmlp_out_g",
          "lru_out_g", "post_g")
WEIGHTS = ("pre_g", "w_in", "gmlp_ln_g", "gmlp_ln_b", "gmlp_ws", "gmlp_bs", "conv_w", "conv_b", "w_a", "b_a", "w_x",
           "b_x", "lam", "gmlp_out_g", "lru_out_g", "w_out", "post_g", "w_pe", "w_pg")
LANES = 128


PACK_ROWS = 3200


def _pack(parts):
    rows = [p.reshape(-1, LANES) for p in parts]
    used = sum(r.shape[0] for r in rows)
    return jnp.concatenate(rows + [jnp.zeros((PACK_ROWS - used, LANES), F32)], axis=0)


def _pad_rows(a, rows):
    return jnp.concatenate([a, jnp.zeros((rows - a.shape[0],) + a.shape[1:], a.dtype)], axis=0)


def kernel(x, p, pre_g, w_in, gmlp_ln_g, gmlp_ln_b, gmlp_ws, gmlp_bs, conv_w, conv_b, w_a, b_a, w_x, b_x, lam, gmlp_out_g, lru_out_g, w_out, post_g, w_pe, w_pg, loss_target, m_pre_g, m_w_in, m_gmlp_ln_g, m_gmlp_ln_b, m_gmlp_ws, m_gmlp_bs, m_conv_w, m_conv_b, m_w_a, m_b_a, m_w_x, m_b_x, m_lam, m_gmlp_out_g, m_lru_out_g, m_w_out, m_post_g, m_w_pe, m_w_pg, v_pre_g, v_w_in, v_gmlp_ln_g, v_gmlp_ln_b, v_gmlp_ws, v_gmlp_bs, v_conv_w, v_conv_b, v_w_a, v_b_a, v_w_x, v_b_x, v_lam, v_gmlp_out_g, v_lru_out_g, v_w_out, v_post_g, v_w_pe, v_w_pg):
    args = dict(locals())
    weights = {n: args[n] for n in WEIGHTS}
    m_in = {n: args["m_" + n] for n in WEIGHTS}
    v_in = {n: args["v_" + n] for n in WEIGHTS}
    sm = {n: weights[n][0] for n in PACKED}
    shard_rows = D_MODEL // N_DEV
    xs, ps, tgt = x[0], p[0, 0], loss_target[0]

    vec = lambda a: a.reshape(1, -1)
    tril = jnp.tril(jnp.ones((CHUNK, CHUNK), dtype=bool))
    wm32 = jnp.where(tril[None], sm["gmlp_ws"], 0.0)
    wm, wm_t = wm32.astype(BF16), jnp.swapaxes(wm32, 1, 2).astype(BF16)
    bias = jnp.repeat(sm["gmlp_bs"].T, HEAD, axis=1)
    wax32 = jnp.concatenate([sm["w_a"], sm["w_x"]], axis=2)
    wax, wax_t = wax32.astype(BF16), jnp.swapaxes(wax32, 1, 2).astype(BF16)
    ln_g, ln_b = vec(sm["gmlp_ln_g"]), vec(sm["gmlp_ln_b"])
    post_g_v = vec(sm["post_g"])

    hn = _pre_norm(xs, pre_g)
    cw_shard = _pad_rows(conv_w.reshape(CONV_W, HEAD), ROWS)
    z, w_in_g, (cw_g,) = _in_proj(hn, w_in[0].astype(BF16), [cw_shard])
    cw_full = jnp.transpose(cw_g[:, :CONV_W, :], (1, 0, 2)).reshape(CONV_W, D_BR)
    mixer_consts = dict(cw=_pad_rows(cw_full, ROWS), cb=vec(sm["conv_b"]), ba=vec(sm["b_a"]), bx=vec(sm["b_x"]),
                        lam=vec(sm["lam"]), goa=vec(sm["gmlp_out_g"]), gob=vec(sm["lru_out_g"]))
    (y, h, vhb, xcb, v_rs), (w_out_g, w_pe_g, w_pg_g) = _mix_fwd(
        z, ln_g, ln_b, wm, bias, wax=wax, **mixer_consts,
        ex_arrs=[w_out[0].astype(BF16), w_pe[0].astype(BF16), w_pg[0].astype(BF16)], ex_scatter=[False, False, False])
    w_out_f, w_pg_f = w_out_g.reshape(D_MODEL, D_MODEL), w_pg_g.reshape(D_MODEL, D_MODEL)
    h1, ob = _out_proj(y, xs, w_out_f, post_g_v)
    dh2, dgl, h1b, loss_part, d_w_pe = _ple_loss(h1, ps, tgt, w_pg_f, w_pe_g)

    dh1, do, dy, d_post_g = _tail_bwd(dh2, dgl, ob, w_pg_f, w_out_f, post_g_v)
    d_w_out, _ = _grad_w(y, do, 1024, False, "grad_w_out")
    d_w_pg, _ = _grad_w(h1b, dgl, 1024, False, "grad_w_pg")
    (dz, vecs, d_ws, d_wax, d_bs), (parts_out, parts_pg, parts_pe) = _mix_bwd(
        z, dy, h, vhb, xcb, v_rs, ln_g, ln_b, wm, wm_t, bias, wax=wax, wax_t=wax_t, **mixer_consts,
        ex_arrs=[d_w_out.reshape(N_DEV, shard_rows, D_MODEL), d_w_pg.reshape(N_DEV, shard_rows, D_MODEL), d_w_pe],
        ex_scatter=[True, True, True])

    small = {"gmlp_ln_g": vecs[V_LN_G], "gmlp_ln_b": vecs[V_LN_B], "gmlp_ws": d_ws, "gmlp_bs": d_bs,
             "conv_b": vecs[V_CONV_B], "w_a": d_wax[:, :, :HEAD], "b_a": vecs[V_B_A], "w_x": d_wax[:, :, HEAD:],
             "b_x": vecs[V_B_X], "lam": vecs[V_LAM], "gmlp_out_g": vecs[V_GOUT_A], "lru_out_g": vecs[V_GOUT_B],
             "post_g": d_post_g}
    small_part = _pack([small[n] for n in PACKED] + [loss_part]).reshape(N_DEV, PACK_ROWS // N_DEV, LANES)
    d_cw_blocks = jnp.transpose(vecs[V_CONV_W:V_CONV_W + CONV_W].reshape(CONV_W, N_DEV, HEAD), (1, 0, 2))
    d_cw_blocks = jnp.concatenate([d_cw_blocks, jnp.zeros((N_DEV, ROWS - CONV_W, HEAD), F32)], axis=1)
    parts_in, slots_in, (small_blocks, parts_cw) = _grad_w_in_pairs(
        hn, dz, ex_arrs=[small_part, d_cw_blocks], ex_scatter=[True, True])
    small_sum = _sum_parts(small_blocks, "sum_small")
    grad_x, d_pre_g = _in_bwd(dz, w_in_g, xs, dh1, pre_g)
    pre_rows = D_MODEL // LANES
    small_all, parts_pre = _exchange([small_sum, d_pre_g.reshape(pre_rows, LANES)], False, "gather_small_grads")

    pad_cw = lambda a: _pad_rows(a.reshape(CONV_W, HEAD), ROWS)
    flat = lambda a: a.reshape(pre_rows, LANES)
    outs = {
        "w_in": _adamw_slots(parts_in, slots_in, w_in[0], m_w_in[0], v_w_in[0], "adamw_w_in", 256),
        "w_out": _adamw(parts_out, w_out[0], m_w_out[0], v_w_out[0], "adamw_w_out", 128),
        "w_pe": _adamw(parts_pe, w_pe[0], m_w_pe[0], v_w_pe[0], "adamw_w_pe", 256),
        "w_pg": _adamw(parts_pg, w_pg[0], m_w_pg[0], v_w_pg[0], "adamw_w_pg", 128),
        "conv_w": [a[:CONV_W] for a in
                   _adamw(parts_cw, pad_cw(conv_w), pad_cw(m_conv_w), pad_cw(v_conv_w), "adamw_conv_w", ROWS)],
        "pre_g": _adamw(parts_pre, flat(pre_g), flat(m_pre_g), flat(v_pre_g), "adamw_pre_g", pre_rows),
    }
    as_rows = lambda a: a.reshape(-1, LANES)
    small_res = _adamw_unpacked(small_all.reshape(PACK_ROWS, LANES),
                                [(as_rows(weights[n]), as_rows(m_in[n]), as_rows(v_in[n])) for n in PACKED], "adamw_small")
    for i, n in enumerate(PACKED):
        outs[n] = small_res[4 * i:4 * i + 4]
    loss = small_res[-1][0, 0]

    result = [loss, grad_x[None]]
    for q in range(4):
        result += [outs[n][q].reshape(weights[n].shape) for n in WEIGHTS]
    return tuple(result)
```

```python
import jax
import jax.numpy as jnp
from jax import lax
from jax.experimental import pallas as pl
from jax.experimental.pallas import tpu as pltpu

F32 = jnp.float32
BF16 = jnp.bfloat16
SDS = jax.ShapeDtypeStruct

D_MODEL = 2048
D_BR = 1024
D_IN = 5 * D_BR
D_PLE = 256
N_HEAD = 8
HEAD = 128
CHUNK = 128
ROWS = 8
N_GROUP = CHUNK // ROWS
MIX_SUB = 2
N_DEV = 8
W_IN_SHARD = D_IN // N_DEV
EPS = 1e-6
LRU_C = 8.0
CONV_W = 4
MIB = 1 << 20

ADAM_LR, ADAM_B1, ADAM_B2, ADAM_EPS, ADAM_WD, ADAM_STEP = 0.001, 0.9, 0.999, 1e-08, 0.01, 10

_GELU_C = 0.7978845608028654
_GELU_A = 0.044715

V_LN_G, V_LN_B, V_CONV_B, V_B_A, V_B_X, V_LAM, V_GOUT_A, V_GOUT_B, V_CONV_W = 0, 1, 2, 3, 4, 5, 6, 7, 8
N_VEC = 16


def _params(sem, vmem_mib):
    return pltpu.CompilerParams(dimension_semantics=sem, vmem_limit_bytes=int(vmem_mib * MIB))


def _sig(x):
    return 0.5 * jnp.tanh(0.5 * x) + 0.5


def _gelu(x, with_grad=False):
    sq = x * x
    t = jnp.tanh(x * (_GELU_C + (_GELU_C * _GELU_A) * sq))
    half, one_t = 0.5 * x, 1.0 + t
    if not with_grad:
        return half * one_t
    grad = 0.5 * one_t + half * ((1.0 - t) * one_t) * (_GELU_C + (3.0 * _GELU_C * _GELU_A) * sq)
    return half * one_t, grad


def _silu_grad(s, xs):
    return s + xs * (1.0 - s)


def _neg_expm1(y, exp_y):
    series = -y * (1.0 + y * (0.5 + y * (1.0 / 6.0)))
    return jnp.where(y > -0.01, series, 1.0 - exp_y)


def _softplus(x):
    return jnp.maximum(x, 0.0) + jnp.log(1.0 + jnp.exp(-jnp.abs(x)))


def _row_ids(width):
    return lax.broadcasted_iota(jnp.int32, (ROWS, width), 0)


def _shift_down(cur, prev, k, rid):
    return jnp.where(rid >= k, pltpu.roll(cur, k, 0), pltpu.roll(prev, k, 0))


def _shift_up(cur, nxt, k, rid):
    return jnp.where(rid < ROWS - k, pltpu.roll(cur, ROWS - k, 0), pltpu.roll(nxt, ROWS - k, 0))


def _mean_last(x):
    return jnp.mean(x, axis=-1, keepdims=True)


def _rows(g):
    return pl.ds(pl.multiple_of(g * ROWS, ROWS), ROWS)


TILE_ROWS = 16


def _tile_rows(q):
    return pl.ds(pl.multiple_of(q * TILE_ROWS, TILE_ROWS), TILE_ROWS)


UNROLL = 4
TILE_UNROLL = 8


def _loop(n, body, init, unroll=UNROLL):
    def wide(i, carry):
        for u in range(unroll):
            carry = body(i * unroll + u, carry)
        return carry

    return lax.fori_loop(0, n // unroll, wide, init)


def _fold_rows(x):
    return x[0:ROWS, :] + x[ROWS:TILE_ROWS, :]


def _bcast_row(x, r):
    return jnp.broadcast_to(x[r:r + 1, :], x.shape)


def _dot(a, b):
    return jnp.dot(a, b, preferred_element_type=F32)


def _dot_nt(a, b):
    return lax.dot_general(a, b, (((1,), (1,)), ((), ())), preferred_element_type=F32)


def _dot_tn(a, b):
    return lax.dot_general(a, b, (((0,), (0,)), ((), ())), preferred_element_type=F32)


def _mesh_place():
    x, y, c = lax.axis_index("x"), lax.axis_index("y"), lax.axis_index("c")
    return x, y, c, 4 * x + 2 * y + c


def _peer(x, y, c, k):
    px = 1 - x if k & 4 else x
    py = 1 - y if k & 2 else y
    pc = 1 - c if k & 1 else c
    return (px, py, pc), 4 * px + 2 * py + pc


def _remote(src, dst, send_sem, recv_sem, dev):
    return pltpu.make_async_remote_copy(src_ref=src, dst_ref=dst, send_sem=send_sem, recv_sem=recv_sem, device_id=dev,
                                        device_id_type=pl.DeviceIdType.MESH)


ANY_SPEC = pl.BlockSpec(memory_space=pl.ANY)


class _Exchange:
    def __init__(self, arrs, scatter):
        self.n = len(arrs)
        self.scatter = tuple(scatter)
        self.out_shape = [SDS(a.shape if s else (N_DEV,) + a.shape, a.dtype) for a, s in zip(arrs, scatter)]
        self.scratch = [pltpu.SemaphoreType.DMA((self.n * N_DEV,)), pltpu.SemaphoreType.DMA((self.n * N_DEV,)),
                        pltpu.SemaphoreType.DMA((self.n,))]

    def _copies(self, ins, outs, sems):
        send_sems, recv_sems, local_sems = sems
        x, y, c, me = _mesh_place()
        local, sends, recvs = [], [], []
        for a in range(self.n):
            src = ins[a].at[me] if self.scatter[a] else ins[a]
            local.append(pltpu.make_async_copy(src, outs[a].at[me], local_sems.at[a]))
        for k in range(1, N_DEV):
            dev, lin = _peer(x, y, c, k)
            for a in range(self.n):
                src = ins[a].at[lin] if self.scatter[a] else ins[a]
                pair = (send_sems.at[a * N_DEV + k], recv_sems.at[a * N_DEV + k], dev)
                sends.append(_remote(src, outs[a].at[me], *pair))
                recvs.append(_remote(src, outs[a].at[lin], *pair))
        return local, sends, recvs

    def start(self, ins, outs, sems):
        local, sends, _ = self._copies(ins, outs, sems)
        for cp in local + sends:
            cp.start()

    def wait(self, ins, outs, sems):
        local, sends, recvs = self._copies(ins, outs, sems)
        for cp in recvs:
            cp.wait_recv()
        for cp in sends:
            cp.wait_send()
        for cp in local:
            cp.wait()


def _exchange(arrs, scatter, name):
    ex = _Exchange(arrs, [scatter] * len(arrs))
    n = ex.n

    def body(*refs):
        ins, outs, sems = refs[:n], refs[n:2 * n], refs[2 * n:]
        ex.start(ins, outs, sems)
        ex.wait(ins, outs, sems)

    return pl.pallas_call(
        body, name=name, out_shape=ex.out_shape, in_specs=[ANY_SPEC] * n, out_specs=[ANY_SPEC] * n,
        scratch_shapes=ex.scratch,
    )(*arrs)


def _pre_norm(x, pre_g, tm=512):
    t_len = x.shape[0]

    def body(x_ref, g_ref, hn_ref):
        g = g_ref[...]

        def rows_body(q, _):
            rows = _tile_rows(q)
            xv = x_ref[rows, :]
            hn_ref[rows, :] = (xv * lax.rsqrt(_mean_last(xv * xv) + EPS) * g).astype(BF16)
            return 0

        _loop(tm // TILE_ROWS, rows_body, 0, unroll=TILE_UNROLL)

    tile = pl.BlockSpec((tm, D_MODEL), lambda i: (i, 0))
    return pl.pallas_call(
        body, name="pre_norm", grid=(t_len // tm,),
        in_specs=[tile, pl.BlockSpec((1, D_MODEL), lambda i: (0, 0))], out_specs=tile,
        out_shape=SDS((t_len, D_MODEL), BF16),
        compiler_params=_params(("arbitrary",), 24),
    )(x, pre_g)


CHIP_ORDER = (0, 2, 4, 6)
W_BODY, W_TAIL = 512, 128
SIBLING = 1
ICI_MASKS = (2, 4, 6)
DIRECT_MASKS = (SIBLING,) + ICI_MASKS
Y_NEIGHBOUR, X_NEIGHBOUR, DIAGONAL = 2, 4, 6
W_DIRECT = (SIBLING, Y_NEIGHBOUR, X_NEIGHBOUR)


def _in_proj(hn, w_shard, others, tm=1024):
    t_len = hn.shape[0]
    n_i = t_len // tm
    n_o = len(others)
    me_out = 4 * lax.axis_index("x") + 2 * lax.axis_index("y") + lax.axis_index("c")
    order = jnp.stack([(me_out ^ chip) // 2 for chip in CHIP_ORDER]).astype(jnp.int32)

    def body(order_ref, hn_ref, w_hbm, *refs):
        o_in = refs[:n_o]
        z_ref, wg_hbm = refs[n_o], refs[n_o + 1]
        o_out = refs[n_o + 2:2 * n_o + 2]
        (wbuf, tail_s, send_w, recv_w, fsend_w, frecv_w, send_o, recv_o, fsend_o, frecv_o, wb_sems, loc_sems, rsend,
         rrecv) = refs[2 * n_o + 2:]
        j, i = pl.program_id(0), pl.program_id(1)
        x, y, c, me = _mesh_place()
        sib = _peer(x, y, c, SIBLING)[0]

        def relay(core):
            src, dst = (Y_NEIGHBOUR, X_NEIGHBOUR) if core == 0 else (X_NEIGHBOUR, Y_NEIGHBOUR)
            held, diag = _peer(x, y, c, src)[1], _peer(x, y, c, DIAGONAL)[1]
            pair = (rsend.at[0], rrecv.at[0], _peer(x, y, c, dst)[0])
            return _remote(wbuf.at[held], wbuf.at[held], *pair), _remote(wbuf.at[diag], wbuf.at[diag], *pair)

        def direct(k, a=None):
            dev, lin = _peer(x, y, c, k)
            if a is None:
                return (_remote(w_hbm, wbuf.at[me], send_w.at[k], recv_w.at[k], dev),
                        _remote(w_hbm, wbuf.at[lin], send_w.at[k], recv_w.at[k], dev))
            pair = (send_o.at[a * N_DEV + k], recv_o.at[a * N_DEV + k], dev)
            return _remote(o_in[a], o_out[a].at[me], *pair), _remote(o_in[a], o_out[a].at[lin], *pair)

        def passed(k, a=None):
            mine, theirs = _peer(x, y, c, k)[1], _peer(x, y, c, k ^ SIBLING)[1]
            if a is None:
                pair = (fsend_w.at[k], frecv_w.at[k], sib)
                return _remote(wbuf.at[mine], wbuf.at[mine], *pair), _remote(wbuf.at[theirs], wbuf.at[theirs], *pair)
            pair = (fsend_o.at[a * N_DEV + k], frecv_o.at[a * N_DEV + k], sib)
            return (_remote(o_out[a].at[mine], o_out[a].at[mine], *pair),
                    _remote(o_out[a].at[theirs], o_out[a].at[theirs], *pair))

        def own_copies():
            return [pltpu.make_async_copy(o_in[a], o_out[a].at[me], loc_sems.at[1 + a]) for a in range(n_o)]

        @pl.when(jnp.logical_and(j == 0, i == 0))
        def _():
            own = pltpu.make_async_copy(w_hbm, wbuf.at[me], loc_sems.at[0])
            own.start()
            for cp in own_copies():
                cp.start()
            for k in W_DIRECT:
                direct(k)[0].start()
            for k in DIRECT_MASKS:
                for a in range(n_o):
                    direct(k, a)[0].start()
            own.wait()

        low = 2 * order_ref[j]

        for jp, chip in enumerate(CHIP_ORDER):
            @pl.when(jnp.logical_and(j == jp, i == 0))
            def _(jp=jp, chip=chip):
                if chip == 0:
                    direct(SIBLING)[1].wait_recv()
                elif chip == Y_NEIGHBOUR:
                    for mask in (Y_NEIGHBOUR, X_NEIGHBOUR):
                        direct(mask)[1].wait_recv()
                        passed(mask)[0].start()
                    for core in (0, 1):
                        @pl.when(c == core)
                        def _(core=core):
                            relay(core)[0].start()
                    passed(Y_NEIGHBOUR)[1].wait_recv()
                elif chip == X_NEIGHBOUR:
                    passed(X_NEIGHBOUR)[1].wait_recv()
                    for core in (0, 1):
                        @pl.when(c == core)
                        def _(core=core):
                            relay(core)[1].wait_recv()
                    passed(DIAGONAL)[0].start()
                    for k in ICI_MASKS:
                        for a in range(n_o):
                            direct(k, a)[1].wait_recv()
                            passed(k, a)[0].start()
                else:
                    passed(DIAGONAL)[1].wait_recv()
                for half in (0, 1):
                    pltpu.make_async_copy(wbuf.at[low + half], wg_hbm.at[low + half], wb_sems.at[2 * jp + half]).start()
                tail_s[:, 0:W_TAIL] = wbuf[low, :, W_BODY:W_IN_SHARD]
                tail_s[:, W_TAIL:2 * W_TAIL] = wbuf[low + 1, :, W_BODY:W_IN_SHARD]

        hn = hn_ref[...]
        z_ref[:, 0:W_BODY] = _dot(hn, wbuf[low, :, 0:W_BODY])
        z_ref[:, W_IN_SHARD:W_IN_SHARD + W_BODY] = _dot(hn, wbuf[low + 1, :, 0:W_BODY])
        tails = _dot(hn, tail_s[...])
        z_ref[:, W_BODY:W_IN_SHARD] = tails[:, 0:W_TAIL]
        z_ref[:, W_IN_SHARD + W_BODY:2 * W_IN_SHARD] = tails[:, W_TAIL:2 * W_TAIL]

        @pl.when(jnp.logical_and(j == len(CHIP_ORDER) - 1, i == n_i - 1))
        def _():
            for a in range(n_o):
                direct(SIBLING, a)[1].wait_recv()
            for k in ICI_MASKS:
                for a in range(n_o):
                    passed(k, a)[1].wait_recv()
            for k in W_DIRECT:
                direct(k)[0].wait_send()
            for core in (0, 1):
                @pl.when(c == core)
                def _(core=core):
                    relay(core)[0].wait_send()
            for k in DIRECT_MASKS:
                for a in range(n_o):
                    direct(k, a)[0].wait_send()
            for k in ICI_MASKS:
                passed(k)[0].wait_send()
                for a in range(n_o):
                    passed(k, a)[0].wait_send()
            for cp in own_copies():
                cp.wait()
            for jj in range(N_DEV):
                pltpu.make_async_copy(wbuf.at[0], wg_hbm.at[0], wb_sems.at[jj]).wait()

    dma = lambda n: pltpu.SemaphoreType.DMA((n,))
    grid_spec = pltpu.PrefetchScalarGridSpec(
        num_scalar_prefetch=1, grid=(len(CHIP_ORDER), n_i),
        in_specs=[pl.BlockSpec((tm, D_MODEL), lambda j, i, order: (i, 0)), ANY_SPEC] + [ANY_SPEC] * n_o,
        out_specs=[pl.BlockSpec((tm, 2 * W_IN_SHARD), lambda j, i, order: (i, order[j])), ANY_SPEC] + [ANY_SPEC] * n_o,
        scratch_shapes=[pltpu.VMEM((N_DEV, D_MODEL, W_IN_SHARD), BF16), pltpu.VMEM((D_MODEL, 2 * W_TAIL), BF16),
                        dma(N_DEV), dma(N_DEV), dma(N_DEV), dma(N_DEV),
                        dma(n_o * N_DEV), dma(n_o * N_DEV), dma(n_o * N_DEV), dma(n_o * N_DEV), dma(N_DEV), dma(1 + n_o),
                        dma(1), dma(1)])
    res = pl.pallas_call(
        body, name="in_proj", grid_spec=grid_spec,
        out_shape=[SDS((t_len, D_IN), F32), SDS((N_DEV, D_MODEL, W_IN_SHARD), BF16)]
        + [SDS((N_DEV,) + o.shape, o.dtype) for o in others],
        compiler_params=_params(("arbitrary", "arbitrary"), 54),
    )(order, hn, w_shard, *others)
    return res[0], res[1], res[2:]


def _conv_rows(cur, prev, cw_ref, cb, rid):
    acc = cw_ref[3:4, :] * cur + cb
    for k in range(1, CONV_W):
        acc = acc + cw_ref[3 - k:4 - k, :] * _shift_down(cur, prev, k, rid)
    return acc


ROW0_LOG_A = -1e30


def _row0_mask(rid):
    return jnp.where(rid == 0, ROW0_LOG_A, 0.0)


def _row0_bias(is_first_group, row0_mask):
    return is_first_group.astype(F32) * row0_mask


def _lru_gates(pa, px, ba, bx, sp8, row0_bias):
    r = _sig(pa + ba)
    i = _sig(px + bx)
    la = row0_bias - r * sp8
    a = jnp.exp(la)
    return r, i, a, _neg_expm1(2.0 * la, a * a)


def _mix_fwd(z, ln_g, ln_b, wm, bias, cw, cb, wax, ba, bx, lam, goa, gob, ex_arrs, ex_scatter):
    t_len = z.shape[0]
    n_chunk = t_len // CHUNK
    ex = _Exchange(ex_arrs, ex_scatter)
    n_in, n_out, n_scratch = 13, 5, 7

    def body(*refs):
        (z_ref, lng_ref, lnb_ref, wm_ref, bias_ref, cw_ref, cb_ref, wax_ref, ba_ref, bx_ref, lam_ref, goa_ref,
         gob_ref) = refs[:n_in]
        ex_in = refs[n_in:n_in + ex.n]
        y_ref, h_ref, vhb_ref, xcb_ref, rs_ref = refs[n_in + ex.n:n_in + ex.n + n_out]
        ex_out = refs[n_in + ex.n + n_out:n_in + 2 * ex.n + n_out]
        vn_s, xc_s, mixed_s, pre_s, y_s, carry_s, halo_s = refs[n_in + 2 * ex.n + n_out:n_in + 2 * ex.n + n_out + n_scratch]
        ex_sems = refs[n_in + 2 * ex.n + n_out + n_scratch:]
        step = pl.program_id(0)
        rid = _row_ids(D_BR)

        @pl.when(step == 0)
        def _():
            ex.start(ex_in, ex_out, ex_sems)
            carry_s[...] = jnp.zeros_like(carry_s)
            halo_s[...] = jnp.zeros_like(halo_s)

        lng, lnb, cb = lng_ref[...], lnb_ref[...], cb_ref[...]
        ba, bx, goa, gob = ba_ref[...], bx_ref[...], goa_ref[...], gob_ref[...]
        sp8 = LRU_C * _softplus(-lam_ref[...])
        row0 = _row0_mask(rid)

        def chunk(c_id, z_ref, y_ref, h_ref, vhb_ref, xcb_ref, rs_ref):
            def phase1(g, prev):
                rows = _rows(g)
                vg = _gelu(z_ref[rows, D_BR:2 * D_BR])
                xm = vg - _mean_last(vg)
                rs = lax.rsqrt(_mean_last(xm * xm) + EPS)
                vn_s[rows, :] = xm * rs
                rs_ref[rows, :] = jnp.broadcast_to(rs, (ROWS, HEAD))
                xb = z_ref[rows, 3 * D_BR:4 * D_BR]
                xc_s[rows, :] = _conv_rows(xb, prev, cw_ref, cb, rid)
                return xb

            halo_s[...] = _loop(N_GROUP, phase1, halo_s[...], unroll=8)
            vhb_ref[...] = vn_s[...].astype(BF16)
            xcb_ref[...] = xc_s[...].astype(BF16)

            for h in range(N_HEAD):
                cs = slice(h * HEAD, (h + 1) * HEAD)
                mixed_s[:, cs] = _dot(wm_ref[h], (vn_s[:, cs] * lng[:, cs] + lnb[:, cs]).astype(BF16))
                pre = _dot(xcb_ref[:, cs], wax_ref[h])
                pre_s[:, cs] = pre[:, :HEAD]
                pre_s[:, D_BR + h * HEAD:D_BR + (h + 1) * HEAD] = pre[:, HEAD:]

            def phase3(g, carry):
                rows = _rows(g)
                ug = _gelu(z_ref[rows, 0:D_BR])
                ga = z_ref[rows, 2 * D_BR:3 * D_BR]
                ya = ug * (mixed_s[rows, :] + bias_ref[rows, :]) * (ga * _sig(ga))
                y_s[rows, 0:D_BR] = ya * lax.rsqrt(_mean_last(ya * ya) + EPS) * goa

                bias0 = _row0_bias(jnp.logical_and(c_id == 0, g == 0), row0)
                _, i, a, m2 = _lru_gates(pre_s[rows, 0:D_BR], pre_s[rows, D_BR:2 * D_BR], ba, bx, sp8, bias0)
                b = jnp.sqrt(m2) * i * xc_s[rows, :]
                for d in (1, 2, 4):
                    a_sh = jnp.where(rid >= d, pltpu.roll(a, d, 0), 1.0)
                    b_sh = jnp.where(rid >= d, pltpu.roll(b, d, 0), 0.0)
                    b = a * b_sh + b
                    a = a * a_sh
                hh = b + a * carry
                h_ref[rows, :] = hh
                gb = z_ref[rows, 4 * D_BR:5 * D_BR]
                yb = hh * (gb * _sig(gb))
                y_s[rows, D_BR:2 * D_BR] = yb * lax.rsqrt(_mean_last(yb * yb) + EPS) * gob
                return _bcast_row(hh, ROWS - 1)

            carry_s[...] = _loop(N_GROUP, phase3, carry_s[...])
            y_ref[...] = y_s[...].astype(BF16)

        for sub in range(MIX_SUB):
            part = lambda ref, sub=sub: ref.at[pl.ds(sub * CHUNK, CHUNK)]
            chunk(step * MIX_SUB + sub, part(z_ref), part(y_ref), part(h_ref), part(vhb_ref), part(xcb_ref),
                  part(rs_ref))

        @pl.when(step == n_chunk // MIX_SUB - 1)
        def _():
            ex.wait(ex_in, ex_out, ex_sems)

    vec = pl.BlockSpec((1, D_BR), lambda i: (0, 0))
    blk = MIX_SUB * CHUNK
    res = pl.pallas_call(
        body, name="mix_fwd", grid=(n_chunk // MIX_SUB,),
        in_specs=[pl.BlockSpec((blk, D_IN), lambda i: (i, 0)), vec, vec,
                  pl.BlockSpec((N_HEAD, HEAD, HEAD), lambda i: (0, 0, 0)),
                  pl.BlockSpec((CHUNK, D_BR), lambda i: (0, 0)),
                  pl.BlockSpec((ROWS, D_BR), lambda i: (0, 0)), vec,
                  pl.BlockSpec((N_HEAD, HEAD, 2 * HEAD), lambda i: (0, 0, 0)), vec, vec, vec, vec, vec]
        + [ANY_SPEC] * ex.n,
        out_specs=[pl.BlockSpec((blk, 2 * D_BR), lambda i: (i, 0)), pl.BlockSpec((blk, D_BR), lambda i: (i, 0)),
                   pl.BlockSpec((blk, D_BR), lambda i: (i, 0)), pl.BlockSpec((blk, D_BR), lambda i: (i, 0)),
                   pl.BlockSpec((blk, HEAD), lambda i: (i, 0))] + [ANY_SPEC] * ex.n,
        out_shape=[SDS((t_len, 2 * D_BR), BF16), SDS((t_len, D_BR), F32), SDS((t_len, D_BR), BF16),
                   SDS((t_len, D_BR), BF16), SDS((t_len, HEAD), F32)] + ex.out_shape,
        scratch_shapes=[pltpu.VMEM((CHUNK, D_BR), F32), pltpu.VMEM((CHUNK, D_BR), F32), pltpu.VMEM((CHUNK, D_BR), F32),
                        pltpu.VMEM((CHUNK, 2 * D_BR), F32), pltpu.VMEM((CHUNK, 2 * D_BR), F32),
                        pltpu.VMEM((ROWS, D_BR), F32), pltpu.VMEM((ROWS, D_BR), F32)] + ex.scratch,
        compiler_params=_params(("arbitrary",), 32),
    )(z, ln_g, ln_b, wm, bias, cw, cb, wax, ba, bx, lam, goa, gob, *ex_arrs)
    return res[:n_out], res[n_out:]


def _load_weight(w_hbm, w_vmem, sem):
    @pl.when(pl.program_id(0) == 0)
    def _():
        cp = pltpu.make_async_copy(w_hbm, w_vmem, sem)
        cp.start()
        cp.wait()


def _out_proj(y, x, w_out, post_g, tm=512):
    t_len = y.shape[0]

    def body(y_ref, x_ref, w_hbm, g_ref, h1_ref, ob_ref, w_s, o_s, sem):
        _load_weight(w_hbm, w_s, sem)
        o_s[...] = _dot(y_ref[...], w_s[...])
        g = g_ref[...]

        def rows_body(q, _):
            rows = _tile_rows(q)
            o = o_s[rows, :]
            h1_ref[rows, :] = x_ref[rows, :] + o * lax.rsqrt(_mean_last(o * o) + EPS) * g
            ob_ref[rows, :] = o.astype(BF16)
            return 0

        _loop(tm // TILE_ROWS, rows_body, 0, unroll=TILE_UNROLL)

    tile = pl.BlockSpec((tm, D_MODEL), lambda i: (i, 0))
    return pl.pallas_call(
        body, name="out_proj", grid=(t_len // tm,),
        in_specs=[tile, tile, pl.BlockSpec(memory_space=pl.ANY), pl.BlockSpec((1, D_MODEL), lambda i: (0, 0))],
        out_specs=[tile, tile],
        out_shape=[SDS((t_len, D_MODEL), F32), SDS((t_len, D_MODEL), BF16)],
        scratch_shapes=[pltpu.VMEM((D_MODEL, D_MODEL), BF16), pltpu.VMEM((tm, D_MODEL), F32), pltpu.SemaphoreType.DMA],
        compiler_params=_params(("arbitrary",), 44),
    )(y, x, w_out, post_g)


def _ple_loss(h1, p, tgt, w_pg, w_pe_g, tm=256):
    t_len = h1.shape[0]
    n_tile = t_len // tm
    pe_shard = D_MODEL // N_DEV

    def body(h1_ref, p_ref, t_ref, w_hbm, wpe_ref, dh2_ref, dgl_ref, h1b_ref, loss_ref, dwpe_ref, w_s, pe_s, gl_s, acc_s,
             dpe_s, gpe_s, sem):
        _load_weight(w_hbm, w_s, sem)
        i = pl.program_id(0)

        @pl.when(i == 0)
        def _():
            acc_s[...] = jnp.zeros_like(acc_s)
            gpe_s[...] = jnp.zeros_like(gpe_s)

        h1b_ref[...] = h1_ref[...].astype(BF16)
        pb = p_ref[...].astype(BF16)
        for j in range(N_DEV):
            pe_s[:, j * pe_shard:(j + 1) * pe_shard] = _dot(pb, wpe_ref[j])
        gl_s[...] = _dot(h1b_ref[...], w_s[...])

        def rows_body(q, acc):
            rows = _tile_rows(q)
            pe = pe_s[rows, :]
            g = _sig(gl_s[rows, :])
            e = h1_ref[rows, :] + pe * g - t_ref[rows, :]
            dh2 = e * (1.0 / D_MODEL)
            dh2_ref[rows, :] = dh2
            dpe_s[rows, :] = (dh2 * g).astype(BF16)
            dgl_ref[rows, :] = (dh2 * pe * g * (1.0 - g)).astype(BF16)
            return acc + _fold_rows(e * e)

        acc_s[...] = _loop(tm // TILE_ROWS, rows_body, acc_s[...], unroll=TILE_UNROLL)
        gpe_s[...] += _dot_tn(pb, dpe_s[...])

        @pl.when(i == n_tile - 1)
        def _():
            loss_ref[...] = jnp.full(loss_ref.shape, 0.5 / D_MODEL * jnp.sum(acc_s[...]), F32)
            for j in range(N_DEV):
                dwpe_ref[j] = gpe_s[:, j * pe_shard:(j + 1) * pe_shard].astype(BF16)

    tile = pl.BlockSpec((tm, D_MODEL), lambda i: (i, 0))
    pe_blocks = pl.BlockSpec((N_DEV, D_PLE, pe_shard), lambda i: (0, 0, 0))
    return pl.pallas_call(
        body, name="ple_loss", grid=(n_tile,),
        in_specs=[tile, pl.BlockSpec((tm, D_PLE), lambda i: (i, 0)), tile, pl.BlockSpec(memory_space=pl.ANY), pe_blocks],
        out_specs=[tile, tile, tile, pl.BlockSpec((ROWS, HEAD), lambda i: (0, 0)), pe_blocks],
        out_shape=[SDS((t_len, D_MODEL), F32), SDS((t_len, D_MODEL), BF16), SDS((t_len, D_MODEL), BF16),
                   SDS((ROWS, HEAD), F32), SDS((N_DEV, D_PLE, pe_shard), BF16)],
        scratch_shapes=[pltpu.VMEM((D_MODEL, D_MODEL), BF16), pltpu.VMEM((tm, D_MODEL), F32),
                        pltpu.VMEM((tm, D_MODEL), F32), pltpu.VMEM((ROWS, D_MODEL), F32), pltpu.VMEM((tm, D_MODEL), BF16),
                        pltpu.VMEM((D_PLE, D_MODEL), F32), pltpu.SemaphoreType.DMA],
        compiler_params=_params(("arbitrary",), 48),
    )(h1, p, tgt, w_pg, w_pe_g)


def _tail_bwd(dh2, dgl, ob, w_pg, w_out, post_g, tm=256):
    t_len = dh2.shape[0]
    n_tile = t_len // tm

    def body(dh2_ref, dgl_ref, ob_ref, wpg_hbm, wout_hbm, g_ref, dh1_ref, do_ref, dy_ref, dg_ref, wpg_s, wout_s, t_s,
             acc_s, sems):
        i = pl.program_id(0)
        load_wpg = pltpu.make_async_copy(wpg_hbm, wpg_s, sems.at[0])
        load_wout = pltpu.make_async_copy(wout_hbm, wout_s, sems.at[1])

        @pl.when(i == 0)
        def _():
            load_wpg.start()
            load_wout.start()
            acc_s[...] = jnp.zeros_like(acc_s)
            load_wpg.wait()

        t_s[...] = _dot_nt(dgl_ref[...], wpg_s[...])
        g = g_ref[...]

        def rows_body(q, acc):
            rows = _tile_rows(q)
            dh1 = dh2_ref[rows, :] + t_s[rows, :]
            dh1_ref[rows, :] = dh1
            o = ob_ref[rows, :].astype(F32)
            rr = lax.rsqrt(_mean_last(o * o) + EPS)
            on = o * rr
            dog = dh1 * g
            do_ref[rows, :] = (rr * (dog - on * _mean_last(dog * on))).astype(BF16)
            return acc + _fold_rows(dh1 * on)

        acc_s[...] = _loop(tm // TILE_ROWS, rows_body, acc_s[...], unroll=TILE_UNROLL)

        @pl.when(i == 0)
        def _():
            load_wout.wait()

        dy_ref[...] = _dot_nt(do_ref[...], wout_s[...]).astype(BF16)

        @pl.when(i == n_tile - 1)
        def _():
            dg_ref[...] = jnp.sum(acc_s[...], axis=0, keepdims=True)

    tile = pl.BlockSpec((tm, D_MODEL), lambda i: (i, 0))
    vec = pl.BlockSpec((1, D_MODEL), lambda i: (0, 0))
    hbm = pl.BlockSpec(memory_space=pl.ANY)
    return pl.pallas_call(
        body, name="tail_bwd", grid=(n_tile,),
        in_specs=[tile, tile, tile, hbm, hbm, vec],
        out_specs=[tile, tile, tile, vec],
        out_shape=[SDS((t_len, D_MODEL), F32), SDS((t_len, D_MODEL), BF16), SDS((t_len, D_MODEL), BF16),
                   SDS((1, D_MODEL), F32)],
        scratch_shapes=[pltpu.VMEM((D_MODEL, D_MODEL), BF16), pltpu.VMEM((D_MODEL, D_MODEL), BF16),
                        pltpu.VMEM((tm, D_MODEL), F32), pltpu.VMEM((ROWS, D_MODEL), F32), pltpu.SemaphoreType.DMA((2,))],
        compiler_params=_params(("arbitrary",), 48),
    )(dh2, dgl, ob, w_pg, w_out, post_g)


def _mix_bwd(z, dy, h, vhb, xcb, rs, ln_g, ln_b, wm, wm_t, bias, cw, cb, wax, wax_t, ba, bx, lam, goa, gob, ex_arrs,
             ex_scatter):
    t_len = z.shape[0]
    n_chunk = t_len // CHUNK
    halo_blocks = CHUNK // ROWS
    ex = _Exchange(ex_arrs, ex_scatter)
    n_in, n_out, n_scratch = 21, 5, 16

    blocked = (0, 1, 2, 4, 5, 6, n_in + ex.n)

    def body(*refs):
        step = pl.program_id(0)
        for sub in reversed(range(MIX_SUB)):
            views = list(refs)
            for idx in blocked:
                views[idx] = refs[idx].at[pl.ds(sub * CHUNK, CHUNK)]
            h_before = refs[2].at[pl.ds(sub * CHUNK - ROWS, ROWS)] if sub else refs[3]
            chunk((n_chunk // MIX_SUB - 1 - step) * MIX_SUB + sub,
                  step == 0 if sub == MIX_SUB - 1 else None,
                  step == n_chunk // MIX_SUB - 1 if sub == 0 else None, h_before, *views)

    def chunk(c_id, first, last, h_before, *refs):
        (z_ref, dy_ref, h_ref, hhalo_ref, vhb_ref, xcb_ref, rs_ref, lng_ref, lnb_ref, wm_ref, wmt_ref, bias_ref, cw_ref,
         cb_ref, wax_ref, waxt_ref, ba_ref, bx_ref, lam_ref, goa_ref, gob_ref) = refs[:n_in]
        ex_in = refs[n_in:n_in + ex.n]
        dz_ref, vecs_ref, dws_ref, dwax_ref, dbs_ref = refs[n_in + ex.n:n_in + ex.n + n_out]
        ex_out = refs[n_in + ex.n + n_out:n_in + 2 * ex.n + n_out]
        (vnb_s, vh_s, xc_s, mixed_s, pre_s, dmix_s, dvn_s, dho_s, dxc_s, dpre_s, dz_s, acc_s, accdm_s,
         cg_s, ca_s, dxchalo_s) = refs[n_in + 2 * ex.n + n_out:n_in + 2 * ex.n + n_out + n_scratch]
        ex_sems = refs[n_in + 2 * ex.n + n_out + n_scratch:]
        rid = _row_ids(D_BR)
        first_chunk = c_id == 0

        if first is not None:
            @pl.when(first)
            def _():
                ex.start(ex_in, ex_out, ex_sems)
                acc_s[...] = jnp.zeros_like(acc_s)
                accdm_s[...] = jnp.zeros_like(accdm_s)
                cg_s[...] = jnp.zeros_like(cg_s)
                ca_s[...] = jnp.zeros_like(ca_s)
                dxchalo_s[...] = jnp.zeros_like(dxchalo_s)
                dws_ref[...] = jnp.zeros_like(dws_ref)
                dwax_ref[...] = jnp.zeros_like(dwax_ref)

        lng, lnb = lng_ref[...], lnb_ref[...]
        h_halo = jnp.where(first_chunk, 0.0, h_before[...])

        def prev_rows(ref, cols, g, halo):
            before = ref[pl.ds(pl.multiple_of(jnp.maximum(g - 1, 0) * ROWS, ROWS), ROWS), cols]
            return jnp.where(g > 0, before, halo)

        vh_s[...] = vhb_ref[...].astype(F32)
        xc_s[...] = xcb_ref[...].astype(F32)

        for hd in range(N_HEAD):
            cs = slice(hd * HEAD, (hd + 1) * HEAD)
            vnb_s[:, cs] = (vh_s[:, cs] * lng[:, cs] + lnb[:, cs]).astype(BF16)
            mixed_s[:, cs] = _dot(wm_ref[hd], vnb_s[:, cs])
            pre = _dot(xcb_ref[:, cs], wax_ref[hd])
            pre_s[:, cs] = pre[:, :HEAD]
            pre_s[:, D_BR + hd * HEAD:D_BR + (hd + 1) * HEAD] = pre[:, HEAD:]

        goa, gob = goa_ref[...], gob_ref[...]

        def phase3(g, _):
            rows = _rows(g)
            ug, dug = _gelu(z_ref[rows, 0:D_BR], with_grad=True)
            ga = z_ref[rows, 2 * D_BR:3 * D_BR]
            sga = _sig(ga)
            sa = ga * sga
            mixed = mixed_s[rows, :] + bias_ref[rows, :]
            ya0 = ug * mixed
            ya = ya0 * sa
            ra = lax.rsqrt(_mean_last(ya * ya) + EPS)
            dyan = dy_ref[rows, 0:D_BR].astype(F32)
            acc_s[V_GOUT_A] += dyan * ya * ra
            dyg = dyan * goa
            dya = ra * dyg - ya * (ra * ra * ra) * _mean_last(dyg * ya)
            dya0 = dya * sa
            dz_s[rows, 2 * D_BR:3 * D_BR] = dya * ya0 * _silu_grad(sga, sa)
            dmix = dya0 * ug
            dmix_s[rows, :] = dmix
            accdm_s[rows, :] += dmix
            dz_s[rows, 0:D_BR] = dya0 * mixed * dug

            hh = h_ref[rows, :]
            gb = z_ref[rows, 4 * D_BR:5 * D_BR]
            sgb = _sig(gb)
            sb = gb * sgb
            yb = hh * sb
            rb = lax.rsqrt(_mean_last(yb * yb) + EPS)
            dybn = dy_ref[rows, D_BR:2 * D_BR].astype(F32)
            acc_s[V_GOUT_B] += dybn * yb * rb
            dyg = dybn * gob
            dyb = rb * dyg - yb * (rb * rb * rb) * _mean_last(dyg * yb)
            dho_s[rows, :] = dyb * sb
            dz_s[rows, 4 * D_BR:5 * D_BR] = dyb * hh * _silu_grad(sgb, sb)
            return 0

        _loop(N_GROUP, phase3, 0)

        for hd in range(N_HEAD):
            cs = slice(hd * HEAD, (hd + 1) * HEAD)
            dmb = dmix_s[:, cs].astype(BF16)
            dvn_s[:, cs] = _dot(wmt_ref[hd], dmb)
            dws_ref[hd] += _dot_nt(dmb, vnb_s[:, cs])

        def phase5(g, _):
            rows = _rows(g)
            dvn = dvn_s[rows, :]
            vh = vh_s[rows, :]
            acc_s[V_LN_G] += dvn * vh
            acc_s[V_LN_B] += dvn
            dvh = dvn * lng
            rs = rs_ref[rows, 0:1]
            dvg = rs * (dvh - _mean_last(dvh) - vh * _mean_last(dvh * vh))
            dz_s[rows, D_BR:2 * D_BR] = dvg * _gelu(z_ref[rows, D_BR:2 * D_BR], with_grad=True)[1]
            return 0

        _loop(N_GROUP, phase5, 0)

        ba, bx = ba_ref[...], bx_ref[...]
        sp8 = LRU_C * _softplus(-lam_ref[...])
        row0 = _row0_mask(rid)

        def phase6(k, carry):
            cg, ca = carry
            g = N_GROUP - 1 - k
            rows = _rows(g)
            bias0 = _row0_bias(jnp.logical_and(first_chunk, g == 0), row0)
            r, i, a, m2 = _lru_gates(pre_s[rows, 0:D_BR], pre_s[rows, D_BR:2 * D_BR], ba, bx, sp8, bias0)
            a_nx = jnp.where(rid < ROWS - 1, pltpu.roll(a, ROWS - 1, 0), ca)
            aa, bb = a_nx, dho_s[rows, :]
            for d in (1, 2, 4):
                a_sh = jnp.where(rid < ROWS - d, pltpu.roll(aa, ROWS - d, 0), 1.0)
                b_sh = jnp.where(rid < ROWS - d, pltpu.roll(bb, ROWS - d, 0), 0.0)
                bb = aa * b_sh + bb
                aa = aa * a_sh
            gg = bb + aa * cg
            hh = h_ref[rows, :]
            hprev = _shift_down(hh, prev_rows(h_ref, slice(None), g, h_halo), 1, rid)
            xc = xc_s[rows, :]
            gx = gg * xc
            dla = gg * hprev * a - gx * i * (a * a) * lax.rsqrt(m2)
            acc_s[V_LAM] += -(dla * r)
            dpa = -(dla * sp8) * r * (1.0 - r)
            mi = jnp.sqrt(m2) * i
            dpx = gx * mi * (1.0 - i)
            acc_s[V_B_A] += dpa
            acc_s[V_B_X] += dpx
            dpre_s[rows, 0:D_BR] = dpa
            dpre_s[rows, D_BR:2 * D_BR] = dpx
            dxc_s[rows, :] = gg * mi
            return _bcast_row(gg, 0), _bcast_row(a, 0)

        cg, ca = _loop(N_GROUP, phase6, (cg_s[...], ca_s[...]))
        cg_s[...] = cg
        ca_s[...] = ca

        for hd in range(N_HEAD):
            cs = slice(hd * HEAD, (hd + 1) * HEAD)
            dpre = jnp.concatenate([dpre_s[:, cs], dpre_s[:, D_BR + hd * HEAD:D_BR + (hd + 1) * HEAD]], axis=1).astype(BF16)
            dxc_s[:, cs] += _dot(dpre, waxt_ref[hd])
            dwax_ref[hd] += _dot_tn(xcb_ref[:, cs], dpre)

        def phase8(k, nxt):
            g = N_GROUP - 1 - k
            rows = _rows(g)
            dxc = dxc_s[rows, :]
            acc_s[V_CONV_B] += dxc
            xb = z_ref[rows, 3 * D_BR:4 * D_BR]
            dxb = cw_ref[3:4, :] * dxc
            acc_s[V_CONV_W + 3] += dxc * xb
            for j in range(1, CONV_W):
                later = _shift_up(dxc, nxt, j, rid)
                dxb = dxb + cw_ref[3 - j:4 - j, :] * later
                acc_s[V_CONV_W + 3 - j] += later * xb
            dz_s[rows, 3 * D_BR:4 * D_BR] = dxb
            return dxc

        dxchalo_s[...] = _loop(N_GROUP, phase8, dxchalo_s[...])
        dz_ref[...] = dz_s[...].astype(BF16)

        if last is not None:
            @pl.when(last)
            def _():
                for v in range(N_VEC):
                    vecs_ref[v:v + 1, :] = jnp.sum(acc_s[v], axis=0, keepdims=True)
                lam = lam_ref[...]
                vecs_ref[V_LAM:V_LAM + 1, :] = vecs_ref[V_LAM:V_LAM + 1, :] * (-LRU_C * _sig(-lam))
                tril = (lax.broadcasted_iota(jnp.int32, (HEAD, HEAD), 0)
                        >= lax.broadcasted_iota(jnp.int32, (HEAD, HEAD), 1))
                ones = jnp.ones((ROWS, HEAD), BF16)
                for hd in range(N_HEAD):
                    cs = slice(hd * HEAD, (hd + 1) * HEAD)
                    dws_ref[hd] = jnp.where(tril, dws_ref[hd], 0.0)
                    blk = accdm_s[:, cs]
                    hi = blk.astype(BF16)
                    lo = (blk - hi.astype(F32)).astype(BF16)
                    dbs_ref[hd:hd + 1, :] = (_dot_nt(ones, hi) + _dot_nt(ones, lo))[0:1, :]
                ex.wait(ex_in, ex_out, ex_sems)

    vec = pl.BlockSpec((1, D_BR), lambda i: (0, 0))
    n_step = n_chunk // MIX_SUB
    rows_blk = MIX_SUB * CHUNK
    rev = lambda i: (n_step - 1 - i, 0)
    halo = lambda col: (lambda i: (jnp.maximum((n_step - 1 - i) * MIX_SUB * halo_blocks - 1, 0), col))
    full3 = lambda a, b, c: pl.BlockSpec((a, b, c), lambda i: (0, 0, 0))
    big = lambda w: pltpu.VMEM((CHUNK, w), F32)
    res = pl.pallas_call(
        body, name="mix_bwd", grid=(n_step,),
        in_specs=[pl.BlockSpec((rows_blk, D_IN), rev), pl.BlockSpec((rows_blk, 2 * D_BR), rev),
                  pl.BlockSpec((rows_blk, D_BR), rev),
                  pl.BlockSpec((ROWS, D_BR), halo(0)), pl.BlockSpec((rows_blk, D_BR), rev),
                  pl.BlockSpec((rows_blk, D_BR), rev),
                  pl.BlockSpec((rows_blk, HEAD), rev), vec, vec,
                  full3(N_HEAD, HEAD, HEAD), full3(N_HEAD, HEAD, HEAD),
                  pl.BlockSpec((CHUNK, D_BR), lambda i: (0, 0)), pl.BlockSpec((ROWS, D_BR), lambda i: (0, 0)), vec,
                  full3(N_HEAD, HEAD, 2 * HEAD), full3(N_HEAD, 2 * HEAD, HEAD), vec, vec, vec, vec, vec]
        + [ANY_SPEC] * ex.n,
        out_specs=[pl.BlockSpec((rows_blk, D_IN), rev), pl.BlockSpec((N_VEC, D_BR), lambda i: (0, 0)),
                   full3(N_HEAD, HEAD, HEAD), full3(N_HEAD, HEAD, 2 * HEAD),
                   pl.BlockSpec((N_HEAD, HEAD), lambda i: (0, 0))] + [ANY_SPEC] * ex.n,
        out_shape=[SDS((t_len, D_IN), BF16), SDS((N_VEC, D_BR), F32), SDS((N_HEAD, HEAD, HEAD), F32),
                   SDS((N_HEAD, HEAD, 2 * HEAD), F32), SDS((N_HEAD, HEAD), F32)] + ex.out_shape,
        scratch_shapes=[pltpu.VMEM((CHUNK, D_BR), BF16), big(D_BR), big(D_BR), big(D_BR), big(2 * D_BR), big(D_BR),
                        big(D_BR), big(D_BR), big(D_BR), big(2 * D_BR), big(D_IN),
                        pltpu.VMEM((N_VEC, ROWS, D_BR), F32), big(D_BR),
                        pltpu.VMEM((ROWS, D_BR), F32), pltpu.VMEM((ROWS, D_BR), F32), pltpu.VMEM((ROWS, D_BR), F32)]
        + ex.scratch,
        compiler_params=_params(("arbitrary",), 48),
    )(z, dy, h, h, vhb, xcb, rs, ln_g, ln_b, wm, wm_t, bias, cw, cb, wax, wax_t, ba, bx, lam, goa, gob, *ex_arrs)
    return res[:n_out], res[n_out:]


def _in_bwd(dz, w_in_g, x, dh1, pre_g, tm=256):
    t_len = x.shape[0]
    n_tile = t_len // tm

    def body(dz_ref, w_hbm, x_ref, dh1_ref, g_ref, gx_ref, dg_ref, w_s, t_even, t_odd, dg_s, w_sems):
        i = pl.program_id(0)

        @pl.when(i == 0)
        def _():
            loads = [pltpu.make_async_copy(w_hbm.at[s], w_s.at[:, s * W_IN_SHARD:(s + 1) * W_IN_SHARD], w_sems.at[s])
                     for s in range(N_DEV)]
            for cp in loads:
                cp.start()
            dg_s[...] = jnp.zeros_like(dg_s)
            for s, cp in enumerate(loads):
                cp.wait()
                cols = slice(s * W_IN_SHARD, (s + 1) * W_IN_SHARD)
                part = _dot_nt(dz_ref[:, cols], w_s[:, cols])
                t_even[...] = part if s == 0 else t_even[...] + part

        def step(t_new, t_old):
            g = g_ref[...]
            acc = dg_s[...]
            for q in range(tm // TILE_ROWS):
                rows = slice(q * TILE_ROWS, (q + 1) * TILE_ROWS)
                xv = x_ref[rows, :]
                r = lax.rsqrt(_mean_last(xv * xv) + EPS)
                xh = xv * r
                dhn = t_old[rows, :]
                dg = dhn * g
                gx_ref[rows, :] = dh1_ref[rows, :] + r * (dg - xh * _mean_last(dg * xh))
                acc = acc + _fold_rows(dhn * xh)
            dg_s[...] = acc
            t_new[...] = _dot_nt(dz_ref[...], w_s[...])

        @pl.when((i % 2 == 0) & (i > 0))
        def _():
            step(t_even, t_odd)

        @pl.when(i % 2 == 1)
        def _():
            step(t_odd, t_even)

        @pl.when(i == n_tile)
        def _():
            dg_ref[...] = jnp.sum(dg_s[...], axis=0, keepdims=True)

    matmul_tile = lambda i: (jnp.minimum(i, n_tile - 1), 0)
    rows_tile = lambda i: (jnp.maximum(i - 1, 0), 0)
    res = pl.pallas_call(
        body, name="in_bwd", grid=(n_tile + 1,),
        in_specs=[pl.BlockSpec((tm, D_IN), matmul_tile), ANY_SPEC, pl.BlockSpec((tm, D_MODEL), rows_tile),
                  pl.BlockSpec((tm, D_MODEL), rows_tile), pl.BlockSpec((1, D_MODEL), lambda i: (0, 0))],
        out_specs=[pl.BlockSpec((tm, D_MODEL), rows_tile), pl.BlockSpec((1, D_MODEL), lambda i: (0, 0))],
        out_shape=[SDS((t_len, D_MODEL), F32), SDS((1, D_MODEL), F32)],
        scratch_shapes=[pltpu.VMEM((D_MODEL, D_IN), BF16), pltpu.VMEM((tm, D_MODEL), F32), pltpu.VMEM((tm, D_MODEL), F32),
                        pltpu.VMEM((ROWS, D_MODEL), F32), pltpu.SemaphoreType.DMA((N_DEV,))],
        compiler_params=_params(("arbitrary",), 54),
    )(dz, w_in_g, x, dh1, pre_g)
    return res[0], res[1]


def _grad_w(a, b, bn, shard_major, name, tk=1024, ex_arrs=(), ex_scatter=()):
    t_len, m = a.shape
    n = b.shape[1]
    n_j, n_k = n // bn, t_len // tk
    ex = _Exchange(ex_arrs, ex_scatter)

    def body(a_ref, b_ref, *refs):
        ex_in, o_ref, ex_out = refs[:ex.n], refs[ex.n], refs[ex.n + 1:2 * ex.n + 1]
        acc_s, ex_sems = refs[2 * ex.n + 1], refs[2 * ex.n + 2:]
        j, k = pl.program_id(0), pl.program_id(1)
        if ex.n:
            @pl.when(jnp.logical_and(j == 0, k == 0))
            def _():
                ex.start(ex_in, ex_out, ex_sems)

        @pl.when(k == 0)
        def _():
            acc_s[...] = jnp.zeros_like(acc_s)

        acc_s[...] += _dot_tn(a_ref[...], b_ref[...])

        @pl.when(k == n_k - 1)
        def _():
            o_ref[...] = acc_s[...].astype(BF16)

        if ex.n:
            @pl.when(jnp.logical_and(j == n_j - 1, k == n_k - 1))
            def _():
                ex.wait(ex_in, ex_out, ex_sems)

    if shard_major:
        out_spec, out_shape = pl.BlockSpec((None, m, bn), lambda j, k: (j, 0, 0)), SDS((n_j, m, bn), BF16)
    else:
        out_spec, out_shape = pl.BlockSpec((m, bn), lambda j, k: (0, j)), SDS((m, n), BF16)
    res = pl.pallas_call(
        body, name=name, grid=(n_j, n_k),
        in_specs=[pl.BlockSpec((tk, m), lambda j, k: (k, 0)), pl.BlockSpec((tk, bn), lambda j, k: (k, j))]
        + [ANY_SPEC] * ex.n,
        out_specs=[out_spec] + [ANY_SPEC] * ex.n, out_shape=[out_shape] + ex.out_shape,
        scratch_shapes=[pltpu.VMEM((m, bn), F32)] + (ex.scratch if ex.n else []),
        compiler_params=_params(("arbitrary", "arbitrary"), 40),
    )(a, b, *ex_arrs)
    return res[0], res[1:]


RS_CHIPS = (6, 2, 4, 0)
RS_SLOTS = (0, 1, 2, 4, 6)


def _grad_w_in_pairs(hn, dz, ex_arrs, ex_scatter, tk=1024):
    t_len = hn.shape[0]
    n_k = t_len // tk
    n_ph = len(RS_CHIPS)
    ex = _Exchange(ex_arrs, ex_scatter)
    me_out = 4 * lax.axis_index("x") + 2 * lax.axis_index("y") + lax.axis_index("c")
    order = jnp.stack([(me_out ^ chip) // 2 for chip in RS_CHIPS]).astype(jnp.int32)
    slots = jnp.stack([me_out ^ k for k in RS_SLOTS]).astype(jnp.int32)
    shard = W_IN_SHARD

    def body(order_ref, a_ref, b_ref, *refs):
        ex_in, parts_hbm, ex_out = refs[:ex.n], refs[ex.n], refs[ex.n + 1:2 * ex.n + 1]
        (acc_s, tb_s, stage_s, rx_s, d2d_send, d2d_recv, ici_send, ici_recv, sib_sems,
         loc_sem) = refs[2 * ex.n + 1:2 * ex.n + 11]
        ex_sems = refs[2 * ex.n + 11:]
        j, k = pl.program_id(0), pl.program_id(1)
        x, y, c, me = _mesh_place()
        sib = _peer(x, y, c, SIBLING)[0]

        def to_sibling(p):
            return _remote(stage_s.at[0], rx_s.at[p % 2], d2d_send.at[p], d2d_recv.at[p], sib)

        def over_ici(p):
            dev = _peer(x, y, c, RS_CHIPS[p])[0]
            return _remote(stage_s.at[1], parts_hbm.at[me], ici_send.at[p], ici_recv.at[p], dev)

        def own_chip():
            return (_remote(stage_s.at[0], parts_hbm.at[me], sib_sems.at[0], sib_sems.at[1], sib),
                    pltpu.make_async_copy(stage_s.at[1], parts_hbm.at[me], loc_sem.at[0]))

        @pl.when(jnp.logical_and(j == 0, k == 0))
        def _():
            ex.start(ex_in, ex_out, ex_sems)

        for p in range(n_ph - 1):
            for core in (0, 1):
                @pl.when(jnp.logical_and(jnp.logical_and(j == p + 1, k == 0), c == core))
                def _(p=p, core=core):
                    to_sibling(p).wait_recv()
                    if p >= 1:
                        over_ici(p - 1).wait_send()
                    mine = acc_s[:, core * shard:(core + 1) * shard]
                    stage_s[1] = (mine + rx_s[p % 2].astype(F32)).astype(BF16)
                    over_ici(p).start()

        @pl.when(k == 0)
        def _():
            acc_s[...] = jnp.zeros_like(acc_s)

        a = a_ref[...]
        acc_s[:, 0:W_BODY] += _dot_tn(a, b_ref[:, 0:W_BODY])
        acc_s[:, shard:shard + W_BODY] += _dot_tn(a, b_ref[:, shard:shard + W_BODY])
        tb_s[:, 0:W_TAIL] = b_ref[:, W_BODY:shard]
        tb_s[:, W_TAIL:2 * W_TAIL] = b_ref[:, shard + W_BODY:2 * shard]
        tails = _dot_tn(a, tb_s[...])
        acc_s[:, W_BODY:shard] += tails[:, 0:W_TAIL]
        acc_s[:, shard + W_BODY:2 * shard] += tails[:, W_TAIL:2 * W_TAIL]

        for p in range(n_ph):
            for core in (0, 1):
                @pl.when(jnp.logical_and(jnp.logical_and(j == p, k == n_k - 1), c == core))
                def _(p=p, core=core):
                    same = acc_s[:, core * shard:(core + 1) * shard]
                    other = acc_s[:, (1 - core) * shard:(2 - core) * shard]
                    if p >= 1:
                        to_sibling(p - 1).wait_send()
                    stage_s[0] = other.astype(BF16)
                    if p < n_ph - 1:
                        to_sibling(p).start()
                    else:
                        over_ici(n_ph - 2).wait_send()
                        stage_s[1] = same.astype(BF16)
                        for cp in own_chip():
                            cp.start()

        @pl.when(jnp.logical_and(j == n_ph - 1, k == n_k - 1))
        def _():
            to_sib, local = own_chip()
            to_sib.wait_send()
            local.wait()
            _remote(stage_s.at[0], parts_hbm.at[_peer(x, y, c, SIBLING)[1]], sib_sems.at[0], sib_sems.at[1], sib).wait_recv()
            for p in range(n_ph - 1):
                dev, lin = _peer(x, y, c, RS_CHIPS[p])
                _remote(stage_s.at[0], parts_hbm.at[lin], ici_send.at[p], ici_recv.at[p], dev).wait_recv()
            ex.wait(ex_in, ex_out, ex_sems)

    dma = lambda n: pltpu.SemaphoreType.DMA((n,))
    grid_spec = pltpu.PrefetchScalarGridSpec(
        num_scalar_prefetch=1, grid=(n_ph, n_k),
        in_specs=[pl.BlockSpec((tk, D_MODEL), lambda j, k, order: (k, 0)),
                  pl.BlockSpec((tk, 2 * shard), lambda j, k, order: (k, order[j]))] + [ANY_SPEC] * ex.n,
        out_specs=[ANY_SPEC] * (1 + ex.n),
        scratch_shapes=[pltpu.VMEM((D_MODEL, 2 * shard), F32), pltpu.VMEM((tk, 2 * W_TAIL), BF16),
                        pltpu.VMEM((2, D_MODEL, shard), BF16),
                        pltpu.VMEM((2, D_MODEL, shard), BF16), dma(n_ph - 1), dma(n_ph - 1), dma(n_ph - 1),
                        dma(n_ph - 1), dma(2), dma(1)] + ex.scratch)
    res = pl.pallas_call(
        body, name="grad_w_in", grid_spec=grid_spec,
        out_shape=[SDS((N_DEV, D_MODEL, shard), BF16)] + ex.out_shape,
        compiler_params=_params(("arbitrary", "arbitrary"), 54),
    )(order, hn, dz, *ex_arrs)
    return res[0], slots, res[1:]


def _sum_parts(parts, name):
    def body(p_ref, o_ref):
        g = p_ref[0].astype(F32)
        for s in range(1, parts.shape[0]):
            g = g + p_ref[s].astype(F32)
        o_ref[...] = g

    return pl.pallas_call(body, name=name, out_shape=SDS(parts.shape[1:], F32))(parts)


def _adamw_math(g, w_ref, m_ref, v_ref, g_ref, d_ref, nm_ref, nv_ref):
    c1 = 1.0 - ADAM_B1 ** ADAM_STEP
    c2 = 1.0 - ADAM_B2 ** ADAM_STEP
    g_ref[...] = g
    nm = ADAM_B1 * m_ref[...] + (1.0 - ADAM_B1) * g
    nv = ADAM_B2 * v_ref[...] + (1.0 - ADAM_B2) * (g * g)
    nm_ref[...] = nm
    nv_ref[...] = nv
    d_ref[...] = -ADAM_LR * ((nm / c1) / (jnp.sqrt(nv / c2) + ADAM_EPS) + ADAM_WD * w_ref[...])


def _adamw(parts, w, m, v, name, tr):
    rows, cols = w.shape
    n_parts = parts.shape[0]

    def body(p_ref, *refs):
        g = p_ref[0].astype(F32)
        for s in range(1, n_parts):
            g = g + p_ref[s].astype(F32)
        _adamw_math(g, *refs)

    tile = pl.BlockSpec((tr, cols), lambda i: (i, 0))
    return pl.pallas_call(
        body, name=name, grid=(rows // tr,),
        in_specs=[pl.BlockSpec((n_parts, tr, cols), lambda i: (0, i, 0)), tile, tile, tile],
        out_specs=[tile] * 4, out_shape=[SDS((rows, cols), F32)] * 4,
        compiler_params=_params(("arbitrary",), 40),
    )(parts, w, m, v)


def _adamw_unpacked(grads, triples, name):
    n = len(triples)
    n_rows = [t[0].shape[0] for t in triples]

    def body(g_ref, *refs):
        ins, outs = refs[:3 * n], refs[3 * n:]
        row = 0
        for i in range(n):
            _adamw_math(g_ref[row:row + n_rows[i], :], *ins[3 * i:3 * i + 3], *outs[4 * i:4 * i + 4])
            row += n_rows[i]
        outs[4 * n][...] = g_ref[row:row + ROWS, :]

    out_shape = [SDS((r, LANES), F32) for r in n_rows for _ in range(4)] + [SDS((ROWS, LANES), F32)]
    return pl.pallas_call(
        body, name=name, out_shape=out_shape,
        compiler_params=pltpu.CompilerParams(vmem_limit_bytes=40 * MIB),
    )(grads, *[a for t in triples for a in t])


def _adamw_slots(parts, slots, w, m, v, name, tr):
    rows, cols = w.shape
    n_slots = slots.shape[0]

    def body(slots_ref, *refs):
        g = refs[0][...].astype(F32)
        for s in range(1, n_slots):
            g = g + refs[s][...].astype(F32)
        _adamw_math(g, *refs[n_slots:])

    tile = pl.BlockSpec((tr, cols), lambda i, slots: (i, 0))
    part = lambda s: pl.BlockSpec((None, tr, cols), lambda i, slots: (slots[s], i, 0))
    grid_spec = pltpu.PrefetchScalarGridSpec(
        num_scalar_prefetch=1, grid=(rows // tr,),
        in_specs=[part(s) for s in range(n_slots)] + [tile, tile, tile], out_specs=[tile] * 4)
    return pl.pallas_call(
        body, name=name, grid_spec=grid_spec, out_shape=[SDS((rows, cols), F32)] * 4,
        compiler_params=_params(("arbitrary",), 40),
    )(slots, *([parts] * n_slots), w, m, v)


PACKED = ("gmlp_ln_g", "gmlp_ln_b", "gmlp_ws", "gmlp_bs", "conv_b", "w_a", "b_a", "w_x", "b_x", "lam", "gmlp_out_g",
          "lru_out_g", "post_g")
WEIGHTS = ("pre_g", "w_in", "gmlp_ln_g", "gmlp_ln_b", "gmlp_ws", "gmlp_bs", "conv_w", "conv_b", "w_a", "b_a", "w_x",
           "b_x", "lam", "gmlp_out_g", "lru_out_g", "w_out", "post_g", "w_pe", "w_pg")
LANES = 128


PACK_ROWS = 3200


def _pack(parts):
    rows = [p.reshape(-1, LANES) for p in parts]
    used = sum(r.shape[0] for r in rows)
    return jnp.concatenate(rows + [jnp.zeros((PACK_ROWS - used, LANES), F32)], axis=0)


def _pad_rows(a, rows):
    return jnp.concatenate([a, jnp.zeros((rows - a.shape[0],) + a.shape[1:], a.dtype)], axis=0)


def kernel(x, p, pre_g, w_in, gmlp_ln_g, gmlp_ln_b, gmlp_ws, gmlp_bs, conv_w, conv_b, w_a, b_a, w_x, b_x, lam, gmlp_out_g, lru_out_g, w_out, post_g, w_pe, w_pg, loss_target, m_pre_g, m_w_in, m_gmlp_ln_g, m_gmlp_ln_b, m_gmlp_ws, m_gmlp_bs, m_conv_w, m_conv_b, m_w_a, m_b_a, m_w_x, m_b_x, m_lam, m_gmlp_out_g, m_lru_out_g, m_w_out, m_post_g, m_w_pe, m_w_pg, v_pre_g, v_w_in, v_gmlp_ln_g, v_gmlp_ln_b, v_gmlp_ws, v_gmlp_bs, v_conv_w, v_conv_b, v_w_a, v_b_a, v_w_x, v_b_x, v_lam, v_gmlp_out_g, v_lru_out_g, v_w_out, v_post_g, v_w_pe, v_w_pg):
    args = dict(locals())
    weights = {n: args[n] for n in WEIGHTS}
    m_in = {n: args["m_" + n] for n in WEIGHTS}
    v_in = {n: args["v_" + n] for n in WEIGHTS}
    sm = {n: weights[n][0] for n in PACKED}
    shard_rows = D_MODEL // N_DEV
    xs, ps, tgt = x[0], p[0, 0], loss_target[0]

    vec = lambda a: a.reshape(1, -1)
    tril = jnp.tril(jnp.ones((CHUNK, CHUNK), dtype=bool))
    wm32 = jnp.where(tril[None], sm["gmlp_ws"], 0.0)
    wm, wm_t = wm32.astype(BF16), jnp.swapaxes(wm32, 1, 2).astype(BF16)
    bias = jnp.repeat(sm["gmlp_bs"].T, HEAD, axis=1)
    wax32 = jnp.concatenate([sm["w_a"], sm["w_x"]], axis=2)
    wax, wax_t = wax32.astype(BF16), jnp.swapaxes(wax32, 1, 2).astype(BF16)
    ln_g, ln_b = vec(sm["gmlp_ln_g"]), vec(sm["gmlp_ln_b"])
    post_g_v = vec(sm["post_g"])

    hn = _pre_norm(xs, pre_g)
    cw_shard = _pad_rows(conv_w.reshape(CONV_W, HEAD), ROWS)
    z, w_in_g, (cw_g,) = _in_proj(hn, w_in[0].astype(BF16), [cw_shard])
    cw_full = jnp.transpose(cw_g[:, :CONV_W, :], (1, 0, 2)).reshape(CONV_W, D_BR)
    mixer_consts = dict(cw=_pad_rows(cw_full, ROWS), cb=vec(sm["conv_b"]), ba=vec(sm["b_a"]), bx=vec(sm["b_x"]),
                        lam=vec(sm["lam"]), goa=vec(sm["gmlp_out_g"]), gob=vec(sm["lru_out_g"]))
    (y, h, vhb, xcb, v_rs), (w_out_g, w_pe_g, w_pg_g) = _mix_fwd(
        z, ln_g, ln_b, wm, bias, wax=wax, **mixer_consts,
        ex_arrs=[w_out[0].astype(BF16), w_pe[0].astype(BF16), w_pg[0].astype(BF16)], ex_scatter=[False, False, False])
    w_out_f, w_pg_f = w_out_g.reshape(D_MODEL, D_MODEL), w_pg_g.reshape(D_MODEL, D_MODEL)
    h1, ob = _out_proj(y, xs, w_out_f, post_g_v)
    dh2, dgl, h1b, loss_part, d_w_pe = _ple_loss(h1, ps, tgt, w_pg_f, w_pe_g)

    dh1, do, dy, d_post_g = _tail_bwd(dh2, dgl, ob, w_pg_f, w_out_f, post_g_v)
    d_w_out, _ = _grad_w(y, do, 1024, False, "grad_w_out")
    d_w_pg, _ = _grad_w(h1b, dgl, 1024, False, "grad_w_pg")
    (dz, vecs, d_ws, d_wax, d_bs), (parts_out, parts_pg, parts_pe) = _mix_bwd(
        z, dy, h, vhb, xcb, v_rs, ln_g, ln_b, wm, wm_t, bias, wax=wax, wax_t=wax_t, **mixer_consts,
        ex_arrs=[d_w_out.reshape(N_DEV, shard_rows, D_MODEL), d_w_pg.reshape(N_DEV, shard_rows, D_MODEL), d_w_pe],
        ex_scatter=[True, True, True])

    small = {"gmlp_ln_g": vecs[V_LN_G], "gmlp_ln_b": vecs[V_LN_B], "gmlp_ws": d_ws, "gmlp_bs": d_bs,
             "conv_b": vecs[V_CONV_B], "w_a": d_wax[:, :, :HEAD], "b_a": vecs[V_B_A], "w_x": d_wax[:, :, HEAD:],
             "b_x": vecs[V_B_X], "lam": vecs[V_LAM], "gmlp_out_g": vecs[V_GOUT_A], "lru_out_g": vecs[V_GOUT_B],
             "post_g": d_post_g}
    small_part = _pack([small[n] for n in PACKED] + [loss_part]).reshape(N_DEV, PACK_ROWS // N_DEV, LANES)
    d_cw_blocks = jnp.transpose(vecs[V_CONV_W:V_CONV_W + CONV_W].reshape(CONV_W, N_DEV, HEAD), (1, 0, 2))
    d_cw_blocks = jnp.concatenate([d_cw_blocks, jnp.zeros((N_DEV, ROWS - CONV_W, HEAD), F32)], axis=1)
    parts_in, slots_in, (small_blocks, parts_cw) = _grad_w_in_pairs(
        hn, dz, ex_arrs=[small_part, d_cw_blocks], ex_scatter=[True, True])
    small_sum = _sum_parts(small_blocks, "sum_small")
    grad_x, d_pre_g = _in_bwd(dz, w_in_g, xs, dh1, pre_g)
    pre_rows = D_MODEL // LANES
    small_all, parts_pre = _exchange([small_sum, d_pre_g.reshape(pre_rows, LANES)], False, "gather_small_grads")

    pad_cw = lambda a: _pad_rows(a.reshape(CONV_W, HEAD), ROWS)
    flat = lambda a: a.reshape(pre_rows, LANES)
    outs = {
        "w_in": _adamw_slots(parts_in, slots_in, w_in[0], m_w_in[0], v_w_in[0], "adamw_w_in", 256),
        "w_out": _adamw(parts_out, w_out[0], m_w_out[0], v_w_out[0], "adamw_w_out", 128),
        "w_pe": _adamw(parts_pe, w_pe[0], m_w_pe[0], v_w_pe[0], "adamw_w_pe", 256),
        "w_pg": _adamw(parts_pg, w_pg[0], m_w_pg[0], v_w_pg[0], "adamw_w_pg", 128),
        "conv_w": [a[:CONV_W] for a in
                   _adamw(parts_cw, pad_cw(conv_w), pad_cw(m_conv_w), pad_cw(v_conv_w), "adamw_conv_w", ROWS)],
        "pre_g": _adamw(parts_pre, flat(pre_g), flat(m_pre_g), flat(v_pre_g), "adamw_pre_g", pre_rows),
    }
    as_rows = lambda a: a.reshape(-1, LANES)
    small_res = _adamw_unpacked(small_all.reshape(PACK_ROWS, LANES),
                                [(as_rows(weights[n]), as_rows(m_in[n]), as_rows(v_in[n])) for n in PACKED], "adamw_small")
    for i, n in enumerate(PACKED):
        outs[n] = small_res[4 * i:4 * i + 4]
    loss = small_res[-1][0, 0]

    result = [loss, grad_x[None]]
    for q in range(4):
        result += [outs[n][q].reshape(weights[n].shape) for n in WEIGHTS]
    return tuple(result)
```

```python
import jax
import jax.numpy as jnp
from jax import lax
from jax.experimental import pallas as pl
from jax.experimental.pallas import tpu as pltpu

F32 = jnp.float32
BF16 = jnp.bfloat16
SDS = jax.ShapeDtypeStruct

D_MODEL = 2048
D_BR = 1024
D_IN = 5 * D_BR
D_PLE = 256
N_HEAD = 8
HEAD = 128
CHUNK = 128
ROWS = 8
N_GROUP = CHUNK // ROWS
MIX_SUB = 2
N_DEV = 8
W_IN_SHARD = D_IN // N_DEV
EPS = 1e-6
LRU_C = 8.0
CONV_W = 4
MIB = 1 << 20

ADAM_LR, ADAM_B1, ADAM_B2, ADAM_EPS, ADAM_WD, ADAM_STEP = 0.001, 0.9, 0.999, 1e-08, 0.01, 10

_GELU_C = 0.7978845608028654
_GELU_A = 0.044715

V_LN_G, V_LN_B, V_CONV_B, V_B_A, V_B_X, V_LAM, V_GOUT_A, V_GOUT_B, V_CONV_W = 0, 1, 2, 3, 4, 5, 6, 7, 8
N_VEC = 16


def _params(sem, vmem_mib):
    return pltpu.CompilerParams(dimension_semantics=sem, vmem_limit_bytes=int(vmem_mib * MIB))


def _sig(x):
    return 0.5 * jnp.tanh(0.5 * x) + 0.5


def _gelu(x, with_grad=False):
    sq = x * x
    t = jnp.tanh(x * (_GELU_C + (_GELU_C * _GELU_A) * sq))
    half, one_t = 0.5 * x, 1.0 + t
    if not with_grad:
        return half * one_t
    grad = 0.5 * one_t + half * ((1.0 - t) * one_t) * (_GELU_C + (3.0 * _GELU_C * _GELU_A) * sq)
    return half * one_t, grad


def _silu_grad(s, xs):
    return s + xs * (1.0 - s)


def _neg_expm1(y, exp_y):
    series = -y * (1.0 + y * (0.5 + y * (1.0 / 6.0)))
    return jnp.where(y > -0.01, series, 1.0 - exp_y)


def _softplus(x):
    return jnp.maximum(x, 0.0) + jnp.log(1.0 + jnp.exp(-jnp.abs(x)))


def _row_ids(width):
    return lax.broadcasted_iota(jnp.int32, (ROWS, width), 0)


def _shift_down(cur, prev, k, rid):
    return jnp.where(rid >= k, pltpu.roll(cur, k, 0), pltpu.roll(prev, k, 0))


def _shift_up(cur, nxt, k, rid):
    return jnp.where(rid < ROWS - k, pltpu.roll(cur, ROWS - k, 0), pltpu.roll(nxt, ROWS - k, 0))


def _mean_last(x):
    return jnp.mean(x, axis=-1, keepdims=True)


def _rows(g):
    return pl.ds(pl.multiple_of(g * ROWS, ROWS), ROWS)


TILE_ROWS = 16


def _tile_rows(q):
    return pl.ds(pl.multiple_of(q * TILE_ROWS, TILE_ROWS), TILE_ROWS)


UNROLL = 4
TILE_UNROLL = 8


def _loop(n, body, init, unroll=UNROLL):
    def wide(i, carry):
        for u in range(unroll):
            carry = body(i * unroll + u, carry)
        return carry

    return lax.fori_loop(0, n // unroll, wide, init)


def _fold_rows(x):
    return x[0:ROWS, :] + x[ROWS:TILE_ROWS, :]


def _bcast_row(x, r):
    return jnp.broadcast_to(x[r:r + 1, :], x.shape)


def _dot(a, b):
    return jnp.dot(a, b, preferred_element_type=F32)


def _dot_nt(a, b):
    return lax.dot_general(a, b, (((1,), (1,)), ((), ())), preferred_element_type=F32)


def _dot_tn(a, b):
    return lax.dot_general(a, b, (((0,), (0,)), ((), ())), preferred_element_type=F32)


def _mesh_place():
    x, y, c = lax.axis_index("x"), lax.axis_index("y"), lax.axis_index("c")
    return x, y, c, 4 * x + 2 * y + c


def _peer(x, y, c, k):
    px = 1 - x if k & 4 else x
    py = 1 - y if k & 2 else y
    pc = 1 - c if k & 1 else c
    return (px, py, pc), 4 * px + 2 * py + pc


def _remote(src, dst, send_sem, recv_sem, dev):
    return pltpu.make_async_remote_copy(src_ref=src, dst_ref=dst, send_sem=send_sem, recv_sem=recv_sem, device_id=dev,
                                        device_id_type=pl.DeviceIdType.MESH)


ANY_SPEC = pl.BlockSpec(memory_space=pl.ANY)


class _Exchange:
    def __init__(self, arrs, scatter):
        self.n = len(arrs)
        self.scatter = tuple(scatter)
        self.out_shape = [SDS(a.shape if s else (N_DEV,) + a.shape, a.dtype) for a, s in zip(arrs, scatter)]
        self.scratch = [pltpu.SemaphoreType.DMA((self.n * N_DEV,)), pltpu.SemaphoreType.DMA((self.n * N_DEV,)),
                        pltpu.SemaphoreType.DMA((self.n,))]

    def _copies(self, ins, outs, sems):
        send_sems, recv_sems, local_sems = sems
        x, y, c, me = _mesh_place()
        local, sends, recvs = [], [], []
        for a in range(self.n):
            src = ins[a].at[me] if self.scatter[a] else ins[a]
            local.append(pltpu.make_async_copy(src, outs[a].at[me], local_sems.at[a]))
        for k in range(1, N_DEV):
            dev, lin = _peer(x, y, c, k)
            for a in range(self.n):
                src = ins[a].at[lin] if self.scatter[a] else ins[a]
                pair = (send_sems.at[a * N_DEV + k], recv_sems.at[a * N_DEV + k], dev)
                sends.append(_remote(src, outs[a].at[me], *pair))
                recvs.append(_remote(src, outs[a].at[lin], *pair))
        return local, sends, recvs

    def start(self, ins, outs, sems):
        local, sends, _ = self._copies(ins, outs, sems)
        for cp in local + sends:
            cp.start()

    def wait(self, ins, outs, sems):
        local, sends, recvs = self._copies(ins, outs, sems)
        for cp in recvs:
            cp.wait_recv()
        for cp in sends:
            cp.wait_send()
        for cp in local:
            cp.wait()


def _exchange(arrs, scatter, name):
    ex = _Exchange(arrs, [scatter] * len(arrs))
    n = ex.n

    def body(*refs):
        ins, outs, sems = refs[:n], refs[n:2 * n], refs[2 * n:]
        ex.start(ins, outs, sems)
        ex.wait(ins, outs, sems)

    return pl.pallas_call(
        body, name=name, out_shape=ex.out_shape, in_specs=[ANY_SPEC] * n, out_specs=[ANY_SPEC] * n,
        scratch_shapes=ex.scratch,
    )(*arrs)


def _pre_norm(x, pre_g, tm=512):
    t_len = x.shape[0]

    def body(x_ref, g_ref, hn_ref):
        g = g_ref[...]

        def rows_body(q, _):
            rows = _tile_rows(q)
            xv = x_ref[rows, :]
            hn_ref[rows, :] = (xv * lax.rsqrt(_mean_last(xv * xv) + EPS) * g).astype(BF16)
            return 0

        _loop(tm // TILE_ROWS, rows_body, 0, unroll=TILE_UNROLL)

    tile = pl.BlockSpec((tm, D_MODEL), lambda i: (i, 0))
    return pl.pallas_call(
        body, name="pre_norm", grid=(t_len // tm,),
        in_specs=[tile, pl.BlockSpec((1, D_MODEL), lambda i: (0, 0))], out_specs=tile,
        out_shape=SDS((t_len, D_MODEL), BF16),
        compiler_params=_params(("arbitrary",), 24),
    )(x, pre_g)


CHIP_ORDER = (0, 2, 4, 6)
W_BODY, W_TAIL = 512, 128
SIBLING = 1
ICI_MASKS = (2, 4, 6)
DIRECT_MASKS = (SIBLING,) + ICI_MASKS
Y_NEIGHBOUR, X_NEIGHBOUR, DIAGONAL = 2, 4, 6
W_DIRECT = (SIBLING, Y_NEIGHBOUR, X_NEIGHBOUR)


def _in_proj(hn, w_shard, others, tm=1024):
    t_len = hn.shape[0]
    n_i = t_len // tm
    n_o = len(others)
    me_out = 4 * lax.axis_index("x") + 2 * lax.axis_index("y") + lax.axis_index("c")
    order = jnp.stack([(me_out ^ chip) // 2 for chip in CHIP_ORDER]).astype(jnp.int32)

    def body(order_ref, hn_ref, w_hbm, *refs):
        o_in = refs[:n_o]
        z_ref, wg_hbm = refs[n_o], refs[n_o + 1]
        o_out = refs[n_o + 2:2 * n_o + 2]
        (wbuf, tail_s, send_w, recv_w, fsend_w, frecv_w, send_o, recv_o, fsend_o, frecv_o, wb_sems, loc_sems, rsend,
         rrecv) = refs[2 * n_o + 2:]
        j, i = pl.program_id(0), pl.program_id(1)
        x, y, c, me = _mesh_place()
        sib = _peer(x, y, c, SIBLING)[0]

        def relay(core):
            src, dst = (Y_NEIGHBOUR, X_NEIGHBOUR) if core == 0 else (X_NEIGHBOUR, Y_NEIGHBOUR)
            held, diag = _peer(x, y, c, src)[1], _peer(x, y, c, DIAGONAL)[1]
            pair = (rsend.at[0], rrecv.at[0], _peer(x, y, c, dst)[0])
            return _remote(wbuf.at[held], wbuf.at[held], *pair), _remote(wbuf.at[diag], wbuf.at[diag], *pair)

        def direct(k, a=None):
            dev, lin = _peer(x, y, c, k)
            if a is None:
                return (_remote(w_hbm, wbuf.at[me], send_w.at[k], recv_w.at[k], dev),
                        _remote(w_hbm, wbuf.at[lin], send_w.at[k], recv_w.at[k], dev))
            pair = (send_o.at[a * N_DEV + k], recv_o.at[a * N_DEV + k], dev)
            return _remote(o_in[a], o_out[a].at[me], *pair), _remote(o_in[a], o_out[a].at[lin], *pair)

        def passed(k, a=None):
            mine, theirs = _peer(x, y, c, k)[1], _peer(x, y, c, k ^ SIBLING)[1]
            if a is None:
                pair = (fsend_w.at[k], frecv_w.at[k], sib)
                return _remote(wbuf.at[mine], wbuf.at[mine], *pair), _remote(wbuf.at[theirs], wbuf.at[theirs], *pair)
            pair = (fsend_o.at[a * N_DEV + k], frecv_o.at[a * N_DEV + k], sib)
            return (_remote(o_out[a].at[mine], o_out[a].at[mine], *pair),
                    _remote(o_out[a].at[theirs], o_out[a].at[theirs], *pair))

        def own_copies():
            return [pltpu.make_async_copy(o_in[a], o_out[a].at[me], loc_sems.at[1 + a]) for a in range(n_o)]

        @pl.when(jnp.logical_and(j == 0, i == 0))
        def _():
            own = pltpu.make_async_copy(w_hbm, wbuf.at[me], loc_sems.at[0])
            own.start()
            for cp in own_copies():
                cp.start()
            for k in W_DIRECT:
                direct(k)[0].start()
            for k in DIRECT_MASKS:
                for a in range(n_o):
                    direct(k, a)[0].start()
            own.wait()

        low = 2 * order_ref[j]

        for jp, chip in enumerate(CHIP_ORDER):
            @pl.when(jnp.logical_and(j == jp, i == 0))
            def _(jp=jp, chip=chip):
                if chip == 0:
                    direct(SIBLING)[1].wait_recv()
                elif chip == Y_NEIGHBOUR:
                    for mask in (Y_NEIGHBOUR, X_NEIGHBOUR):
                        direct(mask)[1].wait_recv()
                        passed(mask)[0].start()
                    for core in (0, 1):
                        @pl.when(c == core)
                        def _(core=core):
                            relay(core)[0].start()
                    passed(Y_NEIGHBOUR)[1].wait_recv()
                elif chip == X_NEIGHBOUR:
                    passed(X_NEIGHBOUR)[1].wait_recv()
                    for core in (0, 1):
                        @pl.when(c == core)
                        def _(core=core):
                            relay(core)[1].wait_recv()
                    passed(DIAGONAL)[0].start()
                    for k in ICI_MASKS:
                        for a in range(n_o):
                            direct(k, a)[1].wait_recv()
                            passed(k, a)[0].start()
                else:
                    passed(DIAGONAL)[1].wait_recv()
                for half in (0, 1):
                    pltpu.make_async_copy(wbuf.at[low + half], wg_hbm.at[low + half], wb_sems.at[2 * jp + half]).start()
                tail_s[:, 0:W_TAIL] = wbuf[low, :, W_BODY:W_IN_SHARD]
                tail_s[:, W_TAIL:2 * W_TAIL] = wbuf[low + 1, :, W_BODY:W_IN_SHARD]

        hn = hn_ref[...]
        z_ref[:, 0:W_BODY] = _dot(hn, wbuf[low, :, 0:W_BODY])
        z_ref[:, W_IN_SHARD:W_IN_SHARD + W_BODY] = _dot(hn, wbuf[low + 1, :, 0:W_BODY])
        tails = _dot(hn, tail_s[...])
        z_ref[:, W_BODY:W_IN_SHARD] = tails[:, 0:W_TAIL]
        z_ref[:, W_IN_SHARD + W_BODY:2 * W_IN_SHARD] = tails[:, W_TAIL:2 * W_TAIL]

        @pl.when(jnp.logical_and(j == len(CHIP_ORDER) - 1, i == n_i - 1))
        def _():
            for a in range(n_o):
                direct(SIBLING, a)[1].wait_recv()
            for k in ICI_MASKS:
                for a in range(n_o):
                    passed(k, a)[1].wait_recv()
            for k in W_DIRECT:
                direct(k)[0].wait_send()
            for core in (0, 1):
                @pl.when(c == core)
                def _(core=core):
                    relay(core)[0].wait_send()
            for k in DIRECT_MASKS:
                for a in range(n_o):
                    direct(k, a)[0].wait_send()
            for k in ICI_MASKS:
                passed(k)[0].wait_send()
                for a in range(n_o):
                    passed(k, a)[0].wait_send()
            for cp in own_copies():
                cp.wait()
            for jj in range(N_DEV):
                pltpu.make_async_copy(wbuf.at[0], wg_hbm.at[0], wb_sems.at[jj]).wait()

    dma = lambda n: pltpu.SemaphoreType.DMA((n,))
    grid_spec = pltpu.PrefetchScalarGridSpec(
        num_scalar_prefetch=1, grid=(len(CHIP_ORDER), n_i),
        in_specs=[pl.BlockSpec((tm, D_MODEL), lambda j, i, order: (i, 0)), ANY_SPEC] + [ANY_SPEC] * n_o,
        out_specs=[pl.BlockSpec((tm, 2 * W_IN_SHARD), lambda j, i, order: (i, order[j])), ANY_SPEC] + [ANY_SPEC] * n_o,
        scratch_shapes=[pltpu.VMEM((N_DEV, D_MODEL, W_IN_SHARD), BF16), pltpu.VMEM((D_MODEL, 2 * W_TAIL), BF16),
                        dma(N_DEV), dma(N_DEV), dma(N_DEV), dma(N_DEV),
                        dma(n_o * N_DEV), dma(n_o * N_DEV), dma(n_o * N_DEV), dma(n_o * N_DEV), dma(N_DEV), dma(1 + n_o),
                        dma(1), dma(1)])
    res = pl.pallas_call(
        body, name="in_proj", grid_spec=grid_spec,
        out_shape=[SDS((t_len, D_IN), F32), SDS((N_DEV, D_MODEL, W_IN_SHARD), BF16)]
        + [SDS((N_DEV,) + o.shape, o.dtype) for o in others],
        compiler_params=_params(("arbitrary", "arbitrary"), 54),
    )(order, hn, w_shard, *others)
    return res[0], res[1], res[2:]


def _conv_rows(cur, prev, cw_ref, cb, rid):
    acc = cw_ref[3:4, :] * cur + cb
    for k in range(1, CONV_W):
        acc = acc + cw_ref[3 - k:4 - k, :] * _shift_down(cur, prev, k, rid)
    return acc


ROW0_LOG_A = -1e30


def _row0_mask(rid):
    return jnp.where(rid == 0, ROW0_LOG_A, 0.0)


def _row0_bias(is_first_group, row0_mask):
    return is_first_group.astype(F32) * row0_mask


def _lru_gates(pa, px, ba, bx, sp8, row0_bias):
    r = _sig(pa + ba)
    i = _sig(px + bx)
    la = row0_bias - r * sp8
    a = jnp.exp(la)
    return r, i, a, _neg_expm1(2.0 * la, a * a)


def _mix_fwd(z, ln_g, ln_b, wm, bias, cw, cb, wax, ba, bx, lam, goa, gob, ex_arrs, ex_scatter):
    t_len = z.shape[0]
    n_chunk = t_len // CHUNK
    ex = _Exchange(ex_arrs, ex_scatter)
    n_in, n_out, n_scratch = 13, 5, 7

    def body(*refs):
        (z_ref, lng_ref, lnb_ref, wm_ref, bias_ref, cw_ref, cb_ref, wax_ref, ba_ref, bx_ref, lam_ref, goa_ref,
         gob_ref) = refs[:n_in]
        ex_in = refs[n_in:n_in + ex.n]
        y_ref, h_ref, vhb_ref, xcb_ref, rs_ref = refs[n_in + ex.n:n_in + ex.n + n_out]
        ex_out = refs[n_in + ex.n + n_out:n_in + 2 * ex.n + n_out]
        vn_s, xc_s, mixed_s, pre_s, y_s, carry_s, halo_s = refs[n_in + 2 * ex.n + n_out:n_in + 2 * ex.n + n_out + n_scratch]
        ex_sems = refs[n_in + 2 * ex.n + n_out + n_scratch:]
        step = pl.program_id(0)
        rid = _row_ids(D_BR)

        @pl.when(step == 0)
        def _():
            ex.start(ex_in, ex_out, ex_sems)
            carry_s[...] = jnp.zeros_like(carry_s)
            halo_s[...] = jnp.zeros_like(halo_s)

        lng, lnb, cb = lng_ref[...], lnb_ref[...], cb_ref[...]
        ba, bx, goa, gob = ba_ref[...], bx_ref[...], goa_ref[...], gob_ref[...]
        sp8 = LRU_C * _softplus(-lam_ref[...])
        row0 = _row0_mask(rid)

        def chunk(c_id, z_ref, y_ref, h_ref, vhb_ref, xcb_ref, rs_ref):
            def phase1(g, prev):
                rows = _rows(g)
                vg = _gelu(z_ref[rows, D_BR:2 * D_BR])
                xm = vg - _mean_last(vg)
                rs = lax.rsqrt(_mean_last(xm * xm) + EPS)
                vn_s[rows, :] = xm * rs
                rs_ref[rows, :] = jnp.broadcast_to(rs, (ROWS, HEAD))
                xb = z_ref[rows, 3 * D_BR:4 * D_BR]
                xc_s[rows, :] = _conv_rows(xb, prev, cw_ref, cb, rid)
                return xb

            halo_s[...] = _loop(N_GROUP, phase1, halo_s[...], unroll=8)
            vhb_ref[...] = vn_s[...].astype(BF16)
            xcb_ref[...] = xc_s[...].astype(BF16)

            for h in range(N_HEAD):
                cs = slice(h * HEAD, (h + 1) * HEAD)
                mixed_s[:, cs] = _dot(wm_ref[h], (vn_s[:, cs] * lng[:, cs] + lnb[:, cs]).astype(BF16))
                pre = _dot(xcb_ref[:, cs], wax_ref[h])
                pre_s[:, cs] = pre[:, :HEAD]
                pre_s[:, D_BR + h * HEAD:D_BR + (h + 1) * HEAD] = pre[:, HEAD:]

            def phase3(g, carry):
                rows = _rows(g)
                ug = _gelu(z_ref[rows, 0:D_BR])
                ga = z_ref[rows, 2 * D_BR:3 * D_BR]
                ya = ug * (mixed_s[rows, :] + bias_ref[rows, :]) * (ga * _sig(ga))
                y_s[rows, 0:D_BR] = ya * lax.rsqrt(_mean_last(ya * ya) + EPS) * goa

                bias0 = _row0_bias(jnp.logical_and(c_id == 0, g == 0), row0)
                _, i, a, m2 = _lru_gates(pre_s[rows, 0:D_BR], pre_s[rows, D_BR:2 * D_BR], ba, bx, sp8, bias0)
                b = jnp.sqrt(m2) * i * xc_s[rows, :]
                for d in (1, 2, 4):
                    a_sh = jnp.where(rid >= d, pltpu.roll(a, d, 0), 1.0)
                    b_sh = jnp.where(rid >= d, pltpu.roll(b, d, 0), 0.0)
                    b = a * b_sh + b
                    a = a * a_sh
                hh = b + a * carry
                h_ref[rows, :] = hh
                gb = z_ref[rows, 4 * D_BR:5 * D_BR]
                yb = hh * (gb * _sig(gb))
                y_s[rows, D_BR:2 * D_BR] = yb * lax.rsqrt(_mean_last(yb * yb) + EPS) * gob
                return _bcast_row(hh, ROWS - 1)

            carry_s[...] = _loop(N_GROUP, phase3, carry_s[...])
            y_ref[...] = y_s[...].astype(BF16)

        for sub in range(MIX_SUB):
            part = lambda ref, sub=sub: ref.at[pl.ds(sub * CHUNK, CHUNK)]
            chunk(step * MIX_SUB + sub, part(z_ref), part(y_ref), part(h_ref), part(vhb_ref), part(xcb_ref),
                  part(rs_ref))

        @pl.when(step == n_chunk // MIX_SUB - 1)
        def _():
            ex.wait(ex_in, ex_out, ex_sems)

    vec = pl.BlockSpec((1, D_BR), lambda i: (0, 0))
    blk = MIX_SUB * CHUNK
    res = pl.pallas_call(
        body, name="mix_fwd", grid=(n_chunk // MIX_SUB,),
        in_specs=[pl.BlockSpec((blk, D_IN), lambda i: (i, 0)), vec, vec,
                  pl.BlockSpec((N_HEAD, HEAD, HEAD), lambda i: (0, 0, 0)),
                  pl.BlockSpec((CHUNK, D_BR), lambda i: (0, 0)),
                  pl.BlockSpec((ROWS, D_BR), lambda i: (0, 0)), vec,
                  pl.BlockSpec((N_HEAD, HEAD, 2 * HEAD), lambda i: (0, 0, 0)), vec, vec, vec, vec, vec]
        + [ANY_SPEC] * ex.n,
        out_specs=[pl.BlockSpec((blk, 2 * D_BR), lambda i: (i, 0)), pl.BlockSpec((blk, D_BR), lambda i: (i, 0)),
                   pl.BlockSpec((blk, D_BR), lambda i: (i, 0)), pl.BlockSpec((blk, D_BR), lambda i: (i, 0)),
                   pl.BlockSpec((blk, HEAD), lambda i: (i, 0))] + [ANY_SPEC] * ex.n,
        out_shape=[SDS((t_len, 2 * D_BR), BF16), SDS((t_len, D_BR), F32), SDS((t_len, D_BR), BF16),
                   SDS((t_len, D_BR), BF16), SDS((t_len, HEAD), F32)] + ex.out_shape,
        scratch_shapes=[pltpu.VMEM((CHUNK, D_BR), F32), pltpu.VMEM((CHUNK, D_BR), F32), pltpu.VMEM((CHUNK, D_BR), F32),
                        pltpu.VMEM((CHUNK, 2 * D_BR), F32), pltpu.VMEM((CHUNK, 2 * D_BR), F32),
                        pltpu.VMEM((ROWS, D_BR), F32), pltpu.VMEM((ROWS, D_BR), F32)] + ex.scratch,
        compiler_params=_params(("arbitrary",), 32),
    )(z, ln_g, ln_b, wm, bias, cw, cb, wax, ba, bx, lam, goa, gob, *ex_arrs)
    return res[:n_out], res[n_out:]


W_LOAD_BLOCKS = 4


def _dot_loading(a_ref, w_hbm, w_s, sems, out_s):
    first = pl.program_id(0) == 0
    k = w_s.shape[0] // W_LOAD_BLOCKS

    @pl.when(first)
    def _():
        loads = [pltpu.make_async_copy(w_hbm.at[pl.ds(c * k, k)], w_s.at[pl.ds(c * k, k)], sems.at[c])
                 for c in range(W_LOAD_BLOCKS)]
        for cp in loads:
            cp.start()
        for c, cp in enumerate(loads):
            cp.wait()
            part = _dot(a_ref[:, c * k:(c + 1) * k], w_s[c * k:(c + 1) * k, :])
            out_s[...] = part if c == 0 else out_s[...] + part

    @pl.when(jnp.logical_not(first))
    def _():
        out_s[...] = _dot(a_ref[...], w_s[...])


def _out_proj(y, x, w_out, post_g, tm=512):
    t_len = y.shape[0]

    def body(y_ref, x_ref, w_hbm, g_ref, h1_ref, ob_ref, w_s, o_s, sems):
        _dot_loading(y_ref, w_hbm, w_s, sems, o_s)
        g = g_ref[...]

        def rows_body(q, _):
            rows = _tile_rows(q)
            o = o_s[rows, :]
            h1_ref[rows, :] = x_ref[rows, :] + o * lax.rsqrt(_mean_last(o * o) + EPS) * g
            ob_ref[rows, :] = o.astype(BF16)
            return 0

        _loop(tm // TILE_ROWS, rows_body, 0, unroll=TILE_UNROLL)

    tile = pl.BlockSpec((tm, D_MODEL), lambda i: (i, 0))
    return pl.pallas_call(
        body, name="out_proj", grid=(t_len // tm,),
        in_specs=[tile, tile, pl.BlockSpec(memory_space=pl.ANY), pl.BlockSpec((1, D_MODEL), lambda i: (0, 0))],
        out_specs=[tile, tile],
        out_shape=[SDS((t_len, D_MODEL), F32), SDS((t_len, D_MODEL), BF16)],
        scratch_shapes=[pltpu.VMEM((D_MODEL, D_MODEL), BF16), pltpu.VMEM((tm, D_MODEL), F32),
                        pltpu.SemaphoreType.DMA((W_LOAD_BLOCKS,))],
        compiler_params=_params(("arbitrary",), 44),
    )(y, x, w_out, post_g)


def _ple_loss(h1, p, tgt, w_pg, w_pe_g, tm=256):
    t_len = h1.shape[0]
    n_tile = t_len // tm
    pe_shard = D_MODEL // N_DEV

    def body(h1_ref, p_ref, t_ref, w_hbm, wpe_ref, dh2_ref, dgl_ref, h1b_ref, loss_ref, dwpe_ref, w_s, pe_s, gl_s, acc_s,
             dpe_s, gpe_s, sems):
        i = pl.program_id(0)

        @pl.when(i == 0)
        def _():
            acc_s[...] = jnp.zeros_like(acc_s)
            gpe_s[...] = jnp.zeros_like(gpe_s)

        h1b_ref[...] = h1_ref[...].astype(BF16)
        pb = p_ref[...].astype(BF16)
        for j in range(N_DEV):
            pe_s[:, j * pe_shard:(j + 1) * pe_shard] = _dot(pb, wpe_ref[j])
        _dot_loading(h1b_ref, w_hbm, w_s, sems, gl_s)

        def rows_body(q, acc):
            rows = _tile_rows(q)
            pe = pe_s[rows, :]
            g = _sig(gl_s[rows, :])
            e = h1_ref[rows, :] + pe * g - t_ref[rows, :]
            dh2 = e * (1.0 / D_MODEL)
            dh2_ref[rows, :] = dh2
            dpe_s[rows, :] = (dh2 * g).astype(BF16)
            dgl_ref[rows, :] = (dh2 * pe * g * (1.0 - g)).astype(BF16)
            return acc + _fold_rows(e * e)

        acc_s[...] = _loop(tm // TILE_ROWS, rows_body, acc_s[...], unroll=TILE_UNROLL)
        gpe_s[...] += _dot_tn(pb, dpe_s[...])

        @pl.when(i == n_tile - 1)
        def _():
            loss_ref[...] = jnp.full(loss_ref.shape, 0.5 / D_MODEL * jnp.sum(acc_s[...]), F32)
            for j in range(N_DEV):
                dwpe_ref[j] = gpe_s[:, j * pe_shard:(j + 1) * pe_shard].astype(BF16)

    tile = pl.BlockSpec((tm, D_MODEL), lambda i: (i, 0))
    pe_blocks = pl.BlockSpec((N_DEV, D_PLE, pe_shard), lambda i: (0, 0, 0))
    return pl.pallas_call(
        body, name="ple_loss", grid=(n_tile,),
        in_specs=[tile, pl.BlockSpec((tm, D_PLE), lambda i: (i, 0)), tile, pl.BlockSpec(memory_space=pl.ANY), pe_blocks],
        out_specs=[tile, tile, tile, pl.BlockSpec((ROWS, HEAD), lambda i: (0, 0)), pe_blocks],
        out_shape=[SDS((t_len, D_MODEL), F32), SDS((t_len, D_MODEL), BF16), SDS((t_len, D_MODEL), BF16),
                   SDS((ROWS, HEAD), F32), SDS((N_DEV, D_PLE, pe_shard), BF16)],
        scratch_shapes=[pltpu.VMEM((D_MODEL, D_MODEL), BF16), pltpu.VMEM((tm, D_MODEL), F32),
                        pltpu.VMEM((tm, D_MODEL), F32), pltpu.VMEM((ROWS, D_MODEL), F32), pltpu.VMEM((tm, D_MODEL), BF16),
                        pltpu.VMEM((D_PLE, D_MODEL), F32), pltpu.SemaphoreType.DMA((W_LOAD_BLOCKS,))],
        compiler_params=_params(("arbitrary",), 48),
    )(h1, p, tgt, w_pg, w_pe_g)


def _tail_bwd(dh2, dgl, ob, w_pg, w_out, post_g, tm=256):
    t_len = dh2.shape[0]
    n_tile = t_len // tm

    def body(dh2_ref, dgl_ref, ob_ref, wpg_hbm, wout_hbm, g_ref, dh1_ref, do_ref, dy_ref, dg_ref, wpg_s, wout_s, t_s,
             acc_s, sems):
        i = pl.program_id(0)
        k = D_MODEL // W_LOAD_BLOCKS
        load_wout = pltpu.make_async_copy(wout_hbm, wout_s, sems.at[W_LOAD_BLOCKS])

        @pl.when(i == 0)
        def _():
            loads = [pltpu.make_async_copy(wpg_hbm.at[pl.ds(c * k, k)], wpg_s.at[pl.ds(c * k, k)], sems.at[c])
                     for c in range(W_LOAD_BLOCKS)]
            for cp in loads:
                cp.start()
            load_wout.start()
            acc_s[...] = jnp.zeros_like(acc_s)
            for c, cp in enumerate(loads):
                cp.wait()
                t_s[:, c * k:(c + 1) * k] = _dot_nt(dgl_ref[...], wpg_s[c * k:(c + 1) * k, :])

        @pl.when(i > 0)
        def _():
            t_s[...] = _dot_nt(dgl_ref[...], wpg_s[...])

        g = g_ref[...]

        def rows_body(q, acc):
            rows = _tile_rows(q)
            dh1 = dh2_ref[rows, :] + t_s[rows, :]
            dh1_ref[rows, :] = dh1
            o = ob_ref[rows, :].astype(F32)
            rr = lax.rsqrt(_mean_last(o * o) + EPS)
            on = o * rr
            dog = dh1 * g
            do_ref[rows, :] = (rr * (dog - on * _mean_last(dog * on))).astype(BF16)
            return acc + _fold_rows(dh1 * on)

        acc_s[...] = _loop(tm // TILE_ROWS, rows_body, acc_s[...], unroll=TILE_UNROLL)

        @pl.when(i == 0)
        def _():
            load_wout.wait()

        dy_ref[...] = _dot_nt(do_ref[...], wout_s[...]).astype(BF16)

        @pl.when(i == n_tile - 1)
        def _():
            dg_ref[...] = jnp.sum(acc_s[...], axis=0, keepdims=True)

    tile = pl.BlockSpec((tm, D_MODEL), lambda i: (i, 0))
    vec = pl.BlockSpec((1, D_MODEL), lambda i: (0, 0))
    hbm = pl.BlockSpec(memory_space=pl.ANY)
    return pl.pallas_call(
        body, name="tail_bwd", grid=(n_tile,),
        in_specs=[tile, tile, tile, hbm, hbm, vec],
        out_specs=[tile, tile, tile, vec],
        out_shape=[SDS((t_len, D_MODEL), F32), SDS((t_len, D_MODEL), BF16), SDS((t_len, D_MODEL), BF16),
                   SDS((1, D_MODEL), F32)],
        scratch_shapes=[pltpu.VMEM((D_MODEL, D_MODEL), BF16), pltpu.VMEM((D_MODEL, D_MODEL), BF16),
                        pltpu.VMEM((tm, D_MODEL), F32), pltpu.VMEM((ROWS, D_MODEL), F32),
                        pltpu.SemaphoreType.DMA((W_LOAD_BLOCKS + 1,))],
        compiler_params=_params(("arbitrary",), 48),
    )(dh2, dgl, ob, w_pg, w_out, post_g)


def _mix_bwd(z, dy, h, vhb, xcb, rs, ln_g, ln_b, wm, wm_t, bias, cw, cb, wax, wax_t, ba, bx, lam, goa, gob, ex_arrs,
             ex_scatter):
    t_len = z.shape[0]
    n_chunk = t_len // CHUNK
    halo_blocks = CHUNK // ROWS
    ex = _Exchange(ex_arrs, ex_scatter)
    n_in, n_out, n_scratch = 21, 5, 16

    blocked = (0, 1, 2, 4, 5, 6, n_in + ex.n)

    def body(*refs):
        step = pl.program_id(0)
        for sub in reversed(range(MIX_SUB)):
            views = list(refs)
            for idx in blocked:
                views[idx] = refs[idx].at[pl.ds(sub * CHUNK, CHUNK)]
            h_before = refs[2].at[pl.ds(sub * CHUNK - ROWS, ROWS)] if sub else refs[3]
            chunk((n_chunk // MIX_SUB - 1 - step) * MIX_SUB + sub,
                  step == 0 if sub == MIX_SUB - 1 else None,
                  step == n_chunk // MIX_SUB - 1 if sub == 0 else None, h_before, *views)

    def chunk(c_id, first, last, h_before, *refs):
        (z_ref, dy_ref, h_ref, hhalo_ref, vhb_ref, xcb_ref, rs_ref, lng_ref, lnb_ref, wm_ref, wmt_ref, bias_ref, cw_ref,
         cb_ref, wax_ref, waxt_ref, ba_ref, bx_ref, lam_ref, goa_ref, gob_ref) = refs[:n_in]
        ex_in = refs[n_in:n_in + ex.n]
        dz_ref, vecs_ref, dws_ref, dwax_ref, dbs_ref = refs[n_in + ex.n:n_in + ex.n + n_out]
        ex_out = refs[n_in + ex.n + n_out:n_in + 2 * ex.n + n_out]
        (vnb_s, vh_s, xc_s, mixed_s, pre_s, dmix_s, dvn_s, dho_s, dxc_s, dpre_s, dz_s, acc_s, accdm_s,
         cg_s, ca_s, dxchalo_s) = refs[n_in + 2 * ex.n + n_out:n_in + 2 * ex.n + n_out + n_scratch]
        ex_sems = refs[n_in + 2 * ex.n + n_out + n_scratch:]
        rid = _row_ids(D_BR)
        first_chunk = c_id == 0

        if first is not None:
            @pl.when(first)
            def _():
                ex.start(ex_in, ex_out, ex_sems)
                acc_s[...] = jnp.zeros_like(acc_s)
                accdm_s[...] = jnp.zeros_like(accdm_s)
                cg_s[...] = jnp.zeros_like(cg_s)
                ca_s[...] = jnp.zeros_like(ca_s)
                dxchalo_s[...] = jnp.zeros_like(dxchalo_s)
                dws_ref[...] = jnp.zeros_like(dws_ref)
                dwax_ref[...] = jnp.zeros_like(dwax_ref)

        lng, lnb = lng_ref[...], lnb_ref[...]
        h_halo = jnp.where(first_chunk, 0.0, h_before[...])

        def prev_rows(ref, cols, g, halo):
            before = ref[pl.ds(pl.multiple_of(jnp.maximum(g - 1, 0) * ROWS, ROWS), ROWS), cols]
            return jnp.where(g > 0, before, halo)

        vh_s[...] = vhb_ref[...].astype(F32)
        xc_s[...] = xcb_ref[...].astype(F32)

        for hd in range(N_HEAD):
            cs = slice(hd * HEAD, (hd + 1) * HEAD)
            vnb_s[:, cs] = (vh_s[:, cs] * lng[:, cs] + lnb[:, cs]).astype(BF16)
            mixed_s[:, cs] = _dot(wm_ref[hd], vnb_s[:, cs])
            pre = _dot(xcb_ref[:, cs], wax_ref[hd])
            pre_s[:, cs] = pre[:, :HEAD]
            pre_s[:, D_BR + hd * HEAD:D_BR + (hd + 1) * HEAD] = pre[:, HEAD:]

        goa, gob = goa_ref[...], gob_ref[...]

        def phase3(g, _):
            rows = _rows(g)
            ug, dug = _gelu(z_ref[rows, 0:D_BR], with_grad=True)
            ga = z_ref[rows, 2 * D_BR:3 * D_BR]
            sga = _sig(ga)
            sa = ga * sga
            mixed = mixed_s[rows, :] + bias_ref[rows, :]
            ya0 = ug * mixed
            ya = ya0 * sa
            ra = lax.rsqrt(_mean_last(ya * ya) + EPS)
            dyan = dy_ref[rows, 0:D_BR].astype(F32)
            acc_s[V_GOUT_A] += dyan * ya * ra
            dyg = dyan * goa
            dya = ra * dyg - ya * (ra * ra * ra) * _mean_last(dyg * ya)
            dya0 = dya * sa
            dz_s[rows, 2 * D_BR:3 * D_BR] = dya * ya0 * _silu_grad(sga, sa)
            dmix = dya0 * ug
            dmix_s[rows, :] = dmix
            accdm_s[rows, :] += dmix
            dz_s[rows, 0:D_BR] = dya0 * mixed * dug

            hh = h_ref[rows, :]
            gb = z_ref[rows, 4 * D_BR:5 * D_BR]
            sgb = _sig(gb)
            sb = gb * sgb
            yb = hh * sb
            rb = lax.rsqrt(_mean_last(yb * yb) + EPS)
            dybn = dy_ref[rows, D_BR:2 * D_BR].astype(F32)
            acc_s[V_GOUT_B] += dybn * yb * rb
            dyg = dybn * gob
            dyb = rb * dyg - yb * (rb * rb * rb) * _mean_last(dyg * yb)
            dho_s[rows, :] = dyb * sb
            dz_s[rows, 4 * D_BR:5 * D_BR] = dyb * hh * _silu_grad(sgb, sb)
            return 0

        _loop(N_GROUP, phase3, 0)

        for hd in range(N_HEAD):
            cs = slice(hd * HEAD, (hd + 1) * HEAD)
            dmb = dmix_s[:, cs].astype(BF16)
            dvn_s[:, cs] = _dot(wmt_ref[hd], dmb)
            dws_ref[hd] += _dot_nt(dmb, vnb_s[:, cs])

        def phase5(g, _):
            rows = _rows(g)
            dvn = dvn_s[rows, :]
            vh = vh_s[rows, :]
            acc_s[V_LN_G] += dvn * vh
            acc_s[V_LN_B] += dvn
            dvh = dvn * lng
            rs = rs_ref[rows, 0:1]
            dvg = rs * (dvh - _mean_last(dvh) - vh * _mean_last(dvh * vh))
            dz_s[rows, D_BR:2 * D_BR] = dvg * _gelu(z_ref[rows, D_BR:2 * D_BR], with_grad=True)[1]
            return 0

        _loop(N_GROUP, phase5, 0)

        ba, bx = ba_ref[...], bx_ref[...]
        sp8 = LRU_C * _softplus(-lam_ref[...])
        row0 = _row0_mask(rid)

        def phase6(k, carry):
            cg, ca = carry
            g = N_GROUP - 1 - k
            rows = _rows(g)
            bias0 = _row0_bias(jnp.logical_and(first_chunk, g == 0), row0)
            r, i, a, m2 = _lru_gates(pre_s[rows, 0:D_BR], pre_s[rows, D_BR:2 * D_BR], ba, bx, sp8, bias0)
            a_nx = jnp.where(rid < ROWS - 1, pltpu.roll(a, ROWS - 1, 0), ca)
            aa, bb = a_nx, dho_s[rows, :]
            for d in (1, 2, 4):
                a_sh = jnp.where(rid < ROWS - d, pltpu.roll(aa, ROWS - d, 0), 1.0)
                b_sh = jnp.where(rid < ROWS - d, pltpu.roll(bb, ROWS - d, 0), 0.0)
                bb = aa * b_sh + bb
                aa = aa * a_sh
            gg = bb + aa * cg
            hh = h_ref[rows, :]
            hprev = _shift_down(hh, prev_rows(h_ref, slice(None), g, h_halo), 1, rid)
            xc = xc_s[rows, :]
            gx = gg * xc
            dla = gg * hprev * a - gx * i * (a * a) * lax.rsqrt(m2)
            acc_s[V_LAM] += -(dla * r)
            dpa = -(dla * sp8) * r * (1.0 - r)
            mi = jnp.sqrt(m2) * i
            dpx = gx * mi * (1.0 - i)
            acc_s[V_B_A] += dpa
            acc_s[V_B_X] += dpx
            dpre_s[rows, 0:D_BR] = dpa
            dpre_s[rows, D_BR:2 * D_BR] = dpx
            dxc_s[rows, :] = gg * mi
            return _bcast_row(gg, 0), _bcast_row(a, 0)

        cg, ca = _loop(N_GROUP, phase6, (cg_s[...], ca_s[...]))
        cg_s[...] = cg
        ca_s[...] = ca

        for hd in range(N_HEAD):
            cs = slice(hd * HEAD, (hd + 1) * HEAD)
            dpre = jnp.concatenate([dpre_s[:, cs], dpre_s[:, D_BR + hd * HEAD:D_BR + (hd + 1) * HEAD]], axis=1).astype(BF16)
            dxc_s[:, cs] += _dot(dpre, waxt_ref[hd])
            dwax_ref[hd] += _dot_tn(xcb_ref[:, cs], dpre)

        def phase8(k, nxt):
            g = N_GROUP - 1 - k
            rows = _rows(g)
            dxc = dxc_s[rows, :]
            acc_s[V_CONV_B] += dxc
            xb = z_ref[rows, 3 * D_BR:4 * D_BR]
            dxb = cw_ref[3:4, :] * dxc
            acc_s[V_CONV_W + 3] += dxc * xb
            for j in range(1, CONV_W):
                later = _shift_up(dxc, nxt, j, rid)
                dxb = dxb + cw_ref[3 - j:4 - j, :] * later
                acc_s[V_CONV_W + 3 - j] += later * xb
            dz_s[rows, 3 * D_BR:4 * D_BR] = dxb
            return dxc

        dxchalo_s[...] = _loop(N_GROUP, phase8, dxchalo_s[...])
        dz_ref[...] = dz_s[...].astype(BF16)

        if last is not None:
            @pl.when(last)
            def _():
                for v in range(N_VEC):
                    vecs_ref[v:v + 1, :] = jnp.sum(acc_s[v], axis=0, keepdims=True)
                lam = lam_ref[...]
                vecs_ref[V_LAM:V_LAM + 1, :] = vecs_ref[V_LAM:V_LAM + 1, :] * (-LRU_C * _sig(-lam))
                tril = (lax.broadcasted_iota(jnp.int32, (HEAD, HEAD), 0)
                        >= lax.broadcasted_iota(jnp.int32, (HEAD, HEAD), 1))
                ones = jnp.ones((ROWS, HEAD), BF16)
                for hd in range(N_HEAD):
                    cs = slice(hd * HEAD, (hd + 1) * HEAD)
                    dws_ref[hd] = jnp.where(tril, dws_ref[hd], 0.0)
                    blk = accdm_s[:, cs]
                    hi = blk.astype(BF16)
                    lo = (blk - hi.astype(F32)).astype(BF16)
                    dbs_ref[hd:hd + 1, :] = (_dot_nt(ones, hi) + _dot_nt(ones, lo))[0:1, :]
                ex.wait(ex_in, ex_out, ex_sems)

    vec = pl.BlockSpec((1, D_BR), lambda i: (0, 0))
    n_step = n_chunk // MIX_SUB
    rows_blk = MIX_SUB * CHUNK
    rev = lambda i: (n_step - 1 - i, 0)
    halo = lambda col: (lambda i: (jnp.maximum((n_step - 1 - i) * MIX_SUB * halo_blocks - 1, 0), col))
    full3 = lambda a, b, c: pl.BlockSpec((a, b, c), lambda i: (0, 0, 0))
    big = lambda w: pltpu.VMEM((CHUNK, w), F32)
    res = pl.pallas_call(
        body, name="mix_bwd", grid=(n_step,),
        in_specs=[pl.BlockSpec((rows_blk, D_IN), rev), pl.BlockSpec((rows_blk, 2 * D_BR), rev),
                  pl.BlockSpec((rows_blk, D_BR), rev),
                  pl.BlockSpec((ROWS, D_BR), halo(0)), pl.BlockSpec((rows_blk, D_BR), rev),
                  pl.BlockSpec((rows_blk, D_BR), rev),
                  pl.BlockSpec((rows_blk, HEAD), rev), vec, vec,
                  full3(N_HEAD, HEAD, HEAD), full3(N_HEAD, HEAD, HEAD),
                  pl.BlockSpec((CHUNK, D_BR), lambda i: (0, 0)), pl.BlockSpec((ROWS, D_BR), lambda i: (0, 0)), vec,
                  full3(N_HEAD, HEAD, 2 * HEAD), full3(N_HEAD, 2 * HEAD, HEAD), vec, vec, vec, vec, vec]
        + [ANY_SPEC] * ex.n,
        out_specs=[pl.BlockSpec((rows_blk, D_IN), rev), pl.BlockSpec((N_VEC, D_BR), lambda i: (0, 0)),
                   full3(N_HEAD, HEAD, HEAD), full3(N_HEAD, HEAD, 2 * HEAD),
                   pl.BlockSpec((N_HEAD, HEAD), lambda i: (0, 0))] + [ANY_SPEC] * ex.n,
        out_shape=[SDS((t_len, D_IN), BF16), SDS((N_VEC, D_BR), F32), SDS((N_HEAD, HEAD, HEAD), F32),
                   SDS((N_HEAD, HEAD, 2 * HEAD), F32), SDS((N_HEAD, HEAD), F32)] + ex.out_shape,
        scratch_shapes=[pltpu.VMEM((CHUNK, D_BR), BF16), big(D_BR), big(D_BR), big(D_BR), big(2 * D_BR), big(D_BR),
                        big(D_BR), big(D_BR), big(D_BR), big(2 * D_BR), big(D_IN),
                        pltpu.VMEM((N_VEC, ROWS, D_BR), F32), big(D_BR),
                        pltpu.VMEM((ROWS, D_BR), F32), pltpu.VMEM((ROWS, D_BR), F32), pltpu.VMEM((ROWS, D_BR), F32)]
        + ex.scratch,
        compiler_params=_params(("arbitrary",), 48),
    )(z, dy, h, h, vhb, xcb, rs, ln_g, ln_b, wm, wm_t, bias, cw, cb, wax, wax_t, ba, bx, lam, goa, gob, *ex_arrs)
    return res[:n_out], res[n_out:]


def _in_bwd(dz, w_in_g, x, dh1, pre_g, tm=256):
    t_len = x.shape[0]
    n_tile = t_len // tm

    def body(dz_ref, w_hbm, x_ref, dh1_ref, g_ref, gx_ref, dg_ref, w_s, t_even, t_odd, dg_s, w_sems):
        i = pl.program_id(0)

        @pl.when(i == 0)
        def _():
            loads = [pltpu.make_async_copy(w_hbm.at[s], w_s.at[:, s * W_IN_SHARD:(s + 1) * W_IN_SHARD], w_sems.at[s])
                     for s in range(N_DEV)]
            for cp in loads:
                cp.start()
            dg_s[...] = jnp.zeros_like(dg_s)
            for s, cp in enumerate(loads):
                cp.wait()
                cols = slice(s * W_IN_SHARD, (s + 1) * W_IN_SHARD)
                part = _dot_nt(dz_ref[:, cols], w_s[:, cols])
                t_even[...] = part if s == 0 else t_even[...] + part

        def step(t_new, t_old):
            g = g_ref[...]
            acc = dg_s[...]
            for q in range(tm // TILE_ROWS):
                rows = slice(q * TILE_ROWS, (q + 1) * TILE_ROWS)
                xv = x_ref[rows, :]
                r = lax.rsqrt(_mean_last(xv * xv) + EPS)
                xh = xv * r
                dhn = t_old[rows, :]
                dg = dhn * g
                gx_ref[rows, :] = dh1_ref[rows, :] + r * (dg - xh * _mean_last(dg * xh))
                acc = acc + _fold_rows(dhn * xh)
            dg_s[...] = acc
            t_new[...] = _dot_nt(dz_ref[...], w_s[...])

        @pl.when((i % 2 == 0) & (i > 0))
        def _():
            step(t_even, t_odd)

        @pl.when(i % 2 == 1)
        def _():
            step(t_odd, t_even)

        @pl.when(i == n_tile)
        def _():
            dg_ref[...] = jnp.sum(dg_s[...], axis=0, keepdims=True)

    matmul_tile = lambda i: (jnp.minimum(i, n_tile - 1), 0)
    rows_tile = lambda i: (jnp.maximum(i - 1, 0), 0)
    res = pl.pallas_call(
        body, name="in_bwd", grid=(n_tile + 1,),
        in_specs=[pl.BlockSpec((tm, D_IN), matmul_tile), ANY_SPEC, pl.BlockSpec((tm, D_MODEL), rows_tile),
                  pl.BlockSpec((tm, D_MODEL), rows_tile), pl.BlockSpec((1, D_MODEL), lambda i: (0, 0))],
        out_specs=[pl.BlockSpec((tm, D_MODEL), rows_tile), pl.BlockSpec((1, D_MODEL), lambda i: (0, 0))],
        out_shape=[SDS((t_len, D_MODEL), F32), SDS((1, D_MODEL), F32)],
        scratch_shapes=[pltpu.VMEM((D_MODEL, D_IN), BF16), pltpu.VMEM((tm, D_MODEL), F32), pltpu.VMEM((tm, D_MODEL), F32),
                        pltpu.VMEM((ROWS, D_MODEL), F32), pltpu.SemaphoreType.DMA((N_DEV,))],
        compiler_params=_params(("arbitrary",), 54),
    )(dz, w_in_g, x, dh1, pre_g)
    return res[0], res[1]


def _grad_w(a, b, bn, shard_major, name, tk=1024, ex_arrs=(), ex_scatter=()):
    t_len, m = a.shape
    n = b.shape[1]
    n_j, n_k = n // bn, t_len // tk
    ex = _Exchange(ex_arrs, ex_scatter)

    def body(a_ref, b_ref, *refs):
        ex_in, o_ref, ex_out = refs[:ex.n], refs[ex.n], refs[ex.n + 1:2 * ex.n + 1]
        acc_s, ex_sems = refs[2 * ex.n + 1], refs[2 * ex.n + 2:]
        j, k = pl.program_id(0), pl.program_id(1)
        if ex.n:
            @pl.when(jnp.logical_and(j == 0, k == 0))
            def _():
                ex.start(ex_in, ex_out, ex_sems)

        @pl.when(k == 0)
        def _():
            acc_s[...] = jnp.zeros_like(acc_s)

        acc_s[...] += _dot_tn(a_ref[...], b_ref[...])

        @pl.when(k == n_k - 1)
        def _():
            o_ref[...] = acc_s[...].astype(BF16)

        if ex.n:
            @pl.when(jnp.logical_and(j == n_j - 1, k == n_k - 1))
            def _():
                ex.wait(ex_in, ex_out, ex_sems)

    if shard_major:
        out_spec, out_shape = pl.BlockSpec((None, m, bn), lambda j, k: (j, 0, 0)), SDS((n_j, m, bn), BF16)
    else:
        out_spec, out_shape = pl.BlockSpec((m, bn), lambda j, k: (0, j)), SDS((m, n), BF16)
    res = pl.pallas_call(
        body, name=name, grid=(n_j, n_k),
        in_specs=[pl.BlockSpec((tk, m), lambda j, k: (k, 0)), pl.BlockSpec((tk, bn), lambda j, k: (k, j))]
        + [ANY_SPEC] * ex.n,
        out_specs=[out_spec] + [ANY_SPEC] * ex.n, out_shape=[out_shape] + ex.out_shape,
        scratch_shapes=[pltpu.VMEM((m, bn), F32)] + (ex.scratch if ex.n else []),
        compiler_params=_params(("arbitrary", "arbitrary"), 40),
    )(a, b, *ex_arrs)
    return res[0], res[1:]


RS_CHIPS = (6, 2, 4, 0)
RS_SLOTS = (0, 1, 2, 4, 6)


def _grad_w_in_pairs(hn, dz, ex_arrs, ex_scatter, tk=1024):
    t_len = hn.shape[0]
    n_k = t_len // tk
    n_ph = len(RS_CHIPS)
    ex = _Exchange(ex_arrs, ex_scatter)
    me_out = 4 * lax.axis_index("x") + 2 * lax.axis_index("y") + lax.axis_index("c")
    order = jnp.stack([(me_out ^ chip) // 2 for chip in RS_CHIPS]).astype(jnp.int32)
    slots = jnp.stack([me_out ^ k for k in RS_SLOTS]).astype(jnp.int32)
    shard = W_IN_SHARD

    def body(order_ref, a_ref, b_ref, *refs):
        ex_in, parts_hbm, ex_out = refs[:ex.n], refs[ex.n], refs[ex.n + 1:2 * ex.n + 1]
        (acc_s, tb_s, stage_s, rx_s, d2d_send, d2d_recv, ici_send, ici_recv, sib_sems,
         loc_sem) = refs[2 * ex.n + 1:2 * ex.n + 11]
        ex_sems = refs[2 * ex.n + 11:]
        j, k = pl.program_id(0), pl.program_id(1)
        x, y, c, me = _mesh_place()
        sib = _peer(x, y, c, SIBLING)[0]

        def to_sibling(p):
            return _remote(stage_s.at[0], rx_s.at[p % 2], d2d_send.at[p], d2d_recv.at[p], sib)

        def over_ici(p):
            dev = _peer(x, y, c, RS_CHIPS[p])[0]
            return _remote(stage_s.at[1], parts_hbm.at[me], ici_send.at[p], ici_recv.at[p], dev)

        def own_chip():
            return (_remote(stage_s.at[0], parts_hbm.at[me], sib_sems.at[0], sib_sems.at[1], sib),
                    pltpu.make_async_copy(stage_s.at[1], parts_hbm.at[me], loc_sem.at[0]))

        @pl.when(jnp.logical_and(j == 0, k == 0))
        def _():
            ex.start(ex_in, ex_out, ex_sems)

        for p in range(n_ph - 1):
            for core in (0, 1):
                @pl.when(jnp.logical_and(jnp.logical_and(j == p + 1, k == 0), c == core))
                def _(p=p, core=core):
                    to_sibling(p).wait_recv()
                    if p >= 1:
                        over_ici(p - 1).wait_send()
                    mine = acc_s[:, core * shard:(core + 1) * shard]
                    stage_s[1] = (mine + rx_s[p % 2].astype(F32)).astype(BF16)
                    over_ici(p).start()

        @pl.when(k == 0)
        def _():
            acc_s[...] = jnp.zeros_like(acc_s)

        a = a_ref[...]
        acc_s[:, 0:W_BODY] += _dot_tn(a, b_ref[:, 0:W_BODY])
        acc_s[:, shard:shard + W_BODY] += _dot_tn(a, b_ref[:, shard:shard + W_BODY])
        tb_s[:, 0:W_TAIL] = b_ref[:, W_BODY:shard]
        tb_s[:, W_TAIL:2 * W_TAIL] = b_ref[:, shard + W_BODY:2 * shard]
        tails = _dot_tn(a, tb_s[...])
        acc_s[:, W_BODY:shard] += tails[:, 0:W_TAIL]
        acc_s[:, shard + W_BODY:2 * shard] += tails[:, W_TAIL:2 * W_TAIL]

        for p in range(n_ph):
            for core in (0, 1):
                @pl.when(jnp.logical_and(jnp.logical_and(j == p, k == n_k - 1), c == core))
                def _(p=p, core=core):
                    same = acc_s[:, core * shard:(core + 1) * shard]
                    other = acc_s[:, (1 - core) * shard:(2 - core) * shard]
                    if p >= 1:
                        to_sibling(p - 1).wait_send()
                    stage_s[0] = other.astype(BF16)
                    if p < n_ph - 1:
                        to_sibling(p).start()
                    else:
                        over_ici(n_ph - 2).wait_send()
                        stage_s[1] = same.astype(BF16)
                        for cp in own_chip():
                            cp.start()

        @pl.when(jnp.logical_and(j == n_ph - 1, k == n_k - 1))
        def _():
            to_sib, local = own_chip()
            to_sib.wait_send()
            local.wait()
            _remote(stage_s.at[0], parts_hbm.at[_peer(x, y, c, SIBLING)[1]], sib_sems.at[0], sib_sems.at[1], sib).wait_recv()
            for p in range(n_ph - 1):
                dev, lin = _peer(x, y, c, RS_CHIPS[p])
                _remote(stage_s.at[0], parts_hbm.at[lin], ici_send.at[p], ici_recv.at[p], dev).wait_recv()
            ex.wait(ex_in, ex_out, ex_sems)

    dma = lambda n: pltpu.SemaphoreType.DMA((n,))
    grid_spec = pltpu.PrefetchScalarGridSpec(
        num_scalar_prefetch=1, grid=(n_ph, n_k),
        in_specs=[pl.BlockSpec((tk, D_MODEL), lambda j, k, order: (k, 0)),
                  pl.BlockSpec((tk, 2 * shard), lambda j, k, order: (k, order[j]))] + [ANY_SPEC] * ex.n,
        out_specs=[ANY_SPEC] * (1 + ex.n),
        scratch_shapes=[pltpu.VMEM((D_MODEL, 2 * shard), F32), pltpu.VMEM((tk, 2 * W_TAIL), BF16),
                        pltpu.VMEM((2, D_MODEL, shard), BF16),
                        pltpu.VMEM((2, D_MODEL, shard), BF16), dma(n_ph - 1), dma(n_ph - 1), dma(n_ph - 1),
                        dma(n_ph - 1), dma(2), dma(1)] + ex.scratch)
    res = pl.pallas_call(
        body, name="grad_w_in", grid_spec=grid_spec,
        out_shape=[SDS((N_DEV, D_MODEL, shard), BF16)] + ex.out_shape,
        compiler_params=_params(("arbitrary", "arbitrary"), 54),
    )(order, hn, dz, *ex_arrs)
    return res[0], slots, res[1:]


def _sum_parts(parts, name):
    def body(p_ref, o_ref):
        g = p_ref[0].astype(F32)
        for s in range(1, parts.shape[0]):
            g = g + p_ref[s].astype(F32)
        o_ref[...] = g

    return pl.pallas_call(body, name=name, out_shape=SDS(parts.shape[1:], F32))(parts)


def _adamw_math(g, w_ref, m_ref, v_ref, g_ref, d_ref, nm_ref, nv_ref):
    c1 = 1.0 - ADAM_B1 ** ADAM_STEP
    c2 = 1.0 - ADAM_B2 ** ADAM_STEP
    g_ref[...] = g
    nm = ADAM_B1 * m_ref[...] + (1.0 - ADAM_B1) * g
    nv = ADAM_B2 * v_ref[...] + (1.0 - ADAM_B2) * (g * g)
    nm_ref[...] = nm
    nv_ref[...] = nv
    d_ref[...] = -ADAM_LR * ((nm / c1) / (jnp.sqrt(nv / c2) + ADAM_EPS) + ADAM_WD * w_ref[...])


def _adamw(parts, w, m, v, name, tr):
    rows, cols = w.shape
    n_parts = parts.shape[0]

    def body(p_ref, *refs):
        g = p_ref[0].astype(F32)
        for s in range(1, n_parts):
            g = g + p_ref[s].astype(F32)
        _adamw_math(g, *refs)

    tile = pl.BlockSpec((tr, cols), lambda i: (i, 0))
    return pl.pallas_call(
        body, name=name, grid=(rows // tr,),
        in_specs=[pl.BlockSpec((n_parts, tr, cols), lambda i: (0, i, 0)), tile, tile, tile],
        out_specs=[tile] * 4, out_shape=[SDS((rows, cols), F32)] * 4,
        compiler_params=_params(("arbitrary",), 40),
    )(parts, w, m, v)


def _adamw_unpacked(grads, triples, name):
    n = len(triples)
    n_rows = [t[0].shape[0] for t in triples]

    def body(g_ref, *refs):
        ins, outs = refs[:3 * n], refs[3 * n:]
        row = 0
        for i in range(n):
            _adamw_math(g_ref[row:row + n_rows[i], :], *ins[3 * i:3 * i + 3], *outs[4 * i:4 * i + 4])
            row += n_rows[i]
        outs[4 * n][...] = g_ref[row:row + ROWS, :]

    out_shape = [SDS((r, LANES), F32) for r in n_rows for _ in range(4)] + [SDS((ROWS, LANES), F32)]
    return pl.pallas_call(
        body, name=name, out_shape=out_shape,
        compiler_params=pltpu.CompilerParams(vmem_limit_bytes=40 * MIB),
    )(grads, *[a for t in triples for a in t])


def _adamw_slots(parts, slots, w, m, v, name, tr):
    rows, cols = w.shape
    n_slots = slots.shape[0]

    def body(slots_ref, *refs):
        g = refs[0][...].astype(F32)
        for s in range(1, n_slots):
            g = g + refs[s][...].astype(F32)
        _adamw_math(g, *refs[n_slots:])

    tile = pl.BlockSpec((tr, cols), lambda i, slots: (i, 0))
    part = lambda s: pl.BlockSpec((None, tr, cols), lambda i, slots: (slots[s], i, 0))
    grid_spec = pltpu.PrefetchScalarGridSpec(
        num_scalar_prefetch=1, grid=(rows // tr,),
        in_specs=[part(s) for s in range(n_slots)] + [tile, tile, tile], out_specs=[tile] * 4)
    return pl.pallas_call(
        body, name=name, grid_spec=grid_spec, out_shape=[SDS((rows, cols), F32)] * 4,
        compiler_params=_params(("arbitrary",), 40),
    )(slots, *([parts] * n_slots), w, m, v)


PACKED = ("gmlp_ln_g", "gmlp_ln_b", "gmlp_ws", "gmlp_bs", "conv_b", "w_a", "b_a", "w_x", "b_x", "lam", "gmlp_out_g",
          "lru_out_g", "post_g")
WEIGHTS = ("pre_g", "w_in", "gmlp_ln_g", "gmlp_ln_b", "gmlp_ws", "gmlp_bs", "conv_w", "conv_b", "w_a", "b_a", "w_x",
           "b_x", "lam", "gmlp_out_g", "lru_out_g", "w_out", "post_g", "w_pe", "w_pg")
LANES = 128


PACK_ROWS = 3200


def _pack(parts):
    rows = [p.reshape(-1, LANES) for p in parts]
    used = sum(r.shape[0] for r in rows)
    return jnp.concatenate(rows + [jnp.zeros((PACK_ROWS - used, LANES), F32)], axis=0)


def _pad_rows(a, rows):
    return jnp.concatenate([a, jnp.zeros((rows - a.shape[0],) + a.shape[1:], a.dtype)], axis=0)


def kernel(x, p, pre_g, w_in, gmlp_ln_g, gmlp_ln_b, gmlp_ws, gmlp_bs, conv_w, conv_b, w_a, b_a, w_x, b_x, lam, gmlp_out_g, lru_out_g, w_out, post_g, w_pe, w_pg, loss_target, m_pre_g, m_w_in, m_gmlp_ln_g, m_gmlp_ln_b, m_gmlp_ws, m_gmlp_bs, m_conv_w, m_conv_b, m_w_a, m_b_a, m_w_x, m_b_x, m_lam, m_gmlp_out_g, m_lru_out_g, m_w_out, m_post_g, m_w_pe, m_w_pg, v_pre_g, v_w_in, v_gmlp_ln_g, v_gmlp_ln_b, v_gmlp_ws, v_gmlp_bs, v_conv_w, v_conv_b, v_w_a, v_b_a, v_w_x, v_b_x, v_lam, v_gmlp_out_g, v_lru_out_g, v_w_out, v_post_g, v_w_pe, v_w_pg):
    args = dict(locals())
    weights = {n: args[n] for n in WEIGHTS}
    m_in = {n: args["m_" + n] for n in WEIGHTS}
    v_in = {n: args["v_" + n] for n in WEIGHTS}
    sm = {n: weights[n][0] for n in PACKED}
    shard_rows = D_MODEL // N_DEV
    xs, ps, tgt = x[0], p[0, 0], loss_target[0]

    vec = lambda a: a.reshape(1, -1)
    tril = jnp.tril(jnp.ones((CHUNK, CHUNK), dtype=bool))
    wm32 = jnp.where(tril[None], sm["gmlp_ws"], 0.0)
    wm, wm_t = wm32.astype(BF16), jnp.swapaxes(wm32, 1, 2).astype(BF16)
    bias = jnp.repeat(sm["gmlp_bs"].T, HEAD, axis=1)
    wax32 = jnp.concatenate([sm["w_a"], sm["w_x"]], axis=2)
    wax, wax_t = wax32.astype(BF16), jnp.swapaxes(wax32, 1, 2).astype(BF16)
    ln_g, ln_b = vec(sm["gmlp_ln_g"]), vec(sm["gmlp_ln_b"])
    post_g_v = vec(sm["post_g"])

    hn = _pre_norm(xs, pre_g)
    cw_shard = _pad_rows(conv_w.reshape(CONV_W, HEAD), ROWS)
    z, w_in_g, (cw_g,) = _in_proj(hn, w_in[0].astype(BF16), [cw_shard])
    cw_full = jnp.transpose(cw_g[:, :CONV_W, :], (1, 0, 2)).reshape(CONV_W, D_BR)
    mixer_consts = dict(cw=_pad_rows(cw_full, ROWS), cb=vec(sm["conv_b"]), ba=vec(sm["b_a"]), bx=vec(sm["b_x"]),
                        lam=vec(sm["lam"]), goa=vec(sm["gmlp_out_g"]), gob=vec(sm["lru_out_g"]))
    (y, h, vhb, xcb, v_rs), (w_out_g, w_pe_g, w_pg_g) = _mix_fwd(
        z, ln_g, ln_b, wm, bias, wax=wax, **mixer_consts,
        ex_arrs=[w_out[0].astype(BF16), w_pe[0].astype(BF16), w_pg[0].astype(BF16)], ex_scatter=[False, False, False])
    w_out_f, w_pg_f = w_out_g.reshape(D_MODEL, D_MODEL), w_pg_g.reshape(D_MODEL, D_MODEL)
    h1, ob = _out_proj(y, xs, w_out_f, post_g_v)
    dh2, dgl, h1b, loss_part, d_w_pe = _ple_loss(h1, ps, tgt, w_pg_f, w_pe_g)

    dh1, do, dy, d_post_g = _tail_bwd(dh2, dgl, ob, w_pg_f, w_out_f, post_g_v)
    d_w_out, _ = _grad_w(y, do, 1024, False, "grad_w_out")
    d_w_pg, _ = _grad_w(h1b, dgl, 1024, False, "grad_w_pg")
    (dz, vecs, d_ws, d_wax, d_bs), (parts_out, parts_pg, parts_pe) = _mix_bwd(
        z, dy, h, vhb, xcb, v_rs, ln_g, ln_b, wm, wm_t, bias, wax=wax, wax_t=wax_t, **mixer_consts,
        ex_arrs=[d_w_out.reshape(N_DEV, shard_rows, D_MODEL), d_w_pg.reshape(N_DEV, shard_rows, D_MODEL), d_w_pe],
        ex_scatter=[True, True, True])

    small = {"gmlp_ln_g": vecs[V_LN_G], "gmlp_ln_b": vecs[V_LN_B], "gmlp_ws": d_ws, "gmlp_bs": d_bs,
             "conv_b": vecs[V_CONV_B], "w_a": d_wax[:, :, :HEAD], "b_a": vecs[V_B_A], "w_x": d_wax[:, :, HEAD:],
             "b_x": vecs[V_B_X], "lam": vecs[V_LAM], "gmlp_out_g": vecs[V_GOUT_A], "lru_out_g": vecs[V_GOUT_B],
             "post_g": d_post_g}
    small_part = _pack([small[n] for n in PACKED] + [loss_part]).reshape(N_DEV, PACK_ROWS // N_DEV, LANES)
    d_cw_blocks = jnp.transpose(vecs[V_CONV_W:V_CONV_W + CONV_W].reshape(CONV_W, N_DEV, HEAD), (1, 0, 2))
    d_cw_blocks = jnp.concatenate([d_cw_blocks, jnp.zeros((N_DEV, ROWS - CONV_W, HEAD), F32)], axis=1)
    parts_in, slots_in, (small_blocks, parts_cw) = _grad_w_in_pairs(
        hn, dz, ex_arrs=[small_part, d_cw_blocks], ex_scatter=[True, True])
    small_sum = _sum_parts(small_blocks, "sum_small")
    grad_x, d_pre_g = _in_bwd(dz, w_in_g, xs, dh1, pre_g)
    pre_rows = D_MODEL // LANES
    small_all, parts_pre = _exchange([small_sum, d_pre_g.reshape(pre_rows, LANES)], False, "gather_small_grads")

    pad_cw = lambda a: _pad_rows(a.reshape(CONV_W, HEAD), ROWS)
    flat = lambda a: a.reshape(pre_rows, LANES)
    outs = {
        "w_in": _adamw_slots(parts_in, slots_in, w_in[0], m_w_in[0], v_w_in[0], "adamw_w_in", 256),
        "w_out": _adamw(parts_out, w_out[0], m_w_out[0], v_w_out[0], "adamw_w_out", 128),
        "w_pe": _adamw(parts_pe, w_pe[0], m_w_pe[0], v_w_pe[0], "adamw_w_pe", 256),
        "w_pg": _adamw(parts_pg, w_pg[0], m_w_pg[0], v_w_pg[0], "adamw_w_pg", 128),
        "conv_w": [a[:CONV_W] for a in
                   _adamw(parts_cw, pad_cw(conv_w), pad_cw(m_conv_w), pad_cw(v_conv_w), "adamw_conv_w", ROWS)],
        "pre_g": _adamw(parts_pre, flat(pre_g), flat(m_pre_g), flat(v_pre_g), "adamw_pre_g", pre_rows),
    }
    as_rows = lambda a: a.reshape(-1, LANES)
    small_res = _adamw_unpacked(small_all.reshape(PACK_ROWS, LANES),
                                [(as_rows(weights[n]), as_rows(m_in[n]), as_rows(v_in[n])) for n in PACKED], "adamw_small")
    for i, n in enumerate(PACKED):
        outs[n] = small_res[4 * i:4 * i + 4]
    loss = small_res[-1][0, 0]

    result = [loss, grad_x[None]]
    for q in range(4):
        result += [outs[n][q].reshape(weights[n].shape) for n in WEIGHTS]
    return tuple(result)
```

```python
import jax
import jax.numpy as jnp
from jax import lax
from jax.experimental import pallas as pl
from jax.experimental.pallas import tpu as pltpu

F32 = jnp.float32
BF16 = jnp.bfloat16
SDS = jax.ShapeDtypeStruct

D_MODEL = 2048
D_BR = 1024
D_IN = 5 * D_BR
D_PLE = 256
N_HEAD = 8
HEAD = 128
CHUNK = 128
ROWS = 8
N_GROUP = CHUNK // ROWS
MIX_SUB = 2
N_DEV = 8
W_IN_SHARD = D_IN // N_DEV
EPS = 1e-6
LRU_C = 8.0
CONV_W = 4
MIB = 1 << 20

ADAM_LR, ADAM_B1, ADAM_B2, ADAM_EPS, ADAM_WD, ADAM_STEP = 0.001, 0.9, 0.999, 1e-08, 0.01, 10

_GELU_C = 0.7978845608028654
_GELU_A = 0.044715

V_LN_G, V_LN_B, V_CONV_B, V_B_A, V_B_X, V_LAM, V_GOUT_A, V_GOUT_B, V_CONV_W = 0, 1, 2, 3, 4, 5, 6, 7, 8
N_VEC = 16


def _params(sem, vmem_mib):
    return pltpu.CompilerParams(dimension_semantics=sem, vmem_limit_bytes=int(vmem_mib * MIB))


def _sig(x):
    return 0.5 * jnp.tanh(0.5 * x) + 0.5


def _gelu(x, with_grad=False):
    sq = x * x
    t = jnp.tanh(x * (_GELU_C + (_GELU_C * _GELU_A) * sq))
    half, one_t = 0.5 * x, 1.0 + t
    if not with_grad:
        return half * one_t
    grad = 0.5 * one_t + half * ((1.0 - t) * one_t) * (_GELU_C + (3.0 * _GELU_C * _GELU_A) * sq)
    return half * one_t, grad


def _silu_grad(s, xs):
    return s + xs * (1.0 - s)


def _neg_expm1(y, exp_y):
    series = -y * (1.0 + y * (0.5 + y * (1.0 / 6.0)))
    return jnp.where(y > -0.01, series, 1.0 - exp_y)


def _softplus(x):
    return jnp.maximum(x, 0.0) + jnp.log(1.0 + jnp.exp(-jnp.abs(x)))


def _row_ids(width):
    return lax.broadcasted_iota(jnp.int32, (ROWS, width), 0)


def _shift_down(cur, prev, k, rid):
    return jnp.where(rid >= k, pltpu.roll(cur, k, 0), pltpu.roll(prev, k, 0))


def _shift_up(cur, nxt, k, rid):
    return jnp.where(rid < ROWS - k, pltpu.roll(cur, ROWS - k, 0), pltpu.roll(nxt, ROWS - k, 0))


def _mean_last(x):
    return jnp.mean(x, axis=-1, keepdims=True)


def _rows(g):
    return pl.ds(pl.multiple_of(g * ROWS, ROWS), ROWS)


TILE_ROWS = 16


def _tile_rows(q):
    return pl.ds(pl.multiple_of(q * TILE_ROWS, TILE_ROWS), TILE_ROWS)


UNROLL = 4
TILE_UNROLL = 8


def _loop(n, body, init, unroll=UNROLL):
    def wide(i, carry):
        for u in range(unroll):
            carry = body(i * unroll + u, carry)
        return carry

    return lax.fori_loop(0, n // unroll, wide, init)


def _fold_rows(x):
    return x[0:ROWS, :] + x[ROWS:TILE_ROWS, :]


def _bcast_row(x, r):
    return jnp.broadcast_to(x[r:r + 1, :], x.shape)


def _dot(a, b):
    return jnp.dot(a, b, preferred_element_type=F32)


def _dot_nt(a, b):
    return lax.dot_general(a, b, (((1,), (1,)), ((), ())), preferred_element_type=F32)


def _dot_tn(a, b):
    return lax.dot_general(a, b, (((0,), (0,)), ((), ())), preferred_element_type=F32)


def _mesh_place():
    x, y, c = lax.axis_index("x"), lax.axis_index("y"), lax.axis_index("c")
    return x, y, c, 4 * x + 2 * y + c


def _peer(x, y, c, k):
    px = 1 - x if k & 4 else x
    py = 1 - y if k & 2 else y
    pc = 1 - c if k & 1 else c
    return (px, py, pc), 4 * px + 2 * py + pc


def _remote(src, dst, send_sem, recv_sem, dev):
    return pltpu.make_async_remote_copy(src_ref=src, dst_ref=dst, send_sem=send_sem, recv_sem=recv_sem, device_id=dev,
                                        device_id_type=pl.DeviceIdType.MESH)


ANY_SPEC = pl.BlockSpec(memory_space=pl.ANY)


class _Exchange:
    def __init__(self, arrs, scatter):
        self.n = len(arrs)
        self.scatter = tuple(scatter)
        self.out_shape = [SDS(a.shape if s else (N_DEV,) + a.shape, a.dtype) for a, s in zip(arrs, scatter)]
        self.scratch = [pltpu.SemaphoreType.DMA((self.n * N_DEV,)), pltpu.SemaphoreType.DMA((self.n * N_DEV,)),
                        pltpu.SemaphoreType.DMA((self.n,))]

    def _copies(self, ins, outs, sems):
        send_sems, recv_sems, local_sems = sems
        x, y, c, me = _mesh_place()
        local, sends, recvs = [], [], []
        for a in range(self.n):
            src = ins[a].at[me] if self.scatter[a] else ins[a]
            local.append(pltpu.make_async_copy(src, outs[a].at[me], local_sems.at[a]))
        for k in range(1, N_DEV):
            dev, lin = _peer(x, y, c, k)
            for a in range(self.n):
                src = ins[a].at[lin] if self.scatter[a] else ins[a]
                pair = (send_sems.at[a * N_DEV + k], recv_sems.at[a * N_DEV + k], dev)
                sends.append(_remote(src, outs[a].at[me], *pair))
                recvs.append(_remote(src, outs[a].at[lin], *pair))
        return local, sends, recvs

    def start(self, ins, outs, sems):
        local, sends, _ = self._copies(ins, outs, sems)
        for cp in local + sends:
            cp.start()

    def wait(self, ins, outs, sems):
        local, sends, recvs = self._copies(ins, outs, sems)
        for cp in recvs:
            cp.wait_recv()
        for cp in sends:
            cp.wait_send()
        for cp in local:
            cp.wait()


def _exchange(arrs, scatter, name):
    ex = _Exchange(arrs, [scatter] * len(arrs))
    n = ex.n

    def body(*refs):
        ins, outs, sems = refs[:n], refs[n:2 * n], refs[2 * n:]
        ex.start(ins, outs, sems)
        ex.wait(ins, outs, sems)

    return pl.pallas_call(
        body, name=name, out_shape=ex.out_shape, in_specs=[ANY_SPEC] * n, out_specs=[ANY_SPEC] * n,
        scratch_shapes=ex.scratch,
    )(*arrs)


def _pre_norm(x, pre_g, tm=512):
    t_len = x.shape[0]

    def body(x_ref, g_ref, hn_ref):
        g = g_ref[...]

        def rows_body(q, _):
            rows = _tile_rows(q)
            xv = x_ref[rows, :]
            hn_ref[rows, :] = (xv * lax.rsqrt(_mean_last(xv * xv) + EPS) * g).astype(BF16)
            return 0

        _loop(tm // TILE_ROWS, rows_body, 0, unroll=TILE_UNROLL)

    tile = pl.BlockSpec((tm, D_MODEL), lambda i: (i, 0))
    return pl.pallas_call(
        body, name="pre_norm", grid=(t_len // tm,),
        in_specs=[tile, pl.BlockSpec((1, D_MODEL), lambda i: (0, 0))], out_specs=tile,
        out_shape=SDS((t_len, D_MODEL), BF16),
        compiler_params=_params(("arbitrary",), 24),
    )(x, pre_g)


CHIP_ORDER = (0, 2, 4, 6)
W_BODY, W_TAIL = 512, 128
SIBLING = 1
ICI_MASKS = (2, 4, 6)
DIRECT_MASKS = (SIBLING,) + ICI_MASKS
Y_NEIGHBOUR, X_NEIGHBOUR, DIAGONAL = 2, 4, 6
W_DIRECT = (SIBLING, Y_NEIGHBOUR, X_NEIGHBOUR)


def _in_proj(hn, w_shard, others, tm=1024):
    t_len = hn.shape[0]
    n_i = t_len // tm
    n_o = len(others)
    me_out = 4 * lax.axis_index("x") + 2 * lax.axis_index("y") + lax.axis_index("c")
    order = jnp.stack([(me_out ^ chip) // 2 for chip in CHIP_ORDER]).astype(jnp.int32)

    def body(order_ref, hn_ref, w_hbm, *refs):
        o_in = refs[:n_o]
        z_ref, wg_hbm = refs[n_o], refs[n_o + 1]
        o_out = refs[n_o + 2:2 * n_o + 2]
        (wbuf, tail_s, send_w, recv_w, fsend_w, frecv_w, send_o, recv_o, fsend_o, frecv_o, wb_sems, loc_sems, rsend,
         rrecv) = refs[2 * n_o + 2:]
        j, i = pl.program_id(0), pl.program_id(1)
        x, y, c, me = _mesh_place()
        sib = _peer(x, y, c, SIBLING)[0]

        def relay(core):
            src, dst = (Y_NEIGHBOUR, X_NEIGHBOUR) if core == 0 else (X_NEIGHBOUR, Y_NEIGHBOUR)
            held, diag = _peer(x, y, c, src)[1], _peer(x, y, c, DIAGONAL)[1]
            pair = (rsend.at[0], rrecv.at[0], _peer(x, y, c, dst)[0])
            return _remote(wbuf.at[held], wbuf.at[held], *pair), _remote(wbuf.at[diag], wbuf.at[diag], *pair)

        def direct(k, a=None):
            dev, lin = _peer(x, y, c, k)
            if a is None:
                return (_remote(w_hbm, wbuf.at[me], send_w.at[k], recv_w.at[k], dev),
                        _remote(w_hbm, wbuf.at[lin], send_w.at[k], recv_w.at[k], dev))
            pair = (send_o.at[a * N_DEV + k], recv_o.at[a * N_DEV + k], dev)
            return _remote(o_in[a], o_out[a].at[me], *pair), _remote(o_in[a], o_out[a].at[lin], *pair)

        def passed(k, a=None):
            mine, theirs = _peer(x, y, c, k)[1], _peer(x, y, c, k ^ SIBLING)[1]
            if a is None:
                pair = (fsend_w.at[k], frecv_w.at[k], sib)
                return _remote(wbuf.at[mine], wbuf.at[mine], *pair), _remote(wbuf.at[theirs], wbuf.at[theirs], *pair)
            pair = (fsend_o.at[a * N_DEV + k], frecv_o.at[a * N_DEV + k], sib)
            return (_remote(o_out[a].at[mine], o_out[a].at[mine], *pair),
                    _remote(o_out[a].at[theirs], o_out[a].at[theirs], *pair))

        def own_copies():
            return [pltpu.make_async_copy(o_in[a], o_out[a].at[me], loc_sems.at[1 + a]) for a in range(n_o)]

        @pl.when(jnp.logical_and(j == 0, i == 0))
        def _():
            own = pltpu.make_async_copy(w_hbm, wbuf.at[me], loc_sems.at[0])
            own.start()
            for cp in own_copies():
                cp.start()
            for k in W_DIRECT:
                direct(k)[0].start()
            for k in DIRECT_MASKS:
                for a in range(n_o):
                    direct(k, a)[0].start()
            own.wait()

        low = 2 * order_ref[j]

        for jp, chip in enumerate(CHIP_ORDER):
            @pl.when(jnp.logical_and(j == jp, i == 0))
            def _(jp=jp, chip=chip):
                if chip == 0:
                    direct(SIBLING)[1].wait_recv()
                elif chip == Y_NEIGHBOUR:
                    for mask in (Y_NEIGHBOUR, X_NEIGHBOUR):
                        direct(mask)[1].wait_recv()
                        passed(mask)[0].start()
                    for core in (0, 1):
                        @pl.when(c == core)
                        def _(core=core):
                            relay(core)[0].start()
                    passed(Y_NEIGHBOUR)[1].wait_recv()
                elif chip == X_NEIGHBOUR:
                    passed(X_NEIGHBOUR)[1].wait_recv()
                    for core in (0, 1):
                        @pl.when(c == core)
                        def _(core=core):
                            relay(core)[1].wait_recv()
                    passed(DIAGONAL)[0].start()
                    for k in ICI_MASKS:
                        for a in range(n_o):
                            direct(k, a)[1].wait_recv()
                            passed(k, a)[0].start()
                else:
                    passed(DIAGONAL)[1].wait_recv()
                for half in (0, 1):
                    pltpu.make_async_copy(wbuf.at[low + half], wg_hbm.at[low + half], wb_sems.at[2 * jp + half]).start()
                tail_s[:, 0:W_TAIL] = wbuf[low, :, W_BODY:W_IN_SHARD]
                tail_s[:, W_TAIL:2 * W_TAIL] = wbuf[low + 1, :, W_BODY:W_IN_SHARD]

        hn = hn_ref[...]
        z_ref[:, 0:W_BODY] = _dot(hn, wbuf[low, :, 0:W_BODY])
        z_ref[:, W_IN_SHARD:W_IN_SHARD + W_BODY] = _dot(hn, wbuf[low + 1, :, 0:W_BODY])
        tails = _dot(hn, tail_s[...])
        z_ref[:, W_BODY:W_IN_SHARD] = tails[:, 0:W_TAIL]
        z_ref[:, W_IN_SHARD + W_BODY:2 * W_IN_SHARD] = tails[:, W_TAIL:2 * W_TAIL]

        @pl.when(jnp.logical_and(j == len(CHIP_ORDER) - 1, i == n_i - 1))
        def _():
            for a in range(n_o):
                direct(SIBLING, a)[1].wait_recv()
            for k in ICI_MASKS:
                for a in range(n_o):
                    passed(k, a)[1].wait_recv()
            for k in W_DIRECT:
                direct(k)[0].wait_send()
            for core in (0, 1):
                @pl.when(c == core)
                def _(core=core):
                    relay(core)[0].wait_send()
            for k in DIRECT_MASKS:
                for a in range(n_o):
                    direct(k, a)[0].wait_send()
            for k in ICI_MASKS:
                passed(k)[0].wait_send()
                for a in range(n_o):
                    passed(k, a)[0].wait_send()
            for cp in own_copies():
                cp.wait()
            for jj in range(N_DEV):
                pltpu.make_async_copy(wbuf.at[0], wg_hbm.at[0], wb_sems.at[jj]).wait()

    dma = lambda n: pltpu.SemaphoreType.DMA((n,))
    grid_spec = pltpu.PrefetchScalarGridSpec(
        num_scalar_prefetch=1, grid=(len(CHIP_ORDER), n_i),
        in_specs=[pl.BlockSpec((tm, D_MODEL), lambda j, i, order: (i, 0)), ANY_SPEC] + [ANY_SPEC] * n_o,
        out_specs=[pl.BlockSpec((tm, 2 * W_IN_SHARD), lambda j, i, order: (i, order[j])), ANY_SPEC] + [ANY_SPEC] * n_o,
        scratch_shapes=[pltpu.VMEM((N_DEV, D_MODEL, W_IN_SHARD), BF16), pltpu.VMEM((D_MODEL, 2 * W_TAIL), BF16),
                        dma(N_DEV), dma(N_DEV), dma(N_DEV), dma(N_DEV),
                        dma(n_o * N_DEV), dma(n_o * N_DEV), dma(n_o * N_DEV), dma(n_o * N_DEV), dma(N_DEV), dma(1 + n_o),
                        dma(1), dma(1)])
    res = pl.pallas_call(
        body, name="in_proj", grid_spec=grid_spec,
        out_shape=[SDS((t_len, D_IN), F32), SDS((N_DEV, D_MODEL, W_IN_SHARD), BF16)]
        + [SDS((N_DEV,) + o.shape, o.dtype) for o in others],
        compiler_params=_params(("arbitrary", "arbitrary"), 54),
    )(order, hn, w_shard, *others)
    return res[0], res[1], res[2:]


def _conv_rows(cur, prev, cw_ref, cb, rid):
    acc = cw_ref[3:4, :] * cur + cb
    for k in range(1, CONV_W):
        acc = acc + cw_ref[3 - k:4 - k, :] * _shift_down(cur, prev, k, rid)
    return acc


ROW0_LOG_A = -1e30


def _row0_mask(rid):
    return jnp.where(rid == 0, ROW0_LOG_A, 0.0)


def _row0_bias(is_first_group, row0_mask):
    return is_first_group.astype(F32) * row0_mask


def _lru_gates(pa, px, ba, bx, sp8, row0_bias):
    r = _sig(pa + ba)
    i = _sig(px + bx)
    la = row0_bias - r * sp8
    a = jnp.exp(la)
    return r, i, a, _neg_expm1(2.0 * la, a * a)


def _mix_fwd(z, ln_g, ln_b, wm, bias, cw, cb, wax, ba, bx, lam, goa, gob, ex_arrs, ex_scatter):
    t_len = z.shape[0]
    n_chunk = t_len // CHUNK
    ex = _Exchange(ex_arrs, ex_scatter)
    n_in, n_out, n_scratch = 13, 5, 7

    def body(*refs):
        (z_ref, lng_ref, lnb_ref, wm_ref, bias_ref, cw_ref, cb_ref, wax_ref, ba_ref, bx_ref, lam_ref, goa_ref,
         gob_ref) = refs[:n_in]
        ex_in = refs[n_in:n_in + ex.n]
        y_ref, h_ref, vhb_ref, xcb_ref, rs_ref = refs[n_in + ex.n:n_in + ex.n + n_out]
        ex_out = refs[n_in + ex.n + n_out:n_in + 2 * ex.n + n_out]
        vn_s, xc_s, mixed_s, pre_s, y_s, carry_s, halo_s = refs[n_in + 2 * ex.n + n_out:n_in + 2 * ex.n + n_out + n_scratch]
        ex_sems = refs[n_in + 2 * ex.n + n_out + n_scratch:]
        step = pl.program_id(0)
        rid = _row_ids(D_BR)

        @pl.when(step == 0)
        def _():
            ex.start(ex_in, ex_out, ex_sems)
            carry_s[...] = jnp.zeros_like(carry_s)
            halo_s[...] = jnp.zeros_like(halo_s)

        lng, lnb, cb = lng_ref[...], lnb_ref[...], cb_ref[...]
        ba, bx, goa, gob = ba_ref[...], bx_ref[...], goa_ref[...], gob_ref[...]
        sp8 = LRU_C * _softplus(-lam_ref[...])
        row0 = _row0_mask(rid)

        def chunk(c_id, z_ref, y_ref, h_ref, vhb_ref, xcb_ref, rs_ref):
            def phase1(g, prev):
                rows = _rows(g)
                vg = _gelu(z_ref[rows, D_BR:2 * D_BR])
                xm = vg - _mean_last(vg)
                rs = lax.rsqrt(_mean_last(xm * xm) + EPS)
                vn_s[rows, :] = xm * rs
                rs_ref[rows, :] = jnp.broadcast_to(rs, (ROWS, HEAD))
                xb = z_ref[rows, 3 * D_BR:4 * D_BR]
                xc_s[rows, :] = _conv_rows(xb, prev, cw_ref, cb, rid)
                return xb

            halo_s[...] = _loop(N_GROUP, phase1, halo_s[...], unroll=8)
            vhb_ref[...] = vn_s[...].astype(BF16)
            xcb_ref[...] = xc_s[...].astype(BF16)

            for h in range(N_HEAD):
                cs = slice(h * HEAD, (h + 1) * HEAD)
                mixed_s[:, cs] = _dot(wm_ref[h], (vn_s[:, cs] * lng[:, cs] + lnb[:, cs]).astype(BF16))
                pre = _dot(xcb_ref[:, cs], wax_ref[h])
                pre_s[:, cs] = pre[:, :HEAD]
                pre_s[:, D_BR + h * HEAD:D_BR + (h + 1) * HEAD] = pre[:, HEAD:]

            def phase3(g, carry):
                rows = _rows(g)
                ug = _gelu(z_ref[rows, 0:D_BR])
                ga = z_ref[rows, 2 * D_BR:3 * D_BR]
                ya = ug * (mixed_s[rows, :] + bias_ref[rows, :]) * (ga * _sig(ga))
                y_s[rows, 0:D_BR] = ya * lax.rsqrt(_mean_last(ya * ya) + EPS) * goa

                bias0 = _row0_bias(jnp.logical_and(c_id == 0, g == 0), row0)
                _, i, a, m2 = _lru_gates(pre_s[rows, 0:D_BR], pre_s[rows, D_BR:2 * D_BR], ba, bx, sp8, bias0)
                b = jnp.sqrt(m2) * i * xc_s[rows, :]
                for d in (1, 2, 4):
                    a_sh = jnp.where(rid >= d, pltpu.roll(a, d, 0), 1.0)
                    b_sh = jnp.where(rid >= d, pltpu.roll(b, d, 0), 0.0)
                    b = a * b_sh + b
                    a = a * a_sh
                hh = b + a * carry
                h_ref[rows, :] = hh
                gb = z_ref[rows, 4 * D_BR:5 * D_BR]
                yb = hh * (gb * _sig(gb))
                y_s[rows, D_BR:2 * D_BR] = yb * lax.rsqrt(_mean_last(yb * yb) + EPS) * gob
                return _bcast_row(hh, ROWS - 1)

            carry_s[...] = _loop(N_GROUP, phase3, carry_s[...])
            y_ref[...] = y_s[...].astype(BF16)

        for sub in range(MIX_SUB):
            part = lambda ref, sub=sub: ref.at[pl.ds(sub * CHUNK, CHUNK)]
            chunk(step * MIX_SUB + sub, part(z_ref), part(y_ref), part(h_ref), part(vhb_ref), part(xcb_ref),
                  part(rs_ref))

        @pl.when(step == n_chunk // MIX_SUB - 1)
        def _():
            ex.wait(ex_in, ex_out, ex_sems)

    vec = pl.BlockSpec((1, D_BR), lambda i: (0, 0))
    blk = MIX_SUB * CHUNK
    res = pl.pallas_call(
        body, name="mix_fwd", grid=(n_chunk // MIX_SUB,),
        in_specs=[pl.BlockSpec((blk, D_IN), lambda i: (i, 0)), vec, vec,
                  pl.BlockSpec((N_HEAD, HEAD, HEAD), lambda i: (0, 0, 0)),
                  pl.BlockSpec((CHUNK, D_BR), lambda i: (0, 0)),
                  pl.BlockSpec((ROWS, D_BR), lambda i: (0, 0)), vec,
                  pl.BlockSpec((N_HEAD, HEAD, 2 * HEAD), lambda i: (0, 0, 0)), vec, vec, vec, vec, vec]
        + [ANY_SPEC] * ex.n,
        out_specs=[pl.BlockSpec((blk, 2 * D_BR), lambda i: (i, 0)), pl.BlockSpec((blk, D_BR), lambda i: (i, 0)),
                   pl.BlockSpec((blk, D_BR), lambda i: (i, 0)), pl.BlockSpec((blk, D_BR), lambda i: (i, 0)),
                   pl.BlockSpec((blk, HEAD), lambda i: (i, 0))] + [ANY_SPEC] * ex.n,
        out_shape=[SDS((t_len, 2 * D_BR), BF16), SDS((t_len, D_BR), F32), SDS((t_len, D_BR), BF16),
                   SDS((t_len, D_BR), BF16), SDS((t_len, HEAD), F32)] + ex.out_shape,
        scratch_shapes=[pltpu.VMEM((CHUNK, D_BR), F32), pltpu.VMEM((CHUNK, D_BR), F32), pltpu.VMEM((CHUNK, D_BR), F32),
                        pltpu.VMEM((CHUNK, 2 * D_BR), F32), pltpu.VMEM((CHUNK, 2 * D_BR), F32),
                        pltpu.VMEM((ROWS, D_BR), F32), pltpu.VMEM((ROWS, D_BR), F32)] + ex.scratch,
        compiler_params=_params(("arbitrary",), 32),
    )(z, ln_g, ln_b, wm, bias, cw, cb, wax, ba, bx, lam, goa, gob, *ex_arrs)
    return res[:n_out], res[n_out:]


def _load_weight(w_hbm, w_vmem, sem):
    @pl.when(pl.program_id(0) == 0)
    def _():
        cp = pltpu.make_async_copy(w_hbm, w_vmem, sem)
        cp.start()
        cp.wait()


def _out_proj(y, x, w_out, post_g, tm=512):
    t_len = y.shape[0]

    def body(y_ref, x_ref, w_hbm, g_ref, h1_ref, ob_ref, w_s, o_s, sem):
        _load_weight(w_hbm, w_s, sem)
        o_s[...] = _dot(y_ref[...], w_s[...])
        g = g_ref[...]

        def rows_body(q, _):
            rows = _tile_rows(q)
            o = o_s[rows, :]
            h1_ref[rows, :] = x_ref[rows, :] + o * lax.rsqrt(_mean_last(o * o) + EPS) * g
            ob_ref[rows, :] = o.astype(BF16)
            return 0

        _loop(tm // TILE_ROWS, rows_body, 0, unroll=TILE_UNROLL)

    tile = pl.BlockSpec((tm, D_MODEL), lambda i: (i, 0))
    return pl.pallas_call(
        body, name="out_proj", grid=(t_len // tm,),
        in_specs=[tile, tile, pl.BlockSpec(memory_space=pl.ANY), pl.BlockSpec((1, D_MODEL), lambda i: (0, 0))],
        out_specs=[tile, tile],
        out_shape=[SDS((t_len, D_MODEL), F32), SDS((t_len, D_MODEL), BF16)],
        scratch_shapes=[pltpu.VMEM((D_MODEL, D_MODEL), BF16), pltpu.VMEM((tm, D_MODEL), F32), pltpu.SemaphoreType.DMA],
        compiler_params=_params(("arbitrary",), 44),
    )(y, x, w_out, post_g)


def _ple_loss(h1, p, tgt, w_pg, w_pe_g, tm=256):
    t_len = h1.shape[0]
    n_tile = t_len // tm
    pe_shard = D_MODEL // N_DEV

    def body(h1_ref, p_ref, t_ref, w_hbm, wpe_ref, dh2_ref, dgl_ref, h1b_ref, loss_ref, dwpe_ref, w_s, pe_s, gl_s, acc_s,
             dpe_s, gpe_s, sem):
        _load_weight(w_hbm, w_s, sem)
        i = pl.program_id(0)

        @pl.when(i == 0)
        def _():
            acc_s[...] = jnp.zeros_like(acc_s)
            gpe_s[...] = jnp.zeros_like(gpe_s)

        h1b_ref[...] = h1_ref[...].astype(BF16)
        pb = p_ref[...].astype(BF16)
        for j in range(N_DEV):
            cols = slice(j * pe_shard, (j + 1) * pe_shard)
            pe_s[:, cols] = _dot(pb, wpe_ref[j])
            gl_s[:, cols] = _dot(h1b_ref[...], w_s[:, cols])
            acc = acc_s[:, cols]
            for q in range(tm // TILE_ROWS):
                rows = slice(q * TILE_ROWS, (q + 1) * TILE_ROWS)
                pe = pe_s[rows, cols]
                g = _sig(gl_s[rows, cols])
                e = h1_ref[rows, cols] + pe * g - t_ref[rows, cols]
                dh2 = e * (1.0 / D_MODEL)
                dh2_ref[rows, cols] = dh2
                dpe_s[rows, cols] = (dh2 * g).astype(BF16)
                dgl_ref[rows, cols] = (dh2 * pe * g * (1.0 - g)).astype(BF16)
                acc = acc + _fold_rows(e * e)
            acc_s[:, cols] = acc
        gpe_s[...] += _dot_tn(pb, dpe_s[...])

        @pl.when(i == n_tile - 1)
        def _():
            loss_ref[...] = jnp.full(loss_ref.shape, 0.5 / D_MODEL * jnp.sum(acc_s[...]), F32)
            for j in range(N_DEV):
                dwpe_ref[j] = gpe_s[:, j * pe_shard:(j + 1) * pe_shard].astype(BF16)

    tile = pl.BlockSpec((tm, D_MODEL), lambda i: (i, 0))
    pe_blocks = pl.BlockSpec((N_DEV, D_PLE, pe_shard), lambda i: (0, 0, 0))
    return pl.pallas_call(
        body, name="ple_loss", grid=(n_tile,),
        in_specs=[tile, pl.BlockSpec((tm, D_PLE), lambda i: (i, 0)), tile, pl.BlockSpec(memory_space=pl.ANY), pe_blocks],
        out_specs=[tile, tile, tile, pl.BlockSpec((ROWS, HEAD), lambda i: (0, 0)), pe_blocks],
        out_shape=[SDS((t_len, D_MODEL), F32), SDS((t_len, D_MODEL), BF16), SDS((t_len, D_MODEL), BF16),
                   SDS((ROWS, HEAD), F32), SDS((N_DEV, D_PLE, pe_shard), BF16)],
        scratch_shapes=[pltpu.VMEM((D_MODEL, D_MODEL), BF16), pltpu.VMEM((tm, D_MODEL), F32),
                        pltpu.VMEM((tm, D_MODEL), F32), pltpu.VMEM((ROWS, D_MODEL), F32), pltpu.VMEM((tm, D_MODEL), BF16),
                        pltpu.VMEM((D_PLE, D_MODEL), F32), pltpu.SemaphoreType.DMA],
        compiler_params=_params(("arbitrary",), 48),
    )(h1, p, tgt, w_pg, w_pe_g)


def _tail_bwd(dh2, dgl, ob, w_pg, w_out, post_g, tm=256):
    t_len = dh2.shape[0]
    n_tile = t_len // tm

    def body(dh2_ref, dgl_ref, ob_ref, wpg_hbm, wout_hbm, g_ref, dh1_ref, do_ref, dy_ref, dg_ref, wpg_s, wout_s, t_s,
             acc_s, sems):
        i = pl.program_id(0)
        load_wpg = pltpu.make_async_copy(wpg_hbm, wpg_s, sems.at[0])
        load_wout = pltpu.make_async_copy(wout_hbm, wout_s, sems.at[1])

        @pl.when(i == 0)
        def _():
            load_wpg.start()
            load_wout.start()
            acc_s[...] = jnp.zeros_like(acc_s)
            load_wpg.wait()

        t_s[...] = _dot_nt(dgl_ref[...], wpg_s[...])
        g = g_ref[...]

        def rows_body(q, acc):
            rows = _tile_rows(q)
            dh1 = dh2_ref[rows, :] + t_s[rows, :]
            dh1_ref[rows, :] = dh1
            o = ob_ref[rows, :].astype(F32)
            rr = lax.rsqrt(_mean_last(o * o) + EPS)
            on = o * rr
            dog = dh1 * g
            do_ref[rows, :] = (rr * (dog - on * _mean_last(dog * on))).astype(BF16)
            return acc + _fold_rows(dh1 * on)

        acc_s[...] = _loop(tm // TILE_ROWS, rows_body, acc_s[...], unroll=TILE_UNROLL)

        @pl.when(i == 0)
        def _():
            load_wout.wait()

        dy_ref[...] = _dot_nt(do_ref[...], wout_s[...]).astype(BF16)

        @pl.when(i == n_tile - 1)
        def _():
            dg_ref[...] = jnp.sum(acc_s[...], axis=0, keepdims=True)

    tile = pl.BlockSpec((tm, D_MODEL), lambda i: (i, 0))
    vec = pl.BlockSpec((1, D_MODEL), lambda i: (0, 0))
    hbm = pl.BlockSpec(memory_space=pl.ANY)
    return pl.pallas_call(
        body, name="tail_bwd", grid=(n_tile,),
        in_specs=[tile, tile, tile, hbm, hbm, vec],
        out_specs=[tile, tile, tile, vec],
        out_shape=[SDS((t_len, D_MODEL), F32), SDS((t_len, D_MODEL), BF16), SDS((t_len, D_MODEL), BF16),
                   SDS((1, D_MODEL), F32)],
        scratch_shapes=[pltpu.VMEM((D_MODEL, D_MODEL), BF16), pltpu.VMEM((D_MODEL, D_MODEL), BF16),
                        pltpu.VMEM((tm, D_MODEL), F32), pltpu.VMEM((ROWS, D_MODEL), F32), pltpu.SemaphoreType.DMA((2,))],
        compiler_params=_params(("arbitrary",), 48),
    )(dh2, dgl, ob, w_pg, w_out, post_g)


def _mix_bwd(z, dy, h, vhb, xcb, rs, ln_g, ln_b, wm, wm_t, bias, cw, cb, wax, wax_t, ba, bx, lam, goa, gob, ex_arrs,
             ex_scatter):
    t_len = z.shape[0]
    n_chunk = t_len // CHUNK
    halo_blocks = CHUNK // ROWS
    ex = _Exchange(ex_arrs, ex_scatter)
    n_in, n_out, n_scratch = 21, 5, 16

    blocked = (0, 1, 2, 4, 5, 6, n_in + ex.n)

    def body(*refs):
        step = pl.program_id(0)
        for sub in reversed(range(MIX_SUB)):
            views = list(refs)
            for idx in blocked:
                views[idx] = refs[idx].at[pl.ds(sub * CHUNK, CHUNK)]
            h_before = refs[2].at[pl.ds(sub * CHUNK - ROWS, ROWS)] if sub else refs[3]
            chunk((n_chunk // MIX_SUB - 1 - step) * MIX_SUB + sub,
                  step == 0 if sub == MIX_SUB - 1 else None,
                  step == n_chunk // MIX_SUB - 1 if sub == 0 else None, h_before, *views)

    def chunk(c_id, first, last, h_before, *refs):
        (z_ref, dy_ref, h_ref, hhalo_ref, vhb_ref, xcb_ref, rs_ref, lng_ref, lnb_ref, wm_ref, wmt_ref, bias_ref, cw_ref,
         cb_ref, wax_ref, waxt_ref, ba_ref, bx_ref, lam_ref, goa_ref, gob_ref) = refs[:n_in]
        ex_in = refs[n_in:n_in + ex.n]
        dz_ref, vecs_ref, dws_ref, dwax_ref, dbs_ref = refs[n_in + ex.n:n_in + ex.n + n_out]
        ex_out = refs[n_in + ex.n + n_out:n_in + 2 * ex.n + n_out]
        (vnb_s, vh_s, xc_s, mixed_s, pre_s, dmix_s, dvn_s, dho_s, dxc_s, dpre_s, dz_s, acc_s, accdm_s,
         cg_s, ca_s, dxchalo_s) = refs[n_in + 2 * ex.n + n_out:n_in + 2 * ex.n + n_out + n_scratch]
        ex_sems = refs[n_in + 2 * ex.n + n_out + n_scratch:]
        rid = _row_ids(D_BR)
        first_chunk = c_id == 0

        if first is not None:
            @pl.when(first)
            def _():
                ex.start(ex_in, ex_out, ex_sems)
                acc_s[...] = jnp.zeros_like(acc_s)
                accdm_s[...] = jnp.zeros_like(accdm_s)
                cg_s[...] = jnp.zeros_like(cg_s)
                ca_s[...] = jnp.zeros_like(ca_s)
                dxchalo_s[...] = jnp.zeros_like(dxchalo_s)
                dws_ref[...] = jnp.zeros_like(dws_ref)
                dwax_ref[...] = jnp.zeros_like(dwax_ref)

        lng, lnb = lng_ref[...], lnb_ref[...]
        h_halo = jnp.where(first_chunk, 0.0, h_before[...])

        def prev_rows(ref, cols, g, halo):
            before = ref[pl.ds(pl.multiple_of(jnp.maximum(g - 1, 0) * ROWS, ROWS), ROWS), cols]
            return jnp.where(g > 0, before, halo)

        vh_s[...] = vhb_ref[...].astype(F32)
        xc_s[...] = xcb_ref[...].astype(F32)

        for hd in range(N_HEAD):
            cs = slice(hd * HEAD, (hd + 1) * HEAD)
            vnb_s[:, cs] = (vh_s[:, cs] * lng[:, cs] + lnb[:, cs]).astype(BF16)
            mixed_s[:, cs] = _dot(wm_ref[hd], vnb_s[:, cs])
            pre = _dot(xcb_ref[:, cs], wax_ref[hd])
            pre_s[:, cs] = pre[:, :HEAD]
            pre_s[:, D_BR + hd * HEAD:D_BR + (hd + 1) * HEAD] = pre[:, HEAD:]

        goa, gob = goa_ref[...], gob_ref[...]

        def phase3(g, _):
            rows = _rows(g)
            ug, dug = _gelu(z_ref[rows, 0:D_BR], with_grad=True)
            ga = z_ref[rows, 2 * D_BR:3 * D_BR]
            sga = _sig(ga)
            sa = ga * sga
            mixed = mixed_s[rows, :] + bias_ref[rows, :]
            ya0 = ug * mixed
            ya = ya0 * sa
            ra = lax.rsqrt(_mean_last(ya * ya) + EPS)
            dyan = dy_ref[rows, 0:D_BR].astype(F32)
            acc_s[V_GOUT_A] += dyan * ya * ra
            dyg = dyan * goa
            dya = ra * dyg - ya * (ra * ra * ra) * _mean_last(dyg * ya)
            dya0 = dya * sa
            dz_s[rows, 2 * D_BR:3 * D_BR] = dya * ya0 * _silu_grad(sga, sa)
            dmix = dya0 * ug
            dmix_s[rows, :] = dmix
            accdm_s[rows, :] += dmix
            dz_s[rows, 0:D_BR] = dya0 * mixed * dug

            hh = h_ref[rows, :]
            gb = z_ref[rows, 4 * D_BR:5 * D_BR]
            sgb = _sig(gb)
            sb = gb * sgb
            yb = hh * sb
            rb = lax.rsqrt(_mean_last(yb * yb) + EPS)
            dybn = dy_ref[rows, D_BR:2 * D_BR].astype(F32)
            acc_s[V_GOUT_B] += dybn * yb * rb
            dyg = dybn * gob
            dyb = rb * dyg - yb * (rb * rb * rb) * _mean_last(dyg * yb)
            dho_s[rows, :] = dyb * sb
            dz_s[rows, 4 * D_BR:5 * D_BR] = dyb * hh * _silu_grad(sgb, sb)
            return 0

        _loop(N_GROUP, phase3, 0)

        for hd in range(N_HEAD):
            cs = slice(hd * HEAD, (hd + 1) * HEAD)
            dmb = dmix_s[:, cs].astype(BF16)
            dvn_s[:, cs] = _dot(wmt_ref[hd], dmb)
            dws_ref[hd] += _dot_nt(dmb, vnb_s[:, cs])

        def phase5(g, _):
            rows = _rows(g)
            dvn = dvn_s[rows, :]
            vh = vh_s[rows, :]
            acc_s[V_LN_G] += dvn * vh
            acc_s[V_LN_B] += dvn
            dvh = dvn * lng
            rs = rs_ref[rows, 0:1]
            dvg = rs * (dvh - _mean_last(dvh) - vh * _mean_last(dvh * vh))
            dz_s[rows, D_BR:2 * D_BR] = dvg * _gelu(z_ref[rows, D_BR:2 * D_BR], with_grad=True)[1]
            return 0

        _loop(N_GROUP, phase5, 0)

        ba, bx = ba_ref[...], bx_ref[...]
        sp8 = LRU_C * _softplus(-lam_ref[...])
        row0 = _row0_mask(rid)

        def phase6(k, carry):
            cg, ca = carry
            g = N_GROUP - 1 - k
            rows = _rows(g)
            bias0 = _row0_bias(jnp.logical_and(first_chunk, g == 0), row0)
            r, i, a, m2 = _lru_gates(pre_s[rows, 0:D_BR], pre_s[rows, D_BR:2 * D_BR], ba, bx, sp8, bias0)
            a_nx = jnp.where(rid < ROWS - 1, pltpu.roll(a, ROWS - 1, 0), ca)
            aa, bb = a_nx, dho_s[rows, :]
            for d in (1, 2, 4):
                a_sh = jnp.where(rid < ROWS - d, pltpu.roll(aa, ROWS - d, 0), 1.0)
                b_sh = jnp.where(rid < ROWS - d, pltpu.roll(bb, ROWS - d, 0), 0.0)
                bb = aa * b_sh + bb
                aa = aa * a_sh
            gg = bb + aa * cg
            hh = h_ref[rows, :]
            hprev = _shift_down(hh, prev_rows(h_ref, slice(None), g, h_halo), 1, rid)
            xc = xc_s[rows, :]
            gx = gg * xc
            dla = gg * hprev * a - gx * i * (a * a) * lax.rsqrt(m2)
            acc_s[V_LAM] += -(dla * r)
            dpa = -(dla * sp8) * r * (1.0 - r)
            mi = jnp.sqrt(m2) * i
            dpx = gx * mi * (1.0 - i)
            acc_s[V_B_A] += dpa
            acc_s[V_B_X] += dpx
            dpre_s[rows, 0:D_BR] = dpa
            dpre_s[rows, D_BR:2 * D_BR] = dpx
            dxc_s[rows, :] = gg * mi
            return _bcast_row(gg, 0), _bcast_row(a, 0)

        cg, ca = _loop(N_GROUP, phase6, (cg_s[...], ca_s[...]))
        cg_s[...] = cg
        ca_s[...] = ca

        for hd in range(N_HEAD):
            cs = slice(hd * HEAD, (hd + 1) * HEAD)
            dpre = jnp.concatenate([dpre_s[:, cs], dpre_s[:, D_BR + hd * HEAD:D_BR + (hd + 1) * HEAD]], axis=1).astype(BF16)
            dxc_s[:, cs] += _dot(dpre, waxt_ref[hd])
            dwax_ref[hd] += _dot_tn(xcb_ref[:, cs], dpre)

        def phase8(k, nxt):
            g = N_GROUP - 1 - k
            rows = _rows(g)
            dxc = dxc_s[rows, :]
            acc_s[V_CONV_B] += dxc
            xb = z_ref[rows, 3 * D_BR:4 * D_BR]
            dxb = cw_ref[3:4, :] * dxc
            acc_s[V_CONV_W + 3] += dxc * xb
            for j in range(1, CONV_W):
                later = _shift_up(dxc, nxt, j, rid)
                dxb = dxb + cw_ref[3 - j:4 - j, :] * later
                acc_s[V_CONV_W + 3 - j] += later * xb
            dz_s[rows, 3 * D_BR:4 * D_BR] = dxb
            return dxc

        dxchalo_s[...] = _loop(N_GROUP, phase8, dxchalo_s[...])
        dz_ref[...] = dz_s[...].astype(BF16)

        if last is not None:
            @pl.when(last)
            def _():
                for v in range(N_VEC):
                    vecs_ref[v:v + 1, :] = jnp.sum(acc_s[v], axis=0, keepdims=True)
                lam = lam_ref[...]
                vecs_ref[V_LAM:V_LAM + 1, :] = vecs_ref[V_LAM:V_LAM + 1, :] * (-LRU_C * _sig(-lam))
                tril = (lax.broadcasted_iota(jnp.int32, (HEAD, HEAD), 0)
                        >= lax.broadcasted_iota(jnp.int32, (HEAD, HEAD), 1))
                ones = jnp.ones((ROWS, HEAD), BF16)
                for hd in range(N_HEAD):
                    cs = slice(hd * HEAD, (hd + 1) * HEAD)
                    dws_ref[hd] = jnp.where(tril, dws_ref[hd], 0.0)
                    blk = accdm_s[:, cs]
                    hi = blk.astype(BF16)
                    lo = (blk - hi.astype(F32)).astype(BF16)
                    dbs_ref[hd:hd + 1, :] = (_dot_nt(ones, hi) + _dot_nt(ones, lo))[0:1, :]
                ex.wait(ex_in, ex_out, ex_sems)

    vec = pl.BlockSpec((1, D_BR), lambda i: (0, 0))
    n_step = n_chunk // MIX_SUB
    rows_blk = MIX_SUB * CHUNK
    rev = lambda i: (n_step - 1 - i, 0)
    halo = lambda col: (lambda i: (jnp.maximum((n_step - 1 - i) * MIX_SUB * halo_blocks - 1, 0), col))
    full3 = lambda a, b, c: pl.BlockSpec((a, b, c), lambda i: (0, 0, 0))
    big = lambda w: pltpu.VMEM((CHUNK, w), F32)
    res = pl.pallas_call(
        body, name="mix_bwd", grid=(n_step,),
        in_specs=[pl.BlockSpec((rows_blk, D_IN), rev), pl.BlockSpec((rows_blk, 2 * D_BR), rev),
                  pl.BlockSpec((rows_blk, D_BR), rev),
                  pl.BlockSpec((ROWS, D_BR), halo(0)), pl.BlockSpec((rows_blk, D_BR), rev),
                  pl.BlockSpec((rows_blk, D_BR), rev),
                  pl.BlockSpec((rows_blk, HEAD), rev), vec, vec,
                  full3(N_HEAD, HEAD, HEAD), full3(N_HEAD, HEAD, HEAD),
                  pl.BlockSpec((CHUNK, D_BR), lambda i: (0, 0)), pl.BlockSpec((ROWS, D_BR), lambda i: (0, 0)), vec,
                  full3(N_HEAD, HEAD, 2 * HEAD), full3(N_HEAD, 2 * HEAD, HEAD), vec, vec, vec, vec, vec]
        + [ANY_SPEC] * ex.n,
        out_specs=[pl.BlockSpec((rows_blk, D_IN), rev), pl.BlockSpec((N_VEC, D_BR), lambda i: (0, 0)),
                   full3(N_HEAD, HEAD, HEAD), full3(N_HEAD, HEAD, 2 * HEAD),
                   pl.BlockSpec((N_HEAD, HEAD), lambda i: (0, 0))] + [ANY_SPEC] * ex.n,
        out_shape=[SDS((t_len, D_IN), BF16), SDS((N_VEC, D_BR), F32), SDS((N_HEAD, HEAD, HEAD), F32),
                   SDS((N_HEAD, HEAD, 2 * HEAD), F32), SDS((N_HEAD, HEAD), F32)] + ex.out_shape,
        scratch_shapes=[pltpu.VMEM((CHUNK, D_BR), BF16), big(D_BR), big(D_BR), big(D_BR), big(2 * D_BR), big(D_BR),
                        big(D_BR), big(D_BR), big(D_BR), big(2 * D_BR), big(D_IN),
                        pltpu.VMEM((N_VEC, ROWS, D_BR), F32), big(D_BR),
                        pltpu.VMEM((ROWS, D_BR), F32), pltpu.VMEM((ROWS, D_BR), F32), pltpu.VMEM((ROWS, D_BR), F32)]
        + ex.scratch,
        compiler_params=_params(("arbitrary",), 48),
    )(z, dy, h, h, vhb, xcb, rs, ln_g, ln_b, wm, wm_t, bias, cw, cb, wax, wax_t, ba, bx, lam, goa, gob, *ex_arrs)
    return res[:n_out], res[n_out:]


def _in_bwd(dz, w_in_g, x, dh1, pre_g, tm=256):
    t_len = x.shape[0]
    n_tile = t_len // tm

    def body(dz_ref, w_hbm, x_ref, dh1_ref, g_ref, gx_ref, dg_ref, w_s, t_even, t_odd, dg_s, w_sems):
        i = pl.program_id(0)

        @pl.when(i == 0)
        def _():
            loads = [pltpu.make_async_copy(w_hbm.at[s], w_s.at[:, s * W_IN_SHARD:(s + 1) * W_IN_SHARD], w_sems.at[s])
                     for s in range(N_DEV)]
            for cp in loads:
                cp.start()
            dg_s[...] = jnp.zeros_like(dg_s)
            for s, cp in enumerate(loads):
                cp.wait()
                cols = slice(s * W_IN_SHARD, (s + 1) * W_IN_SHARD)
                part = _dot_nt(dz_ref[:, cols], w_s[:, cols])
                t_even[...] = part if s == 0 else t_even[...] + part

        def step(t_new, t_old):
            g = g_ref[...]
            acc = dg_s[...]
            for q in range(tm // TILE_ROWS):
                rows = slice(q * TILE_ROWS, (q + 1) * TILE_ROWS)
                xv = x_ref[rows, :]
                r = lax.rsqrt(_mean_last(xv * xv) + EPS)
                xh = xv * r
                dhn = t_old[rows, :]
                dg = dhn * g
                gx_ref[rows, :] = dh1_ref[rows, :] + r * (dg - xh * _mean_last(dg * xh))
                acc = acc + _fold_rows(dhn * xh)
            dg_s[...] = acc
            t_new[...] = _dot_nt(dz_ref[...], w_s[...])

        @pl.when((i % 2 == 0) & (i > 0))
        def _():
            step(t_even, t_odd)

        @pl.when(i % 2 == 1)
        def _():
            step(t_odd, t_even)

        @pl.when(i == n_tile)
        def _():
            dg_ref[...] = jnp.sum(dg_s[...], axis=0, keepdims=True)

    matmul_tile = lambda i: (jnp.minimum(i, n_tile - 1), 0)
    rows_tile = lambda i: (jnp.maximum(i - 1, 0), 0)
    res = pl.pallas_call(
        body, name="in_bwd", grid=(n_tile + 1,),
        in_specs=[pl.BlockSpec((tm, D_IN), matmul_tile), ANY_SPEC, pl.BlockSpec((tm, D_MODEL), rows_tile),
                  pl.BlockSpec((tm, D_MODEL), rows_tile), pl.BlockSpec((1, D_MODEL), lambda i: (0, 0))],
        out_specs=[pl.BlockSpec((tm, D_MODEL), rows_tile), pl.BlockSpec((1, D_MODEL), lambda i: (0, 0))],
        out_shape=[SDS((t_len, D_MODEL), F32), SDS((1, D_MODEL), F32)],
        scratch_shapes=[pltpu.VMEM((D_MODEL, D_IN), BF16), pltpu.VMEM((tm, D_MODEL), F32), pltpu.VMEM((tm, D_MODEL), F32),
                        pltpu.VMEM((ROWS, D_MODEL), F32), pltpu.SemaphoreType.DMA((N_DEV,))],
        compiler_params=_params(("arbitrary",), 54),
    )(dz, w_in_g, x, dh1, pre_g)
    return res[0], res[1]


def _grad_w(a, b, bn, shard_major, name, tk=1024, ex_arrs=(), ex_scatter=()):
    t_len, m = a.shape
    n = b.shape[1]
    n_j, n_k = n // bn, t_len // tk
    ex = _Exchange(ex_arrs, ex_scatter)

    def body(a_ref, b_ref, *refs):
        ex_in, o_ref, ex_out = refs[:ex.n], refs[ex.n], refs[ex.n + 1:2 * ex.n + 1]
        acc_s, ex_sems = refs[2 * ex.n + 1], refs[2 * ex.n + 2:]
        j, k = pl.program_id(0), pl.program_id(1)
        if ex.n:
            @pl.when(jnp.logical_and(j == 0, k == 0))
            def _():
                ex.start(ex_in, ex_out, ex_sems)

        @pl.when(k == 0)
        def _():
            acc_s[...] = jnp.zeros_like(acc_s)

        acc_s[...] += _dot_tn(a_ref[...], b_ref[...])

        @pl.when(k == n_k - 1)
        def _():
            o_ref[...] = acc_s[...].astype(BF16)

        if ex.n:
            @pl.when(jnp.logical_and(j == n_j - 1, k == n_k - 1))
            def _():
                ex.wait(ex_in, ex_out, ex_sems)

    if shard_major:
        out_spec, out_shape = pl.BlockSpec((None, m, bn), lambda j, k: (j, 0, 0)), SDS((n_j, m, bn), BF16)
    else:
        out_spec, out_shape = pl.BlockSpec((m, bn), lambda j, k: (0, j)), SDS((m, n), BF16)
    res = pl.pallas_call(
        body, name=name, grid=(n_j, n_k),
        in_specs=[pl.BlockSpec((tk, m), lambda j, k: (k, 0)), pl.BlockSpec((tk, bn), lambda j, k: (k, j))]
        + [ANY_SPEC] * ex.n,
        out_specs=[out_spec] + [ANY_SPEC] * ex.n, out_shape=[out_shape] + ex.out_shape,
        scratch_shapes=[pltpu.VMEM((m, bn), F32)] + (ex.scratch if ex.n else []),
        compiler_params=_params(("arbitrary", "arbitrary"), 40),
    )(a, b, *ex_arrs)
    return res[0], res[1:]


RS_CHIPS = (6, 2, 4, 0)
RS_SLOTS = (0, 1, 2, 4, 6)


def _grad_w_in_pairs(hn, dz, ex_arrs, ex_scatter, tk=1024):
    t_len = hn.shape[0]
    n_k = t_len // tk
    n_ph = len(RS_CHIPS)
    ex = _Exchange(ex_arrs, ex_scatter)
    me_out = 4 * lax.axis_index("x") + 2 * lax.axis_index("y") + lax.axis_index("c")
    order = jnp.stack([(me_out ^ chip) // 2 for chip in RS_CHIPS]).astype(jnp.int32)
    slots = jnp.stack([me_out ^ k for k in RS_SLOTS]).astype(jnp.int32)
    shard = W_IN_SHARD

    def body(order_ref, a_ref, b_ref, *refs):
        ex_in, parts_hbm, ex_out = refs[:ex.n], refs[ex.n], refs[ex.n + 1:2 * ex.n + 1]
        (acc_s, tb_s, stage_s, rx_s, d2d_send, d2d_recv, ici_send, ici_recv, sib_sems,
         loc_sem) = refs[2 * ex.n + 1:2 * ex.n + 11]
        ex_sems = refs[2 * ex.n + 11:]
        j, k = pl.program_id(0), pl.program_id(1)
        x, y, c, me = _mesh_place()
        sib = _peer(x, y, c, SIBLING)[0]

        def to_sibling(p):
            return _remote(stage_s.at[0], rx_s.at[p % 2], d2d_send.at[p], d2d_recv.at[p], sib)

        def over_ici(p):
            dev = _peer(x, y, c, RS_CHIPS[p])[0]
            return _remote(stage_s.at[1], parts_hbm.at[me], ici_send.at[p], ici_recv.at[p], dev)

        def own_chip():
            return (_remote(stage_s.at[0], parts_hbm.at[me], sib_sems.at[0], sib_sems.at[1], sib),
                    pltpu.make_async_copy(stage_s.at[1], parts_hbm.at[me], loc_sem.at[0]))

        @pl.when(jnp.logical_and(j == 0, k == 0))
        def _():
            ex.start(ex_in, ex_out, ex_sems)

        for p in range(n_ph - 1):
            for core in (0, 1):
                @pl.when(jnp.logical_and(jnp.logical_and(j == p + 1, k == 0), c == core))
                def _(p=p, core=core):
                    to_sibling(p).wait_recv()
                    if p >= 1:
                        over_ici(p - 1).wait_send()
                    mine = acc_s[:, core * shard:(core + 1) * shard]
                    stage_s[1] = (mine + rx_s[p % 2].astype(F32)).astype(BF16)
                    over_ici(p).start()

        @pl.when(k == 0)
        def _():
            acc_s[...] = jnp.zeros_like(acc_s)

        a = a_ref[...]
        acc_s[:, 0:W_BODY] += _dot_tn(a, b_ref[:, 0:W_BODY])
        acc_s[:, shard:shard + W_BODY] += _dot_tn(a, b_ref[:, shard:shard + W_BODY])
        tb_s[:, 0:W_TAIL] = b_ref[:, W_BODY:shard]
        tb_s[:, W_TAIL:2 * W_TAIL] = b_ref[:, shard + W_BODY:2 * shard]
        tails = _dot_tn(a, tb_s[...])
        acc_s[:, W_BODY:shard] += tails[:, 0:W_TAIL]
        acc_s[:, shard + W_BODY:2 * shard] += tails[:, W_TAIL:2 * W_TAIL]

        for p in range(n_ph):
            for core in (0, 1):
                @pl.when(jnp.logical_and(jnp.logical_and(j == p, k == n_k - 1), c == core))
                def _(p=p, core=core):
                    same = acc_s[:, core * shard:(core + 1) * shard]
                    other = acc_s[:, (1 - core) * shard:(2 - core) * shard]
                    if p >= 1:
                        to_sibling(p - 1).wait_send()
                    stage_s[0] = other.astype(BF16)
                    if p < n_ph - 1:
                        to_sibling(p).start()
                    else:
                        over_ici(n_ph - 2).wait_send()
                        stage_s[1] = same.astype(BF16)
                        for cp in own_chip():
                            cp.start()

        @pl.when(jnp.logical_and(j == n_ph - 1, k == n_k - 1))
        def _():
            to_sib, local = own_chip()
            to_sib.wait_send()
            local.wait()
            _remote(stage_s.at[0], parts_hbm.at[_peer(x, y, c, SIBLING)[1]], sib_sems.at[0], sib_sems.at[1], sib).wait_recv()
            for p in range(n_ph - 1):
                dev, lin = _peer(x, y, c, RS_CHIPS[p])
                _remote(stage_s.at[0], parts_hbm.at[lin], ici_send.at[p], ici_recv.at[p], dev).wait_recv()
            ex.wait(ex_in, ex_out, ex_sems)

    dma = lambda n: pltpu.SemaphoreType.DMA((n,))
    grid_spec = pltpu.PrefetchScalarGridSpec(
        num_scalar_prefetch=1, grid=(n_ph, n_k),
        in_specs=[pl.BlockSpec((tk, D_MODEL), lambda j, k, order: (k, 0)),
                  pl.BlockSpec((tk, 2 * shard), lambda j, k, order: (k, order[j]))] + [ANY_SPEC] * ex.n,
        out_specs=[ANY_SPEC] * (1 + ex.n),
        scratch_shapes=[pltpu.VMEM((D_MODEL, 2 * shard), F32), pltpu.VMEM((tk, 2 * W_TAIL), BF16),
                        pltpu.VMEM((2, D_MODEL, shard), BF16),
                        pltpu.VMEM((2, D_MODEL, shard), BF16), dma(n_ph - 1), dma(n_ph - 1), dma(n_ph - 1),
                        dma(n_ph - 1), dma(2), dma(1)] + ex.scratch)
    res = pl.pallas_call(
        body, name="grad_w_in", grid_spec=grid_spec,
        out_shape=[SDS((N_DEV, D_MODEL, shard), BF16)] + ex.out_shape,
        compiler_params=_params(("arbitrary", "arbitrary"), 54),
    )(order, hn, dz, *ex_arrs)
    return res[0], slots, res[1:]


def _sum_parts(parts, name):
    def body(p_ref, o_ref):
        g = p_ref[0].astype(F32)
        for s in range(1, parts.shape[0]):
            g = g + p_ref[s].astype(F32)
        o_ref[...] = g

    return pl.pallas_call(body, name=name, out_shape=SDS(parts.shape[1:], F32))(parts)


def _adamw_math(g, w_ref, m_ref, v_ref, g_ref, d_ref, nm_ref, nv_ref):
    c1 = 1.0 - ADAM_B1 ** ADAM_STEP
    c2 = 1.0 - ADAM_B2 ** ADAM_STEP
    g_ref[...] = g
    nm = ADAM_B1 * m_ref[...] + (1.0 - ADAM_B1) * g
    nv = ADAM_B2 * v_ref[...] + (1.0 - ADAM_B2) * (g * g)
    nm_ref[...] = nm
    nv_ref[...] = nv
    d_ref[...] = -ADAM_LR * ((nm / c1) / (jnp.sqrt(nv / c2) + ADAM_EPS) + ADAM_WD * w_ref[...])


def _adamw(parts, w, m, v, name, tr):
    rows, cols = w.shape
    n_parts = parts.shape[0]

    def body(p_ref, *refs):
        g = p_ref[0].astype(F32)
        for s in range(1, n_parts):
            g = g + p_ref[s].astype(F32)
        _adamw_math(g, *refs)

    tile = pl.BlockSpec((tr, cols), lambda i: (i, 0))
    return pl.pallas_call(
        body, name=name, grid=(rows // tr,),
        in_specs=[pl.BlockSpec((n_parts, tr, cols), lambda i: (0, i, 0)), tile, tile, tile],
        out_specs=[tile] * 4, out_shape=[SDS((rows, cols), F32)] * 4,
        compiler_params=_params(("arbitrary",), 40),
    )(parts, w, m, v)


def _adamw_unpacked(grads, triples, name):
    n = len(triples)
    n_rows = [t[0].shape[0] for t in triples]

    def body(g_ref, *refs):
        ins, outs = refs[:3 * n], refs[3 * n:]
        row = 0
        for i in range(n):
            _adamw_math(g_ref[row:row + n_rows[i], :], *ins[3 * i:3 * i + 3], *outs[4 * i:4 * i + 4])
            row += n_rows[i]
        outs[4 * n][...] = g_ref[row:row + ROWS, :]

    out_shape = [SDS((r, LANES), F32) for r in n_rows for _ in range(4)] + [SDS((ROWS, LANES), F32)]
    return pl.pallas_call(
        body, name=name, out_shape=out_shape,
        compiler_params=pltpu.CompilerParams(vmem_limit_bytes=40 * MIB),
    )(grads, *[a for t in triples for a in t])


def _adamw_slots(parts, slots, w, m, v, name, tr):
    rows, cols = w.shape
    n_slots = slots.shape[0]

    def body(slots_ref, *refs):
        g = refs[0][...].astype(F32)
        for s in range(1, n_slots):
            g = g + refs[s][...].astype(F32)
        _adamw_math(g, *refs[n_slots:])

    tile = pl.BlockSpec((tr, cols), lambda i, slots: (i, 0))
    part = lambda s: pl.BlockSpec((None, tr, cols), lambda i, slots: (slots[s], i, 0))
    grid_spec = pltpu.PrefetchScalarGridSpec(
        num_scalar_prefetch=1, grid=(rows // tr,),
        in_specs=[part(s) for s in range(n_slots)] + [tile, tile, tile], out_specs=[tile] * 4)
    return pl.pallas_call(
        body, name=name, grid_spec=grid_spec, out_shape=[SDS((rows, cols), F32)] * 4,
        compiler_params=_params(("arbitrary",), 40),
    )(slots, *([parts] * n_slots), w, m, v)


PACKED = ("gmlp_ln_g", "gmlp_ln_b", "gmlp_ws", "gmlp_bs", "conv_b", "w_a", "b_a", "w_x", "b_x", "lam", "gmlp_out_g",
          "lru_out_g", "post_g")
WEIGHTS = ("pre_g", "w_in", "gmlp_ln_g", "gmlp_ln_b", "gmlp_ws", "gmlp_bs", "conv_w", "conv_b", "w_a", "b_a", "w_x",
           "b_x", "lam", "gmlp_out_g", "lru_out_g", "w_out", "post_g", "w_pe", "w_pg")
LANES = 128


PACK_ROWS = 3200


def _pack(parts):
    rows = [p.reshape(-1, LANES) for p in parts]
    used = sum(r.shape[0] for r in rows)
    return jnp.concatenate(rows + [jnp.zeros((PACK_ROWS - used, LANES), F32)], axis=0)


def _pad_rows(a, rows):
    return jnp.concatenate([a, jnp.zeros((rows - a.shape[0],) + a.shape[1:], a.dtype)], axis=0)


def kernel(x, p, pre_g, w_in, gmlp_ln_g, gmlp_ln_b, gmlp_ws, gmlp_bs, conv_w, conv_b, w_a, b_a, w_x, b_x, lam, gmlp_out_g, lru_out_g, w_out, post_g, w_pe, w_pg, loss_target, m_pre_g, m_w_in, m_gmlp_ln_g, m_gmlp_ln_b, m_gmlp_ws, m_gmlp_bs, m_conv_w, m_conv_b, m_w_a, m_b_a, m_w_x, m_b_x, m_lam, m_gmlp_out_g, m_lru_out_g, m_w_out, m_post_g, m_w_pe, m_w_pg, v_pre_g, v_w_in, v_gmlp_ln_g, v_gmlp_ln_b, v_gmlp_ws, v_gmlp_bs, v_conv_w, v_conv_b, v_w_a, v_b_a, v_w_x, v_b_x, v_lam, v_gmlp_out_g, v_lru_out_g, v_w_out, v_post_g, v_w_pe, v_w_pg):
    args = dict(locals())
    weights = {n: args[n] for n in WEIGHTS}
    m_in = {n: args["m_" + n] for n in WEIGHTS}
    v_in = {n: args["v_" + n] for n in WEIGHTS}
    sm = {n: weights[n][0] for n in PACKED}
    shard_rows = D_MODEL // N_DEV
    xs, ps, tgt = x[0], p[0, 0], loss_target[0]

    vec = lambda a: a.reshape(1, -1)
    tril = jnp.tril(jnp.ones((CHUNK, CHUNK), dtype=bool))
    wm32 = jnp.where(tril[None], sm["gmlp_ws"], 0.0)
    wm, wm_t = wm32.astype(BF16), jnp.swapaxes(wm32, 1, 2).astype(BF16)
    bias = jnp.repeat(sm["gmlp_bs"].T, HEAD, axis=1)
    wax32 = jnp.concatenate([sm["w_a"], sm["w_x"]], axis=2)
    wax, wax_t = wax32.astype(BF16), jnp.swapaxes(wax32, 1, 2).astype(BF16)
    ln_g, ln_b = vec(sm["gmlp_ln_g"]), vec(sm["gmlp_ln_b"])
    post_g_v = vec(sm["post_g"])

    hn = _pre_norm(xs, pre_g)
    cw_shard = _pad_rows(conv_w.reshape(CONV_W, HEAD), ROWS)
    z, w_in_g, (cw_g,) = _in_proj(hn, w_in[0].astype(BF16), [cw_shard])
    cw_full = jnp.transpose(cw_g[:, :CONV_W, :], (1, 0, 2)).reshape(CONV_W, D_BR)
    mixer_consts = dict(cw=_pad_rows(cw_full, ROWS), cb=vec(sm["conv_b"]), ba=vec(sm["b_a"]), bx=vec(sm["b_x"]),
                        lam=vec(sm["lam"]), goa=vec(sm["gmlp_out_g"]), gob=vec(sm["lru_out_g"]))
    (y, h, vhb, xcb, v_rs), (w_out_g, w_pe_g, w_pg_g) = _mix_fwd(
        z, ln_g, ln_b, wm, bias, wax=wax, **mixer_consts,
        ex_arrs=[w_out[0].astype(BF16), w_pe[0].astype(BF16), w_pg[0].astype(BF16)], ex_scatter=[False, False, False])
    w_out_f, w_pg_f = w_out_g.reshape(D_MODEL, D_MODEL), w_pg_g.reshape(D_MODEL, D_MODEL)
    h1, ob = _out_proj(y, xs, w_out_f, post_g_v)
    dh2, dgl, h1b, loss_part, d_w_pe = _ple_loss(h1, ps, tgt, w_pg_f, w_pe_g)

    dh1, do, dy, d_post_g = _tail_bwd(dh2, dgl, ob, w_pg_f, w_out_f, post_g_v)
    d_w_out, _ = _grad_w(y, do, 1024, False, "grad_w_out")
    d_w_pg, _ = _grad_w(h1b, dgl, 1024, False, "grad_w_pg")
    (dz, vecs, d_ws, d_wax, d_bs), (parts_out, parts_pg, parts_pe) = _mix_bwd(
        z, dy, h, vhb, xcb, v_rs, ln_g, ln_b, wm, wm_t, bias, wax=wax, wax_t=wax_t, **mixer_consts,
        ex_arrs=[d_w_out.reshape(N_DEV, shard_rows, D_MODEL), d_w_pg.reshape(N_DEV, shard_rows, D_MODEL), d_w_pe],
        ex_scatter=[True, True, True])

    small = {"gmlp_ln_g": vecs[V_LN_G], "gmlp_ln_b": vecs[V_LN_B], "gmlp_ws": d_ws, "gmlp_bs": d_bs,
             "conv_b": vecs[V_CONV_B], "w_a": d_wax[:, :, :HEAD], "b_a": vecs[V_B_A], "w_x": d_wax[:, :, HEAD:],
             "b_x": vecs[V_B_X], "lam": vecs[V_LAM], "gmlp_out_g": vecs[V_GOUT_A], "lru_out_g": vecs[V_GOUT_B],
             "post_g": d_post_g}
    small_part = _pack([small[n] for n in PACKED] + [loss_part]).reshape(N_DEV, PACK_ROWS // N_DEV, LANES)
    d_cw_blocks = jnp.transpose(vecs[V_CONV_W:V_CONV_W + CONV_W].reshape(CONV_W, N_DEV, HEAD), (1, 0, 2))
    d_cw_blocks = jnp.concatenate([d_cw_blocks, jnp.zeros((N_DEV, ROWS - CONV_W, HEAD), F32)], axis=1)
    parts_in, slots_in, (small_blocks, parts_cw) = _grad_w_in_pairs(
        hn, dz, ex_arrs=[small_part, d_cw_blocks], ex_scatter=[True, True])
    small_sum = _sum_parts(small_blocks, "sum_small")
    grad_x, d_pre_g = _in_bwd(dz, w_in_g, xs, dh1, pre_g)
    pre_rows = D_MODEL // LANES
    small_all, parts_pre = _exchange([small_sum, d_pre_g.reshape(pre_rows, LANES)], False, "gather_small_grads")

    pad_cw = lambda a: _pad_rows(a.reshape(CONV_W, HEAD), ROWS)
    flat = lambda a: a.reshape(pre_rows, LANES)
    outs = {
        "w_in": _adamw_slots(parts_in, slots_in, w_in[0], m_w_in[0], v_w_in[0], "adamw_w_in", 256),
        "w_out": _adamw(parts_out, w_out[0], m_w_out[0], v_w_out[0], "adamw_w_out", 128),
        "w_pe": _adamw(parts_pe, w_pe[0], m_w_pe[0], v_w_pe[0], "adamw_w_pe", 256),
        "w_pg": _adamw(parts_pg, w_pg[0], m_w_pg[0], v_w_pg[0], "adamw_w_pg", 128),
        "conv_w": [a[:CONV_W] for a in
                   _adamw(parts_cw, pad_cw(conv_w), pad_cw(m_conv_w), pad_cw(v_conv_w), "adamw_conv_w", ROWS)],
        "pre_g": _adamw(parts_pre, flat(pre_g), flat(m_pre_g), flat(v_pre_g), "adamw_pre_g", pre_rows),
    }
    as_rows = lambda a: a.reshape(-1, LANES)
    small_res = _adamw_unpacked(small_all.reshape(PACK_ROWS, LANES),
                                [(as_rows(weights[n]), as_rows(m_in[n]), as_rows(v_in[n])) for n in PACKED], "adamw_small")
    for i, n in enumerate(PACKED):
        outs[n] = small_res[4 * i:4 * i + 4]
    loss = small_res[-1][0, 0]

    result = [loss, grad_x[None]]
    for q in range(4):
        result += [outs[n][q].reshape(weights[n].shape) for n in WEIGHTS]
    return tuple(result)
```

```python
import jax
import jax.numpy as jnp
from jax import lax
from jax.experimental import pallas as pl
from jax.experimental.pallas import tpu as pltpu

F32 = jnp.float32
BF16 = jnp.bfloat16
SDS = jax.ShapeDtypeStruct

D_MODEL = 2048
D_BR = 1024
D_IN = 5 * D_BR
D_PLE = 256
N_HEAD = 8
HEAD = 128
CHUNK = 128
ROWS = 8
N_GROUP = CHUNK // ROWS
MIX_SUB = 2
N_DEV = 8
W_IN_SHARD = D_IN // N_DEV
EPS = 1e-6
LRU_C = 8.0
CONV_W = 4
MIB = 1 << 20

ADAM_LR, ADAM_B1, ADAM_B2, ADAM_EPS, ADAM_WD, ADAM_STEP = 0.001, 0.9, 0.999, 1e-08, 0.01, 10

_GELU_C = 0.7978845608028654
_GELU_A = 0.044715

V_LN_G, V_LN_B, V_CONV_B, V_B_A, V_B_X, V_LAM, V_GOUT_A, V_GOUT_B, V_CONV_W = 0, 1, 2, 3, 4, 5, 6, 7, 8
N_VEC = 16


def _params(sem, vmem_mib):
    return pltpu.CompilerParams(dimension_semantics=sem, vmem_limit_bytes=int(vmem_mib * MIB))


def _sig(x):
    return 0.5 * jnp.tanh(0.5 * x) + 0.5


def _gelu(x, with_grad=False):
    sq = x * x
    t = jnp.tanh(x * (_GELU_C + (_GELU_C * _GELU_A) * sq))
    half, one_t = 0.5 * x, 1.0 + t
    if not with_grad:
        return half * one_t
    grad = 0.5 * one_t + half * ((1.0 - t) * one_t) * (_GELU_C + (3.0 * _GELU_C * _GELU_A) * sq)
    return half * one_t, grad


def _silu_grad(s, xs):
    return s + xs * (1.0 - s)


def _neg_expm1(y, exp_y):
    series = -y * (1.0 + y * (0.5 + y * (1.0 / 6.0)))
    return jnp.where(y > -0.01, series, 1.0 - exp_y)


def _softplus(x):
    return jnp.maximum(x, 0.0) + jnp.log(1.0 + jnp.exp(-jnp.abs(x)))


def _row_ids(width):
    return lax.broadcasted_iota(jnp.int32, (ROWS, width), 0)


def _shift_down(cur, prev, k, rid):
    return jnp.where(rid >= k, pltpu.roll(cur, k, 0), pltpu.roll(prev, k, 0))


def _shift_up(cur, nxt, k, rid):
    return jnp.where(rid < ROWS - k, pltpu.roll(cur, ROWS - k, 0), pltpu.roll(nxt, ROWS - k, 0))


def _mean_last(x):
    return jnp.mean(x, axis=-1, keepdims=True)


def _rows(g):
    return pl.ds(pl.multiple_of(g * ROWS, ROWS), ROWS)


TILE_ROWS = 16


def _tile_rows(q):
    return pl.ds(pl.multiple_of(q * TILE_ROWS, TILE_ROWS), TILE_ROWS)


UNROLL = 4
TILE_UNROLL = 8


def _loop(n, body, init, unroll=UNROLL):
    def wide(i, carry):
        for u in range(unroll):
            carry = body(i * unroll + u, carry)
        return carry

    return lax.fori_loop(0, n // unroll, wide, init)


def _fold_rows(x):
    return x[0:ROWS, :] + x[ROWS:TILE_ROWS, :]


def _bcast_row(x, r):
    return jnp.broadcast_to(x[r:r + 1, :], x.shape)


def _dot(a, b):
    return jnp.dot(a, b, preferred_element_type=F32)


def _dot_nt(a, b):
    return lax.dot_general(a, b, (((1,), (1,)), ((), ())), preferred_element_type=F32)


def _dot_tn(a, b):
    return lax.dot_general(a, b, (((0,), (0,)), ((), ())), preferred_element_type=F32)


def _mesh_place():
    x, y, c = lax.axis_index("x"), lax.axis_index("y"), lax.axis_index("c")
    return x, y, c, 4 * x + 2 * y + c


def _peer(x, y, c, k):
    px = 1 - x if k & 4 else x
    py = 1 - y if k & 2 else y
    pc = 1 - c if k & 1 else c
    return (px, py, pc), 4 * px + 2 * py + pc


def _remote(src, dst, send_sem, recv_sem, dev):
    return pltpu.make_async_remote_copy(src_ref=src, dst_ref=dst, send_sem=send_sem, recv_sem=recv_sem, device_id=dev,
                                        device_id_type=pl.DeviceIdType.MESH)


ANY_SPEC = pl.BlockSpec(memory_space=pl.ANY)


class _Exchange:
    def __init__(self, arrs, scatter):
        self.n = len(arrs)
        self.scatter = tuple(scatter)
        self.out_shape = [SDS(a.shape if s else (N_DEV,) + a.shape, a.dtype) for a, s in zip(arrs, scatter)]
        self.scratch = [pltpu.SemaphoreType.DMA((self.n * N_DEV,)), pltpu.SemaphoreType.DMA((self.n * N_DEV,)),
                        pltpu.SemaphoreType.DMA((self.n,))]

    def _copies(self, ins, outs, sems):
        send_sems, recv_sems, local_sems = sems
        x, y, c, me = _mesh_place()
        local, sends, recvs = [], [], []
        for a in range(self.n):
            src = ins[a].at[me] if self.scatter[a] else ins[a]
            local.append(pltpu.make_async_copy(src, outs[a].at[me], local_sems.at[a]))
        for k in range(1, N_DEV):
            dev, lin = _peer(x, y, c, k)
            for a in range(self.n):
                src = ins[a].at[lin] if self.scatter[a] else ins[a]
                pair = (send_sems.at[a * N_DEV + k], recv_sems.at[a * N_DEV + k], dev)
                sends.append(_remote(src, outs[a].at[me], *pair))
                recvs.append(_remote(src, outs[a].at[lin], *pair))
        return local, sends, recvs

    def start(self, ins, outs, sems):
        local, sends, _ = self._copies(ins, outs, sems)
        for cp in local + sends:
            cp.start()

    def wait(self, ins, outs, sems):
        local, sends, recvs = self._copies(ins, outs, sems)
        for cp in recvs:
            cp.wait_recv()
        for cp in sends:
            cp.wait_send()
        for cp in local:
            cp.wait()


def _exchange(arrs, scatter, name):
    ex = _Exchange(arrs, [scatter] * len(arrs))
    n = ex.n

    def body(*refs):
        ins, outs, sems = refs[:n], refs[n:2 * n], refs[2 * n:]
        ex.start(ins, outs, sems)
        ex.wait(ins, outs, sems)

    return pl.pallas_call(
        body, name=name, out_shape=ex.out_shape, in_specs=[ANY_SPEC] * n, out_specs=[ANY_SPEC] * n,
        scratch_shapes=ex.scratch,
    )(*arrs)


def _pre_norm(x, pre_g, tm=512):
    t_len = x.shape[0]

    def body(x_ref, g_ref, hn_ref):
        g = g_ref[...]

        def rows_body(q, _):
            rows = _tile_rows(q)
            xv = x_ref[rows, :]
            hn_ref[rows, :] = (xv * lax.rsqrt(_mean_last(xv * xv) + EPS) * g).astype(BF16)
            return 0

        _loop(tm // TILE_ROWS, rows_body, 0, unroll=TILE_UNROLL)

    tile = pl.BlockSpec((tm, D_MODEL), lambda i: (i, 0))
    return pl.pallas_call(
        body, name="pre_norm", grid=(t_len // tm,),
        in_specs=[tile, pl.BlockSpec((1, D_MODEL), lambda i: (0, 0))], out_specs=tile,
        out_shape=SDS((t_len, D_MODEL), BF16),
        compiler_params=_params(("arbitrary",), 24),
    )(x, pre_g)


CHIP_ORDER = (0, 2, 4, 6)
W_BODY, W_TAIL = 512, 128
SIBLING = 1
ICI_MASKS = (2, 4, 6)
DIRECT_MASKS = (SIBLING,) + ICI_MASKS
Y_NEIGHBOUR, X_NEIGHBOUR, DIAGONAL = 2, 4, 6
W_DIRECT = (SIBLING, Y_NEIGHBOUR, X_NEIGHBOUR)


def _in_proj(hn, w_shard, others, tm=1024):
    t_len = hn.shape[0]
    n_i = t_len // tm
    n_o = len(others)
    me_out = 4 * lax.axis_index("x") + 2 * lax.axis_index("y") + lax.axis_index("c")
    order = jnp.stack([(me_out ^ chip) // 2 for chip in CHIP_ORDER]).astype(jnp.int32)

    def body(order_ref, hn_ref, w_hbm, *refs):
        o_in = refs[:n_o]
        z_ref, wg_hbm = refs[n_o], refs[n_o + 1]
        o_out = refs[n_o + 2:2 * n_o + 2]
        (wbuf, tail_s, send_w, recv_w, fsend_w, frecv_w, send_o, recv_o, fsend_o, frecv_o, wb_sems, loc_sems, rsend,
         rrecv) = refs[2 * n_o + 2:]
        j, i = pl.program_id(0), pl.program_id(1)
        x, y, c, me = _mesh_place()
        sib = _peer(x, y, c, SIBLING)[0]

        def relay(core):
            src, dst = (Y_NEIGHBOUR, X_NEIGHBOUR) if core == 0 else (X_NEIGHBOUR, Y_NEIGHBOUR)
            held, diag = _peer(x, y, c, src)[1], _peer(x, y, c, DIAGONAL)[1]
            pair = (rsend.at[0], rrecv.at[0], _peer(x, y, c, dst)[0])
            return _remote(wbuf.at[held], wbuf.at[held], *pair), _remote(wbuf.at[diag], wbuf.at[diag], *pair)

        def direct(k, a=None):
            dev, lin = _peer(x, y, c, k)
            if a is None:
                return (_remote(w_hbm, wbuf.at[me], send_w.at[k], recv_w.at[k], dev),
                        _remote(w_hbm, wbuf.at[lin], send_w.at[k], recv_w.at[k], dev))
            pair = (send_o.at[a * N_DEV + k], recv_o.at[a * N_DEV + k], dev)
            return _remote(o_in[a], o_out[a].at[me], *pair), _remote(o_in[a], o_out[a].at[lin], *pair)

        def passed(k, a=None):
            mine, theirs = _peer(x, y, c, k)[1], _peer(x, y, c, k ^ SIBLING)[1]
            if a is None:
                pair = (fsend_w.at[k], frecv_w.at[k], sib)
                return _remote(wbuf.at[mine], wbuf.at[mine], *pair), _remote(wbuf.at[theirs], wbuf.at[theirs], *pair)
            pair = (fsend_o.at[a * N_DEV + k], frecv_o.at[a * N_DEV + k], sib)
            return (_remote(o_out[a].at[mine], o_out[a].at[mine], *pair),
                    _remote(o_out[a].at[theirs], o_out[a].at[theirs], *pair))

        def own_copies():
            return [pltpu.make_async_copy(o_in[a], o_out[a].at[me], loc_sems.at[1 + a]) for a in range(n_o)]

        @pl.when(jnp.logical_and(j == 0, i == 0))
        def _():
            own = pltpu.make_async_copy(w_hbm, wbuf.at[me], loc_sems.at[0])
            own.start()
            for cp in own_copies():
                cp.start()
            for k in W_DIRECT:
                direct(k)[0].start()
            for k in DIRECT_MASKS:
                for a in range(n_o):
                    direct(k, a)[0].start()
            own.wait()

        low = 2 * order_ref[j]

        for jp, chip in enumerate(CHIP_ORDER):
            @pl.when(jnp.logical_and(j == jp, i == 0))
            def _(jp=jp, chip=chip):
                if chip == 0:
                    direct(SIBLING)[1].wait_recv()
                elif chip == Y_NEIGHBOUR:
                    for mask in (Y_NEIGHBOUR, X_NEIGHBOUR):
                        direct(mask)[1].wait_recv()
                        passed(mask)[0].start()
                    for core in (0, 1):
                        @pl.when(c == core)
                        def _(core=core):
                            relay(core)[0].start()
                    passed(Y_NEIGHBOUR)[1].wait_recv()
                elif chip == X_NEIGHBOUR:
                    passed(X_NEIGHBOUR)[1].wait_recv()
                    for core in (0, 1):
                        @pl.when(c == core)
                        def _(core=core):
                            relay(core)[1].wait_recv()
                    passed(DIAGONAL)[0].start()
                    for k in ICI_MASKS:
                        for a in range(n_o):
                            direct(k, a)[1].wait_recv()
                            passed(k, a)[0].start()
                else:
                    passed(DIAGONAL)[1].wait_recv()
                for half in (0, 1):
                    pltpu.make_async_copy(wbuf.at[low + half], wg_hbm.at[low + half], wb_sems.at[2 * jp + half]).start()
                tail_s[:, 0:W_TAIL] = wbuf[low, :, W_BODY:W_IN_SHARD]
                tail_s[:, W_TAIL:2 * W_TAIL] = wbuf[low + 1, :, W_BODY:W_IN_SHARD]

        hn = hn_ref[...]
        z_ref[:, 0:W_BODY] = _dot(hn, wbuf[low, :, 0:W_BODY])
        z_ref[:, W_IN_SHARD:W_IN_SHARD + W_BODY] = _dot(hn, wbuf[low + 1, :, 0:W_BODY])
        tails = _dot(hn, tail_s[...])
        z_ref[:, W_BODY:W_IN_SHARD] = tails[:, 0:W_TAIL]
        z_ref[:, W_IN_SHARD + W_BODY:2 * W_IN_SHARD] = tails[:, W_TAIL:2 * W_TAIL]

        @pl.when(jnp.logical_and(j == len(CHIP_ORDER) - 1, i == n_i - 1))
        def _():
            for a in range(n_o):
                direct(SIBLING, a)[1].wait_recv()
            for k in ICI_MASKS:
                for a in range(n_o):
                    passed(k, a)[1].wait_recv()
            for k in W_DIRECT:
                direct(k)[0].wait_send()
            for core in (0, 1):
                @pl.when(c == core)
                def _(core=core):
                    relay(core)[0].wait_send()
            for k in DIRECT_MASKS:
                for a in range(n_o):
                    direct(k, a)[0].wait_send()
            for k in ICI_MASKS:
                passed(k)[0].wait_send()
                for a in range(n_o):
                    passed(k, a)[0].wait_send()
            for cp in own_copies():
                cp.wait()
            for jj in range(N_DEV):
                pltpu.make_async_copy(wbuf.at[0], wg_hbm.at[0], wb_sems.at[jj]).wait()

    dma = lambda n: pltpu.SemaphoreType.DMA((n,))
    grid_spec = pltpu.PrefetchScalarGridSpec(
        num_scalar_prefetch=1, grid=(len(CHIP_ORDER), n_i),
        in_specs=[pl.BlockSpec((tm, D_MODEL), lambda j, i, order: (i, 0)), ANY_SPEC] + [ANY_SPEC] * n_o,
        out_specs=[pl.BlockSpec((tm, 2 * W_IN_SHARD), lambda j, i, order: (i, order[j])), ANY_SPEC] + [ANY_SPEC] * n_o,
        scratch_shapes=[pltpu.VMEM((N_DEV, D_MODEL, W_IN_SHARD), BF16), pltpu.VMEM((D_MODEL, 2 * W_TAIL), BF16),
                        dma(N_DEV), dma(N_DEV), dma(N_DEV), dma(N_DEV),
                        dma(n_o * N_DEV), dma(n_o * N_DEV), dma(n_o * N_DEV), dma(n_o * N_DEV), dma(N_DEV), dma(1 + n_o),
                        dma(1), dma(1)])
    res = pl.pallas_call(
        body, name="in_proj", grid_spec=grid_spec,
        out_shape=[SDS((t_len, D_IN), F32), SDS((N_DEV, D_MODEL, W_IN_SHARD), BF16)]
        + [SDS((N_DEV,) + o.shape, o.dtype) for o in others],
        compiler_params=_params(("arbitrary", "arbitrary"), 54),
    )(order, hn, w_shard, *others)
    return res[0], res[1], res[2:]


def _conv_rows(cur, prev, cw_ref, cb, rid):
    acc = cw_ref[3:4, :] * cur + cb
    for k in range(1, CONV_W):
        acc = acc + cw_ref[3 - k:4 - k, :] * _shift_down(cur, prev, k, rid)
    return acc


ROW0_LOG_A = -1e30


def _row0_mask(rid):
    return jnp.where(rid == 0, ROW0_LOG_A, 0.0)


def _row0_bias(is_first_group, row0_mask):
    return is_first_group.astype(F32) * row0_mask


def _lru_gates(pa, px, ba, bx, sp8, row0_bias):
    r = _sig(pa + ba)
    i = _sig(px + bx)
    la = row0_bias - r * sp8
    a = jnp.exp(la)
    return r, i, a, _neg_expm1(2.0 * la, a * a)


def _mix_fwd(z, ln_g, ln_b, wm, bias, cw, cb, wax, ba, bx, lam, goa, gob, ex_arrs, ex_scatter):
    t_len = z.shape[0]
    n_chunk = t_len // CHUNK
    ex = _Exchange(ex_arrs, ex_scatter)
    n_in, n_out, n_scratch = 13, 5, 7

    def body(*refs):
        (z_ref, lng_ref, lnb_ref, wm_ref, bias_ref, cw_ref, cb_ref, wax_ref, ba_ref, bx_ref, lam_ref, goa_ref,
         gob_ref) = refs[:n_in]
        ex_in = refs[n_in:n_in + ex.n]
        y_ref, h_ref, vhb_ref, xcb_ref, rs_ref = refs[n_in + ex.n:n_in + ex.n + n_out]
        ex_out = refs[n_in + ex.n + n_out:n_in + 2 * ex.n + n_out]
        vn_s, xc_s, mixed_s, pre_s, y_s, carry_s, halo_s = refs[n_in + 2 * ex.n + n_out:n_in + 2 * ex.n + n_out + n_scratch]
        ex_sems = refs[n_in + 2 * ex.n + n_out + n_scratch:]
        step = pl.program_id(0)
        rid = _row_ids(D_BR)

        @pl.when(step == 0)
        def _():
            ex.start(ex_in, ex_out, ex_sems)
            carry_s[...] = jnp.zeros_like(carry_s)
            halo_s[...] = jnp.zeros_like(halo_s)

        lng, lnb, cb = lng_ref[...], lnb_ref[...], cb_ref[...]
        ba, bx, goa, gob = ba_ref[...], bx_ref[...], goa_ref[...], gob_ref[...]
        sp8 = LRU_C * _softplus(-lam_ref[...])
        row0 = _row0_mask(rid)

        def chunk(c_id, z_ref, y_ref, h_ref, vhb_ref, xcb_ref, rs_ref):
            def phase1(g, prev):
                rows = _rows(g)
                vg = _gelu(z_ref[rows, D_BR:2 * D_BR])
                xm = vg - _mean_last(vg)
                rs = lax.rsqrt(_mean_last(xm * xm) + EPS)
                vn_s[rows, :] = xm * rs
                rs_ref[rows, :] = jnp.broadcast_to(rs, (ROWS, HEAD))
                xb = z_ref[rows, 3 * D_BR:4 * D_BR]
                xc_s[rows, :] = _conv_rows(xb, prev, cw_ref, cb, rid)
                return xb

            halo_s[...] = _loop(N_GROUP, phase1, halo_s[...], unroll=8)
            vhb_ref[...] = vn_s[...].astype(BF16)
            xcb_ref[...] = xc_s[...].astype(BF16)

            for h in range(N_HEAD):
                cs = slice(h * HEAD, (h + 1) * HEAD)
                mixed_s[:, cs] = _dot(wm_ref[h], (vn_s[:, cs] * lng[:, cs] + lnb[:, cs]).astype(BF16))
                pre = _dot(xcb_ref[:, cs], wax_ref[h])
                pre_s[:, cs] = pre[:, :HEAD]
                pre_s[:, D_BR + h * HEAD:D_BR + (h + 1) * HEAD] = pre[:, HEAD:]

            def phase3(g, carry):
                rows = _rows(g)
                ug = _gelu(z_ref[rows, 0:D_BR])
                ga = z_ref[rows, 2 * D_BR:3 * D_BR]
                ya = ug * (mixed_s[rows, :] + bias_ref[rows, :]) * (ga * _sig(ga))
                y_s[rows, 0:D_BR] = ya * lax.rsqrt(_mean_last(ya * ya) + EPS) * goa

                bias0 = _row0_bias(jnp.logical_and(c_id == 0, g == 0), row0)
                _, i, a, m2 = _lru_gates(pre_s[rows, 0:D_BR], pre_s[rows, D_BR:2 * D_BR], ba, bx, sp8, bias0)
                b = jnp.sqrt(m2) * i * xc_s[rows, :]
                for d in (1, 2, 4):
                    a_sh = jnp.where(rid >= d, pltpu.roll(a, d, 0), 1.0)
                    b_sh = jnp.where(rid >= d, pltpu.roll(b, d, 0), 0.0)
                    b = a * b_sh + b
                    a = a * a_sh
                hh = b + a * carry
                h_ref[rows, :] = hh
                gb = z_ref[rows, 4 * D_BR:5 * D_BR]
                yb = hh * (gb * _sig(gb))
                y_s[rows, D_BR:2 * D_BR] = yb * lax.rsqrt(_mean_last(yb * yb) + EPS) * gob
                return _bcast_row(hh, ROWS - 1)

            carry_s[...] = _loop(N_GROUP, phase3, carry_s[...])
            y_ref[...] = y_s[...].astype(BF16)

        for sub in range(MIX_SUB):
            part = lambda ref, sub=sub: ref.at[pl.ds(sub * CHUNK, CHUNK)]
            chunk(step * MIX_SUB + sub, part(z_ref), part(y_ref), part(h_ref), part(vhb_ref), part(xcb_ref),
                  part(rs_ref))

        @pl.when(step == n_chunk // MIX_SUB - 1)
        def _():
            ex.wait(ex_in, ex_out, ex_sems)

    vec = pl.BlockSpec((1, D_BR), lambda i: (0, 0))
    blk = MIX_SUB * CHUNK
    res = pl.pallas_call(
        body, name="mix_fwd", grid=(n_chunk // MIX_SUB,),
        in_specs=[pl.BlockSpec((blk, D_IN), lambda i: (i, 0)), vec, vec,
                  pl.BlockSpec((N_HEAD, HEAD, HEAD), lambda i: (0, 0, 0)),
                  pl.BlockSpec((CHUNK, D_BR), lambda i: (0, 0)),
                  pl.BlockSpec((ROWS, D_BR), lambda i: (0, 0)), vec,
                  pl.BlockSpec((N_HEAD, HEAD, 2 * HEAD), lambda i: (0, 0, 0)), vec, vec, vec, vec, vec]
        + [ANY_SPEC] * ex.n,
        out_specs=[pl.BlockSpec((blk, 2 * D_BR), lambda i: (i, 0)), pl.BlockSpec((blk, D_BR), lambda i: (i, 0)),
                   pl.BlockSpec((blk, D_BR), lambda i: (i, 0)), pl.BlockSpec((blk, D_BR), lambda i: (i, 0)),
                   pl.BlockSpec((blk, HEAD), lambda i: (i, 0))] + [ANY_SPEC] * ex.n,
        out_shape=[SDS((t_len, 2 * D_BR), BF16), SDS((t_len, D_BR), F32), SDS((t_len, D_BR), BF16),
                   SDS((t_len, D_BR), BF16), SDS((t_len, HEAD), F32)] + ex.out_shape,
        scratch_shapes=[pltpu.VMEM((CHUNK, D_BR), F32), pltpu.VMEM((CHUNK, D_BR), F32), pltpu.VMEM((CHUNK, D_BR), F32),
                        pltpu.VMEM((CHUNK, 2 * D_BR), F32), pltpu.VMEM((CHUNK, 2 * D_BR), F32),
                        pltpu.VMEM((ROWS, D_BR), F32), pltpu.VMEM((ROWS, D_BR), F32)] + ex.scratch,
        compiler_params=_params(("arbitrary",), 32),
    )(z, ln_g, ln_b, wm, bias, cw, cb, wax, ba, bx, lam, goa, gob, *ex_arrs)
    return res[:n_out], res[n_out:]


def _load_weight(w_hbm, w_vmem, sem):
    @pl.when(pl.program_id(0) == 0)
    def _():
        cp = pltpu.make_async_copy(w_hbm, w_vmem, sem)
        cp.start()
        cp.wait()


def _out_proj(y, x, w_out, post_g, tm=512):
    t_len = y.shape[0]

    def body(y_ref, x_ref, w_hbm, g_ref, h1_ref, ob_ref, w_s, o_s, sem):
        _load_weight(w_hbm, w_s, sem)
        o_s[...] = _dot(y_ref[...], w_s[...])
        g = g_ref[...]

        def rows_body(q, _):
            rows = _tile_rows(q)
            o = o_s[rows, :]
            h1_ref[rows, :] = x_ref[rows, :] + o * lax.rsqrt(_mean_last(o * o) + EPS) * g
            ob_ref[rows, :] = o.astype(BF16)
            return 0

        _loop(tm // TILE_ROWS, rows_body, 0, unroll=TILE_UNROLL)

    tile = pl.BlockSpec((tm, D_MODEL), lambda i: (i, 0))
    return pl.pallas_call(
        body, name="out_proj", grid=(t_len // tm,),
        in_specs=[tile, tile, pl.BlockSpec(memory_space=pl.ANY), pl.BlockSpec((1, D_MODEL), lambda i: (0, 0))],
        out_specs=[tile, tile],
        out_shape=[SDS((t_len, D_MODEL), F32), SDS((t_len, D_MODEL), BF16)],
        scratch_shapes=[pltpu.VMEM((D_MODEL, D_MODEL), BF16), pltpu.VMEM((tm, D_MODEL), F32), pltpu.SemaphoreType.DMA],
        compiler_params=_params(("arbitrary",), 44),
    )(y, x, w_out, post_g)


def _ple_loss(h1, p, tgt, w_pg, w_pe_g, tm=256):
    t_len = h1.shape[0]
    n_tile = t_len // tm
    pe_shard = D_MODEL // N_DEV

    def body(h1_ref, p_ref, t_ref, w_hbm, wpe_ref, dh2_ref, dgl_ref, h1b_ref, loss_ref, dwpe_ref, w_s, pe_s, gl_s, acc_s,
             dpe_s, gpe_s, sem):
        _load_weight(w_hbm, w_s, sem)
        i = pl.program_id(0)

        @pl.when(i == 0)
        def _():
            acc_s[...] = jnp.zeros_like(acc_s)
            gpe_s[...] = jnp.zeros_like(gpe_s)

        h1b_ref[...] = h1_ref[...].astype(BF16)
        pb = p_ref[...].astype(BF16)
        for j in range(N_DEV):
            cols = slice(j * pe_shard, (j + 1) * pe_shard)
            pe_s[:, cols] = _dot(pb, wpe_ref[j])
            gl_s[:, cols] = _dot(h1b_ref[...], w_s[:, cols])
            acc = acc_s[:, cols]
            for q in range(tm // TILE_ROWS):
                rows = slice(q * TILE_ROWS, (q + 1) * TILE_ROWS)
                pe = pe_s[rows, cols]
                g = _sig(gl_s[rows, cols])
                e = h1_ref[rows, cols] + pe * g - t_ref[rows, cols]
                dh2 = e * (1.0 / D_MODEL)
                dh2_ref[rows, cols] = dh2
                dpe_s[rows, cols] = (dh2 * g).astype(BF16)
                dgl_ref[rows, cols] = (dh2 * pe * g * (1.0 - g)).astype(BF16)
                acc = acc + _fold_rows(e * e)
            acc_s[:, cols] = acc
        gpe_s[...] += _dot_tn(pb, dpe_s[...])

        @pl.when(i == n_tile - 1)
        def _():
            loss_ref[...] = jnp.full(loss_ref.shape, 0.5 / D_MODEL * jnp.sum(acc_s[...]), F32)
            for j in range(N_DEV):
                dwpe_ref[j] = gpe_s[:, j * pe_shard:(j + 1) * pe_shard].astype(BF16)

    tile = pl.BlockSpec((tm, D_MODEL), lambda i: (i, 0))
    pe_blocks = pl.BlockSpec((N_DEV, D_PLE, pe_shard), lambda i: (0, 0, 0))
    return pl.pallas_call(
        body, name="ple_loss", grid=(n_tile,),
        in_specs=[tile, pl.BlockSpec((tm, D_PLE), lambda i: (i, 0)), tile, pl.BlockSpec(memory_space=pl.ANY), pe_blocks],
        out_specs=[tile, tile, tile, pl.BlockSpec((ROWS, HEAD), lambda i: (0, 0)), pe_blocks],
        out_shape=[SDS((t_len, D_MODEL), F32), SDS((t_len, D_MODEL), BF16), SDS((t_len, D_MODEL), BF16),
                   SDS((ROWS, HEAD), F32), SDS((N_DEV, D_PLE, pe_shard), BF16)],
        scratch_shapes=[pltpu.VMEM((D_MODEL, D_MODEL), BF16), pltpu.VMEM((tm, D_MODEL), F32),
                        pltpu.VMEM((tm, D_MODEL), F32), pltpu.VMEM((ROWS, D_MODEL), F32), pltpu.VMEM((tm, D_MODEL), BF16),
                        pltpu.VMEM((D_PLE, D_MODEL), F32), pltpu.SemaphoreType.DMA],
        compiler_params=_params(("arbitrary",), 48),
    )(h1, p, tgt, w_pg, w_pe_g)


def _tail_bwd(dh2, dgl, ob, w_pg, w_out, post_g, tm=256):
    t_len = dh2.shape[0]
    n_tile = t_len // tm

    def body(dh2_ref, dgl_ref, ob_ref, wpg_hbm, wout_hbm, g_ref, dh1_ref, do_ref, dy_ref, dg_ref, wpg_s, wout_s, t_s,
             acc_s, sems):
        i = pl.program_id(0)
        load_wpg = pltpu.make_async_copy(wpg_hbm, wpg_s, sems.at[0])
        load_wout = pltpu.make_async_copy(wout_hbm, wout_s, sems.at[1])

        @pl.when(i == 0)
        def _():
            load_wpg.start()
            load_wout.start()
            acc_s[...] = jnp.zeros_like(acc_s)
            load_wpg.wait()

        g = g_ref[...]
        blk = D_MODEL // N_DEV
        row_groups = [slice(q * TILE_ROWS, (q + 1) * TILE_ROWS) for q in range(tm // TILE_ROWS)]
        rr = []
        for rows in row_groups:
            o = ob_ref[rows, :].astype(F32)
            rr.append(lax.rsqrt(_mean_last(o * o) + EPS))
        part = [jnp.zeros((TILE_ROWS, LANES), F32) for _ in row_groups]
        for j in range(N_DEV):
            cols = slice(j * blk, (j + 1) * blk)
            t_s[:, cols] = _dot_nt(dgl_ref[...], wpg_s[cols, :])
            acc = acc_s[:, cols]
            for q, rows in enumerate(row_groups):
                dh1 = dh2_ref[rows, cols] + t_s[rows, cols]
                dh1_ref[rows, cols] = dh1
                on = ob_ref[rows, cols].astype(F32) * rr[q]
                pr = dh1 * g[:, cols] * on
                for k in range(blk // LANES):
                    part[q] = part[q] + pr[:, k * LANES:(k + 1) * LANES]
                acc = acc + _fold_rows(dh1 * on)
            acc_s[:, cols] = acc
        for q, rows in enumerate(row_groups):
            m = jnp.sum(part[q], axis=-1, keepdims=True) * (1.0 / D_MODEL)
            on = ob_ref[rows, :].astype(F32) * rr[q]
            do_ref[rows, :] = (rr[q] * (dh1_ref[rows, :] * g - on * m)).astype(BF16)

        @pl.when(i == 0)
        def _():
            load_wout.wait()

        dy_ref[...] = _dot_nt(do_ref[...], wout_s[...]).astype(BF16)

        @pl.when(i == n_tile - 1)
        def _():
            dg_ref[...] = jnp.sum(acc_s[...], axis=0, keepdims=True)

    tile = pl.BlockSpec((tm, D_MODEL), lambda i: (i, 0))
    vec = pl.BlockSpec((1, D_MODEL), lambda i: (0, 0))
    hbm = pl.BlockSpec(memory_space=pl.ANY)
    return pl.pallas_call(
        body, name="tail_bwd", grid=(n_tile,),
        in_specs=[tile, tile, tile, hbm, hbm, vec],
        out_specs=[tile, tile, tile, vec],
        out_shape=[SDS((t_len, D_MODEL), F32), SDS((t_len, D_MODEL), BF16), SDS((t_len, D_MODEL), BF16),
                   SDS((1, D_MODEL), F32)],
        scratch_shapes=[pltpu.VMEM((D_MODEL, D_MODEL), BF16), pltpu.VMEM((D_MODEL, D_MODEL), BF16),
                        pltpu.VMEM((tm, D_MODEL), F32), pltpu.VMEM((ROWS, D_MODEL), F32), pltpu.SemaphoreType.DMA((2,))],
        compiler_params=_params(("arbitrary",), 48),
    )(dh2, dgl, ob, w_pg, w_out, post_g)


def _mix_bwd(z, dy, h, vhb, xcb, rs, ln_g, ln_b, wm, wm_t, bias, cw, cb, wax, wax_t, ba, bx, lam, goa, gob, ex_arrs,
             ex_scatter):
    t_len = z.shape[0]
    n_chunk = t_len // CHUNK
    halo_blocks = CHUNK // ROWS
    ex = _Exchange(ex_arrs, ex_scatter)
    n_in, n_out, n_scratch = 21, 5, 16

    blocked = (0, 1, 2, 4, 5, 6, n_in + ex.n)

    def body(*refs):
        step = pl.program_id(0)
        for sub in reversed(range(MIX_SUB)):
            views = list(refs)
            for idx in blocked:
                views[idx] = refs[idx].at[pl.ds(sub * CHUNK, CHUNK)]
            h_before = refs[2].at[pl.ds(sub * CHUNK - ROWS, ROWS)] if sub else refs[3]
            chunk((n_chunk // MIX_SUB - 1 - step) * MIX_SUB + sub,
                  step == 0 if sub == MIX_SUB - 1 else None,
                  step == n_chunk // MIX_SUB - 1 if sub == 0 else None, h_before, *views)

    def chunk(c_id, first, last, h_before, *refs):
        (z_ref, dy_ref, h_ref, hhalo_ref, vhb_ref, xcb_ref, rs_ref, lng_ref, lnb_ref, wm_ref, wmt_ref, bias_ref, cw_ref,
         cb_ref, wax_ref, waxt_ref, ba_ref, bx_ref, lam_ref, goa_ref, gob_ref) = refs[:n_in]
        ex_in = refs[n_in:n_in + ex.n]
        dz_ref, vecs_ref, dws_ref, dwax_ref, dbs_ref = refs[n_in + ex.n:n_in + ex.n + n_out]
        ex_out = refs[n_in + ex.n + n_out:n_in + 2 * ex.n + n_out]
        (vnb_s, vh_s, xc_s, mixed_s, pre_s, dmix_s, dvn_s, dho_s, dxc_s, dpre_s, dz_s, acc_s, accdm_s,
         cg_s, ca_s, dxchalo_s) = refs[n_in + 2 * ex.n + n_out:n_in + 2 * ex.n + n_out + n_scratch]
        ex_sems = refs[n_in + 2 * ex.n + n_out + n_scratch:]
        rid = _row_ids(D_BR)
        first_chunk = c_id == 0

        if first is not None:
            @pl.when(first)
            def _():
                ex.start(ex_in, ex_out, ex_sems)
                acc_s[...] = jnp.zeros_like(acc_s)
                accdm_s[...] = jnp.zeros_like(accdm_s)
                cg_s[...] = jnp.zeros_like(cg_s)
                ca_s[...] = jnp.zeros_like(ca_s)
                dxchalo_s[...] = jnp.zeros_like(dxchalo_s)
                dws_ref[...] = jnp.zeros_like(dws_ref)
                dwax_ref[...] = jnp.zeros_like(dwax_ref)

        lng, lnb = lng_ref[...], lnb_ref[...]
        h_halo = jnp.where(first_chunk, 0.0, h_before[...])

        def prev_rows(ref, cols, g, halo):
            before = ref[pl.ds(pl.multiple_of(jnp.maximum(g - 1, 0) * ROWS, ROWS), ROWS), cols]
            return jnp.where(g > 0, before, halo)

        vh_s[...] = vhb_ref[...].astype(F32)
        xc_s[...] = xcb_ref[...].astype(F32)

        for hd in range(N_HEAD):
            cs = slice(hd * HEAD, (hd + 1) * HEAD)
            vnb_s[:, cs] = (vh_s[:, cs] * lng[:, cs] + lnb[:, cs]).astype(BF16)
            mixed_s[:, cs] = _dot(wm_ref[hd], vnb_s[:, cs])
            pre = _dot(xcb_ref[:, cs], wax_ref[hd])
            pre_s[:, cs] = pre[:, :HEAD]
            pre_s[:, D_BR + hd * HEAD:D_BR + (hd + 1) * HEAD] = pre[:, HEAD:]

        goa, gob = goa_ref[...], gob_ref[...]

        def phase3(g, _):
            rows = _rows(g)
            ug, dug = _gelu(z_ref[rows, 0:D_BR], with_grad=True)
            ga = z_ref[rows, 2 * D_BR:3 * D_BR]
            sga = _sig(ga)
            sa = ga * sga
            mixed = mixed_s[rows, :] + bias_ref[rows, :]
            ya0 = ug * mixed
            ya = ya0 * sa
            ra = lax.rsqrt(_mean_last(ya * ya) + EPS)
            dyan = dy_ref[rows, 0:D_BR].astype(F32)
            acc_s[V_GOUT_A] += dyan * ya * ra
            dyg = dyan * goa
            dya = ra * dyg - ya * (ra * ra * ra) * _mean_last(dyg * ya)
            dya0 = dya * sa
            dz_s[rows, 2 * D_BR:3 * D_BR] = dya * ya0 * _silu_grad(sga, sa)
            dmix = dya0 * ug
            dmix_s[rows, :] = dmix
            accdm_s[rows, :] += dmix
            dz_s[rows, 0:D_BR] = dya0 * mixed * dug

            hh = h_ref[rows, :]
            gb = z_ref[rows, 4 * D_BR:5 * D_BR]
            sgb = _sig(gb)
            sb = gb * sgb
            yb = hh * sb
            rb = lax.rsqrt(_mean_last(yb * yb) + EPS)
            dybn = dy_ref[rows, D_BR:2 * D_BR].astype(F32)
            acc_s[V_GOUT_B] += dybn * yb * rb
            dyg = dybn * gob
            dyb = rb * dyg - yb * (rb * rb * rb) * _mean_last(dyg * yb)
            dho_s[rows, :] = dyb * sb
            dz_s[rows, 4 * D_BR:5 * D_BR] = dyb * hh * _silu_grad(sgb, sb)
            return 0

        _loop(N_GROUP, phase3, 0)

        for hd in range(N_HEAD):
            cs = slice(hd * HEAD, (hd + 1) * HEAD)
            dmb = dmix_s[:, cs].astype(BF16)
            dvn_s[:, cs] = _dot(wmt_ref[hd], dmb)
            dws_ref[hd] += _dot_nt(dmb, vnb_s[:, cs])

        def phase5(g, _):
            rows = _rows(g)
            dvn = dvn_s[rows, :]
            vh = vh_s[rows, :]
            acc_s[V_LN_G] += dvn * vh
            acc_s[V_LN_B] += dvn
            dvh = dvn * lng
            rs = rs_ref[rows, 0:1]
            dvg = rs * (dvh - _mean_last(dvh) - vh * _mean_last(dvh * vh))
            dz_s[rows, D_BR:2 * D_BR] = dvg * _gelu(z_ref[rows, D_BR:2 * D_BR], with_grad=True)[1]
            return 0

        _loop(N_GROUP, phase5, 0)

        ba, bx = ba_ref[...], bx_ref[...]
        sp8 = LRU_C * _softplus(-lam_ref[...])
        row0 = _row0_mask(rid)

        def phase6(k, carry):
            cg, ca = carry
            g = N_GROUP - 1 - k
            rows = _rows(g)
            bias0 = _row0_bias(jnp.logical_and(first_chunk, g == 0), row0)
            r, i, a, m2 = _lru_gates(pre_s[rows, 0:D_BR], pre_s[rows, D_BR:2 * D_BR], ba, bx, sp8, bias0)
            a_nx = jnp.where(rid < ROWS - 1, pltpu.roll(a, ROWS - 1, 0), ca)
            aa, bb = a_nx, dho_s[rows, :]
            for d in (1, 2, 4):
                a_sh = jnp.where(rid < ROWS - d, pltpu.roll(aa, ROWS - d, 0), 1.0)
                b_sh = jnp.where(rid < ROWS - d, pltpu.roll(bb, ROWS - d, 0), 0.0)
                bb = aa * b_sh + bb
                aa = aa * a_sh
            gg = bb + aa * cg
            hh = h_ref[rows, :]
            hprev = _shift_down(hh, prev_rows(h_ref, slice(None), g, h_halo), 1, rid)
            xc = xc_s[rows, :]
            gx = gg * xc
            dla = gg * hprev * a - gx * i * (a * a) * lax.rsqrt(m2)
            acc_s[V_LAM] += -(dla * r)
            dpa = -(dla * sp8) * r * (1.0 - r)
            mi = jnp.sqrt(m2) * i
            dpx = gx * mi * (1.0 - i)
            acc_s[V_B_A] += dpa
            acc_s[V_B_X] += dpx
            dpre_s[rows, 0:D_BR] = dpa
            dpre_s[rows, D_BR:2 * D_BR] = dpx
            dxc_s[rows, :] = gg * mi
            return _bcast_row(gg, 0), _bcast_row(a, 0)

        cg, ca = _loop(N_GROUP, phase6, (cg_s[...], ca_s[...]))
        cg_s[...] = cg
        ca_s[...] = ca

        for hd in range(N_HEAD):
            cs = slice(hd * HEAD, (hd + 1) * HEAD)
            dpre = jnp.concatenate([dpre_s[:, cs], dpre_s[:, D_BR + hd * HEAD:D_BR + (hd + 1) * HEAD]], axis=1).astype(BF16)
            dxc_s[:, cs] += _dot(dpre, waxt_ref[hd])
            dwax_ref[hd] += _dot_tn(xcb_ref[:, cs], dpre)

        def phase8(k, nxt):
            g = N_GROUP - 1 - k
            rows = _rows(g)
            dxc = dxc_s[rows, :]
            acc_s[V_CONV_B] += dxc
            xb = z_ref[rows, 3 * D_BR:4 * D_BR]
            dxb = cw_ref[3:4, :] * dxc
            acc_s[V_CONV_W + 3] += dxc * xb
            for j in range(1, CONV_W):
                later = _shift_up(dxc, nxt, j, rid)
                dxb = dxb + cw_ref[3 - j:4 - j, :] * later
                acc_s[V_CONV_W + 3 - j] += later * xb
            dz_s[rows, 3 * D_BR:4 * D_BR] = dxb
            return dxc

        dxchalo_s[...] = _loop(N_GROUP, phase8, dxchalo_s[...])
        dz_ref[...] = dz_s[...].astype(BF16)

        if last is not None:
            @pl.when(last)
            def _():
                for v in range(N_VEC):
                    vecs_ref[v:v + 1, :] = jnp.sum(acc_s[v], axis=0, keepdims=True)
                lam = lam_ref[...]
                vecs_ref[V_LAM:V_LAM + 1, :] = vecs_ref[V_LAM:V_LAM + 1, :] * (-LRU_C * _sig(-lam))
                tril = (lax.broadcasted_iota(jnp.int32, (HEAD, HEAD), 0)
                        >= lax.broadcasted_iota(jnp.int32, (HEAD, HEAD), 1))
                ones = jnp.ones((ROWS, HEAD), BF16)
                for hd in range(N_HEAD):
                    cs = slice(hd * HEAD, (hd + 1) * HEAD)
                    dws_ref[hd] = jnp.where(tril, dws_ref[hd], 0.0)
                    blk = accdm_s[:, cs]
                    hi = blk.astype(BF16)
                    lo = (blk - hi.astype(F32)).astype(BF16)
                    dbs_ref[hd:hd + 1, :] = (_dot_nt(ones, hi) + _dot_nt(ones, lo))[0:1, :]
                ex.wait(ex_in, ex_out, ex_sems)

    vec = pl.BlockSpec((1, D_BR), lambda i: (0, 0))
    n_step = n_chunk // MIX_SUB
    rows_blk = MIX_SUB * CHUNK
    rev = lambda i: (n_step - 1 - i, 0)
    halo = lambda col: (lambda i: (jnp.maximum((n_step - 1 - i) * MIX_SUB * halo_blocks - 1, 0), col))
    full3 = lambda a, b, c: pl.BlockSpec((a, b, c), lambda i: (0, 0, 0))
    big = lambda w: pltpu.VMEM((CHUNK, w), F32)
    res = pl.pallas_call(
        body, name="mix_bwd", grid=(n_step,),
        in_specs=[pl.BlockSpec((rows_blk, D_IN), rev), pl.BlockSpec((rows_blk, 2 * D_BR), rev),
                  pl.BlockSpec((rows_blk, D_BR), rev),
                  pl.BlockSpec((ROWS, D_BR), halo(0)), pl.BlockSpec((rows_blk, D_BR), rev),
                  pl.BlockSpec((rows_blk, D_BR), rev),
                  pl.BlockSpec((rows_blk, HEAD), rev), vec, vec,
                  full3(N_HEAD, HEAD, HEAD), full3(N_HEAD, HEAD, HEAD),
                  pl.BlockSpec((CHUNK, D_BR), lambda i: (0, 0)), pl.BlockSpec((ROWS, D_BR), lambda i: (0, 0)), vec,
                  full3(N_HEAD, HEAD, 2 * HEAD), full3(N_HEAD, 2 * HEAD, HEAD), vec, vec, vec, vec, vec]
        + [ANY_SPEC] * ex.n,
        out_specs=[pl.BlockSpec((rows_blk, D_IN), rev), pl.BlockSpec((N_VEC, D_BR), lambda i: (0, 0)),
                   full3(N_HEAD, HEAD, HEAD), full3(N_HEAD, HEAD, 2 * HEAD),
                   pl.BlockSpec((N_HEAD, HEAD), lambda i: (0, 0))] + [ANY_SPEC] * ex.n,
        out_shape=[SDS((t_len, D_IN), BF16), SDS((N_VEC, D_BR), F32), SDS((N_HEAD, HEAD, HEAD), F32),
                   SDS((N_HEAD, HEAD, 2 * HEAD), F32), SDS((N_HEAD, HEAD), F32)] + ex.out_shape,
        scratch_shapes=[pltpu.VMEM((CHUNK, D_BR), BF16), big(D_BR), big(D_BR), big(D_BR), big(2 * D_BR), big(D_BR),
                        big(D_BR), big(D_BR), big(D_BR), big(2 * D_BR), big(D_IN),
                        pltpu.VMEM((N_VEC, ROWS, D_BR), F32), big(D_BR),
                        pltpu.VMEM((ROWS, D_BR), F32), pltpu.VMEM((ROWS, D_BR), F32), pltpu.VMEM((ROWS, D_BR), F32)]
        + ex.scratch,
        compiler_params=_params(("arbitrary",), 48),
    )(z, dy, h, h, vhb, xcb, rs, ln_g, ln_b, wm, wm_t, bias, cw, cb, wax, wax_t, ba, bx, lam, goa, gob, *ex_arrs)
    return res[:n_out], res[n_out:]


def _in_bwd(dz, w_in_g, x, dh1, pre_g, tm=256):
    t_len = x.shape[0]
    n_tile = t_len // tm

    def body(dz_ref, w_hbm, x_ref, dh1_ref, g_ref, gx_ref, dg_ref, w_s, t_even, t_odd, dg_s, w_sems):
        i = pl.program_id(0)

        @pl.when(i == 0)
        def _():
            loads = [pltpu.make_async_copy(w_hbm.at[s], w_s.at[:, s * W_IN_SHARD:(s + 1) * W_IN_SHARD], w_sems.at[s])
                     for s in range(N_DEV)]
            for cp in loads:
                cp.start()
            dg_s[...] = jnp.zeros_like(dg_s)
            for s, cp in enumerate(loads):
                cp.wait()
                cols = slice(s * W_IN_SHARD, (s + 1) * W_IN_SHARD)
                part = _dot_nt(dz_ref[:, cols], w_s[:, cols])
                t_even[...] = part if s == 0 else t_even[...] + part

        def step(t_new, t_old):
            g = g_ref[...]
            acc = dg_s[...]
            for q in range(tm // TILE_ROWS):
                rows = slice(q * TILE_ROWS, (q + 1) * TILE_ROWS)
                xv = x_ref[rows, :]
                r = lax.rsqrt(_mean_last(xv * xv) + EPS)
                xh = xv * r
                dhn = t_old[rows, :]
                dg = dhn * g
                gx_ref[rows, :] = dh1_ref[rows, :] + r * (dg - xh * _mean_last(dg * xh))
                acc = acc + _fold_rows(dhn * xh)
            dg_s[...] = acc
            t_new[...] = _dot_nt(dz_ref[...], w_s[...])

        @pl.when((i % 2 == 0) & (i > 0))
        def _():
            step(t_even, t_odd)

        @pl.when(i % 2 == 1)
        def _():
            step(t_odd, t_even)

        @pl.when(i == n_tile)
        def _():
            dg_ref[...] = jnp.sum(dg_s[...], axis=0, keepdims=True)

    matmul_tile = lambda i: (jnp.minimum(i, n_tile - 1), 0)
    rows_tile = lambda i: (jnp.maximum(i - 1, 0), 0)
    res = pl.pallas_call(
        body, name="in_bwd", grid=(n_tile + 1,),
        in_specs=[pl.BlockSpec((tm, D_IN), matmul_tile), ANY_SPEC, pl.BlockSpec((tm, D_MODEL), rows_tile),
                  pl.BlockSpec((tm, D_MODEL), rows_tile), pl.BlockSpec((1, D_MODEL), lambda i: (0, 0))],
        out_specs=[pl.BlockSpec((tm, D_MODEL), rows_tile), pl.BlockSpec((1, D_MODEL), lambda i: (0, 0))],
        out_shape=[SDS((t_len, D_MODEL), F32), SDS((1, D_MODEL), F32)],
        scratch_shapes=[pltpu.VMEM((D_MODEL, D_IN), BF16), pltpu.VMEM((tm, D_MODEL), F32), pltpu.VMEM((tm, D_MODEL), F32),
                        pltpu.VMEM((ROWS, D_MODEL), F32), pltpu.SemaphoreType.DMA((N_DEV,))],
        compiler_params=_params(("arbitrary",), 54),
    )(dz, w_in_g, x, dh1, pre_g)
    return res[0], res[1]


def _grad_w(a, b, bn, shard_major, name, tk=1024, ex_arrs=(), ex_scatter=()):
    t_len, m = a.shape
    n = b.shape[1]
    n_j, n_k = n // bn, t_len // tk
    ex = _Exchange(ex_arrs, ex_scatter)

    def body(a_ref, b_ref, *refs):
        ex_in, o_ref, ex_out = refs[:ex.n], refs[ex.n], refs[ex.n + 1:2 * ex.n + 1]
        acc_s, ex_sems = refs[2 * ex.n + 1], refs[2 * ex.n + 2:]
        j, k = pl.program_id(0), pl.program_id(1)
        if ex.n:
            @pl.when(jnp.logical_and(j == 0, k == 0))
            def _():
                ex.start(ex_in, ex_out, ex_sems)

        @pl.when(k == 0)
        def _():
            acc_s[...] = jnp.zeros_like(acc_s)

        acc_s[...] += _dot_tn(a_ref[...], b_ref[...])

        @pl.when(k == n_k - 1)
        def _():
            o_ref[...] = acc_s[...].astype(BF16)

        if ex.n:
            @pl.when(jnp.logical_and(j == n_j - 1, k == n_k - 1))
            def _():
                ex.wait(ex_in, ex_out, ex_sems)

    if shard_major:
        out_spec, out_shape = pl.BlockSpec((None, m, bn), lambda j, k: (j, 0, 0)), SDS((n_j, m, bn), BF16)
    else:
        out_spec, out_shape = pl.BlockSpec((m, bn), lambda j, k: (0, j)), SDS((m, n), BF16)
    res = pl.pallas_call(
        body, name=name, grid=(n_j, n_k),
        in_specs=[pl.BlockSpec((tk, m), lambda j, k: (k, 0)), pl.BlockSpec((tk, bn), lambda j, k: (k, j))]
        + [ANY_SPEC] * ex.n,
        out_specs=[out_spec] + [ANY_SPEC] * ex.n, out_shape=[out_shape] + ex.out_shape,
        scratch_shapes=[pltpu.VMEM((m, bn), F32)] + (ex.scratch if ex.n else []),
        compiler_params=_params(("arbitrary", "arbitrary"), 40),
    )(a, b, *ex_arrs)
    return res[0], res[1:]


RS_CHIPS = (6, 2, 4, 0)
RS_SLOTS = (0, 1, 2, 4, 6)


def _grad_w_in_pairs(hn, dz, ex_arrs, ex_scatter, tk=1024):
    t_len = hn.shape[0]
    n_k = t_len // tk
    n_ph = len(RS_CHIPS)
    ex = _Exchange(ex_arrs, ex_scatter)
    me_out = 4 * lax.axis_index("x") + 2 * lax.axis_index("y") + lax.axis_index("c")
    order = jnp.stack([(me_out ^ chip) // 2 for chip in RS_CHIPS]).astype(jnp.int32)
    slots = jnp.stack([me_out ^ k for k in RS_SLOTS]).astype(jnp.int32)
    shard = W_IN_SHARD

    def body(order_ref, a_ref, b_ref, *refs):
        ex_in, parts_hbm, ex_out = refs[:ex.n], refs[ex.n], refs[ex.n + 1:2 * ex.n + 1]
        (acc_s, tb_s, stage_s, rx_s, d2d_send, d2d_recv, ici_send, ici_recv, sib_sems,
         loc_sem) = refs[2 * ex.n + 1:2 * ex.n + 11]
        ex_sems = refs[2 * ex.n + 11:]
        j, k = pl.program_id(0), pl.program_id(1)
        x, y, c, me = _mesh_place()
        sib = _peer(x, y, c, SIBLING)[0]

        def to_sibling(p):
            return _remote(stage_s.at[0], rx_s.at[p % 2], d2d_send.at[p], d2d_recv.at[p], sib)

        def over_ici(p):
            dev = _peer(x, y, c, RS_CHIPS[p])[0]
            return _remote(stage_s.at[1], parts_hbm.at[me], ici_send.at[p], ici_recv.at[p], dev)

        def own_chip():
            return (_remote(stage_s.at[0], parts_hbm.at[me], sib_sems.at[0], sib_sems.at[1], sib),
                    pltpu.make_async_copy(stage_s.at[1], parts_hbm.at[me], loc_sem.at[0]))

        @pl.when(jnp.logical_and(j == 0, k == 0))
        def _():
            ex.start(ex_in, ex_out, ex_sems)

        for p in range(n_ph - 1):
            for core in (0, 1):
                @pl.when(jnp.logical_and(jnp.logical_and(j == p + 1, k == 0), c == core))
                def _(p=p, core=core):
                    to_sibling(p).wait_recv()
                    if p >= 1:
                        over_ici(p - 1).wait_send()
                    mine = acc_s[:, core * shard:(core + 1) * shard]
                    stage_s[1] = (mine + rx_s[p % 2].astype(F32)).astype(BF16)
                    over_ici(p).start()

        @pl.when(k == 0)
        def _():
            acc_s[...] = jnp.zeros_like(acc_s)

        a = a_ref[...]
        acc_s[:, 0:W_BODY] += _dot_tn(a, b_ref[:, 0:W_BODY])
        acc_s[:, shard:shard + W_BODY] += _dot_tn(a, b_ref[:, shard:shard + W_BODY])
        tb_s[:, 0:W_TAIL] = b_ref[:, W_BODY:shard]
        tb_s[:, W_TAIL:2 * W_TAIL] = b_ref[:, shard + W_BODY:2 * shard]
        tails = _dot_tn(a, tb_s[...])
        acc_s[:, W_BODY:shard] += tails[:, 0:W_TAIL]
        acc_s[:, shard + W_BODY:2 * shard] += tails[:, W_TAIL:2 * W_TAIL]

        for p in range(n_ph):
            for core in (0, 1):
                @pl.when(jnp.logical_and(jnp.logical_and(j == p, k == n_k - 1), c == core))
                def _(p=p, core=core):
                    same = acc_s[:, core * shard:(core + 1) * shard]
                    other = acc_s[:, (1 - core) * shard:(2 - core) * shard]
                    if p >= 1:
                        to_sibling(p - 1).wait_send()
                    stage_s[0] = other.astype(BF16)
                    if p < n_ph - 1:
                        to_sibling(p).start()
                    else:
                        over_ici(n_ph - 2).wait_send()
                        stage_s[1] = same.astype(BF16)
                        for cp in own_chip():
                            cp.start()

        @pl.when(jnp.logical_and(j == n_ph - 1, k == n_k - 1))
        def _():
            to_sib, local = own_chip()
            to_sib.wait_send()
            local.wait()
            _remote(stage_s.at[0], parts_hbm.at[_peer(x, y, c, SIBLING)[1]], sib_sems.at[0], sib_sems.at[1], sib).wait_recv()
            for p in range(n_ph - 1):
                dev, lin = _peer(x, y, c, RS_CHIPS[p])
                _remote(stage_s.at[0], parts_hbm.at[lin], ici_send.at[p], ici_recv.at[p], dev).wait_recv()
            ex.wait(ex_in, ex_out, ex_sems)

    dma = lambda n: pltpu.SemaphoreType.DMA((n,))
    grid_spec = pltpu.PrefetchScalarGridSpec(
        num_scalar_prefetch=1, grid=(n_ph, n_k),
        in_specs=[pl.BlockSpec((tk, D_MODEL), lambda j, k, order: (k, 0)),
                  pl.BlockSpec((tk, 2 * shard), lambda j, k, order: (k, order[j]))] + [ANY_SPEC] * ex.n,
        out_specs=[ANY_SPEC] * (1 + ex.n),
        scratch_shapes=[pltpu.VMEM((D_MODEL, 2 * shard), F32), pltpu.VMEM((tk, 2 * W_TAIL), BF16),
                        pltpu.VMEM((2, D_MODEL, shard), BF16),
                        pltpu.VMEM((2, D_MODEL, shard), BF16), dma(n_ph - 1), dma(n_ph - 1), dma(n_ph - 1),
                        dma(n_ph - 1), dma(2), dma(1)] + ex.scratch)
    res = pl.pallas_call(
        body, name="grad_w_in", grid_spec=grid_spec,
        out_shape=[SDS((N_DEV, D_MODEL, shard), BF16)] + ex.out_shape,
        compiler_params=_params(("arbitrary", "arbitrary"), 54),
    )(order, hn, dz, *ex_arrs)
    return res[0], slots, res[1:]


def _sum_parts(parts, name):
    def body(p_ref, o_ref):
        g = p_ref[0].astype(F32)
        for s in range(1, parts.shape[0]):
            g = g + p_ref[s].astype(F32)
        o_ref[...] = g

    return pl.pallas_call(body, name=name, out_shape=SDS(parts.shape[1:], F32))(parts)


def _adamw_math(g, w_ref, m_ref, v_ref, g_ref, d_ref, nm_ref, nv_ref):
    c1 = 1.0 - ADAM_B1 ** ADAM_STEP
    c2 = 1.0 - ADAM_B2 ** ADAM_STEP
    g_ref[...] = g
    nm = ADAM_B1 * m_ref[...] + (1.0 - ADAM_B1) * g
    nv = ADAM_B2 * v_ref[...] + (1.0 - ADAM_B2) * (g * g)
    nm_ref[...] = nm
    nv_ref[...] = nv
    d_ref[...] = -ADAM_LR * ((nm / c1) / (jnp.sqrt(nv / c2) + ADAM_EPS) + ADAM_WD * w_ref[...])


def _adamw(parts, w, m, v, name, tr):
    rows, cols = w.shape
    n_parts = parts.shape[0]

    def body(p_ref, *refs):
        g = p_ref[0].astype(F32)
        for s in range(1, n_parts):
            g = g + p_ref[s].astype(F32)
        _adamw_math(g, *refs)

    tile = pl.BlockSpec((tr, cols), lambda i: (i, 0))
    return pl.pallas_call(
        body, name=name, grid=(rows // tr,),
        in_specs=[pl.BlockSpec((n_parts, tr, cols), lambda i: (0, i, 0)), tile, tile, tile],
        out_specs=[tile] * 4, out_shape=[SDS((rows, cols), F32)] * 4,
        compiler_params=_params(("arbitrary",), 40),
    )(parts, w, m, v)


def _adamw_unpacked(grads, triples, name):
    n = len(triples)
    n_rows = [t[0].shape[0] for t in triples]

    def body(g_ref, *refs):
        ins, outs = refs[:3 * n], refs[3 * n:]
        row = 0
        for i in range(n):
            _adamw_math(g_ref[row:row + n_rows[i], :], *ins[3 * i:3 * i + 3], *outs[4 * i:4 * i + 4])
            row += n_rows[i]
        outs[4 * n][...] = g_ref[row:row + ROWS, :]

    out_shape = [SDS((r, LANES), F32) for r in n_rows for _ in range(4)] + [SDS((ROWS, LANES), F32)]
    return pl.pallas_call(
        body, name=name, out_shape=out_shape,
        compiler_params=pltpu.CompilerParams(vmem_limit_bytes=40 * MIB),
    )(grads, *[a for t in triples for a in t])


def _adamw_slots(parts, slots, w, m, v, name, tr):
    rows, cols = w.shape
    n_slots = slots.shape[0]

    def body(slots_ref, *refs):
        g = refs[0][...].astype(F32)
        for s in range(1, n_slots):
            g = g + refs[s][...].astype(F32)
        _adamw_math(g, *refs[n_slots:])

    tile = pl.BlockSpec((tr, cols), lambda i, slots: (i, 0))
    part = lambda s: pl.BlockSpec((None, tr, cols), lambda i, slots: (slots[s], i, 0))
    grid_spec = pltpu.PrefetchScalarGridSpec(
        num_scalar_prefetch=1, grid=(rows // tr,),
        in_specs=[part(s) for s in range(n_slots)] + [tile, tile, tile], out_specs=[tile] * 4)
    return pl.pallas_call(
        body, name=name, grid_spec=grid_spec, out_shape=[SDS((rows, cols), F32)] * 4,
        compiler_params=_params(("arbitrary",), 40),
    )(slots, *([parts] * n_slots), w, m, v)


PACKED = ("gmlp_ln_g", "gmlp_ln_b", "gmlp_ws", "gmlp_bs", "conv_b", "w_a", "b_a", "w_x", "b_x", "lam", "gmlp_out_g",
          "lru_out_g", "post_g")
WEIGHTS = ("pre_g", "w_in", "gmlp_ln_g", "gmlp_ln_b", "gmlp_ws", "gmlp_bs", "conv_w", "conv_b", "w_a", "b_a", "w_x",
           "b_x", "lam", "gmlp_out_g", "lru_out_g", "w_out", "post_g", "w_pe", "w_pg")
LANES = 128


PACK_ROWS = 3200


def _pack(parts):
    rows = [p.reshape(-1, LANES) for p in parts]
    used = sum(r.shape[0] for r in rows)
    return jnp.concatenate(rows + [jnp.zeros((PACK_ROWS - used, LANES), F32)], axis=0)


def _pad_rows(a, rows):
    return jnp.concatenate([a, jnp.zeros((rows - a.shape[0],) + a.shape[1:], a.dtype)], axis=0)


def kernel(x, p, pre_g, w_in, gmlp_ln_g, gmlp_ln_b, gmlp_ws, gmlp_bs, conv_w, conv_b, w_a, b_a, w_x, b_x, lam, gmlp_out_g, lru_out_g, w_out, post_g, w_pe, w_pg, loss_target, m_pre_g, m_w_in, m_gmlp_ln_g, m_gmlp_ln_b, m_gmlp_ws, m_gmlp_bs, m_conv_w, m_conv_b, m_w_a, m_b_a, m_w_x, m_b_x, m_lam, m_gmlp_out_g, m_lru_out_g, m_w_out, m_post_g, m_w_pe, m_w_pg, v_pre_g, v_w_in, v_gmlp_ln_g, v_gmlp_ln_b, v_gmlp_ws, v_gmlp_bs, v_conv_w, v_conv_b, v_w_a, v_b_a, v_w_x, v_b_x, v_lam, v_gmlp_out_g, v_lru_out_g, v_w_out, v_post_g, v_w_pe, v_w_pg):
    args = dict(locals())
    weights = {n: args[n] for n in WEIGHTS}
    m_in = {n: args["m_" + n] for n in WEIGHTS}
    v_in = {n: args["v_" + n] for n in WEIGHTS}
    sm = {n: weights[n][0] for n in PACKED}
    shard_rows = D_MODEL // N_DEV
    xs, ps, tgt = x[0], p[0, 0], loss_target[0]

    vec = lambda a: a.reshape(1, -1)
    tril = jnp.tril(jnp.ones((CHUNK, CHUNK), dtype=bool))
    wm32 = jnp.where(tril[None], sm["gmlp_ws"], 0.0)
    wm, wm_t = wm32.astype(BF16), jnp.swapaxes(wm32, 1, 2).astype(BF16)
    bias = jnp.repeat(sm["gmlp_bs"].T, HEAD, axis=1)
    wax32 = jnp.concatenate([sm["w_a"], sm["w_x"]], axis=2)
    wax, wax_t = wax32.astype(BF16), jnp.swapaxes(wax32, 1, 2).astype(BF16)
    ln_g, ln_b = vec(sm["gmlp_ln_g"]), vec(sm["gmlp_ln_b"])
    post_g_v = vec(sm["post_g"])

    hn = _pre_norm(xs, pre_g)
    cw_shard = _pad_rows(conv_w.reshape(CONV_W, HEAD), ROWS)
    z, w_in_g, (cw_g,) = _in_proj(hn, w_in[0].astype(BF16), [cw_shard])
    cw_full = jnp.transpose(cw_g[:, :CONV_W, :], (1, 0, 2)).reshape(CONV_W, D_BR)
    mixer_consts = dict(cw=_pad_rows(cw_full, ROWS), cb=vec(sm["conv_b"]), ba=vec(sm["b_a"]), bx=vec(sm["b_x"]),
                        lam=vec(sm["lam"]), goa=vec(sm["gmlp_out_g"]), gob=vec(sm["lru_out_g"]))
    (y, h, vhb, xcb, v_rs), (w_out_g, w_pe_g, w_pg_g) = _mix_fwd(
        z, ln_g, ln_b, wm, bias, wax=wax, **mixer_consts,
        ex_arrs=[w_out[0].astype(BF16), w_pe[0].astype(BF16), w_pg[0].astype(BF16)], ex_scatter=[False, False, False])
    w_out_f, w_pg_f = w_out_g.reshape(D_MODEL, D_MODEL), w_pg_g.reshape(D_MODEL, D_MODEL)
    h1, ob = _out_proj(y, xs, w_out_f, post_g_v)
    dh2, dgl, h1b, loss_part, d_w_pe = _ple_loss(h1, ps, tgt, w_pg_f, w_pe_g)

    dh1, do, dy, d_post_g = _tail_bwd(dh2, dgl, ob, w_pg_f, w_out_f, post_g_v)
    d_w_out, _ = _grad_w(y, do, 1024, False, "grad_w_out")
    d_w_pg, _ = _grad_w(h1b, dgl, 1024, False, "grad_w_pg")
    (dz, vecs, d_ws, d_wax, d_bs), (parts_out, parts_pg, parts_pe) = _mix_bwd(
        z, dy, h, vhb, xcb, v_rs, ln_g, ln_b, wm, wm_t, bias, wax=wax, wax_t=wax_t, **mixer_consts,
        ex_arrs=[d_w_out.reshape(N_DEV, shard_rows, D_MODEL), d_w_pg.reshape(N_DEV, shard_rows, D_MODEL), d_w_pe],
        ex_scatter=[True, True, True])

    small = {"gmlp_ln_g": vecs[V_LN_G], "gmlp_ln_b": vecs[V_LN_B], "gmlp_ws": d_ws, "gmlp_bs": d_bs,
             "conv_b": vecs[V_CONV_B], "w_a": d_wax[:, :, :HEAD], "b_a": vecs[V_B_A], "w_x": d_wax[:, :, HEAD:],
             "b_x": vecs[V_B_X], "lam": vecs[V_LAM], "gmlp_out_g": vecs[V_GOUT_A], "lru_out_g": vecs[V_GOUT_B],
             "post_g": d_post_g}
    small_part = _pack([small[n] for n in PACKED] + [loss_part]).reshape(N_DEV, PACK_ROWS // N_DEV, LANES)
    d_cw_blocks = jnp.transpose(vecs[V_CONV_W:V_CONV_W + CONV_W].reshape(CONV_W, N_DEV, HEAD), (1, 0, 2))
    d_cw_blocks = jnp.concatenate([d_cw_blocks, jnp.zeros((N_DEV, ROWS - CONV_W, HEAD), F32)], axis=1)
    parts_in, slots_in, (small_blocks, parts_cw) = _grad_w_in_pairs(
        hn, dz, ex_arrs=[small_part, d_cw_blocks], ex_scatter=[True, True])
    small_sum = _sum_parts(small_blocks, "sum_small")
    grad_x, d_pre_g = _in_bwd(dz, w_in_g, xs, dh1, pre_g)
    pre_rows = D_MODEL // LANES
    small_all, parts_pre = _exchange([small_sum, d_pre_g.reshape(pre_rows, LANES)], False, "gather_small_grads")

    pad_cw = lambda a: _pad_rows(a.reshape(CONV_W, HEAD), ROWS)
    flat = lambda a: a.reshape(pre_rows, LANES)
    outs = {
        "w_in": _adamw_slots(parts_in, slots_in, w_in[0], m_w_in[0], v_w_in[0], "adamw_w_in", 256),
        "w_out": _adamw(parts_out, w_out[0], m_w_out[0], v_w_out[0], "adamw_w_out", 128),
        "w_pe": _adamw(parts_pe, w_pe[0], m_w_pe[0], v_w_pe[0], "adamw_w_pe", 256),
        "w_pg": _adamw(parts_pg, w_pg[0], m_w_pg[0], v_w_pg[0], "adamw_w_pg", 128),
        "conv_w": [a[:CONV_W] for a in
                   _adamw(parts_cw, pad_cw(conv_w), pad_cw(m_conv_w), pad_cw(v_conv_w), "adamw_conv_w", ROWS)],
        "pre_g": _adamw(parts_pre, flat(pre_g), flat(m_pre_g), flat(v_pre_g), "adamw_pre_g", pre_rows),
    }
    as_rows = lambda a: a.reshape(-1, LANES)
    small_res = _adamw_unpacked(small_all.reshape(PACK_ROWS, LANES),
                                [(as_rows(weights[n]), as_rows(m_in[n]), as_rows(v_in[n])) for n in PACKED], "adamw_small")
    for i, n in enumerate(PACKED):
        outs[n] = small_res[4 * i:4 * i + 4]
    loss = small_res[-1][0, 0]

    result = [loss, grad_x[None]]
    for q in range(4):
        result += [outs[n][q].reshape(weights[n].shape) for n in WEIGHTS]
    return tuple(result)
```

```python
import jax
import jax.numpy as jnp
from jax import lax
from jax.experimental import pallas as pl
from jax.experimental.pallas import tpu as pltpu

F32 = jnp.float32
BF16 = jnp.bfloat16
SDS = jax.ShapeDtypeStruct

D_MODEL = 2048
D_BR = 1024
D_IN = 5 * D_BR
D_PLE = 256
N_HEAD = 8
HEAD = 128
CHUNK = 128
ROWS = 8
N_GROUP = CHUNK // ROWS
MIX_SUB = 2
N_DEV = 8
W_IN_SHARD = D_IN // N_DEV
EPS = 1e-6
LRU_C = 8.0
CONV_W = 4
MIB = 1 << 20

ADAM_LR, ADAM_B1, ADAM_B2, ADAM_EPS, ADAM_WD, ADAM_STEP = 0.001, 0.9, 0.999, 1e-08, 0.01, 10

_GELU_C = 0.7978845608028654
_GELU_A = 0.044715

V_LN_G, V_LN_B, V_CONV_B, V_B_A, V_B_X, V_LAM, V_GOUT_A, V_GOUT_B, V_CONV_W = 0, 1, 2, 3, 4, 5, 6, 7, 8
N_VEC = 16


def _params(sem, vmem_mib):
    return pltpu.CompilerParams(dimension_semantics=sem, vmem_limit_bytes=int(vmem_mib * MIB))


def _sig(x):
    return 0.5 * jnp.tanh(0.5 * x) + 0.5


def _gelu(x, with_grad=False):
    sq = x * x
    t = jnp.tanh(x * (_GELU_C + (_GELU_C * _GELU_A) * sq))
    half, one_t = 0.5 * x, 1.0 + t
    if not with_grad:
        return half * one_t
    grad = 0.5 * one_t + half * ((1.0 - t) * one_t) * (_GELU_C + (3.0 * _GELU_C * _GELU_A) * sq)
    return half * one_t, grad


def _silu_grad(s, xs):
    return s + xs * (1.0 - s)


def _neg_expm1(y, exp_y):
    series = -y * (1.0 + y * (0.5 + y * (1.0 / 6.0)))
    return jnp.where(y > -0.01, series, 1.0 - exp_y)


def _softplus(x):
    return jnp.maximum(x, 0.0) + jnp.log(1.0 + jnp.exp(-jnp.abs(x)))


def _row_ids(width):
    return lax.broadcasted_iota(jnp.int32, (ROWS, width), 0)


def _shift_down(cur, prev, k, rid):
    return jnp.where(rid >= k, pltpu.roll(cur, k, 0), pltpu.roll(prev, k, 0))


def _shift_up(cur, nxt, k, rid):
    return jnp.where(rid < ROWS - k, pltpu.roll(cur, ROWS - k, 0), pltpu.roll(nxt, ROWS - k, 0))


def _mean_last(x):
    return jnp.mean(x, axis=-1, keepdims=True)


def _rows(g):
    return pl.ds(pl.multiple_of(g * ROWS, ROWS), ROWS)


TILE_ROWS = 16


def _tile_rows(q):
    return pl.ds(pl.multiple_of(q * TILE_ROWS, TILE_ROWS), TILE_ROWS)


UNROLL = 4
TILE_UNROLL = 8


def _loop(n, body, init, unroll=UNROLL):
    def wide(i, carry):
        for u in range(unroll):
            carry = body(i * unroll + u, carry)
        return carry

    return lax.fori_loop(0, n // unroll, wide, init)


def _fold_rows(x):
    return x[0:ROWS, :] + x[ROWS:TILE_ROWS, :]


def _bcast_row(x, r):
    return jnp.broadcast_to(x[r:r + 1, :], x.shape)


def _dot(a, b):
    return jnp.dot(a, b, preferred_element_type=F32)


def _dot_nt(a, b):
    return lax.dot_general(a, b, (((1,), (1,)), ((), ())), preferred_element_type=F32)


def _dot_tn(a, b):
    return lax.dot_general(a, b, (((0,), (0,)), ((), ())), preferred_element_type=F32)


def _mesh_place():
    x, y, c = lax.axis_index("x"), lax.axis_index("y"), lax.axis_index("c")
    return x, y, c, 4 * x + 2 * y + c


def _peer(x, y, c, k):
    px = 1 - x if k & 4 else x
    py = 1 - y if k & 2 else y
    pc = 1 - c if k & 1 else c
    return (px, py, pc), 4 * px + 2 * py + pc


def _remote(src, dst, send_sem, recv_sem, dev):
    return pltpu.make_async_remote_copy(src_ref=src, dst_ref=dst, send_sem=send_sem, recv_sem=recv_sem, device_id=dev,
                                        device_id_type=pl.DeviceIdType.MESH)


ANY_SPEC = pl.BlockSpec(memory_space=pl.ANY)


class _Exchange:
    def __init__(self, arrs, scatter):
        self.n = len(arrs)
        self.scatter = tuple(scatter)
        self.out_shape = [SDS(a.shape if s else (N_DEV,) + a.shape, a.dtype) for a, s in zip(arrs, scatter)]
        self.scratch = [pltpu.SemaphoreType.DMA((self.n * N_DEV,)), pltpu.SemaphoreType.DMA((self.n * N_DEV,)),
                        pltpu.SemaphoreType.DMA((self.n,))]

    def _copies(self, ins, outs, sems):
        send_sems, recv_sems, local_sems = sems
        x, y, c, me = _mesh_place()
        local, sends, recvs = [], [], []
        for a in range(self.n):
            src = ins[a].at[me] if self.scatter[a] else ins[a]
            local.append(pltpu.make_async_copy(src, outs[a].at[me], local_sems.at[a]))
        for k in range(1, N_DEV):
            dev, lin = _peer(x, y, c, k)
            for a in range(self.n):
                src = ins[a].at[lin] if self.scatter[a] else ins[a]
                pair = (send_sems.at[a * N_DEV + k], recv_sems.at[a * N_DEV + k], dev)
                sends.append(_remote(src, outs[a].at[me], *pair))
                recvs.append(_remote(src, outs[a].at[lin], *pair))
        return local, sends, recvs

    def start(self, ins, outs, sems):
        local, sends, _ = self._copies(ins, outs, sems)
        for cp in local + sends:
            cp.start()

    def wait(self, ins, outs, sems):
        local, sends, recvs = self._copies(ins, outs, sems)
        for cp in recvs:
            cp.wait_recv()
        for cp in sends:
            cp.wait_send()
        for cp in local:
            cp.wait()


def _exchange(arrs, scatter, name):
    ex = _Exchange(arrs, [scatter] * len(arrs))
    n = ex.n

    def body(*refs):
        ins, outs, sems = refs[:n], refs[n:2 * n], refs[2 * n:]
        ex.start(ins, outs, sems)
        ex.wait(ins, outs, sems)

    return pl.pallas_call(
        body, name=name, out_shape=ex.out_shape, in_specs=[ANY_SPEC] * n, out_specs=[ANY_SPEC] * n,
        scratch_shapes=ex.scratch,
    )(*arrs)


def _pre_norm(x, pre_g, tm=512):
    t_len = x.shape[0]

    def body(x_ref, g_ref, hn_ref):
        g = g_ref[...]

        def rows_body(q, _):
            rows = _tile_rows(q)
            xv = x_ref[rows, :]
            hn_ref[rows, :] = (xv * lax.rsqrt(_mean_last(xv * xv) + EPS) * g).astype(BF16)
            return 0

        _loop(tm // TILE_ROWS, rows_body, 0, unroll=TILE_UNROLL)

    tile = pl.BlockSpec((tm, D_MODEL), lambda i: (i, 0))
    return pl.pallas_call(
        body, name="pre_norm", grid=(t_len // tm,),
        in_specs=[tile, pl.BlockSpec((1, D_MODEL), lambda i: (0, 0))], out_specs=tile,
        out_shape=SDS((t_len, D_MODEL), BF16),
        compiler_params=_params(("arbitrary",), 24),
    )(x, pre_g)


CHIP_ORDER = (0, 2, 4, 6)
W_BODY, W_TAIL = 512, 128
SIBLING = 1
ICI_MASKS = (2, 4, 6)
DIRECT_MASKS = (SIBLING,) + ICI_MASKS
Y_NEIGHBOUR, X_NEIGHBOUR, DIAGONAL = 2, 4, 6
W_DIRECT = (SIBLING, Y_NEIGHBOUR, X_NEIGHBOUR)


def _in_proj(hn, w_shard, others, tm=1024):
    t_len = hn.shape[0]
    n_i = t_len // tm
    n_o = len(others)
    me_out = 4 * lax.axis_index("x") + 2 * lax.axis_index("y") + lax.axis_index("c")
    order = jnp.stack([(me_out ^ chip) // 2 for chip in CHIP_ORDER]).astype(jnp.int32)

    def body(order_ref, hn_ref, w_hbm, *refs):
        o_in = refs[:n_o]
        z_ref, wg_hbm = refs[n_o], refs[n_o + 1]
        o_out = refs[n_o + 2:2 * n_o + 2]
        (wbuf, tail_s, send_w, recv_w, fsend_w, frecv_w, send_o, recv_o, fsend_o, frecv_o, wb_sems, loc_sems, rsend,
         rrecv) = refs[2 * n_o + 2:]
        j, i = pl.program_id(0), pl.program_id(1)
        x, y, c, me = _mesh_place()
        sib = _peer(x, y, c, SIBLING)[0]

        def relay(core):
            src, dst = (Y_NEIGHBOUR, X_NEIGHBOUR) if core == 0 else (X_NEIGHBOUR, Y_NEIGHBOUR)
            held, diag = _peer(x, y, c, src)[1], _peer(x, y, c, DIAGONAL)[1]
            pair = (rsend.at[0], rrecv.at[0], _peer(x, y, c, dst)[0])
            return _remote(wbuf.at[held], wbuf.at[held], *pair), _remote(wbuf.at[diag], wbuf.at[diag], *pair)

        def direct(k, a=None):
            dev, lin = _peer(x, y, c, k)
            if a is None:
                return (_remote(w_hbm, wbuf.at[me], send_w.at[k], recv_w.at[k], dev),
                        _remote(w_hbm, wbuf.at[lin], send_w.at[k], recv_w.at[k], dev))
            pair = (send_o.at[a * N_DEV + k], recv_o.at[a * N_DEV + k], dev)
            return _remote(o_in[a], o_out[a].at[me], *pair), _remote(o_in[a], o_out[a].at[lin], *pair)

        def passed(k, a=None):
            mine, theirs = _peer(x, y, c, k)[1], _peer(x, y, c, k ^ SIBLING)[1]
            if a is None:
                pair = (fsend_w.at[k], frecv_w.at[k], sib)
                return _remote(wbuf.at[mine], wbuf.at[mine], *pair), _remote(wbuf.at[theirs], wbuf.at[theirs], *pair)
            pair = (fsend_o.at[a * N_DEV + k], frecv_o.at[a * N_DEV + k], sib)
            return (_remote(o_out[a].at[mine], o_out[a].at[mine], *pair),
                    _remote(o_out[a].at[theirs], o_out[a].at[theirs], *pair))

        def own_copies():
            return [pltpu.make_async_copy(o_in[a], o_out[a].at[me], loc_sems.at[1 + a]) for a in range(n_o)]

        @pl.when(jnp.logical_and(j == 0, i == 0))
        def _():
            own = pltpu.make_async_copy(w_hbm, wbuf.at[me], loc_sems.at[0])
            own.start()
            for cp in own_copies():
                cp.start()
            for k in W_DIRECT:
                direct(k)[0].start()
            for k in DIRECT_MASKS:
                for a in range(n_o):
                    direct(k, a)[0].start()
            own.wait()

        low = 2 * order_ref[j]

        for jp, chip in enumerate(CHIP_ORDER):
            @pl.when(jnp.logical_and(j == jp, i == 0))
            def _(jp=jp, chip=chip):
                if chip == 0:
                    direct(SIBLING)[1].wait_recv()
                elif chip == Y_NEIGHBOUR:
                    for mask in (Y_NEIGHBOUR, X_NEIGHBOUR):
                        direct(mask)[1].wait_recv()
                        passed(mask)[0].start()
                    for core in (0, 1):
                        @pl.when(c == core)
                        def _(core=core):
                            relay(core)[0].start()
                    passed(Y_NEIGHBOUR)[1].wait_recv()
                elif chip == X_NEIGHBOUR:
                    passed(X_NEIGHBOUR)[1].wait_recv()
                    for core in (0, 1):
                        @pl.when(c == core)
                        def _(core=core):
                            relay(core)[1].wait_recv()
                    passed(DIAGONAL)[0].start()
                    for k in ICI_MASKS:
                        for a in range(n_o):
                            direct(k, a)[1].wait_recv()
                            passed(k, a)[0].start()
                else:
                    passed(DIAGONAL)[1].wait_recv()
                for half in (0, 1):
                    pltpu.make_async_copy(wbuf.at[low + half], wg_hbm.at[low + half], wb_sems.at[2 * jp + half]).start()
                tail_s[:, 0:W_TAIL] = wbuf[low, :, W_BODY:W_IN_SHARD]
                tail_s[:, W_TAIL:2 * W_TAIL] = wbuf[low + 1, :, W_BODY:W_IN_SHARD]

        hn = hn_ref[...]
        z_ref[:, 0:W_BODY] = _dot(hn, wbuf[low, :, 0:W_BODY])
        z_ref[:, W_IN_SHARD:W_IN_SHARD + W_BODY] = _dot(hn, wbuf[low + 1, :, 0:W_BODY])
        tails = _dot(hn, tail_s[...])
        z_ref[:, W_BODY:W_IN_SHARD] = tails[:, 0:W_TAIL]
        z_ref[:, W_IN_SHARD + W_BODY:2 * W_IN_SHARD] = tails[:, W_TAIL:2 * W_TAIL]

        @pl.when(jnp.logical_and(j == len(CHIP_ORDER) - 1, i == n_i - 1))
        def _():
            for a in range(n_o):
                direct(SIBLING, a)[1].wait_recv()
            for k in ICI_MASKS:
                for a in range(n_o):
                    passed(k, a)[1].wait_recv()
            for k in W_DIRECT:
                direct(k)[0].wait_send()
            for core in (0, 1):
                @pl.when(c == core)
                def _(core=core):
                    relay(core)[0].wait_send()
            for k in DIRECT_MASKS:
                for a in range(n_o):
                    direct(k, a)[0].wait_send()
            for k in ICI_MASKS:
                passed(k)[0].wait_send()
                for a in range(n_o):
                    passed(k, a)[0].wait_send()
            for cp in own_copies():
                cp.wait()
            for jj in range(N_DEV):
                pltpu.make_async_copy(wbuf.at[0], wg_hbm.at[0], wb_sems.at[jj]).wait()

    dma = lambda n: pltpu.SemaphoreType.DMA((n,))
    grid_spec = pltpu.PrefetchScalarGridSpec(
        num_scalar_prefetch=1, grid=(len(CHIP_ORDER), n_i),
        in_specs=[pl.BlockSpec((tm, D_MODEL), lambda j, i, order: (i, 0)), ANY_SPEC] + [ANY_SPEC] * n_o,
        out_specs=[pl.BlockSpec((tm, 2 * W_IN_SHARD), lambda j, i, order: (i, order[j])), ANY_SPEC] + [ANY_SPEC] * n_o,
        scratch_shapes=[pltpu.VMEM((N_DEV, D_MODEL, W_IN_SHARD), BF16), pltpu.VMEM((D_MODEL, 2 * W_TAIL), BF16),
                        dma(N_DEV), dma(N_DEV), dma(N_DEV), dma(N_DEV),
                        dma(n_o * N_DEV), dma(n_o * N_DEV), dma(n_o * N_DEV), dma(n_o * N_DEV), dma(N_DEV), dma(1 + n_o),
                        dma(1), dma(1)])
    res = pl.pallas_call(
        body, name="in_proj", grid_spec=grid_spec,
        out_shape=[SDS((t_len, D_IN), F32), SDS((N_DEV, D_MODEL, W_IN_SHARD), BF16)]
        + [SDS((N_DEV,) + o.shape, o.dtype) for o in others],
        compiler_params=_params(("arbitrary", "arbitrary"), 54),
    )(order, hn, w_shard, *others)
    return res[0], res[1], res[2:]


def _conv_rows(cur, prev, cw_ref, cb, rid):
    acc = cw_ref[3:4, :] * cur + cb
    for k in range(1, CONV_W):
        acc = acc + cw_ref[3 - k:4 - k, :] * _shift_down(cur, prev, k, rid)
    return acc


ROW0_LOG_A = -1e30


def _row0_mask(rid):
    return jnp.where(rid == 0, ROW0_LOG_A, 0.0)


def _row0_bias(is_first_group, row0_mask):
    return is_first_group.astype(F32) * row0_mask


def _lru_gates(pa, px, ba, bx, sp8, row0_bias):
    r = _sig(pa + ba)
    i = _sig(px + bx)
    la = row0_bias - r * sp8
    a = jnp.exp(la)
    return r, i, a, _neg_expm1(2.0 * la, a * a)


def _mix_fwd(z, ln_g, ln_b, wm, bias, cw, cb, wax, ba, bx, lam, goa, gob, ex_arrs, ex_scatter):
    t_len = z.shape[0]
    n_chunk = t_len // CHUNK
    ex = _Exchange(ex_arrs, ex_scatter)
    n_in, n_out, n_scratch = 13, 5, 7

    def body(*refs):
        (z_ref, lng_ref, lnb_ref, wm_ref, bias_ref, cw_ref, cb_ref, wax_ref, ba_ref, bx_ref, lam_ref, goa_ref,
         gob_ref) = refs[:n_in]
        ex_in = refs[n_in:n_in + ex.n]
        y_ref, h_ref, vhb_ref, xcb_ref, rs_ref = refs[n_in + ex.n:n_in + ex.n + n_out]
        ex_out = refs[n_in + ex.n + n_out:n_in + 2 * ex.n + n_out]
        vn_s, xc_s, mixed_s, pre_s, y_s, carry_s, halo_s = refs[n_in + 2 * ex.n + n_out:n_in + 2 * ex.n + n_out + n_scratch]
        ex_sems = refs[n_in + 2 * ex.n + n_out + n_scratch:]
        step = pl.program_id(0)
        rid = _row_ids(D_BR)

        @pl.when(step == 0)
        def _():
            ex.start(ex_in, ex_out, ex_sems)
            carry_s[...] = jnp.zeros_like(carry_s)
            halo_s[...] = jnp.zeros_like(halo_s)

        lng, lnb, cb = lng_ref[...], lnb_ref[...], cb_ref[...]
        ba, bx, goa, gob = ba_ref[...], bx_ref[...], goa_ref[...], gob_ref[...]
        sp8 = LRU_C * _softplus(-lam_ref[...])
        row0 = _row0_mask(rid)

        def chunk(c_id, z_ref, y_ref, h_ref, vhb_ref, xcb_ref, rs_ref):
            def phase1(g, prev):
                rows = _rows(g)
                vg = _gelu(z_ref[rows, D_BR:2 * D_BR])
                xm = vg - _mean_last(vg)
                rs = lax.rsqrt(_mean_last(xm * xm) + EPS)
                vn_s[rows, :] = xm * rs
                rs_ref[rows, :] = jnp.broadcast_to(rs, (ROWS, HEAD))
                xb = z_ref[rows, 3 * D_BR:4 * D_BR]
                xc_s[rows, :] = _conv_rows(xb, prev, cw_ref, cb, rid)
                return xb

            halo_s[...] = _loop(N_GROUP, phase1, halo_s[...], unroll=8)
            vhb_ref[...] = vn_s[...].astype(BF16)
            xcb_ref[...] = xc_s[...].astype(BF16)

            for h in range(N_HEAD):
                cs = slice(h * HEAD, (h + 1) * HEAD)
                mixed_s[:, cs] = _dot(wm_ref[h], (vn_s[:, cs] * lng[:, cs] + lnb[:, cs]).astype(BF16))
                pre = _dot(xcb_ref[:, cs], wax_ref[h])
                pre_s[:, cs] = pre[:, :HEAD]
                pre_s[:, D_BR + h * HEAD:D_BR + (h + 1) * HEAD] = pre[:, HEAD:]

            def phase3(g, carry):
                rows = _rows(g)
                ug = _gelu(z_ref[rows, 0:D_BR])
                ga = z_ref[rows, 2 * D_BR:3 * D_BR]
                ya = ug * (mixed_s[rows, :] + bias_ref[rows, :]) * (ga * _sig(ga))
                y_s[rows, 0:D_BR] = ya * lax.rsqrt(_mean_last(ya * ya) + EPS) * goa

                bias0 = _row0_bias(jnp.logical_and(c_id == 0, g == 0), row0)
                _, i, a, m2 = _lru_gates(pre_s[rows, 0:D_BR], pre_s[rows, D_BR:2 * D_BR], ba, bx, sp8, bias0)
                b = jnp.sqrt(m2) * i * xc_s[rows, :]
                for d in (1, 2, 4):
                    a_sh = jnp.where(rid >= d, pltpu.roll(a, d, 0), 1.0)
                    b_sh = jnp.where(rid >= d, pltpu.roll(b, d, 0), 0.0)
                    b = a * b_sh + b
                    a = a * a_sh
                hh = b + a * carry
                h_ref[rows, :] = hh
                gb = z_ref[rows, 4 * D_BR:5 * D_BR]
                yb = hh * (gb * _sig(gb))
                y_s[rows, D_BR:2 * D_BR] = yb * lax.rsqrt(_mean_last(yb * yb) + EPS) * gob
                return _bcast_row(hh, ROWS - 1)

            carry_s[...] = _loop(N_GROUP, phase3, carry_s[...])
            y_ref[...] = y_s[...].astype(BF16)

        for sub in range(MIX_SUB):
            part = lambda ref, sub=sub: ref.at[pl.ds(sub * CHUNK, CHUNK)]
            chunk(step * MIX_SUB + sub, part(z_ref), part(y_ref), part(h_ref), part(vhb_ref), part(xcb_ref),
                  part(rs_ref))

        @pl.when(step == n_chunk // MIX_SUB - 1)
        def _():
            ex.wait(ex_in, ex_out, ex_sems)

    vec = pl.BlockSpec((1, D_BR), lambda i: (0, 0))
    blk = MIX_SUB * CHUNK
    res = pl.pallas_call(
        body, name="mix_fwd", grid=(n_chunk // MIX_SUB,),
        in_specs=[pl.BlockSpec((blk, D_IN), lambda i: (i, 0)), vec, vec,
                  pl.BlockSpec((N_HEAD, HEAD, HEAD), lambda i: (0, 0, 0)),
                  pl.BlockSpec((CHUNK, D_BR), lambda i: (0, 0)),
                  pl.BlockSpec((ROWS, D_BR), lambda i: (0, 0)), vec,
                  pl.BlockSpec((N_HEAD, HEAD, 2 * HEAD), lambda i: (0, 0, 0)), vec, vec, vec, vec, vec]
        + [ANY_SPEC] * ex.n,
        out_specs=[pl.BlockSpec((blk, 2 * D_BR), lambda i: (i, 0)), pl.BlockSpec((blk, D_BR), lambda i: (i, 0)),
                   pl.BlockSpec((blk, D_BR), lambda i: (i, 0)), pl.BlockSpec((blk, D_BR), lambda i: (i, 0)),
                   pl.BlockSpec((blk, HEAD), lambda i: (i, 0))] + [ANY_SPEC] * ex.n,
        out_shape=[SDS((t_len, 2 * D_BR), BF16), SDS((t_len, D_BR), F32), SDS((t_len, D_BR), BF16),
                   SDS((t_len, D_BR), BF16), SDS((t_len, HEAD), F32)] + ex.out_shape,
        scratch_shapes=[pltpu.VMEM((CHUNK, D_BR), F32), pltpu.VMEM((CHUNK, D_BR), F32), pltpu.VMEM((CHUNK, D_BR), F32),
                        pltpu.VMEM((CHUNK, 2 * D_BR), F32), pltpu.VMEM((CHUNK, 2 * D_BR), F32),
                        pltpu.VMEM((ROWS, D_BR), F32), pltpu.VMEM((ROWS, D_BR), F32)] + ex.scratch,
        compiler_params=_params(("arbitrary",), 32),
    )(z, ln_g, ln_b, wm, bias, cw, cb, wax, ba, bx, lam, goa, gob, *ex_arrs)
    return res[:n_out], res[n_out:]


def _load_weight(w_hbm, w_vmem, sem):
    @pl.when(pl.program_id(0) == 0)
    def _():
        cp = pltpu.make_async_copy(w_hbm, w_vmem, sem)
        cp.start()
        cp.wait()


def _out_proj(y, x, w_out, post_g, tm=512):
    t_len = y.shape[0]

    def body(y_ref, x_ref, w_hbm, g_ref, h1_ref, ob_ref, w_s, o_s, sem):
        _load_weight(w_hbm, w_s, sem)
        g = g_ref[...]
        blk = D_MODEL // N_DEV
        row_groups = [slice(q * TILE_ROWS, (q + 1) * TILE_ROWS) for q in range(tm // TILE_ROWS)]
        part = [jnp.zeros((TILE_ROWS, LANES), F32) for _ in row_groups]
        for j in range(N_DEV):
            cols = slice(j * blk, (j + 1) * blk)
            o_s[:, cols] = _dot(y_ref[...], w_s[:, cols])
            for q, rows in enumerate(row_groups):
                o = o_s[rows, cols]
                ob_ref[rows, cols] = o.astype(BF16)
                sq = o * o
                for k in range(blk // LANES):
                    part[q] = part[q] + sq[:, k * LANES:(k + 1) * LANES]
        for q, rows in enumerate(row_groups):
            ms = jnp.sum(part[q], axis=-1, keepdims=True) * (1.0 / D_MODEL)
            h1_ref[rows, :] = x_ref[rows, :] + o_s[rows, :] * lax.rsqrt(ms + EPS) * g

    tile = pl.BlockSpec((tm, D_MODEL), lambda i: (i, 0))
    return pl.pallas_call(
        body, name="out_proj", grid=(t_len // tm,),
        in_specs=[tile, tile, pl.BlockSpec(memory_space=pl.ANY), pl.BlockSpec((1, D_MODEL), lambda i: (0, 0))],
        out_specs=[tile, tile],
        out_shape=[SDS((t_len, D_MODEL), F32), SDS((t_len, D_MODEL), BF16)],
        scratch_shapes=[pltpu.VMEM((D_MODEL, D_MODEL), BF16), pltpu.VMEM((tm, D_MODEL), F32), pltpu.SemaphoreType.DMA],
        compiler_params=_params(("arbitrary",), 44),
    )(y, x, w_out, post_g)


def _ple_loss(h1, p, tgt, w_pg, w_pe_g, tm=256):
    t_len = h1.shape[0]
    n_tile = t_len // tm
    pe_shard = D_MODEL // N_DEV

    def body(h1_ref, p_ref, t_ref, w_hbm, wpe_ref, dh2_ref, dgl_ref, h1b_ref, loss_ref, dwpe_ref, w_s, pe_s, gl_s, acc_s,
             dpe_s, gpe_s, sem):
        _load_weight(w_hbm, w_s, sem)
        i = pl.program_id(0)

        @pl.when(i == 0)
        def _():
            acc_s[...] = jnp.zeros_like(acc_s)
            gpe_s[...] = jnp.zeros_like(gpe_s)

        h1b_ref[...] = h1_ref[...].astype(BF16)
        pb = p_ref[...].astype(BF16)
        for j in range(N_DEV):
            cols = slice(j * pe_shard, (j + 1) * pe_shard)
            pe_s[:, cols] = _dot(pb, wpe_ref[j])
            gl_s[:, cols] = _dot(h1b_ref[...], w_s[:, cols])
            acc = acc_s[:, cols]
            for q in range(tm // TILE_ROWS):
                rows = slice(q * TILE_ROWS, (q + 1) * TILE_ROWS)
                pe = pe_s[rows, cols]
                g = _sig(gl_s[rows, cols])
                e = h1_ref[rows, cols] + pe * g - t_ref[rows, cols]
                dh2 = e * (1.0 / D_MODEL)
                dh2_ref[rows, cols] = dh2
                dpe_s[rows, cols] = (dh2 * g).astype(BF16)
                dgl_ref[rows, cols] = (dh2 * pe * g * (1.0 - g)).astype(BF16)
                acc = acc + _fold_rows(e * e)
            acc_s[:, cols] = acc
        gpe_s[...] += _dot_tn(pb, dpe_s[...])

        @pl.when(i == n_tile - 1)
        def _():
            loss_ref[...] = jnp.full(loss_ref.shape, 0.5 / D_MODEL * jnp.sum(acc_s[...]), F32)
            for j in range(N_DEV):
                dwpe_ref[j] = gpe_s[:, j * pe_shard:(j + 1) * pe_shard].astype(BF16)

    tile = pl.BlockSpec((tm, D_MODEL), lambda i: (i, 0))
    pe_blocks = pl.BlockSpec((N_DEV, D_PLE, pe_shard), lambda i: (0, 0, 0))
    return pl.pallas_call(
        body, name="ple_loss", grid=(n_tile,),
        in_specs=[tile, pl.BlockSpec((tm, D_PLE), lambda i: (i, 0)), tile, pl.BlockSpec(memory_space=pl.ANY), pe_blocks],
        out_specs=[tile, tile, tile, pl.BlockSpec((ROWS, HEAD), lambda i: (0, 0)), pe_blocks],
        out_shape=[SDS((t_len, D_MODEL), F32), SDS((t_len, D_MODEL), BF16), SDS((t_len, D_MODEL), BF16),
                   SDS((ROWS, HEAD), F32), SDS((N_DEV, D_PLE, pe_shard), BF16)],
        scratch_shapes=[pltpu.VMEM((D_MODEL, D_MODEL), BF16), pltpu.VMEM((tm, D_MODEL), F32),
                        pltpu.VMEM((tm, D_MODEL), F32), pltpu.VMEM((ROWS, D_MODEL), F32), pltpu.VMEM((tm, D_MODEL), BF16),
                        pltpu.VMEM((D_PLE, D_MODEL), F32), pltpu.SemaphoreType.DMA],
        compiler_params=_params(("arbitrary",), 48),
    )(h1, p, tgt, w_pg, w_pe_g)


def _tail_bwd(dh2, dgl, ob, w_pg, w_out, post_g, tm=256):
    t_len = dh2.shape[0]
    n_tile = t_len // tm

    def body(dh2_ref, dgl_ref, ob_ref, wpg_hbm, wout_hbm, g_ref, dh1_ref, do_ref, dy_ref, dg_ref, wpg_s, wout_s, t_s,
             acc_s, sems):
        i = pl.program_id(0)
        load_wpg = pltpu.make_async_copy(wpg_hbm, wpg_s, sems.at[0])
        load_wout = pltpu.make_async_copy(wout_hbm, wout_s, sems.at[1])

        @pl.when(i == 0)
        def _():
            load_wpg.start()
            load_wout.start()
            acc_s[...] = jnp.zeros_like(acc_s)
            load_wpg.wait()

        g = g_ref[...]
        blk = D_MODEL // N_DEV
        row_groups = [slice(q * TILE_ROWS, (q + 1) * TILE_ROWS) for q in range(tm // TILE_ROWS)]
        rr = []
        for rows in row_groups:
            o = ob_ref[rows, :].astype(F32)
            rr.append(lax.rsqrt(_mean_last(o * o) + EPS))
        part = [jnp.zeros((TILE_ROWS, LANES), F32) for _ in row_groups]
        for j in range(N_DEV):
            cols = slice(j * blk, (j + 1) * blk)
            t_s[:, cols] = _dot_nt(dgl_ref[...], wpg_s[cols, :])
            acc = acc_s[:, cols]
            for q, rows in enumerate(row_groups):
                dh1 = dh2_ref[rows, cols] + t_s[rows, cols]
                dh1_ref[rows, cols] = dh1
                on = ob_ref[rows, cols].astype(F32) * rr[q]
                pr = dh1 * g[:, cols] * on
                for k in range(blk // LANES):
                    part[q] = part[q] + pr[:, k * LANES:(k + 1) * LANES]
                acc = acc + _fold_rows(dh1 * on)
            acc_s[:, cols] = acc
        for q, rows in enumerate(row_groups):
            m = jnp.sum(part[q], axis=-1, keepdims=True) * (1.0 / D_MODEL)
            on = ob_ref[rows, :].astype(F32) * rr[q]
            do_ref[rows, :] = (rr[q] * (dh1_ref[rows, :] * g - on * m)).astype(BF16)

        @pl.when(i == 0)
        def _():
            load_wout.wait()

        dy_ref[...] = _dot_nt(do_ref[...], wout_s[...]).astype(BF16)

        @pl.when(i == n_tile - 1)
        def _():
            dg_ref[...] = jnp.sum(acc_s[...], axis=0, keepdims=True)

    tile = pl.BlockSpec((tm, D_MODEL), lambda i: (i, 0))
    vec = pl.BlockSpec((1, D_MODEL), lambda i: (0, 0))
    hbm = pl.BlockSpec(memory_space=pl.ANY)
    return pl.pallas_call(
        body, name="tail_bwd", grid=(n_tile,),
        in_specs=[tile, tile, tile, hbm, hbm, vec],
        out_specs=[tile, tile, tile, vec],
        out_shape=[SDS((t_len, D_MODEL), F32), SDS((t_len, D_MODEL), BF16), SDS((t_len, D_MODEL), BF16),
                   SDS((1, D_MODEL), F32)],
        scratch_shapes=[pltpu.VMEM((D_MODEL, D_MODEL), BF16), pltpu.VMEM((D_MODEL, D_MODEL), BF16),
                        pltpu.VMEM((tm, D_MODEL), F32), pltpu.VMEM((ROWS, D_MODEL), F32), pltpu.SemaphoreType.DMA((2,))],
        compiler_params=_params(("arbitrary",), 48),
    )(dh2, dgl, ob, w_pg, w_out, post_g)


def _mix_bwd(z, dy, h, vhb, xcb, rs, ln_g, ln_b, wm, wm_t, bias, cw, cb, wax, wax_t, ba, bx, lam, goa, gob, ex_arrs,
             ex_scatter):
    t_len = z.shape[0]
    n_chunk = t_len // CHUNK
    halo_blocks = CHUNK // ROWS
    ex = _Exchange(ex_arrs, ex_scatter)
    n_in, n_out, n_scratch = 21, 5, 16

    blocked = (0, 1, 2, 4, 5, 6, n_in + ex.n)

    def body(*refs):
        step = pl.program_id(0)
        for sub in reversed(range(MIX_SUB)):
            views = list(refs)
            for idx in blocked:
                views[idx] = refs[idx].at[pl.ds(sub * CHUNK, CHUNK)]
            h_before = refs[2].at[pl.ds(sub * CHUNK - ROWS, ROWS)] if sub else refs[3]
            chunk((n_chunk // MIX_SUB - 1 - step) * MIX_SUB + sub,
                  step == 0 if sub == MIX_SUB - 1 else None,
                  step == n_chunk // MIX_SUB - 1 if sub == 0 else None, h_before, *views)

    def chunk(c_id, first, last, h_before, *refs):
        (z_ref, dy_ref, h_ref, hhalo_ref, vhb_ref, xcb_ref, rs_ref, lng_ref, lnb_ref, wm_ref, wmt_ref, bias_ref, cw_ref,
         cb_ref, wax_ref, waxt_ref, ba_ref, bx_ref, lam_ref, goa_ref, gob_ref) = refs[:n_in]
        ex_in = refs[n_in:n_in + ex.n]
        dz_ref, vecs_ref, dws_ref, dwax_ref, dbs_ref = refs[n_in + ex.n:n_in + ex.n + n_out]
        ex_out = refs[n_in + ex.n + n_out:n_in + 2 * ex.n + n_out]
        (vnb_s, vh_s, xc_s, mixed_s, pre_s, dmix_s, dvn_s, dho_s, dxc_s, dpre_s, dz_s, acc_s, accdm_s,
         cg_s, ca_s, dxchalo_s) = refs[n_in + 2 * ex.n + n_out:n_in + 2 * ex.n + n_out + n_scratch]
        ex_sems = refs[n_in + 2 * ex.n + n_out + n_scratch:]
        rid = _row_ids(D_BR)
        first_chunk = c_id == 0

        if first is not None:
            @pl.when(first)
            def _():
                ex.start(ex_in, ex_out, ex_sems)
                acc_s[...] = jnp.zeros_like(acc_s)
                accdm_s[...] = jnp.zeros_like(accdm_s)
                cg_s[...] = jnp.zeros_like(cg_s)
                ca_s[...] = jnp.zeros_like(ca_s)
                dxchalo_s[...] = jnp.zeros_like(dxchalo_s)
                dws_ref[...] = jnp.zeros_like(dws_ref)
                dwax_ref[...] = jnp.zeros_like(dwax_ref)

        lng, lnb = lng_ref[...], lnb_ref[...]
        h_halo = jnp.where(first_chunk, 0.0, h_before[...])

        def prev_rows(ref, cols, g, halo):
            before = ref[pl.ds(pl.multiple_of(jnp.maximum(g - 1, 0) * ROWS, ROWS), ROWS), cols]
            return jnp.where(g > 0, before, halo)

        vh_s[...] = vhb_ref[...].astype(F32)
        xc_s[...] = xcb_ref[...].astype(F32)

        for hd in range(N_HEAD):
            cs = slice(hd * HEAD, (hd + 1) * HEAD)
            vnb_s[:, cs] = (vh_s[:, cs] * lng[:, cs] + lnb[:, cs]).astype(BF16)
            mixed_s[:, cs] = _dot(wm_ref[hd], vnb_s[:, cs])
            pre = _dot(xcb_ref[:, cs], wax_ref[hd])
            pre_s[:, cs] = pre[:, :HEAD]
            pre_s[:, D_BR + hd * HEAD:D_BR + (hd + 1) * HEAD] = pre[:, HEAD:]

        goa, gob = goa_ref[...], gob_ref[...]

        def phase3(g, _):
            rows = _rows(g)
            ug, dug = _gelu(z_ref[rows, 0:D_BR], with_grad=True)
            ga = z_ref[rows, 2 * D_BR:3 * D_BR]
            sga = _sig(ga)
            sa = ga * sga
            mixed = mixed_s[rows, :] + bias_ref[rows, :]
            ya0 = ug * mixed
            ya = ya0 * sa
            ra = lax.rsqrt(_mean_last(ya * ya) + EPS)
            dyan = dy_ref[rows, 0:D_BR].astype(F32)
            acc_s[V_GOUT_A] += dyan * ya * ra
            dyg = dyan * goa
            dya = ra * dyg - ya * (ra * ra * ra) * _mean_last(dyg * ya)
            dya0 = dya * sa
            dz_s[rows, 2 * D_BR:3 * D_BR] = dya * ya0 * _silu_grad(sga, sa)
            dmix = dya0 * ug
            dmix_s[rows, :] = dmix
            accdm_s[rows, :] += dmix
            dz_s[rows, 0:D_BR] = dya0 * mixed * dug

            hh = h_ref[rows, :]
            gb = z_ref[rows, 4 * D_BR:5 * D_BR]
            sgb = _sig(gb)
            sb = gb * sgb
            yb = hh * sb
            rb = lax.rsqrt(_mean_last(yb * yb) + EPS)
            dybn = dy_ref[rows, D_BR:2 * D_BR].astype(F32)
            acc_s[V_GOUT_B] += dybn * yb * rb
            dyg = dybn * gob
            dyb = rb * dyg - yb * (rb * rb * rb) * _mean_last(dyg * yb)
            dho_s[rows, :] = dyb * sb
            dz_s[rows, 4 * D_BR:5 * D_BR] = dyb * hh * _silu_grad(sgb, sb)
            return 0

        _loop(N_GROUP, phase3, 0)

        for hd in range(N_HEAD):
            cs = slice(hd * HEAD, (hd + 1) * HEAD)
            dmb = dmix_s[:, cs].astype(BF16)
            dvn_s[:, cs] = _dot(wmt_ref[hd], dmb)
            dws_ref[hd] += _dot_nt(dmb, vnb_s[:, cs])

        def phase5(g, _):
            rows = _rows(g)
            dvn = dvn_s[rows, :]
            vh = vh_s[rows, :]
            acc_s[V_LN_G] += dvn * vh
            acc_s[V_LN_B] += dvn
            dvh = dvn * lng
            rs = rs_ref[rows, 0:1]
            dvg = rs * (dvh - _mean_last(dvh) - vh * _mean_last(dvh * vh))
            dz_s[rows, D_BR:2 * D_BR] = dvg * _gelu(z_ref[rows, D_BR:2 * D_BR], with_grad=True)[1]
            return 0

        _loop(N_GROUP, phase5, 0)

        ba, bx = ba_ref[...], bx_ref[...]
        sp8 = LRU_C * _softplus(-lam_ref[...])
        row0 = _row0_mask(rid)

        def phase6(k, carry):
            cg, ca = carry
            g = N_GROUP - 1 - k
            rows = _rows(g)
            bias0 = _row0_bias(jnp.logical_and(first_chunk, g == 0), row0)
            r, i, a, m2 = _lru_gates(pre_s[rows, 0:D_BR], pre_s[rows, D_BR:2 * D_BR], ba, bx, sp8, bias0)
            a_nx = jnp.where(rid < ROWS - 1, pltpu.roll(a, ROWS - 1, 0), ca)
            aa, bb = a_nx, dho_s[rows, :]
            for d in (1, 2, 4):
                a_sh = jnp.where(rid < ROWS - d, pltpu.roll(aa, ROWS - d, 0), 1.0)
                b_sh = jnp.where(rid < ROWS - d, pltpu.roll(bb, ROWS - d, 0), 0.0)
                bb = aa * b_sh + bb
                aa = aa * a_sh
            gg = bb + aa * cg
            hh = h_ref[rows, :]
            hprev = _shift_down(hh, prev_rows(h_ref, slice(None), g, h_halo), 1, rid)
            xc = xc_s[rows, :]
            gx = gg * xc
            dla = gg * hprev * a - gx * i * (a * a) * lax.rsqrt(m2)
            acc_s[V_LAM] += -(dla * r)
            dpa = -(dla * sp8) * r * (1.0 - r)
            mi = jnp.sqrt(m2) * i
            dpx = gx * mi * (1.0 - i)
            acc_s[V_B_A] += dpa
            acc_s[V_B_X] += dpx
            dpre_s[rows, 0:D_BR] = dpa
            dpre_s[rows, D_BR:2 * D_BR] = dpx
            dxc_s[rows, :] = gg * mi
            return _bcast_row(gg, 0), _bcast_row(a, 0)

        cg, ca = _loop(N_GROUP, phase6, (cg_s[...], ca_s[...]))
        cg_s[...] = cg
        ca_s[...] = ca

        for hd in range(N_HEAD):
            cs = slice(hd * HEAD, (hd + 1) * HEAD)
            dpre = jnp.concatenate([dpre_s[:, cs], dpre_s[:, D_BR + hd * HEAD:D_BR + (hd + 1) * HEAD]], axis=1).astype(BF16)
            dxc_s[:, cs] += _dot(dpre, waxt_ref[hd])
            dwax_ref[hd] += _dot_tn(xcb_ref[:, cs], dpre)

        def phase8(k, nxt):
            g = N_GROUP - 1 - k
            rows = _rows(g)
            dxc = dxc_s[rows, :]
            acc_s[V_CONV_B] += dxc
            xb = z_ref[rows, 3 * D_BR:4 * D_BR]
            dxb = cw_ref[3:4, :] * dxc
            acc_s[V_CONV_W + 3] += dxc * xb
            for j in range(1, CONV_W):
                later = _shift_up(dxc, nxt, j, rid)
                dxb = dxb + cw_ref[3 - j:4 - j, :] * later
                acc_s[V_CONV_W + 3 - j] += later * xb
            dz_s[rows, 3 * D_BR:4 * D_BR] = dxb
            return dxc

        dxchalo_s[...] = _loop(N_GROUP, phase8, dxchalo_s[...])
        dz_ref[...] = dz_s[...].astype(BF16)

        if last is not None:
            @pl.when(last)
            def _():
                for v in range(N_VEC):
                    vecs_ref[v:v + 1, :] = jnp.sum(acc_s[v], axis=0, keepdims=True)
                lam = lam_ref[...]
                vecs_ref[V_LAM:V_LAM + 1, :] = vecs_ref[V_LAM:V_LAM + 1, :] * (-LRU_C * _sig(-lam))
                tril = (lax.broadcasted_iota(jnp.int32, (HEAD, HEAD), 0)
                        >= lax.broadcasted_iota(jnp.int32, (HEAD, HEAD), 1))
                ones = jnp.ones((ROWS, HEAD), BF16)
                for hd in range(N_HEAD):
                    cs = slice(hd * HEAD, (hd + 1) * HEAD)
                    dws_ref[hd] = jnp.where(tril, dws_ref[hd], 0.0)
                    blk = accdm_s[:, cs]
                    hi = blk.astype(BF16)
                    lo = (blk - hi.astype(F32)).astype(BF16)
                    dbs_ref[hd:hd + 1, :] = (_dot_nt(ones, hi) + _dot_nt(ones, lo))[0:1, :]
                ex.wait(ex_in, ex_out, ex_sems)

    vec = pl.BlockSpec((1, D_BR), lambda i: (0, 0))
    n_step = n_chunk // MIX_SUB
    rows_blk = MIX_SUB * CHUNK
    rev = lambda i: (n_step - 1 - i, 0)
    halo = lambda col: (lambda i: (jnp.maximum((n_step - 1 - i) * MIX_SUB * halo_blocks - 1, 0), col))
    full3 = lambda a, b, c: pl.BlockSpec((a, b, c), lambda i: (0, 0, 0))
    big = lambda w: pltpu.VMEM((CHUNK, w), F32)
    res = pl.pallas_call(
        body, name="mix_bwd", grid=(n_step,),
        in_specs=[pl.BlockSpec((rows_blk, D_IN), rev), pl.BlockSpec((rows_blk, 2 * D_BR), rev),
                  pl.BlockSpec((rows_blk, D_BR), rev),
                  pl.BlockSpec((ROWS, D_BR), halo(0)), pl.BlockSpec((rows_blk, D_BR), rev),
                  pl.BlockSpec((rows_blk, D_BR), rev),
                  pl.BlockSpec((rows_blk, HEAD), rev), vec, vec,
                  full3(N_HEAD, HEAD, HEAD), full3(N_HEAD, HEAD, HEAD),
                  pl.BlockSpec((CHUNK, D_BR), lambda i: (0, 0)), pl.BlockSpec((ROWS, D_BR), lambda i: (0, 0)), vec,
                  full3(N_HEAD, HEAD, 2 * HEAD), full3(N_HEAD, 2 * HEAD, HEAD), vec, vec, vec, vec, vec]
        + [ANY_SPEC] * ex.n,
        out_specs=[pl.BlockSpec((rows_blk, D_IN), rev), pl.BlockSpec((N_VEC, D_BR), lambda i: (0, 0)),
                   full3(N_HEAD, HEAD, HEAD), full3(N_HEAD, HEAD, 2 * HEAD),
                   pl.BlockSpec((N_HEAD, HEAD), lambda i: (0, 0))] + [ANY_SPEC] * ex.n,
        out_shape=[SDS((t_len, D_IN), BF16), SDS((N_VEC, D_BR), F32), SDS((N_HEAD, HEAD, HEAD), F32),
                   SDS((N_HEAD, HEAD, 2 * HEAD), F32), SDS((N_HEAD, HEAD), F32)] + ex.out_shape,
        scratch_shapes=[pltpu.VMEM((CHUNK, D_BR), BF16), big(D_BR), big(D_BR), big(D_BR), big(2 * D_BR), big(D_BR),
                        big(D_BR), big(D_BR), big(D_BR), big(2 * D_BR), big(D_IN),
                        pltpu.VMEM((N_VEC, ROWS, D_BR), F32), big(D_BR),
                        pltpu.VMEM((ROWS, D_BR), F32), pltpu.VMEM((ROWS, D_BR), F32), pltpu.VMEM((ROWS, D_BR), F32)]
        + ex.scratch,
        compiler_params=_params(("arbitrary",), 48),
    )(z, dy, h, h, vhb, xcb, rs, ln_g, ln_b, wm, wm_t, bias, cw, cb, wax, wax_t, ba, bx, lam, goa, gob, *ex_arrs)
    return res[:n_out], res[n_out:]


def _in_bwd(dz, w_in_g, x, dh1, pre_g, tm=256):
    t_len = x.shape[0]
    n_tile = t_len // tm

    def body(dz_ref, w_hbm, x_ref, dh1_ref, g_ref, gx_ref, dg_ref, w_s, t_even, t_odd, dg_s, w_sems):
        i = pl.program_id(0)

        @pl.when(i == 0)
        def _():
            loads = [pltpu.make_async_copy(w_hbm.at[s], w_s.at[:, s * W_IN_SHARD:(s + 1) * W_IN_SHARD], w_sems.at[s])
                     for s in range(N_DEV)]
            for cp in loads:
                cp.start()
            dg_s[...] = jnp.zeros_like(dg_s)
            for s, cp in enumerate(loads):
                cp.wait()
                cols = slice(s * W_IN_SHARD, (s + 1) * W_IN_SHARD)
                part = _dot_nt(dz_ref[:, cols], w_s[:, cols])
                t_even[...] = part if s == 0 else t_even[...] + part

        def step(t_new, t_old):
            g = g_ref[...]
            acc = dg_s[...]
            for q in range(tm // TILE_ROWS):
                rows = slice(q * TILE_ROWS, (q + 1) * TILE_ROWS)
                xv = x_ref[rows, :]
                r = lax.rsqrt(_mean_last(xv * xv) + EPS)
                xh = xv * r
                dhn = t_old[rows, :]
                dg = dhn * g
                gx_ref[rows, :] = dh1_ref[rows, :] + r * (dg - xh * _mean_last(dg * xh))
                acc = acc + _fold_rows(dhn * xh)
            dg_s[...] = acc
            t_new[...] = _dot_nt(dz_ref[...], w_s[...])

        @pl.when((i % 2 == 0) & (i > 0))
        def _():
            step(t_even, t_odd)

        @pl.when(i % 2 == 1)
        def _():
            step(t_odd, t_even)

        @pl.when(i == n_tile)
        def _():
            dg_ref[...] = jnp.sum(dg_s[...], axis=0, keepdims=True)

    matmul_tile = lambda i: (jnp.minimum(i, n_tile - 1), 0)
    rows_tile = lambda i: (jnp.maximum(i - 1, 0), 0)
    res = pl.pallas_call(
        body, name="in_bwd", grid=(n_tile + 1,),
        in_specs=[pl.BlockSpec((tm, D_IN), matmul_tile), ANY_SPEC, pl.BlockSpec((tm, D_MODEL), rows_tile),
                  pl.BlockSpec((tm, D_MODEL), rows_tile), pl.BlockSpec((1, D_MODEL), lambda i: (0, 0))],
        out_specs=[pl.BlockSpec((tm, D_MODEL), rows_tile), pl.BlockSpec((1, D_MODEL), lambda i: (0, 0))],
        out_shape=[SDS((t_len, D_MODEL), F32), SDS((1, D_MODEL), F32)],
        scratch_shapes=[pltpu.VMEM((D_MODEL, D_IN), BF16), pltpu.VMEM((tm, D_MODEL), F32), pltpu.VMEM((tm, D_MODEL), F32),
                        pltpu.VMEM((ROWS, D_MODEL), F32), pltpu.SemaphoreType.DMA((N_DEV,))],
        compiler_params=_params(("arbitrary",), 54),
    )(dz, w_in_g, x, dh1, pre_g)
    return res[0], res[1]


def _grad_w(a, b, bn, shard_major, name, tk=1024, ex_arrs=(), ex_scatter=()):
    t_len, m = a.shape
    n = b.shape[1]
    n_j, n_k = n // bn, t_len // tk
    ex = _Exchange(ex_arrs, ex_scatter)

    def body(a_ref, b_ref, *refs):
        ex_in, o_ref, ex_out = refs[:ex.n], refs[ex.n], refs[ex.n + 1:2 * ex.n + 1]
        acc_s, ex_sems = refs[2 * ex.n + 1], refs[2 * ex.n + 2:]
        j, k = pl.program_id(0), pl.program_id(1)
        if ex.n:
            @pl.when(jnp.logical_and(j == 0, k == 0))
            def _():
                ex.start(ex_in, ex_out, ex_sems)

        @pl.when(k == 0)
        def _():
            acc_s[...] = jnp.zeros_like(acc_s)

        acc_s[...] += _dot_tn(a_ref[...], b_ref[...])

        @pl.when(k == n_k - 1)
        def _():
            o_ref[...] = acc_s[...].astype(BF16)

        if ex.n:
            @pl.when(jnp.logical_and(j == n_j - 1, k == n_k - 1))
            def _():
                ex.wait(ex_in, ex_out, ex_sems)

    if shard_major:
        out_spec, out_shape = pl.BlockSpec((None, m, bn), lambda j, k: (j, 0, 0)), SDS((n_j, m, bn), BF16)
    else:
        out_spec, out_shape = pl.BlockSpec((m, bn), lambda j, k: (0, j)), SDS((m, n), BF16)
    res = pl.pallas_call(
        body, name=name, grid=(n_j, n_k),
        in_specs=[pl.BlockSpec((tk, m), lambda j, k: (k, 0)), pl.BlockSpec((tk, bn), lambda j, k: (k, j))]
        + [ANY_SPEC] * ex.n,
        out_specs=[out_spec] + [ANY_SPEC] * ex.n, out_shape=[out_shape] + ex.out_shape,
        scratch_shapes=[pltpu.VMEM((m, bn), F32)] + (ex.scratch if ex.n else []),
        compiler_params=_params(("arbitrary", "arbitrary"), 40),
    )(a, b, *ex_arrs)
    return res[0], res[1:]


RS_CHIPS = (6, 2, 4, 0)
RS_SLOTS = (0, 1, 2, 4, 6)


def _grad_w_in_pairs(hn, dz, ex_arrs, ex_scatter, tk=1024):
    t_len = hn.shape[0]
    n_k = t_len // tk
    n_ph = len(RS_CHIPS)
    ex = _Exchange(ex_arrs, ex_scatter)
    me_out = 4 * lax.axis_index("x") + 2 * lax.axis_index("y") + lax.axis_index("c")
    order = jnp.stack([(me_out ^ chip) // 2 for chip in RS_CHIPS]).astype(jnp.int32)
    slots = jnp.stack([me_out ^ k for k in RS_SLOTS]).astype(jnp.int32)
    shard = W_IN_SHARD

    def body(order_ref, a_ref, b_ref, *refs):
        ex_in, parts_hbm, ex_out = refs[:ex.n], refs[ex.n], refs[ex.n + 1:2 * ex.n + 1]
        (acc_s, tb_s, stage_s, rx_s, d2d_send, d2d_recv, ici_send, ici_recv, sib_sems,
         loc_sem) = refs[2 * ex.n + 1:2 * ex.n + 11]
        ex_sems = refs[2 * ex.n + 11:]
        j, k = pl.program_id(0), pl.program_id(1)
        x, y, c, me = _mesh_place()
        sib = _peer(x, y, c, SIBLING)[0]

        def to_sibling(p):
            return _remote(stage_s.at[0], rx_s.at[p % 2], d2d_send.at[p], d2d_recv.at[p], sib)

        def over_ici(p):
            dev = _peer(x, y, c, RS_CHIPS[p])[0]
            return _remote(stage_s.at[1], parts_hbm.at[me], ici_send.at[p], ici_recv.at[p], dev)

        def own_chip():
            return (_remote(stage_s.at[0], parts_hbm.at[me], sib_sems.at[0], sib_sems.at[1], sib),
                    pltpu.make_async_copy(stage_s.at[1], parts_hbm.at[me], loc_sem.at[0]))

        @pl.when(jnp.logical_and(j == 0, k == 0))
        def _():
            ex.start(ex_in, ex_out, ex_sems)

        for p in range(n_ph - 1):
            for core in (0, 1):
                @pl.when(jnp.logical_and(jnp.logical_and(j == p + 1, k == 0), c == core))
                def _(p=p, core=core):
                    to_sibling(p).wait_recv()
                    if p >= 1:
                        over_ici(p - 1).wait_send()
                    mine = acc_s[:, core * shard:(core + 1) * shard]
                    stage_s[1] = (mine + rx_s[p % 2].astype(F32)).astype(BF16)
                    over_ici(p).start()

        @pl.when(k == 0)
        def _():
            acc_s[...] = jnp.zeros_like(acc_s)

        a = a_ref[...]
        acc_s[:, 0:W_BODY] += _dot_tn(a, b_ref[:, 0:W_BODY])
        acc_s[:, shard:shard + W_BODY] += _dot_tn(a, b_ref[:, shard:shard + W_BODY])
        tb_s[:, 0:W_TAIL] = b_ref[:, W_BODY:shard]
        tb_s[:, W_TAIL:2 * W_TAIL] = b_ref[:, shard + W_BODY:2 * shard]
        tails = _dot_tn(a, tb_s[...])
        acc_s[:, W_BODY:shard] += tails[:, 0:W_TAIL]
        acc_s[:, shard + W_BODY:2 * shard] += tails[:, W_TAIL:2 * W_TAIL]

        for p in range(n_ph):
            for core in (0, 1):
                @pl.when(jnp.logical_and(jnp.logical_and(j == p, k == n_k - 1), c == core))
                def _(p=p, core=core):
                    same = acc_s[:, core * shard:(core + 1) * shard]
                    other = acc_s[:, (1 - core) * shard:(2 - core) * shard]
                    if p >= 1:
                        to_sibling(p - 1).wait_send()
                    stage_s[0] = other.astype(BF16)
                    if p < n_ph - 1:
                        to_sibling(p).start()
                    else:
                        over_ici(n_ph - 2).wait_send()
                        stage_s[1] = same.astype(BF16)
                        for cp in own_chip():
                            cp.start()

        @pl.when(jnp.logical_and(j == n_ph - 1, k == n_k - 1))
        def _():
            to_sib, local = own_chip()
            to_sib.wait_send()
            local.wait()
            _remote(stage_s.at[0], parts_hbm.at[_peer(x, y, c, SIBLING)[1]], sib_sems.at[0], sib_sems.at[1], sib).wait_recv()
            for p in range(n_ph - 1):
                dev, lin = _peer(x, y, c, RS_CHIPS[p])
                _remote(stage_s.at[0], parts_hbm.at[lin], ici_send.at[p], ici_recv.at[p], dev).wait_recv()
            ex.wait(ex_in, ex_out, ex_sems)

    dma = lambda n: pltpu.SemaphoreType.DMA((n,))
    grid_spec = pltpu.PrefetchScalarGridSpec(
        num_scalar_prefetch=1, grid=(n_ph, n_k),
        in_specs=[pl.BlockSpec((tk, D_MODEL), lambda j, k, order: (k, 0)),
                  pl.BlockSpec((tk, 2 * shard), lambda j, k, order: (k, order[j]))] + [ANY_SPEC] * ex.n,
        out_specs=[ANY_SPEC] * (1 + ex.n),
        scratch_shapes=[pltpu.VMEM((D_MODEL, 2 * shard), F32), pltpu.VMEM((tk, 2 * W_TAIL), BF16),
                        pltpu.VMEM((2, D_MODEL, shard), BF16),
                        pltpu.VMEM((2, D_MODEL, shard), BF16), dma(n_ph - 1), dma(n_ph - 1), dma(n_ph - 1),
                        dma(n_ph - 1), dma(2), dma(1)] + ex.scratch)
    res = pl.pallas_call(
        body, name="grad_w_in", grid_spec=grid_spec,
        out_shape=[SDS((N_DEV, D_MODEL, shard), BF16)] + ex.out_shape,
        compiler_params=_params(("arbitrary", "arbitrary"), 54),
    )(order, hn, dz, *ex_arrs)
    return res[0], slots, res[1:]


def _sum_parts(parts, name):
    def body(p_ref, o_ref):
        g = p_ref[0].astype(F32)
        for s in range(1, parts.shape[0]):
            g = g + p_ref[s].astype(F32)
        o_ref[...] = g

    return pl.pallas_call(body, name=name, out_shape=SDS(parts.shape[1:], F32))(parts)


def _adamw_math(g, w_ref, m_ref, v_ref, g_ref, d_ref, nm_ref, nv_ref):
    c1 = 1.0 - ADAM_B1 ** ADAM_STEP
    c2 = 1.0 - ADAM_B2 ** ADAM_STEP
    g_ref[...] = g
    nm = ADAM_B1 * m_ref[...] + (1.0 - ADAM_B1) * g
    nv = ADAM_B2 * v_ref[...] + (1.0 - ADAM_B2) * (g * g)
    nm_ref[...] = nm
    nv_ref[...] = nv
    d_ref[...] = -ADAM_LR * ((nm / c1) / (jnp.sqrt(nv / c2) + ADAM_EPS) + ADAM_WD * w_ref[...])


def _adamw(parts, w, m, v, name, tr):
    rows, cols = w.shape
    n_parts = parts.shape[0]

    def body(p_ref, *refs):
        g = p_ref[0].astype(F32)
        for s in range(1, n_parts):
            g = g + p_ref[s].astype(F32)
        _adamw_math(g, *refs)

    tile = pl.BlockSpec((tr, cols), lambda i: (i, 0))
    return pl.pallas_call(
        body, name=name, grid=(rows // tr,),
        in_specs=[pl.BlockSpec((n_parts, tr, cols), lambda i: (0, i, 0)), tile, tile, tile],
        out_specs=[tile] * 4, out_shape=[SDS((rows, cols), F32)] * 4,
        compiler_params=_params(("arbitrary",), 40),
    )(parts, w, m, v)


def _adamw_unpacked(grads, triples, name):
    n = len(triples)
    n_rows = [t[0].shape[0] for t in triples]

    def body(g_ref, *refs):
        ins, outs = refs[:3 * n], refs[3 * n:]
        row = 0
        for i in range(n):
            _adamw_math(g_ref[row:row + n_rows[i], :], *ins[3 * i:3 * i + 3], *outs[4 * i:4 * i + 4])
            row += n_rows[i]
        outs[4 * n][...] = g_ref[row:row + ROWS, :]

    out_shape = [SDS((r, LANES), F32) for r in n_rows for _ in range(4)] + [SDS((ROWS, LANES), F32)]
    return pl.pallas_call(
        body, name=name, out_shape=out_shape,
        compiler_params=pltpu.CompilerParams(vmem_limit_bytes=40 * MIB),
    )(grads, *[a for t in triples for a in t])


def _adamw_slots(parts, slots, w, m, v, name, tr):
    rows, cols = w.shape
    n_slots = slots.shape[0]

    def body(slots_ref, *refs):
        g = refs[0][...].astype(F32)
        for s in range(1, n_slots):
            g = g + refs[s][...].astype(F32)
        _adamw_math(g, *refs[n_slots:])

    tile = pl.BlockSpec((tr, cols), lambda i, slots: (i, 0))
    part = lambda s: pl.BlockSpec((None, tr, cols), lambda i, slots: (slots[s], i, 0))
    grid_spec = pltpu.PrefetchScalarGridSpec(
        num_scalar_prefetch=1, grid=(rows // tr,),
        in_specs=[part(s) for s in range(n_slots)] + [tile, tile, tile], out_specs=[tile] * 4)
    return pl.pallas_call(
        body, name=name, grid_spec=grid_spec, out_shape=[SDS((rows, cols), F32)] * 4,
        compiler_params=_params(("arbitrary",), 40),
    )(slots, *([parts] * n_slots), w, m, v)


PACKED = ("gmlp_ln_g", "gmlp_ln_b", "gmlp_ws", "gmlp_bs", "conv_b", "w_a", "b_a", "w_x", "b_x", "lam", "gmlp_out_g",
          "lru_out_g", "post_g")
WEIGHTS = ("pre_g", "w_in", "gmlp_ln_g", "gmlp_ln_b", "gmlp_ws", "gmlp_bs", "conv_w", "conv_b", "w_a", "b_a", "w_x",
           "b_x", "lam", "gmlp_out_g", "lru_out_g", "w_out", "post_g", "w_pe", "w_pg")
LANES = 128


PACK_ROWS = 3200


def _pack(parts):
    rows = [p.reshape(-1, LANES) for p in parts]
    used = sum(r.shape[0] for r in rows)
    return jnp.concatenate(rows + [jnp.zeros((PACK_ROWS - used, LANES), F32)], axis=0)


def _pad_rows(a, rows):
    return jnp.concatenate([a, jnp.zeros((rows - a.shape[0],) + a.shape[1:], a.dtype)], axis=0)


def kernel(x, p, pre_g, w_in, gmlp_ln_g, gmlp_ln_b, gmlp_ws, gmlp_bs, conv_w, conv_b, w_a, b_a, w_x, b_x, lam, gmlp_out_g, lru_out_g, w_out, post_g, w_pe, w_pg, loss_target, m_pre_g, m_w_in, m_gmlp_ln_g, m_gmlp_ln_b, m_gmlp_ws, m_gmlp_bs, m_conv_w, m_conv_b, m_w_a, m_b_a, m_w_x, m_b_x, m_lam, m_gmlp_out_g, m_lru_out_g, m_w_out, m_post_g, m_w_pe, m_w_pg, v_pre_g, v_w_in, v_gmlp_ln_g, v_gmlp_ln_b, v_gmlp_ws, v_gmlp_bs, v_conv_w, v_conv_b, v_w_a, v_b_a, v_w_x, v_b_x, v_lam, v_gmlp_out_g, v_lru_out_g, v_w_out, v_post_g, v_w_pe, v_w_pg):
    args = dict(locals())
    weights = {n: args[n] for n in WEIGHTS}
    m_in = {n: args["m_" + n] for n in WEIGHTS}
    v_in = {n: args["v_" + n] for n in WEIGHTS}
    sm = {n: weights[n][0] for n in PACKED}
    shard_rows = D_MODEL // N_DEV
    xs, ps, tgt = x[0], p[0, 0], loss_target[0]

    vec = lambda a: a.reshape(1, -1)
    tril = jnp.tril(jnp.ones((CHUNK, CHUNK), dtype=bool))
    wm32 = jnp.where(tril[None], sm["gmlp_ws"], 0.0)
    wm, wm_t = wm32.astype(BF16), jnp.swapaxes(wm32, 1, 2).astype(BF16)
    bias = jnp.repeat(sm["gmlp_bs"].T, HEAD, axis=1)
    wax32 = jnp.concatenate([sm["w_a"], sm["w_x"]], axis=2)
    wax, wax_t = wax32.astype(BF16), jnp.swapaxes(wax32, 1, 2).astype(BF16)
    ln_g, ln_b = vec(sm["gmlp_ln_g"]), vec(sm["gmlp_ln_b"])
    post_g_v = vec(sm["post_g"])

    hn = _pre_norm(xs, pre_g)
    cw_shard = _pad_rows(conv_w.reshape(CONV_W, HEAD), ROWS)
    z, w_in_g, (cw_g,) = _in_proj(hn, w_in[0].astype(BF16), [cw_shard])
    cw_full = jnp.transpose(cw_g[:, :CONV_W, :], (1, 0, 2)).reshape(CONV_W, D_BR)
    mixer_consts = dict(cw=_pad_rows(cw_full, ROWS), cb=vec(sm["conv_b"]), ba=vec(sm["b_a"]), bx=vec(sm["b_x"]),
                        lam=vec(sm["lam"]), goa=vec(sm["gmlp_out_g"]), gob=vec(sm["lru_out_g"]))
    (y, h, vhb, xcb, v_rs), (w_out_g, w_pe_g, w_pg_g) = _mix_fwd(
        z, ln_g, ln_b, wm, bias, wax=wax, **mixer_consts,
        ex_arrs=[w_out[0].astype(BF16), w_pe[0].astype(BF16), w_pg[0].astype(BF16)], ex_scatter=[False, False, False])
    w_out_f, w_pg_f = w_out_g.reshape(D_MODEL, D_MODEL), w_pg_g.reshape(D_MODEL, D_MODEL)
    h1, ob = _out_proj(y, xs, w_out_f, post_g_v)
    dh2, dgl, h1b, loss_part, d_w_pe = _ple_loss(h1, ps, tgt, w_pg_f, w_pe_g)

    dh1, do, dy, d_post_g = _tail_bwd(dh2, dgl, ob, w_pg_f, w_out_f, post_g_v)
    d_w_out, _ = _grad_w(y, do, 1024, False, "grad_w_out")
    d_w_pg, _ = _grad_w(h1b, dgl, 1024, False, "grad_w_pg")
    (dz, vecs, d_ws, d_wax, d_bs), (parts_out, parts_pg, parts_pe) = _mix_bwd(
        z, dy, h, vhb, xcb, v_rs, ln_g, ln_b, wm, wm_t, bias, wax=wax, wax_t=wax_t, **mixer_consts,
        ex_arrs=[d_w_out.reshape(N_DEV, shard_rows, D_MODEL), d_w_pg.reshape(N_DEV, shard_rows, D_MODEL), d_w_pe],
        ex_scatter=[True, True, True])

    small = {"gmlp_ln_g": vecs[V_LN_G], "gmlp_ln_b": vecs[V_LN_B], "gmlp_ws": d_ws, "gmlp_bs": d_bs,
             "conv_b": vecs[V_CONV_B], "w_a": d_wax[:, :, :HEAD], "b_a": vecs[V_B_A], "w_x": d_wax[:, :, HEAD:],
             "b_x": vecs[V_B_X], "lam": vecs[V_LAM], "gmlp_out_g": vecs[V_GOUT_A], "lru_out_g": vecs[V_GOUT_B],
             "post_g": d_post_g}
    small_part = _pack([small[n] for n in PACKED] + [loss_part]).reshape(N_DEV, PACK_ROWS // N_DEV, LANES)
    d_cw_blocks = jnp.transpose(vecs[V_CONV_W:V_CONV_W + CONV_W].reshape(CONV_W, N_DEV, HEAD), (1, 0, 2))
    d_cw_blocks = jnp.concatenate([d_cw_blocks, jnp.zeros((N_DEV, ROWS - CONV_W, HEAD), F32)], axis=1)
    parts_in, slots_in, (small_blocks, parts_cw) = _grad_w_in_pairs(
        hn, dz, ex_arrs=[small_part, d_cw_blocks], ex_scatter=[True, True])
    small_sum = _sum_parts(small_blocks, "sum_small")
    grad_x, d_pre_g = _in_bwd(dz, w_in_g, xs, dh1, pre_g)
    pre_rows = D_MODEL // LANES
    small_all, parts_pre = _exchange([small_sum, d_pre_g.reshape(pre_rows, LANES)], False, "gather_small_grads")

    pad_cw = lambda a: _pad_rows(a.reshape(CONV_W, HEAD), ROWS)
    flat = lambda a: a.reshape(pre_rows, LANES)
    outs = {
        "w_in": _adamw_slots(parts_in, slots_in, w_in[0], m_w_in[0], v_w_in[0], "adamw_w_in", 256),
        "w_out": _adamw(parts_out, w_out[0], m_w_out[0], v_w_out[0], "adamw_w_out", 128),
        "w_pe": _adamw(parts_pe, w_pe[0], m_w_pe[0], v_w_pe[0], "adamw_w_pe", 256),
        "w_pg": _adamw(parts_pg, w_pg[0], m_w_pg[0], v_w_pg[0], "adamw_w_pg", 128),
        "conv_w": [a[:CONV_W] for a in
                   _adamw(parts_cw, pad_cw(conv_w), pad_cw(m_conv_w), pad_cw(v_conv_w), "adamw_conv_w", ROWS)],
        "pre_g": _adamw(parts_pre, flat(pre_g), flat(m_pre_g), flat(v_pre_g), "adamw_pre_g", pre_rows),
    }
    as_rows = lambda a: a.reshape(-1, LANES)
    small_res = _adamw_unpacked(small_all.reshape(PACK_ROWS, LANES),
                                [(as_rows(weights[n]), as_rows(m_in[n]), as_rows(v_in[n])) for n in PACKED], "adamw_small")
    for i, n in enumerate(PACKED):
        outs[n] = small_res[4 * i:4 * i + 4]
    loss = small_res[-1][0, 0]

    result = [loss, grad_x[None]]
    for q in range(4):
        result += [outs[n][q].reshape(weights[n].shape) for n in WEIGHTS]
    return tuple(result)
```

```python
import jax
import jax.numpy as jnp
from jax import lax
from jax.experimental import pallas as pl
from jax.experimental.pallas import tpu as pltpu

F32 = jnp.float32
BF16 = jnp.bfloat16
SDS = jax.ShapeDtypeStruct

D_MODEL = 2048
D_BR = 1024
D_IN = 5 * D_BR
D_PLE = 256
N_HEAD = 8
HEAD = 128
CHUNK = 128
ROWS = 8
N_GROUP = CHUNK // ROWS
MIX_SUB = 2
N_DEV = 8
W_IN_SHARD = D_IN // N_DEV
EPS = 1e-6
LRU_C = 8.0
CONV_W = 4
MIB = 1 << 20

ADAM_LR, ADAM_B1, ADAM_B2, ADAM_EPS, ADAM_WD, ADAM_STEP = 0.001, 0.9, 0.999, 1e-08, 0.01, 10

_GELU_C = 0.7978845608028654
_GELU_A = 0.044715

V_LN_G, V_LN_B, V_CONV_B, V_B_A, V_B_X, V_LAM, V_GOUT_A, V_GOUT_B, V_CONV_W = 0, 1, 2, 3, 4, 5, 6, 7, 8
N_VEC = 16


def _params(sem, vmem_mib):
    return pltpu.CompilerParams(dimension_semantics=sem, vmem_limit_bytes=int(vmem_mib * MIB))


def _sig(x):
    return 0.5 * jnp.tanh(0.5 * x) + 0.5


def _gelu(x, with_grad=False):
    sq = x * x
    t = jnp.tanh(x * (_GELU_C + (_GELU_C * _GELU_A) * sq))
    half, one_t = 0.5 * x, 1.0 + t
    if not with_grad:
        return half * one_t
    grad = 0.5 * one_t + half * ((1.0 - t) * one_t) * (_GELU_C + (3.0 * _GELU_C * _GELU_A) * sq)
    return half * one_t, grad


def _silu_grad(s, xs):
    return s + xs * (1.0 - s)


def _neg_expm1(y, exp_y):
    series = -y * (1.0 + y * (0.5 + y * (1.0 / 6.0)))
    return jnp.where(y > -0.01, series, 1.0 - exp_y)


def _softplus(x):
    return jnp.maximum(x, 0.0) + jnp.log(1.0 + jnp.exp(-jnp.abs(x)))


def _row_ids(width):
    return lax.broadcasted_iota(jnp.int32, (ROWS, width), 0)


def _shift_down(cur, prev, k, rid):
    return jnp.where(rid >= k, pltpu.roll(cur, k, 0), pltpu.roll(prev, k, 0))


def _shift_up(cur, nxt, k, rid):
    return jnp.where(rid < ROWS - k, pltpu.roll(cur, ROWS - k, 0), pltpu.roll(nxt, ROWS - k, 0))


def _mean_last(x):
    return jnp.mean(x, axis=-1, keepdims=True)


def _rows(g):
    return pl.ds(pl.multiple_of(g * ROWS, ROWS), ROWS)


TILE_ROWS = 16


def _tile_rows(q):
    return pl.ds(pl.multiple_of(q * TILE_ROWS, TILE_ROWS), TILE_ROWS)


UNROLL = 4
TILE_UNROLL = 8


def _loop(n, body, init, unroll=UNROLL):
    def wide(i, carry):
        for u in range(unroll):
            carry = body(i * unroll + u, carry)
        return carry

    return lax.fori_loop(0, n // unroll, wide, init)


def _fold_rows(x):
    return x[0:ROWS, :] + x[ROWS:TILE_ROWS, :]


def _bcast_row(x, r):
    return jnp.broadcast_to(x[r:r + 1, :], x.shape)


def _dot(a, b):
    return jnp.dot(a, b, preferred_element_type=F32)


def _dot_nt(a, b):
    return lax.dot_general(a, b, (((1,), (1,)), ((), ())), preferred_element_type=F32)


def _dot_tn(a, b):
    return lax.dot_general(a, b, (((0,), (0,)), ((), ())), preferred_element_type=F32)


def _mesh_place():
    x, y, c = lax.axis_index("x"), lax.axis_index("y"), lax.axis_index("c")
    return x, y, c, 4 * x + 2 * y + c


def _peer(x, y, c, k):
    px = 1 - x if k & 4 else x
    py = 1 - y if k & 2 else y
    pc = 1 - c if k & 1 else c
    return (px, py, pc), 4 * px + 2 * py + pc


def _remote(src, dst, send_sem, recv_sem, dev):
    return pltpu.make_async_remote_copy(src_ref=src, dst_ref=dst, send_sem=send_sem, recv_sem=recv_sem, device_id=dev,
                                        device_id_type=pl.DeviceIdType.MESH)


ANY_SPEC = pl.BlockSpec(memory_space=pl.ANY)


class _Exchange:
    def __init__(self, arrs, scatter):
        self.n = len(arrs)
        self.scatter = tuple(scatter)
        self.out_shape = [SDS(a.shape if s else (N_DEV,) + a.shape, a.dtype) for a, s in zip(arrs, scatter)]
        self.scratch = [pltpu.SemaphoreType.DMA((self.n * N_DEV,)), pltpu.SemaphoreType.DMA((self.n * N_DEV,)),
                        pltpu.SemaphoreType.DMA((self.n,))]

    def _copies(self, ins, outs, sems):
        send_sems, recv_sems, local_sems = sems
        x, y, c, me = _mesh_place()
        local, sends, recvs = [], [], []
        for a in range(self.n):
            src = ins[a].at[me] if self.scatter[a] else ins[a]
            local.append(pltpu.make_async_copy(src, outs[a].at[me], local_sems.at[a]))
        for k in range(1, N_DEV):
            dev, lin = _peer(x, y, c, k)
            for a in range(self.n):
                src = ins[a].at[lin] if self.scatter[a] else ins[a]
                pair = (send_sems.at[a * N_DEV + k], recv_sems.at[a * N_DEV + k], dev)
                sends.append(_remote(src, outs[a].at[me], *pair))
                recvs.append(_remote(src, outs[a].at[lin], *pair))
        return local, sends, recvs

    def start(self, ins, outs, sems):
        local, sends, _ = self._copies(ins, outs, sems)
        for cp in local + sends:
            cp.start()

    def wait(self, ins, outs, sems):
        local, sends, recvs = self._copies(ins, outs, sems)
        for cp in recvs:
            cp.wait_recv()
        for cp in sends:
            cp.wait_send()
        for cp in local:
            cp.wait()


def _exchange(arrs, scatter, name):
    ex = _Exchange(arrs, [scatter] * len(arrs))
    n = ex.n

    def body(*refs):
        ins, outs, sems = refs[:n], refs[n:2 * n], refs[2 * n:]
        ex.start(ins, outs, sems)
        ex.wait(ins, outs, sems)

    return pl.pallas_call(
        body, name=name, out_shape=ex.out_shape, in_specs=[ANY_SPEC] * n, out_specs=[ANY_SPEC] * n,
        scratch_shapes=ex.scratch,
    )(*arrs)


def _pre_norm(x, pre_g, tm=512):
    t_len = x.shape[0]

    def body(x_ref, g_ref, hn_ref):
        g = g_ref[...]

        def rows_body(q, _):
            rows = _tile_rows(q)
            xv = x_ref[rows, :]
            hn_ref[rows, :] = (xv * lax.rsqrt(_mean_last(xv * xv) + EPS) * g).astype(BF16)
            return 0

        _loop(tm // TILE_ROWS, rows_body, 0, unroll=TILE_UNROLL)

    tile = pl.BlockSpec((tm, D_MODEL), lambda i: (i, 0))
    return pl.pallas_call(
        body, name="pre_norm", grid=(t_len // tm,),
        in_specs=[tile, pl.BlockSpec((1, D_MODEL), lambda i: (0, 0))], out_specs=tile,
        out_shape=SDS((t_len, D_MODEL), BF16),
        compiler_params=_params(("arbitrary",), 24),
    )(x, pre_g)


CHIP_ORDER = (0, 2, 4, 6)
W_BODY, W_TAIL = 512, 128
SIBLING = 1
ICI_MASKS = (2, 4, 6)
DIRECT_MASKS = (SIBLING,) + ICI_MASKS
Y_NEIGHBOUR, X_NEIGHBOUR, DIAGONAL = 2, 4, 6
W_DIRECT = (SIBLING, Y_NEIGHBOUR, X_NEIGHBOUR)


def _in_proj(hn, w_shard, others, tm=1024):
    t_len = hn.shape[0]
    n_i = t_len // tm
    n_o = len(others)
    me_out = 4 * lax.axis_index("x") + 2 * lax.axis_index("y") + lax.axis_index("c")
    order = jnp.stack([(me_out ^ chip) // 2 for chip in CHIP_ORDER]).astype(jnp.int32)

    def body(order_ref, hn_ref, w_hbm, *refs):
        o_in = refs[:n_o]
        z_ref, wg_hbm = refs[n_o], refs[n_o + 1]
        o_out = refs[n_o + 2:2 * n_o + 2]
        (wbuf, tail_s, send_w, recv_w, fsend_w, frecv_w, send_o, recv_o, fsend_o, frecv_o, wb_sems, loc_sems, rsend,
         rrecv) = refs[2 * n_o + 2:]
        j, i = pl.program_id(0), pl.program_id(1)
        x, y, c, me = _mesh_place()
        sib = _peer(x, y, c, SIBLING)[0]

        def relay(core):
            src, dst = (Y_NEIGHBOUR, X_NEIGHBOUR) if core == 0 else (X_NEIGHBOUR, Y_NEIGHBOUR)
            held, diag = _peer(x, y, c, src)[1], _peer(x, y, c, DIAGONAL)[1]
            pair = (rsend.at[0], rrecv.at[0], _peer(x, y, c, dst)[0])
            return _remote(wbuf.at[held], wbuf.at[held], *pair), _remote(wbuf.at[diag], wbuf.at[diag], *pair)

        def direct(k, a=None):
            dev, lin = _peer(x, y, c, k)
            if a is None:
                return (_remote(w_hbm, wbuf.at[me], send_w.at[k], recv_w.at[k], dev),
                        _remote(w_hbm, wbuf.at[lin], send_w.at[k], recv_w.at[k], dev))
            pair = (send_o.at[a * N_DEV + k], recv_o.at[a * N_DEV + k], dev)
            return _remote(o_in[a], o_out[a].at[me], *pair), _remote(o_in[a], o_out[a].at[lin], *pair)

        def passed(k, a=None):
            mine, theirs = _peer(x, y, c, k)[1], _peer(x, y, c, k ^ SIBLING)[1]
            if a is None:
                pair = (fsend_w.at[k], frecv_w.at[k], sib)
                return _remote(wbuf.at[mine], wbuf.at[mine], *pair), _remote(wbuf.at[theirs], wbuf.at[theirs], *pair)
            pair = (fsend_o.at[a * N_DEV + k], frecv_o.at[a * N_DEV + k], sib)
            return (_remote(o_out[a].at[mine], o_out[a].at[mine], *pair),
                    _remote(o_out[a].at[theirs], o_out[a].at[theirs], *pair))

        def own_copies():
            return [pltpu.make_async_copy(o_in[a], o_out[a].at[me], loc_sems.at[1 + a]) for a in range(n_o)]

        @pl.when(jnp.logical_and(j == 0, i == 0))
        def _():
            own = pltpu.make_async_copy(w_hbm, wbuf.at[me], loc_sems.at[0])
            own.start()
            for cp in own_copies():
                cp.start()
            for k in W_DIRECT:
                direct(k)[0].start()
            for k in DIRECT_MASKS:
                for a in range(n_o):
                    direct(k, a)[0].start()
            own.wait()

        low = 2 * order_ref[j]

        for jp, chip in enumerate(CHIP_ORDER):
            @pl.when(jnp.logical_and(j == jp, i == 0))
            def _(jp=jp, chip=chip):
                if chip == 0:
                    direct(SIBLING)[1].wait_recv()
                elif chip == Y_NEIGHBOUR:
                    for mask in (Y_NEIGHBOUR, X_NEIGHBOUR):
                        direct(mask)[1].wait_recv()
                        passed(mask)[0].start()
                    for core in (0, 1):
                        @pl.when(c == core)
                        def _(core=core):
                            relay(core)[0].start()
                    passed(Y_NEIGHBOUR)[1].wait_recv()
                elif chip == X_NEIGHBOUR:
                    passed(X_NEIGHBOUR)[1].wait_recv()
                    for core in (0, 1):
                        @pl.when(c == core)
                        def _(core=core):
                            relay(core)[1].wait_recv()
                    passed(DIAGONAL)[0].start()
                    for k in ICI_MASKS:
                        for a in range(n_o):
                            direct(k, a)[1].wait_recv()
                            passed(k, a)[0].start()
                else:
                    passed(DIAGONAL)[1].wait_recv()
                for half in (0, 1):
                    pltpu.make_async_copy(wbuf.at[low + half], wg_hbm.at[low + half], wb_sems.at[2 * jp + half]).start()
                tail_s[:, 0:W_TAIL] = wbuf[low, :, W_BODY:W_IN_SHARD]
                tail_s[:, W_TAIL:2 * W_TAIL] = wbuf[low + 1, :, W_BODY:W_IN_SHARD]

        hn = hn_ref[...]
        z_ref[:, 0:W_BODY] = _dot(hn, wbuf[low, :, 0:W_BODY])
        z_ref[:, W_IN_SHARD:W_IN_SHARD + W_BODY] = _dot(hn, wbuf[low + 1, :, 0:W_BODY])
        tails = _dot(hn, tail_s[...])
        z_ref[:, W_BODY:W_IN_SHARD] = tails[:, 0:W_TAIL]
        z_ref[:, W_IN_SHARD + W_BODY:2 * W_IN_SHARD] = tails[:, W_TAIL:2 * W_TAIL]

        @pl.when(jnp.logical_and(j == len(CHIP_ORDER) - 1, i == n_i - 1))
        def _():
            for a in range(n_o):
                direct(SIBLING, a)[1].wait_recv()
            for k in ICI_MASKS:
                for a in range(n_o):
                    passed(k, a)[1].wait_recv()
            for k in W_DIRECT:
                direct(k)[0].wait_send()
            for core in (0, 1):
                @pl.when(c == core)
                def _(core=core):
                    relay(core)[0].wait_send()
            for k in DIRECT_MASKS:
                for a in range(n_o):
                    direct(k, a)[0].wait_send()
            for k in ICI_MASKS:
                passed(k)[0].wait_send()
                for a in range(n_o):
                    passed(k, a)[0].wait_send()
            for cp in own_copies():
                cp.wait()
            for jj in range(N_DEV):
                pltpu.make_async_copy(wbuf.at[0], wg_hbm.at[0], wb_sems.at[jj]).wait()

    dma = lambda n: pltpu.SemaphoreType.DMA((n,))
    grid_spec = pltpu.PrefetchScalarGridSpec(
        num_scalar_prefetch=1, grid=(len(CHIP_ORDER), n_i),
        in_specs=[pl.BlockSpec((tm, D_MODEL), lambda j, i, order: (i, 0)), ANY_SPEC] + [ANY_SPEC] * n_o,
        out_specs=[pl.BlockSpec((tm, 2 * W_IN_SHARD), lambda j, i, order: (i, order[j])), ANY_SPEC] + [ANY_SPEC] * n_o,
        scratch_shapes=[pltpu.VMEM((N_DEV, D_MODEL, W_IN_SHARD), BF16), pltpu.VMEM((D_MODEL, 2 * W_TAIL), BF16),
                        dma(N_DEV), dma(N_DEV), dma(N_DEV), dma(N_DEV),
                        dma(n_o * N_DEV), dma(n_o * N_DEV), dma(n_o * N_DEV), dma(n_o * N_DEV), dma(N_DEV), dma(1 + n_o),
                        dma(1), dma(1)])
    res = pl.pallas_call(
        body, name="in_proj", grid_spec=grid_spec,
        out_shape=[SDS((t_len, D_IN), F32), SDS((N_DEV, D_MODEL, W_IN_SHARD), BF16)]
        + [SDS((N_DEV,) + o.shape, o.dtype) for o in others],
        compiler_params=_params(("arbitrary", "arbitrary"), 54),
    )(order, hn, w_shard, *others)
    return res[0], res[1], res[2:]


def _conv_rows(cur, prev, cw_ref, cb, rid):
    acc = cw_ref[3:4, :] * cur + cb
    for k in range(1, CONV_W):
        acc = acc + cw_ref[3 - k:4 - k, :] * _shift_down(cur, prev, k, rid)
    return acc


ROW0_LOG_A = -1e30


def _row0_mask(rid):
    return jnp.where(rid == 0, ROW0_LOG_A, 0.0)


def _row0_bias(is_first_group, row0_mask):
    return is_first_group.astype(F32) * row0_mask


def _lru_gates(pa, px, ba, bx, sp8, row0_bias):
    r = _sig(pa + ba)
    i = _sig(px + bx)
    la = row0_bias - r * sp8
    a = jnp.exp(la)
    return r, i, a, _neg_expm1(2.0 * la, a * a)


def _mix_fwd(z, ln_g, ln_b, wm, bias, cw, cb, wax, ba, bx, lam, goa, gob, ex_arrs, ex_scatter):
    t_len = z.shape[0]
    n_chunk = t_len // CHUNK
    ex = _Exchange(ex_arrs, ex_scatter)
    n_in, n_out, n_scratch = 13, 5, 7

    def body(*refs):
        (z_ref, lng_ref, lnb_ref, wm_ref, bias_ref, cw_ref, cb_ref, wax_ref, ba_ref, bx_ref, lam_ref, goa_ref,
         gob_ref) = refs[:n_in]
        ex_in = refs[n_in:n_in + ex.n]
        y_ref, h_ref, vhb_ref, xcb_ref, rs_ref = refs[n_in + ex.n:n_in + ex.n + n_out]
        ex_out = refs[n_in + ex.n + n_out:n_in + 2 * ex.n + n_out]
        vn_s, xc_s, mixed_s, pre_s, y_s, carry_s, halo_s = refs[n_in + 2 * ex.n + n_out:n_in + 2 * ex.n + n_out + n_scratch]
        ex_sems = refs[n_in + 2 * ex.n + n_out + n_scratch:]
        step = pl.program_id(0)
        rid = _row_ids(D_BR)

        @pl.when(step == 0)
        def _():
            ex.start(ex_in, ex_out, ex_sems)
            carry_s[...] = jnp.zeros_like(carry_s)
            halo_s[...] = jnp.zeros_like(halo_s)

        lng, lnb, cb = lng_ref[...], lnb_ref[...], cb_ref[...]
        ba, bx, goa, gob = ba_ref[...], bx_ref[...], goa_ref[...], gob_ref[...]
        sp8 = LRU_C * _softplus(-lam_ref[...])
        row0 = _row0_mask(rid)

        def chunk(c_id, z_ref, y_ref, h_ref, vhb_ref, xcb_ref, rs_ref):
            def phase1(g, prev):
                rows = _rows(g)
                vg = _gelu(z_ref[rows, D_BR:2 * D_BR])
                xm = vg - _mean_last(vg)
                rs = lax.rsqrt(_mean_last(xm * xm) + EPS)
                vn_s[rows, :] = xm * rs
                rs_ref[rows, :] = jnp.broadcast_to(rs, (ROWS, HEAD))
                xb = z_ref[rows, 3 * D_BR:4 * D_BR]
                xc_s[rows, :] = _conv_rows(xb, prev, cw_ref, cb, rid)
                return xb

            halo_s[...] = _loop(N_GROUP, phase1, halo_s[...], unroll=8)
            vhb_ref[...] = vn_s[...].astype(BF16)
            xcb_ref[...] = xc_s[...].astype(BF16)

            for h in range(N_HEAD):
                cs = slice(h * HEAD, (h + 1) * HEAD)
                mixed_s[:, cs] = _dot(wm_ref[h], (vn_s[:, cs] * lng[:, cs] + lnb[:, cs]).astype(BF16))
                pre = _dot(xcb_ref[:, cs], wax_ref[h])
                pre_s[:, cs] = pre[:, :HEAD]
                pre_s[:, D_BR + h * HEAD:D_BR + (h + 1) * HEAD] = pre[:, HEAD:]

            def phase3(g, carry):
                rows = _rows(g)
                ug = _gelu(z_ref[rows, 0:D_BR])
                ga = z_ref[rows, 2 * D_BR:3 * D_BR]
                ya = ug * (mixed_s[rows, :] + bias_ref[rows, :]) * (ga * _sig(ga))
                y_s[rows, 0:D_BR] = ya * lax.rsqrt(_mean_last(ya * ya) + EPS) * goa

                bias0 = _row0_bias(jnp.logical_and(c_id == 0, g == 0), row0)
                _, i, a, m2 = _lru_gates(pre_s[rows, 0:D_BR], pre_s[rows, D_BR:2 * D_BR], ba, bx, sp8, bias0)
                b = jnp.sqrt(m2) * i * xc_s[rows, :]
                for d in (1, 2, 4):
                    a_sh = jnp.where(rid >= d, pltpu.roll(a, d, 0), 1.0)
                    b_sh = jnp.where(rid >= d, pltpu.roll(b, d, 0), 0.0)
                    b = a * b_sh + b
                    a = a * a_sh
                hh = b + a * carry
                h_ref[rows, :] = hh
                gb = z_ref[rows, 4 * D_BR:5 * D_BR]
                yb = hh * (gb * _sig(gb))
                y_s[rows, D_BR:2 * D_BR] = yb * lax.rsqrt(_mean_last(yb * yb) + EPS) * gob
                return _bcast_row(hh, ROWS - 1)

            carry_s[...] = _loop(N_GROUP, phase3, carry_s[...])
            y_ref[...] = y_s[...].astype(BF16)

        for sub in range(MIX_SUB):
            part = lambda ref, sub=sub: ref.at[pl.ds(sub * CHUNK, CHUNK)]
            chunk(step * MIX_SUB + sub, part(z_ref), part(y_ref), part(h_ref), part(vhb_ref), part(xcb_ref),
                  part(rs_ref))

        @pl.when(step == n_chunk // MIX_SUB - 1)
        def _():
            ex.wait(ex_in, ex_out, ex_sems)

    vec = pl.BlockSpec((1, D_BR), lambda i: (0, 0))
    blk = MIX_SUB * CHUNK
    res = pl.pallas_call(
        body, name="mix_fwd", grid=(n_chunk // MIX_SUB,),
        in_specs=[pl.BlockSpec((blk, D_IN), lambda i: (i, 0)), vec, vec,
                  pl.BlockSpec((N_HEAD, HEAD, HEAD), lambda i: (0, 0, 0)),
                  pl.BlockSpec((CHUNK, D_BR), lambda i: (0, 0)),
                  pl.BlockSpec((ROWS, D_BR), lambda i: (0, 0)), vec,
                  pl.BlockSpec((N_HEAD, HEAD, 2 * HEAD), lambda i: (0, 0, 0)), vec, vec, vec, vec, vec]
        + [ANY_SPEC] * ex.n,
        out_specs=[pl.BlockSpec((blk, 2 * D_BR), lambda i: (i, 0)), pl.BlockSpec((blk, D_BR), lambda i: (i, 0)),
                   pl.BlockSpec((blk, D_BR), lambda i: (i, 0)), pl.BlockSpec((blk, D_BR), lambda i: (i, 0)),
                   pl.BlockSpec((blk, HEAD), lambda i: (i, 0))] + [ANY_SPEC] * ex.n,
        out_shape=[SDS((t_len, 2 * D_BR), BF16), SDS((t_len, D_BR), F32), SDS((t_len, D_BR), BF16),
                   SDS((t_len, D_BR), BF16), SDS((t_len, HEAD), F32)] + ex.out_shape,
        scratch_shapes=[pltpu.VMEM((CHUNK, D_BR), F32), pltpu.VMEM((CHUNK, D_BR), F32), pltpu.VMEM((CHUNK, D_BR), F32),
                        pltpu.VMEM((CHUNK, 2 * D_BR), F32), pltpu.VMEM((CHUNK, 2 * D_BR), F32),
                        pltpu.VMEM((ROWS, D_BR), F32), pltpu.VMEM((ROWS, D_BR), F32)] + ex.scratch,
        compiler_params=_params(("arbitrary",), 32),
    )(z, ln_g, ln_b, wm, bias, cw, cb, wax, ba, bx, lam, goa, gob, *ex_arrs)
    return res[:n_out], res[n_out:]


def _load_weight(w_hbm, w_vmem, sem):
    @pl.when(pl.program_id(0) == 0)
    def _():
        cp = pltpu.make_async_copy(w_hbm, w_vmem, sem)
        cp.start()
        cp.wait()


def _out_proj(y, x, w_out, post_g, tm=512):
    t_len = y.shape[0]

    def body(y_ref, x_ref, w_hbm, g_ref, h1_ref, ob_ref, w_s, o_s, sem):
        _load_weight(w_hbm, w_s, sem)
        g = g_ref[...]
        blk = D_MODEL // N_DEV
        row_groups = [slice(q * TILE_ROWS, (q + 1) * TILE_ROWS) for q in range(tm // TILE_ROWS)]
        part = [jnp.zeros((TILE_ROWS, LANES), F32) for _ in row_groups]
        for j in range(N_DEV):
            cols = slice(j * blk, (j + 1) * blk)
            o_s[:, cols] = _dot(y_ref[...], w_s[:, cols])
            for q, rows in enumerate(row_groups):
                o = o_s[rows, cols]
                ob_ref[rows, cols] = o.astype(BF16)
                sq = o * o
                for k in range(blk // LANES):
                    part[q] = part[q] + sq[:, k * LANES:(k + 1) * LANES]
        for q, rows in enumerate(row_groups):
            ms = jnp.sum(part[q], axis=-1, keepdims=True) * (1.0 / D_MODEL)
            h1_ref[rows, :] = x_ref[rows, :] + o_s[rows, :] * lax.rsqrt(ms + EPS) * g

    tile = pl.BlockSpec((tm, D_MODEL), lambda i: (i, 0))
    return pl.pallas_call(
        body, name="out_proj", grid=(t_len // tm,),
        in_specs=[tile, tile, pl.BlockSpec(memory_space=pl.ANY), pl.BlockSpec((1, D_MODEL), lambda i: (0, 0))],
        out_specs=[tile, tile],
        out_shape=[SDS((t_len, D_MODEL), F32), SDS((t_len, D_MODEL), BF16)],
        scratch_shapes=[pltpu.VMEM((D_MODEL, D_MODEL), BF16), pltpu.VMEM((tm, D_MODEL), F32), pltpu.SemaphoreType.DMA],
        compiler_params=_params(("arbitrary",), 44),
    )(y, x, w_out, post_g)


def _ple_loss(h1, p, tgt, w_pg, w_pe_g, tm=256):
    t_len = h1.shape[0]
    n_tile = t_len // tm
    pe_shard = D_MODEL // N_DEV

    def body(h1_ref, p_ref, t_ref, w_hbm, wpe_ref, dh2_ref, dgl_ref, h1b_ref, loss_ref, dwpe_ref, w_s, pe_s, gl_s, acc_s,
             dpe_s, gpe_s, sem):
        _load_weight(w_hbm, w_s, sem)
        i = pl.program_id(0)

        @pl.when(i == 0)
        def _():
            acc_s[...] = jnp.zeros_like(acc_s)
            gpe_s[...] = jnp.zeros_like(gpe_s)

        h1b_ref[...] = h1_ref[...].astype(BF16)
        pb = p_ref[...].astype(BF16)
        for j in range(N_DEV):
            cols = slice(j * pe_shard, (j + 1) * pe_shard)
            pe_s[:, cols] = _dot(pb, wpe_ref[j])
            gl_s[:, cols] = _dot(h1b_ref[...], w_s[:, cols])
            acc = acc_s[:, cols]
            for q in range(tm // TILE_ROWS):
                rows = slice(q * TILE_ROWS, (q + 1) * TILE_ROWS)
                pe = pe_s[rows, cols]
                g = _sig(gl_s[rows, cols])
                e = h1_ref[rows, cols] + pe * g - t_ref[rows, cols]
                dh2 = e * (1.0 / D_MODEL)
                dh2_ref[rows, cols] = dh2
                dpe_s[rows, cols] = (dh2 * g).astype(BF16)
                dgl_ref[rows, cols] = (dh2 * pe * g * (1.0 - g)).astype(BF16)
                acc = acc + _fold_rows(e * e)
            acc_s[:, cols] = acc
        gpe_s[...] += _dot_tn(pb, dpe_s[...])

        @pl.when(i == n_tile - 1)
        def _():
            loss_ref[...] = jnp.full(loss_ref.shape, 0.5 / D_MODEL * jnp.sum(acc_s[...]), F32)
            for j in range(N_DEV):
                dwpe_ref[j] = gpe_s[:, j * pe_shard:(j + 1) * pe_shard].astype(BF16)

    tile = pl.BlockSpec((tm, D_MODEL), lambda i: (i, 0))
    pe_blocks = pl.BlockSpec((N_DEV, D_PLE, pe_shard), lambda i: (0, 0, 0))
    return pl.pallas_call(
        body, name="ple_loss", grid=(n_tile,),
        in_specs=[tile, pl.BlockSpec((tm, D_PLE), lambda i: (i, 0)), tile, pl.BlockSpec(memory_space=pl.ANY), pe_blocks],
        out_specs=[tile, tile, tile, pl.BlockSpec((ROWS, HEAD), lambda i: (0, 0)), pe_blocks],
        out_shape=[SDS((t_len, D_MODEL), F32), SDS((t_len, D_MODEL), BF16), SDS((t_len, D_MODEL), BF16),
                   SDS((ROWS, HEAD), F32), SDS((N_DEV, D_PLE, pe_shard), BF16)],
        scratch_shapes=[pltpu.VMEM((D_MODEL, D_MODEL), BF16), pltpu.VMEM((tm, D_MODEL), F32),
                        pltpu.VMEM((tm, D_MODEL), F32), pltpu.VMEM((ROWS, D_MODEL), F32), pltpu.VMEM((tm, D_MODEL), BF16),
                        pltpu.VMEM((D_PLE, D_MODEL), F32), pltpu.SemaphoreType.DMA],
        compiler_params=_params(("arbitrary",), 48),
    )(h1, p, tgt, w_pg, w_pe_g)


def _tail_bwd(dh2, dgl, ob, w_pg, w_out, post_g, tm=256):
    t_len = dh2.shape[0]
    n_tile = t_len // tm

    def body(dh2_ref, dgl_ref, ob_ref, wpg_hbm, wout_hbm, g_ref, dh1_ref, do_ref, dy_ref, dg_ref, wpg_s, wout_s, t_s,
             acc_s, sems):
        i = pl.program_id(0)
        load_wpg = pltpu.make_async_copy(wpg_hbm, wpg_s, sems.at[0])
        load_wout = pltpu.make_async_copy(wout_hbm, wout_s, sems.at[1])

        @pl.when(i == 0)
        def _():
            load_wpg.start()
            load_wout.start()
            acc_s[...] = jnp.zeros_like(acc_s)
            load_wpg.wait()

        g = g_ref[...]
        blk = D_MODEL // N_DEV
        row_groups = [slice(q * TILE_ROWS, (q + 1) * TILE_ROWS) for q in range(tm // TILE_ROWS)]
        rr = []
        for rows in row_groups:
            o = ob_ref[rows, :].astype(F32)
            rr.append(lax.rsqrt(_mean_last(o * o) + EPS))
        part = [jnp.zeros((TILE_ROWS, LANES), F32) for _ in row_groups]
        for j in range(N_DEV):
            cols = slice(j * blk, (j + 1) * blk)
            t_s[:, cols] = _dot_nt(dgl_ref[...], wpg_s[cols, :])
            acc = acc_s[:, cols]
            for q, rows in enumerate(row_groups):
                dh1 = dh2_ref[rows, cols] + t_s[rows, cols]
                dh1_ref[rows, cols] = dh1
                on = ob_ref[rows, cols].astype(F32) * rr[q]
                pr = dh1 * g[:, cols] * on
                for k in range(blk // LANES):
                    part[q] = part[q] + pr[:, k * LANES:(k + 1) * LANES]
                acc = acc + _fold_rows(dh1 * on)
            acc_s[:, cols] = acc
        for q, rows in enumerate(row_groups):
            m = jnp.sum(part[q], axis=-1, keepdims=True) * (1.0 / D_MODEL)
            on = ob_ref[rows, :].astype(F32) * rr[q]
            do_ref[rows, :] = (rr[q] * (dh1_ref[rows, :] * g - on * m)).astype(BF16)

        @pl.when(i == 0)
        def _():
            load_wout.wait()

        dy_ref[...] = _dot_nt(do_ref[...], wout_s[...]).astype(BF16)

        @pl.when(i == n_tile - 1)
        def _():
            dg_ref[...] = jnp.sum(acc_s[...], axis=0, keepdims=True)

    tile = pl.BlockSpec((tm, D_MODEL), lambda i: (i, 0))
    vec = pl.BlockSpec((1, D_MODEL), lambda i: (0, 0))
    hbm = pl.BlockSpec(memory_space=pl.ANY)
    return pl.pallas_call(
        body, name="tail_bwd", grid=(n_tile,),
        in_specs=[tile, tile, tile, hbm, hbm, vec],
        out_specs=[tile, tile, tile, vec],
        out_shape=[SDS((t_len, D_MODEL), F32), SDS((t_len, D_MODEL), BF16), SDS((t_len, D_MODEL), BF16),
                   SDS((1, D_MODEL), F32)],
        scratch_shapes=[pltpu.VMEM((D_MODEL, D_MODEL), BF16), pltpu.VMEM((D_MODEL, D_MODEL), BF16),
                        pltpu.VMEM((tm, D_MODEL), F32), pltpu.VMEM((ROWS, D_MODEL), F32), pltpu.SemaphoreType.DMA((2,))],
        compiler_params=_params(("arbitrary",), 48),
    )(dh2, dgl, ob, w_pg, w_out, post_g)


def _mix_bwd(z, dy, h, vhb, xcb, rs, ln_g, ln_b, wm, wm_t, bias, cw, cb, wax, wax_t, ba, bx, lam, goa, gob, ex_arrs,
             ex_scatter):
    t_len = z.shape[0]
    n_chunk = t_len // CHUNK
    halo_blocks = CHUNK // ROWS
    ex = _Exchange(ex_arrs, ex_scatter)
    n_in, n_out, n_scratch = 21, 5, 16

    blocked = (0, 1, 2, 4, 5, 6, n_in + ex.n)

    def body(*refs):
        step = pl.program_id(0)
        for sub in reversed(range(MIX_SUB)):
            views = list(refs)
            for idx in blocked:
                views[idx] = refs[idx].at[pl.ds(sub * CHUNK, CHUNK)]
            h_before = refs[2].at[pl.ds(sub * CHUNK - ROWS, ROWS)] if sub else refs[3]
            chunk((n_chunk // MIX_SUB - 1 - step) * MIX_SUB + sub,
                  step == 0 if sub == MIX_SUB - 1 else None,
                  step == n_chunk // MIX_SUB - 1 if sub == 0 else None, h_before, *views)

    def chunk(c_id, first, last, h_before, *refs):
        (z_ref, dy_ref, h_ref, hhalo_ref, vhb_ref, xcb_ref, rs_ref, lng_ref, lnb_ref, wm_ref, wmt_ref, bias_ref, cw_ref,
         cb_ref, wax_ref, waxt_ref, ba_ref, bx_ref, lam_ref, goa_ref, gob_ref) = refs[:n_in]
        ex_in = refs[n_in:n_in + ex.n]
        dz_ref, vecs_ref, dws_ref, dwax_ref, dbs_ref = refs[n_in + ex.n:n_in + ex.n + n_out]
        ex_out = refs[n_in + ex.n + n_out:n_in + 2 * ex.n + n_out]
        (vnb_s, vh_s, xc_s, mixed_s, pre_s, dmix_s, dvn_s, dho_s, dxc_s, dpre_s, dz_s, acc_s, accdm_s,
         cg_s, ca_s, dxchalo_s) = refs[n_in + 2 * ex.n + n_out:n_in + 2 * ex.n + n_out + n_scratch]
        ex_sems = refs[n_in + 2 * ex.n + n_out + n_scratch:]
        rid = _row_ids(D_BR)
        first_chunk = c_id == 0

        if first is not None:
            @pl.when(first)
            def _():
                ex.start(ex_in, ex_out, ex_sems)
                acc_s[...] = jnp.zeros_like(acc_s)
                accdm_s[...] = jnp.zeros_like(accdm_s)
                cg_s[...] = jnp.zeros_like(cg_s)
                ca_s[...] = jnp.zeros_like(ca_s)
                dxchalo_s[...] = jnp.zeros_like(dxchalo_s)
                dws_ref[...] = jnp.zeros_like(dws_ref)
                dwax_ref[...] = jnp.zeros_like(dwax_ref)

        lng, lnb = lng_ref[...], lnb_ref[...]
        h_halo = jnp.where(first_chunk, 0.0, h_before[...])

        def prev_rows(ref, cols, g, halo):
            before = ref[pl.ds(pl.multiple_of(jnp.maximum(g - 1, 0) * ROWS, ROWS), ROWS), cols]
            return jnp.where(g > 0, before, halo)

        vh_s[...] = vhb_ref[...].astype(F32)
        xc_s[...] = xcb_ref[...].astype(F32)

        for hd in range(N_HEAD):
            cs = slice(hd * HEAD, (hd + 1) * HEAD)
            vnb_s[:, cs] = (vh_s[:, cs] * lng[:, cs] + lnb[:, cs]).astype(BF16)
            mixed_s[:, cs] = _dot(wm_ref[hd], vnb_s[:, cs])
            pre = _dot(xcb_ref[:, cs], wax_ref[hd])
            pre_s[:, cs] = pre[:, :HEAD]
            pre_s[:, D_BR + hd * HEAD:D_BR + (hd + 1) * HEAD] = pre[:, HEAD:]

        goa, gob = goa_ref[...], gob_ref[...]

        def phase3(g, _):
            rows = _rows(g)
            ug, dug = _gelu(z_ref[rows, 0:D_BR], with_grad=True)
            ga = z_ref[rows, 2 * D_BR:3 * D_BR]
            sga = _sig(ga)
            sa = ga * sga
            mixed = mixed_s[rows, :] + bias_ref[rows, :]
            ya0 = ug * mixed
            ya = ya0 * sa
            ra = lax.rsqrt(_mean_last(ya * ya) + EPS)
            dyan = dy_ref[rows, 0:D_BR].astype(F32)
            acc_s[V_GOUT_A] += dyan * ya * ra
            dyg = dyan * goa
            dya = ra * dyg - ya * (ra * ra * ra) * _mean_last(dyg * ya)
            dya0 = dya * sa
            dz_s[rows, 2 * D_BR:3 * D_BR] = dya * ya0 * _silu_grad(sga, sa)
            dmix = dya0 * ug
            dmix_s[rows, :] = dmix
            accdm_s[rows, :] += dmix
            dz_s[rows, 0:D_BR] = dya0 * mixed * dug

            hh = h_ref[rows, :]
            gb = z_ref[rows, 4 * D_BR:5 * D_BR]
            sgb = _sig(gb)
            sb = gb * sgb
            yb = hh * sb
            rb = lax.rsqrt(_mean_last(yb * yb) + EPS)
            dybn = dy_ref[rows, D_BR:2 * D_BR].astype(F32)
            acc_s[V_GOUT_B] += dybn * yb * rb
            dyg = dybn * gob
            dyb = rb * dyg - yb * (rb * rb * rb) * _mean_last(dyg * yb)
            dho_s[rows, :] = dyb * sb
            dz_s[rows, 4 * D_BR:5 * D_BR] = dyb * hh * _silu_grad(sgb, sb)
            return 0

        _loop(N_GROUP, phase3, 0)

        for hd in range(N_HEAD):
            cs = slice(hd * HEAD, (hd + 1) * HEAD)
            dmb = dmix_s[:, cs].astype(BF16)
            dvn_s[:, cs] = _dot(wmt_ref[hd], dmb)
            dws_ref[hd] += _dot_nt(dmb, vnb_s[:, cs])

        def phase5(g, _):
            rows = _rows(g)
            dvn = dvn_s[rows, :]
            vh = vh_s[rows, :]
            acc_s[V_LN_G] += dvn * vh
            acc_s[V_LN_B] += dvn
            dvh = dvn * lng
            rs = rs_ref[rows, 0:1]
            dvg = rs * (dvh - _mean_last(dvh) - vh * _mean_last(dvh * vh))
            dz_s[rows, D_BR:2 * D_BR] = dvg * _gelu(z_ref[rows, D_BR:2 * D_BR], with_grad=True)[1]
            return 0

        _loop(N_GROUP, phase5, 0)

        ba, bx = ba_ref[...], bx_ref[...]
        sp8 = LRU_C * _softplus(-lam_ref[...])
        row0 = _row0_mask(rid)

        def phase6(k, carry):
            cg, ca = carry
            g = N_GROUP - 1 - k
            rows = _rows(g)
            bias0 = _row0_bias(jnp.logical_and(first_chunk, g == 0), row0)
            r, i, a, m2 = _lru_gates(pre_s[rows, 0:D_BR], pre_s[rows, D_BR:2 * D_BR], ba, bx, sp8, bias0)
            a_nx = jnp.where(rid < ROWS - 1, pltpu.roll(a, ROWS - 1, 0), ca)
            aa, bb = a_nx, dho_s[rows, :]
            for d in (1, 2, 4):
                a_sh = jnp.where(rid < ROWS - d, pltpu.roll(aa, ROWS - d, 0), 1.0)
                b_sh = jnp.where(rid < ROWS - d, pltpu.roll(bb, ROWS - d, 0), 0.0)
                bb = aa * b_sh + bb
                aa = aa * a_sh
            gg = bb + aa * cg
            hh = h_ref[rows, :]
            hprev = _shift_down(hh, prev_rows(h_ref, slice(None), g, h_halo), 1, rid)
            xc = xc_s[rows, :]
            gx = gg * xc
            dla = gg * hprev * a - gx * i * (a * a) * lax.rsqrt(m2)
            acc_s[V_LAM] += -(dla * r)
            dpa = -(dla * sp8) * r * (1.0 - r)
            mi = jnp.sqrt(m2) * i
            dpx = gx * mi * (1.0 - i)
            acc_s[V_B_A] += dpa
            acc_s[V_B_X] += dpx
            dpre_s[rows, 0:D_BR] = dpa
            dpre_s[rows, D_BR:2 * D_BR] = dpx
            dxc_s[rows, :] = gg * mi
            return _bcast_row(gg, 0), _bcast_row(a, 0)

        cg, ca = _loop(N_GROUP, phase6, (cg_s[...], ca_s[...]))
        cg_s[...] = cg
        ca_s[...] = ca

        for hd in range(N_HEAD):
            cs = slice(hd * HEAD, (hd + 1) * HEAD)
            dpre = jnp.concatenate([dpre_s[:, cs], dpre_s[:, D_BR + hd * HEAD:D_BR + (hd + 1) * HEAD]], axis=1).astype(BF16)
            dxc_s[:, cs] += _dot(dpre, waxt_ref[hd])
            dwax_ref[hd] += _dot_tn(xcb_ref[:, cs], dpre)

        def phase8(k, nxt):
            g = N_GROUP - 1 - k
            rows = _rows(g)
            dxc = dxc_s[rows, :]
            acc_s[V_CONV_B] += dxc
            xb = z_ref[rows, 3 * D_BR:4 * D_BR]
            dxb = cw_ref[3:4, :] * dxc
            acc_s[V_CONV_W + 3] += dxc * xb
            for j in range(1, CONV_W):
                later = _shift_up(dxc, nxt, j, rid)
                dxb = dxb + cw_ref[3 - j:4 - j, :] * later
                acc_s[V_CONV_W + 3 - j] += later * xb
            dz_s[rows, 3 * D_BR:4 * D_BR] = dxb
            return dxc

        dxchalo_s[...] = _loop(N_GROUP, phase8, dxchalo_s[...])
        dz_ref[...] = dz_s[...].astype(BF16)

        if last is not None:
            @pl.when(last)
            def _():
                for v in range(N_VEC):
                    vecs_ref[v:v + 1, :] = jnp.sum(acc_s[v], axis=0, keepdims=True)
                lam = lam_ref[...]
                vecs_ref[V_LAM:V_LAM + 1, :] = vecs_ref[V_LAM:V_LAM + 1, :] * (-LRU_C * _sig(-lam))
                tril = (lax.broadcasted_iota(jnp.int32, (HEAD, HEAD), 0)
                        >= lax.broadcasted_iota(jnp.int32, (HEAD, HEAD), 1))
                ones = jnp.ones((ROWS, HEAD), BF16)
                for hd in range(N_HEAD):
                    cs = slice(hd * HEAD, (hd + 1) * HEAD)
                    dws_ref[hd] = jnp.where(tril, dws_ref[hd], 0.0)
                    blk = accdm_s[:, cs]
                    hi = blk.astype(BF16)
                    lo = (blk - hi.astype(F32)).astype(BF16)
                    dbs_ref[hd:hd + 1, :] = (_dot_nt(ones, hi) + _dot_nt(ones, lo))[0:1, :]
                ex.wait(ex_in, ex_out, ex_sems)

    vec = pl.BlockSpec((1, D_BR), lambda i: (0, 0))
    n_step = n_chunk // MIX_SUB
    rows_blk = MIX_SUB * CHUNK
    rev = lambda i: (n_step - 1 - i, 0)
    halo = lambda col: (lambda i: (jnp.maximum((n_step - 1 - i) * MIX_SUB * halo_blocks - 1, 0), col))
    full3 = lambda a, b, c: pl.BlockSpec((a, b, c), lambda i: (0, 0, 0))
    big = lambda w: pltpu.VMEM((CHUNK, w), F32)
    res = pl.pallas_call(
        body, name="mix_bwd", grid=(n_step,),
        in_specs=[pl.BlockSpec((rows_blk, D_IN), rev), pl.BlockSpec((rows_blk, 2 * D_BR), rev),
                  pl.BlockSpec((rows_blk, D_BR), rev),
                  pl.BlockSpec((ROWS, D_BR), halo(0)), pl.BlockSpec((rows_blk, D_BR), rev),
                  pl.BlockSpec((rows_blk, D_BR), rev),
                  pl.BlockSpec((rows_blk, HEAD), rev), vec, vec,
                  full3(N_HEAD, HEAD, HEAD), full3(N_HEAD, HEAD, HEAD),
                  pl.BlockSpec((CHUNK, D_BR), lambda i: (0, 0)), pl.BlockSpec((ROWS, D_BR), lambda i: (0, 0)), vec,
                  full3(N_HEAD, HEAD, 2 * HEAD), full3(N_HEAD, 2 * HEAD, HEAD), vec, vec, vec, vec, vec]
        + [ANY_SPEC] * ex.n,
        out_specs=[pl.BlockSpec((rows_blk, D_IN), rev), pl.BlockSpec((N_VEC, D_BR), lambda i: (0, 0)),
                   full3(N_HEAD, HEAD, HEAD), full3(N_HEAD, HEAD, 2 * HEAD),
                   pl.BlockSpec((N_HEAD, HEAD), lambda i: (0, 0))] + [ANY_SPEC] * ex.n,
        out_shape=[SDS((t_len, D_IN), BF16), SDS((N_VEC, D_BR), F32), SDS((N_HEAD, HEAD, HEAD), F32),
                   SDS((N_HEAD, HEAD, 2 * HEAD), F32), SDS((N_HEAD, HEAD), F32)] + ex.out_shape,
        scratch_shapes=[pltpu.VMEM((CHUNK, D_BR), BF16), big(D_BR), big(D_BR), big(D_BR), big(2 * D_BR), big(D_BR),
                        big(D_BR), big(D_BR), big(D_BR), big(2 * D_BR), big(D_IN),
                        pltpu.VMEM((N_VEC, ROWS, D_BR), F32), big(D_BR),
                        pltpu.VMEM((ROWS, D_BR), F32), pltpu.VMEM((ROWS, D_BR), F32), pltpu.VMEM((ROWS, D_BR), F32)]
        + ex.scratch,
        compiler_params=_params(("arbitrary",), 48),
    )(z, dy, h, h, vhb, xcb, rs, ln_g, ln_b, wm, wm_t, bias, cw, cb, wax, wax_t, ba, bx, lam, goa, gob, *ex_arrs)
    return res[:n_out], res[n_out:]


def _in_bwd(dz, w_in_g, x, dh1, pre_g, tm=256):
    t_len = x.shape[0]
    n_tile = t_len // tm

    def body(dz_ref, w_hbm, x_ref, dh1_ref, g_ref, gx_ref, dg_ref, w_s, t_even, t_odd, dg_s, w_sems):
        i = pl.program_id(0)

        @pl.when(i == 0)
        def _():
            loads = [pltpu.make_async_copy(w_hbm.at[s], w_s.at[:, s * W_IN_SHARD:(s + 1) * W_IN_SHARD], w_sems.at[s])
                     for s in range(N_DEV)]
            for cp in loads:
                cp.start()
            dg_s[...] = jnp.zeros_like(dg_s)
            for s, cp in enumerate(loads):
                cp.wait()
                cols = slice(s * W_IN_SHARD, (s + 1) * W_IN_SHARD)
                part = _dot_nt(dz_ref[:, cols], w_s[:, cols])
                t_even[...] = part if s == 0 else t_even[...] + part

        def step(t_new, t_old):
            g = g_ref[...]
            acc = dg_s[...]
            blk = D_MODEL // N_DEV
            per = tm // TILE_ROWS // N_DEV
            for j in range(N_DEV):
                t_new[:, j * blk:(j + 1) * blk] = _dot_nt(dz_ref[...], w_s[j * blk:(j + 1) * blk, :])
                for q in range(j * per, (j + 1) * per):
                    rows = slice(q * TILE_ROWS, (q + 1) * TILE_ROWS)
                    xv = x_ref[rows, :]
                    r = lax.rsqrt(_mean_last(xv * xv) + EPS)
                    xh = xv * r
                    dhn = t_old[rows, :]
                    dg = dhn * g
                    gx_ref[rows, :] = dh1_ref[rows, :] + r * (dg - xh * _mean_last(dg * xh))
                    acc = acc + _fold_rows(dhn * xh)
            dg_s[...] = acc

        @pl.when((i % 2 == 0) & (i > 0))
        def _():
            step(t_even, t_odd)

        @pl.when(i % 2 == 1)
        def _():
            step(t_odd, t_even)

        @pl.when(i == n_tile)
        def _():
            dg_ref[...] = jnp.sum(dg_s[...], axis=0, keepdims=True)

    matmul_tile = lambda i: (jnp.minimum(i, n_tile - 1), 0)
    rows_tile = lambda i: (jnp.maximum(i - 1, 0), 0)
    res = pl.pallas_call(
        body, name="in_bwd", grid=(n_tile + 1,),
        in_specs=[pl.BlockSpec((tm, D_IN), matmul_tile), ANY_SPEC, pl.BlockSpec((tm, D_MODEL), rows_tile),
                  pl.BlockSpec((tm, D_MODEL), rows_tile), pl.BlockSpec((1, D_MODEL), lambda i: (0, 0))],
        out_specs=[pl.BlockSpec((tm, D_MODEL), rows_tile), pl.BlockSpec((1, D_MODEL), lambda i: (0, 0))],
        out_shape=[SDS((t_len, D_MODEL), F32), SDS((1, D_MODEL), F32)],
        scratch_shapes=[pltpu.VMEM((D_MODEL, D_IN), BF16), pltpu.VMEM((tm, D_MODEL), F32), pltpu.VMEM((tm, D_MODEL), F32),
                        pltpu.VMEM((ROWS, D_MODEL), F32), pltpu.SemaphoreType.DMA((N_DEV,))],
        compiler_params=_params(("arbitrary",), 54),
    )(dz, w_in_g, x, dh1, pre_g)
    return res[0], res[1]


def _grad_w(a, b, bn, shard_major, name, tk=1024, ex_arrs=(), ex_scatter=()):
    t_len, m = a.shape
    n = b.shape[1]
    n_j, n_k = n // bn, t_len // tk
    ex = _Exchange(ex_arrs, ex_scatter)

    def body(a_ref, b_ref, *refs):
        ex_in, o_ref, ex_out = refs[:ex.n], refs[ex.n], refs[ex.n + 1:2 * ex.n + 1]
        acc_s, ex_sems = refs[2 * ex.n + 1], refs[2 * ex.n + 2:]
        j, k = pl.program_id(0), pl.program_id(1)
        if ex.n:
            @pl.when(jnp.logical_and(j == 0, k == 0))
            def _():
                ex.start(ex_in, ex_out, ex_sems)

        @pl.when(k == 0)
        def _():
            acc_s[...] = jnp.zeros_like(acc_s)

        acc_s[...] += _dot_tn(a_ref[...], b_ref[...])

        @pl.when(k == n_k - 1)
        def _():
            o_ref[...] = acc_s[...].astype(BF16)

        if ex.n:
            @pl.when(jnp.logical_and(j == n_j - 1, k == n_k - 1))
            def _():
                ex.wait(ex_in, ex_out, ex_sems)

    if shard_major:
        out_spec, out_shape = pl.BlockSpec((None, m, bn), lambda j, k: (j, 0, 0)), SDS((n_j, m, bn), BF16)
    else:
        out_spec, out_shape = pl.BlockSpec((m, bn), lambda j, k: (0, j)), SDS((m, n), BF16)
    res = pl.pallas_call(
        body, name=name, grid=(n_j, n_k),
        in_specs=[pl.BlockSpec((tk, m), lambda j, k: (k, 0)), pl.BlockSpec((tk, bn), lambda j, k: (k, j))]
        + [ANY_SPEC] * ex.n,
        out_specs=[out_spec] + [ANY_SPEC] * ex.n, out_shape=[out_shape] + ex.out_shape,
        scratch_shapes=[pltpu.VMEM((m, bn), F32)] + (ex.scratch if ex.n else []),
        compiler_params=_params(("arbitrary", "arbitrary"), 40),
    )(a, b, *ex_arrs)
    return res[0], res[1:]


RS_CHIPS = (6, 2, 4, 0)
RS_SLOTS = (0, 1, 2, 4, 6)


def _grad_w_in_pairs(hn, dz, ex_arrs, ex_scatter, tk=1024):
    t_len = hn.shape[0]
    n_k = t_len // tk
    n_ph = len(RS_CHIPS)
    ex = _Exchange(ex_arrs, ex_scatter)
    me_out = 4 * lax.axis_index("x") + 2 * lax.axis_index("y") + lax.axis_index("c")
    order = jnp.stack([(me_out ^ chip) // 2 for chip in RS_CHIPS]).astype(jnp.int32)
    slots = jnp.stack([me_out ^ k for k in RS_SLOTS]).astype(jnp.int32)
    shard = W_IN_SHARD

    def body(order_ref, a_ref, b_ref, *refs):
        ex_in, parts_hbm, ex_out = refs[:ex.n], refs[ex.n], refs[ex.n + 1:2 * ex.n + 1]
        (acc_s, tb_s, stage_s, rx_s, d2d_send, d2d_recv, ici_send, ici_recv, sib_sems,
         loc_sem) = refs[2 * ex.n + 1:2 * ex.n + 11]
        ex_sems = refs[2 * ex.n + 11:]
        j, k = pl.program_id(0), pl.program_id(1)
        x, y, c, me = _mesh_place()
        sib = _peer(x, y, c, SIBLING)[0]

        def to_sibling(p):
            return _remote(stage_s.at[0], rx_s.at[p % 2], d2d_send.at[p], d2d_recv.at[p], sib)

        def over_ici(p):
            dev = _peer(x, y, c, RS_CHIPS[p])[0]
            return _remote(stage_s.at[1], parts_hbm.at[me], ici_send.at[p], ici_recv.at[p], dev)

        def own_chip():
            return (_remote(stage_s.at[0], parts_hbm.at[me], sib_sems.at[0], sib_sems.at[1], sib),
                    pltpu.make_async_copy(stage_s.at[1], parts_hbm.at[me], loc_sem.at[0]))

        @pl.when(jnp.logical_and(j == 0, k == 0))
        def _():
            ex.start(ex_in, ex_out, ex_sems)

        for p in range(n_ph - 1):
            for core in (0, 1):
                @pl.when(jnp.logical_and(jnp.logical_and(j == p + 1, k == 0), c == core))
                def _(p=p, core=core):
                    to_sibling(p).wait_recv()
                    if p >= 1:
                        over_ici(p - 1).wait_send()
                    mine = acc_s[:, core * shard:(core + 1) * shard]
                    stage_s[1] = (mine + rx_s[p % 2].astype(F32)).astype(BF16)
                    over_ici(p).start()

        @pl.when(k == 0)
        def _():
            acc_s[...] = jnp.zeros_like(acc_s)

        a = a_ref[...]
        acc_s[:, 0:W_BODY] += _dot_tn(a, b_ref[:, 0:W_BODY])
        acc_s[:, shard:shard + W_BODY] += _dot_tn(a, b_ref[:, shard:shard + W_BODY])
        tb_s[:, 0:W_TAIL] = b_ref[:, W_BODY:shard]
        tb_s[:, W_TAIL:2 * W_TAIL] = b_ref[:, shard + W_BODY:2 * shard]
        tails = _dot_tn(a, tb_s[...])
        acc_s[:, W_BODY:shard] += tails[:, 0:W_TAIL]
        acc_s[:, shard + W_BODY:2 * shard] += tails[:, W_TAIL:2 * W_TAIL]

        for p in range(n_ph):
            for core in (0, 1):
                @pl.when(jnp.logical_and(jnp.logical_and(j == p, k == n_k - 1), c == core))
                def _(p=p, core=core):
                    same = acc_s[:, core * shard:(core + 1) * shard]
                    other = acc_s[:, (1 - core) * shard:(2 - core) * shard]
                    if p >= 1:
                        to_sibling(p - 1).wait_send()
                    stage_s[0] = other.astype(BF16)
                    if p < n_ph - 1:
                        to_sibling(p).start()
                    else:
                        over_ici(n_ph - 2).wait_send()
                        stage_s[1] = same.astype(BF16)
                        for cp in own_chip():
                            cp.start()

        @pl.when(jnp.logical_and(j == n_ph - 1, k == n_k - 1))
        def _():
            to_sib, local = own_chip()
            to_sib.wait_send()
            local.wait()
            _remote(stage_s.at[0], parts_hbm.at[_peer(x, y, c, SIBLING)[1]], sib_sems.at[0], sib_sems.at[1], sib).wait_recv()
            for p in range(n_ph - 1):
                dev, lin = _peer(x, y, c, RS_CHIPS[p])
                _remote(stage_s.at[0], parts_hbm.at[lin], ici_send.at[p], ici_recv.at[p], dev).wait_recv()
            ex.wait(ex_in, ex_out, ex_sems)

    dma = lambda n: pltpu.SemaphoreType.DMA((n,))
    grid_spec = pltpu.PrefetchScalarGridSpec(
        num_scalar_prefetch=1, grid=(n_ph, n_k),
        in_specs=[pl.BlockSpec((tk, D_MODEL), lambda j, k, order: (k, 0)),
                  pl.BlockSpec((tk, 2 * shard), lambda j, k, order: (k, order[j]))] + [ANY_SPEC] * ex.n,
        out_specs=[ANY_SPEC] * (1 + ex.n),
        scratch_shapes=[pltpu.VMEM((D_MODEL, 2 * shard), F32), pltpu.VMEM((tk, 2 * W_TAIL), BF16),
                        pltpu.VMEM((2, D_MODEL, shard), BF16),
                        pltpu.VMEM((2, D_MODEL, shard), BF16), dma(n_ph - 1), dma(n_ph - 1), dma(n_ph - 1),
                        dma(n_ph - 1), dma(2), dma(1)] + ex.scratch)
    res = pl.pallas_call(
        body, name="grad_w_in", grid_spec=grid_spec,
        out_shape=[SDS((N_DEV, D_MODEL, shard), BF16)] + ex.out_shape,
        compiler_params=_params(("arbitrary", "arbitrary"), 54),
    )(order, hn, dz, *ex_arrs)
    return res[0], slots, res[1:]


def _sum_parts(parts, name):
    def body(p_ref, o_ref):
        g = p_ref[0].astype(F32)
        for s in range(1, parts.shape[0]):
            g = g + p_ref[s].astype(F32)
        o_ref[...] = g

    return pl.pallas_call(body, name=name, out_shape=SDS(parts.shape[1:], F32))(parts)


def _adamw_math(g, w_ref, m_ref, v_ref, g_ref, d_ref, nm_ref, nv_ref):
    c1 = 1.0 - ADAM_B1 ** ADAM_STEP
    c2 = 1.0 - ADAM_B2 ** ADAM_STEP
    g_ref[...] = g
    nm = ADAM_B1 * m_ref[...] + (1.0 - ADAM_B1) * g
    nv = ADAM_B2 * v_ref[...] + (1.0 - ADAM_B2) * (g * g)
    nm_ref[...] = nm
    nv_ref[...] = nv
    d_ref[...] = -ADAM_LR * ((nm / c1) / (jnp.sqrt(nv / c2) + ADAM_EPS) + ADAM_WD * w_ref[...])


def _adamw(parts, w, m, v, name, tr):
    rows, cols = w.shape
    n_parts = parts.shape[0]

    def body(p_ref, *refs):
        g = p_ref[0].astype(F32)
        for s in range(1, n_parts):
            g = g + p_ref[s].astype(F32)
        _adamw_math(g, *refs)

    tile = pl.BlockSpec((tr, cols), lambda i: (i, 0))
    return pl.pallas_call(
        body, name=name, grid=(rows // tr,),
        in_specs=[pl.BlockSpec((n_parts, tr, cols), lambda i: (0, i, 0)), tile, tile, tile],
        out_specs=[tile] * 4, out_shape=[SDS((rows, cols), F32)] * 4,
        compiler_params=_params(("arbitrary",), 40),
    )(parts, w, m, v)


def _adamw_unpacked(grads, triples, name):
    n = len(triples)
    n_rows = [t[0].shape[0] for t in triples]

    def body(g_ref, *refs):
        ins, outs = refs[:3 * n], refs[3 * n:]
        row = 0
        for i in range(n):
            _adamw_math(g_ref[row:row + n_rows[i], :], *ins[3 * i:3 * i + 3], *outs[4 * i:4 * i + 4])
            row += n_rows[i]
        outs[4 * n][...] = g_ref[row:row + ROWS, :]

    out_shape = [SDS((r, LANES), F32) for r in n_rows for _ in range(4)] + [SDS((ROWS, LANES), F32)]
    return pl.pallas_call(
        body, name=name, out_shape=out_shape,
        compiler_params=pltpu.CompilerParams(vmem_limit_bytes=40 * MIB),
    )(grads, *[a for t in triples for a in t])


def _adamw_slots(parts, slots, w, m, v, name, tr):
    rows, cols = w.shape
    n_slots = slots.shape[0]

    def body(slots_ref, *refs):
        g = refs[0][...].astype(F32)
        for s in range(1, n_slots):
            g = g + refs[s][...].astype(F32)
        _adamw_math(g, *refs[n_slots:])

    tile = pl.BlockSpec((tr, cols), lambda i, slots: (i, 0))
    part = lambda s: pl.BlockSpec((None, tr, cols), lambda i, slots: (slots[s], i, 0))
    grid_spec = pltpu.PrefetchScalarGridSpec(
        num_scalar_prefetch=1, grid=(rows // tr,),
        in_specs=[part(s) for s in range(n_slots)] + [tile, tile, tile], out_specs=[tile] * 4)
    return pl.pallas_call(
        body, name=name, grid_spec=grid_spec, out_shape=[SDS((rows, cols), F32)] * 4,
        compiler_params=_params(("arbitrary",), 40),
    )(slots, *([parts] * n_slots), w, m, v)


PACKED = ("gmlp_ln_g", "gmlp_ln_b", "gmlp_ws", "gmlp_bs", "conv_b", "w_a", "b_a", "w_x", "b_x", "lam", "gmlp_out_g",
          "lru_out_g", "post_g")
WEIGHTS = ("pre_g", "w_in", "gmlp_ln_g", "gmlp_ln_b", "gmlp_ws", "gmlp_bs", "conv_w", "conv_b", "w_a", "b_a", "w_x",
           "b_x", "lam", "gmlp_out_g", "lru_out_g", "w_out", "post_g", "w_pe", "w_pg")
LANES = 128


PACK_ROWS = 3200


def _pack(parts):
    rows = [p.reshape(-1, LANES) for p in parts]
    used = sum(r.shape[0] for r in rows)
    return jnp.concatenate(rows + [jnp.zeros((PACK_ROWS - used, LANES), F32)], axis=0)


def _pad_rows(a, rows):
    return jnp.concatenate([a, jnp.zeros((rows - a.shape[0],) + a.shape[1:], a.dtype)], axis=0)


def kernel(x, p, pre_g, w_in, gmlp_ln_g, gmlp_ln_b, gmlp_ws, gmlp_bs, conv_w, conv_b, w_a, b_a, w_x, b_x, lam, gmlp_out_g, lru_out_g, w_out, post_g, w_pe, w_pg, loss_target, m_pre_g, m_w_in, m_gmlp_ln_g, m_gmlp_ln_b, m_gmlp_ws, m_gmlp_bs, m_conv_w, m_conv_b, m_w_a, m_b_a, m_w_x, m_b_x, m_lam, m_gmlp_out_g, m_lru_out_g, m_w_out, m_post_g, m_w_pe, m_w_pg, v_pre_g, v_w_in, v_gmlp_ln_g, v_gmlp_ln_b, v_gmlp_ws, v_gmlp_bs, v_conv_w, v_conv_b, v_w_a, v_b_a, v_w_x, v_b_x, v_lam, v_gmlp_out_g, v_lru_out_g, v_w_out, v_post_g, v_w_pe, v_w_pg):
    args = dict(locals())
    weights = {n: args[n] for n in WEIGHTS}
    m_in = {n: args["m_" + n] for n in WEIGHTS}
    v_in = {n: args["v_" + n] for n in WEIGHTS}
    sm = {n: weights[n][0] for n in PACKED}
    shard_rows = D_MODEL // N_DEV
    xs, ps, tgt = x[0], p[0, 0], loss_target[0]

    vec = lambda a: a.reshape(1, -1)
    tril = jnp.tril(jnp.ones((CHUNK, CHUNK), dtype=bool))
    wm32 = jnp.where(tril[None], sm["gmlp_ws"], 0.0)
    wm, wm_t = wm32.astype(BF16), jnp.swapaxes(wm32, 1, 2).astype(BF16)
    bias = jnp.repeat(sm["gmlp_bs"].T, HEAD, axis=1)
    wax32 = jnp.concatenate([sm["w_a"], sm["w_x"]], axis=2)
    wax, wax_t = wax32.astype(BF16), jnp.swapaxes(wax32, 1, 2).astype(BF16)
    ln_g, ln_b = vec(sm["gmlp_ln_g"]), vec(sm["gmlp_ln_b"])
    post_g_v = vec(sm["post_g"])

    hn = _pre_norm(xs, pre_g)
    cw_shard = _pad_rows(conv_w.reshape(CONV_W, HEAD), ROWS)
    z, w_in_g, (cw_g,) = _in_proj(hn, w_in[0].astype(BF16), [cw_shard])
    cw_full = jnp.transpose(cw_g[:, :CONV_W, :], (1, 0, 2)).reshape(CONV_W, D_BR)
    mixer_consts = dict(cw=_pad_rows(cw_full, ROWS), cb=vec(sm["conv_b"]), ba=vec(sm["b_a"]), bx=vec(sm["b_x"]),
                        lam=vec(sm["lam"]), goa=vec(sm["gmlp_out_g"]), gob=vec(sm["lru_out_g"]))
    (y, h, vhb, xcb, v_rs), (w_out_g, w_pe_g, w_pg_g) = _mix_fwd(
        z, ln_g, ln_b, wm, bias, wax=wax, **mixer_consts,
        ex_arrs=[w_out[0].astype(BF16), w_pe[0].astype(BF16), w_pg[0].astype(BF16)], ex_scatter=[False, False, False])
    w_out_f, w_pg_f = w_out_g.reshape(D_MODEL, D_MODEL), w_pg_g.reshape(D_MODEL, D_MODEL)
    h1, ob = _out_proj(y, xs, w_out_f, post_g_v)
    dh2, dgl, h1b, loss_part, d_w_pe = _ple_loss(h1, ps, tgt, w_pg_f, w_pe_g)

    dh1, do, dy, d_post_g = _tail_bwd(dh2, dgl, ob, w_pg_f, w_out_f, post_g_v)
    d_w_out, _ = _grad_w(y, do, 1024, False, "grad_w_out")
    d_w_pg, _ = _grad_w(h1b, dgl, 1024, False, "grad_w_pg")
    (dz, vecs, d_ws, d_wax, d_bs), (parts_out, parts_pg, parts_pe) = _mix_bwd(
        z, dy, h, vhb, xcb, v_rs, ln_g, ln_b, wm, wm_t, bias, wax=wax, wax_t=wax_t, **mixer_consts,
        ex_arrs=[d_w_out.reshape(N_DEV, shard_rows, D_MODEL), d_w_pg.reshape(N_DEV, shard_rows, D_MODEL), d_w_pe],
        ex_scatter=[True, True, True])

    small = {"gmlp_ln_g": vecs[V_LN_G], "gmlp_ln_b": vecs[V_LN_B], "gmlp_ws": d_ws, "gmlp_bs": d_bs,
             "conv_b": vecs[V_CONV_B], "w_a": d_wax[:, :, :HEAD], "b_a": vecs[V_B_A], "w_x": d_wax[:, :, HEAD:],
             "b_x": vecs[V_B_X], "lam": vecs[V_LAM], "gmlp_out_g": vecs[V_GOUT_A], "lru_out_g": vecs[V_GOUT_B],
             "post_g": d_post_g}
    small_part = _pack([small[n] for n in PACKED] + [loss_part]).reshape(N_DEV, PACK_ROWS // N_DEV, LANES)
    d_cw_blocks = jnp.transpose(vecs[V_CONV_W:V_CONV_W + CONV_W].reshape(CONV_W, N_DEV, HEAD), (1, 0, 2))
    d_cw_blocks = jnp.concatenate([d_cw_blocks, jnp.zeros((N_DEV, ROWS - CONV_W, HEAD), F32)], axis=1)
    parts_in, slots_in, (small_blocks, parts_cw) = _grad_w_in_pairs(
        hn, dz, ex_arrs=[small_part, d_cw_blocks], ex_scatter=[True, True])
    small_sum = _sum_parts(small_blocks, "sum_small")
    grad_x, d_pre_g = _in_bwd(dz, w_in_g, xs, dh1, pre_g)
    pre_rows = D_MODEL // LANES
    small_all, parts_pre = _exchange([small_sum, d_pre_g.reshape(pre_rows, LANES)], False, "gather_small_grads")

    pad_cw = lambda a: _pad_rows(a.reshape(CONV_W, HEAD), ROWS)
    flat = lambda a: a.reshape(pre_rows, LANES)
    outs = {
        "w_in": _adamw_slots(parts_in, slots_in, w_in[0], m_w_in[0], v_w_in[0], "adamw_w_in", 256),
        "w_out": _adamw(parts_out, w_out[0], m_w_out[0], v_w_out[0], "adamw_w_out", 128),
        "w_pe": _adamw(parts_pe, w_pe[0], m_w_pe[0], v_w_pe[0], "adamw_w_pe", 256),
        "w_pg": _adamw(parts_pg, w_pg[0], m_w_pg[0], v_w_pg[0], "adamw_w_pg", 128),
        "conv_w": [a[:CONV_W] for a in
                   _adamw(parts_cw, pad_cw(conv_w), pad_cw(m_conv_w), pad_cw(v_conv_w), "adamw_conv_w", ROWS)],
        "pre_g": _adamw(parts_pre, flat(pre_g), flat(m_pre_g), flat(v_pre_g), "adamw_pre_g", pre_rows),
    }
    as_rows = lambda a: a.reshape(-1, LANES)
    small_res = _adamw_unpacked(small_all.reshape(PACK_ROWS, LANES),
                                [(as_rows(weights[n]), as_rows(m_in[n]), as_rows(v_in[n])) for n in PACKED], "adamw_small")
    for i, n in enumerate(PACKED):
        outs[n] = small_res[4 * i:4 * i + 4]
    loss = small_res[-1][0, 0]

    result = [loss, grad_x[None]]
    for q in range(4):
        result += [outs[n][q].reshape(weights[n].shape) for n in WEIGHTS]
    return tuple(result)
```
